```python
import jax
import jax.numpy as jnp
from jax import lax
import numpy as np

D_MODEL = 1024
BATCH = 8
SEQ = 4096
DEPTH = 1

GLA_HEADS = 4
GLA_DK = D_MODEL // (2 * GLA_HEADS)
GLA_DV = D_MODEL // GLA_HEADS
GLA_LOWRANK = 16
GLA_TAU = 16.0
GLA_CHUNK = 16

ATTN_GROUPS = ((128, 1), (512, 4), (2048, 16))
ATTN_HEADS_PER_GROUP = 4
ATTN_HEAD_DIM = 64
ATTN_HEADS = ATTN_HEADS_PER_GROUP * len(ATTN_GROUPS)
ROPE_THETA = 10000.0

D_FF = 2816
CONV_WIDTH = 3
EPS = 1e-6

GLA_QK_WIDTH = GLA_HEADS * GLA_DK
GLA_V_WIDTH = GLA_HEADS * GLA_DV
ATTN_WIDTH = ATTN_HEADS * ATTN_HEAD_DIM
ATTN_OUT_WIDTH = ATTN_HEADS_PER_GROUP * ATTN_HEAD_DIM
IN_WIDTHS = (GLA_QK_WIDTH, GLA_QK_WIDTH, GLA_V_WIDTH, GLA_V_WIDTH, GLA_LOWRANK,
             ATTN_WIDTH, ATTN_WIDTH, ATTN_WIDTH, D_MODEL, D_MODEL)
D_IN = sum(IN_WIDTHS)

kernel_name = "hybrid_gla_dilated_attn_convffn_block"


def rmsnorm(x, w):
    xf = x.astype(jnp.float32)
    y = xf * lax.rsqrt(jnp.mean(xf * xf, axis=-1, keepdims=True) + EPS)
    return (y * w.astype(jnp.float32)).astype(x.dtype)


def rotary(t, positions):
    half = t.shape[-1] // 2
    inv_freq = ROPE_THETA ** (-jnp.arange(half, dtype=jnp.float32) / half)
    ang = positions.astype(jnp.float32)[..., None] * inv_freq
    cos = jnp.cos(ang)[:, :, None, :]
    sin = jnp.sin(ang)[:, :, None, :]
    tf = t.astype(jnp.float32)
    t1, t2 = tf[..., :half], tf[..., half:]
    return jnp.concatenate([t1 * cos - t2 * sin, t2 * cos + t1 * sin], axis=-1).astype(t.dtype)


def gla_chunked(q, k, v, log_a):
    B, S, H, dk = q.shape
    dv = v.shape[-1]
    C = GLA_CHUNK
    N = S // C

    def chunks(t):
        return t.astype(jnp.float32).reshape(B, N, C, H, -1).transpose(0, 3, 1, 2, 4)

    qc = chunks(q) * (dk ** -0.5)
    kc, vc, lac = chunks(k), chunks(v), chunks(log_a)
    b = jnp.cumsum(lac, axis=3)
    b_last = b[:, :, :, -1:, :]
    qg = qc * jnp.exp(b)
    kg = kc * jnp.exp(-b)
    kd = kc * jnp.exp(b_last - b)

    causal = jnp.tril(jnp.ones((C, C), dtype=bool))
    attn = jnp.where(causal, jnp.einsum('bhnid,bhnjd->bhnij', qg, kg), 0.0)
    o_intra = jnp.einsum('bhnij,bhnjv->bhniv', attn, vc)

    def step(state, xs):
        qg_n, kd_n, v_n, dec_n = xs
        o_n = jnp.einsum('bhcd,bhdv->bhcv', qg_n, state)
        state = dec_n[..., None] * state + jnp.einsum('bhcd,bhcv->bhdv', kd_n, v_n)
        return state, o_n

    xs = (qg.transpose(2, 0, 1, 3, 4), kd.transpose(2, 0, 1, 3, 4), vc.transpose(2, 0, 1, 3, 4),
          jnp.exp(b_last[:, :, :, 0, :]).transpose(2, 0, 1, 3))
    state0 = jnp.zeros((B, H, dk, dv), jnp.float32)
    _, o_inter = lax.scan(step, state0, xs)
    o = o_intra + o_inter.transpose(1, 2, 0, 3, 4)
    return o.transpose(0, 2, 3, 1, 4).reshape(B, S, H, dv)


def dilated_group_attention(q, k, v, window, dilation):
    B, S, H, Dh = q.shape
    r = dilation
    L = S // r
    blk = window // dilation
    nblk = -(-L // blk)
    Lp = nblk * blk

    def to_sub(t):
        t = t.reshape(B, L, r, H, Dh).transpose(0, 2, 3, 1, 4)
        t = jnp.pad(t, ((0, 0), (0, 0), (0, 0), (0, Lp - L), (0, 0)))
        return t.reshape(B, r, H, nblk, blk, Dh)

    def with_prev(t):
        prev = jnp.pad(t, ((0, 0), (0, 0), (0, 0), (1, 0), (0, 0), (0, 0)))[:, :, :, :-1]
        return jnp.concatenate([prev, t], axis=-2)

    qb = to_sub(q)
    kc = with_prev(to_sub(k))
    vc = with_prev(to_sub(v))
    s = jnp.einsum('brhnid,brhnjd->brhnij', qb, kc).astype(jnp.float32) * (Dh ** -0.5)
    i_idx = jnp.arange(blk)[:, None]
    j_idx = jnp.arange(2 * blk)[None, :]
    dist = i_idx + blk - j_idx
    band = (dist >= 0) & (dist <= blk)
    n_idx = jnp.arange(nblk)[:, None, None]
    valid = band[None] & ((n_idx > 0) | (j_idx >= blk)[None])
    s = jnp.where(valid, s, -jnp.inf)
    m = jnp.max(s, axis=-1, keepdims=True)
    p = jnp.exp(s - m)
    denom = jnp.sum(p, axis=-1, keepdims=True)
    o = jnp.einsum('brhnij,brhnjd->brhnid', p, vc.astype(jnp.float32)) / denom
    lse = (m + jnp.log(denom))[..., 0]
    o = o.reshape(B, r, H, Lp, Dh)[:, :, :, :L].transpose(0, 3, 1, 2, 4).reshape(B, S, H, Dh)
    lse = lse.reshape(B, r, H, Lp)[..., :L].transpose(0, 3, 1, 2).reshape(B, S, H)
    return o, lse


def dilated_attention(aq, ak, av, positions):
    B, S, _ = aq.shape
    q = rotary(aq.reshape(B, S, ATTN_HEADS, ATTN_HEAD_DIM), positions)
    k = rotary(ak.reshape(B, S, ATTN_HEADS, ATTN_HEAD_DIM), positions)
    v = av.reshape(B, S, ATTN_HEADS, ATTN_HEAD_DIM)
    outs, lses = [], []
    for g, (window, dilation) in enumerate(ATTN_GROUPS):
        sl = slice(g * ATTN_HEADS_PER_GROUP, (g + 1) * ATTN_HEADS_PER_GROUP)
        o_g, lse_g = dilated_group_attention(q[:, :, sl], k[:, :, sl], v[:, :, sl], window, dilation)
        outs.append(o_g)
        lses.append(lse_g)
    wts = jax.nn.softmax(jnp.stack(lses, axis=0), axis=0)
    o = jnp.sum(wts[..., None] * jnp.stack(outs, axis=0), axis=0)
    return o.reshape(B, S, ATTN_OUT_WIDTH).astype(aq.dtype)


def causal_conv_ffn_hidden(h, w_up, conv_w, conv_b):
    S = h.shape[1]
    u = h @ w_up
    up = jnp.pad(u, ((0, 0), (CONV_WIDTH - 1, 0), (0, 0)))
    u = conv_b + sum(conv_w[i] * up[:, i:i + S] for i in range(CONV_WIDTH))
    value, gate = jnp.split(u, 2, axis=-1)
    return jax.nn.gelu(gate, approximate=False) * value


def _fwd_setup_inputs(seed: int = 0) -> dict:
    key = jax.random.key(seed)
    ks = jax.random.split(key, 20)
    f32 = jnp.float32

    def nrm(k, shape, scale):
        return jax.random.normal(k, shape, f32) * scale

    def gain(k, shape):
        return 1.0 + 0.02 * jax.random.normal(k, shape, f32)

    L = DEPTH
    x = nrm(ks[0], (BATCH, SEQ, D_MODEL), 1.0)
    c = nrm(ks[1], (BATCH, D_MODEL), 1.0)
    positions = (jax.random.randint(ks[2], (BATCH, 1), 0, 1024, dtype=jnp.int32)
                 + jnp.arange(SEQ, dtype=jnp.int32)[None, :])
    return {
        'x': x,
        'c': c,
        'positions': positions,
        'ada_w': nrm(ks[3], (L, D_MODEL, 6 * D_MODEL), D_MODEL ** -0.5),
        'ada_b': nrm(ks[4], (L, 6 * D_MODEL), 0.02),
        'norm1_w': gain(ks[5], (L, D_MODEL)),
        'w_in': nrm(ks[6], (L, D_MODEL, D_IN), D_MODEL ** -0.5),
        'gla_gate_w2': nrm(ks[7], (L, GLA_LOWRANK, GLA_QK_WIDTH), GLA_LOWRANK ** -0.5),
        'gla_gate_b': nrm(ks[8], (L, GLA_QK_WIDTH), 0.1),
        'gla_norm_w': gain(ks[9], (L, GLA_DV)),
        'w_gla_branch': nrm(ks[10], (L, GLA_V_WIDTH, D_MODEL), GLA_V_WIDTH ** -0.5),
        'w_attn_branch': nrm(ks[11], (L, ATTN_OUT_WIDTH, D_MODEL), ATTN_OUT_WIDTH ** -0.5),
        'w_out': nrm(ks[12], (L, D_MODEL, D_MODEL), D_MODEL ** -0.5),
        'norm2_w': gain(ks[13], (L, D_MODEL)),
        'w_up': nrm(ks[14], (L, D_MODEL, 2 * D_FF), D_MODEL ** -0.5),
        'conv_w': nrm(ks[15], (L, CONV_WIDTH, 2 * D_FF), CONV_WIDTH ** -0.5),
        'conv_b': nrm(ks[16], (L, 2 * D_FF), 0.01),
        'w_down': nrm(ks[17], (L, D_FF, D_MODEL), D_FF ** -0.5),
        'final_norm_w': gain(ks[18], (D_MODEL,)),
    }


def _fwd_reference(x, c, positions, ada_w, ada_b, norm1_w, w_in, gla_gate_w2, gla_gate_b, gla_norm_w,
              w_gla_branch, w_attn_branch, w_out, norm2_w, w_up, conv_w, conv_b, w_down, final_norm_w):
    B, S, _ = x.shape
    split_at = np.cumsum(IN_WIDTHS)[:-1].tolist()
    for layer in range(DEPTH):
        mod = jax.nn.silu(c) @ ada_w[layer] + ada_b[layer]
        shift1, scale1, gate1, shift2, scale2, gate2 = jnp.split(mod[:, None, :], 6, axis=-1)

        h = rmsnorm(x, norm1_w[layer]) * (1 + scale1) + shift1
        gq, gk, gv, gr, g_lr, aq, ak, av, merge_a, merge_b = jnp.split(h @ w_in[layer], split_at, axis=-1)

        log_a = jax.nn.log_sigmoid((g_lr @ gla_gate_w2[layer] + gla_gate_b[layer]).astype(jnp.float32)) / GLA_TAU
        o_gla = gla_chunked(gq.reshape(B, S, GLA_HEADS, GLA_DK), gk.reshape(B, S, GLA_HEADS, GLA_DK),
                            gv.reshape(B, S, GLA_HEADS, GLA_DV), log_a.reshape(B, S, GLA_HEADS, GLA_DK))
        o_gla = rmsnorm(o_gla.astype(x.dtype), gla_norm_w[layer]).reshape(B, S, GLA_V_WIDTH) * jax.nn.silu(gr)
        y_gla = o_gla @ w_gla_branch[layer]

        y_attn = dilated_attention(aq, ak, av, positions) @ w_attn_branch[layer]

        mixed = jax.nn.sigmoid(merge_a) * y_gla + jax.nn.sigmoid(merge_b) * y_attn
        x = x + gate1 * (mixed @ w_out[layer])

        h2 = rmsnorm(x, norm2_w[layer]) * (1 + scale2) + shift2
        hidden = causal_conv_ffn_hidden(h2, w_up[layer], conv_w[layer], conv_b[layer])
        x = x + gate2 * (hidden @ w_down[layer])
    return rmsnorm(x, final_norm_w)


import jax as _jax
import jax.numpy as _jnp

TWIN_FORMAT = 'train_step'
FWD_PARAMS = ['x', 'c', 'positions', 'ada_w', 'ada_b', 'norm1_w', 'w_in', 'gla_gate_w2', 'gla_gate_b', 'gla_norm_w', 'w_gla_branch', 'w_attn_branch', 'w_out', 'norm2_w', 'w_up', 'conv_w', 'conv_b', 'w_down', 'final_norm_w']
TWIN_WEIGHTS = ['ada_w', 'ada_b', 'norm1_w', 'w_in', 'gla_gate_w2', 'gla_gate_b', 'gla_norm_w', 'w_gla_branch', 'w_attn_branch', 'w_out', 'norm2_w', 'w_up', 'conv_w', 'conv_b', 'w_down', 'final_norm_w']
TWIN_DIFF_INPUT = 'x'
TWIN_INPUTS = ['x', 'c', 'positions', 'ada_w', 'ada_b', 'norm1_w', 'w_in', 'gla_gate_w2', 'gla_gate_b', 'gla_norm_w', 'w_gla_branch', 'w_attn_branch', 'w_out', 'norm2_w', 'w_up', 'conv_w', 'conv_b', 'w_down', 'final_norm_w', 'loss_target', 'm_ada_w', 'm_ada_b', 'm_norm1_w', 'm_w_in', 'm_gla_gate_w2', 'm_gla_gate_b', 'm_gla_norm_w', 'm_w_gla_branch', 'm_w_attn_branch', 'm_w_out', 'm_norm2_w', 'm_w_up', 'm_conv_w', 'm_conv_b', 'm_w_down', 'm_final_norm_w', 'v_ada_w', 'v_ada_b', 'v_norm1_w', 'v_w_in', 'v_gla_gate_w2', 'v_gla_gate_b', 'v_gla_norm_w', 'v_w_gla_branch', 'v_w_attn_branch', 'v_w_out', 'v_norm2_w', 'v_w_up', 'v_conv_w', 'v_conv_b', 'v_w_down', 'v_final_norm_w']
TWIN_OUTPUTS = ['loss', 'grad_x', 'grad_ada_w', 'grad_ada_b', 'grad_norm1_w', 'grad_w_in', 'grad_gla_gate_w2', 'grad_gla_gate_b', 'grad_gla_norm_w', 'grad_w_gla_branch', 'grad_w_attn_branch', 'grad_w_out', 'grad_norm2_w', 'grad_w_up', 'grad_conv_w', 'grad_conv_b', 'grad_w_down', 'grad_final_norm_w', 'delta_ada_w', 'delta_ada_b', 'delta_norm1_w', 'delta_w_in', 'delta_gla_gate_w2', 'delta_gla_gate_b', 'delta_gla_norm_w', 'delta_w_gla_branch', 'delta_w_attn_branch', 'delta_w_out', 'delta_norm2_w', 'delta_w_up', 'delta_conv_w', 'delta_conv_b', 'delta_w_down', 'delta_final_norm_w', 'new_m_ada_w', 'new_m_ada_b', 'new_m_norm1_w', 'new_m_w_in', 'new_m_gla_gate_w2', 'new_m_gla_gate_b', 'new_m_gla_norm_w', 'new_m_w_gla_branch', 'new_m_w_attn_branch', 'new_m_w_out', 'new_m_norm2_w', 'new_m_w_up', 'new_m_conv_w', 'new_m_conv_b', 'new_m_w_down', 'new_m_final_norm_w', 'new_v_ada_w', 'new_v_ada_b', 'new_v_norm1_w', 'new_v_w_in', 'new_v_gla_gate_w2', 'new_v_gla_gate_b', 'new_v_gla_norm_w', 'new_v_w_gla_branch', 'new_v_w_attn_branch', 'new_v_w_out', 'new_v_norm2_w', 'new_v_w_up', 'new_v_conv_w', 'new_v_conv_b', 'new_v_w_down', 'new_v_final_norm_w']
TWIN_LEAF_KINDS = {'loss': 'loss', 'grad_x': 'grad_x', 'grad_ada_w': 'grad_w', 'grad_ada_b': 'grad_w', 'grad_norm1_w': 'grad_w', 'grad_w_in': 'grad_w', 'grad_gla_gate_w2': 'grad_w', 'grad_gla_gate_b': 'grad_w', 'grad_gla_norm_w': 'grad_w', 'grad_w_gla_branch': 'grad_w', 'grad_w_attn_branch': 'grad_w', 'grad_w_out': 'grad_w', 'grad_norm2_w': 'grad_w', 'grad_w_up': 'grad_w', 'grad_conv_w': 'grad_w', 'grad_conv_b': 'grad_w', 'grad_w_down': 'grad_w', 'grad_final_norm_w': 'grad_w', 'delta_ada_w': 'delta_w', 'delta_ada_b': 'delta_w', 'delta_norm1_w': 'delta_w', 'delta_w_in': 'delta_w', 'delta_gla_gate_w2': 'delta_w', 'delta_gla_gate_b': 'delta_w', 'delta_gla_norm_w': 'delta_w', 'delta_w_gla_branch': 'delta_w', 'delta_w_attn_branch': 'delta_w', 'delta_w_out': 'delta_w', 'delta_norm2_w': 'delta_w', 'delta_w_up': 'delta_w', 'delta_conv_w': 'delta_w', 'delta_conv_b': 'delta_w', 'delta_w_down': 'delta_w', 'delta_final_norm_w': 'delta_w', 'new_m_ada_w': 'new_m', 'new_m_ada_b': 'new_m', 'new_m_norm1_w': 'new_m', 'new_m_w_in': 'new_m', 'new_m_gla_gate_w2': 'new_m', 'new_m_gla_gate_b': 'new_m', 'new_m_gla_norm_w': 'new_m', 'new_m_w_gla_branch': 'new_m', 'new_m_w_attn_branch': 'new_m', 'new_m_w_out': 'new_m', 'new_m_norm2_w': 'new_m', 'new_m_w_up': 'new_m', 'new_m_conv_w': 'new_m', 'new_m_conv_b': 'new_m', 'new_m_w_down': 'new_m', 'new_m_final_norm_w': 'new_m', 'new_v_ada_w': 'new_v', 'new_v_ada_b': 'new_v', 'new_v_norm1_w': 'new_v', 'new_v_w_in': 'new_v', 'new_v_gla_gate_w2': 'new_v', 'new_v_gla_gate_b': 'new_v', 'new_v_gla_norm_w': 'new_v', 'new_v_w_gla_branch': 'new_v', 'new_v_w_attn_branch': 'new_v', 'new_v_w_out': 'new_v', 'new_v_norm2_w': 'new_v', 'new_v_w_up': 'new_v', 'new_v_conv_w': 'new_v', 'new_v_conv_b': 'new_v', 'new_v_w_down': 'new_v', 'new_v_final_norm_w': 'new_v'}


def _forward(args):
    return _fwd_reference(*[args[k] for k in FWD_PARAMS])


def _output_shape():
    def fwd():
        inp = _fwd_setup_inputs(0)
        return _fwd_reference(*[inp[k] for k in FWD_PARAMS])
    out = _jax.eval_shape(fwd)
    return out.shape, out.dtype

N_MICROBATCH = 1
ADAM_LR = 0.001
ADAM_B1 = 0.9
ADAM_B2 = 0.999
ADAM_EPS = 1e-08
ADAM_WD = 0.01
ADAM_STEP = 10
PER_EXAMPLE_BATCH_AXIS = {'x': 0, 'c': 0, 'positions': 0, 'loss_target': 0}
SHARED_INPUTS = []
_WEIGHT_DTYPES = {'ada_w': _jnp.float32, 'ada_b': _jnp.float32, 'norm1_w': _jnp.float32, 'w_in': _jnp.float32, 'gla_gate_w2': _jnp.float32, 'gla_gate_b': _jnp.float32, 'gla_norm_w': _jnp.float32, 'w_gla_branch': _jnp.float32, 'w_attn_branch': _jnp.float32, 'w_out': _jnp.float32, 'norm2_w': _jnp.float32, 'w_up': _jnp.float32, 'conv_w': _jnp.float32, 'conv_b': _jnp.float32, 'w_down': _jnp.float32, 'final_norm_w': _jnp.float32}
MOMENT_SCALE = {'ada_w': 8.559375e-02, 'ada_b': 1.509986e-01, 'norm1_w': 1.102259e-01, 'w_in': 4.998684e-02, 'gla_gate_w2': 3.615769e-02, 'gla_gate_b': 6.134259e-02, 'gla_norm_w': 1.094600e-01, 'w_gla_branch': 5.208891e-02, 'w_attn_branch': 3.646734e-02, 'w_out': 6.317616e-02, 'norm2_w': 1.309925e-01, 'w_up': 6.194589e-02, 'conv_w': 6.165750e-02, 'conv_b': 4.654150e-02, 'w_down': 1.020602e-01, 'final_norm_w': 3.250240e+01}


def _to_microbatches(a, axis):
    t = _jnp.moveaxis(a, axis, 0)
    t = t.reshape((N_MICROBATCH, t.shape[0] // N_MICROBATCH) + t.shape[1:])
    return _jnp.moveaxis(t, 1, axis + 1)


def setup_inputs(seed: int = 0) -> dict:
    inp = _fwd_setup_inputs(seed)
    key = _jax.random.fold_in(_jax.random.key(seed), 7919)
    shape, _ = _output_shape()
    out = dict(inp)
    out["loss_target"] = _jax.random.normal(_jax.random.fold_in(key, 0), shape, _jnp.float32)
    for i, name in enumerate(TWIN_WEIGHTS):
        w = inp[name].astype(_jnp.float32)
        if MOMENT_SCALE is None:
            s = _jnp.sqrt(_jnp.mean(_jnp.square(w)) + 1e-30)
        else:
            s = MOMENT_SCALE[name]
        km, kv = _jax.random.split(_jax.random.fold_in(key, i + 1))
        out[name] = w
        out["m_" + name] = s * _jax.random.normal(km, w.shape, _jnp.float32)
        out["v_" + name] = (s * s) * _jax.random.uniform(kv, w.shape, _jnp.float32, 0.5, 1.5)
    if N_MICROBATCH > 1:
        for name, axis in PER_EXAMPLE_BATCH_AXIS.items():
            out[name] = _to_microbatches(out[name], axis)
    return {'x': out['x'], 'c': out['c'], 'positions': out['positions'], 'ada_w': out['ada_w'], 'ada_b': out['ada_b'], 'norm1_w': out['norm1_w'], 'w_in': out['w_in'], 'gla_gate_w2': out['gla_gate_w2'], 'gla_gate_b': out['gla_gate_b'], 'gla_norm_w': out['gla_norm_w'], 'w_gla_branch': out['w_gla_branch'], 'w_attn_branch': out['w_attn_branch'], 'w_out': out['w_out'], 'norm2_w': out['norm2_w'], 'w_up': out['w_up'], 'conv_w': out['conv_w'], 'conv_b': out['conv_b'], 'w_down': out['w_down'], 'final_norm_w': out['final_norm_w'], 'loss_target': out['loss_target'], 'm_ada_w': out['m_ada_w'], 'm_ada_b': out['m_ada_b'], 'm_norm1_w': out['m_norm1_w'], 'm_w_in': out['m_w_in'], 'm_gla_gate_w2': out['m_gla_gate_w2'], 'm_gla_gate_b': out['m_gla_gate_b'], 'm_gla_norm_w': out['m_gla_norm_w'], 'm_w_gla_branch': out['m_w_gla_branch'], 'm_w_attn_branch': out['m_w_attn_branch'], 'm_w_out': out['m_w_out'], 'm_norm2_w': out['m_norm2_w'], 'm_w_up': out['m_w_up'], 'm_conv_w': out['m_conv_w'], 'm_conv_b': out['m_conv_b'], 'm_w_down': out['m_w_down'], 'm_final_norm_w': out['m_final_norm_w'], 'v_ada_w': out['v_ada_w'], 'v_ada_b': out['v_ada_b'], 'v_norm1_w': out['v_norm1_w'], 'v_w_in': out['v_w_in'], 'v_gla_gate_w2': out['v_gla_gate_w2'], 'v_gla_gate_b': out['v_gla_gate_b'], 'v_gla_norm_w': out['v_gla_norm_w'], 'v_w_gla_branch': out['v_w_gla_branch'], 'v_w_attn_branch': out['v_w_attn_branch'], 'v_w_out': out['v_w_out'], 'v_norm2_w': out['v_norm2_w'], 'v_w_up': out['v_w_up'], 'v_conv_w': out['v_conv_w'], 'v_conv_b': out['v_conv_b'], 'v_w_down': out['v_w_down'], 'v_final_norm_w': out['v_final_norm_w']}


def _loss(weights, diff, rest, loss_target):
    with _jax.named_scope("forward"):
        args = {**rest, TWIN_DIFF_INPUT: diff, **{k: w.astype(_WEIGHT_DTYPES[k]) for k, w in weights.items()}}
        y = _forward(args)
    with _jax.named_scope("loss_head"):
        err = _jnp.square(y.astype(_jnp.float32) - loss_target)
        return 0.5 * _jnp.sum(_jnp.mean(err, axis=-1)) if err.ndim else 0.5 * err


def _adamw(w, g, m, v):
    m = ADAM_B1 * m + (1.0 - ADAM_B1) * g
    v = ADAM_B2 * v + (1.0 - ADAM_B2) * _jnp.square(g)
    m_hat = m / (1.0 - ADAM_B1 ** ADAM_STEP)
    v_hat = v / (1.0 - ADAM_B2 ** ADAM_STEP)
    delta = -ADAM_LR * (m_hat / (_jnp.sqrt(v_hat) + ADAM_EPS) + ADAM_WD * w)
    return delta, m, v


def reference(x, c, positions, ada_w, ada_b, norm1_w, w_in, gla_gate_w2, gla_gate_b, gla_norm_w, w_gla_branch, w_attn_branch, w_out, norm2_w, w_up, conv_w, conv_b, w_down, final_norm_w, loss_target, m_ada_w, m_ada_b, m_norm1_w, m_w_in, m_gla_gate_w2, m_gla_gate_b, m_gla_norm_w, m_w_gla_branch, m_w_attn_branch, m_w_out, m_norm2_w, m_w_up, m_conv_w, m_conv_b, m_w_down, m_final_norm_w, v_ada_w, v_ada_b, v_norm1_w, v_w_in, v_gla_gate_w2, v_gla_gate_b, v_gla_norm_w, v_w_gla_branch, v_w_attn_branch, v_w_out, v_norm2_w, v_w_up, v_conv_w, v_conv_b, v_w_down, v_final_norm_w):
    given = dict(x=x, c=c, positions=positions, ada_w=ada_w, ada_b=ada_b, norm1_w=norm1_w, w_in=w_in, gla_gate_w2=gla_gate_w2, gla_gate_b=gla_gate_b, gla_norm_w=gla_norm_w, w_gla_branch=w_gla_branch, w_attn_branch=w_attn_branch, w_out=w_out, norm2_w=norm2_w, w_up=w_up, conv_w=conv_w, conv_b=conv_b, w_down=w_down, final_norm_w=final_norm_w, loss_target=loss_target, m_ada_w=m_ada_w, m_ada_b=m_ada_b, m_norm1_w=m_norm1_w, m_w_in=m_w_in, m_gla_gate_w2=m_gla_gate_w2, m_gla_gate_b=m_gla_gate_b, m_gla_norm_w=m_gla_norm_w, m_w_gla_branch=m_w_gla_branch, m_w_attn_branch=m_w_attn_branch, m_w_out=m_w_out, m_norm2_w=m_norm2_w, m_w_up=m_w_up, m_conv_w=m_conv_w, m_conv_b=m_conv_b, m_w_down=m_w_down, m_final_norm_w=m_final_norm_w, v_ada_w=v_ada_w, v_ada_b=v_ada_b, v_norm1_w=v_norm1_w, v_w_in=v_w_in, v_gla_gate_w2=v_gla_gate_w2, v_gla_gate_b=v_gla_gate_b, v_gla_norm_w=v_gla_norm_w, v_w_gla_branch=v_w_gla_branch, v_w_attn_branch=v_w_attn_branch, v_w_out=v_w_out, v_norm2_w=v_norm2_w, v_w_up=v_w_up, v_conv_w=v_conv_w, v_conv_b=v_conv_b, v_w_down=v_w_down, v_final_norm_w=v_final_norm_w)
    weights = {n: given[n] for n in TWIN_WEIGHTS}
    shared = {n: given[n] for n in SHARED_INPUTS}
    per_example = {n: given[n] for n in ['x', 'c', 'positions']}
    grad_fn = _jax.value_and_grad(_loss, argnums=(0, 1))

    def one_microbatch(ex, loss_target):
        ex = dict(ex)
        diff = ex.pop(TWIN_DIFF_INPUT)
        return grad_fn(weights, diff, {**shared, **ex}, loss_target)

    if N_MICROBATCH == 1:
        loss, (grad_w, grad_x) = one_microbatch(per_example, given["loss_target"])
    else:
        def body(carry, xs):
            loss_sum, grad_sum = carry
            l_k, (gw_k, gx_k) = one_microbatch(xs[0], xs[1])
            with _jax.named_scope("update"):
                return (loss_sum + l_k, _jax.tree.map(_jnp.add, grad_sum, gw_k)), gx_k

        init = (_jnp.zeros((), _jnp.float32), _jax.tree.map(_jnp.zeros_like, weights))
        (loss, grad_w), grad_x = _jax.lax.scan(body, init, (per_example, given["loss_target"]))
    with _jax.named_scope("update"):
        delta_w, new_m, new_v = {}, {}, {}
        for n in TWIN_WEIGHTS:
            delta_w[n], new_m[n], new_v[n] = _adamw(weights[n], grad_w[n], given["m_" + n], given["v_" + n])
    return (loss, grad_x, *[grad_w[n] for n in TWIN_WEIGHTS], *[delta_w[n] for n in TWIN_WEIGHTS],
            *[new_m[n] for n in TWIN_WEIGHTS], *[new_v[n] for n in TWIN_WEIGHTS])
```

```python
import math

import jax
import jax.numpy as jnp
from jax import lax
from jax.experimental import pallas as pl
from jax.experimental.pallas import tpu as pltpu

F32, BF16 = jnp.float32, jnp.bfloat16
MESH = pl.DeviceIdType.MESH

D = 1024
EPS = 1e-6
GLA_H, GLA_DK, GLA_DV, GLA_LR = 4, 128, 256, 16
GLA_TAU = 16.0
GLA_CHUNK = 64
GLA_BLOCK = 512
ATT_GROUPS = ((128, 1), (512, 4), (2048, 16))
ATT_BLK = 128
ATT_HD = 64
ATT_W = 768
D_FF = 2816
ROPE_THETA = 10000.0
P_W = 7680
P_GQ, P_GK, P_GV, P_GR, P_AQ, P_AK, P_AV, P_MA, P_MB, P_LR = 0, 512, 1024, 2048, 3072, 3840, 4608, 5376, 6400, 7424
W_IN = 7440
W_IN_SH, W_UP_SH, W_DOWN_SH = 1860, 1408, 704
PK_SIZES = (1024 * W_IN_SH, 256 * 1024, 256 * 256, 256 * 1024, 1024 * W_UP_SH, W_DOWN_SH * 1024)
PK_ROWS = 4608
PK_HALF = PK_ROWS // 2
VMEM_LIMIT = 56 * 1024 * 1024
ADAM_LR, ADAM_B1, ADAM_B2, ADAM_EPS, ADAM_WD, ADAM_STEP = 0.001, 0.9, 0.999, 1e-08, 0.01, 10
NEG = -1e30


def _tile(n, target, unit=128):
    best = None
    for t in range(unit, min(n, target) + 1, unit):
        if n % t == 0:
            best = t
    return best or n


def _params(sem):
    return pltpu.CompilerParams(dimension_semantics=sem, vmem_limit_bytes=VMEM_LIMIT)


def _dg(a, b, ca, cb):
    return lax.dot_general(a, b, (((ca,), (cb,)), ((), ())), preferred_element_type=F32)


def _sigmoid(v):
    return 1.0 / (1.0 + jnp.exp(-v))


def _mm(a, b, name, *, ta=False, tb=False, out_dtype=BF16, tm=1024, tn=1536, tk=1024, n_outer=True):
    m = a.shape[1] if ta else a.shape[0]
    k = a.shape[0] if ta else a.shape[1]
    n = b.shape[0] if tb else b.shape[1]
    tm, tn, tk = _tile(m, tm), _tile(n, tn), _tile(k, tk)
    nm, nn, nk = m // tm, n // tn, k // tk
    in_out = out_dtype == F32

    def body(a_ref, b_ref, o_ref, *scr):
        kk = pl.program_id(2)
        p = _dg(a_ref[...].astype(BF16), b_ref[...].astype(BF16), 0 if ta else 1, 1 if tb else 0)
        if nk == 1:
            o_ref[...] = p.astype(o_ref.dtype)
        else:
            acc = o_ref if in_out else scr[0]

            @pl.when(kk == 0)
            def _():
                acc[...] = p

            @pl.when(kk > 0)
            def _():
                acc[...] += p

            if not in_out:
                @pl.when(kk == nk - 1)
                def _():
                    o_ref[...] = acc[...].astype(o_ref.dtype)

    if n_outer:
        ij = lambda g0, g1: (g1, g0)
        grid = (nn, nm, nk)
    else:
        ij = lambda g0, g1: (g0, g1)
        grid = (nm, nn, nk)
    a_map = (lambda g0, g1, kk: (kk, ij(g0, g1)[0])) if ta else (lambda g0, g1, kk: (ij(g0, g1)[0], kk))
    b_map = (lambda g0, g1, kk: (ij(g0, g1)[1], kk)) if tb else (lambda g0, g1, kk: (kk, ij(g0, g1)[1]))
    return pl.pallas_call(
        body, name=name, grid=grid,
        in_specs=[pl.BlockSpec((tk, tm) if ta else (tm, tk), a_map),
                  pl.BlockSpec((tn, tk) if tb else (tk, tn), b_map)],
        out_specs=pl.BlockSpec((tm, tn), lambda g0, g1, kk: ij(g0, g1)),
        out_shape=jax.ShapeDtypeStruct((m, n), out_dtype),
        scratch_shapes=[] if (in_out or nk == 1) else [pltpu.VMEM((tm, tn), F32)],
        compiler_params=_params(("parallel", "parallel", "arbitrary")),
    )(a, b)


def _rows(arr, rb, w=None, j=0):
    w = arr.shape[1] if w is None else w
    if callable(j):
        return arr, pl.BlockSpec((rb, w), lambda c, i: (i, j(c)))
    return arr, pl.BlockSpec((rb, w), lambda c, i: (i, j))


def _full(arr, w=None, j=0):
    w = arr.shape[1] if w is None else w
    if callable(j):
        return arr, pl.BlockSpec((arr.shape[0], w), lambda c, i: (0, j(c)))
    return arr, pl.BlockSpec((arr.shape[0], w), lambda c, i: (0, j))


def _halo(arr, rb, hb, w, j, before):
    per = rb // hb
    last = arr.shape[0] // hb - 1
    if before:
        rmap = lambda i: jnp.maximum(i * per - 1, 0)
    else:
        rmap = lambda i: jnp.minimum((i + 1) * per, last)
    return arr, pl.BlockSpec((hb, w), lambda c, i: (rmap(i), j(c) if callable(j) else j))


def _rowcall(fn, ins, outs, *, n_rows, rb, name, ncol=1):
    n_in = len(ins)
    nr = n_rows // rb

    def body(*refs):
        c, i = pl.program_id(0), pl.program_id(1)
        res = fn(c, i, *[r[...] for r in refs[:n_in]])
        for val, spec, o_ref in zip(res, outs, refs[n_in:]):
            if spec[2] == "row":
                o_ref[...] = val.astype(o_ref.dtype)
            else:
                @pl.when(i == 0)
                def _(o_ref=o_ref, val=val):
                    o_ref[...] = val.astype(o_ref.dtype)

                @pl.when(i > 0)
                def _(o_ref=o_ref, val=val):
                    o_ref[...] += val.astype(o_ref.dtype)

    out_specs = []
    for shape, dt, kind, block, col in outs:
        if kind == "row":
            out_specs.append(pl.BlockSpec(block, lambda c, i, col=col: (i, col(c))))
        else:
            out_specs.append(pl.BlockSpec(block, lambda c, i, col=col: (0, col(c))))
    return pl.pallas_call(
        body, name=name, grid=(ncol, nr),
        in_specs=[s for _, s in ins], out_specs=out_specs,
        out_shape=[jax.ShapeDtypeStruct(o[0], o[1]) for o in outs],
        compiler_params=_params(("parallel", "arbitrary")),
    )(*[a for a, _ in ins])


def _orow(n_rows, w, dt, rb, bw=None, col=lambda c: 0):
    return ((n_rows, w), dt, "row", (rb, bw or w), col)


def _oacc(r, w, bw=None, col=lambda c: 0):
    return ((r, w), F32, "acc", (r, bw or w), col)


def _csum(v):
    return jnp.sum(v, axis=0, keepdims=True)


def _rms(v):
    return lax.rsqrt(jnp.mean(v * v, axis=-1, keepdims=True) + EPS)


def _norm_bwd(xv, dh, w, scale):
    r = _rms(xv)
    xh = xv * r
    dxh = dh * (w * (1.0 + scale))
    dx = r * (dxh - xh * jnp.mean(dxh * xh, axis=-1, keepdims=True))
    t = dh * xh
    return dx, _csum(dh), _csum(t * w), _csum(t * (1.0 + scale))


def _rope_tables(pos_col, invf, s):
    def fn(c, i, pos, f):
        ang = pos.astype(F32) * f
        lane = lax.broadcasted_iota(jnp.int32, ang.shape, 1)
        sign = jnp.where((lane % ATT_HD) < ATT_HD // 2, -1.0, 1.0)
        return jnp.cos(ang), jnp.sin(ang) * sign

    rb = 512
    return _rowcall(fn, [_rows(pos_col, rb), _full(invf)], [_orow(s, 128, F32, rb), _orow(s, 128, F32, rb)],
                    n_rows=s, rb=rb, name="rope_tables")


def _swap_halves(t):
    n = t.shape[1]
    lane = lax.broadcasted_iota(jnp.int32, t.shape, 1)
    return jnp.where((lane % ATT_HD) < ATT_HD // 2, pltpu.roll(t, n - 32, 1), pltpu.roll(t, 32, 1))


def _rope_apply(t, cos, sin_signed, inverse):
    cw = jnp.concatenate([cos] * (t.shape[1] // 128), axis=1)
    sw = jnp.concatenate([sin_signed] * (t.shape[1] // 128), axis=1)
    if inverse:
        sw = -sw
    return t * cw + _swap_halves(t) * sw


def _gla_decays(la_c, tri):
    b = jnp.dot(tri, la_c, precision=lax.Precision.HIGHEST, preferred_element_type=F32)
    row = lax.broadcasted_iota(jnp.int32, b.shape, 0)
    bmid = jnp.sum(jnp.where(row == GLA_CHUNK // 2 - 1, b, 0.0), axis=0, keepdims=True)
    blast = jnp.sum(jnp.where(row == GLA_CHUNK - 1, b, 0.0), axis=0, keepdims=True)
    return b, bmid, blast


def _gla_fwd(p, la, s):
    tb, ch = GLA_BLOCK, GLA_CHUNK
    nb, nc = s // tb, tb // ch
    scale = GLA_DK ** -0.5

    def body(q_ref, k_ref, v_ref, la_ref, o_ref, st_ref, state):
        @pl.when(pl.program_id(1) == 0)
        def _():
            state[...] = jnp.zeros_like(state)

        ri = lax.broadcasted_iota(jnp.int32, (ch, ch), 0)
        ci = lax.broadcasted_iota(jnp.int32, (ch, ch), 1)
        causal = ci <= ri
        tri = causal.astype(F32)
        for c in range(nc):
            sl = pl.ds(c * ch, ch)
            b, bmid, blast = _gla_decays(la_ref[sl, :], tri)
            q = q_ref[sl, :].astype(F32) * scale
            k = k_ref[sl, :].astype(F32)
            v = v_ref[sl, :]
            qgt = (q * jnp.exp(b)).astype(BF16)
            qgn = (q * jnp.exp(b - bmid)).astype(BF16)
            kgn = (k * jnp.exp(bmid - b)).astype(BF16)
            kd = (k * jnp.exp(blast - b)).astype(BF16)
            a = jnp.where(causal, _dg(qgn, kgn, 1, 1), 0.0)
            st = state[...]
            st_ref[0, c] = st
            o_ref[sl, :] = _dg(a.astype(BF16), v, 1, 0) + _dg(qgt, st.astype(BF16), 1, 1)
            state[...] = jnp.exp(blast) * st + _dg(v, kd, 0, 0)

    return pl.pallas_call(
        body, name="gla_fwd", grid=(GLA_H, nb),
        in_specs=[pl.BlockSpec((tb, GLA_DK), lambda h, t: (t, P_GQ // GLA_DK + h)),
                  pl.BlockSpec((tb, GLA_DK), lambda h, t: (t, P_GK // GLA_DK + h)),
                  pl.BlockSpec((tb, GLA_DV), lambda h, t: (t, P_GV // GLA_DV + h)),
                  pl.BlockSpec((tb, GLA_DK), lambda h, t: (t, h))],
        out_specs=[pl.BlockSpec((tb, GLA_DV), lambda h, t: (t, h)),
                   pl.BlockSpec((1, nc, GLA_DV, GLA_DK), lambda h, t: (h, t, 0, 0))],
        out_shape=[jax.ShapeDtypeStruct((s, GLA_H * GLA_DV), F32),
                   jax.ShapeDtypeStruct((GLA_H, s // ch, GLA_DV, GLA_DK), F32)],
        scratch_shapes=[pltpu.VMEM((GLA_DV, GLA_DK), F32)],
        compiler_params=_params(("parallel", "arbitrary")),
    )(p, p, p, la)


def _gla_bwd(p, la, states, do, s):
    tb, ch = GLA_BLOCK, GLA_CHUNK
    nb, nc = s // tb, tb // ch
    scale = GLA_DK ** -0.5

    def body(q_ref, k_ref, v_ref, la_ref, st_ref, do_ref, dq_ref, dk_ref, dv_ref, dla_ref, dstate):
        @pl.when(pl.program_id(1) == 0)
        def _():
            dstate[...] = jnp.zeros_like(dstate)

        ri = lax.broadcasted_iota(jnp.int32, (ch, ch), 0)
        ci = lax.broadcasted_iota(jnp.int32, (ch, ch), 1)
        causal = ci <= ri
        tri = causal.astype(F32)
        tri_t = (ci >= ri).astype(F32)
        for c in reversed(range(nc)):
            sl = pl.ds(c * ch, ch)
            b, bmid, blast = _gla_decays(la_ref[sl, :], tri)
            q = q_ref[sl, :].astype(F32) * scale
            k = k_ref[sl, :].astype(F32)
            v = v_ref[sl, :]
            e_b, e_qn, e_kn, e_kd = jnp.exp(b), jnp.exp(b - bmid), jnp.exp(bmid - b), jnp.exp(blast - b)
            dec = jnp.exp(blast)
            qgt, qgn, kgn, kd = q * e_b, q * e_qn, k * e_kn, k * e_kd
            qgt_b, qgn_b, kgn_b, kd_b = qgt.astype(BF16), qgn.astype(BF16), kgn.astype(BF16), kd.astype(BF16)
            st0 = st_ref[0, c]
            dst = dstate[...]
            dst_b = dst.astype(BF16)
            do_b = do_ref[sl, :].astype(BF16)
            a = jnp.where(causal, _dg(qgn_b, kgn_b, 1, 1), 0.0).astype(BF16)
            da = jnp.where(causal, _dg(do_b, v, 1, 1), 0.0).astype(BF16)
            dqgn = _dg(da, kgn_b, 1, 0)
            dqgt = _dg(do_b, st0.astype(BF16), 1, 0)
            dkgn = _dg(da, qgn_b, 0, 0)
            dv = _dg(a, do_b, 0, 0) + _dg(kd_b, dst_b, 1, 1)
            dkd = _dg(v, dst_b, 1, 0)
            ddec = jnp.sum(st0 * dst, axis=0, keepdims=True)
            dstate[...] = dec * dst + _dg(do_b, qgt_b, 0, 0)
            dq_ref[sl, :] = (scale * (dqgn * e_qn + dqgt * e_b)).astype(dq_ref.dtype)
            dk_ref[sl, :] = (dkgn * e_kn + dkd * e_kd).astype(dk_ref.dtype)
            dv_ref[sl, :] = dv.astype(dv_ref.dtype)
            db = dqgn * qgn + dqgt * qgt - dkgn * kgn - dkd * kd
            extra = jnp.sum(dkd * kd, axis=0, keepdims=True) + ddec * dec
            dla_ref[sl, :] = jnp.dot(tri_t, db, precision=lax.Precision.HIGHEST, preferred_element_type=F32) + extra

    rev = lambda t: nb - 1 - t
    return pl.pallas_call(
        body, name="gla_bwd", grid=(GLA_H, nb),
        in_specs=[pl.BlockSpec((tb, GLA_DK), lambda h, t: (rev(t), P_GQ // GLA_DK + h)),
                  pl.BlockSpec((tb, GLA_DK), lambda h, t: (rev(t), P_GK // GLA_DK + h)),
                  pl.BlockSpec((tb, GLA_DV), lambda h, t: (rev(t), P_GV // GLA_DV + h)),
                  pl.BlockSpec((tb, GLA_DK), lambda h, t: (rev(t), h)),
                  pl.BlockSpec((1, nc, GLA_DV, GLA_DK), lambda h, t: (h, rev(t), 0, 0)),
                  pl.BlockSpec((tb, GLA_DV), lambda h, t: (rev(t), h))],
        out_specs=[pl.BlockSpec((tb, GLA_DK), lambda h, t: (rev(t), h)),
                   pl.BlockSpec((tb, GLA_DK), lambda h, t: (rev(t), h)),
                   pl.BlockSpec((tb, GLA_DV), lambda h, t: (rev(t), h)),
                   pl.BlockSpec((tb, GLA_DK), lambda h, t: (rev(t), h))],
        out_shape=[jax.ShapeDtypeStruct((s, GLA_H * GLA_DK), BF16),
                   jax.ShapeDtypeStruct((s, GLA_H * GLA_DK), BF16),
                   jax.ShapeDtypeStruct((s, GLA_H * GLA_DV), BF16),
                   jax.ShapeDtypeStruct((s, GLA_H * GLA_DK), F32)],
        scratch_shapes=[pltpu.VMEM((GLA_DV, GLA_DK), F32)],
        compiler_params=_params(("parallel", "arbitrary")),
    )(p, p, p, la, states, do)


def _head_masks():
    lane = lax.broadcasted_iota(jnp.int32, (1, 4 * ATT_HD), 1)
    return [(lane >= h * ATT_HD) & (lane < (h + 1) * ATT_HD) for h in range(4)]


def _attn_fwd(qrot, krot, p, g, r, s):
    ln = s // r
    nblk = ln // ATT_BLK
    qv, kv, pv = qrot.reshape(ln, r * ATT_W), krot.reshape(ln, r * ATT_W), p.reshape(ln, r * P_W)
    qcol = lambda pr: pr * 3 + g
    vcol = lambda pr: pr * (P_W // 256) + P_AV // 256 + g
    prev = lambda n: jnp.maximum(n - 1, 0)

    def body(q_ref, kp_ref, kc_ref, vp_ref, vc_ref, o_ref, lse_ref):
        has_prev = pl.program_id(1) > 0
        ri = lax.broadcasted_iota(jnp.int32, (ATT_BLK, ATT_BLK), 0)
        ci = lax.broadcasted_iota(jnp.int32, (ATT_BLK, ATT_BLK), 1)
        m_cur = ci <= ri
        m_prev = (ci >= ri) & has_prev
        q, kp, kc, vp, vc = q_ref[...], kp_ref[...], kc_ref[...], vp_ref[...], vc_ref[...]
        o = jnp.zeros((ATT_BLK, 256), F32)
        lse = jnp.zeros((ATT_BLK, 256), F32)
        for hm in _head_masks():
            qm = jnp.where(hm, q, jnp.zeros_like(q))
            sc = jnp.where(m_cur, _dg(qm, kc, 1, 1) * 0.125, NEG)
            sp = jnp.where(m_prev, _dg(qm, kp, 1, 1) * 0.125, NEG)
            mx = jnp.maximum(jnp.max(sc, axis=1, keepdims=True), jnp.max(sp, axis=1, keepdims=True))
            pc, pp = jnp.exp(sc - mx), jnp.exp(sp - mx)
            den = jnp.sum(pc, axis=1, keepdims=True) + jnp.sum(pp, axis=1, keepdims=True)
            oh = (_dg(pc.astype(BF16), vc, 1, 0) + _dg(pp.astype(BF16), vp, 1, 0)) / den
            o = jnp.where(hm, oh, o)
            lse = jnp.where(hm, mx + jnp.log(den), lse)
        o_ref[...] = o.astype(o_ref.dtype)
        lse_ref[...] = lse

    blk = (ATT_BLK, 256)
    o, lse = pl.pallas_call(
        body, name=f"attn_fwd_{g}", grid=(r, nblk),
        in_specs=[pl.BlockSpec(blk, lambda pr, n: (n, qcol(pr))),
                  pl.BlockSpec(blk, lambda pr, n: (prev(n), qcol(pr))),
                  pl.BlockSpec(blk, lambda pr, n: (n, qcol(pr))),
                  pl.BlockSpec(blk, lambda pr, n: (prev(n), vcol(pr))),
                  pl.BlockSpec(blk, lambda pr, n: (n, vcol(pr)))],
        out_specs=[pl.BlockSpec(blk, lambda pr, n: (n, pr)), pl.BlockSpec(blk, lambda pr, n: (n, pr))],
        out_shape=[jax.ShapeDtypeStruct((ln, r * 256), BF16), jax.ShapeDtypeStruct((ln, r * 256), F32)],
        compiler_params=_params(("parallel", "parallel")),
    )(qv, kv, kv, pv, pv)
    return o.reshape(s, 256), lse.reshape(s, 256)


def _attn_bwd(qrot, krot, p, do, o, lse, g, r, s):
    ln = s // r
    nblk = ln // ATT_BLK
    qv, kv, pv = qrot.reshape(ln, r * ATT_W), krot.reshape(ln, r * ATT_W), p.reshape(ln, r * P_W)
    dov, ov, lv = do.reshape(ln, r * 256), o.reshape(ln, r * 256), lse.reshape(ln, r * 256)
    qcol = lambda pr: pr * 3 + g
    vcol = lambda pr: pr * (P_W // 256) + P_AV // 256 + g
    prev = lambda n: jnp.maximum(n - 1, 0)
    nxt = lambda n: jnp.minimum(n + 1, nblk - 1)

    def body(qc_ref, qn_ref, kp_ref, kc_ref, vp_ref, vc_ref, doc_ref, don_ref, oc_ref, on_ref, lc_ref, ln_ref,
             dq_ref, dk_ref, dv_ref):
        n = pl.program_id(1)
        has_prev, has_next = n > 0, n < nblk - 1
        ri = lax.broadcasted_iota(jnp.int32, (ATT_BLK, ATT_BLK), 0)
        ci = lax.broadcasted_iota(jnp.int32, (ATT_BLK, ATT_BLK), 1)
        m_cur = ci <= ri
        m_prev = (ci >= ri) & has_prev
        m_next = (ci >= ri) & has_next
        qc, qn, kp, kc, vp, vc = qc_ref[...], qn_ref[...], kp_ref[...], kc_ref[...], vp_ref[...], vc_ref[...]
        doc, don = doc_ref[...], don_ref[...]
        pc_full = doc.astype(F32) * oc_ref[...].astype(F32)
        pn_full = don.astype(F32) * on_ref[...].astype(F32)
        lc, lnx = lc_ref[...], ln_ref[...]
        dq = jnp.zeros((ATT_BLK, 256), F32)
        dk = jnp.zeros((ATT_BLK, 256), F32)
        dv = jnp.zeros((ATT_BLK, 256), F32)
        zb = jnp.zeros_like(qc)
        for hm in _head_masks():
            qcm, qnm = jnp.where(hm, qc, zb), jnp.where(hm, qn, zb)
            docm, donm = jnp.where(hm, doc, zb), jnp.where(hm, don, zb)
            lse_c = jnp.max(jnp.where(hm, lc, NEG), axis=1, keepdims=True)
            lse_n = jnp.max(jnp.where(hm, lnx, NEG), axis=1, keepdims=True)
            del_c = jnp.sum(jnp.where(hm, pc_full, 0.0), axis=1, keepdims=True)
            del_n = jnp.sum(jnp.where(hm, pn_full, 0.0), axis=1, keepdims=True)
            pr_ = jnp.where(m_cur, jnp.exp(_dg(qcm, kc, 1, 1) * 0.125 - lse_c), 0.0)
            ds = (pr_ * (_dg(docm, vc, 1, 1) - del_c) * 0.125).astype(BF16)
            dqh = _dg(ds, kc, 1, 0)
            dkh = _dg(ds, qc, 0, 0)
            dvh = _dg(pr_.astype(BF16), doc, 0, 0)
            pr_ = jnp.where(m_prev, jnp.exp(_dg(qcm, kp, 1, 1) * 0.125 - lse_c), 0.0)
            ds = (pr_ * (_dg(docm, vp, 1, 1) - del_c) * 0.125).astype(BF16)
            dqh = dqh + _dg(ds, kp, 1, 0)
            pr_ = jnp.where(m_next, jnp.exp(_dg(qnm, kc, 1, 1) * 0.125 - lse_n), 0.0)
            ds = (pr_ * (_dg(donm, vc, 1, 1) - del_n) * 0.125).astype(BF16)
            dkh = dkh + _dg(ds, qn, 0, 0)
            dvh = dvh + _dg(pr_.astype(BF16), don, 0, 0)
            dq = jnp.where(hm, dqh, dq)
            dk = jnp.where(hm, dkh, dk)
            dv = jnp.where(hm, dvh, dv)
        dq_ref[...] = dq.astype(dq_ref.dtype)
        dk_ref[...] = dk.astype(dk_ref.dtype)
        dv_ref[...] = dv.astype(dv_ref.dtype)

    blk = (ATT_BLK, 256)
    cur = lambda col: pl.BlockSpec(blk, lambda pr, n: (n, col(pr)))
    prv = lambda col: pl.BlockSpec(blk, lambda pr, n: (prev(n), col(pr)))
    nx = lambda col: pl.BlockSpec(blk, lambda pr, n: (nxt(n), col(pr)))
    own = lambda pr: pr
    outs = pl.pallas_call(
        body, name=f"attn_bwd_{g}", grid=(r, nblk),
        in_specs=[cur(qcol), nx(qcol), prv(qcol), cur(qcol), prv(vcol), cur(vcol),
                  cur(own), nx(own), cur(own), nx(own), cur(own), nx(own)],
        out_specs=[cur(own), cur(own), cur(own)],
        out_shape=[jax.ShapeDtypeStruct((ln, r * 256), BF16)] * 3,
        compiler_params=_params(("parallel", "parallel")),
    )(qv, qv, kv, kv, pv, pv, dov, dov, ov, ov, lv, lv)
    return [t.reshape(s, 256) for t in outs]


def _gelu_parts(gv):
    cdf = 0.5 * (1.0 + lax.erf(gv * (2.0 ** -0.5)))
    pdf = jnp.exp(-0.5 * gv * gv) * (1.0 / math.sqrt(2.0 * math.pi))
    return cdf, pdf


def _pick_row(t, k):
    row = lax.broadcasted_iota(jnp.int32, t.shape, 0)
    return jnp.sum(jnp.where(row == k, t, 0.0), axis=0, keepdims=True)


def _shift_rows(u, halo, n):
    row = lax.broadcasted_iota(jnp.int32, u.shape, 0)
    out = pltpu.roll(u, n, 0)
    for k in range(n):
        out = jnp.where(row == k, _pick_row(halo, 16 - n + k), out)
    return out


def _shift_rows_up(u, halo, n):
    rb = u.shape[0]
    row = lax.broadcasted_iota(jnp.int32, u.shape, 0)
    out = pltpu.roll(u, rb - n, 0)
    for k in range(n):
        out = jnp.where(row == rb - n + k, _pick_row(halo, k), out)
    return out


def _conv(u, halo, cw, cb):
    return cb + _pick_row(cw, 0) * _shift_rows(u, halo, 2) + _pick_row(cw, 1) * _shift_rows(u, halo, 1) + _pick_row(cw, 2) * u


def _local_step(x, mod, pos_col, target, sm, w):
    s = x.shape[0]
    shift1, scale1, gate1, shift2, scale2, gate2 = [mod[i:i + 1, :] for i in range(6)]
    rb = 256
    zc = lambda c: 0

    def f_norm1(c, i, xv, nw, sc, sh):
        return ((xv * _rms(xv) * nw) * (1.0 + sc) + sh,)

    (h,) = _rowcall(f_norm1, [_rows(x, rb), _full(sm["n1w"]), _full(scale1), _full(shift1)],
                    [_orow(s, D, BF16, rb)], n_rows=s, rb=rb, name="norm1")
    p = _mm(h, w["win"], "in_proj", tm=1024, tn=1536)

    def f_gla_pre(c, i, glr, w2, gb):
        z = _dg(glr, w2.astype(BF16), 1, 0) + gb
        return ((jnp.minimum(z, 0.0) - jnp.log(1.0 + jnp.exp(-jnp.abs(z)))) * (1.0 / GLA_TAU),)

    (la,) = _rowcall(f_gla_pre, [_rows(p, rb, 128, P_LR // 128), _full(sm["w2"]), _full(sm["gb"])],
                     [_orow(s, 512, F32, rb)], n_rows=s, rb=rb, name="gla_pre")
    o_gla, states = _gla_fwd(p, la, s)

    def f_gla_post(c, i, ov, gnw, gr):
        on = jnp.concatenate([ov[:, k * 256:(k + 1) * 256] * _rms(ov[:, k * 256:(k + 1) * 256]) * gnw
                              for k in range(GLA_H)], axis=1)
        g = gr.astype(F32)
        return (on * (g * _sigmoid(g)),)

    (og,) = _rowcall(f_gla_post, [_rows(o_gla, rb), _full(sm["gnw"]), _rows(p, rb, 1024, P_GR // 1024)],
                     [_orow(s, 1024, BF16, rb)], n_rows=s, rb=rb, name="gla_post")
    y_gla = _mm(og, w["wgb"], "gla_branch")

    invf = jnp.tile(ROPE_THETA ** (-jnp.arange(ATT_HD // 2, dtype=F32) / (ATT_HD // 2)), 4).reshape(1, 128)
    cos_t, sin_t = _rope_tables(pos_col, invf, s)

    def f_rope(c, i, aq, ak, cs, sn):
        return (_rope_apply(aq.astype(F32), cs, sn, False), _rope_apply(ak.astype(F32), cs, sn, False))

    qrot, krot = _rowcall(f_rope, [_rows(p, rb, ATT_W, P_AQ // ATT_W), _rows(p, rb, ATT_W, P_AK // ATT_W),
                                   _rows(cos_t, rb), _rows(sin_t, rb)],
                          [_orow(s, ATT_W, BF16, rb), _orow(s, ATT_W, BF16, rb)], n_rows=s, rb=rb, name="rope")
    att = [_attn_fwd(qrot, krot, p, g, r, s) for g, (_, r) in enumerate(ATT_GROUPS)]

    def f_comb(c, i, o0, o1, o2, l0, l1, l2):
        mx = jnp.maximum(jnp.maximum(l0, l1), l2)
        e0, e1, e2 = jnp.exp(l0 - mx), jnp.exp(l1 - mx), jnp.exp(l2 - mx)
        z = e0 + e1 + e2
        o = (e0 * o0.astype(F32) + e1 * o1.astype(F32) + e2 * o2.astype(F32)) / z
        return (o, mx + jnp.log(z))

    o_att, lse = _rowcall(f_comb, [_rows(a[0], 512) for a in att] + [_rows(a[1], 512) for a in att],
                          [_orow(s, 256, BF16, 512), _orow(s, 256, F32, 512)], n_rows=s, rb=512, name="attn_combine")
    y_att = _mm(o_att, w["wab"], "attn_branch")

    def f_merge(c, i, ma, mb, yg, ya):
        return (_sigmoid(ma.astype(F32)) * yg.astype(F32) + _sigmoid(mb.astype(F32)) * ya.astype(F32),)

    (mixed,) = _rowcall(f_merge, [_rows(p, rb, 256, lambda c: P_MA // 256 + c),
                                  _rows(p, rb, 256, lambda c: P_MB // 256 + c),
                                  _rows(y_gla, rb, 256, lambda c: c), _rows(y_att, rb, 256, lambda c: c)],
                        [_orow(s, 1024, BF16, rb, 256, lambda c: c)], n_rows=s, rb=rb, name="merge", ncol=4)
    z1 = _mm(mixed, w["wout"], "out_proj")

    def f_norm2(c, i, xv, z, g1, nw, sc, sh):
        x1 = xv + g1 * z.astype(F32)
        return (x1, (x1 * _rms(x1) * nw) * (1.0 + sc) + sh)

    x1, h2 = _rowcall(f_norm2, [_rows(x, rb), _rows(z1, rb), _full(gate1), _full(sm["n2w"]), _full(scale2), _full(shift2)],
                      [_orow(s, D, F32, rb), _orow(s, D, BF16, rb)], n_rows=s, rb=rb, name="norm2")
    u = _mm(h2, w["wup"], "up_proj", tn=1408)

    cwid = 2 * W_UP_SH

    def f_ffn(c, i, uv, hl, cw, cb):
        uc = _conv(uv.astype(F32), hl.astype(F32) * (i > 0).astype(F32), cw, cb)
        val, gt = uc[:, :W_UP_SH], uc[:, W_UP_SH:]
        cdf, _ = _gelu_parts(gt)
        return (gt * cdf * val,)

    ccol = lambda c: c
    (hidden,) = _rowcall(f_ffn, [_rows(u, rb, cwid, ccol), _halo(u, rb, 16, cwid, ccol, True),
                                 _full(sm["cw"], cwid, ccol), _full(sm["cb"], cwid, ccol)],
                         [_orow(s, D_FF, BF16, rb, W_UP_SH, ccol)], n_rows=s, rb=rb, name="conv_geglu", ncol=2)
    z2 = _mm(hidden, w["wdown"], "down_proj", tk=1408)

    def f_final(c, i, x1v, z, g2, fw, tgt):
        x2 = x1v + g2 * z.astype(F32)
        r = _rms(x2)
        xh = x2 * r
        e = xh * fw - tgt
        loss = 0.5 * jnp.sum(jnp.mean(e * e, axis=-1, keepdims=True), axis=0, keepdims=True)
        dy = e * (1.0 / D)
        dxh = dy * fw
        dx2 = r * (dxh - xh * jnp.mean(dxh * xh, axis=-1, keepdims=True))
        return (loss, dx2, dx2 * g2, _csum(dy * xh), _csum(dx2 * z.astype(F32)))

    loss, dx2, dz2, d_fnw, d_gate2 = _rowcall(
        f_final, [_rows(x1, rb), _rows(z2, rb), _full(gate2), _full(sm["fnw"]), _rows(target, rb)],
        [_oacc(1, 1), _orow(s, D, F32, rb), _orow(s, D, BF16, rb), _oacc(1, D), _oacc(1, D)],
        n_rows=s, rb=rb, name="final_loss")
    d_hidden = _mm(dz2, w["wdown"], "down_proj_dx", tb=True, tn=1408)
    g_wdown = _mm(hidden, dz2, "down_proj_dw", ta=True, out_dtype=F32, tm=1408, tn=1024, tk=512)

    def f_ffn_bwd(c, i, uv, hl, dh, cw, cb):
        uf = uv.astype(F32)
        hf = hl.astype(F32) * (i > 0).astype(F32)
        u1, u2 = _shift_rows(uf, hf, 1), _shift_rows(uf, hf, 2)
        uc = cb + _pick_row(cw, 0) * u2 + _pick_row(cw, 1) * u1 + _pick_row(cw, 2) * uf
        val, gt = uc[:, :W_UP_SH], uc[:, W_UP_SH:]
        cdf, pdf = _gelu_parts(gt)
        dhf = dh.astype(F32)
        duc = jnp.concatenate([dhf * (gt * cdf), dhf * val * (cdf + gt * pdf)], axis=1)
        dcw = jnp.concatenate([_csum(duc * u2), _csum(duc * u1), _csum(duc * uf)], axis=0)
        return (duc, _csum(duc), dcw)

    duc, d_cb, d_cw = _rowcall(
        f_ffn_bwd, [_rows(u, rb, cwid, ccol), _halo(u, rb, 16, cwid, ccol, True), _rows(d_hidden, rb, W_UP_SH, ccol),
                    _full(sm["cw"], cwid, ccol), _full(sm["cb"], cwid, ccol)],
        [_orow(s, 2 * D_FF, BF16, rb, cwid, ccol), _oacc(1, 2 * D_FF, cwid, ccol), _oacc(3, 2 * D_FF, cwid, ccol)],
        n_rows=s, rb=rb, name="conv_geglu_bwd", ncol=2)

    def f_conv_t(c, i, dv, hl, cw):
        df = dv.astype(F32)
        hf = hl.astype(F32) * (i < s // rb - 1).astype(F32)
        return (_pick_row(cw, 2) * df + _pick_row(cw, 1) * _shift_rows_up(df, hf, 1) + _pick_row(cw, 0) * _shift_rows_up(df, hf, 2),)

    (du,) = _rowcall(f_conv_t, [_rows(duc, rb, cwid, ccol), _halo(duc, rb, 16, cwid, ccol, False), _full(sm["cw"], cwid, ccol)],
                     [_orow(s, 2 * D_FF, BF16, rb, cwid, ccol)], n_rows=s, rb=rb, name="conv_transpose", ncol=2)
    d_h2 = _mm(du, w["wup"], "up_proj_dx", tb=True, tk=1408)
    g_wup = _mm(h2, du, "up_proj_dw", ta=True, out_dtype=F32, tm=1024, tn=1408, tk=512)

    def f_norm2_bwd(c, i, x1v, dh, dxr, z, nw, sc, g1):
        dxn, dsh, dsc, dnw = _norm_bwd(x1v, dh.astype(F32), nw, sc)
        dx1 = dxr + dxn
        return (dx1, dx1 * g1, dsh, dsc, dnw, _csum(dx1 * z.astype(F32)))

    dx1, dz1, d_shift2, d_scale2, d_n2w, d_gate1 = _rowcall(
        f_norm2_bwd, [_rows(x1, rb), _rows(d_h2, rb), _rows(dx2, rb), _rows(z1, rb), _full(sm["n2w"]), _full(scale2), _full(gate1)],
        [_orow(s, D, F32, rb), _orow(s, D, BF16, rb), _oacc(1, D), _oacc(1, D), _oacc(1, D), _oacc(1, D)],
        n_rows=s, rb=rb, name="norm2_bwd")
    d_mixed = _mm(dz1, w["wout"], "out_proj_dx", tb=True)
    g_wout = _mm(mixed, dz1, "out_proj_dw", ta=True, out_dtype=F32, tk=512)

    def f_merge_bwd(c, i, dm, ma, mb, yg, ya):
        dmf, ygf, yaf = dm.astype(F32), yg.astype(F32), ya.astype(F32)
        sa, sb = _sigmoid(ma.astype(F32)), _sigmoid(mb.astype(F32))
        return (dmf * sa, dmf * sb, dmf * ygf * sa * (1.0 - sa), dmf * yaf * sb * (1.0 - sb))

    c4 = lambda c: c
    dy_gla, dy_att, d_ma, d_mb = _rowcall(
        f_merge_bwd, [_rows(d_mixed, rb, 256, c4), _rows(p, rb, 256, lambda c: P_MA // 256 + c),
                      _rows(p, rb, 256, lambda c: P_MB // 256 + c), _rows(y_gla, rb, 256, c4), _rows(y_att, rb, 256, c4)],
        [_orow(s, 1024, BF16, rb, 256, c4)] * 4, n_rows=s, rb=rb, name="merge_bwd", ncol=4)
    d_og = _mm(dy_gla, w["wgb"], "gla_branch_dx", tb=True)
    g_wgb = _mm(og, dy_gla, "gla_branch_dw", ta=True, out_dtype=F32, tk=512)
    d_oatt = _mm(dy_att, w["wab"], "attn_branch_dx", tb=True)
    g_wab = _mm(o_att, dy_att, "attn_branch_dw", ta=True, out_dtype=F32, tk=512)

    def f_gla_post_bwd(c, i, ov, gnw, gr, dog):
        g = gr.astype(F32)
        sg = _sigmoid(g)
        silu = g * sg
        dof = dog.astype(F32)
        don = dof * silu
        on_parts, do_parts, dgn = [], [], jnp.zeros((1, 256), F32)
        for k in range(GLA_H):
            oh = ov[:, k * 256:(k + 1) * 256]
            dh = don[:, k * 256:(k + 1) * 256]
            r = _rms(oh)
            xh = oh * r
            dgn = dgn + _csum(dh * xh)
            dxh = dh * gnw
            do_parts.append(r * (dxh - xh * jnp.mean(dxh * xh, axis=-1, keepdims=True)))
            on_parts.append(xh * gnw)
        on = jnp.concatenate(on_parts, axis=1)
        dgr = dof * on * (sg * (1.0 + g * (1.0 - sg)))
        return (jnp.concatenate(do_parts, axis=1), dgr, dgn)

    do_gla, d_gr, d_gnw = _rowcall(
        f_gla_post_bwd, [_rows(o_gla, rb), _full(sm["gnw"]), _rows(p, rb, 1024, P_GR // 1024), _rows(d_og, rb)],
        [_orow(s, 1024, F32, rb), _orow(s, 1024, BF16, rb), _oacc(1, 256)], n_rows=s, rb=rb, name="gla_post_bwd")
    d_gq, d_gk, d_gv, d_la = _gla_bwd(p, la, states, do_gla, s)

    def f_gla_pre_bwd(c, i, lav, dlav, glr, w2):
        dz = dlav * (1.0 / GLA_TAU) * (1.0 - jnp.exp(GLA_TAU * lav))
        dzb = dz.astype(BF16)
        return (_dg(dzb, w2.astype(BF16), 1, 1), _csum(dz), _dg(glr, dzb, 0, 0))

    d_glr, d_gb, d_w2 = _rowcall(
        f_gla_pre_bwd, [_rows(la, rb), _rows(d_la, rb), _rows(p, rb, 128, P_LR // 128), _full(sm["w2"])],
        [_orow(s, 128, BF16, rb), _oacc(1, 512), _oacc(128, 512)], n_rows=s, rb=rb, name="gla_pre_bwd")

    datt = [_attn_bwd(qrot, krot, p, d_oatt, o_att, lse, g, r, s) for g, (_, r) in enumerate(ATT_GROUPS)]
    dq_rot = jnp.concatenate([d[0] for d in datt], axis=1)
    dk_rot = jnp.concatenate([d[1] for d in datt], axis=1)
    d_av = jnp.concatenate([d[2] for d in datt], axis=1)

    def f_rope_bwd(c, i, dq, dk, cs, sn):
        return (_rope_apply(dq.astype(F32), cs, sn, True), _rope_apply(dk.astype(F32), cs, sn, True))

    d_aq, d_ak = _rowcall(f_rope_bwd, [_rows(dq_rot, rb), _rows(dk_rot, rb), _rows(cos_t, rb), _rows(sin_t, rb)],
                          [_orow(s, ATT_W, BF16, rb), _orow(s, ATT_W, BF16, rb)], n_rows=s, rb=rb, name="rope_bwd")
    dp = jnp.concatenate([d_gq, d_gk, d_gv, d_gr, d_aq, d_ak, d_av, d_ma, d_mb, d_glr,
                          jnp.zeros((s, P_W - P_LR - 128), BF16)], axis=1)
    d_h = _mm(dp, w["win"], "in_proj_dx", tb=True, tk=1536)
    g_win = _mm(h, dp, "in_proj_dw", ta=True, out_dtype=F32, tm=1024, tn=1536, tk=512)

    def f_norm1_bwd(c, i, xv, dh, dxr, nw, sc):
        dxn, dsh, dsc, dnw = _norm_bwd(xv, dh.astype(F32), nw, sc)
        return (dxr + dxn, dsh, dsc, dnw)

    grad_x, d_shift1, d_scale1, d_n1w = _rowcall(
        f_norm1_bwd, [_rows(x, rb), _rows(d_h, rb), _rows(dx1, rb), _full(sm["n1w"]), _full(scale1)],
        [_orow(s, D, F32, rb), _oacc(1, D), _oacc(1, D), _oacc(1, D)], n_rows=s, rb=rb, name="norm1_bwd")

    dmod = jnp.concatenate([d_shift1, d_scale1, d_gate1, d_shift2, d_scale2, d_gate2], axis=1)
    big = dict(win=g_win, wgb=g_wgb, wab=g_wab, wout=g_wout, wup=g_wup, wdown=g_wdown)
    small = dict(dmod=dmod, n1w=d_n1w, gb=d_gb, gnw=d_gnw, n2w=d_n2w, cb=d_cb, fnw=d_fnw, w2=d_w2, cw=d_cw)
    return loss, grad_x, big, small


def _win_to_kernel(wfull):
    return jnp.concatenate([wfull[:, :3072], wfull[:, 3088:W_IN], wfull[:, 3072:3088],
                            jnp.zeros((D, P_W - W_IN), wfull.dtype)], axis=1)


def _win_from_kernel(g):
    return jnp.concatenate([g[:, :3072], g[:, P_LR:P_LR + GLA_LR], g[:, 3072:P_LR]], axis=1)


def _ff_to_kernel(a):
    h = W_UP_SH
    return jnp.concatenate([a[:, 0:h], a[:, D_FF:D_FF + h], a[:, h:D_FF], a[:, D_FF + h:]], axis=1)


def _ff_from_kernel(a):
    h = W_UP_SH
    return jnp.concatenate([a[:, 0:h], a[:, 2 * h:3 * h], a[:, h:2 * h], a[:, 3 * h:]], axis=1)


def _pack_rows(parts, rows, dtype):
    flat = jnp.concatenate([t.reshape(-1).astype(dtype) for t in parts])
    return jnp.pad(flat, (0, rows * 1024 - flat.shape[0])).reshape(rows, 1024)


def _unpack(flat, shapes):
    out, off = [], 0
    for shp in shapes:
        n = math.prod(shp)
        out.append(flat[off:off + n].reshape(shp))
        off += n
    return out


PK_SHARD_SHAPES = ((1024, W_IN_SH), (256, 1024), (256, 256), (256, 1024), (1024, W_UP_SH), (W_DOWN_SH, 1024))


def _weights_from_gathered(wg):
    flat = wg.reshape(4, PK_ROWS * 1024)
    off, parts = 0, []
    for shp in PK_SHARD_SHAPES:
        n = math.prod(shp)
        parts.append(flat[:, off:off + n].reshape((4,) + shp))
        off += n
    win = parts[0].transpose(1, 0, 2).reshape(D, W_IN)
    wup = parts[4].transpose(1, 0, 2).reshape(D, 2 * D_FF)
    return dict(win=_win_to_kernel(win), wgb=parts[1].reshape(1024, D), wab=parts[2].transpose(1, 0, 2).reshape(256, D),
                wout=parts[3].reshape(D, D), wup=_ff_to_kernel(wup), wdown=parts[5].reshape(D_FF, D))


def _grads_to_packed(big):
    win = _win_from_kernel(big["win"]).reshape(D, 4, W_IN_SH).transpose(1, 0, 2)
    wup = _ff_from_kernel(big["wup"]).reshape(D, 4, W_UP_SH).transpose(1, 0, 2)
    parts = [win, big["wgb"].reshape(4, 256, 1024), big["wab"].reshape(256, 4, 256).transpose(1, 0, 2),
             big["wout"].reshape(4, 256, 1024), wup, big["wdown"].reshape(4, W_DOWN_SH, 1024)]
    flat = jnp.concatenate([t.reshape(4, -1) for t in parts], axis=1)
    return jnp.pad(flat, ((0, 0), (0, PK_ROWS * 1024 - flat.shape[1]))).reshape(4, PK_ROWS, 1024)


def _me():
    return lax.axis_index("x"), lax.axis_index("y"), lax.axis_index("c")


HBM = pl.BlockSpec(memory_space=pltpu.HBM)
VMEM_SPEC = pl.BlockSpec(memory_space=pltpu.VMEM)


def _allgather8(xs, name):
    rows = xs.shape[0]

    def body(x_ref, out_ref, send_sems, recv_sems, local_sem):
        x, y, c = _me()
        me = 4 * x + 2 * y + c
        mine = pltpu.make_async_copy(x_ref, out_ref.at[me], local_sem)
        mine.start()
        flips = [(k >> 2 & 1, k >> 1 & 1, k & 1) for k in range(1, 8)]

        def peer(f):
            return (jnp.where(f[0] == 1, 1 - x, x), jnp.where(f[1] == 1, 1 - y, y), jnp.where(f[2] == 1, 1 - c, c))

        sends = []
        for k, f in enumerate(flips):
            cp = pltpu.make_async_remote_copy(src_ref=x_ref, dst_ref=out_ref.at[me], send_sem=send_sems.at[k],
                                              recv_sem=recv_sems.at[k], device_id=peer(f), device_id_type=MESH)
            cp.start()
            sends.append(cp)
        for k, f in enumerate(flips):
            px, py, pc = peer(f)
            pltpu.make_async_remote_copy(src_ref=x_ref, dst_ref=out_ref.at[4 * px + 2 * py + pc], send_sem=send_sems.at[k],
                                         recv_sem=recv_sems.at[k], device_id=peer(f), device_id_type=MESH).wait_recv()
        for cp in sends:
            cp.wait_send()
        mine.wait()

    return pl.pallas_call(
        body, name=name, out_shape=jax.ShapeDtypeStruct((8, rows, 128), F32),
        in_specs=[VMEM_SPEC], out_specs=VMEM_SPEC,
        scratch_shapes=[pltpu.SemaphoreType.DMA((7,)), pltpu.SemaphoreType.DMA((7,)), pltpu.SemaphoreType.DMA],
        compiler_params=pltpu.CompilerParams(vmem_limit_bytes=VMEM_LIMIT),
    )(xs)


def _gather_weights(wpk):
    def body(w_ref, out_ref, send_sems, recv_sems, local_sem):
        x, y, c = _me()
        s_me = 2 * x + y
        sibling = (x, y, 1 - c)
        chips = [(1 - x, y), (x, 1 - y), (1 - x, 1 - y)]
        half = lambda cc: pl.ds(pl.multiple_of(cc * PK_HALF, 16), PK_HALF)
        mine = pltpu.make_async_copy(w_ref, out_ref.at[s_me], local_sem)
        mine.start()

        def copy(k, src, chip_of_block, cc, to):
            return pltpu.make_async_remote_copy(src_ref=src, dst_ref=out_ref.at[chip_of_block, half(cc)],
                                                send_sem=send_sems.at[k], recv_sem=recv_sems.at[k], device_id=to, device_id_type=MESH)

        first = [copy(j, w_ref.at[half(c)], s_me, c, (*chip, c)) for j, chip in enumerate(chips)]
        for cp in first:
            cp.start()
        passed = []
        for j, (px, py) in enumerate(chips):
            s_j = 2 * px + py
            copy(j, w_ref.at[half(c)], s_j, c, (px, py, c)).wait_recv()
            cp = copy(3 + j, out_ref.at[s_j, half(c)], s_j, c, sibling)
            cp.start()
            passed.append(cp)
        for j, (px, py) in enumerate(chips):
            copy(3 + j, w_ref.at[half(c)], 2 * px + py, 1 - c, sibling).wait_recv()
        for cp in first + passed:
            cp.wait_send()
        mine.wait()

    return pl.pallas_call(
        body, name="gather_weights", out_shape=jax.ShapeDtypeStruct((4, PK_ROWS, 1024), BF16),
        in_specs=[HBM], out_specs=HBM,
        scratch_shapes=[pltpu.SemaphoreType.DMA((6,)), pltpu.SemaphoreType.DMA((6,)), pltpu.SemaphoreType.DMA],
    )(wpk)


def _pair_split(g):
    def body(g_ref, mine_ref, land_ref, send_sems, recv_sems, local_sems):
        x, y, c = _me()
        half = lambda cc: pl.ds(pl.multiple_of(cc * PK_HALF, 8), PK_HALF)
        loc, rem = [], []
        for sh in range(4):
            cp = pltpu.make_async_copy(g_ref.at[sh, half(c)], mine_ref.at[sh], local_sems.at[sh])
            cp.start()
            loc.append(cp)
            cp = pltpu.make_async_remote_copy(src_ref=g_ref.at[sh, half(1 - c)], dst_ref=land_ref.at[sh], send_sem=send_sems.at[sh],
                                              recv_sem=recv_sems.at[sh], device_id=(x, y, 1 - c), device_id_type=MESH)
            cp.start()
            rem.append(cp)
        for cp in rem:
            cp.wait()
        for cp in loc:
            cp.wait()

    shp = jax.ShapeDtypeStruct((4, PK_HALF, 1024), F32)
    return pl.pallas_call(
        body, name="grad_pair_split", out_shape=[shp, shp], in_specs=[HBM], out_specs=[HBM, HBM],
        scratch_shapes=[pltpu.SemaphoreType.DMA((4,)), pltpu.SemaphoreType.DMA((4,)), pltpu.SemaphoreType.DMA((4,))],
    )(g)


def _chip_exchange(t):
    def body(t_ref, r_ref, send_sems, recv_sems, local_sem):
        x, y, c = _me()
        s_me = 2 * x + y
        chips = [(1 - x, y), (x, 1 - y), (1 - x, 1 - y)]
        mine = pltpu.make_async_copy(t_ref.at[s_me], r_ref.at[s_me], local_sem)
        mine.start()
        sends = []
        for j, (px, py) in enumerate(chips):
            cp = pltpu.make_async_remote_copy(src_ref=t_ref.at[2 * px + py], dst_ref=r_ref.at[s_me], send_sem=send_sems.at[j],
                                              recv_sem=recv_sems.at[j], device_id=(px, py, c), device_id_type=MESH)
            cp.start()
            sends.append(cp)
        for j, (px, py) in enumerate(chips):
            pltpu.make_async_remote_copy(src_ref=t_ref.at[s_me], dst_ref=r_ref.at[2 * px + py], send_sem=send_sems.at[j],
                                         recv_sem=recv_sems.at[j], device_id=(px, py, c), device_id_type=MESH).wait_recv()
        for cp in sends:
            cp.wait_send()
        mine.wait()

    return pl.pallas_call(
        body, name="grad_chip_exchange", out_shape=jax.ShapeDtypeStruct(t.shape, t.dtype), in_specs=[HBM], out_specs=HBM,
        scratch_shapes=[pltpu.SemaphoreType.DMA((3,)), pltpu.SemaphoreType.DMA((3,)), pltpu.SemaphoreType.DMA],
    )(t)


def _pair_join(half_sum):
    def body(h_ref, out_ref, send_sem, recv_sem, local_sem):
        x, y, c = _me()
        mine = pltpu.make_async_copy(h_ref, out_ref.at[c], local_sem)
        mine.start()
        cp = pltpu.make_async_remote_copy(src_ref=h_ref, dst_ref=out_ref.at[c], send_sem=send_sem, recv_sem=recv_sem,
                                          device_id=(x, y, 1 - c), device_id_type=MESH)
        cp.start()
        pltpu.make_async_remote_copy(src_ref=h_ref, dst_ref=out_ref.at[1 - c], send_sem=send_sem, recv_sem=recv_sem,
                                     device_id=(x, y, 1 - c), device_id_type=MESH).wait_recv()
        cp.wait_send()
        mine.wait()

    return pl.pallas_call(
        body, name="grad_pair_join", out_shape=jax.ShapeDtypeStruct((2, PK_HALF, 1024), F32), in_specs=[HBM], out_specs=HBM,
        scratch_shapes=[pltpu.SemaphoreType.DMA, pltpu.SemaphoreType.DMA, pltpu.SemaphoreType.DMA],
    )(half_sum)


def _add2(a, b, name):
    n = a.shape[0]

    def fn(c, i, av, bv):
        return (av + bv,)

    return _rowcall(fn, [_rows(a, 768), _rows(b, 768)], [_orow(n, 1024, F32, 768)], n_rows=n, rb=768, name=name)[0]


def _sum4(r, name):
    nb = PK_HALF // 768

    def body(a, b, c, d, o_ref):
        o_ref[...] = ((a[...] + b[...]) + c[...]) + d[...]

    return pl.pallas_call(
        body, name=name, grid=(nb,),
        in_specs=[pl.BlockSpec((768, 1024), lambda i, j=j: (j * nb + i, 0)) for j in range(4)],
        out_specs=pl.BlockSpec((768, 1024), lambda i: (i, 0)),
        out_shape=jax.ShapeDtypeStruct((PK_HALF, 1024), F32),
        compiler_params=_params(("parallel",)),
    )(r, r, r, r)


SG_ROWS = 352
SG_W2, SG_CW = 128, 192
SP_ROWS = 184


def _mod_shard(c_all, ada_w_sh):
    def body(c_ref, w_ref, o_ref):
        cv = c_ref[...]
        o_ref[...] = _dg((cv * _sigmoid(cv)).astype(BF16), w_ref[...].astype(BF16), 1, 0)

    return pl.pallas_call(body, name="mod_shard", out_shape=jax.ShapeDtypeStruct((8, 1536), F32),
                          in_specs=[VMEM_SPEC, VMEM_SPEC], out_specs=VMEM_SPEC,
                          compiler_params=pltpu.CompilerParams(vmem_limit_bytes=VMEM_LIMIT))(c_all, ada_w_sh)


def _mod_select(mod_all, ada_b4):
    def body(m_ref, b_ref, o_ref):
        x, y, c = _me()
        me = 4 * x + 2 * y + c
        for sh in range(4):
            o_ref[sh] = m_ref[2 * sh, me] + b_ref[sh]

    return pl.pallas_call(body, name="mod_select", out_shape=jax.ShapeDtypeStruct((4, 12, 128), F32),
                          in_specs=[VMEM_SPEC, VMEM_SPEC], out_specs=VMEM_SPEC)(mod_all, ada_b4)


def _small_reduce(sg_all):
    def body(g_ref, o_ref):
        x, y, c = _me()
        s_me = 2 * x + y
        w2_rows = pl.ds(pl.multiple_of(SG_W2 + 16 * s_me, 8), 16)
        cw_rows = pl.ds(pl.multiple_of(SG_CW + 40 * s_me, 8), 40)
        a = g_ref[0, 0:128, :]
        b = g_ref[0, w2_rows, :]
        d = g_ref[0, cw_rows, :]
        for dev in range(1, 8):
            a = a + g_ref[dev, 0:128, :]
            b = b + g_ref[dev, w2_rows, :]
            d = d + g_ref[dev, cw_rows, :]
        o_ref[0:128, :] = a
        o_ref[128:144, :] = b
        o_ref[144:184, :] = d

    return pl.pallas_call(body, name="small_grad_reduce", out_shape=jax.ShapeDtypeStruct((SP_ROWS, 128), F32),
                          in_specs=[VMEM_SPEC], out_specs=VMEM_SPEC)(sg_all)


def _ada_grad(dmod_all, c_bc):
    def body(g_ref, c_ref, o_ref):
        x, y, c = _me()
        s_me = 2 * x + y
        for k in range(12):
            acc = jnp.zeros((D, 128), F32)
            for b in range(8):
                cv = c_ref[b]
                acc = acc + (cv * _sigmoid(cv)) * g_ref[s_me, k, b:b + 1, :]
            o_ref[:, k * 128:(k + 1) * 128] = acc

    return pl.pallas_call(body, name="ada_w_grad", out_shape=jax.ShapeDtypeStruct((D, 1536), F32),
                          in_specs=[VMEM_SPEC, VMEM_SPEC], out_specs=VMEM_SPEC,
                          compiler_params=pltpu.CompilerParams(vmem_limit_bytes=VMEM_LIMIT))(dmod_all, c_bc)


def _adamw(wt, g, m, v, name):
    rows, cols = wt.shape
    rb = _tile(rows, 256, 8)

    def fn(c, i, wv, gv, mv, vv):
        mn = ADAM_B1 * mv + (1.0 - ADAM_B1) * gv
        vn = ADAM_B2 * vv + (1.0 - ADAM_B2) * (gv * gv)
        m_hat = mn / (1.0 - ADAM_B1 ** ADAM_STEP)
        v_hat = vn / (1.0 - ADAM_B2 ** ADAM_STEP)
        return (-ADAM_LR * (m_hat / (jnp.sqrt(v_hat) + ADAM_EPS) + ADAM_WD * wv), mn, vn)

    return _rowcall(fn, [_rows(t, rb) for t in (wt, g, m, v)], [_orow(rows, cols, F32, rb)] * 3,
                    n_rows=rows, rb=rb, name=name)


def _pad_rows(t, rows):
    flat = t.reshape(-1)
    return jnp.pad(flat, (0, rows * 128 - flat.shape[0])).reshape(rows, 128)


SP_LAYOUT = (("ada_b", 48), ("norm1_w", 8), ("gla_gate_b", 4), ("gla_norm_w", 2), ("norm2_w", 8), ("conv_b", 44),
             ("final_norm_w", 8), (None, 6), ("gla_gate_w2", 16), ("conv_w", 40))


def _pack_small(d):
    return jnp.concatenate([jnp.zeros((rows, 128), F32) if n is None else _pad_rows(d[n].astype(F32), rows)
                            for n, rows in SP_LAYOUT], axis=0)


def _unpack_small(pk, shapes):
    out, off = {}, 0
    for n, rows in SP_LAYOUT:
        if n is not None:
            shp = shapes[n]
            out[n] = pk[off:off + rows].reshape(-1)[:math.prod(shp)].reshape(shp)
        off += rows
    return out


def kernel(x, c, positions, ada_w, ada_b, norm1_w, w_in, gla_gate_w2, gla_gate_b, gla_norm_w, w_gla_branch, w_attn_branch, w_out, norm2_w, w_up, conv_w, conv_b, w_down, final_norm_w, loss_target, m_ada_w, m_ada_b, m_norm1_w, m_w_in, m_gla_gate_w2, m_gla_gate_b, m_gla_norm_w, m_w_gla_branch, m_w_attn_branch, m_w_out, m_norm2_w, m_w_up, m_conv_w, m_conv_b, m_w_down, m_final_norm_w, v_ada_w, v_ada_b, v_norm1_w, v_w_in, v_gla_gate_w2, v_gla_gate_b, v_gla_norm_w, v_w_gla_branch, v_w_attn_branch, v_w_out, v_norm2_w, v_w_up, v_conv_w, v_conv_b, v_w_down, v_final_norm_w):
    s = x.shape[1]
    names = ("ada_w", "ada_b", "norm1_w", "w_in", "gla_gate_w2", "gla_gate_b", "gla_norm_w", "w_gla_branch", "w_attn_branch",
             "w_out", "norm2_w", "w_up", "conv_w", "conv_b", "w_down", "final_norm_w")
    wts = dict(zip(names, (ada_w, ada_b, norm1_w, w_in, gla_gate_w2, gla_gate_b, gla_norm_w, w_gla_branch, w_attn_branch,
                           w_out, norm2_w, w_up, conv_w, conv_b, w_down, final_norm_w)))
    ms = dict(zip(names, (m_ada_w, m_ada_b, m_norm1_w, m_w_in, m_gla_gate_w2, m_gla_gate_b, m_gla_norm_w, m_w_gla_branch,
                          m_w_attn_branch, m_w_out, m_norm2_w, m_w_up, m_conv_w, m_conv_b, m_w_down, m_final_norm_w)))
    vs = dict(zip(names, (v_ada_w, v_ada_b, v_norm1_w, v_w_in, v_gla_gate_w2, v_gla_gate_b, v_gla_norm_w, v_w_gla_branch,
                          v_w_attn_branch, v_w_out, v_norm2_w, v_w_up, v_conv_w, v_conv_b, v_w_down, v_final_norm_w)))

    pk0 = jnp.concatenate([_pad_rows(c, 8), _pad_rows(gla_gate_w2, 16), _pad_rows(conv_w, 40)], axis=0)
    sm_all = _allgather8(pk0, "gather_small")
    c_all = sm_all[:, 0:8, :].reshape(8, D)
    w2_full = sm_all[0::2, 8:24, :].transpose(1, 0, 2).reshape(GLA_LR, 512)
    cw_full = sm_all[0::2, 24:64, :].reshape(4, 40 * 128)[:, :3 * W_UP_SH].reshape(4, 3, W_UP_SH).transpose(1, 0, 2).reshape(3, 2 * D_FF)

    mod_sh = _mod_shard(c_all, ada_w[0])
    mod_all = _allgather8(mod_sh.reshape(96, 128), "gather_mod")
    mod = _mod_select(mod_all.reshape(8, 8, 12, 128), ada_b.reshape(4, 12, 128)).reshape(6, D)

    wpk = _pack_rows([w_in[0], w_gla_branch[0], w_attn_branch[0], w_out[0], w_up[0], w_down[0]], PK_ROWS, BF16)
    w = _weights_from_gathered(_gather_weights(wpk))

    sm = dict(n1w=norm1_w, n2w=norm2_w, fnw=final_norm_w.reshape(1, D), gnw=gla_norm_w, gb=gla_gate_b,
              w2=jnp.pad(w2_full, ((0, 128 - GLA_LR), (0, 0))), cw=_ff_to_kernel(cw_full), cb=_ff_to_kernel(conv_b))
    loss, grad_x, big, small = _local_step(x[0], mod, positions.reshape(s, 1), loss_target[0], sm, w)

    mine, theirs = _pair_split(_grads_to_packed(big))
    t = _add2(mine.reshape(4 * PK_HALF, 1024), theirs.reshape(4 * PK_HALF, 1024), "grad_pair_add")
    r = _chip_exchange(t.reshape(4, PK_HALF, 1024))
    gsh = _pair_join(_sum4(r.reshape(4 * PK_HALF, 1024), "grad_chip_sum")).reshape(PK_ROWS * 1024)
    g_big = dict(zip(("w_in", "w_gla_branch", "w_attn_branch", "w_out", "w_up", "w_down"), _unpack(gsh, PK_SHARD_SHAPES)))

    dcw = _ff_from_kernel(small["cw"]).reshape(3, 4, W_UP_SH).transpose(1, 0, 2)
    dw2 = small["w2"][:GLA_LR].reshape(GLA_LR, 4, 128).transpose(1, 0, 2)
    sg = jnp.concatenate(
        [_pad_rows(small["dmod"], 48), _pad_rows(small["n1w"], 8), _pad_rows(small["gb"], 4), _pad_rows(small["gnw"], 2),
         _pad_rows(small["n2w"], 8), _pad_rows(_ff_from_kernel(small["cb"]), 44), _pad_rows(small["fnw"], 8),
         jnp.zeros((6, 128), F32)]
        + [_pad_rows(dw2[k], 16) for k in range(4)] + [_pad_rows(dcw[k], 40) for k in range(4)], axis=0)
    sg_all = _allgather8(sg, "gather_small_grads")
    g_small_pk = _small_reduce(sg_all)
    dmod_all = sg_all[:, 0:48, :].reshape(8, 4, 12, 128).transpose(1, 2, 0, 3)
    g_ada_w = _ada_grad(dmod_all, jnp.broadcast_to(c_all[:, :, None], (8, D, 128)))

    shapes = {n: wts[n].shape for n in names}
    g_small = _unpack_small(g_small_pk, shapes)
    grads = {"ada_w": g_ada_w.reshape(1, D, 1536), **g_small}
    deltas, new_m, new_v = {}, {}, {}
    for n in ("w_in", "w_gla_branch", "w_attn_branch", "w_out", "w_up", "w_down"):
        grads[n] = g_big[n].reshape(wts[n].shape)
    for n in ("ada_w", "w_in", "w_gla_branch", "w_attn_branch", "w_out", "w_up", "w_down"):
        shp = wts[n].shape
        d_, m_, v_ = _adamw(wts[n][0], grads[n][0], ms[n][0], vs[n][0], "adamw_" + n)
        deltas[n], new_m[n], new_v[n] = d_.reshape(shp), m_.reshape(shp), v_.reshape(shp)
    d_, m_, v_ = _adamw(_pack_small(wts), g_small_pk, _pack_small(ms), _pack_small(vs), "adamw_small")
    for dst, pk in ((deltas, d_), (new_m, m_), (new_v, v_)):
        dst.update(_unpack_small(pk, shapes))

    loss_all = lax.psum(loss[0, 0], ("x", "y", "c"))
    return (loss_all, grad_x.reshape(1, s, D), *[grads[n] for n in names], *[deltas[n] for n in names],
            *[new_m[n] for n in names], *[new_v[n] for n in names])
```

```python
import math

import jax
import jax.numpy as jnp
from jax import lax
from jax.experimental import pallas as pl
from jax.experimental.pallas import tpu as pltpu

F32, BF16 = jnp.float32, jnp.bfloat16
MESH = pl.DeviceIdType.MESH

D = 1024
EPS = 1e-6
GLA_H, GLA_DK, GLA_DV, GLA_LR = 4, 128, 256, 16
GLA_TAU = 16.0
GLA_CHUNK = 64
GLA_BLOCK = 512
ATT_GROUPS = ((128, 1), (512, 4), (2048, 16))
ATT_BLK = 128
ATT_HD = 64
ATT_W = 768
D_FF = 2816
ROPE_THETA = 10000.0
P_W = 7680
P_GQ, P_GK, P_GV, P_GR, P_AQ, P_AK, P_AV, P_MA, P_MB, P_LR = 0, 512, 1024, 2048, 3072, 3840, 4608, 5376, 6400, 7424
W_IN = 7440
W_IN_SH, W_UP_SH, W_DOWN_SH = 1860, 1408, 704
VMEM_LIMIT = 56 * 1024 * 1024
ADAM_LR, ADAM_B1, ADAM_B2, ADAM_EPS, ADAM_WD, ADAM_STEP = 0.001, 0.9, 0.999, 1e-08, 0.01, 10
NEG = -1e30


def _tile(n, target, unit=128):
    best = None
    for t in range(unit, min(n, target) + 1, unit):
        if n % t == 0:
            best = t
    return best or n


def _params(sem):
    return pltpu.CompilerParams(dimension_semantics=sem, vmem_limit_bytes=VMEM_LIMIT)


def _dg(a, b, ca, cb):
    return lax.dot_general(a, b, (((ca,), (cb,)), ((), ())), preferred_element_type=F32)


def _sigmoid(v):
    return 1.0 / (1.0 + jnp.exp(-v))


def _mm(a, b, name, *, ta=False, tb=False, out_dtype=BF16, tm=1024, tn=1536, tk=1024, n_outer=True):
    m = a.shape[1] if ta else a.shape[0]
    k = a.shape[0] if ta else a.shape[1]
    n = b.shape[0] if tb else b.shape[1]
    tm, tn, tk = _tile(m, tm), _tile(n, tn), _tile(k, tk)
    nm, nn, nk = m // tm, n // tn, k // tk
    in_out = out_dtype == F32

    def body(a_ref, b_ref, o_ref, *scr):
        kk = pl.program_id(2)
        p = _dg(a_ref[...].astype(BF16), b_ref[...].astype(BF16), 0 if ta else 1, 1 if tb else 0)
        if nk == 1:
            o_ref[...] = p.astype(o_ref.dtype)
        else:
            acc = o_ref if in_out else scr[0]

            @pl.when(kk == 0)
            def _():
                acc[...] = p

            @pl.when(kk > 0)
            def _():
                acc[...] += p

            if not in_out:
                @pl.when(kk == nk - 1)
                def _():
                    o_ref[...] = acc[...].astype(o_ref.dtype)

    if n_outer:
        ij = lambda g0, g1: (g1, g0)
        grid = (nn, nm, nk)
    else:
        ij = lambda g0, g1: (g0, g1)
        grid = (nm, nn, nk)
    a_map = (lambda g0, g1, kk: (kk, ij(g0, g1)[0])) if ta else (lambda g0, g1, kk: (ij(g0, g1)[0], kk))
    b_map = (lambda g0, g1, kk: (ij(g0, g1)[1], kk)) if tb else (lambda g0, g1, kk: (kk, ij(g0, g1)[1]))
    return pl.pallas_call(
        body, name=name, grid=grid,
        in_specs=[pl.BlockSpec((tk, tm) if ta else (tm, tk), a_map),
                  pl.BlockSpec((tn, tk) if tb else (tk, tn), b_map)],
        out_specs=pl.BlockSpec((tm, tn), lambda g0, g1, kk: ij(g0, g1)),
        out_shape=jax.ShapeDtypeStruct((m, n), out_dtype),
        scratch_shapes=[] if (in_out or nk == 1) else [pltpu.VMEM((tm, tn), F32)],
        compiler_params=_params(("parallel", "parallel", "arbitrary")),
    )(a, b)


def _rows(arr, rb, w=None, j=0):
    w = arr.shape[1] if w is None else w
    if callable(j):
        return arr, pl.BlockSpec((rb, w), lambda c, i: (i, j(c)))
    return arr, pl.BlockSpec((rb, w), lambda c, i: (i, j))


def _full(arr, w=None, j=0):
    w = arr.shape[1] if w is None else w
    if callable(j):
        return arr, pl.BlockSpec((arr.shape[0], w), lambda c, i: (0, j(c)))
    return arr, pl.BlockSpec((arr.shape[0], w), lambda c, i: (0, j))


def _halo(arr, rb, hb, w, j, before):
    per = rb // hb
    last = arr.shape[0] // hb - 1
    if before:
        rmap = lambda i: jnp.maximum(i * per - 1, 0)
    else:
        rmap = lambda i: jnp.minimum((i + 1) * per, last)
    return arr, pl.BlockSpec((hb, w), lambda c, i: (rmap(i), j(c) if callable(j) else j))


def _rowcall(fn, ins, outs, *, n_rows, rb, name, ncol=1):
    n_in = len(ins)
    nr = n_rows // rb

    def body(*refs):
        c, i = pl.program_id(0), pl.program_id(1)
        res = fn(c, i, *[r[...] for r in refs[:n_in]])
        for val, spec, o_ref in zip(res, outs, refs[n_in:]):
            if spec[2] == "row":
                o_ref[...] = val.astype(o_ref.dtype)
            else:
                @pl.when(i == 0)
                def _(o_ref=o_ref, val=val):
                    o_ref[...] = val.astype(o_ref.dtype)

                @pl.when(i > 0)
                def _(o_ref=o_ref, val=val):
                    o_ref[...] += val.astype(o_ref.dtype)

    out_specs = []
    for shape, dt, kind, block, col in outs:
        if kind == "row":
            out_specs.append(pl.BlockSpec(block, lambda c, i, col=col: (i, col(c))))
        else:
            out_specs.append(pl.BlockSpec(block, lambda c, i, col=col: (0, col(c))))
    return pl.pallas_call(
        body, name=name, grid=(ncol, nr),
        in_specs=[s for _, s in ins], out_specs=out_specs,
        out_shape=[jax.ShapeDtypeStruct(o[0], o[1]) for o in outs],
        compiler_params=_params(("parallel", "arbitrary")),
    )(*[a for a, _ in ins])


def _orow(n_rows, w, dt, rb, bw=None, col=lambda c: 0):
    return ((n_rows, w), dt, "row", (rb, bw or w), col)


def _oacc(r, w, bw=None, col=lambda c: 0):
    return ((r, w), F32, "acc", (r, bw or w), col)


def _csum(v):
    return jnp.sum(v, axis=0, keepdims=True)


def _rms(v):
    return lax.rsqrt(jnp.mean(v * v, axis=-1, keepdims=True) + EPS)


def _norm_bwd(xv, dh, w, scale):
    r = _rms(xv)
    xh = xv * r
    dxh = dh * (w * (1.0 + scale))
    dx = r * (dxh - xh * jnp.mean(dxh * xh, axis=-1, keepdims=True))
    t = dh * xh
    return dx, _csum(dh), _csum(t * w), _csum(t * (1.0 + scale))


def _rope_tables(pos_col, invf, s):
    def fn(c, i, pos, f):
        ang = pos.astype(F32) * f
        lane = lax.broadcasted_iota(jnp.int32, ang.shape, 1)
        sign = jnp.where((lane % ATT_HD) < ATT_HD // 2, -1.0, 1.0)
        return jnp.cos(ang), jnp.sin(ang) * sign

    rb = 512
    return _rowcall(fn, [_rows(pos_col, rb), _full(invf)], [_orow(s, 128, F32, rb), _orow(s, 128, F32, rb)],
                    n_rows=s, rb=rb, name="rope_tables")


def _swap_halves(t):
    n = t.shape[1]
    lane = lax.broadcasted_iota(jnp.int32, t.shape, 1)
    return jnp.where((lane % ATT_HD) < ATT_HD // 2, pltpu.roll(t, n - 32, 1), pltpu.roll(t, 32, 1))


def _rope_apply(t, cos, sin_signed, inverse):
    cw = jnp.concatenate([cos] * (t.shape[1] // 128), axis=1)
    sw = jnp.concatenate([sin_signed] * (t.shape[1] // 128), axis=1)
    if inverse:
        sw = -sw
    return t * cw + _swap_halves(t) * sw


def _gla_decays(la_c, tri):
    b = jnp.dot(tri, la_c, precision=lax.Precision.HIGHEST, preferred_element_type=F32)
    row = lax.broadcasted_iota(jnp.int32, b.shape, 0)
    bmid = jnp.sum(jnp.where(row == GLA_CHUNK // 2 - 1, b, 0.0), axis=0, keepdims=True)
    blast = jnp.sum(jnp.where(row == GLA_CHUNK - 1, b, 0.0), axis=0, keepdims=True)
    return b, bmid, blast


def _gla_fwd(p, la, s):
    tb, ch = GLA_BLOCK, GLA_CHUNK
    nb, nc = s // tb, tb // ch
    scale = GLA_DK ** -0.5

    def body(q_ref, k_ref, v_ref, la_ref, o_ref, st_ref, state):
        @pl.when(pl.program_id(1) == 0)
        def _():
            state[...] = jnp.zeros_like(state)

        ri = lax.broadcasted_iota(jnp.int32, (ch, ch), 0)
        ci = lax.broadcasted_iota(jnp.int32, (ch, ch), 1)
        causal = ci <= ri
        tri = causal.astype(F32)
        for c in range(nc):
            sl = pl.ds(c * ch, ch)
            b, bmid, blast = _gla_decays(la_ref[sl, :], tri)
            q = q_ref[sl, :].astype(F32) * scale
            k = k_ref[sl, :].astype(F32)
            v = v_ref[sl, :]
            qgt = (q * jnp.exp(b)).astype(BF16)
            qgn = (q * jnp.exp(b - bmid)).astype(BF16)
            kgn = (k * jnp.exp(bmid - b)).astype(BF16)
            kd = (k * jnp.exp(blast - b)).astype(BF16)
            a = jnp.where(causal, _dg(qgn, kgn, 1, 1), 0.0)
            st = state[...]
            st_ref[0, c] = st
            o_ref[sl, :] = _dg(a.astype(BF16), v, 1, 0) + _dg(qgt, st.astype(BF16), 1, 1)
            state[...] = jnp.exp(blast) * st + _dg(v, kd, 0, 0)

    return pl.pallas_call(
        body, name="gla_fwd", grid=(GLA_H, nb),
        in_specs=[pl.BlockSpec((tb, GLA_DK), lambda h, t: (t, P_GQ // GLA_DK + h)),
                  pl.BlockSpec((tb, GLA_DK), lambda h, t: (t, P_GK // GLA_DK + h)),
                  pl.BlockSpec((tb, GLA_DV), lambda h, t: (t, P_GV // GLA_DV + h)),
                  pl.BlockSpec((tb, GLA_DK), lambda h, t: (t, h))],
        out_specs=[pl.BlockSpec((tb, GLA_DV), lambda h, t: (t, h)),
                   pl.BlockSpec((1, nc, GLA_DV, GLA_DK), lambda h, t: (h, t, 0, 0))],
        out_shape=[jax.ShapeDtypeStruct((s, GLA_H * GLA_DV), F32),
                   jax.ShapeDtypeStruct((GLA_H, s // ch, GLA_DV, GLA_DK), F32)],
        scratch_shapes=[pltpu.VMEM((GLA_DV, GLA_DK), F32)],
        compiler_params=_params(("parallel", "arbitrary")),
    )(p, p, p, la)


def _gla_bwd(p, la, states, do, s):
    tb, ch = GLA_BLOCK, GLA_CHUNK
    nb, nc = s // tb, tb // ch
    scale = GLA_DK ** -0.5

    def body(q_ref, k_ref, v_ref, la_ref, st_ref, do_ref, dq_ref, dk_ref, dv_ref, dla_ref, dstate):
        @pl.when(pl.program_id(1) == 0)
        def _():
            dstate[...] = jnp.zeros_like(dstate)

        ri = lax.broadcasted_iota(jnp.int32, (ch, ch), 0)
        ci = lax.broadcasted_iota(jnp.int32, (ch, ch), 1)
        causal = ci <= ri
        tri = causal.astype(F32)
        tri_t = (ci >= ri).astype(F32)
        for c in reversed(range(nc)):
            sl = pl.ds(c * ch, ch)
            b, bmid, blast = _gla_decays(la_ref[sl, :], tri)
            q = q_ref[sl, :].astype(F32) * scale
            k = k_ref[sl, :].astype(F32)
            v = v_ref[sl, :]
            e_b, e_qn, e_kn, e_kd = jnp.exp(b), jnp.exp(b - bmid), jnp.exp(bmid - b), jnp.exp(blast - b)
            dec = jnp.exp(blast)
            qgt, qgn, kgn, kd = q * e_b, q * e_qn, k * e_kn, k * e_kd
            qgt_b, qgn_b, kgn_b, kd_b = qgt.astype(BF16), qgn.astype(BF16), kgn.astype(BF16), kd.astype(BF16)
            st0 = st_ref[0, c]
            dst = dstate[...]
            dst_b = dst.astype(BF16)
            do_b = do_ref[sl, :].astype(BF16)
            a = jnp.where(causal, _dg(qgn_b, kgn_b, 1, 1), 0.0).astype(BF16)
            da = jnp.where(causal, _dg(do_b, v, 1, 1), 0.0).astype(BF16)
            dqgn = _dg(da, kgn_b, 1, 0)
            dqgt = _dg(do_b, st0.astype(BF16), 1, 0)
            dkgn = _dg(da, qgn_b, 0, 0)
            dv = _dg(a, do_b, 0, 0) + _dg(kd_b, dst_b, 1, 1)
            dkd = _dg(v, dst_b, 1, 0)
            ddec = jnp.sum(st0 * dst, axis=0, keepdims=True)
            dstate[...] = dec * dst + _dg(do_b, qgt_b, 0, 0)
            dq_ref[sl, :] = (scale * (dqgn * e_qn + dqgt * e_b)).astype(dq_ref.dtype)
            dk_ref[sl, :] = (dkgn * e_kn + dkd * e_kd).astype(dk_ref.dtype)
            dv_ref[sl, :] = dv.astype(dv_ref.dtype)
            db = dqgn * qgn + dqgt * qgt - dkgn * kgn - dkd * kd
            extra = jnp.sum(dkd * kd, axis=0, keepdims=True) + ddec * dec
            dla_ref[sl, :] = jnp.dot(tri_t, db, precision=lax.Precision.HIGHEST, preferred_element_type=F32) + extra

    rev = lambda t: nb - 1 - t
    return pl.pallas_call(
        body, name="gla_bwd", grid=(GLA_H, nb),
        in_specs=[pl.BlockSpec((tb, GLA_DK), lambda h, t: (rev(t), P_GQ // GLA_DK + h)),
                  pl.BlockSpec((tb, GLA_DK), lambda h, t: (rev(t), P_GK // GLA_DK + h)),
                  pl.BlockSpec((tb, GLA_DV), lambda h, t: (rev(t), P_GV // GLA_DV + h)),
                  pl.BlockSpec((tb, GLA_DK), lambda h, t: (rev(t), h)),
                  pl.BlockSpec((1, nc, GLA_DV, GLA_DK), lambda h, t: (h, rev(t), 0, 0)),
                  pl.BlockSpec((tb, GLA_DV), lambda h, t: (rev(t), h))],
        out_specs=[pl.BlockSpec((tb, GLA_DK), lambda h, t: (rev(t), h)),
                   pl.BlockSpec((tb, GLA_DK), lambda h, t: (rev(t), h)),
                   pl.BlockSpec((tb, GLA_DV), lambda h, t: (rev(t), h)),
                   pl.BlockSpec((tb, GLA_DK), lambda h, t: (rev(t), h))],
        out_shape=[jax.ShapeDtypeStruct((s, GLA_H * GLA_DK), BF16),
                   jax.ShapeDtypeStruct((s, GLA_H * GLA_DK), BF16),
                   jax.ShapeDtypeStruct((s, GLA_H * GLA_DV), BF16),
                   jax.ShapeDtypeStruct((s, GLA_H * GLA_DK), F32)],
        scratch_shapes=[pltpu.VMEM((GLA_DV, GLA_DK), F32)],
        compiler_params=_params(("parallel", "arbitrary")),
    )(p, p, p, la, states, do)


def _head_masks():
    lane = lax.broadcasted_iota(jnp.int32, (1, 4 * ATT_HD), 1)
    return [(lane >= h * ATT_HD) & (lane < (h + 1) * ATT_HD) for h in range(4)]


def _attn_fwd(qrot, krot, vatt, g, r, s):
    ln = s // r
    nblk = ln // ATT_BLK
    qv, kv, pv = qrot.reshape(ln, r * ATT_W), krot.reshape(ln, r * ATT_W), vatt.reshape(ln, r * ATT_W)
    qcol = lambda pr: pr * 3 + g
    vcol = qcol
    prev = lambda n: jnp.maximum(n - 1, 0)

    def body(q_ref, kp_ref, kc_ref, vp_ref, vc_ref, o_ref, lse_ref):
        has_prev = pl.program_id(1) > 0
        ri = lax.broadcasted_iota(jnp.int32, (ATT_BLK, ATT_BLK), 0)
        ci = lax.broadcasted_iota(jnp.int32, (ATT_BLK, ATT_BLK), 1)
        m_cur = ci <= ri
        m_prev = (ci >= ri) & has_prev
        q, kp, kc, vp, vc = q_ref[...], kp_ref[...], kc_ref[...], vp_ref[...], vc_ref[...]
        o = jnp.zeros((ATT_BLK, 256), F32)
        lse = jnp.zeros((ATT_BLK, 256), F32)
        for hm in _head_masks():
            qm = jnp.where(hm, q, jnp.zeros_like(q))
            sc = jnp.where(m_cur, _dg(qm, kc, 1, 1) * 0.125, NEG)
            sp = jnp.where(m_prev, _dg(qm, kp, 1, 1) * 0.125, NEG)
            mx = jnp.maximum(jnp.max(sc, axis=1, keepdims=True), jnp.max(sp, axis=1, keepdims=True))
            pc, pp = jnp.exp(sc - mx), jnp.exp(sp - mx)
            den = jnp.sum(pc, axis=1, keepdims=True) + jnp.sum(pp, axis=1, keepdims=True)
            oh = (_dg(pc.astype(BF16), vc, 1, 0) + _dg(pp.astype(BF16), vp, 1, 0)) / den
            o = jnp.where(hm, oh, o)
            lse = jnp.where(hm, mx + jnp.log(den), lse)
        o_ref[...] = o.astype(o_ref.dtype)
        lse_ref[...] = lse

    blk = (ATT_BLK, 256)
    o, lse = pl.pallas_call(
        body, name=f"attn_fwd_{g}", grid=(r, nblk),
        in_specs=[pl.BlockSpec(blk, lambda pr, n: (n, qcol(pr))),
                  pl.BlockSpec(blk, lambda pr, n: (prev(n), qcol(pr))),
                  pl.BlockSpec(blk, lambda pr, n: (n, qcol(pr))),
                  pl.BlockSpec(blk, lambda pr, n: (prev(n), vcol(pr))),
                  pl.BlockSpec(blk, lambda pr, n: (n, vcol(pr)))],
        out_specs=[pl.BlockSpec(blk, lambda pr, n: (n, pr)), pl.BlockSpec(blk, lambda pr, n: (n, pr))],
        out_shape=[jax.ShapeDtypeStruct((ln, r * 256), BF16), jax.ShapeDtypeStruct((ln, r * 256), F32)],
        compiler_params=_params(("parallel", "parallel")),
    )(qv, kv, kv, pv, pv)
    return o.reshape(s, 256), lse.reshape(s, 256)


def _attn_bwd(qrot, krot, vatt, do, o, lse, g, r, s):
    ln = s // r
    nblk = ln // ATT_BLK
    qv, kv, pv = qrot.reshape(ln, r * ATT_W), krot.reshape(ln, r * ATT_W), vatt.reshape(ln, r * ATT_W)
    dov, ov, lv = do.reshape(ln, r * 256), o.reshape(ln, r * 256), lse.reshape(ln, r * 256)
    qcol = lambda pr: pr * 3 + g
    vcol = qcol
    prev = lambda n: jnp.maximum(n - 1, 0)
    nxt = lambda n: jnp.minimum(n + 1, nblk - 1)

    def body(qc_ref, qn_ref, kp_ref, kc_ref, vp_ref, vc_ref, doc_ref, don_ref, oc_ref, on_ref, lc_ref, ln_ref,
             dq_ref, dk_ref, dv_ref):
        n = pl.program_id(1)
        has_prev, has_next = n > 0, n < nblk - 1
        ri = lax.broadcasted_iota(jnp.int32, (ATT_BLK, ATT_BLK), 0)
        ci = lax.broadcasted_iota(jnp.int32, (ATT_BLK, ATT_BLK), 1)
        m_cur = ci <= ri
        m_prev = (ci >= ri) & has_prev
        m_next = (ci >= ri) & has_next
        qc, qn, kp, kc, vp, vc = qc_ref[...], qn_ref[...], kp_ref[...], kc_ref[...], vp_ref[...], vc_ref[...]
        doc, don = doc_ref[...], don_ref[...]
        pc_full = doc.astype(F32) * oc_ref[...].astype(F32)
        pn_full = don.astype(F32) * on_ref[...].astype(F32)
        lc, lnx = lc_ref[...], ln_ref[...]
        dq = jnp.zeros((ATT_BLK, 256), F32)
        dk = jnp.zeros((ATT_BLK, 256), F32)
        dv = jnp.zeros((ATT_BLK, 256), F32)
        zb = jnp.zeros_like(qc)
        for hm in _head_masks():
            qcm, qnm = jnp.where(hm, qc, zb), jnp.where(hm, qn, zb)
            docm, donm = jnp.where(hm, doc, zb), jnp.where(hm, don, zb)
            lse_c = jnp.max(jnp.where(hm, lc, NEG), axis=1, keepdims=True)
            lse_n = jnp.max(jnp.where(hm, lnx, NEG), axis=1, keepdims=True)
            del_c = jnp.sum(jnp.where(hm, pc_full, 0.0), axis=1, keepdims=True)
            del_n = jnp.sum(jnp.where(hm, pn_full, 0.0), axis=1, keepdims=True)
            pr_ = jnp.where(m_cur, jnp.exp(_dg(qcm, kc, 1, 1) * 0.125 - lse_c), 0.0)
            ds = (pr_ * (_dg(docm, vc, 1, 1) - del_c) * 0.125).astype(BF16)
            dqh = _dg(ds, kc, 1, 0)
            dkh = _dg(ds, qc, 0, 0)
            dvh = _dg(pr_.astype(BF16), doc, 0, 0)
            pr_ = jnp.where(m_prev, jnp.exp(_dg(qcm, kp, 1, 1) * 0.125 - lse_c), 0.0)
            ds = (pr_ * (_dg(docm, vp, 1, 1) - del_c) * 0.125).astype(BF16)
            dqh = dqh + _dg(ds, kp, 1, 0)
            pr_ = jnp.where(m_next, jnp.exp(_dg(qnm, kc, 1, 1) * 0.125 - lse_n), 0.0)
            ds = (pr_ * (_dg(donm, vc, 1, 1) - del_n) * 0.125).astype(BF16)
            dkh = dkh + _dg(ds, qn, 0, 0)
            dvh = dvh + _dg(pr_.astype(BF16), don, 0, 0)
            dq = jnp.where(hm, dqh, dq)
            dk = jnp.where(hm, dkh, dk)
            dv = jnp.where(hm, dvh, dv)
        dq_ref[...] = dq.astype(dq_ref.dtype)
        dk_ref[...] = dk.astype(dk_ref.dtype)
        dv_ref[...] = dv.astype(dv_ref.dtype)

    blk = (ATT_BLK, 256)
    cur = lambda col: pl.BlockSpec(blk, lambda pr, n: (n, col(pr)))
    prv = lambda col: pl.BlockSpec(blk, lambda pr, n: (prev(n), col(pr)))
    nx = lambda col: pl.BlockSpec(blk, lambda pr, n: (nxt(n), col(pr)))
    own = lambda pr: pr
    outs = pl.pallas_call(
        body, name=f"attn_bwd_{g}", grid=(r, nblk),
        in_specs=[cur(qcol), nx(qcol), prv(qcol), cur(qcol), prv(vcol), cur(vcol),
                  cur(own), nx(own), cur(own), nx(own), cur(own), nx(own)],
        out_specs=[cur(own), cur(own), cur(own)],
        out_shape=[jax.ShapeDtypeStruct((ln, r * 256), BF16)] * 3,
        compiler_params=_params(("parallel", "parallel")),
    )(qv, qv, kv, kv, pv, pv, dov, dov, ov, ov, lv, lv)
    return [t.reshape(s, 256) for t in outs]


def _gelu_parts(gv):
    cdf = 0.5 * (1.0 + lax.erf(gv * (2.0 ** -0.5)))
    pdf = jnp.exp(-0.5 * gv * gv) * (1.0 / math.sqrt(2.0 * math.pi))
    return cdf, pdf


def _pick_row(t, k):
    row = lax.broadcasted_iota(jnp.int32, t.shape, 0)
    return jnp.sum(jnp.where(row == k, t, 0.0), axis=0, keepdims=True)


def _shift_rows(u, halo, n):
    row = lax.broadcasted_iota(jnp.int32, u.shape, 0)
    out = pltpu.roll(u, n, 0)
    for k in range(n):
        out = jnp.where(row == k, _pick_row(halo, 16 - n + k), out)
    return out


def _shift_rows_up(u, halo, n):
    rb = u.shape[0]
    row = lax.broadcasted_iota(jnp.int32, u.shape, 0)
    out = pltpu.roll(u, rb - n, 0)
    for k in range(n):
        out = jnp.where(row == rb - n + k, _pick_row(halo, k), out)
    return out


def _conv(u, halo, cw, cb):
    return cb + _pick_row(cw, 0) * _shift_rows(u, halo, 2) + _pick_row(cw, 1) * _shift_rows(u, halo, 1) + _pick_row(cw, 2) * u


def _local_step(x, mod, pos_col, target, sm, w):
    s = x.shape[0]
    shift1, scale1, gate1, shift2, scale2, gate2 = [mod[i:i + 1, :] for i in range(6)]
    rb = 256
    zc = lambda c: 0

    def f_norm1(c, i, xv, nw, sc, sh):
        return ((xv * _rms(xv) * nw) * (1.0 + sc) + sh,)

    (h,) = _rowcall(f_norm1, [_rows(x, rb), _full(sm["n1w"]), _full(scale1), _full(shift1)],
                    [_orow(s, D, BF16, rb)], n_rows=s, rb=rb, name="norm1")
    p = _mm(h, w["win"], "in_proj", tm=1024, tn=1536)

    def f_gla_pre(c, i, glr, w2, gb):
        z = _dg(glr, w2.astype(BF16), 1, 0) + gb
        return ((jnp.minimum(z, 0.0) - jnp.log(1.0 + jnp.exp(-jnp.abs(z)))) * (1.0 / GLA_TAU),)

    (la,) = _rowcall(f_gla_pre, [_rows(p, rb, 128, P_LR // 128), _full(sm["w2"]), _full(sm["gb"])],
                     [_orow(s, 512, F32, rb)], n_rows=s, rb=rb, name="gla_pre")
    o_gla, states = _gla_fwd(p, la, s)

    def f_gla_post(c, i, ov, gnw, gr):
        on = jnp.concatenate([ov[:, k * 256:(k + 1) * 256] * _rms(ov[:, k * 256:(k + 1) * 256]) * gnw
                              for k in range(GLA_H)], axis=1)
        g = gr.astype(F32)
        return (on * (g * _sigmoid(g)),)

    (og,) = _rowcall(f_gla_post, [_rows(o_gla, rb), _full(sm["gnw"]), _rows(p, rb, 1024, P_GR // 1024)],
                     [_orow(s, 1024, BF16, rb)], n_rows=s, rb=rb, name="gla_post")
    y_gla = _mm(og, w["wgb"], "gla_branch")

    invf = jnp.tile(ROPE_THETA ** (-jnp.arange(ATT_HD // 2, dtype=F32) / (ATT_HD // 2)), 4).reshape(1, 128)
    cos_t, sin_t = _rope_tables(pos_col, invf, s)

    def f_rope(c, i, aq, ak, av, cs, sn):
        return (_rope_apply(aq.astype(F32), cs, sn, False), _rope_apply(ak.astype(F32), cs, sn, False), av)

    qrot, krot, vatt = _rowcall(f_rope, [_rows(p, rb, ATT_W, P_AQ // ATT_W), _rows(p, rb, ATT_W, P_AK // ATT_W),
                                         _rows(p, rb, ATT_W, P_AV // ATT_W), _rows(cos_t, rb), _rows(sin_t, rb)],
                                [_orow(s, ATT_W, BF16, rb)] * 3, n_rows=s, rb=rb, name="rope")
    att = [_attn_fwd(qrot, krot, vatt, g, r, s) for g, (_, r) in enumerate(ATT_GROUPS)]

    def f_comb(c, i, o0, o1, o2, l0, l1, l2):
        mx = jnp.maximum(jnp.maximum(l0, l1), l2)
        e0, e1, e2 = jnp.exp(l0 - mx), jnp.exp(l1 - mx), jnp.exp(l2 - mx)
        z = e0 + e1 + e2
        o = (e0 * o0.astype(F32) + e1 * o1.astype(F32) + e2 * o2.astype(F32)) / z
        return (o, mx + jnp.log(z))

    o_att, lse = _rowcall(f_comb, [_rows(a[0], 512) for a in att] + [_rows(a[1], 512) for a in att],
                          [_orow(s, 256, BF16, 512), _orow(s, 256, F32, 512)], n_rows=s, rb=512, name="attn_combine")
    y_att = _mm(o_att, w["wab"], "attn_branch")

    def f_merge(c, i, ma, mb, yg, ya):
        return (_sigmoid(ma.astype(F32)) * yg.astype(F32) + _sigmoid(mb.astype(F32)) * ya.astype(F32),)

    (mixed,) = _rowcall(f_merge, [_rows(p, rb, 256, lambda c: P_MA // 256 + c),
                                  _rows(p, rb, 256, lambda c: P_MB // 256 + c),
                                  _rows(y_gla, rb, 256, lambda c: c), _rows(y_att, rb, 256, lambda c: c)],
                        [_orow(s, 1024, BF16, rb, 256, lambda c: c)], n_rows=s, rb=rb, name="merge", ncol=4)
    z1 = _mm(mixed, w["wout"], "out_proj")

    def f_norm2(c, i, xv, z, g1, nw, sc, sh):
        x1 = xv + g1 * z.astype(F32)
        return (x1, (x1 * _rms(x1) * nw) * (1.0 + sc) + sh)

    x1, h2 = _rowcall(f_norm2, [_rows(x, rb), _rows(z1, rb), _full(gate1), _full(sm["n2w"]), _full(scale2), _full(shift2)],
                      [_orow(s, D, F32, rb), _orow(s, D, BF16, rb)], n_rows=s, rb=rb, name="norm2")
    u = _mm(h2, w["wup"], "up_proj", tn=1408)

    cwid = 2 * W_UP_SH

    def f_ffn(c, i, uv, hl, cw, cb):
        uc = _conv(uv.astype(F32), hl.astype(F32) * (i > 0).astype(F32), cw, cb)
        val, gt = uc[:, :W_UP_SH], uc[:, W_UP_SH:]
        cdf, _ = _gelu_parts(gt)
        return (gt * cdf * val,)

    ccol = lambda c: c
    (hidden,) = _rowcall(f_ffn, [_rows(u, rb, cwid, ccol), _halo(u, rb, 16, cwid, ccol, True),
                                 _full(sm["cw"], cwid, ccol), _full(sm["cb"], cwid, ccol)],
                         [_orow(s, D_FF, BF16, rb, W_UP_SH, ccol)], n_rows=s, rb=rb, name="conv_geglu", ncol=2)
    z2 = _mm(hidden, w["wdown"], "down_proj", tk=1408)

    def f_final(c, i, x1v, z, g2, fw, tgt):
        x2 = x1v + g2 * z.astype(F32)
        r = _rms(x2)
        xh = x2 * r
        e = xh * fw - tgt
        loss = 0.5 * jnp.sum(jnp.mean(e * e, axis=-1, keepdims=True), axis=0, keepdims=True)
        dy = e * (1.0 / D)
        dxh = dy * fw
        dx2 = r * (dxh - xh * jnp.mean(dxh * xh, axis=-1, keepdims=True))
        return (loss, dx2, dx2 * g2, _csum(dy * xh), _csum(dx2 * z.astype(F32)))

    loss, dx2, dz2, d_fnw, d_gate2 = _rowcall(
        f_final, [_rows(x1, rb), _rows(z2, rb), _full(gate2), _full(sm["fnw"]), _rows(target, rb)],
        [_oacc(1, 1), _orow(s, D, F32, rb), _orow(s, D, BF16, rb), _oacc(1, D), _oacc(1, D)],
        n_rows=s, rb=rb, name="final_loss")
    d_hidden = _mm(dz2, w["wdown"], "down_proj_dx", tb=True, tn=1408)
    g_wdown = _mm(hidden, dz2, "down_proj_dw", ta=True, out_dtype=F32, tm=1408, tn=1024, tk=512)

    def f_ffn_bwd(c, i, uv, hl, dh, cw, cb):
        uf = uv.astype(F32)
        hf = hl.astype(F32) * (i > 0).astype(F32)
        u1, u2 = _shift_rows(uf, hf, 1), _shift_rows(uf, hf, 2)
        uc = cb + _pick_row(cw, 0) * u2 + _pick_row(cw, 1) * u1 + _pick_row(cw, 2) * uf
        val, gt = uc[:, :W_UP_SH], uc[:, W_UP_SH:]
        cdf, pdf = _gelu_parts(gt)
        dhf = dh.astype(F32)
        duc = jnp.concatenate([dhf * (gt * cdf), dhf * val * (cdf + gt * pdf)], axis=1)
        dcw = jnp.concatenate([_csum(duc * u2), _csum(duc * u1), _csum(duc * uf)], axis=0)
        return (duc, _csum(duc), dcw)

    duc, d_cb, d_cw = _rowcall(
        f_ffn_bwd, [_rows(u, rb, cwid, ccol), _halo(u, rb, 16, cwid, ccol, True), _rows(d_hidden, rb, W_UP_SH, ccol),
                    _full(sm["cw"], cwid, ccol), _full(sm["cb"], cwid, ccol)],
        [_orow(s, 2 * D_FF, BF16, rb, cwid, ccol), _oacc(1, 2 * D_FF, cwid, ccol), _oacc(3, 2 * D_FF, cwid, ccol)],
        n_rows=s, rb=rb, name="conv_geglu_bwd", ncol=2)

    def f_conv_t(c, i, dv, hl, cw):
        df = dv.astype(F32)
        hf = hl.astype(F32) * (i < s // rb - 1).astype(F32)
        return (_pick_row(cw, 2) * df + _pick_row(cw, 1) * _shift_rows_up(df, hf, 1) + _pick_row(cw, 0) * _shift_rows_up(df, hf, 2),)

    (du,) = _rowcall(f_conv_t, [_rows(duc, rb, cwid, ccol), _halo(duc, rb, 16, cwid, ccol, False), _full(sm["cw"], cwid, ccol)],
                     [_orow(s, 2 * D_FF, BF16, rb, cwid, ccol)], n_rows=s, rb=rb, name="conv_transpose", ncol=2)
    d_h2 = _mm(du, w["wup"], "up_proj_dx", tb=True, tk=1408)
    g_wup = _mm(h2, du, "up_proj_dw", ta=True, out_dtype=F32, tm=1024, tn=1408, tk=512)

    def f_norm2_bwd(c, i, x1v, dh, dxr, z, nw, sc, g1):
        dxn, dsh, dsc, dnw = _norm_bwd(x1v, dh.astype(F32), nw, sc)
        dx1 = dxr + dxn
        return (dx1, dx1 * g1, dsh, dsc, dnw, _csum(dx1 * z.astype(F32)))

    dx1, dz1, d_shift2, d_scale2, d_n2w, d_gate1 = _rowcall(
        f_norm2_bwd, [_rows(x1, rb), _rows(d_h2, rb), _rows(dx2, rb), _rows(z1, rb), _full(sm["n2w"]), _full(scale2), _full(gate1)],
        [_orow(s, D, F32, rb), _orow(s, D, BF16, rb), _oacc(1, D), _oacc(1, D), _oacc(1, D), _oacc(1, D)],
        n_rows=s, rb=rb, name="norm2_bwd")
    d_mixed = _mm(dz1, w["wout"], "out_proj_dx", tb=True)
    g_wout = _mm(mixed, dz1, "out_proj_dw", ta=True, out_dtype=F32, tk=512)

    def f_merge_bwd(c, i, dm, ma, mb, yg, ya):
        dmf, ygf, yaf = dm.astype(F32), yg.astype(F32), ya.astype(F32)
        sa, sb = _sigmoid(ma.astype(F32)), _sigmoid(mb.astype(F32))
        return (dmf * sa, dmf * sb, dmf * ygf * sa * (1.0 - sa), dmf * yaf * sb * (1.0 - sb))

    c4 = lambda c: c
    dy_gla, dy_att, d_ma, d_mb = _rowcall(
        f_merge_bwd, [_rows(d_mixed, rb, 256, c4), _rows(p, rb, 256, lambda c: P_MA // 256 + c),
                      _rows(p, rb, 256, lambda c: P_MB // 256 + c), _rows(y_gla, rb, 256, c4), _rows(y_att, rb, 256, c4)],
        [_orow(s, 1024, BF16, rb, 256, c4)] * 4, n_rows=s, rb=rb, name="merge_bwd", ncol=4)
    d_og = _mm(dy_gla, w["wgb"], "gla_branch_dx", tb=True)
    g_wgb = _mm(og, dy_gla, "gla_branch_dw", ta=True, out_dtype=F32, tk=512)
    d_oatt = _mm(dy_att, w["wab"], "attn_branch_dx", tb=True)
    g_wab = _mm(o_att, dy_att, "attn_branch_dw", ta=True, out_dtype=F32, tk=512)

    def f_gla_post_bwd(c, i, ov, gnw, gr, dog):
        g = gr.astype(F32)
        sg = _sigmoid(g)
        silu = g * sg
        dof = dog.astype(F32)
        don = dof * silu
        on_parts, do_parts, dgn = [], [], jnp.zeros((1, 256), F32)
        for k in range(GLA_H):
            oh = ov[:, k * 256:(k + 1) * 256]
            dh = don[:, k * 256:(k + 1) * 256]
            r = _rms(oh)
            xh = oh * r
            dgn = dgn + _csum(dh * xh)
            dxh = dh * gnw
            do_parts.append(r * (dxh - xh * jnp.mean(dxh * xh, axis=-1, keepdims=True)))
            on_parts.append(xh * gnw)
        on = jnp.concatenate(on_parts, axis=1)
        dgr = dof * on * (sg * (1.0 + g * (1.0 - sg)))
        return (jnp.concatenate(do_parts, axis=1), dgr, dgn)

    do_gla, d_gr, d_gnw = _rowcall(
        f_gla_post_bwd, [_rows(o_gla, rb), _full(sm["gnw"]), _rows(p, rb, 1024, P_GR // 1024), _rows(d_og, rb)],
        [_orow(s, 1024, F32, rb), _orow(s, 1024, BF16, rb), _oacc(1, 256)], n_rows=s, rb=rb, name="gla_post_bwd")
    d_gq, d_gk, d_gv, d_la = _gla_bwd(p, la, states, do_gla, s)

    def f_gla_pre_bwd(c, i, lav, dlav, glr, w2):
        dz = dlav * (1.0 / GLA_TAU) * (1.0 - jnp.exp(GLA_TAU * lav))
        dzb = dz.astype(BF16)
        return (_dg(dzb, w2.astype(BF16), 1, 1), _csum(dz), _dg(glr, dzb, 0, 0))

    d_glr, d_gb, d_w2 = _rowcall(
        f_gla_pre_bwd, [_rows(la, rb), _rows(d_la, rb), _rows(p, rb, 128, P_LR // 128), _full(sm["w2"])],
        [_orow(s, 128, BF16, rb), _oacc(1, 512), _oacc(128, 512)], n_rows=s, rb=rb, name="gla_pre_bwd")

    datt = [_attn_bwd(qrot, krot, vatt, d_oatt, o_att, lse, g, r, s) for g, (_, r) in enumerate(ATT_GROUPS)]
    dq_rot = jnp.concatenate([d[0] for d in datt], axis=1)
    dk_rot = jnp.concatenate([d[1] for d in datt], axis=1)
    d_av = jnp.concatenate([d[2] for d in datt], axis=1)

    def f_rope_bwd(c, i, dq, dk, cs, sn):
        return (_rope_apply(dq.astype(F32), cs, sn, True), _rope_apply(dk.astype(F32), cs, sn, True))

    d_aq, d_ak = _rowcall(f_rope_bwd, [_rows(dq_rot, rb), _rows(dk_rot, rb), _rows(cos_t, rb), _rows(sin_t, rb)],
                          [_orow(s, ATT_W, BF16, rb), _orow(s, ATT_W, BF16, rb)], n_rows=s, rb=rb, name="rope_bwd")
    dp = jnp.concatenate([d_gq, d_gk, d_gv, d_gr, d_aq, d_ak, d_av, d_ma, d_mb, d_glr,
                          jnp.zeros((s, P_W - P_LR - 128), BF16)], axis=1)
    d_h = _mm(dp, w["win"], "in_proj_dx", tb=True, tk=1536)
    g_win = _mm(h, dp, "in_proj_dw", ta=True, out_dtype=F32, tm=1024, tn=1536, tk=512)

    def f_norm1_bwd(c, i, xv, dh, dxr, nw, sc):
        dxn, dsh, dsc, dnw = _norm_bwd(xv, dh.astype(F32), nw, sc)
        return (dxr + dxn, dsh, dsc, dnw)

    grad_x, d_shift1, d_scale1, d_n1w = _rowcall(
        f_norm1_bwd, [_rows(x, rb), _rows(d_h, rb), _rows(dx1, rb), _full(sm["n1w"]), _full(scale1)],
        [_orow(s, D, F32, rb), _oacc(1, D), _oacc(1, D), _oacc(1, D)], n_rows=s, rb=rb, name="norm1_bwd")

    dmod = jnp.concatenate([d_shift1, d_scale1, d_gate1, d_shift2, d_scale2, d_gate2], axis=1)
    big = dict(win=g_win, wgb=g_wgb, wab=g_wab, wout=g_wout, wup=g_wup, wdown=g_wdown)
    small = dict(dmod=dmod, n1w=d_n1w, gb=d_gb, gnw=d_gnw, n2w=d_n2w, cb=d_cb, fnw=d_fnw, w2=d_w2, cw=d_cw)
    return loss, grad_x, big, small


def _win_to_kernel(wfull):
    return jnp.concatenate([wfull[:, :3072], wfull[:, 3088:W_IN], wfull[:, 3072:3088],
                            jnp.zeros((D, P_W - W_IN), wfull.dtype)], axis=1)


def _win_from_kernel(g):
    return jnp.concatenate([g[:, :3072], g[:, P_LR:P_LR + GLA_LR], g[:, 3072:P_LR]], axis=1)


def _ff_to_kernel(a):
    h = W_UP_SH
    return jnp.concatenate([a[:, 0:h], a[:, D_FF:D_FF + h], a[:, h:D_FF], a[:, D_FF + h:]], axis=1)


def _ff_from_kernel(a):
    h = W_UP_SH
    return jnp.concatenate([a[:, 0:h], a[:, 2 * h:3 * h], a[:, h:2 * h], a[:, 3 * h:]], axis=1)


BIG = ("w_in", "w_gla_branch", "w_attn_branch", "w_out", "w_up", "w_down")
SH_SHAPES = ((1024, W_IN_SH), (256, 1024), (256, 256), (256, 1024), (1024, W_UP_SH), (W_DOWN_SH, 1024))
N_BIG = len(BIG)


def _cols_join(t):
    return jnp.concatenate([t[k] for k in range(4)], axis=1)


def _cols_split(t):
    cols = t.shape[1] // 4
    return jnp.stack([t[:, k * cols:(k + 1) * cols] for k in range(4)])


def _weights_from_gathered(wg):
    return dict(win=_win_to_kernel(_cols_join(wg[0])), wgb=wg[1].reshape(1024, D), wab=_cols_join(wg[2]),
                wout=wg[3].reshape(D, D), wup=_ff_to_kernel(_cols_join(wg[4])), wdown=wg[5].reshape(D_FF, D))


def _grads_to_shards(big):
    return [_cols_split(_win_from_kernel(big["win"])), big["wgb"].reshape(4, 256, 1024), _cols_split(big["wab"]),
            big["wout"].reshape(4, 256, 1024), _cols_split(_ff_from_kernel(big["wup"])), big["wdown"].reshape(4, W_DOWN_SH, 1024)]


def _me():
    return lax.axis_index("x"), lax.axis_index("y"), lax.axis_index("c")


HBM = pl.BlockSpec(memory_space=pltpu.HBM)
VMEM_SPEC = pl.BlockSpec(memory_space=pltpu.VMEM)


def _allgather8(xs, name):
    rows = xs.shape[0]

    def body(x_ref, out_ref, send_sems, recv_sems, local_sem):
        x, y, c = _me()
        me = 4 * x + 2 * y + c
        mine = pltpu.make_async_copy(x_ref, out_ref.at[me], local_sem)
        mine.start()
        flips = [(k >> 2 & 1, k >> 1 & 1, k & 1) for k in range(1, 8)]

        def peer(f):
            return (jnp.where(f[0] == 1, 1 - x, x), jnp.where(f[1] == 1, 1 - y, y), jnp.where(f[2] == 1, 1 - c, c))

        sends = []
        for k, f in enumerate(flips):
            cp = pltpu.make_async_remote_copy(src_ref=x_ref, dst_ref=out_ref.at[me], send_sem=send_sems.at[k],
                                              recv_sem=recv_sems.at[k], device_id=peer(f), device_id_type=MESH)
            cp.start()
            sends.append(cp)
        for k, f in enumerate(flips):
            px, py, pc = peer(f)
            pltpu.make_async_remote_copy(src_ref=x_ref, dst_ref=out_ref.at[4 * px + 2 * py + pc], send_sem=send_sems.at[k],
                                         recv_sem=recv_sems.at[k], device_id=peer(f), device_id_type=MESH).wait_recv()
        for cp in sends:
            cp.wait_send()
        mine.wait()

    return pl.pallas_call(
        body, name=name, out_shape=jax.ShapeDtypeStruct((8, rows, 128), F32),
        in_specs=[VMEM_SPEC], out_specs=VMEM_SPEC,
        scratch_shapes=[pltpu.SemaphoreType.DMA((7,)), pltpu.SemaphoreType.DMA((7,)), pltpu.SemaphoreType.DMA],
        compiler_params=pltpu.CompilerParams(vmem_limit_bytes=VMEM_LIMIT),
    )(xs)


def _half_rows(i, cc, unit):
    rows = SH_SHAPES[i][0] // 2
    return pl.ds(pl.multiple_of(cc * rows, unit), rows)


def _gather_weights(ws):
    def body(*refs):
        w_refs, out_refs = refs[:N_BIG], refs[N_BIG:2 * N_BIG]
        send_sems, recv_sems = refs[2 * N_BIG:]
        x, y, c = _me()
        s_me = 2 * x + y
        sibling = (x, y, 1 - c)
        chips = [(1 - x, y), (x, 1 - y), (1 - x, 1 - y)]

        def copy(k, i, src, chip_of_block, cc, to):
            return pltpu.make_async_remote_copy(src_ref=src, dst_ref=out_refs[i].at[chip_of_block, _half_rows(i, cc, 16)],
                                                send_sem=send_sems.at[k], recv_sem=recv_sems.at[k], device_id=to, device_id_type=MESH)

        first = [copy(j * N_BIG + i, i, w_refs[i].at[_half_rows(i, c, 16)], s_me, c, (*chip, c))
                 for j, chip in enumerate(chips) for i in range(N_BIG)]
        for cp in first:
            cp.start()
        passed = []
        for j, (px, py) in enumerate(chips):
            s_j = 2 * px + py
            for i in range(N_BIG):
                copy(j * N_BIG + i, i, w_refs[i].at[_half_rows(i, c, 16)], s_j, c, (px, py, c)).wait_recv()
                cp = copy((3 + j) * N_BIG + i, i, out_refs[i].at[s_j, _half_rows(i, c, 16)], s_j, c, sibling)
                cp.start()
                passed.append(cp)
        for j, (px, py) in enumerate(chips):
            for i in range(N_BIG):
                copy((3 + j) * N_BIG + i, i, w_refs[i].at[_half_rows(i, c, 16)], 2 * px + py, 1 - c, sibling).wait_recv()
        for cp in first + passed:
            cp.wait_send()

    return pl.pallas_call(
        body, name="gather_weights", out_shape=[jax.ShapeDtypeStruct((4,) + shp, BF16) for shp in SH_SHAPES],
        in_specs=[HBM] * N_BIG, out_specs=[HBM] * N_BIG,
        scratch_shapes=[pltpu.SemaphoreType.DMA((6 * N_BIG,)), pltpu.SemaphoreType.DMA((6 * N_BIG,))],
    )(*ws)


def _pair_send(gs):
    def body(*refs):
        g_refs, land_refs = refs[:N_BIG], refs[N_BIG:2 * N_BIG]
        send_sems, recv_sems = refs[2 * N_BIG:]
        x, y, c = _me()
        cps = []
        for i in range(N_BIG):
            for sh in range(4):
                cp = pltpu.make_async_remote_copy(src_ref=g_refs[i].at[sh, _half_rows(i, 1 - c, 8)], dst_ref=land_refs[i].at[sh],
                                                  send_sem=send_sems.at[4 * i + sh], recv_sem=recv_sems.at[4 * i + sh],
                                                  device_id=(x, y, 1 - c), device_id_type=MESH)
                cp.start()
                cps.append(cp)
        for cp in cps:
            cp.wait()

    return pl.pallas_call(
        body, name="grad_pair_send", out_shape=[jax.ShapeDtypeStruct((4, r // 2, cl), F32) for r, cl in SH_SHAPES],
        in_specs=[HBM] * N_BIG, out_specs=[HBM] * N_BIG,
        scratch_shapes=[pltpu.SemaphoreType.DMA((4 * N_BIG,)), pltpu.SemaphoreType.DMA((4 * N_BIG,))],
    )(*gs)


def _pair_add(g, land, core, name):
    _, rows, cols = g.shape
    half = rows // 2
    rb = _tile(half, 256, 8)
    nb = half // rb

    def body(c_ref, g_ref, l_ref, o_ref):
        o_ref[...] = g_ref[...] + l_ref[...]

    return pl.pallas_call(
        body, name=name,
        grid_spec=pltpu.PrefetchScalarGridSpec(
            num_scalar_prefetch=1, grid=(4, nb),
            in_specs=[pl.BlockSpec((1, rb, cols), lambda s, i, c_ref: (s, c_ref[0] * nb + i, 0)),
                      pl.BlockSpec((1, rb, cols), lambda s, i, c_ref: (s, i, 0))],
            out_specs=pl.BlockSpec((1, rb, cols), lambda s, i, c_ref: (s, i, 0))),
        out_shape=jax.ShapeDtypeStruct((4, half, cols), F32),
        compiler_params=_params(("parallel", "parallel")),
    )(core, g, land)


def _chip_exchange(ts):
    def body(*refs):
        t_refs, r_refs = refs[:N_BIG], refs[N_BIG:2 * N_BIG]
        send_sems, recv_sems = refs[2 * N_BIG:]
        x, y, c = _me()
        chips = [(1 - x, y), (x, 1 - y), (1 - x, 1 - y)]
        sends = []
        for j, (px, py) in enumerate(chips):
            for i in range(N_BIG):
                cp = pltpu.make_async_remote_copy(src_ref=t_refs[i].at[2 * px + py], dst_ref=r_refs[i].at[j],
                                                  send_sem=send_sems.at[j * N_BIG + i], recv_sem=recv_sems.at[j * N_BIG + i],
                                                  device_id=(px, py, c), device_id_type=MESH)
                cp.start()
                sends.append(cp)
        for cp in sends:
            cp.wait()

    return pl.pallas_call(
        body, name="grad_chip_exchange", out_shape=[jax.ShapeDtypeStruct((3,) + t.shape[1:], t.dtype) for t in ts],
        in_specs=[HBM] * N_BIG, out_specs=[HBM] * N_BIG,
        scratch_shapes=[pltpu.SemaphoreType.DMA((3 * N_BIG,)), pltpu.SemaphoreType.DMA((3 * N_BIG,))],
    )(*ts)


def _chip_sum(t, r, chip, name):
    _, half, cols = t.shape
    rb = _tile(half, 256, 8)

    def body(s_ref, t_ref, r_ref, o_ref):
        o_ref[...] = ((t_ref[0] + r_ref[0]) + r_ref[1]) + r_ref[2]

    return pl.pallas_call(
        body, name=name,
        grid_spec=pltpu.PrefetchScalarGridSpec(
            num_scalar_prefetch=1, grid=(half // rb,),
            in_specs=[pl.BlockSpec((1, rb, cols), lambda i, s_ref: (s_ref[0], i, 0)),
                      pl.BlockSpec((3, rb, cols), lambda i, s_ref: (0, i, 0))],
            out_specs=pl.BlockSpec((rb, cols), lambda i, s_ref: (i, 0))),
        out_shape=jax.ShapeDtypeStruct((half, cols), F32),
        compiler_params=_params(("parallel",)),
    )(chip, t, r)


def _pair_join(hs):
    def body(*refs):
        h_refs, o_refs = refs[:N_BIG], refs[N_BIG:2 * N_BIG]
        send_sems, recv_sems = refs[2 * N_BIG:]
        x, y, c = _me()
        cps = []
        for i in range(N_BIG):
            cp = pltpu.make_async_remote_copy(src_ref=h_refs[i], dst_ref=o_refs[i], send_sem=send_sems.at[i], recv_sem=recv_sems.at[i],
                                              device_id=(x, y, 1 - c), device_id_type=MESH)
            cp.start()
            cps.append(cp)
        for cp in cps:
            cp.wait()

    return pl.pallas_call(
        body, name="grad_pair_join", out_shape=[jax.ShapeDtypeStruct(h.shape, F32) for h in hs],
        in_specs=[HBM] * N_BIG, out_specs=[HBM] * N_BIG,
        scratch_shapes=[pltpu.SemaphoreType.DMA((N_BIG,)), pltpu.SemaphoreType.DMA((N_BIG,))],
    )(*hs)


def _adam_math(wv, gv, mv, vv):
    mn = ADAM_B1 * mv + (1.0 - ADAM_B1) * gv
    vn = ADAM_B2 * vv + (1.0 - ADAM_B2) * (gv * gv)
    m_hat = mn / (1.0 - ADAM_B1 ** ADAM_STEP)
    v_hat = vn / (1.0 - ADAM_B2 ** ADAM_STEP)
    return -ADAM_LR * (m_hat / (jnp.sqrt(v_hat) + ADAM_EPS) + ADAM_WD * wv), mn, vn


def _adamw_halves(wt, mt, vt, mine, theirs, core, name):
    rows, cols = wt.shape
    half = rows // 2
    rb = _tile(half, 256, 8)
    nb = half // rb

    def body(c_ref, w_ref, m_ref, v_ref, a_ref, b_ref, g_ref, d_ref, mo_ref, vo_ref):
        gv = jnp.where(pl.program_id(0) == c_ref[0], a_ref[...], b_ref[...])
        dl, mn, vn = _adam_math(w_ref[...], gv, m_ref[...], v_ref[...])
        g_ref[...] = gv
        d_ref[...] = dl
        mo_ref[...] = mn
        vo_ref[...] = vn

    full = pl.BlockSpec((rb, cols), lambda hf, i, c_ref: (hf * nb + i, 0))
    part = pl.BlockSpec((rb, cols), lambda hf, i, c_ref: (i, 0))
    return pl.pallas_call(
        body, name=name,
        grid_spec=pltpu.PrefetchScalarGridSpec(num_scalar_prefetch=1, grid=(2, nb), in_specs=[full, full, full, part, part],
                                               out_specs=[full] * 4),
        out_shape=[jax.ShapeDtypeStruct((rows, cols), F32)] * 4,
        compiler_params=_params(("parallel", "parallel")),
    )(core, wt, mt, vt, mine, theirs)


SG_REP = 136
SG_W2, SG_CW = SG_REP, SG_REP + 4 * 16
SG_ROWS = SG_CW + 4 * 40
SP_ROWS = SG_REP + 16 + 40


def _mod_shard(c_all, ada_w_sh):
    def body(c_ref, w_ref, o_ref):
        cv = c_ref[...]
        o_ref[...] = _dg((cv * _sigmoid(cv)).astype(BF16), w_ref[...].astype(BF16), 1, 0)

    return pl.pallas_call(body, name="mod_shard", out_shape=jax.ShapeDtypeStruct((8, 1536), F32),
                          in_specs=[VMEM_SPEC, VMEM_SPEC], out_specs=VMEM_SPEC,
                          compiler_params=pltpu.CompilerParams(vmem_limit_bytes=VMEM_LIMIT))(c_all, ada_w_sh)


def _mod_select(mod_all, ada_b4):
    def body(m_ref, b_ref, o_ref):
        x, y, c = _me()
        me = 4 * x + 2 * y + c
        for sh in range(4):
            o_ref[sh] = m_ref[2 * sh, me] + b_ref[sh]

    return pl.pallas_call(body, name="mod_select", out_shape=jax.ShapeDtypeStruct((4, 12, 128), F32),
                          in_specs=[VMEM_SPEC, VMEM_SPEC], out_specs=VMEM_SPEC)(mod_all, ada_b4)


def _small_reduce(sg_all):
    def body(g_ref, o_ref):
        x, y, c = _me()
        s_me = 2 * x + y
        w2_rows = pl.ds(pl.multiple_of(SG_W2 + 16 * s_me, 8), 16)
        cw_rows = pl.ds(pl.multiple_of(SG_CW + 40 * s_me, 8), 40)
        a = g_ref[0, 0:SG_REP, :]
        b = g_ref[0, w2_rows, :]
        d = g_ref[0, cw_rows, :]
        for dev in range(1, 8):
            a = a + g_ref[dev, 0:SG_REP, :]
            b = b + g_ref[dev, w2_rows, :]
            d = d + g_ref[dev, cw_rows, :]
        o_ref[0:SG_REP, :] = a
        o_ref[SG_REP:SG_REP + 16, :] = b
        o_ref[SG_REP + 16:SP_ROWS, :] = d

    return pl.pallas_call(body, name="small_grad_reduce", out_shape=jax.ShapeDtypeStruct((SP_ROWS, 128), F32),
                          in_specs=[VMEM_SPEC], out_specs=VMEM_SPEC)(sg_all)


def _ada_grad(dmod_all, c_bc):
    def body(g_ref, c_ref, o_ref):
        x, y, c = _me()
        s_me = 2 * x + y
        for k in range(12):
            acc = jnp.zeros((D, 128), F32)
            for b in range(8):
                cv = c_ref[b]
                acc = acc + (cv * _sigmoid(cv)) * g_ref[s_me, k, b:b + 1, :]
            o_ref[:, k * 128:(k + 1) * 128] = acc

    return pl.pallas_call(body, name="ada_w_grad", out_shape=jax.ShapeDtypeStruct((D, 1536), F32),
                          in_specs=[VMEM_SPEC, VMEM_SPEC], out_specs=VMEM_SPEC,
                          compiler_params=pltpu.CompilerParams(vmem_limit_bytes=VMEM_LIMIT))(dmod_all, c_bc)


def _adamw(wt, g, m, v, name):
    rows, cols = wt.shape
    rb = _tile(rows, 256, 8)

    def fn(c, i, wv, gv, mv, vv):
        return _adam_math(wv, gv, mv, vv)

    return _rowcall(fn, [_rows(t, rb) for t in (wt, g, m, v)], [_orow(rows, cols, F32, rb)] * 3,
                    n_rows=rows, rb=rb, name=name)


def _pad_rows(t, rows):
    flat = t.reshape(-1)
    return jnp.pad(flat, (0, rows * 128 - flat.shape[0])).reshape(rows, 128)


SP_LAYOUT = (("ada_b", 48), ("norm1_w", 8), ("gla_gate_b", 8), ("gla_norm_w", 8), ("norm2_w", 8), ("conv_b", 48),
             ("final_norm_w", 8), ("gla_gate_w2", 16), ("conv_w", 40))


def _pack_small(d):
    return jnp.concatenate([_pad_rows(d[n].astype(F32), rows) for n, rows in SP_LAYOUT], axis=0)


def _unpack_small(pk, shapes):
    out, off = {}, 0
    for n, rows in SP_LAYOUT:
        shp = shapes[n]
        out[n] = pk[off:off + rows].reshape(-1)[:math.prod(shp)].reshape(shp)
        off += rows
    return out


def kernel(x, c, positions, ada_w, ada_b, norm1_w, w_in, gla_gate_w2, gla_gate_b, gla_norm_w, w_gla_branch, w_attn_branch, w_out, norm2_w, w_up, conv_w, conv_b, w_down, final_norm_w, loss_target, m_ada_w, m_ada_b, m_norm1_w, m_w_in, m_gla_gate_w2, m_gla_gate_b, m_gla_norm_w, m_w_gla_branch, m_w_attn_branch, m_w_out, m_norm2_w, m_w_up, m_conv_w, m_conv_b, m_w_down, m_final_norm_w, v_ada_w, v_ada_b, v_norm1_w, v_w_in, v_gla_gate_w2, v_gla_gate_b, v_gla_norm_w, v_w_gla_branch, v_w_attn_branch, v_w_out, v_norm2_w, v_w_up, v_conv_w, v_conv_b, v_w_down, v_final_norm_w):
    s = x.shape[1]
    names = ("ada_w", "ada_b", "norm1_w", "w_in", "gla_gate_w2", "gla_gate_b", "gla_norm_w", "w_gla_branch", "w_attn_branch",
             "w_out", "norm2_w", "w_up", "conv_w", "conv_b", "w_down", "final_norm_w")
    wts = dict(zip(names, (ada_w, ada_b, norm1_w, w_in, gla_gate_w2, gla_gate_b, gla_norm_w, w_gla_branch, w_attn_branch,
                           w_out, norm2_w, w_up, conv_w, conv_b, w_down, final_norm_w)))
    ms = dict(zip(names, (m_ada_w, m_ada_b, m_norm1_w, m_w_in, m_gla_gate_w2, m_gla_gate_b, m_gla_norm_w, m_w_gla_branch,
                          m_w_attn_branch, m_w_out, m_norm2_w, m_w_up, m_conv_w, m_conv_b, m_w_down, m_final_norm_w)))
    vs = dict(zip(names, (v_ada_w, v_ada_b, v_norm1_w, v_w_in, v_gla_gate_w2, v_gla_gate_b, v_gla_norm_w, v_w_gla_branch,
                          v_w_attn_branch, v_w_out, v_norm2_w, v_w_up, v_conv_w, v_conv_b, v_w_down, v_final_norm_w)))

    pk0 = jnp.concatenate([_pad_rows(c, 8), _pad_rows(gla_gate_w2, 16), _pad_rows(conv_w, 40)], axis=0)
    sm_all = _allgather8(pk0, "gather_small")
    c_all = sm_all[:, 0:8, :].reshape(8, D)
    w2_full = sm_all[0::2, 8:24, :].transpose(1, 0, 2).reshape(GLA_LR, 512)
    cw_full = sm_all[0::2, 24:64, :].reshape(4, 40 * 128)[:, :3 * W_UP_SH].reshape(4, 3, W_UP_SH).transpose(1, 0, 2).reshape(3, 2 * D_FF)

    mod_sh = _mod_shard(c_all, ada_w[0])
    mod_all = _allgather8(mod_sh.reshape(96, 128), "gather_mod")
    mod = _mod_select(mod_all.reshape(8, 8, 12, 128), ada_b.reshape(4, 12, 128)).reshape(6, D)

    core = lax.axis_index("c").astype(jnp.int32).reshape(1)
    chip = (2 * lax.axis_index("x") + lax.axis_index("y")).astype(jnp.int32)
    w_sh = [wts[n][0].astype(BF16) for n in BIG]
    w = _weights_from_gathered([lax.dynamic_update_slice(got, own[None], (chip, 0, 0))
                                for got, own in zip(_gather_weights(w_sh), w_sh)])

    sm = dict(n1w=norm1_w, n2w=norm2_w, fnw=final_norm_w.reshape(1, D), gnw=gla_norm_w, gb=gla_gate_b,
              w2=jnp.pad(w2_full, ((0, 128 - GLA_LR), (0, 0))), cw=_ff_to_kernel(cw_full), cb=_ff_to_kernel(conv_b))
    loss, grad_x, big, small = _local_step(x[0], mod, positions.reshape(s, 1), loss_target[0], sm, w)

    gs = _grads_to_shards(big)
    ts = [_pair_add(g, land, core, "grad_pair_add_" + n) for g, land, n in zip(gs, _pair_send(gs), BIG)]
    halves = [_chip_sum(t, r, chip.reshape(1), "grad_chip_sum_" + n) for t, r, n in zip(ts, _chip_exchange(ts), BIG)]
    others = _pair_join(halves)

    dcw = _ff_from_kernel(small["cw"]).reshape(3, 4, W_UP_SH).transpose(1, 0, 2)
    dw2 = small["w2"][:GLA_LR].reshape(GLA_LR, 4, 128).transpose(1, 0, 2)
    sg = jnp.concatenate(
        [_pad_rows(small["dmod"], 48), _pad_rows(small["n1w"], 8), _pad_rows(small["gb"], 8), _pad_rows(small["gnw"], 8),
         _pad_rows(small["n2w"], 8), _pad_rows(_ff_from_kernel(small["cb"]), 48), _pad_rows(small["fnw"], 8)]
        + [_pad_rows(dw2[k], 16) for k in range(4)] + [_pad_rows(dcw[k], 40) for k in range(4)], axis=0)
    sg_all = _allgather8(sg, "gather_small_grads")
    g_small_pk = _small_reduce(sg_all)
    dmod_all = sg_all[:, 0:48, :].reshape(8, 4, 12, 128).transpose(1, 2, 0, 3)
    g_ada_w = _ada_grad(dmod_all, jnp.broadcast_to(c_all[:, :, None], (8, D, 128)))

    shapes = {n: wts[n].shape for n in names}
    g_small = _unpack_small(g_small_pk, shapes)
    grads = {"ada_w": g_ada_w.reshape(1, D, 1536), **g_small}
    deltas, new_m, new_v = {}, {}, {}
    for n, mine, theirs in zip(BIG, halves, others):
        shp = wts[n].shape
        g_, d_, m_, v_ = _adamw_halves(wts[n][0], ms[n][0], vs[n][0], mine, theirs, core, "adamw_" + n)
        grads[n], deltas[n], new_m[n], new_v[n] = g_.reshape(shp), d_.reshape(shp), m_.reshape(shp), v_.reshape(shp)
    shp = ada_w.shape
    d_, m_, v_ = _adamw(ada_w[0], g_ada_w, m_ada_w[0], v_ada_w[0], "adamw_ada_w")
    deltas["ada_w"], new_m["ada_w"], new_v["ada_w"] = d_.reshape(shp), m_.reshape(shp), v_.reshape(shp)
    d_, m_, v_ = _adamw(_pack_small(wts), g_small_pk, _pack_small(ms), _pack_small(vs), "adamw_small")
    for dst, pk in ((deltas, d_), (new_m, m_), (new_v, v_)):
        dst.update(_unpack_small(pk, shapes))

    loss_all = lax.psum(loss[0, 0], ("x", "y", "c"))
    return (loss_all, grad_x.reshape(1, s, D), *[grads[n] for n in names], *[deltas[n] for n in names],
            *[new_m[n] for n in names], *[new_v[n] for n in names])
```

```python
import math

import jax
import jax.numpy as jnp
from jax import lax
from jax.experimental import pallas as pl
from jax.experimental.pallas import tpu as pltpu

F32, BF16 = jnp.float32, jnp.bfloat16
MESH = pl.DeviceIdType.MESH

D = 1024
EPS = 1e-6
GLA_H, GLA_DK, GLA_DV, GLA_LR = 4, 128, 256, 16
GLA_TAU = 16.0
GLA_CHUNK = 64
GLA_BLOCK = 512
ATT_GROUPS = ((128, 1), (512, 4), (2048, 16))
ATT_BLK = 128
ATT_HD = 64
ATT_W = 768
D_FF = 2816
ROPE_THETA = 10000.0
P_W = 7680
P_GV, P_GR, P_MA, P_MB, P_GQ, P_GK, P_AQ, P_AK, P_AV, P_LR = 0, 1024, 2048, 3072, 4096, 4608, 5120, 5888, 6656, 7424
W_IN = 7440
W_IN_SH, W_UP_SH, W_DOWN_SH = 1860, 1408, 704
VMEM_LIMIT = 56 * 1024 * 1024
ADAM_LR, ADAM_B1, ADAM_B2, ADAM_EPS, ADAM_WD, ADAM_STEP = 0.001, 0.9, 0.999, 1e-08, 0.01, 10
NEG = -1e30


def _tile(n, target, unit=128):
    best = None
    for t in range(unit, min(n, target) + 1, unit):
        if n % t == 0:
            best = t
    return best or n


def _params(sem):
    return pltpu.CompilerParams(dimension_semantics=sem, vmem_limit_bytes=VMEM_LIMIT)


def _dg(a, b, ca, cb):
    return lax.dot_general(a, b, (((ca,), (cb,)), ((), ())), preferred_element_type=F32)


def _sigmoid(v):
    return 1.0 / (1.0 + jnp.exp(-v))


def _mm(a, b, name, *, ta=False, tb=False, out_dtype=BF16, tm=1024, tn=1536, tk=1024, n_outer=True):
    m = a.shape[1] if ta else a.shape[0]
    k = a.shape[0] if ta else a.shape[1]
    n = b.shape[0] if tb else b.shape[1]
    tm, tn, tk = _tile(m, tm), _tile(n, tn), _tile(k, tk)
    nm, nn, nk = m // tm, n // tn, k // tk
    in_out = out_dtype == F32

    def body(a_ref, b_ref, o_ref, *scr):
        kk = pl.program_id(2)
        p = _dg(a_ref[...].astype(BF16), b_ref[...].astype(BF16), 0 if ta else 1, 1 if tb else 0)
        if nk == 1:
            o_ref[...] = p.astype(o_ref.dtype)
        else:
            acc = o_ref if in_out else scr[0]

            @pl.when(kk == 0)
            def _():
                acc[...] = p

            @pl.when(kk > 0)
            def _():
                acc[...] += p

            if not in_out:
                @pl.when(kk == nk - 1)
                def _():
                    o_ref[...] = acc[...].astype(o_ref.dtype)

    if n_outer:
        ij = lambda g0, g1: (g1, g0)
        grid = (nn, nm, nk)
    else:
        ij = lambda g0, g1: (g0, g1)
        grid = (nm, nn, nk)
    a_map = (lambda g0, g1, kk: (kk, ij(g0, g1)[0])) if ta else (lambda g0, g1, kk: (ij(g0, g1)[0], kk))
    b_map = (lambda g0, g1, kk: (ij(g0, g1)[1], kk)) if tb else (lambda g0, g1, kk: (kk, ij(g0, g1)[1]))
    return pl.pallas_call(
        body, name=name, grid=grid,
        in_specs=[pl.BlockSpec((tk, tm) if ta else (tm, tk), a_map),
                  pl.BlockSpec((tn, tk) if tb else (tk, tn), b_map)],
        out_specs=pl.BlockSpec((tm, tn), lambda g0, g1, kk: ij(g0, g1)),
        out_shape=jax.ShapeDtypeStruct((m, n), out_dtype),
        scratch_shapes=[] if (in_out or nk == 1) else [pltpu.VMEM((tm, tn), F32)],
        compiler_params=_params(("parallel", "parallel", "arbitrary")),
    )(a, b)


def _rows(arr, rb, w=None, j=0):
    w = arr.shape[1] if w is None else w
    if callable(j):
        return arr, pl.BlockSpec((rb, w), lambda c, i: (i, j(c)))
    return arr, pl.BlockSpec((rb, w), lambda c, i: (i, j))


def _full(arr, w=None, j=0):
    w = arr.shape[1] if w is None else w
    if callable(j):
        return arr, pl.BlockSpec((arr.shape[0], w), lambda c, i: (0, j(c)))
    return arr, pl.BlockSpec((arr.shape[0], w), lambda c, i: (0, j))


def _halo(arr, rb, hb, w, j, before):
    per = rb // hb
    last = arr.shape[0] // hb - 1
    if before:
        rmap = lambda i: jnp.maximum(i * per - 1, 0)
    else:
        rmap = lambda i: jnp.minimum((i + 1) * per, last)
    return arr, pl.BlockSpec((hb, w), lambda c, i: (rmap(i), j(c) if callable(j) else j))


def _rowcall(fn, ins, outs, *, n_rows, rb, name, ncol=1):
    n_in = len(ins)
    nr = n_rows // rb

    def body(*refs):
        c, i = pl.program_id(0), pl.program_id(1)
        res = fn(c, i, *[r[...] for r in refs[:n_in]])
        for val, spec, o_ref in zip(res, outs, refs[n_in:]):
            if spec[2] == "row":
                o_ref[...] = val.astype(o_ref.dtype)
            else:
                @pl.when(i == 0)
                def _(o_ref=o_ref, val=val):
                    o_ref[...] = val.astype(o_ref.dtype)

                @pl.when(i > 0)
                def _(o_ref=o_ref, val=val):
                    o_ref[...] += val.astype(o_ref.dtype)

    out_specs = []
    for shape, dt, kind, block, col in outs:
        if kind == "row":
            out_specs.append(pl.BlockSpec(block, lambda c, i, col=col: (i, col(c))))
        else:
            out_specs.append(pl.BlockSpec(block, lambda c, i, col=col: (0, col(c))))
    return pl.pallas_call(
        body, name=name, grid=(ncol, nr),
        in_specs=[s for _, s in ins], out_specs=out_specs,
        out_shape=[jax.ShapeDtypeStruct(o[0], o[1]) for o in outs],
        compiler_params=_params(("parallel", "arbitrary")),
    )(*[a for a, _ in ins])


def _orow(n_rows, w, dt, rb, bw=None, col=lambda c: 0):
    return ((n_rows, w), dt, "row", (rb, bw or w), col)


def _oacc(r, w, bw=None, col=lambda c: 0):
    return ((r, w), F32, "acc", (r, bw or w), col)


def _csum(v):
    return jnp.sum(v, axis=0, keepdims=True)


def _rms(v):
    return lax.rsqrt(jnp.mean(v * v, axis=-1, keepdims=True) + EPS)


def _norm_bwd(xv, dh, w, scale):
    r = _rms(xv)
    xh = xv * r
    dxh = dh * (w * (1.0 + scale))
    dx = r * (dxh - xh * jnp.mean(dxh * xh, axis=-1, keepdims=True))
    t = dh * xh
    return dx, _csum(dh), _csum(t * w), _csum(t * (1.0 + scale))


def _rope_tables(pos_col, invf, s):
    def fn(c, i, pos, f):
        ang = pos.astype(F32) * f
        lane = lax.broadcasted_iota(jnp.int32, ang.shape, 1)
        sign = jnp.where((lane % ATT_HD) < ATT_HD // 2, -1.0, 1.0)
        return jnp.cos(ang), jnp.sin(ang) * sign

    rb = 512
    return _rowcall(fn, [_rows(pos_col, rb), _full(invf)], [_orow(s, 128, F32, rb), _orow(s, 128, F32, rb)],
                    n_rows=s, rb=rb, name="rope_tables")


def _swap_halves(t):
    n = t.shape[1]
    lane = lax.broadcasted_iota(jnp.int32, t.shape, 1)
    return jnp.where((lane % ATT_HD) < ATT_HD // 2, pltpu.roll(t, n - 32, 1), pltpu.roll(t, 32, 1))


def _rope_apply(t, cos, sin_signed, inverse):
    cw = jnp.concatenate([cos] * (t.shape[1] // 128), axis=1)
    sw = jnp.concatenate([sin_signed] * (t.shape[1] // 128), axis=1)
    if inverse:
        sw = -sw
    return t * cw + _swap_halves(t) * sw


def _gla_decays(la_c, tri):
    b = jnp.dot(tri, la_c, precision=lax.Precision.HIGHEST, preferred_element_type=F32)
    row = lax.broadcasted_iota(jnp.int32, b.shape, 0)
    bmid = jnp.sum(jnp.where(row == GLA_CHUNK // 2 - 1, b, 0.0), axis=0, keepdims=True)
    blast = jnp.sum(jnp.where(row == GLA_CHUNK - 1, b, 0.0), axis=0, keepdims=True)
    return b, bmid, blast


def _gla_fwd(p, la, s):
    tb, ch = GLA_BLOCK, GLA_CHUNK
    nb, nc = s // tb, tb // ch
    scale = GLA_DK ** -0.5

    def body(q_ref, k_ref, v_ref, la_ref, o_ref, st_ref, state):
        @pl.when(pl.program_id(1) == 0)
        def _():
            state[...] = jnp.zeros_like(state)

        ri = lax.broadcasted_iota(jnp.int32, (ch, ch), 0)
        ci = lax.broadcasted_iota(jnp.int32, (ch, ch), 1)
        causal = ci <= ri
        tri = causal.astype(F32)
        for c in range(nc):
            sl = pl.ds(c * ch, ch)
            b, bmid, blast = _gla_decays(la_ref[sl, :], tri)
            q = q_ref[sl, :].astype(F32) * scale
            k = k_ref[sl, :].astype(F32)
            v = v_ref[sl, :]
            qgt = (q * jnp.exp(b)).astype(BF16)
            qgn = (q * jnp.exp(b - bmid)).astype(BF16)
            kgn = (k * jnp.exp(bmid - b)).astype(BF16)
            kd = (k * jnp.exp(blast - b)).astype(BF16)
            a = jnp.where(causal, _dg(qgn, kgn, 1, 1), 0.0)
            st = state[...]
            st_ref[0, c] = st
            o_ref[sl, :] = _dg(a.astype(BF16), v, 1, 0) + _dg(qgt, st.astype(BF16), 1, 1)
            state[...] = jnp.exp(blast) * st + _dg(v, kd, 0, 0)

    return pl.pallas_call(
        body, name="gla_fwd", grid=(GLA_H, nb),
        in_specs=[pl.BlockSpec((tb, GLA_DK), lambda h, t: (t, P_GQ // GLA_DK + h)),
                  pl.BlockSpec((tb, GLA_DK), lambda h, t: (t, P_GK // GLA_DK + h)),
                  pl.BlockSpec((tb, GLA_DV), lambda h, t: (t, P_GV // GLA_DV + h)),
                  pl.BlockSpec((tb, GLA_DK), lambda h, t: (t, h))],
        out_specs=[pl.BlockSpec((tb, GLA_DV), lambda h, t: (t, h)),
                   pl.BlockSpec((1, nc, GLA_DV, GLA_DK), lambda h, t: (h, t, 0, 0))],
        out_shape=[jax.ShapeDtypeStruct((s, GLA_H * GLA_DV), F32),
                   jax.ShapeDtypeStruct((GLA_H, s // ch, GLA_DV, GLA_DK), F32)],
        scratch_shapes=[pltpu.VMEM((GLA_DV, GLA_DK), F32)],
        compiler_params=_params(("parallel", "arbitrary")),
    )(p, p, p, la)


def _gla_bwd(p, la, states, do, s):
    tb, ch = GLA_BLOCK, GLA_CHUNK
    nb, nc = s // tb, tb // ch
    scale = GLA_DK ** -0.5

    def body(q_ref, k_ref, v_ref, la_ref, st_ref, do_ref, dq_ref, dk_ref, dv_ref, dla_ref, dstate):
        @pl.when(pl.program_id(1) == 0)
        def _():
            dstate[...] = jnp.zeros_like(dstate)

        ri = lax.broadcasted_iota(jnp.int32, (ch, ch), 0)
        ci = lax.broadcasted_iota(jnp.int32, (ch, ch), 1)
        causal = ci <= ri
        tri = causal.astype(F32)
        tri_t = (ci >= ri).astype(F32)
        for c in reversed(range(nc)):
            sl = pl.ds(c * ch, ch)
            b, bmid, blast = _gla_decays(la_ref[sl, :], tri)
            q = q_ref[sl, :].astype(F32) * scale
            k = k_ref[sl, :].astype(F32)
            v = v_ref[sl, :]
            e_b, e_qn, e_kn, e_kd = jnp.exp(b), jnp.exp(b - bmid), jnp.exp(bmid - b), jnp.exp(blast - b)
            dec = jnp.exp(blast)
            qgt, qgn, kgn, kd = q * e_b, q * e_qn, k * e_kn, k * e_kd
            qgt_b, qgn_b, kgn_b, kd_b = qgt.astype(BF16), qgn.astype(BF16), kgn.astype(BF16), kd.astype(BF16)
            st0 = st_ref[0, c]
            dst = dstate[...]
            dst_b = dst.astype(BF16)
            do_b = do_ref[sl, :].astype(BF16)
            a = jnp.where(causal, _dg(qgn_b, kgn_b, 1, 1), 0.0).astype(BF16)
            da = jnp.where(causal, _dg(do_b, v, 1, 1), 0.0).astype(BF16)
            dqgn = _dg(da, kgn_b, 1, 0)
            dqgt = _dg(do_b, st0.astype(BF16), 1, 0)
            dkgn = _dg(da, qgn_b, 0, 0)
            dv = _dg(a, do_b, 0, 0) + _dg(kd_b, dst_b, 1, 1)
            dkd = _dg(v, dst_b, 1, 0)
            ddec = jnp.sum(st0 * dst, axis=0, keepdims=True)
            dstate[...] = dec * dst + _dg(do_b, qgt_b, 0, 0)
            dq_ref[sl, :] = (scale * (dqgn * e_qn + dqgt * e_b)).astype(dq_ref.dtype)
            dk_ref[sl, :] = (dkgn * e_kn + dkd * e_kd).astype(dk_ref.dtype)
            dv_ref[sl, :] = dv.astype(dv_ref.dtype)
            db = dqgn * qgn + dqgt * qgt - dkgn * kgn - dkd * kd
            extra = jnp.sum(dkd * kd, axis=0, keepdims=True) + ddec * dec
            dla_ref[sl, :] = jnp.dot(tri_t, db, precision=lax.Precision.HIGHEST, preferred_element_type=F32) + extra

    rev = lambda t: nb - 1 - t
    return pl.pallas_call(
        body, name="gla_bwd", grid=(GLA_H, nb),
        in_specs=[pl.BlockSpec((tb, GLA_DK), lambda h, t: (rev(t), P_GQ // GLA_DK + h)),
                  pl.BlockSpec((tb, GLA_DK), lambda h, t: (rev(t), P_GK // GLA_DK + h)),
                  pl.BlockSpec((tb, GLA_DV), lambda h, t: (rev(t), P_GV // GLA_DV + h)),
                  pl.BlockSpec((tb, GLA_DK), lambda h, t: (rev(t), h)),
                  pl.BlockSpec((1, nc, GLA_DV, GLA_DK), lambda h, t: (h, rev(t), 0, 0)),
                  pl.BlockSpec((tb, GLA_DV), lambda h, t: (rev(t), h))],
        out_specs=[pl.BlockSpec((tb, GLA_DK), lambda h, t: (rev(t), h)),
                   pl.BlockSpec((tb, GLA_DK), lambda h, t: (rev(t), h)),
                   pl.BlockSpec((tb, GLA_DV), lambda h, t: (rev(t), h)),
                   pl.BlockSpec((tb, GLA_DK), lambda h, t: (rev(t), h))],
        out_shape=[jax.ShapeDtypeStruct((s, GLA_H * GLA_DK), BF16),
                   jax.ShapeDtypeStruct((s, GLA_H * GLA_DK), BF16),
                   jax.ShapeDtypeStruct((s, GLA_H * GLA_DV), BF16),
                   jax.ShapeDtypeStruct((s, GLA_H * GLA_DK), F32)],
        scratch_shapes=[pltpu.VMEM((GLA_DV, GLA_DK), F32)],
        compiler_params=_params(("parallel", "arbitrary")),
    )(p, p, p, la, states, do)


def _head_masks():
    lane = lax.broadcasted_iota(jnp.int32, (1, 4 * ATT_HD), 1)
    return [(lane >= h * ATT_HD) & (lane < (h + 1) * ATT_HD) for h in range(4)]


def _attn_fwd(qrot, krot, vatt, g, r, s):
    ln = s // r
    nblk = ln // ATT_BLK
    qv, kv, pv = qrot.reshape(ln, r * ATT_W), krot.reshape(ln, r * ATT_W), vatt.reshape(ln, r * ATT_W)
    qcol = lambda pr: pr * 3 + g
    vcol = qcol
    prev = lambda n: jnp.maximum(n - 1, 0)

    def body(q_ref, kp_ref, kc_ref, vp_ref, vc_ref, o_ref, lse_ref):
        has_prev = pl.program_id(1) > 0
        ri = lax.broadcasted_iota(jnp.int32, (ATT_BLK, ATT_BLK), 0)
        ci = lax.broadcasted_iota(jnp.int32, (ATT_BLK, ATT_BLK), 1)
        m_cur = ci <= ri
        m_prev = (ci >= ri) & has_prev
        q, kp, kc, vp, vc = q_ref[...], kp_ref[...], kc_ref[...], vp_ref[...], vc_ref[...]
        o = jnp.zeros((ATT_BLK, 256), F32)
        lse = jnp.zeros((ATT_BLK, 256), F32)
        for hm in _head_masks():
            qm = jnp.where(hm, q, jnp.zeros_like(q))
            sc = jnp.where(m_cur, _dg(qm, kc, 1, 1) * 0.125, NEG)
            sp = jnp.where(m_prev, _dg(qm, kp, 1, 1) * 0.125, NEG)
            mx = jnp.maximum(jnp.max(sc, axis=1, keepdims=True), jnp.max(sp, axis=1, keepdims=True))
            pc, pp = jnp.exp(sc - mx), jnp.exp(sp - mx)
            den = jnp.sum(pc, axis=1, keepdims=True) + jnp.sum(pp, axis=1, keepdims=True)
            oh = (_dg(pc.astype(BF16), vc, 1, 0) + _dg(pp.astype(BF16), vp, 1, 0)) / den
            o = jnp.where(hm, oh, o)
            lse = jnp.where(hm, mx + jnp.log(den), lse)
        o_ref[...] = o.astype(o_ref.dtype)
        lse_ref[...] = lse

    blk = (ATT_BLK, 256)
    o, lse = pl.pallas_call(
        body, name=f"attn_fwd_{g}", grid=(r, nblk),
        in_specs=[pl.BlockSpec(blk, lambda pr, n: (n, qcol(pr))),
                  pl.BlockSpec(blk, lambda pr, n: (prev(n), qcol(pr))),
                  pl.BlockSpec(blk, lambda pr, n: (n, qcol(pr))),
                  pl.BlockSpec(blk, lambda pr, n: (prev(n), vcol(pr))),
                  pl.BlockSpec(blk, lambda pr, n: (n, vcol(pr)))],
        out_specs=[pl.BlockSpec(blk, lambda pr, n: (n, pr)), pl.BlockSpec(blk, lambda pr, n: (n, pr))],
        out_shape=[jax.ShapeDtypeStruct((ln, r * 256), BF16), jax.ShapeDtypeStruct((ln, r * 256), F32)],
        compiler_params=_params(("parallel", "parallel")),
    )(qv, kv, kv, pv, pv)
    return o.reshape(s, 256), lse.reshape(s, 256)


def _attn_bwd(qrot, krot, vatt, do, o, lse, g, r, s):
    ln = s // r
    nblk = ln // ATT_BLK
    qv, kv, pv = qrot.reshape(ln, r * ATT_W), krot.reshape(ln, r * ATT_W), vatt.reshape(ln, r * ATT_W)
    dov, ov, lv = do.reshape(ln, r * 256), o.reshape(ln, r * 256), lse.reshape(ln, r * 256)
    qcol = lambda pr: pr * 3 + g
    vcol = qcol
    prev = lambda n: jnp.maximum(n - 1, 0)
    nxt = lambda n: jnp.minimum(n + 1, nblk - 1)

    def body(qc_ref, qn_ref, kp_ref, kc_ref, vp_ref, vc_ref, doc_ref, don_ref, oc_ref, on_ref, lc_ref, ln_ref,
             dq_ref, dk_ref, dv_ref):
        n = pl.program_id(1)
        has_prev, has_next = n > 0, n < nblk - 1
        ri = lax.broadcasted_iota(jnp.int32, (ATT_BLK, ATT_BLK), 0)
        ci = lax.broadcasted_iota(jnp.int32, (ATT_BLK, ATT_BLK), 1)
        m_cur = ci <= ri
        m_prev = (ci >= ri) & has_prev
        m_next = (ci >= ri) & has_next
        qc, qn, kp, kc, vp, vc = qc_ref[...], qn_ref[...], kp_ref[...], kc_ref[...], vp_ref[...], vc_ref[...]
        doc, don = doc_ref[...], don_ref[...]
        pc_full = doc.astype(F32) * oc_ref[...].astype(F32)
        pn_full = don.astype(F32) * on_ref[...].astype(F32)
        lc, lnx = lc_ref[...], ln_ref[...]
        dq = jnp.zeros((ATT_BLK, 256), F32)
        dk = jnp.zeros((ATT_BLK, 256), F32)
        dv = jnp.zeros((ATT_BLK, 256), F32)
        zb = jnp.zeros_like(qc)
        for hm in _head_masks():
            qcm, qnm = jnp.where(hm, qc, zb), jnp.where(hm, qn, zb)
            docm, donm = jnp.where(hm, doc, zb), jnp.where(hm, don, zb)
            lse_c = jnp.max(jnp.where(hm, lc, NEG), axis=1, keepdims=True)
            lse_n = jnp.max(jnp.where(hm, lnx, NEG), axis=1, keepdims=True)
            del_c = jnp.sum(jnp.where(hm, pc_full, 0.0), axis=1, keepdims=True)
            del_n = jnp.sum(jnp.where(hm, pn_full, 0.0), axis=1, keepdims=True)
            pr_ = jnp.where(m_cur, jnp.exp(_dg(qcm, kc, 1, 1) * 0.125 - lse_c), 0.0)
            ds = (pr_ * (_dg(docm, vc, 1, 1) - del_c) * 0.125).astype(BF16)
            dqh = _dg(ds, kc, 1, 0)
            dkh = _dg(ds, qc, 0, 0)
            dvh = _dg(pr_.astype(BF16), doc, 0, 0)
            pr_ = jnp.where(m_prev, jnp.exp(_dg(qcm, kp, 1, 1) * 0.125 - lse_c), 0.0)
            ds = (pr_ * (_dg(docm, vp, 1, 1) - del_c) * 0.125).astype(BF16)
            dqh = dqh + _dg(ds, kp, 1, 0)
            pr_ = jnp.where(m_next, jnp.exp(_dg(qnm, kc, 1, 1) * 0.125 - lse_n), 0.0)
            ds = (pr_ * (_dg(donm, vc, 1, 1) - del_n) * 0.125).astype(BF16)
            dkh = dkh + _dg(ds, qn, 0, 0)
            dvh = dvh + _dg(pr_.astype(BF16), don, 0, 0)
            dq = jnp.where(hm, dqh, dq)
            dk = jnp.where(hm, dkh, dk)
            dv = jnp.where(hm, dvh, dv)
        dq_ref[...] = dq.astype(dq_ref.dtype)
        dk_ref[...] = dk.astype(dk_ref.dtype)
        dv_ref[...] = dv.astype(dv_ref.dtype)

    blk = (ATT_BLK, 256)
    cur = lambda col: pl.BlockSpec(blk, lambda pr, n: (n, col(pr)))
    prv = lambda col: pl.BlockSpec(blk, lambda pr, n: (prev(n), col(pr)))
    nx = lambda col: pl.BlockSpec(blk, lambda pr, n: (nxt(n), col(pr)))
    own = lambda pr: pr
    outs = pl.pallas_call(
        body, name=f"attn_bwd_{g}", grid=(r, nblk),
        in_specs=[cur(qcol), nx(qcol), prv(qcol), cur(qcol), prv(vcol), cur(vcol),
                  cur(own), nx(own), cur(own), nx(own), cur(own), nx(own)],
        out_specs=[cur(own), cur(own), cur(own)],
        out_shape=[jax.ShapeDtypeStruct((ln, r * 256), BF16)] * 3,
        compiler_params=_params(("parallel", "parallel")),
    )(qv, qv, kv, kv, pv, pv, dov, dov, ov, ov, lv, lv)
    return [t.reshape(s, 256) for t in outs]


def _gelu_parts(gv):
    cdf = 0.5 * (1.0 + lax.erf(gv * (2.0 ** -0.5)))
    pdf = jnp.exp(-0.5 * gv * gv) * (1.0 / math.sqrt(2.0 * math.pi))
    return cdf, pdf


def _pick_row(t, k):
    row = lax.broadcasted_iota(jnp.int32, t.shape, 0)
    return jnp.sum(jnp.where(row == k, t, 0.0), axis=0, keepdims=True)


def _shift_rows(u, halo, n):
    row = lax.broadcasted_iota(jnp.int32, u.shape, 0)
    out = pltpu.roll(u, n, 0)
    for k in range(n):
        out = jnp.where(row == k, _pick_row(halo, 16 - n + k), out)
    return out


def _shift_rows_up(u, halo, n):
    rb = u.shape[0]
    row = lax.broadcasted_iota(jnp.int32, u.shape, 0)
    out = pltpu.roll(u, rb - n, 0)
    for k in range(n):
        out = jnp.where(row == rb - n + k, _pick_row(halo, k), out)
    return out


def _conv(u, halo, cw, cb):
    return cb + _pick_row(cw, 0) * _shift_rows(u, halo, 2) + _pick_row(cw, 1) * _shift_rows(u, halo, 1) + _pick_row(cw, 2) * u


def _local_step(x, mod, pos_col, target, sm, w):
    s = x.shape[0]
    shift1, scale1, gate1, shift2, scale2, gate2 = [mod[i:i + 1, :] for i in range(6)]
    rb = 256
    zc = lambda c: 0

    def f_norm1(c, i, xv, nw, sc, sh):
        return ((xv * _rms(xv) * nw) * (1.0 + sc) + sh,)

    (h,) = _rowcall(f_norm1, [_rows(x, rb), _full(sm["n1w"]), _full(scale1), _full(shift1)],
                    [_orow(s, D, BF16, rb)], n_rows=s, rb=rb, name="norm1")
    p = _mm(h, w["win"], "in_proj", tm=1024, tn=1536)

    def f_gla_pre(c, i, glr, w2, gb):
        z = _dg(glr, w2.astype(BF16), 1, 0) + gb
        return ((jnp.minimum(z, 0.0) - jnp.log(1.0 + jnp.exp(-jnp.abs(z)))) * (1.0 / GLA_TAU),)

    (la,) = _rowcall(f_gla_pre, [_rows(p, rb, 128, P_LR // 128), _full(sm["w2"]), _full(sm["gb"])],
                     [_orow(s, 512, F32, rb)], n_rows=s, rb=rb, name="gla_pre")
    o_gla, states = _gla_fwd(p, la, s)

    def f_gla_post(c, i, ov, gnw, gr):
        on = jnp.concatenate([ov[:, k * 256:(k + 1) * 256] * _rms(ov[:, k * 256:(k + 1) * 256]) * gnw
                              for k in range(GLA_H)], axis=1)
        g = gr.astype(F32)
        return (on * (g * _sigmoid(g)),)

    (og,) = _rowcall(f_gla_post, [_rows(o_gla, rb), _full(sm["gnw"]), _rows(p, rb, 1024, P_GR // 1024)],
                     [_orow(s, 1024, BF16, rb)], n_rows=s, rb=rb, name="gla_post")
    y_gla = _mm(og, w["wgb"], "gla_branch")

    invf = jnp.tile(ROPE_THETA ** (-jnp.arange(ATT_HD // 2, dtype=F32) / (ATT_HD // 2)), 4).reshape(1, 128)
    cos_t, sin_t = _rope_tables(pos_col, invf, s)

    def f_rope(c, i, aq, ak, av, cs, sn):
        return (_rope_apply(aq.astype(F32), cs, sn, False), _rope_apply(ak.astype(F32), cs, sn, False), av)

    grp = lambda c: c
    qrot, krot, vatt = _rowcall(f_rope, [_rows(p, 1024, 256, lambda c: P_AQ // 256 + c), _rows(p, 1024, 256, lambda c: P_AK // 256 + c),
                                         _rows(p, 1024, 256, lambda c: P_AV // 256 + c), _rows(cos_t, 1024), _rows(sin_t, 1024)],
                                [_orow(s, ATT_W, BF16, 1024, 256, grp)] * 3, n_rows=s, rb=1024, name="rope", ncol=3)
    att = [_attn_fwd(qrot, krot, vatt, g, r, s) for g, (_, r) in enumerate(ATT_GROUPS)]

    def f_comb(c, i, o0, o1, o2, l0, l1, l2):
        mx = jnp.maximum(jnp.maximum(l0, l1), l2)
        e0, e1, e2 = jnp.exp(l0 - mx), jnp.exp(l1 - mx), jnp.exp(l2 - mx)
        z = e0 + e1 + e2
        o = (e0 * o0.astype(F32) + e1 * o1.astype(F32) + e2 * o2.astype(F32)) / z
        return (o, mx + jnp.log(z))

    o_att, lse = _rowcall(f_comb, [_rows(a[0], 512) for a in att] + [_rows(a[1], 512) for a in att],
                          [_orow(s, 256, BF16, 512), _orow(s, 256, F32, 512)], n_rows=s, rb=512, name="attn_combine")
    y_att = _mm(o_att, w["wab"], "attn_branch")

    def f_merge(c, i, ma, mb, yg, ya):
        return (_sigmoid(ma.astype(F32)) * yg.astype(F32) + _sigmoid(mb.astype(F32)) * ya.astype(F32),)

    (mixed,) = _rowcall(f_merge, [_rows(p, rb, D, P_MA // D), _rows(p, rb, D, P_MB // D), _rows(y_gla, rb), _rows(y_att, rb)],
                        [_orow(s, D, BF16, rb)], n_rows=s, rb=rb, name="merge")
    z1 = _mm(mixed, w["wout"], "out_proj")

    def f_norm2(c, i, xv, z, g1, nw, sc, sh):
        x1 = xv + g1 * z.astype(F32)
        return (x1, (x1 * _rms(x1) * nw) * (1.0 + sc) + sh)

    x1, h2 = _rowcall(f_norm2, [_rows(x, rb), _rows(z1, rb), _full(gate1), _full(sm["n2w"]), _full(scale2), _full(shift2)],
                      [_orow(s, D, F32, rb), _orow(s, D, BF16, rb)], n_rows=s, rb=rb, name="norm2")
    u = _mm(h2, w["wup"], "up_proj", tn=1408)

    cwid = 2 * W_UP_SH

    def f_ffn(c, i, uv, hl, cw, cb):
        uc = _conv(uv.astype(F32), hl.astype(F32) * (i > 0).astype(F32), cw, cb)
        val, gt = uc[:, :W_UP_SH], uc[:, W_UP_SH:]
        cdf, _ = _gelu_parts(gt)
        return (gt * cdf * val,)

    ccol = lambda c: c
    (hidden,) = _rowcall(f_ffn, [_rows(u, rb, cwid, ccol), _halo(u, rb, 16, cwid, ccol, True),
                                 _full(sm["cw"], cwid, ccol), _full(sm["cb"], cwid, ccol)],
                         [_orow(s, D_FF, BF16, rb, W_UP_SH, ccol)], n_rows=s, rb=rb, name="conv_geglu", ncol=2)
    z2 = _mm(hidden, w["wdown"], "down_proj", tk=1408)

    def f_final(c, i, x1v, z, g2, fw, tgt):
        x2 = x1v + g2 * z.astype(F32)
        r = _rms(x2)
        xh = x2 * r
        e = xh * fw - tgt
        loss = 0.5 * jnp.sum(jnp.mean(e * e, axis=-1, keepdims=True), axis=0, keepdims=True)
        dy = e * (1.0 / D)
        dxh = dy * fw
        dx2 = r * (dxh - xh * jnp.mean(dxh * xh, axis=-1, keepdims=True))
        return (loss, dx2, dx2 * g2, _csum(dy * xh), _csum(dx2 * z.astype(F32)))

    loss, dx2, dz2, d_fnw, d_gate2 = _rowcall(
        f_final, [_rows(x1, rb), _rows(z2, rb), _full(gate2), _full(sm["fnw"]), _rows(target, rb)],
        [_oacc(1, 1), _orow(s, D, F32, rb), _orow(s, D, BF16, rb), _oacc(1, D), _oacc(1, D)],
        n_rows=s, rb=rb, name="final_loss")
    d_hidden = _mm(dz2, w["wdown"], "down_proj_dx", tb=True, tn=1408)
    g_wdown = _mm(hidden, dz2, "down_proj_dw", ta=True, out_dtype=F32, tm=1408, tn=1024, tk=512)

    def f_ffn_bwd(c, i, uv, hl, dh, cw, cb):
        uf = uv.astype(F32)
        hf = hl.astype(F32) * (i > 0).astype(F32)
        u1, u2 = _shift_rows(uf, hf, 1), _shift_rows(uf, hf, 2)
        uc = cb + _pick_row(cw, 0) * u2 + _pick_row(cw, 1) * u1 + _pick_row(cw, 2) * uf
        val, gt = uc[:, :W_UP_SH], uc[:, W_UP_SH:]
        cdf, pdf = _gelu_parts(gt)
        dhf = dh.astype(F32)
        duc = jnp.concatenate([dhf * (gt * cdf), dhf * val * (cdf + gt * pdf)], axis=1)
        dcw = jnp.concatenate([_csum(duc * u2), _csum(duc * u1), _csum(duc * uf)], axis=0)
        return (duc, _csum(duc), dcw)

    duc, d_cb, d_cw = _rowcall(
        f_ffn_bwd, [_rows(u, rb, cwid, ccol), _halo(u, rb, 16, cwid, ccol, True), _rows(d_hidden, rb, W_UP_SH, ccol),
                    _full(sm["cw"], cwid, ccol), _full(sm["cb"], cwid, ccol)],
        [_orow(s, 2 * D_FF, BF16, rb, cwid, ccol), _oacc(1, 2 * D_FF, cwid, ccol), _oacc(3, 2 * D_FF, cwid, ccol)],
        n_rows=s, rb=rb, name="conv_geglu_bwd", ncol=2)

    def f_conv_t(c, i, dv, hl, cw):
        df = dv.astype(F32)
        hf = hl.astype(F32) * (i < s // rb - 1).astype(F32)
        return (_pick_row(cw, 2) * df + _pick_row(cw, 1) * _shift_rows_up(df, hf, 1) + _pick_row(cw, 0) * _shift_rows_up(df, hf, 2),)

    (du,) = _rowcall(f_conv_t, [_rows(duc, rb, cwid, ccol), _halo(duc, rb, 16, cwid, ccol, False), _full(sm["cw"], cwid, ccol)],
                     [_orow(s, 2 * D_FF, BF16, rb, cwid, ccol)], n_rows=s, rb=rb, name="conv_transpose", ncol=2)
    d_h2 = _mm(du, w["wup"], "up_proj_dx", tb=True, tk=1408)
    g_wup = _mm(h2, du, "up_proj_dw", ta=True, out_dtype=F32, tm=1024, tn=1408, tk=512)

    def f_norm2_bwd(c, i, x1v, dh, dxr, z, nw, sc, g1):
        dxn, dsh, dsc, dnw = _norm_bwd(x1v, dh.astype(F32), nw, sc)
        dx1 = dxr + dxn
        return (dx1, dx1 * g1, dsh, dsc, dnw, _csum(dx1 * z.astype(F32)))

    dx1, dz1, d_shift2, d_scale2, d_n2w, d_gate1 = _rowcall(
        f_norm2_bwd, [_rows(x1, rb), _rows(d_h2, rb), _rows(dx2, rb), _rows(z1, rb), _full(sm["n2w"]), _full(scale2), _full(gate1)],
        [_orow(s, D, F32, rb), _orow(s, D, BF16, rb), _oacc(1, D), _oacc(1, D), _oacc(1, D), _oacc(1, D)],
        n_rows=s, rb=rb, name="norm2_bwd")
    d_mixed = _mm(dz1, w["wout"], "out_proj_dx", tb=True)
    g_wout = _mm(mixed, dz1, "out_proj_dw", ta=True, out_dtype=F32, tk=512)

    def f_merge_bwd(c, i, dm, ma, mb, yg, ya):
        dmf, ygf, yaf = dm.astype(F32), yg.astype(F32), ya.astype(F32)
        sa, sb = _sigmoid(ma.astype(F32)), _sigmoid(mb.astype(F32))
        return (dmf * sa, dmf * sb, dmf * ygf * sa * (1.0 - sa), dmf * yaf * sb * (1.0 - sb))

    dy_gla, dy_att, d_ma, d_mb = _rowcall(
        f_merge_bwd, [_rows(d_mixed, rb), _rows(p, rb, D, P_MA // D), _rows(p, rb, D, P_MB // D), _rows(y_gla, rb), _rows(y_att, rb)],
        [_orow(s, D, BF16, rb)] * 4, n_rows=s, rb=rb, name="merge_bwd")
    d_og = _mm(dy_gla, w["wgb"], "gla_branch_dx", tb=True)
    g_wgb = _mm(og, dy_gla, "gla_branch_dw", ta=True, out_dtype=F32, tk=512)
    d_oatt = _mm(dy_att, w["wab"], "attn_branch_dx", tb=True)
    g_wab = _mm(o_att, dy_att, "attn_branch_dw", ta=True, out_dtype=F32, tk=512)

    def f_gla_post_bwd(c, i, ov, gnw, gr, dog):
        g = gr.astype(F32)
        sg = _sigmoid(g)
        silu = g * sg
        dof = dog.astype(F32)
        don = dof * silu
        on_parts, do_parts, dgn = [], [], jnp.zeros((1, 256), F32)
        for k in range(GLA_H):
            oh = ov[:, k * 256:(k + 1) * 256]
            dh = don[:, k * 256:(k + 1) * 256]
            r = _rms(oh)
            xh = oh * r
            dgn = dgn + _csum(dh * xh)
            dxh = dh * gnw
            do_parts.append(r * (dxh - xh * jnp.mean(dxh * xh, axis=-1, keepdims=True)))
            on_parts.append(xh * gnw)
        on = jnp.concatenate(on_parts, axis=1)
        dgr = dof * on * (sg * (1.0 + g * (1.0 - sg)))
        return (jnp.concatenate(do_parts, axis=1), dgr, dgn)

    do_gla, d_gr, d_gnw = _rowcall(
        f_gla_post_bwd, [_rows(o_gla, rb), _full(sm["gnw"]), _rows(p, rb, 1024, P_GR // 1024), _rows(d_og, rb)],
        [_orow(s, 1024, F32, rb), _orow(s, 1024, BF16, rb), _oacc(1, 256)], n_rows=s, rb=rb, name="gla_post_bwd")
    d_gq, d_gk, d_gv, d_la = _gla_bwd(p, la, states, do_gla, s)

    def f_gla_pre_bwd(c, i, lav, dlav, glr, w2):
        dz = dlav * (1.0 / GLA_TAU) * (1.0 - jnp.exp(GLA_TAU * lav))
        dzb = dz.astype(BF16)
        return (_dg(dzb, w2.astype(BF16), 1, 1), _csum(dz), _dg(glr, dzb, 0, 0))

    d_glr, d_gb, d_w2 = _rowcall(
        f_gla_pre_bwd, [_rows(la, rb), _rows(d_la, rb), _rows(p, rb, 128, P_LR // 128), _full(sm["w2"])],
        [_orow(s, 128, BF16, rb), _oacc(1, 512), _oacc(128, 512)], n_rows=s, rb=rb, name="gla_pre_bwd")

    datt = [_attn_bwd(qrot, krot, vatt, d_oatt, o_att, lse, g, r, s) for g, (_, r) in enumerate(ATT_GROUPS)]
    dq_rot = jnp.concatenate([d[0] for d in datt], axis=1)
    dk_rot = jnp.concatenate([d[1] for d in datt], axis=1)
    d_av = jnp.concatenate([d[2] for d in datt], axis=1)

    def f_rope_bwd(c, i, dq, dk, cs, sn):
        return (_rope_apply(dq.astype(F32), cs, sn, True), _rope_apply(dk.astype(F32), cs, sn, True))

    d_aq, d_ak = _rowcall(f_rope_bwd, [_rows(dq_rot, rb), _rows(dk_rot, rb), _rows(cos_t, rb), _rows(sin_t, rb)],
                          [_orow(s, ATT_W, BF16, rb), _orow(s, ATT_W, BF16, rb)], n_rows=s, rb=rb, name="rope_bwd")
    dp = jnp.concatenate([d_gv, d_gr, d_ma, d_mb, d_gq, d_gk, d_aq, d_ak, d_av, d_glr,
                          jnp.zeros((s, P_W - P_LR - 128), BF16)], axis=1)
    d_h = _mm(dp, w["win"], "in_proj_dx", tb=True, tk=1536)
    g_win = _mm(h, dp, "in_proj_dw", ta=True, out_dtype=F32, tm=1024, tn=1536, tk=512)

    def f_norm1_bwd(c, i, xv, dh, dxr, nw, sc):
        dxn, dsh, dsc, dnw = _norm_bwd(xv, dh.astype(F32), nw, sc)
        return (dxr + dxn, dsh, dsc, dnw)

    grad_x, d_shift1, d_scale1, d_n1w = _rowcall(
        f_norm1_bwd, [_rows(x, rb), _rows(d_h, rb), _rows(dx1, rb), _full(sm["n1w"]), _full(scale1)],
        [_orow(s, D, F32, rb), _oacc(1, D), _oacc(1, D), _oacc(1, D)], n_rows=s, rb=rb, name="norm1_bwd")

    dmod = jnp.concatenate([d_shift1, d_scale1, d_gate1, d_shift2, d_scale2, d_gate2], axis=1)
    big = dict(win=g_win, wgb=g_wgb, wab=g_wab, wout=g_wout, wup=g_wup, wdown=g_wdown)
    small = dict(dmod=dmod, n1w=d_n1w, gb=d_gb, gnw=d_gnw, n2w=d_n2w, cb=d_cb, fnw=d_fnw, w2=d_w2, cw=d_cw)
    return loss, grad_x, big, small


def _win_to_kernel(wfull):
    return jnp.concatenate([wfull[:, 1024:3072], wfull[:, 5392:W_IN], wfull[:, 0:1024], wfull[:, 3088:5392],
                            wfull[:, 3072:3088], jnp.zeros((D, P_W - W_IN), wfull.dtype)], axis=1)


def _win_from_kernel(g):
    return jnp.concatenate([g[:, P_GQ:P_AQ], g[:, P_GV:P_MA], g[:, P_LR:P_LR + GLA_LR], g[:, P_AQ:P_LR], g[:, P_MA:P_GQ]], axis=1)


def _ff_to_kernel(a):
    h = W_UP_SH
    return jnp.concatenate([a[:, 0:h], a[:, D_FF:D_FF + h], a[:, h:D_FF], a[:, D_FF + h:]], axis=1)


def _ff_from_kernel(a):
    h = W_UP_SH
    return jnp.concatenate([a[:, 0:h], a[:, 2 * h:3 * h], a[:, h:2 * h], a[:, 3 * h:]], axis=1)


BIG = ("w_in", "w_gla_branch", "w_attn_branch", "w_out", "w_up", "w_down")
SH_SHAPES = ((1024, W_IN_SH), (256, 1024), (256, 256), (256, 1024), (1024, W_UP_SH), (W_DOWN_SH, 1024))
N_BIG = len(BIG)


def _cols_join(t):
    return jnp.concatenate([t[k] for k in range(4)], axis=1)


def _cols_split(t):
    cols = t.shape[1] // 4
    return jnp.stack([t[:, k * cols:(k + 1) * cols] for k in range(4)])


def _weights_from_gathered(wg):
    return dict(win=_win_to_kernel(_cols_join(wg[0])), wgb=wg[1].reshape(1024, D), wab=_cols_join(wg[2]),
                wout=wg[3].reshape(D, D), wup=_ff_to_kernel(_cols_join(wg[4])), wdown=wg[5].reshape(D_FF, D))


def _grads_to_shards(big):
    return [_cols_split(_win_from_kernel(big["win"])), big["wgb"].reshape(4, 256, 1024), _cols_split(big["wab"]),
            big["wout"].reshape(4, 256, 1024), _cols_split(_ff_from_kernel(big["wup"])), big["wdown"].reshape(4, W_DOWN_SH, 1024)]


def _me():
    return lax.axis_index("x"), lax.axis_index("y"), lax.axis_index("c")


HBM = pl.BlockSpec(memory_space=pltpu.HBM)
VMEM_SPEC = pl.BlockSpec(memory_space=pltpu.VMEM)


def _allgather8(xs, name):
    rows = xs.shape[0]

    def body(x_ref, out_ref, send_sems, recv_sems, local_sem):
        x, y, c = _me()
        me = 4 * x + 2 * y + c
        mine = pltpu.make_async_copy(x_ref, out_ref.at[me], local_sem)
        mine.start()
        flips = [(k >> 2 & 1, k >> 1 & 1, k & 1) for k in range(1, 8)]

        def peer(f):
            return (jnp.where(f[0] == 1, 1 - x, x), jnp.where(f[1] == 1, 1 - y, y), jnp.where(f[2] == 1, 1 - c, c))

        sends = []
        for k, f in enumerate(flips):
            cp = pltpu.make_async_remote_copy(src_ref=x_ref, dst_ref=out_ref.at[me], send_sem=send_sems.at[k],
                                              recv_sem=recv_sems.at[k], device_id=peer(f), device_id_type=MESH)
            cp.start()
            sends.append(cp)
        for k, f in enumerate(flips):
            px, py, pc = peer(f)
            pltpu.make_async_remote_copy(src_ref=x_ref, dst_ref=out_ref.at[4 * px + 2 * py + pc], send_sem=send_sems.at[k],
                                         recv_sem=recv_sems.at[k], device_id=peer(f), device_id_type=MESH).wait_recv()
        for cp in sends:
            cp.wait_send()
        mine.wait()

    return pl.pallas_call(
        body, name=name, out_shape=jax.ShapeDtypeStruct((8, rows, 128), F32),
        in_specs=[VMEM_SPEC], out_specs=VMEM_SPEC,
        scratch_shapes=[pltpu.SemaphoreType.DMA((7,)), pltpu.SemaphoreType.DMA((7,)), pltpu.SemaphoreType.DMA],
        compiler_params=pltpu.CompilerParams(vmem_limit_bytes=VMEM_LIMIT),
    )(xs)


def _half_rows(i, cc, unit):
    rows = SH_SHAPES[i][0] // 2
    return pl.ds(pl.multiple_of(cc * rows, unit), rows)


def _gather_weights(ws):
    def body(*refs):
        w_refs, out_refs = refs[:N_BIG], refs[N_BIG:2 * N_BIG]
        send_sems, recv_sems = refs[2 * N_BIG:]
        x, y, c = _me()
        s_me = 2 * x + y
        sibling = (x, y, 1 - c)
        chips = [(1 - x, y), (x, 1 - y), (1 - x, 1 - y)]

        def copy(k, i, src, chip_of_block, cc, to):
            return pltpu.make_async_remote_copy(src_ref=src, dst_ref=out_refs[i].at[chip_of_block, _half_rows(i, cc, 16)],
                                                send_sem=send_sems.at[k], recv_sem=recv_sems.at[k], device_id=to, device_id_type=MESH)

        first = [copy(j * N_BIG + i, i, w_refs[i].at[0, _half_rows(i, c, 16)], s_me, c, (*chip, c))
                 for j, chip in enumerate(chips) for i in range(N_BIG)]
        for cp in first:
            cp.start()
        passed = []
        for j, (px, py) in enumerate(chips):
            s_j = 2 * px + py
            for i in range(N_BIG):
                copy(j * N_BIG + i, i, w_refs[i].at[0, _half_rows(i, c, 16)], s_j, c, (px, py, c)).wait_recv()
                cp = copy((3 + j) * N_BIG + i, i, out_refs[i].at[s_j, _half_rows(i, c, 16)], s_j, c, sibling)
                cp.start()
                passed.append(cp)
        for j, (px, py) in enumerate(chips):
            for i in range(N_BIG):
                copy((3 + j) * N_BIG + i, i, w_refs[i].at[0, _half_rows(i, c, 16)], 2 * px + py, 1 - c, sibling).wait_recv()
        for cp in first + passed:
            cp.wait_send()

    return pl.pallas_call(
        body, name="gather_weights", out_shape=[jax.ShapeDtypeStruct((4,) + shp, BF16) for shp in SH_SHAPES],
        in_specs=[HBM] * N_BIG, out_specs=[HBM] * N_BIG,
        scratch_shapes=[pltpu.SemaphoreType.DMA((6 * N_BIG,)), pltpu.SemaphoreType.DMA((6 * N_BIG,))],
    )(*ws)


def _pair_send(gs):
    def body(*refs):
        g_refs, land_refs = refs[:N_BIG], refs[N_BIG:2 * N_BIG]
        send_sems, recv_sems = refs[2 * N_BIG:]
        x, y, c = _me()
        cps = []
        for i in range(N_BIG):
            for sh in range(4):
                cp = pltpu.make_async_remote_copy(src_ref=g_refs[i].at[sh, _half_rows(i, 1 - c, 8)], dst_ref=land_refs[i].at[sh],
                                                  send_sem=send_sems.at[4 * i + sh], recv_sem=recv_sems.at[4 * i + sh],
                                                  device_id=(x, y, 1 - c), device_id_type=MESH)
                cp.start()
                cps.append(cp)
        for cp in cps:
            cp.wait()

    return pl.pallas_call(
        body, name="grad_pair_send", out_shape=[jax.ShapeDtypeStruct((4, r // 2, cl), F32) for r, cl in SH_SHAPES],
        in_specs=[HBM] * N_BIG, out_specs=[HBM] * N_BIG,
        scratch_shapes=[pltpu.SemaphoreType.DMA((4 * N_BIG,)), pltpu.SemaphoreType.DMA((4 * N_BIG,))],
    )(*gs)


def _pair_add(g, land, core, name):
    _, rows, cols = g.shape
    half = rows // 2
    rb = _tile(half, 256, 16)
    nb = half // rb

    def body(c_ref, g_ref, l_ref, o_ref):
        o_ref[...] = (g_ref[...] + l_ref[...]).astype(BF16)

    return pl.pallas_call(
        body, name=name,
        grid_spec=pltpu.PrefetchScalarGridSpec(
            num_scalar_prefetch=1, grid=(4, nb),
            in_specs=[pl.BlockSpec((1, rb, cols), lambda s, i, c_ref: (s, c_ref[0] * nb + i, 0)),
                      pl.BlockSpec((1, rb, cols), lambda s, i, c_ref: (s, i, 0))],
            out_specs=pl.BlockSpec((1, rb, cols), lambda s, i, c_ref: (s, i, 0))),
        out_shape=jax.ShapeDtypeStruct((4, half, cols), BF16),
        compiler_params=_params(("parallel", "parallel")),
    )(core, g, land)


def _chip_exchange(ts):
    def body(*refs):
        t_refs, r_refs = refs[:N_BIG], refs[N_BIG:2 * N_BIG]
        send_sems, recv_sems = refs[2 * N_BIG:]
        x, y, c = _me()
        chips = [(1 - x, y), (x, 1 - y), (1 - x, 1 - y)]
        sends = []
        for j, (px, py) in enumerate(chips):
            for i in range(N_BIG):
                cp = pltpu.make_async_remote_copy(src_ref=t_refs[i].at[2 * px + py], dst_ref=r_refs[i].at[j],
                                                  send_sem=send_sems.at[j * N_BIG + i], recv_sem=recv_sems.at[j * N_BIG + i],
                                                  device_id=(px, py, c), device_id_type=MESH)
                cp.start()
                sends.append(cp)
        for cp in sends:
            cp.wait()

    return pl.pallas_call(
        body, name="grad_chip_exchange", out_shape=[jax.ShapeDtypeStruct((3,) + t.shape[1:], t.dtype) for t in ts],
        in_specs=[HBM] * N_BIG, out_specs=[HBM] * N_BIG,
        scratch_shapes=[pltpu.SemaphoreType.DMA((3 * N_BIG,)), pltpu.SemaphoreType.DMA((3 * N_BIG,))],
    )(*ts)


def _chip_sum(t, r, chip, name):
    _, half, cols = t.shape
    rb = _tile(half, 256, 16)

    def body(s_ref, t_ref, r_ref, o_ref):
        o_ref[...] = ((t_ref[0].astype(F32) + r_ref[0].astype(F32)) + r_ref[1].astype(F32)) + r_ref[2].astype(F32)

    return pl.pallas_call(
        body, name=name,
        grid_spec=pltpu.PrefetchScalarGridSpec(
            num_scalar_prefetch=1, grid=(half // rb,),
            in_specs=[pl.BlockSpec((1, rb, cols), lambda i, s_ref: (s_ref[0], i, 0)),
                      pl.BlockSpec((3, rb, cols), lambda i, s_ref: (0, i, 0))],
            out_specs=pl.BlockSpec((rb, cols), lambda i, s_ref: (i, 0))),
        out_shape=jax.ShapeDtypeStruct((half, cols), F32),
        compiler_params=_params(("parallel",)),
    )(chip, t, r)


def _pair_join(hs):
    def body(*refs):
        h_refs, o_refs = refs[:N_BIG], refs[N_BIG:2 * N_BIG]
        send_sems, recv_sems = refs[2 * N_BIG:]
        x, y, c = _me()
        cps = []
        for i in range(N_BIG):
            cp = pltpu.make_async_remote_copy(src_ref=h_refs[i], dst_ref=o_refs[i], send_sem=send_sems.at[i], recv_sem=recv_sems.at[i],
                                              device_id=(x, y, 1 - c), device_id_type=MESH)
            cp.start()
            cps.append(cp)
        for cp in cps:
            cp.wait()

    return pl.pallas_call(
        body, name="grad_pair_join", out_shape=[jax.ShapeDtypeStruct(h.shape, F32) for h in hs],
        in_specs=[HBM] * N_BIG, out_specs=[HBM] * N_BIG,
        scratch_shapes=[pltpu.SemaphoreType.DMA((N_BIG,)), pltpu.SemaphoreType.DMA((N_BIG,))],
    )(*hs)


def _adam_math(wv, gv, mv, vv):
    mn = ADAM_B1 * mv + (1.0 - ADAM_B1) * gv
    vn = ADAM_B2 * vv + (1.0 - ADAM_B2) * (gv * gv)
    m_hat = mn / (1.0 - ADAM_B1 ** ADAM_STEP)
    v_hat = vn / (1.0 - ADAM_B2 ** ADAM_STEP)
    return -ADAM_LR * (m_hat / (jnp.sqrt(v_hat) + ADAM_EPS) + ADAM_WD * wv), mn, vn


def _adamw_halves(wt, mt, vt, mine, theirs, core, name):
    _, rows, cols = wt.shape
    half = rows // 2
    rb = _tile(half, 256, 8)
    nb = half // rb

    def body(c_ref, w_ref, m_ref, v_ref, a_ref, b_ref, g_ref, d_ref, mo_ref, vo_ref):
        gv = jnp.where(pl.program_id(0) == c_ref[0], a_ref[...], b_ref[...])
        dl, mn, vn = _adam_math(w_ref[...], gv, m_ref[...], v_ref[...])
        g_ref[...] = gv
        d_ref[...] = dl
        mo_ref[...] = mn
        vo_ref[...] = vn

    full = pl.BlockSpec((None, rb, cols), lambda hf, i, c_ref: (0, hf * nb + i, 0))
    part = pl.BlockSpec((rb, cols), lambda hf, i, c_ref: (i, 0))
    return pl.pallas_call(
        body, name=name,
        grid_spec=pltpu.PrefetchScalarGridSpec(num_scalar_prefetch=1, grid=(2, nb), in_specs=[full, full, full, part, part],
                                               out_specs=[full] * 4),
        out_shape=[jax.ShapeDtypeStruct((1, rows, cols), F32)] * 4,
        compiler_params=_params(("parallel", "parallel")),
    )(core, wt, mt, vt, mine, theirs)


SG_REP = 136
SG_W2, SG_CW = SG_REP, SG_REP + 4 * 16
SG_ROWS = SG_CW + 4 * 40
SP_ROWS = SG_REP + 16 + 40


def _mod_shard(c_all, ada_w_sh):
    def body(c_ref, w_ref, o_ref):
        cv = c_ref[...]
        o_ref[...] = _dg((cv * _sigmoid(cv)).astype(BF16), w_ref[...].astype(BF16), 1, 0)

    return pl.pallas_call(body, name="mod_shard", out_shape=jax.ShapeDtypeStruct((8, 1536), F32),
                          in_specs=[VMEM_SPEC, VMEM_SPEC], out_specs=VMEM_SPEC,
                          compiler_params=pltpu.CompilerParams(vmem_limit_bytes=VMEM_LIMIT))(c_all, ada_w_sh)


def _mod_select(mod_all, ada_b4):
    def body(m_ref, b_ref, o_ref):
        x, y, c = _me()
        me = 4 * x + 2 * y + c
        for sh in range(4):
            o_ref[sh] = m_ref[2 * sh, me] + b_ref[sh]

    return pl.pallas_call(body, name="mod_select", out_shape=jax.ShapeDtypeStruct((4, 12, 128), F32),
                          in_specs=[VMEM_SPEC, VMEM_SPEC], out_specs=VMEM_SPEC)(mod_all, ada_b4)


def _small_reduce(sg_all):
    def body(g_ref, o_ref):
        x, y, c = _me()
        s_me = 2 * x + y
        w2_rows = pl.ds(pl.multiple_of(SG_W2 + 16 * s_me, 8), 16)
        cw_rows = pl.ds(pl.multiple_of(SG_CW + 40 * s_me, 8), 40)
        a = g_ref[0, 0:SG_REP, :]
        b = g_ref[0, w2_rows, :]
        d = g_ref[0, cw_rows, :]
        for dev in range(1, 8):
            a = a + g_ref[dev, 0:SG_REP, :]
            b = b + g_ref[dev, w2_rows, :]
            d = d + g_ref[dev, cw_rows, :]
        o_ref[0:SG_REP, :] = a
        o_ref[SG_REP:SG_REP + 16, :] = b
        o_ref[SG_REP + 16:SP_ROWS, :] = d

    return pl.pallas_call(body, name="small_grad_reduce", out_shape=jax.ShapeDtypeStruct((SP_ROWS, 128), F32),
                          in_specs=[VMEM_SPEC], out_specs=VMEM_SPEC)(sg_all)


def _ada_grad(dmod_all, c_bc):
    def body(g_ref, c_ref, o_ref):
        x, y, c = _me()
        s_me = 2 * x + y
        for k in range(12):
            acc = jnp.zeros((D, 128), F32)
            for b in range(8):
                cv = c_ref[b]
                acc = acc + (cv * _sigmoid(cv)) * g_ref[s_me, k, b:b + 1, :]
            o_ref[:, k * 128:(k + 1) * 128] = acc

    return pl.pallas_call(body, name="ada_w_grad", out_shape=jax.ShapeDtypeStruct((D, 1536), F32),
                          in_specs=[VMEM_SPEC, VMEM_SPEC], out_specs=VMEM_SPEC,
                          compiler_params=pltpu.CompilerParams(vmem_limit_bytes=VMEM_LIMIT))(dmod_all, c_bc)


def _adamw(wt, g, m, v, name):
    rows, cols = wt.shape
    rb = _tile(rows, 256, 8)

    def fn(c, i, wv, gv, mv, vv):
        return _adam_math(wv, gv, mv, vv)

    return _rowcall(fn, [_rows(t, rb) for t in (wt, g, m, v)], [_orow(rows, cols, F32, rb)] * 3,
                    n_rows=rows, rb=rb, name=name)


def _pad_rows(t, rows):
    flat = t.reshape(-1)
    return jnp.pad(flat, (0, rows * 128 - flat.shape[0])).reshape(rows, 128)


SP_LAYOUT = (("ada_b", 48), ("norm1_w", 8), ("gla_gate_b", 8), ("gla_norm_w", 8), ("norm2_w", 8), ("conv_b", 48),
             ("final_norm_w", 8), ("gla_gate_w2", 16), ("conv_w", 40))


def _pack_small(d):
    return jnp.concatenate([_pad_rows(d[n].astype(F32), rows) for n, rows in SP_LAYOUT], axis=0)


def _unpack_small(pk, shapes):
    out, off = {}, 0
    for n, rows in SP_LAYOUT:
        shp = shapes[n]
        out[n] = pk[off:off + rows].reshape(-1)[:math.prod(shp)].reshape(shp)
        off += rows
    return out


def kernel(x, c, positions, ada_w, ada_b, norm1_w, w_in, gla_gate_w2, gla_gate_b, gla_norm_w, w_gla_branch, w_attn_branch, w_out, norm2_w, w_up, conv_w, conv_b, w_down, final_norm_w, loss_target, m_ada_w, m_ada_b, m_norm1_w, m_w_in, m_gla_gate_w2, m_gla_gate_b, m_gla_norm_w, m_w_gla_branch, m_w_attn_branch, m_w_out, m_norm2_w, m_w_up, m_conv_w, m_conv_b, m_w_down, m_final_norm_w, v_ada_w, v_ada_b, v_norm1_w, v_w_in, v_gla_gate_w2, v_gla_gate_b, v_gla_norm_w, v_w_gla_branch, v_w_attn_branch, v_w_out, v_norm2_w, v_w_up, v_conv_w, v_conv_b, v_w_down, v_final_norm_w):
    s = x.shape[1]
    names = ("ada_w", "ada_b", "norm1_w", "w_in", "gla_gate_w2", "gla_gate_b", "gla_norm_w", "w_gla_branch", "w_attn_branch",
             "w_out", "norm2_w", "w_up", "conv_w", "conv_b", "w_down", "final_norm_w")
    wts = dict(zip(names, (ada_w, ada_b, norm1_w, w_in, gla_gate_w2, gla_gate_b, gla_norm_w, w_gla_branch, w_attn_branch,
                           w_out, norm2_w, w_up, conv_w, conv_b, w_down, final_norm_w)))
    ms = dict(zip(names, (m_ada_w, m_ada_b, m_norm1_w, m_w_in, m_gla_gate_w2, m_gla_gate_b, m_gla_norm_w, m_w_gla_branch,
                          m_w_attn_branch, m_w_out, m_norm2_w, m_w_up, m_conv_w, m_conv_b, m_w_down, m_final_norm_w)))
    vs = dict(zip(names, (v_ada_w, v_ada_b, v_norm1_w, v_w_in, v_gla_gate_w2, v_gla_gate_b, v_gla_norm_w, v_w_gla_branch,
                          v_w_attn_branch, v_w_out, v_norm2_w, v_w_up, v_conv_w, v_conv_b, v_w_down, v_final_norm_w)))

    pk0 = jnp.concatenate([_pad_rows(c, 8), _pad_rows(gla_gate_w2, 16), _pad_rows(conv_w, 40)], axis=0)
    sm_all = _allgather8(pk0, "gather_small")
    c_all = sm_all[:, 0:8, :].reshape(8, D)
    w2_full = sm_all[0::2, 8:24, :].transpose(1, 0, 2).reshape(GLA_LR, 512)
    cw_full = sm_all[0::2, 24:64, :].reshape(4, 40 * 128)[:, :3 * W_UP_SH].reshape(4, 3, W_UP_SH).transpose(1, 0, 2).reshape(3, 2 * D_FF)

    mod_sh = _mod_shard(c_all, ada_w[0])
    mod_all = _allgather8(mod_sh.reshape(96, 128), "gather_mod")
    mod = _mod_select(mod_all.reshape(8, 8, 12, 128), ada_b.reshape(4, 12, 128)).reshape(6, D)

    core = lax.axis_index("c").astype(jnp.int32).reshape(1)
    chip = (2 * lax.axis_index("x") + lax.axis_index("y")).astype(jnp.int32)
    w_sh = [wts[n].astype(BF16) for n in BIG]
    w = _weights_from_gathered([lax.dynamic_update_slice(got, own, (chip, 0, 0))
                                for got, own in zip(_gather_weights(w_sh), w_sh)])

    sm = dict(n1w=norm1_w, n2w=norm2_w, fnw=final_norm_w.reshape(1, D), gnw=gla_norm_w, gb=gla_gate_b,
              w2=jnp.pad(w2_full, ((0, 128 - GLA_LR), (0, 0))), cw=_ff_to_kernel(cw_full), cb=_ff_to_kernel(conv_b))
    loss, grad_x, big, small = _local_step(x[0], mod, positions.reshape(s, 1), loss_target[0], sm, w)

    gs = _grads_to_shards(big)
    ts = [_pair_add(g, land, core, "grad_pair_add_" + n) for g, land, n in zip(gs, _pair_send(gs), BIG)]
    halves = [_chip_sum(t, r, chip.reshape(1), "grad_chip_sum_" + n) for t, r, n in zip(ts, _chip_exchange(ts), BIG)]
    others = _pair_join(halves)

    dcw = _ff_from_kernel(small["cw"]).reshape(3, 4, W_UP_SH).transpose(1, 0, 2)
    dw2 = small["w2"][:GLA_LR].reshape(GLA_LR, 4, 128).transpose(1, 0, 2)
    sg = jnp.concatenate(
        [_pad_rows(small["dmod"], 48), _pad_rows(small["n1w"], 8), _pad_rows(small["gb"], 8), _pad_rows(small["gnw"], 8),
         _pad_rows(small["n2w"], 8), _pad_rows(_ff_from_kernel(small["cb"]), 48), _pad_rows(small["fnw"], 8)]
        + [_pad_rows(dw2[k], 16) for k in range(4)] + [_pad_rows(dcw[k], 40) for k in range(4)], axis=0)
    sg_all = _allgather8(sg, "gather_small_grads")
    g_small_pk = _small_reduce(sg_all)
    dmod_all = sg_all[:, 0:48, :].reshape(8, 4, 12, 128).transpose(1, 2, 0, 3)
    g_ada_w = _ada_grad(dmod_all, jnp.broadcast_to(c_all[:, :, None], (8, D, 128)))

    shapes = {n: wts[n].shape for n in names}
    g_small = _unpack_small(g_small_pk, shapes)
    grads = {"ada_w": g_ada_w.reshape(1, D, 1536), **g_small}
    deltas, new_m, new_v = {}, {}, {}
    for n, mine, theirs in zip(BIG, halves, others):
        grads[n], deltas[n], new_m[n], new_v[n] = _adamw_halves(wts[n], ms[n], vs[n], mine, theirs, core, "adamw_" + n)
    shp = ada_w.shape
    d_, m_, v_ = _adamw(ada_w[0], g_ada_w, m_ada_w[0], v_ada_w[0], "adamw_ada_w")
    deltas["ada_w"], new_m["ada_w"], new_v["ada_w"] = d_.reshape(shp), m_.reshape(shp), v_.reshape(shp)
    d_, m_, v_ = _adamw(_pack_small(wts), g_small_pk, _pack_small(ms), _pack_small(vs), "adamw_small")
    for dst, pk in ((deltas, d_), (new_m, m_), (new_v, v_)):
        dst.update(_unpack_small(pk, shapes))

    loss_all = lax.psum(loss[0, 0], ("x", "y", "c"))
    return (loss_all, grad_x.reshape(1, s, D), *[grads[n] for n in names], *[deltas[n] for n in names],
            *[new_m[n] for n in names], *[new_v[n] for n in names])
```

```python
import math

import jax
import jax.numpy as jnp
from jax import lax
from jax.experimental import pallas as pl
from jax.experimental.pallas import tpu as pltpu

F32, BF16 = jnp.float32, jnp.bfloat16
MESH = pl.DeviceIdType.MESH

D = 1024
EPS = 1e-6
GLA_H, GLA_DK, GLA_DV, GLA_LR = 4, 128, 256, 16
GLA_TAU = 16.0
GLA_CHUNK = 64
GLA_BLOCK = 512
ATT_GROUPS = ((128, 1), (512, 4), (2048, 16))
ATT_BLK = 128
ATT_HD = 64
ATT_W = 768
D_FF = 2816
ROPE_THETA = 10000.0
P_W = 7680
P_GV, P_GR, P_MA, P_MB, P_GQ, P_GK, P_AQ, P_AK, P_AV, P_LR = 0, 1024, 2048, 3072, 4096, 4608, 5120, 5888, 6656, 7424
W_IN = 7440
W_IN_SH, W_UP_SH, W_DOWN_SH = 1860, 1408, 704
VMEM_LIMIT = 56 * 1024 * 1024
ADAM_LR, ADAM_B1, ADAM_B2, ADAM_EPS, ADAM_WD, ADAM_STEP = 0.001, 0.9, 0.999, 1e-08, 0.01, 10
NEG = -1e30


def _tile(n, target, unit=128):
    best = None
    for t in range(unit, min(n, target) + 1, unit):
        if n % t == 0:
            best = t
    return best or n


def _params(sem):
    return pltpu.CompilerParams(dimension_semantics=sem, vmem_limit_bytes=VMEM_LIMIT)


def _dg(a, b, ca, cb):
    return lax.dot_general(a, b, (((ca,), (cb,)), ((), ())), preferred_element_type=F32)


def _sigmoid(v):
    return 1.0 / (1.0 + jnp.exp(-v))


def _mm(a, b, name, *, ta=False, tb=False, out_dtype=BF16, tm=1024, tn=1536, tk=1024, n_outer=True, comm=()):
    m = a.shape[1] if ta else a.shape[0]
    k = a.shape[0] if ta else a.shape[1]
    n = b.shape[0] if tb else b.shape[1]
    tm, tn, tk = _tile(m, tm), _tile(n, tn), _tile(k, tk)
    nm, nn, nk = m // tm, n // tn, k // tk
    in_out = out_dtype == F32
    c_ins, c_outs, c_alias, c_scratch = _carry(comm, 2, 1)

    def body(a_ref, b_ref, *rest):
        ci, o_ref, co = rest[:len(c_ins)], rest[len(c_ins)], rest[len(c_ins) + 1:len(c_ins) + 1 + len(c_outs)]
        scr = rest[len(c_ins) + 1 + len(c_outs):]
        kk = pl.program_id(2)
        if comm:
            step = (pl.program_id(0) * (nm if n_outer else nn) + pl.program_id(1)) * nk + kk

            @pl.when(step == 0)
            def _():
                _comm_phase(comm, ci, co, scr[-2], scr[-1], True)

        _mm_step(a_ref, b_ref, o_ref, scr, kk)
        if comm:
            @pl.when(step == nm * nn * nk - 1)
            def _():
                _comm_phase(comm, ci, co, scr[-2], scr[-1], False)

    def _mm_step(a_ref, b_ref, o_ref, scr, kk):
        p = _dg(a_ref[...].astype(BF16), b_ref[...].astype(BF16), 0 if ta else 1, 1 if tb else 0)
        if nk == 1:
            o_ref[...] = p.astype(o_ref.dtype)
        else:
            acc = o_ref if in_out else scr[0]

            @pl.when(kk == 0)
            def _():
                acc[...] = p

            @pl.when(kk > 0)
            def _():
                acc[...] += p

            if not in_out:
                @pl.when(kk == nk - 1)
                def _():
                    o_ref[...] = acc[...].astype(o_ref.dtype)

    if n_outer:
        ij = lambda g0, g1: (g1, g0)
        grid = (nn, nm, nk)
    else:
        ij = lambda g0, g1: (g0, g1)
        grid = (nm, nn, nk)
    a_map = (lambda g0, g1, kk: (kk, ij(g0, g1)[0])) if ta else (lambda g0, g1, kk: (ij(g0, g1)[0], kk))
    b_map = (lambda g0, g1, kk: (ij(g0, g1)[1], kk)) if tb else (lambda g0, g1, kk: (kk, ij(g0, g1)[1]))
    res = pl.pallas_call(
        body, name=name, grid=grid,
        in_specs=[pl.BlockSpec((tk, tm) if ta else (tm, tk), a_map),
                  pl.BlockSpec((tn, tk) if tb else (tk, tn), b_map)] + [HBM] * len(c_ins),
        out_specs=[pl.BlockSpec((tm, tn), lambda g0, g1, kk: ij(g0, g1))] + [HBM] * len(c_outs),
        out_shape=[jax.ShapeDtypeStruct((m, n), out_dtype)] + c_outs,
        scratch_shapes=([] if (in_out or nk == 1) else [pltpu.VMEM((tm, tn), F32)]) + c_scratch,
        input_output_aliases=c_alias,
        compiler_params=_params(("arbitrary",) * 3 if comm else ("parallel", "parallel", "arbitrary")),
    )(a, b, *c_ins)
    return (res[0], _split_units(comm, res[1:])) if comm else res[0]


def _rows(arr, rb, w=None, j=0):
    w = arr.shape[1] if w is None else w
    if callable(j):
        return arr, pl.BlockSpec((rb, w), lambda c, i: (i, j(c)))
    return arr, pl.BlockSpec((rb, w), lambda c, i: (i, j))


def _full(arr, w=None, j=0):
    w = arr.shape[1] if w is None else w
    if callable(j):
        return arr, pl.BlockSpec((arr.shape[0], w), lambda c, i: (0, j(c)))
    return arr, pl.BlockSpec((arr.shape[0], w), lambda c, i: (0, j))


def _halo(arr, rb, hb, w, j, before):
    per = rb // hb
    last = arr.shape[0] // hb - 1
    if before:
        rmap = lambda i: jnp.maximum(i * per - 1, 0)
    else:
        rmap = lambda i: jnp.minimum((i + 1) * per, last)
    return arr, pl.BlockSpec((hb, w), lambda c, i: (rmap(i), j(c) if callable(j) else j))


def _rowcall(fn, ins, outs, *, n_rows, rb, name, ncol=1):
    n_in = len(ins)
    nr = n_rows // rb

    def body(*refs):
        c, i = pl.program_id(0), pl.program_id(1)
        res = fn(c, i, *[r[...] for r in refs[:n_in]])
        for val, spec, o_ref in zip(res, outs, refs[n_in:]):
            if spec[2] == "row":
                o_ref[...] = val.astype(o_ref.dtype)
            else:
                @pl.when(i == 0)
                def _(o_ref=o_ref, val=val):
                    o_ref[...] = val.astype(o_ref.dtype)

                @pl.when(i > 0)
                def _(o_ref=o_ref, val=val):
                    o_ref[...] += val.astype(o_ref.dtype)

    out_specs = []
    for shape, dt, kind, block, col in outs:
        if kind == "row":
            out_specs.append(pl.BlockSpec(block, lambda c, i, col=col: (i, col(c))))
        else:
            out_specs.append(pl.BlockSpec(block, lambda c, i, col=col: (0, col(c))))
    return pl.pallas_call(
        body, name=name, grid=(ncol, nr),
        in_specs=[s for _, s in ins], out_specs=out_specs,
        out_shape=[jax.ShapeDtypeStruct(o[0], o[1]) for o in outs],
        compiler_params=_params(("parallel", "arbitrary")),
    )(*[a for a, _ in ins])


def _orow(n_rows, w, dt, rb, bw=None, col=lambda c: 0):
    return ((n_rows, w), dt, "row", (rb, bw or w), col)


def _oacc(r, w, bw=None, col=lambda c: 0):
    return ((r, w), F32, "acc", (r, bw or w), col)


def _csum(v):
    return jnp.sum(v, axis=0, keepdims=True)


def _rms(v):
    return lax.rsqrt(jnp.mean(v * v, axis=-1, keepdims=True) + EPS)


def _norm_bwd(xv, dh, w, scale):
    r = _rms(xv)
    xh = xv * r
    dxh = dh * (w * (1.0 + scale))
    dx = r * (dxh - xh * jnp.mean(dxh * xh, axis=-1, keepdims=True))
    t = dh * xh
    return dx, _csum(dh), _csum(t * w), _csum(t * (1.0 + scale))


def _rope_tables(pos_col, invf, s):
    def fn(c, i, pos, f):
        ang = pos.astype(F32) * f
        lane = lax.broadcasted_iota(jnp.int32, ang.shape, 1)
        sign = jnp.where((lane % ATT_HD) < ATT_HD // 2, -1.0, 1.0)
        return jnp.cos(ang), jnp.sin(ang) * sign

    rb = 512
    return _rowcall(fn, [_rows(pos_col, rb), _full(invf)], [_orow(s, 128, F32, rb), _orow(s, 128, F32, rb)],
                    n_rows=s, rb=rb, name="rope_tables")


def _swap_halves(t):
    n = t.shape[1]
    lane = lax.broadcasted_iota(jnp.int32, t.shape, 1)
    return jnp.where((lane % ATT_HD) < ATT_HD // 2, pltpu.roll(t, n - 32, 1), pltpu.roll(t, 32, 1))


def _rope_apply(t, cos, sin_signed, inverse):
    cw = jnp.concatenate([cos] * (t.shape[1] // 128), axis=1)
    sw = jnp.concatenate([sin_signed] * (t.shape[1] // 128), axis=1)
    if inverse:
        sw = -sw
    return t * cw + _swap_halves(t) * sw


def _gla_decays(la_c, tri):
    b = jnp.dot(tri, la_c, precision=lax.Precision.HIGHEST, preferred_element_type=F32)
    row = lax.broadcasted_iota(jnp.int32, b.shape, 0)
    bmid = jnp.sum(jnp.where(row == GLA_CHUNK // 2 - 1, b, 0.0), axis=0, keepdims=True)
    blast = jnp.sum(jnp.where(row == GLA_CHUNK - 1, b, 0.0), axis=0, keepdims=True)
    return b, bmid, blast


def _gla_fwd(p, la, s, comm=()):
    tb, ch = GLA_BLOCK, GLA_CHUNK
    nb, nc = s // tb, tb // ch
    scale = GLA_DK ** -0.5
    c_ins, c_outs, c_alias, c_scratch = _carry(comm, 4, 2)

    def body(q_ref, k_ref, v_ref, la_ref, *rest):
        ci, (o_ref, st_ref) = rest[:len(c_ins)], rest[len(c_ins):len(c_ins) + 2]
        co, state = rest[len(c_ins) + 2:len(c_ins) + 2 + len(c_outs)], rest[len(c_ins) + 2 + len(c_outs)]
        step = pl.program_id(0) * nb + pl.program_id(1)
        if comm:
            @pl.when(step == 0)
            def _():
                _comm_phase(comm, ci, co, rest[-2], rest[-1], True)

        _gla_fwd_step(q_ref, k_ref, v_ref, la_ref, o_ref, st_ref, state)
        if comm:
            @pl.when(step == GLA_H * nb - 1)
            def _():
                _comm_phase(comm, ci, co, rest[-2], rest[-1], False)

    def _gla_fwd_step(q_ref, k_ref, v_ref, la_ref, o_ref, st_ref, state):
        @pl.when(pl.program_id(1) == 0)
        def _():
            state[...] = jnp.zeros_like(state)

        ri = lax.broadcasted_iota(jnp.int32, (ch, ch), 0)
        ci = lax.broadcasted_iota(jnp.int32, (ch, ch), 1)
        causal = ci <= ri
        tri = causal.astype(F32)
        for c in range(nc):
            sl = pl.ds(c * ch, ch)
            b, bmid, blast = _gla_decays(la_ref[sl, :], tri)
            q = q_ref[sl, :].astype(F32) * scale
            k = k_ref[sl, :].astype(F32)
            v = v_ref[sl, :]
            qgt = (q * jnp.exp(b)).astype(BF16)
            qgn = (q * jnp.exp(b - bmid)).astype(BF16)
            kgn = (k * jnp.exp(bmid - b)).astype(BF16)
            kd = (k * jnp.exp(blast - b)).astype(BF16)
            a = jnp.where(causal, _dg(qgn, kgn, 1, 1), 0.0)
            st = state[...]
            st_ref[0, c] = st
            o_ref[sl, :] = _dg(a.astype(BF16), v, 1, 0) + _dg(qgt, st.astype(BF16), 1, 1)
            state[...] = jnp.exp(blast) * st + _dg(v, kd, 0, 0)

    res = pl.pallas_call(
        body, name="gla_fwd", grid=(GLA_H, nb),
        in_specs=[pl.BlockSpec((tb, GLA_DK), lambda h, t: (t, P_GQ // GLA_DK + h)),
                  pl.BlockSpec((tb, GLA_DK), lambda h, t: (t, P_GK // GLA_DK + h)),
                  pl.BlockSpec((tb, GLA_DV), lambda h, t: (t, P_GV // GLA_DV + h)),
                  pl.BlockSpec((tb, GLA_DK), lambda h, t: (t, h))] + [HBM] * len(c_ins),
        out_specs=[pl.BlockSpec((tb, GLA_DV), lambda h, t: (t, h)),
                   pl.BlockSpec((1, nc, GLA_DV, GLA_DK), lambda h, t: (h, t, 0, 0))] + [HBM] * len(c_outs),
        out_shape=[jax.ShapeDtypeStruct((s, GLA_H * GLA_DV), F32),
                   jax.ShapeDtypeStruct((GLA_H, s // ch, GLA_DV, GLA_DK), F32)] + c_outs,
        scratch_shapes=[pltpu.VMEM((GLA_DV, GLA_DK), F32)] + c_scratch,
        input_output_aliases=c_alias,
        compiler_params=_params(("arbitrary", "arbitrary") if comm else ("parallel", "arbitrary")),
    )(p, p, p, la, *c_ins)
    return res[0], res[1], _split_units(comm, res[2:])


def _gla_bwd(p, la, states, do, s, comm=()):
    tb, ch = GLA_BLOCK, GLA_CHUNK
    nb, nc = s // tb, tb // ch
    scale = GLA_DK ** -0.5
    c_ins, c_outs, c_alias, c_scratch = _carry(comm, 6, 4)

    def body(q_ref, k_ref, v_ref, la_ref, st_ref, do_ref, *rest):
        ci, outs = rest[:len(c_ins)], rest[len(c_ins):len(c_ins) + 4]
        co, dstate = rest[len(c_ins) + 4:len(c_ins) + 4 + len(c_outs)], rest[len(c_ins) + 4 + len(c_outs)]
        step = pl.program_id(0) * nb + pl.program_id(1)
        if comm:
            @pl.when(step == 0)
            def _():
                _comm_phase(comm, ci, co, rest[-2], rest[-1], True)

        _gla_bwd_step(q_ref, k_ref, v_ref, la_ref, st_ref, do_ref, *outs, dstate)
        if comm:
            @pl.when(step == GLA_H * nb - 1)
            def _():
                _comm_phase(comm, ci, co, rest[-2], rest[-1], False)

    def _gla_bwd_step(q_ref, k_ref, v_ref, la_ref, st_ref, do_ref, dq_ref, dk_ref, dv_ref, dla_ref, dstate):
        @pl.when(pl.program_id(1) == 0)
        def _():
            dstate[...] = jnp.zeros_like(dstate)

        ri = lax.broadcasted_iota(jnp.int32, (ch, ch), 0)
        ci = lax.broadcasted_iota(jnp.int32, (ch, ch), 1)
        causal = ci <= ri
        tri = causal.astype(F32)
        tri_t = (ci >= ri).astype(F32)
        for c in reversed(range(nc)):
            sl = pl.ds(c * ch, ch)
            b, bmid, blast = _gla_decays(la_ref[sl, :], tri)
            q = q_ref[sl, :].astype(F32) * scale
            k = k_ref[sl, :].astype(F32)
            v = v_ref[sl, :]
            e_b, e_qn, e_kn, e_kd = jnp.exp(b), jnp.exp(b - bmid), jnp.exp(bmid - b), jnp.exp(blast - b)
            dec = jnp.exp(blast)
            qgt, qgn, kgn, kd = q * e_b, q * e_qn, k * e_kn, k * e_kd
            qgt_b, qgn_b, kgn_b, kd_b = qgt.astype(BF16), qgn.astype(BF16), kgn.astype(BF16), kd.astype(BF16)
            st0 = st_ref[0, c]
            dst = dstate[...]
            dst_b = dst.astype(BF16)
            do_b = do_ref[sl, :].astype(BF16)
            a = jnp.where(causal, _dg(qgn_b, kgn_b, 1, 1), 0.0).astype(BF16)
            da = jnp.where(causal, _dg(do_b, v, 1, 1), 0.0).astype(BF16)
            dqgn = _dg(da, kgn_b, 1, 0)
            dqgt = _dg(do_b, st0.astype(BF16), 1, 0)
            dkgn = _dg(da, qgn_b, 0, 0)
            dv = _dg(a, do_b, 0, 0) + _dg(kd_b, dst_b, 1, 1)
            dkd = _dg(v, dst_b, 1, 0)
            ddec = jnp.sum(st0 * dst, axis=0, keepdims=True)
            dstate[...] = dec * dst + _dg(do_b, qgt_b, 0, 0)
            dq_ref[sl, :] = (scale * (dqgn * e_qn + dqgt * e_b)).astype(dq_ref.dtype)
            dk_ref[sl, :] = (dkgn * e_kn + dkd * e_kd).astype(dk_ref.dtype)
            dv_ref[sl, :] = dv.astype(dv_ref.dtype)
            db = dqgn * qgn + dqgt * qgt - dkgn * kgn - dkd * kd
            extra = jnp.sum(dkd * kd, axis=0, keepdims=True) + ddec * dec
            dla_ref[sl, :] = jnp.dot(tri_t, db, precision=lax.Precision.HIGHEST, preferred_element_type=F32) + extra

    rev = lambda t: nb - 1 - t
    res = pl.pallas_call(
        body, name="gla_bwd", grid=(GLA_H, nb),
        in_specs=[pl.BlockSpec((tb, GLA_DK), lambda h, t: (rev(t), P_GQ // GLA_DK + h)),
                  pl.BlockSpec((tb, GLA_DK), lambda h, t: (rev(t), P_GK // GLA_DK + h)),
                  pl.BlockSpec((tb, GLA_DV), lambda h, t: (rev(t), P_GV // GLA_DV + h)),
                  pl.BlockSpec((tb, GLA_DK), lambda h, t: (rev(t), h)),
                  pl.BlockSpec((1, nc, GLA_DV, GLA_DK), lambda h, t: (h, rev(t), 0, 0)),
                  pl.BlockSpec((tb, GLA_DV), lambda h, t: (rev(t), h))] + [HBM] * len(c_ins),
        out_specs=[pl.BlockSpec((tb, GLA_DK), lambda h, t: (rev(t), h)),
                   pl.BlockSpec((tb, GLA_DK), lambda h, t: (rev(t), h)),
                   pl.BlockSpec((tb, GLA_DV), lambda h, t: (rev(t), h)),
                   pl.BlockSpec((tb, GLA_DK), lambda h, t: (rev(t), h))] + [HBM] * len(c_outs),
        out_shape=[jax.ShapeDtypeStruct((s, GLA_H * GLA_DK), BF16),
                   jax.ShapeDtypeStruct((s, GLA_H * GLA_DK), BF16),
                   jax.ShapeDtypeStruct((s, GLA_H * GLA_DV), BF16),
                   jax.ShapeDtypeStruct((s, GLA_H * GLA_DK), F32)] + c_outs,
        scratch_shapes=[pltpu.VMEM((GLA_DV, GLA_DK), F32)] + c_scratch,
        input_output_aliases=c_alias,
        compiler_params=_params(("arbitrary", "arbitrary") if comm else ("parallel", "arbitrary")),
    )(p, p, p, la, states, do, *c_ins)
    return res[0], res[1], res[2], res[3], _split_units(comm, res[4:])


def _head_masks():
    lane = lax.broadcasted_iota(jnp.int32, (1, 4 * ATT_HD), 1)
    return [(lane >= h * ATT_HD) & (lane < (h + 1) * ATT_HD) for h in range(4)]


def _attn_fwd(qrot, krot, vatt, g, r, s):
    ln = s // r
    nblk = ln // ATT_BLK
    qv, kv, pv = qrot.reshape(ln, r * ATT_W), krot.reshape(ln, r * ATT_W), vatt.reshape(ln, r * ATT_W)
    qcol = lambda pr: pr * 3 + g
    vcol = qcol
    prev = lambda n: jnp.maximum(n - 1, 0)

    def body(q_ref, kp_ref, kc_ref, vp_ref, vc_ref, o_ref, lse_ref):
        has_prev = pl.program_id(1) > 0
        ri = lax.broadcasted_iota(jnp.int32, (ATT_BLK, ATT_BLK), 0)
        ci = lax.broadcasted_iota(jnp.int32, (ATT_BLK, ATT_BLK), 1)
        m_cur = ci <= ri
        m_prev = (ci >= ri) & has_prev
        q, kp, kc, vp, vc = q_ref[...], kp_ref[...], kc_ref[...], vp_ref[...], vc_ref[...]
        o = jnp.zeros((ATT_BLK, 256), F32)
        lse = jnp.zeros((ATT_BLK, 256), F32)
        for hm in _head_masks():
            qm = jnp.where(hm, q, jnp.zeros_like(q))
            sc = jnp.where(m_cur, _dg(qm, kc, 1, 1) * 0.125, NEG)
            sp = jnp.where(m_prev, _dg(qm, kp, 1, 1) * 0.125, NEG)
            mx = jnp.maximum(jnp.max(sc, axis=1, keepdims=True), jnp.max(sp, axis=1, keepdims=True))
            pc, pp = jnp.exp(sc - mx), jnp.exp(sp - mx)
            den = jnp.sum(pc, axis=1, keepdims=True) + jnp.sum(pp, axis=1, keepdims=True)
            oh = (_dg(pc.astype(BF16), vc, 1, 0) + _dg(pp.astype(BF16), vp, 1, 0)) / den
            o = jnp.where(hm, oh, o)
            lse = jnp.where(hm, mx + jnp.log(den), lse)
        o_ref[...] = o.astype(o_ref.dtype)
        lse_ref[...] = lse

    blk = (ATT_BLK, 256)
    o, lse = pl.pallas_call(
        body, name=f"attn_fwd_{g}", grid=(r, nblk),
        in_specs=[pl.BlockSpec(blk, lambda pr, n: (n, qcol(pr))),
                  pl.BlockSpec(blk, lambda pr, n: (prev(n), qcol(pr))),
                  pl.BlockSpec(blk, lambda pr, n: (n, qcol(pr))),
                  pl.BlockSpec(blk, lambda pr, n: (prev(n), vcol(pr))),
                  pl.BlockSpec(blk, lambda pr, n: (n, vcol(pr)))],
        out_specs=[pl.BlockSpec(blk, lambda pr, n: (n, pr)), pl.BlockSpec(blk, lambda pr, n: (n, pr))],
        out_shape=[jax.ShapeDtypeStruct((ln, r * 256), BF16), jax.ShapeDtypeStruct((ln, r * 256), F32)],
        compiler_params=_params(("parallel", "parallel")),
    )(qv, kv, kv, pv, pv)
    return o.reshape(s, 256), lse.reshape(s, 256)


def _attn_bwd(qrot, krot, vatt, do, o, lse, g, r, s):
    ln = s // r
    nblk = ln // ATT_BLK
    qv, kv, pv = qrot.reshape(ln, r * ATT_W), krot.reshape(ln, r * ATT_W), vatt.reshape(ln, r * ATT_W)
    dov, ov, lv = do.reshape(ln, r * 256), o.reshape(ln, r * 256), lse.reshape(ln, r * 256)
    qcol = lambda pr: pr * 3 + g
    vcol = qcol
    prev = lambda n: jnp.maximum(n - 1, 0)
    nxt = lambda n: jnp.minimum(n + 1, nblk - 1)

    def body(qc_ref, qn_ref, kp_ref, kc_ref, vp_ref, vc_ref, doc_ref, don_ref, oc_ref, on_ref, lc_ref, ln_ref,
             dq_ref, dk_ref, dv_ref):
        n = pl.program_id(1)
        has_prev, has_next = n > 0, n < nblk - 1
        ri = lax.broadcasted_iota(jnp.int32, (ATT_BLK, ATT_BLK), 0)
        ci = lax.broadcasted_iota(jnp.int32, (ATT_BLK, ATT_BLK), 1)
        m_cur = ci <= ri
        m_prev = (ci >= ri) & has_prev
        m_next = (ci >= ri) & has_next
        qc, qn, kp, kc, vp, vc = qc_ref[...], qn_ref[...], kp_ref[...], kc_ref[...], vp_ref[...], vc_ref[...]
        doc, don = doc_ref[...], don_ref[...]
        pc_full = doc.astype(F32) * oc_ref[...].astype(F32)
        pn_full = don.astype(F32) * on_ref[...].astype(F32)
        lc, lnx = lc_ref[...], ln_ref[...]
        dq = jnp.zeros((ATT_BLK, 256), F32)
        dk = jnp.zeros((ATT_BLK, 256), F32)
        dv = jnp.zeros((ATT_BLK, 256), F32)
        zb = jnp.zeros_like(qc)
        for hm in _head_masks():
            qcm, qnm = jnp.where(hm, qc, zb), jnp.where(hm, qn, zb)
            docm, donm = jnp.where(hm, doc, zb), jnp.where(hm, don, zb)
            lse_c = jnp.max(jnp.where(hm, lc, NEG), axis=1, keepdims=True)
            lse_n = jnp.max(jnp.where(hm, lnx, NEG), axis=1, keepdims=True)
            del_c = jnp.sum(jnp.where(hm, pc_full, 0.0), axis=1, keepdims=True)
            del_n = jnp.sum(jnp.where(hm, pn_full, 0.0), axis=1, keepdims=True)
            pr_ = jnp.where(m_cur, jnp.exp(_dg(qcm, kc, 1, 1) * 0.125 - lse_c), 0.0)
            ds = (pr_ * (_dg(docm, vc, 1, 1) - del_c) * 0.125).astype(BF16)
            dqh = _dg(ds, kc, 1, 0)
            dkh = _dg(ds, qc, 0, 0)
            dvh = _dg(pr_.astype(BF16), doc, 0, 0)
            pr_ = jnp.where(m_prev, jnp.exp(_dg(qcm, kp, 1, 1) * 0.125 - lse_c), 0.0)
            ds = (pr_ * (_dg(docm, vp, 1, 1) - del_c) * 0.125).astype(BF16)
            dqh = dqh + _dg(ds, kp, 1, 0)
            pr_ = jnp.where(m_next, jnp.exp(_dg(qnm, kc, 1, 1) * 0.125 - lse_n), 0.0)
            ds = (pr_ * (_dg(donm, vc, 1, 1) - del_n) * 0.125).astype(BF16)
            dkh = dkh + _dg(ds, qn, 0, 0)
            dvh = dvh + _dg(pr_.astype(BF16), don, 0, 0)
            dq = jnp.where(hm, dqh, dq)
            dk = jnp.where(hm, dkh, dk)
            dv = jnp.where(hm, dvh, dv)
        dq_ref[...] = dq.astype(dq_ref.dtype)
        dk_ref[...] = dk.astype(dk_ref.dtype)
        dv_ref[...] = dv.astype(dv_ref.dtype)

    blk = (ATT_BLK, 256)
    cur = lambda col: pl.BlockSpec(blk, lambda pr, n: (n, col(pr)))
    prv = lambda col: pl.BlockSpec(blk, lambda pr, n: (prev(n), col(pr)))
    nx = lambda col: pl.BlockSpec(blk, lambda pr, n: (nxt(n), col(pr)))
    own = lambda pr: pr
    outs = pl.pallas_call(
        body, name=f"attn_bwd_{g}", grid=(r, nblk),
        in_specs=[cur(qcol), nx(qcol), prv(qcol), cur(qcol), prv(vcol), cur(vcol),
                  cur(own), nx(own), cur(own), nx(own), cur(own), nx(own)],
        out_specs=[cur(own), cur(own), cur(own)],
        out_shape=[jax.ShapeDtypeStruct((ln, r * 256), BF16)] * 3,
        compiler_params=_params(("parallel", "parallel")),
    )(qv, qv, kv, kv, pv, pv, dov, dov, ov, ov, lv, lv)
    return [t.reshape(s, 256) for t in outs]


def _gelu_parts(gv):
    cdf = 0.5 * (1.0 + lax.erf(gv * (2.0 ** -0.5)))
    pdf = jnp.exp(-0.5 * gv * gv) * (1.0 / math.sqrt(2.0 * math.pi))
    return cdf, pdf


def _pick_row(t, k):
    row = lax.broadcasted_iota(jnp.int32, t.shape, 0)
    return jnp.sum(jnp.where(row == k, t, 0.0), axis=0, keepdims=True)


def _shift_rows(u, halo, n):
    row = lax.broadcasted_iota(jnp.int32, u.shape, 0)
    out = pltpu.roll(u, n, 0)
    for k in range(n):
        out = jnp.where(row == k, _pick_row(halo, 16 - n + k), out)
    return out


def _shift_rows_up(u, halo, n):
    rb = u.shape[0]
    row = lax.broadcasted_iota(jnp.int32, u.shape, 0)
    out = pltpu.roll(u, rb - n, 0)
    for k in range(n):
        out = jnp.where(row == rb - n + k, _pick_row(halo, k), out)
    return out


def _conv(u, halo, cw, cb):
    return cb + _pick_row(cw, 0) * _shift_rows(u, halo, 2) + _pick_row(cw, 1) * _shift_rows(u, halo, 1) + _pick_row(cw, 2) * u


def _local_step(x, mod, pos_col, target, sm, w_sh, chip, core):
    s = x.shape[0]
    shift1, scale1, gate1, shift2, scale2, gate2 = [mod[i:i + 1, :] for i in range(6)]
    rb = 256
    chip1 = chip.reshape(1)

    def f_norm1(c, i, xv, nw, sc, sh):
        return ((xv * _rms(xv) * nw) * (1.0 + sc) + sh,)

    (h,) = _rowcall(f_norm1, [_rows(x, rb), _full(sm["n1w"]), _full(scale1), _full(shift1)],
                    [_orow(s, D, BF16, rb)], n_rows=s, rb=rb, name="norm1")
    own = lambda got, i: lax.dynamic_update_slice(got, w_sh[i], (chip, 0, 0))
    [got0] = _comm_call("gather_w_in_ici", [_u_gather_ici(w_sh, (0,))])
    [got0] = _comm_call("gather_w_in_d2d", [_u_gather_d2d(got0, (0,))])
    w = dict(win=_win_to_kernel(_cols_join(own(got0[0], 0))))
    p, [got123] = _mm(h, w["win"], "in_proj", tm=1024, tn=1536, comm=[_u_gather_ici(w_sh, (1, 2, 3))])

    def f_gla_pre(c, i, glr, w2, gb):
        z = _dg(glr, w2.astype(BF16), 1, 0) + gb
        return ((jnp.minimum(z, 0.0) - jnp.log(1.0 + jnp.exp(-jnp.abs(z)))) * (1.0 / GLA_TAU),)

    (la,) = _rowcall(f_gla_pre, [_rows(p, rb, 128, P_LR // 128), _full(sm["w2"]), _full(sm["gb"])],
                     [_orow(s, 512, F32, rb)], n_rows=s, rb=rb, name="gla_pre")
    o_gla, states, [got123, got45] = _gla_fwd(p, la, s, comm=[_u_gather_d2d(got123, (1, 2, 3)), _u_gather_ici(w_sh, (4, 5))])
    w.update(wgb=own(got123[0], 1).reshape(1024, D), wab=_cols_join(own(got123[1], 2)), wout=own(got123[2], 3).reshape(D, D))

    def f_gla_post(c, i, ov, gnw, gr):
        on = jnp.concatenate([ov[:, k * 256:(k + 1) * 256] * _rms(ov[:, k * 256:(k + 1) * 256]) * gnw
                              for k in range(GLA_H)], axis=1)
        g = gr.astype(F32)
        return (on * (g * _sigmoid(g)),)

    (og,) = _rowcall(f_gla_post, [_rows(o_gla, rb), _full(sm["gnw"]), _rows(p, rb, 1024, P_GR // 1024)],
                     [_orow(s, 1024, BF16, rb)], n_rows=s, rb=rb, name="gla_post")
    y_gla = _mm(og, w["wgb"], "gla_branch")

    invf = jnp.tile(ROPE_THETA ** (-jnp.arange(ATT_HD // 2, dtype=F32) / (ATT_HD // 2)), 4).reshape(1, 128)
    cos_t, sin_t = _rope_tables(pos_col, invf, s)

    def f_rope(c, i, aq, ak, av, cs, sn):
        return (_rope_apply(aq.astype(F32), cs, sn, False), _rope_apply(ak.astype(F32), cs, sn, False), av)

    grp = lambda c: c
    qrot, krot, vatt = _rowcall(f_rope, [_rows(p, 1024, 256, lambda c: P_AQ // 256 + c), _rows(p, 1024, 256, lambda c: P_AK // 256 + c),
                                         _rows(p, 1024, 256, lambda c: P_AV // 256 + c), _rows(cos_t, 1024), _rows(sin_t, 1024)],
                                [_orow(s, ATT_W, BF16, 1024, 256, grp)] * 3, n_rows=s, rb=1024, name="rope", ncol=3)
    att = [_attn_fwd(qrot, krot, vatt, g, r, s) for g, (_, r) in enumerate(ATT_GROUPS)]

    def f_comb(c, i, o0, o1, o2, l0, l1, l2):
        mx = jnp.maximum(jnp.maximum(l0, l1), l2)
        e0, e1, e2 = jnp.exp(l0 - mx), jnp.exp(l1 - mx), jnp.exp(l2 - mx)
        z = e0 + e1 + e2
        o = (e0 * o0.astype(F32) + e1 * o1.astype(F32) + e2 * o2.astype(F32)) / z
        return (o, mx + jnp.log(z))

    o_att, lse = _rowcall(f_comb, [_rows(a[0], 512) for a in att] + [_rows(a[1], 512) for a in att],
                          [_orow(s, 256, BF16, 512), _orow(s, 256, F32, 512)], n_rows=s, rb=512, name="attn_combine")
    y_att = _mm(o_att, w["wab"], "attn_branch")

    def f_merge(c, i, ma, mb, yg, ya):
        return (_sigmoid(ma.astype(F32)) * yg.astype(F32) + _sigmoid(mb.astype(F32)) * ya.astype(F32),)

    (mixed,) = _rowcall(f_merge, [_rows(p, rb, D, P_MA // D), _rows(p, rb, D, P_MB // D), _rows(y_gla, rb), _rows(y_att, rb)],
                        [_orow(s, D, BF16, rb)], n_rows=s, rb=rb, name="merge")
    z1, [got45] = _mm(mixed, w["wout"], "out_proj", comm=[_u_gather_d2d(got45, (4, 5))])
    w.update(wup=_ff_to_kernel(_cols_join(own(got45[0], 4))), wdown=own(got45[1], 5).reshape(D_FF, D))

    def f_norm2(c, i, xv, z, g1, nw, sc, sh):
        x1 = xv + g1 * z.astype(F32)
        return (x1, (x1 * _rms(x1) * nw) * (1.0 + sc) + sh)

    x1, h2 = _rowcall(f_norm2, [_rows(x, rb), _rows(z1, rb), _full(gate1), _full(sm["n2w"]), _full(scale2), _full(shift2)],
                      [_orow(s, D, F32, rb), _orow(s, D, BF16, rb)], n_rows=s, rb=rb, name="norm2")
    u = _mm(h2, w["wup"], "up_proj", tn=1408)

    cwid = 2 * W_UP_SH

    def f_ffn(c, i, uv, hl, cw, cb):
        uc = _conv(uv.astype(F32), hl.astype(F32) * (i > 0).astype(F32), cw, cb)
        val, gt = uc[:, :W_UP_SH], uc[:, W_UP_SH:]
        cdf, _ = _gelu_parts(gt)
        return (gt * cdf * val,)

    ccol = lambda c: c
    (hidden,) = _rowcall(f_ffn, [_rows(u, rb, cwid, ccol), _halo(u, rb, 16, cwid, ccol, True),
                                 _full(sm["cw"], cwid, ccol), _full(sm["cb"], cwid, ccol)],
                         [_orow(s, D_FF, BF16, rb, W_UP_SH, ccol)], n_rows=s, rb=rb, name="conv_geglu", ncol=2)
    z2 = _mm(hidden, w["wdown"], "down_proj", tk=1408)

    def f_final(c, i, x1v, z, g2, fw, tgt):
        x2 = x1v + g2 * z.astype(F32)
        r = _rms(x2)
        xh = x2 * r
        e = xh * fw - tgt
        loss = 0.5 * jnp.sum(jnp.mean(e * e, axis=-1, keepdims=True), axis=0, keepdims=True)
        dy = e * (1.0 / D)
        dxh = dy * fw
        dx2 = r * (dxh - xh * jnp.mean(dxh * xh, axis=-1, keepdims=True))
        return (loss, dx2, dx2 * g2, _csum(dy * xh), _csum(dx2 * z.astype(F32)))

    loss, dx2, dz2, d_fnw, d_gate2 = _rowcall(
        f_final, [_rows(x1, rb), _rows(z2, rb), _full(gate2), _full(sm["fnw"]), _rows(target, rb)],
        [_oacc(1, 1), _orow(s, D, F32, rb), _orow(s, D, BF16, rb), _oacc(1, D), _oacc(1, D)],
        n_rows=s, rb=rb, name="final_loss")
    d_hidden = _mm(dz2, w["wdown"], "down_proj_dx", tb=True, tn=1408)
    g_wdown = _mm(hidden, dz2, "down_proj_dw", ta=True, out_dtype=F32, tm=1408, tn=1024, tk=512)

    def f_ffn_bwd(c, i, uv, hl, dh, cw, cb):
        uf = uv.astype(F32)
        hf = hl.astype(F32) * (i > 0).astype(F32)
        u1, u2 = _shift_rows(uf, hf, 1), _shift_rows(uf, hf, 2)
        uc = cb + _pick_row(cw, 0) * u2 + _pick_row(cw, 1) * u1 + _pick_row(cw, 2) * uf
        val, gt = uc[:, :W_UP_SH], uc[:, W_UP_SH:]
        cdf, pdf = _gelu_parts(gt)
        dhf = dh.astype(F32)
        duc = jnp.concatenate([dhf * (gt * cdf), dhf * val * (cdf + gt * pdf)], axis=1)
        dcw = jnp.concatenate([_csum(duc * u2), _csum(duc * u1), _csum(duc * uf)], axis=0)
        return (duc, _csum(duc), dcw)

    duc, d_cb, d_cw = _rowcall(
        f_ffn_bwd, [_rows(u, rb, cwid, ccol), _halo(u, rb, 16, cwid, ccol, True), _rows(d_hidden, rb, W_UP_SH, ccol),
                    _full(sm["cw"], cwid, ccol), _full(sm["cb"], cwid, ccol)],
        [_orow(s, 2 * D_FF, BF16, rb, cwid, ccol), _oacc(1, 2 * D_FF, cwid, ccol), _oacc(3, 2 * D_FF, cwid, ccol)],
        n_rows=s, rb=rb, name="conv_geglu_bwd", ncol=2)

    def f_conv_t(c, i, dv, hl, cw):
        df = dv.astype(F32)
        hf = hl.astype(F32) * (i < s // rb - 1).astype(F32)
        return (_pick_row(cw, 2) * df + _pick_row(cw, 1) * _shift_rows_up(df, hf, 1) + _pick_row(cw, 0) * _shift_rows_up(df, hf, 2),)

    (du,) = _rowcall(f_conv_t, [_rows(duc, rb, cwid, ccol), _halo(duc, rb, 16, cwid, ccol, False), _full(sm["cw"], cwid, ccol)],
                     [_orow(s, 2 * D_FF, BF16, rb, cwid, ccol)], n_rows=s, rb=rb, name="conv_transpose", ncol=2)
    g_wup = _mm(h2, du, "up_proj_dw", ta=True, out_dtype=F32, tm=1024, tn=1408, tk=512)
    gs45 = [_cols_split(_ff_from_kernel(g_wup)), g_wdown.reshape(4, W_DOWN_SH, 1024)]
    d_h2, [land45] = _mm(du, w["wup"], "up_proj_dx", tb=True, tk=1408, comm=[_u_pair_send(gs45, (4, 5))])
    ts45 = [_pair_add(g, ld, core, "grad_pair_add_" + BIG[i]) for g, ld, i in zip(gs45, land45, (4, 5))]

    def f_norm2_bwd(c, i, x1v, dh, dxr, z, nw, sc, g1):
        dxn, dsh, dsc, dnw = _norm_bwd(x1v, dh.astype(F32), nw, sc)
        dx1 = dxr + dxn
        return (dx1, dx1 * g1, dsh, dsc, dnw, _csum(dx1 * z.astype(F32)))

    dx1, dz1, d_shift2, d_scale2, d_n2w, d_gate1 = _rowcall(
        f_norm2_bwd, [_rows(x1, rb), _rows(d_h2, rb), _rows(dx2, rb), _rows(z1, rb), _full(sm["n2w"]), _full(scale2), _full(gate1)],
        [_orow(s, D, F32, rb), _orow(s, D, BF16, rb), _oacc(1, D), _oacc(1, D), _oacc(1, D), _oacc(1, D)],
        n_rows=s, rb=rb, name="norm2_bwd")
    d_mixed = _mm(dz1, w["wout"], "out_proj_dx", tb=True)
    g_wout = _mm(mixed, dz1, "out_proj_dw", ta=True, out_dtype=F32, tk=512)

    def f_merge_bwd(c, i, dm, ma, mb, yg, ya):
        dmf, ygf, yaf = dm.astype(F32), yg.astype(F32), ya.astype(F32)
        sa, sb = _sigmoid(ma.astype(F32)), _sigmoid(mb.astype(F32))
        return (dmf * sa, dmf * sb, dmf * ygf * sa * (1.0 - sa), dmf * yaf * sb * (1.0 - sb))

    dy_gla, dy_att, d_ma, d_mb = _rowcall(
        f_merge_bwd, [_rows(d_mixed, rb), _rows(p, rb, D, P_MA // D), _rows(p, rb, D, P_MB // D), _rows(y_gla, rb), _rows(y_att, rb)],
        [_orow(s, D, BF16, rb)] * 4, n_rows=s, rb=rb, name="merge_bwd")
    d_og = _mm(dy_gla, w["wgb"], "gla_branch_dx", tb=True)
    g_wgb = _mm(og, dy_gla, "gla_branch_dw", ta=True, out_dtype=F32, tk=512)
    d_oatt = _mm(dy_att, w["wab"], "attn_branch_dx", tb=True)
    g_wab = _mm(o_att, dy_att, "attn_branch_dw", ta=True, out_dtype=F32, tk=512)

    def f_gla_post_bwd(c, i, ov, gnw, gr, dog):
        g = gr.astype(F32)
        sg = _sigmoid(g)
        silu = g * sg
        dof = dog.astype(F32)
        don = dof * silu
        on_parts, do_parts, dgn = [], [], jnp.zeros((1, 256), F32)
        for k in range(GLA_H):
            oh = ov[:, k * 256:(k + 1) * 256]
            dh = don[:, k * 256:(k + 1) * 256]
            r = _rms(oh)
            xh = oh * r
            dgn = dgn + _csum(dh * xh)
            dxh = dh * gnw
            do_parts.append(r * (dxh - xh * jnp.mean(dxh * xh, axis=-1, keepdims=True)))
            on_parts.append(xh * gnw)
        on = jnp.concatenate(on_parts, axis=1)
        dgr = dof * on * (sg * (1.0 + g * (1.0 - sg)))
        return (jnp.concatenate(do_parts, axis=1), dgr, dgn)

    do_gla, d_gr, d_gnw = _rowcall(
        f_gla_post_bwd, [_rows(o_gla, rb), _full(sm["gnw"]), _rows(p, rb, 1024, P_GR // 1024), _rows(d_og, rb)],
        [_orow(s, 1024, F32, rb), _orow(s, 1024, BF16, rb), _oacc(1, 256)], n_rows=s, rb=rb, name="gla_post_bwd")
    gs123 = [g_wgb.reshape(4, 256, 1024), _cols_split(g_wab), g_wout.reshape(4, 256, 1024)]
    d_gq, d_gk, d_gv, d_la, [r45, land123] = _gla_bwd(p, la, states, do_gla, s,
                                                      comm=[_u_chip_exchange(ts45), _u_pair_send(gs123, (1, 2, 3))])
    half45 = [_chip_sum(t, r, chip1, "grad_chip_sum_" + BIG[i]) for t, r, i in zip(ts45, r45, (4, 5))]
    ts123 = [_pair_add(g, ld, core, "grad_pair_add_" + BIG[i]) for g, ld, i in zip(gs123, land123, (1, 2, 3))]

    def f_gla_pre_bwd(c, i, lav, dlav, glr, w2):
        dz = dlav * (1.0 / GLA_TAU) * (1.0 - jnp.exp(GLA_TAU * lav))
        dzb = dz.astype(BF16)
        return (_dg(dzb, w2.astype(BF16), 1, 1), _csum(dz), _dg(glr, dzb, 0, 0))

    d_glr, d_gb, d_w2 = _rowcall(
        f_gla_pre_bwd, [_rows(la, rb), _rows(d_la, rb), _rows(p, rb, 128, P_LR // 128), _full(sm["w2"])],
        [_orow(s, 128, BF16, rb), _oacc(1, 512), _oacc(128, 512)], n_rows=s, rb=rb, name="gla_pre_bwd")

    datt = [_attn_bwd(qrot, krot, vatt, d_oatt, o_att, lse, g, r, s) for g, (_, r) in enumerate(ATT_GROUPS)]
    dq_rot = jnp.concatenate([d[0] for d in datt], axis=1)
    dk_rot = jnp.concatenate([d[1] for d in datt], axis=1)
    d_av = jnp.concatenate([d[2] for d in datt], axis=1)

    def f_rope_bwd(c, i, dq, dk, cs, sn):
        return (_rope_apply(dq.astype(F32), cs, sn, True), _rope_apply(dk.astype(F32), cs, sn, True))

    d_aq, d_ak = _rowcall(f_rope_bwd, [_rows(dq_rot, rb), _rows(dk_rot, rb), _rows(cos_t, rb), _rows(sin_t, rb)],
                          [_orow(s, ATT_W, BF16, rb), _orow(s, ATT_W, BF16, rb)], n_rows=s, rb=rb, name="rope_bwd")
    dp = jnp.concatenate([d_gv, d_gr, d_ma, d_mb, d_gq, d_gk, d_aq, d_ak, d_av, d_glr,
                          jnp.zeros((s, P_W - P_LR - 128), BF16)], axis=1)
    g_win, [r123, oth45] = _mm(h, dp, "in_proj_dw", ta=True, out_dtype=F32, tm=1024, tn=1536, tk=512,
                               comm=[_u_chip_exchange(ts123), _u_pair_join(half45)])
    half123 = [_chip_sum(t, r, chip1, "grad_chip_sum_" + BIG[i]) for t, r, i in zip(ts123, r123, (1, 2, 3))]
    gs0 = [_cols_split(_win_from_kernel(g_win))]
    d_h, [land0, oth123] = _mm(dp, w["win"], "in_proj_dx", tb=True, tk=1536,
                               comm=[_u_pair_send(gs0, (0,)), _u_pair_join(half123)])
    ts0 = [_pair_add(gs0[0], land0[0], core, "grad_pair_add_w_in")]
    [r0] = _comm_call("grad_exchange_w_in", [_u_chip_exchange(ts0)])
    half0 = [_chip_sum(ts0[0], r0[0], chip1, "grad_chip_sum_w_in")]
    [oth0] = _comm_call("grad_join_w_in", [_u_pair_join(half0)])

    def f_norm1_bwd(c, i, xv, dh, dxr, nw, sc):
        dxn, dsh, dsc, dnw = _norm_bwd(xv, dh.astype(F32), nw, sc)
        return (dxr + dxn, dsh, dsc, dnw)

    grad_x, d_shift1, d_scale1, d_n1w = _rowcall(
        f_norm1_bwd, [_rows(x, rb), _rows(d_h, rb), _rows(dx1, rb), _full(sm["n1w"]), _full(scale1)],
        [_orow(s, D, F32, rb), _oacc(1, D), _oacc(1, D), _oacc(1, D)], n_rows=s, rb=rb, name="norm1_bwd")

    dmod = jnp.concatenate([d_shift1, d_scale1, d_gate1, d_shift2, d_scale2, d_gate2], axis=1)
    small = dict(dmod=dmod, n1w=d_n1w, gb=d_gb, gnw=d_gnw, n2w=d_n2w, cb=d_cb, fnw=d_fnw, w2=d_w2, cw=d_cw)
    return loss, grad_x, half0 + half123 + half45, oth0 + oth123 + oth45, small


def _win_to_kernel(wfull):
    return jnp.concatenate([wfull[:, 1024:3072], wfull[:, 5392:W_IN], wfull[:, 0:1024], wfull[:, 3088:5392],
                            wfull[:, 3072:3088], jnp.zeros((D, P_W - W_IN), wfull.dtype)], axis=1)


def _win_from_kernel(g):
    return jnp.concatenate([g[:, P_GQ:P_AQ], g[:, P_GV:P_MA], g[:, P_LR:P_LR + GLA_LR], g[:, P_AQ:P_LR], g[:, P_MA:P_GQ]], axis=1)


def _ff_to_kernel(a):
    h = W_UP_SH
    return jnp.concatenate([a[:, 0:h], a[:, D_FF:D_FF + h], a[:, h:D_FF], a[:, D_FF + h:]], axis=1)


def _ff_from_kernel(a):
    h = W_UP_SH
    return jnp.concatenate([a[:, 0:h], a[:, 2 * h:3 * h], a[:, h:2 * h], a[:, 3 * h:]], axis=1)


BIG = ("w_in", "w_gla_branch", "w_attn_branch", "w_out", "w_up", "w_down")
SH_SHAPES = ((1024, W_IN_SH), (256, 1024), (256, 256), (256, 1024), (1024, W_UP_SH), (W_DOWN_SH, 1024))
N_BIG = len(BIG)


def _cols_join(t):
    return jnp.concatenate([t[k] for k in range(4)], axis=1)


def _cols_split(t):
    cols = t.shape[1] // 4
    return jnp.stack([t[:, k * cols:(k + 1) * cols] for k in range(4)])


def _me():
    return lax.axis_index("x"), lax.axis_index("y"), lax.axis_index("c")


HBM = pl.BlockSpec(memory_space=pltpu.HBM)
VMEM_SPEC = pl.BlockSpec(memory_space=pltpu.VMEM)


def _allgather8(xs, name):
    rows = xs.shape[0]

    def body(x_ref, out_ref, send_sems, recv_sems, local_sem):
        x, y, c = _me()
        me = 4 * x + 2 * y + c
        mine = pltpu.make_async_copy(x_ref, out_ref.at[me], local_sem)
        mine.start()
        flips = [(k >> 2 & 1, k >> 1 & 1, k & 1) for k in range(1, 8)]

        def peer(f):
            return (jnp.where(f[0] == 1, 1 - x, x), jnp.where(f[1] == 1, 1 - y, y), jnp.where(f[2] == 1, 1 - c, c))

        sends = []
        for k, f in enumerate(flips):
            cp = pltpu.make_async_remote_copy(src_ref=x_ref, dst_ref=out_ref.at[me], send_sem=send_sems.at[k],
                                              recv_sem=recv_sems.at[k], device_id=peer(f), device_id_type=MESH)
            cp.start()
            sends.append(cp)
        for k, f in enumerate(flips):
            px, py, pc = peer(f)
            pltpu.make_async_remote_copy(src_ref=x_ref, dst_ref=out_ref.at[4 * px + 2 * py + pc], send_sem=send_sems.at[k],
                                         recv_sem=recv_sems.at[k], device_id=peer(f), device_id_type=MESH).wait_recv()
        for cp in sends:
            cp.wait_send()
        mine.wait()

    return pl.pallas_call(
        body, name=name, out_shape=jax.ShapeDtypeStruct((8, rows, 128), F32),
        in_specs=[VMEM_SPEC], out_specs=VMEM_SPEC,
        scratch_shapes=[pltpu.SemaphoreType.DMA((7,)), pltpu.SemaphoreType.DMA((7,)), pltpu.SemaphoreType.DMA],
        compiler_params=pltpu.CompilerParams(vmem_limit_bytes=VMEM_LIMIT),
    )(xs)


def _half_rows(i, cc, unit):
    rows = SH_SHAPES[i][0] // 2
    return pl.ds(pl.multiple_of(cc * rows, unit), rows)


def _rc(src, dst, sems, to):
    return pltpu.make_async_remote_copy(src_ref=src, dst_ref=dst, send_sem=sems[0], recv_sem=sems[1], device_id=to, device_id_type=MESH)


def _other_chips(x, y):
    return [(1 - x, y), (x, 1 - y), (1 - x, 1 - y)]


def _u_gather_ici(w_sh, idxs):
    def copies(ins, outs, sem):
        x, y, c = _me()
        res = []
        for j, (px, py) in enumerate(_other_chips(x, y)):
            for n, i in enumerate(idxs):
                src = ins[n].at[0, _half_rows(i, c, 16)]
                res.append((_rc(src, outs[n].at[2 * x + y, _half_rows(i, c, 16)], sem(j * len(idxs) + n), (px, py, c)),
                            _rc(src, outs[n].at[2 * px + py, _half_rows(i, c, 16)], sem(j * len(idxs) + n), (px, py, c))))
        return res

    return dict(ins=[w_sh[i] for i in idxs], outs=[jax.ShapeDtypeStruct((4,) + SH_SHAPES[i], BF16) for i in idxs],
                nsem=3 * len(idxs), alias={}, copies=copies)


def _u_gather_d2d(got, idxs):
    def copies(ins, outs, sem):
        x, y, c = _me()
        res = []
        for j, (px, py) in enumerate(_other_chips(x, y)):
            for n, i in enumerate(idxs):
                src = ins[n].at[2 * px + py, _half_rows(i, c, 16)]
                res.append((_rc(src, outs[n].at[2 * px + py, _half_rows(i, c, 16)], sem(j * len(idxs) + n), (x, y, 1 - c)),
                            _rc(src, outs[n].at[2 * px + py, _half_rows(i, 1 - c, 16)], sem(j * len(idxs) + n), (x, y, 1 - c))))
        return res

    return dict(ins=list(got), outs=[jax.ShapeDtypeStruct(g.shape, g.dtype) for g in got], nsem=3 * len(idxs),
                alias={n: n for n in range(len(idxs))}, copies=copies)


def _u_pair_send(gs, idxs):
    def copies(ins, outs, sem):
        x, y, c = _me()
        res = []
        for n, i in enumerate(idxs):
            for sh in range(4):
                cp = _rc(ins[n].at[sh, _half_rows(i, 1 - c, 8)], outs[n].at[sh], sem(4 * n + sh), (x, y, 1 - c))
                res.append((cp, cp))
        return res

    return dict(ins=list(gs), outs=[jax.ShapeDtypeStruct((4, SH_SHAPES[i][0] // 2, SH_SHAPES[i][1]), F32) for i in idxs],
                nsem=4 * len(idxs), alias={}, copies=copies)


def _u_chip_exchange(ts):
    def copies(ins, outs, sem):
        x, y, c = _me()
        res = []
        for j, (px, py) in enumerate(_other_chips(x, y)):
            for n in range(len(ts)):
                cp = _rc(ins[n].at[2 * px + py], outs[n].at[j], sem(j * len(ts) + n), (px, py, c))
                res.append((cp, cp))
        return res

    return dict(ins=list(ts), outs=[jax.ShapeDtypeStruct((3,) + t.shape[1:], t.dtype) for t in ts], nsem=3 * len(ts),
                alias={}, copies=copies)


def _u_pair_join(hs):
    def copies(ins, outs, sem):
        x, y, c = _me()
        res = []
        for n in range(len(hs)):
            cp = _rc(ins[n], outs[n], sem(n), (x, y, 1 - c))
            res.append((cp, cp))
        return res

    return dict(ins=list(hs), outs=[jax.ShapeDtypeStruct(h.shape, h.dtype) for h in hs], nsem=len(hs), alias={}, copies=copies)


def _comm_phase(units, ci, co, send_sems, recv_sems, start):
    ii = oo = off = 0
    for u in units:
        ni, no = len(u["ins"]), len(u["outs"])
        for st, arrival in u["copies"](ci[ii:ii + ni], co[oo:oo + no], lambda k, off=off: (send_sems.at[off + k], recv_sems.at[off + k])):
            if start:
                st.start()
            else:
                st.wait_send()
                arrival.wait_recv()
        ii, oo, off = ii + ni, oo + no, off + u["nsem"]


def _carry(units, n_in, n_out):
    ins = [a for u in units for a in u["ins"]]
    outs = [o for u in units for o in u["outs"]]
    alias, ii, oo = {}, 0, 0
    for u in units:
        for a, b in u["alias"].items():
            alias[n_in + ii + a] = n_out + oo + b
        ii, oo = ii + len(u["ins"]), oo + len(u["outs"])
    nsem = sum(u["nsem"] for u in units)
    scratch = [pltpu.SemaphoreType.DMA((nsem,)), pltpu.SemaphoreType.DMA((nsem,))] if units else []
    return ins, outs, alias, scratch


def _split_units(units, res):
    out, oo = [], 0
    for u in units:
        out.append(list(res[oo:oo + len(u["outs"])]))
        oo += len(u["outs"])
    return out


def _comm_call(name, units):
    ins, outs, alias, scratch = _carry(units, 0, 0)

    def body(*refs):
        ci, co = refs[:len(ins)], refs[len(ins):len(ins) + len(outs)]
        _comm_phase(units, ci, co, refs[-2], refs[-1], True)
        _comm_phase(units, ci, co, refs[-2], refs[-1], False)

    res = pl.pallas_call(body, name=name, out_shape=outs, in_specs=[HBM] * len(ins), out_specs=[HBM] * len(outs),
                         scratch_shapes=scratch, input_output_aliases=alias)(*ins)
    return _split_units(units, res)


def _pair_add(g, land, core, name):
    _, rows, cols = g.shape
    half = rows // 2
    rb = _tile(half, 256, 16)
    nb = half // rb

    def body(c_ref, g_ref, l_ref, o_ref):
        o_ref[...] = (g_ref[...] + l_ref[...]).astype(BF16)

    return pl.pallas_call(
        body, name=name,
        grid_spec=pltpu.PrefetchScalarGridSpec(
            num_scalar_prefetch=1, grid=(4, nb),
            in_specs=[pl.BlockSpec((1, rb, cols), lambda s, i, c_ref: (s, c_ref[0] * nb + i, 0)),
                      pl.BlockSpec((1, rb, cols), lambda s, i, c_ref: (s, i, 0))],
            out_specs=pl.BlockSpec((1, rb, cols), lambda s, i, c_ref: (s, i, 0))),
        out_shape=jax.ShapeDtypeStruct((4, half, cols), BF16),
        compiler_params=_params(("parallel", "parallel")),
    )(core, g, land)


def _chip_sum(t, r, chip, name):
    _, half, cols = t.shape
    rb = _tile(half, 256, 16)

    def body(s_ref, t_ref, r_ref, o_ref):
        o_ref[...] = ((t_ref[0].astype(F32) + r_ref[0].astype(F32)) + r_ref[1].astype(F32)) + r_ref[2].astype(F32)

    return pl.pallas_call(
        body, name=name,
        grid_spec=pltpu.PrefetchScalarGridSpec(
            num_scalar_prefetch=1, grid=(half // rb,),
            in_specs=[pl.BlockSpec((1, rb, cols), lambda i, s_ref: (s_ref[0], i, 0)),
                      pl.BlockSpec((3, rb, cols), lambda i, s_ref: (0, i, 0))],
            out_specs=pl.BlockSpec((rb, cols), lambda i, s_ref: (i, 0))),
        out_shape=jax.ShapeDtypeStruct((half, cols), F32),
        compiler_params=_params(("parallel",)),
    )(chip, t, r)


def _adam_math(wv, gv, mv, vv):
    mn = ADAM_B1 * mv + (1.0 - ADAM_B1) * gv
    vn = ADAM_B2 * vv + (1.0 - ADAM_B2) * (gv * gv)
    m_hat = mn / (1.0 - ADAM_B1 ** ADAM_STEP)
    v_hat = vn / (1.0 - ADAM_B2 ** ADAM_STEP)
    return -ADAM_LR * (m_hat / (jnp.sqrt(v_hat) + ADAM_EPS) + ADAM_WD * wv), mn, vn


def _adamw_halves(wt, mt, vt, mine, theirs, core, name):
    _, rows, cols = wt.shape
    half = rows // 2
    rb = _tile(half, 256, 8)
    nb = half // rb

    def body(c_ref, w_ref, m_ref, v_ref, a_ref, b_ref, g_ref, d_ref, mo_ref, vo_ref):
        gv = jnp.where(pl.program_id(0) == c_ref[0], a_ref[...], b_ref[...])
        dl, mn, vn = _adam_math(w_ref[...], gv, m_ref[...], v_ref[...])
        g_ref[...] = gv
        d_ref[...] = dl
        mo_ref[...] = mn
        vo_ref[...] = vn

    full = pl.BlockSpec((None, rb, cols), lambda hf, i, c_ref: (0, hf * nb + i, 0))
    part = pl.BlockSpec((rb, cols), lambda hf, i, c_ref: (i, 0))
    return pl.pallas_call(
        body, name=name,
        grid_spec=pltpu.PrefetchScalarGridSpec(num_scalar_prefetch=1, grid=(2, nb), in_specs=[full, full, full, part, part],
                                               out_specs=[full] * 4),
        out_shape=[jax.ShapeDtypeStruct((1, rows, cols), F32)] * 4,
        compiler_params=_params(("parallel", "parallel")),
    )(core, wt, mt, vt, mine, theirs)


SG_REP = 136
SG_W2, SG_CW = SG_REP, SG_REP + 4 * 16
SG_ROWS = SG_CW + 4 * 40
SP_ROWS = SG_REP + 16 + 40


def _mod_shard(c_all, ada_w_sh):
    def body(c_ref, w_ref, o_ref):
        cv = c_ref[...]
        o_ref[...] = _dg((cv * _sigmoid(cv)).astype(BF16), w_ref[...].astype(BF16), 1, 0)

    return pl.pallas_call(body, name="mod_shard", out_shape=jax.ShapeDtypeStruct((8, 1536), F32),
                          in_specs=[VMEM_SPEC, VMEM_SPEC], out_specs=VMEM_SPEC,
                          compiler_params=pltpu.CompilerParams(vmem_limit_bytes=VMEM_LIMIT))(c_all, ada_w_sh)


def _mod_select(mod_all, ada_b4):
    def body(m_ref, b_ref, o_ref):
        x, y, c = _me()
        me = 4 * x + 2 * y + c
        for sh in range(4):
            o_ref[sh] = m_ref[2 * sh, me] + b_ref[sh]

    return pl.pallas_call(body, name="mod_select", out_shape=jax.ShapeDtypeStruct((4, 12, 128), F32),
                          in_specs=[VMEM_SPEC, VMEM_SPEC], out_specs=VMEM_SPEC)(mod_all, ada_b4)


def _small_reduce(sg_all):
    def body(g_ref, o_ref):
        x, y, c = _me()
        s_me = 2 * x + y
        w2_rows = pl.ds(pl.multiple_of(SG_W2 + 16 * s_me, 8), 16)
        cw_rows = pl.ds(pl.multiple_of(SG_CW + 40 * s_me, 8), 40)
        a = g_ref[0, 0:SG_REP, :]
        b = g_ref[0, w2_rows, :]
        d = g_ref[0, cw_rows, :]
        for dev in range(1, 8):
            a = a + g_ref[dev, 0:SG_REP, :]
            b = b + g_ref[dev, w2_rows, :]
            d = d + g_ref[dev, cw_rows, :]
        o_ref[0:SG_REP, :] = a
        o_ref[SG_REP:SG_REP + 16, :] = b
        o_ref[SG_REP + 16:SP_ROWS, :] = d

    return pl.pallas_call(body, name="small_grad_reduce", out_shape=jax.ShapeDtypeStruct((SP_ROWS, 128), F32),
                          in_specs=[VMEM_SPEC], out_specs=VMEM_SPEC)(sg_all)


def _ada_grad(dmod_all, c_bc):
    def body(g_ref, c_ref, o_ref):
        x, y, c = _me()
        s_me = 2 * x + y
        for k in range(12):
            acc = jnp.zeros((D, 128), F32)
            for b in range(8):
                cv = c_ref[b]
                acc = acc + (cv * _sigmoid(cv)) * g_ref[s_me, k, b:b + 1, :]
            o_ref[:, k * 128:(k + 1) * 128] = acc

    return pl.pallas_call(body, name="ada_w_grad", out_shape=jax.ShapeDtypeStruct((D, 1536), F32),
                          in_specs=[VMEM_SPEC, VMEM_SPEC], out_specs=VMEM_SPEC,
                          compiler_params=pltpu.CompilerParams(vmem_limit_bytes=VMEM_LIMIT))(dmod_all, c_bc)


def _adamw(wt, g, m, v, name):
    rows, cols = wt.shape
    rb = _tile(rows, 256, 8)

    def fn(c, i, wv, gv, mv, vv):
        return _adam_math(wv, gv, mv, vv)

    return _rowcall(fn, [_rows(t, rb) for t in (wt, g, m, v)], [_orow(rows, cols, F32, rb)] * 3,
                    n_rows=rows, rb=rb, name=name)


def _pad_rows(t, rows):
    flat = t.reshape(-1)
    return jnp.pad(flat, (0, rows * 128 - flat.shape[0])).reshape(rows, 128)


SP_LAYOUT = (("ada_b", 48), ("norm1_w", 8), ("gla_gate_b", 8), ("gla_norm_w", 8), ("norm2_w", 8), ("conv_b", 48),
             ("final_norm_w", 8), ("gla_gate_w2", 16), ("conv_w", 40))


def _pack_small(d):
    return jnp.concatenate([_pad_rows(d[n].astype(F32), rows) for n, rows in SP_LAYOUT], axis=0)


def _unpack_small(pk, shapes):
    out, off = {}, 0
    for n, rows in SP_LAYOUT:
        shp = shapes[n]
        out[n] = pk[off:off + rows].reshape(-1)[:math.prod(shp)].reshape(shp)
        off += rows
    return out


def kernel(x, c, positions, ada_w, ada_b, norm1_w, w_in, gla_gate_w2, gla_gate_b, gla_norm_w, w_gla_branch, w_attn_branch, w_out, norm2_w, w_up, conv_w, conv_b, w_down, final_norm_w, loss_target, m_ada_w, m_ada_b, m_norm1_w, m_w_in, m_gla_gate_w2, m_gla_gate_b, m_gla_norm_w, m_w_gla_branch, m_w_attn_branch, m_w_out, m_norm2_w, m_w_up, m_conv_w, m_conv_b, m_w_down, m_final_norm_w, v_ada_w, v_ada_b, v_norm1_w, v_w_in, v_gla_gate_w2, v_gla_gate_b, v_gla_norm_w, v_w_gla_branch, v_w_attn_branch, v_w_out, v_norm2_w, v_w_up, v_conv_w, v_conv_b, v_w_down, v_final_norm_w):
    s = x.shape[1]
    names = ("ada_w", "ada_b", "norm1_w", "w_in", "gla_gate_w2", "gla_gate_b", "gla_norm_w", "w_gla_branch", "w_attn_branch",
             "w_out", "norm2_w", "w_up", "conv_w", "conv_b", "w_down", "final_norm_w")
    wts = dict(zip(names, (ada_w, ada_b, norm1_w, w_in, gla_gate_w2, gla_gate_b, gla_norm_w, w_gla_branch, w_attn_branch,
                           w_out, norm2_w, w_up, conv_w, conv_b, w_down, final_norm_w)))
    ms = dict(zip(names, (m_ada_w, m_ada_b, m_norm1_w, m_w_in, m_gla_gate_w2, m_gla_gate_b, m_gla_norm_w, m_w_gla_branch,
                          m_w_attn_branch, m_w_out, m_norm2_w, m_w_up, m_conv_w, m_conv_b, m_w_down, m_final_norm_w)))
    vs = dict(zip(names, (v_ada_w, v_ada_b, v_norm1_w, v_w_in, v_gla_gate_w2, v_gla_gate_b, v_gla_norm_w, v_w_gla_branch,
                          v_w_attn_branch, v_w_out, v_norm2_w, v_w_up, v_conv_w, v_conv_b, v_w_down, v_final_norm_w)))

    pk0 = jnp.concatenate([_pad_rows(c, 8), _pad_rows(gla_gate_w2, 16), _pad_rows(conv_w, 40)], axis=0)
    sm_all = _allgather8(pk0, "gather_small")
    c_all = sm_all[:, 0:8, :].reshape(8, D)
    w2_full = sm_all[0::2, 8:24, :].transpose(1, 0, 2).reshape(GLA_LR, 512)
    cw_full = sm_all[0::2, 24:64, :].reshape(4, 40 * 128)[:, :3 * W_UP_SH].reshape(4, 3, W_UP_SH).transpose(1, 0, 2).reshape(3, 2 * D_FF)

    mod_sh = _mod_shard(c_all, ada_w[0])
    mod_all = _allgather8(mod_sh.reshape(96, 128), "gather_mod")
    mod = _mod_select(mod_all.reshape(8, 8, 12, 128), ada_b.reshape(4, 12, 128)).reshape(6, D)

    core = lax.axis_index("c").astype(jnp.int32).reshape(1)
    chip = (2 * lax.axis_index("x") + lax.axis_index("y")).astype(jnp.int32)
    w_sh = [wts[n].astype(BF16) for n in BIG]
    sm = dict(n1w=norm1_w, n2w=norm2_w, fnw=final_norm_w.reshape(1, D), gnw=gla_norm_w, gb=gla_gate_b,
              w2=jnp.pad(w2_full, ((0, 128 - GLA_LR), (0, 0))), cw=_ff_to_kernel(cw_full), cb=_ff_to_kernel(conv_b))
    loss, grad_x, halves, others, small = _local_step(x[0], mod, positions.reshape(s, 1), loss_target[0], sm, w_sh, chip, core)

    dcw = _ff_from_kernel(small["cw"]).reshape(3, 4, W_UP_SH).transpose(1, 0, 2)
    dw2 = small["w2"][:GLA_LR].reshape(GLA_LR, 4, 128).transpose(1, 0, 2)
    sg = jnp.concatenate(
        [_pad_rows(small["dmod"], 48), _pad_rows(small["n1w"], 8), _pad_rows(small["gb"], 8), _pad_rows(small["gnw"], 8),
         _pad_rows(small["n2w"], 8), _pad_rows(_ff_from_kernel(small["cb"]), 48), _pad_rows(small["fnw"], 8)]
        + [_pad_rows(dw2[k], 16) for k in range(4)] + [_pad_rows(dcw[k], 40) for k in range(4)], axis=0)
    sg_all = _allgather8(sg, "gather_small_grads")
    g_small_pk = _small_reduce(sg_all)
    dmod_all = sg_all[:, 0:48, :].reshape(8, 4, 12, 128).transpose(1, 2, 0, 3)
    g_ada_w = _ada_grad(dmod_all, jnp.broadcast_to(c_all[:, :, None], (8, D, 128)))

    shapes = {n: wts[n].shape for n in names}
    g_small = _unpack_small(g_small_pk, shapes)
    grads = {"ada_w": g_ada_w.reshape(1, D, 1536), **g_small}
    deltas, new_m, new_v = {}, {}, {}
    for n, mine, theirs in zip(BIG, halves, others):
        grads[n], deltas[n], new_m[n], new_v[n] = _adamw_halves(wts[n], ms[n], vs[n], mine, theirs, core, "adamw_" + n)
    shp = ada_w.shape
    d_, m_, v_ = _adamw(ada_w[0], g_ada_w, m_ada_w[0], v_ada_w[0], "adamw_ada_w")
    deltas["ada_w"], new_m["ada_w"], new_v["ada_w"] = d_.reshape(shp), m_.reshape(shp), v_.reshape(shp)
    d_, m_, v_ = _adamw(_pack_small(wts), g_small_pk, _pack_small(ms), _pack_small(vs), "adamw_small")
    for dst, pk in ((deltas, d_), (new_m, m_), (new_v, v_)):
        dst.update(_unpack_small(pk, shapes))

    loss_all = lax.psum(loss[0, 0], ("x", "y", "c"))
    return (loss_all, grad_x.reshape(1, s, D), *[grads[n] for n in names], *[deltas[n] for n in names],
            *[new_m[n] for n in names], *[new_v[n] for n in names])
```

```python
import math

import jax
import jax.numpy as jnp
from jax import lax
from jax.experimental import pallas as pl
from jax.experimental.pallas import tpu as pltpu

F32, BF16 = jnp.float32, jnp.bfloat16
MESH = pl.DeviceIdType.MESH

D = 1024
EPS = 1e-6
GLA_H, GLA_DK, GLA_DV, GLA_LR = 4, 128, 256, 16
GLA_TAU = 16.0
GLA_CHUNK = 64
GLA_BLOCK = 512
ATT_GROUPS = ((128, 1), (512, 4), (2048, 16))
ATT_BLK = 128
ATT_HD = 64
ATT_W = 768
D_FF = 2816
ROPE_THETA = 10000.0
P_W = 7680
P_GV, P_GR, P_MA, P_MB, P_GQ, P_GK, P_AQ, P_AK, P_AV, P_LR = 0, 1024, 2048, 3072, 4096, 4608, 5120, 5888, 6656, 7424
W_IN = 7440
W_IN_SH, W_UP_SH, W_DOWN_SH = 1860, 1408, 704
VMEM_LIMIT = 56 * 1024 * 1024
ADAM_LR, ADAM_B1, ADAM_B2, ADAM_EPS, ADAM_WD, ADAM_STEP = 0.001, 0.9, 0.999, 1e-08, 0.01, 10
NEG = -1e30


def _tile(n, target, unit=128):
    best = None
    for t in range(unit, min(n, target) + 1, unit):
        if n % t == 0:
            best = t
    return best or n


def _params(sem):
    return pltpu.CompilerParams(dimension_semantics=sem, vmem_limit_bytes=VMEM_LIMIT)


def _dg(a, b, ca, cb):
    return lax.dot_general(a, b, (((ca,), (cb,)), ((), ())), preferred_element_type=F32)


def _sigmoid(v):
    return 1.0 / (1.0 + jnp.exp(-v))


def _mm(a, b, name, *, ta=False, tb=False, out_dtype=BF16, tm=1024, tn=1536, tk=1024, n_outer=True, comm=()):
    m = a.shape[1] if ta else a.shape[0]
    k = a.shape[0] if ta else a.shape[1]
    n = b.shape[0] if tb else b.shape[1]
    tm, tn, tk = _tile(m, tm), _tile(n, tn), _tile(k, tk)
    nm, nn, nk = m // tm, n // tn, k // tk
    in_out = out_dtype == F32
    c_ins, c_outs, c_alias, c_scratch = _carry(comm, 2, 1)

    def body(a_ref, b_ref, *rest):
        ci, o_ref, co = rest[:len(c_ins)], rest[len(c_ins)], rest[len(c_ins) + 1:len(c_ins) + 1 + len(c_outs)]
        scr = rest[len(c_ins) + 1 + len(c_outs):]
        kk = pl.program_id(2)
        if comm:
            step = (pl.program_id(0) * (nm if n_outer else nn) + pl.program_id(1)) * nk + kk

            @pl.when(step == 0)
            def _():
                _comm_phase(comm, ci, co, scr[-2], scr[-1], True)

        _mm_step(a_ref, b_ref, o_ref, scr, kk)
        if comm:
            @pl.when(step == nm * nn * nk - 1)
            def _():
                _comm_phase(comm, ci, co, scr[-2], scr[-1], False)

    def _mm_step(a_ref, b_ref, o_ref, scr, kk):
        p = _dg(a_ref[...].astype(BF16), b_ref[...].astype(BF16), 0 if ta else 1, 1 if tb else 0)
        if nk == 1:
            o_ref[...] = p.astype(o_ref.dtype)
        else:
            acc = o_ref if in_out else scr[0]

            @pl.when(kk == 0)
            def _():
                acc[...] = p

            @pl.when(kk > 0)
            def _():
                acc[...] += p

            if not in_out:
                @pl.when(kk == nk - 1)
                def _():
                    o_ref[...] = acc[...].astype(o_ref.dtype)

    if n_outer:
        ij = lambda g0, g1: (g1, g0)
        grid = (nn, nm, nk)
    else:
        ij = lambda g0, g1: (g0, g1)
        grid = (nm, nn, nk)
    a_map = (lambda g0, g1, kk: (kk, ij(g0, g1)[0])) if ta else (lambda g0, g1, kk: (ij(g0, g1)[0], kk))
    b_map = (lambda g0, g1, kk: (ij(g0, g1)[1], kk)) if tb else (lambda g0, g1, kk: (kk, ij(g0, g1)[1]))
    res = pl.pallas_call(
        body, name=name, grid=grid,
        in_specs=[pl.BlockSpec((tk, tm) if ta else (tm, tk), a_map),
                  pl.BlockSpec((tn, tk) if tb else (tk, tn), b_map)] + [HBM] * len(c_ins),
        out_specs=[pl.BlockSpec((tm, tn), lambda g0, g1, kk: ij(g0, g1))] + [HBM] * len(c_outs),
        out_shape=[jax.ShapeDtypeStruct((m, n), out_dtype)] + c_outs,
        scratch_shapes=([] if (in_out or nk == 1) else [pltpu.VMEM((tm, tn), F32)]) + c_scratch,
        input_output_aliases=c_alias,
        compiler_params=_params(("arbitrary",) * 3 if comm else ("parallel", "parallel", "arbitrary")),
    )(a, b, *c_ins)
    return (res[0], _split_units(comm, res[1:])) if comm else res[0]


def _rows(arr, rb, w=None, j=0):
    w = arr.shape[1] if w is None else w
    if callable(j):
        return arr, pl.BlockSpec((rb, w), lambda c, i: (i, j(c)))
    return arr, pl.BlockSpec((rb, w), lambda c, i: (i, j))


def _full(arr, w=None, j=0):
    w = arr.shape[1] if w is None else w
    if callable(j):
        return arr, pl.BlockSpec((arr.shape[0], w), lambda c, i: (0, j(c)))
    return arr, pl.BlockSpec((arr.shape[0], w), lambda c, i: (0, j))


def _halo(arr, rb, hb, w, j, before):
    per = rb // hb
    last = arr.shape[0] // hb - 1
    if before:
        rmap = lambda i: jnp.maximum(i * per - 1, 0)
    else:
        rmap = lambda i: jnp.minimum((i + 1) * per, last)
    return arr, pl.BlockSpec((hb, w), lambda c, i: (rmap(i), j(c) if callable(j) else j))


def _rowcall(fn, ins, outs, *, n_rows, rb, name, ncol=1):
    n_in = len(ins)
    nr = n_rows // rb

    def body(*refs):
        c, i = pl.program_id(0), pl.program_id(1)
        res = fn(c, i, *[r[...] for r in refs[:n_in]])
        for val, spec, o_ref in zip(res, outs, refs[n_in:]):
            if spec[2] == "row":
                o_ref[...] = val.astype(o_ref.dtype)
            else:
                @pl.when(i == 0)
                def _(o_ref=o_ref, val=val):
                    o_ref[...] = val.astype(o_ref.dtype)

                @pl.when(i > 0)
                def _(o_ref=o_ref, val=val):
                    o_ref[...] += val.astype(o_ref.dtype)

    out_specs = []
    for shape, dt, kind, block, col in outs:
        if kind == "row":
            out_specs.append(pl.BlockSpec(block, lambda c, i, col=col: (i, col(c))))
        else:
            out_specs.append(pl.BlockSpec(block, lambda c, i, col=col: (0, col(c))))
    return pl.pallas_call(
        body, name=name, grid=(ncol, nr),
        in_specs=[s for _, s in ins], out_specs=out_specs,
        out_shape=[jax.ShapeDtypeStruct(o[0], o[1]) for o in outs],
        compiler_params=_params(("parallel", "arbitrary")),
    )(*[a for a, _ in ins])


def _orow(n_rows, w, dt, rb, bw=None, col=lambda c: 0):
    return ((n_rows, w), dt, "row", (rb, bw or w), col)


def _oacc(r, w, bw=None, col=lambda c: 0):
    return ((r, w), F32, "acc", (r, bw or w), col)


def _csum(v):
    return jnp.sum(v, axis=0, keepdims=True)


def _rms(v):
    return lax.rsqrt(jnp.mean(v * v, axis=-1, keepdims=True) + EPS)


def _norm_bwd(xv, dh, w, scale):
    r = _rms(xv)
    xh = xv * r
    dxh = dh * (w * (1.0 + scale))
    dx = r * (dxh - xh * jnp.mean(dxh * xh, axis=-1, keepdims=True))
    t = dh * xh
    return dx, _csum(dh), _csum(t * w), _csum(t * (1.0 + scale))


def _rope_tables(pos_col, invf, s):
    def fn(c, i, pos, f):
        ang = pos.astype(F32) * f
        lane = lax.broadcasted_iota(jnp.int32, ang.shape, 1)
        sign = jnp.where((lane % ATT_HD) < ATT_HD // 2, -1.0, 1.0)
        return jnp.cos(ang), jnp.sin(ang) * sign

    rb = 512
    return _rowcall(fn, [_rows(pos_col, rb), _full(invf)], [_orow(s, 128, F32, rb), _orow(s, 128, F32, rb)],
                    n_rows=s, rb=rb, name="rope_tables")


def _swap_halves(t):
    n = t.shape[1]
    lane = lax.broadcasted_iota(jnp.int32, t.shape, 1)
    return jnp.where((lane % ATT_HD) < ATT_HD // 2, pltpu.roll(t, n - 32, 1), pltpu.roll(t, 32, 1))


def _rope_apply(t, cos, sin_signed, inverse):
    cw = jnp.concatenate([cos] * (t.shape[1] // 128), axis=1)
    sw = jnp.concatenate([sin_signed] * (t.shape[1] // 128), axis=1)
    if inverse:
        sw = -sw
    return t * cw + _swap_halves(t) * sw


DIL_ROWS = 512


def _to_dilated(scr, val, out_ref, r):
    if r == 1:
        out_ref[...] = val.astype(out_ref.dtype)
        return
    n = val.shape[0] // r
    for hh in range(2):
        scr[hh] = val[:, hh * 128:(hh + 1) * 128]
        for pr in range(r):
            out_ref[:, pr * 256 + hh * 128:pr * 256 + (hh + 1) * 128] = scr[hh, pl.ds(pr, n, stride=r), :].astype(out_ref.dtype)


def _from_dilated(scr, in_ref, r):
    if r == 1:
        return in_ref[...].astype(F32)
    n = in_ref.shape[0]
    for hh in range(2):
        for pr in range(r):
            scr[hh, pl.ds(pr, n, stride=r), :] = in_ref[:, pr * 256 + hh * 128:pr * 256 + (hh + 1) * 128].astype(F32)
    return jnp.concatenate([scr[0], scr[1]], axis=1)


def _dil_spec(r):
    return pl.BlockSpec((DIL_ROWS // r, r * 256), lambda i: (i, 0))


def _dil_shape(s, r, dt):
    return jax.ShapeDtypeStruct((s // r, r * 256), dt)


_DIL_SCRATCH = [pltpu.VMEM((2, DIL_ROWS, 128), F32)]
_RS = tuple(r for _, r in ATT_GROUPS)


def _rope_fwd(p, cos_t, sin_t, s):
    def body(*refs):
        ins, cs, sn, outs, scr = refs[:9], refs[9][...], refs[10][...], refs[11:20], refs[20]
        for t in range(3):
            for g, r in enumerate(_RS):
                val = ins[3 * t + g][...].astype(F32)
                _to_dilated(scr, _rope_apply(val, cs, sn, False) if t < 2 else val, outs[3 * t + g], r)

    res = pl.pallas_call(
        body, name="rope", grid=(s // DIL_ROWS,),
        in_specs=[pl.BlockSpec((DIL_ROWS, 256), lambda i, c=base // 256 + g: (i, c)) for base in (P_AQ, P_AK, P_AV) for g in range(3)]
        + [pl.BlockSpec((DIL_ROWS, 128), lambda i: (i, 0))] * 2,
        out_specs=[_dil_spec(r) for _ in range(3) for r in _RS],
        out_shape=[_dil_shape(s, r, BF16) for _ in range(3) for r in _RS],
        scratch_shapes=_DIL_SCRATCH, compiler_params=_params(("parallel",)),
    )(*([p] * 9), cos_t, sin_t)
    return res[0:3], res[3:6], res[6:9]


def _attn_combine(att, s):
    def body(o0, o1, o2, l0, l1, l2, o_ref, lse_ref, od1, od2, ld1, ld2, scr):
        ov = [_from_dilated(scr, ref, r) for ref, r in zip((o0, o1, o2), _RS)]
        lv = [_from_dilated(scr, ref, r) for ref, r in zip((l0, l1, l2), _RS)]
        mx = jnp.maximum(jnp.maximum(lv[0], lv[1]), lv[2])
        ev = [jnp.exp(l - mx) for l in lv]
        z = ev[0] + ev[1] + ev[2]
        o = ((ev[0] * ov[0] + ev[1] * ov[1] + ev[2] * ov[2]) / z).astype(BF16)
        lse = mx + jnp.log(z)
        o_ref[...] = o
        lse_ref[...] = lse
        for ref, r in zip((od1, od2), _RS[1:]):
            _to_dilated(scr, o.astype(F32), ref, r)
        for ref, r in zip((ld1, ld2), _RS[1:]):
            _to_dilated(scr, lse, ref, r)

    return pl.pallas_call(
        body, name="attn_combine", grid=(s // DIL_ROWS,),
        in_specs=[_dil_spec(r) for r in _RS] * 2,
        out_specs=[_dil_spec(1)] * 2 + [_dil_spec(r) for r in _RS[1:]] * 2,
        out_shape=[_dil_shape(s, 1, BF16), _dil_shape(s, 1, F32)] + [_dil_shape(s, r, BF16) for r in _RS[1:]]
        + [_dil_shape(s, r, F32) for r in _RS[1:]],
        scratch_shapes=_DIL_SCRATCH, compiler_params=_params(("parallel",)),
    )(*[a[0] for a in att], *[a[1] for a in att])


def _dilate(t, s):
    def body(t_ref, o1, o2, scr):
        val = t_ref[...].astype(F32)
        for ref, r in zip((o1, o2), _RS[1:]):
            _to_dilated(scr, val, ref, r)

    return pl.pallas_call(
        body, name="attn_dilate", grid=(s // DIL_ROWS,), in_specs=[_dil_spec(1)], out_specs=[_dil_spec(r) for r in _RS[1:]],
        out_shape=[_dil_shape(s, r, t.dtype) for r in _RS[1:]], scratch_shapes=_DIL_SCRATCH, compiler_params=_params(("parallel",)),
    )(t)


def _rope_bwd(datt, cos_t, sin_t, s):
    def body(*refs):
        ins, cs, sn, outs, scr = refs[:9], refs[9][...], refs[10][...], refs[11:14], refs[14]
        for t in range(3):
            for g, r in enumerate(_RS):
                val = _from_dilated(scr, ins[3 * t + g], r)
                outs[t][:, g * 256:(g + 1) * 256] = (_rope_apply(val, cs, sn, True) if t < 2 else val).astype(BF16)

    return pl.pallas_call(
        body, name="rope_bwd", grid=(s // DIL_ROWS,),
        in_specs=[_dil_spec(r) for _ in range(3) for r in _RS] + [pl.BlockSpec((DIL_ROWS, 128), lambda i: (i, 0))] * 2,
        out_specs=[pl.BlockSpec((DIL_ROWS, ATT_W), lambda i: (i, 0))] * 3,
        out_shape=[jax.ShapeDtypeStruct((s, ATT_W), BF16)] * 3,
        scratch_shapes=_DIL_SCRATCH, compiler_params=_params(("parallel",)),
    )(*[datt[g][t] for t in range(3) for g in range(3)], cos_t, sin_t)


def _gla_decays(la_c, tri):
    b = jnp.dot(tri, la_c, precision=lax.Precision.HIGHEST, preferred_element_type=F32)
    row = lax.broadcasted_iota(jnp.int32, b.shape, 0)
    bmid = jnp.sum(jnp.where(row == GLA_CHUNK // 2 - 1, b, 0.0), axis=0, keepdims=True)
    blast = jnp.sum(jnp.where(row == GLA_CHUNK - 1, b, 0.0), axis=0, keepdims=True)
    return b, bmid, blast


def _gla_fwd(p, la, s, comm=()):
    tb, ch = GLA_BLOCK, GLA_CHUNK
    nb, nc = s // tb, tb // ch
    scale = GLA_DK ** -0.5
    c_ins, c_outs, c_alias, c_scratch = _carry(comm, 4, 2)

    def body(q_ref, k_ref, v_ref, la_ref, *rest):
        ci, (o_ref, st_ref) = rest[:len(c_ins)], rest[len(c_ins):len(c_ins) + 2]
        co, state = rest[len(c_ins) + 2:len(c_ins) + 2 + len(c_outs)], rest[len(c_ins) + 2 + len(c_outs)]
        step = pl.program_id(0) * nb + pl.program_id(1)
        if comm:
            @pl.when(step == 0)
            def _():
                _comm_phase(comm, ci, co, rest[-2], rest[-1], True)

        _gla_fwd_step(q_ref, k_ref, v_ref, la_ref, o_ref, st_ref, state)
        if comm:
            @pl.when(step == GLA_H * nb - 1)
            def _():
                _comm_phase(comm, ci, co, rest[-2], rest[-1], False)

    def _gla_fwd_step(q_ref, k_ref, v_ref, la_ref, o_ref, st_ref, state):
        @pl.when(pl.program_id(1) == 0)
        def _():
            state[...] = jnp.zeros_like(state)

        ri = lax.broadcasted_iota(jnp.int32, (ch, ch), 0)
        ci = lax.broadcasted_iota(jnp.int32, (ch, ch), 1)
        causal = ci <= ri
        tri = causal.astype(F32)
        for c in range(nc):
            sl = pl.ds(c * ch, ch)
            b, bmid, blast = _gla_decays(la_ref[sl, :], tri)
            q = q_ref[sl, :].astype(F32) * scale
            k = k_ref[sl, :].astype(F32)
            v = v_ref[sl, :]
            qgt = (q * jnp.exp(b)).astype(BF16)
            qgn = (q * jnp.exp(b - bmid)).astype(BF16)
            kgn = (k * jnp.exp(bmid - b)).astype(BF16)
            kd = (k * jnp.exp(blast - b)).astype(BF16)
            a = jnp.where(causal, _dg(qgn, kgn, 1, 1), 0.0)
            st = state[...]
            st_ref[0, c] = st
            o_ref[sl, :] = _dg(a.astype(BF16), v, 1, 0) + _dg(qgt, st.astype(BF16), 1, 1)
            state[...] = jnp.exp(blast) * st + _dg(v, kd, 0, 0)

    res = pl.pallas_call(
        body, name="gla_fwd", grid=(GLA_H, nb),
        in_specs=[pl.BlockSpec((tb, GLA_DK), lambda h, t: (t, P_GQ // GLA_DK + h)),
                  pl.BlockSpec((tb, GLA_DK), lambda h, t: (t, P_GK // GLA_DK + h)),
                  pl.BlockSpec((tb, GLA_DV), lambda h, t: (t, P_GV // GLA_DV + h)),
                  pl.BlockSpec((tb, GLA_DK), lambda h, t: (t, h))] + [HBM] * len(c_ins),
        out_specs=[pl.BlockSpec((tb, GLA_DV), lambda h, t: (t, h)),
                   pl.BlockSpec((1, nc, GLA_DV, GLA_DK), lambda h, t: (h, t, 0, 0))] + [HBM] * len(c_outs),
        out_shape=[jax.ShapeDtypeStruct((s, GLA_H * GLA_DV), F32),
                   jax.ShapeDtypeStruct((GLA_H, s // ch, GLA_DV, GLA_DK), F32)] + c_outs,
        scratch_shapes=[pltpu.VMEM((GLA_DV, GLA_DK), F32)] + c_scratch,
        input_output_aliases=c_alias,
        compiler_params=_params(("arbitrary", "arbitrary") if comm else ("parallel", "arbitrary")),
    )(p, p, p, la, *c_ins)
    return res[0], res[1], _split_units(comm, res[2:])


def _gla_bwd(p, la, states, do, s, comm=()):
    tb, ch = GLA_BLOCK, GLA_CHUNK
    nb, nc = s // tb, tb // ch
    scale = GLA_DK ** -0.5
    c_ins, c_outs, c_alias, c_scratch = _carry(comm, 6, 4)

    def body(q_ref, k_ref, v_ref, la_ref, st_ref, do_ref, *rest):
        ci, outs = rest[:len(c_ins)], rest[len(c_ins):len(c_ins) + 4]
        co, dstate = rest[len(c_ins) + 4:len(c_ins) + 4 + len(c_outs)], rest[len(c_ins) + 4 + len(c_outs)]
        step = pl.program_id(0) * nb + pl.program_id(1)
        if comm:
            @pl.when(step == 0)
            def _():
                _comm_phase(comm, ci, co, rest[-2], rest[-1], True)

        _gla_bwd_step(q_ref, k_ref, v_ref, la_ref, st_ref, do_ref, *outs, dstate)
        if comm:
            @pl.when(step == GLA_H * nb - 1)
            def _():
                _comm_phase(comm, ci, co, rest[-2], rest[-1], False)

    def _gla_bwd_step(q_ref, k_ref, v_ref, la_ref, st_ref, do_ref, dq_ref, dk_ref, dv_ref, dla_ref, dstate):
        @pl.when(pl.program_id(1) == 0)
        def _():
            dstate[...] = jnp.zeros_like(dstate)

        ri = lax.broadcasted_iota(jnp.int32, (ch, ch), 0)
        ci = lax.broadcasted_iota(jnp.int32, (ch, ch), 1)
        causal = ci <= ri
        tri = causal.astype(F32)
        tri_t = (ci >= ri).astype(F32)
        for c in reversed(range(nc)):
            sl = pl.ds(c * ch, ch)
            b, bmid, blast = _gla_decays(la_ref[sl, :], tri)
            q = q_ref[sl, :].astype(F32) * scale
            k = k_ref[sl, :].astype(F32)
            v = v_ref[sl, :]
            e_b, e_qn, e_kn, e_kd = jnp.exp(b), jnp.exp(b - bmid), jnp.exp(bmid - b), jnp.exp(blast - b)
            dec = jnp.exp(blast)
            qgt, qgn, kgn, kd = q * e_b, q * e_qn, k * e_kn, k * e_kd
            qgt_b, qgn_b, kgn_b, kd_b = qgt.astype(BF16), qgn.astype(BF16), kgn.astype(BF16), kd.astype(BF16)
            st0 = st_ref[0, c]
            dst = dstate[...]
            dst_b = dst.astype(BF16)
            do_b = do_ref[sl, :].astype(BF16)
            a = jnp.where(causal, _dg(qgn_b, kgn_b, 1, 1), 0.0).astype(BF16)
            da = jnp.where(causal, _dg(do_b, v, 1, 1), 0.0).astype(BF16)
            dqgn = _dg(da, kgn_b, 1, 0)
            dqgt = _dg(do_b, st0.astype(BF16), 1, 0)
            dkgn = _dg(da, qgn_b, 0, 0)
            dv = _dg(a, do_b, 0, 0) + _dg(kd_b, dst_b, 1, 1)
            dkd = _dg(v, dst_b, 1, 0)
            ddec = jnp.sum(st0 * dst, axis=0, keepdims=True)
            dstate[...] = dec * dst + _dg(do_b, qgt_b, 0, 0)
            dq_ref[sl, :] = (scale * (dqgn * e_qn + dqgt * e_b)).astype(dq_ref.dtype)
            dk_ref[sl, :] = (dkgn * e_kn + dkd * e_kd).astype(dk_ref.dtype)
            dv_ref[sl, :] = dv.astype(dv_ref.dtype)
            db = dqgn * qgn + dqgt * qgt - dkgn * kgn - dkd * kd
            extra = jnp.sum(dkd * kd, axis=0, keepdims=True) + ddec * dec
            dla_ref[sl, :] = jnp.dot(tri_t, db, precision=lax.Precision.HIGHEST, preferred_element_type=F32) + extra

    rev = lambda t: nb - 1 - t
    res = pl.pallas_call(
        body, name="gla_bwd", grid=(GLA_H, nb),
        in_specs=[pl.BlockSpec((tb, GLA_DK), lambda h, t: (rev(t), P_GQ // GLA_DK + h)),
                  pl.BlockSpec((tb, GLA_DK), lambda h, t: (rev(t), P_GK // GLA_DK + h)),
                  pl.BlockSpec((tb, GLA_DV), lambda h, t: (rev(t), P_GV // GLA_DV + h)),
                  pl.BlockSpec((tb, GLA_DK), lambda h, t: (rev(t), h)),
                  pl.BlockSpec((1, nc, GLA_DV, GLA_DK), lambda h, t: (h, rev(t), 0, 0)),
                  pl.BlockSpec((tb, GLA_DV), lambda h, t: (rev(t), h))] + [HBM] * len(c_ins),
        out_specs=[pl.BlockSpec((tb, GLA_DK), lambda h, t: (rev(t), h)),
                   pl.BlockSpec((tb, GLA_DK), lambda h, t: (rev(t), h)),
                   pl.BlockSpec((tb, GLA_DV), lambda h, t: (rev(t), h)),
                   pl.BlockSpec((tb, GLA_DK), lambda h, t: (rev(t), h))] + [HBM] * len(c_outs),
        out_shape=[jax.ShapeDtypeStruct((s, GLA_H * GLA_DK), BF16),
                   jax.ShapeDtypeStruct((s, GLA_H * GLA_DK), BF16),
                   jax.ShapeDtypeStruct((s, GLA_H * GLA_DV), BF16),
                   jax.ShapeDtypeStruct((s, GLA_H * GLA_DK), F32)] + c_outs,
        scratch_shapes=[pltpu.VMEM((GLA_DV, GLA_DK), F32)] + c_scratch,
        input_output_aliases=c_alias,
        compiler_params=_params(("arbitrary", "arbitrary") if comm else ("parallel", "arbitrary")),
    )(p, p, p, la, states, do, *c_ins)
    return res[0], res[1], res[2], res[3], _split_units(comm, res[4:])


def _head_masks():
    lane = lax.broadcasted_iota(jnp.int32, (1, 4 * ATT_HD), 1)
    return [(lane >= h * ATT_HD) & (lane < (h + 1) * ATT_HD) for h in range(4)]


def _attn_fwd(qv, kv, pv, g, r, s):
    ln = s // r
    nblk = ln // ATT_BLK
    qcol = lambda pr: pr
    vcol = qcol
    prev = lambda n: jnp.maximum(n - 1, 0)

    def body(q_ref, kp_ref, kc_ref, vp_ref, vc_ref, o_ref, lse_ref):
        has_prev = pl.program_id(1) > 0
        ri = lax.broadcasted_iota(jnp.int32, (ATT_BLK, ATT_BLK), 0)
        ci = lax.broadcasted_iota(jnp.int32, (ATT_BLK, ATT_BLK), 1)
        m_cur = ci <= ri
        m_prev = (ci >= ri) & has_prev
        q, kp, kc, vp, vc = q_ref[...], kp_ref[...], kc_ref[...], vp_ref[...], vc_ref[...]
        o = jnp.zeros((ATT_BLK, 256), F32)
        lse = jnp.zeros((ATT_BLK, 256), F32)
        for hm in _head_masks():
            qm = jnp.where(hm, q, jnp.zeros_like(q))
            sc = jnp.where(m_cur, _dg(qm, kc, 1, 1) * 0.125, NEG)
            sp = jnp.where(m_prev, _dg(qm, kp, 1, 1) * 0.125, NEG)
            mx = jnp.maximum(jnp.max(sc, axis=1, keepdims=True), jnp.max(sp, axis=1, keepdims=True))
            pc, pp = jnp.exp(sc - mx), jnp.exp(sp - mx)
            den = jnp.sum(pc, axis=1, keepdims=True) + jnp.sum(pp, axis=1, keepdims=True)
            oh = (_dg(pc.astype(BF16), vc, 1, 0) + _dg(pp.astype(BF16), vp, 1, 0)) / den
            o = jnp.where(hm, oh, o)
            lse = jnp.where(hm, mx + jnp.log(den), lse)
        o_ref[...] = o.astype(o_ref.dtype)
        lse_ref[...] = lse

    blk = (ATT_BLK, 256)
    o, lse = pl.pallas_call(
        body, name=f"attn_fwd_{g}", grid=(r, nblk),
        in_specs=[pl.BlockSpec(blk, lambda pr, n: (n, qcol(pr))),
                  pl.BlockSpec(blk, lambda pr, n: (prev(n), qcol(pr))),
                  pl.BlockSpec(blk, lambda pr, n: (n, qcol(pr))),
                  pl.BlockSpec(blk, lambda pr, n: (prev(n), vcol(pr))),
                  pl.BlockSpec(blk, lambda pr, n: (n, vcol(pr)))],
        out_specs=[pl.BlockSpec(blk, lambda pr, n: (n, pr)), pl.BlockSpec(blk, lambda pr, n: (n, pr))],
        out_shape=[jax.ShapeDtypeStruct((ln, r * 256), BF16), jax.ShapeDtypeStruct((ln, r * 256), F32)],
        compiler_params=_params(("parallel", "parallel")),
    )(qv, kv, kv, pv, pv)
    return o, lse


def _attn_bwd(qv, kv, pv, dov, ov, lv, g, r, s):
    ln = s // r
    nblk = ln // ATT_BLK
    qcol = lambda pr: pr
    vcol = qcol
    prev = lambda n: jnp.maximum(n - 1, 0)
    nxt = lambda n: jnp.minimum(n + 1, nblk - 1)

    def body(qc_ref, qn_ref, kp_ref, kc_ref, vp_ref, vc_ref, doc_ref, don_ref, oc_ref, on_ref, lc_ref, ln_ref,
             dq_ref, dk_ref, dv_ref):
        n = pl.program_id(1)
        has_prev, has_next = n > 0, n < nblk - 1
        ri = lax.broadcasted_iota(jnp.int32, (ATT_BLK, ATT_BLK), 0)
        ci = lax.broadcasted_iota(jnp.int32, (ATT_BLK, ATT_BLK), 1)
        m_cur = ci <= ri
        m_prev = (ci >= ri) & has_prev
        m_next = (ci >= ri) & has_next
        qc, qn, kp, kc, vp, vc = qc_ref[...], qn_ref[...], kp_ref[...], kc_ref[...], vp_ref[...], vc_ref[...]
        doc, don = doc_ref[...], don_ref[...]
        pc_full = doc.astype(F32) * oc_ref[...].astype(F32)
        pn_full = don.astype(F32) * on_ref[...].astype(F32)
        lc, lnx = lc_ref[...], ln_ref[...]
        dq = jnp.zeros((ATT_BLK, 256), F32)
        dk = jnp.zeros((ATT_BLK, 256), F32)
        dv = jnp.zeros((ATT_BLK, 256), F32)
        zb = jnp.zeros_like(qc)
        for hm in _head_masks():
            qcm, qnm = jnp.where(hm, qc, zb), jnp.where(hm, qn, zb)
            docm, donm = jnp.where(hm, doc, zb), jnp.where(hm, don, zb)
            lse_c = jnp.max(jnp.where(hm, lc, NEG), axis=1, keepdims=True)
            lse_n = jnp.max(jnp.where(hm, lnx, NEG), axis=1, keepdims=True)
            del_c = jnp.sum(jnp.where(hm, pc_full, 0.0), axis=1, keepdims=True)
            del_n = jnp.sum(jnp.where(hm, pn_full, 0.0), axis=1, keepdims=True)
            pr_ = jnp.where(m_cur, jnp.exp(_dg(qcm, kc, 1, 1) * 0.125 - lse_c), 0.0)
            ds = (pr_ * (_dg(docm, vc, 1, 1) - del_c) * 0.125).astype(BF16)
            dqh = _dg(ds, kc, 1, 0)
            dkh = _dg(ds, qc, 0, 0)
            dvh = _dg(pr_.astype(BF16), doc, 0, 0)
            pr_ = jnp.where(m_prev, jnp.exp(_dg(qcm, kp, 1, 1) * 0.125 - lse_c), 0.0)
            ds = (pr_ * (_dg(docm, vp, 1, 1) - del_c) * 0.125).astype(BF16)
            dqh = dqh + _dg(ds, kp, 1, 0)
            pr_ = jnp.where(m_next, jnp.exp(_dg(qnm, kc, 1, 1) * 0.125 - lse_n), 0.0)
            ds = (pr_ * (_dg(donm, vc, 1, 1) - del_n) * 0.125).astype(BF16)
            dkh = dkh + _dg(ds, qn, 0, 0)
            dvh = dvh + _dg(pr_.astype(BF16), don, 0, 0)
            dq = jnp.where(hm, dqh, dq)
            dk = jnp.where(hm, dkh, dk)
            dv = jnp.where(hm, dvh, dv)
        dq_ref[...] = dq.astype(dq_ref.dtype)
        dk_ref[...] = dk.astype(dk_ref.dtype)
        dv_ref[...] = dv.astype(dv_ref.dtype)

    blk = (ATT_BLK, 256)
    cur = lambda col: pl.BlockSpec(blk, lambda pr, n: (n, col(pr)))
    prv = lambda col: pl.BlockSpec(blk, lambda pr, n: (prev(n), col(pr)))
    nx = lambda col: pl.BlockSpec(blk, lambda pr, n: (nxt(n), col(pr)))
    own = lambda pr: pr
    outs = pl.pallas_call(
        body, name=f"attn_bwd_{g}", grid=(r, nblk),
        in_specs=[cur(qcol), nx(qcol), prv(qcol), cur(qcol), prv(vcol), cur(vcol),
                  cur(own), nx(own), cur(own), nx(own), cur(own), nx(own)],
        out_specs=[cur(own), cur(own), cur(own)],
        out_shape=[jax.ShapeDtypeStruct((ln, r * 256), BF16)] * 3,
        compiler_params=_params(("parallel", "parallel")),
    )(qv, qv, kv, kv, pv, pv, dov, dov, ov, ov, lv, lv)
    return outs


def _gelu_parts(gv):
    cdf = 0.5 * (1.0 + lax.erf(gv * (2.0 ** -0.5)))
    pdf = jnp.exp(-0.5 * gv * gv) * (1.0 / math.sqrt(2.0 * math.pi))
    return cdf, pdf


def _pick_row(t, k):
    row = lax.broadcasted_iota(jnp.int32, t.shape, 0)
    return jnp.sum(jnp.where(row == k, t, 0.0), axis=0, keepdims=True)


def _shift_rows(u, halo, n):
    row = lax.broadcasted_iota(jnp.int32, u.shape, 0)
    out = pltpu.roll(u, n, 0)
    for k in range(n):
        out = jnp.where(row == k, _pick_row(halo, 16 - n + k), out)
    return out


def _shift_rows_up(u, halo, n):
    rb = u.shape[0]
    row = lax.broadcasted_iota(jnp.int32, u.shape, 0)
    out = pltpu.roll(u, rb - n, 0)
    for k in range(n):
        out = jnp.where(row == rb - n + k, _pick_row(halo, k), out)
    return out


def _conv(u, halo, cw, cb):
    return cb + _pick_row(cw, 0) * _shift_rows(u, halo, 2) + _pick_row(cw, 1) * _shift_rows(u, halo, 1) + _pick_row(cw, 2) * u


def _local_step(x, mod, pos_col, target, sm, w_sh, chip, core):
    s = x.shape[0]
    shift1, scale1, gate1, shift2, scale2, gate2 = [mod[i:i + 1, :] for i in range(6)]
    rb = 256
    chip1 = chip.reshape(1)

    def f_norm1(c, i, xv, nw, sc, sh):
        return ((xv * _rms(xv) * nw) * (1.0 + sc) + sh,)

    (h,) = _rowcall(f_norm1, [_rows(x, rb), _full(sm["n1w"]), _full(scale1), _full(shift1)],
                    [_orow(s, D, BF16, rb)], n_rows=s, rb=rb, name="norm1")
    own = lambda got, i: lax.dynamic_update_slice(got, w_sh[i], (chip, 0, 0))
    [got0] = _comm_call("gather_w_in_ici", [_u_gather_ici(w_sh, (0,))])
    [got0] = _comm_call("gather_w_in_d2d", [_u_gather_d2d(got0, (0,))])
    w = dict(win=_win_to_kernel(_cols_join(own(got0[0], 0))))
    p, [got123] = _mm(h, w["win"], "in_proj", tm=1024, tn=1536, comm=[_u_gather_ici(w_sh, (1, 2, 3))])

    def f_gla_pre(c, i, glr, w2, gb):
        z = _dg(glr, w2.astype(BF16), 1, 0) + gb
        return ((jnp.minimum(z, 0.0) - jnp.log(1.0 + jnp.exp(-jnp.abs(z)))) * (1.0 / GLA_TAU),)

    (la,) = _rowcall(f_gla_pre, [_rows(p, rb, 128, P_LR // 128), _full(sm["w2"]), _full(sm["gb"])],
                     [_orow(s, 512, F32, rb)], n_rows=s, rb=rb, name="gla_pre")
    o_gla, states, [got123, got45] = _gla_fwd(p, la, s, comm=[_u_gather_d2d(got123, (1, 2, 3)), _u_gather_ici(w_sh, (4, 5))])
    w.update(wgb=own(got123[0], 1).reshape(1024, D), wab=_cols_join(own(got123[1], 2)), wout=own(got123[2], 3).reshape(D, D))

    def f_gla_post(c, i, ov, gnw, gr):
        on = jnp.concatenate([ov[:, k * 256:(k + 1) * 256] * _rms(ov[:, k * 256:(k + 1) * 256]) * gnw
                              for k in range(GLA_H)], axis=1)
        g = gr.astype(F32)
        return (on * (g * _sigmoid(g)),)

    (og,) = _rowcall(f_gla_post, [_rows(o_gla, rb), _full(sm["gnw"]), _rows(p, rb, 1024, P_GR // 1024)],
                     [_orow(s, 1024, BF16, rb)], n_rows=s, rb=rb, name="gla_post")
    y_gla = _mm(og, w["wgb"], "gla_branch")

    invf = jnp.tile(ROPE_THETA ** (-jnp.arange(ATT_HD // 2, dtype=F32) / (ATT_HD // 2)), 4).reshape(1, 128)
    cos_t, sin_t = _rope_tables(pos_col, invf, s)

    q_d, k_d, v_d = _rope_fwd(p, cos_t, sin_t, s)
    att = [_attn_fwd(q_d[g], k_d[g], v_d[g], g, r, s) for g, r in enumerate(_RS)]
    o_att, lse, o_d1, o_d2, lse_d1, lse_d2 = _attn_combine(att, s)
    y_att = _mm(o_att, w["wab"], "attn_branch")

    def f_merge(c, i, ma, mb, yg, ya):
        return (_sigmoid(ma.astype(F32)) * yg.astype(F32) + _sigmoid(mb.astype(F32)) * ya.astype(F32),)

    (mixed,) = _rowcall(f_merge, [_rows(p, rb, D, P_MA // D), _rows(p, rb, D, P_MB // D), _rows(y_gla, rb), _rows(y_att, rb)],
                        [_orow(s, D, BF16, rb)], n_rows=s, rb=rb, name="merge")
    z1, [got45] = _mm(mixed, w["wout"], "out_proj", comm=[_u_gather_d2d(got45, (4, 5))])
    w.update(wup=_ff_to_kernel(_cols_join(own(got45[0], 4))), wdown=own(got45[1], 5).reshape(D_FF, D))

    def f_norm2(c, i, xv, z, g1, nw, sc, sh):
        x1 = xv + g1 * z.astype(F32)
        return (x1, (x1 * _rms(x1) * nw) * (1.0 + sc) + sh)

    x1, h2 = _rowcall(f_norm2, [_rows(x, rb), _rows(z1, rb), _full(gate1), _full(sm["n2w"]), _full(scale2), _full(shift2)],
                      [_orow(s, D, F32, rb), _orow(s, D, BF16, rb)], n_rows=s, rb=rb, name="norm2")
    u = _mm(h2, w["wup"], "up_proj", tn=1408)

    cwid = 2 * W_UP_SH

    def f_ffn(c, i, uv, hl, cw, cb):
        uc = _conv(uv.astype(F32), hl.astype(F32) * (i > 0).astype(F32), cw, cb)
        val, gt = uc[:, :W_UP_SH], uc[:, W_UP_SH:]
        cdf, _ = _gelu_parts(gt)
        return (gt * cdf * val,)

    ccol = lambda c: c
    (hidden,) = _rowcall(f_ffn, [_rows(u, rb, cwid, ccol), _halo(u, rb, 16, cwid, ccol, True),
                                 _full(sm["cw"], cwid, ccol), _full(sm["cb"], cwid, ccol)],
                         [_orow(s, D_FF, BF16, rb, W_UP_SH, ccol)], n_rows=s, rb=rb, name="conv_geglu", ncol=2)
    z2 = _mm(hidden, w["wdown"], "down_proj", tk=1408)

    def f_final(c, i, x1v, z, g2, fw, tgt):
        x2 = x1v + g2 * z.astype(F32)
        r = _rms(x2)
        xh = x2 * r
        e = xh * fw - tgt
        loss = 0.5 * jnp.sum(jnp.mean(e * e, axis=-1, keepdims=True), axis=0, keepdims=True)
        dy = e * (1.0 / D)
        dxh = dy * fw
        dx2 = r * (dxh - xh * jnp.mean(dxh * xh, axis=-1, keepdims=True))
        return (loss, dx2, dx2 * g2, _csum(dy * xh), _csum(dx2 * z.astype(F32)))

    loss, dx2, dz2, d_fnw, d_gate2 = _rowcall(
        f_final, [_rows(x1, rb), _rows(z2, rb), _full(gate2), _full(sm["fnw"]), _rows(target, rb)],
        [_oacc(1, 1), _orow(s, D, F32, rb), _orow(s, D, BF16, rb), _oacc(1, D), _oacc(1, D)],
        n_rows=s, rb=rb, name="final_loss")
    d_hidden = _mm(dz2, w["wdown"], "down_proj_dx", tb=True, tn=1408)
    g_wdown = _mm(hidden, dz2, "down_proj_dw", ta=True, out_dtype=F32, tm=1408, tn=1024, tk=2048)

    def f_ffn_bwd(c, i, uv, hl, dh, cw, cb):
        uf = uv.astype(F32)
        hf = hl.astype(F32) * (i > 0).astype(F32)
        u1, u2 = _shift_rows(uf, hf, 1), _shift_rows(uf, hf, 2)
        uc = cb + _pick_row(cw, 0) * u2 + _pick_row(cw, 1) * u1 + _pick_row(cw, 2) * uf
        val, gt = uc[:, :W_UP_SH], uc[:, W_UP_SH:]
        cdf, pdf = _gelu_parts(gt)
        dhf = dh.astype(F32)
        duc = jnp.concatenate([dhf * (gt * cdf), dhf * val * (cdf + gt * pdf)], axis=1)
        dcw = jnp.concatenate([_csum(duc * u2), _csum(duc * u1), _csum(duc * uf)], axis=0)
        return (duc, _csum(duc), dcw)

    duc, d_cb, d_cw = _rowcall(
        f_ffn_bwd, [_rows(u, rb, cwid, ccol), _halo(u, rb, 16, cwid, ccol, True), _rows(d_hidden, rb, W_UP_SH, ccol),
                    _full(sm["cw"], cwid, ccol), _full(sm["cb"], cwid, ccol)],
        [_orow(s, 2 * D_FF, BF16, rb, cwid, ccol), _oacc(1, 2 * D_FF, cwid, ccol), _oacc(3, 2 * D_FF, cwid, ccol)],
        n_rows=s, rb=rb, name="conv_geglu_bwd", ncol=2)

    def f_conv_t(c, i, dv, hl, cw):
        df = dv.astype(F32)
        hf = hl.astype(F32) * (i < s // rb - 1).astype(F32)
        return (_pick_row(cw, 2) * df + _pick_row(cw, 1) * _shift_rows_up(df, hf, 1) + _pick_row(cw, 0) * _shift_rows_up(df, hf, 2),)

    (du,) = _rowcall(f_conv_t, [_rows(duc, rb, cwid, ccol), _halo(duc, rb, 16, cwid, ccol, False), _full(sm["cw"], cwid, ccol)],
                     [_orow(s, 2 * D_FF, BF16, rb, cwid, ccol)], n_rows=s, rb=rb, name="conv_transpose", ncol=2)
    g_wup = _mm(h2, du, "up_proj_dw", ta=True, out_dtype=F32, tm=1024, tn=1408, tk=2048)
    gs45 = [_cols_split(_ff_from_kernel(g_wup)), g_wdown.reshape(4, W_DOWN_SH, 1024)]
    d_h2, [land45] = _mm(du, w["wup"], "up_proj_dx", tb=True, tk=1408, comm=[_u_pair_send(gs45, (4, 5))])
    ts45 = [_pair_add(g, ld, core, "grad_pair_add_" + BIG[i]) for g, ld, i in zip(gs45, land45, (4, 5))]

    def f_norm2_bwd(c, i, x1v, dh, dxr, z, nw, sc, g1):
        dxn, dsh, dsc, dnw = _norm_bwd(x1v, dh.astype(F32), nw, sc)
        dx1 = dxr + dxn
        return (dx1, dx1 * g1, dsh, dsc, dnw, _csum(dx1 * z.astype(F32)))

    dx1, dz1, d_shift2, d_scale2, d_n2w, d_gate1 = _rowcall(
        f_norm2_bwd, [_rows(x1, rb), _rows(d_h2, rb), _rows(dx2, rb), _rows(z1, rb), _full(sm["n2w"]), _full(scale2), _full(gate1)],
        [_orow(s, D, F32, rb), _orow(s, D, BF16, rb), _oacc(1, D), _oacc(1, D), _oacc(1, D), _oacc(1, D)],
        n_rows=s, rb=rb, name="norm2_bwd")
    d_mixed = _mm(dz1, w["wout"], "out_proj_dx", tb=True)
    g_wout = _mm(mixed, dz1, "out_proj_dw", ta=True, out_dtype=F32, tk=2048)

    def f_merge_bwd(c, i, dm, ma, mb, yg, ya):
        dmf, ygf, yaf = dm.astype(F32), yg.astype(F32), ya.astype(F32)
        sa, sb = _sigmoid(ma.astype(F32)), _sigmoid(mb.astype(F32))
        return (dmf * sa, dmf * sb, dmf * ygf * sa * (1.0 - sa), dmf * yaf * sb * (1.0 - sb))

    dy_gla, dy_att, d_ma, d_mb = _rowcall(
        f_merge_bwd, [_rows(d_mixed, rb), _rows(p, rb, D, P_MA // D), _rows(p, rb, D, P_MB // D), _rows(y_gla, rb), _rows(y_att, rb)],
        [_orow(s, D, BF16, rb)] * 4, n_rows=s, rb=rb, name="merge_bwd")
    d_og = _mm(dy_gla, w["wgb"], "gla_branch_dx", tb=True)
    g_wgb = _mm(og, dy_gla, "gla_branch_dw", ta=True, out_dtype=F32, tk=2048)
    d_oatt = _mm(dy_att, w["wab"], "attn_branch_dx", tb=True)
    g_wab = _mm(o_att, dy_att, "attn_branch_dw", ta=True, out_dtype=F32, tk=2048)

    def f_gla_post_bwd(c, i, ov, gnw, gr, dog):
        g = gr.astype(F32)
        sg = _sigmoid(g)
        silu = g * sg
        dof = dog.astype(F32)
        don = dof * silu
        on_parts, do_parts, dgn = [], [], jnp.zeros((1, 256), F32)
        for k in range(GLA_H):
            oh = ov[:, k * 256:(k + 1) * 256]
            dh = don[:, k * 256:(k + 1) * 256]
            r = _rms(oh)
            xh = oh * r
            dgn = dgn + _csum(dh * xh)
            dxh = dh * gnw
            do_parts.append(r * (dxh - xh * jnp.mean(dxh * xh, axis=-1, keepdims=True)))
            on_parts.append(xh * gnw)
        on = jnp.concatenate(on_parts, axis=1)
        dgr = dof * on * (sg * (1.0 + g * (1.0 - sg)))
        return (jnp.concatenate(do_parts, axis=1), dgr, dgn)

    do_gla, d_gr, d_gnw = _rowcall(
        f_gla_post_bwd, [_rows(o_gla, rb), _full(sm["gnw"]), _rows(p, rb, 1024, P_GR // 1024), _rows(d_og, rb)],
        [_orow(s, 1024, F32, rb), _orow(s, 1024, BF16, rb), _oacc(1, 256)], n_rows=s, rb=rb, name="gla_post_bwd")
    gs123 = [g_wgb.reshape(4, 256, 1024), _cols_split(g_wab), g_wout.reshape(4, 256, 1024)]
    d_gq, d_gk, d_gv, d_la, [r45, land123] = _gla_bwd(p, la, states, do_gla, s,
                                                      comm=[_u_chip_exchange(ts45), _u_pair_send(gs123, (1, 2, 3))])
    half45 = [_chip_sum(t, r, chip1, "grad_chip_sum_" + BIG[i]) for t, r, i in zip(ts45, r45, (4, 5))]
    ts123 = [_pair_add(g, ld, core, "grad_pair_add_" + BIG[i]) for g, ld, i in zip(gs123, land123, (1, 2, 3))]

    def f_gla_pre_bwd(c, i, lav, dlav, glr, w2):
        dz = dlav * (1.0 / GLA_TAU) * (1.0 - jnp.exp(GLA_TAU * lav))
        dzb = dz.astype(BF16)
        return (_dg(dzb, w2.astype(BF16), 1, 1), _csum(dz), _dg(glr, dzb, 0, 0))

    d_glr, d_gb, d_w2 = _rowcall(
        f_gla_pre_bwd, [_rows(la, rb), _rows(d_la, rb), _rows(p, rb, 128, P_LR // 128), _full(sm["w2"])],
        [_orow(s, 128, BF16, rb), _oacc(1, 512), _oacc(128, 512)], n_rows=s, rb=rb, name="gla_pre_bwd")

    do_d = [d_oatt] + list(_dilate(d_oatt, s))
    datt = [_attn_bwd(q_d[g], k_d[g], v_d[g], do_d[g], (o_att, o_d1, o_d2)[g], (lse, lse_d1, lse_d2)[g], g, r, s)
            for g, r in enumerate(_RS)]
    d_aq, d_ak, d_av = _rope_bwd(datt, cos_t, sin_t, s)
    dp = jnp.concatenate([d_gv, d_gr, d_ma, d_mb, d_gq, d_gk, d_aq, d_ak, d_av, d_glr,
                          jnp.zeros((s, P_W - P_LR - 128), BF16)], axis=1)
    g_win, [r123, oth45] = _mm(h, dp, "in_proj_dw", ta=True, out_dtype=F32, tm=1024, tn=1536, tk=2048,
                               comm=[_u_chip_exchange(ts123), _u_pair_join(half45)])
    half123 = [_chip_sum(t, r, chip1, "grad_chip_sum_" + BIG[i]) for t, r, i in zip(ts123, r123, (1, 2, 3))]
    gs0 = [_cols_split(_win_from_kernel(g_win))]
    d_h, [land0, oth123] = _mm(dp, w["win"], "in_proj_dx", tb=True, tk=1536,
                               comm=[_u_pair_send(gs0, (0,)), _u_pair_join(half123)])
    ts0 = [_pair_add(gs0[0], land0[0], core, "grad_pair_add_w_in")]
    [r0] = _comm_call("grad_exchange_w_in", [_u_chip_exchange(ts0)])
    half0 = [_chip_sum(ts0[0], r0[0], chip1, "grad_chip_sum_w_in")]
    [oth0] = _comm_call("grad_join_w_in", [_u_pair_join(half0)])

    def f_norm1_bwd(c, i, xv, dh, dxr, nw, sc):
        dxn, dsh, dsc, dnw = _norm_bwd(xv, dh.astype(F32), nw, sc)
        return (dxr + dxn, dsh, dsc, dnw)

    grad_x, d_shift1, d_scale1, d_n1w = _rowcall(
        f_norm1_bwd, [_rows(x, rb), _rows(d_h, rb), _rows(dx1, rb), _full(sm["n1w"]), _full(scale1)],
        [_orow(s, D, F32, rb), _oacc(1, D), _oacc(1, D), _oacc(1, D)], n_rows=s, rb=rb, name="norm1_bwd")

    dmod = jnp.concatenate([d_shift1, d_scale1, d_gate1, d_shift2, d_scale2, d_gate2], axis=1)
    small = dict(dmod=dmod, n1w=d_n1w, gb=d_gb, gnw=d_gnw, n2w=d_n2w, cb=d_cb, fnw=d_fnw, w2=d_w2, cw=d_cw)
    return loss, grad_x, half0 + half123 + half45, oth0 + oth123 + oth45, small


def _win_to_kernel(wfull):
    return jnp.concatenate([wfull[:, 1024:3072], wfull[:, 5392:W_IN], wfull[:, 0:1024], wfull[:, 3088:5392],
                            wfull[:, 3072:3088], jnp.zeros((D, P_W - W_IN), wfull.dtype)], axis=1)


def _win_from_kernel(g):
    return jnp.concatenate([g[:, P_GQ:P_AQ], g[:, P_GV:P_MA], g[:, P_LR:P_LR + GLA_LR], g[:, P_AQ:P_LR], g[:, P_MA:P_GQ]], axis=1)


def _ff_to_kernel(a):
    h = W_UP_SH
    return jnp.concatenate([a[:, 0:h], a[:, D_FF:D_FF + h], a[:, h:D_FF], a[:, D_FF + h:]], axis=1)


def _ff_from_kernel(a):
    h = W_UP_SH
    return jnp.concatenate([a[:, 0:h], a[:, 2 * h:3 * h], a[:, h:2 * h], a[:, 3 * h:]], axis=1)


BIG = ("w_in", "w_gla_branch", "w_attn_branch", "w_out", "w_up", "w_down")
SH_SHAPES = ((1024, W_IN_SH), (256, 1024), (256, 256), (256, 1024), (1024, W_UP_SH), (W_DOWN_SH, 1024))
N_BIG = len(BIG)


def _cols_join(t):
    return jnp.concatenate([t[k] for k in range(4)], axis=1)


def _cols_split(t):
    cols = t.shape[1] // 4
    return jnp.stack([t[:, k * cols:(k + 1) * cols] for k in range(4)])


def _me():
    return lax.axis_index("x"), lax.axis_index("y"), lax.axis_index("c")


HBM = pl.BlockSpec(memory_space=pltpu.HBM)
VMEM_SPEC = pl.BlockSpec(memory_space=pltpu.VMEM)


def _allgather8(xs, name):
    rows = xs.shape[0]

    def body(x_ref, out_ref, send_sems, recv_sems, local_sem):
        x, y, c = _me()
        me = 4 * x + 2 * y + c
        mine = pltpu.make_async_copy(x_ref, out_ref.at[me], local_sem)
        mine.start()
        flips = [(k >> 2 & 1, k >> 1 & 1, k & 1) for k in range(1, 8)]

        def peer(f):
            return (jnp.where(f[0] == 1, 1 - x, x), jnp.where(f[1] == 1, 1 - y, y), jnp.where(f[2] == 1, 1 - c, c))

        sends = []
        for k, f in enumerate(flips):
            cp = pltpu.make_async_remote_copy(src_ref=x_ref, dst_ref=out_ref.at[me], send_sem=send_sems.at[k],
                                              recv_sem=recv_sems.at[k], device_id=peer(f), device_id_type=MESH)
            cp.start()
            sends.append(cp)
        for k, f in enumerate(flips):
            px, py, pc = peer(f)
            pltpu.make_async_remote_copy(src_ref=x_ref, dst_ref=out_ref.at[4 * px + 2 * py + pc], send_sem=send_sems.at[k],
                                         recv_sem=recv_sems.at[k], device_id=peer(f), device_id_type=MESH).wait_recv()
        for cp in sends:
            cp.wait_send()
        mine.wait()

    return pl.pallas_call(
        body, name=name, out_shape=jax.ShapeDtypeStruct((8, rows, 128), F32),
        in_specs=[VMEM_SPEC], out_specs=VMEM_SPEC,
        scratch_shapes=[pltpu.SemaphoreType.DMA((7,)), pltpu.SemaphoreType.DMA((7,)), pltpu.SemaphoreType.DMA],
        compiler_params=pltpu.CompilerParams(vmem_limit_bytes=VMEM_LIMIT),
    )(xs)


def _half_rows(i, cc, unit):
    rows = SH_SHAPES[i][0] // 2
    return pl.ds(pl.multiple_of(cc * rows, unit), rows)


def _rc(src, dst, sems, to):
    return pltpu.make_async_remote_copy(src_ref=src, dst_ref=dst, send_sem=sems[0], recv_sem=sems[1], device_id=to, device_id_type=MESH)


def _other_chips(x, y):
    return [(1 - x, y), (x, 1 - y), (1 - x, 1 - y)]


def _u_gather_ici(w_sh, idxs):
    def copies(ins, outs, sem):
        x, y, c = _me()
        res = []
        for j, (px, py) in enumerate(_other_chips(x, y)):
            for n, i in enumerate(idxs):
                src = ins[n].at[0, _half_rows(i, c, 16)]
                res.append((_rc(src, outs[n].at[2 * x + y, _half_rows(i, c, 16)], sem(j * len(idxs) + n), (px, py, c)),
                            _rc(src, outs[n].at[2 * px + py, _half_rows(i, c, 16)], sem(j * len(idxs) + n), (px, py, c))))
        return res

    return dict(ins=[w_sh[i] for i in idxs], outs=[jax.ShapeDtypeStruct((4,) + SH_SHAPES[i], BF16) for i in idxs],
                nsem=3 * len(idxs), alias={}, copies=copies)


def _u_gather_d2d(got, idxs):
    def copies(ins, outs, sem):
        x, y, c = _me()
        res = []
        for j, (px, py) in enumerate(_other_chips(x, y)):
            for n, i in enumerate(idxs):
                src = ins[n].at[2 * px + py, _half_rows(i, c, 16)]
                res.append((_rc(src, outs[n].at[2 * px + py, _half_rows(i, c, 16)], sem(j * len(idxs) + n), (x, y, 1 - c)),
                            _rc(src, outs[n].at[2 * px + py, _half_rows(i, 1 - c, 16)], sem(j * len(idxs) + n), (x, y, 1 - c))))
        return res

    return dict(ins=list(got), outs=[jax.ShapeDtypeStruct(g.shape, g.dtype) for g in got], nsem=3 * len(idxs),
                alias={n: n for n in range(len(idxs))}, copies=copies)


def _u_pair_send(gs, idxs):
    def copies(ins, outs, sem):
        x, y, c = _me()
        res = []
        for n, i in enumerate(idxs):
            for sh in range(4):
                cp = _rc(ins[n].at[sh, _half_rows(i, 1 - c, 8)], outs[n].at[sh], sem(4 * n + sh), (x, y, 1 - c))
                res.append((cp, cp))
        return res

    return dict(ins=list(gs), outs=[jax.ShapeDtypeStruct((4, SH_SHAPES[i][0] // 2, SH_SHAPES[i][1]), F32) for i in idxs],
                nsem=4 * len(idxs), alias={}, copies=copies)


def _u_chip_exchange(ts):
    def copies(ins, outs, sem):
        x, y, c = _me()
        res = []
        for j, (px, py) in enumerate(_other_chips(x, y)):
            for n in range(len(ts)):
                cp = _rc(ins[n].at[2 * px + py], outs[n].at[j], sem(j * len(ts) + n), (px, py, c))
                res.append((cp, cp))
        return res

    return dict(ins=list(ts), outs=[jax.ShapeDtypeStruct((3,) + t.shape[1:], t.dtype) for t in ts], nsem=3 * len(ts),
                alias={}, copies=copies)


def _u_pair_join(hs):
    def copies(ins, outs, sem):
        x, y, c = _me()
        res = []
        for n in range(len(hs)):
            cp = _rc(ins[n], outs[n], sem(n), (x, y, 1 - c))
            res.append((cp, cp))
        return res

    return dict(ins=list(hs), outs=[jax.ShapeDtypeStruct(h.shape, h.dtype) for h in hs], nsem=len(hs), alias={}, copies=copies)


def _comm_phase(units, ci, co, send_sems, recv_sems, start):
    ii = oo = off = 0
    for u in units:
        ni, no = len(u["ins"]), len(u["outs"])
        for st, arrival in u["copies"](ci[ii:ii + ni], co[oo:oo + no], lambda k, off=off: (send_sems.at[off + k], recv_sems.at[off + k])):
            if start:
                st.start()
            else:
                st.wait_send()
                arrival.wait_recv()
        ii, oo, off = ii + ni, oo + no, off + u["nsem"]


def _carry(units, n_in, n_out):
    ins = [a for u in units for a in u["ins"]]
    outs = [o for u in units for o in u["outs"]]
    alias, ii, oo = {}, 0, 0
    for u in units:
        for a, b in u["alias"].items():
            alias[n_in + ii + a] = n_out + oo + b
        ii, oo = ii + len(u["ins"]), oo + len(u["outs"])
    nsem = sum(u["nsem"] for u in units)
    scratch = [pltpu.SemaphoreType.DMA((nsem,)), pltpu.SemaphoreType.DMA((nsem,))] if units else []
    return ins, outs, alias, scratch


def _split_units(units, res):
    out, oo = [], 0
    for u in units:
        out.append(list(res[oo:oo + len(u["outs"])]))
        oo += len(u["outs"])
    return out


def _comm_call(name, units):
    ins, outs, alias, scratch = _carry(units, 0, 0)

    def body(*refs):
        ci, co = refs[:len(ins)], refs[len(ins):len(ins) + len(outs)]
        _comm_phase(units, ci, co, refs[-2], refs[-1], True)
        _comm_phase(units, ci, co, refs[-2], refs[-1], False)

    res = pl.pallas_call(body, name=name, out_shape=outs, in_specs=[HBM] * len(ins), out_specs=[HBM] * len(outs),
                         scratch_shapes=scratch, input_output_aliases=alias)(*ins)
    return _split_units(units, res)


def _pair_add(g, land, core, name):
    _, rows, cols = g.shape
    half = rows // 2
    rb = _tile(half, 256, 16)
    nb = half // rb

    def body(c_ref, g_ref, l_ref, o_ref):
        o_ref[...] = (g_ref[...] + l_ref[...]).astype(BF16)

    return pl.pallas_call(
        body, name=name,
        grid_spec=pltpu.PrefetchScalarGridSpec(
            num_scalar_prefetch=1, grid=(4, nb),
            in_specs=[pl.BlockSpec((1, rb, cols), lambda s, i, c_ref: (s, c_ref[0] * nb + i, 0)),
                      pl.BlockSpec((1, rb, cols), lambda s, i, c_ref: (s, i, 0))],
            out_specs=pl.BlockSpec((1, rb, cols), lambda s, i, c_ref: (s, i, 0))),
        out_shape=jax.ShapeDtypeStruct((4, half, cols), BF16),
        compiler_params=_params(("parallel", "parallel")),
    )(core, g, land)


def _chip_sum(t, r, chip, name):
    _, half, cols = t.shape
    rb = _tile(half, 256, 16)

    def body(s_ref, t_ref, r_ref, o_ref):
        o_ref[...] = ((t_ref[0].astype(F32) + r_ref[0].astype(F32)) + r_ref[1].astype(F32)) + r_ref[2].astype(F32)

    return pl.pallas_call(
        body, name=name,
        grid_spec=pltpu.PrefetchScalarGridSpec(
            num_scalar_prefetch=1, grid=(half // rb,),
            in_specs=[pl.BlockSpec((1, rb, cols), lambda i, s_ref: (s_ref[0], i, 0)),
                      pl.BlockSpec((3, rb, cols), lambda i, s_ref: (0, i, 0))],
            out_specs=pl.BlockSpec((rb, cols), lambda i, s_ref: (i, 0))),
        out_shape=jax.ShapeDtypeStruct((half, cols), F32),
        compiler_params=_params(("parallel",)),
    )(chip, t, r)


def _adam_math(wv, gv, mv, vv):
    mn = ADAM_B1 * mv + (1.0 - ADAM_B1) * gv
    vn = ADAM_B2 * vv + (1.0 - ADAM_B2) * (gv * gv)
    m_hat = mn / (1.0 - ADAM_B1 ** ADAM_STEP)
    v_hat = vn / (1.0 - ADAM_B2 ** ADAM_STEP)
    return -ADAM_LR * (m_hat / (jnp.sqrt(v_hat) + ADAM_EPS) + ADAM_WD * wv), mn, vn


def _adamw_halves(wt, mt, vt, mine, theirs, core, name):
    _, rows, cols = wt.shape
    half = rows // 2
    rb = _tile(half, 256, 8)
    nb = half // rb

    def body(c_ref, w_ref, m_ref, v_ref, a_ref, b_ref, g_ref, d_ref, mo_ref, vo_ref):
        gv = jnp.where(pl.program_id(0) == c_ref[0], a_ref[...], b_ref[...])
        dl, mn, vn = _adam_math(w_ref[...], gv, m_ref[...], v_ref[...])
        g_ref[...] = gv
        d_ref[...] = dl
        mo_ref[...] = mn
        vo_ref[...] = vn

    full = pl.BlockSpec((None, rb, cols), lambda hf, i, c_ref: (0, hf * nb + i, 0))
    part = pl.BlockSpec((rb, cols), lambda hf, i, c_ref: (i, 0))
    return pl.pallas_call(
        body, name=name,
        grid_spec=pltpu.PrefetchScalarGridSpec(num_scalar_prefetch=1, grid=(2, nb), in_specs=[full, full, full, part, part],
                                               out_specs=[full] * 4),
        out_shape=[jax.ShapeDtypeStruct((1, rows, cols), F32)] * 4,
        compiler_params=_params(("parallel", "parallel")),
    )(core, wt, mt, vt, mine, theirs)


SG_REP = 136
SG_W2, SG_CW = SG_REP, SG_REP + 4 * 16
SG_ROWS = SG_CW + 4 * 40
SP_ROWS = SG_REP + 16 + 40


def _mod_shard(c_all, ada_w_sh):
    def body(c_ref, w_ref, o_ref):
        cv = c_ref[...]
        o_ref[...] = _dg((cv * _sigmoid(cv)).astype(BF16), w_ref[...].astype(BF16), 1, 0)

    return pl.pallas_call(body, name="mod_shard", out_shape=jax.ShapeDtypeStruct((8, 1536), F32),
                          in_specs=[VMEM_SPEC, VMEM_SPEC], out_specs=VMEM_SPEC,
                          compiler_params=pltpu.CompilerParams(vmem_limit_bytes=VMEM_LIMIT))(c_all, ada_w_sh)


def _mod_select(mod_all, ada_b4):
    def body(m_ref, b_ref, o_ref):
        x, y, c = _me()
        me = 4 * x + 2 * y + c
        for sh in range(4):
            o_ref[sh] = m_ref[2 * sh, me] + b_ref[sh]

    return pl.pallas_call(body, name="mod_select", out_shape=jax.ShapeDtypeStruct((4, 12, 128), F32),
                          in_specs=[VMEM_SPEC, VMEM_SPEC], out_specs=VMEM_SPEC)(mod_all, ada_b4)


def _small_reduce(sg_all):
    def body(g_ref, o_ref):
        x, y, c = _me()
        s_me = 2 * x + y
        w2_rows = pl.ds(pl.multiple_of(SG_W2 + 16 * s_me, 8), 16)
        cw_rows = pl.ds(pl.multiple_of(SG_CW + 40 * s_me, 8), 40)
        a = g_ref[0, 0:SG_REP, :]
        b = g_ref[0, w2_rows, :]
        d = g_ref[0, cw_rows, :]
        for dev in range(1, 8):
            a = a + g_ref[dev, 0:SG_REP, :]
            b = b + g_ref[dev, w2_rows, :]
            d = d + g_ref[dev, cw_rows, :]
        o_ref[0:SG_REP, :] = a
        o_ref[SG_REP:SG_REP + 16, :] = b
        o_ref[SG_REP + 16:SP_ROWS, :] = d

    return pl.pallas_call(body, name="small_grad_reduce", out_shape=jax.ShapeDtypeStruct((SP_ROWS, 128), F32),
                          in_specs=[VMEM_SPEC], out_specs=VMEM_SPEC)(sg_all)


def _ada_grad(dmod_all, c_bc):
    def body(g_ref, c_ref, o_ref):
        x, y, c = _me()
        s_me = 2 * x + y
        for k in range(12):
            acc = jnp.zeros((D, 128), F32)
            for b in range(8):
                cv = c_ref[b]
                acc = acc + (cv * _sigmoid(cv)) * g_ref[s_me, k, b:b + 1, :]
            o_ref[:, k * 128:(k + 1) * 128] = acc

    return pl.pallas_call(body, name="ada_w_grad", out_shape=jax.ShapeDtypeStruct((D, 1536), F32),
                          in_specs=[VMEM_SPEC, VMEM_SPEC], out_specs=VMEM_SPEC,
                          compiler_params=pltpu.CompilerParams(vmem_limit_bytes=VMEM_LIMIT))(dmod_all, c_bc)


def _adamw(wt, g, m, v, name):
    rows, cols = wt.shape
    rb = _tile(rows, 256, 8)

    def fn(c, i, wv, gv, mv, vv):
        return _adam_math(wv, gv, mv, vv)

    return _rowcall(fn, [_rows(t, rb) for t in (wt, g, m, v)], [_orow(rows, cols, F32, rb)] * 3,
                    n_rows=rows, rb=rb, name=name)


def _pad_rows(t, rows):
    flat = t.reshape(-1)
    return jnp.pad(flat, (0, rows * 128 - flat.shape[0])).reshape(rows, 128)


SP_LAYOUT = (("ada_b", 48), ("norm1_w", 8), ("gla_gate_b", 8), ("gla_norm_w", 8), ("norm2_w", 8), ("conv_b", 48),
             ("final_norm_w", 8), ("gla_gate_w2", 16), ("conv_w", 40))


def _pack_small(d):
    return jnp.concatenate([_pad_rows(d[n].astype(F32), rows) for n, rows in SP_LAYOUT], axis=0)


def _unpack_small(pk, shapes):
    out, off = {}, 0
    for n, rows in SP_LAYOUT:
        shp = shapes[n]
        out[n] = pk[off:off + rows].reshape(-1)[:math.prod(shp)].reshape(shp)
        off += rows
    return out


def kernel(x, c, positions, ada_w, ada_b, norm1_w, w_in, gla_gate_w2, gla_gate_b, gla_norm_w, w_gla_branch, w_attn_branch, w_out, norm2_w, w_up, conv_w, conv_b, w_down, final_norm_w, loss_target, m_ada_w, m_ada_b, m_norm1_w, m_w_in, m_gla_gate_w2, m_gla_gate_b, m_gla_norm_w, m_w_gla_branch, m_w_attn_branch, m_w_out, m_norm2_w, m_w_up, m_conv_w, m_conv_b, m_w_down, m_final_norm_w, v_ada_w, v_ada_b, v_norm1_w, v_w_in, v_gla_gate_w2, v_gla_gate_b, v_gla_norm_w, v_w_gla_branch, v_w_attn_branch, v_w_out, v_norm2_w, v_w_up, v_conv_w, v_conv_b, v_w_down, v_final_norm_w):
    s = x.shape[1]
    names = ("ada_w", "ada_b", "norm1_w", "w_in", "gla_gate_w2", "gla_gate_b", "gla_norm_w", "w_gla_branch", "w_attn_branch",
             "w_out", "norm2_w", "w_up", "conv_w", "conv_b", "w_down", "final_norm_w")
    wts = dict(zip(names, (ada_w, ada_b, norm1_w, w_in, gla_gate_w2, gla_gate_b, gla_norm_w, w_gla_branch, w_attn_branch,
                           w_out, norm2_w, w_up, conv_w, conv_b, w_down, final_norm_w)))
    ms = dict(zip(names, (m_ada_w, m_ada_b, m_norm1_w, m_w_in, m_gla_gate_w2, m_gla_gate_b, m_gla_norm_w, m_w_gla_branch,
                          m_w_attn_branch, m_w_out, m_norm2_w, m_w_up, m_conv_w, m_conv_b, m_w_down, m_final_norm_w)))
    vs = dict(zip(names, (v_ada_w, v_ada_b, v_norm1_w, v_w_in, v_gla_gate_w2, v_gla_gate_b, v_gla_norm_w, v_w_gla_branch,
                          v_w_attn_branch, v_w_out, v_norm2_w, v_w_up, v_conv_w, v_conv_b, v_w_down, v_final_norm_w)))

    pk0 = jnp.concatenate([_pad_rows(c, 8), _pad_rows(gla_gate_w2, 16), _pad_rows(conv_w, 40)], axis=0)
    sm_all = _allgather8(pk0, "gather_small")
    c_all = sm_all[:, 0:8, :].reshape(8, D)
    w2_full = sm_all[0::2, 8:24, :].transpose(1, 0, 2).reshape(GLA_LR, 512)
    cw_full = sm_all[0::2, 24:64, :].reshape(4, 40 * 128)[:, :3 * W_UP_SH].reshape(4, 3, W_UP_SH).transpose(1, 0, 2).reshape(3, 2 * D_FF)

    mod_sh = _mod_shard(c_all, ada_w[0])
    mod_all = _allgather8(mod_sh.reshape(96, 128), "gather_mod")
    mod = _mod_select(mod_all.reshape(8, 8, 12, 128), ada_b.reshape(4, 12, 128)).reshape(6, D)

    core = lax.axis_index("c").astype(jnp.int32).reshape(1)
    chip = (2 * lax.axis_index("x") + lax.axis_index("y")).astype(jnp.int32)
    w_sh = [wts[n].astype(BF16) for n in BIG]
    sm = dict(n1w=norm1_w, n2w=norm2_w, fnw=final_norm_w.reshape(1, D), gnw=gla_norm_w, gb=gla_gate_b,
              w2=jnp.pad(w2_full, ((0, 128 - GLA_LR), (0, 0))), cw=_ff_to_kernel(cw_full), cb=_ff_to_kernel(conv_b))
    loss, grad_x, halves, others, small = _local_step(x[0], mod, positions.reshape(s, 1), loss_target[0], sm, w_sh, chip, core)

    dcw = _ff_from_kernel(small["cw"]).reshape(3, 4, W_UP_SH).transpose(1, 0, 2)
    dw2 = small["w2"][:GLA_LR].reshape(GLA_LR, 4, 128).transpose(1, 0, 2)
    sg = jnp.concatenate(
        [_pad_rows(small["dmod"], 48), _pad_rows(small["n1w"], 8), _pad_rows(small["gb"], 8), _pad_rows(small["gnw"], 8),
         _pad_rows(small["n2w"], 8), _pad_rows(_ff_from_kernel(small["cb"]), 48), _pad_rows(small["fnw"], 8)]
        + [_pad_rows(dw2[k], 16) for k in range(4)] + [_pad_rows(dcw[k], 40) for k in range(4)], axis=0)
    sg_all = _allgather8(sg, "gather_small_grads")
    g_small_pk = _small_reduce(sg_all)
    dmod_all = sg_all[:, 0:48, :].reshape(8, 4, 12, 128).transpose(1, 2, 0, 3)
    g_ada_w = _ada_grad(dmod_all, jnp.broadcast_to(c_all[:, :, None], (8, D, 128)))

    shapes = {n: wts[n].shape for n in names}
    g_small = _unpack_small(g_small_pk, shapes)
    grads = {"ada_w": g_ada_w.reshape(1, D, 1536), **g_small}
    deltas, new_m, new_v = {}, {}, {}
    for n, mine, theirs in zip(BIG, halves, others):
        grads[n], deltas[n], new_m[n], new_v[n] = _adamw_halves(wts[n], ms[n], vs[n], mine, theirs, core, "adamw_" + n)
    shp = ada_w.shape
    d_, m_, v_ = _adamw(ada_w[0], g_ada_w, m_ada_w[0], v_ada_w[0], "adamw_ada_w")
    deltas["ada_w"], new_m["ada_w"], new_v["ada_w"] = d_.reshape(shp), m_.reshape(shp), v_.reshape(shp)
    d_, m_, v_ = _adamw(_pack_small(wts), g_small_pk, _pack_small(ms), _pack_small(vs), "adamw_small")
    for dst, pk in ((deltas, d_), (new_m, m_), (new_v, v_)):
        dst.update(_unpack_small(pk, shapes))

    loss_all = lax.psum(loss[0, 0], ("x", "y", "c"))
    return (loss_all, grad_x.reshape(1, s, D), *[grads[n] for n in names], *[deltas[n] for n in names],
            *[new_m[n] for n in names], *[new_v[n] for n in names])
```

```python
import math

import jax
import jax.numpy as jnp
from jax import lax
from jax.experimental import pallas as pl
from jax.experimental.pallas import tpu as pltpu

F32, BF16 = jnp.float32, jnp.bfloat16
MESH = pl.DeviceIdType.MESH

D = 1024
EPS = 1e-6
GLA_H, GLA_DK, GLA_DV, GLA_LR = 4, 128, 256, 16
GLA_TAU = 16.0
GLA_CHUNK = 64
GLA_BLOCK = 512
ATT_GROUPS = ((128, 1), (512, 4), (2048, 16))
ATT_BLK = 128
ATT_HD = 64
ATT_W = 768
D_FF = 2816
ROPE_THETA = 10000.0
P_W = 7680
P_GV, P_GR, P_MA, P_MB, P_GQ, P_GK, P_AQ, P_AK, P_AV, P_LR = 0, 1024, 2048, 3072, 4096, 4608, 5120, 5888, 6656, 7424
W_IN = 7440
W_IN_SH, W_UP_SH, W_DOWN_SH = 1860, 1408, 704
VMEM_LIMIT = 56 * 1024 * 1024
ADAM_LR, ADAM_B1, ADAM_B2, ADAM_EPS, ADAM_WD, ADAM_STEP = 0.001, 0.9, 0.999, 1e-08, 0.01, 10
NEG = -1e30


def _tile(n, target, unit=128):
    best = None
    for t in range(unit, min(n, target) + 1, unit):
        if n % t == 0:
            best = t
    return best or n


def _params(sem):
    return pltpu.CompilerParams(dimension_semantics=sem, vmem_limit_bytes=VMEM_LIMIT)


def _dg(a, b, ca, cb):
    return lax.dot_general(a, b, (((ca,), (cb,)), ((), ())), preferred_element_type=F32)


def _sigmoid(v):
    return 1.0 / (1.0 + jnp.exp(-v))


def _ff_block(j):
    return (j % 2) * 2 + j // 2


def _mm(a, b, name, *, ta=False, tb=False, out_dtype=BF16, tm=1024, tn=1536, tk=1024, n_outer=True, comm=(),
        b_shards=False, o_shards=False):
    m = a.shape[1] if ta else a.shape[0]
    k = a.shape[0] if ta else a.shape[1]
    if b_shards:
        n = b.shape[1] if tb else 4 * W_UP_SH
        tn, tk = (tn, W_UP_SH) if tb else (W_UP_SH, tk)
    else:
        n = b.shape[0] if tb else b.shape[1]
    if o_shards:
        tn = W_UP_SH
    tm, tn, tk = _tile(m, tm), _tile(n, tn), _tile(k, tk)
    nm, nn, nk = m // tm, n // tn, k // tk
    in_out = out_dtype == F32
    c_ins, c_outs, c_alias, c_scratch = _carry(comm, 2, 1)

    def body(a_ref, b_ref, *rest):
        ci, o_ref, co = rest[:len(c_ins)], rest[len(c_ins)], rest[len(c_ins) + 1:len(c_ins) + 1 + len(c_outs)]
        scr = rest[len(c_ins) + 1 + len(c_outs):]
        kk = pl.program_id(2)
        if comm:
            step = (pl.program_id(0) * (nm if n_outer else nn) + pl.program_id(1)) * nk + kk

            @pl.when(step == 0)
            def _():
                _comm_phase(comm, ci, co, scr[-2], scr[-1], True)

        _mm_step(a_ref, b_ref, o_ref, scr, kk)
        if comm:
            @pl.when(step == nm * nn * nk - 1)
            def _():
                _comm_phase(comm, ci, co, scr[-2], scr[-1], False)

    def _mm_step(a_ref, b_ref, o_ref, scr, kk):
        p = _dg(a_ref[...].astype(BF16), b_ref[...].astype(BF16), 0 if ta else 1, 1 if tb else 0)
        if nk == 1:
            o_ref[...] = p.astype(o_ref.dtype)
        else:
            acc = o_ref if in_out else scr[0]

            @pl.when(kk == 0)
            def _():
                acc[...] = p

            @pl.when(kk > 0)
            def _():
                acc[...] += p

            if not in_out:
                @pl.when(kk == nk - 1)
                def _():
                    o_ref[...] = acc[...].astype(o_ref.dtype)

    if n_outer:
        ij = lambda g0, g1: (g1, g0)
        grid = (nn, nm, nk)
    else:
        ij = lambda g0, g1: (g0, g1)
        grid = (nm, nn, nk)
    a_map = (lambda g0, g1, kk: (kk, ij(g0, g1)[0])) if ta else (lambda g0, g1, kk: (ij(g0, g1)[0], kk))
    if b_shards and tb:
        b_spec = pl.BlockSpec((None, tn, tk), lambda g0, g1, kk: (_ff_block(kk), ij(g0, g1)[1], 0))
    elif b_shards:
        b_spec = pl.BlockSpec((None, tk, tn), lambda g0, g1, kk: (_ff_block(ij(g0, g1)[1]), kk, 0))
    elif tb:
        b_spec = pl.BlockSpec((tn, tk), lambda g0, g1, kk: (ij(g0, g1)[1], kk))
    else:
        b_spec = pl.BlockSpec((tk, tn), lambda g0, g1, kk: (kk, ij(g0, g1)[1]))
    if o_shards:
        o_spec = pl.BlockSpec((None, tm, tn), lambda g0, g1, kk: (_ff_block(ij(g0, g1)[1]), ij(g0, g1)[0], 0))
        o_shape = jax.ShapeDtypeStruct((4, m, W_UP_SH), out_dtype)
    else:
        o_spec = pl.BlockSpec((tm, tn), lambda g0, g1, kk: ij(g0, g1))
        o_shape = jax.ShapeDtypeStruct((m, n), out_dtype)
    res = pl.pallas_call(
        body, name=name, grid=grid,
        in_specs=[pl.BlockSpec((tk, tm) if ta else (tm, tk), a_map), b_spec] + [HBM] * len(c_ins),
        out_specs=[o_spec] + [HBM] * len(c_outs),
        out_shape=[o_shape] + c_outs,
        scratch_shapes=([] if (in_out or nk == 1) else [pltpu.VMEM((tm, tn), F32)]) + c_scratch,
        input_output_aliases=c_alias,
        compiler_params=_params(("arbitrary",) * 3 if comm else ("parallel", "parallel", "arbitrary")),
    )(a, b, *c_ins)
    return (res[0], _split_units(comm, res[1:])) if comm else res[0]


def _rows(arr, rb, w=None, j=0):
    w = arr.shape[1] if w is None else w
    if callable(j):
        return arr, pl.BlockSpec((rb, w), lambda c, i: (i, j(c)))
    return arr, pl.BlockSpec((rb, w), lambda c, i: (i, j))


def _full(arr, w=None, j=0):
    w = arr.shape[1] if w is None else w
    if callable(j):
        return arr, pl.BlockSpec((arr.shape[0], w), lambda c, i: (0, j(c)))
    return arr, pl.BlockSpec((arr.shape[0], w), lambda c, i: (0, j))


def _halo(arr, rb, hb, w, j, before):
    per = rb // hb
    last = arr.shape[0] // hb - 1
    if before:
        rmap = lambda i: jnp.maximum(i * per - 1, 0)
    else:
        rmap = lambda i: jnp.minimum((i + 1) * per, last)
    return arr, pl.BlockSpec((hb, w), lambda c, i: (rmap(i), j(c) if callable(j) else j))


def _rowcall(fn, ins, outs, *, n_rows, rb, name, ncol=1):
    n_in = len(ins)
    nr = n_rows // rb

    def body(*refs):
        c, i = pl.program_id(0), pl.program_id(1)
        res = fn(c, i, *[r[...] for r in refs[:n_in]])
        for val, spec, o_ref in zip(res, outs, refs[n_in:]):
            if spec[2] == "row":
                o_ref[...] = val.astype(o_ref.dtype)
            else:
                @pl.when(i == 0)
                def _(o_ref=o_ref, val=val):
                    o_ref[...] = val.astype(o_ref.dtype)

                @pl.when(i > 0)
                def _(o_ref=o_ref, val=val):
                    o_ref[...] += val.astype(o_ref.dtype)

    out_specs = []
    for shape, dt, kind, block, col in outs:
        if kind == "row":
            out_specs.append(pl.BlockSpec(block, lambda c, i, col=col: (i, col(c))))
        else:
            out_specs.append(pl.BlockSpec(block, lambda c, i, col=col: (0, col(c))))
    return pl.pallas_call(
        body, name=name, grid=(ncol, nr),
        in_specs=[s for _, s in ins], out_specs=out_specs,
        out_shape=[jax.ShapeDtypeStruct(o[0], o[1]) for o in outs],
        compiler_params=_params(("parallel", "arbitrary")),
    )(*[a for a, _ in ins])


def _orow(n_rows, w, dt, rb, bw=None, col=lambda c: 0):
    return ((n_rows, w), dt, "row", (rb, bw or w), col)


def _oacc(r, w, bw=None, col=lambda c: 0):
    return ((r, w), F32, "acc", (r, bw or w), col)


def _csum(v):
    return jnp.sum(v, axis=0, keepdims=True)


def _rms(v):
    return lax.rsqrt(jnp.mean(v * v, axis=-1, keepdims=True) + EPS)


def _norm_bwd(xv, dh, w, scale):
    r = _rms(xv)
    xh = xv * r
    dxh = dh * (w * (1.0 + scale))
    dx = r * (dxh - xh * jnp.mean(dxh * xh, axis=-1, keepdims=True))
    t = dh * xh
    return dx, _csum(dh), _csum(t * w), _csum(t * (1.0 + scale))


def _rope_tables(pos_col, invf, s):
    def fn(c, i, pos, f):
        ang = pos.astype(F32) * f
        lane = lax.broadcasted_iota(jnp.int32, ang.shape, 1)
        sign = jnp.where((lane % ATT_HD) < ATT_HD // 2, -1.0, 1.0)
        return jnp.cos(ang), jnp.sin(ang) * sign

    rb = 512
    return _rowcall(fn, [_rows(pos_col, rb), _full(invf)], [_orow(s, 128, F32, rb), _orow(s, 128, F32, rb)],
                    n_rows=s, rb=rb, name="rope_tables")


def _swap_halves(t):
    n = t.shape[1]
    lane = lax.broadcasted_iota(jnp.int32, t.shape, 1)
    return jnp.where((lane % ATT_HD) < ATT_HD // 2, pltpu.roll(t, n - 32, 1), pltpu.roll(t, 32, 1))


def _rope_apply(t, cos, sin_signed, inverse):
    cw = jnp.concatenate([cos] * (t.shape[1] // 128), axis=1)
    sw = jnp.concatenate([sin_signed] * (t.shape[1] // 128), axis=1)
    if inverse:
        sw = -sw
    return t * cw + _swap_halves(t) * sw


DIL_ROWS = 512


def _to_dilated(scr, val, out_ref, r):
    if r == 1:
        out_ref[...] = val.astype(out_ref.dtype)
        return
    n = val.shape[0] // r
    for hh in range(2):
        scr[hh] = val[:, hh * 128:(hh + 1) * 128]
        for pr in range(r):
            out_ref[:, pr * 256 + hh * 128:pr * 256 + (hh + 1) * 128] = scr[hh, pl.ds(pr, n, stride=r), :].astype(out_ref.dtype)


def _from_dilated(scr, in_ref, r):
    if r == 1:
        return in_ref[...].astype(F32)
    n = in_ref.shape[0]
    for hh in range(2):
        for pr in range(r):
            scr[hh, pl.ds(pr, n, stride=r), :] = in_ref[:, pr * 256 + hh * 128:pr * 256 + (hh + 1) * 128].astype(F32)
    return jnp.concatenate([scr[0], scr[1]], axis=1)


def _dil_spec(r):
    return pl.BlockSpec((DIL_ROWS // r, r * 256), lambda i: (i, 0))


def _dil_shape(s, r, dt):
    return jax.ShapeDtypeStruct((s // r, r * 256), dt)


_DIL_SCRATCH = [pltpu.VMEM((2, DIL_ROWS, 128), F32)]
_RS = tuple(r for _, r in ATT_GROUPS)


def _rope_fwd(p, cos_t, sin_t, s):
    def body(*refs):
        ins, cs, sn, outs, scr = refs[:9], refs[9][...], refs[10][...], refs[11:20], refs[20]
        for t in range(3):
            for g, r in enumerate(_RS):
                val = ins[3 * t + g][...].astype(F32)
                _to_dilated(scr, _rope_apply(val, cs, sn, False) if t < 2 else val, outs[3 * t + g], r)

    res = pl.pallas_call(
        body, name="rope", grid=(s // DIL_ROWS,),
        in_specs=[pl.BlockSpec((DIL_ROWS, 256), lambda i, c=base // 256 + g: (i, c)) for base in (P_AQ, P_AK, P_AV) for g in range(3)]
        + [pl.BlockSpec((DIL_ROWS, 128), lambda i: (i, 0))] * 2,
        out_specs=[_dil_spec(r) for _ in range(3) for r in _RS],
        out_shape=[_dil_shape(s, r, BF16) for _ in range(3) for r in _RS],
        scratch_shapes=_DIL_SCRATCH, compiler_params=_params(("parallel",)),
    )(*([p] * 9), cos_t, sin_t)
    return res[0:3], res[3:6], res[6:9]


def _attn_combine(att, s):
    def body(o0, o1, o2, l0, l1, l2, o_ref, lse_ref, od1, od2, ld1, ld2, scr):
        ov = [_from_dilated(scr, ref, r) for ref, r in zip((o0, o1, o2), _RS)]
        lv = [_from_dilated(scr, ref, r) for ref, r in zip((l0, l1, l2), _RS)]
        mx = jnp.maximum(jnp.maximum(lv[0], lv[1]), lv[2])
        ev = [jnp.exp(l - mx) for l in lv]
        z = ev[0] + ev[1] + ev[2]
        o = ((ev[0] * ov[0] + ev[1] * ov[1] + ev[2] * ov[2]) / z).astype(BF16)
        lse = mx + jnp.log(z)
        o_ref[...] = o
        lse_ref[...] = lse
        for ref, r in zip((od1, od2), _RS[1:]):
            _to_dilated(scr, o.astype(F32), ref, r)
        for ref, r in zip((ld1, ld2), _RS[1:]):
            _to_dilated(scr, lse, ref, r)

    return pl.pallas_call(
        body, name="attn_combine", grid=(s // DIL_ROWS,),
        in_specs=[_dil_spec(r) for r in _RS] * 2,
        out_specs=[_dil_spec(1)] * 2 + [_dil_spec(r) for r in _RS[1:]] * 2,
        out_shape=[_dil_shape(s, 1, BF16), _dil_shape(s, 1, F32)] + [_dil_shape(s, r, BF16) for r in _RS[1:]]
        + [_dil_shape(s, r, F32) for r in _RS[1:]],
        scratch_shapes=_DIL_SCRATCH, compiler_params=_params(("parallel",)),
    )(*[a[0] for a in att], *[a[1] for a in att])


def _dilate(t, s):
    def body(t_ref, o1, o2, scr):
        val = t_ref[...].astype(F32)
        for ref, r in zip((o1, o2), _RS[1:]):
            _to_dilated(scr, val, ref, r)

    return pl.pallas_call(
        body, name="attn_dilate", grid=(s // DIL_ROWS,), in_specs=[_dil_spec(1)], out_specs=[_dil_spec(r) for r in _RS[1:]],
        out_shape=[_dil_shape(s, r, t.dtype) for r in _RS[1:]], scratch_shapes=_DIL_SCRATCH, compiler_params=_params(("parallel",)),
    )(t)


def _rope_bwd(datt, cos_t, sin_t, s):
    def body(*refs):
        ins, cs, sn, outs, scr = refs[:9], refs[9][...], refs[10][...], refs[11:14], refs[14]
        for t in range(3):
            for g, r in enumerate(_RS):
                val = _from_dilated(scr, ins[3 * t + g], r)
                outs[t][:, g * 256:(g + 1) * 256] = (_rope_apply(val, cs, sn, True) if t < 2 else val).astype(BF16)

    return pl.pallas_call(
        body, name="rope_bwd", grid=(s // DIL_ROWS,),
        in_specs=[_dil_spec(r) for _ in range(3) for r in _RS] + [pl.BlockSpec((DIL_ROWS, 128), lambda i: (i, 0))] * 2,
        out_specs=[pl.BlockSpec((DIL_ROWS, ATT_W), lambda i: (i, 0))] * 3,
        out_shape=[jax.ShapeDtypeStruct((s, ATT_W), BF16)] * 3,
        scratch_shapes=_DIL_SCRATCH, compiler_params=_params(("parallel",)),
    )(*[datt[g][t] for t in range(3) for g in range(3)], cos_t, sin_t)


def _gla_decays(la_c, tri):
    b = jnp.dot(tri, la_c, precision=lax.Precision.HIGHEST, preferred_element_type=F32)
    row = lax.broadcasted_iota(jnp.int32, b.shape, 0)
    bmid = jnp.sum(jnp.where(row == GLA_CHUNK // 2 - 1, b, 0.0), axis=0, keepdims=True)
    blast = jnp.sum(jnp.where(row == GLA_CHUNK - 1, b, 0.0), axis=0, keepdims=True)
    return b, bmid, blast


def _gla_fwd(p, la, s, comm=()):
    tb, ch = GLA_BLOCK, GLA_CHUNK
    nb, nc = s // tb, tb // ch
    scale = GLA_DK ** -0.5
    c_ins, c_outs, c_alias, c_scratch = _carry(comm, 4, 2)

    def body(q_ref, k_ref, v_ref, la_ref, *rest):
        ci, (o_ref, st_ref) = rest[:len(c_ins)], rest[len(c_ins):len(c_ins) + 2]
        co, state = rest[len(c_ins) + 2:len(c_ins) + 2 + len(c_outs)], rest[len(c_ins) + 2 + len(c_outs)]
        step = pl.program_id(0) * nb + pl.program_id(1)
        if comm:
            @pl.when(step == 0)
            def _():
                _comm_phase(comm, ci, co, rest[-2], rest[-1], True)

        _gla_fwd_step(q_ref, k_ref, v_ref, la_ref, o_ref, st_ref, state)
        if comm:
            @pl.when(step == GLA_H * nb - 1)
            def _():
                _comm_phase(comm, ci, co, rest[-2], rest[-1], False)

    def _gla_fwd_step(q_ref, k_ref, v_ref, la_ref, o_ref, st_ref, state):
        @pl.when(pl.program_id(1) == 0)
        def _():
            state[...] = jnp.zeros_like(state)

        ri = lax.broadcasted_iota(jnp.int32, (ch, ch), 0)
        ci = lax.broadcasted_iota(jnp.int32, (ch, ch), 1)
        causal = ci <= ri
        tri = causal.astype(F32)
        for c in range(nc):
            sl = pl.ds(c * ch, ch)
            b, bmid, blast = _gla_decays(la_ref[sl, :], tri)
            q = q_ref[sl, :].astype(F32) * scale
            k = k_ref[sl, :].astype(F32)
            v = v_ref[sl, :]
            qgt = (q * jnp.exp(b)).astype(BF16)
            qgn = (q * jnp.exp(b - bmid)).astype(BF16)
            kgn = (k * jnp.exp(bmid - b)).astype(BF16)
            kd = (k * jnp.exp(blast - b)).astype(BF16)
            a = jnp.where(causal, _dg(qgn, kgn, 1, 1), 0.0)
            st = state[...]
            st_ref[0, c] = st
            o_ref[sl, :] = _dg(a.astype(BF16), v, 1, 0) + _dg(qgt, st.astype(BF16), 1, 1)
            state[...] = jnp.exp(blast) * st + _dg(v, kd, 0, 0)

    res = pl.pallas_call(
        body, name="gla_fwd", grid=(GLA_H, nb),
        in_specs=[pl.BlockSpec((tb, GLA_DK), lambda h, t: (t, P_GQ // GLA_DK + h)),
                  pl.BlockSpec((tb, GLA_DK), lambda h, t: (t, P_GK // GLA_DK + h)),
                  pl.BlockSpec((tb, GLA_DV), lambda h, t: (t, P_GV // GLA_DV + h)),
                  pl.BlockSpec((tb, GLA_DK), lambda h, t: (t, h))] + [HBM] * len(c_ins),
        out_specs=[pl.BlockSpec((tb, GLA_DV), lambda h, t: (t, h)),
                   pl.BlockSpec((1, nc, GLA_DV, GLA_DK), lambda h, t: (h, t, 0, 0))] + [HBM] * len(c_outs),
        out_shape=[jax.ShapeDtypeStruct((s, GLA_H * GLA_DV), F32),
                   jax.ShapeDtypeStruct((GLA_H, s // ch, GLA_DV, GLA_DK), F32)] + c_outs,
        scratch_shapes=[pltpu.VMEM((GLA_DV, GLA_DK), F32)] + c_scratch,
        input_output_aliases=c_alias,
        compiler_params=_params(("arbitrary", "arbitrary") if comm else ("parallel", "arbitrary")),
    )(p, p, p, la, *c_ins)
    return res[0], res[1], _split_units(comm, res[2:])


def _gla_bwd(p, la, states, do, s, comm=()):
    tb, ch = GLA_BLOCK, GLA_CHUNK
    nb, nc = s // tb, tb // ch
    scale = GLA_DK ** -0.5
    c_ins, c_outs, c_alias, c_scratch = _carry(comm, 6, 4)

    def body(q_ref, k_ref, v_ref, la_ref, st_ref, do_ref, *rest):
        ci, outs = rest[:len(c_ins)], rest[len(c_ins):len(c_ins) + 4]
        co, dstate = rest[len(c_ins) + 4:len(c_ins) + 4 + len(c_outs)], rest[len(c_ins) + 4 + len(c_outs)]
        step = pl.program_id(0) * nb + pl.program_id(1)
        if comm:
            @pl.when(step == 0)
            def _():
                _comm_phase(comm, ci, co, rest[-2], rest[-1], True)

        _gla_bwd_step(q_ref, k_ref, v_ref, la_ref, st_ref, do_ref, *outs, dstate)
        if comm:
            @pl.when(step == GLA_H * nb - 1)
            def _():
                _comm_phase(comm, ci, co, rest[-2], rest[-1], False)

    def _gla_bwd_step(q_ref, k_ref, v_ref, la_ref, st_ref, do_ref, dq_ref, dk_ref, dv_ref, dla_ref, dstate):
        @pl.when(pl.program_id(1) == 0)
        def _():
            dstate[...] = jnp.zeros_like(dstate)

        ri = lax.broadcasted_iota(jnp.int32, (ch, ch), 0)
        ci = lax.broadcasted_iota(jnp.int32, (ch, ch), 1)
        causal = ci <= ri
        tri = causal.astype(F32)
        tri_t = (ci >= ri).astype(F32)
        for c in reversed(range(nc)):
            sl = pl.ds(c * ch, ch)
            b, bmid, blast = _gla_decays(la_ref[sl, :], tri)
            q = q_ref[sl, :].astype(F32) * scale
            k = k_ref[sl, :].astype(F32)
            v = v_ref[sl, :]
            e_b, e_qn, e_kn, e_kd = jnp.exp(b), jnp.exp(b - bmid), jnp.exp(bmid - b), jnp.exp(blast - b)
            dec = jnp.exp(blast)
            qgt, qgn, kgn, kd = q * e_b, q * e_qn, k * e_kn, k * e_kd
            qgt_b, qgn_b, kgn_b, kd_b = qgt.astype(BF16), qgn.astype(BF16), kgn.astype(BF16), kd.astype(BF16)
            st0 = st_ref[0, c]
            dst = dstate[...]
            dst_b = dst.astype(BF16)
            do_b = do_ref[sl, :].astype(BF16)
            a = jnp.where(causal, _dg(qgn_b, kgn_b, 1, 1), 0.0).astype(BF16)
            da = jnp.where(causal, _dg(do_b, v, 1, 1), 0.0).astype(BF16)
            dqgn = _dg(da, kgn_b, 1, 0)
            dqgt = _dg(do_b, st0.astype(BF16), 1, 0)
            dkgn = _dg(da, qgn_b, 0, 0)
            dv = _dg(a, do_b, 0, 0) + _dg(kd_b, dst_b, 1, 1)
            dkd = _dg(v, dst_b, 1, 0)
            ddec = jnp.sum(st0 * dst, axis=0, keepdims=True)
            dstate[...] = dec * dst + _dg(do_b, qgt_b, 0, 0)
            dq_ref[sl, :] = (scale * (dqgn * e_qn + dqgt * e_b)).astype(dq_ref.dtype)
            dk_ref[sl, :] = (dkgn * e_kn + dkd * e_kd).astype(dk_ref.dtype)
            dv_ref[sl, :] = dv.astype(dv_ref.dtype)
            db = dqgn * qgn + dqgt * qgt - dkgn * kgn - dkd * kd
            extra = jnp.sum(dkd * kd, axis=0, keepdims=True) + ddec * dec
            dla_ref[sl, :] = jnp.dot(tri_t, db, precision=lax.Precision.HIGHEST, preferred_element_type=F32) + extra

    rev = lambda t: nb - 1 - t
    res = pl.pallas_call(
        body, name="gla_bwd", grid=(GLA_H, nb),
        in_specs=[pl.BlockSpec((tb, GLA_DK), lambda h, t: (rev(t), P_GQ // GLA_DK + h)),
                  pl.BlockSpec((tb, GLA_DK), lambda h, t: (rev(t), P_GK // GLA_DK + h)),
                  pl.BlockSpec((tb, GLA_DV), lambda h, t: (rev(t), P_GV // GLA_DV + h)),
                  pl.BlockSpec((tb, GLA_DK), lambda h, t: (rev(t), h)),
                  pl.BlockSpec((1, nc, GLA_DV, GLA_DK), lambda h, t: (h, rev(t), 0, 0)),
                  pl.BlockSpec((tb, GLA_DV), lambda h, t: (rev(t), h))] + [HBM] * len(c_ins),
        out_specs=[pl.BlockSpec((tb, GLA_DK), lambda h, t: (rev(t), h)),
                   pl.BlockSpec((tb, GLA_DK), lambda h, t: (rev(t), h)),
                   pl.BlockSpec((tb, GLA_DV), lambda h, t: (rev(t), h)),
                   pl.BlockSpec((tb, GLA_DK), lambda h, t: (rev(t), h))] + [HBM] * len(c_outs),
        out_shape=[jax.ShapeDtypeStruct((s, GLA_H * GLA_DK), BF16),
                   jax.ShapeDtypeStruct((s, GLA_H * GLA_DK), BF16),
                   jax.ShapeDtypeStruct((s, GLA_H * GLA_DV), BF16),
                   jax.ShapeDtypeStruct((s, GLA_H * GLA_DK), F32)] + c_outs,
        scratch_shapes=[pltpu.VMEM((GLA_DV, GLA_DK), F32)] + c_scratch,
        input_output_aliases=c_alias,
        compiler_params=_params(("arbitrary", "arbitrary") if comm else ("parallel", "arbitrary")),
    )(p, p, p, la, states, do, *c_ins)
    return res[0], res[1], res[2], res[3], _split_units(comm, res[4:])


def _head_masks():
    lane = lax.broadcasted_iota(jnp.int32, (1, 4 * ATT_HD), 1)
    return [(lane >= h * ATT_HD) & (lane < (h + 1) * ATT_HD) for h in range(4)]


def _attn_fwd(qv, kv, pv, g, r, s):
    ln = s // r
    nblk = ln // ATT_BLK
    qcol = lambda pr: pr
    vcol = qcol
    prev = lambda n: jnp.maximum(n - 1, 0)

    def body(q_ref, kp_ref, kc_ref, vp_ref, vc_ref, o_ref, lse_ref):
        has_prev = pl.program_id(1) > 0
        ri = lax.broadcasted_iota(jnp.int32, (ATT_BLK, ATT_BLK), 0)
        ci = lax.broadcasted_iota(jnp.int32, (ATT_BLK, ATT_BLK), 1)
        m_cur = ci <= ri
        m_prev = (ci >= ri) & has_prev
        q, kp, kc, vp, vc = q_ref[...], kp_ref[...], kc_ref[...], vp_ref[...], vc_ref[...]
        o = jnp.zeros((ATT_BLK, 256), F32)
        lse = jnp.zeros((ATT_BLK, 256), F32)
        for hm in _head_masks():
            qm = jnp.where(hm, q, jnp.zeros_like(q))
            sc = jnp.where(m_cur, _dg(qm, kc, 1, 1) * 0.125, NEG)
            sp = jnp.where(m_prev, _dg(qm, kp, 1, 1) * 0.125, NEG)
            mx = jnp.maximum(jnp.max(sc, axis=1, keepdims=True), jnp.max(sp, axis=1, keepdims=True))
            pc, pp = jnp.exp(sc - mx), jnp.exp(sp - mx)
            den = jnp.sum(pc, axis=1, keepdims=True) + jnp.sum(pp, axis=1, keepdims=True)
            oh = (_dg(pc.astype(BF16), vc, 1, 0) + _dg(pp.astype(BF16), vp, 1, 0)) / den
            o = jnp.where(hm, oh, o)
            lse = jnp.where(hm, mx + jnp.log(den), lse)
        o_ref[...] = o.astype(o_ref.dtype)
        lse_ref[...] = lse

    blk = (ATT_BLK, 256)
    o, lse = pl.pallas_call(
        body, name=f"attn_fwd_{g}", grid=(r, nblk),
        in_specs=[pl.BlockSpec(blk, lambda pr, n: (n, qcol(pr))),
                  pl.BlockSpec(blk, lambda pr, n: (prev(n), qcol(pr))),
                  pl.BlockSpec(blk, lambda pr, n: (n, qcol(pr))),
                  pl.BlockSpec(blk, lambda pr, n: (prev(n), vcol(pr))),
                  pl.BlockSpec(blk, lambda pr, n: (n, vcol(pr)))],
        out_specs=[pl.BlockSpec(blk, lambda pr, n: (n, pr)), pl.BlockSpec(blk, lambda pr, n: (n, pr))],
        out_shape=[jax.ShapeDtypeStruct((ln, r * 256), BF16), jax.ShapeDtypeStruct((ln, r * 256), F32)],
        compiler_params=_params(("parallel", "parallel")),
    )(qv, kv, kv, pv, pv)
    return o, lse


def _attn_bwd(qv, kv, pv, dov, ov, lv, g, r, s):
    ln = s // r
    nblk = ln // ATT_BLK
    qcol = lambda pr: pr
    vcol = qcol
    prev = lambda n: jnp.maximum(n - 1, 0)
    nxt = lambda n: jnp.minimum(n + 1, nblk - 1)

    def body(qc_ref, qn_ref, kp_ref, kc_ref, vp_ref, vc_ref, doc_ref, don_ref, oc_ref, on_ref, lc_ref, ln_ref,
             dq_ref, dk_ref, dv_ref):
        n = pl.program_id(1)
        has_prev, has_next = n > 0, n < nblk - 1
        ri = lax.broadcasted_iota(jnp.int32, (ATT_BLK, ATT_BLK), 0)
        ci = lax.broadcasted_iota(jnp.int32, (ATT_BLK, ATT_BLK), 1)
        m_cur = ci <= ri
        m_prev = (ci >= ri) & has_prev
        m_next = (ci >= ri) & has_next
        qc, qn, kp, kc, vp, vc = qc_ref[...], qn_ref[...], kp_ref[...], kc_ref[...], vp_ref[...], vc_ref[...]
        doc, don = doc_ref[...], don_ref[...]
        pc_full = doc.astype(F32) * oc_ref[...].astype(F32)
        pn_full = don.astype(F32) * on_ref[...].astype(F32)
        lc, lnx = lc_ref[...], ln_ref[...]
        dq = jnp.zeros((ATT_BLK, 256), F32)
        dk = jnp.zeros((ATT_BLK, 256), F32)
        dv = jnp.zeros((ATT_BLK, 256), F32)
        zb = jnp.zeros_like(qc)
        for hm in _head_masks():
            qcm, qnm = jnp.where(hm, qc, zb), jnp.where(hm, qn, zb)
            docm, donm = jnp.where(hm, doc, zb), jnp.where(hm, don, zb)
            lse_c = jnp.max(jnp.where(hm, lc, NEG), axis=1, keepdims=True)
            lse_n = jnp.max(jnp.where(hm, lnx, NEG), axis=1, keepdims=True)
            del_c = jnp.sum(jnp.where(hm, pc_full, 0.0), axis=1, keepdims=True)
            del_n = jnp.sum(jnp.where(hm, pn_full, 0.0), axis=1, keepdims=True)
            pr_ = jnp.where(m_cur, jnp.exp(_dg(qcm, kc, 1, 1) * 0.125 - lse_c), 0.0)
            ds = (pr_ * (_dg(docm, vc, 1, 1) - del_c) * 0.125).astype(BF16)
            dqh = _dg(ds, kc, 1, 0)
            dkh = _dg(ds, qc, 0, 0)
            dvh = _dg(pr_.astype(BF16), doc, 0, 0)
            pr_ = jnp.where(m_prev, jnp.exp(_dg(qcm, kp, 1, 1) * 0.125 - lse_c), 0.0)
            ds = (pr_ * (_dg(docm, vp, 1, 1) - del_c) * 0.125).astype(BF16)
            dqh = dqh + _dg(ds, kp, 1, 0)
            pr_ = jnp.where(m_next, jnp.exp(_dg(qnm, kc, 1, 1) * 0.125 - lse_n), 0.0)
            ds = (pr_ * (_dg(donm, vc, 1, 1) - del_n) * 0.125).astype(BF16)
            dkh = dkh + _dg(ds, qn, 0, 0)
            dvh = dvh + _dg(pr_.astype(BF16), don, 0, 0)
            dq = jnp.where(hm, dqh, dq)
            dk = jnp.where(hm, dkh, dk)
            dv = jnp.where(hm, dvh, dv)
        dq_ref[...] = dq.astype(dq_ref.dtype)
        dk_ref[...] = dk.astype(dk_ref.dtype)
        dv_ref[...] = dv.astype(dv_ref.dtype)

    blk = (ATT_BLK, 256)
    cur = lambda col: pl.BlockSpec(blk, lambda pr, n: (n, col(pr)))
    prv = lambda col: pl.BlockSpec(blk, lambda pr, n: (prev(n), col(pr)))
    nx = lambda col: pl.BlockSpec(blk, lambda pr, n: (nxt(n), col(pr)))
    own = lambda pr: pr
    outs = pl.pallas_call(
        body, name=f"attn_bwd_{g}", grid=(r, nblk),
        in_specs=[cur(qcol), nx(qcol), prv(qcol), cur(qcol), prv(vcol), cur(vcol),
                  cur(own), nx(own), cur(own), nx(own), cur(own), nx(own)],
        out_specs=[cur(own), cur(own), cur(own)],
        out_shape=[jax.ShapeDtypeStruct((ln, r * 256), BF16)] * 3,
        compiler_params=_params(("parallel", "parallel")),
    )(qv, qv, kv, kv, pv, pv, dov, dov, ov, ov, lv, lv)
    return outs


def _gelu_parts(gv):
    cdf = 0.5 * (1.0 + lax.erf(gv * (2.0 ** -0.5)))
    pdf = jnp.exp(-0.5 * gv * gv) * (1.0 / math.sqrt(2.0 * math.pi))
    return cdf, pdf


def _pick_row(t, k):
    row = lax.broadcasted_iota(jnp.int32, t.shape, 0)
    return jnp.sum(jnp.where(row == k, t, 0.0), axis=0, keepdims=True)


def _shift_rows(u, halo, n):
    row = lax.broadcasted_iota(jnp.int32, u.shape, 0)
    out = pltpu.roll(u, n, 0)
    for k in range(n):
        out = jnp.where(row == k, _pick_row(halo, 16 - n + k), out)
    return out


def _shift_rows_up(u, halo, n):
    rb = u.shape[0]
    row = lax.broadcasted_iota(jnp.int32, u.shape, 0)
    out = pltpu.roll(u, rb - n, 0)
    for k in range(n):
        out = jnp.where(row == rb - n + k, _pick_row(halo, k), out)
    return out


def _conv(u, halo, cw, cb):
    return cb + _pick_row(cw, 0) * _shift_rows(u, halo, 2) + _pick_row(cw, 1) * _shift_rows(u, halo, 1) + _pick_row(cw, 2) * u


def _local_step(x, mod, pos_col, target, sm, w_sh, chip, core):
    s = x.shape[0]
    shift1, scale1, gate1, shift2, scale2, gate2 = [mod[i:i + 1, :] for i in range(6)]
    rb = 256
    chip1 = chip.reshape(1)

    def f_norm1(c, i, xv, nw, sc, sh):
        return ((xv * _rms(xv) * nw) * (1.0 + sc) + sh,)

    (h,) = _rowcall(f_norm1, [_rows(x, rb), _full(sm["n1w"]), _full(scale1), _full(shift1)],
                    [_orow(s, D, BF16, rb)], n_rows=s, rb=rb, name="norm1")
    own = lambda got, i: lax.dynamic_update_slice(got, w_sh[i], (chip, 0, 0))
    [got0] = _comm_call("gather_w_in_ici", [_u_gather_ici(w_sh, (0,))])
    [got0] = _comm_call("gather_w_in_d2d", [_u_gather_d2d(got0, (0,))])
    w = dict(win=_win_assemble(own(got0[0], 0)))
    p, [got123] = _mm(h, w["win"], "in_proj", tm=1024, tn=1536, comm=[_u_gather_ici(w_sh, (1, 2, 3))])

    def f_gla_pre(c, i, glr, w2, gb):
        z = _dg(glr, w2.astype(BF16), 1, 0) + gb
        return ((jnp.minimum(z, 0.0) - jnp.log(1.0 + jnp.exp(-jnp.abs(z)))) * (1.0 / GLA_TAU),)

    (la,) = _rowcall(f_gla_pre, [_rows(p, rb, 128, P_LR // 128), _full(sm["w2"]), _full(sm["gb"])],
                     [_orow(s, 512, F32, rb)], n_rows=s, rb=rb, name="gla_pre")
    o_gla, states, [got123, got45] = _gla_fwd(p, la, s, comm=[_u_gather_d2d(got123, (1, 2, 3)), _u_gather_ici(w_sh, (4, 5))])
    w.update(wgb=own(got123[0], 1).reshape(1024, D), wab=_cols_join(own(got123[1], 2)), wout=own(got123[2], 3).reshape(D, D))

    def f_gla_post(c, i, ov, gnw, gr):
        on = jnp.concatenate([ov[:, k * 256:(k + 1) * 256] * _rms(ov[:, k * 256:(k + 1) * 256]) * gnw
                              for k in range(GLA_H)], axis=1)
        g = gr.astype(F32)
        return (on * (g * _sigmoid(g)),)

    (og,) = _rowcall(f_gla_post, [_rows(o_gla, rb), _full(sm["gnw"]), _rows(p, rb, 1024, P_GR // 1024)],
                     [_orow(s, 1024, BF16, rb)], n_rows=s, rb=rb, name="gla_post")
    y_gla = _mm(og, w["wgb"], "gla_branch")

    invf = jnp.tile(ROPE_THETA ** (-jnp.arange(ATT_HD // 2, dtype=F32) / (ATT_HD // 2)), 4).reshape(1, 128)
    cos_t, sin_t = _rope_tables(pos_col, invf, s)

    q_d, k_d, v_d = _rope_fwd(p, cos_t, sin_t, s)
    att = [_attn_fwd(q_d[g], k_d[g], v_d[g], g, r, s) for g, r in enumerate(_RS)]
    o_att, lse, o_d1, o_d2, lse_d1, lse_d2 = _attn_combine(att, s)
    y_att = _mm(o_att, w["wab"], "attn_branch")

    def f_merge(c, i, ma, mb, yg, ya):
        return (_sigmoid(ma.astype(F32)) * yg.astype(F32) + _sigmoid(mb.astype(F32)) * ya.astype(F32),)

    (mixed,) = _rowcall(f_merge, [_rows(p, rb, D, P_MA // D), _rows(p, rb, D, P_MB // D), _rows(y_gla, rb), _rows(y_att, rb)],
                        [_orow(s, D, BF16, rb)], n_rows=s, rb=rb, name="merge")
    z1, [got45] = _mm(mixed, w["wout"], "out_proj", comm=[_u_gather_d2d(got45, (4, 5))])
    w.update(wup=own(got45[0], 4), wdown=own(got45[1], 5).reshape(D_FF, D))

    def f_norm2(c, i, xv, z, g1, nw, sc, sh):
        x1 = xv + g1 * z.astype(F32)
        return (x1, (x1 * _rms(x1) * nw) * (1.0 + sc) + sh)

    x1, h2 = _rowcall(f_norm2, [_rows(x, rb), _rows(z1, rb), _full(gate1), _full(sm["n2w"]), _full(scale2), _full(shift2)],
                      [_orow(s, D, F32, rb), _orow(s, D, BF16, rb)], n_rows=s, rb=rb, name="norm2")
    u = _mm(h2, w["wup"], "up_proj", b_shards=True)

    cwid = 2 * W_UP_SH

    def f_ffn(c, i, uv, hl, cw, cb):
        uc = _conv(uv.astype(F32), hl.astype(F32) * (i > 0).astype(F32), cw, cb)
        val, gt = uc[:, :W_UP_SH], uc[:, W_UP_SH:]
        cdf, _ = _gelu_parts(gt)
        return (gt * cdf * val,)

    ccol = lambda c: c
    (hidden,) = _rowcall(f_ffn, [_rows(u, rb, cwid, ccol), _halo(u, rb, 16, cwid, ccol, True),
                                 _full(sm["cw"], cwid, ccol), _full(sm["cb"], cwid, ccol)],
                         [_orow(s, D_FF, BF16, rb, W_UP_SH, ccol)], n_rows=s, rb=rb, name="conv_geglu", ncol=2)
    z2 = _mm(hidden, w["wdown"], "down_proj", tk=1408)

    def f_final(c, i, x1v, z, g2, fw, tgt):
        x2 = x1v + g2 * z.astype(F32)
        r = _rms(x2)
        xh = x2 * r
        e = xh * fw - tgt
        loss = 0.5 * jnp.sum(jnp.mean(e * e, axis=-1, keepdims=True), axis=0, keepdims=True)
        dy = e * (1.0 / D)
        dxh = dy * fw
        dx2 = r * (dxh - xh * jnp.mean(dxh * xh, axis=-1, keepdims=True))
        return (loss, dx2, dx2 * g2, _csum(dy * xh), _csum(dx2 * z.astype(F32)))

    loss, dx2, dz2, d_fnw, d_gate2 = _rowcall(
        f_final, [_rows(x1, rb), _rows(z2, rb), _full(gate2), _full(sm["fnw"]), _rows(target, rb)],
        [_oacc(1, 1), _orow(s, D, F32, rb), _orow(s, D, BF16, rb), _oacc(1, D), _oacc(1, D)],
        n_rows=s, rb=rb, name="final_loss")
    d_hidden = _mm(dz2, w["wdown"], "down_proj_dx", tb=True, tn=1408)
    g_wdown = _mm(hidden, dz2, "down_proj_dw", ta=True, out_dtype=F32, tm=1408, tn=1024, tk=2048)

    def f_ffn_bwd(c, i, uv, hl, dh, cw, cb):
        uf = uv.astype(F32)
        hf = hl.astype(F32) * (i > 0).astype(F32)
        u1, u2 = _shift_rows(uf, hf, 1), _shift_rows(uf, hf, 2)
        uc = cb + _pick_row(cw, 0) * u2 + _pick_row(cw, 1) * u1 + _pick_row(cw, 2) * uf
        val, gt = uc[:, :W_UP_SH], uc[:, W_UP_SH:]
        cdf, pdf = _gelu_parts(gt)
        dhf = dh.astype(F32)
        duc = jnp.concatenate([dhf * (gt * cdf), dhf * val * (cdf + gt * pdf)], axis=1)
        dcw = jnp.concatenate([_csum(duc * u2), _csum(duc * u1), _csum(duc * uf)], axis=0)
        return (duc, _csum(duc), dcw)

    duc, d_cb, d_cw = _rowcall(
        f_ffn_bwd, [_rows(u, rb, cwid, ccol), _halo(u, rb, 16, cwid, ccol, True), _rows(d_hidden, rb, W_UP_SH, ccol),
                    _full(sm["cw"], cwid, ccol), _full(sm["cb"], cwid, ccol)],
        [_orow(s, 2 * D_FF, BF16, rb, cwid, ccol), _oacc(1, 2 * D_FF, cwid, ccol), _oacc(3, 2 * D_FF, cwid, ccol)],
        n_rows=s, rb=rb, name="conv_geglu_bwd", ncol=2)

    def f_conv_t(c, i, dv, hl, cw):
        df = dv.astype(F32)
        hf = hl.astype(F32) * (i < s // rb - 1).astype(F32)
        return (_pick_row(cw, 2) * df + _pick_row(cw, 1) * _shift_rows_up(df, hf, 1) + _pick_row(cw, 0) * _shift_rows_up(df, hf, 2),)

    (du,) = _rowcall(f_conv_t, [_rows(duc, rb, cwid, ccol), _halo(duc, rb, 16, cwid, ccol, False), _full(sm["cw"], cwid, ccol)],
                     [_orow(s, 2 * D_FF, BF16, rb, cwid, ccol)], n_rows=s, rb=rb, name="conv_transpose", ncol=2)
    g_wup = _mm(h2, du, "up_proj_dw", ta=True, out_dtype=F32, tm=1024, tk=2048, o_shards=True)
    gs45 = [g_wup, g_wdown.reshape(4, W_DOWN_SH, 1024)]
    d_h2, [land45] = _mm(du, w["wup"], "up_proj_dx", tb=True, b_shards=True, comm=[_u_pair_send(gs45, (4, 5))])
    ts45 = [_pair_add(g, ld, core, "grad_pair_add_" + BIG[i]) for g, ld, i in zip(gs45, land45, (4, 5))]

    def f_norm2_bwd(c, i, x1v, dh, dxr, z, nw, sc, g1):
        dxn, dsh, dsc, dnw = _norm_bwd(x1v, dh.astype(F32), nw, sc)
        dx1 = dxr + dxn
        return (dx1, dx1 * g1, dsh, dsc, dnw, _csum(dx1 * z.astype(F32)))

    dx1, dz1, d_shift2, d_scale2, d_n2w, d_gate1 = _rowcall(
        f_norm2_bwd, [_rows(x1, rb), _rows(d_h2, rb), _rows(dx2, rb), _rows(z1, rb), _full(sm["n2w"]), _full(scale2), _full(gate1)],
        [_orow(s, D, F32, rb), _orow(s, D, BF16, rb), _oacc(1, D), _oacc(1, D), _oacc(1, D), _oacc(1, D)],
        n_rows=s, rb=rb, name="norm2_bwd")
    d_mixed = _mm(dz1, w["wout"], "out_proj_dx", tb=True)
    g_wout = _mm(mixed, dz1, "out_proj_dw", ta=True, out_dtype=F32, tk=2048)

    def f_merge_bwd(c, i, dm, ma, mb, yg, ya):
        dmf, ygf, yaf = dm.astype(F32), yg.astype(F32), ya.astype(F32)
        sa, sb = _sigmoid(ma.astype(F32)), _sigmoid(mb.astype(F32))
        return (dmf * sa, dmf * sb, dmf * ygf * sa * (1.0 - sa), dmf * yaf * sb * (1.0 - sb))

    dy_gla, dy_att, d_ma, d_mb = _rowcall(
        f_merge_bwd, [_rows(d_mixed, rb), _rows(p, rb, D, P_MA // D), _rows(p, rb, D, P_MB // D), _rows(y_gla, rb), _rows(y_att, rb)],
        [_orow(s, D, BF16, rb)] * 4, n_rows=s, rb=rb, name="merge_bwd")
    d_og = _mm(dy_gla, w["wgb"], "gla_branch_dx", tb=True)
    g_wgb = _mm(og, dy_gla, "gla_branch_dw", ta=True, out_dtype=F32, tk=2048)
    d_oatt = _mm(dy_att, w["wab"], "attn_branch_dx", tb=True)
    g_wab = _mm(o_att, dy_att, "attn_branch_dw", ta=True, out_dtype=F32, tk=2048)

    def f_gla_post_bwd(c, i, ov, gnw, gr, dog):
        g = gr.astype(F32)
        sg = _sigmoid(g)
        silu = g * sg
        dof = dog.astype(F32)
        don = dof * silu
        on_parts, do_parts, dgn = [], [], jnp.zeros((1, 256), F32)
        for k in range(GLA_H):
            oh = ov[:, k * 256:(k + 1) * 256]
            dh = don[:, k * 256:(k + 1) * 256]
            r = _rms(oh)
            xh = oh * r
            dgn = dgn + _csum(dh * xh)
            dxh = dh * gnw
            do_parts.append(r * (dxh - xh * jnp.mean(dxh * xh, axis=-1, keepdims=True)))
            on_parts.append(xh * gnw)
        on = jnp.concatenate(on_parts, axis=1)
        dgr = dof * on * (sg * (1.0 + g * (1.0 - sg)))
        return (jnp.concatenate(do_parts, axis=1), dgr, dgn)

    do_gla, d_gr, d_gnw = _rowcall(
        f_gla_post_bwd, [_rows(o_gla, rb), _full(sm["gnw"]), _rows(p, rb, 1024, P_GR // 1024), _rows(d_og, rb)],
        [_orow(s, 1024, F32, rb), _orow(s, 1024, BF16, rb), _oacc(1, 256)], n_rows=s, rb=rb, name="gla_post_bwd")
    gs123 = [g_wgb.reshape(4, 256, 1024), _cols_split(g_wab), g_wout.reshape(4, 256, 1024)]
    d_gq, d_gk, d_gv, d_la, [r45, land123] = _gla_bwd(p, la, states, do_gla, s,
                                                      comm=[_u_chip_exchange(ts45), _u_pair_send(gs123, (1, 2, 3))])
    half45 = [_chip_sum(t, r, chip1, "grad_chip_sum_" + BIG[i]) for t, r, i in zip(ts45, r45, (4, 5))]
    ts123 = [_pair_add(g, ld, core, "grad_pair_add_" + BIG[i]) for g, ld, i in zip(gs123, land123, (1, 2, 3))]

    def f_gla_pre_bwd(c, i, lav, dlav, glr, w2):
        dz = dlav * (1.0 / GLA_TAU) * (1.0 - jnp.exp(GLA_TAU * lav))
        dzb = dz.astype(BF16)
        return (_dg(dzb, w2.astype(BF16), 1, 1), _csum(dz), _dg(glr, dzb, 0, 0))

    d_glr, d_gb, d_w2 = _rowcall(
        f_gla_pre_bwd, [_rows(la, rb), _rows(d_la, rb), _rows(p, rb, 128, P_LR // 128), _full(sm["w2"])],
        [_orow(s, 128, BF16, rb), _oacc(1, 512), _oacc(128, 512)], n_rows=s, rb=rb, name="gla_pre_bwd")

    do_d = [d_oatt] + list(_dilate(d_oatt, s))
    datt = [_attn_bwd(q_d[g], k_d[g], v_d[g], do_d[g], (o_att, o_d1, o_d2)[g], (lse, lse_d1, lse_d2)[g], g, r, s)
            for g, r in enumerate(_RS)]
    d_aq, d_ak, d_av = _rope_bwd(datt, cos_t, sin_t, s)
    dp = jnp.concatenate([d_gv, d_gr, d_ma, d_mb, d_gq, d_gk, d_aq, d_ak, d_av, d_glr,
                          jnp.zeros((s, P_W - P_LR - 128), BF16)], axis=1)
    g_win, [r123, oth45] = _mm(h, dp, "in_proj_dw", ta=True, out_dtype=F32, tm=1024, tn=1536, tk=2048,
                               comm=[_u_chip_exchange(ts123), _u_pair_join(half45)])
    half123 = [_chip_sum(t, r, chip1, "grad_chip_sum_" + BIG[i]) for t, r, i in zip(ts123, r123, (1, 2, 3))]
    gs0 = [_win_split(g_win)]
    d_h, [land0, oth123] = _mm(dp, w["win"], "in_proj_dx", tb=True, tk=1536,
                               comm=[_u_pair_send(gs0, (0,)), _u_pair_join(half123)])
    ts0 = [_pair_add(gs0[0], land0[0], core, "grad_pair_add_w_in")]
    [r0] = _comm_call("grad_exchange_w_in", [_u_chip_exchange(ts0)])
    half0 = [_chip_sum(ts0[0], r0[0], chip1, "grad_chip_sum_w_in")]
    [oth0] = _comm_call("grad_join_w_in", [_u_pair_join(half0)])

    def f_norm1_bwd(c, i, xv, dh, dxr, nw, sc):
        dxn, dsh, dsc, dnw = _norm_bwd(xv, dh.astype(F32), nw, sc)
        return (dxr + dxn, dsh, dsc, dnw)

    grad_x, d_shift1, d_scale1, d_n1w = _rowcall(
        f_norm1_bwd, [_rows(x, rb), _rows(d_h, rb), _rows(dx1, rb), _full(sm["n1w"]), _full(scale1)],
        [_orow(s, D, F32, rb), _oacc(1, D), _oacc(1, D), _oacc(1, D)], n_rows=s, rb=rb, name="norm1_bwd")

    dmod = jnp.concatenate([d_shift1, d_scale1, d_gate1, d_shift2, d_scale2, d_gate2], axis=1)
    small = dict(dmod=dmod, n1w=d_n1w, gb=d_gb, gnw=d_gnw, n2w=d_n2w, cb=d_cb, fnw=d_fnw, w2=d_w2, cw=d_cw)
    return loss, grad_x, half0 + half123 + half45, oth0 + oth123 + oth45, small


def _win_pieces():
    runs = [(P_GV, 1024, 2048), (P_MA, 5392, 2048), (P_GQ, 0, 1024), (P_AQ, 3088, 2304), (P_LR, 3072, GLA_LR)]
    out = []
    for kc, rc, ln in runs:
        while ln > 0:
            step = min(ln, W_IN_SH - rc % W_IN_SH)
            out.append((kc, rc, step))
            kc, rc, ln = kc + step, rc + step, ln - step
    return out


def _win_assemble(shards):
    rb = 256

    def body(s_ref, o_ref):
        o_ref[:, W_IN:] = jnp.zeros((rb, P_W - W_IN), o_ref.dtype)
        for kc, rc, ln in _win_pieces():
            o_ref[:, kc:kc + ln] = s_ref[rc // W_IN_SH, :, rc % W_IN_SH:rc % W_IN_SH + ln]

    return pl.pallas_call(
        body, name="w_in_assemble", grid=(D // rb,),
        in_specs=[pl.BlockSpec((4, rb, W_IN_SH), lambda i: (0, i, 0))], out_specs=pl.BlockSpec((rb, P_W), lambda i: (i, 0)),
        out_shape=jax.ShapeDtypeStruct((D, P_W), shards.dtype), compiler_params=_params(("parallel",)),
    )(shards)


def _win_split(g):
    rb = 256

    def body(g_ref, o_ref):
        for kc, rc, ln in _win_pieces():
            o_ref[rc // W_IN_SH, :, rc % W_IN_SH:rc % W_IN_SH + ln] = g_ref[:, kc:kc + ln]

    return pl.pallas_call(
        body, name="w_in_grad_split", grid=(D // rb,),
        in_specs=[pl.BlockSpec((rb, P_W), lambda i: (i, 0))], out_specs=pl.BlockSpec((4, rb, W_IN_SH), lambda i: (0, i, 0)),
        out_shape=jax.ShapeDtypeStruct((4, D, W_IN_SH), g.dtype), compiler_params=_params(("parallel",)),
    )(g)


def _ff_to_kernel(a):
    h = W_UP_SH
    return jnp.concatenate([a[:, 0:h], a[:, D_FF:D_FF + h], a[:, h:D_FF], a[:, D_FF + h:]], axis=1)


def _ff_from_kernel(a):
    h = W_UP_SH
    return jnp.concatenate([a[:, 0:h], a[:, 2 * h:3 * h], a[:, h:2 * h], a[:, 3 * h:]], axis=1)


BIG = ("w_in", "w_gla_branch", "w_attn_branch", "w_out", "w_up", "w_down")
SH_SHAPES = ((1024, W_IN_SH), (256, 1024), (256, 256), (256, 1024), (1024, W_UP_SH), (W_DOWN_SH, 1024))
N_BIG = len(BIG)


def _cols_join(t):
    return jnp.concatenate([t[k] for k in range(4)], axis=1)


def _cols_split(t):
    cols = t.shape[1] // 4
    return jnp.stack([t[:, k * cols:(k + 1) * cols] for k in range(4)])


def _me():
    return lax.axis_index("x"), lax.axis_index("y"), lax.axis_index("c")


HBM = pl.BlockSpec(memory_space=pltpu.HBM)
VMEM_SPEC = pl.BlockSpec(memory_space=pltpu.VMEM)


def _allgather8(xs, name):
    rows = xs.shape[0]

    def body(x_ref, out_ref, send_sems, recv_sems, local_sem):
        x, y, c = _me()
        me = 4 * x + 2 * y + c
        mine = pltpu.make_async_copy(x_ref, out_ref.at[me], local_sem)
        mine.start()
        flips = [(k >> 2 & 1, k >> 1 & 1, k & 1) for k in range(1, 8)]

        def peer(f):
            return (jnp.where(f[0] == 1, 1 - x, x), jnp.where(f[1] == 1, 1 - y, y), jnp.where(f[2] == 1, 1 - c, c))

        sends = []
        for k, f in enumerate(flips):
            cp = pltpu.make_async_remote_copy(src_ref=x_ref, dst_ref=out_ref.at[me], send_sem=send_sems.at[k],
                                              recv_sem=recv_sems.at[k], device_id=peer(f), device_id_type=MESH)
            cp.start()
            sends.append(cp)
        for k, f in enumerate(flips):
            px, py, pc = peer(f)
            pltpu.make_async_remote_copy(src_ref=x_ref, dst_ref=out_ref.at[4 * px + 2 * py + pc], send_sem=send_sems.at[k],
                                         recv_sem=recv_sems.at[k], device_id=peer(f), device_id_type=MESH).wait_recv()
        for cp in sends:
            cp.wait_send()
        mine.wait()

    return pl.pallas_call(
        body, name=name, out_shape=jax.ShapeDtypeStruct((8, rows, 128), F32),
        in_specs=[VMEM_SPEC], out_specs=VMEM_SPEC,
        scratch_shapes=[pltpu.SemaphoreType.DMA((7,)), pltpu.SemaphoreType.DMA((7,)), pltpu.SemaphoreType.DMA],
        compiler_params=pltpu.CompilerParams(vmem_limit_bytes=VMEM_LIMIT),
    )(xs)


def _half_rows(i, cc, unit):
    rows = SH_SHAPES[i][0] // 2
    return pl.ds(pl.multiple_of(cc * rows, unit), rows)


def _rc(src, dst, sems, to):
    return pltpu.make_async_remote_copy(src_ref=src, dst_ref=dst, send_sem=sems[0], recv_sem=sems[1], device_id=to, device_id_type=MESH)


def _other_chips(x, y):
    return [(1 - x, y), (x, 1 - y), (1 - x, 1 - y)]


def _u_gather_ici(w_sh, idxs):
    def copies(ins, outs, sem):
        x, y, c = _me()
        res = []
        for j, (px, py) in enumerate(_other_chips(x, y)):
            for n, i in enumerate(idxs):
                src = ins[n].at[0, _half_rows(i, c, 16)]
                res.append((_rc(src, outs[n].at[2 * x + y, _half_rows(i, c, 16)], sem(j * len(idxs) + n), (px, py, c)),
                            _rc(src, outs[n].at[2 * px + py, _half_rows(i, c, 16)], sem(j * len(idxs) + n), (px, py, c))))
        return res

    return dict(ins=[w_sh[i] for i in idxs], outs=[jax.ShapeDtypeStruct((4,) + SH_SHAPES[i], BF16) for i in idxs],
                nsem=3 * len(idxs), alias={}, copies=copies)


def _u_gather_d2d(got, idxs):
    def copies(ins, outs, sem):
        x, y, c = _me()
        res = []
        for j, (px, py) in enumerate(_other_chips(x, y)):
            for n, i in enumerate(idxs):
                src = ins[n].at[2 * px + py, _half_rows(i, c, 16)]
                res.append((_rc(src, outs[n].at[2 * px + py, _half_rows(i, c, 16)], sem(j * len(idxs) + n), (x, y, 1 - c)),
                            _rc(src, outs[n].at[2 * px + py, _half_rows(i, 1 - c, 16)], sem(j * len(idxs) + n), (x, y, 1 - c))))
        return res

    return dict(ins=list(got), outs=[jax.ShapeDtypeStruct(g.shape, g.dtype) for g in got], nsem=3 * len(idxs),
                alias={n: n for n in range(len(idxs))}, copies=copies)


def _u_pair_send(gs, idxs):
    def copies(ins, outs, sem):
        x, y, c = _me()
        res = []
        for n, i in enumerate(idxs):
            for sh in range(4):
                cp = _rc(ins[n].at[sh, _half_rows(i, 1 - c, 8)], outs[n].at[sh], sem(4 * n + sh), (x, y, 1 - c))
                res.append((cp, cp))
        return res

    return dict(ins=list(gs), outs=[jax.ShapeDtypeStruct((4, SH_SHAPES[i][0] // 2, SH_SHAPES[i][1]), F32) for i in idxs],
                nsem=4 * len(idxs), alias={}, copies=copies)


def _u_chip_exchange(ts):
    def copies(ins, outs, sem):
        x, y, c = _me()
        res = []
        for j, (px, py) in enumerate(_other_chips(x, y)):
            for n in range(len(ts)):
                cp = _rc(ins[n].at[2 * px + py], outs[n].at[j], sem(j * len(ts) + n), (px, py, c))
                res.append((cp, cp))
        return res

    return dict(ins=list(ts), outs=[jax.ShapeDtypeStruct((3,) + t.shape[1:], t.dtype) for t in ts], nsem=3 * len(ts),
                alias={}, copies=copies)


def _u_pair_join(hs):
    def copies(ins, outs, sem):
        x, y, c = _me()
        res = []
        for n in range(len(hs)):
            cp = _rc(ins[n], outs[n], sem(n), (x, y, 1 - c))
            res.append((cp, cp))
        return res

    return dict(ins=list(hs), outs=[jax.ShapeDtypeStruct(h.shape, h.dtype) for h in hs], nsem=len(hs), alias={}, copies=copies)


def _comm_phase(units, ci, co, send_sems, recv_sems, start):
    ii = oo = off = 0
    for u in units:
        ni, no = len(u["ins"]), len(u["outs"])
        for st, arrival in u["copies"](ci[ii:ii + ni], co[oo:oo + no], lambda k, off=off: (send_sems.at[off + k], recv_sems.at[off + k])):
            if start:
                st.start()
            else:
                st.wait_send()
                arrival.wait_recv()
        ii, oo, off = ii + ni, oo + no, off + u["nsem"]


def _carry(units, n_in, n_out):
    ins = [a for u in units for a in u["ins"]]
    outs = [o for u in units for o in u["outs"]]
    alias, ii, oo = {}, 0, 0
    for u in units:
        for a, b in u["alias"].items():
            alias[n_in + ii + a] = n_out + oo + b
        ii, oo = ii + len(u["ins"]), oo + len(u["outs"])
    nsem = sum(u["nsem"] for u in units)
    scratch = [pltpu.SemaphoreType.DMA((nsem,)), pltpu.SemaphoreType.DMA((nsem,))] if units else []
    return ins, outs, alias, scratch


def _split_units(units, res):
    out, oo = [], 0
    for u in units:
        out.append(list(res[oo:oo + len(u["outs"])]))
        oo += len(u["outs"])
    return out


def _comm_call(name, units):
    ins, outs, alias, scratch = _carry(units, 0, 0)

    def body(*refs):
        ci, co = refs[:len(ins)], refs[len(ins):len(ins) + len(outs)]
        _comm_phase(units, ci, co, refs[-2], refs[-1], True)
        _comm_phase(units, ci, co, refs[-2], refs[-1], False)

    res = pl.pallas_call(body, name=name, out_shape=outs, in_specs=[HBM] * len(ins), out_specs=[HBM] * len(outs),
                         scratch_shapes=scratch, input_output_aliases=alias)(*ins)
    return _split_units(units, res)


def _pair_add(g, land, core, name):
    _, rows, cols = g.shape
    half = rows // 2
    rb = _tile(half, 256, 16)
    nb = half // rb

    def body(c_ref, g_ref, l_ref, o_ref):
        o_ref[...] = (g_ref[...] + l_ref[...]).astype(BF16)

    return pl.pallas_call(
        body, name=name,
        grid_spec=pltpu.PrefetchScalarGridSpec(
            num_scalar_prefetch=1, grid=(4, nb),
            in_specs=[pl.BlockSpec((1, rb, cols), lambda s, i, c_ref: (s, c_ref[0] * nb + i, 0)),
                      pl.BlockSpec((1, rb, cols), lambda s, i, c_ref: (s, i, 0))],
            out_specs=pl.BlockSpec((1, rb, cols), lambda s, i, c_ref: (s, i, 0))),
        out_shape=jax.ShapeDtypeStruct((4, half, cols), BF16),
        compiler_params=_params(("parallel", "parallel")),
    )(core, g, land)


def _chip_sum(t, r, chip, name):
    _, half, cols = t.shape
    rb = _tile(half, 256, 16)

    def body(s_ref, t_ref, r_ref, o_ref):
        o_ref[...] = ((t_ref[0].astype(F32) + r_ref[0].astype(F32)) + r_ref[1].astype(F32)) + r_ref[2].astype(F32)

    return pl.pallas_call(
        body, name=name,
        grid_spec=pltpu.PrefetchScalarGridSpec(
            num_scalar_prefetch=1, grid=(half // rb,),
            in_specs=[pl.BlockSpec((1, rb, cols), lambda i, s_ref: (s_ref[0], i, 0)),
                      pl.BlockSpec((3, rb, cols), lambda i, s_ref: (0, i, 0))],
            out_specs=pl.BlockSpec((rb, cols), lambda i, s_ref: (i, 0))),
        out_shape=jax.ShapeDtypeStruct((half, cols), F32),
        compiler_params=_params(("parallel",)),
    )(chip, t, r)


def _adam_math(wv, gv, mv, vv):
    mn = ADAM_B1 * mv + (1.0 - ADAM_B1) * gv
    vn = ADAM_B2 * vv + (1.0 - ADAM_B2) * (gv * gv)
    m_hat = mn / (1.0 - ADAM_B1 ** ADAM_STEP)
    v_hat = vn / (1.0 - ADAM_B2 ** ADAM_STEP)
    return -ADAM_LR * (m_hat / (jnp.sqrt(v_hat) + ADAM_EPS) + ADAM_WD * wv), mn, vn


def _adamw_halves(wt, mt, vt, mine, theirs, core, name):
    _, rows, cols = wt.shape
    half = rows // 2
    rb = _tile(half, 256, 8)
    nb = half // rb

    def body(c_ref, w_ref, m_ref, v_ref, a_ref, b_ref, g_ref, d_ref, mo_ref, vo_ref):
        gv = jnp.where(pl.program_id(0) == c_ref[0], a_ref[...], b_ref[...])
        dl, mn, vn = _adam_math(w_ref[...], gv, m_ref[...], v_ref[...])
        g_ref[...] = gv
        d_ref[...] = dl
        mo_ref[...] = mn
        vo_ref[...] = vn

    full = pl.BlockSpec((None, rb, cols), lambda hf, i, c_ref: (0, hf * nb + i, 0))
    part = pl.BlockSpec((rb, cols), lambda hf, i, c_ref: (i, 0))
    return pl.pallas_call(
        body, name=name,
        grid_spec=pltpu.PrefetchScalarGridSpec(num_scalar_prefetch=1, grid=(2, nb), in_specs=[full, full, full, part, part],
                                               out_specs=[full] * 4),
        out_shape=[jax.ShapeDtypeStruct((1, rows, cols), F32)] * 4,
        compiler_params=_params(("parallel", "parallel")),
    )(core, wt, mt, vt, mine, theirs)


SG_REP = 136
SG_W2, SG_CW = SG_REP, SG_REP + 4 * 16
SG_ROWS = SG_CW + 4 * 40
SP_ROWS = SG_REP + 16 + 40


def _mod_shard(c_all, ada_w_sh):
    def body(c_ref, w_ref, o_ref):
        cv = c_ref[...]
        o_ref[...] = _dg((cv * _sigmoid(cv)).astype(BF16), w_ref[...].astype(BF16), 1, 0)

    return pl.pallas_call(body, name="mod_shard", out_shape=jax.ShapeDtypeStruct((8, 1536), F32),
                          in_specs=[VMEM_SPEC, VMEM_SPEC], out_specs=VMEM_SPEC,
                          compiler_params=pltpu.CompilerParams(vmem_limit_bytes=VMEM_LIMIT))(c_all, ada_w_sh)


def _mod_select(mod_all, ada_b4):
    def body(m_ref, b_ref, o_ref):
        x, y, c = _me()
        me = 4 * x + 2 * y + c
        for sh in range(4):
            o_ref[sh] = m_ref[2 * sh, me] + b_ref[sh]

    return pl.pallas_call(body, name="mod_select", out_shape=jax.ShapeDtypeStruct((4, 12, 128), F32),
                          in_specs=[VMEM_SPEC, VMEM_SPEC], out_specs=VMEM_SPEC)(mod_all, ada_b4)


def _small_reduce(sg_all):
    def body(g_ref, o_ref):
        x, y, c = _me()
        s_me = 2 * x + y
        w2_rows = pl.ds(pl.multiple_of(SG_W2 + 16 * s_me, 8), 16)
        cw_rows = pl.ds(pl.multiple_of(SG_CW + 40 * s_me, 8), 40)
        a = g_ref[0, 0:SG_REP, :]
        b = g_ref[0, w2_rows, :]
        d = g_ref[0, cw_rows, :]
        for dev in range(1, 8):
            a = a + g_ref[dev, 0:SG_REP, :]
            b = b + g_ref[dev, w2_rows, :]
            d = d + g_ref[dev, cw_rows, :]
        o_ref[0:SG_REP, :] = a
        o_ref[SG_REP:SG_REP + 16, :] = b
        o_ref[SG_REP + 16:SP_ROWS, :] = d

    return pl.pallas_call(body, name="small_grad_reduce", out_shape=jax.ShapeDtypeStruct((SP_ROWS, 128), F32),
                          in_specs=[VMEM_SPEC], out_specs=VMEM_SPEC)(sg_all)


def _ada_grad(dmod_all, c_bc):
    def body(g_ref, c_ref, o_ref):
        x, y, c = _me()
        s_me = 2 * x + y
        for k in range(12):
            acc = jnp.zeros((D, 128), F32)
            for b in range(8):
                cv = c_ref[b]
                acc = acc + (cv * _sigmoid(cv)) * g_ref[s_me, k, b:b + 1, :]
            o_ref[:, k * 128:(k + 1) * 128] = acc

    return pl.pallas_call(body, name="ada_w_grad", out_shape=jax.ShapeDtypeStruct((D, 1536), F32),
                          in_specs=[VMEM_SPEC, VMEM_SPEC], out_specs=VMEM_SPEC,
                          compiler_params=pltpu.CompilerParams(vmem_limit_bytes=VMEM_LIMIT))(dmod_all, c_bc)


def _adamw(wt, g, m, v, name):
    rows, cols = wt.shape
    rb = _tile(rows, 256, 8)

    def fn(c, i, wv, gv, mv, vv):
        return _adam_math(wv, gv, mv, vv)

    return _rowcall(fn, [_rows(t, rb) for t in (wt, g, m, v)], [_orow(rows, cols, F32, rb)] * 3,
                    n_rows=rows, rb=rb, name=name)


def _pad_rows(t, rows):
    flat = t.reshape(-1)
    return jnp.pad(flat, (0, rows * 128 - flat.shape[0])).reshape(rows, 128)


SP_LAYOUT = (("ada_b", 48), ("norm1_w", 8), ("gla_gate_b", 8), ("gla_norm_w", 8), ("norm2_w", 8), ("conv_b", 48),
             ("final_norm_w", 8), ("gla_gate_w2", 16), ("conv_w", 40))


def _pack_small(d):
    return jnp.concatenate([_pad_rows(d[n].astype(F32), rows) for n, rows in SP_LAYOUT], axis=0)


def _unpack_small(pk, shapes):
    out, off = {}, 0
    for n, rows in SP_LAYOUT:
        shp = shapes[n]
        out[n] = pk[off:off + rows].reshape(-1)[:math.prod(shp)].reshape(shp)
        off += rows
    return out


def kernel(x, c, positions, ada_w, ada_b, norm1_w, w_in, gla_gate_w2, gla_gate_b, gla_norm_w, w_gla_branch, w_attn_branch, w_out, norm2_w, w_up, conv_w, conv_b, w_down, final_norm_w, loss_target, m_ada_w, m_ada_b, m_norm1_w, m_w_in, m_gla_gate_w2, m_gla_gate_b, m_gla_norm_w, m_w_gla_branch, m_w_attn_branch, m_w_out, m_norm2_w, m_w_up, m_conv_w, m_conv_b, m_w_down, m_final_norm_w, v_ada_w, v_ada_b, v_norm1_w, v_w_in, v_gla_gate_w2, v_gla_gate_b, v_gla_norm_w, v_w_gla_branch, v_w_attn_branch, v_w_out, v_norm2_w, v_w_up, v_conv_w, v_conv_b, v_w_down, v_final_norm_w):
    s = x.shape[1]
    names = ("ada_w", "ada_b", "norm1_w", "w_in", "gla_gate_w2", "gla_gate_b", "gla_norm_w", "w_gla_branch", "w_attn_branch",
             "w_out", "norm2_w", "w_up", "conv_w", "conv_b", "w_down", "final_norm_w")
    wts = dict(zip(names, (ada_w, ada_b, norm1_w, w_in, gla_gate_w2, gla_gate_b, gla_norm_w, w_gla_branch, w_attn_branch,
                           w_out, norm2_w, w_up, conv_w, conv_b, w_down, final_norm_w)))
    ms = dict(zip(names, (m_ada_w, m_ada_b, m_norm1_w, m_w_in, m_gla_gate_w2, m_gla_gate_b, m_gla_norm_w, m_w_gla_branch,
                          m_w_attn_branch, m_w_out, m_norm2_w, m_w_up, m_conv_w, m_conv_b, m_w_down, m_final_norm_w)))
    vs = dict(zip(names, (v_ada_w, v_ada_b, v_norm1_w, v_w_in, v_gla_gate_w2, v_gla_gate_b, v_gla_norm_w, v_w_gla_branch,
                          v_w_attn_branch, v_w_out, v_norm2_w, v_w_up, v_conv_w, v_conv_b, v_w_down, v_final_norm_w)))

    pk0 = jnp.concatenate([_pad_rows(c, 8), _pad_rows(gla_gate_w2, 16), _pad_rows(conv_w, 40)], axis=0)
    sm_all = _allgather8(pk0, "gather_small")
    c_all = sm_all[:, 0:8, :].reshape(8, D)
    w2_full = sm_all[0::2, 8:24, :].transpose(1, 0, 2).reshape(GLA_LR, 512)
    cw_full = sm_all[0::2, 24:64, :].reshape(4, 40 * 128)[:, :3 * W_UP_SH].reshape(4, 3, W_UP_SH).transpose(1, 0, 2).reshape(3, 2 * D_FF)

    mod_sh = _mod_shard(c_all, ada_w[0])
    mod_all = _allgather8(mod_sh.reshape(96, 128), "gather_mod")
    mod = _mod_select(mod_all.reshape(8, 8, 12, 128), ada_b.reshape(4, 12, 128)).reshape(6, D)

    core = lax.axis_index("c").astype(jnp.int32).reshape(1)
    chip = (2 * lax.axis_index("x") + lax.axis_index("y")).astype(jnp.int32)
    w_sh = [wts[n].astype(BF16) for n in BIG]
    sm = dict(n1w=norm1_w, n2w=norm2_w, fnw=final_norm_w.reshape(1, D), gnw=gla_norm_w, gb=gla_gate_b,
              w2=jnp.pad(w2_full, ((0, 128 - GLA_LR), (0, 0))), cw=_ff_to_kernel(cw_full), cb=_ff_to_kernel(conv_b))
    loss, grad_x, halves, others, small = _local_step(x[0], mod, positions.reshape(s, 1), loss_target[0], sm, w_sh, chip, core)

    dcw = _ff_from_kernel(small["cw"]).reshape(3, 4, W_UP_SH).transpose(1, 0, 2)
    dw2 = small["w2"][:GLA_LR].reshape(GLA_LR, 4, 128).transpose(1, 0, 2)
    sg = jnp.concatenate(
        [_pad_rows(small["dmod"], 48), _pad_rows(small["n1w"], 8), _pad_rows(small["gb"], 8), _pad_rows(small["gnw"], 8),
         _pad_rows(small["n2w"], 8), _pad_rows(_ff_from_kernel(small["cb"]), 48), _pad_rows(small["fnw"], 8)]
        + [_pad_rows(dw2[k], 16) for k in range(4)] + [_pad_rows(dcw[k], 40) for k in range(4)], axis=0)
    sg_all = _allgather8(sg, "gather_small_grads")
    g_small_pk = _small_reduce(sg_all)
    dmod_all = sg_all[:, 0:48, :].reshape(8, 4, 12, 128).transpose(1, 2, 0, 3)
    g_ada_w = _ada_grad(dmod_all, jnp.broadcast_to(c_all[:, :, None], (8, D, 128)))

    shapes = {n: wts[n].shape for n in names}
    g_small = _unpack_small(g_small_pk, shapes)
    grads = {"ada_w": g_ada_w.reshape(1, D, 1536), **g_small}
    deltas, new_m, new_v = {}, {}, {}
    for n, mine, theirs in zip(BIG, halves, others):
        grads[n], deltas[n], new_m[n], new_v[n] = _adamw_halves(wts[n], ms[n], vs[n], mine, theirs, core, "adamw_" + n)
    shp = ada_w.shape
    d_, m_, v_ = _adamw(ada_w[0], g_ada_w, m_ada_w[0], v_ada_w[0], "adamw_ada_w")
    deltas["ada_w"], new_m["ada_w"], new_v["ada_w"] = d_.reshape(shp), m_.reshape(shp), v_.reshape(shp)
    d_, m_, v_ = _adamw(_pack_small(wts), g_small_pk, _pack_small(ms), _pack_small(vs), "adamw_small")
    for dst, pk in ((deltas, d_), (new_m, m_), (new_v, v_)):
        dst.update(_unpack_small(pk, shapes))

    loss_all = lax.psum(loss[0, 0], ("x", "y", "c"))
    return (loss_all, grad_x.reshape(1, s, D), *[grads[n] for n in names], *[deltas[n] for n in names],
            *[new_m[n] for n in names], *[new_v[n] for n in names])
```

```python
import math

import jax
import jax.numpy as jnp
from jax import lax
from jax.experimental import pallas as pl
from jax.experimental.pallas import tpu as pltpu

F32, BF16 = jnp.float32, jnp.bfloat16
MESH = pl.DeviceIdType.MESH

D = 1024
EPS = 1e-6
GLA_H, GLA_DK, GLA_DV, GLA_LR = 4, 128, 256, 16
GLA_TAU = 16.0
GLA_CHUNK = 64
GLA_BLOCK = 512
ATT_GROUPS = ((128, 1), (512, 4), (2048, 16))
ATT_BLK = 128
ATT_HD = 64
ATT_W = 768
D_FF = 2816
ROPE_THETA = 10000.0
P_W = 7680
P_GV, P_GR, P_MA, P_MB, P_GQ, P_GK, P_AQ, P_AK, P_AV, P_LR = 0, 1024, 2048, 3072, 4096, 4608, 5120, 5888, 6656, 7424
W_IN = 7440
W_IN_SH, W_UP_SH, W_DOWN_SH = 1860, 1408, 704
VMEM_LIMIT = 56 * 1024 * 1024
ADAM_LR, ADAM_B1, ADAM_B2, ADAM_EPS, ADAM_WD, ADAM_STEP = 0.001, 0.9, 0.999, 1e-08, 0.01, 10
NEG = -1e30


def _tile(n, target, unit=128):
    best = None
    for t in range(unit, min(n, target) + 1, unit):
        if n % t == 0:
            best = t
    return best or n


def _params(sem):
    return pltpu.CompilerParams(dimension_semantics=sem, vmem_limit_bytes=VMEM_LIMIT)


def _dg(a, b, ca, cb):
    return lax.dot_general(a, b, (((ca,), (cb,)), ((), ())), preferred_element_type=F32)


def _sigmoid(v):
    return 1.0 / (1.0 + jnp.exp(-v))


def _ff_block(j):
    return (j % 2) * 2 + j // 2


def _mm(a, b, name, *, ta=False, tb=False, out_dtype=BF16, tm=1024, tn=1536, tk=1024, n_outer=True, comm=(),
        b_shards=False, o_shards=False):
    m = a.shape[1] if ta else a.shape[0]
    k = a.shape[0] if ta else a.shape[1]
    if b_shards:
        n = b.shape[1] if tb else 4 * W_UP_SH
        tn, tk = (tn, W_UP_SH) if tb else (W_UP_SH, tk)
    else:
        n = b.shape[0] if tb else b.shape[1]
    if o_shards:
        tn = W_UP_SH
    tm, tn, tk = _tile(m, tm), _tile(n, tn), _tile(k, tk)
    nm, nn, nk = m // tm, n // tn, k // tk
    in_out = out_dtype == F32
    c_ins, c_outs, c_alias, c_scratch = _carry(comm, 2, 1)

    def body(a_ref, b_ref, *rest):
        ci, o_ref, co = rest[:len(c_ins)], rest[len(c_ins)], rest[len(c_ins) + 1:len(c_ins) + 1 + len(c_outs)]
        scr = rest[len(c_ins) + 1 + len(c_outs):]
        kk = pl.program_id(2)
        if comm:
            step = (pl.program_id(0) * (nm if n_outer else nn) + pl.program_id(1)) * nk + kk

            @pl.when(step == 0)
            def _():
                _comm_phase(comm, ci, co, scr[-2], scr[-1], True)

        _mm_step(a_ref, b_ref, o_ref, scr, kk)
        if comm:
            @pl.when(step == nm * nn * nk - 1)
            def _():
                _comm_phase(comm, ci, co, scr[-2], scr[-1], False)

    def _mm_step(a_ref, b_ref, o_ref, scr, kk):
        p = _dg(a_ref[...].astype(BF16), b_ref[...].astype(BF16), 0 if ta else 1, 1 if tb else 0)
        if nk == 1:
            o_ref[...] = p.astype(o_ref.dtype)
        else:
            acc = o_ref if in_out else scr[0]

            @pl.when(kk == 0)
            def _():
                acc[...] = p

            @pl.when(kk > 0)
            def _():
                acc[...] += p

            if not in_out:
                @pl.when(kk == nk - 1)
                def _():
                    o_ref[...] = acc[...].astype(o_ref.dtype)

    if n_outer:
        ij = lambda g0, g1: (g1, g0)
        grid = (nn, nm, nk)
    else:
        ij = lambda g0, g1: (g0, g1)
        grid = (nm, nn, nk)
    a_map = (lambda g0, g1, kk: (kk, ij(g0, g1)[0])) if ta else (lambda g0, g1, kk: (ij(g0, g1)[0], kk))
    if b_shards and tb:
        b_spec = pl.BlockSpec((None, tn, tk), lambda g0, g1, kk: (_ff_block(kk), ij(g0, g1)[1], 0))
    elif b_shards:
        b_spec = pl.BlockSpec((None, tk, tn), lambda g0, g1, kk: (_ff_block(ij(g0, g1)[1]), kk, 0))
    elif tb:
        b_spec = pl.BlockSpec((tn, tk), lambda g0, g1, kk: (ij(g0, g1)[1], kk))
    else:
        b_spec = pl.BlockSpec((tk, tn), lambda g0, g1, kk: (kk, ij(g0, g1)[1]))
    if o_shards:
        o_spec = pl.BlockSpec((None, tm, tn), lambda g0, g1, kk: (_ff_block(ij(g0, g1)[1]), ij(g0, g1)[0], 0))
        o_shape = jax.ShapeDtypeStruct((4, m, W_UP_SH), out_dtype)
    else:
        o_spec = pl.BlockSpec((tm, tn), lambda g0, g1, kk: ij(g0, g1))
        o_shape = jax.ShapeDtypeStruct((m, n), out_dtype)
    res = pl.pallas_call(
        body, name=name, grid=grid,
        in_specs=[pl.BlockSpec((tk, tm) if ta else (tm, tk), a_map), b_spec] + [HBM] * len(c_ins),
        out_specs=[o_spec] + [HBM] * len(c_outs),
        out_shape=[o_shape] + c_outs,
        scratch_shapes=([] if (in_out or nk == 1) else [pltpu.VMEM((tm, tn), F32)]) + c_scratch,
        input_output_aliases=c_alias,
        compiler_params=_params(("arbitrary",) * 3 if comm else ("parallel", "parallel", "arbitrary")),
    )(a, b, *c_ins)
    return (res[0], _split_units(comm, res[1:])) if comm else res[0]


def _rows(arr, rb, w=None, j=0):
    w = arr.shape[1] if w is None else w
    if callable(j):
        return arr, pl.BlockSpec((rb, w), lambda c, i: (i, j(c)))
    return arr, pl.BlockSpec((rb, w), lambda c, i: (i, j))


def _full(arr, w=None, j=0):
    w = arr.shape[1] if w is None else w
    if callable(j):
        return arr, pl.BlockSpec((arr.shape[0], w), lambda c, i: (0, j(c)))
    return arr, pl.BlockSpec((arr.shape[0], w), lambda c, i: (0, j))


def _halo(arr, rb, hb, w, j, before):
    per = rb // hb
    last = arr.shape[0] // hb - 1
    if before:
        rmap = lambda i: jnp.maximum(i * per - 1, 0)
    else:
        rmap = lambda i: jnp.minimum((i + 1) * per, last)
    return arr, pl.BlockSpec((hb, w), lambda c, i: (rmap(i), j(c) if callable(j) else j))


def _rowcall(fn, ins, outs, *, n_rows, rb, name, ncol=1, into=None):
    n_in = len(ins)
    nr = n_rows // rb
    n_skip = 0 if into is None else 1

    def body(*refs):
        c, i = pl.program_id(0), pl.program_id(1)
        res = fn(c, i, *[r[...] for r in refs[:n_in]])
        for val, spec, o_ref in zip(res, outs, refs[n_in + n_skip:]):
            if spec[2] == "row":
                o_ref[...] = val.astype(o_ref.dtype)
            else:
                @pl.when(i == 0)
                def _(o_ref=o_ref, val=val):
                    o_ref[...] = val.astype(o_ref.dtype)

                @pl.when(i > 0)
                def _(o_ref=o_ref, val=val):
                    o_ref[...] += val.astype(o_ref.dtype)

    out_specs = []
    for shape, dt, kind, block, col in outs:
        if kind == "row":
            out_specs.append(pl.BlockSpec(block, lambda c, i, col=col: (i, col(c))))
        else:
            out_specs.append(pl.BlockSpec(block, lambda c, i, col=col: (0, col(c))))
    return pl.pallas_call(
        body, name=name, grid=(ncol, nr),
        in_specs=[s for _, s in ins] + [pl.BlockSpec(memory_space=pl.ANY)] * n_skip, out_specs=out_specs,
        out_shape=[jax.ShapeDtypeStruct(o[0], o[1]) for o in outs],
        input_output_aliases={} if into is None else {n_in: into[1]},
        compiler_params=_params(("parallel", "arbitrary")),
    )(*[a for a, _ in ins], *([] if into is None else [into[0]]))


def _orow(n_rows, w, dt, rb, bw=None, col=lambda c: 0):
    return ((n_rows, w), dt, "row", (rb, bw or w), col)


def _oacc(r, w, bw=None, col=lambda c: 0):
    return ((r, w), F32, "acc", (r, bw or w), col)


def _csum(v):
    return jnp.sum(v, axis=0, keepdims=True)


def _rms(v):
    return lax.rsqrt(jnp.mean(v * v, axis=-1, keepdims=True) + EPS)


def _norm_bwd(xv, dh, w, scale):
    r = _rms(xv)
    xh = xv * r
    dxh = dh * (w * (1.0 + scale))
    dx = r * (dxh - xh * jnp.mean(dxh * xh, axis=-1, keepdims=True))
    t = dh * xh
    return dx, _csum(dh), _csum(t * w), _csum(t * (1.0 + scale))


def _rope_tables(pos_col, invf, s):
    def fn(c, i, pos, f):
        ang = pos.astype(F32) * f
        lane = lax.broadcasted_iota(jnp.int32, ang.shape, 1)
        sign = jnp.where((lane % ATT_HD) < ATT_HD // 2, -1.0, 1.0)
        return jnp.cos(ang), jnp.sin(ang) * sign

    rb = 512
    return _rowcall(fn, [_rows(pos_col, rb), _full(invf)], [_orow(s, 128, F32, rb), _orow(s, 128, F32, rb)],
                    n_rows=s, rb=rb, name="rope_tables")


def _swap_halves(t):
    n = t.shape[1]
    lane = lax.broadcasted_iota(jnp.int32, t.shape, 1)
    return jnp.where((lane % ATT_HD) < ATT_HD // 2, pltpu.roll(t, n - 32, 1), pltpu.roll(t, 32, 1))


def _rope_apply(t, cos, sin_signed, inverse):
    cw = jnp.concatenate([cos] * (t.shape[1] // 128), axis=1)
    sw = jnp.concatenate([sin_signed] * (t.shape[1] // 128), axis=1)
    if inverse:
        sw = -sw
    return t * cw + _swap_halves(t) * sw


DIL_ROWS = 512


def _to_dilated(scr, val, out_ref, r):
    if r == 1:
        out_ref[...] = val.astype(out_ref.dtype)
        return
    n = val.shape[0] // r
    for hh in range(2):
        scr[hh] = val[:, hh * 128:(hh + 1) * 128]
        for pr in range(r):
            out_ref[:, pr * 256 + hh * 128:pr * 256 + (hh + 1) * 128] = scr[hh, pl.ds(pr, n, stride=r), :].astype(out_ref.dtype)


def _from_dilated(scr, in_ref, r):
    if r == 1:
        return in_ref[...].astype(F32)
    n = in_ref.shape[0]
    for hh in range(2):
        for pr in range(r):
            scr[hh, pl.ds(pr, n, stride=r), :] = in_ref[:, pr * 256 + hh * 128:pr * 256 + (hh + 1) * 128].astype(F32)
    return jnp.concatenate([scr[0], scr[1]], axis=1)


def _dil_spec(r):
    return pl.BlockSpec((DIL_ROWS // r, r * 256), lambda i: (i, 0))


def _dil_shape(s, r, dt):
    return jax.ShapeDtypeStruct((s // r, r * 256), dt)


_DIL_SCRATCH = [pltpu.VMEM((2, DIL_ROWS, 128), F32)]
_RS = tuple(r for _, r in ATT_GROUPS)


def _rope_fwd(p, cos_t, sin_t, s):
    def body(*refs):
        ins, cs, sn, outs, scr = refs[:9], refs[9][...], refs[10][...], refs[11:20], refs[20]
        for t in range(3):
            for g, r in enumerate(_RS):
                val = ins[3 * t + g][...].astype(F32)
                _to_dilated(scr, _rope_apply(val, cs, sn, False) if t < 2 else val, outs[3 * t + g], r)

    res = pl.pallas_call(
        body, name="rope", grid=(s // DIL_ROWS,),
        in_specs=[pl.BlockSpec((DIL_ROWS, 256), lambda i, c=base // 256 + g: (i, c)) for base in (P_AQ, P_AK, P_AV) for g in range(3)]
        + [pl.BlockSpec((DIL_ROWS, 128), lambda i: (i, 0))] * 2,
        out_specs=[_dil_spec(r) for _ in range(3) for r in _RS],
        out_shape=[_dil_shape(s, r, BF16) for _ in range(3) for r in _RS],
        scratch_shapes=_DIL_SCRATCH, compiler_params=_params(("parallel",)),
    )(*([p] * 9), cos_t, sin_t)
    return res[0:3], res[3:6], res[6:9]


def _attn_combine(att, s):
    def body(o0, o1, o2, l0, l1, l2, o_ref, lse_ref, od1, od2, ld1, ld2, scr):
        ov = [_from_dilated(scr, ref, r) for ref, r in zip((o0, o1, o2), _RS)]
        lv = [_from_dilated(scr, ref, r) for ref, r in zip((l0, l1, l2), _RS)]
        mx = jnp.maximum(jnp.maximum(lv[0], lv[1]), lv[2])
        ev = [jnp.exp(l - mx) for l in lv]
        z = ev[0] + ev[1] + ev[2]
        o = ((ev[0] * ov[0] + ev[1] * ov[1] + ev[2] * ov[2]) / z).astype(BF16)
        lse = mx + jnp.log(z)
        o_ref[...] = o
        lse_ref[...] = lse
        for ref, r in zip((od1, od2), _RS[1:]):
            _to_dilated(scr, o.astype(F32), ref, r)
        for ref, r in zip((ld1, ld2), _RS[1:]):
            _to_dilated(scr, lse, ref, r)

    return pl.pallas_call(
        body, name="attn_combine", grid=(s // DIL_ROWS,),
        in_specs=[_dil_spec(r) for r in _RS] * 2,
        out_specs=[_dil_spec(1)] * 2 + [_dil_spec(r) for r in _RS[1:]] * 2,
        out_shape=[_dil_shape(s, 1, BF16), _dil_shape(s, 1, F32)] + [_dil_shape(s, r, BF16) for r in _RS[1:]]
        + [_dil_shape(s, r, F32) for r in _RS[1:]],
        scratch_shapes=_DIL_SCRATCH, compiler_params=_params(("parallel",)),
    )(*[a[0] for a in att], *[a[1] for a in att])


def _dilate(t, s):
    def body(t_ref, o1, o2, scr):
        val = t_ref[...].astype(F32)
        for ref, r in zip((o1, o2), _RS[1:]):
            _to_dilated(scr, val, ref, r)

    return pl.pallas_call(
        body, name="attn_dilate", grid=(s // DIL_ROWS,), in_specs=[_dil_spec(1)], out_specs=[_dil_spec(r) for r in _RS[1:]],
        out_shape=[_dil_shape(s, r, t.dtype) for r in _RS[1:]], scratch_shapes=_DIL_SCRATCH, compiler_params=_params(("parallel",)),
    )(t)


def _rope_bwd(datt, d_glr, dp, cos_t, sin_t, s):
    tail = P_W - P_AQ

    def body(*refs):
        ins, cs, sn, glr, o_ref, scr = refs[:9], refs[9][...], refs[10][...], refs[11], refs[13], refs[14]
        for t in range(3):
            for g, r in enumerate(_RS):
                val = _from_dilated(scr, ins[3 * t + g], r)
                o_ref[:, t * ATT_W + g * 256:t * ATT_W + (g + 1) * 256] = (_rope_apply(val, cs, sn, True) if t < 2 else val).astype(BF16)
        o_ref[:, 3 * ATT_W:3 * ATT_W + 128] = glr[...]
        o_ref[:, 3 * ATT_W + 128:] = jnp.zeros((DIL_ROWS, tail - 3 * ATT_W - 128), BF16)

    return pl.pallas_call(
        body, name="rope_bwd", grid=(s // DIL_ROWS,),
        in_specs=[_dil_spec(r) for _ in range(3) for r in _RS] + [pl.BlockSpec((DIL_ROWS, 128), lambda i: (i, 0))] * 3
        + [pl.BlockSpec(memory_space=pl.ANY)],
        out_specs=pl.BlockSpec((DIL_ROWS, tail), lambda i: (i, P_AQ // tail)),
        out_shape=jax.ShapeDtypeStruct((s, P_W), BF16), input_output_aliases={12: 0},
        scratch_shapes=_DIL_SCRATCH, compiler_params=_params(("parallel",)),
    )(*[datt[g][t] for t in range(3) for g in range(3)], cos_t, sin_t, d_glr, dp)


def _gla_decays(la_c, tri):
    b = jnp.dot(tri, la_c, precision=lax.Precision.HIGHEST, preferred_element_type=F32)
    row = lax.broadcasted_iota(jnp.int32, b.shape, 0)
    bmid = jnp.sum(jnp.where(row == GLA_CHUNK // 2 - 1, b, 0.0), axis=0, keepdims=True)
    blast = jnp.sum(jnp.where(row == GLA_CHUNK - 1, b, 0.0), axis=0, keepdims=True)
    return b, bmid, blast


def _gla_fwd(p, la, s, comm=()):
    tb, ch = GLA_BLOCK, GLA_CHUNK
    nb, nc = s // tb, tb // ch
    scale = GLA_DK ** -0.5
    c_ins, c_outs, c_alias, c_scratch = _carry(comm, 4, 2)

    def body(q_ref, k_ref, v_ref, la_ref, *rest):
        ci, (o_ref, st_ref) = rest[:len(c_ins)], rest[len(c_ins):len(c_ins) + 2]
        co, state = rest[len(c_ins) + 2:len(c_ins) + 2 + len(c_outs)], rest[len(c_ins) + 2 + len(c_outs)]
        step = pl.program_id(0) * nb + pl.program_id(1)
        if comm:
            @pl.when(step == 0)
            def _():
                _comm_phase(comm, ci, co, rest[-2], rest[-1], True)

        _gla_fwd_step(q_ref, k_ref, v_ref, la_ref, o_ref, st_ref, state)
        if comm:
            @pl.when(step == GLA_H * nb - 1)
            def _():
                _comm_phase(comm, ci, co, rest[-2], rest[-1], False)

    def _gla_fwd_step(q_ref, k_ref, v_ref, la_ref, o_ref, st_ref, state):
        @pl.when(pl.program_id(1) == 0)
        def _():
            state[...] = jnp.zeros_like(state)

        ri = lax.broadcasted_iota(jnp.int32, (ch, ch), 0)
        ci = lax.broadcasted_iota(jnp.int32, (ch, ch), 1)
        causal = ci <= ri
        tri = causal.astype(F32)
        for c in range(nc):
            sl = pl.ds(c * ch, ch)
            b, bmid, blast = _gla_decays(la_ref[sl, :], tri)
            q = q_ref[sl, :].astype(F32) * scale
            k = k_ref[sl, :].astype(F32)
            v = v_ref[sl, :]
            qgt = (q * jnp.exp(b)).astype(BF16)
            qgn = (q * jnp.exp(b - bmid)).astype(BF16)
            kgn = (k * jnp.exp(bmid - b)).astype(BF16)
            kd = (k * jnp.exp(blast - b)).astype(BF16)
            a = jnp.where(causal, _dg(qgn, kgn, 1, 1), 0.0)
            st = state[...]
            st_ref[0, c] = st
            o_ref[sl, :] = _dg(a.astype(BF16), v, 1, 0) + _dg(qgt, st.astype(BF16), 1, 1)
            state[...] = jnp.exp(blast) * st + _dg(v, kd, 0, 0)

    res = pl.pallas_call(
        body, name="gla_fwd", grid=(GLA_H, nb),
        in_specs=[pl.BlockSpec((tb, GLA_DK), lambda h, t: (t, P_GQ // GLA_DK + h)),
                  pl.BlockSpec((tb, GLA_DK), lambda h, t: (t, P_GK // GLA_DK + h)),
                  pl.BlockSpec((tb, GLA_DV), lambda h, t: (t, P_GV // GLA_DV + h)),
                  pl.BlockSpec((tb, GLA_DK), lambda h, t: (t, h))] + [HBM] * len(c_ins),
        out_specs=[pl.BlockSpec((tb, GLA_DV), lambda h, t: (t, h)),
                   pl.BlockSpec((1, nc, GLA_DV, GLA_DK), lambda h, t: (h, t, 0, 0))] + [HBM] * len(c_outs),
        out_shape=[jax.ShapeDtypeStruct((s, GLA_H * GLA_DV), F32),
                   jax.ShapeDtypeStruct((GLA_H, s // ch, GLA_DV, GLA_DK), F32)] + c_outs,
        scratch_shapes=[pltpu.VMEM((GLA_DV, GLA_DK), F32)] + c_scratch,
        input_output_aliases=c_alias,
        compiler_params=_params(("arbitrary", "arbitrary") if comm else ("parallel", "arbitrary")),
    )(p, p, p, la, *c_ins)
    return res[0], res[1], _split_units(comm, res[2:])


def _gla_bwd(p, la, states, do, s, dp, comm=()):
    tb, ch = GLA_BLOCK, GLA_CHUNK
    nb, nc = s // tb, tb // ch
    scale = GLA_DK ** -0.5
    c_ins, c_outs, c_alias, c_scratch = _carry(comm, 7, 4)

    def body(q_ref, k_ref, v_ref, la_ref, st_ref, do_ref, dp_in, *rest):
        ci, outs = rest[:len(c_ins)], rest[len(c_ins):len(c_ins) + 4]
        co, dstate = rest[len(c_ins) + 4:len(c_ins) + 4 + len(c_outs)], rest[len(c_ins) + 4 + len(c_outs)]
        step = pl.program_id(0) * nb + pl.program_id(1)
        if comm:
            @pl.when(step == 0)
            def _():
                _comm_phase(comm, ci, co, rest[-2], rest[-1], True)

        _gla_bwd_step(q_ref, k_ref, v_ref, la_ref, st_ref, do_ref, *outs, dstate)
        if comm:
            @pl.when(step == GLA_H * nb - 1)
            def _():
                _comm_phase(comm, ci, co, rest[-2], rest[-1], False)

    def _gla_bwd_step(q_ref, k_ref, v_ref, la_ref, st_ref, do_ref, dq_ref, dk_ref, dv_ref, dla_ref, dstate):
        @pl.when(pl.program_id(1) == 0)
        def _():
            dstate[...] = jnp.zeros_like(dstate)

        ri = lax.broadcasted_iota(jnp.int32, (ch, ch), 0)
        ci = lax.broadcasted_iota(jnp.int32, (ch, ch), 1)
        causal = ci <= ri
        tri = causal.astype(F32)
        tri_t = (ci >= ri).astype(F32)
        for c in reversed(range(nc)):
            sl = pl.ds(c * ch, ch)
            b, bmid, blast = _gla_decays(la_ref[sl, :], tri)
            q = q_ref[sl, :].astype(F32) * scale
            k = k_ref[sl, :].astype(F32)
            v = v_ref[sl, :]
            e_b, e_qn, e_kn, e_kd = jnp.exp(b), jnp.exp(b - bmid), jnp.exp(bmid - b), jnp.exp(blast - b)
            dec = jnp.exp(blast)
            qgt, qgn, kgn, kd = q * e_b, q * e_qn, k * e_kn, k * e_kd
            qgt_b, qgn_b, kgn_b, kd_b = qgt.astype(BF16), qgn.astype(BF16), kgn.astype(BF16), kd.astype(BF16)
            st0 = st_ref[0, c]
            dst = dstate[...]
            dst_b = dst.astype(BF16)
            do_b = do_ref[sl, :].astype(BF16)
            a = jnp.where(causal, _dg(qgn_b, kgn_b, 1, 1), 0.0).astype(BF16)
            da = jnp.where(causal, _dg(do_b, v, 1, 1), 0.0).astype(BF16)
            dqgn = _dg(da, kgn_b, 1, 0)
            dqgt = _dg(do_b, st0.astype(BF16), 1, 0)
            dkgn = _dg(da, qgn_b, 0, 0)
            dv = _dg(a, do_b, 0, 0) + _dg(kd_b, dst_b, 1, 1)
            dkd = _dg(v, dst_b, 1, 0)
            ddec = jnp.sum(st0 * dst, axis=0, keepdims=True)
            dstate[...] = dec * dst + _dg(do_b, qgt_b, 0, 0)
            dq_ref[sl, :] = (scale * (dqgn * e_qn + dqgt * e_b)).astype(dq_ref.dtype)
            dk_ref[sl, :] = (dkgn * e_kn + dkd * e_kd).astype(dk_ref.dtype)
            dv_ref[sl, :] = dv.astype(dv_ref.dtype)
            db = dqgn * qgn + dqgt * qgt - dkgn * kgn - dkd * kd
            extra = jnp.sum(dkd * kd, axis=0, keepdims=True) + ddec * dec
            dla_ref[sl, :] = jnp.dot(tri_t, db, precision=lax.Precision.HIGHEST, preferred_element_type=F32) + extra

    rev = lambda t: nb - 1 - t
    res = pl.pallas_call(
        body, name="gla_bwd", grid=(GLA_H, nb),
        in_specs=[pl.BlockSpec((tb, GLA_DK), lambda h, t: (rev(t), P_GQ // GLA_DK + h)),
                  pl.BlockSpec((tb, GLA_DK), lambda h, t: (rev(t), P_GK // GLA_DK + h)),
                  pl.BlockSpec((tb, GLA_DV), lambda h, t: (rev(t), P_GV // GLA_DV + h)),
                  pl.BlockSpec((tb, GLA_DK), lambda h, t: (rev(t), h)),
                  pl.BlockSpec((1, nc, GLA_DV, GLA_DK), lambda h, t: (h, rev(t), 0, 0)),
                  pl.BlockSpec((tb, GLA_DV), lambda h, t: (rev(t), h)), pl.BlockSpec(memory_space=pl.ANY)] + [HBM] * len(c_ins),
        out_specs=[pl.BlockSpec((tb, GLA_DK), lambda h, t: (rev(t), h)),
                   pl.BlockSpec((tb, GLA_DK), lambda h, t: (rev(t), h)),
                   pl.BlockSpec((tb, GLA_DV), lambda h, t: (rev(t), P_GV // GLA_DV + h)),
                   pl.BlockSpec((tb, GLA_DK), lambda h, t: (rev(t), h))] + [HBM] * len(c_outs),
        out_shape=[jax.ShapeDtypeStruct((s, GLA_H * GLA_DK), BF16),
                   jax.ShapeDtypeStruct((s, GLA_H * GLA_DK), BF16),
                   jax.ShapeDtypeStruct((s, P_W), BF16),
                   jax.ShapeDtypeStruct((s, GLA_H * GLA_DK), F32)] + c_outs,
        scratch_shapes=[pltpu.VMEM((GLA_DV, GLA_DK), F32)] + c_scratch,
        input_output_aliases={6: 2, **c_alias},
        compiler_params=_params(("arbitrary", "arbitrary") if comm else ("parallel", "arbitrary")),
    )(p, p, p, la, states, do, dp, *c_ins)
    return res[0], res[1], res[2], res[3], _split_units(comm, res[4:])


def _head_masks():
    lane = lax.broadcasted_iota(jnp.int32, (1, 4 * ATT_HD), 1)
    return [(lane >= h * ATT_HD) & (lane < (h + 1) * ATT_HD) for h in range(4)]


def _attn_fwd(qv, kv, pv, g, r, s):
    ln = s // r
    nblk = ln // ATT_BLK
    qcol = lambda pr: pr
    vcol = qcol
    prev = lambda n: jnp.maximum(n - 1, 0)

    def body(q_ref, kp_ref, kc_ref, vp_ref, vc_ref, o_ref, lse_ref):
        has_prev = pl.program_id(1) > 0
        ri = lax.broadcasted_iota(jnp.int32, (ATT_BLK, ATT_BLK), 0)
        ci = lax.broadcasted_iota(jnp.int32, (ATT_BLK, ATT_BLK), 1)
        m_cur = ci <= ri
        m_prev = (ci >= ri) & has_prev
        q, kp, kc, vp, vc = q_ref[...], kp_ref[...], kc_ref[...], vp_ref[...], vc_ref[...]
        o = jnp.zeros((ATT_BLK, 256), F32)
        lse = jnp.zeros((ATT_BLK, 256), F32)
        for hm in _head_masks():
            qm = jnp.where(hm, q, jnp.zeros_like(q))
            sc = jnp.where(m_cur, _dg(qm, kc, 1, 1) * 0.125, NEG)
            sp = jnp.where(m_prev, _dg(qm, kp, 1, 1) * 0.125, NEG)
            mx = jnp.maximum(jnp.max(sc, axis=1, keepdims=True), jnp.max(sp, axis=1, keepdims=True))
            pc, pp = jnp.exp(sc - mx), jnp.exp(sp - mx)
            den = jnp.sum(pc, axis=1, keepdims=True) + jnp.sum(pp, axis=1, keepdims=True)
            oh = (_dg(pc.astype(BF16), vc, 1, 0) + _dg(pp.astype(BF16), vp, 1, 0)) / den
            o = jnp.where(hm, oh, o)
            lse = jnp.where(hm, mx + jnp.log(den), lse)
        o_ref[...] = o.astype(o_ref.dtype)
        lse_ref[...] = lse

    blk = (ATT_BLK, 256)
    o, lse = pl.pallas_call(
        body, name=f"attn_fwd_{g}", grid=(r, nblk),
        in_specs=[pl.BlockSpec(blk, lambda pr, n: (n, qcol(pr))),
                  pl.BlockSpec(blk, lambda pr, n: (prev(n), qcol(pr))),
                  pl.BlockSpec(blk, lambda pr, n: (n, qcol(pr))),
                  pl.BlockSpec(blk, lambda pr, n: (prev(n), vcol(pr))),
                  pl.BlockSpec(blk, lambda pr, n: (n, vcol(pr)))],
        out_specs=[pl.BlockSpec(blk, lambda pr, n: (n, pr)), pl.BlockSpec(blk, lambda pr, n: (n, pr))],
        out_shape=[jax.ShapeDtypeStruct((ln, r * 256), BF16), jax.ShapeDtypeStruct((ln, r * 256), F32)],
        compiler_params=_params(("parallel", "parallel")),
    )(qv, kv, kv, pv, pv)
    return o, lse


def _attn_bwd(qv, kv, pv, dov, ov, lv, g, r, s):
    ln = s // r
    nblk = ln // ATT_BLK
    qcol = lambda pr: pr
    vcol = qcol
    prev = lambda n: jnp.maximum(n - 1, 0)
    nxt = lambda n: jnp.minimum(n + 1, nblk - 1)

    def body(qc_ref, qn_ref, kp_ref, kc_ref, vp_ref, vc_ref, doc_ref, don_ref, oc_ref, on_ref, lc_ref, ln_ref,
             dq_ref, dk_ref, dv_ref):
        n = pl.program_id(1)
        has_prev, has_next = n > 0, n < nblk - 1
        ri = lax.broadcasted_iota(jnp.int32, (ATT_BLK, ATT_BLK), 0)
        ci = lax.broadcasted_iota(jnp.int32, (ATT_BLK, ATT_BLK), 1)
        m_cur = ci <= ri
        m_prev = (ci >= ri) & has_prev
        m_next = (ci >= ri) & has_next
        qc, qn, kp, kc, vp, vc = qc_ref[...], qn_ref[...], kp_ref[...], kc_ref[...], vp_ref[...], vc_ref[...]
        doc, don = doc_ref[...], don_ref[...]
        pc_full = doc.astype(F32) * oc_ref[...].astype(F32)
        pn_full = don.astype(F32) * on_ref[...].astype(F32)
        lc, lnx = lc_ref[...], ln_ref[...]
        dq = jnp.zeros((ATT_BLK, 256), F32)
        dk = jnp.zeros((ATT_BLK, 256), F32)
        dv = jnp.zeros((ATT_BLK, 256), F32)
        zb = jnp.zeros_like(qc)
        for hm in _head_masks():
            qcm, qnm = jnp.where(hm, qc, zb), jnp.where(hm, qn, zb)
            docm, donm = jnp.where(hm, doc, zb), jnp.where(hm, don, zb)
            lse_c = jnp.max(jnp.where(hm, lc, NEG), axis=1, keepdims=True)
            lse_n = jnp.max(jnp.where(hm, lnx, NEG), axis=1, keepdims=True)
            del_c = jnp.sum(jnp.where(hm, pc_full, 0.0), axis=1, keepdims=True)
            del_n = jnp.sum(jnp.where(hm, pn_full, 0.0), axis=1, keepdims=True)
            pr_ = jnp.where(m_cur, jnp.exp(_dg(qcm, kc, 1, 1) * 0.125 - lse_c), 0.0)
            ds = (pr_ * (_dg(docm, vc, 1, 1) - del_c) * 0.125).astype(BF16)
            dqh = _dg(ds, kc, 1, 0)
            dkh = _dg(ds, qc, 0, 0)
            dvh = _dg(pr_.astype(BF16), doc, 0, 0)
            pr_ = jnp.where(m_prev, jnp.exp(_dg(qcm, kp, 1, 1) * 0.125 - lse_c), 0.0)
            ds = (pr_ * (_dg(docm, vp, 1, 1) - del_c) * 0.125).astype(BF16)
            dqh = dqh + _dg(ds, kp, 1, 0)
            pr_ = jnp.where(m_next, jnp.exp(_dg(qnm, kc, 1, 1) * 0.125 - lse_n), 0.0)
            ds = (pr_ * (_dg(donm, vc, 1, 1) - del_n) * 0.125).astype(BF16)
            dkh = dkh + _dg(ds, qn, 0, 0)
            dvh = dvh + _dg(pr_.astype(BF16), don, 0, 0)
            dq = jnp.where(hm, dqh, dq)
            dk = jnp.where(hm, dkh, dk)
            dv = jnp.where(hm, dvh, dv)
        dq_ref[...] = dq.astype(dq_ref.dtype)
        dk_ref[...] = dk.astype(dk_ref.dtype)
        dv_ref[...] = dv.astype(dv_ref.dtype)

    blk = (ATT_BLK, 256)
    cur = lambda col: pl.BlockSpec(blk, lambda pr, n: (n, col(pr)))
    prv = lambda col: pl.BlockSpec(blk, lambda pr, n: (prev(n), col(pr)))
    nx = lambda col: pl.BlockSpec(blk, lambda pr, n: (nxt(n), col(pr)))
    own = lambda pr: pr
    outs = pl.pallas_call(
        body, name=f"attn_bwd_{g}", grid=(r, nblk),
        in_specs=[cur(qcol), nx(qcol), prv(qcol), cur(qcol), prv(vcol), cur(vcol),
                  cur(own), nx(own), cur(own), nx(own), cur(own), nx(own)],
        out_specs=[cur(own), cur(own), cur(own)],
        out_shape=[jax.ShapeDtypeStruct((ln, r * 256), BF16)] * 3,
        compiler_params=_params(("parallel", "parallel")),
    )(qv, qv, kv, kv, pv, pv, dov, dov, ov, ov, lv, lv)
    return outs


def _gelu_parts(gv):
    cdf = 0.5 * (1.0 + lax.erf(gv * (2.0 ** -0.5)))
    pdf = jnp.exp(-0.5 * gv * gv) * (1.0 / math.sqrt(2.0 * math.pi))
    return cdf, pdf


def _pick_row(t, k):
    row = lax.broadcasted_iota(jnp.int32, t.shape, 0)
    return jnp.sum(jnp.where(row == k, t, 0.0), axis=0, keepdims=True)


def _shift_rows(u, halo, n):
    row = lax.broadcasted_iota(jnp.int32, u.shape, 0)
    out = pltpu.roll(u, n, 0)
    for k in range(n):
        out = jnp.where(row == k, _pick_row(halo, 16 - n + k), out)
    return out


def _shift_rows_up(u, halo, n):
    rb = u.shape[0]
    row = lax.broadcasted_iota(jnp.int32, u.shape, 0)
    out = pltpu.roll(u, rb - n, 0)
    for k in range(n):
        out = jnp.where(row == rb - n + k, _pick_row(halo, k), out)
    return out


def _conv(u, halo, cw, cb):
    return cb + _pick_row(cw, 0) * _shift_rows(u, halo, 2) + _pick_row(cw, 1) * _shift_rows(u, halo, 1) + _pick_row(cw, 2) * u


def _local_step(x, mod, pos_col, target, sm, w_sh, chip, core):
    s = x.shape[0]
    shift1, scale1, gate1, shift2, scale2, gate2 = [mod[i:i + 1, :] for i in range(6)]
    rb = 256
    chip1 = chip.reshape(1)

    def f_norm1(c, i, xv, nw, sc, sh):
        return ((xv * _rms(xv) * nw) * (1.0 + sc) + sh,)

    (h,) = _rowcall(f_norm1, [_rows(x, rb), _full(sm["n1w"]), _full(scale1), _full(shift1)],
                    [_orow(s, D, BF16, rb)], n_rows=s, rb=rb, name="norm1")
    own = lambda got, i: lax.dynamic_update_slice(got, w_sh[i], (chip, 0, 0))
    [got0] = _comm_call("gather_w_in_ici", [_u_gather_ici(w_sh, (0,))])
    [got0] = _comm_call("gather_w_in_d2d", [_u_gather_d2d(got0, (0,))])
    w = dict(win=_win_assemble(own(got0[0], 0)))
    p, [got123] = _mm(h, w["win"], "in_proj", tm=1024, tn=1536, comm=[_u_gather_ici(w_sh, (1, 2, 3))])

    def f_gla_pre(c, i, glr, w2, gb):
        z = _dg(glr, w2.astype(BF16), 1, 0) + gb
        return ((jnp.minimum(z, 0.0) - jnp.log(1.0 + jnp.exp(-jnp.abs(z)))) * (1.0 / GLA_TAU),)

    (la,) = _rowcall(f_gla_pre, [_rows(p, rb, 128, P_LR // 128), _full(sm["w2"]), _full(sm["gb"])],
                     [_orow(s, 512, F32, rb)], n_rows=s, rb=rb, name="gla_pre")
    o_gla, states, [got123, got45] = _gla_fwd(p, la, s, comm=[_u_gather_d2d(got123, (1, 2, 3)), _u_gather_ici(w_sh, (4, 5))])
    w.update(wgb=own(got123[0], 1).reshape(1024, D), wab=_cols_join(own(got123[1], 2)), wout=own(got123[2], 3).reshape(D, D))

    def f_gla_post(c, i, ov, gnw, gr):
        on = jnp.concatenate([ov[:, k * 256:(k + 1) * 256] * _rms(ov[:, k * 256:(k + 1) * 256]) * gnw
                              for k in range(GLA_H)], axis=1)
        g = gr.astype(F32)
        return (on * (g * _sigmoid(g)),)

    (og,) = _rowcall(f_gla_post, [_rows(o_gla, rb), _full(sm["gnw"]), _rows(p, rb, 1024, P_GR // 1024)],
                     [_orow(s, 1024, BF16, rb)], n_rows=s, rb=rb, name="gla_post")
    y_gla = _mm(og, w["wgb"], "gla_branch")

    invf = jnp.tile(ROPE_THETA ** (-jnp.arange(ATT_HD // 2, dtype=F32) / (ATT_HD // 2)), 4).reshape(1, 128)
    cos_t, sin_t = _rope_tables(pos_col, invf, s)

    q_d, k_d, v_d = _rope_fwd(p, cos_t, sin_t, s)
    att = [_attn_fwd(q_d[g], k_d[g], v_d[g], g, r, s) for g, r in enumerate(_RS)]
    o_att, lse, o_d1, o_d2, lse_d1, lse_d2 = _attn_combine(att, s)
    y_att = _mm(o_att, w["wab"], "attn_branch")

    def f_merge(c, i, ma, mb, yg, ya):
        return (_sigmoid(ma.astype(F32)) * yg.astype(F32) + _sigmoid(mb.astype(F32)) * ya.astype(F32),)

    (mixed,) = _rowcall(f_merge, [_rows(p, rb, D, P_MA // D), _rows(p, rb, D, P_MB // D), _rows(y_gla, rb), _rows(y_att, rb)],
                        [_orow(s, D, BF16, rb)], n_rows=s, rb=rb, name="merge")
    z1, [got45] = _mm(mixed, w["wout"], "out_proj", comm=[_u_gather_d2d(got45, (4, 5))])
    w.update(wup=own(got45[0], 4), wdown=own(got45[1], 5).reshape(D_FF, D))

    def f_norm2(c, i, xv, z, g1, nw, sc, sh):
        x1 = xv + g1 * z.astype(F32)
        return (x1, (x1 * _rms(x1) * nw) * (1.0 + sc) + sh)

    x1, h2 = _rowcall(f_norm2, [_rows(x, rb), _rows(z1, rb), _full(gate1), _full(sm["n2w"]), _full(scale2), _full(shift2)],
                      [_orow(s, D, F32, rb), _orow(s, D, BF16, rb)], n_rows=s, rb=rb, name="norm2")
    u = _mm(h2, w["wup"], "up_proj", b_shards=True)

    cwid = 2 * W_UP_SH

    def f_ffn(c, i, uv, hl, cw, cb):
        uc = _conv(uv.astype(F32), hl.astype(F32) * (i > 0).astype(F32), cw, cb)
        val, gt = uc[:, :W_UP_SH], uc[:, W_UP_SH:]
        cdf, _ = _gelu_parts(gt)
        return (gt * cdf * val,)

    ccol = lambda c: c
    (hidden,) = _rowcall(f_ffn, [_rows(u, rb, cwid, ccol), _halo(u, rb, 16, cwid, ccol, True),
                                 _full(sm["cw"], cwid, ccol), _full(sm["cb"], cwid, ccol)],
                         [_orow(s, D_FF, BF16, rb, W_UP_SH, ccol)], n_rows=s, rb=rb, name="conv_geglu", ncol=2)
    z2 = _mm(hidden, w["wdown"], "down_proj", tk=1408)

    def f_final(c, i, x1v, z, g2, fw, tgt):
        x2 = x1v + g2 * z.astype(F32)
        r = _rms(x2)
        xh = x2 * r
        e = xh * fw - tgt
        loss = 0.5 * jnp.sum(jnp.mean(e * e, axis=-1, keepdims=True), axis=0, keepdims=True)
        dy = e * (1.0 / D)
        dxh = dy * fw
        dx2 = r * (dxh - xh * jnp.mean(dxh * xh, axis=-1, keepdims=True))
        return (loss, dx2, dx2 * g2, _csum(dy * xh), _csum(dx2 * z.astype(F32)))

    loss, dx2, dz2, d_fnw, d_gate2 = _rowcall(
        f_final, [_rows(x1, rb), _rows(z2, rb), _full(gate2), _full(sm["fnw"]), _rows(target, rb)],
        [_oacc(1, 1), _orow(s, D, F32, rb), _orow(s, D, BF16, rb), _oacc(1, D), _oacc(1, D)],
        n_rows=s, rb=rb, name="final_loss")
    d_hidden = _mm(dz2, w["wdown"], "down_proj_dx", tb=True, tn=1408)
    g_wdown = _mm(hidden, dz2, "down_proj_dw", ta=True, out_dtype=F32, tm=1408, tn=1024, tk=2048)

    def f_ffn_bwd(c, i, uv, hl, dh, cw, cb):
        uf = uv.astype(F32)
        hf = hl.astype(F32) * (i > 0).astype(F32)
        u1, u2 = _shift_rows(uf, hf, 1), _shift_rows(uf, hf, 2)
        uc = cb + _pick_row(cw, 0) * u2 + _pick_row(cw, 1) * u1 + _pick_row(cw, 2) * uf
        val, gt = uc[:, :W_UP_SH], uc[:, W_UP_SH:]
        cdf, pdf = _gelu_parts(gt)
        dhf = dh.astype(F32)
        duc = jnp.concatenate([dhf * (gt * cdf), dhf * val * (cdf + gt * pdf)], axis=1)
        dcw = jnp.concatenate([_csum(duc * u2), _csum(duc * u1), _csum(duc * uf)], axis=0)
        return (duc, _csum(duc), dcw)

    duc, d_cb, d_cw = _rowcall(
        f_ffn_bwd, [_rows(u, rb, cwid, ccol), _halo(u, rb, 16, cwid, ccol, True), _rows(d_hidden, rb, W_UP_SH, ccol),
                    _full(sm["cw"], cwid, ccol), _full(sm["cb"], cwid, ccol)],
        [_orow(s, 2 * D_FF, BF16, rb, cwid, ccol), _oacc(1, 2 * D_FF, cwid, ccol), _oacc(3, 2 * D_FF, cwid, ccol)],
        n_rows=s, rb=rb, name="conv_geglu_bwd", ncol=2)

    def f_conv_t(c, i, dv, hl, cw):
        df = dv.astype(F32)
        hf = hl.astype(F32) * (i < s // rb - 1).astype(F32)
        return (_pick_row(cw, 2) * df + _pick_row(cw, 1) * _shift_rows_up(df, hf, 1) + _pick_row(cw, 0) * _shift_rows_up(df, hf, 2),)

    (du,) = _rowcall(f_conv_t, [_rows(duc, rb, cwid, ccol), _halo(duc, rb, 16, cwid, ccol, False), _full(sm["cw"], cwid, ccol)],
                     [_orow(s, 2 * D_FF, BF16, rb, cwid, ccol)], n_rows=s, rb=rb, name="conv_transpose", ncol=2)
    g_wup = _mm(h2, du, "up_proj_dw", ta=True, out_dtype=F32, tm=1024, tk=2048, o_shards=True)
    gs45 = [g_wup, g_wdown.reshape(4, W_DOWN_SH, 1024)]
    d_h2, [land45] = _mm(du, w["wup"], "up_proj_dx", tb=True, b_shards=True, comm=[_u_pair_send(gs45, (4, 5))])
    ts45 = [_pair_add(g, ld, core, "grad_pair_add_" + BIG[i]) for g, ld, i in zip(gs45, land45, (4, 5))]

    def f_norm2_bwd(c, i, x1v, dh, dxr, z, nw, sc, g1):
        dxn, dsh, dsc, dnw = _norm_bwd(x1v, dh.astype(F32), nw, sc)
        dx1 = dxr + dxn
        return (dx1, dx1 * g1, dsh, dsc, dnw, _csum(dx1 * z.astype(F32)))

    dx1, dz1, d_shift2, d_scale2, d_n2w, d_gate1 = _rowcall(
        f_norm2_bwd, [_rows(x1, rb), _rows(d_h2, rb), _rows(dx2, rb), _rows(z1, rb), _full(sm["n2w"]), _full(scale2), _full(gate1)],
        [_orow(s, D, F32, rb), _orow(s, D, BF16, rb), _oacc(1, D), _oacc(1, D), _oacc(1, D), _oacc(1, D)],
        n_rows=s, rb=rb, name="norm2_bwd")
    d_mixed = _mm(dz1, w["wout"], "out_proj_dx", tb=True)
    g_wout = _mm(mixed, dz1, "out_proj_dw", ta=True, out_dtype=F32, tk=2048)

    def f_merge_bwd(c, i, dm, ma, mb, yg, ya):
        dmf, ygf, yaf = dm.astype(F32), yg.astype(F32), ya.astype(F32)
        sa, sb = _sigmoid(ma.astype(F32)), _sigmoid(mb.astype(F32))
        return (dmf * sa, dmf * sb, jnp.concatenate([dmf * ygf * sa * (1.0 - sa), dmf * yaf * sb * (1.0 - sb)], axis=1))

    dy_gla, dy_att, dp = _rowcall(
        f_merge_bwd, [_rows(d_mixed, rb), _rows(p, rb, D, P_MA // D), _rows(p, rb, D, P_MB // D), _rows(y_gla, rb), _rows(y_att, rb)],
        [_orow(s, D, BF16, rb)] * 2 + [_orow(s, P_W, BF16, rb, 2 * D, lambda c: P_MA // (2 * D))], n_rows=s, rb=rb, name="merge_bwd")
    d_og = _mm(dy_gla, w["wgb"], "gla_branch_dx", tb=True)
    g_wgb = _mm(og, dy_gla, "gla_branch_dw", ta=True, out_dtype=F32, tk=2048)
    d_oatt = _mm(dy_att, w["wab"], "attn_branch_dx", tb=True)
    g_wab = _mm(o_att, dy_att, "attn_branch_dw", ta=True, out_dtype=F32, tk=2048)

    def f_gla_post_bwd(c, i, ov, gnw, gr, dog):
        g = gr.astype(F32)
        sg = _sigmoid(g)
        silu = g * sg
        dof = dog.astype(F32)
        don = dof * silu
        on_parts, do_parts, dgn = [], [], jnp.zeros((1, 256), F32)
        for k in range(GLA_H):
            oh = ov[:, k * 256:(k + 1) * 256]
            dh = don[:, k * 256:(k + 1) * 256]
            r = _rms(oh)
            xh = oh * r
            dgn = dgn + _csum(dh * xh)
            dxh = dh * gnw
            do_parts.append(r * (dxh - xh * jnp.mean(dxh * xh, axis=-1, keepdims=True)))
            on_parts.append(xh * gnw)
        on = jnp.concatenate(on_parts, axis=1)
        dgr = dof * on * (sg * (1.0 + g * (1.0 - sg)))
        return (jnp.concatenate(do_parts, axis=1), dgr, dgn)

    do_gla, dp, d_gnw = _rowcall(
        f_gla_post_bwd, [_rows(o_gla, rb), _full(sm["gnw"]), _rows(p, rb, 1024, P_GR // 1024), _rows(d_og, rb)],
        [_orow(s, 1024, F32, rb), _orow(s, P_W, BF16, rb, 1024, lambda c: P_GR // 1024), _oacc(1, 256)],
        n_rows=s, rb=rb, name="gla_post_bwd", into=(dp, 1))
    gs123 = [g_wgb.reshape(4, 256, 1024), _cols_split(g_wab), g_wout.reshape(4, 256, 1024)]
    d_gq, d_gk, dp, d_la, [r45, land123] = _gla_bwd(p, la, states, do_gla, s, dp,
                                                    comm=[_u_chip_exchange(ts45), _u_pair_send(gs123, (1, 2, 3))])
    half45 = [_chip_sum(t, r, chip1, "grad_chip_sum_" + BIG[i]) for t, r, i in zip(ts45, r45, (4, 5))]
    ts123 = [_pair_add(g, ld, core, "grad_pair_add_" + BIG[i]) for g, ld, i in zip(gs123, land123, (1, 2, 3))]

    def f_gla_pre_bwd(c, i, lav, dlav, glr, w2):
        dz = dlav * (1.0 / GLA_TAU) * (1.0 - jnp.exp(GLA_TAU * lav))
        dzb = dz.astype(BF16)
        return (_dg(dzb, w2.astype(BF16), 1, 1), _csum(dz), _dg(glr, dzb, 0, 0))

    d_glr, d_gb, d_w2 = _rowcall(
        f_gla_pre_bwd, [_rows(la, rb), _rows(d_la, rb), _rows(p, rb, 128, P_LR // 128), _full(sm["w2"])],
        [_orow(s, 128, BF16, rb), _oacc(1, 512), _oacc(128, 512)], n_rows=s, rb=rb, name="gla_pre_bwd")

    do_d = [d_oatt] + list(_dilate(d_oatt, s))
    datt = [_attn_bwd(q_d[g], k_d[g], v_d[g], do_d[g], (o_att, o_d1, o_d2)[g], (lse, lse_d1, lse_d2)[g], g, r, s)
            for g, r in enumerate(_RS)]
    dp = _rope_bwd(datt, d_glr, dp, cos_t, sin_t, s)
    dp = lax.dynamic_update_slice(dp, jnp.concatenate([d_gq, d_gk], axis=1), (0, P_GQ))
    g_win, [r123, oth45] = _mm(h, dp, "in_proj_dw", ta=True, out_dtype=F32, tm=1024, tn=1536, tk=2048,
                               comm=[_u_chip_exchange(ts123), _u_pair_join(half45)])
    half123 = [_chip_sum(t, r, chip1, "grad_chip_sum_" + BIG[i]) for t, r, i in zip(ts123, r123, (1, 2, 3))]
    gs0 = [_win_split(g_win)]
    d_h, [land0, oth123] = _mm(dp, w["win"], "in_proj_dx", tb=True, tk=1536,
                               comm=[_u_pair_send(gs0, (0,)), _u_pair_join(half123)])
    ts0 = [_pair_add(gs0[0], land0[0], core, "grad_pair_add_w_in")]
    [r0] = _comm_call("grad_exchange_w_in", [_u_chip_exchange(ts0)])
    half0 = [_chip_sum(ts0[0], r0[0], chip1, "grad_chip_sum_w_in")]
    [oth0] = _comm_call("grad_join_w_in", [_u_pair_join(half0)])

    def f_norm1_bwd(c, i, xv, dh, dxr, nw, sc):
        dxn, dsh, dsc, dnw = _norm_bwd(xv, dh.astype(F32), nw, sc)
        return (dxr + dxn, dsh, dsc, dnw)

    grad_x, d_shift1, d_scale1, d_n1w = _rowcall(
        f_norm1_bwd, [_rows(x, rb), _rows(d_h, rb), _rows(dx1, rb), _full(sm["n1w"]), _full(scale1)],
        [_orow(s, D, F32, rb), _oacc(1, D), _oacc(1, D), _oacc(1, D)], n_rows=s, rb=rb, name="norm1_bwd")

    dmod = jnp.concatenate([d_shift1, d_scale1, d_gate1, d_shift2, d_scale2, d_gate2], axis=1)
    small = dict(dmod=dmod, n1w=d_n1w, gb=d_gb, gnw=d_gnw, n2w=d_n2w, cb=d_cb, fnw=d_fnw, w2=d_w2, cw=d_cw)
    return loss, grad_x, half0 + half123 + half45, oth0 + oth123 + oth45, small


def _win_pieces():
    runs = [(P_GV, 1024, 2048), (P_MA, 5392, 2048), (P_GQ, 0, 1024), (P_AQ, 3088, 2304), (P_LR, 3072, GLA_LR)]
    out = []
    for kc, rc, ln in runs:
        while ln > 0:
            step = min(ln, W_IN_SH - rc % W_IN_SH)
            out.append((kc, rc, step))
            kc, rc, ln = kc + step, rc + step, ln - step
    return out


def _win_assemble(shards):
    rb = 256

    def body(s_ref, o_ref):
        o_ref[:, W_IN:] = jnp.zeros((rb, P_W - W_IN), o_ref.dtype)
        for kc, rc, ln in _win_pieces():
            o_ref[:, kc:kc + ln] = s_ref[rc // W_IN_SH, :, rc % W_IN_SH:rc % W_IN_SH + ln]

    return pl.pallas_call(
        body, name="w_in_assemble", grid=(D // rb,),
        in_specs=[pl.BlockSpec((4, rb, W_IN_SH), lambda i: (0, i, 0))], out_specs=pl.BlockSpec((rb, P_W), lambda i: (i, 0)),
        out_shape=jax.ShapeDtypeStruct((D, P_W), shards.dtype), compiler_params=_params(("parallel",)),
    )(shards)


def _win_split(g):
    rb = 256

    def body(g_ref, o_ref):
        for kc, rc, ln in _win_pieces():
            o_ref[rc // W_IN_SH, :, rc % W_IN_SH:rc % W_IN_SH + ln] = g_ref[:, kc:kc + ln]

    return pl.pallas_call(
        body, name="w_in_grad_split", grid=(D // rb,),
        in_specs=[pl.BlockSpec((rb, P_W), lambda i: (i, 0))], out_specs=pl.BlockSpec((4, rb, W_IN_SH), lambda i: (0, i, 0)),
        out_shape=jax.ShapeDtypeStruct((4, D, W_IN_SH), g.dtype), compiler_params=_params(("parallel",)),
    )(g)


def _ff_to_kernel(a):
    h = W_UP_SH
    return jnp.concatenate([a[:, 0:h], a[:, D_FF:D_FF + h], a[:, h:D_FF], a[:, D_FF + h:]], axis=1)


def _ff_from_kernel(a):
    h = W_UP_SH
    return jnp.concatenate([a[:, 0:h], a[:, 2 * h:3 * h], a[:, h:2 * h], a[:, 3 * h:]], axis=1)


BIG = ("w_in", "w_gla_branch", "w_attn_branch", "w_out", "w_up", "w_down")
SH_SHAPES = ((1024, W_IN_SH), (256, 1024), (256, 256), (256, 1024), (1024, W_UP_SH), (W_DOWN_SH, 1024))
N_BIG = len(BIG)


def _cols_join(t):
    return jnp.concatenate([t[k] for k in range(4)], axis=1)


def _cols_split(t):
    cols = t.shape[1] // 4
    return jnp.stack([t[:, k * cols:(k + 1) * cols] for k in range(4)])


def _me():
    return lax.axis_index("x"), lax.axis_index("y"), lax.axis_index("c")


HBM = pl.BlockSpec(memory_space=pltpu.HBM)
VMEM_SPEC = pl.BlockSpec(memory_space=pltpu.VMEM)


def _allgather8(xs, name):
    rows = xs.shape[0]

    def body(x_ref, out_ref, send_sems, recv_sems, local_sem):
        x, y, c = _me()
        me = 4 * x + 2 * y + c
        mine = pltpu.make_async_copy(x_ref, out_ref.at[me], local_sem)
        mine.start()
        flips = [(k >> 2 & 1, k >> 1 & 1, k & 1) for k in range(1, 8)]

        def peer(f):
            return (jnp.where(f[0] == 1, 1 - x, x), jnp.where(f[1] == 1, 1 - y, y), jnp.where(f[2] == 1, 1 - c, c))

        sends = []
        for k, f in enumerate(flips):
            cp = pltpu.make_async_remote_copy(src_ref=x_ref, dst_ref=out_ref.at[me], send_sem=send_sems.at[k],
                                              recv_sem=recv_sems.at[k], device_id=peer(f), device_id_type=MESH)
            cp.start()
            sends.append(cp)
        for k, f in enumerate(flips):
            px, py, pc = peer(f)
            pltpu.make_async_remote_copy(src_ref=x_ref, dst_ref=out_ref.at[4 * px + 2 * py + pc], send_sem=send_sems.at[k],
                                         recv_sem=recv_sems.at[k], device_id=peer(f), device_id_type=MESH).wait_recv()
        for cp in sends:
            cp.wait_send()
        mine.wait()

    return pl.pallas_call(
        body, name=name, out_shape=jax.ShapeDtypeStruct((8, rows, 128), F32),
        in_specs=[VMEM_SPEC], out_specs=VMEM_SPEC,
        scratch_shapes=[pltpu.SemaphoreType.DMA((7,)), pltpu.SemaphoreType.DMA((7,)), pltpu.SemaphoreType.DMA],
        compiler_params=pltpu.CompilerParams(vmem_limit_bytes=VMEM_LIMIT),
    )(xs)


def _half_rows(i, cc, unit):
    rows = SH_SHAPES[i][0] // 2
    return pl.ds(pl.multiple_of(cc * rows, unit), rows)


def _rc(src, dst, sems, to):
    return pltpu.make_async_remote_copy(src_ref=src, dst_ref=dst, send_sem=sems[0], recv_sem=sems[1], device_id=to, device_id_type=MESH)


def _other_chips(x, y):
    return [(1 - x, y), (x, 1 - y), (1 - x, 1 - y)]


def _u_gather_ici(w_sh, idxs):
    def copies(ins, outs, sem):
        x, y, c = _me()
        res = []
        for j, (px, py) in enumerate(_other_chips(x, y)):
            for n, i in enumerate(idxs):
                src = ins[n].at[0, _half_rows(i, c, 16)]
                res.append((_rc(src, outs[n].at[2 * x + y, _half_rows(i, c, 16)], sem(j * len(idxs) + n), (px, py, c)),
                            _rc(src, outs[n].at[2 * px + py, _half_rows(i, c, 16)], sem(j * len(idxs) + n), (px, py, c))))
        return res

    return dict(ins=[w_sh[i] for i in idxs], outs=[jax.ShapeDtypeStruct((4,) + SH_SHAPES[i], BF16) for i in idxs],
                nsem=3 * len(idxs), alias={}, copies=copies)


def _u_gather_d2d(got, idxs):
    def copies(ins, outs, sem):
        x, y, c = _me()
        res = []
        for j, (px, py) in enumerate(_other_chips(x, y)):
            for n, i in enumerate(idxs):
                src = ins[n].at[2 * px + py, _half_rows(i, c, 16)]
                res.append((_rc(src, outs[n].at[2 * px + py, _half_rows(i, c, 16)], sem(j * len(idxs) + n), (x, y, 1 - c)),
                            _rc(src, outs[n].at[2 * px + py, _half_rows(i, 1 - c, 16)], sem(j * len(idxs) + n), (x, y, 1 - c))))
        return res

    return dict(ins=list(got), outs=[jax.ShapeDtypeStruct(g.shape, g.dtype) for g in got], nsem=3 * len(idxs),
                alias={n: n for n in range(len(idxs))}, copies=copies)


def _u_pair_send(gs, idxs):
    def copies(ins, outs, sem):
        x, y, c = _me()
        res = []
        for n, i in enumerate(idxs):
            for sh in range(4):
                cp = _rc(ins[n].at[sh, _half_rows(i, 1 - c, 8)], outs[n].at[sh], sem(4 * n + sh), (x, y, 1 - c))
                res.append((cp, cp))
        return res

    return dict(ins=list(gs), outs=[jax.ShapeDtypeStruct((4, SH_SHAPES[i][0] // 2, SH_SHAPES[i][1]), F32) for i in idxs],
                nsem=4 * len(idxs), alias={}, copies=copies)


def _u_chip_exchange(ts):
    def copies(ins, outs, sem):
        x, y, c = _me()
        res = []
        for j, (px, py) in enumerate(_other_chips(x, y)):
            for n in range(len(ts)):
                cp = _rc(ins[n].at[2 * px + py], outs[n].at[j], sem(j * len(ts) + n), (px, py, c))
                res.append((cp, cp))
        return res

    return dict(ins=list(ts), outs=[jax.ShapeDtypeStruct((3,) + t.shape[1:], t.dtype) for t in ts], nsem=3 * len(ts),
                alias={}, copies=copies)


def _u_pair_join(hs):
    def copies(ins, outs, sem):
        x, y, c = _me()
        res = []
        for n in range(len(hs)):
            cp = _rc(ins[n], outs[n], sem(n), (x, y, 1 - c))
            res.append((cp, cp))
        return res

    return dict(ins=list(hs), outs=[jax.ShapeDtypeStruct(h.shape, h.dtype) for h in hs], nsem=len(hs), alias={}, copies=copies)


def _comm_phase(units, ci, co, send_sems, recv_sems, start):
    ii = oo = off = 0
    for u in units:
        ni, no = len(u["ins"]), len(u["outs"])
        for st, arrival in u["copies"](ci[ii:ii + ni], co[oo:oo + no], lambda k, off=off: (send_sems.at[off + k], recv_sems.at[off + k])):
            if start:
                st.start()
            else:
                st.wait_send()
                arrival.wait_recv()
        ii, oo, off = ii + ni, oo + no, off + u["nsem"]


def _carry(units, n_in, n_out):
    ins = [a for u in units for a in u["ins"]]
    outs = [o for u in units for o in u["outs"]]
    alias, ii, oo = {}, 0, 0
    for u in units:
        for a, b in u["alias"].items():
            alias[n_in + ii + a] = n_out + oo + b
        ii, oo = ii + len(u["ins"]), oo + len(u["outs"])
    nsem = sum(u["nsem"] for u in units)
    scratch = [pltpu.SemaphoreType.DMA((nsem,)), pltpu.SemaphoreType.DMA((nsem,))] if units else []
    return ins, outs, alias, scratch


def _split_units(units, res):
    out, oo = [], 0
    for u in units:
        out.append(list(res[oo:oo + len(u["outs"])]))
        oo += len(u["outs"])
    return out


def _comm_call(name, units):
    ins, outs, alias, scratch = _carry(units, 0, 0)

    def body(*refs):
        ci, co = refs[:len(ins)], refs[len(ins):len(ins) + len(outs)]
        _comm_phase(units, ci, co, refs[-2], refs[-1], True)
        _comm_phase(units, ci, co, refs[-2], refs[-1], False)

    res = pl.pallas_call(body, name=name, out_shape=outs, in_specs=[HBM] * len(ins), out_specs=[HBM] * len(outs),
                         scratch_shapes=scratch, input_output_aliases=alias)(*ins)
    return _split_units(units, res)


def _pair_add(g, land, core, name):
    _, rows, cols = g.shape
    half = rows // 2
    rb = _tile(half, 256, 16)
    nb = half // rb

    def body(c_ref, g_ref, l_ref, o_ref):
        o_ref[...] = (g_ref[...] + l_ref[...]).astype(BF16)

    return pl.pallas_call(
        body, name=name,
        grid_spec=pltpu.PrefetchScalarGridSpec(
            num_scalar_prefetch=1, grid=(4, nb),
            in_specs=[pl.BlockSpec((1, rb, cols), lambda s, i, c_ref: (s, c_ref[0] * nb + i, 0)),
                      pl.BlockSpec((1, rb, cols), lambda s, i, c_ref: (s, i, 0))],
            out_specs=pl.BlockSpec((1, rb, cols), lambda s, i, c_ref: (s, i, 0))),
        out_shape=jax.ShapeDtypeStruct((4, half, cols), BF16),
        compiler_params=_params(("parallel", "parallel")),
    )(core, g, land)


def _chip_sum(t, r, chip, name):
    _, half, cols = t.shape
    rb = _tile(half, 256, 16)

    def body(s_ref, t_ref, r_ref, o_ref):
        o_ref[...] = ((t_ref[0].astype(F32) + r_ref[0].astype(F32)) + r_ref[1].astype(F32)) + r_ref[2].astype(F32)

    return pl.pallas_call(
        body, name=name,
        grid_spec=pltpu.PrefetchScalarGridSpec(
            num_scalar_prefetch=1, grid=(half // rb,),
            in_specs=[pl.BlockSpec((1, rb, cols), lambda i, s_ref: (s_ref[0], i, 0)),
                      pl.BlockSpec((3, rb, cols), lambda i, s_ref: (0, i, 0))],
            out_specs=pl.BlockSpec((rb, cols), lambda i, s_ref: (i, 0))),
        out_shape=jax.ShapeDtypeStruct((half, cols), F32),
        compiler_params=_params(("parallel",)),
    )(chip, t, r)


def _adam_math(wv, gv, mv, vv):
    mn = ADAM_B1 * mv + (1.0 - ADAM_B1) * gv
    vn = ADAM_B2 * vv + (1.0 - ADAM_B2) * (gv * gv)
    m_hat = mn / (1.0 - ADAM_B1 ** ADAM_STEP)
    v_hat = vn / (1.0 - ADAM_B2 ** ADAM_STEP)
    return -ADAM_LR * (m_hat / (jnp.sqrt(v_hat) + ADAM_EPS) + ADAM_WD * wv), mn, vn


def _adamw_halves(wt, mt, vt, mine, theirs, core, name):
    _, rows, cols = wt.shape
    half = rows // 2
    rb = _tile(half, 256, 8)
    nb = half // rb

    def body(c_ref, w_ref, m_ref, v_ref, a_ref, b_ref, g_ref, d_ref, mo_ref, vo_ref):
        gv = jnp.where(pl.program_id(0) == c_ref[0], a_ref[...], b_ref[...])
        dl, mn, vn = _adam_math(w_ref[...], gv, m_ref[...], v_ref[...])
        g_ref[...] = gv
        d_ref[...] = dl
        mo_ref[...] = mn
        vo_ref[...] = vn

    full = pl.BlockSpec((None, rb, cols), lambda hf, i, c_ref: (0, hf * nb + i, 0))
    part = pl.BlockSpec((rb, cols), lambda hf, i, c_ref: (i, 0))
    return pl.pallas_call(
        body, name=name,
        grid_spec=pltpu.PrefetchScalarGridSpec(num_scalar_prefetch=1, grid=(2, nb), in_specs=[full, full, full, part, part],
                                               out_specs=[full] * 4),
        out_shape=[jax.ShapeDtypeStruct((1, rows, cols), F32)] * 4,
        compiler_params=_params(("parallel", "parallel")),
    )(core, wt, mt, vt, mine, theirs)


SG_REP = 136
SG_W2, SG_CW = SG_REP, SG_REP + 4 * 16
SG_ROWS = SG_CW + 4 * 40
SP_ROWS = SG_REP + 16 + 40


def _mod_shard(c_all, ada_w_sh):
    def body(c_ref, w_ref, o_ref):
        cv = c_ref[...]
        o_ref[...] = _dg((cv * _sigmoid(cv)).astype(BF16), w_ref[...].astype(BF16), 1, 0)

    return pl.pallas_call(body, name="mod_shard", out_shape=jax.ShapeDtypeStruct((8, 1536), F32),
                          in_specs=[VMEM_SPEC, VMEM_SPEC], out_specs=VMEM_SPEC,
                          compiler_params=pltpu.CompilerParams(vmem_limit_bytes=VMEM_LIMIT))(c_all, ada_w_sh)


def _mod_select(mod_all, ada_b4):
    def body(m_ref, b_ref, o_ref):
        x, y, c = _me()
        me = 4 * x + 2 * y + c
        for sh in range(4):
            o_ref[sh] = m_ref[2 * sh, me] + b_ref[sh]

    return pl.pallas_call(body, name="mod_select", out_shape=jax.ShapeDtypeStruct((4, 12, 128), F32),
                          in_specs=[VMEM_SPEC, VMEM_SPEC], out_specs=VMEM_SPEC)(mod_all, ada_b4)


def _small_reduce(sg_all):
    def body(g_ref, o_ref):
        x, y, c = _me()
        s_me = 2 * x + y
        w2_rows = pl.ds(pl.multiple_of(SG_W2 + 16 * s_me, 8), 16)
        cw_rows = pl.ds(pl.multiple_of(SG_CW + 40 * s_me, 8), 40)
        a = g_ref[0, 0:SG_REP, :]
        b = g_ref[0, w2_rows, :]
        d = g_ref[0, cw_rows, :]
        for dev in range(1, 8):
            a = a + g_ref[dev, 0:SG_REP, :]
            b = b + g_ref[dev, w2_rows, :]
            d = d + g_ref[dev, cw_rows, :]
        o_ref[0:SG_REP, :] = a
        o_ref[SG_REP:SG_REP + 16, :] = b
        o_ref[SG_REP + 16:SP_ROWS, :] = d

    return pl.pallas_call(body, name="small_grad_reduce", out_shape=jax.ShapeDtypeStruct((SP_ROWS, 128), F32),
                          in_specs=[VMEM_SPEC], out_specs=VMEM_SPEC)(sg_all)


def _ada_grad(dmod_all, c_bc):
    def body(g_ref, c_ref, o_ref):
        x, y, c = _me()
        s_me = 2 * x + y
        for k in range(12):
            acc = jnp.zeros((D, 128), F32)
            for b in range(8):
                cv = c_ref[b]
                acc = acc + (cv * _sigmoid(cv)) * g_ref[s_me, k, b:b + 1, :]
            o_ref[:, k * 128:(k + 1) * 128] = acc

    return pl.pallas_call(body, name="ada_w_grad", out_shape=jax.ShapeDtypeStruct((D, 1536), F32),
                          in_specs=[VMEM_SPEC, VMEM_SPEC], out_specs=VMEM_SPEC,
                          compiler_params=pltpu.CompilerParams(vmem_limit_bytes=VMEM_LIMIT))(dmod_all, c_bc)


def _adamw(wt, g, m, v, name):
    rows, cols = wt.shape
    rb = _tile(rows, 256, 8)

    def fn(c, i, wv, gv, mv, vv):
        return _adam_math(wv, gv, mv, vv)

    return _rowcall(fn, [_rows(t, rb) for t in (wt, g, m, v)], [_orow(rows, cols, F32, rb)] * 3,
                    n_rows=rows, rb=rb, name=name)


def _pad_rows(t, rows):
    flat = t.reshape(-1)
    return jnp.pad(flat, (0, rows * 128 - flat.shape[0])).reshape(rows, 128)


SP_LAYOUT = (("ada_b", 48), ("norm1_w", 8), ("gla_gate_b", 8), ("gla_norm_w", 8), ("norm2_w", 8), ("conv_b", 48),
             ("final_norm_w", 8), ("gla_gate_w2", 16), ("conv_w", 40))


def _pack_small(d):
    return jnp.concatenate([_pad_rows(d[n].astype(F32), rows) for n, rows in SP_LAYOUT], axis=0)


def _unpack_small(pk, shapes):
    out, off = {}, 0
    for n, rows in SP_LAYOUT:
        shp = shapes[n]
        out[n] = pk[off:off + rows].reshape(-1)[:math.prod(shp)].reshape(shp)
        off += rows
    return out


def kernel(x, c, positions, ada_w, ada_b, norm1_w, w_in, gla_gate_w2, gla_gate_b, gla_norm_w, w_gla_branch, w_attn_branch, w_out, norm2_w, w_up, conv_w, conv_b, w_down, final_norm_w, loss_target, m_ada_w, m_ada_b, m_norm1_w, m_w_in, m_gla_gate_w2, m_gla_gate_b, m_gla_norm_w, m_w_gla_branch, m_w_attn_branch, m_w_out, m_norm2_w, m_w_up, m_conv_w, m_conv_b, m_w_down, m_final_norm_w, v_ada_w, v_ada_b, v_norm1_w, v_w_in, v_gla_gate_w2, v_gla_gate_b, v_gla_norm_w, v_w_gla_branch, v_w_attn_branch, v_w_out, v_norm2_w, v_w_up, v_conv_w, v_conv_b, v_w_down, v_final_norm_w):
    s = x.shape[1]
    names = ("ada_w", "ada_b", "norm1_w", "w_in", "gla_gate_w2", "gla_gate_b", "gla_norm_w", "w_gla_branch", "w_attn_branch",
             "w_out", "norm2_w", "w_up", "conv_w", "conv_b", "w_down", "final_norm_w")
    wts = dict(zip(names, (ada_w, ada_b, norm1_w, w_in, gla_gate_w2, gla_gate_b, gla_norm_w, w_gla_branch, w_attn_branch,
                           w_out, norm2_w, w_up, conv_w, conv_b, w_down, final_norm_w)))
    ms = dict(zip(names, (m_ada_w, m_ada_b, m_norm1_w, m_w_in, m_gla_gate_w2, m_gla_gate_b, m_gla_norm_w, m_w_gla_branch,
                          m_w_attn_branch, m_w_out, m_norm2_w, m_w_up, m_conv_w, m_conv_b, m_w_down, m_final_norm_w)))
    vs = dict(zip(names, (v_ada_w, v_ada_b, v_norm1_w, v_w_in, v_gla_gate_w2, v_gla_gate_b, v_gla_norm_w, v_w_gla_branch,
                          v_w_attn_branch, v_w_out, v_norm2_w, v_w_up, v_conv_w, v_conv_b, v_w_down, v_final_norm_w)))

    pk0 = jnp.concatenate([_pad_rows(c, 8), _pad_rows(gla_gate_w2, 16), _pad_rows(conv_w, 40)], axis=0)
    sm_all = _allgather8(pk0, "gather_small")
    c_all = sm_all[:, 0:8, :].reshape(8, D)
    w2_full = sm_all[0::2, 8:24, :].transpose(1, 0, 2).reshape(GLA_LR, 512)
    cw_full = sm_all[0::2, 24:64, :].reshape(4, 40 * 128)[:, :3 * W_UP_SH].reshape(4, 3, W_UP_SH).transpose(1, 0, 2).reshape(3, 2 * D_FF)

    mod_sh = _mod_shard(c_all, ada_w[0])
    mod_all = _allgather8(mod_sh.reshape(96, 128), "gather_mod")
    mod = _mod_select(mod_all.reshape(8, 8, 12, 128), ada_b.reshape(4, 12, 128)).reshape(6, D)

    core = lax.axis_index("c").astype(jnp.int32).reshape(1)
    chip = (2 * lax.axis_index("x") + lax.axis_index("y")).astype(jnp.int32)
    w_sh = [wts[n].astype(BF16) for n in BIG]
    sm = dict(n1w=norm1_w, n2w=norm2_w, fnw=final_norm_w.reshape(1, D), gnw=gla_norm_w, gb=gla_gate_b,
              w2=jnp.pad(w2_full, ((0, 128 - GLA_LR), (0, 0))), cw=_ff_to_kernel(cw_full), cb=_ff_to_kernel(conv_b))
    loss, grad_x, halves, others, small = _local_step(x[0], mod, positions.reshape(s, 1), loss_target[0], sm, w_sh, chip, core)

    dcw = _ff_from_kernel(small["cw"]).reshape(3, 4, W_UP_SH).transpose(1, 0, 2)
    dw2 = small["w2"][:GLA_LR].reshape(GLA_LR, 4, 128).transpose(1, 0, 2)
    sg = jnp.concatenate(
        [_pad_rows(small["dmod"], 48), _pad_rows(small["n1w"], 8), _pad_rows(small["gb"], 8), _pad_rows(small["gnw"], 8),
         _pad_rows(small["n2w"], 8), _pad_rows(_ff_from_kernel(small["cb"]), 48), _pad_rows(small["fnw"], 8)]
        + [_pad_rows(dw2[k], 16) for k in range(4)] + [_pad_rows(dcw[k], 40) for k in range(4)], axis=0)
    sg_all = _allgather8(sg, "gather_small_grads")
    g_small_pk = _small_reduce(sg_all)
    dmod_all = sg_all[:, 0:48, :].reshape(8, 4, 12, 128).transpose(1, 2, 0, 3)
    g_ada_w = _ada_grad(dmod_all, jnp.broadcast_to(c_all[:, :, None], (8, D, 128)))

    shapes = {n: wts[n].shape for n in names}
    g_small = _unpack_small(g_small_pk, shapes)
    grads = {"ada_w": g_ada_w.reshape(1, D, 1536), **g_small}
    deltas, new_m, new_v = {}, {}, {}
    for n, mine, theirs in zip(BIG, halves, others):
        grads[n], deltas[n], new_m[n], new_v[n] = _adamw_halves(wts[n], ms[n], vs[n], mine, theirs, core, "adamw_" + n)
    shp = ada_w.shape
    d_, m_, v_ = _adamw(ada_w[0], g_ada_w, m_ada_w[0], v_ada_w[0], "adamw_ada_w")
    deltas["ada_w"], new_m["ada_w"], new_v["ada_w"] = d_.reshape(shp), m_.reshape(shp), v_.reshape(shp)
    d_, m_, v_ = _adamw(_pack_small(wts), g_small_pk, _pack_small(ms), _pack_small(vs), "adamw_small")
    for dst, pk in ((deltas, d_), (new_m, m_), (new_v, v_)):
        dst.update(_unpack_small(pk, shapes))

    loss_all = lax.psum(loss[0, 0], ("x", "y", "c"))
    return (loss_all, grad_x.reshape(1, s, D), *[grads[n] for n in names], *[deltas[n] for n in names],
            *[new_m[n] for n in names], *[new_v[n] for n in names])
```

```python
import math

import jax
import jax.numpy as jnp
from jax import lax
from jax.experimental import pallas as pl
from jax.experimental.pallas import tpu as pltpu

F32, BF16 = jnp.float32, jnp.bfloat16
MESH = pl.DeviceIdType.MESH

D = 1024
EPS = 1e-6
GLA_H, GLA_DK, GLA_DV, GLA_LR = 4, 128, 256, 16
GLA_TAU = 16.0
GLA_CHUNK = 64
GLA_BLOCK = 512
ATT_GROUPS = ((128, 1), (512, 4), (2048, 16))
ATT_BLK = 128
ATT_HD = 64
ATT_W = 768
D_FF = 2816
ROPE_THETA = 10000.0
P_W = 7680
P_GV, P_GR, P_MA, P_MB, P_GQ, P_GK, P_AQ, P_AK, P_AV, P_LR = 0, 1024, 2048, 3072, 4096, 4608, 5120, 5888, 6656, 7424
W_IN = 7440
W_IN_SH, W_UP_SH, W_DOWN_SH = 1860, 1408, 704
VMEM_LIMIT = 56 * 1024 * 1024
ADAM_LR, ADAM_B1, ADAM_B2, ADAM_EPS, ADAM_WD, ADAM_STEP = 0.001, 0.9, 0.999, 1e-08, 0.01, 10
NEG = -1e30


def _tile(n, target, unit=128):
    best = None
    for t in range(unit, min(n, target) + 1, unit):
        if n % t == 0:
            best = t
    return best or n


def _params(sem):
    return pltpu.CompilerParams(dimension_semantics=sem, vmem_limit_bytes=VMEM_LIMIT)


def _dg(a, b, ca, cb):
    return lax.dot_general(a, b, (((ca,), (cb,)), ((), ())), preferred_element_type=F32)


def _sigmoid(v):
    return 1.0 / (1.0 + jnp.exp(-v))


def _ff_block(j):
    return (j % 2) * 2 + j // 2


def _mm(a, b, name, *, ta=False, tb=False, out_dtype=BF16, tm=1024, tn=1536, tk=1024, n_outer=True, comm=(),
        b_shards=False, o_shards=False):
    m = a.shape[1] if ta else a.shape[0]
    k = a.shape[0] if ta else a.shape[1]
    if b_shards:
        n = b.shape[1] if tb else 4 * W_UP_SH
        tn, tk = (tn, W_UP_SH) if tb else (W_UP_SH, tk)
    else:
        n = b.shape[0] if tb else b.shape[1]
    if o_shards:
        tn = W_UP_SH
    tm, tn, tk = _tile(m, tm), _tile(n, tn), _tile(k, tk)
    nm, nn, nk = m // tm, n // tn, k // tk
    in_out = out_dtype == F32
    c_ins, c_outs, c_alias, c_scratch = _carry(comm, 2, 1)

    def body(a_ref, b_ref, *rest):
        ci, o_ref, co = rest[:len(c_ins)], rest[len(c_ins)], rest[len(c_ins) + 1:len(c_ins) + 1 + len(c_outs)]
        scr = rest[len(c_ins) + 1 + len(c_outs):]
        kk = pl.program_id(2)
        if comm:
            step = (pl.program_id(0) * (nm if n_outer else nn) + pl.program_id(1)) * nk + kk

            @pl.when(step == 0)
            def _():
                _comm_phase(comm, ci, co, scr[-2], scr[-1], True)

        _mm_step(a_ref, b_ref, o_ref, scr, kk)
        if comm:
            @pl.when(step == nm * nn * nk - 1)
            def _():
                _comm_phase(comm, ci, co, scr[-2], scr[-1], False)

    def _mm_step(a_ref, b_ref, o_ref, scr, kk):
        p = _dg(a_ref[...].astype(BF16), b_ref[...].astype(BF16), 0 if ta else 1, 1 if tb else 0)
        if nk == 1:
            o_ref[...] = p.astype(o_ref.dtype)
        else:
            acc = o_ref if in_out else scr[0]

            @pl.when(kk == 0)
            def _():
                acc[...] = p

            @pl.when(kk > 0)
            def _():
                acc[...] += p

            if not in_out:
                @pl.when(kk == nk - 1)
                def _():
                    o_ref[...] = acc[...].astype(o_ref.dtype)

    if n_outer:
        ij = lambda g0, g1: (g1, g0)
        grid = (nn, nm, nk)
    else:
        ij = lambda g0, g1: (g0, g1)
        grid = (nm, nn, nk)
    a_map = (lambda g0, g1, kk: (kk, ij(g0, g1)[0])) if ta else (lambda g0, g1, kk: (ij(g0, g1)[0], kk))
    if b_shards and tb:
        b_spec = pl.BlockSpec((None, tn, tk), lambda g0, g1, kk: (_ff_block(kk), ij(g0, g1)[1], 0))
    elif b_shards:
        b_spec = pl.BlockSpec((None, tk, tn), lambda g0, g1, kk: (_ff_block(ij(g0, g1)[1]), kk, 0))
    elif tb:
        b_spec = pl.BlockSpec((tn, tk), lambda g0, g1, kk: (ij(g0, g1)[1], kk))
    else:
        b_spec = pl.BlockSpec((tk, tn), lambda g0, g1, kk: (kk, ij(g0, g1)[1]))
    if o_shards:
        o_spec = pl.BlockSpec((None, tm, tn), lambda g0, g1, kk: (_ff_block(ij(g0, g1)[1]), ij(g0, g1)[0], 0))
        o_shape = jax.ShapeDtypeStruct((4, m, W_UP_SH), out_dtype)
    else:
        o_spec = pl.BlockSpec((tm, tn), lambda g0, g1, kk: ij(g0, g1))
        o_shape = jax.ShapeDtypeStruct((m, n), out_dtype)
    res = pl.pallas_call(
        body, name=name, grid=grid,
        in_specs=[pl.BlockSpec((tk, tm) if ta else (tm, tk), a_map), b_spec] + [HBM] * len(c_ins),
        out_specs=[o_spec] + [HBM] * len(c_outs),
        out_shape=[o_shape] + c_outs,
        scratch_shapes=([] if (in_out or nk == 1) else [pltpu.VMEM((tm, tn), F32)]) + c_scratch,
        input_output_aliases=c_alias,
        compiler_params=_params(("arbitrary",) * 3 if comm else ("parallel", "parallel", "arbitrary")),
    )(a, b, *c_ins)
    return (res[0], _split_units(comm, res[1:])) if comm else res[0]


def _rows(arr, rb, w=None, j=0):
    w = arr.shape[1] if w is None else w
    if callable(j):
        return arr, pl.BlockSpec((rb, w), lambda c, i: (i, j(c)))
    return arr, pl.BlockSpec((rb, w), lambda c, i: (i, j))


def _full(arr, w=None, j=0):
    w = arr.shape[1] if w is None else w
    if callable(j):
        return arr, pl.BlockSpec((arr.shape[0], w), lambda c, i: (0, j(c)))
    return arr, pl.BlockSpec((arr.shape[0], w), lambda c, i: (0, j))


def _halo(arr, rb, hb, w, j, before):
    per = rb // hb
    last = arr.shape[0] // hb - 1
    if before:
        rmap = lambda i: jnp.maximum(i * per - 1, 0)
    else:
        rmap = lambda i: jnp.minimum((i + 1) * per, last)
    return arr, pl.BlockSpec((hb, w), lambda c, i: (rmap(i), j(c) if callable(j) else j))


def _rowcall(fn, ins, outs, *, n_rows, rb, name, ncol=1, into=None):
    n_in = len(ins)
    nr = n_rows // rb
    n_skip = 0 if into is None else 1

    def body(*refs):
        c, i = pl.program_id(0), pl.program_id(1)
        res = fn(c, i, *[r[...] for r in refs[:n_in]])
        for val, spec, o_ref in zip(res, outs, refs[n_in + n_skip:]):
            if spec[2] == "row":
                o_ref[...] = val.astype(o_ref.dtype)
            else:
                @pl.when(i == 0)
                def _(o_ref=o_ref, val=val):
                    o_ref[...] = val.astype(o_ref.dtype)

                @pl.when(i > 0)
                def _(o_ref=o_ref, val=val):
                    o_ref[...] += val.astype(o_ref.dtype)

    out_specs = []
    for shape, dt, kind, block, col in outs:
        if kind == "row":
            out_specs.append(pl.BlockSpec(block, lambda c, i, col=col: (i, col(c))))
        else:
            out_specs.append(pl.BlockSpec(block, lambda c, i, col=col: (0, col(c))))
    return pl.pallas_call(
        body, name=name, grid=(ncol, nr),
        in_specs=[s for _, s in ins] + [pl.BlockSpec(memory_space=pl.ANY)] * n_skip, out_specs=out_specs,
        out_shape=[jax.ShapeDtypeStruct(o[0], o[1]) for o in outs],
        input_output_aliases={} if into is None else {n_in: into[1]},
        compiler_params=_params(("parallel", "arbitrary")),
    )(*[a for a, _ in ins], *([] if into is None else [into[0]]))


def _orow(n_rows, w, dt, rb, bw=None, col=lambda c: 0):
    return ((n_rows, w), dt, "row", (rb, bw or w), col)


def _oacc(r, w, bw=None, col=lambda c: 0):
    return ((r, w), F32, "acc", (r, bw or w), col)


def _csum(v):
    return jnp.sum(v, axis=0, keepdims=True)


def _rms(v):
    return lax.rsqrt(jnp.mean(v * v, axis=-1, keepdims=True) + EPS)


def _norm_bwd(xv, dh, w, scale):
    r = _rms(xv)
    xh = xv * r
    dxh = dh * (w * (1.0 + scale))
    dx = r * (dxh - xh * jnp.mean(dxh * xh, axis=-1, keepdims=True))
    t = dh * xh
    return dx, _csum(dh), _csum(t * w), _csum(t * (1.0 + scale))


def _rope_tables(pos_col, invf, s):
    def fn(c, i, pos, f):
        ang = pos.astype(F32) * f
        lane = lax.broadcasted_iota(jnp.int32, ang.shape, 1)
        sign = jnp.where((lane % ATT_HD) < ATT_HD // 2, -1.0, 1.0)
        return jnp.cos(ang), jnp.sin(ang) * sign

    rb = 512
    return _rowcall(fn, [_rows(pos_col, rb), _full(invf)], [_orow(s, 128, F32, rb), _orow(s, 128, F32, rb)],
                    n_rows=s, rb=rb, name="rope_tables")


def _swap_halves(t):
    n = t.shape[1]
    lane = lax.broadcasted_iota(jnp.int32, t.shape, 1)
    return jnp.where((lane % ATT_HD) < ATT_HD // 2, pltpu.roll(t, n - 32, 1), pltpu.roll(t, 32, 1))


def _rope_apply(t, cos, sin_signed, inverse):
    cw = jnp.concatenate([cos] * (t.shape[1] // 128), axis=1)
    sw = jnp.concatenate([sin_signed] * (t.shape[1] // 128), axis=1)
    if inverse:
        sw = -sw
    return t * cw + _swap_halves(t) * sw


DIL_ROWS = 512


def _to_dilated(scr, val, out_ref, r):
    if r == 1:
        out_ref[...] = val.astype(out_ref.dtype)
        return
    n = val.shape[0] // r
    for hh in range(2):
        scr[hh] = val[:, hh * 128:(hh + 1) * 128]
        for pr in range(r):
            out_ref[:, pr * 256 + hh * 128:pr * 256 + (hh + 1) * 128] = scr[hh, pl.ds(pr, n, stride=r), :].astype(out_ref.dtype)


def _from_dilated(scr, in_ref, r):
    if r == 1:
        return in_ref[...].astype(F32)
    n = in_ref.shape[0]
    for hh in range(2):
        for pr in range(r):
            scr[hh, pl.ds(pr, n, stride=r), :] = in_ref[:, pr * 256 + hh * 128:pr * 256 + (hh + 1) * 128].astype(F32)
    return jnp.concatenate([scr[0], scr[1]], axis=1)


def _dil_spec(r):
    return pl.BlockSpec((DIL_ROWS // r, r * 256), lambda i: (i, 0))


def _dil_shape(s, r, dt):
    return jax.ShapeDtypeStruct((s // r, r * 256), dt)


_DIL_SCRATCH = [pltpu.VMEM((2, DIL_ROWS, 128), F32)]
_RS = tuple(r for _, r in ATT_GROUPS)


def _rope_fwd(p, cos_t, sin_t, s):
    def body(*refs):
        ins, cs, sn, outs, scr = refs[:9], refs[9][...], refs[10][...], refs[11:20], refs[20]
        for t in range(3):
            for g, r in enumerate(_RS):
                val = ins[3 * t + g][...].astype(F32)
                _to_dilated(scr, _rope_apply(val, cs, sn, False) if t < 2 else val, outs[3 * t + g], r)

    res = pl.pallas_call(
        body, name="rope", grid=(s // DIL_ROWS,),
        in_specs=[pl.BlockSpec((DIL_ROWS, 256), lambda i, c=base // 256 + g: (i, c)) for base in (P_AQ, P_AK, P_AV) for g in range(3)]
        + [pl.BlockSpec((DIL_ROWS, 128), lambda i: (i, 0))] * 2,
        out_specs=[_dil_spec(r) for _ in range(3) for r in _RS],
        out_shape=[_dil_shape(s, r, BF16) for _ in range(3) for r in _RS],
        scratch_shapes=_DIL_SCRATCH, compiler_params=_params(("parallel",)),
    )(*([p] * 9), cos_t, sin_t)
    return res[0:3], res[3:6], res[6:9]


def _attn_combine(att, s):
    def body(o0, o1, o2, l0, l1, l2, o_ref, lse_ref, od1, od2, ld1, ld2, scr):
        ov = [_from_dilated(scr, ref, r) for ref, r in zip((o0, o1, o2), _RS)]
        lv = [_from_dilated(scr, ref, r) for ref, r in zip((l0, l1, l2), _RS)]
        mx = jnp.maximum(jnp.maximum(lv[0], lv[1]), lv[2])
        ev = [jnp.exp(l - mx) for l in lv]
        z = ev[0] + ev[1] + ev[2]
        o = ((ev[0] * ov[0] + ev[1] * ov[1] + ev[2] * ov[2]) / z).astype(BF16)
        lse = mx + jnp.log(z)
        o_ref[...] = o
        lse_ref[...] = lse
        for ref, r in zip((od1, od2), _RS[1:]):
            _to_dilated(scr, o.astype(F32), ref, r)
        for ref, r in zip((ld1, ld2), _RS[1:]):
            _to_dilated(scr, lse, ref, r)

    return pl.pallas_call(
        body, name="attn_combine", grid=(s // DIL_ROWS,),
        in_specs=[_dil_spec(r) for r in _RS] * 2,
        out_specs=[_dil_spec(1)] * 2 + [_dil_spec(r) for r in _RS[1:]] * 2,
        out_shape=[_dil_shape(s, 1, BF16), _dil_shape(s, 1, F32)] + [_dil_shape(s, r, BF16) for r in _RS[1:]]
        + [_dil_shape(s, r, F32) for r in _RS[1:]],
        scratch_shapes=_DIL_SCRATCH, compiler_params=_params(("parallel",)),
    )(*[a[0] for a in att], *[a[1] for a in att])


def _dilate(t, s):
    def body(t_ref, o1, o2, scr):
        val = t_ref[...].astype(F32)
        for ref, r in zip((o1, o2), _RS[1:]):
            _to_dilated(scr, val, ref, r)

    return pl.pallas_call(
        body, name="attn_dilate", grid=(s // DIL_ROWS,), in_specs=[_dil_spec(1)], out_specs=[_dil_spec(r) for r in _RS[1:]],
        out_shape=[_dil_shape(s, r, t.dtype) for r in _RS[1:]], scratch_shapes=_DIL_SCRATCH, compiler_params=_params(("parallel",)),
    )(t)


def _rope_bwd(datt, d_glr, dp, cos_t, sin_t, s):
    tail = P_W - P_AQ

    def body(*refs):
        ins, cs, sn, glr, o_ref, scr = refs[:9], refs[9][...], refs[10][...], refs[11], refs[13], refs[14]
        for t in range(3):
            for g, r in enumerate(_RS):
                val = _from_dilated(scr, ins[3 * t + g], r)
                o_ref[:, t * ATT_W + g * 256:t * ATT_W + (g + 1) * 256] = (_rope_apply(val, cs, sn, True) if t < 2 else val).astype(BF16)
        o_ref[:, 3 * ATT_W:3 * ATT_W + 128] = glr[...]
        o_ref[:, 3 * ATT_W + 128:] = jnp.zeros((DIL_ROWS, tail - 3 * ATT_W - 128), BF16)

    return pl.pallas_call(
        body, name="rope_bwd", grid=(s // DIL_ROWS,),
        in_specs=[_dil_spec(r) for _ in range(3) for r in _RS] + [pl.BlockSpec((DIL_ROWS, 128), lambda i: (i, 0))] * 3
        + [pl.BlockSpec(memory_space=pl.ANY)],
        out_specs=pl.BlockSpec((DIL_ROWS, tail), lambda i: (i, P_AQ // tail)),
        out_shape=jax.ShapeDtypeStruct((s, P_W), BF16), input_output_aliases={12: 0},
        scratch_shapes=_DIL_SCRATCH, compiler_params=_params(("parallel",)),
    )(*[datt[g][t] for t in range(3) for g in range(3)], cos_t, sin_t, d_glr, dp)


def _gla_decays(la_c, tri):
    b = jnp.dot(tri, la_c, precision=lax.Precision.HIGHEST, preferred_element_type=F32)
    row = lax.broadcasted_iota(jnp.int32, b.shape, 0)
    bmid = jnp.sum(jnp.where(row == GLA_CHUNK // 2 - 1, b, 0.0), axis=0, keepdims=True)
    blast = jnp.sum(jnp.where(row == GLA_CHUNK - 1, b, 0.0), axis=0, keepdims=True)
    return b, bmid, blast


def _gla_fwd(p, la, s, comm=()):
    tb, ch = GLA_BLOCK, GLA_CHUNK
    nb, nc = s // tb, tb // ch
    scale = GLA_DK ** -0.5
    c_ins, c_outs, c_alias, c_scratch = _carry(comm, 4, 2)

    def body(q_ref, k_ref, v_ref, la_ref, *rest):
        ci, (o_ref, st_ref) = rest[:len(c_ins)], rest[len(c_ins):len(c_ins) + 2]
        co, state = rest[len(c_ins) + 2:len(c_ins) + 2 + len(c_outs)], rest[len(c_ins) + 2 + len(c_outs)]
        step = pl.program_id(0) * nb + pl.program_id(1)
        if comm:
            @pl.when(step == 0)
            def _():
                _comm_phase(comm, ci, co, rest[-2], rest[-1], True)

        _gla_fwd_step(q_ref, k_ref, v_ref, la_ref, o_ref, st_ref, state)
        if comm:
            @pl.when(step == GLA_H * nb - 1)
            def _():
                _comm_phase(comm, ci, co, rest[-2], rest[-1], False)

    def _gla_fwd_step(q_ref, k_ref, v_ref, la_ref, o_ref, st_ref, state):
        @pl.when(pl.program_id(1) == 0)
        def _():
            state[...] = jnp.zeros_like(state)

        ri = lax.broadcasted_iota(jnp.int32, (ch, ch), 0)
        ci = lax.broadcasted_iota(jnp.int32, (ch, ch), 1)
        causal = ci <= ri
        tri = causal.astype(F32)
        for c in range(nc):
            sl = pl.ds(c * ch, ch)
            b, bmid, blast = _gla_decays(la_ref[sl, :], tri)
            q = q_ref[sl, :].astype(F32) * scale
            k = k_ref[sl, :].astype(F32)
            v = v_ref[sl, :]
            qgt = (q * jnp.exp(b)).astype(BF16)
            qgn = (q * jnp.exp(b - bmid)).astype(BF16)
            kgn = (k * jnp.exp(bmid - b)).astype(BF16)
            kd = (k * jnp.exp(blast - b)).astype(BF16)
            a = jnp.where(causal, _dg(qgn, kgn, 1, 1), 0.0)
            st = state[...]
            st_ref[0, c] = st
            o_ref[sl, :] = _dg(a.astype(BF16), v, 1, 0) + _dg(qgt, st.astype(BF16), 1, 1)
            state[...] = jnp.exp(blast) * st + _dg(v, kd, 0, 0)

    res = pl.pallas_call(
        body, name="gla_fwd", grid=(GLA_H, nb),
        in_specs=[pl.BlockSpec((tb, GLA_DK), lambda h, t: (t, P_GQ // GLA_DK + h)),
                  pl.BlockSpec((tb, GLA_DK), lambda h, t: (t, P_GK // GLA_DK + h)),
                  pl.BlockSpec((tb, GLA_DV), lambda h, t: (t, P_GV // GLA_DV + h)),
                  pl.BlockSpec((tb, GLA_DK), lambda h, t: (t, h))] + [HBM] * len(c_ins),
        out_specs=[pl.BlockSpec((tb, GLA_DV), lambda h, t: (t, h)),
                   pl.BlockSpec((1, nc, GLA_DV, GLA_DK), lambda h, t: (h, t, 0, 0))] + [HBM] * len(c_outs),
        out_shape=[jax.ShapeDtypeStruct((s, GLA_H * GLA_DV), F32),
                   jax.ShapeDtypeStruct((GLA_H, s // ch, GLA_DV, GLA_DK), F32)] + c_outs,
        scratch_shapes=[pltpu.VMEM((GLA_DV, GLA_DK), F32)] + c_scratch,
        input_output_aliases=c_alias,
        compiler_params=_params(("arbitrary", "arbitrary") if comm else ("parallel", "arbitrary")),
    )(p, p, p, la, *c_ins)
    return res[0], res[1], _split_units(comm, res[2:])


def _gla_bwd(p, la, states, do, s, dp, comm=()):
    tb, ch = GLA_BLOCK, GLA_CHUNK
    nb, nc = s // tb, tb // ch
    scale = GLA_DK ** -0.5
    c_ins, c_outs, c_alias, c_scratch = _carry(comm, 7, 4)

    def body(q_ref, k_ref, v_ref, la_ref, st_ref, do_ref, dp_in, *rest):
        ci, outs = rest[:len(c_ins)], rest[len(c_ins):len(c_ins) + 4]
        co, dstate = rest[len(c_ins) + 4:len(c_ins) + 4 + len(c_outs)], rest[len(c_ins) + 4 + len(c_outs)]
        step = pl.program_id(0) * nb + pl.program_id(1)
        if comm:
            @pl.when(step == 0)
            def _():
                _comm_phase(comm, ci, co, rest[-2], rest[-1], True)

        _gla_bwd_step(q_ref, k_ref, v_ref, la_ref, st_ref, do_ref, *outs, dstate)
        if comm:
            @pl.when(step == GLA_H * nb - 1)
            def _():
                _comm_phase(comm, ci, co, rest[-2], rest[-1], False)

    def _gla_bwd_step(q_ref, k_ref, v_ref, la_ref, st_ref, do_ref, dq_ref, dk_ref, dv_ref, dla_ref, dstate):
        @pl.when(pl.program_id(1) == 0)
        def _():
            dstate[...] = jnp.zeros_like(dstate)

        ri = lax.broadcasted_iota(jnp.int32, (ch, ch), 0)
        ci = lax.broadcasted_iota(jnp.int32, (ch, ch), 1)
        causal = ci <= ri
        tri = causal.astype(F32)
        tri_t = (ci >= ri).astype(F32)
        for c in reversed(range(nc)):
            sl = pl.ds(c * ch, ch)
            b, bmid, blast = _gla_decays(la_ref[sl, :], tri)
            q = q_ref[sl, :].astype(F32) * scale
            k = k_ref[sl, :].astype(F32)
            v = v_ref[sl, :]
            e_b, e_qn, e_kn, e_kd = jnp.exp(b), jnp.exp(b - bmid), jnp.exp(bmid - b), jnp.exp(blast - b)
            dec = jnp.exp(blast)
            qgt, qgn, kgn, kd = q * e_b, q * e_qn, k * e_kn, k * e_kd
            qgt_b, qgn_b, kgn_b, kd_b = qgt.astype(BF16), qgn.astype(BF16), kgn.astype(BF16), kd.astype(BF16)
            st0 = st_ref[0, c]
            dst = dstate[...]
            dst_b = dst.astype(BF16)
            do_b = do_ref[sl, :].astype(BF16)
            a = jnp.where(causal, _dg(qgn_b, kgn_b, 1, 1), 0.0).astype(BF16)
            da = jnp.where(causal, _dg(do_b, v, 1, 1), 0.0).astype(BF16)
            dqgn = _dg(da, kgn_b, 1, 0)
            dqgt = _dg(do_b, st0.astype(BF16), 1, 0)
            dkgn = _dg(da, qgn_b, 0, 0)
            dv = _dg(a, do_b, 0, 0) + _dg(kd_b, dst_b, 1, 1)
            dkd = _dg(v, dst_b, 1, 0)
            ddec = jnp.sum(st0 * dst, axis=0, keepdims=True)
            dstate[...] = dec * dst + _dg(do_b, qgt_b, 0, 0)
            dq_ref[sl, :] = (scale * (dqgn * e_qn + dqgt * e_b)).astype(dq_ref.dtype)
            dk_ref[sl, :] = (dkgn * e_kn + dkd * e_kd).astype(dk_ref.dtype)
            dv_ref[sl, :] = dv.astype(dv_ref.dtype)
            db = dqgn * qgn + dqgt * qgt - dkgn * kgn - dkd * kd
            extra = jnp.sum(dkd * kd, axis=0, keepdims=True) + ddec * dec
            dla_ref[sl, :] = jnp.dot(tri_t, db, precision=lax.Precision.HIGHEST, preferred_element_type=F32) + extra

    rev = lambda t: nb - 1 - t
    res = pl.pallas_call(
        body, name="gla_bwd", grid=(GLA_H, nb),
        in_specs=[pl.BlockSpec((tb, GLA_DK), lambda h, t: (rev(t), P_GQ // GLA_DK + h)),
                  pl.BlockSpec((tb, GLA_DK), lambda h, t: (rev(t), P_GK // GLA_DK + h)),
                  pl.BlockSpec((tb, GLA_DV), lambda h, t: (rev(t), P_GV // GLA_DV + h)),
                  pl.BlockSpec((tb, GLA_DK), lambda h, t: (rev(t), h)),
                  pl.BlockSpec((1, nc, GLA_DV, GLA_DK), lambda h, t: (h, rev(t), 0, 0)),
                  pl.BlockSpec((tb, GLA_DV), lambda h, t: (rev(t), h)), pl.BlockSpec(memory_space=pl.ANY)] + [HBM] * len(c_ins),
        out_specs=[pl.BlockSpec((tb, GLA_DK), lambda h, t: (rev(t), h)),
                   pl.BlockSpec((tb, GLA_DK), lambda h, t: (rev(t), h)),
                   pl.BlockSpec((tb, GLA_DV), lambda h, t: (rev(t), P_GV // GLA_DV + h)),
                   pl.BlockSpec((tb, GLA_DK), lambda h, t: (rev(t), h))] + [HBM] * len(c_outs),
        out_shape=[jax.ShapeDtypeStruct((s, GLA_H * GLA_DK), BF16),
                   jax.ShapeDtypeStruct((s, GLA_H * GLA_DK), BF16),
                   jax.ShapeDtypeStruct((s, P_W), BF16),
                   jax.ShapeDtypeStruct((s, GLA_H * GLA_DK), F32)] + c_outs,
        scratch_shapes=[pltpu.VMEM((GLA_DV, GLA_DK), F32)] + c_scratch,
        input_output_aliases={6: 2, **c_alias},
        compiler_params=_params(("arbitrary", "arbitrary") if comm else ("parallel", "arbitrary")),
    )(p, p, p, la, states, do, dp, *c_ins)
    return res[0], res[1], res[2], res[3], _split_units(comm, res[4:])


def _head_masks():
    lane = lax.broadcasted_iota(jnp.int32, (1, 4 * ATT_HD), 1)
    return [(lane >= h * ATT_HD) & (lane < (h + 1) * ATT_HD) for h in range(4)]


def _attn_fwd(qv, kv, pv, g, r, s):
    ln = s // r
    nblk = ln // ATT_BLK
    qcol = lambda pr: pr
    vcol = qcol
    prev = lambda n: jnp.maximum(n - 1, 0)

    def body(q_ref, kp_ref, kc_ref, vp_ref, vc_ref, o_ref, lse_ref):
        has_prev = pl.program_id(1) > 0
        ri = lax.broadcasted_iota(jnp.int32, (ATT_BLK, ATT_BLK), 0)
        ci = lax.broadcasted_iota(jnp.int32, (ATT_BLK, ATT_BLK), 1)
        m_cur = ci <= ri
        m_prev = (ci >= ri) & has_prev
        q, kp, kc, vp, vc = q_ref[...], kp_ref[...], kc_ref[...], vp_ref[...], vc_ref[...]
        o = jnp.zeros((ATT_BLK, 256), F32)
        lse = jnp.zeros((ATT_BLK, 256), F32)
        for hm in _head_masks():
            qm = jnp.where(hm, q, jnp.zeros_like(q))
            sc = jnp.where(m_cur, _dg(qm, kc, 1, 1) * 0.125, NEG)
            sp = jnp.where(m_prev, _dg(qm, kp, 1, 1) * 0.125, NEG)
            mx = jnp.maximum(jnp.max(sc, axis=1, keepdims=True), jnp.max(sp, axis=1, keepdims=True))
            pc, pp = jnp.exp(sc - mx), jnp.exp(sp - mx)
            den = jnp.sum(pc, axis=1, keepdims=True) + jnp.sum(pp, axis=1, keepdims=True)
            oh = (_dg(pc.astype(BF16), vc, 1, 0) + _dg(pp.astype(BF16), vp, 1, 0)) / den
            o = jnp.where(hm, oh, o)
            lse = jnp.where(hm, mx + jnp.log(den), lse)
        o_ref[...] = o.astype(o_ref.dtype)
        lse_ref[...] = lse

    blk = (ATT_BLK, 256)
    o, lse = pl.pallas_call(
        body, name=f"attn_fwd_{g}", grid=(r, nblk),
        in_specs=[pl.BlockSpec(blk, lambda pr, n: (n, qcol(pr))),
                  pl.BlockSpec(blk, lambda pr, n: (prev(n), qcol(pr))),
                  pl.BlockSpec(blk, lambda pr, n: (n, qcol(pr))),
                  pl.BlockSpec(blk, lambda pr, n: (prev(n), vcol(pr))),
                  pl.BlockSpec(blk, lambda pr, n: (n, vcol(pr)))],
        out_specs=[pl.BlockSpec(blk, lambda pr, n: (n, pr)), pl.BlockSpec(blk, lambda pr, n: (n, pr))],
        out_shape=[jax.ShapeDtypeStruct((ln, r * 256), BF16), jax.ShapeDtypeStruct((ln, r * 256), F32)],
        compiler_params=_params(("parallel", "parallel")),
    )(qv, kv, kv, pv, pv)
    return o, lse


def _attn_bwd(qv, kv, pv, dov, ov, lv, g, r, s):
    ln = s // r
    nblk = ln // ATT_BLK
    qcol = lambda pr: pr
    vcol = qcol
    prev = lambda n: jnp.maximum(n - 1, 0)
    nxt = lambda n: jnp.minimum(n + 1, nblk - 1)

    def body(qc_ref, qn_ref, kp_ref, kc_ref, vp_ref, vc_ref, doc_ref, don_ref, oc_ref, on_ref, lc_ref, ln_ref,
             dq_ref, dk_ref, dv_ref):
        n = pl.program_id(1)
        has_prev, has_next = n > 0, n < nblk - 1
        ri = lax.broadcasted_iota(jnp.int32, (ATT_BLK, ATT_BLK), 0)
        ci = lax.broadcasted_iota(jnp.int32, (ATT_BLK, ATT_BLK), 1)
        m_cur = ci <= ri
        m_prev = (ci >= ri) & has_prev
        m_next = (ci >= ri) & has_next
        qc, qn, kp, kc, vp, vc = qc_ref[...], qn_ref[...], kp_ref[...], kc_ref[...], vp_ref[...], vc_ref[...]
        doc, don = doc_ref[...], don_ref[...]
        pc_full = doc.astype(F32) * oc_ref[...].astype(F32)
        pn_full = don.astype(F32) * on_ref[...].astype(F32)
        lc, lnx = lc_ref[...], ln_ref[...]
        dq = jnp.zeros((ATT_BLK, 256), F32)
        dk = jnp.zeros((ATT_BLK, 256), F32)
        dv = jnp.zeros((ATT_BLK, 256), F32)
        zb = jnp.zeros_like(qc)
        for hm in _head_masks():
            qcm, qnm = jnp.where(hm, qc, zb), jnp.where(hm, qn, zb)
            docm, donm = jnp.where(hm, doc, zb), jnp.where(hm, don, zb)
            lse_c = jnp.max(jnp.where(hm, lc, NEG), axis=1, keepdims=True)
            lse_n = jnp.max(jnp.where(hm, lnx, NEG), axis=1, keepdims=True)
            del_c = jnp.sum(jnp.where(hm, pc_full, 0.0), axis=1, keepdims=True)
            del_n = jnp.sum(jnp.where(hm, pn_full, 0.0), axis=1, keepdims=True)
            pr_ = jnp.where(m_cur, jnp.exp(_dg(qcm, kc, 1, 1) * 0.125 - lse_c), 0.0)
            ds = (pr_ * (_dg(docm, vc, 1, 1) - del_c) * 0.125).astype(BF16)
            dqh = _dg(ds, kc, 1, 0)
            dkh = _dg(ds, qc, 0, 0)
            dvh = _dg(pr_.astype(BF16), doc, 0, 0)
            pr_ = jnp.where(m_prev, jnp.exp(_dg(qcm, kp, 1, 1) * 0.125 - lse_c), 0.0)
            ds = (pr_ * (_dg(docm, vp, 1, 1) - del_c) * 0.125).astype(BF16)
            dqh = dqh + _dg(ds, kp, 1, 0)
            pr_ = jnp.where(m_next, jnp.exp(_dg(qnm, kc, 1, 1) * 0.125 - lse_n), 0.0)
            ds = (pr_ * (_dg(donm, vc, 1, 1) - del_n) * 0.125).astype(BF16)
            dkh = dkh + _dg(ds, qn, 0, 0)
            dvh = dvh + _dg(pr_.astype(BF16), don, 0, 0)
            dq = jnp.where(hm, dqh, dq)
            dk = jnp.where(hm, dkh, dk)
            dv = jnp.where(hm, dvh, dv)
        dq_ref[...] = dq.astype(dq_ref.dtype)
        dk_ref[...] = dk.astype(dk_ref.dtype)
        dv_ref[...] = dv.astype(dv_ref.dtype)

    blk = (ATT_BLK, 256)
    cur = lambda col: pl.BlockSpec(blk, lambda pr, n: (n, col(pr)))
    prv = lambda col: pl.BlockSpec(blk, lambda pr, n: (prev(n), col(pr)))
    nx = lambda col: pl.BlockSpec(blk, lambda pr, n: (nxt(n), col(pr)))
    own = lambda pr: pr
    outs = pl.pallas_call(
        body, name=f"attn_bwd_{g}", grid=(r, nblk),
        in_specs=[cur(qcol), nx(qcol), prv(qcol), cur(qcol), prv(vcol), cur(vcol),
                  cur(own), nx(own), cur(own), nx(own), cur(own), nx(own)],
        out_specs=[cur(own), cur(own), cur(own)],
        out_shape=[jax.ShapeDtypeStruct((ln, r * 256), BF16)] * 3,
        compiler_params=_params(("parallel", "parallel")),
    )(qv, qv, kv, kv, pv, pv, dov, dov, ov, ov, lv, lv)
    return outs


def _gelu_parts(gv):
    cdf = 0.5 * (1.0 + lax.erf(gv * (2.0 ** -0.5)))
    pdf = jnp.exp(-0.5 * gv * gv) * (1.0 / math.sqrt(2.0 * math.pi))
    return cdf, pdf


def _pick_row(t, k):
    row = lax.broadcasted_iota(jnp.int32, t.shape, 0)
    return jnp.sum(jnp.where(row == k, t, 0.0), axis=0, keepdims=True)


def _shift_rows(u, halo, n):
    row = lax.broadcasted_iota(jnp.int32, u.shape, 0)
    out = pltpu.roll(u, n, 0)
    for k in range(n):
        out = jnp.where(row == k, _pick_row(halo, 16 - n + k), out)
    return out


def _shift_rows_up(u, halo, n):
    rb = u.shape[0]
    row = lax.broadcasted_iota(jnp.int32, u.shape, 0)
    out = pltpu.roll(u, rb - n, 0)
    for k in range(n):
        out = jnp.where(row == rb - n + k, _pick_row(halo, k), out)
    return out


def _conv(u, halo, cw, cb):
    return cb + _pick_row(cw, 0) * _shift_rows(u, halo, 2) + _pick_row(cw, 1) * _shift_rows(u, halo, 1) + _pick_row(cw, 2) * u


def _local_step(x, mod, pos_col, target, sm, w_sh, chip, core):
    s = x.shape[0]
    shift1, scale1, gate1, shift2, scale2, gate2 = [mod[i:i + 1, :] for i in range(6)]
    rb = 256
    chip1 = chip.reshape(1)

    def f_norm1(c, i, xv, nw, sc, sh):
        return ((xv * _rms(xv) * nw) * (1.0 + sc) + sh,)

    (h,) = _rowcall(f_norm1, [_rows(x, rb), _full(sm["n1w"]), _full(scale1), _full(shift1)],
                    [_orow(s, D, BF16, rb)], n_rows=s, rb=rb, name="norm1")
    own = lambda got, i: lax.dynamic_update_slice(got, w_sh[i], (chip, 0, 0))
    [got0] = _comm_call("gather_w_in_ici", [_u_gather_ici(w_sh, (0,))])
    [got0] = _comm_call("gather_w_in_d2d", [_u_gather_d2d(got0, (0,))])
    w = dict(win=_win_assemble(own(got0[0], 0)))
    p, [got123] = _mm(h, w["win"], "in_proj", tm=1024, tn=1536, comm=[_u_gather_ici(w_sh, (1, 2, 3))])

    def f_gla_pre(c, i, glr, w2, gb):
        z = _dg(glr, w2.astype(BF16), 1, 0) + gb
        return ((jnp.minimum(z, 0.0) - jnp.log(1.0 + jnp.exp(-jnp.abs(z)))) * (1.0 / GLA_TAU),)

    (la,) = _rowcall(f_gla_pre, [_rows(p, rb, 128, P_LR // 128), _full(sm["w2"]), _full(sm["gb"])],
                     [_orow(s, 512, F32, rb)], n_rows=s, rb=rb, name="gla_pre")
    o_gla, states, [got123, got45] = _gla_fwd(p, la, s, comm=[_u_gather_d2d(got123, (1, 2, 3)), _u_gather_ici(w_sh, (4, 5))])
    w.update(wgb=own(got123[0], 1).reshape(1024, D), wab=_cols_join(own(got123[1], 2)), wout=own(got123[2], 3).reshape(D, D))

    def f_gla_post(c, i, ov, gnw, gr):
        on = jnp.concatenate([ov[:, k * 256:(k + 1) * 256] * _rms(ov[:, k * 256:(k + 1) * 256]) * gnw
                              for k in range(GLA_H)], axis=1)
        g = gr.astype(F32)
        return (on * (g * _sigmoid(g)),)

    (og,) = _rowcall(f_gla_post, [_rows(o_gla, rb), _full(sm["gnw"]), _rows(p, rb, 1024, P_GR // 1024)],
                     [_orow(s, 1024, BF16, rb)], n_rows=s, rb=rb, name="gla_post")
    y_gla = _mm(og, w["wgb"], "gla_branch")

    invf = jnp.tile(ROPE_THETA ** (-jnp.arange(ATT_HD // 2, dtype=F32) / (ATT_HD // 2)), 4).reshape(1, 128)
    cos_t, sin_t = _rope_tables(pos_col, invf, s)

    q_d, k_d, v_d = _rope_fwd(p, cos_t, sin_t, s)
    att = [_attn_fwd(q_d[g], k_d[g], v_d[g], g, r, s) for g, r in enumerate(_RS)]
    o_att, lse, o_d1, o_d2, lse_d1, lse_d2 = _attn_combine(att, s)
    y_att = _mm(o_att, w["wab"], "attn_branch")

    def f_merge(c, i, ma, mb, yg, ya):
        return (_sigmoid(ma.astype(F32)) * yg.astype(F32) + _sigmoid(mb.astype(F32)) * ya.astype(F32),)

    (mixed,) = _rowcall(f_merge, [_rows(p, rb, D, P_MA // D), _rows(p, rb, D, P_MB // D), _rows(y_gla, rb), _rows(y_att, rb)],
                        [_orow(s, D, BF16, rb)], n_rows=s, rb=rb, name="merge")
    z1, [got45] = _mm(mixed, w["wout"], "out_proj", comm=[_u_gather_d2d(got45, (4, 5))])
    w.update(wup=own(got45[0], 4), wdown=own(got45[1], 5).reshape(D_FF, D))

    def f_norm2(c, i, xv, z, g1, nw, sc, sh):
        x1 = xv + g1 * z.astype(F32)
        return (x1, (x1 * _rms(x1) * nw) * (1.0 + sc) + sh)

    x1, h2 = _rowcall(f_norm2, [_rows(x, rb), _rows(z1, rb), _full(gate1), _full(sm["n2w"]), _full(scale2), _full(shift2)],
                      [_orow(s, D, F32, rb), _orow(s, D, BF16, rb)], n_rows=s, rb=rb, name="norm2")
    u = _mm(h2, w["wup"], "up_proj", b_shards=True)

    cwid = 2 * W_UP_SH

    def f_ffn(c, i, uv, hl, cw, cb):
        uc = _conv(uv.astype(F32), hl.astype(F32) * (i > 0).astype(F32), cw, cb)
        val, gt = uc[:, :W_UP_SH], uc[:, W_UP_SH:]
        cdf, _ = _gelu_parts(gt)
        return (gt * cdf * val,)

    ccol = lambda c: c
    (hidden,) = _rowcall(f_ffn, [_rows(u, rb, cwid, ccol), _halo(u, rb, 16, cwid, ccol, True),
                                 _full(sm["cw"], cwid, ccol), _full(sm["cb"], cwid, ccol)],
                         [_orow(s, D_FF, BF16, rb, W_UP_SH, ccol)], n_rows=s, rb=rb, name="conv_geglu", ncol=2)
    z2 = _mm(hidden, w["wdown"], "down_proj", tk=D_FF)

    def f_final(c, i, x1v, z, g2, fw, tgt):
        x2 = x1v + g2 * z.astype(F32)
        r = _rms(x2)
        xh = x2 * r
        e = xh * fw - tgt
        loss = 0.5 * jnp.sum(jnp.mean(e * e, axis=-1, keepdims=True), axis=0, keepdims=True)
        dy = e * (1.0 / D)
        dxh = dy * fw
        dx2 = r * (dxh - xh * jnp.mean(dxh * xh, axis=-1, keepdims=True))
        return (loss, dx2, dx2 * g2, _csum(dy * xh), _csum(dx2 * z.astype(F32)))

    loss, dx2, dz2, d_fnw, d_gate2 = _rowcall(
        f_final, [_rows(x1, rb), _rows(z2, rb), _full(gate2), _full(sm["fnw"]), _rows(target, rb)],
        [_oacc(1, 1), _orow(s, D, F32, rb), _orow(s, D, BF16, rb), _oacc(1, D), _oacc(1, D)],
        n_rows=s, rb=rb, name="final_loss")
    d_hidden = _mm(dz2, w["wdown"], "down_proj_dx", tb=True, tn=1408)
    g_wdown = _mm(hidden, dz2, "down_proj_dw", ta=True, out_dtype=F32, tm=1408, tn=1024, tk=2048)

    def f_ffn_bwd(c, i, uv, hl, dh, cw, cb):
        uf = uv.astype(F32)
        hf = hl.astype(F32) * (i > 0).astype(F32)
        u1, u2 = _shift_rows(uf, hf, 1), _shift_rows(uf, hf, 2)
        uc = cb + _pick_row(cw, 0) * u2 + _pick_row(cw, 1) * u1 + _pick_row(cw, 2) * uf
        val, gt = uc[:, :W_UP_SH], uc[:, W_UP_SH:]
        cdf, pdf = _gelu_parts(gt)
        dhf = dh.astype(F32)
        duc = jnp.concatenate([dhf * (gt * cdf), dhf * val * (cdf + gt * pdf)], axis=1)
        dcw = jnp.concatenate([_csum(duc * u2), _csum(duc * u1), _csum(duc * uf)], axis=0)
        return (duc, _csum(duc), dcw)

    duc, d_cb, d_cw = _rowcall(
        f_ffn_bwd, [_rows(u, rb, cwid, ccol), _halo(u, rb, 16, cwid, ccol, True), _rows(d_hidden, rb, W_UP_SH, ccol),
                    _full(sm["cw"], cwid, ccol), _full(sm["cb"], cwid, ccol)],
        [_orow(s, 2 * D_FF, BF16, rb, cwid, ccol), _oacc(1, 2 * D_FF, cwid, ccol), _oacc(3, 2 * D_FF, cwid, ccol)],
        n_rows=s, rb=rb, name="conv_geglu_bwd", ncol=2)

    def f_conv_t(c, i, dv, hl, cw):
        df = dv.astype(F32)
        hf = hl.astype(F32) * (i < s // rb - 1).astype(F32)
        return (_pick_row(cw, 2) * df + _pick_row(cw, 1) * _shift_rows_up(df, hf, 1) + _pick_row(cw, 0) * _shift_rows_up(df, hf, 2),)

    (du,) = _rowcall(f_conv_t, [_rows(duc, rb, cwid, ccol), _halo(duc, rb, 16, cwid, ccol, False), _full(sm["cw"], cwid, ccol)],
                     [_orow(s, 2 * D_FF, BF16, rb, cwid, ccol)], n_rows=s, rb=rb, name="conv_transpose", ncol=2)
    g_wup = _mm(h2, du, "up_proj_dw", ta=True, out_dtype=F32, tm=1024, tk=2048, o_shards=True)
    gs45 = [g_wup, g_wdown.reshape(4, W_DOWN_SH, 1024)]
    d_h2, [land45] = _mm(du, w["wup"], "up_proj_dx", tb=True, b_shards=True, comm=[_u_pair_send(gs45, (4, 5))])
    ts45 = [_pair_add(g, ld, core, "grad_pair_add_" + BIG[i]) for g, ld, i in zip(gs45, land45, (4, 5))]

    def f_norm2_bwd(c, i, x1v, dh, dxr, z, nw, sc, g1):
        dxn, dsh, dsc, dnw = _norm_bwd(x1v, dh.astype(F32), nw, sc)
        dx1 = dxr + dxn
        return (dx1, dx1 * g1, dsh, dsc, dnw, _csum(dx1 * z.astype(F32)))

    dx1, dz1, d_shift2, d_scale2, d_n2w, d_gate1 = _rowcall(
        f_norm2_bwd, [_rows(x1, rb), _rows(d_h2, rb), _rows(dx2, rb), _rows(z1, rb), _full(sm["n2w"]), _full(scale2), _full(gate1)],
        [_orow(s, D, F32, rb), _orow(s, D, BF16, rb), _oacc(1, D), _oacc(1, D), _oacc(1, D), _oacc(1, D)],
        n_rows=s, rb=rb, name="norm2_bwd")
    d_mixed = _mm(dz1, w["wout"], "out_proj_dx", tb=True)
    g_wout = _mm(mixed, dz1, "out_proj_dw", ta=True, out_dtype=F32, tk=2048)

    def f_merge_bwd(c, i, dm, ma, mb, yg, ya):
        dmf, ygf, yaf = dm.astype(F32), yg.astype(F32), ya.astype(F32)
        sa, sb = _sigmoid(ma.astype(F32)), _sigmoid(mb.astype(F32))
        return (dmf * sa, dmf * sb, jnp.concatenate([dmf * ygf * sa * (1.0 - sa), dmf * yaf * sb * (1.0 - sb)], axis=1))

    dy_gla, dy_att, dp = _rowcall(
        f_merge_bwd, [_rows(d_mixed, rb), _rows(p, rb, D, P_MA // D), _rows(p, rb, D, P_MB // D), _rows(y_gla, rb), _rows(y_att, rb)],
        [_orow(s, D, BF16, rb)] * 2 + [_orow(s, P_W, BF16, rb, 2 * D, lambda c: P_MA // (2 * D))], n_rows=s, rb=rb, name="merge_bwd")
    d_og = _mm(dy_gla, w["wgb"], "gla_branch_dx", tb=True)
    g_wgb = _mm(og, dy_gla, "gla_branch_dw", ta=True, out_dtype=F32, tk=2048)
    d_oatt = _mm(dy_att, w["wab"], "attn_branch_dx", tb=True)
    g_wab = _mm(o_att, dy_att, "attn_branch_dw", ta=True, out_dtype=F32, tk=2048)

    def f_gla_post_bwd(c, i, ov, gnw, gr, dog):
        g = gr.astype(F32)
        sg = _sigmoid(g)
        silu = g * sg
        dof = dog.astype(F32)
        don = dof * silu
        on_parts, do_parts, dgn = [], [], jnp.zeros((1, 256), F32)
        for k in range(GLA_H):
            oh = ov[:, k * 256:(k + 1) * 256]
            dh = don[:, k * 256:(k + 1) * 256]
            r = _rms(oh)
            xh = oh * r
            dgn = dgn + _csum(dh * xh)
            dxh = dh * gnw
            do_parts.append(r * (dxh - xh * jnp.mean(dxh * xh, axis=-1, keepdims=True)))
            on_parts.append(xh * gnw)
        on = jnp.concatenate(on_parts, axis=1)
        dgr = dof * on * (sg * (1.0 + g * (1.0 - sg)))
        return (jnp.concatenate(do_parts, axis=1), dgr, dgn)

    do_gla, dp, d_gnw = _rowcall(
        f_gla_post_bwd, [_rows(o_gla, rb), _full(sm["gnw"]), _rows(p, rb, 1024, P_GR // 1024), _rows(d_og, rb)],
        [_orow(s, 1024, F32, rb), _orow(s, P_W, BF16, rb, 1024, lambda c: P_GR // 1024), _oacc(1, 256)],
        n_rows=s, rb=rb, name="gla_post_bwd", into=(dp, 1))
    gs123 = [g_wgb.reshape(4, 256, 1024), _cols_split(g_wab), g_wout.reshape(4, 256, 1024)]
    d_gq, d_gk, dp, d_la, [r45, land123] = _gla_bwd(p, la, states, do_gla, s, dp,
                                                    comm=[_u_chip_exchange(ts45), _u_pair_send(gs123, (1, 2, 3))])
    half45 = [_chip_sum(t, r, chip1, "grad_chip_sum_" + BIG[i]) for t, r, i in zip(ts45, r45, (4, 5))]
    ts123 = [_pair_add(g, ld, core, "grad_pair_add_" + BIG[i]) for g, ld, i in zip(gs123, land123, (1, 2, 3))]

    def f_gla_pre_bwd(c, i, lav, dlav, glr, w2):
        dz = dlav * (1.0 / GLA_TAU) * (1.0 - jnp.exp(GLA_TAU * lav))
        dzb = dz.astype(BF16)
        return (_dg(dzb, w2.astype(BF16), 1, 1), _csum(dz), _dg(glr, dzb, 0, 0))

    d_glr, d_gb, d_w2 = _rowcall(
        f_gla_pre_bwd, [_rows(la, rb), _rows(d_la, rb), _rows(p, rb, 128, P_LR // 128), _full(sm["w2"])],
        [_orow(s, 128, BF16, rb), _oacc(1, 512), _oacc(128, 512)], n_rows=s, rb=rb, name="gla_pre_bwd")

    do_d = [d_oatt] + list(_dilate(d_oatt, s))
    datt = [_attn_bwd(q_d[g], k_d[g], v_d[g], do_d[g], (o_att, o_d1, o_d2)[g], (lse, lse_d1, lse_d2)[g], g, r, s)
            for g, r in enumerate(_RS)]
    dp = _rope_bwd(datt, d_glr, dp, cos_t, sin_t, s)
    dp = lax.dynamic_update_slice(dp, jnp.concatenate([d_gq, d_gk], axis=1), (0, P_GQ))
    g_win, [r123, oth45] = _mm(h, dp, "in_proj_dw", ta=True, out_dtype=F32, tm=1024, tn=1536, tk=2048,
                               comm=[_u_chip_exchange(ts123), _u_pair_join(half45)])
    half123 = [_chip_sum(t, r, chip1, "grad_chip_sum_" + BIG[i]) for t, r, i in zip(ts123, r123, (1, 2, 3))]
    gs0 = [_win_split(g_win)]
    d_h, [land0, oth123] = _mm(dp, w["win"], "in_proj_dx", tb=True, tk=3840,
                               comm=[_u_pair_send(gs0, (0,)), _u_pair_join(half123)])
    ts0 = [_pair_add(gs0[0], land0[0], core, "grad_pair_add_w_in")]
    [r0] = _comm_call("grad_exchange_w_in", [_u_chip_exchange(ts0)])
    half0 = [_chip_sum(ts0[0], r0[0], chip1, "grad_chip_sum_w_in")]
    [oth0] = _comm_call("grad_join_w_in", [_u_pair_join(half0)])

    def f_norm1_bwd(c, i, xv, dh, dxr, nw, sc):
        dxn, dsh, dsc, dnw = _norm_bwd(xv, dh.astype(F32), nw, sc)
        return (dxr + dxn, dsh, dsc, dnw)

    grad_x, d_shift1, d_scale1, d_n1w = _rowcall(
        f_norm1_bwd, [_rows(x, rb), _rows(d_h, rb), _rows(dx1, rb), _full(sm["n1w"]), _full(scale1)],
        [_orow(s, D, F32, rb), _oacc(1, D), _oacc(1, D), _oacc(1, D)], n_rows=s, rb=rb, name="norm1_bwd")

    dmod = jnp.concatenate([d_shift1, d_scale1, d_gate1, d_shift2, d_scale2, d_gate2], axis=1)
    small = dict(dmod=dmod, n1w=d_n1w, gb=d_gb, gnw=d_gnw, n2w=d_n2w, cb=d_cb, fnw=d_fnw, w2=d_w2, cw=d_cw)
    return loss, grad_x, half0 + half123 + half45, oth0 + oth123 + oth45, small


def _win_pieces():
    runs = [(P_GV, 1024, 2048), (P_MA, 5392, 2048), (P_GQ, 0, 1024), (P_AQ, 3088, 2304), (P_LR, 3072, GLA_LR)]
    out = []
    for kc, rc, ln in runs:
        while ln > 0:
            step = min(ln, W_IN_SH - rc % W_IN_SH)
            out.append((kc, rc, step))
            kc, rc, ln = kc + step, rc + step, ln - step
    return out


def _win_assemble(shards):
    rb = 256

    def body(s_ref, o_ref):
        o_ref[:, W_IN:] = jnp.zeros((rb, P_W - W_IN), o_ref.dtype)
        for kc, rc, ln in _win_pieces():
            o_ref[:, kc:kc + ln] = s_ref[rc // W_IN_SH, :, rc % W_IN_SH:rc % W_IN_SH + ln]

    return pl.pallas_call(
        body, name="w_in_assemble", grid=(D // rb,),
        in_specs=[pl.BlockSpec((4, rb, W_IN_SH), lambda i: (0, i, 0))], out_specs=pl.BlockSpec((rb, P_W), lambda i: (i, 0)),
        out_shape=jax.ShapeDtypeStruct((D, P_W), shards.dtype), compiler_params=_params(("parallel",)),
    )(shards)


def _win_split(g):
    rb = 256

    def body(g_ref, o_ref):
        for kc, rc, ln in _win_pieces():
            o_ref[rc // W_IN_SH, :, rc % W_IN_SH:rc % W_IN_SH + ln] = g_ref[:, kc:kc + ln]

    return pl.pallas_call(
        body, name="w_in_grad_split", grid=(D // rb,),
        in_specs=[pl.BlockSpec((rb, P_W), lambda i: (i, 0))], out_specs=pl.BlockSpec((4, rb, W_IN_SH), lambda i: (0, i, 0)),
        out_shape=jax.ShapeDtypeStruct((4, D, W_IN_SH), g.dtype), compiler_params=_params(("parallel",)),
    )(g)


def _ff_to_kernel(a):
    h = W_UP_SH
    return jnp.concatenate([a[:, 0:h], a[:, D_FF:D_FF + h], a[:, h:D_FF], a[:, D_FF + h:]], axis=1)


def _ff_from_kernel(a):
    h = W_UP_SH
    return jnp.concatenate([a[:, 0:h], a[:, 2 * h:3 * h], a[:, h:2 * h], a[:, 3 * h:]], axis=1)


BIG = ("w_in", "w_gla_branch", "w_attn_branch", "w_out", "w_up", "w_down")
SH_SHAPES = ((1024, W_IN_SH), (256, 1024), (256, 256), (256, 1024), (1024, W_UP_SH), (W_DOWN_SH, 1024))
N_BIG = len(BIG)


def _cols_join(t):
    return jnp.concatenate([t[k] for k in range(4)], axis=1)


def _cols_split(t):
    cols = t.shape[1] // 4
    return jnp.stack([t[:, k * cols:(k + 1) * cols] for k in range(4)])


def _me():
    return lax.axis_index("x"), lax.axis_index("y"), lax.axis_index("c")


HBM = pl.BlockSpec(memory_space=pltpu.HBM)
VMEM_SPEC = pl.BlockSpec(memory_space=pltpu.VMEM)


def _allgather8(xs, name):
    rows = xs.shape[0]

    def body(x_ref, out_ref, send_sems, recv_sems, local_sem):
        x, y, c = _me()
        me = 4 * x + 2 * y + c
        mine = pltpu.make_async_copy(x_ref, out_ref.at[me], local_sem)
        mine.start()
        flips = [(k >> 2 & 1, k >> 1 & 1, k & 1) for k in range(1, 8)]

        def peer(f):
            return (jnp.where(f[0] == 1, 1 - x, x), jnp.where(f[1] == 1, 1 - y, y), jnp.where(f[2] == 1, 1 - c, c))

        sends = []
        for k, f in enumerate(flips):
            cp = pltpu.make_async_remote_copy(src_ref=x_ref, dst_ref=out_ref.at[me], send_sem=send_sems.at[k],
                                              recv_sem=recv_sems.at[k], device_id=peer(f), device_id_type=MESH)
            cp.start()
            sends.append(cp)
        for k, f in enumerate(flips):
            px, py, pc = peer(f)
            pltpu.make_async_remote_copy(src_ref=x_ref, dst_ref=out_ref.at[4 * px + 2 * py + pc], send_sem=send_sems.at[k],
                                         recv_sem=recv_sems.at[k], device_id=peer(f), device_id_type=MESH).wait_recv()
        for cp in sends:
            cp.wait_send()
        mine.wait()

    return pl.pallas_call(
        body, name=name, out_shape=jax.ShapeDtypeStruct((8, rows, 128), F32),
        in_specs=[VMEM_SPEC], out_specs=VMEM_SPEC,
        scratch_shapes=[pltpu.SemaphoreType.DMA((7,)), pltpu.SemaphoreType.DMA((7,)), pltpu.SemaphoreType.DMA],
        compiler_params=pltpu.CompilerParams(vmem_limit_bytes=VMEM_LIMIT),
    )(xs)


def _half_rows(i, cc, unit):
    rows = SH_SHAPES[i][0] // 2
    return pl.ds(pl.multiple_of(cc * rows, unit), rows)


def _rc(src, dst, sems, to):
    return pltpu.make_async_remote_copy(src_ref=src, dst_ref=dst, send_sem=sems[0], recv_sem=sems[1], device_id=to, device_id_type=MESH)


def _other_chips(x, y):
    return [(1 - x, y), (x, 1 - y), (1 - x, 1 - y)]


def _u_gather_ici(w_sh, idxs):
    def copies(ins, outs, sem):
        x, y, c = _me()
        res = []
        for j, (px, py) in enumerate(_other_chips(x, y)):
            for n, i in enumerate(idxs):
                src = ins[n].at[0, _half_rows(i, c, 16)]
                res.append((_rc(src, outs[n].at[2 * x + y, _half_rows(i, c, 16)], sem(j * len(idxs) + n), (px, py, c)),
                            _rc(src, outs[n].at[2 * px + py, _half_rows(i, c, 16)], sem(j * len(idxs) + n), (px, py, c))))
        return res

    return dict(ins=[w_sh[i] for i in idxs], outs=[jax.ShapeDtypeStruct((4,) + SH_SHAPES[i], BF16) for i in idxs],
                nsem=3 * len(idxs), alias={}, copies=copies)


def _u_gather_d2d(got, idxs):
    def copies(ins, outs, sem):
        x, y, c = _me()
        res = []
        for j, (px, py) in enumerate(_other_chips(x, y)):
            for n, i in enumerate(idxs):
                src = ins[n].at[2 * px + py, _half_rows(i, c, 16)]
                res.append((_rc(src, outs[n].at[2 * px + py, _half_rows(i, c, 16)], sem(j * len(idxs) + n), (x, y, 1 - c)),
                            _rc(src, outs[n].at[2 * px + py, _half_rows(i, 1 - c, 16)], sem(j * len(idxs) + n), (x, y, 1 - c))))
        return res

    return dict(ins=list(got), outs=[jax.ShapeDtypeStruct(g.shape, g.dtype) for g in got], nsem=3 * len(idxs),
                alias={n: n for n in range(len(idxs))}, copies=copies)


def _u_pair_send(gs, idxs):
    def copies(ins, outs, sem):
        x, y, c = _me()
        res = []
        for n, i in enumerate(idxs):
            for sh in range(4):
                cp = _rc(ins[n].at[sh, _half_rows(i, 1 - c, 8)], outs[n].at[sh], sem(4 * n + sh), (x, y, 1 - c))
                res.append((cp, cp))
        return res

    return dict(ins=list(gs), outs=[jax.ShapeDtypeStruct((4, SH_SHAPES[i][0] // 2, SH_SHAPES[i][1]), F32) for i in idxs],
                nsem=4 * len(idxs), alias={}, copies=copies)


def _u_chip_exchange(ts):
    def copies(ins, outs, sem):
        x, y, c = _me()
        res = []
        for j, (px, py) in enumerate(_other_chips(x, y)):
            for n in range(len(ts)):
                cp = _rc(ins[n].at[2 * px + py], outs[n].at[j], sem(j * len(ts) + n), (px, py, c))
                res.append((cp, cp))
        return res

    return dict(ins=list(ts), outs=[jax.ShapeDtypeStruct((3,) + t.shape[1:], t.dtype) for t in ts], nsem=3 * len(ts),
                alias={}, copies=copies)


def _u_pair_join(hs):
    def copies(ins, outs, sem):
        x, y, c = _me()
        res = []
        for n in range(len(hs)):
            cp = _rc(ins[n], outs[n], sem(n), (x, y, 1 - c))
            res.append((cp, cp))
        return res

    return dict(ins=list(hs), outs=[jax.ShapeDtypeStruct(h.shape, h.dtype) for h in hs], nsem=len(hs), alias={}, copies=copies)


def _comm_phase(units, ci, co, send_sems, recv_sems, start):
    ii = oo = off = 0
    for u in units:
        ni, no = len(u["ins"]), len(u["outs"])
        for st, arrival in u["copies"](ci[ii:ii + ni], co[oo:oo + no], lambda k, off=off: (send_sems.at[off + k], recv_sems.at[off + k])):
            if start:
                st.start()
            else:
                st.wait_send()
                arrival.wait_recv()
        ii, oo, off = ii + ni, oo + no, off + u["nsem"]


def _carry(units, n_in, n_out):
    ins = [a for u in units for a in u["ins"]]
    outs = [o for u in units for o in u["outs"]]
    alias, ii, oo = {}, 0, 0
    for u in units:
        for a, b in u["alias"].items():
            alias[n_in + ii + a] = n_out + oo + b
        ii, oo = ii + len(u["ins"]), oo + len(u["outs"])
    nsem = sum(u["nsem"] for u in units)
    scratch = [pltpu.SemaphoreType.DMA((nsem,)), pltpu.SemaphoreType.DMA((nsem,))] if units else []
    return ins, outs, alias, scratch


def _split_units(units, res):
    out, oo = [], 0
    for u in units:
        out.append(list(res[oo:oo + len(u["outs"])]))
        oo += len(u["outs"])
    return out


def _comm_call(name, units):
    ins, outs, alias, scratch = _carry(units, 0, 0)

    def body(*refs):
        ci, co = refs[:len(ins)], refs[len(ins):len(ins) + len(outs)]
        _comm_phase(units, ci, co, refs[-2], refs[-1], True)
        _comm_phase(units, ci, co, refs[-2], refs[-1], False)

    res = pl.pallas_call(body, name=name, out_shape=outs, in_specs=[HBM] * len(ins), out_specs=[HBM] * len(outs),
                         scratch_shapes=scratch, input_output_aliases=alias)(*ins)
    return _split_units(units, res)


def _pair_add(g, land, core, name):
    _, rows, cols = g.shape
    half = rows // 2
    rb = _tile(half, 256, 16)
    nb = half // rb

    def body(c_ref, g_ref, l_ref, o_ref):
        o_ref[...] = (g_ref[...] + l_ref[...]).astype(BF16)

    return pl.pallas_call(
        body, name=name,
        grid_spec=pltpu.PrefetchScalarGridSpec(
            num_scalar_prefetch=1, grid=(4, nb),
            in_specs=[pl.BlockSpec((1, rb, cols), lambda s, i, c_ref: (s, c_ref[0] * nb + i, 0)),
                      pl.BlockSpec((1, rb, cols), lambda s, i, c_ref: (s, i, 0))],
            out_specs=pl.BlockSpec((1, rb, cols), lambda s, i, c_ref: (s, i, 0))),
        out_shape=jax.ShapeDtypeStruct((4, half, cols), BF16),
        compiler_params=_params(("parallel", "parallel")),
    )(core, g, land)


def _chip_sum(t, r, chip, name):
    _, half, cols = t.shape
    rb = _tile(half, 256, 16)

    def body(s_ref, t_ref, r_ref, o_ref):
        o_ref[...] = ((t_ref[0].astype(F32) + r_ref[0].astype(F32)) + r_ref[1].astype(F32)) + r_ref[2].astype(F32)

    return pl.pallas_call(
        body, name=name,
        grid_spec=pltpu.PrefetchScalarGridSpec(
            num_scalar_prefetch=1, grid=(half // rb,),
            in_specs=[pl.BlockSpec((1, rb, cols), lambda i, s_ref: (s_ref[0], i, 0)),
                      pl.BlockSpec((3, rb, cols), lambda i, s_ref: (0, i, 0))],
            out_specs=pl.BlockSpec((rb, cols), lambda i, s_ref: (i, 0))),
        out_shape=jax.ShapeDtypeStruct((half, cols), F32),
        compiler_params=_params(("parallel",)),
    )(chip, t, r)


def _adam_math(wv, gv, mv, vv):
    mn = ADAM_B1 * mv + (1.0 - ADAM_B1) * gv
    vn = ADAM_B2 * vv + (1.0 - ADAM_B2) * (gv * gv)
    m_hat = mn / (1.0 - ADAM_B1 ** ADAM_STEP)
    v_hat = vn / (1.0 - ADAM_B2 ** ADAM_STEP)
    return -ADAM_LR * (m_hat / (jnp.sqrt(v_hat) + ADAM_EPS) + ADAM_WD * wv), mn, vn


def _adamw_halves(wt, mt, vt, mine, theirs, core, name):
    _, rows, cols = wt.shape
    half = rows // 2
    rb = _tile(half, 256, 8)
    nb = half // rb

    def body(c_ref, w_ref, m_ref, v_ref, a_ref, b_ref, g_ref, d_ref, mo_ref, vo_ref):
        gv = jnp.where(pl.program_id(0) == c_ref[0], a_ref[...], b_ref[...])
        dl, mn, vn = _adam_math(w_ref[...], gv, m_ref[...], v_ref[...])
        g_ref[...] = gv
        d_ref[...] = dl
        mo_ref[...] = mn
        vo_ref[...] = vn

    full = pl.BlockSpec((None, rb, cols), lambda hf, i, c_ref: (0, hf * nb + i, 0))
    part = pl.BlockSpec((rb, cols), lambda hf, i, c_ref: (i, 0))
    return pl.pallas_call(
        body, name=name,
        grid_spec=pltpu.PrefetchScalarGridSpec(num_scalar_prefetch=1, grid=(2, nb), in_specs=[full, full, full, part, part],
                                               out_specs=[full] * 4),
        out_shape=[jax.ShapeDtypeStruct((1, rows, cols), F32)] * 4,
        compiler_params=_params(("parallel", "parallel")),
    )(core, wt, mt, vt, mine, theirs)


SG_REP = 144
SG_LOSS = 136
SG_W2, SG_CW = SG_REP, SG_REP + 4 * 16
SG_ROWS = SG_CW + 4 * 40
SP_ROWS = SG_REP + 16 + 40


def _mod_shard(c_all, ada_w_sh):
    def body(c_ref, w_ref, o_ref):
        cv = c_ref[...]
        o_ref[...] = _dg((cv * _sigmoid(cv)).astype(BF16), w_ref[...].astype(BF16), 1, 0)

    return pl.pallas_call(body, name="mod_shard", out_shape=jax.ShapeDtypeStruct((8, 1536), F32),
                          in_specs=[VMEM_SPEC, VMEM_SPEC], out_specs=VMEM_SPEC,
                          compiler_params=pltpu.CompilerParams(vmem_limit_bytes=VMEM_LIMIT))(c_all, ada_w_sh)


def _mod_select(mod_all, ada_b4):
    def body(m_ref, b_ref, o_ref):
        x, y, c = _me()
        me = 4 * x + 2 * y + c
        for sh in range(4):
            o_ref[sh] = m_ref[2 * sh, me] + b_ref[sh]

    return pl.pallas_call(body, name="mod_select", out_shape=jax.ShapeDtypeStruct((4, 12, 128), F32),
                          in_specs=[VMEM_SPEC, VMEM_SPEC], out_specs=VMEM_SPEC)(mod_all, ada_b4)


def _small_reduce(sg_all):
    def body(g_ref, o_ref):
        x, y, c = _me()
        s_me = 2 * x + y
        w2_rows = pl.ds(pl.multiple_of(SG_W2 + 16 * s_me, 8), 16)
        cw_rows = pl.ds(pl.multiple_of(SG_CW + 40 * s_me, 8), 40)
        a = g_ref[0, 0:SG_REP, :]
        b = g_ref[0, w2_rows, :]
        d = g_ref[0, cw_rows, :]
        for dev in range(1, 8):
            a = a + g_ref[dev, 0:SG_REP, :]
            b = b + g_ref[dev, w2_rows, :]
            d = d + g_ref[dev, cw_rows, :]
        o_ref[0:SG_REP, :] = a
        o_ref[SG_REP:SG_REP + 16, :] = b
        o_ref[SG_REP + 16:SP_ROWS, :] = d

    return pl.pallas_call(body, name="small_grad_reduce", out_shape=jax.ShapeDtypeStruct((SP_ROWS, 128), F32),
                          in_specs=[VMEM_SPEC], out_specs=VMEM_SPEC)(sg_all)


def _ada_grad(dmod_all, c_bc):
    def body(g_ref, c_ref, o_ref):
        x, y, c = _me()
        s_me = 2 * x + y
        for k in range(12):
            acc = jnp.zeros((D, 128), F32)
            for b in range(8):
                cv = c_ref[b]
                acc = acc + (cv * _sigmoid(cv)) * g_ref[s_me, k, b:b + 1, :]
            o_ref[:, k * 128:(k + 1) * 128] = acc

    return pl.pallas_call(body, name="ada_w_grad", out_shape=jax.ShapeDtypeStruct((D, 1536), F32),
                          in_specs=[VMEM_SPEC, VMEM_SPEC], out_specs=VMEM_SPEC,
                          compiler_params=pltpu.CompilerParams(vmem_limit_bytes=VMEM_LIMIT))(dmod_all, c_bc)


def _adamw(wt, g, m, v, name):
    rows, cols = wt.shape
    rb = _tile(rows, 256, 8)

    def fn(c, i, wv, gv, mv, vv):
        return _adam_math(wv, gv, mv, vv)

    return _rowcall(fn, [_rows(t, rb) for t in (wt, g, m, v)], [_orow(rows, cols, F32, rb)] * 3,
                    n_rows=rows, rb=rb, name=name)


def _pad_rows(t, rows):
    flat = t.reshape(-1)
    return jnp.pad(flat, (0, rows * 128 - flat.shape[0])).reshape(rows, 128)


SP_LAYOUT = (("ada_b", 48), ("norm1_w", 8), ("gla_gate_b", 8), ("gla_norm_w", 8), ("norm2_w", 8), ("conv_b", 48),
             ("final_norm_w", 8), (None, 8), ("gla_gate_w2", 16), ("conv_w", 40))


def _pack_small(d):
    return jnp.concatenate([jnp.zeros((rows, 128), F32) if n is None else _pad_rows(d[n].astype(F32), rows)
                            for n, rows in SP_LAYOUT], axis=0)


def _unpack_small(pk, shapes):
    out, off = {}, 0
    for n, rows in SP_LAYOUT:
        if n is not None:
            shp = shapes[n]
            out[n] = pk[off:off + rows].reshape(-1)[:math.prod(shp)].reshape(shp)
        off += rows
    return out


def kernel(x, c, positions, ada_w, ada_b, norm1_w, w_in, gla_gate_w2, gla_gate_b, gla_norm_w, w_gla_branch, w_attn_branch, w_out, norm2_w, w_up, conv_w, conv_b, w_down, final_norm_w, loss_target, m_ada_w, m_ada_b, m_norm1_w, m_w_in, m_gla_gate_w2, m_gla_gate_b, m_gla_norm_w, m_w_gla_branch, m_w_attn_branch, m_w_out, m_norm2_w, m_w_up, m_conv_w, m_conv_b, m_w_down, m_final_norm_w, v_ada_w, v_ada_b, v_norm1_w, v_w_in, v_gla_gate_w2, v_gla_gate_b, v_gla_norm_w, v_w_gla_branch, v_w_attn_branch, v_w_out, v_norm2_w, v_w_up, v_conv_w, v_conv_b, v_w_down, v_final_norm_w):
    s = x.shape[1]
    names = ("ada_w", "ada_b", "norm1_w", "w_in", "gla_gate_w2", "gla_gate_b", "gla_norm_w", "w_gla_branch", "w_attn_branch",
             "w_out", "norm2_w", "w_up", "conv_w", "conv_b", "w_down", "final_norm_w")
    wts = dict(zip(names, (ada_w, ada_b, norm1_w, w_in, gla_gate_w2, gla_gate_b, gla_norm_w, w_gla_branch, w_attn_branch,
                           w_out, norm2_w, w_up, conv_w, conv_b, w_down, final_norm_w)))
    ms = dict(zip(names, (m_ada_w, m_ada_b, m_norm1_w, m_w_in, m_gla_gate_w2, m_gla_gate_b, m_gla_norm_w, m_w_gla_branch,
                          m_w_attn_branch, m_w_out, m_norm2_w, m_w_up, m_conv_w, m_conv_b, m_w_down, m_final_norm_w)))
    vs = dict(zip(names, (v_ada_w, v_ada_b, v_norm1_w, v_w_in, v_gla_gate_w2, v_gla_gate_b, v_gla_norm_w, v_w_gla_branch,
                          v_w_attn_branch, v_w_out, v_norm2_w, v_w_up, v_conv_w, v_conv_b, v_w_down, v_final_norm_w)))

    pk0 = jnp.concatenate([_pad_rows(c, 8), _pad_rows(gla_gate_w2, 16), _pad_rows(conv_w, 40)], axis=0)
    sm_all = _allgather8(pk0, "gather_small")
    c_all = sm_all[:, 0:8, :].reshape(8, D)
    w2_full = sm_all[0::2, 8:24, :].transpose(1, 0, 2).reshape(GLA_LR, 512)
    cw_full = sm_all[0::2, 24:64, :].reshape(4, 40 * 128)[:, :3 * W_UP_SH].reshape(4, 3, W_UP_SH).transpose(1, 0, 2).reshape(3, 2 * D_FF)

    mod_sh = _mod_shard(c_all, ada_w[0])
    mod_all = _allgather8(mod_sh.reshape(96, 128), "gather_mod")
    mod = _mod_select(mod_all.reshape(8, 8, 12, 128), ada_b.reshape(4, 12, 128)).reshape(6, D)

    core = lax.axis_index("c").astype(jnp.int32).reshape(1)
    chip = (2 * lax.axis_index("x") + lax.axis_index("y")).astype(jnp.int32)
    w_sh = [wts[n].astype(BF16) for n in BIG]
    sm = dict(n1w=norm1_w, n2w=norm2_w, fnw=final_norm_w.reshape(1, D), gnw=gla_norm_w, gb=gla_gate_b,
              w2=jnp.pad(w2_full, ((0, 128 - GLA_LR), (0, 0))), cw=_ff_to_kernel(cw_full), cb=_ff_to_kernel(conv_b))
    loss, grad_x, halves, others, small = _local_step(x[0], mod, positions.reshape(s, 1), loss_target[0], sm, w_sh, chip, core)

    dcw = _ff_from_kernel(small["cw"]).reshape(3, 4, W_UP_SH).transpose(1, 0, 2)
    dw2 = small["w2"][:GLA_LR].reshape(GLA_LR, 4, 128).transpose(1, 0, 2)
    sg = jnp.concatenate(
        [_pad_rows(small["dmod"], 48), _pad_rows(small["n1w"], 8), _pad_rows(small["gb"], 8), _pad_rows(small["gnw"], 8),
         _pad_rows(small["n2w"], 8), _pad_rows(_ff_from_kernel(small["cb"]), 48), _pad_rows(small["fnw"], 8), _pad_rows(loss, 8)]
        + [_pad_rows(dw2[k], 16) for k in range(4)] + [_pad_rows(dcw[k], 40) for k in range(4)], axis=0)
    sg_all = _allgather8(sg, "gather_small_grads")
    g_small_pk = _small_reduce(sg_all)
    dmod_all = sg_all[:, 0:48, :].reshape(8, 4, 12, 128).transpose(1, 2, 0, 3)
    g_ada_w = _ada_grad(dmod_all, jnp.broadcast_to(c_all[:, :, None], (8, D, 128)))

    shapes = {n: wts[n].shape for n in names}
    g_small = _unpack_small(g_small_pk, shapes)
    grads = {"ada_w": g_ada_w.reshape(1, D, 1536), **g_small}
    deltas, new_m, new_v = {}, {}, {}
    for n, mine, theirs in zip(BIG, halves, others):
        grads[n], deltas[n], new_m[n], new_v[n] = _adamw_halves(wts[n], ms[n], vs[n], mine, theirs, core, "adamw_" + n)
    shp = ada_w.shape
    d_, m_, v_ = _adamw(ada_w[0], g_ada_w, m_ada_w[0], v_ada_w[0], "adamw_ada_w")
    deltas["ada_w"], new_m["ada_w"], new_v["ada_w"] = d_.reshape(shp), m_.reshape(shp), v_.reshape(shp)
    d_, m_, v_ = _adamw(_pack_small(wts), g_small_pk, _pack_small(ms), _pack_small(vs), "adamw_small")
    for dst, pk in ((deltas, d_), (new_m, m_), (new_v, v_)):
        dst.update(_unpack_small(pk, shapes))

    return (g_small_pk[SG_LOSS, 0], grad_x.reshape(1, s, D), *[grads[n] for n in names], *[deltas[n] for n in names],
            *[new_m[n] for n in names], *[new_v[n] for n in names])
```

```python
import math

import jax
import jax.numpy as jnp
from jax import lax
from jax.experimental import pallas as pl
from jax.experimental.pallas import tpu as pltpu

F32, BF16 = jnp.float32, jnp.bfloat16
MESH = pl.DeviceIdType.MESH

D = 1024
EPS = 1e-6
GLA_H, GLA_DK, GLA_DV, GLA_LR = 4, 128, 256, 16
GLA_TAU = 16.0
GLA_CHUNK = 64
GLA_BLOCK = 512
ATT_GROUPS = ((128, 1), (512, 4), (2048, 16))
ATT_BLK = 128
ATT_HD = 64
ATT_W = 768
D_FF = 2816
ROPE_THETA = 10000.0
P_W = 7680
P_GV, P_GR, P_MA, P_MB, P_GQ, P_GK, P_AQ, P_AK, P_AV, P_LR = 0, 1024, 2048, 3072, 4096, 4608, 5120, 5888, 6656, 7424
W_IN = 7440
W_IN_SH, W_UP_SH, W_DOWN_SH = 1860, 1408, 704
VMEM_LIMIT = 56 * 1024 * 1024
ADAM_LR, ADAM_B1, ADAM_B2, ADAM_EPS, ADAM_WD, ADAM_STEP = 0.001, 0.9, 0.999, 1e-08, 0.01, 10
NEG = -1e30


def _tile(n, target, unit=128):
    best = None
    for t in range(unit, min(n, target) + 1, unit):
        if n % t == 0:
            best = t
    return best or n


def _params(sem):
    return pltpu.CompilerParams(dimension_semantics=sem, vmem_limit_bytes=VMEM_LIMIT)


def _dg(a, b, ca, cb):
    return lax.dot_general(a, b, (((ca,), (cb,)), ((), ())), preferred_element_type=F32)


def _sigmoid(v):
    return 1.0 / (1.0 + jnp.exp(-v))


def _ff_block(j):
    return (j % 2) * 2 + j // 2


def _mm(a, b, name, *, ta=False, tb=False, out_dtype=BF16, tm=1024, tn=1536, tk=1024, n_outer=True, comm=(),
        b_shards=False, o_shards=False):
    m = a.shape[1] if ta else a.shape[0]
    k = a.shape[0] if ta else a.shape[1]
    if b_shards:
        n = b.shape[1] if tb else 4 * W_UP_SH
        tn, tk = (tn, W_UP_SH) if tb else (W_UP_SH, tk)
    else:
        n = b.shape[0] if tb else b.shape[1]
    if o_shards:
        tn = W_UP_SH
    tm, tn, tk = _tile(m, tm), _tile(n, tn), _tile(k, tk)
    nm, nn, nk = m // tm, n // tn, k // tk
    in_out = out_dtype == F32
    c_ins, c_outs, c_alias, c_scratch = _carry(comm, 2, 1)

    def body(a_ref, b_ref, *rest):
        ci, o_ref, co = rest[:len(c_ins)], rest[len(c_ins)], rest[len(c_ins) + 1:len(c_ins) + 1 + len(c_outs)]
        scr = rest[len(c_ins) + 1 + len(c_outs):]
        kk = pl.program_id(2)
        if comm:
            step = (pl.program_id(0) * (nm if n_outer else nn) + pl.program_id(1)) * nk + kk

            @pl.when(step == 0)
            def _():
                _comm_phase(comm, ci, co, scr[-2], scr[-1], True)

        _mm_step(a_ref, b_ref, o_ref, scr, kk)
        if comm:
            @pl.when(step == nm * nn * nk - 1)
            def _():
                _comm_phase(comm, ci, co, scr[-2], scr[-1], False)

    def _mm_step(a_ref, b_ref, o_ref, scr, kk):
        p = _dg(a_ref[...].astype(BF16), b_ref[...].astype(BF16), 0 if ta else 1, 1 if tb else 0)
        if nk == 1:
            o_ref[...] = p.astype(o_ref.dtype)
        else:
            acc = o_ref if in_out else scr[0]

            @pl.when(kk == 0)
            def _():
                acc[...] = p

            @pl.when(kk > 0)
            def _():
                acc[...] += p

            if not in_out:
                @pl.when(kk == nk - 1)
                def _():
                    o_ref[...] = acc[...].astype(o_ref.dtype)

    if n_outer:
        ij = lambda g0, g1: (g1, g0)
        grid = (nn, nm, nk)
    else:
        ij = lambda g0, g1: (g0, g1)
        grid = (nm, nn, nk)
    a_map = (lambda g0, g1, kk: (kk, ij(g0, g1)[0])) if ta else (lambda g0, g1, kk: (ij(g0, g1)[0], kk))
    if b_shards and tb:
        b_spec = pl.BlockSpec((None, tn, tk), lambda g0, g1, kk: (_ff_block(kk), ij(g0, g1)[1], 0))
    elif b_shards:
        b_spec = pl.BlockSpec((None, tk, tn), lambda g0, g1, kk: (_ff_block(ij(g0, g1)[1]), kk, 0))
    elif tb:
        b_spec = pl.BlockSpec((tn, tk), lambda g0, g1, kk: (ij(g0, g1)[1], kk))
    else:
        b_spec = pl.BlockSpec((tk, tn), lambda g0, g1, kk: (kk, ij(g0, g1)[1]))
    if o_shards:
        o_spec = pl.BlockSpec((None, tm, tn), lambda g0, g1, kk: (_ff_block(ij(g0, g1)[1]), ij(g0, g1)[0], 0))
        o_shape = jax.ShapeDtypeStruct((4, m, W_UP_SH), out_dtype)
    else:
        o_spec = pl.BlockSpec((tm, tn), lambda g0, g1, kk: ij(g0, g1))
        o_shape = jax.ShapeDtypeStruct((m, n), out_dtype)
    res = pl.pallas_call(
        body, name=name, grid=grid,
        in_specs=[pl.BlockSpec((tk, tm) if ta else (tm, tk), a_map), b_spec] + [HBM] * len(c_ins),
        out_specs=[o_spec] + [HBM] * len(c_outs),
        out_shape=[o_shape] + c_outs,
        scratch_shapes=([] if (in_out or nk == 1) else [pltpu.VMEM((tm, tn), F32)]) + c_scratch,
        input_output_aliases=c_alias,
        compiler_params=_params(("arbitrary",) * 3 if comm else ("parallel", "parallel", "arbitrary")),
    )(a, b, *c_ins)
    return (res[0], _split_units(comm, res[1:])) if comm else res[0]


def _rows(arr, rb, w=None, j=0):
    w = arr.shape[1] if w is None else w
    if callable(j):
        return arr, pl.BlockSpec((rb, w), lambda c, i: (i, j(c)))
    return arr, pl.BlockSpec((rb, w), lambda c, i: (i, j))


def _full(arr, w=None, j=0):
    w = arr.shape[1] if w is None else w
    if callable(j):
        return arr, pl.BlockSpec((arr.shape[0], w), lambda c, i: (0, j(c)))
    return arr, pl.BlockSpec((arr.shape[0], w), lambda c, i: (0, j))


def _halo(arr, rb, hb, w, j, before):
    per = rb // hb
    last = arr.shape[0] // hb - 1
    if before:
        rmap = lambda i: jnp.maximum(i * per - 1, 0)
    else:
        rmap = lambda i: jnp.minimum((i + 1) * per, last)
    return arr, pl.BlockSpec((hb, w), lambda c, i: (rmap(i), j(c) if callable(j) else j))


def _rowcall(fn, ins, outs, *, n_rows, rb, name, ncol=1, into=None):
    n_in = len(ins)
    nr = n_rows // rb
    n_skip = 0 if into is None else 1

    def body(*refs):
        c, i = pl.program_id(0), pl.program_id(1)
        res = fn(c, i, *[r[...] for r in refs[:n_in]])
        for val, spec, o_ref in zip(res, outs, refs[n_in + n_skip:]):
            if spec[2] == "row":
                o_ref[...] = val.astype(o_ref.dtype)
            else:
                @pl.when(i == 0)
                def _(o_ref=o_ref, val=val):
                    o_ref[...] = val.astype(o_ref.dtype)

                @pl.when(i > 0)
                def _(o_ref=o_ref, val=val):
                    o_ref[...] += val.astype(o_ref.dtype)

    out_specs = []
    for shape, dt, kind, block, col in outs:
        if kind == "row":
            out_specs.append(pl.BlockSpec(block, lambda c, i, col=col: (i, col(c))))
        else:
            out_specs.append(pl.BlockSpec(block, lambda c, i, col=col: (0, col(c))))
    return pl.pallas_call(
        body, name=name, grid=(ncol, nr),
        in_specs=[s for _, s in ins] + [pl.BlockSpec(memory_space=pl.ANY)] * n_skip, out_specs=out_specs,
        out_shape=[jax.ShapeDtypeStruct(o[0], o[1]) for o in outs],
        input_output_aliases={} if into is None else {n_in: into[1]},
        compiler_params=_params(("parallel", "arbitrary")),
    )(*[a for a, _ in ins], *([] if into is None else [into[0]]))


def _orow(n_rows, w, dt, rb, bw=None, col=lambda c: 0):
    return ((n_rows, w), dt, "row", (rb, bw or w), col)


def _oacc(r, w, bw=None, col=lambda c: 0):
    return ((r, w), F32, "acc", (r, bw or w), col)


def _csum(v):
    return jnp.sum(v, axis=0, keepdims=True)


def _rms(v):
    return lax.rsqrt(jnp.mean(v * v, axis=-1, keepdims=True) + EPS)


def _norm_bwd(xv, dh, w, scale):
    r = _rms(xv)
    xh = xv * r
    dxh = dh * (w * (1.0 + scale))
    dx = r * (dxh - xh * jnp.mean(dxh * xh, axis=-1, keepdims=True))
    t = dh * xh
    return dx, _csum(dh), _csum(t * w), _csum(t * (1.0 + scale))


def _rope_tables(pos_col, invf, s):
    def fn(c, i, pos, f):
        ang = pos.astype(F32) * f
        lane = lax.broadcasted_iota(jnp.int32, ang.shape, 1)
        sign = jnp.where((lane % ATT_HD) < ATT_HD // 2, -1.0, 1.0)
        return jnp.cos(ang), jnp.sin(ang) * sign

    rb = 512
    return _rowcall(fn, [_rows(pos_col, rb), _full(invf)], [_orow(s, 128, F32, rb), _orow(s, 128, F32, rb)],
                    n_rows=s, rb=rb, name="rope_tables")


def _swap_halves(t):
    n = t.shape[1]
    lane = lax.broadcasted_iota(jnp.int32, t.shape, 1)
    return jnp.where((lane % ATT_HD) < ATT_HD // 2, pltpu.roll(t, n - 32, 1), pltpu.roll(t, 32, 1))


def _rope_apply(t, cos, sin_signed, inverse):
    cw = jnp.concatenate([cos] * (t.shape[1] // 128), axis=1)
    sw = jnp.concatenate([sin_signed] * (t.shape[1] // 128), axis=1)
    if inverse:
        sw = -sw
    return t * cw + _swap_halves(t) * sw


DIL_ROWS = 512


def _to_dilated(scr, val, out_ref, r):
    if r == 1:
        out_ref[...] = val.astype(out_ref.dtype)
        return
    n = val.shape[0] // r
    for hh in range(2):
        scr[hh] = val[:, hh * 128:(hh + 1) * 128]
        for pr in range(r):
            out_ref[:, pr * 256 + hh * 128:pr * 256 + (hh + 1) * 128] = scr[hh, pl.ds(pr, n, stride=r), :].astype(out_ref.dtype)


def _from_dilated(scr, in_ref, r):
    if r == 1:
        return in_ref[...].astype(F32)
    n = in_ref.shape[0]
    for hh in range(2):
        for pr in range(r):
            scr[hh, pl.ds(pr, n, stride=r), :] = in_ref[:, pr * 256 + hh * 128:pr * 256 + (hh + 1) * 128].astype(F32)
    return jnp.concatenate([scr[0], scr[1]], axis=1)


def _dil_spec(r):
    return pl.BlockSpec((DIL_ROWS // r, r * 256), lambda i: (i, 0))


def _dil_shape(s, r, dt):
    return jax.ShapeDtypeStruct((s // r, r * 256), dt)


_DIL_SCRATCH = [pltpu.VMEM((2, DIL_ROWS, 128), F32)]
_RS = tuple(r for _, r in ATT_GROUPS)


def _rope_fwd(p, cos_t, sin_t, s):
    def body(*refs):
        ins, cs, sn, outs, scr = refs[:9], refs[9][...], refs[10][...], refs[11:20], refs[20]
        for t in range(3):
            for g, r in enumerate(_RS):
                val = ins[3 * t + g][...].astype(F32)
                _to_dilated(scr, _rope_apply(val, cs, sn, False) if t < 2 else val, outs[3 * t + g], r)

    res = pl.pallas_call(
        body, name="rope", grid=(s // DIL_ROWS,),
        in_specs=[pl.BlockSpec((DIL_ROWS, 256), lambda i, c=base // 256 + g: (i, c)) for base in (P_AQ, P_AK, P_AV) for g in range(3)]
        + [pl.BlockSpec((DIL_ROWS, 128), lambda i: (i, 0))] * 2,
        out_specs=[_dil_spec(r) for _ in range(3) for r in _RS],
        out_shape=[_dil_shape(s, r, BF16) for _ in range(3) for r in _RS],
        scratch_shapes=_DIL_SCRATCH, compiler_params=_params(("parallel",)),
    )(*([p] * 9), cos_t, sin_t)
    return res[0:3], res[3:6], res[6:9]


def _attn_combine(att, s):
    def body(o0, o1, o2, l0, l1, l2, o_ref, lse_ref, od1, od2, ld1, ld2, scr):
        ov = [_from_dilated(scr, ref, r) for ref, r in zip((o0, o1, o2), _RS)]
        lv = [_from_dilated(scr, ref, r) for ref, r in zip((l0, l1, l2), _RS)]
        mx = jnp.maximum(jnp.maximum(lv[0], lv[1]), lv[2])
        ev = [jnp.exp(l - mx) for l in lv]
        z = ev[0] + ev[1] + ev[2]
        o = ((ev[0] * ov[0] + ev[1] * ov[1] + ev[2] * ov[2]) / z).astype(BF16)
        lse = mx + jnp.log(z)
        o_ref[...] = o
        lse_ref[...] = lse
        for ref, r in zip((od1, od2), _RS[1:]):
            _to_dilated(scr, o.astype(F32), ref, r)
        for ref, r in zip((ld1, ld2), _RS[1:]):
            _to_dilated(scr, lse, ref, r)

    return pl.pallas_call(
        body, name="attn_combine", grid=(s // DIL_ROWS,),
        in_specs=[_dil_spec(r) for r in _RS] * 2,
        out_specs=[_dil_spec(1)] * 2 + [_dil_spec(r) for r in _RS[1:]] * 2,
        out_shape=[_dil_shape(s, 1, BF16), _dil_shape(s, 1, F32)] + [_dil_shape(s, r, BF16) for r in _RS[1:]]
        + [_dil_shape(s, r, F32) for r in _RS[1:]],
        scratch_shapes=_DIL_SCRATCH, compiler_params=_params(("parallel",)),
    )(*[a[0] for a in att], *[a[1] for a in att])


def _dilate(t, s):
    def body(t_ref, o1, o2, scr):
        val = t_ref[...].astype(F32)
        for ref, r in zip((o1, o2), _RS[1:]):
            _to_dilated(scr, val, ref, r)

    return pl.pallas_call(
        body, name="attn_dilate", grid=(s // DIL_ROWS,), in_specs=[_dil_spec(1)], out_specs=[_dil_spec(r) for r in _RS[1:]],
        out_shape=[_dil_shape(s, r, t.dtype) for r in _RS[1:]], scratch_shapes=_DIL_SCRATCH, compiler_params=_params(("parallel",)),
    )(t)


def _rope_bwd(datt, d_glr, dp, cos_t, sin_t, s):
    tail = P_W - P_AQ

    def body(*refs):
        ins, cs, sn, glr, o_ref, scr = refs[:9], refs[9][...], refs[10][...], refs[11], refs[13], refs[14]
        for t in range(3):
            for g, r in enumerate(_RS):
                val = _from_dilated(scr, ins[3 * t + g], r)
                o_ref[:, t * ATT_W + g * 256:t * ATT_W + (g + 1) * 256] = (_rope_apply(val, cs, sn, True) if t < 2 else val).astype(BF16)
        o_ref[:, 3 * ATT_W:3 * ATT_W + 128] = glr[...]
        o_ref[:, 3 * ATT_W + 128:] = jnp.zeros((DIL_ROWS, tail - 3 * ATT_W - 128), BF16)

    return pl.pallas_call(
        body, name="rope_bwd", grid=(s // DIL_ROWS,),
        in_specs=[_dil_spec(r) for _ in range(3) for r in _RS] + [pl.BlockSpec((DIL_ROWS, 128), lambda i: (i, 0))] * 3
        + [pl.BlockSpec(memory_space=pl.ANY)],
        out_specs=pl.BlockSpec((DIL_ROWS, tail), lambda i: (i, P_AQ // tail)),
        out_shape=jax.ShapeDtypeStruct((s, P_W), BF16), input_output_aliases={12: 0},
        scratch_shapes=_DIL_SCRATCH, compiler_params=_params(("parallel",)),
    )(*[datt[g][t] for t in range(3) for g in range(3)], cos_t, sin_t, d_glr, dp)


def _tri_dot(tri, t):
    tb = tri.astype(BF16)
    hi = t.astype(BF16)
    r1 = t - hi.astype(F32)
    mid = r1.astype(BF16)
    lo = (r1 - mid.astype(F32)).astype(BF16)
    return _dg(tb, hi, 1, 0) + _dg(tb, mid, 1, 0) + _dg(tb, lo, 1, 0)


def _gla_decays(la_c, tri):
    b = _tri_dot(tri, la_c)
    row = lax.broadcasted_iota(jnp.int32, b.shape, 0)
    bmid = jnp.sum(jnp.where(row == GLA_CHUNK // 2 - 1, b, 0.0), axis=0, keepdims=True)
    blast = jnp.sum(jnp.where(row == GLA_CHUNK - 1, b, 0.0), axis=0, keepdims=True)
    return b, bmid, blast


def _gla_fwd(p, la, s, comm=()):
    tb, ch = GLA_BLOCK, GLA_CHUNK
    nb, nc = s // tb, tb // ch
    scale = GLA_DK ** -0.5
    c_ins, c_outs, c_alias, c_scratch = _carry(comm, 4, 2)

    def body(q_ref, k_ref, v_ref, la_ref, *rest):
        ci, (o_ref, st_ref) = rest[:len(c_ins)], rest[len(c_ins):len(c_ins) + 2]
        co, state = rest[len(c_ins) + 2:len(c_ins) + 2 + len(c_outs)], rest[len(c_ins) + 2 + len(c_outs)]
        step = pl.program_id(0)
        if comm:
            @pl.when(step == 0)
            def _():
                _comm_phase(comm, ci, co, rest[-2], rest[-1], True)

        _gla_fwd_step(q_ref, k_ref, v_ref, la_ref, o_ref, st_ref, state)
        if comm:
            @pl.when(step == nb - 1)
            def _():
                _comm_phase(comm, ci, co, rest[-2], rest[-1], False)

    def _gla_fwd_step(q_ref, k_ref, v_ref, la_ref, o_ref, st_ref, state):
        @pl.when(pl.program_id(0) == 0)
        def _():
            state[...] = jnp.zeros_like(state)

        ri = lax.broadcasted_iota(jnp.int32, (ch, ch), 0)
        ci = lax.broadcasted_iota(jnp.int32, (ch, ch), 1)
        causal = ci <= ri
        tri = causal.astype(F32)

        def chunk(c, carry):
            sl = pl.ds(pl.multiple_of(c * ch, ch), ch)
            b, bmid, blast = _gla_decays(la_ref[sl, :], tri)
            q = q_ref[sl, :].astype(F32) * scale
            k = k_ref[sl, :].astype(F32)
            v = v_ref[sl, :]
            qgt = (q * jnp.exp(b)).astype(BF16)
            qgn = (q * jnp.exp(b - bmid)).astype(BF16)
            kgn = (k * jnp.exp(bmid - b)).astype(BF16)
            kd = (k * jnp.exp(blast - b)).astype(BF16)
            dec = jnp.exp(blast)
            sts = [state[h] for h in range(GLA_H)]
            outs, news = [], []
            for h in range(GLA_H):
                hk, hv = slice(h * GLA_DK, (h + 1) * GLA_DK), slice(h * GLA_DV, (h + 1) * GLA_DV)
                a = jnp.where(causal, _dg(qgn[:, hk], kgn[:, hk], 1, 1), 0.0)
                outs.append(_dg(a.astype(BF16), v[:, hv], 1, 0) + _dg(qgt[:, hk], sts[h].astype(BF16), 1, 1))
                news.append(dec[:, hk] * sts[h] + _dg(v[:, hv], kd[:, hk], 0, 0))
            for h in range(GLA_H):
                st_ref[h, c] = sts[h]
                state[h] = news[h]
            o_ref[sl, :] = jnp.concatenate(outs, axis=1)
            return carry

        lax.fori_loop(0, nc, chunk, 0)

    hw = GLA_H * GLA_DK
    res = pl.pallas_call(
        body, name="gla_fwd", grid=(nb,),
        in_specs=[pl.BlockSpec((tb, hw), lambda t: (t, P_GQ // hw)),
                  pl.BlockSpec((tb, hw), lambda t: (t, P_GK // hw)),
                  pl.BlockSpec((tb, GLA_H * GLA_DV), lambda t: (t, P_GV // (GLA_H * GLA_DV))),
                  pl.BlockSpec((tb, hw), lambda t: (t, 0))] + [HBM] * len(c_ins),
        out_specs=[pl.BlockSpec((tb, GLA_H * GLA_DV), lambda t: (t, 0)),
                   pl.BlockSpec((GLA_H, nc, GLA_DV, GLA_DK), lambda t: (0, t, 0, 0))] + [HBM] * len(c_outs),
        out_shape=[jax.ShapeDtypeStruct((s, GLA_H * GLA_DV), F32),
                   jax.ShapeDtypeStruct((GLA_H, s // ch, GLA_DV, GLA_DK), F32)] + c_outs,
        scratch_shapes=[pltpu.VMEM((GLA_H, GLA_DV, GLA_DK), F32)] + c_scratch,
        input_output_aliases=c_alias,
        compiler_params=_params(("arbitrary",)),
    )(p, p, p, la, *c_ins)
    return res[0], res[1], _split_units(comm, res[2:])


def _gla_bwd(p, la, states, do, s, dp, comm=()):
    tb, ch = GLA_BLOCK, GLA_CHUNK
    nb, nc = s // tb, tb // ch
    scale = GLA_DK ** -0.5
    c_ins, c_outs, c_alias, c_scratch = _carry(comm, 7, 4)

    def body(q_ref, k_ref, v_ref, la_ref, st_ref, do_ref, dp_in, *rest):
        ci, outs = rest[:len(c_ins)], rest[len(c_ins):len(c_ins) + 4]
        co, dstate = rest[len(c_ins) + 4:len(c_ins) + 4 + len(c_outs)], rest[len(c_ins) + 4 + len(c_outs)]
        step = pl.program_id(0)
        if comm:
            @pl.when(step == 0)
            def _():
                _comm_phase(comm, ci, co, rest[-2], rest[-1], True)

        _gla_bwd_step(q_ref, k_ref, v_ref, la_ref, st_ref, do_ref, *outs, dstate)
        if comm:
            @pl.when(step == nb - 1)
            def _():
                _comm_phase(comm, ci, co, rest[-2], rest[-1], False)

    def _gla_bwd_step(q_ref, k_ref, v_ref, la_ref, st_ref, do_ref, dq_ref, dk_ref, dv_ref, dla_ref, dstate):
        @pl.when(pl.program_id(0) == 0)
        def _():
            dstate[...] = jnp.zeros_like(dstate)

        ri = lax.broadcasted_iota(jnp.int32, (ch, ch), 0)
        ci = lax.broadcasted_iota(jnp.int32, (ch, ch), 1)
        causal = ci <= ri
        tri = causal.astype(F32)
        tri_t = (ci >= ri).astype(F32)

        def chunk(cc, carry):
            c = nc - 1 - cc
            sl = pl.ds(pl.multiple_of(c * ch, ch), ch)
            b, bmid, blast = _gla_decays(la_ref[sl, :], tri)
            q = q_ref[sl, :].astype(F32) * scale
            k = k_ref[sl, :].astype(F32)
            v = v_ref[sl, :]
            e_b, e_qn, e_kn, e_kd = jnp.exp(b), jnp.exp(b - bmid), jnp.exp(bmid - b), jnp.exp(blast - b)
            dec = jnp.exp(blast)
            qgt, qgn, kgn, kd = q * e_b, q * e_qn, k * e_kn, k * e_kd
            qgt_b, qgn_b, kgn_b, kd_b = qgt.astype(BF16), qgn.astype(BF16), kgn.astype(BF16), kd.astype(BF16)
            do_b = do_ref[sl, :].astype(BF16)
            st0s = [st_ref[h, c] for h in range(GLA_H)]
            dsts = [dstate[h] for h in range(GLA_H)]
            dqgn, dqgt, dkgn, dkd, dvs, ddec, news = [], [], [], [], [], [], []
            for h in range(GLA_H):
                hk, hv = slice(h * GLA_DK, (h + 1) * GLA_DK), slice(h * GLA_DV, (h + 1) * GLA_DV)
                dst_b = dsts[h].astype(BF16)
                a = jnp.where(causal, _dg(qgn_b[:, hk], kgn_b[:, hk], 1, 1), 0.0).astype(BF16)
                da = jnp.where(causal, _dg(do_b[:, hv], v[:, hv], 1, 1), 0.0).astype(BF16)
                dqgn.append(_dg(da, kgn_b[:, hk], 1, 0))
                dqgt.append(_dg(do_b[:, hv], st0s[h].astype(BF16), 1, 0))
                dkgn.append(_dg(da, qgn_b[:, hk], 0, 0))
                dvs.append(_dg(a, do_b[:, hv], 0, 0) + _dg(kd_b[:, hk], dst_b, 1, 1))
                dkd.append(_dg(v[:, hv], dst_b, 1, 0))
                ddec.append(jnp.sum(st0s[h] * dsts[h], axis=0, keepdims=True))
                news.append(dec[:, hk] * dsts[h] + _dg(do_b[:, hv], qgt_b[:, hk], 0, 0))
            for h in range(GLA_H):
                dstate[h] = news[h]
            cat = lambda parts: jnp.concatenate(parts, axis=1)
            dqgn, dqgt, dkgn, dkd, ddec = cat(dqgn), cat(dqgt), cat(dkgn), cat(dkd), cat(ddec)
            dq_ref[sl, :] = (scale * (dqgn * e_qn + dqgt * e_b)).astype(dq_ref.dtype)
            dk_ref[sl, :] = (dkgn * e_kn + dkd * e_kd).astype(dk_ref.dtype)
            dv_ref[sl, :] = cat(dvs).astype(dv_ref.dtype)
            db = dqgn * qgn + dqgt * qgt - dkgn * kgn - dkd * kd
            extra = jnp.sum(dkd * kd, axis=0, keepdims=True) + ddec * dec
            dla_ref[sl, :] = _tri_dot(tri_t, db) + extra
            return carry

        lax.fori_loop(0, nc, chunk, 0)

    rev = lambda t: nb - 1 - t
    hw, vw = GLA_H * GLA_DK, GLA_H * GLA_DV
    res = pl.pallas_call(
        body, name="gla_bwd", grid=(nb,),
        in_specs=[pl.BlockSpec((tb, hw), lambda t: (rev(t), P_GQ // hw)),
                  pl.BlockSpec((tb, hw), lambda t: (rev(t), P_GK // hw)),
                  pl.BlockSpec((tb, vw), lambda t: (rev(t), P_GV // vw)),
                  pl.BlockSpec((tb, hw), lambda t: (rev(t), 0)),
                  pl.BlockSpec((GLA_H, nc, GLA_DV, GLA_DK), lambda t: (0, rev(t), 0, 0)),
                  pl.BlockSpec((tb, vw), lambda t: (rev(t), 0)), pl.BlockSpec(memory_space=pl.ANY)] + [HBM] * len(c_ins),
        out_specs=[pl.BlockSpec((tb, hw), lambda t: (rev(t), 0)),
                   pl.BlockSpec((tb, hw), lambda t: (rev(t), 0)),
                   pl.BlockSpec((tb, vw), lambda t: (rev(t), P_GV // vw)),
                   pl.BlockSpec((tb, hw), lambda t: (rev(t), 0))] + [HBM] * len(c_outs),
        out_shape=[jax.ShapeDtypeStruct((s, hw), BF16),
                   jax.ShapeDtypeStruct((s, hw), BF16),
                   jax.ShapeDtypeStruct((s, P_W), BF16),
                   jax.ShapeDtypeStruct((s, hw), F32)] + c_outs,
        scratch_shapes=[pltpu.VMEM((GLA_H, GLA_DV, GLA_DK), F32)] + c_scratch,
        input_output_aliases={6: 2, **c_alias},
        compiler_params=_params(("arbitrary",)),
    )(p, p, p, la, states, do, dp, *c_ins)
    return res[0], res[1], res[2], res[3], _split_units(comm, res[4:])


def _head_masks():
    lane = lax.broadcasted_iota(jnp.int32, (1, 4 * ATT_HD), 1)
    return [(lane >= h * ATT_HD) & (lane < (h + 1) * ATT_HD) for h in range(4)]


def _attn_fwd(qv, kv, pv, g, r, s):
    ln = s // r
    nblk = ln // ATT_BLK
    qcol = lambda pr: pr
    vcol = qcol
    prev = lambda n: jnp.maximum(n - 1, 0)

    def body(q_ref, kp_ref, kc_ref, vp_ref, vc_ref, o_ref, lse_ref):
        has_prev = pl.program_id(1) > 0
        ri = lax.broadcasted_iota(jnp.int32, (ATT_BLK, ATT_BLK), 0)
        ci = lax.broadcasted_iota(jnp.int32, (ATT_BLK, ATT_BLK), 1)
        m_cur = ci <= ri
        m_prev = (ci >= ri) & has_prev
        q, kp, kc, vp, vc = q_ref[...], kp_ref[...], kc_ref[...], vp_ref[...], vc_ref[...]
        o = jnp.zeros((ATT_BLK, 256), F32)
        lse = jnp.zeros((ATT_BLK, 256), F32)
        for hm in _head_masks():
            qm = jnp.where(hm, q, jnp.zeros_like(q))
            sc = jnp.where(m_cur, _dg(qm, kc, 1, 1) * 0.125, NEG)
            sp = jnp.where(m_prev, _dg(qm, kp, 1, 1) * 0.125, NEG)
            mx = jnp.maximum(jnp.max(sc, axis=1, keepdims=True), jnp.max(sp, axis=1, keepdims=True))
            pc, pp = jnp.exp(sc - mx), jnp.exp(sp - mx)
            den = jnp.sum(pc, axis=1, keepdims=True) + jnp.sum(pp, axis=1, keepdims=True)
            oh = (_dg(pc.astype(BF16), vc, 1, 0) + _dg(pp.astype(BF16), vp, 1, 0)) / den
            o = jnp.where(hm, oh, o)
            lse = jnp.where(hm, mx + jnp.log(den), lse)
        o_ref[...] = o.astype(o_ref.dtype)
        lse_ref[...] = lse

    blk = (ATT_BLK, 256)
    o, lse = pl.pallas_call(
        body, name=f"attn_fwd_{g}", grid=(r, nblk),
        in_specs=[pl.BlockSpec(blk, lambda pr, n: (n, qcol(pr))),
                  pl.BlockSpec(blk, lambda pr, n: (prev(n), qcol(pr))),
                  pl.BlockSpec(blk, lambda pr, n: (n, qcol(pr))),
                  pl.BlockSpec(blk, lambda pr, n: (prev(n), vcol(pr))),
                  pl.BlockSpec(blk, lambda pr, n: (n, vcol(pr)))],
        out_specs=[pl.BlockSpec(blk, lambda pr, n: (n, pr)), pl.BlockSpec(blk, lambda pr, n: (n, pr))],
        out_shape=[jax.ShapeDtypeStruct((ln, r * 256), BF16), jax.ShapeDtypeStruct((ln, r * 256), F32)],
        compiler_params=_params(("parallel", "parallel")),
    )(qv, kv, kv, pv, pv)
    return o, lse


def _attn_bwd(qv, kv, pv, dov, ov, lv, g, r, s):
    ln = s // r
    nblk = ln // ATT_BLK
    qcol = lambda pr: pr
    vcol = qcol
    prev = lambda n: jnp.maximum(n - 1, 0)
    nxt = lambda n: jnp.minimum(n + 1, nblk - 1)

    def body(qc_ref, qn_ref, kp_ref, kc_ref, vp_ref, vc_ref, doc_ref, don_ref, oc_ref, on_ref, lc_ref, ln_ref,
             dq_ref, dk_ref, dv_ref):
        n = pl.program_id(1)
        has_prev, has_next = n > 0, n < nblk - 1
        ri = lax.broadcasted_iota(jnp.int32, (ATT_BLK, ATT_BLK), 0)
        ci = lax.broadcasted_iota(jnp.int32, (ATT_BLK, ATT_BLK), 1)
        m_cur = ci <= ri
        m_prev = (ci >= ri) & has_prev
        m_next = (ci >= ri) & has_next
        qc, qn, kp, kc, vp, vc = qc_ref[...], qn_ref[...], kp_ref[...], kc_ref[...], vp_ref[...], vc_ref[...]
        doc, don = doc_ref[...], don_ref[...]
        pc_full = doc.astype(F32) * oc_ref[...].astype(F32)
        pn_full = don.astype(F32) * on_ref[...].astype(F32)
        lc, lnx = lc_ref[...], ln_ref[...]
        dq = jnp.zeros((ATT_BLK, 256), F32)
        dk = jnp.zeros((ATT_BLK, 256), F32)
        dv = jnp.zeros((ATT_BLK, 256), F32)
        zb = jnp.zeros_like(qc)
        for hm in _head_masks():
            qcm, qnm = jnp.where(hm, qc, zb), jnp.where(hm, qn, zb)
            docm, donm = jnp.where(hm, doc, zb), jnp.where(hm, don, zb)
            lse_c = jnp.max(jnp.where(hm, lc, NEG), axis=1, keepdims=True)
            lse_n = jnp.max(jnp.where(hm, lnx, NEG), axis=1, keepdims=True)
            del_c = jnp.sum(jnp.where(hm, pc_full, 0.0), axis=1, keepdims=True)
            del_n = jnp.sum(jnp.where(hm, pn_full, 0.0), axis=1, keepdims=True)
            pr_ = jnp.where(m_cur, jnp.exp(_dg(qcm, kc, 1, 1) * 0.125 - lse_c), 0.0)
            ds = (pr_ * (_dg(docm, vc, 1, 1) - del_c) * 0.125).astype(BF16)
            dqh = _dg(ds, kc, 1, 0)
            dkh = _dg(ds, qc, 0, 0)
            dvh = _dg(pr_.astype(BF16), doc, 0, 0)
            pr_ = jnp.where(m_prev, jnp.exp(_dg(qcm, kp, 1, 1) * 0.125 - lse_c), 0.0)
            ds = (pr_ * (_dg(docm, vp, 1, 1) - del_c) * 0.125).astype(BF16)
            dqh = dqh + _dg(ds, kp, 1, 0)
            pr_ = jnp.where(m_next, jnp.exp(_dg(qnm, kc, 1, 1) * 0.125 - lse_n), 0.0)
            ds = (pr_ * (_dg(donm, vc, 1, 1) - del_n) * 0.125).astype(BF16)
            dkh = dkh + _dg(ds, qn, 0, 0)
            dvh = dvh + _dg(pr_.astype(BF16), don, 0, 0)
            dq = jnp.where(hm, dqh, dq)
            dk = jnp.where(hm, dkh, dk)
            dv = jnp.where(hm, dvh, dv)
        dq_ref[...] = dq.astype(dq_ref.dtype)
        dk_ref[...] = dk.astype(dk_ref.dtype)
        dv_ref[...] = dv.astype(dv_ref.dtype)

    blk = (ATT_BLK, 256)
    cur = lambda col: pl.BlockSpec(blk, lambda pr, n: (n, col(pr)))
    prv = lambda col: pl.BlockSpec(blk, lambda pr, n: (prev(n), col(pr)))
    nx = lambda col: pl.BlockSpec(blk, lambda pr, n: (nxt(n), col(pr)))
    own = lambda pr: pr
    outs = pl.pallas_call(
        body, name=f"attn_bwd_{g}", grid=(r, nblk),
        in_specs=[cur(qcol), nx(qcol), prv(qcol), cur(qcol), prv(vcol), cur(vcol),
                  cur(own), nx(own), cur(own), nx(own), cur(own), nx(own)],
        out_specs=[cur(own), cur(own), cur(own)],
        out_shape=[jax.ShapeDtypeStruct((ln, r * 256), BF16)] * 3,
        compiler_params=_params(("parallel", "parallel")),
    )(qv, qv, kv, kv, pv, pv, dov, dov, ov, ov, lv, lv)
    return outs


def _gelu_parts(gv):
    cdf = 0.5 * (1.0 + lax.erf(gv * (2.0 ** -0.5)))
    pdf = jnp.exp(-0.5 * gv * gv) * (1.0 / math.sqrt(2.0 * math.pi))
    return cdf, pdf


def _pick_row(t, k):
    row = lax.broadcasted_iota(jnp.int32, t.shape, 0)
    return jnp.sum(jnp.where(row == k, t, 0.0), axis=0, keepdims=True)


def _shift_rows(u, halo, n):
    row = lax.broadcasted_iota(jnp.int32, u.shape, 0)
    out = pltpu.roll(u, n, 0)
    for k in range(n):
        out = jnp.where(row == k, _pick_row(halo, 16 - n + k), out)
    return out


def _shift_rows_up(u, halo, n):
    rb = u.shape[0]
    row = lax.broadcasted_iota(jnp.int32, u.shape, 0)
    out = pltpu.roll(u, rb - n, 0)
    for k in range(n):
        out = jnp.where(row == rb - n + k, _pick_row(halo, k), out)
    return out


def _conv(u, halo, cw, cb):
    return cb + _pick_row(cw, 0) * _shift_rows(u, halo, 2) + _pick_row(cw, 1) * _shift_rows(u, halo, 1) + _pick_row(cw, 2) * u


def _local_step(x, mod, pos_col, target, sm, w_sh, chip, core):
    s = x.shape[0]
    shift1, scale1, gate1, shift2, scale2, gate2 = [mod[i:i + 1, :] for i in range(6)]
    rb = 256
    chip1 = chip.reshape(1)

    def f_norm1(c, i, xv, nw, sc, sh):
        return ((xv * _rms(xv) * nw) * (1.0 + sc) + sh,)

    (h,) = _rowcall(f_norm1, [_rows(x, rb), _full(sm["n1w"]), _full(scale1), _full(shift1)],
                    [_orow(s, D, BF16, rb)], n_rows=s, rb=rb, name="norm1")
    own = lambda got, i: lax.dynamic_update_slice(got, w_sh[i], (chip, 0, 0))
    [got0] = _comm_call("gather_w_in_ici", [_u_gather_ici(w_sh, (0,))])
    [got0] = _comm_call("gather_w_in_d2d", [_u_gather_d2d(got0, (0,))])
    w = dict(win=_win_assemble(own(got0[0], 0)))
    p, [got123] = _mm(h, w["win"], "in_proj", tm=1024, tn=1536, comm=[_u_gather_ici(w_sh, (1, 2, 3))])

    def f_gla_pre(c, i, glr, w2, gb):
        z = _dg(glr, w2.astype(BF16), 1, 0) + gb
        return ((jnp.minimum(z, 0.0) - jnp.log(1.0 + jnp.exp(-jnp.abs(z)))) * (1.0 / GLA_TAU),)

    (la,) = _rowcall(f_gla_pre, [_rows(p, rb, 128, P_LR // 128), _full(sm["w2"]), _full(sm["gb"])],
                     [_orow(s, 512, F32, rb)], n_rows=s, rb=rb, name="gla_pre")
    o_gla, states, [got123, got45] = _gla_fwd(p, la, s, comm=[_u_gather_d2d(got123, (1, 2, 3)), _u_gather_ici(w_sh, (4, 5))])
    w.update(wgb=own(got123[0], 1).reshape(1024, D), wab=_cols_join(own(got123[1], 2)), wout=own(got123[2], 3).reshape(D, D))

    def f_gla_post(c, i, ov, gnw, gr):
        on = jnp.concatenate([ov[:, k * 256:(k + 1) * 256] * _rms(ov[:, k * 256:(k + 1) * 256]) * gnw
                              for k in range(GLA_H)], axis=1)
        g = gr.astype(F32)
        return (on * (g * _sigmoid(g)),)

    (og,) = _rowcall(f_gla_post, [_rows(o_gla, rb), _full(sm["gnw"]), _rows(p, rb, 1024, P_GR // 1024)],
                     [_orow(s, 1024, BF16, rb)], n_rows=s, rb=rb, name="gla_post")
    y_gla = _mm(og, w["wgb"], "gla_branch")

    invf = jnp.tile(ROPE_THETA ** (-jnp.arange(ATT_HD // 2, dtype=F32) / (ATT_HD // 2)), 4).reshape(1, 128)
    cos_t, sin_t = _rope_tables(pos_col, invf, s)

    q_d, k_d, v_d = _rope_fwd(p, cos_t, sin_t, s)
    att = [_attn_fwd(q_d[g], k_d[g], v_d[g], g, r, s) for g, r in enumerate(_RS)]
    o_att, lse, o_d1, o_d2, lse_d1, lse_d2 = _attn_combine(att, s)
    y_att = _mm(o_att, w["wab"], "attn_branch")

    def f_merge(c, i, ma, mb, yg, ya):
        return (_sigmoid(ma.astype(F32)) * yg.astype(F32) + _sigmoid(mb.astype(F32)) * ya.astype(F32),)

    (mixed,) = _rowcall(f_merge, [_rows(p, rb, D, P_MA // D), _rows(p, rb, D, P_MB // D), _rows(y_gla, rb), _rows(y_att, rb)],
                        [_orow(s, D, BF16, rb)], n_rows=s, rb=rb, name="merge")
    z1, [got45] = _mm(mixed, w["wout"], "out_proj", comm=[_u_gather_d2d(got45, (4, 5))])
    w.update(wup=own(got45[0], 4), wdown=own(got45[1], 5).reshape(D_FF, D))

    def f_norm2(c, i, xv, z, g1, nw, sc, sh):
        x1 = xv + g1 * z.astype(F32)
        return (x1, (x1 * _rms(x1) * nw) * (1.0 + sc) + sh)

    x1, h2 = _rowcall(f_norm2, [_rows(x, rb), _rows(z1, rb), _full(gate1), _full(sm["n2w"]), _full(scale2), _full(shift2)],
                      [_orow(s, D, F32, rb), _orow(s, D, BF16, rb)], n_rows=s, rb=rb, name="norm2")
    u = _mm(h2, w["wup"], "up_proj", b_shards=True)

    cwid = 2 * W_UP_SH

    def f_ffn(c, i, uv, hl, cw, cb):
        uc = _conv(uv.astype(F32), hl.astype(F32) * (i > 0).astype(F32), cw, cb)
        val, gt = uc[:, :W_UP_SH], uc[:, W_UP_SH:]
        cdf, _ = _gelu_parts(gt)
        return (gt * cdf * val,)

    ccol = lambda c: c
    (hidden,) = _rowcall(f_ffn, [_rows(u, rb, cwid, ccol), _halo(u, rb, 16, cwid, ccol, True),
                                 _full(sm["cw"], cwid, ccol), _full(sm["cb"], cwid, ccol)],
                         [_orow(s, D_FF, BF16, rb, W_UP_SH, ccol)], n_rows=s, rb=rb, name="conv_geglu", ncol=2)
    z2 = _mm(hidden, w["wdown"], "down_proj", tk=D_FF)

    def f_final(c, i, x1v, z, g2, fw, tgt):
        x2 = x1v + g2 * z.astype(F32)
        r = _rms(x2)
        xh = x2 * r
        e = xh * fw - tgt
        loss = 0.5 * jnp.sum(jnp.mean(e * e, axis=-1, keepdims=True), axis=0, keepdims=True)
        dy = e * (1.0 / D)
        dxh = dy * fw
        dx2 = r * (dxh - xh * jnp.mean(dxh * xh, axis=-1, keepdims=True))
        return (loss, dx2, dx2 * g2, _csum(dy * xh), _csum(dx2 * z.astype(F32)))

    loss, dx2, dz2, d_fnw, d_gate2 = _rowcall(
        f_final, [_rows(x1, rb), _rows(z2, rb), _full(gate2), _full(sm["fnw"]), _rows(target, rb)],
        [_oacc(1, 1), _orow(s, D, F32, rb), _orow(s, D, BF16, rb), _oacc(1, D), _oacc(1, D)],
        n_rows=s, rb=rb, name="final_loss")
    d_hidden = _mm(dz2, w["wdown"], "down_proj_dx", tb=True, tn=1408)
    g_wdown = _mm(hidden, dz2, "down_proj_dw", ta=True, out_dtype=F32, tm=1408, tn=1024, tk=2048)

    def f_ffn_bwd(c, i, uv, hl, dh, cw, cb):
        uf = uv.astype(F32)
        hf = hl.astype(F32) * (i > 0).astype(F32)
        u1, u2 = _shift_rows(uf, hf, 1), _shift_rows(uf, hf, 2)
        uc = cb + _pick_row(cw, 0) * u2 + _pick_row(cw, 1) * u1 + _pick_row(cw, 2) * uf
        val, gt = uc[:, :W_UP_SH], uc[:, W_UP_SH:]
        cdf, pdf = _gelu_parts(gt)
        dhf = dh.astype(F32)
        duc = jnp.concatenate([dhf * (gt * cdf), dhf * val * (cdf + gt * pdf)], axis=1)
        dcw = jnp.concatenate([_csum(duc * u2), _csum(duc * u1), _csum(duc * uf)], axis=0)
        return (duc, _csum(duc), dcw)

    duc, d_cb, d_cw = _rowcall(
        f_ffn_bwd, [_rows(u, rb, cwid, ccol), _halo(u, rb, 16, cwid, ccol, True), _rows(d_hidden, rb, W_UP_SH, ccol),
                    _full(sm["cw"], cwid, ccol), _full(sm["cb"], cwid, ccol)],
        [_orow(s, 2 * D_FF, BF16, rb, cwid, ccol), _oacc(1, 2 * D_FF, cwid, ccol), _oacc(3, 2 * D_FF, cwid, ccol)],
        n_rows=s, rb=rb, name="conv_geglu_bwd", ncol=2)

    def f_conv_t(c, i, dv, hl, cw):
        df = dv.astype(F32)
        hf = hl.astype(F32) * (i < s // rb - 1).astype(F32)
        return (_pick_row(cw, 2) * df + _pick_row(cw, 1) * _shift_rows_up(df, hf, 1) + _pick_row(cw, 0) * _shift_rows_up(df, hf, 2),)

    (du,) = _rowcall(f_conv_t, [_rows(duc, rb, cwid, ccol), _halo(duc, rb, 16, cwid, ccol, False), _full(sm["cw"], cwid, ccol)],
                     [_orow(s, 2 * D_FF, BF16, rb, cwid, ccol)], n_rows=s, rb=rb, name="conv_transpose", ncol=2)
    g_wup = _mm(h2, du, "up_proj_dw", ta=True, out_dtype=F32, tm=1024, tk=2048, o_shards=True)
    gs45 = [g_wup, g_wdown.reshape(4, W_DOWN_SH, 1024)]
    d_h2, [land45] = _mm(du, w["wup"], "up_proj_dx", tb=True, b_shards=True, comm=[_u_pair_send(gs45, (4, 5))])
    ts45 = [_pair_add(g, ld, core, "grad_pair_add_" + BIG[i]) for g, ld, i in zip(gs45, land45, (4, 5))]

    def f_norm2_bwd(c, i, x1v, dh, dxr, z, nw, sc, g1):
        dxn, dsh, dsc, dnw = _norm_bwd(x1v, dh.astype(F32), nw, sc)
        dx1 = dxr + dxn
        return (dx1, dx1 * g1, dsh, dsc, dnw, _csum(dx1 * z.astype(F32)))

    dx1, dz1, d_shift2, d_scale2, d_n2w, d_gate1 = _rowcall(
        f_norm2_bwd, [_rows(x1, rb), _rows(d_h2, rb), _rows(dx2, rb), _rows(z1, rb), _full(sm["n2w"]), _full(scale2), _full(gate1)],
        [_orow(s, D, F32, rb), _orow(s, D, BF16, rb), _oacc(1, D), _oacc(1, D), _oacc(1, D), _oacc(1, D)],
        n_rows=s, rb=rb, name="norm2_bwd")
    d_mixed = _mm(dz1, w["wout"], "out_proj_dx", tb=True)
    g_wout = _mm(mixed, dz1, "out_proj_dw", ta=True, out_dtype=F32, tk=2048)

    def f_merge_bwd(c, i, dm, ma, mb, yg, ya):
        dmf, ygf, yaf = dm.astype(F32), yg.astype(F32), ya.astype(F32)
        sa, sb = _sigmoid(ma.astype(F32)), _sigmoid(mb.astype(F32))
        return (dmf * sa, dmf * sb, jnp.concatenate([dmf * ygf * sa * (1.0 - sa), dmf * yaf * sb * (1.0 - sb)], axis=1))

    dy_gla, dy_att, dp = _rowcall(
        f_merge_bwd, [_rows(d_mixed, rb), _rows(p, rb, D, P_MA // D), _rows(p, rb, D, P_MB // D), _rows(y_gla, rb), _rows(y_att, rb)],
        [_orow(s, D, BF16, rb)] * 2 + [_orow(s, P_W, BF16, rb, 2 * D, lambda c: P_MA // (2 * D))], n_rows=s, rb=rb, name="merge_bwd")
    d_og = _mm(dy_gla, w["wgb"], "gla_branch_dx", tb=True)
    g_wgb = _mm(og, dy_gla, "gla_branch_dw", ta=True, out_dtype=F32, tk=2048)
    d_oatt = _mm(dy_att, w["wab"], "attn_branch_dx", tb=True)
    g_wab = _mm(o_att, dy_att, "attn_branch_dw", ta=True, out_dtype=F32, tk=2048)

    def f_gla_post_bwd(c, i, ov, gnw, gr, dog):
        g = gr.astype(F32)
        sg = _sigmoid(g)
        silu = g * sg
        dof = dog.astype(F32)
        don = dof * silu
        on_parts, do_parts, dgn = [], [], jnp.zeros((1, 256), F32)
        for k in range(GLA_H):
            oh = ov[:, k * 256:(k + 1) * 256]
            dh = don[:, k * 256:(k + 1) * 256]
            r = _rms(oh)
            xh = oh * r
            dgn = dgn + _csum(dh * xh)
            dxh = dh * gnw
            do_parts.append(r * (dxh - xh * jnp.mean(dxh * xh, axis=-1, keepdims=True)))
            on_parts.append(xh * gnw)
        on = jnp.concatenate(on_parts, axis=1)
        dgr = dof * on * (sg * (1.0 + g * (1.0 - sg)))
        return (jnp.concatenate(do_parts, axis=1), dgr, dgn)

    do_gla, dp, d_gnw = _rowcall(
        f_gla_post_bwd, [_rows(o_gla, rb), _full(sm["gnw"]), _rows(p, rb, 1024, P_GR // 1024), _rows(d_og, rb)],
        [_orow(s, 1024, F32, rb), _orow(s, P_W, BF16, rb, 1024, lambda c: P_GR // 1024), _oacc(1, 256)],
        n_rows=s, rb=rb, name="gla_post_bwd", into=(dp, 1))
    gs123 = [g_wgb.reshape(4, 256, 1024), _cols_split(g_wab), g_wout.reshape(4, 256, 1024)]
    d_gq, d_gk, dp, d_la, [r45, land123] = _gla_bwd(p, la, states, do_gla, s, dp,
                                                    comm=[_u_chip_exchange(ts45), _u_pair_send(gs123, (1, 2, 3))])
    half45 = [_chip_sum(t, r, chip1, "grad_chip_sum_" + BIG[i]) for t, r, i in zip(ts45, r45, (4, 5))]
    ts123 = [_pair_add(g, ld, core, "grad_pair_add_" + BIG[i]) for g, ld, i in zip(gs123, land123, (1, 2, 3))]

    def f_gla_pre_bwd(c, i, lav, dlav, glr, w2):
        dz = dlav * (1.0 / GLA_TAU) * (1.0 - jnp.exp(GLA_TAU * lav))
        dzb = dz.astype(BF16)
        return (_dg(dzb, w2.astype(BF16), 1, 1), _csum(dz), _dg(glr, dzb, 0, 0))

    d_glr, d_gb, d_w2 = _rowcall(
        f_gla_pre_bwd, [_rows(la, rb), _rows(d_la, rb), _rows(p, rb, 128, P_LR // 128), _full(sm["w2"])],
        [_orow(s, 128, BF16, rb), _oacc(1, 512), _oacc(128, 512)], n_rows=s, rb=rb, name="gla_pre_bwd")

    do_d = [d_oatt] + list(_dilate(d_oatt, s))
    datt = [_attn_bwd(q_d[g], k_d[g], v_d[g], do_d[g], (o_att, o_d1, o_d2)[g], (lse, lse_d1, lse_d2)[g], g, r, s)
            for g, r in enumerate(_RS)]
    dp = _rope_bwd(datt, d_glr, dp, cos_t, sin_t, s)
    dp = lax.dynamic_update_slice(dp, jnp.concatenate([d_gq, d_gk], axis=1), (0, P_GQ))
    g_win, [r123, oth45] = _mm(h, dp, "in_proj_dw", ta=True, out_dtype=F32, tm=1024, tn=1536, tk=2048,
                               comm=[_u_chip_exchange(ts123), _u_pair_join(half45)])
    half123 = [_chip_sum(t, r, chip1, "grad_chip_sum_" + BIG[i]) for t, r, i in zip(ts123, r123, (1, 2, 3))]
    gs0 = [_win_split(g_win)]
    d_h, [land0, oth123] = _mm(dp, w["win"], "in_proj_dx", tb=True, tk=3840,
                               comm=[_u_pair_send(gs0, (0,)), _u_pair_join(half123)])
    ts0 = [_pair_add(gs0[0], land0[0], core, "grad_pair_add_w_in")]
    [r0] = _comm_call("grad_exchange_w_in", [_u_chip_exchange(ts0)])
    half0 = [_chip_sum(ts0[0], r0[0], chip1, "grad_chip_sum_w_in")]
    [oth0] = _comm_call("grad_join_w_in", [_u_pair_join(half0)])

    def f_norm1_bwd(c, i, xv, dh, dxr, nw, sc):
        dxn, dsh, dsc, dnw = _norm_bwd(xv, dh.astype(F32), nw, sc)
        return (dxr + dxn, dsh, dsc, dnw)

    grad_x, d_shift1, d_scale1, d_n1w = _rowcall(
        f_norm1_bwd, [_rows(x, rb), _rows(d_h, rb), _rows(dx1, rb), _full(sm["n1w"]), _full(scale1)],
        [_orow(s, D, F32, rb), _oacc(1, D), _oacc(1, D), _oacc(1, D)], n_rows=s, rb=rb, name="norm1_bwd")

    dmod = jnp.concatenate([d_shift1, d_scale1, d_gate1, d_shift2, d_scale2, d_gate2], axis=1)
    small = dict(dmod=dmod, n1w=d_n1w, gb=d_gb, gnw=d_gnw, n2w=d_n2w, cb=d_cb, fnw=d_fnw, w2=d_w2, cw=d_cw)
    return loss, grad_x, half0 + half123 + half45, oth0 + oth123 + oth45, small


def _win_pieces():
    runs = [(P_GV, 1024, 2048), (P_MA, 5392, 2048), (P_GQ, 0, 1024), (P_AQ, 3088, 2304), (P_LR, 3072, GLA_LR)]
    out = []
    for kc, rc, ln in runs:
        while ln > 0:
            step = min(ln, W_IN_SH - rc % W_IN_SH)
            out.append((kc, rc, step))
            kc, rc, ln = kc + step, rc + step, ln - step
    return out


def _win_assemble(shards):
    rb = 256

    def body(s_ref, o_ref):
        o_ref[:, W_IN:] = jnp.zeros((rb, P_W - W_IN), o_ref.dtype)
        for kc, rc, ln in _win_pieces():
            o_ref[:, kc:kc + ln] = s_ref[rc // W_IN_SH, :, rc % W_IN_SH:rc % W_IN_SH + ln]

    return pl.pallas_call(
        body, name="w_in_assemble", grid=(D // rb,),
        in_specs=[pl.BlockSpec((4, rb, W_IN_SH), lambda i: (0, i, 0))], out_specs=pl.BlockSpec((rb, P_W), lambda i: (i, 0)),
        out_shape=jax.ShapeDtypeStruct((D, P_W), shards.dtype), compiler_params=_params(("parallel",)),
    )(shards)


def _win_split(g):
    rb = 256

    def body(g_ref, o_ref):
        for kc, rc, ln in _win_pieces():
            o_ref[rc // W_IN_SH, :, rc % W_IN_SH:rc % W_IN_SH + ln] = g_ref[:, kc:kc + ln]

    return pl.pallas_call(
        body, name="w_in_grad_split", grid=(D // rb,),
        in_specs=[pl.BlockSpec((rb, P_W), lambda i: (i, 0))], out_specs=pl.BlockSpec((4, rb, W_IN_SH), lambda i: (0, i, 0)),
        out_shape=jax.ShapeDtypeStruct((4, D, W_IN_SH), g.dtype), compiler_params=_params(("parallel",)),
    )(g)


def _ff_to_kernel(a):
    h = W_UP_SH
    return jnp.concatenate([a[:, 0:h], a[:, D_FF:D_FF + h], a[:, h:D_FF], a[:, D_FF + h:]], axis=1)


def _ff_from_kernel(a):
    h = W_UP_SH
    return jnp.concatenate([a[:, 0:h], a[:, 2 * h:3 * h], a[:, h:2 * h], a[:, 3 * h:]], axis=1)


BIG = ("w_in", "w_gla_branch", "w_attn_branch", "w_out", "w_up", "w_down")
SH_SHAPES = ((1024, W_IN_SH), (256, 1024), (256, 256), (256, 1024), (1024, W_UP_SH), (W_DOWN_SH, 1024))
N_BIG = len(BIG)


def _cols_join(t):
    return jnp.concatenate([t[k] for k in range(4)], axis=1)


def _cols_split(t):
    cols = t.shape[1] // 4
    return jnp.stack([t[:, k * cols:(k + 1) * cols] for k in range(4)])


def _me():
    return lax.axis_index("x"), lax.axis_index("y"), lax.axis_index("c")


HBM = pl.BlockSpec(memory_space=pltpu.HBM)
VMEM_SPEC = pl.BlockSpec(memory_space=pltpu.VMEM)


def _allgather8(xs, name):
    rows = xs.shape[0]

    def body(x_ref, out_ref, send_sems, recv_sems, local_sem):
        x, y, c = _me()
        me = 4 * x + 2 * y + c
        mine = pltpu.make_async_copy(x_ref, out_ref.at[me], local_sem)
        mine.start()
        flips = [(k >> 2 & 1, k >> 1 & 1, k & 1) for k in range(1, 8)]

        def peer(f):
            return (jnp.where(f[0] == 1, 1 - x, x), jnp.where(f[1] == 1, 1 - y, y), jnp.where(f[2] == 1, 1 - c, c))

        sends = []
        for k, f in enumerate(flips):
            cp = pltpu.make_async_remote_copy(src_ref=x_ref, dst_ref=out_ref.at[me], send_sem=send_sems.at[k],
                                              recv_sem=recv_sems.at[k], device_id=peer(f), device_id_type=MESH)
            cp.start()
            sends.append(cp)
        for k, f in enumerate(flips):
            px, py, pc = peer(f)
            pltpu.make_async_remote_copy(src_ref=x_ref, dst_ref=out_ref.at[4 * px + 2 * py + pc], send_sem=send_sems.at[k],
                                         recv_sem=recv_sems.at[k], device_id=peer(f), device_id_type=MESH).wait_recv()
        for cp in sends:
            cp.wait_send()
        mine.wait()

    return pl.pallas_call(
        body, name=name, out_shape=jax.ShapeDtypeStruct((8, rows, 128), F32),
        in_specs=[VMEM_SPEC], out_specs=VMEM_SPEC,
        scratch_shapes=[pltpu.SemaphoreType.DMA((7,)), pltpu.SemaphoreType.DMA((7,)), pltpu.SemaphoreType.DMA],
        compiler_params=pltpu.CompilerParams(vmem_limit_bytes=VMEM_LIMIT),
    )(xs)


def _half_rows(i, cc, unit):
    rows = SH_SHAPES[i][0] // 2
    return pl.ds(pl.multiple_of(cc * rows, unit), rows)


def _rc(src, dst, sems, to):
    return pltpu.make_async_remote_copy(src_ref=src, dst_ref=dst, send_sem=sems[0], recv_sem=sems[1], device_id=to, device_id_type=MESH)


def _other_chips(x, y):
    return [(1 - x, y), (x, 1 - y), (1 - x, 1 - y)]


def _u_gather_ici(w_sh, idxs):
    def copies(ins, outs, sem):
        x, y, c = _me()
        res = []
        for j, (px, py) in enumerate(_other_chips(x, y)):
            for n, i in enumerate(idxs):
                src = ins[n].at[0, _half_rows(i, c, 16)]
                res.append((_rc(src, outs[n].at[2 * x + y, _half_rows(i, c, 16)], sem(j * len(idxs) + n), (px, py, c)),
                            _rc(src, outs[n].at[2 * px + py, _half_rows(i, c, 16)], sem(j * len(idxs) + n), (px, py, c))))
        return res

    return dict(ins=[w_sh[i] for i in idxs], outs=[jax.ShapeDtypeStruct((4,) + SH_SHAPES[i], BF16) for i in idxs],
                nsem=3 * len(idxs), alias={}, copies=copies)


def _u_gather_d2d(got, idxs):
    def copies(ins, outs, sem):
        x, y, c = _me()
        res = []
        for j, (px, py) in enumerate(_other_chips(x, y)):
            for n, i in enumerate(idxs):
                src = ins[n].at[2 * px + py, _half_rows(i, c, 16)]
                res.append((_rc(src, outs[n].at[2 * px + py, _half_rows(i, c, 16)], sem(j * len(idxs) + n), (x, y, 1 - c)),
                            _rc(src, outs[n].at[2 * px + py, _half_rows(i, 1 - c, 16)], sem(j * len(idxs) + n), (x, y, 1 - c))))
        return res

    return dict(ins=list(got), outs=[jax.ShapeDtypeStruct(g.shape, g.dtype) for g in got], nsem=3 * len(idxs),
                alias={n: n for n in range(len(idxs))}, copies=copies)


def _u_pair_send(gs, idxs):
    def copies(ins, outs, sem):
        x, y, c = _me()
        res = []
        for n, i in enumerate(idxs):
            for sh in range(4):
                cp = _rc(ins[n].at[sh, _half_rows(i, 1 - c, 8)], outs[n].at[sh], sem(4 * n + sh), (x, y, 1 - c))
                res.append((cp, cp))
        return res

    return dict(ins=list(gs), outs=[jax.ShapeDtypeStruct((4, SH_SHAPES[i][0] // 2, SH_SHAPES[i][1]), F32) for i in idxs],
                nsem=4 * len(idxs), alias={}, copies=copies)


def _u_chip_exchange(ts):
    def copies(ins, outs, sem):
        x, y, c = _me()
        res = []
        for j, (px, py) in enumerate(_other_chips(x, y)):
            for n in range(len(ts)):
                cp = _rc(ins[n].at[2 * px + py], outs[n].at[j], sem(j * len(ts) + n), (px, py, c))
                res.append((cp, cp))
        return res

    return dict(ins=list(ts), outs=[jax.ShapeDtypeStruct((3,) + t.shape[1:], t.dtype) for t in ts], nsem=3 * len(ts),
                alias={}, copies=copies)


def _u_pair_join(hs):
    def copies(ins, outs, sem):
        x, y, c = _me()
        res = []
        for n in range(len(hs)):
            cp = _rc(ins[n], outs[n], sem(n), (x, y, 1 - c))
            res.append((cp, cp))
        return res

    return dict(ins=list(hs), outs=[jax.ShapeDtypeStruct(h.shape, h.dtype) for h in hs], nsem=len(hs), alias={}, copies=copies)


def _comm_phase(units, ci, co, send_sems, recv_sems, start):
    ii = oo = off = 0
    for u in units:
        ni, no = len(u["ins"]), len(u["outs"])
        for st, arrival in u["copies"](ci[ii:ii + ni], co[oo:oo + no], lambda k, off=off: (send_sems.at[off + k], recv_sems.at[off + k])):
            if start:
                st.start()
            else:
                st.wait_send()
                arrival.wait_recv()
        ii, oo, off = ii + ni, oo + no, off + u["nsem"]


def _carry(units, n_in, n_out):
    ins = [a for u in units for a in u["ins"]]
    outs = [o for u in units for o in u["outs"]]
    alias, ii, oo = {}, 0, 0
    for u in units:
        for a, b in u["alias"].items():
            alias[n_in + ii + a] = n_out + oo + b
        ii, oo = ii + len(u["ins"]), oo + len(u["outs"])
    nsem = sum(u["nsem"] for u in units)
    scratch = [pltpu.SemaphoreType.DMA((nsem,)), pltpu.SemaphoreType.DMA((nsem,))] if units else []
    return ins, outs, alias, scratch


def _split_units(units, res):
    out, oo = [], 0
    for u in units:
        out.append(list(res[oo:oo + len(u["outs"])]))
        oo += len(u["outs"])
    return out


def _comm_call(name, units):
    ins, outs, alias, scratch = _carry(units, 0, 0)

    def body(*refs):
        ci, co = refs[:len(ins)], refs[len(ins):len(ins) + len(outs)]
        _comm_phase(units, ci, co, refs[-2], refs[-1], True)
        _comm_phase(units, ci, co, refs[-2], refs[-1], False)

    res = pl.pallas_call(body, name=name, out_shape=outs, in_specs=[HBM] * len(ins), out_specs=[HBM] * len(outs),
                         scratch_shapes=scratch, input_output_aliases=alias)(*ins)
    return _split_units(units, res)


def _pair_add(g, land, core, name):
    _, rows, cols = g.shape
    half = rows // 2
    rb = _tile(half, 256, 16)
    nb = half // rb

    def body(c_ref, g_ref, l_ref, o_ref):
        o_ref[...] = (g_ref[...] + l_ref[...]).astype(BF16)

    return pl.pallas_call(
        body, name=name,
        grid_spec=pltpu.PrefetchScalarGridSpec(
            num_scalar_prefetch=1, grid=(4, nb),
            in_specs=[pl.BlockSpec((1, rb, cols), lambda s, i, c_ref: (s, c_ref[0] * nb + i, 0)),
                      pl.BlockSpec((1, rb, cols), lambda s, i, c_ref: (s, i, 0))],
            out_specs=pl.BlockSpec((1, rb, cols), lambda s, i, c_ref: (s, i, 0))),
        out_shape=jax.ShapeDtypeStruct((4, half, cols), BF16),
        compiler_params=_params(("parallel", "parallel")),
    )(core, g, land)


def _chip_sum(t, r, chip, name):
    _, half, cols = t.shape
    rb = _tile(half, 256, 16)

    def body(s_ref, t_ref, r_ref, o_ref):
        o_ref[...] = ((t_ref[0].astype(F32) + r_ref[0].astype(F32)) + r_ref[1].astype(F32)) + r_ref[2].astype(F32)

    return pl.pallas_call(
        body, name=name,
        grid_spec=pltpu.PrefetchScalarGridSpec(
            num_scalar_prefetch=1, grid=(half // rb,),
            in_specs=[pl.BlockSpec((1, rb, cols), lambda i, s_ref: (s_ref[0], i, 0)),
                      pl.BlockSpec((3, rb, cols), lambda i, s_ref: (0, i, 0))],
            out_specs=pl.BlockSpec((rb, cols), lambda i, s_ref: (i, 0))),
        out_shape=jax.ShapeDtypeStruct((half, cols), F32),
        compiler_params=_params(("parallel",)),
    )(chip, t, r)


def _adam_math(wv, gv, mv, vv):
    mn = ADAM_B1 * mv + (1.0 - ADAM_B1) * gv
    vn = ADAM_B2 * vv + (1.0 - ADAM_B2) * (gv * gv)
    m_hat = mn / (1.0 - ADAM_B1 ** ADAM_STEP)
    v_hat = vn / (1.0 - ADAM_B2 ** ADAM_STEP)
    return -ADAM_LR * (m_hat / (jnp.sqrt(v_hat) + ADAM_EPS) + ADAM_WD * wv), mn, vn


def _adamw_halves(wt, mt, vt, mine, theirs, core, name):
    _, rows, cols = wt.shape
    half = rows // 2
    rb = _tile(half, 256, 8)
    nb = half // rb

    def body(c_ref, w_ref, m_ref, v_ref, a_ref, b_ref, g_ref, d_ref, mo_ref, vo_ref):
        gv = jnp.where(pl.program_id(0) == c_ref[0], a_ref[...], b_ref[...])
        dl, mn, vn = _adam_math(w_ref[...], gv, m_ref[...], v_ref[...])
        g_ref[...] = gv
        d_ref[...] = dl
        mo_ref[...] = mn
        vo_ref[...] = vn

    full = pl.BlockSpec((None, rb, cols), lambda hf, i, c_ref: (0, hf * nb + i, 0))
    part = pl.BlockSpec((rb, cols), lambda hf, i, c_ref: (i, 0))
    return pl.pallas_call(
        body, name=name,
        grid_spec=pltpu.PrefetchScalarGridSpec(num_scalar_prefetch=1, grid=(2, nb), in_specs=[full, full, full, part, part],
                                               out_specs=[full] * 4),
        out_shape=[jax.ShapeDtypeStruct((1, rows, cols), F32)] * 4,
        compiler_params=_params(("parallel", "parallel")),
    )(core, wt, mt, vt, mine, theirs)


SG_REP = 144
SG_LOSS = 136
SG_W2, SG_CW = SG_REP, SG_REP + 4 * 16
SG_ROWS = SG_CW + 4 * 40
SP_ROWS = SG_REP + 16 + 40


def _mod_shard(c_all, ada_w_sh):
    def body(c_ref, w_ref, o_ref):
        cv = c_ref[...]
        o_ref[...] = _dg((cv * _sigmoid(cv)).astype(BF16), w_ref[...].astype(BF16), 1, 0)

    return pl.pallas_call(body, name="mod_shard", out_shape=jax.ShapeDtypeStruct((8, 1536), F32),
                          in_specs=[VMEM_SPEC, VMEM_SPEC], out_specs=VMEM_SPEC,
                          compiler_params=pltpu.CompilerParams(vmem_limit_bytes=VMEM_LIMIT))(c_all, ada_w_sh)


def _mod_select(mod_all, ada_b4):
    def body(m_ref, b_ref, o_ref):
        x, y, c = _me()
        me = 4 * x + 2 * y + c
        for sh in range(4):
            o_ref[sh] = m_ref[2 * sh, me] + b_ref[sh]

    return pl.pallas_call(body, name="mod_select", out_shape=jax.ShapeDtypeStruct((4, 12, 128), F32),
                          in_specs=[VMEM_SPEC, VMEM_SPEC], out_specs=VMEM_SPEC)(mod_all, ada_b4)


def _small_reduce(sg_all):
    def body(g_ref, o_ref):
        x, y, c = _me()
        s_me = 2 * x + y
        w2_rows = pl.ds(pl.multiple_of(SG_W2 + 16 * s_me, 8), 16)
        cw_rows = pl.ds(pl.multiple_of(SG_CW + 40 * s_me, 8), 40)
        a = g_ref[0, 0:SG_REP, :]
        b = g_ref[0, w2_rows, :]
        d = g_ref[0, cw_rows, :]
        for dev in range(1, 8):
            a = a + g_ref[dev, 0:SG_REP, :]
            b = b + g_ref[dev, w2_rows, :]
            d = d + g_ref[dev, cw_rows, :]
        o_ref[0:SG_REP, :] = a
        o_ref[SG_REP:SG_REP + 16, :] = b
        o_ref[SG_REP + 16:SP_ROWS, :] = d

    return pl.pallas_call(body, name="small_grad_reduce", out_shape=jax.ShapeDtypeStruct((SP_ROWS, 128), F32),
                          in_specs=[VMEM_SPEC], out_specs=VMEM_SPEC)(sg_all)


def _ada_grad(dmod_all, c_bc):
    def body(g_ref, c_ref, o_ref):
        x, y, c = _me()
        s_me = 2 * x + y
        for k in range(12):
            acc = jnp.zeros((D, 128), F32)
            for b in range(8):
                cv = c_ref[b]
                acc = acc + (cv * _sigmoid(cv)) * g_ref[s_me, k, b:b + 1, :]
            o_ref[:, k * 128:(k + 1) * 128] = acc

    return pl.pallas_call(body, name="ada_w_grad", out_shape=jax.ShapeDtypeStruct((D, 1536), F32),
                          in_specs=[VMEM_SPEC, VMEM_SPEC], out_specs=VMEM_SPEC,
                          compiler_params=pltpu.CompilerParams(vmem_limit_bytes=VMEM_LIMIT))(dmod_all, c_bc)


def _adamw(wt, g, m, v, name):
    rows, cols = wt.shape
    rb = _tile(rows, 256, 8)

    def fn(c, i, wv, gv, mv, vv):
        return _adam_math(wv, gv, mv, vv)

    return _rowcall(fn, [_rows(t, rb) for t in (wt, g, m, v)], [_orow(rows, cols, F32, rb)] * 3,
                    n_rows=rows, rb=rb, name=name)


def _pad_rows(t, rows):
    flat = t.reshape(-1)
    return jnp.pad(flat, (0, rows * 128 - flat.shape[0])).reshape(rows, 128)


SP_LAYOUT = (("ada_b", 48), ("norm1_w", 8), ("gla_gate_b", 8), ("gla_norm_w", 8), ("norm2_w", 8), ("conv_b", 48),
             ("final_norm_w", 8), (None, 8), ("gla_gate_w2", 16), ("conv_w", 40))


def _pack_small(d):
    return jnp.concatenate([jnp.zeros((rows, 128), F32) if n is None else _pad_rows(d[n].astype(F32), rows)
                            for n, rows in SP_LAYOUT], axis=0)


def _unpack_small(pk, shapes):
    out, off = {}, 0
    for n, rows in SP_LAYOUT:
        if n is not None:
            shp = shapes[n]
            out[n] = pk[off:off + rows].reshape(-1)[:math.prod(shp)].reshape(shp)
        off += rows
    return out


def kernel(x, c, positions, ada_w, ada_b, norm1_w, w_in, gla_gate_w2, gla_gate_b, gla_norm_w, w_gla_branch, w_attn_branch, w_out, norm2_w, w_up, conv_w, conv_b, w_down, final_norm_w, loss_target, m_ada_w, m_ada_b, m_norm1_w, m_w_in, m_gla_gate_w2, m_gla_gate_b, m_gla_norm_w, m_w_gla_branch, m_w_attn_branch, m_w_out, m_norm2_w, m_w_up, m_conv_w, m_conv_b, m_w_down, m_final_norm_w, v_ada_w, v_ada_b, v_norm1_w, v_w_in, v_gla_gate_w2, v_gla_gate_b, v_gla_norm_w, v_w_gla_branch, v_w_attn_branch, v_w_out, v_norm2_w, v_w_up, v_conv_w, v_conv_b, v_w_down, v_final_norm_w):
    s = x.shape[1]
    names = ("ada_w", "ada_b", "norm1_w", "w_in", "gla_gate_w2", "gla_gate_b", "gla_norm_w", "w_gla_branch", "w_attn_branch",
             "w_out", "norm2_w", "w_up", "conv_w", "conv_b", "w_down", "final_norm_w")
    wts = dict(zip(names, (ada_w, ada_b, norm1_w, w_in, gla_gate_w2, gla_gate_b, gla_norm_w, w_gla_branch, w_attn_branch,
                           w_out, norm2_w, w_up, conv_w, conv_b, w_down, final_norm_w)))
    ms = dict(zip(names, (m_ada_w, m_ada_b, m_norm1_w, m_w_in, m_gla_gate_w2, m_gla_gate_b, m_gla_norm_w, m_w_gla_branch,
                          m_w_attn_branch, m_w_out, m_norm2_w, m_w_up, m_conv_w, m_conv_b, m_w_down, m_final_norm_w)))
    vs = dict(zip(names, (v_ada_w, v_ada_b, v_norm1_w, v_w_in, v_gla_gate_w2, v_gla_gate_b, v_gla_norm_w, v_w_gla_branch,
                          v_w_attn_branch, v_w_out, v_norm2_w, v_w_up, v_conv_w, v_conv_b, v_w_down, v_final_norm_w)))

    pk0 = jnp.concatenate([_pad_rows(c, 8), _pad_rows(gla_gate_w2, 16), _pad_rows(conv_w, 40)], axis=0)
    sm_all = _allgather8(pk0, "gather_small")
    c_all = sm_all[:, 0:8, :].reshape(8, D)
    w2_full = sm_all[0::2, 8:24, :].transpose(1, 0, 2).reshape(GLA_LR, 512)
    cw_full = sm_all[0::2, 24:64, :].reshape(4, 40 * 128)[:, :3 * W_UP_SH].reshape(4, 3, W_UP_SH).transpose(1, 0, 2).reshape(3, 2 * D_FF)

    mod_sh = _mod_shard(c_all, ada_w[0])
    mod_all = _allgather8(mod_sh.reshape(96, 128), "gather_mod")
    mod = _mod_select(mod_all.reshape(8, 8, 12, 128), ada_b.reshape(4, 12, 128)).reshape(6, D)

    core = lax.axis_index("c").astype(jnp.int32).reshape(1)
    chip = (2 * lax.axis_index("x") + lax.axis_index("y")).astype(jnp.int32)
    w_sh = [wts[n].astype(BF16) for n in BIG]
    sm = dict(n1w=norm1_w, n2w=norm2_w, fnw=final_norm_w.reshape(1, D), gnw=gla_norm_w, gb=gla_gate_b,
              w2=jnp.pad(w2_full, ((0, 128 - GLA_LR), (0, 0))), cw=_ff_to_kernel(cw_full), cb=_ff_to_kernel(conv_b))
    loss, grad_x, halves, others, small = _local_step(x[0], mod, positions.reshape(s, 1), loss_target[0], sm, w_sh, chip, core)

    dcw = _ff_from_kernel(small["cw"]).reshape(3, 4, W_UP_SH).transpose(1, 0, 2)
    dw2 = small["w2"][:GLA_LR].reshape(GLA_LR, 4, 128).transpose(1, 0, 2)
    sg = jnp.concatenate(
        [_pad_rows(small["dmod"], 48), _pad_rows(small["n1w"], 8), _pad_rows(small["gb"], 8), _pad_rows(small["gnw"], 8),
         _pad_rows(small["n2w"], 8), _pad_rows(_ff_from_kernel(small["cb"]), 48), _pad_rows(small["fnw"], 8), _pad_rows(loss, 8)]
        + [_pad_rows(dw2[k], 16) for k in range(4)] + [_pad_rows(dcw[k], 40) for k in range(4)], axis=0)
    sg_all = _allgather8(sg, "gather_small_grads")
    g_small_pk = _small_reduce(sg_all)
    dmod_all = sg_all[:, 0:48, :].reshape(8, 4, 12, 128).transpose(1, 2, 0, 3)
    g_ada_w = _ada_grad(dmod_all, jnp.broadcast_to(c_all[:, :, None], (8, D, 128)))

    shapes = {n: wts[n].shape for n in names}
    g_small = _unpack_small(g_small_pk, shapes)
    grads = {"ada_w": g_ada_w.reshape(1, D, 1536), **g_small}
    deltas, new_m, new_v = {}, {}, {}
    for n, mine, theirs in zip(BIG, halves, others):
        grads[n], deltas[n], new_m[n], new_v[n] = _adamw_halves(wts[n], ms[n], vs[n], mine, theirs, core, "adamw_" + n)
    shp = ada_w.shape
    d_, m_, v_ = _adamw(ada_w[0], g_ada_w, m_ada_w[0], v_ada_w[0], "adamw_ada_w")
    deltas["ada_w"], new_m["ada_w"], new_v["ada_w"] = d_.reshape(shp), m_.reshape(shp), v_.reshape(shp)
    d_, m_, v_ = _adamw(_pack_small(wts), g_small_pk, _pack_small(ms), _pack_small(vs), "adamw_small")
    for dst, pk in ((deltas, d_), (new_m, m_), (new_v, v_)):
        dst.update(_unpack_small(pk, shapes))

    return (g_small_pk[SG_LOSS, 0], grad_x.reshape(1, s, D), *[grads[n] for n in names], *[deltas[n] for n in names],
            *[new_m[n] for n in names], *[new_v[n] for n in names])
```

```python
import math

import jax
import jax.numpy as jnp
from jax import lax
from jax.experimental import pallas as pl
from jax.experimental.pallas import tpu as pltpu

F32, BF16 = jnp.float32, jnp.bfloat16
MESH = pl.DeviceIdType.MESH

D = 1024
EPS = 1e-6
GLA_H, GLA_DK, GLA_DV, GLA_LR = 4, 128, 256, 16
GLA_TAU = 16.0
GLA_CHUNK = 64
GLA_BLOCK = 512
ATT_GROUPS = ((128, 1), (512, 4), (2048, 16))
ATT_BLK = 128
ATT_HD = 64
ATT_W = 768
D_FF = 2816
ROPE_THETA = 10000.0
P_W = 7680
P_GV, P_GR, P_MA, P_MB, P_GQ, P_GK, P_AQ, P_AK, P_AV, P_LR = 0, 1024, 2048, 3072, 4096, 4608, 5120, 5888, 6656, 7424
W_IN = 7440
W_IN_SH, W_UP_SH, W_DOWN_SH = 1860, 1408, 704
VMEM_LIMIT = 56 * 1024 * 1024
ADAM_LR, ADAM_B1, ADAM_B2, ADAM_EPS, ADAM_WD, ADAM_STEP = 0.001, 0.9, 0.999, 1e-08, 0.01, 10
NEG = -1e30


def _tile(n, target, unit=128):
    best = None
    for t in range(unit, min(n, target) + 1, unit):
        if n % t == 0:
            best = t
    return best or n


def _params(sem):
    return pltpu.CompilerParams(dimension_semantics=sem, vmem_limit_bytes=VMEM_LIMIT)


def _dg(a, b, ca, cb):
    return lax.dot_general(a, b, (((ca,), (cb,)), ((), ())), preferred_element_type=F32)


def _sigmoid(v):
    return 1.0 / (1.0 + jnp.exp(-v))


def _ff_block(j):
    return (j % 2) * 2 + j // 2


def _mm(a, b, name, *, ta=False, tb=False, out_dtype=BF16, tm=1024, tn=1536, tk=1024, n_outer=True, comm=(),
        b_shards=False, o_shards=False):
    m = a.shape[1] if ta else a.shape[0]
    k = a.shape[0] if ta else a.shape[1]
    if b_shards:
        n = b.shape[1] if tb else 4 * W_UP_SH
        tn, tk = (tn, W_UP_SH) if tb else (W_UP_SH, tk)
    else:
        n = b.shape[0] if tb else b.shape[1]
    if o_shards:
        tn = W_UP_SH
    tm, tn, tk = _tile(m, tm), _tile(n, tn), _tile(k, tk)
    nm, nn, nk = m // tm, n // tn, k // tk
    in_out = out_dtype == F32
    c_ins, c_outs, c_alias, c_scratch = _carry(comm, 2, 1)

    def body(a_ref, b_ref, *rest):
        ci, o_ref, co = rest[:len(c_ins)], rest[len(c_ins)], rest[len(c_ins) + 1:len(c_ins) + 1 + len(c_outs)]
        scr = rest[len(c_ins) + 1 + len(c_outs):]
        kk = pl.program_id(2)
        if comm:
            step = (pl.program_id(0) * (nm if n_outer else nn) + pl.program_id(1)) * nk + kk

            @pl.when(step == 0)
            def _():
                _comm_phase(comm, ci, co, scr[-2], scr[-1], True)

        _mm_step(a_ref, b_ref, o_ref, scr, kk)
        if comm:
            @pl.when(step == nm * nn * nk - 1)
            def _():
                _comm_phase(comm, ci, co, scr[-2], scr[-1], False)

    def _mm_step(a_ref, b_ref, o_ref, scr, kk):
        p = _dg(a_ref[...].astype(BF16), b_ref[...].astype(BF16), 0 if ta else 1, 1 if tb else 0)
        if nk == 1:
            o_ref[...] = p.astype(o_ref.dtype)
        else:
            acc = o_ref if in_out else scr[0]

            @pl.when(kk == 0)
            def _():
                acc[...] = p

            @pl.when(kk > 0)
            def _():
                acc[...] += p

            if not in_out:
                @pl.when(kk == nk - 1)
                def _():
                    o_ref[...] = acc[...].astype(o_ref.dtype)

    if n_outer:
        ij = lambda g0, g1: (g1, g0)
        grid = (nn, nm, nk)
    else:
        ij = lambda g0, g1: (g0, g1)
        grid = (nm, nn, nk)
    a_map = (lambda g0, g1, kk: (kk, ij(g0, g1)[0])) if ta else (lambda g0, g1, kk: (ij(g0, g1)[0], kk))
    if b_shards and tb:
        b_spec = pl.BlockSpec((None, tn, tk), lambda g0, g1, kk: (_ff_block(kk), ij(g0, g1)[1], 0))
    elif b_shards:
        b_spec = pl.BlockSpec((None, tk, tn), lambda g0, g1, kk: (_ff_block(ij(g0, g1)[1]), kk, 0))
    elif tb:
        b_spec = pl.BlockSpec((tn, tk), lambda g0, g1, kk: (ij(g0, g1)[1], kk))
    else:
        b_spec = pl.BlockSpec((tk, tn), lambda g0, g1, kk: (kk, ij(g0, g1)[1]))
    if o_shards:
        o_spec = pl.BlockSpec((None, tm, tn), lambda g0, g1, kk: (_ff_block(ij(g0, g1)[1]), ij(g0, g1)[0], 0))
        o_shape = jax.ShapeDtypeStruct((4, m, W_UP_SH), out_dtype)
    else:
        o_spec = pl.BlockSpec((tm, tn), lambda g0, g1, kk: ij(g0, g1))
        o_shape = jax.ShapeDtypeStruct((m, n), out_dtype)
    res = pl.pallas_call(
        body, name=name, grid=grid,
        in_specs=[pl.BlockSpec((tk, tm) if ta else (tm, tk), a_map), b_spec] + [HBM] * len(c_ins),
        out_specs=[o_spec] + [HBM] * len(c_outs),
        out_shape=[o_shape] + c_outs,
        scratch_shapes=([] if (in_out or nk == 1) else [pltpu.VMEM((tm, tn), F32)]) + c_scratch,
        input_output_aliases=c_alias,
        compiler_params=_params(("arbitrary",) * 3 if comm else ("parallel", "parallel", "arbitrary")),
    )(a, b, *c_ins)
    return (res[0], _split_units(comm, res[1:])) if comm else res[0]


def _rows(arr, rb, w=None, j=0):
    w = arr.shape[1] if w is None else w
    if callable(j):
        return arr, pl.BlockSpec((rb, w), lambda c, i: (i, j(c)))
    return arr, pl.BlockSpec((rb, w), lambda c, i: (i, j))


def _full(arr, w=None, j=0):
    w = arr.shape[1] if w is None else w
    if callable(j):
        return arr, pl.BlockSpec((arr.shape[0], w), lambda c, i: (0, j(c)))
    return arr, pl.BlockSpec((arr.shape[0], w), lambda c, i: (0, j))


def _halo(arr, rb, hb, w, j, before):
    per = rb // hb
    last = arr.shape[0] // hb - 1
    if before:
        rmap = lambda i: jnp.maximum(i * per - 1, 0)
    else:
        rmap = lambda i: jnp.minimum((i + 1) * per, last)
    return arr, pl.BlockSpec((hb, w), lambda c, i: (rmap(i), j(c) if callable(j) else j))


def _rowcall(fn, ins, outs, *, n_rows, rb, name, ncol=1, into=None):
    n_in = len(ins)
    nr = n_rows // rb
    n_skip = 0 if into is None else 1

    def body(*refs):
        c, i = pl.program_id(0), pl.program_id(1)
        res = fn(c, i, *[r[...] for r in refs[:n_in]])
        for val, spec, o_ref in zip(res, outs, refs[n_in + n_skip:]):
            if spec[2] == "row":
                o_ref[...] = val.astype(o_ref.dtype)
            else:
                @pl.when(i == 0)
                def _(o_ref=o_ref, val=val):
                    o_ref[...] = val.astype(o_ref.dtype)

                @pl.when(i > 0)
                def _(o_ref=o_ref, val=val):
                    o_ref[...] += val.astype(o_ref.dtype)

    out_specs = []
    for shape, dt, kind, block, col in outs:
        if kind == "row":
            out_specs.append(pl.BlockSpec(block, lambda c, i, col=col: (i, col(c))))
        else:
            out_specs.append(pl.BlockSpec(block, lambda c, i, col=col: (0, col(c))))
    return pl.pallas_call(
        body, name=name, grid=(ncol, nr),
        in_specs=[s for _, s in ins] + [pl.BlockSpec(memory_space=pl.ANY)] * n_skip, out_specs=out_specs,
        out_shape=[jax.ShapeDtypeStruct(o[0], o[1]) for o in outs],
        input_output_aliases={} if into is None else {n_in: into[1]},
        compiler_params=_params(("parallel", "arbitrary")),
    )(*[a for a, _ in ins], *([] if into is None else [into[0]]))


def _orow(n_rows, w, dt, rb, bw=None, col=lambda c: 0):
    return ((n_rows, w), dt, "row", (rb, bw or w), col)


def _oacc(r, w, bw=None, col=lambda c: 0):
    return ((r, w), F32, "acc", (r, bw or w), col)


def _csum(v):
    return jnp.sum(v, axis=0, keepdims=True)


def _rms(v):
    return lax.rsqrt(jnp.mean(v * v, axis=-1, keepdims=True) + EPS)


def _norm_bwd(xv, dh, w, scale):
    r = _rms(xv)
    xh = xv * r
    dxh = dh * (w * (1.0 + scale))
    dx = r * (dxh - xh * jnp.mean(dxh * xh, axis=-1, keepdims=True))
    t = dh * xh
    return dx, _csum(dh), _csum(t * w), _csum(t * (1.0 + scale))


def _rope_tables(pos_col, invf, s):
    def fn(c, i, pos, f):
        ang = pos.astype(F32) * f
        lane = lax.broadcasted_iota(jnp.int32, ang.shape, 1)
        sign = jnp.where((lane % ATT_HD) < ATT_HD // 2, -1.0, 1.0)
        return jnp.cos(ang), jnp.sin(ang) * sign

    rb = 512
    return _rowcall(fn, [_rows(pos_col, rb), _full(invf)], [_orow(s, 128, F32, rb), _orow(s, 128, F32, rb)],
                    n_rows=s, rb=rb, name="rope_tables")


def _swap_halves(t):
    n = t.shape[1]
    lane = lax.broadcasted_iota(jnp.int32, t.shape, 1)
    return jnp.where((lane % ATT_HD) < ATT_HD // 2, pltpu.roll(t, n - 32, 1), pltpu.roll(t, 32, 1))


def _rope_apply(t, cos, sin_signed, inverse):
    cw = jnp.concatenate([cos] * (t.shape[1] // 128), axis=1)
    sw = jnp.concatenate([sin_signed] * (t.shape[1] // 128), axis=1)
    if inverse:
        sw = -sw
    return t * cw + _swap_halves(t) * sw


DIL_ROWS = 512


def _to_dilated(scr, val, out_ref, r):
    if r == 1:
        out_ref[...] = val.astype(out_ref.dtype)
        return
    n = val.shape[0] // r
    for hh in range(2):
        scr[hh] = val[:, hh * 128:(hh + 1) * 128]
        for pr in range(r):
            out_ref[:, pr * 256 + hh * 128:pr * 256 + (hh + 1) * 128] = scr[hh, pl.ds(pr, n, stride=r), :].astype(out_ref.dtype)


def _from_dilated(scr, in_ref, r):
    if r == 1:
        return in_ref[...].astype(F32)
    n = in_ref.shape[0]
    for hh in range(2):
        for pr in range(r):
            scr[hh, pl.ds(pr, n, stride=r), :] = in_ref[:, pr * 256 + hh * 128:pr * 256 + (hh + 1) * 128].astype(F32)
    return jnp.concatenate([scr[0], scr[1]], axis=1)


def _dil_spec(r):
    return pl.BlockSpec((DIL_ROWS // r, r * 256), lambda i: (i, 0))


def _dil_shape(s, r, dt):
    return jax.ShapeDtypeStruct((s // r, r * 256), dt)


_DIL_SCRATCH = [pltpu.VMEM((2, DIL_ROWS, 128), F32)]
_RS = tuple(r for _, r in ATT_GROUPS)


def _rope_fwd(p, cos_t, sin_t, s):
    def body(*refs):
        ins, cs, sn, outs, scr = refs[:9], refs[9][...], refs[10][...], refs[11:20], refs[20]
        for t in range(3):
            for g, r in enumerate(_RS):
                val = ins[3 * t + g][...].astype(F32)
                _to_dilated(scr, _rope_apply(val, cs, sn, False) if t < 2 else val, outs[3 * t + g], r)

    res = pl.pallas_call(
        body, name="rope", grid=(s // DIL_ROWS,),
        in_specs=[pl.BlockSpec((DIL_ROWS, 256), lambda i, c=base // 256 + g: (i, c)) for base in (P_AQ, P_AK, P_AV) for g in range(3)]
        + [pl.BlockSpec((DIL_ROWS, 128), lambda i: (i, 0))] * 2,
        out_specs=[_dil_spec(r) for _ in range(3) for r in _RS],
        out_shape=[_dil_shape(s, r, BF16) for _ in range(3) for r in _RS],
        scratch_shapes=_DIL_SCRATCH, compiler_params=_params(("parallel",)),
    )(*([p] * 9), cos_t, sin_t)
    return res[0:3], res[3:6], res[6:9]


def _attn_combine(att, s):
    def body(o0, o1, o2, l0, l1, l2, o_ref, lse_ref, od1, od2, ld1, ld2, scr):
        ov = [_from_dilated(scr, ref, r) for ref, r in zip((o0, o1, o2), _RS)]
        lv = [_from_dilated(scr, ref, r) for ref, r in zip((l0, l1, l2), _RS)]
        mx = jnp.maximum(jnp.maximum(lv[0], lv[1]), lv[2])
        ev = [jnp.exp(l - mx) for l in lv]
        z = ev[0] + ev[1] + ev[2]
        o = ((ev[0] * ov[0] + ev[1] * ov[1] + ev[2] * ov[2]) / z).astype(BF16)
        lse = mx + jnp.log(z)
        o_ref[...] = o
        lse_ref[...] = lse
        for ref, r in zip((od1, od2), _RS[1:]):
            _to_dilated(scr, o.astype(F32), ref, r)
        for ref, r in zip((ld1, ld2), _RS[1:]):
            _to_dilated(scr, lse, ref, r)

    return pl.pallas_call(
        body, name="attn_combine", grid=(s // DIL_ROWS,),
        in_specs=[_dil_spec(r) for r in _RS] * 2,
        out_specs=[_dil_spec(1)] * 2 + [_dil_spec(r) for r in _RS[1:]] * 2,
        out_shape=[_dil_shape(s, 1, BF16), _dil_shape(s, 1, F32)] + [_dil_shape(s, r, BF16) for r in _RS[1:]]
        + [_dil_shape(s, r, F32) for r in _RS[1:]],
        scratch_shapes=_DIL_SCRATCH, compiler_params=_params(("parallel",)),
    )(*[a[0] for a in att], *[a[1] for a in att])


def _dilate(t, s):
    def body(t_ref, o1, o2, scr):
        val = t_ref[...].astype(F32)
        for ref, r in zip((o1, o2), _RS[1:]):
            _to_dilated(scr, val, ref, r)

    return pl.pallas_call(
        body, name="attn_dilate", grid=(s // DIL_ROWS,), in_specs=[_dil_spec(1)], out_specs=[_dil_spec(r) for r in _RS[1:]],
        out_shape=[_dil_shape(s, r, t.dtype) for r in _RS[1:]], scratch_shapes=_DIL_SCRATCH, compiler_params=_params(("parallel",)),
    )(t)


def _rope_bwd(datt, d_glr, dp, cos_t, sin_t, s):
    tail = P_W - P_AQ

    def body(*refs):
        ins, cs, sn, glr, o_ref, scr = refs[:9], refs[9][...], refs[10][...], refs[11], refs[13], refs[14]
        for t in range(3):
            for g, r in enumerate(_RS):
                val = _from_dilated(scr, ins[3 * t + g], r)
                o_ref[:, t * ATT_W + g * 256:t * ATT_W + (g + 1) * 256] = (_rope_apply(val, cs, sn, True) if t < 2 else val).astype(BF16)
        o_ref[:, 3 * ATT_W:3 * ATT_W + 128] = glr[...]
        o_ref[:, 3 * ATT_W + 128:] = jnp.zeros((DIL_ROWS, tail - 3 * ATT_W - 128), BF16)

    return pl.pallas_call(
        body, name="rope_bwd", grid=(s // DIL_ROWS,),
        in_specs=[_dil_spec(r) for _ in range(3) for r in _RS] + [pl.BlockSpec((DIL_ROWS, 128), lambda i: (i, 0))] * 3
        + [pl.BlockSpec(memory_space=pl.ANY)],
        out_specs=pl.BlockSpec((DIL_ROWS, tail), lambda i: (i, P_AQ // tail)),
        out_shape=jax.ShapeDtypeStruct((s, P_W), BF16), input_output_aliases={12: 0},
        scratch_shapes=_DIL_SCRATCH, compiler_params=_params(("parallel",)),
    )(*[datt[g][t] for t in range(3) for g in range(3)], cos_t, sin_t, d_glr, dp)


def _tri_dot(tri, t):
    tb = tri.astype(BF16)
    hi = t.astype(BF16)
    r1 = t - hi.astype(F32)
    mid = r1.astype(BF16)
    lo = (r1 - mid.astype(F32)).astype(BF16)
    return _dg(tb, hi, 1, 0) + _dg(tb, mid, 1, 0) + _dg(tb, lo, 1, 0)


def _gla_decays(la_c, tri):
    b = _tri_dot(tri, la_c)
    row = lax.broadcasted_iota(jnp.int32, b.shape, 0)
    bmid = jnp.sum(jnp.where(row == GLA_CHUNK // 2 - 1, b, 0.0), axis=0, keepdims=True)
    blast = jnp.sum(jnp.where(row == GLA_CHUNK - 1, b, 0.0), axis=0, keepdims=True)
    return b, bmid, blast


def _gla_fwd(p, la, s, comm=()):
    tb, ch = GLA_BLOCK, GLA_CHUNK
    nb, nc = s // tb, tb // ch
    scale = GLA_DK ** -0.5
    c_ins, c_outs, c_alias, c_scratch = _carry(comm, 4, 2)

    def body(q_ref, k_ref, v_ref, la_ref, *rest):
        ci, (o_ref, st_ref) = rest[:len(c_ins)], rest[len(c_ins):len(c_ins) + 2]
        co, state = rest[len(c_ins) + 2:len(c_ins) + 2 + len(c_outs)], rest[len(c_ins) + 2 + len(c_outs)]
        step = pl.program_id(0)
        if comm:
            @pl.when(step == 0)
            def _():
                _comm_phase(comm, ci, co, rest[-2], rest[-1], True)

        _gla_fwd_step(q_ref, k_ref, v_ref, la_ref, o_ref, st_ref, state)
        if comm:
            @pl.when(step == nb - 1)
            def _():
                _comm_phase(comm, ci, co, rest[-2], rest[-1], False)

    def _gla_fwd_step(q_ref, k_ref, v_ref, la_ref, o_ref, st_ref, state):
        @pl.when(pl.program_id(0) == 0)
        def _():
            state[...] = jnp.zeros_like(state)

        ri = lax.broadcasted_iota(jnp.int32, (ch, ch), 0)
        ci = lax.broadcasted_iota(jnp.int32, (ch, ch), 1)
        causal = ci <= ri
        tri = causal.astype(F32)

        def chunk(c, carry):
            sl = pl.ds(pl.multiple_of(c * ch, ch), ch)
            b, bmid, blast = _gla_decays(la_ref[sl, :], tri)
            q = q_ref[sl, :].astype(F32) * scale
            k = k_ref[sl, :].astype(F32)
            v = v_ref[sl, :]
            qgt = (q * jnp.exp(b)).astype(BF16)
            qgn = (q * jnp.exp(b - bmid)).astype(BF16)
            kgn = (k * jnp.exp(bmid - b)).astype(BF16)
            kd = (k * jnp.exp(blast - b)).astype(BF16)
            dec = jnp.exp(blast)
            sts = [state[h] for h in range(GLA_H)]
            outs, news = [], []
            for h in range(GLA_H):
                hk, hv = slice(h * GLA_DK, (h + 1) * GLA_DK), slice(h * GLA_DV, (h + 1) * GLA_DV)
                a = jnp.where(causal, _dg(qgn[:, hk], kgn[:, hk], 1, 1), 0.0)
                outs.append(_dg(a.astype(BF16), v[:, hv], 1, 0) + _dg(qgt[:, hk], sts[h].astype(BF16), 1, 1))
                news.append(dec[:, hk] * sts[h] + _dg(v[:, hv], kd[:, hk], 0, 0))
            for h in range(GLA_H):
                st_ref[h, c] = sts[h]
                state[h] = news[h]
            o_ref[sl, :] = jnp.concatenate(outs, axis=1)
            return carry

        lax.fori_loop(0, nc, chunk, 0)

    hw = GLA_H * GLA_DK
    res = pl.pallas_call(
        body, name="gla_fwd", grid=(nb,),
        in_specs=[pl.BlockSpec((tb, hw), lambda t: (t, P_GQ // hw)),
                  pl.BlockSpec((tb, hw), lambda t: (t, P_GK // hw)),
                  pl.BlockSpec((tb, GLA_H * GLA_DV), lambda t: (t, P_GV // (GLA_H * GLA_DV))),
                  pl.BlockSpec((tb, hw), lambda t: (t, 0))] + [HBM] * len(c_ins),
        out_specs=[pl.BlockSpec((tb, GLA_H * GLA_DV), lambda t: (t, 0)),
                   pl.BlockSpec((GLA_H, nc, GLA_DV, GLA_DK), lambda t: (0, t, 0, 0))] + [HBM] * len(c_outs),
        out_shape=[jax.ShapeDtypeStruct((s, GLA_H * GLA_DV), F32),
                   jax.ShapeDtypeStruct((GLA_H, s // ch, GLA_DV, GLA_DK), F32)] + c_outs,
        scratch_shapes=[pltpu.VMEM((GLA_H, GLA_DV, GLA_DK), F32)] + c_scratch,
        input_output_aliases=c_alias,
        compiler_params=_params(("arbitrary",)),
    )(p, p, p, la, *c_ins)
    return res[0], res[1], _split_units(comm, res[2:])


def _gla_bwd(p, la, states, do, s, dp, comm=()):
    tb, ch = GLA_BLOCK, GLA_CHUNK
    nb, nc = s // tb, tb // ch
    scale = GLA_DK ** -0.5
    c_ins, c_outs, c_alias, c_scratch = _carry(comm, 7, 4)

    def body(q_ref, k_ref, v_ref, la_ref, st_ref, do_ref, dp_in, *rest):
        ci, outs = rest[:len(c_ins)], rest[len(c_ins):len(c_ins) + 4]
        co, dstate = rest[len(c_ins) + 4:len(c_ins) + 4 + len(c_outs)], rest[len(c_ins) + 4 + len(c_outs)]
        step = pl.program_id(0)
        if comm:
            @pl.when(step == 0)
            def _():
                _comm_phase(comm, ci, co, rest[-2], rest[-1], True)

        _gla_bwd_step(q_ref, k_ref, v_ref, la_ref, st_ref, do_ref, *outs, dstate)
        if comm:
            @pl.when(step == nb - 1)
            def _():
                _comm_phase(comm, ci, co, rest[-2], rest[-1], False)

    def _gla_bwd_step(q_ref, k_ref, v_ref, la_ref, st_ref, do_ref, dq_ref, dk_ref, dv_ref, dla_ref, dstate):
        @pl.when(pl.program_id(0) == 0)
        def _():
            dstate[...] = jnp.zeros_like(dstate)

        ri = lax.broadcasted_iota(jnp.int32, (ch, ch), 0)
        ci = lax.broadcasted_iota(jnp.int32, (ch, ch), 1)
        causal = ci <= ri
        tri = causal.astype(F32)
        tri_t = (ci >= ri).astype(F32)

        def chunk(cc, carry):
            c = nc - 1 - cc
            sl = pl.ds(pl.multiple_of(c * ch, ch), ch)
            b, bmid, blast = _gla_decays(la_ref[sl, :], tri)
            q = q_ref[sl, :].astype(F32) * scale
            k = k_ref[sl, :].astype(F32)
            v = v_ref[sl, :]
            e_b, e_qn, e_kn, e_kd = jnp.exp(b), jnp.exp(b - bmid), jnp.exp(bmid - b), jnp.exp(blast - b)
            dec = jnp.exp(blast)
            qgt, qgn, kgn, kd = q * e_b, q * e_qn, k * e_kn, k * e_kd
            qgt_b, qgn_b, kgn_b, kd_b = qgt.astype(BF16), qgn.astype(BF16), kgn.astype(BF16), kd.astype(BF16)
            do_b = do_ref[sl, :].astype(BF16)
            st0s = [st_ref[h, c] for h in range(GLA_H)]
            dsts = [dstate[h] for h in range(GLA_H)]
            dqgn, dqgt, dkgn, dkd, dvs, ddec, news = [], [], [], [], [], [], []
            for h in range(GLA_H):
                hk, hv = slice(h * GLA_DK, (h + 1) * GLA_DK), slice(h * GLA_DV, (h + 1) * GLA_DV)
                dst_b = dsts[h].astype(BF16)
                a = jnp.where(causal, _dg(qgn_b[:, hk], kgn_b[:, hk], 1, 1), 0.0).astype(BF16)
                da = jnp.where(causal, _dg(do_b[:, hv], v[:, hv], 1, 1), 0.0).astype(BF16)
                dqgn.append(_dg(da, kgn_b[:, hk], 1, 0))
                dqgt.append(_dg(do_b[:, hv], st0s[h].astype(BF16), 1, 0))
                dkgn.append(_dg(da, qgn_b[:, hk], 0, 0))
                dvs.append(_dg(a, do_b[:, hv], 0, 0) + _dg(kd_b[:, hk], dst_b, 1, 1))
                dkd.append(_dg(v[:, hv], dst_b, 1, 0))
                ddec.append(jnp.sum(st0s[h] * dsts[h], axis=0, keepdims=True))
                news.append(dec[:, hk] * dsts[h] + _dg(do_b[:, hv], qgt_b[:, hk], 0, 0))
            for h in range(GLA_H):
                dstate[h] = news[h]
            cat = lambda parts: jnp.concatenate(parts, axis=1)
            dqgn, dqgt, dkgn, dkd, ddec = cat(dqgn), cat(dqgt), cat(dkgn), cat(dkd), cat(ddec)
            dq_ref[sl, :] = (scale * (dqgn * e_qn + dqgt * e_b)).astype(dq_ref.dtype)
            dk_ref[sl, :] = (dkgn * e_kn + dkd * e_kd).astype(dk_ref.dtype)
            dv_ref[sl, :] = cat(dvs).astype(dv_ref.dtype)
            db = dqgn * qgn + dqgt * qgt - dkgn * kgn - dkd * kd
            extra = jnp.sum(dkd * kd, axis=0, keepdims=True) + ddec * dec
            dla_ref[sl, :] = _tri_dot(tri_t, db) + extra
            return carry

        lax.fori_loop(0, nc, chunk, 0)

    rev = lambda t: nb - 1 - t
    hw, vw = GLA_H * GLA_DK, GLA_H * GLA_DV
    res = pl.pallas_call(
        body, name="gla_bwd", grid=(nb,),
        in_specs=[pl.BlockSpec((tb, hw), lambda t: (rev(t), P_GQ // hw)),
                  pl.BlockSpec((tb, hw), lambda t: (rev(t), P_GK // hw)),
                  pl.BlockSpec((tb, vw), lambda t: (rev(t), P_GV // vw)),
                  pl.BlockSpec((tb, hw), lambda t: (rev(t), 0)),
                  pl.BlockSpec((GLA_H, nc, GLA_DV, GLA_DK), lambda t: (0, rev(t), 0, 0)),
                  pl.BlockSpec((tb, vw), lambda t: (rev(t), 0)), pl.BlockSpec(memory_space=pl.ANY)] + [HBM] * len(c_ins),
        out_specs=[pl.BlockSpec((tb, hw), lambda t: (rev(t), 0)),
                   pl.BlockSpec((tb, hw), lambda t: (rev(t), 0)),
                   pl.BlockSpec((tb, vw), lambda t: (rev(t), P_GV // vw)),
                   pl.BlockSpec((tb, hw), lambda t: (rev(t), 0))] + [HBM] * len(c_outs),
        out_shape=[jax.ShapeDtypeStruct((s, hw), BF16),
                   jax.ShapeDtypeStruct((s, hw), BF16),
                   jax.ShapeDtypeStruct((s, P_W), BF16),
                   jax.ShapeDtypeStruct((s, hw), F32)] + c_outs,
        scratch_shapes=[pltpu.VMEM((GLA_H, GLA_DV, GLA_DK), F32)] + c_scratch,
        input_output_aliases={6: 2, **c_alias},
        compiler_params=_params(("arbitrary",)),
    )(p, p, p, la, states, do, dp, *c_ins)
    return res[0], res[1], res[2], res[3], _split_units(comm, res[4:])


def _head_masks():
    lane = lax.broadcasted_iota(jnp.int32, (1, 4 * ATT_HD), 1)
    return [(lane >= h * ATT_HD) & (lane < (h + 1) * ATT_HD) for h in range(4)]


def _attn_fwd(qv, kv, pv, g, r, s):
    ln = s // r
    nblk = ln // ATT_BLK
    qcol = lambda pr: pr
    vcol = qcol
    prev = lambda n: jnp.maximum(n - 1, 0)

    def body(q_ref, kp_ref, kc_ref, vp_ref, vc_ref, o_ref, lse_ref):
        has_prev = pl.program_id(1) > 0
        ri = lax.broadcasted_iota(jnp.int32, (ATT_BLK, ATT_BLK), 0)
        ci = lax.broadcasted_iota(jnp.int32, (ATT_BLK, ATT_BLK), 1)
        m_cur = ci <= ri
        m_prev = (ci >= ri) & has_prev
        q, kp, kc, vp, vc = q_ref[...], kp_ref[...], kc_ref[...], vp_ref[...], vc_ref[...]
        o = jnp.zeros((ATT_BLK, 256), F32)
        lse = jnp.zeros((ATT_BLK, 256), F32)
        for hm in _head_masks():
            qm = jnp.where(hm, q, jnp.zeros_like(q))
            sc = jnp.where(m_cur, _dg(qm, kc, 1, 1) * 0.125, NEG)
            sp = jnp.where(m_prev, _dg(qm, kp, 1, 1) * 0.125, NEG)
            mx = jnp.maximum(jnp.max(sc, axis=1, keepdims=True), jnp.max(sp, axis=1, keepdims=True))
            pc, pp = jnp.exp(sc - mx), jnp.exp(sp - mx)
            den = jnp.sum(pc, axis=1, keepdims=True) + jnp.sum(pp, axis=1, keepdims=True)
            oh = (_dg(pc.astype(BF16), vc, 1, 0) + _dg(pp.astype(BF16), vp, 1, 0)) / den
            o = jnp.where(hm, oh, o)
            lse = jnp.where(hm, mx + jnp.log(den), lse)
        o_ref[...] = o.astype(o_ref.dtype)
        lse_ref[...] = lse

    blk = (ATT_BLK, 256)
    o, lse = pl.pallas_call(
        body, name=f"attn_fwd_{g}", grid=(r, nblk),
        in_specs=[pl.BlockSpec(blk, lambda pr, n: (n, qcol(pr))),
                  pl.BlockSpec(blk, lambda pr, n: (prev(n), qcol(pr))),
                  pl.BlockSpec(blk, lambda pr, n: (n, qcol(pr))),
                  pl.BlockSpec(blk, lambda pr, n: (prev(n), vcol(pr))),
                  pl.BlockSpec(blk, lambda pr, n: (n, vcol(pr)))],
        out_specs=[pl.BlockSpec(blk, lambda pr, n: (n, pr)), pl.BlockSpec(blk, lambda pr, n: (n, pr))],
        out_shape=[jax.ShapeDtypeStruct((ln, r * 256), BF16), jax.ShapeDtypeStruct((ln, r * 256), F32)],
        compiler_params=_params(("parallel", "parallel")),
    )(qv, kv, kv, pv, pv)
    return o, lse


def _attn_bwd(qv, kv, pv, dov, ov, lv, g, r, s):
    ln = s // r
    nblk = ln // ATT_BLK
    qcol = lambda pr: pr
    vcol = qcol
    prev = lambda n: jnp.maximum(n - 1, 0)
    nxt = lambda n: jnp.minimum(n + 1, nblk - 1)

    def body(qc_ref, qn_ref, kp_ref, kc_ref, vp_ref, vc_ref, doc_ref, don_ref, oc_ref, on_ref, lc_ref, ln_ref,
             dq_ref, dk_ref, dv_ref):
        n = pl.program_id(1)
        has_prev, has_next = n > 0, n < nblk - 1
        ri = lax.broadcasted_iota(jnp.int32, (ATT_BLK, ATT_BLK), 0)
        ci = lax.broadcasted_iota(jnp.int32, (ATT_BLK, ATT_BLK), 1)
        m_cur = ci <= ri
        m_prev = (ci >= ri) & has_prev
        m_next = (ci >= ri) & has_next
        qc, qn, kp, kc, vp, vc = qc_ref[...], qn_ref[...], kp_ref[...], kc_ref[...], vp_ref[...], vc_ref[...]
        doc, don = doc_ref[...], don_ref[...]
        pc_full = doc.astype(F32) * oc_ref[...].astype(F32)
        pn_full = don.astype(F32) * on_ref[...].astype(F32)
        lc, lnx = lc_ref[...], ln_ref[...]
        dq = jnp.zeros((ATT_BLK, 256), F32)
        dk = jnp.zeros((ATT_BLK, 256), F32)
        dv = jnp.zeros((ATT_BLK, 256), F32)
        zb = jnp.zeros_like(qc)
        for hm in _head_masks():
            qcm, qnm = jnp.where(hm, qc, zb), jnp.where(hm, qn, zb)
            docm, donm = jnp.where(hm, doc, zb), jnp.where(hm, don, zb)
            lse_c = jnp.max(jnp.where(hm, lc, NEG), axis=1, keepdims=True)
            lse_n = jnp.max(jnp.where(hm, lnx, NEG), axis=1, keepdims=True)
            del_c = jnp.sum(jnp.where(hm, pc_full, 0.0), axis=1, keepdims=True)
            del_n = jnp.sum(jnp.where(hm, pn_full, 0.0), axis=1, keepdims=True)
            pr_ = jnp.where(m_cur, jnp.exp(_dg(qcm, kc, 1, 1) * 0.125 - lse_c), 0.0)
            ds = (pr_ * (_dg(docm, vc, 1, 1) - del_c) * 0.125).astype(BF16)
            dqh = _dg(ds, kc, 1, 0)
            dkh = _dg(ds, qc, 0, 0)
            dvh = _dg(pr_.astype(BF16), doc, 0, 0)
            pr_ = jnp.where(m_prev, jnp.exp(_dg(qcm, kp, 1, 1) * 0.125 - lse_c), 0.0)
            ds = (pr_ * (_dg(docm, vp, 1, 1) - del_c) * 0.125).astype(BF16)
            dqh = dqh + _dg(ds, kp, 1, 0)
            pr_ = jnp.where(m_next, jnp.exp(_dg(qnm, kc, 1, 1) * 0.125 - lse_n), 0.0)
            ds = (pr_ * (_dg(donm, vc, 1, 1) - del_n) * 0.125).astype(BF16)
            dkh = dkh + _dg(ds, qn, 0, 0)
            dvh = dvh + _dg(pr_.astype(BF16), don, 0, 0)
            dq = jnp.where(hm, dqh, dq)
            dk = jnp.where(hm, dkh, dk)
            dv = jnp.where(hm, dvh, dv)
        dq_ref[...] = dq.astype(dq_ref.dtype)
        dk_ref[...] = dk.astype(dk_ref.dtype)
        dv_ref[...] = dv.astype(dv_ref.dtype)

    blk = (ATT_BLK, 256)
    cur = lambda col: pl.BlockSpec(blk, lambda pr, n: (n, col(pr)))
    prv = lambda col: pl.BlockSpec(blk, lambda pr, n: (prev(n), col(pr)))
    nx = lambda col: pl.BlockSpec(blk, lambda pr, n: (nxt(n), col(pr)))
    own = lambda pr: pr
    outs = pl.pallas_call(
        body, name=f"attn_bwd_{g}", grid=(r, nblk),
        in_specs=[cur(qcol), nx(qcol), prv(qcol), cur(qcol), prv(vcol), cur(vcol),
                  cur(own), nx(own), cur(own), nx(own), cur(own), nx(own)],
        out_specs=[cur(own), cur(own), cur(own)],
        out_shape=[jax.ShapeDtypeStruct((ln, r * 256), BF16)] * 3,
        compiler_params=_params(("parallel", "parallel")),
    )(qv, qv, kv, kv, pv, pv, dov, dov, ov, ov, lv, lv)
    return outs


def _gelu_parts(gv):
    cdf = 0.5 * (1.0 + lax.erf(gv * (2.0 ** -0.5)))
    pdf = jnp.exp(-0.5 * gv * gv) * (1.0 / math.sqrt(2.0 * math.pi))
    return cdf, pdf


def _pick_row(t, k):
    row = lax.broadcasted_iota(jnp.int32, t.shape, 0)
    return jnp.sum(jnp.where(row == k, t, 0.0), axis=0, keepdims=True)


def _shift_rows(u, halo, n):
    row = lax.broadcasted_iota(jnp.int32, u.shape, 0)
    out = pltpu.roll(u, n, 0)
    for k in range(n):
        out = jnp.where(row == k, _pick_row(halo, 16 - n + k), out)
    return out


def _shift_rows_up(u, halo, n):
    rb = u.shape[0]
    row = lax.broadcasted_iota(jnp.int32, u.shape, 0)
    out = pltpu.roll(u, rb - n, 0)
    for k in range(n):
        out = jnp.where(row == rb - n + k, _pick_row(halo, k), out)
    return out


def _conv(u, halo, cw, cb):
    return cb + _pick_row(cw, 0) * _shift_rows(u, halo, 2) + _pick_row(cw, 1) * _shift_rows(u, halo, 1) + _pick_row(cw, 2) * u


def _local_step(x, mod, pos_col, target, sm, w_sh, chip, core):
    s = x.shape[0]
    shift1, scale1, gate1, shift2, scale2, gate2 = [mod[i:i + 1, :] for i in range(6)]
    rb = 256
    chip1 = chip.reshape(1)

    def f_norm1(c, i, xv, nw, sc, sh):
        return ((xv * _rms(xv) * nw) * (1.0 + sc) + sh,)

    (h,) = _rowcall(f_norm1, [_rows(x, rb), _full(sm["n1w"]), _full(scale1), _full(shift1)],
                    [_orow(s, D, BF16, rb)], n_rows=s, rb=rb, name="norm1")
    own = lambda got, i: lax.dynamic_update_slice(got, w_sh[i], (chip, 0, 0))
    [got0] = _comm_call("gather_w_in_ici", [_u_gather_ici(w_sh, (0,))])
    [got0] = _comm_call("gather_w_in_d2d", [_u_gather_d2d(got0, (0,))])
    w = dict(win=_win_assemble(own(got0[0], 0)))
    p, [got123, got5] = _mm(h, w["win"], "in_proj", tm=1024, tn=1536,
                            comm=[_u_gather_ici(w_sh, (1, 2, 3)), _u_gather_ici(w_sh, (5,))])

    def f_gla_pre(c, i, glr, w2, gb):
        z = _dg(glr, w2.astype(BF16), 1, 0) + gb
        return ((jnp.minimum(z, 0.0) - jnp.log(1.0 + jnp.exp(-jnp.abs(z)))) * (1.0 / GLA_TAU),)

    (la,) = _rowcall(f_gla_pre, [_rows(p, rb, 128, P_LR // 128), _full(sm["w2"]), _full(sm["gb"])],
                     [_orow(s, 512, F32, rb)], n_rows=s, rb=rb, name="gla_pre")
    o_gla, states, [got123, got4] = _gla_fwd(p, la, s, comm=[_u_gather_d2d(got123, (1, 2, 3)), _u_gather_ici(w_sh, (4,))])
    got45 = got4 + got5
    w.update(wgb=own(got123[0], 1).reshape(1024, D), wab=_cols_join(own(got123[1], 2)), wout=own(got123[2], 3).reshape(D, D))

    def f_gla_post(c, i, ov, gnw, gr):
        on = jnp.concatenate([ov[:, k * 256:(k + 1) * 256] * _rms(ov[:, k * 256:(k + 1) * 256]) * gnw
                              for k in range(GLA_H)], axis=1)
        g = gr.astype(F32)
        return (on * (g * _sigmoid(g)),)

    (og,) = _rowcall(f_gla_post, [_rows(o_gla, rb), _full(sm["gnw"]), _rows(p, rb, 1024, P_GR // 1024)],
                     [_orow(s, 1024, BF16, rb)], n_rows=s, rb=rb, name="gla_post")
    y_gla = _mm(og, w["wgb"], "gla_branch")

    invf = jnp.tile(ROPE_THETA ** (-jnp.arange(ATT_HD // 2, dtype=F32) / (ATT_HD // 2)), 4).reshape(1, 128)
    cos_t, sin_t = _rope_tables(pos_col, invf, s)

    q_d, k_d, v_d = _rope_fwd(p, cos_t, sin_t, s)
    att = [_attn_fwd(q_d[g], k_d[g], v_d[g], g, r, s) for g, r in enumerate(_RS)]
    o_att, lse, o_d1, o_d2, lse_d1, lse_d2 = _attn_combine(att, s)
    y_att = _mm(o_att, w["wab"], "attn_branch")

    def f_merge(c, i, ma, mb, yg, ya):
        return (_sigmoid(ma.astype(F32)) * yg.astype(F32) + _sigmoid(mb.astype(F32)) * ya.astype(F32),)

    (mixed,) = _rowcall(f_merge, [_rows(p, rb, D, P_MA // D), _rows(p, rb, D, P_MB // D), _rows(y_gla, rb), _rows(y_att, rb)],
                        [_orow(s, D, BF16, rb)], n_rows=s, rb=rb, name="merge")
    z1, [got45] = _mm(mixed, w["wout"], "out_proj", comm=[_u_gather_d2d(got45, (4, 5))])
    w.update(wup=own(got45[0], 4), wdown=own(got45[1], 5).reshape(D_FF, D))

    def f_norm2(c, i, xv, z, g1, nw, sc, sh):
        x1 = xv + g1 * z.astype(F32)
        return (x1, (x1 * _rms(x1) * nw) * (1.0 + sc) + sh)

    x1, h2 = _rowcall(f_norm2, [_rows(x, rb), _rows(z1, rb), _full(gate1), _full(sm["n2w"]), _full(scale2), _full(shift2)],
                      [_orow(s, D, F32, rb), _orow(s, D, BF16, rb)], n_rows=s, rb=rb, name="norm2")
    u = _mm(h2, w["wup"], "up_proj", b_shards=True)

    cwid = 2 * W_UP_SH

    def f_ffn(c, i, uv, hl, cw, cb):
        uc = _conv(uv.astype(F32), hl.astype(F32) * (i > 0).astype(F32), cw, cb)
        val, gt = uc[:, :W_UP_SH], uc[:, W_UP_SH:]
        cdf, _ = _gelu_parts(gt)
        return (gt * cdf * val,)

    ccol = lambda c: c
    (hidden,) = _rowcall(f_ffn, [_rows(u, rb, cwid, ccol), _halo(u, rb, 16, cwid, ccol, True),
                                 _full(sm["cw"], cwid, ccol), _full(sm["cb"], cwid, ccol)],
                         [_orow(s, D_FF, BF16, rb, W_UP_SH, ccol)], n_rows=s, rb=rb, name="conv_geglu", ncol=2)
    z2 = _mm(hidden, w["wdown"], "down_proj", tk=D_FF)

    def f_final(c, i, x1v, z, g2, fw, tgt):
        x2 = x1v + g2 * z.astype(F32)
        r = _rms(x2)
        xh = x2 * r
        e = xh * fw - tgt
        loss = 0.5 * jnp.sum(jnp.mean(e * e, axis=-1, keepdims=True), axis=0, keepdims=True)
        dy = e * (1.0 / D)
        dxh = dy * fw
        dx2 = r * (dxh - xh * jnp.mean(dxh * xh, axis=-1, keepdims=True))
        return (loss, dx2, dx2 * g2, _csum(dy * xh), _csum(dx2 * z.astype(F32)))

    loss, dx2, dz2, d_fnw, d_gate2 = _rowcall(
        f_final, [_rows(x1, rb), _rows(z2, rb), _full(gate2), _full(sm["fnw"]), _rows(target, rb)],
        [_oacc(1, 1), _orow(s, D, F32, rb), _orow(s, D, BF16, rb), _oacc(1, D), _oacc(1, D)],
        n_rows=s, rb=rb, name="final_loss")
    d_hidden = _mm(dz2, w["wdown"], "down_proj_dx", tb=True, tn=1408)
    g_wdown = _mm(hidden, dz2, "down_proj_dw", ta=True, out_dtype=F32, tm=1408, tn=1024, tk=2048)

    def f_ffn_bwd(c, i, uv, hl, dh, cw, cb):
        uf = uv.astype(F32)
        hf = hl.astype(F32) * (i > 0).astype(F32)
        u1, u2 = _shift_rows(uf, hf, 1), _shift_rows(uf, hf, 2)
        uc = cb + _pick_row(cw, 0) * u2 + _pick_row(cw, 1) * u1 + _pick_row(cw, 2) * uf
        val, gt = uc[:, :W_UP_SH], uc[:, W_UP_SH:]
        cdf, pdf = _gelu_parts(gt)
        dhf = dh.astype(F32)
        duc = jnp.concatenate([dhf * (gt * cdf), dhf * val * (cdf + gt * pdf)], axis=1)
        dcw = jnp.concatenate([_csum(duc * u2), _csum(duc * u1), _csum(duc * uf)], axis=0)
        return (duc, _csum(duc), dcw)

    duc, d_cb, d_cw = _rowcall(
        f_ffn_bwd, [_rows(u, rb, cwid, ccol), _halo(u, rb, 16, cwid, ccol, True), _rows(d_hidden, rb, W_UP_SH, ccol),
                    _full(sm["cw"], cwid, ccol), _full(sm["cb"], cwid, ccol)],
        [_orow(s, 2 * D_FF, BF16, rb, cwid, ccol), _oacc(1, 2 * D_FF, cwid, ccol), _oacc(3, 2 * D_FF, cwid, ccol)],
        n_rows=s, rb=rb, name="conv_geglu_bwd", ncol=2)

    def f_conv_t(c, i, dv, hl, cw):
        df = dv.astype(F32)
        hf = hl.astype(F32) * (i < s // rb - 1).astype(F32)
        return (_pick_row(cw, 2) * df + _pick_row(cw, 1) * _shift_rows_up(df, hf, 1) + _pick_row(cw, 0) * _shift_rows_up(df, hf, 2),)

    (du,) = _rowcall(f_conv_t, [_rows(duc, rb, cwid, ccol), _halo(duc, rb, 16, cwid, ccol, False), _full(sm["cw"], cwid, ccol)],
                     [_orow(s, 2 * D_FF, BF16, rb, cwid, ccol)], n_rows=s, rb=rb, name="conv_transpose", ncol=2)
    g_wup = _mm(h2, du, "up_proj_dw", ta=True, out_dtype=F32, tm=1024, tk=2048, o_shards=True)
    gs45 = [g_wup, g_wdown.reshape(4, W_DOWN_SH, 1024)]
    d_h2, [land45] = _mm(du, w["wup"], "up_proj_dx", tb=True, b_shards=True, comm=[_u_pair_send(gs45, (4, 5))])
    ts45 = [_pair_add(g, ld, core, "grad_pair_add_" + BIG[i]) for g, ld, i in zip(gs45, land45, (4, 5))]

    def f_norm2_bwd(c, i, x1v, dh, dxr, z, nw, sc, g1):
        dxn, dsh, dsc, dnw = _norm_bwd(x1v, dh.astype(F32), nw, sc)
        dx1 = dxr + dxn
        return (dx1, dx1 * g1, dsh, dsc, dnw, _csum(dx1 * z.astype(F32)))

    dx1, dz1, d_shift2, d_scale2, d_n2w, d_gate1 = _rowcall(
        f_norm2_bwd, [_rows(x1, rb), _rows(d_h2, rb), _rows(dx2, rb), _rows(z1, rb), _full(sm["n2w"]), _full(scale2), _full(gate1)],
        [_orow(s, D, F32, rb), _orow(s, D, BF16, rb), _oacc(1, D), _oacc(1, D), _oacc(1, D), _oacc(1, D)],
        n_rows=s, rb=rb, name="norm2_bwd")
    d_mixed = _mm(dz1, w["wout"], "out_proj_dx", tb=True)
    g_wout = _mm(mixed, dz1, "out_proj_dw", ta=True, out_dtype=F32, tk=2048)

    def f_merge_bwd(c, i, dm, ma, mb, yg, ya):
        dmf, ygf, yaf = dm.astype(F32), yg.astype(F32), ya.astype(F32)
        sa, sb = _sigmoid(ma.astype(F32)), _sigmoid(mb.astype(F32))
        return (dmf * sa, dmf * sb, jnp.concatenate([dmf * ygf * sa * (1.0 - sa), dmf * yaf * sb * (1.0 - sb)], axis=1))

    dy_gla, dy_att, dp = _rowcall(
        f_merge_bwd, [_rows(d_mixed, rb), _rows(p, rb, D, P_MA // D), _rows(p, rb, D, P_MB // D), _rows(y_gla, rb), _rows(y_att, rb)],
        [_orow(s, D, BF16, rb)] * 2 + [_orow(s, P_W, BF16, rb, 2 * D, lambda c: P_MA // (2 * D))], n_rows=s, rb=rb, name="merge_bwd")
    d_og = _mm(dy_gla, w["wgb"], "gla_branch_dx", tb=True)
    g_wgb = _mm(og, dy_gla, "gla_branch_dw", ta=True, out_dtype=F32, tk=2048)
    d_oatt = _mm(dy_att, w["wab"], "attn_branch_dx", tb=True)
    g_wab = _mm(o_att, dy_att, "attn_branch_dw", ta=True, out_dtype=F32, tk=2048)

    def f_gla_post_bwd(c, i, ov, gnw, gr, dog):
        g = gr.astype(F32)
        sg = _sigmoid(g)
        silu = g * sg
        dof = dog.astype(F32)
        don = dof * silu
        on_parts, do_parts, dgn = [], [], jnp.zeros((1, 256), F32)
        for k in range(GLA_H):
            oh = ov[:, k * 256:(k + 1) * 256]
            dh = don[:, k * 256:(k + 1) * 256]
            r = _rms(oh)
            xh = oh * r
            dgn = dgn + _csum(dh * xh)
            dxh = dh * gnw
            do_parts.append(r * (dxh - xh * jnp.mean(dxh * xh, axis=-1, keepdims=True)))
            on_parts.append(xh * gnw)
        on = jnp.concatenate(on_parts, axis=1)
        dgr = dof * on * (sg * (1.0 + g * (1.0 - sg)))
        return (jnp.concatenate(do_parts, axis=1), dgr, dgn)

    do_gla, dp, d_gnw = _rowcall(
        f_gla_post_bwd, [_rows(o_gla, rb), _full(sm["gnw"]), _rows(p, rb, 1024, P_GR // 1024), _rows(d_og, rb)],
        [_orow(s, 1024, F32, rb), _orow(s, P_W, BF16, rb, 1024, lambda c: P_GR // 1024), _oacc(1, 256)],
        n_rows=s, rb=rb, name="gla_post_bwd", into=(dp, 1))
    gs123 = [g_wgb.reshape(4, 256, 1024), _cols_split(g_wab), g_wout.reshape(4, 256, 1024)]
    d_gq, d_gk, dp, d_la, [r4, land123] = _gla_bwd(p, la, states, do_gla, s, dp,
                                                   comm=[_u_chip_exchange(ts45[:1]), _u_pair_send(gs123, (1, 2, 3))])
    half4 = [_chip_sum(ts45[0], r4[0], chip1, "grad_chip_sum_w_up")]
    ts123 = [_pair_add(g, ld, core, "grad_pair_add_" + BIG[i]) for g, ld, i in zip(gs123, land123, (1, 2, 3))]

    def f_gla_pre_bwd(c, i, lav, dlav, glr, w2):
        dz = dlav * (1.0 / GLA_TAU) * (1.0 - jnp.exp(GLA_TAU * lav))
        dzb = dz.astype(BF16)
        return (_dg(dzb, w2.astype(BF16), 1, 1), _csum(dz), _dg(glr, dzb, 0, 0))

    d_glr, d_gb, d_w2 = _rowcall(
        f_gla_pre_bwd, [_rows(la, rb), _rows(d_la, rb), _rows(p, rb, 128, P_LR // 128), _full(sm["w2"])],
        [_orow(s, 128, BF16, rb), _oacc(1, 512), _oacc(128, 512)], n_rows=s, rb=rb, name="gla_pre_bwd")

    do_d = [d_oatt] + list(_dilate(d_oatt, s))
    datt = [_attn_bwd(q_d[g], k_d[g], v_d[g], do_d[g], (o_att, o_d1, o_d2)[g], (lse, lse_d1, lse_d2)[g], g, r, s)
            for g, r in enumerate(_RS)]
    dp = _rope_bwd(datt, d_glr, dp, cos_t, sin_t, s)
    dp = lax.dynamic_update_slice(dp, jnp.concatenate([d_gq, d_gk], axis=1), (0, P_GQ))
    g_win, [r1235, oth4] = _mm(h, dp, "in_proj_dw", ta=True, out_dtype=F32, tm=1024, tn=1536, tk=2048,
                               comm=[_u_chip_exchange(ts123 + ts45[1:]), _u_pair_join(half4)])
    half1235 = [_chip_sum(t, r, chip1, "grad_chip_sum_" + BIG[i]) for t, r, i in zip(ts123 + ts45[1:], r1235, (1, 2, 3, 5))]
    gs0 = [_win_split(g_win)]
    d_h, [land0, oth1235] = _mm(dp, w["win"], "in_proj_dx", tb=True, tk=3840,
                                comm=[_u_pair_send(gs0, (0,)), _u_pair_join(half1235)])
    half123, half45 = half1235[:3], half4 + half1235[3:]
    oth123, oth45 = oth1235[:3], oth4 + oth1235[3:]
    ts0 = [_pair_add(gs0[0], land0[0], core, "grad_pair_add_w_in")]
    [r0] = _comm_call("grad_exchange_w_in", [_u_chip_exchange(ts0)])
    half0 = [_chip_sum(ts0[0], r0[0], chip1, "grad_chip_sum_w_in")]
    [oth0] = _comm_call("grad_join_w_in", [_u_pair_join(half0)])

    def f_norm1_bwd(c, i, xv, dh, dxr, nw, sc):
        dxn, dsh, dsc, dnw = _norm_bwd(xv, dh.astype(F32), nw, sc)
        return (dxr + dxn, dsh, dsc, dnw)

    grad_x, d_shift1, d_scale1, d_n1w = _rowcall(
        f_norm1_bwd, [_rows(x, rb), _rows(d_h, rb), _rows(dx1, rb), _full(sm["n1w"]), _full(scale1)],
        [_orow(s, D, F32, rb), _oacc(1, D), _oacc(1, D), _oacc(1, D)], n_rows=s, rb=rb, name="norm1_bwd")

    dmod = jnp.concatenate([d_shift1, d_scale1, d_gate1, d_shift2, d_scale2, d_gate2], axis=1)
    small = dict(dmod=dmod, n1w=d_n1w, gb=d_gb, gnw=d_gnw, n2w=d_n2w, cb=d_cb, fnw=d_fnw, w2=d_w2, cw=d_cw)
    return loss, grad_x, half0 + half123 + half45, oth0 + oth123 + oth45, small


def _win_pieces():
    runs = [(P_GV, 1024, 2048), (P_MA, 5392, 2048), (P_GQ, 0, 1024), (P_AQ, 3088, 2304), (P_LR, 3072, GLA_LR)]
    out = []
    for kc, rc, ln in runs:
        while ln > 0:
            step = min(ln, W_IN_SH - rc % W_IN_SH)
            out.append((kc, rc, step))
            kc, rc, ln = kc + step, rc + step, ln - step
    return out


def _win_assemble(shards):
    rb = 256

    def body(s_ref, o_ref):
        o_ref[:, W_IN:] = jnp.zeros((rb, P_W - W_IN), o_ref.dtype)
        for kc, rc, ln in _win_pieces():
            o_ref[:, kc:kc + ln] = s_ref[rc // W_IN_SH, :, rc % W_IN_SH:rc % W_IN_SH + ln]

    return pl.pallas_call(
        body, name="w_in_assemble", grid=(D // rb,),
        in_specs=[pl.BlockSpec((4, rb, W_IN_SH), lambda i: (0, i, 0))], out_specs=pl.BlockSpec((rb, P_W), lambda i: (i, 0)),
        out_shape=jax.ShapeDtypeStruct((D, P_W), shards.dtype), compiler_params=_params(("parallel",)),
    )(shards)


def _win_split(g):
    rb = 256

    def body(g_ref, o_ref):
        for kc, rc, ln in _win_pieces():
            o_ref[rc // W_IN_SH, :, rc % W_IN_SH:rc % W_IN_SH + ln] = g_ref[:, kc:kc + ln]

    return pl.pallas_call(
        body, name="w_in_grad_split", grid=(D // rb,),
        in_specs=[pl.BlockSpec((rb, P_W), lambda i: (i, 0))], out_specs=pl.BlockSpec((4, rb, W_IN_SH), lambda i: (0, i, 0)),
        out_shape=jax.ShapeDtypeStruct((4, D, W_IN_SH), g.dtype), compiler_params=_params(("parallel",)),
    )(g)


def _ff_to_kernel(a):
    h = W_UP_SH
    return jnp.concatenate([a[:, 0:h], a[:, D_FF:D_FF + h], a[:, h:D_FF], a[:, D_FF + h:]], axis=1)


def _ff_from_kernel(a):
    h = W_UP_SH
    return jnp.concatenate([a[:, 0:h], a[:, 2 * h:3 * h], a[:, h:2 * h], a[:, 3 * h:]], axis=1)


BIG = ("w_in", "w_gla_branch", "w_attn_branch", "w_out", "w_up", "w_down")
SH_SHAPES = ((1024, W_IN_SH), (256, 1024), (256, 256), (256, 1024), (1024, W_UP_SH), (W_DOWN_SH, 1024))
N_BIG = len(BIG)


def _cols_join(t):
    return jnp.concatenate([t[k] for k in range(4)], axis=1)


def _cols_split(t):
    cols = t.shape[1] // 4
    return jnp.stack([t[:, k * cols:(k + 1) * cols] for k in range(4)])


def _me():
    return lax.axis_index("x"), lax.axis_index("y"), lax.axis_index("c")


HBM = pl.BlockSpec(memory_space=pltpu.HBM)
VMEM_SPEC = pl.BlockSpec(memory_space=pltpu.VMEM)


def _allgather8(xs, name):
    rows = xs.shape[0]

    def body(x_ref, out_ref, send_sems, recv_sems, local_sem):
        x, y, c = _me()
        me = 4 * x + 2 * y + c
        mine = pltpu.make_async_copy(x_ref, out_ref.at[me], local_sem)
        mine.start()
        flips = [(k >> 2 & 1, k >> 1 & 1, k & 1) for k in range(1, 8)]

        def peer(f):
            return (jnp.where(f[0] == 1, 1 - x, x), jnp.where(f[1] == 1, 1 - y, y), jnp.where(f[2] == 1, 1 - c, c))

        sends = []
        for k, f in enumerate(flips):
            cp = pltpu.make_async_remote_copy(src_ref=x_ref, dst_ref=out_ref.at[me], send_sem=send_sems.at[k],
                                              recv_sem=recv_sems.at[k], device_id=peer(f), device_id_type=MESH)
            cp.start()
            sends.append(cp)
        for k, f in enumerate(flips):
            px, py, pc = peer(f)
            pltpu.make_async_remote_copy(src_ref=x_ref, dst_ref=out_ref.at[4 * px + 2 * py + pc], send_sem=send_sems.at[k],
                                         recv_sem=recv_sems.at[k], device_id=peer(f), device_id_type=MESH).wait_recv()
        for cp in sends:
            cp.wait_send()
        mine.wait()

    return pl.pallas_call(
        body, name=name, out_shape=jax.ShapeDtypeStruct((8, rows, 128), F32),
        in_specs=[VMEM_SPEC], out_specs=VMEM_SPEC,
        scratch_shapes=[pltpu.SemaphoreType.DMA((7,)), pltpu.SemaphoreType.DMA((7,)), pltpu.SemaphoreType.DMA],
        compiler_params=pltpu.CompilerParams(vmem_limit_bytes=VMEM_LIMIT),
    )(xs)


def _half_rows(i, cc, unit):
    rows = SH_SHAPES[i][0] // 2
    return pl.ds(pl.multiple_of(cc * rows, unit), rows)


def _rc(src, dst, sems, to):
    return pltpu.make_async_remote_copy(src_ref=src, dst_ref=dst, send_sem=sems[0], recv_sem=sems[1], device_id=to, device_id_type=MESH)


def _other_chips(x, y):
    return [(1 - x, y), (x, 1 - y), (1 - x, 1 - y)]


def _u_gather_ici(w_sh, idxs):
    def copies(ins, outs, sem):
        x, y, c = _me()
        res = []
        for j, (px, py) in enumerate(_other_chips(x, y)):
            for n, i in enumerate(idxs):
                src = ins[n].at[0, _half_rows(i, c, 16)]
                res.append((_rc(src, outs[n].at[2 * x + y, _half_rows(i, c, 16)], sem(j * len(idxs) + n), (px, py, c)),
                            _rc(src, outs[n].at[2 * px + py, _half_rows(i, c, 16)], sem(j * len(idxs) + n), (px, py, c))))
        return res

    return dict(ins=[w_sh[i] for i in idxs], outs=[jax.ShapeDtypeStruct((4,) + SH_SHAPES[i], BF16) for i in idxs],
                nsem=3 * len(idxs), alias={}, copies=copies)


def _u_gather_d2d(got, idxs):
    def copies(ins, outs, sem):
        x, y, c = _me()
        res = []
        for j, (px, py) in enumerate(_other_chips(x, y)):
            for n, i in enumerate(idxs):
                src = ins[n].at[2 * px + py, _half_rows(i, c, 16)]
                res.append((_rc(src, outs[n].at[2 * px + py, _half_rows(i, c, 16)], sem(j * len(idxs) + n), (x, y, 1 - c)),
                            _rc(src, outs[n].at[2 * px + py, _half_rows(i, 1 - c, 16)], sem(j * len(idxs) + n), (x, y, 1 - c))))
        return res

    return dict(ins=list(got), outs=[jax.ShapeDtypeStruct(g.shape, g.dtype) for g in got], nsem=3 * len(idxs),
                alias={n: n for n in range(len(idxs))}, copies=copies)


def _u_pair_send(gs, idxs):
    def copies(ins, outs, sem):
        x, y, c = _me()
        res = []
        for n, i in enumerate(idxs):
            for sh in range(4):
                cp = _rc(ins[n].at[sh, _half_rows(i, 1 - c, 8)], outs[n].at[sh], sem(4 * n + sh), (x, y, 1 - c))
                res.append((cp, cp))
        return res

    return dict(ins=list(gs), outs=[jax.ShapeDtypeStruct((4, SH_SHAPES[i][0] // 2, SH_SHAPES[i][1]), F32) for i in idxs],
                nsem=4 * len(idxs), alias={}, copies=copies)


def _u_chip_exchange(ts):
    def copies(ins, outs, sem):
        x, y, c = _me()
        res = []
        for j, (px, py) in enumerate(_other_chips(x, y)):
            for n in range(len(ts)):
                cp = _rc(ins[n].at[2 * px + py], outs[n].at[j], sem(j * len(ts) + n), (px, py, c))
                res.append((cp, cp))
        return res

    return dict(ins=list(ts), outs=[jax.ShapeDtypeStruct((3,) + t.shape[1:], t.dtype) for t in ts], nsem=3 * len(ts),
                alias={}, copies=copies)


def _u_pair_join(hs):
    def copies(ins, outs, sem):
        x, y, c = _me()
        res = []
        for n in range(len(hs)):
            cp = _rc(ins[n], outs[n], sem(n), (x, y, 1 - c))
            res.append((cp, cp))
        return res

    return dict(ins=list(hs), outs=[jax.ShapeDtypeStruct(h.shape, h.dtype) for h in hs], nsem=len(hs), alias={}, copies=copies)


def _comm_phase(units, ci, co, send_sems, recv_sems, start):
    ii = oo = off = 0
    for u in units:
        ni, no = len(u["ins"]), len(u["outs"])
        for st, arrival in u["copies"](ci[ii:ii + ni], co[oo:oo + no], lambda k, off=off: (send_sems.at[off + k], recv_sems.at[off + k])):
            if start:
                st.start()
            else:
                st.wait_send()
                arrival.wait_recv()
        ii, oo, off = ii + ni, oo + no, off + u["nsem"]


def _carry(units, n_in, n_out):
    ins = [a for u in units for a in u["ins"]]
    outs = [o for u in units for o in u["outs"]]
    alias, ii, oo = {}, 0, 0
    for u in units:
        for a, b in u["alias"].items():
            alias[n_in + ii + a] = n_out + oo + b
        ii, oo = ii + len(u["ins"]), oo + len(u["outs"])
    nsem = sum(u["nsem"] for u in units)
    scratch = [pltpu.SemaphoreType.DMA((nsem,)), pltpu.SemaphoreType.DMA((nsem,))] if units else []
    return ins, outs, alias, scratch


def _split_units(units, res):
    out, oo = [], 0
    for u in units:
        out.append(list(res[oo:oo + len(u["outs"])]))
        oo += len(u["outs"])
    return out


def _comm_call(name, units):
    ins, outs, alias, scratch = _carry(units, 0, 0)

    def body(*refs):
        ci, co = refs[:len(ins)], refs[len(ins):len(ins) + len(outs)]
        _comm_phase(units, ci, co, refs[-2], refs[-1], True)
        _comm_phase(units, ci, co, refs[-2], refs[-1], False)

    res = pl.pallas_call(body, name=name, out_shape=outs, in_specs=[HBM] * len(ins), out_specs=[HBM] * len(outs),
                         scratch_shapes=scratch, input_output_aliases=alias)(*ins)
    return _split_units(units, res)


def _pair_add(g, land, core, name):
    _, rows, cols = g.shape
    half = rows // 2
    rb = _tile(half, 256, 16)
    nb = half // rb

    def body(c_ref, g_ref, l_ref, o_ref):
        o_ref[...] = (g_ref[...] + l_ref[...]).astype(BF16)

    return pl.pallas_call(
        body, name=name,
        grid_spec=pltpu.PrefetchScalarGridSpec(
            num_scalar_prefetch=1, grid=(4, nb),
            in_specs=[pl.BlockSpec((1, rb, cols), lambda s, i, c_ref: (s, c_ref[0] * nb + i, 0)),
                      pl.BlockSpec((1, rb, cols), lambda s, i, c_ref: (s, i, 0))],
            out_specs=pl.BlockSpec((1, rb, cols), lambda s, i, c_ref: (s, i, 0))),
        out_shape=jax.ShapeDtypeStruct((4, half, cols), BF16),
        compiler_params=_params(("parallel", "parallel")),
    )(core, g, land)


def _chip_sum(t, r, chip, name):
    _, half, cols = t.shape
    rb = _tile(half, 256, 16)

    def body(s_ref, t_ref, r_ref, o_ref):
        o_ref[...] = ((t_ref[0].astype(F32) + r_ref[0].astype(F32)) + r_ref[1].astype(F32)) + r_ref[2].astype(F32)

    return pl.pallas_call(
        body, name=name,
        grid_spec=pltpu.PrefetchScalarGridSpec(
            num_scalar_prefetch=1, grid=(half // rb,),
            in_specs=[pl.BlockSpec((1, rb, cols), lambda i, s_ref: (s_ref[0], i, 0)),
                      pl.BlockSpec((3, rb, cols), lambda i, s_ref: (0, i, 0))],
            out_specs=pl.BlockSpec((rb, cols), lambda i, s_ref: (i, 0))),
        out_shape=jax.ShapeDtypeStruct((half, cols), F32),
        compiler_params=_params(("parallel",)),
    )(chip, t, r)


def _adam_math(wv, gv, mv, vv):
    mn = ADAM_B1 * mv + (1.0 - ADAM_B1) * gv
    vn = ADAM_B2 * vv + (1.0 - ADAM_B2) * (gv * gv)
    m_hat = mn / (1.0 - ADAM_B1 ** ADAM_STEP)
    v_hat = vn / (1.0 - ADAM_B2 ** ADAM_STEP)
    return -ADAM_LR * (m_hat / (jnp.sqrt(v_hat) + ADAM_EPS) + ADAM_WD * wv), mn, vn


def _adamw_halves(wt, mt, vt, mine, theirs, core, name):
    _, rows, cols = wt.shape
    half = rows // 2
    rb = _tile(half, 256, 8)
    nb = half // rb

    def body(c_ref, w_ref, m_ref, v_ref, a_ref, b_ref, g_ref, d_ref, mo_ref, vo_ref):
        gv = jnp.where(pl.program_id(0) == c_ref[0], a_ref[...], b_ref[...])
        dl, mn, vn = _adam_math(w_ref[...], gv, m_ref[...], v_ref[...])
        g_ref[...] = gv
        d_ref[...] = dl
        mo_ref[...] = mn
        vo_ref[...] = vn

    full = pl.BlockSpec((None, rb, cols), lambda hf, i, c_ref: (0, hf * nb + i, 0))
    part = pl.BlockSpec((rb, cols), lambda hf, i, c_ref: (i, 0))
    return pl.pallas_call(
        body, name=name,
        grid_spec=pltpu.PrefetchScalarGridSpec(num_scalar_prefetch=1, grid=(2, nb), in_specs=[full, full, full, part, part],
                                               out_specs=[full] * 4),
        out_shape=[jax.ShapeDtypeStruct((1, rows, cols), F32)] * 4,
        compiler_params=_params(("parallel", "parallel")),
    )(core, wt, mt, vt, mine, theirs)


SG_REP = 144
SG_LOSS = 136
SG_W2, SG_CW = SG_REP, SG_REP + 4 * 16
SG_ROWS = SG_CW + 4 * 40
SP_ROWS = SG_REP + 16 + 40


def _mod_shard(c_all, ada_w_sh):
    def body(c_ref, w_ref, o_ref):
        cv = c_ref[...]
        o_ref[...] = _dg((cv * _sigmoid(cv)).astype(BF16), w_ref[...].astype(BF16), 1, 0)

    return pl.pallas_call(body, name="mod_shard", out_shape=jax.ShapeDtypeStruct((8, 1536), F32),
                          in_specs=[VMEM_SPEC, VMEM_SPEC], out_specs=VMEM_SPEC,
                          compiler_params=pltpu.CompilerParams(vmem_limit_bytes=VMEM_LIMIT))(c_all, ada_w_sh)


def _mod_select(mod_all, ada_b4):
    def body(m_ref, b_ref, o_ref):
        x, y, c = _me()
        me = 4 * x + 2 * y + c
        for sh in range(4):
            o_ref[sh] = m_ref[2 * sh, me] + b_ref[sh]

    return pl.pallas_call(body, name="mod_select", out_shape=jax.ShapeDtypeStruct((4, 12, 128), F32),
                          in_specs=[VMEM_SPEC, VMEM_SPEC], out_specs=VMEM_SPEC)(mod_all, ada_b4)


def _small_reduce(sg_all):
    def body(g_ref, o_ref):
        x, y, c = _me()
        s_me = 2 * x + y
        w2_rows = pl.ds(pl.multiple_of(SG_W2 + 16 * s_me, 8), 16)
        cw_rows = pl.ds(pl.multiple_of(SG_CW + 40 * s_me, 8), 40)
        a = g_ref[0, 0:SG_REP, :]
        b = g_ref[0, w2_rows, :]
        d = g_ref[0, cw_rows, :]
        for dev in range(1, 8):
            a = a + g_ref[dev, 0:SG_REP, :]
            b = b + g_ref[dev, w2_rows, :]
            d = d + g_ref[dev, cw_rows, :]
        o_ref[0:SG_REP, :] = a
        o_ref[SG_REP:SG_REP + 16, :] = b
        o_ref[SG_REP + 16:SP_ROWS, :] = d

    return pl.pallas_call(body, name="small_grad_reduce", out_shape=jax.ShapeDtypeStruct((SP_ROWS, 128), F32),
                          in_specs=[VMEM_SPEC], out_specs=VMEM_SPEC)(sg_all)


def _ada_grad(dmod_all, c_bc):
    def body(g_ref, c_ref, o_ref):
        x, y, c = _me()
        s_me = 2 * x + y
        for k in range(12):
            acc = jnp.zeros((D, 128), F32)
            for b in range(8):
                cv = c_ref[b]
                acc = acc + (cv * _sigmoid(cv)) * g_ref[s_me, k, b:b + 1, :]
            o_ref[:, k * 128:(k + 1) * 128] = acc

    return pl.pallas_call(body, name="ada_w_grad", out_shape=jax.ShapeDtypeStruct((D, 1536), F32),
                          in_specs=[VMEM_SPEC, VMEM_SPEC], out_specs=VMEM_SPEC,
                          compiler_params=pltpu.CompilerParams(vmem_limit_bytes=VMEM_LIMIT))(dmod_all, c_bc)


def _adamw(wt, g, m, v, name):
    rows, cols = wt.shape
    rb = _tile(rows, 256, 8)

    def fn(c, i, wv, gv, mv, vv):
        return _adam_math(wv, gv, mv, vv)

    return _rowcall(fn, [_rows(t, rb) for t in (wt, g, m, v)], [_orow(rows, cols, F32, rb)] * 3,
                    n_rows=rows, rb=rb, name=name)


def _pad_rows(t, rows):
    flat = t.reshape(-1)
    return jnp.pad(flat, (0, rows * 128 - flat.shape[0])).reshape(rows, 128)


SP_LAYOUT = (("ada_b", 48), ("norm1_w", 8), ("gla_gate_b", 8), ("gla_norm_w", 8), ("norm2_w", 8), ("conv_b", 48),
             ("final_norm_w", 8), (None, 8), ("gla_gate_w2", 16), ("conv_w", 40))


def _pack_small(d):
    return jnp.concatenate([jnp.zeros((rows, 128), F32) if n is None else _pad_rows(d[n].astype(F32), rows)
                            for n, rows in SP_LAYOUT], axis=0)


def _unpack_small(pk, shapes):
    out, off = {}, 0
    for n, rows in SP_LAYOUT:
        if n is not None:
            shp = shapes[n]
            out[n] = pk[off:off + rows].reshape(-1)[:math.prod(shp)].reshape(shp)
        off += rows
    return out


def kernel(x, c, positions, ada_w, ada_b, norm1_w, w_in, gla_gate_w2, gla_gate_b, gla_norm_w, w_gla_branch, w_attn_branch, w_out, norm2_w, w_up, conv_w, conv_b, w_down, final_norm_w, loss_target, m_ada_w, m_ada_b, m_norm1_w, m_w_in, m_gla_gate_w2, m_gla_gate_b, m_gla_norm_w, m_w_gla_branch, m_w_attn_branch, m_w_out, m_norm2_w, m_w_up, m_conv_w, m_conv_b, m_w_down, m_final_norm_w, v_ada_w, v_ada_b, v_norm1_w, v_w_in, v_gla_gate_w2, v_gla_gate_b, v_gla_norm_w, v_w_gla_branch, v_w_attn_branch, v_w_out, v_norm2_w, v_w_up, v_conv_w, v_conv_b, v_w_down, v_final_norm_w):
    s = x.shape[1]
    names = ("ada_w", "ada_b", "norm1_w", "w_in", "gla_gate_w2", "gla_gate_b", "gla_norm_w", "w_gla_branch", "w_attn_branch",
             "w_out", "norm2_w", "w_up", "conv_w", "conv_b", "w_down", "final_norm_w")
    wts = dict(zip(names, (ada_w, ada_b, norm1_w, w_in, gla_gate_w2, gla_gate_b, gla_norm_w, w_gla_branch, w_attn_branch,
                           w_out, norm2_w, w_up, conv_w, conv_b, w_down, final_norm_w)))
    ms = dict(zip(names, (m_ada_w, m_ada_b, m_norm1_w, m_w_in, m_gla_gate_w2, m_gla_gate_b, m_gla_norm_w, m_w_gla_branch,
                          m_w_attn_branch, m_w_out, m_norm2_w, m_w_up, m_conv_w, m_conv_b, m_w_down, m_final_norm_w)))
    vs = dict(zip(names, (v_ada_w, v_ada_b, v_norm1_w, v_w_in, v_gla_gate_w2, v_gla_gate_b, v_gla_norm_w, v_w_gla_branch,
                          v_w_attn_branch, v_w_out, v_norm2_w, v_w_up, v_conv_w, v_conv_b, v_w_down, v_final_norm_w)))

    pk0 = jnp.concatenate([_pad_rows(c, 8), _pad_rows(gla_gate_w2, 16), _pad_rows(conv_w, 40)], axis=0)
    sm_all = _allgather8(pk0, "gather_small")
    c_all = sm_all[:, 0:8, :].reshape(8, D)
    w2_full = sm_all[0::2, 8:24, :].transpose(1, 0, 2).reshape(GLA_LR, 512)
    cw_full = sm_all[0::2, 24:64, :].reshape(4, 40 * 128)[:, :3 * W_UP_SH].reshape(4, 3, W_UP_SH).transpose(1, 0, 2).reshape(3, 2 * D_FF)

    mod_sh = _mod_shard(c_all, ada_w[0])
    mod_all = _allgather8(mod_sh.reshape(96, 128), "gather_mod")
    mod = _mod_select(mod_all.reshape(8, 8, 12, 128), ada_b.reshape(4, 12, 128)).reshape(6, D)

    core = lax.axis_index("c").astype(jnp.int32).reshape(1)
    chip = (2 * lax.axis_index("x") + lax.axis_index("y")).astype(jnp.int32)
    w_sh = [wts[n].astype(BF16) for n in BIG]
    sm = dict(n1w=norm1_w, n2w=norm2_w, fnw=final_norm_w.reshape(1, D), gnw=gla_norm_w, gb=gla_gate_b,
              w2=jnp.pad(w2_full, ((0, 128 - GLA_LR), (0, 0))), cw=_ff_to_kernel(cw_full), cb=_ff_to_kernel(conv_b))
    loss, grad_x, halves, others, small = _local_step(x[0], mod, positions.reshape(s, 1), loss_target[0], sm, w_sh, chip, core)

    dcw = _ff_from_kernel(small["cw"]).reshape(3, 4, W_UP_SH).transpose(1, 0, 2)
    dw2 = small["w2"][:GLA_LR].reshape(GLA_LR, 4, 128).transpose(1, 0, 2)
    sg = jnp.concatenate(
        [_pad_rows(small["dmod"], 48), _pad_rows(small["n1w"], 8), _pad_rows(small["gb"], 8), _pad_rows(small["gnw"], 8),
         _pad_rows(small["n2w"], 8), _pad_rows(_ff_from_kernel(small["cb"]), 48), _pad_rows(small["fnw"], 8), _pad_rows(loss, 8)]
        + [_pad_rows(dw2[k], 16) for k in range(4)] + [_pad_rows(dcw[k], 40) for k in range(4)], axis=0)
    sg_all = _allgather8(sg, "gather_small_grads")
    g_small_pk = _small_reduce(sg_all)
    dmod_all = sg_all[:, 0:48, :].reshape(8, 4, 12, 128).transpose(1, 2, 0, 3)
    g_ada_w = _ada_grad(dmod_all, jnp.broadcast_to(c_all[:, :, None], (8, D, 128)))

    shapes = {n: wts[n].shape for n in names}
    g_small = _unpack_small(g_small_pk, shapes)
    grads = {"ada_w": g_ada_w.reshape(1, D, 1536), **g_small}
    deltas, new_m, new_v = {}, {}, {}
    for n, mine, theirs in zip(BIG, halves, others):
        grads[n], deltas[n], new_m[n], new_v[n] = _adamw_halves(wts[n], ms[n], vs[n], mine, theirs, core, "adamw_" + n)
    shp = ada_w.shape
    d_, m_, v_ = _adamw(ada_w[0], g_ada_w, m_ada_w[0], v_ada_w[0], "adamw_ada_w")
    deltas["ada_w"], new_m["ada_w"], new_v["ada_w"] = d_.reshape(shp), m_.reshape(shp), v_.reshape(shp)
    d_, m_, v_ = _adamw(_pack_small(wts), g_small_pk, _pack_small(ms), _pack_small(vs), "adamw_small")
    for dst, pk in ((deltas, d_), (new_m, m_), (new_v, v_)):
        dst.update(_unpack_small(pk, shapes))

    return (g_small_pk[SG_LOSS, 0], grad_x.reshape(1, s, D), *[grads[n] for n in names], *[deltas[n] for n in names],
            *[new_m[n] for n in names], *[new_v[n] for n in names])
```

```python
import math

import jax
import jax.numpy as jnp
from jax import lax
from jax.experimental import pallas as pl
from jax.experimental.pallas import tpu as pltpu

F32, BF16 = jnp.float32, jnp.bfloat16
MESH = pl.DeviceIdType.MESH

D = 1024
EPS = 1e-6
GLA_H, GLA_DK, GLA_DV, GLA_LR = 4, 128, 256, 16
GLA_TAU = 16.0
GLA_CHUNK = 64
GLA_BLOCK = 512
ATT_GROUPS = ((128, 1), (512, 4), (2048, 16))
ATT_BLK = 128
ATT_HD = 64
ATT_W = 768
D_FF = 2816
ROPE_THETA = 10000.0
P_W = 7680
P_GV, P_GR, P_MA, P_MB, P_GQ, P_GK, P_AQ, P_AK, P_AV, P_LR = 0, 1024, 2048, 3072, 4096, 4608, 5120, 5888, 6656, 7424
W_IN = 7440
W_IN_SH, W_UP_SH, W_DOWN_SH = 1860, 1408, 704
VMEM_LIMIT = 56 * 1024 * 1024
ADAM_LR, ADAM_B1, ADAM_B2, ADAM_EPS, ADAM_WD, ADAM_STEP = 0.001, 0.9, 0.999, 1e-08, 0.01, 10
NEG = -1e30


def _tile(n, target, unit=128):
    best = None
    for t in range(unit, min(n, target) + 1, unit):
        if n % t == 0:
            best = t
    return best or n


def _params(sem):
    return pltpu.CompilerParams(dimension_semantics=sem, vmem_limit_bytes=VMEM_LIMIT)


def _dg(a, b, ca, cb):
    return lax.dot_general(a, b, (((ca,), (cb,)), ((), ())), preferred_element_type=F32)


def _sigmoid(v):
    return 1.0 / (1.0 + jnp.exp(-v))


def _ff_block(j):
    return (j % 2) * 2 + j // 2


def _mm(a, b, name, *, ta=False, tb=False, out_dtype=BF16, tm=1024, tn=1536, tk=1024, n_outer=True, comm=(),
        b_shards=False, o_shards=False):
    m = a.shape[1] if ta else a.shape[0]
    k = a.shape[0] if ta else a.shape[1]
    if b_shards:
        n = b.shape[1] if tb else 4 * W_UP_SH
        tn, tk = (tn, W_UP_SH) if tb else (W_UP_SH, tk)
    else:
        n = b.shape[0] if tb else b.shape[1]
    if o_shards:
        tn = W_UP_SH
    tm, tn, tk = _tile(m, tm), _tile(n, tn), _tile(k, tk)
    nm, nn, nk = m // tm, n // tn, k // tk
    in_out = out_dtype == F32
    c_ins, c_outs, c_alias, c_scratch = _carry(comm, 2, 1)

    def body(a_ref, b_ref, *rest):
        ci, o_ref, co = rest[:len(c_ins)], rest[len(c_ins)], rest[len(c_ins) + 1:len(c_ins) + 1 + len(c_outs)]
        scr = rest[len(c_ins) + 1 + len(c_outs):]
        kk = pl.program_id(2)
        if comm:
            step = (pl.program_id(0) * (nm if n_outer else nn) + pl.program_id(1)) * nk + kk

            @pl.when(step == 0)
            def _():
                _comm_phase(comm, ci, co, scr[-2], scr[-1], True)

        _mm_step(a_ref, b_ref, o_ref, scr, kk)
        if comm:
            @pl.when(step == nm * nn * nk - 1)
            def _():
                _comm_phase(comm, ci, co, scr[-2], scr[-1], False)

    def _mm_step(a_ref, b_ref, o_ref, scr, kk):
        p = _dg(a_ref[...].astype(BF16), b_ref[...].astype(BF16), 0 if ta else 1, 1 if tb else 0)
        if nk == 1:
            o_ref[...] = p.astype(o_ref.dtype)
        else:
            acc = o_ref if in_out else scr[0]

            @pl.when(kk == 0)
            def _():
                acc[...] = p

            @pl.when(kk > 0)
            def _():
                acc[...] += p

            if not in_out:
                @pl.when(kk == nk - 1)
                def _():
                    o_ref[...] = acc[...].astype(o_ref.dtype)

    if n_outer:
        ij = lambda g0, g1: (g1, g0)
        grid = (nn, nm, nk)
    else:
        ij = lambda g0, g1: (g0, g1)
        grid = (nm, nn, nk)
    a_map = (lambda g0, g1, kk: (kk, ij(g0, g1)[0])) if ta else (lambda g0, g1, kk: (ij(g0, g1)[0], kk))
    if b_shards and tb:
        b_spec = pl.BlockSpec((None, tn, tk), lambda g0, g1, kk: (_ff_block(kk), ij(g0, g1)[1], 0))
    elif b_shards:
        b_spec = pl.BlockSpec((None, tk, tn), lambda g0, g1, kk: (_ff_block(ij(g0, g1)[1]), kk, 0))
    elif tb:
        b_spec = pl.BlockSpec((tn, tk), lambda g0, g1, kk: (ij(g0, g1)[1], kk))
    else:
        b_spec = pl.BlockSpec((tk, tn), lambda g0, g1, kk: (kk, ij(g0, g1)[1]))
    if o_shards:
        o_spec = pl.BlockSpec((None, tm, tn), lambda g0, g1, kk: (_ff_block(ij(g0, g1)[1]), ij(g0, g1)[0], 0))
        o_shape = jax.ShapeDtypeStruct((4, m, W_UP_SH), out_dtype)
    else:
        o_spec = pl.BlockSpec((tm, tn), lambda g0, g1, kk: ij(g0, g1))
        o_shape = jax.ShapeDtypeStruct((m, n), out_dtype)
    res = pl.pallas_call(
        body, name=name, grid=grid,
        in_specs=[pl.BlockSpec((tk, tm) if ta else (tm, tk), a_map), b_spec] + [HBM] * len(c_ins),
        out_specs=[o_spec] + [HBM] * len(c_outs),
        out_shape=[o_shape] + c_outs,
        scratch_shapes=([] if (in_out or nk == 1) else [pltpu.VMEM((tm, tn), F32)]) + c_scratch,
        input_output_aliases=c_alias,
        compiler_params=_params(("arbitrary",) * 3 if comm else ("parallel", "parallel", "arbitrary")),
    )(a, b, *c_ins)
    return (res[0], _split_units(comm, res[1:])) if comm else res[0]


def _rows(arr, rb, w=None, j=0):
    w = arr.shape[1] if w is None else w
    if callable(j):
        return arr, pl.BlockSpec((rb, w), lambda c, i: (i, j(c)))
    return arr, pl.BlockSpec((rb, w), lambda c, i: (i, j))


def _full(arr, w=None, j=0):
    w = arr.shape[1] if w is None else w
    if callable(j):
        return arr, pl.BlockSpec((arr.shape[0], w), lambda c, i: (0, j(c)))
    return arr, pl.BlockSpec((arr.shape[0], w), lambda c, i: (0, j))


def _halo(arr, rb, hb, w, j, before):
    per = rb // hb
    last = arr.shape[0] // hb - 1
    if before:
        rmap = lambda i: jnp.maximum(i * per - 1, 0)
    else:
        rmap = lambda i: jnp.minimum((i + 1) * per, last)
    return arr, pl.BlockSpec((hb, w), lambda c, i: (rmap(i), j(c) if callable(j) else j))


def _rowcall(fn, ins, outs, *, n_rows, rb, name, ncol=1, into=None, after=()):
    n_in = len(ins)
    nr = n_rows // rb
    unread = ([] if into is None else [into[0]]) + list(after)
    n_skip = len(unread)

    def body(*refs):
        c, i = pl.program_id(0), pl.program_id(1)
        res = fn(c, i, *[r[...] for r in refs[:n_in]])
        for val, spec, o_ref in zip(res, outs, refs[n_in + n_skip:]):
            if spec[2] == "row":
                o_ref[...] = val.astype(o_ref.dtype)
            else:
                @pl.when(i == 0)
                def _(o_ref=o_ref, val=val):
                    o_ref[...] = val.astype(o_ref.dtype)

                @pl.when(i > 0)
                def _(o_ref=o_ref, val=val):
                    o_ref[...] += val.astype(o_ref.dtype)

    out_specs = []
    for shape, dt, kind, block, col in outs:
        if kind == "row":
            out_specs.append(pl.BlockSpec(block, lambda c, i, col=col: (i, col(c))))
        else:
            out_specs.append(pl.BlockSpec(block, lambda c, i, col=col: (0, col(c))))
    return pl.pallas_call(
        body, name=name, grid=(ncol, nr),
        in_specs=[s for _, s in ins] + [pl.BlockSpec(memory_space=pl.ANY)] * n_skip, out_specs=out_specs,
        out_shape=[jax.ShapeDtypeStruct(o[0], o[1]) for o in outs],
        input_output_aliases={} if into is None else {n_in: into[1]},
        compiler_params=_params(("parallel", "arbitrary")),
    )(*[a for a, _ in ins], *unread)


def _orow(n_rows, w, dt, rb, bw=None, col=lambda c: 0):
    return ((n_rows, w), dt, "row", (rb, bw or w), col)


def _oacc(r, w, bw=None, col=lambda c: 0):
    return ((r, w), F32, "acc", (r, bw or w), col)


def _csum(v):
    return jnp.sum(v, axis=0, keepdims=True)


def _rms(v):
    return lax.rsqrt(jnp.mean(v * v, axis=-1, keepdims=True) + EPS)


def _norm_bwd(xv, dh, w, scale):
    r = _rms(xv)
    xh = xv * r
    dxh = dh * (w * (1.0 + scale))
    dx = r * (dxh - xh * jnp.mean(dxh * xh, axis=-1, keepdims=True))
    t = dh * xh
    return dx, _csum(dh), _csum(t * w), _csum(t * (1.0 + scale))


def _rope_tables(pos_col, invf, s):
    def fn(c, i, pos, f):
        ang = pos.astype(F32) * f
        lane = lax.broadcasted_iota(jnp.int32, ang.shape, 1)
        sign = jnp.where((lane % ATT_HD) < ATT_HD // 2, -1.0, 1.0)
        return jnp.cos(ang), jnp.sin(ang) * sign

    rb = 512
    return _rowcall(fn, [_rows(pos_col, rb), _full(invf)], [_orow(s, 128, F32, rb), _orow(s, 128, F32, rb)],
                    n_rows=s, rb=rb, name="rope_tables")


def _swap_halves(t):
    n = t.shape[1]
    lane = lax.broadcasted_iota(jnp.int32, t.shape, 1)
    return jnp.where((lane % ATT_HD) < ATT_HD // 2, pltpu.roll(t, n - 32, 1), pltpu.roll(t, 32, 1))


def _rope_apply(t, cos, sin_signed, inverse):
    cw = jnp.concatenate([cos] * (t.shape[1] // 128), axis=1)
    sw = jnp.concatenate([sin_signed] * (t.shape[1] // 128), axis=1)
    if inverse:
        sw = -sw
    return t * cw + _swap_halves(t) * sw


DIL_ROWS = 512


def _to_dilated(scr, val, out_ref, r):
    if r == 1:
        out_ref[...] = val.astype(out_ref.dtype)
        return
    n = val.shape[0] // r
    for hh in range(2):
        scr[hh] = val[:, hh * 128:(hh + 1) * 128]
        for pr in range(r):
            out_ref[:, pr * 256 + hh * 128:pr * 256 + (hh + 1) * 128] = scr[hh, pl.ds(pr, n, stride=r), :].astype(out_ref.dtype)


def _from_dilated(scr, in_ref, r):
    if r == 1:
        return in_ref[...].astype(F32)
    n = in_ref.shape[0]
    for hh in range(2):
        for pr in range(r):
            scr[hh, pl.ds(pr, n, stride=r), :] = in_ref[:, pr * 256 + hh * 128:pr * 256 + (hh + 1) * 128].astype(F32)
    return jnp.concatenate([scr[0], scr[1]], axis=1)


def _dil_spec(r):
    return pl.BlockSpec((DIL_ROWS // r, r * 256), lambda i: (i, 0))


def _dil_shape(s, r, dt):
    return jax.ShapeDtypeStruct((s // r, r * 256), dt)


_DIL_SCRATCH = [pltpu.VMEM((2, DIL_ROWS, 128), F32)]
_RS = tuple(r for _, r in ATT_GROUPS)


def _rope_fwd(p, cos_t, sin_t, s):
    def body(*refs):
        ins, cs, sn, outs, scr = refs[:9], refs[9][...], refs[10][...], refs[11:20], refs[20]
        for t in range(3):
            for g, r in enumerate(_RS):
                val = ins[3 * t + g][...].astype(F32)
                _to_dilated(scr, _rope_apply(val, cs, sn, False) if t < 2 else val, outs[3 * t + g], r)

    res = pl.pallas_call(
        body, name="rope", grid=(s // DIL_ROWS,),
        in_specs=[pl.BlockSpec((DIL_ROWS, 256), lambda i, c=base // 256 + g: (i, c)) for base in (P_AQ, P_AK, P_AV) for g in range(3)]
        + [pl.BlockSpec((DIL_ROWS, 128), lambda i: (i, 0))] * 2,
        out_specs=[_dil_spec(r) for _ in range(3) for r in _RS],
        out_shape=[_dil_shape(s, r, BF16) for _ in range(3) for r in _RS],
        scratch_shapes=_DIL_SCRATCH, compiler_params=_params(("parallel",)),
    )(*([p] * 9), cos_t, sin_t)
    return res[0:3], res[3:6], res[6:9]


def _attn_combine(att, s):
    def body(o0, o1, o2, l0, l1, l2, o_ref, lse_ref, od1, od2, ld1, ld2, scr):
        ov = [_from_dilated(scr, ref, r) for ref, r in zip((o0, o1, o2), _RS)]
        lv = [_from_dilated(scr, ref, r) for ref, r in zip((l0, l1, l2), _RS)]
        mx = jnp.maximum(jnp.maximum(lv[0], lv[1]), lv[2])
        ev = [jnp.exp(l - mx) for l in lv]
        z = ev[0] + ev[1] + ev[2]
        o = ((ev[0] * ov[0] + ev[1] * ov[1] + ev[2] * ov[2]) / z).astype(BF16)
        lse = mx + jnp.log(z)
        o_ref[...] = o
        lse_ref[...] = lse
        for ref, r in zip((od1, od2), _RS[1:]):
            _to_dilated(scr, o.astype(F32), ref, r)
        for ref, r in zip((ld1, ld2), _RS[1:]):
            _to_dilated(scr, lse, ref, r)

    return pl.pallas_call(
        body, name="attn_combine", grid=(s // DIL_ROWS,),
        in_specs=[_dil_spec(r) for r in _RS] * 2,
        out_specs=[_dil_spec(1)] * 2 + [_dil_spec(r) for r in _RS[1:]] * 2,
        out_shape=[_dil_shape(s, 1, BF16), _dil_shape(s, 1, F32)] + [_dil_shape(s, r, BF16) for r in _RS[1:]]
        + [_dil_shape(s, r, F32) for r in _RS[1:]],
        scratch_shapes=_DIL_SCRATCH, compiler_params=_params(("parallel",)),
    )(*[a[0] for a in att], *[a[1] for a in att])


def _dilate(t, s):
    def body(t_ref, o1, o2, scr):
        val = t_ref[...].astype(F32)
        for ref, r in zip((o1, o2), _RS[1:]):
            _to_dilated(scr, val, ref, r)

    return pl.pallas_call(
        body, name="attn_dilate", grid=(s // DIL_ROWS,), in_specs=[_dil_spec(1)], out_specs=[_dil_spec(r) for r in _RS[1:]],
        out_shape=[_dil_shape(s, r, t.dtype) for r in _RS[1:]], scratch_shapes=_DIL_SCRATCH, compiler_params=_params(("parallel",)),
    )(t)


def _rope_bwd(datt, d_glr, dp, cos_t, sin_t, s):
    tail = P_W - P_AQ

    def body(*refs):
        ins, cs, sn, glr, o_ref, scr = refs[:9], refs[9][...], refs[10][...], refs[11], refs[13], refs[14]
        for t in range(3):
            for g, r in enumerate(_RS):
                val = _from_dilated(scr, ins[3 * t + g], r)
                o_ref[:, t * ATT_W + g * 256:t * ATT_W + (g + 1) * 256] = (_rope_apply(val, cs, sn, True) if t < 2 else val).astype(BF16)
        o_ref[:, 3 * ATT_W:3 * ATT_W + 128] = glr[...]
        o_ref[:, 3 * ATT_W + 128:] = jnp.zeros((DIL_ROWS, tail - 3 * ATT_W - 128), BF16)

    return pl.pallas_call(
        body, name="rope_bwd", grid=(s // DIL_ROWS,),
        in_specs=[_dil_spec(r) for _ in range(3) for r in _RS] + [pl.BlockSpec((DIL_ROWS, 128), lambda i: (i, 0))] * 3
        + [pl.BlockSpec(memory_space=pl.ANY)],
        out_specs=pl.BlockSpec((DIL_ROWS, tail), lambda i: (i, P_AQ // tail)),
        out_shape=jax.ShapeDtypeStruct((s, P_W), BF16), input_output_aliases={12: 0},
        scratch_shapes=_DIL_SCRATCH, compiler_params=_params(("parallel",)),
    )(*[datt[g][t] for t in range(3) for g in range(3)], cos_t, sin_t, d_glr, dp)


def _tri_dot(tri, t):
    tb = tri.astype(BF16)
    hi = t.astype(BF16)
    r1 = t - hi.astype(F32)
    mid = r1.astype(BF16)
    lo = (r1 - mid.astype(F32)).astype(BF16)
    return _dg(tb, hi, 1, 0) + _dg(tb, mid, 1, 0) + _dg(tb, lo, 1, 0)


def _gla_decays(la_c, tri):
    b = _tri_dot(tri, la_c)
    row = lax.broadcasted_iota(jnp.int32, b.shape, 0)
    bmid = jnp.sum(jnp.where(row == GLA_CHUNK // 2 - 1, b, 0.0), axis=0, keepdims=True)
    blast = jnp.sum(jnp.where(row == GLA_CHUNK - 1, b, 0.0), axis=0, keepdims=True)
    return b, bmid, blast


def _gla_fwd(p, la, s, comm=()):
    tb, ch = GLA_BLOCK, GLA_CHUNK
    nb, nc = s // tb, tb // ch
    scale = GLA_DK ** -0.5
    c_ins, c_outs, c_alias, c_scratch = _carry(comm, 4, 2)

    def body(q_ref, k_ref, v_ref, la_ref, *rest):
        ci, (o_ref, st_ref) = rest[:len(c_ins)], rest[len(c_ins):len(c_ins) + 2]
        co, state = rest[len(c_ins) + 2:len(c_ins) + 2 + len(c_outs)], rest[len(c_ins) + 2 + len(c_outs)]
        step = pl.program_id(0)
        if comm:
            @pl.when(step == 0)
            def _():
                _comm_phase(comm, ci, co, rest[-2], rest[-1], True)

        _gla_fwd_step(q_ref, k_ref, v_ref, la_ref, o_ref, st_ref, state)
        if comm:
            @pl.when(step == nb - 1)
            def _():
                _comm_phase(comm, ci, co, rest[-2], rest[-1], False)

    def _gla_fwd_step(q_ref, k_ref, v_ref, la_ref, o_ref, st_ref, state):
        @pl.when(pl.program_id(0) == 0)
        def _():
            state[...] = jnp.zeros_like(state)

        ri = lax.broadcasted_iota(jnp.int32, (ch, ch), 0)
        ci = lax.broadcasted_iota(jnp.int32, (ch, ch), 1)
        causal = ci <= ri
        tri = causal.astype(F32)

        def chunk(c, carry):
            sl = pl.ds(pl.multiple_of(c * ch, ch), ch)
            b, bmid, blast = _gla_decays(la_ref[sl, :], tri)
            q = q_ref[sl, :].astype(F32) * scale
            k = k_ref[sl, :].astype(F32)
            v = v_ref[sl, :]
            qgt = (q * jnp.exp(b)).astype(BF16)
            qgn = (q * jnp.exp(b - bmid)).astype(BF16)
            kgn = (k * jnp.exp(bmid - b)).astype(BF16)
            kd = (k * jnp.exp(blast - b)).astype(BF16)
            dec = jnp.exp(blast)
            sts = [state[h] for h in range(GLA_H)]
            outs, news = [], []
            for h in range(GLA_H):
                hk, hv = slice(h * GLA_DK, (h + 1) * GLA_DK), slice(h * GLA_DV, (h + 1) * GLA_DV)
                a = jnp.where(causal, _dg(qgn[:, hk], kgn[:, hk], 1, 1), 0.0)
                outs.append(_dg(a.astype(BF16), v[:, hv], 1, 0) + _dg(qgt[:, hk], sts[h].astype(BF16), 1, 1))
                news.append(dec[:, hk] * sts[h] + _dg(v[:, hv], kd[:, hk], 0, 0))
            for h in range(GLA_H):
                st_ref[h, c] = sts[h]
                state[h] = news[h]
            o_ref[sl, :] = jnp.concatenate(outs, axis=1)
            return carry

        lax.fori_loop(0, nc, chunk, 0)

    hw = GLA_H * GLA_DK
    res = pl.pallas_call(
        body, name="gla_fwd", grid=(nb,),
        in_specs=[pl.BlockSpec((tb, hw), lambda t: (t, P_GQ // hw)),
                  pl.BlockSpec((tb, hw), lambda t: (t, P_GK // hw)),
                  pl.BlockSpec((tb, GLA_H * GLA_DV), lambda t: (t, P_GV // (GLA_H * GLA_DV))),
                  pl.BlockSpec((tb, hw), lambda t: (t, 0))] + [HBM] * len(c_ins),
        out_specs=[pl.BlockSpec((tb, GLA_H * GLA_DV), lambda t: (t, 0)),
                   pl.BlockSpec((GLA_H, nc, GLA_DV, GLA_DK), lambda t: (0, t, 0, 0))] + [HBM] * len(c_outs),
        out_shape=[jax.ShapeDtypeStruct((s, GLA_H * GLA_DV), F32),
                   jax.ShapeDtypeStruct((GLA_H, s // ch, GLA_DV, GLA_DK), F32)] + c_outs,
        scratch_shapes=[pltpu.VMEM((GLA_H, GLA_DV, GLA_DK), F32)] + c_scratch,
        input_output_aliases=c_alias,
        compiler_params=_params(("arbitrary",)),
    )(p, p, p, la, *c_ins)
    return res[0], res[1], _split_units(comm, res[2:])


def _gla_bwd(p, la, states, do, s, dp, comm=()):
    tb, ch = GLA_BLOCK, GLA_CHUNK
    nb, nc = s // tb, tb // ch
    scale = GLA_DK ** -0.5
    c_ins, c_outs, c_alias, c_scratch = _carry(comm, 7, 4)

    def body(q_ref, k_ref, v_ref, la_ref, st_ref, do_ref, dp_in, *rest):
        ci, outs = rest[:len(c_ins)], rest[len(c_ins):len(c_ins) + 4]
        co, dstate = rest[len(c_ins) + 4:len(c_ins) + 4 + len(c_outs)], rest[len(c_ins) + 4 + len(c_outs)]
        step = pl.program_id(0)
        if comm:
            @pl.when(step == 0)
            def _():
                _comm_phase(comm, ci, co, rest[-2], rest[-1], True)

        _gla_bwd_step(q_ref, k_ref, v_ref, la_ref, st_ref, do_ref, *outs, dstate)
        if comm:
            @pl.when(step == nb - 1)
            def _():
                _comm_phase(comm, ci, co, rest[-2], rest[-1], False)

    def _gla_bwd_step(q_ref, k_ref, v_ref, la_ref, st_ref, do_ref, dq_ref, dk_ref, dv_ref, dla_ref, dstate):
        @pl.when(pl.program_id(0) == 0)
        def _():
            dstate[...] = jnp.zeros_like(dstate)

        ri = lax.broadcasted_iota(jnp.int32, (ch, ch), 0)
        ci = lax.broadcasted_iota(jnp.int32, (ch, ch), 1)
        causal = ci <= ri
        tri = causal.astype(F32)
        tri_t = (ci >= ri).astype(F32)

        def chunk(cc, carry):
            c = nc - 1 - cc
            sl = pl.ds(pl.multiple_of(c * ch, ch), ch)
            b, bmid, blast = _gla_decays(la_ref[sl, :], tri)
            q = q_ref[sl, :].astype(F32) * scale
            k = k_ref[sl, :].astype(F32)
            v = v_ref[sl, :]
            e_b, e_qn, e_kn, e_kd = jnp.exp(b), jnp.exp(b - bmid), jnp.exp(bmid - b), jnp.exp(blast - b)
            dec = jnp.exp(blast)
            qgt, qgn, kgn, kd = q * e_b, q * e_qn, k * e_kn, k * e_kd
            qgt_b, qgn_b, kgn_b, kd_b = qgt.astype(BF16), qgn.astype(BF16), kgn.astype(BF16), kd.astype(BF16)
            do_b = do_ref[sl, :].astype(BF16)
            st0s = [st_ref[h, c] for h in range(GLA_H)]
            dsts = [dstate[h] for h in range(GLA_H)]
            dqgn, dqgt, dkgn, dkd, dvs, ddec, news = [], [], [], [], [], [], []
            for h in range(GLA_H):
                hk, hv = slice(h * GLA_DK, (h + 1) * GLA_DK), slice(h * GLA_DV, (h + 1) * GLA_DV)
                dst_b = dsts[h].astype(BF16)
                a = jnp.where(causal, _dg(qgn_b[:, hk], kgn_b[:, hk], 1, 1), 0.0).astype(BF16)
                da = jnp.where(causal, _dg(do_b[:, hv], v[:, hv], 1, 1), 0.0).astype(BF16)
                dqgn.append(_dg(da, kgn_b[:, hk], 1, 0))
                dqgt.append(_dg(do_b[:, hv], st0s[h].astype(BF16), 1, 0))
                dkgn.append(_dg(da, qgn_b[:, hk], 0, 0))
                dvs.append(_dg(a, do_b[:, hv], 0, 0) + _dg(kd_b[:, hk], dst_b, 1, 1))
                dkd.append(_dg(v[:, hv], dst_b, 1, 0))
                ddec.append(jnp.sum(st0s[h] * dsts[h], axis=0, keepdims=True))
                news.append(dec[:, hk] * dsts[h] + _dg(do_b[:, hv], qgt_b[:, hk], 0, 0))
            for h in range(GLA_H):
                dstate[h] = news[h]
            cat = lambda parts: jnp.concatenate(parts, axis=1)
            dqgn, dqgt, dkgn, dkd, ddec = cat(dqgn), cat(dqgt), cat(dkgn), cat(dkd), cat(ddec)
            dq_ref[sl, :] = (scale * (dqgn * e_qn + dqgt * e_b)).astype(dq_ref.dtype)
            dk_ref[sl, :] = (dkgn * e_kn + dkd * e_kd).astype(dk_ref.dtype)
            dv_ref[sl, :] = cat(dvs).astype(dv_ref.dtype)
            db = dqgn * qgn + dqgt * qgt - dkgn * kgn - dkd * kd
            extra = jnp.sum(dkd * kd, axis=0, keepdims=True) + ddec * dec
            dla_ref[sl, :] = _tri_dot(tri_t, db) + extra
            return carry

        lax.fori_loop(0, nc, chunk, 0)

    rev = lambda t: nb - 1 - t
    hw, vw = GLA_H * GLA_DK, GLA_H * GLA_DV
    res = pl.pallas_call(
        body, name="gla_bwd", grid=(nb,),
        in_specs=[pl.BlockSpec((tb, hw), lambda t: (rev(t), P_GQ // hw)),
                  pl.BlockSpec((tb, hw), lambda t: (rev(t), P_GK // hw)),
                  pl.BlockSpec((tb, vw), lambda t: (rev(t), P_GV // vw)),
                  pl.BlockSpec((tb, hw), lambda t: (rev(t), 0)),
                  pl.BlockSpec((GLA_H, nc, GLA_DV, GLA_DK), lambda t: (0, rev(t), 0, 0)),
                  pl.BlockSpec((tb, vw), lambda t: (rev(t), 0)), pl.BlockSpec(memory_space=pl.ANY)] + [HBM] * len(c_ins),
        out_specs=[pl.BlockSpec((tb, hw), lambda t: (rev(t), 0)),
                   pl.BlockSpec((tb, hw), lambda t: (rev(t), 0)),
                   pl.BlockSpec((tb, vw), lambda t: (rev(t), P_GV // vw)),
                   pl.BlockSpec((tb, hw), lambda t: (rev(t), 0))] + [HBM] * len(c_outs),
        out_shape=[jax.ShapeDtypeStruct((s, hw), BF16),
                   jax.ShapeDtypeStruct((s, hw), BF16),
                   jax.ShapeDtypeStruct((s, P_W), BF16),
                   jax.ShapeDtypeStruct((s, hw), F32)] + c_outs,
        scratch_shapes=[pltpu.VMEM((GLA_H, GLA_DV, GLA_DK), F32)] + c_scratch,
        input_output_aliases={6: 2, **c_alias},
        compiler_params=_params(("arbitrary",)),
    )(p, p, p, la, states, do, dp, *c_ins)
    return res[0], res[1], res[2], res[3], _split_units(comm, res[4:])


def _head_masks():
    lane = lax.broadcasted_iota(jnp.int32, (1, 4 * ATT_HD), 1)
    return [(lane >= h * ATT_HD) & (lane < (h + 1) * ATT_HD) for h in range(4)]


def _attn_fwd(qv, kv, pv, g, r, s):
    ln = s // r
    nblk = ln // ATT_BLK
    qcol = lambda pr: pr
    vcol = qcol
    prev = lambda n: jnp.maximum(n - 1, 0)

    def body(q_ref, kp_ref, kc_ref, vp_ref, vc_ref, o_ref, lse_ref):
        has_prev = pl.program_id(1) > 0
        ri = lax.broadcasted_iota(jnp.int32, (ATT_BLK, ATT_BLK), 0)
        ci = lax.broadcasted_iota(jnp.int32, (ATT_BLK, ATT_BLK), 1)
        m_cur = ci <= ri
        m_prev = (ci >= ri) & has_prev
        q, kp, kc, vp, vc = q_ref[...], kp_ref[...], kc_ref[...], vp_ref[...], vc_ref[...]
        o = jnp.zeros((ATT_BLK, 256), F32)
        lse = jnp.zeros((ATT_BLK, 256), F32)
        for hm in _head_masks():
            qm = jnp.where(hm, q, jnp.zeros_like(q))
            sc = jnp.where(m_cur, _dg(qm, kc, 1, 1) * 0.125, NEG)
            sp = jnp.where(m_prev, _dg(qm, kp, 1, 1) * 0.125, NEG)
            mx = jnp.maximum(jnp.max(sc, axis=1, keepdims=True), jnp.max(sp, axis=1, keepdims=True))
            pc, pp = jnp.exp(sc - mx), jnp.exp(sp - mx)
            den = jnp.sum(pc, axis=1, keepdims=True) + jnp.sum(pp, axis=1, keepdims=True)
            oh = (_dg(pc.astype(BF16), vc, 1, 0) + _dg(pp.astype(BF16), vp, 1, 0)) / den
            o = jnp.where(hm, oh, o)
            lse = jnp.where(hm, mx + jnp.log(den), lse)
        o_ref[...] = o.astype(o_ref.dtype)
        lse_ref[...] = lse

    blk = (ATT_BLK, 256)
    o, lse = pl.pallas_call(
        body, name=f"attn_fwd_{g}", grid=(r, nblk),
        in_specs=[pl.BlockSpec(blk, lambda pr, n: (n, qcol(pr))),
                  pl.BlockSpec(blk, lambda pr, n: (prev(n), qcol(pr))),
                  pl.BlockSpec(blk, lambda pr, n: (n, qcol(pr))),
                  pl.BlockSpec(blk, lambda pr, n: (prev(n), vcol(pr))),
                  pl.BlockSpec(blk, lambda pr, n: (n, vcol(pr)))],
        out_specs=[pl.BlockSpec(blk, lambda pr, n: (n, pr)), pl.BlockSpec(blk, lambda pr, n: (n, pr))],
        out_shape=[jax.ShapeDtypeStruct((ln, r * 256), BF16), jax.ShapeDtypeStruct((ln, r * 256), F32)],
        compiler_params=_params(("parallel", "parallel")),
    )(qv, kv, kv, pv, pv)
    return o, lse


def _attn_bwd(qv, kv, pv, dov, ov, lv, g, r, s):
    ln = s // r
    nblk = ln // ATT_BLK
    qcol = lambda pr: pr
    vcol = qcol
    prev = lambda n: jnp.maximum(n - 1, 0)
    nxt = lambda n: jnp.minimum(n + 1, nblk - 1)

    def body(qc_ref, qn_ref, kp_ref, kc_ref, vp_ref, vc_ref, doc_ref, don_ref, oc_ref, on_ref, lc_ref, ln_ref,
             dq_ref, dk_ref, dv_ref):
        n = pl.program_id(1)
        has_prev, has_next = n > 0, n < nblk - 1
        ri = lax.broadcasted_iota(jnp.int32, (ATT_BLK, ATT_BLK), 0)
        ci = lax.broadcasted_iota(jnp.int32, (ATT_BLK, ATT_BLK), 1)
        m_cur = ci <= ri
        m_prev = (ci >= ri) & has_prev
        m_next = (ci >= ri) & has_next
        qc, qn, kp, kc, vp, vc = qc_ref[...], qn_ref[...], kp_ref[...], kc_ref[...], vp_ref[...], vc_ref[...]
        doc, don = doc_ref[...], don_ref[...]
        pc_full = doc.astype(F32) * oc_ref[...].astype(F32)
        pn_full = don.astype(F32) * on_ref[...].astype(F32)
        lc, lnx = lc_ref[...], ln_ref[...]
        dq = jnp.zeros((ATT_BLK, 256), F32)
        dk = jnp.zeros((ATT_BLK, 256), F32)
        dv = jnp.zeros((ATT_BLK, 256), F32)
        zb = jnp.zeros_like(qc)
        for hm in _head_masks():
            qcm, qnm = jnp.where(hm, qc, zb), jnp.where(hm, qn, zb)
            docm, donm = jnp.where(hm, doc, zb), jnp.where(hm, don, zb)
            lse_c = jnp.max(jnp.where(hm, lc, NEG), axis=1, keepdims=True)
            lse_n = jnp.max(jnp.where(hm, lnx, NEG), axis=1, keepdims=True)
            del_c = jnp.sum(jnp.where(hm, pc_full, 0.0), axis=1, keepdims=True)
            del_n = jnp.sum(jnp.where(hm, pn_full, 0.0), axis=1, keepdims=True)
            pr_ = jnp.where(m_cur, jnp.exp(_dg(qcm, kc, 1, 1) * 0.125 - lse_c), 0.0)
            ds = (pr_ * (_dg(docm, vc, 1, 1) - del_c) * 0.125).astype(BF16)
            dqh = _dg(ds, kc, 1, 0)
            dkh = _dg(ds, qc, 0, 0)
            dvh = _dg(pr_.astype(BF16), doc, 0, 0)
            pr_ = jnp.where(m_prev, jnp.exp(_dg(qcm, kp, 1, 1) * 0.125 - lse_c), 0.0)
            ds = (pr_ * (_dg(docm, vp, 1, 1) - del_c) * 0.125).astype(BF16)
            dqh = dqh + _dg(ds, kp, 1, 0)
            pr_ = jnp.where(m_next, jnp.exp(_dg(qnm, kc, 1, 1) * 0.125 - lse_n), 0.0)
            ds = (pr_ * (_dg(donm, vc, 1, 1) - del_n) * 0.125).astype(BF16)
            dkh = dkh + _dg(ds, qn, 0, 0)
            dvh = dvh + _dg(pr_.astype(BF16), don, 0, 0)
            dq = jnp.where(hm, dqh, dq)
            dk = jnp.where(hm, dkh, dk)
            dv = jnp.where(hm, dvh, dv)
        dq_ref[...] = dq.astype(dq_ref.dtype)
        dk_ref[...] = dk.astype(dk_ref.dtype)
        dv_ref[...] = dv.astype(dv_ref.dtype)

    blk = (ATT_BLK, 256)
    cur = lambda col: pl.BlockSpec(blk, lambda pr, n: (n, col(pr)))
    prv = lambda col: pl.BlockSpec(blk, lambda pr, n: (prev(n), col(pr)))
    nx = lambda col: pl.BlockSpec(blk, lambda pr, n: (nxt(n), col(pr)))
    own = lambda pr: pr
    outs = pl.pallas_call(
        body, name=f"attn_bwd_{g}", grid=(r, nblk),
        in_specs=[cur(qcol), nx(qcol), prv(qcol), cur(qcol), prv(vcol), cur(vcol),
                  cur(own), nx(own), cur(own), nx(own), cur(own), nx(own)],
        out_specs=[cur(own), cur(own), cur(own)],
        out_shape=[jax.ShapeDtypeStruct((ln, r * 256), BF16)] * 3,
        compiler_params=_params(("parallel", "parallel")),
    )(qv, qv, kv, kv, pv, pv, dov, dov, ov, ov, lv, lv)
    return outs


def _gelu_parts(gv):
    cdf = 0.5 * (1.0 + lax.erf(gv * (2.0 ** -0.5)))
    pdf = jnp.exp(-0.5 * gv * gv) * (1.0 / math.sqrt(2.0 * math.pi))
    return cdf, pdf


def _pick_row(t, k):
    row = lax.broadcasted_iota(jnp.int32, t.shape, 0)
    return jnp.sum(jnp.where(row == k, t, 0.0), axis=0, keepdims=True)


def _shift_rows(u, halo, n):
    row = lax.broadcasted_iota(jnp.int32, u.shape, 0)
    out = pltpu.roll(u, n, 0)
    for k in range(n):
        out = jnp.where(row == k, _pick_row(halo, 16 - n + k), out)
    return out


def _shift_rows_up(u, halo, n):
    rb = u.shape[0]
    row = lax.broadcasted_iota(jnp.int32, u.shape, 0)
    out = pltpu.roll(u, rb - n, 0)
    for k in range(n):
        out = jnp.where(row == rb - n + k, _pick_row(halo, k), out)
    return out


def _conv(u, halo, cw, cb):
    return cb + _pick_row(cw, 0) * _shift_rows(u, halo, 2) + _pick_row(cw, 1) * _shift_rows(u, halo, 1) + _pick_row(cw, 2) * u


def _local_step(x, mod, pos_col, target, sm, w_sh, chip, core):
    s = x.shape[0]
    shift1, scale1, gate1, shift2, scale2, gate2 = [mod[i:i + 1, :] for i in range(6)]
    rb = 256
    chip1 = chip.reshape(1)

    def f_norm1(c, i, xv, nw, sc, sh):
        return ((xv * _rms(xv) * nw) * (1.0 + sc) + sh,)

    (h,) = _rowcall(f_norm1, [_rows(x, rb), _full(sm["n1w"]), _full(scale1), _full(shift1)],
                    [_orow(s, D, BF16, rb)], n_rows=s, rb=rb, name="norm1")
    own = lambda got, i: lax.dynamic_update_slice(got, w_sh[i], (chip, 0, 0))
    [got0] = _comm_call("gather_w_in_ici", [_u_gather_ici(w_sh, (0,))])
    [got0] = _comm_call("gather_w_in_d2d", [_u_gather_d2d(got0, (0,))])
    w = dict(win=_win_assemble(own(got0[0], 0)))
    p, [got123, got4] = _mm(h, w["win"], "in_proj", tm=1024, tn=1536,
                            comm=[_u_gather_ici(w_sh, (1, 2, 3)), _u_gather_ici(w_sh, (4,))])

    def f_gla_pre(c, i, glr, w2, gb):
        z = _dg(glr, w2.astype(BF16), 1, 0) + gb
        return ((jnp.minimum(z, 0.0) - jnp.log(1.0 + jnp.exp(-jnp.abs(z)))) * (1.0 / GLA_TAU),)

    (la,) = _rowcall(f_gla_pre, [_rows(p, rb, 128, P_LR // 128), _full(sm["w2"]), _full(sm["gb"])],
                     [_orow(s, 512, F32, rb)], n_rows=s, rb=rb, name="gla_pre")
    o_gla, states, [got123, got5] = _gla_fwd(p, la, s, comm=[_u_gather_d2d(got123, (1, 2, 3)), _u_gather_ici(w_sh, (5,))])
    got45 = got4 + got5
    w.update(wgb=own(got123[0], 1).reshape(1024, D), wab=_cols_join(own(got123[1], 2)), wout=own(got123[2], 3).reshape(D, D))

    def f_gla_post(c, i, ov, gnw, gr):
        on = jnp.concatenate([ov[:, k * 256:(k + 1) * 256] * _rms(ov[:, k * 256:(k + 1) * 256]) * gnw
                              for k in range(GLA_H)], axis=1)
        g = gr.astype(F32)
        return (on * (g * _sigmoid(g)),)

    (og,) = _rowcall(f_gla_post, [_rows(o_gla, rb), _full(sm["gnw"]), _rows(p, rb, 1024, P_GR // 1024)],
                     [_orow(s, 1024, BF16, rb)], n_rows=s, rb=rb, name="gla_post")
    y_gla = _mm(og, w["wgb"], "gla_branch")

    invf = jnp.tile(ROPE_THETA ** (-jnp.arange(ATT_HD // 2, dtype=F32) / (ATT_HD // 2)), 4).reshape(1, 128)
    cos_t, sin_t = _rope_tables(pos_col, invf, s)

    q_d, k_d, v_d = _rope_fwd(p, cos_t, sin_t, s)
    att = [_attn_fwd(q_d[g], k_d[g], v_d[g], g, r, s) for g, r in enumerate(_RS)]
    o_att, lse, o_d1, o_d2, lse_d1, lse_d2 = _attn_combine(att, s)
    y_att = _mm(o_att, w["wab"], "attn_branch")

    def f_merge(c, i, ma, mb, yg, ya):
        return (_sigmoid(ma.astype(F32)) * yg.astype(F32) + _sigmoid(mb.astype(F32)) * ya.astype(F32),)

    (mixed,) = _rowcall(f_merge, [_rows(p, rb, D, P_MA // D), _rows(p, rb, D, P_MB // D), _rows(y_gla, rb), _rows(y_att, rb)],
                        [_orow(s, D, BF16, rb)], n_rows=s, rb=rb, name="merge")
    z1, [got45] = _mm(mixed, w["wout"], "out_proj", comm=[_u_gather_d2d(got45, (4, 5))])
    w.update(wup=own(got45[0], 4), wdown=own(got45[1], 5).reshape(D_FF, D))

    def f_norm2(c, i, xv, z, g1, nw, sc, sh):
        x1 = xv + g1 * z.astype(F32)
        return (x1, (x1 * _rms(x1) * nw) * (1.0 + sc) + sh)

    x1, h2 = _rowcall(f_norm2, [_rows(x, rb), _rows(z1, rb), _full(gate1), _full(sm["n2w"]), _full(scale2), _full(shift2)],
                      [_orow(s, D, F32, rb), _orow(s, D, BF16, rb)], n_rows=s, rb=rb, name="norm2")
    u = _mm(h2, w["wup"], "up_proj", b_shards=True)

    cwid = 2 * W_UP_SH

    def f_ffn(c, i, uv, hl, cw, cb):
        uc = _conv(uv.astype(F32), hl.astype(F32) * (i > 0).astype(F32), cw, cb)
        val, gt = uc[:, :W_UP_SH], uc[:, W_UP_SH:]
        cdf, _ = _gelu_parts(gt)
        return (gt * cdf * val,)

    ccol = lambda c: c
    (hidden,) = _rowcall(f_ffn, [_rows(u, rb, cwid, ccol), _halo(u, rb, 16, cwid, ccol, True),
                                 _full(sm["cw"], cwid, ccol), _full(sm["cb"], cwid, ccol)],
                         [_orow(s, D_FF, BF16, rb, W_UP_SH, ccol)], n_rows=s, rb=rb, name="conv_geglu", ncol=2)
    z2 = _mm(hidden, w["wdown"], "down_proj", tk=D_FF)

    def f_final(c, i, x1v, z, g2, fw, tgt):
        x2 = x1v + g2 * z.astype(F32)
        r = _rms(x2)
        xh = x2 * r
        e = xh * fw - tgt
        loss = 0.5 * jnp.sum(jnp.mean(e * e, axis=-1, keepdims=True), axis=0, keepdims=True)
        dy = e * (1.0 / D)
        dxh = dy * fw
        dx2 = r * (dxh - xh * jnp.mean(dxh * xh, axis=-1, keepdims=True))
        return (loss, dx2, dx2 * g2, _csum(dy * xh), _csum(dx2 * z.astype(F32)))

    loss, dx2, dz2, d_fnw, d_gate2 = _rowcall(
        f_final, [_rows(x1, rb), _rows(z2, rb), _full(gate2), _full(sm["fnw"]), _rows(target, rb)],
        [_oacc(1, 1), _orow(s, D, F32, rb), _orow(s, D, BF16, rb), _oacc(1, D), _oacc(1, D)],
        n_rows=s, rb=rb, name="final_loss")
    d_hidden = _mm(dz2, w["wdown"], "down_proj_dx", tb=True, tn=1408)
    g_wdown = _mm(hidden, dz2, "down_proj_dw", ta=True, out_dtype=F32, tm=1408, tn=1024, tk=2048)

    def f_ffn_bwd(c, i, uv, hl, dh, cw, cb):
        uf = uv.astype(F32)
        hf = hl.astype(F32) * (i > 0).astype(F32)
        u1, u2 = _shift_rows(uf, hf, 1), _shift_rows(uf, hf, 2)
        uc = cb + _pick_row(cw, 0) * u2 + _pick_row(cw, 1) * u1 + _pick_row(cw, 2) * uf
        val, gt = uc[:, :W_UP_SH], uc[:, W_UP_SH:]
        cdf, pdf = _gelu_parts(gt)
        dhf = dh.astype(F32)
        duc = jnp.concatenate([dhf * (gt * cdf), dhf * val * (cdf + gt * pdf)], axis=1)
        dcw = jnp.concatenate([_csum(duc * u2), _csum(duc * u1), _csum(duc * uf)], axis=0)
        return (duc, _csum(duc), dcw)

    duc, d_cb, d_cw = _rowcall(
        f_ffn_bwd, [_rows(u, rb, cwid, ccol), _halo(u, rb, 16, cwid, ccol, True), _rows(d_hidden, rb, W_UP_SH, ccol),
                    _full(sm["cw"], cwid, ccol), _full(sm["cb"], cwid, ccol)],
        [_orow(s, 2 * D_FF, BF16, rb, cwid, ccol), _oacc(1, 2 * D_FF, cwid, ccol), _oacc(3, 2 * D_FF, cwid, ccol)],
        n_rows=s, rb=rb, name="conv_geglu_bwd", ncol=2)

    def f_conv_t(c, i, dv, hl, cw):
        df = dv.astype(F32)
        hf = hl.astype(F32) * (i < s // rb - 1).astype(F32)
        return (_pick_row(cw, 2) * df + _pick_row(cw, 1) * _shift_rows_up(df, hf, 1) + _pick_row(cw, 0) * _shift_rows_up(df, hf, 2),)

    (du,) = _rowcall(f_conv_t, [_rows(duc, rb, cwid, ccol), _halo(duc, rb, 16, cwid, ccol, False), _full(sm["cw"], cwid, ccol)],
                     [_orow(s, 2 * D_FF, BF16, rb, cwid, ccol)], n_rows=s, rb=rb, name="conv_transpose", ncol=2)
    g_wup = _mm(h2, du, "up_proj_dw", ta=True, out_dtype=F32, tm=1024, tk=2048, o_shards=True)
    gs45 = [g_wup, g_wdown.reshape(4, W_DOWN_SH, 1024)]
    d_h2, [land45] = _mm(du, w["wup"], "up_proj_dx", tb=True, b_shards=True, comm=[_u_pair_send(gs45, (4, 5))])
    ts45 = [_pair_add(g, ld, core, "grad_pair_add_" + BIG[i]) for g, ld, i in zip(gs45, land45, (4, 5))]

    def f_norm2_bwd(c, i, x1v, dh, dxr, z, nw, sc, g1):
        dxn, dsh, dsc, dnw = _norm_bwd(x1v, dh.astype(F32), nw, sc)
        dx1 = dxr + dxn
        return (dx1, dx1 * g1, dsh, dsc, dnw, _csum(dx1 * z.astype(F32)))

    dx1, dz1, d_shift2, d_scale2, d_n2w, d_gate1 = _rowcall(
        f_norm2_bwd, [_rows(x1, rb), _rows(d_h2, rb), _rows(dx2, rb), _rows(z1, rb), _full(sm["n2w"]), _full(scale2), _full(gate1)],
        [_orow(s, D, F32, rb), _orow(s, D, BF16, rb), _oacc(1, D), _oacc(1, D), _oacc(1, D), _oacc(1, D)],
        n_rows=s, rb=rb, name="norm2_bwd")
    d_mixed = _mm(dz1, w["wout"], "out_proj_dx", tb=True)
    g_wout = _mm(mixed, dz1, "out_proj_dw", ta=True, out_dtype=F32, tk=2048)

    def f_merge_bwd(c, i, dm, ma, mb, yg, ya):
        dmf, ygf, yaf = dm.astype(F32), yg.astype(F32), ya.astype(F32)
        sa, sb = _sigmoid(ma.astype(F32)), _sigmoid(mb.astype(F32))
        return (dmf * sa, dmf * sb, jnp.concatenate([dmf * ygf * sa * (1.0 - sa), dmf * yaf * sb * (1.0 - sb)], axis=1))

    dy_gla, dy_att, dp = _rowcall(
        f_merge_bwd, [_rows(d_mixed, rb), _rows(p, rb, D, P_MA // D), _rows(p, rb, D, P_MB // D), _rows(y_gla, rb), _rows(y_att, rb)],
        [_orow(s, D, BF16, rb)] * 2 + [_orow(s, P_W, BF16, rb, 2 * D, lambda c: P_MA // (2 * D))], n_rows=s, rb=rb, name="merge_bwd")
    d_og = _mm(dy_gla, w["wgb"], "gla_branch_dx", tb=True)
    g_wgb = _mm(og, dy_gla, "gla_branch_dw", ta=True, out_dtype=F32, tk=2048)
    d_oatt = _mm(dy_att, w["wab"], "attn_branch_dx", tb=True)
    g_wab = _mm(o_att, dy_att, "attn_branch_dw", ta=True, out_dtype=F32, tk=2048)

    def f_gla_post_bwd(c, i, ov, gnw, gr, dog):
        g = gr.astype(F32)
        sg = _sigmoid(g)
        silu = g * sg
        dof = dog.astype(F32)
        don = dof * silu
        on_parts, do_parts, dgn = [], [], jnp.zeros((1, 256), F32)
        for k in range(GLA_H):
            oh = ov[:, k * 256:(k + 1) * 256]
            dh = don[:, k * 256:(k + 1) * 256]
            r = _rms(oh)
            xh = oh * r
            dgn = dgn + _csum(dh * xh)
            dxh = dh * gnw
            do_parts.append(r * (dxh - xh * jnp.mean(dxh * xh, axis=-1, keepdims=True)))
            on_parts.append(xh * gnw)
        on = jnp.concatenate(on_parts, axis=1)
        dgr = dof * on * (sg * (1.0 + g * (1.0 - sg)))
        return (jnp.concatenate(do_parts, axis=1), dgr, dgn)

    do_gla, dp, d_gnw = _rowcall(
        f_gla_post_bwd, [_rows(o_gla, rb), _full(sm["gnw"]), _rows(p, rb, 1024, P_GR // 1024), _rows(d_og, rb)],
        [_orow(s, 1024, F32, rb), _orow(s, P_W, BF16, rb, 1024, lambda c: P_GR // 1024), _oacc(1, 256)],
        n_rows=s, rb=rb, name="gla_post_bwd", into=(dp, 1))
    gs123 = [g_wgb.reshape(4, 256, 1024), _cols_split(g_wab), g_wout.reshape(4, 256, 1024)]
    d_gq, d_gk, dp, d_la, [r4, land123] = _gla_bwd(p, la, states, do_gla, s, dp,
                                                   comm=[_u_chip_exchange(ts45[:1]), _u_pair_send(gs123, (1, 2, 3))])
    half4 = [_chip_sum(ts45[0], r4[0], chip1, "grad_chip_sum_w_up")]
    ts123 = [_pair_add(g, ld, core, "grad_pair_add_" + BIG[i]) for g, ld, i in zip(gs123, land123, (1, 2, 3))]

    def f_gla_pre_bwd(c, i, lav, dlav, glr, w2):
        dz = dlav * (1.0 / GLA_TAU) * (1.0 - jnp.exp(GLA_TAU * lav))
        dzb = dz.astype(BF16)
        return (_dg(dzb, w2.astype(BF16), 1, 1), _csum(dz), _dg(glr, dzb, 0, 0))

    d_glr, d_gb, d_w2 = _rowcall(
        f_gla_pre_bwd, [_rows(la, rb), _rows(d_la, rb), _rows(p, rb, 128, P_LR // 128), _full(sm["w2"])],
        [_orow(s, 128, BF16, rb), _oacc(1, 512), _oacc(128, 512)], n_rows=s, rb=rb, name="gla_pre_bwd")

    do_d = [d_oatt] + list(_dilate(d_oatt, s))
    datt = [_attn_bwd(q_d[g], k_d[g], v_d[g], do_d[g], (o_att, o_d1, o_d2)[g], (lse, lse_d1, lse_d2)[g], g, r, s)
            for g, r in enumerate(_RS)]
    dp = _rope_bwd(datt, d_glr, dp, cos_t, sin_t, s)
    dp = lax.dynamic_update_slice(dp, jnp.concatenate([d_gq, d_gk], axis=1), (0, P_GQ))
    g_win, [r1235, oth4] = _mm(h, dp, "in_proj_dw", ta=True, out_dtype=F32, tm=1024, tn=1536, tk=2048,
                               comm=[_u_chip_exchange(ts123 + ts45[1:]), _u_pair_join(half4)])
    half1235 = [_chip_sum(t, r, chip1, "grad_chip_sum_" + BIG[i]) for t, r, i in zip(ts123 + ts45[1:], r1235, (1, 2, 3, 5))]
    gs0 = [_win_split(g_win)]
    d_h, [land0, oth1235] = _mm(dp, w["win"], "in_proj_dx", tb=True, tk=3840,
                                comm=[_u_pair_send(gs0, (0,)), _u_pair_join(half1235)])
    half123, half45 = half1235[:3], half4 + half1235[3:]
    oth123, oth45 = oth1235[:3], oth4 + oth1235[3:]
    pending = _exchange_start(_pair_add(gs0[0], land0[0], core, "grad_pair_add_w_in"), "grad_exchange_w_in_start")

    def f_norm1_bwd(c, i, xv, dh, dxr, nw, sc):
        dxn, dsh, dsc, dnw = _norm_bwd(xv, dh.astype(F32), nw, sc)
        return (dxr + dxn, dsh, dsc, dnw)

    grad_x, d_shift1, d_scale1, d_n1w = _rowcall(
        f_norm1_bwd, [_rows(x, rb), _rows(d_h, rb), _rows(dx1, rb), _full(sm["n1w"]), _full(scale1)],
        [_orow(s, D, F32, rb), _oacc(1, D), _oacc(1, D), _oacc(1, D)], n_rows=s, rb=rb, name="norm1_bwd", after=pending[4:])

    dmod = jnp.concatenate([d_shift1, d_scale1, d_gate1, d_shift2, d_scale2, d_gate2], axis=1)
    small = dict(dmod=dmod, n1w=d_n1w, gb=d_gb, gnw=d_gnw, n2w=d_n2w, cb=d_cb, fnw=d_fnw, w2=d_w2, cw=d_cw)
    return loss, grad_x, half123 + half45, oth123 + oth45, small, pending


def _win_pieces():
    runs = [(P_GV, 1024, 2048), (P_MA, 5392, 2048), (P_GQ, 0, 1024), (P_AQ, 3088, 2304), (P_LR, 3072, GLA_LR)]
    out = []
    for kc, rc, ln in runs:
        while ln > 0:
            step = min(ln, W_IN_SH - rc % W_IN_SH)
            out.append((kc, rc, step))
            kc, rc, ln = kc + step, rc + step, ln - step
    return out


def _win_assemble(shards):
    rb = 256

    def body(s_ref, o_ref):
        o_ref[:, W_IN:] = jnp.zeros((rb, P_W - W_IN), o_ref.dtype)
        for kc, rc, ln in _win_pieces():
            o_ref[:, kc:kc + ln] = s_ref[rc // W_IN_SH, :, rc % W_IN_SH:rc % W_IN_SH + ln]

    return pl.pallas_call(
        body, name="w_in_assemble", grid=(D // rb,),
        in_specs=[pl.BlockSpec((4, rb, W_IN_SH), lambda i: (0, i, 0))], out_specs=pl.BlockSpec((rb, P_W), lambda i: (i, 0)),
        out_shape=jax.ShapeDtypeStruct((D, P_W), shards.dtype), compiler_params=_params(("parallel",)),
    )(shards)


def _win_split(g):
    rb = 256

    def body(g_ref, o_ref):
        for kc, rc, ln in _win_pieces():
            o_ref[rc // W_IN_SH, :, rc % W_IN_SH:rc % W_IN_SH + ln] = g_ref[:, kc:kc + ln]

    return pl.pallas_call(
        body, name="w_in_grad_split", grid=(D // rb,),
        in_specs=[pl.BlockSpec((rb, P_W), lambda i: (i, 0))], out_specs=pl.BlockSpec((4, rb, W_IN_SH), lambda i: (0, i, 0)),
        out_shape=jax.ShapeDtypeStruct((4, D, W_IN_SH), g.dtype), compiler_params=_params(("parallel",)),
    )(g)


def _ff_to_kernel(a):
    h = W_UP_SH
    return jnp.concatenate([a[:, 0:h], a[:, D_FF:D_FF + h], a[:, h:D_FF], a[:, D_FF + h:]], axis=1)


def _ff_from_kernel(a):
    h = W_UP_SH
    return jnp.concatenate([a[:, 0:h], a[:, 2 * h:3 * h], a[:, h:2 * h], a[:, 3 * h:]], axis=1)


BIG = ("w_in", "w_gla_branch", "w_attn_branch", "w_out", "w_up", "w_down")
SH_SHAPES = ((1024, W_IN_SH), (256, 1024), (256, 256), (256, 1024), (1024, W_UP_SH), (W_DOWN_SH, 1024))
N_BIG = len(BIG)


def _cols_join(t):
    return jnp.concatenate([t[k] for k in range(4)], axis=1)


def _cols_split(t):
    cols = t.shape[1] // 4
    return jnp.stack([t[:, k * cols:(k + 1) * cols] for k in range(4)])


def _me():
    return lax.axis_index("x"), lax.axis_index("y"), lax.axis_index("c")


HBM = pl.BlockSpec(memory_space=pltpu.HBM)
VMEM_SPEC = pl.BlockSpec(memory_space=pltpu.VMEM)


def _allgather8(xs, name):
    rows = xs.shape[0]

    def body(x_ref, out_ref, send_sems, recv_sems, local_sem):
        x, y, c = _me()
        me = 4 * x + 2 * y + c
        mine = pltpu.make_async_copy(x_ref, out_ref.at[me], local_sem)
        mine.start()
        flips = [(k >> 2 & 1, k >> 1 & 1, k & 1) for k in range(1, 8)]

        def peer(f):
            return (jnp.where(f[0] == 1, 1 - x, x), jnp.where(f[1] == 1, 1 - y, y), jnp.where(f[2] == 1, 1 - c, c))

        sends = []
        for k, f in enumerate(flips):
            cp = pltpu.make_async_remote_copy(src_ref=x_ref, dst_ref=out_ref.at[me], send_sem=send_sems.at[k],
                                              recv_sem=recv_sems.at[k], device_id=peer(f), device_id_type=MESH)
            cp.start()
            sends.append(cp)
        for k, f in enumerate(flips):
            px, py, pc = peer(f)
            pltpu.make_async_remote_copy(src_ref=x_ref, dst_ref=out_ref.at[4 * px + 2 * py + pc], send_sem=send_sems.at[k],
                                         recv_sem=recv_sems.at[k], device_id=peer(f), device_id_type=MESH).wait_recv()
        for cp in sends:
            cp.wait_send()
        mine.wait()

    return pl.pallas_call(
        body, name=name, out_shape=jax.ShapeDtypeStruct((8, rows, 128), F32),
        in_specs=[VMEM_SPEC], out_specs=VMEM_SPEC,
        scratch_shapes=[pltpu.SemaphoreType.DMA((7,)), pltpu.SemaphoreType.DMA((7,)), pltpu.SemaphoreType.DMA],
        compiler_params=pltpu.CompilerParams(vmem_limit_bytes=VMEM_LIMIT),
    )(xs)


def _half_rows(i, cc, unit):
    rows = SH_SHAPES[i][0] // 2
    return pl.ds(pl.multiple_of(cc * rows, unit), rows)


def _rc(src, dst, sems, to):
    return pltpu.make_async_remote_copy(src_ref=src, dst_ref=dst, send_sem=sems[0], recv_sem=sems[1], device_id=to, device_id_type=MESH)


def _other_chips(x, y):
    return [(1 - x, y), (x, 1 - y), (1 - x, 1 - y)]


def _u_gather_ici(w_sh, idxs):
    def copies(ins, outs, sem):
        x, y, c = _me()
        res = []
        for j, (px, py) in enumerate(_other_chips(x, y)):
            for n, i in enumerate(idxs):
                src = ins[n].at[0, _half_rows(i, c, 16)]
                res.append((_rc(src, outs[n].at[2 * x + y, _half_rows(i, c, 16)], sem(j * len(idxs) + n), (px, py, c)),
                            _rc(src, outs[n].at[2 * px + py, _half_rows(i, c, 16)], sem(j * len(idxs) + n), (px, py, c))))
        return res

    return dict(ins=[w_sh[i] for i in idxs], outs=[jax.ShapeDtypeStruct((4,) + SH_SHAPES[i], BF16) for i in idxs],
                nsem=3 * len(idxs), alias={}, copies=copies)


def _u_gather_d2d(got, idxs):
    def copies(ins, outs, sem):
        x, y, c = _me()
        res = []
        for j, (px, py) in enumerate(_other_chips(x, y)):
            for n, i in enumerate(idxs):
                src = ins[n].at[2 * px + py, _half_rows(i, c, 16)]
                res.append((_rc(src, outs[n].at[2 * px + py, _half_rows(i, c, 16)], sem(j * len(idxs) + n), (x, y, 1 - c)),
                            _rc(src, outs[n].at[2 * px + py, _half_rows(i, 1 - c, 16)], sem(j * len(idxs) + n), (x, y, 1 - c))))
        return res

    return dict(ins=list(got), outs=[jax.ShapeDtypeStruct(g.shape, g.dtype) for g in got], nsem=3 * len(idxs),
                alias={n: n for n in range(len(idxs))}, copies=copies)


def _u_pair_send(gs, idxs):
    def copies(ins, outs, sem):
        x, y, c = _me()
        res = []
        for n, i in enumerate(idxs):
            for sh in range(4):
                cp = _rc(ins[n].at[sh, _half_rows(i, 1 - c, 8)], outs[n].at[sh], sem(4 * n + sh), (x, y, 1 - c))
                res.append((cp, cp))
        return res

    return dict(ins=list(gs), outs=[jax.ShapeDtypeStruct((4, SH_SHAPES[i][0] // 2, SH_SHAPES[i][1]), F32) for i in idxs],
                nsem=4 * len(idxs), alias={}, copies=copies)


def _u_chip_exchange(ts):
    def copies(ins, outs, sem):
        x, y, c = _me()
        res = []
        for j, (px, py) in enumerate(_other_chips(x, y)):
            for n in range(len(ts)):
                cp = _rc(ins[n].at[2 * px + py], outs[n].at[j], sem(j * len(ts) + n), (px, py, c))
                res.append((cp, cp))
        return res

    return dict(ins=list(ts), outs=[jax.ShapeDtypeStruct((3,) + t.shape[1:], t.dtype) for t in ts], nsem=3 * len(ts),
                alias={}, copies=copies)


def _u_pair_join(hs):
    def copies(ins, outs, sem):
        x, y, c = _me()
        res = []
        for n in range(len(hs)):
            cp = _rc(ins[n], outs[n], sem(n), (x, y, 1 - c))
            res.append((cp, cp))
        return res

    return dict(ins=list(hs), outs=[jax.ShapeDtypeStruct(h.shape, h.dtype) for h in hs], nsem=len(hs), alias={}, copies=copies)


def _comm_phase(units, ci, co, send_sems, recv_sems, start):
    ii = oo = off = 0
    for u in units:
        ni, no = len(u["ins"]), len(u["outs"])
        for st, arrival in u["copies"](ci[ii:ii + ni], co[oo:oo + no], lambda k, off=off: (send_sems.at[off + k], recv_sems.at[off + k])):
            if start:
                st.start()
            else:
                st.wait_send()
                arrival.wait_recv()
        ii, oo, off = ii + ni, oo + no, off + u["nsem"]


def _carry(units, n_in, n_out):
    ins = [a for u in units for a in u["ins"]]
    outs = [o for u in units for o in u["outs"]]
    alias, ii, oo = {}, 0, 0
    for u in units:
        for a, b in u["alias"].items():
            alias[n_in + ii + a] = n_out + oo + b
        ii, oo = ii + len(u["ins"]), oo + len(u["outs"])
    nsem = sum(u["nsem"] for u in units)
    scratch = [pltpu.SemaphoreType.DMA((nsem,)), pltpu.SemaphoreType.DMA((nsem,))] if units else []
    return ins, outs, alias, scratch


def _split_units(units, res):
    out, oo = [], 0
    for u in units:
        out.append(list(res[oo:oo + len(u["outs"])]))
        oo += len(u["outs"])
    return out


def _comm_call(name, units):
    ins, outs, alias, scratch = _carry(units, 0, 0)

    def body(*refs):
        ci, co = refs[:len(ins)], refs[len(ins):len(ins) + len(outs)]
        _comm_phase(units, ci, co, refs[-2], refs[-1], True)
        _comm_phase(units, ci, co, refs[-2], refs[-1], False)

    res = pl.pallas_call(body, name=name, out_shape=outs, in_specs=[HBM] * len(ins), out_specs=[HBM] * len(outs),
                         scratch_shapes=scratch, input_output_aliases=alias)(*ins)
    return _split_units(units, res)


SEM = pl.BlockSpec(memory_space=pltpu.SEMAPHORE)
EFFECT = pltpu.SideEffectType.DATAFLOW_SIDE_EFFECTING


def _exchange_copies(t_ref, land_ref, send_sems, recv_sems):
    x, y, c = _me()
    return [pltpu.make_async_remote_copy(src_ref=t_ref.at[2 * px + py], dst_ref=land_ref.at[j], send_sem=send_sems.at[j],
                                         recv_sem=recv_sems.at[j], device_id=(px, py, c), device_id_type=MESH)
            for j, (px, py) in enumerate(_other_chips(x, y))]


def _exchange_start(t, name):
    land = lax.empty((3,) + t.shape[1:], t.dtype)

    def body(t_ref, land_ref, send_sems, recv_sems, t_thru, land_thru, token):
        for cp in _exchange_copies(t_ref, land_ref, send_sems, recv_sems):
            cp.start()
        token[...] = jnp.zeros_like(token)

    return pl.pallas_call(
        body, name=name,
        out_shape=(pltpu.SemaphoreType.DMA((3,)), pltpu.SemaphoreType.DMA((3,)), pltpu.HBM(t.shape, t.dtype),
                   pltpu.HBM(land.shape, land.dtype), jax.ShapeDtypeStruct((8, 128), F32)),
        in_specs=(HBM, HBM), out_specs=(SEM, SEM, HBM, HBM, VMEM_SPEC), input_output_aliases={0: 2, 1: 3},
        compiler_params=pltpu.CompilerParams(has_side_effects=EFFECT),
    )(pltpu.with_memory_space_constraint(t, pltpu.HBM), pltpu.with_memory_space_constraint(land, pltpu.HBM))


def _exchange_wait(send_sems, recv_sems, t_thru, land_thru, after, name):
    def body(t_ref, land_ref, send_sems, recv_sems, *rest):
        for cp in _exchange_copies(t_ref, land_ref, send_sems, recv_sems):
            cp.wait_send()
            cp.wait_recv()

    return pl.pallas_call(
        body, name=name, out_shape=(pltpu.HBM(t_thru.shape, t_thru.dtype), pltpu.HBM(land_thru.shape, land_thru.dtype)),
        in_specs=(HBM, HBM, SEM, SEM) + (pl.BlockSpec(memory_space=pl.ANY),) * len(after), out_specs=(HBM, HBM),
        input_output_aliases={0: 0, 1: 1}, compiler_params=pltpu.CompilerParams(has_side_effects=EFFECT),
    )(t_thru, land_thru, send_sems, recv_sems, *after)


def _pair_add(g, land, core, name):
    _, rows, cols = g.shape
    half = rows // 2
    rb = _tile(half, 256, 16)
    nb = half // rb

    def body(c_ref, g_ref, l_ref, o_ref):
        o_ref[...] = (g_ref[...] + l_ref[...]).astype(BF16)

    return pl.pallas_call(
        body, name=name,
        grid_spec=pltpu.PrefetchScalarGridSpec(
            num_scalar_prefetch=1, grid=(4, nb),
            in_specs=[pl.BlockSpec((1, rb, cols), lambda s, i, c_ref: (s, c_ref[0] * nb + i, 0)),
                      pl.BlockSpec((1, rb, cols), lambda s, i, c_ref: (s, i, 0))],
            out_specs=pl.BlockSpec((1, rb, cols), lambda s, i, c_ref: (s, i, 0))),
        out_shape=jax.ShapeDtypeStruct((4, half, cols), BF16),
        compiler_params=_params(("parallel", "parallel")),
    )(core, g, land)


def _chip_sum(t, r, chip, name):
    _, half, cols = t.shape
    rb = _tile(half, 256, 16)

    def body(s_ref, t_ref, r_ref, o_ref):
        o_ref[...] = ((t_ref[0].astype(F32) + r_ref[0].astype(F32)) + r_ref[1].astype(F32)) + r_ref[2].astype(F32)

    return pl.pallas_call(
        body, name=name,
        grid_spec=pltpu.PrefetchScalarGridSpec(
            num_scalar_prefetch=1, grid=(half // rb,),
            in_specs=[pl.BlockSpec((1, rb, cols), lambda i, s_ref: (s_ref[0], i, 0)),
                      pl.BlockSpec((3, rb, cols), lambda i, s_ref: (0, i, 0))],
            out_specs=pl.BlockSpec((rb, cols), lambda i, s_ref: (i, 0))),
        out_shape=jax.ShapeDtypeStruct((half, cols), F32),
        compiler_params=_params(("parallel",)),
    )(chip, t, r)


def _adam_math(wv, gv, mv, vv):
    mn = ADAM_B1 * mv + (1.0 - ADAM_B1) * gv
    vn = ADAM_B2 * vv + (1.0 - ADAM_B2) * (gv * gv)
    m_hat = mn / (1.0 - ADAM_B1 ** ADAM_STEP)
    v_hat = vn / (1.0 - ADAM_B2 ** ADAM_STEP)
    return -ADAM_LR * (m_hat / (jnp.sqrt(v_hat) + ADAM_EPS) + ADAM_WD * wv), mn, vn


def _adamw_halves(wt, mt, vt, mine, theirs, core, name):
    _, rows, cols = wt.shape
    half = rows // 2
    rb = _tile(half, 256, 8)
    nb = half // rb

    def body(c_ref, w_ref, m_ref, v_ref, a_ref, b_ref, g_ref, d_ref, mo_ref, vo_ref):
        gv = jnp.where(pl.program_id(0) == c_ref[0], a_ref[...], b_ref[...])
        dl, mn, vn = _adam_math(w_ref[...], gv, m_ref[...], v_ref[...])
        g_ref[...] = gv
        d_ref[...] = dl
        mo_ref[...] = mn
        vo_ref[...] = vn

    full = pl.BlockSpec((None, rb, cols), lambda hf, i, c_ref: (0, hf * nb + i, 0))
    part = pl.BlockSpec((rb, cols), lambda hf, i, c_ref: (i, 0))
    return pl.pallas_call(
        body, name=name,
        grid_spec=pltpu.PrefetchScalarGridSpec(num_scalar_prefetch=1, grid=(2, nb), in_specs=[full, full, full, part, part],
                                               out_specs=[full] * 4),
        out_shape=[jax.ShapeDtypeStruct((1, rows, cols), F32)] * 4,
        compiler_params=_params(("parallel", "parallel")),
    )(core, wt, mt, vt, mine, theirs)


SG_REP = 144
SG_LOSS = 136
SG_W2, SG_CW = SG_REP, SG_REP + 4 * 16
SG_ROWS = SG_CW + 4 * 40
SP_ROWS = SG_REP + 16 + 40


def _mod_shard(c_all, ada_w_sh):
    def body(c_ref, w_ref, o_ref):
        cv = c_ref[...]
        o_ref[...] = _dg((cv * _sigmoid(cv)).astype(BF16), w_ref[...].astype(BF16), 1, 0)

    return pl.pallas_call(body, name="mod_shard", out_shape=jax.ShapeDtypeStruct((8, 1536), F32),
                          in_specs=[VMEM_SPEC, VMEM_SPEC], out_specs=VMEM_SPEC,
                          compiler_params=pltpu.CompilerParams(vmem_limit_bytes=VMEM_LIMIT))(c_all, ada_w_sh)


def _mod_select(mod_all, ada_b4):
    def body(m_ref, b_ref, o_ref):
        x, y, c = _me()
        me = 4 * x + 2 * y + c
        for sh in range(4):
            o_ref[sh] = m_ref[2 * sh, me] + b_ref[sh]

    return pl.pallas_call(body, name="mod_select", out_shape=jax.ShapeDtypeStruct((4, 12, 128), F32),
                          in_specs=[VMEM_SPEC, VMEM_SPEC], out_specs=VMEM_SPEC)(mod_all, ada_b4)


def _small_reduce(sg_all):
    def body(g_ref, o_ref):
        x, y, c = _me()
        s_me = 2 * x + y
        w2_rows = pl.ds(pl.multiple_of(SG_W2 + 16 * s_me, 8), 16)
        cw_rows = pl.ds(pl.multiple_of(SG_CW + 40 * s_me, 8), 40)
        a = g_ref[0, 0:SG_REP, :]
        b = g_ref[0, w2_rows, :]
        d = g_ref[0, cw_rows, :]
        for dev in range(1, 8):
            a = a + g_ref[dev, 0:SG_REP, :]
            b = b + g_ref[dev, w2_rows, :]
            d = d + g_ref[dev, cw_rows, :]
        o_ref[0:SG_REP, :] = a
        o_ref[SG_REP:SG_REP + 16, :] = b
        o_ref[SG_REP + 16:SP_ROWS, :] = d

    return pl.pallas_call(body, name="small_grad_reduce", out_shape=jax.ShapeDtypeStruct((SP_ROWS, 128), F32),
                          in_specs=[VMEM_SPEC], out_specs=VMEM_SPEC)(sg_all)


def _ada_grad(dmod_all, c_bc):
    def body(g_ref, c_ref, o_ref):
        x, y, c = _me()
        s_me = 2 * x + y
        for k in range(12):
            acc = jnp.zeros((D, 128), F32)
            for b in range(8):
                cv = c_ref[b]
                acc = acc + (cv * _sigmoid(cv)) * g_ref[s_me, k, b:b + 1, :]
            o_ref[:, k * 128:(k + 1) * 128] = acc

    return pl.pallas_call(body, name="ada_w_grad", out_shape=jax.ShapeDtypeStruct((D, 1536), F32),
                          in_specs=[VMEM_SPEC, VMEM_SPEC], out_specs=VMEM_SPEC,
                          compiler_params=pltpu.CompilerParams(vmem_limit_bytes=VMEM_LIMIT))(dmod_all, c_bc)


def _adamw(wt, g, m, v, name):
    rows, cols = wt.shape
    rb = _tile(rows, 256, 8)

    def fn(c, i, wv, gv, mv, vv):
        return _adam_math(wv, gv, mv, vv)

    return _rowcall(fn, [_rows(t, rb) for t in (wt, g, m, v)], [_orow(rows, cols, F32, rb)] * 3,
                    n_rows=rows, rb=rb, name=name)


def _pad_rows(t, rows):
    flat = t.reshape(-1)
    return jnp.pad(flat, (0, rows * 128 - flat.shape[0])).reshape(rows, 128)


SP_LAYOUT = (("ada_b", 48), ("norm1_w", 8), ("gla_gate_b", 8), ("gla_norm_w", 8), ("norm2_w", 8), ("conv_b", 48),
             ("final_norm_w", 8), (None, 8), ("gla_gate_w2", 16), ("conv_w", 40))


def _pack_small(d):
    return jnp.concatenate([jnp.zeros((rows, 128), F32) if n is None else _pad_rows(d[n].astype(F32), rows)
                            for n, rows in SP_LAYOUT], axis=0)


def _unpack_small(pk, shapes):
    out, off = {}, 0
    for n, rows in SP_LAYOUT:
        if n is not None:
            shp = shapes[n]
            out[n] = pk[off:off + rows].reshape(-1)[:math.prod(shp)].reshape(shp)
        off += rows
    return out


def kernel(x, c, positions, ada_w, ada_b, norm1_w, w_in, gla_gate_w2, gla_gate_b, gla_norm_w, w_gla_branch, w_attn_branch, w_out, norm2_w, w_up, conv_w, conv_b, w_down, final_norm_w, loss_target, m_ada_w, m_ada_b, m_norm1_w, m_w_in, m_gla_gate_w2, m_gla_gate_b, m_gla_norm_w, m_w_gla_branch, m_w_attn_branch, m_w_out, m_norm2_w, m_w_up, m_conv_w, m_conv_b, m_w_down, m_final_norm_w, v_ada_w, v_ada_b, v_norm1_w, v_w_in, v_gla_gate_w2, v_gla_gate_b, v_gla_norm_w, v_w_gla_branch, v_w_attn_branch, v_w_out, v_norm2_w, v_w_up, v_conv_w, v_conv_b, v_w_down, v_final_norm_w):
    s = x.shape[1]
    names = ("ada_w", "ada_b", "norm1_w", "w_in", "gla_gate_w2", "gla_gate_b", "gla_norm_w", "w_gla_branch", "w_attn_branch",
             "w_out", "norm2_w", "w_up", "conv_w", "conv_b", "w_down", "final_norm_w")
    wts = dict(zip(names, (ada_w, ada_b, norm1_w, w_in, gla_gate_w2, gla_gate_b, gla_norm_w, w_gla_branch, w_attn_branch,
                           w_out, norm2_w, w_up, conv_w, conv_b, w_down, final_norm_w)))
    ms = dict(zip(names, (m_ada_w, m_ada_b, m_norm1_w, m_w_in, m_gla_gate_w2, m_gla_gate_b, m_gla_norm_w, m_w_gla_branch,
                          m_w_attn_branch, m_w_out, m_norm2_w, m_w_up, m_conv_w, m_conv_b, m_w_down, m_final_norm_w)))
    vs = dict(zip(names, (v_ada_w, v_ada_b, v_norm1_w, v_w_in, v_gla_gate_w2, v_gla_gate_b, v_gla_norm_w, v_w_gla_branch,
                          v_w_attn_branch, v_w_out, v_norm2_w, v_w_up, v_conv_w, v_conv_b, v_w_down, v_final_norm_w)))

    pk0 = jnp.concatenate([_pad_rows(c, 8), _pad_rows(gla_gate_w2, 16), _pad_rows(conv_w, 40)], axis=0)
    sm_all = _allgather8(pk0, "gather_small")
    c_all = sm_all[:, 0:8, :].reshape(8, D)
    w2_full = sm_all[0::2, 8:24, :].transpose(1, 0, 2).reshape(GLA_LR, 512)
    cw_full = sm_all[0::2, 24:64, :].reshape(4, 40 * 128)[:, :3 * W_UP_SH].reshape(4, 3, W_UP_SH).transpose(1, 0, 2).reshape(3, 2 * D_FF)

    mod_sh = _mod_shard(c_all, ada_w[0])
    mod_all = _allgather8(mod_sh.reshape(96, 128), "gather_mod")
    mod = _mod_select(mod_all.reshape(8, 8, 12, 128), ada_b.reshape(4, 12, 128)).reshape(6, D)

    core = lax.axis_index("c").astype(jnp.int32).reshape(1)
    chip = (2 * lax.axis_index("x") + lax.axis_index("y")).astype(jnp.int32)
    w_sh = [wts[n].astype(BF16) for n in BIG]
    sm = dict(n1w=norm1_w, n2w=norm2_w, fnw=final_norm_w.reshape(1, D), gnw=gla_norm_w, gb=gla_gate_b,
              w2=jnp.pad(w2_full, ((0, 128 - GLA_LR), (0, 0))), cw=_ff_to_kernel(cw_full), cb=_ff_to_kernel(conv_b))
    loss, grad_x, halves, others, small, pending = _local_step(x[0], mod, positions.reshape(s, 1), loss_target[0], sm, w_sh,
                                                               chip, core)

    dcw = _ff_from_kernel(small["cw"]).reshape(3, 4, W_UP_SH).transpose(1, 0, 2)
    dw2 = small["w2"][:GLA_LR].reshape(GLA_LR, 4, 128).transpose(1, 0, 2)
    sg = jnp.concatenate(
        [_pad_rows(small["dmod"], 48), _pad_rows(small["n1w"], 8), _pad_rows(small["gb"], 8), _pad_rows(small["gnw"], 8),
         _pad_rows(small["n2w"], 8), _pad_rows(_ff_from_kernel(small["cb"]), 48), _pad_rows(small["fnw"], 8), _pad_rows(loss, 8)]
        + [_pad_rows(dw2[k], 16) for k in range(4)] + [_pad_rows(dcw[k], 40) for k in range(4)], axis=0)
    sg_all = _allgather8(sg, "gather_small_grads")
    g_small_pk = _small_reduce(sg_all)
    dmod_all = sg_all[:, 0:48, :].reshape(8, 4, 12, 128).transpose(1, 2, 0, 3)
    g_ada_w = _ada_grad(dmod_all, jnp.broadcast_to(c_all[:, :, None], (8, D, 128)))

    shapes = {n: wts[n].shape for n in names}
    g_small = _unpack_small(g_small_pk, shapes)
    grads = {"ada_w": g_ada_w.reshape(1, D, 1536), **g_small}
    deltas, new_m, new_v = {}, {}, {}
    for n, mine, theirs in zip(BIG[1:], halves, others):
        grads[n], deltas[n], new_m[n], new_v[n] = _adamw_halves(wts[n], ms[n], vs[n], mine, theirs, core, "adamw_" + n)
    shp = ada_w.shape
    d_, m_, v_ = _adamw(ada_w[0], g_ada_w, m_ada_w[0], v_ada_w[0], "adamw_ada_w")
    deltas["ada_w"], new_m["ada_w"], new_v["ada_w"] = d_.reshape(shp), m_.reshape(shp), v_.reshape(shp)
    d_, m_, v_ = _adamw(_pack_small(wts), g_small_pk, _pack_small(ms), _pack_small(vs), "adamw_small")
    for dst, pk in ((deltas, d_), (new_m, m_), (new_v, v_)):
        dst.update(_unpack_small(pk, shapes))

    t0, r0 = _exchange_wait(*pending[:4], after=[d_, deltas["ada_w"], deltas["w_up"], deltas["w_down"]], name="grad_exchange_w_in_wait")
    half0 = _chip_sum(t0, r0, chip.reshape(1), "grad_chip_sum_w_in")
    [[oth0]] = _comm_call("grad_join_w_in", [_u_pair_join([half0])])
    grads["w_in"], deltas["w_in"], new_m["w_in"], new_v["w_in"] = _adamw_halves(w_in, m_w_in, v_w_in, half0, oth0, core, "adamw_w_in")

    return (g_small_pk[SG_LOSS, 0], grad_x.reshape(1, s, D), *[grads[n] for n in names], *[deltas[n] for n in names],
            *[new_m[n] for n in names], *[new_v[n] for n in names])
```

```python
import math

import jax
import jax.numpy as jnp
from jax import lax
from jax.experimental import pallas as pl
from jax.experimental.pallas import tpu as pltpu

F32, BF16 = jnp.float32, jnp.bfloat16
MESH = pl.DeviceIdType.MESH

D = 1024
EPS = 1e-6
GLA_H, GLA_DK, GLA_DV, GLA_LR = 4, 128, 256, 16
GLA_TAU = 16.0
GLA_CHUNK = 64
GLA_BLOCK = 512
ATT_GROUPS = ((128, 1), (512, 4), (2048, 16))
ATT_BLK = 128
ATT_HD = 64
ATT_W = 768
D_FF = 2816
ROPE_THETA = 10000.0
P_W = 7680
P_GV, P_GR, P_MA, P_MB, P_GQ, P_GK, P_AQ, P_AK, P_AV, P_LR = 0, 1024, 2048, 3072, 4096, 4608, 5120, 5888, 6656, 7424
W_IN = 7440
W_IN_SH, W_UP_SH, W_DOWN_SH = 1860, 1408, 704
VMEM_LIMIT = 56 * 1024 * 1024
ADAM_LR, ADAM_B1, ADAM_B2, ADAM_EPS, ADAM_WD, ADAM_STEP = 0.001, 0.9, 0.999, 1e-08, 0.01, 10
NEG = -1e30


def _tile(n, target, unit=128):
    best = None
    for t in range(unit, min(n, target) + 1, unit):
        if n % t == 0:
            best = t
    return best or n


def _params(sem):
    return pltpu.CompilerParams(dimension_semantics=sem, vmem_limit_bytes=VMEM_LIMIT)


def _dg(a, b, ca, cb):
    return lax.dot_general(a, b, (((ca,), (cb,)), ((), ())), preferred_element_type=F32)


def _sigmoid(v):
    return 1.0 / (1.0 + jnp.exp(-v))


def _ff_block(j):
    return (j % 2) * 2 + j // 2


def _mm(a, b, name, *, ta=False, tb=False, out_dtype=BF16, tm=1024, tn=1536, tk=1024, n_outer=True, comm=(),
        b_shards=False, o_shards=False):
    m = a.shape[1] if ta else a.shape[0]
    k = a.shape[0] if ta else a.shape[1]
    if b_shards:
        n = b.shape[1] if tb else 4 * W_UP_SH
        tn, tk = (tn, W_UP_SH) if tb else (W_UP_SH, tk)
    else:
        n = b.shape[0] if tb else b.shape[1]
    if o_shards:
        tn = W_UP_SH
    tm, tn, tk = _tile(m, tm), _tile(n, tn), _tile(k, tk)
    nm, nn, nk = m // tm, n // tn, k // tk
    in_out = out_dtype == F32
    c_ins, c_outs, c_alias, c_scratch = _carry(comm, 2, 1)

    def body(a_ref, b_ref, *rest):
        ci, o_ref, co = rest[:len(c_ins)], rest[len(c_ins)], rest[len(c_ins) + 1:len(c_ins) + 1 + len(c_outs)]
        scr = rest[len(c_ins) + 1 + len(c_outs):]
        kk = pl.program_id(2)
        if comm:
            step = (pl.program_id(0) * (nm if n_outer else nn) + pl.program_id(1)) * nk + kk

            @pl.when(step == 0)
            def _():
                _comm_phase(comm, ci, co, scr[-2], scr[-1], True)

        _mm_step(a_ref, b_ref, o_ref, scr, kk)
        if comm:
            @pl.when(step == nm * nn * nk - 1)
            def _():
                _comm_phase(comm, ci, co, scr[-2], scr[-1], False)

    def _mm_step(a_ref, b_ref, o_ref, scr, kk):
        p = _dg(a_ref[...].astype(BF16), b_ref[...].astype(BF16), 0 if ta else 1, 1 if tb else 0)
        if nk == 1:
            o_ref[...] = p.astype(o_ref.dtype)
        else:
            acc = o_ref if in_out else scr[0]

            @pl.when(kk == 0)
            def _():
                acc[...] = p

            @pl.when(kk > 0)
            def _():
                acc[...] += p

            if not in_out:
                @pl.when(kk == nk - 1)
                def _():
                    o_ref[...] = acc[...].astype(o_ref.dtype)

    if n_outer:
        ij = lambda g0, g1: (g1, g0)
        grid = (nn, nm, nk)
    else:
        ij = lambda g0, g1: (g0, g1)
        grid = (nm, nn, nk)
    a_map = (lambda g0, g1, kk: (kk, ij(g0, g1)[0])) if ta else (lambda g0, g1, kk: (ij(g0, g1)[0], kk))
    if b_shards and tb:
        b_spec = pl.BlockSpec((None, tn, tk), lambda g0, g1, kk: (_ff_block(kk), ij(g0, g1)[1], 0))
    elif b_shards:
        b_spec = pl.BlockSpec((None, tk, tn), lambda g0, g1, kk: (_ff_block(ij(g0, g1)[1]), kk, 0))
    elif tb:
        b_spec = pl.BlockSpec((tn, tk), lambda g0, g1, kk: (ij(g0, g1)[1], kk))
    else:
        b_spec = pl.BlockSpec((tk, tn), lambda g0, g1, kk: (kk, ij(g0, g1)[1]))
    if o_shards:
        o_spec = pl.BlockSpec((None, tm, tn), lambda g0, g1, kk: (_ff_block(ij(g0, g1)[1]), ij(g0, g1)[0], 0))
        o_shape = jax.ShapeDtypeStruct((4, m, W_UP_SH), out_dtype)
    else:
        o_spec = pl.BlockSpec((tm, tn), lambda g0, g1, kk: ij(g0, g1))
        o_shape = jax.ShapeDtypeStruct((m, n), out_dtype)
    res = pl.pallas_call(
        body, name=name, grid=grid,
        in_specs=[pl.BlockSpec((tk, tm) if ta else (tm, tk), a_map), b_spec] + [HBM] * len(c_ins),
        out_specs=[o_spec] + [HBM] * len(c_outs),
        out_shape=[o_shape] + c_outs,
        scratch_shapes=([] if (in_out or nk == 1) else [pltpu.VMEM((tm, tn), F32)]) + c_scratch,
        input_output_aliases=c_alias,
        compiler_params=_params(("arbitrary",) * 3 if comm else ("parallel", "parallel", "arbitrary")),
    )(a, b, *c_ins)
    return (res[0], _split_units(comm, res[1:])) if comm else res[0]


def _rows(arr, rb, w=None, j=0):
    w = arr.shape[1] if w is None else w
    if callable(j):
        return arr, pl.BlockSpec((rb, w), lambda c, i: (i, j(c)))
    return arr, pl.BlockSpec((rb, w), lambda c, i: (i, j))


def _full(arr, w=None, j=0):
    w = arr.shape[1] if w is None else w
    if callable(j):
        return arr, pl.BlockSpec((arr.shape[0], w), lambda c, i: (0, j(c)))
    return arr, pl.BlockSpec((arr.shape[0], w), lambda c, i: (0, j))


def _halo(arr, rb, hb, w, j, before):
    per = rb // hb
    last = arr.shape[0] // hb - 1
    if before:
        rmap = lambda i: jnp.maximum(i * per - 1, 0)
    else:
        rmap = lambda i: jnp.minimum((i + 1) * per, last)
    return arr, pl.BlockSpec((hb, w), lambda c, i: (rmap(i), j(c) if callable(j) else j))


def _rowcall(fn, ins, outs, *, n_rows, rb, name, ncol=1, into=None, after=()):
    n_in = len(ins)
    nr = n_rows // rb
    unread = ([] if into is None else [into[0]]) + list(after)
    n_skip = len(unread)

    def body(*refs):
        c, i = pl.program_id(0), pl.program_id(1)
        res = fn(c, i, *[r[...] for r in refs[:n_in]])
        for val, spec, o_ref in zip(res, outs, refs[n_in + n_skip:]):
            if spec[2] == "row":
                o_ref[...] = val.astype(o_ref.dtype)
            else:
                @pl.when(i == 0)
                def _(o_ref=o_ref, val=val):
                    o_ref[...] = val.astype(o_ref.dtype)

                @pl.when(i > 0)
                def _(o_ref=o_ref, val=val):
                    o_ref[...] += val.astype(o_ref.dtype)

    out_specs = []
    for shape, dt, kind, block, col in outs:
        if kind == "row":
            out_specs.append(pl.BlockSpec(block, lambda c, i, col=col: (i, col(c))))
        else:
            out_specs.append(pl.BlockSpec(block, lambda c, i, col=col: (0, col(c))))
    return pl.pallas_call(
        body, name=name, grid=(ncol, nr),
        in_specs=[s for _, s in ins] + [pl.BlockSpec(memory_space=pl.ANY)] * n_skip, out_specs=out_specs,
        out_shape=[jax.ShapeDtypeStruct(o[0], o[1]) for o in outs],
        input_output_aliases={} if into is None else {n_in: into[1]},
        compiler_params=_params(("parallel", "arbitrary")),
    )(*[a for a, _ in ins], *unread)


def _orow(n_rows, w, dt, rb, bw=None, col=lambda c: 0):
    return ((n_rows, w), dt, "row", (rb, bw or w), col)


def _oacc(r, w, bw=None, col=lambda c: 0):
    return ((r, w), F32, "acc", (r, bw or w), col)


def _csum(v):
    return jnp.sum(v, axis=0, keepdims=True)


def _rms(v):
    return lax.rsqrt(jnp.mean(v * v, axis=-1, keepdims=True) + EPS)


def _norm_bwd(xv, dh, w, scale):
    r = _rms(xv)
    xh = xv * r
    dxh = dh * (w * (1.0 + scale))
    dx = r * (dxh - xh * jnp.mean(dxh * xh, axis=-1, keepdims=True))
    t = dh * xh
    return dx, _csum(dh), _csum(t * w), _csum(t * (1.0 + scale))


def _rope_tables(pos_col, invf, s):
    def fn(c, i, pos, f):
        ang = pos.astype(F32) * f
        lane = lax.broadcasted_iota(jnp.int32, ang.shape, 1)
        sign = jnp.where((lane % ATT_HD) < ATT_HD // 2, -1.0, 1.0)
        return jnp.cos(ang), jnp.sin(ang) * sign

    rb = 512
    return _rowcall(fn, [_rows(pos_col, rb), _full(invf)], [_orow(s, 128, F32, rb), _orow(s, 128, F32, rb)],
                    n_rows=s, rb=rb, name="rope_tables")


def _swap_halves(t):
    n = t.shape[1]
    lane = lax.broadcasted_iota(jnp.int32, t.shape, 1)
    return jnp.where((lane % ATT_HD) < ATT_HD // 2, pltpu.roll(t, n - 32, 1), pltpu.roll(t, 32, 1))


def _rope_apply(t, cos, sin_signed, inverse):
    cw = jnp.concatenate([cos] * (t.shape[1] // 128), axis=1)
    sw = jnp.concatenate([sin_signed] * (t.shape[1] // 128), axis=1)
    if inverse:
        sw = -sw
    return t * cw + _swap_halves(t) * sw


DIL_ROWS = 512


def _to_dilated(scr, val, out_ref, r):
    if r == 1:
        out_ref[...] = val.astype(out_ref.dtype)
        return
    n = val.shape[0] // r
    for hh in range(2):
        scr[hh] = val[:, hh * 128:(hh + 1) * 128]
        for pr in range(r):
            out_ref[:, pr * 256 + hh * 128:pr * 256 + (hh + 1) * 128] = scr[hh, pl.ds(pr, n, stride=r), :].astype(out_ref.dtype)


def _from_dilated(scr, in_ref, r):
    if r == 1:
        return in_ref[...].astype(F32)
    n = in_ref.shape[0]
    for hh in range(2):
        for pr in range(r):
            scr[hh, pl.ds(pr, n, stride=r), :] = in_ref[:, pr * 256 + hh * 128:pr * 256 + (hh + 1) * 128].astype(F32)
    return jnp.concatenate([scr[0], scr[1]], axis=1)


def _dil_spec(r):
    return pl.BlockSpec((DIL_ROWS // r, r * 256), lambda i: (i, 0))


def _dil_shape(s, r, dt):
    return jax.ShapeDtypeStruct((s // r, r * 256), dt)


_DIL_SCRATCH = [pltpu.VMEM((2, DIL_ROWS, 128), F32)]
_RS = tuple(r for _, r in ATT_GROUPS)


def _rope_fwd(p, cos_t, sin_t, s):
    def body(*refs):
        ins, cs, sn, outs, scr = refs[:9], refs[9][...], refs[10][...], refs[11:20], refs[20]
        for t in range(3):
            for g, r in enumerate(_RS):
                val = ins[3 * t + g][...].astype(F32)
                _to_dilated(scr, _rope_apply(val, cs, sn, False) if t < 2 else val, outs[3 * t + g], r)

    res = pl.pallas_call(
        body, name="rope", grid=(s // DIL_ROWS,),
        in_specs=[pl.BlockSpec((DIL_ROWS, 256), lambda i, c=base // 256 + g: (i, c)) for base in (P_AQ, P_AK, P_AV) for g in range(3)]
        + [pl.BlockSpec((DIL_ROWS, 128), lambda i: (i, 0))] * 2,
        out_specs=[_dil_spec(r) for _ in range(3) for r in _RS],
        out_shape=[_dil_shape(s, r, BF16) for _ in range(3) for r in _RS],
        scratch_shapes=_DIL_SCRATCH, compiler_params=_params(("parallel",)),
    )(*([p] * 9), cos_t, sin_t)
    return res[0:3], res[3:6], res[6:9]


def _attn_combine(att, s):
    def body(o0, o1, o2, l0, l1, l2, o_ref, lse_ref, od1, od2, ld1, ld2, scr):
        ov = [_from_dilated(scr, ref, r) for ref, r in zip((o0, o1, o2), _RS)]
        lv = [_from_dilated(scr, ref, r) for ref, r in zip((l0, l1, l2), _RS)]
        mx = jnp.maximum(jnp.maximum(lv[0], lv[1]), lv[2])
        ev = [jnp.exp(l - mx) for l in lv]
        z = ev[0] + ev[1] + ev[2]
        o = ((ev[0] * ov[0] + ev[1] * ov[1] + ev[2] * ov[2]) / z).astype(BF16)
        lse = mx + jnp.log(z)
        o_ref[...] = o
        lse_ref[...] = lse
        for ref, r in zip((od1, od2), _RS[1:]):
            _to_dilated(scr, o.astype(F32), ref, r)
        for ref, r in zip((ld1, ld2), _RS[1:]):
            _to_dilated(scr, lse, ref, r)

    return pl.pallas_call(
        body, name="attn_combine", grid=(s // DIL_ROWS,),
        in_specs=[_dil_spec(r) for r in _RS] * 2,
        out_specs=[_dil_spec(1)] * 2 + [_dil_spec(r) for r in _RS[1:]] * 2,
        out_shape=[_dil_shape(s, 1, BF16), _dil_shape(s, 1, F32)] + [_dil_shape(s, r, BF16) for r in _RS[1:]]
        + [_dil_shape(s, r, F32) for r in _RS[1:]],
        scratch_shapes=_DIL_SCRATCH, compiler_params=_params(("parallel",)),
    )(*[a[0] for a in att], *[a[1] for a in att])


def _dilate(t, s):
    def body(t_ref, o1, o2, scr):
        val = t_ref[...].astype(F32)
        for ref, r in zip((o1, o2), _RS[1:]):
            _to_dilated(scr, val, ref, r)

    return pl.pallas_call(
        body, name="attn_dilate", grid=(s // DIL_ROWS,), in_specs=[_dil_spec(1)], out_specs=[_dil_spec(r) for r in _RS[1:]],
        out_shape=[_dil_shape(s, r, t.dtype) for r in _RS[1:]], scratch_shapes=_DIL_SCRATCH, compiler_params=_params(("parallel",)),
    )(t)


def _rope_bwd(datt, d_glr, dp, cos_t, sin_t, s):
    tail = P_W - P_AQ

    def body(*refs):
        ins, cs, sn, glr, o_ref, scr = refs[:9], refs[9][...], refs[10][...], refs[11], refs[13], refs[14]
        for t in range(3):
            for g, r in enumerate(_RS):
                val = _from_dilated(scr, ins[3 * t + g], r)
                o_ref[:, t * ATT_W + g * 256:t * ATT_W + (g + 1) * 256] = (_rope_apply(val, cs, sn, True) if t < 2 else val).astype(BF16)
        o_ref[:, 3 * ATT_W:3 * ATT_W + 128] = glr[...]
        o_ref[:, 3 * ATT_W + 128:] = jnp.zeros((DIL_ROWS, tail - 3 * ATT_W - 128), BF16)

    return pl.pallas_call(
        body, name="rope_bwd", grid=(s // DIL_ROWS,),
        in_specs=[_dil_spec(r) for _ in range(3) for r in _RS] + [pl.BlockSpec((DIL_ROWS, 128), lambda i: (i, 0))] * 3
        + [pl.BlockSpec(memory_space=pl.ANY)],
        out_specs=pl.BlockSpec((DIL_ROWS, tail), lambda i: (i, P_AQ // tail)),
        out_shape=jax.ShapeDtypeStruct((s, P_W), BF16), input_output_aliases={12: 0},
        scratch_shapes=_DIL_SCRATCH, compiler_params=_params(("parallel",)),
    )(*[datt[g][t] for t in range(3) for g in range(3)], cos_t, sin_t, d_glr, dp)


def _tri_dot(tri, t):
    tb = tri.astype(BF16)
    hi = t.astype(BF16)
    r1 = t - hi.astype(F32)
    mid = r1.astype(BF16)
    lo = (r1 - mid.astype(F32)).astype(BF16)
    return _dg(tb, hi, 1, 0) + _dg(tb, mid, 1, 0) + _dg(tb, lo, 1, 0)


def _gla_decays(la_c, tri):
    b = _tri_dot(tri, la_c)
    row = lax.broadcasted_iota(jnp.int32, b.shape, 0)
    bmid = jnp.sum(jnp.where(row == GLA_CHUNK // 2 - 1, b, 0.0), axis=0, keepdims=True)
    blast = jnp.sum(jnp.where(row == GLA_CHUNK - 1, b, 0.0), axis=0, keepdims=True)
    return b, bmid, blast


def _gla_fwd(p, la, s, comm=()):
    tb, ch = GLA_BLOCK, GLA_CHUNK
    nb, nc = s // tb, tb // ch
    scale = GLA_DK ** -0.5
    c_ins, c_outs, c_alias, c_scratch = _carry(comm, 4, 2)

    def body(q_ref, k_ref, v_ref, la_ref, *rest):
        ci, (o_ref, st_ref) = rest[:len(c_ins)], rest[len(c_ins):len(c_ins) + 2]
        co, state = rest[len(c_ins) + 2:len(c_ins) + 2 + len(c_outs)], rest[len(c_ins) + 2 + len(c_outs)]
        step = pl.program_id(0)
        if comm:
            @pl.when(step == 0)
            def _():
                _comm_phase(comm, ci, co, rest[-2], rest[-1], True)

        _gla_fwd_step(q_ref, k_ref, v_ref, la_ref, o_ref, st_ref, state)
        if comm:
            @pl.when(step == nb - 1)
            def _():
                _comm_phase(comm, ci, co, rest[-2], rest[-1], False)

    def _gla_fwd_step(q_ref, k_ref, v_ref, la_ref, o_ref, st_ref, state):
        @pl.when(pl.program_id(0) == 0)
        def _():
            state[...] = jnp.zeros_like(state)

        ri = lax.broadcasted_iota(jnp.int32, (ch, ch), 0)
        ci = lax.broadcasted_iota(jnp.int32, (ch, ch), 1)
        causal = ci <= ri
        tri = causal.astype(F32)

        def chunk(c, carry):
            sl = pl.ds(pl.multiple_of(c * ch, ch), ch)
            b, bmid, blast = _gla_decays(la_ref[sl, :], tri)
            q = q_ref[sl, :].astype(F32) * scale
            k = k_ref[sl, :].astype(F32)
            v = v_ref[sl, :]
            qgt = (q * jnp.exp(b)).astype(BF16)
            qgn = (q * jnp.exp(b - bmid)).astype(BF16)
            kgn = (k * jnp.exp(bmid - b)).astype(BF16)
            kd = (k * jnp.exp(blast - b)).astype(BF16)
            dec = jnp.exp(blast)
            sts = [state[h] for h in range(GLA_H)]
            outs, news = [], []
            for h in range(GLA_H):
                hk, hv = slice(h * GLA_DK, (h + 1) * GLA_DK), slice(h * GLA_DV, (h + 1) * GLA_DV)
                a = jnp.where(causal, _dg(qgn[:, hk], kgn[:, hk], 1, 1), 0.0)
                outs.append(_dg(a.astype(BF16), v[:, hv], 1, 0) + _dg(qgt[:, hk], sts[h].astype(BF16), 1, 1))
                news.append(dec[:, hk] * sts[h] + _dg(v[:, hv], kd[:, hk], 0, 0))
            for h in range(GLA_H):
                st_ref[h, c] = sts[h]
                state[h] = news[h]
            o_ref[sl, :] = jnp.concatenate(outs, axis=1)
            return carry

        lax.fori_loop(0, nc, chunk, 0)

    hw = GLA_H * GLA_DK
    res = pl.pallas_call(
        body, name="gla_fwd", grid=(nb,),
        in_specs=[pl.BlockSpec((tb, hw), lambda t: (t, P_GQ // hw)),
                  pl.BlockSpec((tb, hw), lambda t: (t, P_GK // hw)),
                  pl.BlockSpec((tb, GLA_H * GLA_DV), lambda t: (t, P_GV // (GLA_H * GLA_DV))),
                  pl.BlockSpec((tb, hw), lambda t: (t, 0))] + [HBM] * len(c_ins),
        out_specs=[pl.BlockSpec((tb, GLA_H * GLA_DV), lambda t: (t, 0)),
                   pl.BlockSpec((GLA_H, nc, GLA_DV, GLA_DK), lambda t: (0, t, 0, 0))] + [HBM] * len(c_outs),
        out_shape=[jax.ShapeDtypeStruct((s, GLA_H * GLA_DV), F32),
                   jax.ShapeDtypeStruct((GLA_H, s // ch, GLA_DV, GLA_DK), F32)] + c_outs,
        scratch_shapes=[pltpu.VMEM((GLA_H, GLA_DV, GLA_DK), F32)] + c_scratch,
        input_output_aliases=c_alias,
        compiler_params=_params(("arbitrary",)),
    )(p, p, p, la, *c_ins)
    return res[0], res[1], _split_units(comm, res[2:])


def _gla_bwd(p, la, states, do, s, dp, comm=()):
    tb, ch = GLA_BLOCK, GLA_CHUNK
    nb, nc = s // tb, tb // ch
    scale = GLA_DK ** -0.5
    c_ins, c_outs, c_alias, c_scratch = _carry(comm, 7, 4)

    def body(q_ref, k_ref, v_ref, la_ref, st_ref, do_ref, dp_in, *rest):
        ci, outs = rest[:len(c_ins)], rest[len(c_ins):len(c_ins) + 4]
        co, dstate = rest[len(c_ins) + 4:len(c_ins) + 4 + len(c_outs)], rest[len(c_ins) + 4 + len(c_outs)]
        step = pl.program_id(0)
        if comm:
            @pl.when(step == 0)
            def _():
                _comm_phase(comm, ci, co, rest[-2], rest[-1], True)

        _gla_bwd_step(q_ref, k_ref, v_ref, la_ref, st_ref, do_ref, *outs, dstate)
        if comm:
            @pl.when(step == nb - 1)
            def _():
                _comm_phase(comm, ci, co, rest[-2], rest[-1], False)

    def _gla_bwd_step(q_ref, k_ref, v_ref, la_ref, st_ref, do_ref, dq_ref, dk_ref, dv_ref, dla_ref, dstate):
        @pl.when(pl.program_id(0) == 0)
        def _():
            dstate[...] = jnp.zeros_like(dstate)

        ri = lax.broadcasted_iota(jnp.int32, (ch, ch), 0)
        ci = lax.broadcasted_iota(jnp.int32, (ch, ch), 1)
        causal = ci <= ri
        tri = causal.astype(F32)
        tri_t = (ci >= ri).astype(F32)

        def chunk(cc, carry):
            c = nc - 1 - cc
            sl = pl.ds(pl.multiple_of(c * ch, ch), ch)
            b, bmid, blast = _gla_decays(la_ref[sl, :], tri)
            q = q_ref[sl, :].astype(F32) * scale
            k = k_ref[sl, :].astype(F32)
            v = v_ref[sl, :]
            e_b, e_qn, e_kn, e_kd = jnp.exp(b), jnp.exp(b - bmid), jnp.exp(bmid - b), jnp.exp(blast - b)
            dec = jnp.exp(blast)
            qgt, qgn, kgn, kd = q * e_b, q * e_qn, k * e_kn, k * e_kd
            qgt_b, qgn_b, kgn_b, kd_b = qgt.astype(BF16), qgn.astype(BF16), kgn.astype(BF16), kd.astype(BF16)
            do_b = do_ref[sl, :].astype(BF16)
            st0s = [st_ref[h, c] for h in range(GLA_H)]
            dsts = [dstate[h] for h in range(GLA_H)]
            dqgn, dqgt, dkgn, dkd, dvs, ddec, news = [], [], [], [], [], [], []
            for h in range(GLA_H):
                hk, hv = slice(h * GLA_DK, (h + 1) * GLA_DK), slice(h * GLA_DV, (h + 1) * GLA_DV)
                dst_b = dsts[h].astype(BF16)
                a = jnp.where(causal, _dg(qgn_b[:, hk], kgn_b[:, hk], 1, 1), 0.0).astype(BF16)
                da = jnp.where(causal, _dg(do_b[:, hv], v[:, hv], 1, 1), 0.0).astype(BF16)
                dqgn.append(_dg(da, kgn_b[:, hk], 1, 0))
                dqgt.append(_dg(do_b[:, hv], st0s[h].astype(BF16), 1, 0))
                dkgn.append(_dg(da, qgn_b[:, hk], 0, 0))
                dvs.append(_dg(a, do_b[:, hv], 0, 0) + _dg(kd_b[:, hk], dst_b, 1, 1))
                dkd.append(_dg(v[:, hv], dst_b, 1, 0))
                ddec.append(jnp.sum(st0s[h] * dsts[h], axis=0, keepdims=True))
                news.append(dec[:, hk] * dsts[h] + _dg(do_b[:, hv], qgt_b[:, hk], 0, 0))
            for h in range(GLA_H):
                dstate[h] = news[h]
            cat = lambda parts: jnp.concatenate(parts, axis=1)
            dqgn, dqgt, dkgn, dkd, ddec = cat(dqgn), cat(dqgt), cat(dkgn), cat(dkd), cat(ddec)
            dq_ref[sl, :] = (scale * (dqgn * e_qn + dqgt * e_b)).astype(dq_ref.dtype)
            dk_ref[sl, :] = (dkgn * e_kn + dkd * e_kd).astype(dk_ref.dtype)
            dv_ref[sl, :] = cat(dvs).astype(dv_ref.dtype)
            db = dqgn * qgn + dqgt * qgt - dkgn * kgn - dkd * kd
            extra = jnp.sum(dkd * kd, axis=0, keepdims=True) + ddec * dec
            dla_ref[sl, :] = _tri_dot(tri_t, db) + extra
            return carry

        lax.fori_loop(0, nc, chunk, 0)

    rev = lambda t: nb - 1 - t
    hw, vw = GLA_H * GLA_DK, GLA_H * GLA_DV
    res = pl.pallas_call(
        body, name="gla_bwd", grid=(nb,),
        in_specs=[pl.BlockSpec((tb, hw), lambda t: (rev(t), P_GQ // hw)),
                  pl.BlockSpec((tb, hw), lambda t: (rev(t), P_GK // hw)),
                  pl.BlockSpec((tb, vw), lambda t: (rev(t), P_GV // vw)),
                  pl.BlockSpec((tb, hw), lambda t: (rev(t), 0)),
                  pl.BlockSpec((GLA_H, nc, GLA_DV, GLA_DK), lambda t: (0, rev(t), 0, 0)),
                  pl.BlockSpec((tb, vw), lambda t: (rev(t), 0)), pl.BlockSpec(memory_space=pl.ANY)] + [HBM] * len(c_ins),
        out_specs=[pl.BlockSpec((tb, hw), lambda t: (rev(t), 0)),
                   pl.BlockSpec((tb, hw), lambda t: (rev(t), 0)),
                   pl.BlockSpec((tb, vw), lambda t: (rev(t), P_GV // vw)),
                   pl.BlockSpec((tb, hw), lambda t: (rev(t), 0))] + [HBM] * len(c_outs),
        out_shape=[jax.ShapeDtypeStruct((s, hw), BF16),
                   jax.ShapeDtypeStruct((s, hw), BF16),
                   jax.ShapeDtypeStruct((s, P_W), BF16),
                   jax.ShapeDtypeStruct((s, hw), F32)] + c_outs,
        scratch_shapes=[pltpu.VMEM((GLA_H, GLA_DV, GLA_DK), F32)] + c_scratch,
        input_output_aliases={6: 2, **c_alias},
        compiler_params=_params(("arbitrary",)),
    )(p, p, p, la, states, do, dp, *c_ins)
    return res[0], res[1], res[2], res[3], _split_units(comm, res[4:])


def _head_masks():
    lane = lax.broadcasted_iota(jnp.int32, (1, 4 * ATT_HD), 1)
    return [(lane >= h * ATT_HD) & (lane < (h + 1) * ATT_HD) for h in range(4)]


def _attn_fwd(qv, kv, pv, g, r, s):
    ln = s // r
    nblk = ln // ATT_BLK
    qcol = lambda pr: pr
    vcol = qcol
    prev = lambda n: jnp.maximum(n - 1, 0)

    def body(q_ref, kp_ref, kc_ref, vp_ref, vc_ref, o_ref, lse_ref):
        has_prev = pl.program_id(1) > 0
        ri = lax.broadcasted_iota(jnp.int32, (ATT_BLK, ATT_BLK), 0)
        ci = lax.broadcasted_iota(jnp.int32, (ATT_BLK, ATT_BLK), 1)
        m_cur = ci <= ri
        m_prev = (ci >= ri) & has_prev
        q, kp, kc, vp, vc = q_ref[...], kp_ref[...], kc_ref[...], vp_ref[...], vc_ref[...]
        o = jnp.zeros((ATT_BLK, 256), F32)
        lse = jnp.zeros((ATT_BLK, 256), F32)
        for hm in _head_masks():
            qm = jnp.where(hm, q, jnp.zeros_like(q))
            sc = jnp.where(m_cur, _dg(qm, kc, 1, 1) * 0.125, NEG)
            sp = jnp.where(m_prev, _dg(qm, kp, 1, 1) * 0.125, NEG)
            mx = jnp.maximum(jnp.max(sc, axis=1, keepdims=True), jnp.max(sp, axis=1, keepdims=True))
            pc, pp = jnp.exp(sc - mx), jnp.exp(sp - mx)
            den = jnp.sum(pc, axis=1, keepdims=True) + jnp.sum(pp, axis=1, keepdims=True)
            oh = (_dg(pc.astype(BF16), vc, 1, 0) + _dg(pp.astype(BF16), vp, 1, 0)) / den
            o = jnp.where(hm, oh, o)
            lse = jnp.where(hm, mx + jnp.log(den), lse)
        o_ref[...] = o.astype(o_ref.dtype)
        lse_ref[...] = lse

    blk = (ATT_BLK, 256)
    o, lse = pl.pallas_call(
        body, name=f"attn_fwd_{g}", grid=(r, nblk),
        in_specs=[pl.BlockSpec(blk, lambda pr, n: (n, qcol(pr))),
                  pl.BlockSpec(blk, lambda pr, n: (prev(n), qcol(pr))),
                  pl.BlockSpec(blk, lambda pr, n: (n, qcol(pr))),
                  pl.BlockSpec(blk, lambda pr, n: (prev(n), vcol(pr))),
                  pl.BlockSpec(blk, lambda pr, n: (n, vcol(pr)))],
        out_specs=[pl.BlockSpec(blk, lambda pr, n: (n, pr)), pl.BlockSpec(blk, lambda pr, n: (n, pr))],
        out_shape=[jax.ShapeDtypeStruct((ln, r * 256), BF16), jax.ShapeDtypeStruct((ln, r * 256), F32)],
        compiler_params=_params(("parallel", "parallel")),
    )(qv, kv, kv, pv, pv)
    return o, lse


def _attn_bwd(qv, kv, pv, dov, ov, lv, g, r, s):
    ln = s // r
    nblk = ln // ATT_BLK
    qcol = lambda pr: pr
    vcol = qcol
    prev = lambda n: jnp.maximum(n - 1, 0)
    nxt = lambda n: jnp.minimum(n + 1, nblk - 1)

    def body(qc_ref, qn_ref, kp_ref, kc_ref, vp_ref, vc_ref, doc_ref, don_ref, oc_ref, on_ref, lc_ref, ln_ref,
             dq_ref, dk_ref, dv_ref):
        n = pl.program_id(1)
        has_prev, has_next = n > 0, n < nblk - 1
        ri = lax.broadcasted_iota(jnp.int32, (ATT_BLK, ATT_BLK), 0)
        ci = lax.broadcasted_iota(jnp.int32, (ATT_BLK, ATT_BLK), 1)
        m_cur = ci <= ri
        m_prev = (ci >= ri) & has_prev
        m_next = (ci >= ri) & has_next
        qc, qn, kp, kc, vp, vc = qc_ref[...], qn_ref[...], kp_ref[...], kc_ref[...], vp_ref[...], vc_ref[...]
        doc, don = doc_ref[...], don_ref[...]
        pc_full = doc.astype(F32) * oc_ref[...].astype(F32)
        pn_full = don.astype(F32) * on_ref[...].astype(F32)
        lc, lnx = lc_ref[...], ln_ref[...]
        dq = jnp.zeros((ATT_BLK, 256), F32)
        dk = jnp.zeros((ATT_BLK, 256), F32)
        dv = jnp.zeros((ATT_BLK, 256), F32)
        zb = jnp.zeros_like(qc)
        for hm in _head_masks():
            qcm, qnm = jnp.where(hm, qc, zb), jnp.where(hm, qn, zb)
            docm, donm = jnp.where(hm, doc, zb), jnp.where(hm, don, zb)
            lse_c = jnp.max(jnp.where(hm, lc, NEG), axis=1, keepdims=True)
            lse_n = jnp.max(jnp.where(hm, lnx, NEG), axis=1, keepdims=True)
            del_c = jnp.sum(jnp.where(hm, pc_full, 0.0), axis=1, keepdims=True)
            del_n = jnp.sum(jnp.where(hm, pn_full, 0.0), axis=1, keepdims=True)
            pr_ = jnp.where(m_cur, jnp.exp(_dg(qcm, kc, 1, 1) * 0.125 - lse_c), 0.0)
            ds = (pr_ * (_dg(docm, vc, 1, 1) - del_c) * 0.125).astype(BF16)
            dqh = _dg(ds, kc, 1, 0)
            dkh = _dg(ds, qc, 0, 0)
            dvh = _dg(pr_.astype(BF16), doc, 0, 0)
            pr_ = jnp.where(m_prev, jnp.exp(_dg(qcm, kp, 1, 1) * 0.125 - lse_c), 0.0)
            ds = (pr_ * (_dg(docm, vp, 1, 1) - del_c) * 0.125).astype(BF16)
            dqh = dqh + _dg(ds, kp, 1, 0)
            pr_ = jnp.where(m_next, jnp.exp(_dg(qnm, kc, 1, 1) * 0.125 - lse_n), 0.0)
            ds = (pr_ * (_dg(donm, vc, 1, 1) - del_n) * 0.125).astype(BF16)
            dkh = dkh + _dg(ds, qn, 0, 0)
            dvh = dvh + _dg(pr_.astype(BF16), don, 0, 0)
            dq = jnp.where(hm, dqh, dq)
            dk = jnp.where(hm, dkh, dk)
            dv = jnp.where(hm, dvh, dv)
        dq_ref[...] = dq.astype(dq_ref.dtype)
        dk_ref[...] = dk.astype(dk_ref.dtype)
        dv_ref[...] = dv.astype(dv_ref.dtype)

    blk = (ATT_BLK, 256)
    cur = lambda col: pl.BlockSpec(blk, lambda pr, n: (n, col(pr)))
    prv = lambda col: pl.BlockSpec(blk, lambda pr, n: (prev(n), col(pr)))
    nx = lambda col: pl.BlockSpec(blk, lambda pr, n: (nxt(n), col(pr)))
    own = lambda pr: pr
    outs = pl.pallas_call(
        body, name=f"attn_bwd_{g}", grid=(r, nblk),
        in_specs=[cur(qcol), nx(qcol), prv(qcol), cur(qcol), prv(vcol), cur(vcol),
                  cur(own), nx(own), cur(own), nx(own), cur(own), nx(own)],
        out_specs=[cur(own), cur(own), cur(own)],
        out_shape=[jax.ShapeDtypeStruct((ln, r * 256), BF16)] * 3,
        compiler_params=_params(("parallel", "parallel")),
    )(qv, qv, kv, kv, pv, pv, dov, dov, ov, ov, lv, lv)
    return outs


def _gelu_parts(gv):
    cdf = 0.5 * (1.0 + lax.erf(gv * (2.0 ** -0.5)))
    pdf = jnp.exp(-0.5 * gv * gv) * (1.0 / math.sqrt(2.0 * math.pi))
    return cdf, pdf


def _pick_row(t, k):
    row = lax.broadcasted_iota(jnp.int32, t.shape, 0)
    return jnp.sum(jnp.where(row == k, t, 0.0), axis=0, keepdims=True)


def _shift_rows(u, halo, n):
    row = lax.broadcasted_iota(jnp.int32, u.shape, 0)
    out = pltpu.roll(u, n, 0)
    for k in range(n):
        out = jnp.where(row == k, _pick_row(halo, 16 - n + k), out)
    return out


def _shift_rows_up(u, halo, n):
    rb = u.shape[0]
    row = lax.broadcasted_iota(jnp.int32, u.shape, 0)
    out = pltpu.roll(u, rb - n, 0)
    for k in range(n):
        out = jnp.where(row == rb - n + k, _pick_row(halo, k), out)
    return out


def _conv(u, halo, cw, cb):
    return cb + _pick_row(cw, 0) * _shift_rows(u, halo, 2) + _pick_row(cw, 1) * _shift_rows(u, halo, 1) + _pick_row(cw, 2) * u


def _local_step(x, mod, pos_col, target, sm, w_sh, g0, chip, core):
    s = x.shape[0]
    shift1, scale1, gate1, shift2, scale2, gate2 = [mod[i:i + 1, :] for i in range(6)]
    rb = 256
    chip1 = chip.reshape(1)

    def f_norm1(c, i, xv, nw, sc, sh):
        return ((xv * _rms(xv) * nw) * (1.0 + sc) + sh,)

    (h,) = _rowcall(f_norm1, [_rows(x, rb), _full(sm["n1w"]), _full(scale1), _full(shift1)],
                    [_orow(s, D, BF16, rb)], n_rows=s, rb=rb, name="norm1")
    own = lambda got, i: lax.dynamic_update_slice(got, w_sh[i], (chip, 0, 0))
    invf = jnp.tile(ROPE_THETA ** (-jnp.arange(ATT_HD // 2, dtype=F32) / (ATT_HD // 2)), 4).reshape(1, 128)
    cos_t, sin_t = _rope_tables(pos_col, invf, s)
    _, got0 = _unit_wait(*g0[:4], after=[h, cos_t, sin_t], name="gather_w_in_wait")
    [got0] = _comm_call("gather_w_in_d2d", [_u_gather_d2d(got0, (0,))])
    w = dict(win=_win_assemble(own(got0[0], 0)))
    p, [got123, got4] = _mm(h, w["win"], "in_proj", tm=1024, tn=1536,
                            comm=[_u_gather_ici(w_sh, (1, 2, 3)), _u_gather_ici(w_sh, (4,))])

    def f_gla_pre(c, i, glr, w2, gb):
        z = _dg(glr, w2.astype(BF16), 1, 0) + gb
        return ((jnp.minimum(z, 0.0) - jnp.log(1.0 + jnp.exp(-jnp.abs(z)))) * (1.0 / GLA_TAU),)

    (la,) = _rowcall(f_gla_pre, [_rows(p, rb, 128, P_LR // 128), _full(sm["w2"]), _full(sm["gb"])],
                     [_orow(s, 512, F32, rb)], n_rows=s, rb=rb, name="gla_pre")
    o_gla, states, [got123, got5] = _gla_fwd(p, la, s, comm=[_u_gather_d2d(got123, (1, 2, 3)), _u_gather_ici(w_sh, (5,))])
    got45 = got4 + got5
    w.update(wgb=own(got123[0], 1).reshape(1024, D), wab=_cols_join(own(got123[1], 2)), wout=own(got123[2], 3).reshape(D, D))

    def f_gla_post(c, i, ov, gnw, gr):
        on = jnp.concatenate([ov[:, k * 256:(k + 1) * 256] * _rms(ov[:, k * 256:(k + 1) * 256]) * gnw
                              for k in range(GLA_H)], axis=1)
        g = gr.astype(F32)
        return (on * (g * _sigmoid(g)),)

    (og,) = _rowcall(f_gla_post, [_rows(o_gla, rb), _full(sm["gnw"]), _rows(p, rb, 1024, P_GR // 1024)],
                     [_orow(s, 1024, BF16, rb)], n_rows=s, rb=rb, name="gla_post")
    y_gla = _mm(og, w["wgb"], "gla_branch")

    q_d, k_d, v_d = _rope_fwd(p, cos_t, sin_t, s)
    att = [_attn_fwd(q_d[g], k_d[g], v_d[g], g, r, s) for g, r in enumerate(_RS)]
    o_att, lse, o_d1, o_d2, lse_d1, lse_d2 = _attn_combine(att, s)
    y_att = _mm(o_att, w["wab"], "attn_branch")

    def f_merge(c, i, ma, mb, yg, ya):
        return (_sigmoid(ma.astype(F32)) * yg.astype(F32) + _sigmoid(mb.astype(F32)) * ya.astype(F32),)

    (mixed,) = _rowcall(f_merge, [_rows(p, rb, D, P_MA // D), _rows(p, rb, D, P_MB // D), _rows(y_gla, rb), _rows(y_att, rb)],
                        [_orow(s, D, BF16, rb)], n_rows=s, rb=rb, name="merge")
    z1, [got45] = _mm(mixed, w["wout"], "out_proj", comm=[_u_gather_d2d(got45, (4, 5))])
    w.update(wup=own(got45[0], 4), wdown=own(got45[1], 5).reshape(D_FF, D))

    def f_norm2(c, i, xv, z, g1, nw, sc, sh):
        x1 = xv + g1 * z.astype(F32)
        return (x1, (x1 * _rms(x1) * nw) * (1.0 + sc) + sh)

    x1, h2 = _rowcall(f_norm2, [_rows(x, rb), _rows(z1, rb), _full(gate1), _full(sm["n2w"]), _full(scale2), _full(shift2)],
                      [_orow(s, D, F32, rb), _orow(s, D, BF16, rb)], n_rows=s, rb=rb, name="norm2")
    u = _mm(h2, w["wup"], "up_proj", b_shards=True)

    cwid = 2 * W_UP_SH

    def f_ffn(c, i, uv, hl, cw, cb):
        uc = _conv(uv.astype(F32), hl.astype(F32) * (i > 0).astype(F32), cw, cb)
        val, gt = uc[:, :W_UP_SH], uc[:, W_UP_SH:]
        cdf, _ = _gelu_parts(gt)
        return (gt * cdf * val,)

    ccol = lambda c: c
    (hidden,) = _rowcall(f_ffn, [_rows(u, rb, cwid, ccol), _halo(u, rb, 16, cwid, ccol, True),
                                 _full(sm["cw"], cwid, ccol), _full(sm["cb"], cwid, ccol)],
                         [_orow(s, D_FF, BF16, rb, W_UP_SH, ccol)], n_rows=s, rb=rb, name="conv_geglu", ncol=2)
    z2 = _mm(hidden, w["wdown"], "down_proj", tk=D_FF)

    def f_final(c, i, x1v, z, g2, fw, tgt):
        x2 = x1v + g2 * z.astype(F32)
        r = _rms(x2)
        xh = x2 * r
        e = xh * fw - tgt
        loss = 0.5 * jnp.sum(jnp.mean(e * e, axis=-1, keepdims=True), axis=0, keepdims=True)
        dy = e * (1.0 / D)
        dxh = dy * fw
        dx2 = r * (dxh - xh * jnp.mean(dxh * xh, axis=-1, keepdims=True))
        return (loss, dx2, dx2 * g2, _csum(dy * xh), _csum(dx2 * z.astype(F32)))

    loss, dx2, dz2, d_fnw, d_gate2 = _rowcall(
        f_final, [_rows(x1, rb), _rows(z2, rb), _full(gate2), _full(sm["fnw"]), _rows(target, rb)],
        [_oacc(1, 1), _orow(s, D, F32, rb), _orow(s, D, BF16, rb), _oacc(1, D), _oacc(1, D)],
        n_rows=s, rb=rb, name="final_loss")
    d_hidden = _mm(dz2, w["wdown"], "down_proj_dx", tb=True, tn=1408)
    g_wdown = _mm(hidden, dz2, "down_proj_dw", ta=True, out_dtype=F32, tm=1408, tn=1024, tk=2048)

    def f_ffn_bwd(c, i, uv, hl, dh, cw, cb):
        uf = uv.astype(F32)
        hf = hl.astype(F32) * (i > 0).astype(F32)
        u1, u2 = _shift_rows(uf, hf, 1), _shift_rows(uf, hf, 2)
        uc = cb + _pick_row(cw, 0) * u2 + _pick_row(cw, 1) * u1 + _pick_row(cw, 2) * uf
        val, gt = uc[:, :W_UP_SH], uc[:, W_UP_SH:]
        cdf, pdf = _gelu_parts(gt)
        dhf = dh.astype(F32)
        duc = jnp.concatenate([dhf * (gt * cdf), dhf * val * (cdf + gt * pdf)], axis=1)
        dcw = jnp.concatenate([_csum(duc * u2), _csum(duc * u1), _csum(duc * uf)], axis=0)
        return (duc, _csum(duc), dcw)

    duc, d_cb, d_cw = _rowcall(
        f_ffn_bwd, [_rows(u, rb, cwid, ccol), _halo(u, rb, 16, cwid, ccol, True), _rows(d_hidden, rb, W_UP_SH, ccol),
                    _full(sm["cw"], cwid, ccol), _full(sm["cb"], cwid, ccol)],
        [_orow(s, 2 * D_FF, BF16, rb, cwid, ccol), _oacc(1, 2 * D_FF, cwid, ccol), _oacc(3, 2 * D_FF, cwid, ccol)],
        n_rows=s, rb=rb, name="conv_geglu_bwd", ncol=2)

    def f_conv_t(c, i, dv, hl, cw):
        df = dv.astype(F32)
        hf = hl.astype(F32) * (i < s // rb - 1).astype(F32)
        return (_pick_row(cw, 2) * df + _pick_row(cw, 1) * _shift_rows_up(df, hf, 1) + _pick_row(cw, 0) * _shift_rows_up(df, hf, 2),)

    (du,) = _rowcall(f_conv_t, [_rows(duc, rb, cwid, ccol), _halo(duc, rb, 16, cwid, ccol, False), _full(sm["cw"], cwid, ccol)],
                     [_orow(s, 2 * D_FF, BF16, rb, cwid, ccol)], n_rows=s, rb=rb, name="conv_transpose", ncol=2)
    g_wup = _mm(h2, du, "up_proj_dw", ta=True, out_dtype=F32, tm=1024, tk=2048, o_shards=True)
    gs45 = [g_wup, g_wdown.reshape(4, W_DOWN_SH, 1024)]
    d_h2, [land45] = _mm(du, w["wup"], "up_proj_dx", tb=True, b_shards=True, comm=[_u_pair_send(gs45, (4, 5))])
    ts45 = [_pair_add(g, ld, core, "grad_pair_add_" + BIG[i]) for g, ld, i in zip(gs45, land45, (4, 5))]

    def f_norm2_bwd(c, i, x1v, dh, dxr, z, nw, sc, g1):
        dxn, dsh, dsc, dnw = _norm_bwd(x1v, dh.astype(F32), nw, sc)
        dx1 = dxr + dxn
        return (dx1, dx1 * g1, dsh, dsc, dnw, _csum(dx1 * z.astype(F32)))

    dx1, dz1, d_shift2, d_scale2, d_n2w, d_gate1 = _rowcall(
        f_norm2_bwd, [_rows(x1, rb), _rows(d_h2, rb), _rows(dx2, rb), _rows(z1, rb), _full(sm["n2w"]), _full(scale2), _full(gate1)],
        [_orow(s, D, F32, rb), _orow(s, D, BF16, rb), _oacc(1, D), _oacc(1, D), _oacc(1, D), _oacc(1, D)],
        n_rows=s, rb=rb, name="norm2_bwd")
    d_mixed = _mm(dz1, w["wout"], "out_proj_dx", tb=True)
    g_wout = _mm(mixed, dz1, "out_proj_dw", ta=True, out_dtype=F32, tk=2048)

    def f_merge_bwd(c, i, dm, ma, mb, yg, ya):
        dmf, ygf, yaf = dm.astype(F32), yg.astype(F32), ya.astype(F32)
        sa, sb = _sigmoid(ma.astype(F32)), _sigmoid(mb.astype(F32))
        return (dmf * sa, dmf * sb, jnp.concatenate([dmf * ygf * sa * (1.0 - sa), dmf * yaf * sb * (1.0 - sb)], axis=1))

    dy_gla, dy_att, dp = _rowcall(
        f_merge_bwd, [_rows(d_mixed, rb), _rows(p, rb, D, P_MA // D), _rows(p, rb, D, P_MB // D), _rows(y_gla, rb), _rows(y_att, rb)],
        [_orow(s, D, BF16, rb)] * 2 + [_orow(s, P_W, BF16, rb, 2 * D, lambda c: P_MA // (2 * D))], n_rows=s, rb=rb, name="merge_bwd")
    d_og = _mm(dy_gla, w["wgb"], "gla_branch_dx", tb=True)
    g_wgb = _mm(og, dy_gla, "gla_branch_dw", ta=True, out_dtype=F32, tk=2048)
    d_oatt = _mm(dy_att, w["wab"], "attn_branch_dx", tb=True)
    g_wab = _mm(o_att, dy_att, "attn_branch_dw", ta=True, out_dtype=F32, tk=2048)

    def f_gla_post_bwd(c, i, ov, gnw, gr, dog):
        g = gr.astype(F32)
        sg = _sigmoid(g)
        silu = g * sg
        dof = dog.astype(F32)
        don = dof * silu
        on_parts, do_parts, dgn = [], [], jnp.zeros((1, 256), F32)
        for k in range(GLA_H):
            oh = ov[:, k * 256:(k + 1) * 256]
            dh = don[:, k * 256:(k + 1) * 256]
            r = _rms(oh)
            xh = oh * r
            dgn = dgn + _csum(dh * xh)
            dxh = dh * gnw
            do_parts.append(r * (dxh - xh * jnp.mean(dxh * xh, axis=-1, keepdims=True)))
            on_parts.append(xh * gnw)
        on = jnp.concatenate(on_parts, axis=1)
        dgr = dof * on * (sg * (1.0 + g * (1.0 - sg)))
        return (jnp.concatenate(do_parts, axis=1), dgr, dgn)

    do_gla, dp, d_gnw = _rowcall(
        f_gla_post_bwd, [_rows(o_gla, rb), _full(sm["gnw"]), _rows(p, rb, 1024, P_GR // 1024), _rows(d_og, rb)],
        [_orow(s, 1024, F32, rb), _orow(s, P_W, BF16, rb, 1024, lambda c: P_GR // 1024), _oacc(1, 256)],
        n_rows=s, rb=rb, name="gla_post_bwd", into=(dp, 1))
    gs123 = [g_wgb.reshape(4, 256, 1024), _cols_split(g_wab), g_wout.reshape(4, 256, 1024)]
    d_gq, d_gk, dp, d_la, [r4, land123] = _gla_bwd(p, la, states, do_gla, s, dp,
                                                   comm=[_u_chip_exchange(ts45[:1]), _u_pair_send(gs123, (1, 2, 3))])
    half4 = [_chip_sum(ts45[0], r4[0], chip1, "grad_chip_sum_w_up")]
    ts123 = [_pair_add(g, ld, core, "grad_pair_add_" + BIG[i]) for g, ld, i in zip(gs123, land123, (1, 2, 3))]

    def f_gla_pre_bwd(c, i, lav, dlav, glr, w2):
        dz = dlav * (1.0 / GLA_TAU) * (1.0 - jnp.exp(GLA_TAU * lav))
        dzb = dz.astype(BF16)
        return (_dg(dzb, w2.astype(BF16), 1, 1), _csum(dz), _dg(glr, dzb, 0, 0))

    d_glr, d_gb, d_w2 = _rowcall(
        f_gla_pre_bwd, [_rows(la, rb), _rows(d_la, rb), _rows(p, rb, 128, P_LR // 128), _full(sm["w2"])],
        [_orow(s, 128, BF16, rb), _oacc(1, 512), _oacc(128, 512)], n_rows=s, rb=rb, name="gla_pre_bwd")

    do_d = [d_oatt] + list(_dilate(d_oatt, s))
    datt = [_attn_bwd(q_d[g], k_d[g], v_d[g], do_d[g], (o_att, o_d1, o_d2)[g], (lse, lse_d1, lse_d2)[g], g, r, s)
            for g, r in enumerate(_RS)]
    dp = _rope_bwd(datt, d_glr, dp, cos_t, sin_t, s)
    dp = lax.dynamic_update_slice(dp, jnp.concatenate([d_gq, d_gk], axis=1), (0, P_GQ))
    g_win, [r1235, oth4] = _mm(h, dp, "in_proj_dw", ta=True, out_dtype=F32, tm=1024, tn=1536, tk=2048,
                               comm=[_u_chip_exchange(ts123 + ts45[1:]), _u_pair_join(half4)])
    half1235 = [_chip_sum(t, r, chip1, "grad_chip_sum_" + BIG[i]) for t, r, i in zip(ts123 + ts45[1:], r1235, (1, 2, 3, 5))]
    gs0 = [_win_split(g_win)]
    d_h, [land0, oth1235] = _mm(dp, w["win"], "in_proj_dx", tb=True, tk=3840,
                                comm=[_u_pair_send(gs0, (0,)), _u_pair_join(half1235)])
    half123, half45 = half1235[:3], half4 + half1235[3:]
    oth123, oth45 = oth1235[:3], oth4 + oth1235[3:]
    u_ex = _u_chip_exchange([_pair_add(gs0[0], land0[0], core, "grad_pair_add_w_in")])
    pending = (u_ex,) + _unit_start(u_ex, "grad_exchange_w_in_start")

    def f_norm1_bwd(c, i, xv, dh, dxr, nw, sc):
        dxn, dsh, dsc, dnw = _norm_bwd(xv, dh.astype(F32), nw, sc)
        return (dxr + dxn, dsh, dsc, dnw)

    grad_x, d_shift1, d_scale1, d_n1w = _rowcall(
        f_norm1_bwd, [_rows(x, rb), _rows(d_h, rb), _rows(dx1, rb), _full(sm["n1w"]), _full(scale1)],
        [_orow(s, D, F32, rb), _oacc(1, D), _oacc(1, D), _oacc(1, D)], n_rows=s, rb=rb, name="norm1_bwd", after=pending[4:5])

    dmod = jnp.concatenate([d_shift1, d_scale1, d_gate1, d_shift2, d_scale2, d_gate2], axis=1)
    small = dict(dmod=dmod, n1w=d_n1w, gb=d_gb, gnw=d_gnw, n2w=d_n2w, cb=d_cb, fnw=d_fnw, w2=d_w2, cw=d_cw)
    return loss, grad_x, half123 + half45, oth123 + oth45, small, pending


def _win_pieces():
    runs = [(P_GV, 1024, 2048), (P_MA, 5392, 2048), (P_GQ, 0, 1024), (P_AQ, 3088, 2304), (P_LR, 3072, GLA_LR)]
    out = []
    for kc, rc, ln in runs:
        while ln > 0:
            step = min(ln, W_IN_SH - rc % W_IN_SH)
            out.append((kc, rc, step))
            kc, rc, ln = kc + step, rc + step, ln - step
    return out


def _win_assemble(shards):
    rb = 256

    def body(s_ref, o_ref):
        o_ref[:, W_IN:] = jnp.zeros((rb, P_W - W_IN), o_ref.dtype)
        for kc, rc, ln in _win_pieces():
            o_ref[:, kc:kc + ln] = s_ref[rc // W_IN_SH, :, rc % W_IN_SH:rc % W_IN_SH + ln]

    return pl.pallas_call(
        body, name="w_in_assemble", grid=(D // rb,),
        in_specs=[pl.BlockSpec((4, rb, W_IN_SH), lambda i: (0, i, 0))], out_specs=pl.BlockSpec((rb, P_W), lambda i: (i, 0)),
        out_shape=jax.ShapeDtypeStruct((D, P_W), shards.dtype), compiler_params=_params(("parallel",)),
    )(shards)


def _win_split(g):
    rb = 256

    def body(g_ref, o_ref):
        for kc, rc, ln in _win_pieces():
            o_ref[rc // W_IN_SH, :, rc % W_IN_SH:rc % W_IN_SH + ln] = g_ref[:, kc:kc + ln]

    return pl.pallas_call(
        body, name="w_in_grad_split", grid=(D // rb,),
        in_specs=[pl.BlockSpec((rb, P_W), lambda i: (i, 0))], out_specs=pl.BlockSpec((4, rb, W_IN_SH), lambda i: (0, i, 0)),
        out_shape=jax.ShapeDtypeStruct((4, D, W_IN_SH), g.dtype), compiler_params=_params(("parallel",)),
    )(g)


def _ff_to_kernel(a):
    h = W_UP_SH
    return jnp.concatenate([a[:, 0:h], a[:, D_FF:D_FF + h], a[:, h:D_FF], a[:, D_FF + h:]], axis=1)


def _ff_from_kernel(a):
    h = W_UP_SH
    return jnp.concatenate([a[:, 0:h], a[:, 2 * h:3 * h], a[:, h:2 * h], a[:, 3 * h:]], axis=1)


BIG = ("w_in", "w_gla_branch", "w_attn_branch", "w_out", "w_up", "w_down")
SH_SHAPES = ((1024, W_IN_SH), (256, 1024), (256, 256), (256, 1024), (1024, W_UP_SH), (W_DOWN_SH, 1024))
N_BIG = len(BIG)


def _cols_join(t):
    return jnp.concatenate([t[k] for k in range(4)], axis=1)


def _cols_split(t):
    cols = t.shape[1] // 4
    return jnp.stack([t[:, k * cols:(k + 1) * cols] for k in range(4)])


def _me():
    return lax.axis_index("x"), lax.axis_index("y"), lax.axis_index("c")


HBM = pl.BlockSpec(memory_space=pltpu.HBM)
VMEM_SPEC = pl.BlockSpec(memory_space=pltpu.VMEM)


def _allgather8(xs, name):
    rows = xs.shape[0]

    def body(x_ref, out_ref, send_sems, recv_sems, local_sem):
        x, y, c = _me()
        me = 4 * x + 2 * y + c
        mine = pltpu.make_async_copy(x_ref, out_ref.at[me], local_sem)
        mine.start()
        flips = [(k >> 2 & 1, k >> 1 & 1, k & 1) for k in range(1, 8)]

        def peer(f):
            return (jnp.where(f[0] == 1, 1 - x, x), jnp.where(f[1] == 1, 1 - y, y), jnp.where(f[2] == 1, 1 - c, c))

        sends = []
        for k, f in enumerate(flips):
            cp = pltpu.make_async_remote_copy(src_ref=x_ref, dst_ref=out_ref.at[me], send_sem=send_sems.at[k],
                                              recv_sem=recv_sems.at[k], device_id=peer(f), device_id_type=MESH)
            cp.start()
            sends.append(cp)
        for k, f in enumerate(flips):
            px, py, pc = peer(f)
            pltpu.make_async_remote_copy(src_ref=x_ref, dst_ref=out_ref.at[4 * px + 2 * py + pc], send_sem=send_sems.at[k],
                                         recv_sem=recv_sems.at[k], device_id=peer(f), device_id_type=MESH).wait_recv()
        for cp in sends:
            cp.wait_send()
        mine.wait()

    return pl.pallas_call(
        body, name=name, out_shape=jax.ShapeDtypeStruct((8, rows, 128), F32),
        in_specs=[VMEM_SPEC], out_specs=VMEM_SPEC,
        scratch_shapes=[pltpu.SemaphoreType.DMA((7,)), pltpu.SemaphoreType.DMA((7,)), pltpu.SemaphoreType.DMA],
        compiler_params=pltpu.CompilerParams(vmem_limit_bytes=VMEM_LIMIT),
    )(xs)


def _half_rows(i, cc, unit):
    rows = SH_SHAPES[i][0] // 2
    return pl.ds(pl.multiple_of(cc * rows, unit), rows)


def _rc(src, dst, sems, to):
    return pltpu.make_async_remote_copy(src_ref=src, dst_ref=dst, send_sem=sems[0], recv_sem=sems[1], device_id=to, device_id_type=MESH)


def _other_chips(x, y):
    return [(1 - x, y), (x, 1 - y), (1 - x, 1 - y)]


def _u_gather_ici(w_sh, idxs):
    def copies(ins, outs, sem):
        x, y, c = _me()
        res = []
        for j, (px, py) in enumerate(_other_chips(x, y)):
            for n, i in enumerate(idxs):
                src = ins[n].at[0, _half_rows(i, c, 16)]
                res.append((_rc(src, outs[n].at[2 * x + y, _half_rows(i, c, 16)], sem(j * len(idxs) + n), (px, py, c)),
                            _rc(src, outs[n].at[2 * px + py, _half_rows(i, c, 16)], sem(j * len(idxs) + n), (px, py, c))))
        return res

    return dict(ins=[w_sh[i] for i in idxs], outs=[jax.ShapeDtypeStruct((4,) + SH_SHAPES[i], BF16) for i in idxs],
                nsem=3 * len(idxs), alias={}, copies=copies)


def _u_gather_d2d(got, idxs):
    def copies(ins, outs, sem):
        x, y, c = _me()
        res = []
        for j, (px, py) in enumerate(_other_chips(x, y)):
            for n, i in enumerate(idxs):
                src = ins[n].at[2 * px + py, _half_rows(i, c, 16)]
                res.append((_rc(src, outs[n].at[2 * px + py, _half_rows(i, c, 16)], sem(j * len(idxs) + n), (x, y, 1 - c)),
                            _rc(src, outs[n].at[2 * px + py, _half_rows(i, 1 - c, 16)], sem(j * len(idxs) + n), (x, y, 1 - c))))
        return res

    return dict(ins=list(got), outs=[jax.ShapeDtypeStruct(g.shape, g.dtype) for g in got], nsem=3 * len(idxs),
                alias={n: n for n in range(len(idxs))}, copies=copies)


def _u_pair_send(gs, idxs):
    def copies(ins, outs, sem):
        x, y, c = _me()
        res = []
        for n, i in enumerate(idxs):
            for sh in range(4):
                cp = _rc(ins[n].at[sh, _half_rows(i, 1 - c, 8)], outs[n].at[sh], sem(4 * n + sh), (x, y, 1 - c))
                res.append((cp, cp))
        return res

    return dict(ins=list(gs), outs=[jax.ShapeDtypeStruct((4, SH_SHAPES[i][0] // 2, SH_SHAPES[i][1]), F32) for i in idxs],
                nsem=4 * len(idxs), alias={}, copies=copies)


def _u_chip_exchange(ts):
    def copies(ins, outs, sem):
        x, y, c = _me()
        res = []
        for j, (px, py) in enumerate(_other_chips(x, y)):
            for n in range(len(ts)):
                cp = _rc(ins[n].at[2 * px + py], outs[n].at[j], sem(j * len(ts) + n), (px, py, c))
                res.append((cp, cp))
        return res

    return dict(ins=list(ts), outs=[jax.ShapeDtypeStruct((3,) + t.shape[1:], t.dtype) for t in ts], nsem=3 * len(ts),
                alias={}, copies=copies)


def _u_pair_join(hs):
    def copies(ins, outs, sem):
        x, y, c = _me()
        res = []
        for n in range(len(hs)):
            cp = _rc(ins[n], outs[n], sem(n), (x, y, 1 - c))
            res.append((cp, cp))
        return res

    return dict(ins=list(hs), outs=[jax.ShapeDtypeStruct(h.shape, h.dtype) for h in hs], nsem=len(hs), alias={}, copies=copies)


def _comm_phase(units, ci, co, send_sems, recv_sems, start):
    ii = oo = off = 0
    for u in units:
        ni, no = len(u["ins"]), len(u["outs"])
        for st, arrival in u["copies"](ci[ii:ii + ni], co[oo:oo + no], lambda k, off=off: (send_sems.at[off + k], recv_sems.at[off + k])):
            if start:
                st.start()
            else:
                st.wait_send()
                arrival.wait_recv()
        ii, oo, off = ii + ni, oo + no, off + u["nsem"]


def _carry(units, n_in, n_out):
    ins = [a for u in units for a in u["ins"]]
    outs = [o for u in units for o in u["outs"]]
    alias, ii, oo = {}, 0, 0
    for u in units:
        for a, b in u["alias"].items():
            alias[n_in + ii + a] = n_out + oo + b
        ii, oo = ii + len(u["ins"]), oo + len(u["outs"])
    nsem = sum(u["nsem"] for u in units)
    scratch = [pltpu.SemaphoreType.DMA((nsem,)), pltpu.SemaphoreType.DMA((nsem,))] if units else []
    return ins, outs, alias, scratch


def _split_units(units, res):
    out, oo = [], 0
    for u in units:
        out.append(list(res[oo:oo + len(u["outs"])]))
        oo += len(u["outs"])
    return out


def _comm_call(name, units):
    ins, outs, alias, scratch = _carry(units, 0, 0)

    def body(*refs):
        ci, co = refs[:len(ins)], refs[len(ins):len(ins) + len(outs)]
        _comm_phase(units, ci, co, refs[-2], refs[-1], True)
        _comm_phase(units, ci, co, refs[-2], refs[-1], False)

    res = pl.pallas_call(body, name=name, out_shape=outs, in_specs=[HBM] * len(ins), out_specs=[HBM] * len(outs),
                         scratch_shapes=scratch, input_output_aliases=alias)(*ins)
    return _split_units(units, res)


SEM = pl.BlockSpec(memory_space=pltpu.SEMAPHORE)
EFFECT = pltpu.SideEffectType.DATAFLOW_SIDE_EFFECTING


def _unit_start(unit, name):
    bufs = list(unit["ins"]) + [lax.empty(o.shape, o.dtype) for o in unit["outs"]]
    n_i, n_b, ns = len(unit["ins"]), len(bufs), unit["nsem"]

    def body(*refs):
        send_sems, recv_sems = refs[n_b], refs[n_b + 1]
        for st, _ in unit["copies"](refs[:n_i], refs[n_i:n_b], lambda k: (send_sems.at[k], recv_sems.at[k])):
            st.start()
        refs[-1][...] = jnp.zeros_like(refs[-1])

    res = pl.pallas_call(
        body, name=name,
        out_shape=[pltpu.SemaphoreType.DMA((ns,)), pltpu.SemaphoreType.DMA((ns,))] + [pltpu.HBM(b.shape, b.dtype) for b in bufs]
        + [jax.ShapeDtypeStruct((8, 128), F32)],
        in_specs=[HBM] * n_b, out_specs=[SEM, SEM] + [HBM] * n_b + [VMEM_SPEC],
        input_output_aliases={i: 2 + i for i in range(n_b)},
        compiler_params=pltpu.CompilerParams(has_side_effects=EFFECT),
    )(*[pltpu.with_memory_space_constraint(b, pltpu.HBM) for b in bufs])
    return res[0], res[1], list(res[2:2 + n_b]), res[-1]


def _unit_wait(unit, send_sems, recv_sems, bufs, after, name):
    n_i, n_b = len(unit["ins"]), len(bufs)

    def body(*refs):
        ss, rs = refs[n_b], refs[n_b + 1]
        for st, arrival in unit["copies"](refs[:n_i], refs[n_i:n_b], lambda k: (ss.at[k], rs.at[k])):
            st.wait_send()
            arrival.wait_recv()

    res = pl.pallas_call(
        body, name=name, out_shape=[pltpu.HBM(b.shape, b.dtype) for b in bufs],
        in_specs=[HBM] * n_b + [SEM, SEM] + [pl.BlockSpec(memory_space=pl.ANY)] * len(after), out_specs=[HBM] * n_b,
        input_output_aliases={i: i for i in range(n_b)}, compiler_params=pltpu.CompilerParams(has_side_effects=EFFECT),
    )(*bufs, send_sems, recv_sems, *after)
    return list(res[:n_i]), list(res[n_i:])


def _pair_add(g, land, core, name):
    _, rows, cols = g.shape
    half = rows // 2
    rb = _tile(half, 256, 16)
    nb = half // rb

    def body(c_ref, g_ref, l_ref, o_ref):
        o_ref[...] = (g_ref[...] + l_ref[...]).astype(BF16)

    return pl.pallas_call(
        body, name=name,
        grid_spec=pltpu.PrefetchScalarGridSpec(
            num_scalar_prefetch=1, grid=(4, nb),
            in_specs=[pl.BlockSpec((1, rb, cols), lambda s, i, c_ref: (s, c_ref[0] * nb + i, 0)),
                      pl.BlockSpec((1, rb, cols), lambda s, i, c_ref: (s, i, 0))],
            out_specs=pl.BlockSpec((1, rb, cols), lambda s, i, c_ref: (s, i, 0))),
        out_shape=jax.ShapeDtypeStruct((4, half, cols), BF16),
        compiler_params=_params(("parallel", "parallel")),
    )(core, g, land)


def _chip_sum(t, r, chip, name):
    _, half, cols = t.shape
    rb = _tile(half, 256, 16)

    def body(s_ref, t_ref, r_ref, o_ref):
        o_ref[...] = ((t_ref[0].astype(F32) + r_ref[0].astype(F32)) + r_ref[1].astype(F32)) + r_ref[2].astype(F32)

    return pl.pallas_call(
        body, name=name,
        grid_spec=pltpu.PrefetchScalarGridSpec(
            num_scalar_prefetch=1, grid=(half // rb,),
            in_specs=[pl.BlockSpec((1, rb, cols), lambda i, s_ref: (s_ref[0], i, 0)),
                      pl.BlockSpec((3, rb, cols), lambda i, s_ref: (0, i, 0))],
            out_specs=pl.BlockSpec((rb, cols), lambda i, s_ref: (i, 0))),
        out_shape=jax.ShapeDtypeStruct((half, cols), F32),
        compiler_params=_params(("parallel",)),
    )(chip, t, r)


def _adam_math(wv, gv, mv, vv):
    mn = ADAM_B1 * mv + (1.0 - ADAM_B1) * gv
    vn = ADAM_B2 * vv + (1.0 - ADAM_B2) * (gv * gv)
    m_hat = mn / (1.0 - ADAM_B1 ** ADAM_STEP)
    v_hat = vn / (1.0 - ADAM_B2 ** ADAM_STEP)
    return -ADAM_LR * (m_hat / (jnp.sqrt(v_hat) + ADAM_EPS) + ADAM_WD * wv), mn, vn


def _adamw_halves(wt, mt, vt, mine, theirs, core, name):
    _, rows, cols = wt.shape
    half = rows // 2
    rb = _tile(half, 256, 8)
    nb = half // rb

    def body(c_ref, w_ref, m_ref, v_ref, a_ref, b_ref, g_ref, d_ref, mo_ref, vo_ref):
        gv = jnp.where(pl.program_id(0) == c_ref[0], a_ref[...], b_ref[...])
        dl, mn, vn = _adam_math(w_ref[...], gv, m_ref[...], v_ref[...])
        g_ref[...] = gv
        d_ref[...] = dl
        mo_ref[...] = mn
        vo_ref[...] = vn

    full = pl.BlockSpec((None, rb, cols), lambda hf, i, c_ref: (0, hf * nb + i, 0))
    part = pl.BlockSpec((rb, cols), lambda hf, i, c_ref: (i, 0))
    return pl.pallas_call(
        body, name=name,
        grid_spec=pltpu.PrefetchScalarGridSpec(num_scalar_prefetch=1, grid=(2, nb), in_specs=[full, full, full, part, part],
                                               out_specs=[full] * 4),
        out_shape=[jax.ShapeDtypeStruct((1, rows, cols), F32)] * 4,
        compiler_params=_params(("parallel", "parallel")),
    )(core, wt, mt, vt, mine, theirs)


SG_REP = 144
SG_LOSS = 136
SG_W2, SG_CW = SG_REP, SG_REP + 4 * 16
SG_ROWS = SG_CW + 4 * 40
SP_ROWS = SG_REP + 16 + 40


def _mod_shard(c_all, ada_w_sh):
    def body(c_ref, w_ref, o_ref):
        cv = c_ref[...]
        o_ref[...] = _dg((cv * _sigmoid(cv)).astype(BF16), w_ref[...].astype(BF16), 1, 0)

    return pl.pallas_call(body, name="mod_shard", out_shape=jax.ShapeDtypeStruct((8, 1536), F32),
                          in_specs=[VMEM_SPEC, VMEM_SPEC], out_specs=VMEM_SPEC,
                          compiler_params=pltpu.CompilerParams(vmem_limit_bytes=VMEM_LIMIT))(c_all, ada_w_sh)


def _mod_select(mod_all, ada_b4):
    def body(m_ref, b_ref, o_ref):
        x, y, c = _me()
        me = 4 * x + 2 * y + c
        for sh in range(4):
            o_ref[sh] = m_ref[2 * sh, me] + b_ref[sh]

    return pl.pallas_call(body, name="mod_select", out_shape=jax.ShapeDtypeStruct((4, 12, 128), F32),
                          in_specs=[VMEM_SPEC, VMEM_SPEC], out_specs=VMEM_SPEC)(mod_all, ada_b4)


def _small_reduce(sg_all):
    def body(g_ref, o_ref):
        x, y, c = _me()
        s_me = 2 * x + y
        w2_rows = pl.ds(pl.multiple_of(SG_W2 + 16 * s_me, 8), 16)
        cw_rows = pl.ds(pl.multiple_of(SG_CW + 40 * s_me, 8), 40)
        a = g_ref[0, 0:SG_REP, :]
        b = g_ref[0, w2_rows, :]
        d = g_ref[0, cw_rows, :]
        for dev in range(1, 8):
            a = a + g_ref[dev, 0:SG_REP, :]
            b = b + g_ref[dev, w2_rows, :]
            d = d + g_ref[dev, cw_rows, :]
        o_ref[0:SG_REP, :] = a
        o_ref[SG_REP:SG_REP + 16, :] = b
        o_ref[SG_REP + 16:SP_ROWS, :] = d

    return pl.pallas_call(body, name="small_grad_reduce", out_shape=jax.ShapeDtypeStruct((SP_ROWS, 128), F32),
                          in_specs=[VMEM_SPEC], out_specs=VMEM_SPEC)(sg_all)


def _ada_grad(dmod_all, c_bc):
    def body(g_ref, c_ref, o_ref):
        x, y, c = _me()
        s_me = 2 * x + y
        for k in range(12):
            acc = jnp.zeros((D, 128), F32)
            for b in range(8):
                cv = c_ref[b]
                acc = acc + (cv * _sigmoid(cv)) * g_ref[s_me, k, b:b + 1, :]
            o_ref[:, k * 128:(k + 1) * 128] = acc

    return pl.pallas_call(body, name="ada_w_grad", out_shape=jax.ShapeDtypeStruct((D, 1536), F32),
                          in_specs=[VMEM_SPEC, VMEM_SPEC], out_specs=VMEM_SPEC,
                          compiler_params=pltpu.CompilerParams(vmem_limit_bytes=VMEM_LIMIT))(dmod_all, c_bc)


def _adamw(wt, g, m, v, name):
    rows, cols = wt.shape
    rb = _tile(rows, 256, 8)

    def fn(c, i, wv, gv, mv, vv):
        return _adam_math(wv, gv, mv, vv)

    return _rowcall(fn, [_rows(t, rb) for t in (wt, g, m, v)], [_orow(rows, cols, F32, rb)] * 3,
                    n_rows=rows, rb=rb, name=name)


def _pad_rows(t, rows):
    flat = t.reshape(-1)
    return jnp.pad(flat, (0, rows * 128 - flat.shape[0])).reshape(rows, 128)


SP_LAYOUT = (("ada_b", 48), ("norm1_w", 8), ("gla_gate_b", 8), ("gla_norm_w", 8), ("norm2_w", 8), ("conv_b", 48),
             ("final_norm_w", 8), (None, 8), ("gla_gate_w2", 16), ("conv_w", 40))


def _pack_small(d):
    return jnp.concatenate([jnp.zeros((rows, 128), F32) if n is None else _pad_rows(d[n].astype(F32), rows)
                            for n, rows in SP_LAYOUT], axis=0)


def _unpack_small(pk, shapes):
    out, off = {}, 0
    for n, rows in SP_LAYOUT:
        if n is not None:
            shp = shapes[n]
            out[n] = pk[off:off + rows].reshape(-1)[:math.prod(shp)].reshape(shp)
        off += rows
    return out


def kernel(x, c, positions, ada_w, ada_b, norm1_w, w_in, gla_gate_w2, gla_gate_b, gla_norm_w, w_gla_branch, w_attn_branch, w_out, norm2_w, w_up, conv_w, conv_b, w_down, final_norm_w, loss_target, m_ada_w, m_ada_b, m_norm1_w, m_w_in, m_gla_gate_w2, m_gla_gate_b, m_gla_norm_w, m_w_gla_branch, m_w_attn_branch, m_w_out, m_norm2_w, m_w_up, m_conv_w, m_conv_b, m_w_down, m_final_norm_w, v_ada_w, v_ada_b, v_norm1_w, v_w_in, v_gla_gate_w2, v_gla_gate_b, v_gla_norm_w, v_w_gla_branch, v_w_attn_branch, v_w_out, v_norm2_w, v_w_up, v_conv_w, v_conv_b, v_w_down, v_final_norm_w):
    s = x.shape[1]
    names = ("ada_w", "ada_b", "norm1_w", "w_in", "gla_gate_w2", "gla_gate_b", "gla_norm_w", "w_gla_branch", "w_attn_branch",
             "w_out", "norm2_w", "w_up", "conv_w", "conv_b", "w_down", "final_norm_w")
    wts = dict(zip(names, (ada_w, ada_b, norm1_w, w_in, gla_gate_w2, gla_gate_b, gla_norm_w, w_gla_branch, w_attn_branch,
                           w_out, norm2_w, w_up, conv_w, conv_b, w_down, final_norm_w)))
    ms = dict(zip(names, (m_ada_w, m_ada_b, m_norm1_w, m_w_in, m_gla_gate_w2, m_gla_gate_b, m_gla_norm_w, m_w_gla_branch,
                          m_w_attn_branch, m_w_out, m_norm2_w, m_w_up, m_conv_w, m_conv_b, m_w_down, m_final_norm_w)))
    vs = dict(zip(names, (v_ada_w, v_ada_b, v_norm1_w, v_w_in, v_gla_gate_w2, v_gla_gate_b, v_gla_norm_w, v_w_gla_branch,
                          v_w_attn_branch, v_w_out, v_norm2_w, v_w_up, v_conv_w, v_conv_b, v_w_down, v_final_norm_w)))

    w_sh = [wts[n].astype(BF16) for n in BIG]
    u_g0 = _u_gather_ici(w_sh, (0,))
    g0 = (u_g0,) + _unit_start(u_g0, "gather_w_in_start")

    pk0 = jnp.concatenate([_pad_rows(c, 8), _pad_rows(gla_gate_w2, 16), _pad_rows(conv_w, 40)], axis=0) + g0[4][0, 0]
    sm_all = _allgather8(pk0, "gather_small")
    c_all = sm_all[:, 0:8, :].reshape(8, D)
    w2_full = sm_all[0::2, 8:24, :].transpose(1, 0, 2).reshape(GLA_LR, 512)
    cw_full = sm_all[0::2, 24:64, :].reshape(4, 40 * 128)[:, :3 * W_UP_SH].reshape(4, 3, W_UP_SH).transpose(1, 0, 2).reshape(3, 2 * D_FF)

    mod_sh = _mod_shard(c_all, ada_w[0])
    mod_all = _allgather8(mod_sh.reshape(96, 128), "gather_mod")
    mod = _mod_select(mod_all.reshape(8, 8, 12, 128), ada_b.reshape(4, 12, 128)).reshape(6, D)

    core = lax.axis_index("c").astype(jnp.int32).reshape(1)
    chip = (2 * lax.axis_index("x") + lax.axis_index("y")).astype(jnp.int32)
    sm = dict(n1w=norm1_w, n2w=norm2_w, fnw=final_norm_w.reshape(1, D), gnw=gla_norm_w, gb=gla_gate_b,
              w2=jnp.pad(w2_full, ((0, 128 - GLA_LR), (0, 0))), cw=_ff_to_kernel(cw_full), cb=_ff_to_kernel(conv_b))
    loss, grad_x, halves, others, small, pending = _local_step(x[0], mod, positions.reshape(s, 1), loss_target[0], sm, w_sh,
                                                               g0, chip, core)

    dcw = _ff_from_kernel(small["cw"]).reshape(3, 4, W_UP_SH).transpose(1, 0, 2)
    dw2 = small["w2"][:GLA_LR].reshape(GLA_LR, 4, 128).transpose(1, 0, 2)
    sg = jnp.concatenate(
        [_pad_rows(small["dmod"], 48), _pad_rows(small["n1w"], 8), _pad_rows(small["gb"], 8), _pad_rows(small["gnw"], 8),
         _pad_rows(small["n2w"], 8), _pad_rows(_ff_from_kernel(small["cb"]), 48), _pad_rows(small["fnw"], 8), _pad_rows(loss, 8)]
        + [_pad_rows(dw2[k], 16) for k in range(4)] + [_pad_rows(dcw[k], 40) for k in range(4)], axis=0)
    sg_all = _allgather8(sg, "gather_small_grads")
    g_small_pk = _small_reduce(sg_all)
    dmod_all = sg_all[:, 0:48, :].reshape(8, 4, 12, 128).transpose(1, 2, 0, 3)
    g_ada_w = _ada_grad(dmod_all, jnp.broadcast_to(c_all[:, :, None], (8, D, 128)))

    shapes = {n: wts[n].shape for n in names}
    g_small = _unpack_small(g_small_pk, shapes)
    grads = {"ada_w": g_ada_w.reshape(1, D, 1536), **g_small}
    deltas, new_m, new_v = {}, {}, {}
    for n, mine, theirs in zip(BIG[1:], halves, others):
        grads[n], deltas[n], new_m[n], new_v[n] = _adamw_halves(wts[n], ms[n], vs[n], mine, theirs, core, "adamw_" + n)
    shp = ada_w.shape
    d_, m_, v_ = _adamw(ada_w[0], g_ada_w, m_ada_w[0], v_ada_w[0], "adamw_ada_w")
    deltas["ada_w"], new_m["ada_w"], new_v["ada_w"] = d_.reshape(shp), m_.reshape(shp), v_.reshape(shp)
    d_, m_, v_ = _adamw(_pack_small(wts), g_small_pk, _pack_small(ms), _pack_small(vs), "adamw_small")
    for dst, pk in ((deltas, d_), (new_m, m_), (new_v, v_)):
        dst.update(_unpack_small(pk, shapes))

    [t0], [r0] = _unit_wait(*pending[:4], after=[d_, deltas["ada_w"], deltas["w_up"], deltas["w_down"]], name="grad_exchange_w_in_wait")
    half0 = _chip_sum(t0, r0, chip.reshape(1), "grad_chip_sum_w_in")
    [[oth0]] = _comm_call("grad_join_w_in", [_u_pair_join([half0])])
    grads["w_in"], deltas["w_in"], new_m["w_in"], new_v["w_in"] = _adamw_halves(w_in, m_w_in, v_w_in, half0, oth0, core, "adamw_w_in")

    return (g_small_pk[SG_LOSS, 0], grad_x.reshape(1, s, D), *[grads[n] for n in names], *[deltas[n] for n in names],
            *[new_m[n] for n in names], *[new_v[n] for n in names])
```

```python
import math

import jax
import jax.numpy as jnp
from jax import lax
from jax.experimental import pallas as pl
from jax.experimental.pallas import tpu as pltpu

F32, BF16 = jnp.float32, jnp.bfloat16
MESH = pl.DeviceIdType.MESH

D = 1024
EPS = 1e-6
GLA_H, GLA_DK, GLA_DV, GLA_LR = 4, 128, 256, 16
GLA_TAU = 16.0
GLA_CHUNK = 64
GLA_BLOCK = 512
ATT_GROUPS = ((128, 1), (512, 4), (2048, 16))
ATT_BLK = 128
ATT_HD = 64
ATT_W = 768
D_FF = 2816
ROPE_THETA = 10000.0
P_W = 7680
P_GV, P_GR, P_MA, P_MB, P_GQ, P_GK, P_AQ, P_AK, P_AV, P_LR = 0, 1024, 2048, 3072, 4096, 4608, 5120, 5888, 6656, 7424
W_IN = 7440
W_IN_SH, W_UP_SH, W_DOWN_SH = 1860, 1408, 704
VMEM_LIMIT = 56 * 1024 * 1024
ADAM_LR, ADAM_B1, ADAM_B2, ADAM_EPS, ADAM_WD, ADAM_STEP = 0.001, 0.9, 0.999, 1e-08, 0.01, 10
NEG = -1e30


def _tile(n, target, unit=128):
    best = None
    for t in range(unit, min(n, target) + 1, unit):
        if n % t == 0:
            best = t
    return best or n


def _params(sem):
    return pltpu.CompilerParams(dimension_semantics=sem, vmem_limit_bytes=VMEM_LIMIT)


def _dg(a, b, ca, cb):
    return lax.dot_general(a, b, (((ca,), (cb,)), ((), ())), preferred_element_type=F32)


def _sigmoid(v):
    return 1.0 / (1.0 + jnp.exp(-v))


def _ff_block(j):
    return (j % 2) * 2 + j // 2


def _mm(a, b, name, *, ta=False, tb=False, out_dtype=BF16, tm=1024, tn=1536, tk=1024, n_outer=True, comm=(),
        b_shards=False, o_shards=False):
    m = a.shape[1] if ta else a.shape[0]
    k = a.shape[0] if ta else a.shape[1]
    if b_shards:
        n = b.shape[1] if tb else 4 * W_UP_SH
        tn, tk = (tn, W_UP_SH) if tb else (W_UP_SH, tk)
    else:
        n = b.shape[0] if tb else b.shape[1]
    if o_shards:
        tn = W_UP_SH
    tm, tn, tk = _tile(m, tm), _tile(n, tn), _tile(k, tk)
    nm, nn, nk = m // tm, n // tn, k // tk
    in_out = out_dtype == F32
    c_ins, c_outs, c_alias, c_scratch = _carry(comm, 2, 1)

    def body(a_ref, b_ref, *rest):
        ci, o_ref, co = rest[:len(c_ins)], rest[len(c_ins)], rest[len(c_ins) + 1:len(c_ins) + 1 + len(c_outs)]
        scr = rest[len(c_ins) + 1 + len(c_outs):]
        kk = pl.program_id(2)
        if comm:
            step = (pl.program_id(0) * (nm if n_outer else nn) + pl.program_id(1)) * nk + kk

            @pl.when(step == 0)
            def _():
                _comm_phase(comm, ci, co, scr[-2], scr[-1], True)

        _mm_step(a_ref, b_ref, o_ref, scr, kk)
        if comm:
            @pl.when(step == nm * nn * nk - 1)
            def _():
                _comm_phase(comm, ci, co, scr[-2], scr[-1], False)

    def _mm_step(a_ref, b_ref, o_ref, scr, kk):
        p = _dg(a_ref[...].astype(BF16), b_ref[...].astype(BF16), 0 if ta else 1, 1 if tb else 0)
        if nk == 1:
            o_ref[...] = p.astype(o_ref.dtype)
        else:
            acc = o_ref if in_out else scr[0]

            @pl.when(kk == 0)
            def _():
                acc[...] = p

            @pl.when(kk > 0)
            def _():
                acc[...] += p

            if not in_out:
                @pl.when(kk == nk - 1)
                def _():
                    o_ref[...] = acc[...].astype(o_ref.dtype)

    if n_outer:
        ij = lambda g0, g1: (g1, g0)
        grid = (nn, nm, nk)
    else:
        ij = lambda g0, g1: (g0, g1)
        grid = (nm, nn, nk)
    a_map = (lambda g0, g1, kk: (kk, ij(g0, g1)[0])) if ta else (lambda g0, g1, kk: (ij(g0, g1)[0], kk))
    if b_shards and tb:
        b_spec = pl.BlockSpec((None, tn, tk), lambda g0, g1, kk: (_ff_block(kk), ij(g0, g1)[1], 0))
    elif b_shards:
        b_spec = pl.BlockSpec((None, tk, tn), lambda g0, g1, kk: (_ff_block(ij(g0, g1)[1]), kk, 0))
    elif tb:
        b_spec = pl.BlockSpec((tn, tk), lambda g0, g1, kk: (ij(g0, g1)[1], kk))
    else:
        b_spec = pl.BlockSpec((tk, tn), lambda g0, g1, kk: (kk, ij(g0, g1)[1]))
    if o_shards:
        o_spec = pl.BlockSpec((None, tm, tn), lambda g0, g1, kk: (_ff_block(ij(g0, g1)[1]), ij(g0, g1)[0], 0))
        o_shape = jax.ShapeDtypeStruct((4, m, W_UP_SH), out_dtype)
    else:
        o_spec = pl.BlockSpec((tm, tn), lambda g0, g1, kk: ij(g0, g1))
        o_shape = jax.ShapeDtypeStruct((m, n), out_dtype)
    res = pl.pallas_call(
        body, name=name, grid=grid,
        in_specs=[pl.BlockSpec((tk, tm) if ta else (tm, tk), a_map), b_spec] + [HBM] * len(c_ins),
        out_specs=[o_spec] + [HBM] * len(c_outs),
        out_shape=[o_shape] + c_outs,
        scratch_shapes=([] if (in_out or nk == 1) else [pltpu.VMEM((tm, tn), F32)]) + c_scratch,
        input_output_aliases=c_alias,
        compiler_params=_params(("arbitrary",) * 3 if comm else ("parallel", "parallel", "arbitrary")),
    )(a, b, *c_ins)
    return (res[0], _split_units(comm, res[1:])) if comm else res[0]


def _rows(arr, rb, w=None, j=0):
    w = arr.shape[1] if w is None else w
    if callable(j):
        return arr, pl.BlockSpec((rb, w), lambda c, i: (i, j(c)))
    return arr, pl.BlockSpec((rb, w), lambda c, i: (i, j))


def _full(arr, w=None, j=0):
    w = arr.shape[1] if w is None else w
    if callable(j):
        return arr, pl.BlockSpec((arr.shape[0], w), lambda c, i: (0, j(c)))
    return arr, pl.BlockSpec((arr.shape[0], w), lambda c, i: (0, j))


def _halo(arr, rb, hb, w, j, before):
    per = rb // hb
    last = arr.shape[0] // hb - 1
    if before:
        rmap = lambda i: jnp.maximum(i * per - 1, 0)
    else:
        rmap = lambda i: jnp.minimum((i + 1) * per, last)
    return arr, pl.BlockSpec((hb, w), lambda c, i: (rmap(i), j(c) if callable(j) else j))


def _rowcall(fn, ins, outs, *, n_rows, rb, name, ncol=1, into=None, after=()):
    n_in = len(ins)
    nr = n_rows // rb
    unread = ([] if into is None else [into[0]]) + list(after)
    n_skip = len(unread)

    def body(*refs):
        c, i = pl.program_id(0), pl.program_id(1)
        res = fn(c, i, *[r[...] for r in refs[:n_in]])
        for val, spec, o_ref in zip(res, outs, refs[n_in + n_skip:]):
            if spec[2] == "row":
                o_ref[...] = val.astype(o_ref.dtype)
            else:
                @pl.when(i == 0)
                def _(o_ref=o_ref, val=val):
                    o_ref[...] = val.astype(o_ref.dtype)

                @pl.when(i > 0)
                def _(o_ref=o_ref, val=val):
                    o_ref[...] += val.astype(o_ref.dtype)

    out_specs = []
    for shape, dt, kind, block, col in outs:
        if kind == "row":
            out_specs.append(pl.BlockSpec(block, lambda c, i, col=col: (i, col(c))))
        else:
            out_specs.append(pl.BlockSpec(block, lambda c, i, col=col: (0, col(c))))
    return pl.pallas_call(
        body, name=name, grid=(ncol, nr),
        in_specs=[s for _, s in ins] + [pl.BlockSpec(memory_space=pl.ANY)] * n_skip, out_specs=out_specs,
        out_shape=[jax.ShapeDtypeStruct(o[0], o[1]) for o in outs],
        input_output_aliases={} if into is None else {n_in: into[1]},
        compiler_params=_params(("parallel", "arbitrary")),
    )(*[a for a, _ in ins], *unread)


def _orow(n_rows, w, dt, rb, bw=None, col=lambda c: 0):
    return ((n_rows, w), dt, "row", (rb, bw or w), col)


def _oacc(r, w, bw=None, col=lambda c: 0):
    return ((r, w), F32, "acc", (r, bw or w), col)


def _csum(v):
    return jnp.sum(v, axis=0, keepdims=True)


def _rms(v):
    return lax.rsqrt(jnp.mean(v * v, axis=-1, keepdims=True) + EPS)


def _norm_bwd(xv, dh, w, scale):
    r = _rms(xv)
    xh = xv * r
    dxh = dh * (w * (1.0 + scale))
    dx = r * (dxh - xh * jnp.mean(dxh * xh, axis=-1, keepdims=True))
    t = dh * xh
    return dx, _csum(dh), _csum(t * w), _csum(t * (1.0 + scale))


def _rope_tables(pos_col, invf, s):
    def fn(c, i, pos, f):
        ang = pos.astype(F32) * f
        lane = lax.broadcasted_iota(jnp.int32, ang.shape, 1)
        sign = jnp.where((lane % ATT_HD) < ATT_HD // 2, -1.0, 1.0)
        return jnp.cos(ang), jnp.sin(ang) * sign

    rb = 512
    return _rowcall(fn, [_rows(pos_col, rb), _full(invf)], [_orow(s, 128, F32, rb), _orow(s, 128, F32, rb)],
                    n_rows=s, rb=rb, name="rope_tables")


def _swap_halves(t):
    n = t.shape[1]
    lane = lax.broadcasted_iota(jnp.int32, t.shape, 1)
    return jnp.where((lane % ATT_HD) < ATT_HD // 2, pltpu.roll(t, n - 32, 1), pltpu.roll(t, 32, 1))


def _rope_apply(t, cos, sin_signed, inverse):
    cw = jnp.concatenate([cos] * (t.shape[1] // 128), axis=1)
    sw = jnp.concatenate([sin_signed] * (t.shape[1] // 128), axis=1)
    if inverse:
        sw = -sw
    return t * cw + _swap_halves(t) * sw


DIL_ROWS = 512


def _to_dilated(scr, val, out_ref, r):
    if r == 1:
        out_ref[...] = val.astype(out_ref.dtype)
        return
    n = val.shape[0] // r
    for hh in range(2):
        scr[hh] = val[:, hh * 128:(hh + 1) * 128]
        for pr in range(r):
            out_ref[:, pr * 256 + hh * 128:pr * 256 + (hh + 1) * 128] = scr[hh, pl.ds(pr, n, stride=r), :].astype(out_ref.dtype)


def _from_dilated(scr, in_ref, r):
    if r == 1:
        return in_ref[...].astype(F32)
    n = in_ref.shape[0]
    for hh in range(2):
        for pr in range(r):
            scr[hh, pl.ds(pr, n, stride=r), :] = in_ref[:, pr * 256 + hh * 128:pr * 256 + (hh + 1) * 128].astype(F32)
    return jnp.concatenate([scr[0], scr[1]], axis=1)


def _dil_spec(r):
    return pl.BlockSpec((DIL_ROWS // r, r * 256), lambda i: (i, 0))


def _dil_shape(s, r, dt):
    return jax.ShapeDtypeStruct((s // r, r * 256), dt)


_DIL_SCRATCH = [pltpu.VMEM((2, DIL_ROWS, 128), F32)]
_RS = tuple(r for _, r in ATT_GROUPS)


def _rope_fwd(p, cos_t, sin_t, s):
    def body(*refs):
        ins, cs, sn, outs, scr = refs[:9], refs[9][...], refs[10][...], refs[11:20], refs[20]
        for t in range(3):
            for g, r in enumerate(_RS):
                val = ins[3 * t + g][...].astype(F32)
                _to_dilated(scr, _rope_apply(val, cs, sn, False) if t < 2 else val, outs[3 * t + g], r)

    res = pl.pallas_call(
        body, name="rope", grid=(s // DIL_ROWS,),
        in_specs=[pl.BlockSpec((DIL_ROWS, 256), lambda i, c=base // 256 + g: (i, c)) for base in (P_AQ, P_AK, P_AV) for g in range(3)]
        + [pl.BlockSpec((DIL_ROWS, 128), lambda i: (i, 0))] * 2,
        out_specs=[_dil_spec(r) for _ in range(3) for r in _RS],
        out_shape=[_dil_shape(s, r, BF16) for _ in range(3) for r in _RS],
        scratch_shapes=_DIL_SCRATCH, compiler_params=_params(("parallel",)),
    )(*([p] * 9), cos_t, sin_t)
    return res[0:3], res[3:6], res[6:9]


def _attn_combine(att, s):
    def body(o0, o1, o2, l0, l1, l2, o_ref, lse_ref, od1, od2, ld1, ld2, scr):
        ov = [_from_dilated(scr, ref, r) for ref, r in zip((o0, o1, o2), _RS)]
        lv = [_from_dilated(scr, ref, r) for ref, r in zip((l0, l1, l2), _RS)]
        mx = jnp.maximum(jnp.maximum(lv[0], lv[1]), lv[2])
        ev = [jnp.exp(l - mx) for l in lv]
        z = ev[0] + ev[1] + ev[2]
        o = ((ev[0] * ov[0] + ev[1] * ov[1] + ev[2] * ov[2]) / z).astype(BF16)
        lse = mx + jnp.log(z)
        o_ref[...] = o
        lse_ref[...] = lse
        for ref, r in zip((od1, od2), _RS[1:]):
            _to_dilated(scr, o.astype(F32), ref, r)
        for ref, r in zip((ld1, ld2), _RS[1:]):
            _to_dilated(scr, lse, ref, r)

    return pl.pallas_call(
        body, name="attn_combine", grid=(s // DIL_ROWS,),
        in_specs=[_dil_spec(r) for r in _RS] * 2,
        out_specs=[_dil_spec(1)] * 2 + [_dil_spec(r) for r in _RS[1:]] * 2,
        out_shape=[_dil_shape(s, 1, BF16), _dil_shape(s, 1, F32)] + [_dil_shape(s, r, BF16) for r in _RS[1:]]
        + [_dil_shape(s, r, F32) for r in _RS[1:]],
        scratch_shapes=_DIL_SCRATCH, compiler_params=_params(("parallel",)),
    )(*[a[0] for a in att], *[a[1] for a in att])


def _dilate(t, s):
    def body(t_ref, o1, o2, scr):
        val = t_ref[...].astype(F32)
        for ref, r in zip((o1, o2), _RS[1:]):
            _to_dilated(scr, val, ref, r)

    return pl.pallas_call(
        body, name="attn_dilate", grid=(s // DIL_ROWS,), in_specs=[_dil_spec(1)], out_specs=[_dil_spec(r) for r in _RS[1:]],
        out_shape=[_dil_shape(s, r, t.dtype) for r in _RS[1:]], scratch_shapes=_DIL_SCRATCH, compiler_params=_params(("parallel",)),
    )(t)


def _rope_bwd(datt, d_glr, dp, cos_t, sin_t, s):
    tail = P_W - P_AQ

    def body(*refs):
        ins, cs, sn, glr, o_ref, scr = refs[:9], refs[9][...], refs[10][...], refs[11], refs[13], refs[14]
        for t in range(3):
            for g, r in enumerate(_RS):
                val = _from_dilated(scr, ins[3 * t + g], r)
                o_ref[:, t * ATT_W + g * 256:t * ATT_W + (g + 1) * 256] = (_rope_apply(val, cs, sn, True) if t < 2 else val).astype(BF16)
        o_ref[:, 3 * ATT_W:3 * ATT_W + 128] = glr[...]
        o_ref[:, 3 * ATT_W + 128:] = jnp.zeros((DIL_ROWS, tail - 3 * ATT_W - 128), BF16)

    return pl.pallas_call(
        body, name="rope_bwd", grid=(s // DIL_ROWS,),
        in_specs=[_dil_spec(r) for _ in range(3) for r in _RS] + [pl.BlockSpec((DIL_ROWS, 128), lambda i: (i, 0))] * 3
        + [pl.BlockSpec(memory_space=pl.ANY)],
        out_specs=pl.BlockSpec((DIL_ROWS, tail), lambda i: (i, P_AQ // tail)),
        out_shape=jax.ShapeDtypeStruct((s, P_W), BF16), input_output_aliases={12: 0},
        scratch_shapes=_DIL_SCRATCH, compiler_params=_params(("parallel",)),
    )(*[datt[g][t] for t in range(3) for g in range(3)], cos_t, sin_t, d_glr, dp)


def _tri_dot(tri, t):
    tb = tri.astype(BF16)
    hi = t.astype(BF16)
    r1 = t - hi.astype(F32)
    mid = r1.astype(BF16)
    lo = (r1 - mid.astype(F32)).astype(BF16)
    return _dg(tb, hi, 1, 0) + _dg(tb, mid, 1, 0) + _dg(tb, lo, 1, 0)


def _gla_decays(la_c, tri):
    b = _tri_dot(tri, la_c)
    row = lax.broadcasted_iota(jnp.int32, b.shape, 0)
    bmid = jnp.sum(jnp.where(row == GLA_CHUNK // 2 - 1, b, 0.0), axis=0, keepdims=True)
    blast = jnp.sum(jnp.where(row == GLA_CHUNK - 1, b, 0.0), axis=0, keepdims=True)
    return b, bmid, blast


def _gla_fwd(p, la, s, comm=()):
    tb, ch = GLA_BLOCK, GLA_CHUNK
    nb, nc = s // tb, tb // ch
    scale = GLA_DK ** -0.5
    c_ins, c_outs, c_alias, c_scratch = _carry(comm, 4, 2)

    def body(q_ref, k_ref, v_ref, la_ref, *rest):
        ci, (o_ref, st_ref) = rest[:len(c_ins)], rest[len(c_ins):len(c_ins) + 2]
        co, state = rest[len(c_ins) + 2:len(c_ins) + 2 + len(c_outs)], rest[len(c_ins) + 2 + len(c_outs)]
        step = pl.program_id(0)
        if comm:
            @pl.when(step == 0)
            def _():
                _comm_phase(comm, ci, co, rest[-2], rest[-1], True)

        _gla_fwd_step(q_ref, k_ref, v_ref, la_ref, o_ref, st_ref, state)
        if comm:
            @pl.when(step == nb - 1)
            def _():
                _comm_phase(comm, ci, co, rest[-2], rest[-1], False)

    def _gla_fwd_step(q_ref, k_ref, v_ref, la_ref, o_ref, st_ref, state):
        @pl.when(pl.program_id(0) == 0)
        def _():
            state[...] = jnp.zeros_like(state)

        ri = lax.broadcasted_iota(jnp.int32, (ch, ch), 0)
        ci = lax.broadcasted_iota(jnp.int32, (ch, ch), 1)
        causal = ci <= ri
        tri = causal.astype(F32)

        def chunk(c, carry):
            sl = pl.ds(pl.multiple_of(c * ch, ch), ch)
            b, bmid, blast = _gla_decays(la_ref[sl, :], tri)
            q = q_ref[sl, :].astype(F32) * scale
            k = k_ref[sl, :].astype(F32)
            v = v_ref[sl, :]
            qgt = (q * jnp.exp(b)).astype(BF16)
            qgn = (q * jnp.exp(b - bmid)).astype(BF16)
            kgn = (k * jnp.exp(bmid - b)).astype(BF16)
            kd = (k * jnp.exp(blast - b)).astype(BF16)
            dec = jnp.exp(blast)
            sts = [state[h] for h in range(GLA_H)]
            outs, news = [], []
            for h in range(GLA_H):
                hk, hv = slice(h * GLA_DK, (h + 1) * GLA_DK), slice(h * GLA_DV, (h + 1) * GLA_DV)
                a = jnp.where(causal, _dg(qgn[:, hk], kgn[:, hk], 1, 1), 0.0)
                outs.append(_dg(a.astype(BF16), v[:, hv], 1, 0) + _dg(qgt[:, hk], sts[h].astype(BF16), 1, 1))
                news.append(dec[:, hk] * sts[h] + _dg(v[:, hv], kd[:, hk], 0, 0))
            for h in range(GLA_H):
                st_ref[h, c] = sts[h]
                state[h] = news[h]
            o_ref[sl, :] = jnp.concatenate(outs, axis=1)
            return carry

        lax.fori_loop(0, nc, chunk, 0)

    hw = GLA_H * GLA_DK
    res = pl.pallas_call(
        body, name="gla_fwd", grid=(nb,),
        in_specs=[pl.BlockSpec((tb, hw), lambda t: (t, P_GQ // hw)),
                  pl.BlockSpec((tb, hw), lambda t: (t, P_GK // hw)),
                  pl.BlockSpec((tb, GLA_H * GLA_DV), lambda t: (t, P_GV // (GLA_H * GLA_DV))),
                  pl.BlockSpec((tb, hw), lambda t: (t, 0))] + [HBM] * len(c_ins),
        out_specs=[pl.BlockSpec((tb, GLA_H * GLA_DV), lambda t: (t, 0)),
                   pl.BlockSpec((GLA_H, nc, GLA_DV, GLA_DK), lambda t: (0, t, 0, 0))] + [HBM] * len(c_outs),
        out_shape=[jax.ShapeDtypeStruct((s, GLA_H * GLA_DV), F32),
                   jax.ShapeDtypeStruct((GLA_H, s // ch, GLA_DV, GLA_DK), F32)] + c_outs,
        scratch_shapes=[pltpu.VMEM((GLA_H, GLA_DV, GLA_DK), F32)] + c_scratch,
        input_output_aliases=c_alias,
        compiler_params=_params(("arbitrary",)),
    )(p, p, p, la, *c_ins)
    return res[0], res[1], _split_units(comm, res[2:])


def _gla_bwd(p, la, states, do, s, dp, comm=()):
    tb, ch = GLA_BLOCK, GLA_CHUNK
    nb, nc = s // tb, tb // ch
    scale = GLA_DK ** -0.5
    c_ins, c_outs, c_alias, c_scratch = _carry(comm, 7, 4)

    def body(q_ref, k_ref, v_ref, la_ref, st_ref, do_ref, dp_in, *rest):
        ci, outs = rest[:len(c_ins)], rest[len(c_ins):len(c_ins) + 4]
        co, dstate = rest[len(c_ins) + 4:len(c_ins) + 4 + len(c_outs)], rest[len(c_ins) + 4 + len(c_outs)]
        step = pl.program_id(0)
        if comm:
            @pl.when(step == 0)
            def _():
                _comm_phase(comm, ci, co, rest[-2], rest[-1], True)

        _gla_bwd_step(q_ref, k_ref, v_ref, la_ref, st_ref, do_ref, *outs, dstate)
        if comm:
            @pl.when(step == nb - 1)
            def _():
                _comm_phase(comm, ci, co, rest[-2], rest[-1], False)

    def _gla_bwd_step(q_ref, k_ref, v_ref, la_ref, st_ref, do_ref, dq_ref, dk_ref, dv_ref, dla_ref, dstate):
        @pl.when(pl.program_id(0) == 0)
        def _():
            dstate[...] = jnp.zeros_like(dstate)

        ri = lax.broadcasted_iota(jnp.int32, (ch, ch), 0)
        ci = lax.broadcasted_iota(jnp.int32, (ch, ch), 1)
        causal = ci <= ri
        tri = causal.astype(F32)
        tri_t = (ci >= ri).astype(F32)

        def chunk(cc, carry):
            c = nc - 1 - cc
            sl = pl.ds(pl.multiple_of(c * ch, ch), ch)
            b, bmid, blast = _gla_decays(la_ref[sl, :], tri)
            q = q_ref[sl, :].astype(F32) * scale
            k = k_ref[sl, :].astype(F32)
            v = v_ref[sl, :]
            e_b, e_qn, e_kn, e_kd = jnp.exp(b), jnp.exp(b - bmid), jnp.exp(bmid - b), jnp.exp(blast - b)
            dec = jnp.exp(blast)
            qgt, qgn, kgn, kd = q * e_b, q * e_qn, k * e_kn, k * e_kd
            qgt_b, qgn_b, kgn_b, kd_b = qgt.astype(BF16), qgn.astype(BF16), kgn.astype(BF16), kd.astype(BF16)
            do_b = do_ref[sl, :].astype(BF16)
            st0s = [st_ref[h, c] for h in range(GLA_H)]
            dsts = [dstate[h] for h in range(GLA_H)]
            dqgn, dqgt, dkgn, dkd, dvs, ddec, news = [], [], [], [], [], [], []
            for h in range(GLA_H):
                hk, hv = slice(h * GLA_DK, (h + 1) * GLA_DK), slice(h * GLA_DV, (h + 1) * GLA_DV)
                dst_b = dsts[h].astype(BF16)
                a = jnp.where(causal, _dg(qgn_b[:, hk], kgn_b[:, hk], 1, 1), 0.0).astype(BF16)
                da = jnp.where(causal, _dg(do_b[:, hv], v[:, hv], 1, 1), 0.0).astype(BF16)
                dqgn.append(_dg(da, kgn_b[:, hk], 1, 0))
                dqgt.append(_dg(do_b[:, hv], st0s[h].astype(BF16), 1, 0))
                dkgn.append(_dg(da, qgn_b[:, hk], 0, 0))
                dvs.append(_dg(a, do_b[:, hv], 0, 0) + _dg(kd_b[:, hk], dst_b, 1, 1))
                dkd.append(_dg(v[:, hv], dst_b, 1, 0))
                ddec.append(jnp.sum(st0s[h] * dsts[h], axis=0, keepdims=True))
                news.append(dec[:, hk] * dsts[h] + _dg(do_b[:, hv], qgt_b[:, hk], 0, 0))
            for h in range(GLA_H):
                dstate[h] = news[h]
            cat = lambda parts: jnp.concatenate(parts, axis=1)
            dqgn, dqgt, dkgn, dkd, ddec = cat(dqgn), cat(dqgt), cat(dkgn), cat(dkd), cat(ddec)
            dq_ref[sl, :] = (scale * (dqgn * e_qn + dqgt * e_b)).astype(dq_ref.dtype)
            dk_ref[sl, :] = (dkgn * e_kn + dkd * e_kd).astype(dk_ref.dtype)
            dv_ref[sl, :] = cat(dvs).astype(dv_ref.dtype)
            db = dqgn * qgn + dqgt * qgt - dkgn * kgn - dkd * kd
            extra = jnp.sum(dkd * kd, axis=0, keepdims=True) + ddec * dec
            dla_ref[sl, :] = _tri_dot(tri_t, db) + extra
            return carry

        lax.fori_loop(0, nc, chunk, 0)

    rev = lambda t: nb - 1 - t
    hw, vw = GLA_H * GLA_DK, GLA_H * GLA_DV
    res = pl.pallas_call(
        body, name="gla_bwd", grid=(nb,),
        in_specs=[pl.BlockSpec((tb, hw), lambda t: (rev(t), P_GQ // hw)),
                  pl.BlockSpec((tb, hw), lambda t: (rev(t), P_GK // hw)),
                  pl.BlockSpec((tb, vw), lambda t: (rev(t), P_GV // vw)),
                  pl.BlockSpec((tb, hw), lambda t: (rev(t), 0)),
                  pl.BlockSpec((GLA_H, nc, GLA_DV, GLA_DK), lambda t: (0, rev(t), 0, 0)),
                  pl.BlockSpec((tb, vw), lambda t: (rev(t), 0)), pl.BlockSpec(memory_space=pl.ANY)] + [HBM] * len(c_ins),
        out_specs=[pl.BlockSpec((tb, hw), lambda t: (rev(t), 0)),
                   pl.BlockSpec((tb, hw), lambda t: (rev(t), 0)),
                   pl.BlockSpec((tb, vw), lambda t: (rev(t), P_GV // vw)),
                   pl.BlockSpec((tb, hw), lambda t: (rev(t), 0))] + [HBM] * len(c_outs),
        out_shape=[jax.ShapeDtypeStruct((s, hw), BF16),
                   jax.ShapeDtypeStruct((s, hw), BF16),
                   jax.ShapeDtypeStruct((s, P_W), BF16),
                   jax.ShapeDtypeStruct((s, hw), F32)] + c_outs,
        scratch_shapes=[pltpu.VMEM((GLA_H, GLA_DV, GLA_DK), F32)] + c_scratch,
        input_output_aliases={6: 2, **c_alias},
        compiler_params=_params(("arbitrary",)),
    )(p, p, p, la, states, do, dp, *c_ins)
    return res[0], res[1], res[2], res[3], _split_units(comm, res[4:])


def _head_masks():
    lane = lax.broadcasted_iota(jnp.int32, (1, 4 * ATT_HD), 1)
    return [(lane >= h * ATT_HD) & (lane < (h + 1) * ATT_HD) for h in range(4)]


def _attn_fwd(qv, kv, pv, g, r, s):
    ln = s // r
    nblk = ln // ATT_BLK
    qcol = lambda pr: pr
    vcol = qcol
    prev = lambda n: jnp.maximum(n - 1, 0)

    def body(q_ref, kp_ref, kc_ref, vp_ref, vc_ref, o_ref, lse_ref):
        has_prev = pl.program_id(1) > 0
        ri = lax.broadcasted_iota(jnp.int32, (ATT_BLK, ATT_BLK), 0)
        ci = lax.broadcasted_iota(jnp.int32, (ATT_BLK, ATT_BLK), 1)
        m_cur = ci <= ri
        m_prev = (ci >= ri) & has_prev
        q, kp, kc, vp, vc = q_ref[...], kp_ref[...], kc_ref[...], vp_ref[...], vc_ref[...]
        o = jnp.zeros((ATT_BLK, 256), F32)
        lse = jnp.zeros((ATT_BLK, 256), F32)
        for hm in _head_masks():
            qm = jnp.where(hm, q, jnp.zeros_like(q))
            sc = jnp.where(m_cur, _dg(qm, kc, 1, 1) * 0.125, NEG)
            sp = jnp.where(m_prev, _dg(qm, kp, 1, 1) * 0.125, NEG)
            mx = jnp.maximum(jnp.max(sc, axis=1, keepdims=True), jnp.max(sp, axis=1, keepdims=True))
            pc, pp = jnp.exp(sc - mx), jnp.exp(sp - mx)
            den = jnp.sum(pc, axis=1, keepdims=True) + jnp.sum(pp, axis=1, keepdims=True)
            oh = (_dg(pc.astype(BF16), vc, 1, 0) + _dg(pp.astype(BF16), vp, 1, 0)) / den
            o = jnp.where(hm, oh, o)
            lse = jnp.where(hm, mx + jnp.log(den), lse)
        o_ref[...] = o.astype(o_ref.dtype)
        lse_ref[...] = lse

    blk = (ATT_BLK, 256)
    o, lse = pl.pallas_call(
        body, name=f"attn_fwd_{g}", grid=(r, nblk),
        in_specs=[pl.BlockSpec(blk, lambda pr, n: (n, qcol(pr))),
                  pl.BlockSpec(blk, lambda pr, n: (prev(n), qcol(pr))),
                  pl.BlockSpec(blk, lambda pr, n: (n, qcol(pr))),
                  pl.BlockSpec(blk, lambda pr, n: (prev(n), vcol(pr))),
                  pl.BlockSpec(blk, lambda pr, n: (n, vcol(pr)))],
        out_specs=[pl.BlockSpec(blk, lambda pr, n: (n, pr)), pl.BlockSpec(blk, lambda pr, n: (n, pr))],
        out_shape=[jax.ShapeDtypeStruct((ln, r * 256), BF16), jax.ShapeDtypeStruct((ln, r * 256), F32)],
        compiler_params=_params(("parallel", "parallel")),
    )(qv, kv, kv, pv, pv)
    return o, lse


def _attn_bwd(qv, kv, pv, dov, ov, lv, g, r, s):
    ln = s // r
    nblk = ln // ATT_BLK
    qcol = lambda pr: pr
    vcol = qcol
    prev = lambda n: jnp.maximum(n - 1, 0)
    nxt = lambda n: jnp.minimum(n + 1, nblk - 1)

    def body(qc_ref, qn_ref, kp_ref, kc_ref, vp_ref, vc_ref, doc_ref, don_ref, oc_ref, on_ref, lc_ref, ln_ref,
             dq_ref, dk_ref, dv_ref):
        n = pl.program_id(1)
        has_prev, has_next = n > 0, n < nblk - 1
        ri = lax.broadcasted_iota(jnp.int32, (ATT_BLK, ATT_BLK), 0)
        ci = lax.broadcasted_iota(jnp.int32, (ATT_BLK, ATT_BLK), 1)
        m_cur = ci <= ri
        m_prev = (ci >= ri) & has_prev
        m_next = (ci >= ri) & has_next
        qc, qn, kp, kc, vp, vc = qc_ref[...], qn_ref[...], kp_ref[...], kc_ref[...], vp_ref[...], vc_ref[...]
        doc, don = doc_ref[...], don_ref[...]
        pc_full = doc.astype(F32) * oc_ref[...].astype(F32)
        pn_full = don.astype(F32) * on_ref[...].astype(F32)
        lc, lnx = lc_ref[...], ln_ref[...]
        dq = jnp.zeros((ATT_BLK, 256), F32)
        dk = jnp.zeros((ATT_BLK, 256), F32)
        dv = jnp.zeros((ATT_BLK, 256), F32)
        zb = jnp.zeros_like(qc)
        for hm in _head_masks():
            qcm, qnm = jnp.where(hm, qc, zb), jnp.where(hm, qn, zb)
            docm, donm = jnp.where(hm, doc, zb), jnp.where(hm, don, zb)
            lse_c = jnp.max(jnp.where(hm, lc, NEG), axis=1, keepdims=True)
            lse_n = jnp.max(jnp.where(hm, lnx, NEG), axis=1, keepdims=True)
            del_c = jnp.sum(jnp.where(hm, pc_full, 0.0), axis=1, keepdims=True)
            del_n = jnp.sum(jnp.where(hm, pn_full, 0.0), axis=1, keepdims=True)
            pr_ = jnp.where(m_cur, jnp.exp(_dg(qcm, kc, 1, 1) * 0.125 - lse_c), 0.0)
            ds = (pr_ * (_dg(docm, vc, 1, 1) - del_c) * 0.125).astype(BF16)
            dqh = _dg(ds, kc, 1, 0)
            dkh = _dg(ds, qc, 0, 0)
            dvh = _dg(pr_.astype(BF16), doc, 0, 0)
            pr_ = jnp.where(m_prev, jnp.exp(_dg(qcm, kp, 1, 1) * 0.125 - lse_c), 0.0)
            ds = (pr_ * (_dg(docm, vp, 1, 1) - del_c) * 0.125).astype(BF16)
            dqh = dqh + _dg(ds, kp, 1, 0)
            pr_ = jnp.where(m_next, jnp.exp(_dg(qnm, kc, 1, 1) * 0.125 - lse_n), 0.0)
            ds = (pr_ * (_dg(donm, vc, 1, 1) - del_n) * 0.125).astype(BF16)
            dkh = dkh + _dg(ds, qn, 0, 0)
            dvh = dvh + _dg(pr_.astype(BF16), don, 0, 0)
            dq = jnp.where(hm, dqh, dq)
            dk = jnp.where(hm, dkh, dk)
            dv = jnp.where(hm, dvh, dv)
        dq_ref[...] = dq.astype(dq_ref.dtype)
        dk_ref[...] = dk.astype(dk_ref.dtype)
        dv_ref[...] = dv.astype(dv_ref.dtype)

    blk = (ATT_BLK, 256)
    cur = lambda col: pl.BlockSpec(blk, lambda pr, n: (n, col(pr)))
    prv = lambda col: pl.BlockSpec(blk, lambda pr, n: (prev(n), col(pr)))
    nx = lambda col: pl.BlockSpec(blk, lambda pr, n: (nxt(n), col(pr)))
    own = lambda pr: pr
    outs = pl.pallas_call(
        body, name=f"attn_bwd_{g}", grid=(r, nblk),
        in_specs=[cur(qcol), nx(qcol), prv(qcol), cur(qcol), prv(vcol), cur(vcol),
                  cur(own), nx(own), cur(own), nx(own), cur(own), nx(own)],
        out_specs=[cur(own), cur(own), cur(own)],
        out_shape=[jax.ShapeDtypeStruct((ln, r * 256), BF16)] * 3,
        compiler_params=_params(("parallel", "parallel")),
    )(qv, qv, kv, kv, pv, pv, dov, dov, ov, ov, lv, lv)
    return outs


def _gelu_parts(gv):
    cdf = 0.5 * (1.0 + lax.erf(gv * (2.0 ** -0.5)))
    pdf = jnp.exp(-0.5 * gv * gv) * (1.0 / math.sqrt(2.0 * math.pi))
    return cdf, pdf


def _pick_row(t, k):
    row = lax.broadcasted_iota(jnp.int32, t.shape, 0)
    return jnp.sum(jnp.where(row == k, t, 0.0), axis=0, keepdims=True)


def _shift_rows(u, halo, n):
    row = lax.broadcasted_iota(jnp.int32, u.shape, 0)
    out = pltpu.roll(u, n, 0)
    for k in range(n):
        out = jnp.where(row == k, _pick_row(halo, 16 - n + k), out)
    return out


def _shift_rows_up(u, halo, n):
    rb = u.shape[0]
    row = lax.broadcasted_iota(jnp.int32, u.shape, 0)
    out = pltpu.roll(u, rb - n, 0)
    for k in range(n):
        out = jnp.where(row == rb - n + k, _pick_row(halo, k), out)
    return out


def _conv(u, halo, cw, cb):
    return cb + _pick_row(cw, 0) * _shift_rows(u, halo, 2) + _pick_row(cw, 1) * _shift_rows(u, halo, 1) + _pick_row(cw, 2) * u


def _local_step(x, mod, pos_col, target, sm, w_sh, g0, chip, core):
    s = x.shape[0]
    shift1, scale1, gate1, shift2, scale2, gate2 = [mod[i:i + 1, :] for i in range(6)]
    rb = 256
    chip1 = chip.reshape(1)

    def f_norm1(c, i, xv, nw, sc, sh):
        return ((xv * _rms(xv) * nw) * (1.0 + sc) + sh,)

    (h,) = _rowcall(f_norm1, [_rows(x, rb), _full(sm["n1w"]), _full(scale1), _full(shift1)],
                    [_orow(s, D, BF16, rb)], n_rows=s, rb=rb, name="norm1")
    own = lambda got, i: lax.dynamic_update_slice(got, w_sh[i], (chip, 0, 0))
    invf = jnp.tile(ROPE_THETA ** (-jnp.arange(ATT_HD // 2, dtype=F32) / (ATT_HD // 2)), 4).reshape(1, 128)
    cos_t, sin_t = _rope_tables(pos_col, invf, s)
    _, got0 = _unit_wait(*g0[:4], after=[h, cos_t, sin_t], name="gather_w_in_wait")
    [got0] = _comm_call("gather_w_in_d2d", [_u_gather_d2d(got0, (0,))])
    w = dict(win=_win_assemble(own(got0[0], 0)))
    p, [got123, got4] = _mm(h, w["win"], "in_proj", tm=1024, tn=1536,
                            comm=[_u_gather_ici(w_sh, (1, 2, 3)), _u_gather_ici(w_sh, (4,))])

    def f_gla_pre(c, i, glr, w2, gb):
        z = _dg(glr, w2.astype(BF16), 1, 0) + gb
        return ((jnp.minimum(z, 0.0) - jnp.log(1.0 + jnp.exp(-jnp.abs(z)))) * (1.0 / GLA_TAU),)

    (la,) = _rowcall(f_gla_pre, [_rows(p, rb, 128, P_LR // 128), _full(sm["w2"]), _full(sm["gb"])],
                     [_orow(s, 512, F32, rb)], n_rows=s, rb=rb, name="gla_pre")
    o_gla, states, [got123, got5] = _gla_fwd(p, la, s, comm=[_u_gather_d2d(got123, (1, 2, 3)), _u_gather_ici(w_sh, (5,))])
    got45 = got4 + got5
    w.update(wgb=own(got123[0], 1).reshape(1024, D), wab=_cols_join(own(got123[1], 2)), wout=own(got123[2], 3).reshape(D, D))

    def f_gla_post(c, i, ov, gnw, gr):
        on = jnp.concatenate([ov[:, k * 256:(k + 1) * 256] * _rms(ov[:, k * 256:(k + 1) * 256]) * gnw
                              for k in range(GLA_H)], axis=1)
        g = gr.astype(F32)
        return (on * (g * _sigmoid(g)),)

    (og,) = _rowcall(f_gla_post, [_rows(o_gla, rb), _full(sm["gnw"]), _rows(p, rb, 1024, P_GR // 1024)],
                     [_orow(s, 1024, BF16, rb)], n_rows=s, rb=rb, name="gla_post")
    y_gla = _mm(og, w["wgb"], "gla_branch")

    q_d, k_d, v_d = _rope_fwd(p, cos_t, sin_t, s)
    att = [_attn_fwd(q_d[g], k_d[g], v_d[g], g, r, s) for g, r in enumerate(_RS)]
    o_att, lse, o_d1, o_d2, lse_d1, lse_d2 = _attn_combine(att, s)
    y_att = _mm(o_att, w["wab"], "attn_branch")

    def f_merge(c, i, ma, mb, yg, ya):
        return (_sigmoid(ma.astype(F32)) * yg.astype(F32) + _sigmoid(mb.astype(F32)) * ya.astype(F32),)

    (mixed,) = _rowcall(f_merge, [_rows(p, rb, D, P_MA // D), _rows(p, rb, D, P_MB // D), _rows(y_gla, rb), _rows(y_att, rb)],
                        [_orow(s, D, BF16, rb)], n_rows=s, rb=rb, name="merge")
    z1, [got45] = _mm(mixed, w["wout"], "out_proj", comm=[_u_gather_d2d(got45, (4, 5))])
    w.update(wup=own(got45[0], 4), wdown=own(got45[1], 5).reshape(D_FF, D))

    def f_norm2(c, i, xv, z, g1, nw, sc, sh):
        x1 = xv + g1 * z.astype(F32)
        return (x1, (x1 * _rms(x1) * nw) * (1.0 + sc) + sh)

    x1, h2 = _rowcall(f_norm2, [_rows(x, rb), _rows(z1, rb), _full(gate1), _full(sm["n2w"]), _full(scale2), _full(shift2)],
                      [_orow(s, D, F32, rb), _orow(s, D, BF16, rb)], n_rows=s, rb=rb, name="norm2")
    u = _mm(h2, w["wup"], "up_proj", b_shards=True)

    cwid = 2 * W_UP_SH

    def f_ffn(c, i, uv, hl, cw, cb):
        uc = _conv(uv.astype(F32), hl.astype(F32) * (i > 0).astype(F32), cw, cb)
        val, gt = uc[:, :W_UP_SH], uc[:, W_UP_SH:]
        cdf, _ = _gelu_parts(gt)
        return (gt * cdf * val,)

    ccol = lambda c: c
    (hidden,) = _rowcall(f_ffn, [_rows(u, rb, cwid, ccol), _halo(u, rb, 16, cwid, ccol, True),
                                 _full(sm["cw"], cwid, ccol), _full(sm["cb"], cwid, ccol)],
                         [_orow(s, D_FF, BF16, rb, W_UP_SH, ccol)], n_rows=s, rb=rb, name="conv_geglu", ncol=2)
    z2 = _mm(hidden, w["wdown"], "down_proj", tk=D_FF)

    def f_final(c, i, x1v, z, g2, fw, tgt):
        x2 = x1v + g2 * z.astype(F32)
        r = _rms(x2)
        xh = x2 * r
        e = xh * fw - tgt
        loss = 0.5 * jnp.sum(jnp.mean(e * e, axis=-1, keepdims=True), axis=0, keepdims=True)
        dy = e * (1.0 / D)
        dxh = dy * fw
        dx2 = r * (dxh - xh * jnp.mean(dxh * xh, axis=-1, keepdims=True))
        return (loss, dx2, dx2 * g2, _csum(dy * xh), _csum(dx2 * z.astype(F32)))

    loss, dx2, dz2, d_fnw, d_gate2 = _rowcall(
        f_final, [_rows(x1, rb), _rows(z2, rb), _full(gate2), _full(sm["fnw"]), _rows(target, rb)],
        [_oacc(1, 1), _orow(s, D, F32, rb), _orow(s, D, BF16, rb), _oacc(1, D), _oacc(1, D)],
        n_rows=s, rb=rb, name="final_loss")
    d_hidden = _mm(dz2, w["wdown"], "down_proj_dx", tb=True, tn=1408)
    g_wdown = _mm(hidden, dz2, "down_proj_dw", ta=True, out_dtype=F32, tm=1408, tn=1024, tk=2048)

    def f_ffn_bwd(c, i, uv, hl, dh, cw, cb):
        uf = uv.astype(F32)
        hf = hl.astype(F32) * (i > 0).astype(F32)
        u1, u2 = _shift_rows(uf, hf, 1), _shift_rows(uf, hf, 2)
        uc = cb + _pick_row(cw, 0) * u2 + _pick_row(cw, 1) * u1 + _pick_row(cw, 2) * uf
        val, gt = uc[:, :W_UP_SH], uc[:, W_UP_SH:]
        cdf, pdf = _gelu_parts(gt)
        dhf = dh.astype(F32)
        duc = jnp.concatenate([dhf * (gt * cdf), dhf * val * (cdf + gt * pdf)], axis=1)
        dcw = jnp.concatenate([_csum(duc * u2), _csum(duc * u1), _csum(duc * uf)], axis=0)
        return (duc, _csum(duc), dcw)

    duc, d_cb, d_cw = _rowcall(
        f_ffn_bwd, [_rows(u, rb, cwid, ccol), _halo(u, rb, 16, cwid, ccol, True), _rows(d_hidden, rb, W_UP_SH, ccol),
                    _full(sm["cw"], cwid, ccol), _full(sm["cb"], cwid, ccol)],
        [_orow(s, 2 * D_FF, BF16, rb, cwid, ccol), _oacc(1, 2 * D_FF, cwid, ccol), _oacc(3, 2 * D_FF, cwid, ccol)],
        n_rows=s, rb=rb, name="conv_geglu_bwd", ncol=2)

    def f_conv_t(c, i, dv, hl, cw):
        df = dv.astype(F32)
        hf = hl.astype(F32) * (i < s // rb - 1).astype(F32)
        return (_pick_row(cw, 2) * df + _pick_row(cw, 1) * _shift_rows_up(df, hf, 1) + _pick_row(cw, 0) * _shift_rows_up(df, hf, 2),)

    (du,) = _rowcall(f_conv_t, [_rows(duc, rb, cwid, ccol), _halo(duc, rb, 16, cwid, ccol, False), _full(sm["cw"], cwid, ccol)],
                     [_orow(s, 2 * D_FF, BF16, rb, cwid, ccol)], n_rows=s, rb=rb, name="conv_transpose", ncol=2)
    g_wup = _mm(h2, du, "up_proj_dw", ta=True, out_dtype=F32, tm=1024, tk=2048, o_shards=True)
    gs45 = [g_wup, g_wdown.reshape(4, W_DOWN_SH, 1024)]
    d_h2, [land45] = _mm(du, w["wup"], "up_proj_dx", tb=True, b_shards=True, comm=[_u_pair_send(gs45, (4, 5))])
    ts45 = [_pair_add(g, ld, core, "grad_pair_add_" + BIG[i]) for g, ld, i in zip(gs45, land45, (4, 5))]

    def f_norm2_bwd(c, i, x1v, dh, dxr, z, nw, sc, g1):
        dxn, dsh, dsc, dnw = _norm_bwd(x1v, dh.astype(F32), nw, sc)
        dx1 = dxr + dxn
        return (dx1, dx1 * g1, dsh, dsc, dnw, _csum(dx1 * z.astype(F32)))

    dx1, dz1, d_shift2, d_scale2, d_n2w, d_gate1 = _rowcall(
        f_norm2_bwd, [_rows(x1, rb), _rows(d_h2, rb), _rows(dx2, rb), _rows(z1, rb), _full(sm["n2w"]), _full(scale2), _full(gate1)],
        [_orow(s, D, F32, rb), _orow(s, D, BF16, rb), _oacc(1, D), _oacc(1, D), _oacc(1, D), _oacc(1, D)],
        n_rows=s, rb=rb, name="norm2_bwd")
    d_mixed = _mm(dz1, w["wout"], "out_proj_dx", tb=True)
    g_wout = _mm(mixed, dz1, "out_proj_dw", ta=True, out_dtype=F32, tk=2048)

    def f_merge_bwd(c, i, dm, ma, mb, yg, ya):
        dmf, ygf, yaf = dm.astype(F32), yg.astype(F32), ya.astype(F32)
        sa, sb = _sigmoid(ma.astype(F32)), _sigmoid(mb.astype(F32))
        return (dmf * sa, dmf * sb, jnp.concatenate([dmf * ygf * sa * (1.0 - sa), dmf * yaf * sb * (1.0 - sb)], axis=1))

    dy_gla, dy_att, dp = _rowcall(
        f_merge_bwd, [_rows(d_mixed, rb), _rows(p, rb, D, P_MA // D), _rows(p, rb, D, P_MB // D), _rows(y_gla, rb), _rows(y_att, rb)],
        [_orow(s, D, BF16, rb)] * 2 + [_orow(s, P_W, BF16, rb, 2 * D, lambda c: P_MA // (2 * D))], n_rows=s, rb=rb, name="merge_bwd")
    d_og = _mm(dy_gla, w["wgb"], "gla_branch_dx", tb=True)
    g_wgb = _mm(og, dy_gla, "gla_branch_dw", ta=True, out_dtype=F32, tk=2048)
    d_oatt = _mm(dy_att, w["wab"], "attn_branch_dx", tb=True)
    g_wab = _mm(o_att, dy_att, "attn_branch_dw", ta=True, out_dtype=F32, tk=2048)

    def f_gla_post_bwd(c, i, ov, gnw, gr, dog):
        g = gr.astype(F32)
        sg = _sigmoid(g)
        silu = g * sg
        dof = dog.astype(F32)
        don = dof * silu
        on_parts, do_parts, dgn = [], [], jnp.zeros((1, 256), F32)
        for k in range(GLA_H):
            oh = ov[:, k * 256:(k + 1) * 256]
            dh = don[:, k * 256:(k + 1) * 256]
            r = _rms(oh)
            xh = oh * r
            dgn = dgn + _csum(dh * xh)
            dxh = dh * gnw
            do_parts.append(r * (dxh - xh * jnp.mean(dxh * xh, axis=-1, keepdims=True)))
            on_parts.append(xh * gnw)
        on = jnp.concatenate(on_parts, axis=1)
        dgr = dof * on * (sg * (1.0 + g * (1.0 - sg)))
        return (jnp.concatenate(do_parts, axis=1), dgr, dgn)

    do_gla, dp, d_gnw = _rowcall(
        f_gla_post_bwd, [_rows(o_gla, rb), _full(sm["gnw"]), _rows(p, rb, 1024, P_GR // 1024), _rows(d_og, rb)],
        [_orow(s, 1024, F32, rb), _orow(s, P_W, BF16, rb, 1024, lambda c: P_GR // 1024), _oacc(1, 256)],
        n_rows=s, rb=rb, name="gla_post_bwd", into=(dp, 1))
    gs123 = [g_wgb.reshape(4, 256, 1024), _cols_split(g_wab), g_wout.reshape(4, 256, 1024)]
    d_gq, d_gk, dp, d_la, [r4, land123] = _gla_bwd(p, la, states, do_gla, s, dp,
                                                   comm=[_u_chip_exchange(ts45[:1]), _u_pair_send(gs123, (1, 2, 3))])
    half4 = [_chip_sum(ts45[0], r4[0], chip1, "grad_chip_sum_w_up")]
    ts123 = [_pair_add(g, ld, core, "grad_pair_add_" + BIG[i]) for g, ld, i in zip(gs123, land123, (1, 2, 3))]

    def f_gla_pre_bwd(c, i, lav, dlav, glr, w2):
        dz = dlav * (1.0 / GLA_TAU) * (1.0 - jnp.exp(GLA_TAU * lav))
        dzb = dz.astype(BF16)
        return (_dg(dzb, w2.astype(BF16), 1, 1), _csum(dz), _dg(glr, dzb, 0, 0))

    d_glr, d_gb, d_w2 = _rowcall(
        f_gla_pre_bwd, [_rows(la, rb), _rows(d_la, rb), _rows(p, rb, 128, P_LR // 128), _full(sm["w2"])],
        [_orow(s, 128, BF16, rb), _oacc(1, 512), _oacc(128, 512)], n_rows=s, rb=rb, name="gla_pre_bwd")

    do_d = [d_oatt] + list(_dilate(d_oatt, s))
    datt = [_attn_bwd(q_d[g], k_d[g], v_d[g], do_d[g], (o_att, o_d1, o_d2)[g], (lse, lse_d1, lse_d2)[g], g, r, s)
            for g, r in enumerate(_RS)]
    dp = _rope_bwd(datt, d_glr, dp, cos_t, sin_t, s)
    dp = lax.dynamic_update_slice(dp, jnp.concatenate([d_gq, d_gk], axis=1), (0, P_GQ))
    g_win, [r1235, oth4] = _mm(h, dp, "in_proj_dw", ta=True, out_dtype=F32, tm=1024, tn=1536, tk=2048,
                               comm=[_u_chip_exchange(ts123 + ts45[1:]), _u_pair_join(half4)])
    half1235 = [_chip_sum(t, r, chip1, "grad_chip_sum_" + BIG[i]) for t, r, i in zip(ts123 + ts45[1:], r1235, (1, 2, 3, 5))]
    gs0 = [_win_split(g_win)]
    d_h, [land0, oth1235] = _mm(dp, w["win"], "in_proj_dx", tb=True, tk=3840,
                                comm=[_u_pair_send(gs0, (0,)), _u_pair_join(half1235)])
    half123, half45 = half1235[:3], half4 + half1235[3:]
    oth123, oth45 = oth1235[:3], oth4 + oth1235[3:]
    ts0 = _pair_add(gs0[0], land0[0], core, "grad_pair_add_w_in")

    def f_norm1_bwd(c, i, xv, dh, dxr, nw, sc):
        dxn, dsh, dsc, dnw = _norm_bwd(xv, dh.astype(F32), nw, sc)
        return (dxr + dxn, dsh, dsc, dnw)

    grad_x, d_shift1, d_scale1, d_n1w = _rowcall(
        f_norm1_bwd, [_rows(x, rb), _rows(d_h, rb), _rows(dx1, rb), _full(sm["n1w"]), _full(scale1)],
        [_orow(s, D, F32, rb), _oacc(1, D), _oacc(1, D), _oacc(1, D)], n_rows=s, rb=rb, name="norm1_bwd")

    dmod = jnp.concatenate([d_shift1, d_scale1, d_gate1, d_shift2, d_scale2, d_gate2], axis=1)
    small = dict(dmod=dmod, n1w=d_n1w, gb=d_gb, gnw=d_gnw, n2w=d_n2w, cb=d_cb, fnw=d_fnw, w2=d_w2, cw=d_cw)
    return loss, grad_x, half123 + half45, oth123 + oth45, small, ts0


def _win_pieces():
    runs = [(P_GV, 1024, 2048), (P_MA, 5392, 2048), (P_GQ, 0, 1024), (P_AQ, 3088, 2304), (P_LR, 3072, GLA_LR)]
    out = []
    for kc, rc, ln in runs:
        while ln > 0:
            step = min(ln, W_IN_SH - rc % W_IN_SH)
            out.append((kc, rc, step))
            kc, rc, ln = kc + step, rc + step, ln - step
    return out


def _win_assemble(shards):
    rb = 256

    def body(s_ref, o_ref):
        o_ref[:, W_IN:] = jnp.zeros((rb, P_W - W_IN), o_ref.dtype)
        for kc, rc, ln in _win_pieces():
            o_ref[:, kc:kc + ln] = s_ref[rc // W_IN_SH, :, rc % W_IN_SH:rc % W_IN_SH + ln]

    return pl.pallas_call(
        body, name="w_in_assemble", grid=(D // rb,),
        in_specs=[pl.BlockSpec((4, rb, W_IN_SH), lambda i: (0, i, 0))], out_specs=pl.BlockSpec((rb, P_W), lambda i: (i, 0)),
        out_shape=jax.ShapeDtypeStruct((D, P_W), shards.dtype), compiler_params=_params(("parallel",)),
    )(shards)


def _win_split(g):
    rb = 256

    def body(g_ref, o_ref):
        for kc, rc, ln in _win_pieces():
            o_ref[rc // W_IN_SH, :, rc % W_IN_SH:rc % W_IN_SH + ln] = g_ref[:, kc:kc + ln]

    return pl.pallas_call(
        body, name="w_in_grad_split", grid=(D // rb,),
        in_specs=[pl.BlockSpec((rb, P_W), lambda i: (i, 0))], out_specs=pl.BlockSpec((4, rb, W_IN_SH), lambda i: (0, i, 0)),
        out_shape=jax.ShapeDtypeStruct((4, D, W_IN_SH), g.dtype), compiler_params=_params(("parallel",)),
    )(g)


def _ff_to_kernel(a):
    h = W_UP_SH
    return jnp.concatenate([a[:, 0:h], a[:, D_FF:D_FF + h], a[:, h:D_FF], a[:, D_FF + h:]], axis=1)


def _ff_from_kernel(a):
    h = W_UP_SH
    return jnp.concatenate([a[:, 0:h], a[:, 2 * h:3 * h], a[:, h:2 * h], a[:, 3 * h:]], axis=1)


BIG = ("w_in", "w_gla_branch", "w_attn_branch", "w_out", "w_up", "w_down")
SH_SHAPES = ((1024, W_IN_SH), (256, 1024), (256, 256), (256, 1024), (1024, W_UP_SH), (W_DOWN_SH, 1024))
N_BIG = len(BIG)


def _cols_join(t):
    return jnp.concatenate([t[k] for k in range(4)], axis=1)


def _cols_split(t):
    cols = t.shape[1] // 4
    return jnp.stack([t[:, k * cols:(k + 1) * cols] for k in range(4)])


def _me():
    return lax.axis_index("x"), lax.axis_index("y"), lax.axis_index("c")


HBM = pl.BlockSpec(memory_space=pltpu.HBM)
VMEM_SPEC = pl.BlockSpec(memory_space=pltpu.VMEM)


def _allgather8(xs, name):
    rows = xs.shape[0]

    def body(x_ref, out_ref, send_sems, recv_sems, local_sem):
        x, y, c = _me()
        me = 4 * x + 2 * y + c
        mine = pltpu.make_async_copy(x_ref, out_ref.at[me], local_sem)
        mine.start()
        flips = [(k >> 2 & 1, k >> 1 & 1, k & 1) for k in range(1, 8)]

        def peer(f):
            return (jnp.where(f[0] == 1, 1 - x, x), jnp.where(f[1] == 1, 1 - y, y), jnp.where(f[2] == 1, 1 - c, c))

        sends = []
        for k, f in enumerate(flips):
            cp = pltpu.make_async_remote_copy(src_ref=x_ref, dst_ref=out_ref.at[me], send_sem=send_sems.at[k],
                                              recv_sem=recv_sems.at[k], device_id=peer(f), device_id_type=MESH)
            cp.start()
            sends.append(cp)
        for k, f in enumerate(flips):
            px, py, pc = peer(f)
            pltpu.make_async_remote_copy(src_ref=x_ref, dst_ref=out_ref.at[4 * px + 2 * py + pc], send_sem=send_sems.at[k],
                                         recv_sem=recv_sems.at[k], device_id=peer(f), device_id_type=MESH).wait_recv()
        for cp in sends:
            cp.wait_send()
        mine.wait()

    return pl.pallas_call(
        body, name=name, out_shape=jax.ShapeDtypeStruct((8, rows, 128), F32),
        in_specs=[VMEM_SPEC], out_specs=VMEM_SPEC,
        scratch_shapes=[pltpu.SemaphoreType.DMA((7,)), pltpu.SemaphoreType.DMA((7,)), pltpu.SemaphoreType.DMA],
        compiler_params=pltpu.CompilerParams(vmem_limit_bytes=VMEM_LIMIT),
    )(xs)


def _half_rows(i, cc, unit):
    rows = SH_SHAPES[i][0] // 2
    return pl.ds(pl.multiple_of(cc * rows, unit), rows)


def _rc(src, dst, sems, to):
    return pltpu.make_async_remote_copy(src_ref=src, dst_ref=dst, send_sem=sems[0], recv_sem=sems[1], device_id=to, device_id_type=MESH)


def _other_chips(x, y):
    return [(1 - x, y), (x, 1 - y), (1 - x, 1 - y)]


def _u_gather_ici(w_sh, idxs):
    def copies(ins, outs, sem):
        x, y, c = _me()
        res = []
        for j, (px, py) in enumerate(_other_chips(x, y)):
            for n, i in enumerate(idxs):
                src = ins[n].at[0, _half_rows(i, c, 16)]
                res.append((_rc(src, outs[n].at[2 * x + y, _half_rows(i, c, 16)], sem(j * len(idxs) + n), (px, py, c)),
                            _rc(src, outs[n].at[2 * px + py, _half_rows(i, c, 16)], sem(j * len(idxs) + n), (px, py, c))))
        return res

    return dict(ins=[w_sh[i] for i in idxs], outs=[jax.ShapeDtypeStruct((4,) + SH_SHAPES[i], BF16) for i in idxs],
                nsem=3 * len(idxs), alias={}, copies=copies)


def _u_gather_d2d(got, idxs):
    def copies(ins, outs, sem):
        x, y, c = _me()
        res = []
        for j, (px, py) in enumerate(_other_chips(x, y)):
            for n, i in enumerate(idxs):
                src = ins[n].at[2 * px + py, _half_rows(i, c, 16)]
                res.append((_rc(src, outs[n].at[2 * px + py, _half_rows(i, c, 16)], sem(j * len(idxs) + n), (x, y, 1 - c)),
                            _rc(src, outs[n].at[2 * px + py, _half_rows(i, 1 - c, 16)], sem(j * len(idxs) + n), (x, y, 1 - c))))
        return res

    return dict(ins=list(got), outs=[jax.ShapeDtypeStruct(g.shape, g.dtype) for g in got], nsem=3 * len(idxs),
                alias={n: n for n in range(len(idxs))}, copies=copies)


def _u_pair_send(gs, idxs):
    def copies(ins, outs, sem):
        x, y, c = _me()
        res = []
        for n, i in enumerate(idxs):
            for sh in range(4):
                cp = _rc(ins[n].at[sh, _half_rows(i, 1 - c, 8)], outs[n].at[sh], sem(4 * n + sh), (x, y, 1 - c))
                res.append((cp, cp))
        return res

    return dict(ins=list(gs), outs=[jax.ShapeDtypeStruct((4, SH_SHAPES[i][0] // 2, SH_SHAPES[i][1]), F32) for i in idxs],
                nsem=4 * len(idxs), alias={}, copies=copies)


def _u_chip_exchange(ts):
    def copies(ins, outs, sem):
        x, y, c = _me()
        res = []
        for j, (px, py) in enumerate(_other_chips(x, y)):
            for n in range(len(ts)):
                cp = _rc(ins[n].at[2 * px + py], outs[n].at[j], sem(j * len(ts) + n), (px, py, c))
                res.append((cp, cp))
        return res

    return dict(ins=list(ts), outs=[jax.ShapeDtypeStruct((3,) + t.shape[1:], t.dtype) for t in ts], nsem=3 * len(ts),
                alias={}, copies=copies)


def _u_pair_join(hs):
    def copies(ins, outs, sem):
        x, y, c = _me()
        res = []
        for n in range(len(hs)):
            cp = _rc(ins[n], outs[n], sem(n), (x, y, 1 - c))
            res.append((cp, cp))
        return res

    return dict(ins=list(hs), outs=[jax.ShapeDtypeStruct(h.shape, h.dtype) for h in hs], nsem=len(hs), alias={}, copies=copies)


def _comm_phase(units, ci, co, send_sems, recv_sems, start):
    ii = oo = off = 0
    for u in units:
        ni, no = len(u["ins"]), len(u["outs"])
        for st, arrival in u["copies"](ci[ii:ii + ni], co[oo:oo + no], lambda k, off=off: (send_sems.at[off + k], recv_sems.at[off + k])):
            if start:
                st.start()
            else:
                st.wait_send()
                arrival.wait_recv()
        ii, oo, off = ii + ni, oo + no, off + u["nsem"]


def _carry(units, n_in, n_out):
    ins = [a for u in units for a in u["ins"]]
    outs = [o for u in units for o in u["outs"]]
    alias, ii, oo = {}, 0, 0
    for u in units:
        for a, b in u["alias"].items():
            alias[n_in + ii + a] = n_out + oo + b
        ii, oo = ii + len(u["ins"]), oo + len(u["outs"])
    nsem = sum(u["nsem"] for u in units)
    scratch = [pltpu.SemaphoreType.DMA((nsem,)), pltpu.SemaphoreType.DMA((nsem,))] if units else []
    return ins, outs, alias, scratch


def _split_units(units, res):
    out, oo = [], 0
    for u in units:
        out.append(list(res[oo:oo + len(u["outs"])]))
        oo += len(u["outs"])
    return out


def _comm_call(name, units):
    ins, outs, alias, scratch = _carry(units, 0, 0)

    def body(*refs):
        ci, co = refs[:len(ins)], refs[len(ins):len(ins) + len(outs)]
        _comm_phase(units, ci, co, refs[-2], refs[-1], True)
        _comm_phase(units, ci, co, refs[-2], refs[-1], False)

    res = pl.pallas_call(body, name=name, out_shape=outs, in_specs=[HBM] * len(ins), out_specs=[HBM] * len(outs),
                         scratch_shapes=scratch, input_output_aliases=alias)(*ins)
    return _split_units(units, res)


SEM = pl.BlockSpec(memory_space=pltpu.SEMAPHORE)
EFFECT = pltpu.SideEffectType.DATAFLOW_SIDE_EFFECTING


def _unit_start(unit, name, after=()):
    bufs = list(unit["ins"]) + [lax.empty(o.shape, o.dtype) for o in unit["outs"]]
    n_i, n_b, ns = len(unit["ins"]), len(bufs), unit["nsem"]

    def body(*refs):
        send_sems, recv_sems = refs[n_b + len(after)], refs[n_b + len(after) + 1]
        for st, _ in unit["copies"](refs[:n_i], refs[n_i:n_b], lambda k: (send_sems.at[k], recv_sems.at[k])):
            st.start()
        refs[-1][...] = jnp.zeros_like(refs[-1])

    res = pl.pallas_call(
        body, name=name,
        out_shape=[pltpu.SemaphoreType.DMA((ns,)), pltpu.SemaphoreType.DMA((ns,))] + [pltpu.HBM(b.shape, b.dtype) for b in bufs]
        + [jax.ShapeDtypeStruct((8, 128), F32)],
        in_specs=[HBM] * n_b + [pl.BlockSpec(memory_space=pl.ANY)] * len(after), out_specs=[SEM, SEM] + [HBM] * n_b + [VMEM_SPEC],
        input_output_aliases={i: 2 + i for i in range(n_b)},
        compiler_params=pltpu.CompilerParams(has_side_effects=EFFECT),
    )(*[pltpu.with_memory_space_constraint(b, pltpu.HBM) for b in bufs], *after)
    return res[0], res[1], list(res[2:2 + n_b]), res[-1]


def _unit_wait(unit, send_sems, recv_sems, bufs, after, name):
    n_i, n_b = len(unit["ins"]), len(bufs)

    def body(*refs):
        ss, rs = refs[n_b], refs[n_b + 1]
        for st, arrival in unit["copies"](refs[:n_i], refs[n_i:n_b], lambda k: (ss.at[k], rs.at[k])):
            st.wait_send()
            arrival.wait_recv()

    res = pl.pallas_call(
        body, name=name, out_shape=[pltpu.HBM(b.shape, b.dtype) for b in bufs],
        in_specs=[HBM] * n_b + [SEM, SEM] + [pl.BlockSpec(memory_space=pl.ANY)] * len(after), out_specs=[HBM] * n_b,
        input_output_aliases={i: i for i in range(n_b)}, compiler_params=pltpu.CompilerParams(has_side_effects=EFFECT),
    )(*bufs, send_sems, recv_sems, *after)
    return list(res[:n_i]), list(res[n_i:])


def _pair_add(g, land, core, name):
    _, rows, cols = g.shape
    half = rows // 2
    rb = _tile(half, 256, 16)
    nb = half // rb

    def body(c_ref, g_ref, l_ref, o_ref):
        o_ref[...] = (g_ref[...] + l_ref[...]).astype(BF16)

    return pl.pallas_call(
        body, name=name,
        grid_spec=pltpu.PrefetchScalarGridSpec(
            num_scalar_prefetch=1, grid=(4, nb),
            in_specs=[pl.BlockSpec((1, rb, cols), lambda s, i, c_ref: (s, c_ref[0] * nb + i, 0)),
                      pl.BlockSpec((1, rb, cols), lambda s, i, c_ref: (s, i, 0))],
            out_specs=pl.BlockSpec((1, rb, cols), lambda s, i, c_ref: (s, i, 0))),
        out_shape=jax.ShapeDtypeStruct((4, half, cols), BF16),
        compiler_params=_params(("parallel", "parallel")),
    )(core, g, land)


def _chip_sum(t, r, chip, name):
    _, half, cols = t.shape
    rb = _tile(half, 256, 16)

    def body(s_ref, t_ref, r_ref, o_ref):
        o_ref[...] = ((t_ref[0].astype(F32) + r_ref[0].astype(F32)) + r_ref[1].astype(F32)) + r_ref[2].astype(F32)

    return pl.pallas_call(
        body, name=name,
        grid_spec=pltpu.PrefetchScalarGridSpec(
            num_scalar_prefetch=1, grid=(half // rb,),
            in_specs=[pl.BlockSpec((1, rb, cols), lambda i, s_ref: (s_ref[0], i, 0)),
                      pl.BlockSpec((3, rb, cols), lambda i, s_ref: (0, i, 0))],
            out_specs=pl.BlockSpec((rb, cols), lambda i, s_ref: (i, 0))),
        out_shape=jax.ShapeDtypeStruct((half, cols), F32),
        compiler_params=_params(("parallel",)),
    )(chip, t, r)


def _adam_math(wv, gv, mv, vv):
    mn = ADAM_B1 * mv + (1.0 - ADAM_B1) * gv
    vn = ADAM_B2 * vv + (1.0 - ADAM_B2) * (gv * gv)
    m_hat = mn / (1.0 - ADAM_B1 ** ADAM_STEP)
    v_hat = vn / (1.0 - ADAM_B2 ** ADAM_STEP)
    return -ADAM_LR * (m_hat / (jnp.sqrt(v_hat) + ADAM_EPS) + ADAM_WD * wv), mn, vn


def _adamw_halves(wt, mt, vt, mine, theirs, core, name):
    _, rows, cols = wt.shape
    half = rows // 2
    rb = _tile(half, 256, 8)
    nb = half // rb

    def body(c_ref, w_ref, m_ref, v_ref, a_ref, b_ref, g_ref, d_ref, mo_ref, vo_ref):
        gv = jnp.where(pl.program_id(0) == c_ref[0], a_ref[...], b_ref[...])
        dl, mn, vn = _adam_math(w_ref[...], gv, m_ref[...], v_ref[...])
        g_ref[...] = gv
        d_ref[...] = dl
        mo_ref[...] = mn
        vo_ref[...] = vn

    full = pl.BlockSpec((None, rb, cols), lambda hf, i, c_ref: (0, hf * nb + i, 0))
    part = pl.BlockSpec((rb, cols), lambda hf, i, c_ref: (i, 0))
    return pl.pallas_call(
        body, name=name,
        grid_spec=pltpu.PrefetchScalarGridSpec(num_scalar_prefetch=1, grid=(2, nb), in_specs=[full, full, full, part, part],
                                               out_specs=[full] * 4),
        out_shape=[jax.ShapeDtypeStruct((1, rows, cols), F32)] * 4,
        compiler_params=_params(("parallel", "parallel")),
    )(core, wt, mt, vt, mine, theirs)


SG_REP = 144
SG_LOSS = 136
SG_W2, SG_CW = SG_REP, SG_REP + 4 * 16
SG_ROWS = SG_CW + 4 * 40
SP_ROWS = SG_REP + 16 + 40


def _mod_shard(c_all, ada_w_sh):
    def body(c_ref, w_ref, o_ref):
        cv = c_ref[...]
        o_ref[...] = _dg((cv * _sigmoid(cv)).astype(BF16), w_ref[...].astype(BF16), 1, 0)

    return pl.pallas_call(body, name="mod_shard", out_shape=jax.ShapeDtypeStruct((8, 1536), F32),
                          in_specs=[VMEM_SPEC, VMEM_SPEC], out_specs=VMEM_SPEC,
                          compiler_params=pltpu.CompilerParams(vmem_limit_bytes=VMEM_LIMIT))(c_all, ada_w_sh)


def _mod_select(mod_all, ada_b4):
    def body(m_ref, b_ref, o_ref):
        x, y, c = _me()
        me = 4 * x + 2 * y + c
        for sh in range(4):
            o_ref[sh] = m_ref[2 * sh, me] + b_ref[sh]

    return pl.pallas_call(body, name="mod_select", out_shape=jax.ShapeDtypeStruct((4, 12, 128), F32),
                          in_specs=[VMEM_SPEC, VMEM_SPEC], out_specs=VMEM_SPEC)(mod_all, ada_b4)


def _small_reduce(sg_all):
    def body(g_ref, o_ref):
        x, y, c = _me()
        s_me = 2 * x + y
        w2_rows = pl.ds(pl.multiple_of(SG_W2 + 16 * s_me, 8), 16)
        cw_rows = pl.ds(pl.multiple_of(SG_CW + 40 * s_me, 8), 40)
        a = g_ref[0, 0:SG_REP, :]
        b = g_ref[0, w2_rows, :]
        d = g_ref[0, cw_rows, :]
        for dev in range(1, 8):
            a = a + g_ref[dev, 0:SG_REP, :]
            b = b + g_ref[dev, w2_rows, :]
            d = d + g_ref[dev, cw_rows, :]
        o_ref[0:SG_REP, :] = a
        o_ref[SG_REP:SG_REP + 16, :] = b
        o_ref[SG_REP + 16:SP_ROWS, :] = d

    return pl.pallas_call(body, name="small_grad_reduce", out_shape=jax.ShapeDtypeStruct((SP_ROWS, 128), F32),
                          in_specs=[VMEM_SPEC], out_specs=VMEM_SPEC)(sg_all)


def _ada_grad(dmod_all, c_bc):
    def body(g_ref, c_ref, o_ref):
        x, y, c = _me()
        s_me = 2 * x + y
        for k in range(12):
            acc = jnp.zeros((D, 128), F32)
            for b in range(8):
                cv = c_ref[b]
                acc = acc + (cv * _sigmoid(cv)) * g_ref[s_me, k, b:b + 1, :]
            o_ref[:, k * 128:(k + 1) * 128] = acc

    return pl.pallas_call(body, name="ada_w_grad", out_shape=jax.ShapeDtypeStruct((D, 1536), F32),
                          in_specs=[VMEM_SPEC, VMEM_SPEC], out_specs=VMEM_SPEC,
                          compiler_params=pltpu.CompilerParams(vmem_limit_bytes=VMEM_LIMIT))(dmod_all, c_bc)


def _adamw(wt, g, m, v, name):
    rows, cols = wt.shape
    rb = _tile(rows, 256, 8)

    def fn(c, i, wv, gv, mv, vv):
        return _adam_math(wv, gv, mv, vv)

    return _rowcall(fn, [_rows(t, rb) for t in (wt, g, m, v)], [_orow(rows, cols, F32, rb)] * 3,
                    n_rows=rows, rb=rb, name=name)


def _pad_rows(t, rows):
    flat = t.reshape(-1)
    return jnp.pad(flat, (0, rows * 128 - flat.shape[0])).reshape(rows, 128)


SP_LAYOUT = (("ada_b", 48), ("norm1_w", 8), ("gla_gate_b", 8), ("gla_norm_w", 8), ("norm2_w", 8), ("conv_b", 48),
             ("final_norm_w", 8), (None, 8), ("gla_gate_w2", 16), ("conv_w", 40))


def _pack_small(d):
    return jnp.concatenate([jnp.zeros((rows, 128), F32) if n is None else _pad_rows(d[n].astype(F32), rows)
                            for n, rows in SP_LAYOUT], axis=0)


def _unpack_small(pk, shapes):
    out, off = {}, 0
    for n, rows in SP_LAYOUT:
        if n is not None:
            shp = shapes[n]
            out[n] = pk[off:off + rows].reshape(-1)[:math.prod(shp)].reshape(shp)
        off += rows
    return out


def kernel(x, c, positions, ada_w, ada_b, norm1_w, w_in, gla_gate_w2, gla_gate_b, gla_norm_w, w_gla_branch, w_attn_branch, w_out, norm2_w, w_up, conv_w, conv_b, w_down, final_norm_w, loss_target, m_ada_w, m_ada_b, m_norm1_w, m_w_in, m_gla_gate_w2, m_gla_gate_b, m_gla_norm_w, m_w_gla_branch, m_w_attn_branch, m_w_out, m_norm2_w, m_w_up, m_conv_w, m_conv_b, m_w_down, m_final_norm_w, v_ada_w, v_ada_b, v_norm1_w, v_w_in, v_gla_gate_w2, v_gla_gate_b, v_gla_norm_w, v_w_gla_branch, v_w_attn_branch, v_w_out, v_norm2_w, v_w_up, v_conv_w, v_conv_b, v_w_down, v_final_norm_w):
    s = x.shape[1]
    names = ("ada_w", "ada_b", "norm1_w", "w_in", "gla_gate_w2", "gla_gate_b", "gla_norm_w", "w_gla_branch", "w_attn_branch",
             "w_out", "norm2_w", "w_up", "conv_w", "conv_b", "w_down", "final_norm_w")
    wts = dict(zip(names, (ada_w, ada_b, norm1_w, w_in, gla_gate_w2, gla_gate_b, gla_norm_w, w_gla_branch, w_attn_branch,
                           w_out, norm2_w, w_up, conv_w, conv_b, w_down, final_norm_w)))
    ms = dict(zip(names, (m_ada_w, m_ada_b, m_norm1_w, m_w_in, m_gla_gate_w2, m_gla_gate_b, m_gla_norm_w, m_w_gla_branch,
                          m_w_attn_branch, m_w_out, m_norm2_w, m_w_up, m_conv_w, m_conv_b, m_w_down, m_final_norm_w)))
    vs = dict(zip(names, (v_ada_w, v_ada_b, v_norm1_w, v_w_in, v_gla_gate_w2, v_gla_gate_b, v_gla_norm_w, v_w_gla_branch,
                          v_w_attn_branch, v_w_out, v_norm2_w, v_w_up, v_conv_w, v_conv_b, v_w_down, v_final_norm_w)))

    pk0 = jnp.concatenate([_pad_rows(c, 8), _pad_rows(gla_gate_w2, 16), _pad_rows(conv_w, 40)], axis=0)
    sm_all = _allgather8(pk0, "gather_small")
    c_all = sm_all[:, 0:8, :].reshape(8, D)
    w2_full = sm_all[0::2, 8:24, :].transpose(1, 0, 2).reshape(GLA_LR, 512)
    cw_full = sm_all[0::2, 24:64, :].reshape(4, 40 * 128)[:, :3 * W_UP_SH].reshape(4, 3, W_UP_SH).transpose(1, 0, 2).reshape(3, 2 * D_FF)

    mod_sh = _mod_shard(c_all, ada_w[0])
    mod_all = _allgather8(mod_sh.reshape(96, 128), "gather_mod")

    w_sh = [wts[n].astype(BF16) for n in BIG]
    u_g0 = _u_gather_ici(w_sh, (0,))
    g0 = (u_g0,) + _unit_start(u_g0, "gather_w_in_start", after=[mod_all])
    mod = _mod_select(mod_all.reshape(8, 8, 12, 128) + g0[4][0, 0], ada_b.reshape(4, 12, 128)).reshape(6, D)

    core = lax.axis_index("c").astype(jnp.int32).reshape(1)
    chip = (2 * lax.axis_index("x") + lax.axis_index("y")).astype(jnp.int32)
    sm = dict(n1w=norm1_w, n2w=norm2_w, fnw=final_norm_w.reshape(1, D), gnw=gla_norm_w, gb=gla_gate_b,
              w2=jnp.pad(w2_full, ((0, 128 - GLA_LR), (0, 0))), cw=_ff_to_kernel(cw_full), cb=_ff_to_kernel(conv_b))
    loss, grad_x, halves, others, small, ts0 = _local_step(x[0], mod, positions.reshape(s, 1), loss_target[0], sm, w_sh,
                                                               g0, chip, core)

    dcw = _ff_from_kernel(small["cw"]).reshape(3, 4, W_UP_SH).transpose(1, 0, 2)
    dw2 = small["w2"][:GLA_LR].reshape(GLA_LR, 4, 128).transpose(1, 0, 2)
    sg = jnp.concatenate(
        [_pad_rows(small["dmod"], 48), _pad_rows(small["n1w"], 8), _pad_rows(small["gb"], 8), _pad_rows(small["gnw"], 8),
         _pad_rows(small["n2w"], 8), _pad_rows(_ff_from_kernel(small["cb"]), 48), _pad_rows(small["fnw"], 8), _pad_rows(loss, 8)]
        + [_pad_rows(dw2[k], 16) for k in range(4)] + [_pad_rows(dcw[k], 40) for k in range(4)], axis=0)
    sg_all = _allgather8(sg, "gather_small_grads")
    u_ex = _u_chip_exchange([ts0])
    pending = (u_ex,) + _unit_start(u_ex, "grad_exchange_w_in_start", after=[sg_all])
    sg_all = sg_all + pending[4][0, 0]
    g_small_pk = _small_reduce(sg_all)
    dmod_all = sg_all[:, 0:48, :].reshape(8, 4, 12, 128).transpose(1, 2, 0, 3)
    g_ada_w = _ada_grad(dmod_all, jnp.broadcast_to(c_all[:, :, None], (8, D, 128)))

    shapes = {n: wts[n].shape for n in names}
    g_small = _unpack_small(g_small_pk, shapes)
    grads = {"ada_w": g_ada_w.reshape(1, D, 1536), **g_small}
    deltas, new_m, new_v = {}, {}, {}
    for n, mine, theirs in zip(BIG[1:], halves, others):
        grads[n], deltas[n], new_m[n], new_v[n] = _adamw_halves(wts[n], ms[n], vs[n], mine, theirs, core, "adamw_" + n)
    shp = ada_w.shape
    d_, m_, v_ = _adamw(ada_w[0], g_ada_w, m_ada_w[0], v_ada_w[0], "adamw_ada_w")
    deltas["ada_w"], new_m["ada_w"], new_v["ada_w"] = d_.reshape(shp), m_.reshape(shp), v_.reshape(shp)
    d_, m_, v_ = _adamw(_pack_small(wts), g_small_pk, _pack_small(ms), _pack_small(vs), "adamw_small")
    for dst, pk in ((deltas, d_), (new_m, m_), (new_v, v_)):
        dst.update(_unpack_small(pk, shapes))

    [t0], [r0] = _unit_wait(*pending[:4], after=[d_, deltas["ada_w"], deltas["w_up"], deltas["w_down"]], name="grad_exchange_w_in_wait")
    half0 = _chip_sum(t0, r0, chip.reshape(1), "grad_chip_sum_w_in")
    [[oth0]] = _comm_call("grad_join_w_in", [_u_pair_join([half0])])
    grads["w_in"], deltas["w_in"], new_m["w_in"], new_v["w_in"] = _adamw_halves(w_in, m_w_in, v_w_in, half0, oth0, core, "adamw_w_in")

    return (g_small_pk[SG_LOSS, 0], grad_x.reshape(1, s, D), *[grads[n] for n in names], *[deltas[n] for n in names],
            *[new_m[n] for n in names], *[new_v[n] for n in names])
```

```python
import math

import jax
import jax.numpy as jnp
from jax import lax
from jax.experimental import pallas as pl
from jax.experimental.pallas import tpu as pltpu

F32, BF16 = jnp.float32, jnp.bfloat16
MESH = pl.DeviceIdType.MESH

D = 1024
EPS = 1e-6
GLA_H, GLA_DK, GLA_DV, GLA_LR = 4, 128, 256, 16
GLA_TAU = 16.0
GLA_CHUNK = 64
GLA_BLOCK = 512
ATT_GROUPS = ((128, 1), (512, 4), (2048, 16))
ATT_BLK = 128
ATT_HD = 64
ATT_W = 768
D_FF = 2816
ROPE_THETA = 10000.0
P_W = 7680
P_GV, P_GR, P_MA, P_MB, P_GQ, P_GK, P_AQ, P_AK, P_AV, P_LR = 0, 1024, 2048, 3072, 4096, 4608, 5120, 5888, 6656, 7424
W_IN = 7440
W_IN_SH, W_UP_SH, W_DOWN_SH = 1860, 1408, 704
VMEM_LIMIT = 56 * 1024 * 1024
ADAM_LR, ADAM_B1, ADAM_B2, ADAM_EPS, ADAM_WD, ADAM_STEP = 0.001, 0.9, 0.999, 1e-08, 0.01, 10
NEG = -1e30


def _tile(n, target, unit=128):
    best = None
    for t in range(unit, min(n, target) + 1, unit):
        if n % t == 0:
            best = t
    return best or n


def _params(sem):
    return pltpu.CompilerParams(dimension_semantics=sem, vmem_limit_bytes=VMEM_LIMIT)


def _dg(a, b, ca, cb):
    return lax.dot_general(a, b, (((ca,), (cb,)), ((), ())), preferred_element_type=F32)


def _sigmoid(v):
    return 1.0 / (1.0 + jnp.exp(-v))


def _ff_block(j):
    return (j % 2) * 2 + j // 2


def _mm(a, b, name, *, ta=False, tb=False, out_dtype=BF16, tm=1024, tn=1536, tk=1024, n_outer=True, comm=(),
        b_shards=False, o_shards=False):
    m = a.shape[1] if ta else a.shape[0]
    k = a.shape[0] if ta else a.shape[1]
    if b_shards:
        n = b.shape[1] if tb else 4 * W_UP_SH
        tn, tk = (tn, W_UP_SH) if tb else (W_UP_SH, tk)
    else:
        n = b.shape[0] if tb else b.shape[1]
    if o_shards:
        tn = W_UP_SH
    tm, tn, tk = _tile(m, tm), _tile(n, tn), _tile(k, tk)
    nm, nn, nk = m // tm, n // tn, k // tk
    in_out = out_dtype == F32
    c_ins, c_outs, c_alias, c_scratch = _carry(comm, 2, 1)

    def body(a_ref, b_ref, *rest):
        ci, o_ref, co = rest[:len(c_ins)], rest[len(c_ins)], rest[len(c_ins) + 1:len(c_ins) + 1 + len(c_outs)]
        scr = rest[len(c_ins) + 1 + len(c_outs):]
        kk = pl.program_id(2)
        if comm:
            step = (pl.program_id(0) * (nm if n_outer else nn) + pl.program_id(1)) * nk + kk

            @pl.when(step == 0)
            def _():
                _comm_phase(comm, ci, co, scr[-2], scr[-1], True)

        _mm_step(a_ref, b_ref, o_ref, scr, kk)
        if comm:
            @pl.when(step == nm * nn * nk - 1)
            def _():
                _comm_phase(comm, ci, co, scr[-2], scr[-1], False)

    def _mm_step(a_ref, b_ref, o_ref, scr, kk):
        p = _dg(a_ref[...].astype(BF16), b_ref[...].astype(BF16), 0 if ta else 1, 1 if tb else 0)
        if nk == 1:
            o_ref[...] = p.astype(o_ref.dtype)
        else:
            acc = o_ref if in_out else scr[0]

            @pl.when(kk == 0)
            def _():
                acc[...] = p

            @pl.when(kk > 0)
            def _():
                acc[...] += p

            if not in_out:
                @pl.when(kk == nk - 1)
                def _():
                    o_ref[...] = acc[...].astype(o_ref.dtype)

    if n_outer:
        ij = lambda g0, g1: (g1, g0)
        grid = (nn, nm, nk)
    else:
        ij = lambda g0, g1: (g0, g1)
        grid = (nm, nn, nk)
    a_map = (lambda g0, g1, kk: (kk, ij(g0, g1)[0])) if ta else (lambda g0, g1, kk: (ij(g0, g1)[0], kk))
    if b_shards and tb:
        b_spec = pl.BlockSpec((None, tn, tk), lambda g0, g1, kk: (_ff_block(kk), ij(g0, g1)[1], 0))
    elif b_shards:
        b_spec = pl.BlockSpec((None, tk, tn), lambda g0, g1, kk: (_ff_block(ij(g0, g1)[1]), kk, 0))
    elif tb:
        b_spec = pl.BlockSpec((tn, tk), lambda g0, g1, kk: (ij(g0, g1)[1], kk))
    else:
        b_spec = pl.BlockSpec((tk, tn), lambda g0, g1, kk: (kk, ij(g0, g1)[1]))
    if o_shards:
        o_spec = pl.BlockSpec((None, tm, tn), lambda g0, g1, kk: (_ff_block(ij(g0, g1)[1]), ij(g0, g1)[0], 0))
        o_shape = jax.ShapeDtypeStruct((4, m, W_UP_SH), out_dtype)
    else:
        o_spec = pl.BlockSpec((tm, tn), lambda g0, g1, kk: ij(g0, g1))
        o_shape = jax.ShapeDtypeStruct((m, n), out_dtype)
    res = pl.pallas_call(
        body, name=name, grid=grid,
        in_specs=[pl.BlockSpec((tk, tm) if ta else (tm, tk), a_map), b_spec] + [HBM] * len(c_ins),
        out_specs=[o_spec] + [HBM] * len(c_outs),
        out_shape=[o_shape] + c_outs,
        scratch_shapes=([] if (in_out or nk == 1) else [pltpu.VMEM((tm, tn), F32)]) + c_scratch,
        input_output_aliases=c_alias,
        compiler_params=_params(("arbitrary",) * 3 if comm else ("parallel", "parallel", "arbitrary")),
    )(a, b, *c_ins)
    return (res[0], _split_units(comm, res[1:])) if comm else res[0]


def _rows(arr, rb, w=None, j=0):
    w = arr.shape[1] if w is None else w
    if callable(j):
        return arr, pl.BlockSpec((rb, w), lambda c, i: (i, j(c)))
    return arr, pl.BlockSpec((rb, w), lambda c, i: (i, j))


def _full(arr, w=None, j=0):
    w = arr.shape[1] if w is None else w
    if callable(j):
        return arr, pl.BlockSpec((arr.shape[0], w), lambda c, i: (0, j(c)))
    return arr, pl.BlockSpec((arr.shape[0], w), lambda c, i: (0, j))


def _halo(arr, rb, hb, w, j, before):
    per = rb // hb
    last = arr.shape[0] // hb - 1
    if before:
        rmap = lambda i: jnp.maximum(i * per - 1, 0)
    else:
        rmap = lambda i: jnp.minimum((i + 1) * per, last)
    return arr, pl.BlockSpec((hb, w), lambda c, i: (rmap(i), j(c) if callable(j) else j))


def _rowcall(fn, ins, outs, *, n_rows, rb, name, ncol=1, into=None, after=()):
    n_in = len(ins)
    nr = n_rows // rb
    unread = ([] if into is None else [into[0]]) + list(after)
    n_skip = len(unread)

    def body(*refs):
        c, i = pl.program_id(0), pl.program_id(1)
        res = fn(c, i, *[r[...] for r in refs[:n_in]])
        for val, spec, o_ref in zip(res, outs, refs[n_in + n_skip:]):
            if spec[2] == "row":
                o_ref[...] = val.astype(o_ref.dtype)
            else:
                @pl.when(i == 0)
                def _(o_ref=o_ref, val=val):
                    o_ref[...] = val.astype(o_ref.dtype)

                @pl.when(i > 0)
                def _(o_ref=o_ref, val=val):
                    o_ref[...] += val.astype(o_ref.dtype)

    out_specs = []
    for shape, dt, kind, block, col in outs:
        if kind == "row":
            out_specs.append(pl.BlockSpec(block, lambda c, i, col=col: (i, col(c))))
        else:
            out_specs.append(pl.BlockSpec(block, lambda c, i, col=col: (0, col(c))))
    return pl.pallas_call(
        body, name=name, grid=(ncol, nr),
        in_specs=[s for _, s in ins] + [pl.BlockSpec(memory_space=pl.ANY)] * n_skip, out_specs=out_specs,
        out_shape=[jax.ShapeDtypeStruct(o[0], o[1]) for o in outs],
        input_output_aliases={} if into is None else {n_in: into[1]},
        compiler_params=_params(("parallel", "arbitrary")),
    )(*[a for a, _ in ins], *unread)


def _orow(n_rows, w, dt, rb, bw=None, col=lambda c: 0):
    return ((n_rows, w), dt, "row", (rb, bw or w), col)


def _oacc(r, w, bw=None, col=lambda c: 0):
    return ((r, w), F32, "acc", (r, bw or w), col)


def _csum(v):
    return jnp.sum(v, axis=0, keepdims=True)


def _rms(v):
    return lax.rsqrt(jnp.mean(v * v, axis=-1, keepdims=True) + EPS)


def _norm_bwd(xv, dh, w, scale):
    r = _rms(xv)
    xh = xv * r
    dxh = dh * (w * (1.0 + scale))
    dx = r * (dxh - xh * jnp.mean(dxh * xh, axis=-1, keepdims=True))
    t = dh * xh
    return dx, _csum(dh), _csum(t * w), _csum(t * (1.0 + scale))


def _rope_tables(pos_col, invf, s):
    def fn(c, i, pos, f):
        ang = pos.astype(F32) * f
        lane = lax.broadcasted_iota(jnp.int32, ang.shape, 1)
        sign = jnp.where((lane % ATT_HD) < ATT_HD // 2, -1.0, 1.0)
        return jnp.cos(ang), jnp.sin(ang) * sign

    rb = 512
    return _rowcall(fn, [_rows(pos_col, rb), _full(invf)], [_orow(s, 128, F32, rb), _orow(s, 128, F32, rb)],
                    n_rows=s, rb=rb, name="rope_tables")


def _swap_halves(t):
    n = t.shape[1]
    lane = lax.broadcasted_iota(jnp.int32, t.shape, 1)
    return jnp.where((lane % ATT_HD) < ATT_HD // 2, pltpu.roll(t, n - 32, 1), pltpu.roll(t, 32, 1))


def _rope_apply(t, cos, sin_signed, inverse):
    cw = jnp.concatenate([cos] * (t.shape[1] // 128), axis=1)
    sw = jnp.concatenate([sin_signed] * (t.shape[1] // 128), axis=1)
    if inverse:
        sw = -sw
    return t * cw + _swap_halves(t) * sw


DIL_ROWS = 512


def _to_dilated(scr, val, out_ref, r):
    if r == 1:
        out_ref[...] = val.astype(out_ref.dtype)
        return
    n = val.shape[0] // r
    for hh in range(2):
        scr[hh] = val[:, hh * 128:(hh + 1) * 128]
        for pr in range(r):
            out_ref[:, pr * 256 + hh * 128:pr * 256 + (hh + 1) * 128] = scr[hh, pl.ds(pr, n, stride=r), :].astype(out_ref.dtype)


def _from_dilated(scr, in_ref, r):
    if r == 1:
        return in_ref[...].astype(F32)
    n = in_ref.shape[0]
    for hh in range(2):
        for pr in range(r):
            scr[hh, pl.ds(pr, n, stride=r), :] = in_ref[:, pr * 256 + hh * 128:pr * 256 + (hh + 1) * 128].astype(F32)
    return jnp.concatenate([scr[0], scr[1]], axis=1)


def _dil_spec(r):
    return pl.BlockSpec((DIL_ROWS // r, r * 256), lambda i: (i, 0))


def _dil_shape(s, r, dt):
    return jax.ShapeDtypeStruct((s // r, r * 256), dt)


_DIL_SCRATCH = [pltpu.VMEM((2, DIL_ROWS, 128), F32)]
_RS = tuple(r for _, r in ATT_GROUPS)


def _rope_fwd(p, cos_t, sin_t, s):
    def body(*refs):
        ins, cs, sn, outs, scr = refs[:9], refs[9][...], refs[10][...], refs[11:20], refs[20]
        for t in range(3):
            for g, r in enumerate(_RS):
                val = ins[3 * t + g][...].astype(F32)
                _to_dilated(scr, _rope_apply(val, cs, sn, False) if t < 2 else val, outs[3 * t + g], r)

    res = pl.pallas_call(
        body, name="rope", grid=(s // DIL_ROWS,),
        in_specs=[pl.BlockSpec((DIL_ROWS, 256), lambda i, c=base // 256 + g: (i, c)) for base in (P_AQ, P_AK, P_AV) for g in range(3)]
        + [pl.BlockSpec((DIL_ROWS, 128), lambda i: (i, 0))] * 2,
        out_specs=[_dil_spec(r) for _ in range(3) for r in _RS],
        out_shape=[_dil_shape(s, r, BF16) for _ in range(3) for r in _RS],
        scratch_shapes=_DIL_SCRATCH, compiler_params=_params(("parallel",)),
    )(*([p] * 9), cos_t, sin_t)
    return res[0:3], res[3:6], res[6:9]


def _attn_combine(att, s):
    def body(o0, o1, o2, l0, l1, l2, o_ref, lse_ref, od1, od2, ld1, ld2, scr):
        ov = [_from_dilated(scr, ref, r) for ref, r in zip((o0, o1, o2), _RS)]
        lv = [_from_dilated(scr, ref, r) for ref, r in zip((l0, l1, l2), _RS)]
        mx = jnp.maximum(jnp.maximum(lv[0], lv[1]), lv[2])
        ev = [jnp.exp(l - mx) for l in lv]
        z = ev[0] + ev[1] + ev[2]
        o = ((ev[0] * ov[0] + ev[1] * ov[1] + ev[2] * ov[2]) / z).astype(BF16)
        lse = mx + jnp.log(z)
        o_ref[...] = o
        lse_ref[...] = lse
        for ref, r in zip((od1, od2), _RS[1:]):
            _to_dilated(scr, o.astype(F32), ref, r)
        for ref, r in zip((ld1, ld2), _RS[1:]):
            _to_dilated(scr, lse, ref, r)

    return pl.pallas_call(
        body, name="attn_combine", grid=(s // DIL_ROWS,),
        in_specs=[_dil_spec(r) for r in _RS] * 2,
        out_specs=[_dil_spec(1)] * 2 + [_dil_spec(r) for r in _RS[1:]] * 2,
        out_shape=[_dil_shape(s, 1, BF16), _dil_shape(s, 1, F32)] + [_dil_shape(s, r, BF16) for r in _RS[1:]]
        + [_dil_shape(s, r, F32) for r in _RS[1:]],
        scratch_shapes=_DIL_SCRATCH, compiler_params=_params(("parallel",)),
    )(*[a[0] for a in att], *[a[1] for a in att])


def _dilate(t, s):
    def body(t_ref, o1, o2, scr):
        val = t_ref[...].astype(F32)
        for ref, r in zip((o1, o2), _RS[1:]):
            _to_dilated(scr, val, ref, r)

    return pl.pallas_call(
        body, name="attn_dilate", grid=(s // DIL_ROWS,), in_specs=[_dil_spec(1)], out_specs=[_dil_spec(r) for r in _RS[1:]],
        out_shape=[_dil_shape(s, r, t.dtype) for r in _RS[1:]], scratch_shapes=_DIL_SCRATCH, compiler_params=_params(("parallel",)),
    )(t)


def _rope_bwd(datt, d_glr, dp, cos_t, sin_t, s):
    tail = P_W - P_AQ

    def body(*refs):
        ins, cs, sn, glr, o_ref, scr = refs[:9], refs[9][...], refs[10][...], refs[11], refs[13], refs[14]
        for t in range(3):
            for g, r in enumerate(_RS):
                val = _from_dilated(scr, ins[3 * t + g], r)
                o_ref[:, t * ATT_W + g * 256:t * ATT_W + (g + 1) * 256] = (_rope_apply(val, cs, sn, True) if t < 2 else val).astype(BF16)
        o_ref[:, 3 * ATT_W:3 * ATT_W + 128] = glr[...]
        o_ref[:, 3 * ATT_W + 128:] = jnp.zeros((DIL_ROWS, tail - 3 * ATT_W - 128), BF16)

    return pl.pallas_call(
        body, name="rope_bwd", grid=(s // DIL_ROWS,),
        in_specs=[_dil_spec(r) for _ in range(3) for r in _RS] + [pl.BlockSpec((DIL_ROWS, 128), lambda i: (i, 0))] * 3
        + [pl.BlockSpec(memory_space=pl.ANY)],
        out_specs=pl.BlockSpec((DIL_ROWS, tail), lambda i: (i, P_AQ // tail)),
        out_shape=jax.ShapeDtypeStruct((s, P_W), BF16), input_output_aliases={12: 0},
        scratch_shapes=_DIL_SCRATCH, compiler_params=_params(("parallel",)),
    )(*[datt[g][t] for t in range(3) for g in range(3)], cos_t, sin_t, d_glr, dp)


def _tri_dot(tri, t):
    tb = tri.astype(BF16)
    hi = t.astype(BF16)
    r1 = t - hi.astype(F32)
    mid = r1.astype(BF16)
    lo = (r1 - mid.astype(F32)).astype(BF16)
    return _dg(tb, hi, 1, 0) + _dg(tb, mid, 1, 0) + _dg(tb, lo, 1, 0)


def _gla_decays(la_c, tri):
    b = _tri_dot(tri, la_c)
    row = lax.broadcasted_iota(jnp.int32, b.shape, 0)
    bmid = jnp.sum(jnp.where(row == GLA_CHUNK // 2 - 1, b, 0.0), axis=0, keepdims=True)
    blast = jnp.sum(jnp.where(row == GLA_CHUNK - 1, b, 0.0), axis=0, keepdims=True)
    return b, bmid, blast


def _gla_fwd(p, la, s, comm=()):
    tb, ch = GLA_BLOCK, GLA_CHUNK
    nb, nc = s // tb, tb // ch
    scale = GLA_DK ** -0.5
    c_ins, c_outs, c_alias, c_scratch = _carry(comm, 4, 2)

    def body(q_ref, k_ref, v_ref, la_ref, *rest):
        ci, (o_ref, st_ref) = rest[:len(c_ins)], rest[len(c_ins):len(c_ins) + 2]
        co, state = rest[len(c_ins) + 2:len(c_ins) + 2 + len(c_outs)], rest[len(c_ins) + 2 + len(c_outs)]
        step = pl.program_id(0)
        if comm:
            @pl.when(step == 0)
            def _():
                _comm_phase(comm, ci, co, rest[-2], rest[-1], True)

        _gla_fwd_step(q_ref, k_ref, v_ref, la_ref, o_ref, st_ref, state)
        if comm:
            @pl.when(step == nb - 1)
            def _():
                _comm_phase(comm, ci, co, rest[-2], rest[-1], False)

    def _gla_fwd_step(q_ref, k_ref, v_ref, la_ref, o_ref, st_ref, state):
        @pl.when(pl.program_id(0) == 0)
        def _():
            state[...] = jnp.zeros_like(state)

        ri = lax.broadcasted_iota(jnp.int32, (ch, ch), 0)
        ci = lax.broadcasted_iota(jnp.int32, (ch, ch), 1)
        causal = ci <= ri
        tri = causal.astype(F32)

        def chunk(c, carry):
            sl = pl.ds(pl.multiple_of(c * ch, ch), ch)
            b, bmid, blast = _gla_decays(la_ref[sl, :], tri)
            q = q_ref[sl, :].astype(F32) * scale
            k = k_ref[sl, :].astype(F32)
            v = v_ref[sl, :]
            qgt = (q * jnp.exp(b)).astype(BF16)
            qgn = (q * jnp.exp(b - bmid)).astype(BF16)
            kgn = (k * jnp.exp(bmid - b)).astype(BF16)
            kd = (k * jnp.exp(blast - b)).astype(BF16)
            dec = jnp.exp(blast)
            sts = [state[h] for h in range(GLA_H)]
            outs, news = [], []
            for h in range(GLA_H):
                hk, hv = slice(h * GLA_DK, (h + 1) * GLA_DK), slice(h * GLA_DV, (h + 1) * GLA_DV)
                a = jnp.where(causal, _dg(qgn[:, hk], kgn[:, hk], 1, 1), 0.0)
                outs.append(_dg(a.astype(BF16), v[:, hv], 1, 0) + _dg(qgt[:, hk], sts[h].astype(BF16), 1, 1))
                news.append(dec[:, hk] * sts[h] + _dg(v[:, hv], kd[:, hk], 0, 0))
            for h in range(GLA_H):
                st_ref[h, c] = sts[h]
                state[h] = news[h]
            o_ref[sl, :] = jnp.concatenate(outs, axis=1)
            return carry

        lax.fori_loop(0, nc, chunk, 0)

    hw = GLA_H * GLA_DK
    res = pl.pallas_call(
        body, name="gla_fwd", grid=(nb,),
        in_specs=[pl.BlockSpec((tb, hw), lambda t: (t, P_GQ // hw)),
                  pl.BlockSpec((tb, hw), lambda t: (t, P_GK // hw)),
                  pl.BlockSpec((tb, GLA_H * GLA_DV), lambda t: (t, P_GV // (GLA_H * GLA_DV))),
                  pl.BlockSpec((tb, hw), lambda t: (t, 0))] + [HBM] * len(c_ins),
        out_specs=[pl.BlockSpec((tb, GLA_H * GLA_DV), lambda t: (t, 0)),
                   pl.BlockSpec((GLA_H, nc, GLA_DV, GLA_DK), lambda t: (0, t, 0, 0))] + [HBM] * len(c_outs),
        out_shape=[jax.ShapeDtypeStruct((s, GLA_H * GLA_DV), F32),
                   jax.ShapeDtypeStruct((GLA_H, s // ch, GLA_DV, GLA_DK), F32)] + c_outs,
        scratch_shapes=[pltpu.VMEM((GLA_H, GLA_DV, GLA_DK), F32)] + c_scratch,
        input_output_aliases=c_alias,
        compiler_params=_params(("arbitrary",)),
    )(p, p, p, la, *c_ins)
    return res[0], res[1], _split_units(comm, res[2:])


def _gla_bwd(p, la, states, do, s, dp, comm=()):
    tb, ch = GLA_BLOCK, GLA_CHUNK
    nb, nc = s // tb, tb // ch
    scale = GLA_DK ** -0.5
    c_ins, c_outs, c_alias, c_scratch = _carry(comm, 7, 4)

    def body(q_ref, k_ref, v_ref, la_ref, st_ref, do_ref, dp_in, *rest):
        ci, outs = rest[:len(c_ins)], rest[len(c_ins):len(c_ins) + 4]
        co, dstate = rest[len(c_ins) + 4:len(c_ins) + 4 + len(c_outs)], rest[len(c_ins) + 4 + len(c_outs)]
        step = pl.program_id(0)
        if comm:
            @pl.when(step == 0)
            def _():
                _comm_phase(comm, ci, co, rest[-2], rest[-1], True)

        _gla_bwd_step(q_ref, k_ref, v_ref, la_ref, st_ref, do_ref, *outs, dstate)
        if comm:
            @pl.when(step == nb - 1)
            def _():
                _comm_phase(comm, ci, co, rest[-2], rest[-1], False)

    def _gla_bwd_step(q_ref, k_ref, v_ref, la_ref, st_ref, do_ref, dq_ref, dk_ref, dv_ref, dla_ref, dstate):
        @pl.when(pl.program_id(0) == 0)
        def _():
            dstate[...] = jnp.zeros_like(dstate)

        ri = lax.broadcasted_iota(jnp.int32, (ch, ch), 0)
        ci = lax.broadcasted_iota(jnp.int32, (ch, ch), 1)
        causal = ci <= ri
        tri = causal.astype(F32)
        tri_t = (ci >= ri).astype(F32)

        def chunk(cc, carry):
            c = nc - 1 - cc
            sl = pl.ds(pl.multiple_of(c * ch, ch), ch)
            b, bmid, blast = _gla_decays(la_ref[sl, :], tri)
            q = q_ref[sl, :].astype(F32) * scale
            k = k_ref[sl, :].astype(F32)
            v = v_ref[sl, :]
            e_b, e_qn, e_kn, e_kd = jnp.exp(b), jnp.exp(b - bmid), jnp.exp(bmid - b), jnp.exp(blast - b)
            dec = jnp.exp(blast)
            qgt, qgn, kgn, kd = q * e_b, q * e_qn, k * e_kn, k * e_kd
            qgt_b, qgn_b, kgn_b, kd_b = qgt.astype(BF16), qgn.astype(BF16), kgn.astype(BF16), kd.astype(BF16)
            do_b = do_ref[sl, :].astype(BF16)
            st0s = [st_ref[h, c] for h in range(GLA_H)]
            dsts = [dstate[h] for h in range(GLA_H)]
            dqgn, dqgt, dkgn, dkd, dvs, ddec, news = [], [], [], [], [], [], []
            for h in range(GLA_H):
                hk, hv = slice(h * GLA_DK, (h + 1) * GLA_DK), slice(h * GLA_DV, (h + 1) * GLA_DV)
                dst_b = dsts[h].astype(BF16)
                a = jnp.where(causal, _dg(qgn_b[:, hk], kgn_b[:, hk], 1, 1), 0.0).astype(BF16)
                da = jnp.where(causal, _dg(do_b[:, hv], v[:, hv], 1, 1), 0.0).astype(BF16)
                dqgn.append(_dg(da, kgn_b[:, hk], 1, 0))
                dqgt.append(_dg(do_b[:, hv], st0s[h].astype(BF16), 1, 0))
                dkgn.append(_dg(da, qgn_b[:, hk], 0, 0))
                dvs.append(_dg(a, do_b[:, hv], 0, 0) + _dg(kd_b[:, hk], dst_b, 1, 1))
                dkd.append(_dg(v[:, hv], dst_b, 1, 0))
                ddec.append(jnp.sum(st0s[h] * dsts[h], axis=0, keepdims=True))
                news.append(dec[:, hk] * dsts[h] + _dg(do_b[:, hv], qgt_b[:, hk], 0, 0))
            for h in range(GLA_H):
                dstate[h] = news[h]
            cat = lambda parts: jnp.concatenate(parts, axis=1)
            dqgn, dqgt, dkgn, dkd, ddec = cat(dqgn), cat(dqgt), cat(dkgn), cat(dkd), cat(ddec)
            dq_ref[sl, :] = (scale * (dqgn * e_qn + dqgt * e_b)).astype(dq_ref.dtype)
            dk_ref[sl, :] = (dkgn * e_kn + dkd * e_kd).astype(dk_ref.dtype)
            dv_ref[sl, :] = cat(dvs).astype(dv_ref.dtype)
            db = dqgn * qgn + dqgt * qgt - dkgn * kgn - dkd * kd
            extra = jnp.sum(dkd * kd, axis=0, keepdims=True) + ddec * dec
            dla_ref[sl, :] = _tri_dot(tri_t, db) + extra
            return carry

        lax.fori_loop(0, nc, chunk, 0)

    rev = lambda t: nb - 1 - t
    hw, vw = GLA_H * GLA_DK, GLA_H * GLA_DV
    res = pl.pallas_call(
        body, name="gla_bwd", grid=(nb,),
        in_specs=[pl.BlockSpec((tb, hw), lambda t: (rev(t), P_GQ // hw)),
                  pl.BlockSpec((tb, hw), lambda t: (rev(t), P_GK // hw)),
                  pl.BlockSpec((tb, vw), lambda t: (rev(t), P_GV // vw)),
                  pl.BlockSpec((tb, hw), lambda t: (rev(t), 0)),
                  pl.BlockSpec((GLA_H, nc, GLA_DV, GLA_DK), lambda t: (0, rev(t), 0, 0)),
                  pl.BlockSpec((tb, vw), lambda t: (rev(t), 0)), pl.BlockSpec(memory_space=pl.ANY)] + [HBM] * len(c_ins),
        out_specs=[pl.BlockSpec((tb, hw), lambda t: (rev(t), 0)),
                   pl.BlockSpec((tb, hw), lambda t: (rev(t), 0)),
                   pl.BlockSpec((tb, vw), lambda t: (rev(t), P_GV // vw)),
                   pl.BlockSpec((tb, hw), lambda t: (rev(t), 0))] + [HBM] * len(c_outs),
        out_shape=[jax.ShapeDtypeStruct((s, hw), BF16),
                   jax.ShapeDtypeStruct((s, hw), BF16),
                   jax.ShapeDtypeStruct((s, P_W), BF16),
                   jax.ShapeDtypeStruct((s, hw), F32)] + c_outs,
        scratch_shapes=[pltpu.VMEM((GLA_H, GLA_DV, GLA_DK), F32)] + c_scratch,
        input_output_aliases={6: 2, **c_alias},
        compiler_params=_params(("arbitrary",)),
    )(p, p, p, la, states, do, dp, *c_ins)
    return res[0], res[1], res[2], res[3], _split_units(comm, res[4:])


def _head_masks():
    lane = lax.broadcasted_iota(jnp.int32, (1, 4 * ATT_HD), 1)
    return [(lane >= h * ATT_HD) & (lane < (h + 1) * ATT_HD) for h in range(4)]


def _attn_fwd(qv, kv, pv, g, r, s):
    ln = s // r
    nblk = ln // ATT_BLK
    qcol = lambda pr: pr
    vcol = qcol
    prev = lambda n: jnp.maximum(n - 1, 0)

    def body(q_ref, kp_ref, kc_ref, vp_ref, vc_ref, o_ref, lse_ref):
        has_prev = pl.program_id(1) > 0
        ri = lax.broadcasted_iota(jnp.int32, (ATT_BLK, ATT_BLK), 0)
        ci = lax.broadcasted_iota(jnp.int32, (ATT_BLK, ATT_BLK), 1)
        m_cur = ci <= ri
        m_prev = (ci >= ri) & has_prev
        q, kp, kc, vp, vc = q_ref[...], kp_ref[...], kc_ref[...], vp_ref[...], vc_ref[...]
        o = jnp.zeros((ATT_BLK, 256), F32)
        lse = jnp.zeros((ATT_BLK, 256), F32)
        for hm in _head_masks():
            qm = jnp.where(hm, q, jnp.zeros_like(q))
            sc = jnp.where(m_cur, _dg(qm, kc, 1, 1) * 0.125, NEG)
            sp = jnp.where(m_prev, _dg(qm, kp, 1, 1) * 0.125, NEG)
            mx = jnp.maximum(jnp.max(sc, axis=1, keepdims=True), jnp.max(sp, axis=1, keepdims=True))
            pc, pp = jnp.exp(sc - mx), jnp.exp(sp - mx)
            den = jnp.sum(pc, axis=1, keepdims=True) + jnp.sum(pp, axis=1, keepdims=True)
            oh = (_dg(pc.astype(BF16), vc, 1, 0) + _dg(pp.astype(BF16), vp, 1, 0)) / den
            o = jnp.where(hm, oh, o)
            lse = jnp.where(hm, mx + jnp.log(den), lse)
        o_ref[...] = o.astype(o_ref.dtype)
        lse_ref[...] = lse

    blk = (ATT_BLK, 256)
    o, lse = pl.pallas_call(
        body, name=f"attn_fwd_{g}", grid=(r, nblk),
        in_specs=[pl.BlockSpec(blk, lambda pr, n: (n, qcol(pr))),
                  pl.BlockSpec(blk, lambda pr, n: (prev(n), qcol(pr))),
                  pl.BlockSpec(blk, lambda pr, n: (n, qcol(pr))),
                  pl.BlockSpec(blk, lambda pr, n: (prev(n), vcol(pr))),
                  pl.BlockSpec(blk, lambda pr, n: (n, vcol(pr)))],
        out_specs=[pl.BlockSpec(blk, lambda pr, n: (n, pr)), pl.BlockSpec(blk, lambda pr, n: (n, pr))],
        out_shape=[jax.ShapeDtypeStruct((ln, r * 256), BF16), jax.ShapeDtypeStruct((ln, r * 256), F32)],
        compiler_params=_params(("parallel", "parallel")),
    )(qv, kv, kv, pv, pv)
    return o, lse


def _attn_bwd(qv, kv, pv, dov, ov, lv, g, r, s):
    ln = s // r
    nblk = ln // ATT_BLK
    qcol = lambda pr: pr
    vcol = qcol
    prev = lambda n: jnp.maximum(n - 1, 0)
    nxt = lambda n: jnp.minimum(n + 1, nblk - 1)

    def body(qc_ref, qn_ref, kp_ref, kc_ref, vp_ref, vc_ref, doc_ref, don_ref, oc_ref, on_ref, lc_ref, ln_ref,
             dq_ref, dk_ref, dv_ref):
        n = pl.program_id(1)
        has_prev, has_next = n > 0, n < nblk - 1
        ri = lax.broadcasted_iota(jnp.int32, (ATT_BLK, ATT_BLK), 0)
        ci = lax.broadcasted_iota(jnp.int32, (ATT_BLK, ATT_BLK), 1)
        m_cur = ci <= ri
        m_prev = (ci >= ri) & has_prev
        m_next = (ci >= ri) & has_next
        qc, qn, kp, kc, vp, vc = qc_ref[...], qn_ref[...], kp_ref[...], kc_ref[...], vp_ref[...], vc_ref[...]
        doc, don = doc_ref[...], don_ref[...]
        pc_full = doc.astype(F32) * oc_ref[...].astype(F32)
        pn_full = don.astype(F32) * on_ref[...].astype(F32)
        lc, lnx = lc_ref[...], ln_ref[...]
        dq = jnp.zeros((ATT_BLK, 256), F32)
        dk = jnp.zeros((ATT_BLK, 256), F32)
        dv = jnp.zeros((ATT_BLK, 256), F32)
        zb = jnp.zeros_like(qc)
        for hm in _head_masks():
            qcm, qnm = jnp.where(hm, qc, zb), jnp.where(hm, qn, zb)
            docm, donm = jnp.where(hm, doc, zb), jnp.where(hm, don, zb)
            lse_c = jnp.max(jnp.where(hm, lc, NEG), axis=1, keepdims=True)
            lse_n = jnp.max(jnp.where(hm, lnx, NEG), axis=1, keepdims=True)
            del_c = jnp.sum(jnp.where(hm, pc_full, 0.0), axis=1, keepdims=True)
            del_n = jnp.sum(jnp.where(hm, pn_full, 0.0), axis=1, keepdims=True)
            pr_ = jnp.where(m_cur, jnp.exp(_dg(qcm, kc, 1, 1) * 0.125 - lse_c), 0.0)
            ds = (pr_ * (_dg(docm, vc, 1, 1) - del_c) * 0.125).astype(BF16)
            dqh = _dg(ds, kc, 1, 0)
            dkh = _dg(ds, qc, 0, 0)
            dvh = _dg(pr_.astype(BF16), doc, 0, 0)
            pr_ = jnp.where(m_prev, jnp.exp(_dg(qcm, kp, 1, 1) * 0.125 - lse_c), 0.0)
            ds = (pr_ * (_dg(docm, vp, 1, 1) - del_c) * 0.125).astype(BF16)
            dqh = dqh + _dg(ds, kp, 1, 0)
            pr_ = jnp.where(m_next, jnp.exp(_dg(qnm, kc, 1, 1) * 0.125 - lse_n), 0.0)
            ds = (pr_ * (_dg(donm, vc, 1, 1) - del_n) * 0.125).astype(BF16)
            dkh = dkh + _dg(ds, qn, 0, 0)
            dvh = dvh + _dg(pr_.astype(BF16), don, 0, 0)
            dq = jnp.where(hm, dqh, dq)
            dk = jnp.where(hm, dkh, dk)
            dv = jnp.where(hm, dvh, dv)
        dq_ref[...] = dq.astype(dq_ref.dtype)
        dk_ref[...] = dk.astype(dk_ref.dtype)
        dv_ref[...] = dv.astype(dv_ref.dtype)

    blk = (ATT_BLK, 256)
    cur = lambda col: pl.BlockSpec(blk, lambda pr, n: (n, col(pr)))
    prv = lambda col: pl.BlockSpec(blk, lambda pr, n: (prev(n), col(pr)))
    nx = lambda col: pl.BlockSpec(blk, lambda pr, n: (nxt(n), col(pr)))
    own = lambda pr: pr
    outs = pl.pallas_call(
        body, name=f"attn_bwd_{g}", grid=(r, nblk),
        in_specs=[cur(qcol), nx(qcol), prv(qcol), cur(qcol), prv(vcol), cur(vcol),
                  cur(own), nx(own), cur(own), nx(own), cur(own), nx(own)],
        out_specs=[cur(own), cur(own), cur(own)],
        out_shape=[jax.ShapeDtypeStruct((ln, r * 256), BF16)] * 3,
        compiler_params=_params(("parallel", "parallel")),
    )(qv, qv, kv, kv, pv, pv, dov, dov, ov, ov, lv, lv)
    return outs


def _gelu_parts(gv):
    cdf = 0.5 * (1.0 + lax.erf(gv * (2.0 ** -0.5)))
    pdf = jnp.exp(-0.5 * gv * gv) * (1.0 / math.sqrt(2.0 * math.pi))
    return cdf, pdf


def _pick_row(t, k):
    row = lax.broadcasted_iota(jnp.int32, t.shape, 0)
    return jnp.sum(jnp.where(row == k, t, 0.0), axis=0, keepdims=True)


def _shift_rows(ub, halo, n):
    rb = ub.shape[0]
    r = lax.broadcasted_iota(jnp.int32, (rb, rb), 0)
    c = lax.broadcasted_iota(jnp.int32, (rb, rb), 1)
    out = _dg((r - c == n).astype(BF16), ub, 1, 0)
    top = out[0:8, :]
    row = lax.broadcasted_iota(jnp.int32, top.shape, 0)
    for k in range(n):
        top = jnp.where(row == k, _pick_row(halo, 16 - n + k), top)
    return jnp.concatenate([top, out[8:, :]], axis=0)


def _shift_rows_up(ub, halo, n):
    rb = ub.shape[0]
    r = lax.broadcasted_iota(jnp.int32, (rb, rb), 0)
    c = lax.broadcasted_iota(jnp.int32, (rb, rb), 1)
    out = _dg((c - r == n).astype(BF16), ub, 1, 0)
    bottom = out[rb - 8:, :]
    row = lax.broadcasted_iota(jnp.int32, bottom.shape, 0)
    for k in range(n):
        bottom = jnp.where(row == 8 - n + k, _pick_row(halo, k), bottom)
    return jnp.concatenate([out[:rb - 8, :], bottom], axis=0)


def _conv(ub, halo, cw, cb):
    return (cb + _pick_row(cw, 0) * _shift_rows(ub, halo, 2) + _pick_row(cw, 1) * _shift_rows(ub, halo, 1)
            + _pick_row(cw, 2) * ub.astype(F32))


def _local_step(x, mod, pos_col, target, sm, w_sh, g0, chip, core):
    s = x.shape[0]
    shift1, scale1, gate1, shift2, scale2, gate2 = [mod[i:i + 1, :] for i in range(6)]
    rb = 256
    chip1 = chip.reshape(1)

    def f_norm1(c, i, xv, nw, sc, sh):
        return ((xv * _rms(xv) * nw) * (1.0 + sc) + sh,)

    (h,) = _rowcall(f_norm1, [_rows(x, rb), _full(sm["n1w"]), _full(scale1), _full(shift1)],
                    [_orow(s, D, BF16, rb)], n_rows=s, rb=rb, name="norm1")
    own = lambda got, i: lax.dynamic_update_slice(got, w_sh[i], (chip, 0, 0))
    invf = jnp.tile(ROPE_THETA ** (-jnp.arange(ATT_HD // 2, dtype=F32) / (ATT_HD // 2)), 4).reshape(1, 128)
    cos_t, sin_t = _rope_tables(pos_col, invf, s)
    _, got0 = _unit_wait(*g0[:4], after=[h, cos_t, sin_t], name="gather_w_in_wait")
    [got0] = _comm_call("gather_w_in_d2d", [_u_gather_d2d(got0, (0,))])
    w = dict(win=_win_assemble(own(got0[0], 0)))
    p, [got123, got4] = _mm(h, w["win"], "in_proj", tm=1024, tn=1536,
                            comm=[_u_gather_ici(w_sh, (1, 2, 3)), _u_gather_ici(w_sh, (4,))])

    def f_gla_pre(c, i, glr, w2, gb):
        z = _dg(glr, w2.astype(BF16), 1, 0) + gb
        return ((jnp.minimum(z, 0.0) - jnp.log(1.0 + jnp.exp(-jnp.abs(z)))) * (1.0 / GLA_TAU),)

    (la,) = _rowcall(f_gla_pre, [_rows(p, rb, 128, P_LR // 128), _full(sm["w2"]), _full(sm["gb"])],
                     [_orow(s, 512, F32, rb)], n_rows=s, rb=rb, name="gla_pre")
    o_gla, states, [got123, got5] = _gla_fwd(p, la, s, comm=[_u_gather_d2d(got123, (1, 2, 3)), _u_gather_ici(w_sh, (5,))])
    got45 = got4 + got5
    w.update(wgb=own(got123[0], 1).reshape(1024, D), wab=_cols_join(own(got123[1], 2)), wout=own(got123[2], 3).reshape(D, D))

    def f_gla_post(c, i, ov, gnw, gr):
        on = jnp.concatenate([ov[:, k * 256:(k + 1) * 256] * _rms(ov[:, k * 256:(k + 1) * 256]) * gnw
                              for k in range(GLA_H)], axis=1)
        g = gr.astype(F32)
        return (on * (g * _sigmoid(g)),)

    (og,) = _rowcall(f_gla_post, [_rows(o_gla, rb), _full(sm["gnw"]), _rows(p, rb, 1024, P_GR // 1024)],
                     [_orow(s, 1024, BF16, rb)], n_rows=s, rb=rb, name="gla_post")
    y_gla = _mm(og, w["wgb"], "gla_branch")

    q_d, k_d, v_d = _rope_fwd(p, cos_t, sin_t, s)
    att = [_attn_fwd(q_d[g], k_d[g], v_d[g], g, r, s) for g, r in enumerate(_RS)]
    o_att, lse, o_d1, o_d2, lse_d1, lse_d2 = _attn_combine(att, s)
    y_att = _mm(o_att, w["wab"], "attn_branch")

    def f_merge(c, i, ma, mb, yg, ya):
        return (_sigmoid(ma.astype(F32)) * yg.astype(F32) + _sigmoid(mb.astype(F32)) * ya.astype(F32),)

    (mixed,) = _rowcall(f_merge, [_rows(p, rb, D, P_MA // D), _rows(p, rb, D, P_MB // D), _rows(y_gla, rb), _rows(y_att, rb)],
                        [_orow(s, D, BF16, rb)], n_rows=s, rb=rb, name="merge")
    z1, [got45] = _mm(mixed, w["wout"], "out_proj", comm=[_u_gather_d2d(got45, (4, 5))])
    w.update(wup=own(got45[0], 4), wdown=own(got45[1], 5).reshape(D_FF, D))

    def f_norm2(c, i, xv, z, g1, nw, sc, sh):
        x1 = xv + g1 * z.astype(F32)
        return (x1, (x1 * _rms(x1) * nw) * (1.0 + sc) + sh)

    x1, h2 = _rowcall(f_norm2, [_rows(x, rb), _rows(z1, rb), _full(gate1), _full(sm["n2w"]), _full(scale2), _full(shift2)],
                      [_orow(s, D, F32, rb), _orow(s, D, BF16, rb)], n_rows=s, rb=rb, name="norm2")
    u = _mm(h2, w["wup"], "up_proj", b_shards=True)

    cwid = 2 * W_UP_SH

    def f_ffn(c, i, uv, hl, cw, cb):
        uc = _conv(uv, hl.astype(F32) * (i > 0).astype(F32), cw, cb)
        val, gt = uc[:, :W_UP_SH], uc[:, W_UP_SH:]
        cdf, _ = _gelu_parts(gt)
        return (gt * cdf * val,)

    ccol = lambda c: c
    (hidden,) = _rowcall(f_ffn, [_rows(u, rb, cwid, ccol), _halo(u, rb, 16, cwid, ccol, True),
                                 _full(sm["cw"], cwid, ccol), _full(sm["cb"], cwid, ccol)],
                         [_orow(s, D_FF, BF16, rb, W_UP_SH, ccol)], n_rows=s, rb=rb, name="conv_geglu", ncol=2)
    z2 = _mm(hidden, w["wdown"], "down_proj", tk=D_FF)

    def f_final(c, i, x1v, z, g2, fw, tgt):
        x2 = x1v + g2 * z.astype(F32)
        r = _rms(x2)
        xh = x2 * r
        e = xh * fw - tgt
        loss = 0.5 * jnp.sum(jnp.mean(e * e, axis=-1, keepdims=True), axis=0, keepdims=True)
        dy = e * (1.0 / D)
        dxh = dy * fw
        dx2 = r * (dxh - xh * jnp.mean(dxh * xh, axis=-1, keepdims=True))
        return (loss, dx2, dx2 * g2, _csum(dy * xh), _csum(dx2 * z.astype(F32)))

    loss, dx2, dz2, d_fnw, d_gate2 = _rowcall(
        f_final, [_rows(x1, rb), _rows(z2, rb), _full(gate2), _full(sm["fnw"]), _rows(target, rb)],
        [_oacc(1, 1), _orow(s, D, F32, rb), _orow(s, D, BF16, rb), _oacc(1, D), _oacc(1, D)],
        n_rows=s, rb=rb, name="final_loss")
    d_hidden = _mm(dz2, w["wdown"], "down_proj_dx", tb=True, tn=1408)
    g_wdown = _mm(hidden, dz2, "down_proj_dw", ta=True, out_dtype=F32, tm=1408, tn=1024, tk=2048)

    def f_ffn_bwd(c, i, uv, hl, dh, cw, cb):
        uf = uv.astype(F32)
        hf = hl.astype(F32) * (i > 0).astype(F32)
        u1, u2 = _shift_rows(uv, hf, 1), _shift_rows(uv, hf, 2)
        uc = cb + _pick_row(cw, 0) * u2 + _pick_row(cw, 1) * u1 + _pick_row(cw, 2) * uf
        val, gt = uc[:, :W_UP_SH], uc[:, W_UP_SH:]
        cdf, pdf = _gelu_parts(gt)
        dhf = dh.astype(F32)
        duc = jnp.concatenate([dhf * (gt * cdf), dhf * val * (cdf + gt * pdf)], axis=1)
        dcw = jnp.concatenate([_csum(duc * u2), _csum(duc * u1), _csum(duc * uf)], axis=0)
        return (duc, _csum(duc), dcw)

    duc, d_cb, d_cw = _rowcall(
        f_ffn_bwd, [_rows(u, rb, cwid, ccol), _halo(u, rb, 16, cwid, ccol, True), _rows(d_hidden, rb, W_UP_SH, ccol),
                    _full(sm["cw"], cwid, ccol), _full(sm["cb"], cwid, ccol)],
        [_orow(s, 2 * D_FF, BF16, rb, cwid, ccol), _oacc(1, 2 * D_FF, cwid, ccol), _oacc(3, 2 * D_FF, cwid, ccol)],
        n_rows=s, rb=rb, name="conv_geglu_bwd", ncol=2)

    def f_conv_t(c, i, dv, hl, cw):
        df = dv.astype(F32)
        hf = hl.astype(F32) * (i < s // rb - 1).astype(F32)
        return (_pick_row(cw, 2) * df + _pick_row(cw, 1) * _shift_rows_up(dv, hf, 1) + _pick_row(cw, 0) * _shift_rows_up(dv, hf, 2),)

    (du,) = _rowcall(f_conv_t, [_rows(duc, rb, cwid, ccol), _halo(duc, rb, 16, cwid, ccol, False), _full(sm["cw"], cwid, ccol)],
                     [_orow(s, 2 * D_FF, BF16, rb, cwid, ccol)], n_rows=s, rb=rb, name="conv_transpose", ncol=2)
    g_wup = _mm(h2, du, "up_proj_dw", ta=True, out_dtype=F32, tm=1024, tk=2048, o_shards=True)
    gs45 = [g_wup, g_wdown.reshape(4, W_DOWN_SH, 1024)]
    d_h2, [land45] = _mm(du, w["wup"], "up_proj_dx", tb=True, b_shards=True, comm=[_u_pair_send(gs45, (4, 5))])
    ts45 = [_pair_add(g, ld, core, "grad_pair_add_" + BIG[i]) for g, ld, i in zip(gs45, land45, (4, 5))]

    def f_norm2_bwd(c, i, x1v, dh, dxr, z, nw, sc, g1):
        dxn, dsh, dsc, dnw = _norm_bwd(x1v, dh.astype(F32), nw, sc)
        dx1 = dxr + dxn
        return (dx1, dx1 * g1, dsh, dsc, dnw, _csum(dx1 * z.astype(F32)))

    dx1, dz1, d_shift2, d_scale2, d_n2w, d_gate1 = _rowcall(
        f_norm2_bwd, [_rows(x1, rb), _rows(d_h2, rb), _rows(dx2, rb), _rows(z1, rb), _full(sm["n2w"]), _full(scale2), _full(gate1)],
        [_orow(s, D, F32, rb), _orow(s, D, BF16, rb), _oacc(1, D), _oacc(1, D), _oacc(1, D), _oacc(1, D)],
        n_rows=s, rb=rb, name="norm2_bwd")
    d_mixed = _mm(dz1, w["wout"], "out_proj_dx", tb=True)
    g_wout = _mm(mixed, dz1, "out_proj_dw", ta=True, out_dtype=F32, tk=2048)

    def f_merge_bwd(c, i, dm, ma, mb, yg, ya):
        dmf, ygf, yaf = dm.astype(F32), yg.astype(F32), ya.astype(F32)
        sa, sb = _sigmoid(ma.astype(F32)), _sigmoid(mb.astype(F32))
        return (dmf * sa, dmf * sb, jnp.concatenate([dmf * ygf * sa * (1.0 - sa), dmf * yaf * sb * (1.0 - sb)], axis=1))

    dy_gla, dy_att, dp = _rowcall(
        f_merge_bwd, [_rows(d_mixed, rb), _rows(p, rb, D, P_MA // D), _rows(p, rb, D, P_MB // D), _rows(y_gla, rb), _rows(y_att, rb)],
        [_orow(s, D, BF16, rb)] * 2 + [_orow(s, P_W, BF16, rb, 2 * D, lambda c: P_MA // (2 * D))], n_rows=s, rb=rb, name="merge_bwd")
    d_og = _mm(dy_gla, w["wgb"], "gla_branch_dx", tb=True)
    g_wgb = _mm(og, dy_gla, "gla_branch_dw", ta=True, out_dtype=F32, tk=2048)
    d_oatt = _mm(dy_att, w["wab"], "attn_branch_dx", tb=True)
    g_wab = _mm(o_att, dy_att, "attn_branch_dw", ta=True, out_dtype=F32, tk=2048)

    def f_gla_post_bwd(c, i, ov, gnw, gr, dog):
        g = gr.astype(F32)
        sg = _sigmoid(g)
        silu = g * sg
        dof = dog.astype(F32)
        don = dof * silu
        on_parts, do_parts, dgn = [], [], jnp.zeros((1, 256), F32)
        for k in range(GLA_H):
            oh = ov[:, k * 256:(k + 1) * 256]
            dh = don[:, k * 256:(k + 1) * 256]
            r = _rms(oh)
            xh = oh * r
            dgn = dgn + _csum(dh * xh)
            dxh = dh * gnw
            do_parts.append(r * (dxh - xh * jnp.mean(dxh * xh, axis=-1, keepdims=True)))
            on_parts.append(xh * gnw)
        on = jnp.concatenate(on_parts, axis=1)
        dgr = dof * on * (sg * (1.0 + g * (1.0 - sg)))
        return (jnp.concatenate(do_parts, axis=1), dgr, dgn)

    do_gla, dp, d_gnw = _rowcall(
        f_gla_post_bwd, [_rows(o_gla, rb), _full(sm["gnw"]), _rows(p, rb, 1024, P_GR // 1024), _rows(d_og, rb)],
        [_orow(s, 1024, F32, rb), _orow(s, P_W, BF16, rb, 1024, lambda c: P_GR // 1024), _oacc(1, 256)],
        n_rows=s, rb=rb, name="gla_post_bwd", into=(dp, 1))
    gs123 = [g_wgb.reshape(4, 256, 1024), _cols_split(g_wab), g_wout.reshape(4, 256, 1024)]
    d_gq, d_gk, dp, d_la, [r4, land123] = _gla_bwd(p, la, states, do_gla, s, dp,
                                                   comm=[_u_chip_exchange(ts45[:1]), _u_pair_send(gs123, (1, 2, 3))])
    half4 = [_chip_sum(ts45[0], r4[0], chip1, "grad_chip_sum_w_up")]
    ts123 = [_pair_add(g, ld, core, "grad_pair_add_" + BIG[i]) for g, ld, i in zip(gs123, land123, (1, 2, 3))]

    def f_gla_pre_bwd(c, i, lav, dlav, glr, w2):
        dz = dlav * (1.0 / GLA_TAU) * (1.0 - jnp.exp(GLA_TAU * lav))
        dzb = dz.astype(BF16)
        return (_dg(dzb, w2.astype(BF16), 1, 1), _csum(dz), _dg(glr, dzb, 0, 0))

    d_glr, d_gb, d_w2 = _rowcall(
        f_gla_pre_bwd, [_rows(la, rb), _rows(d_la, rb), _rows(p, rb, 128, P_LR // 128), _full(sm["w2"])],
        [_orow(s, 128, BF16, rb), _oacc(1, 512), _oacc(128, 512)], n_rows=s, rb=rb, name="gla_pre_bwd")

    do_d = [d_oatt] + list(_dilate(d_oatt, s))
    datt = [_attn_bwd(q_d[g], k_d[g], v_d[g], do_d[g], (o_att, o_d1, o_d2)[g], (lse, lse_d1, lse_d2)[g], g, r, s)
            for g, r in enumerate(_RS)]
    dp = _rope_bwd(datt, d_glr, dp, cos_t, sin_t, s)
    dp = lax.dynamic_update_slice(dp, jnp.concatenate([d_gq, d_gk], axis=1), (0, P_GQ))
    g_win, [r1235, oth4] = _mm(h, dp, "in_proj_dw", ta=True, out_dtype=F32, tm=1024, tn=1536, tk=2048,
                               comm=[_u_chip_exchange(ts123 + ts45[1:]), _u_pair_join(half4)])
    half1235 = [_chip_sum(t, r, chip1, "grad_chip_sum_" + BIG[i]) for t, r, i in zip(ts123 + ts45[1:], r1235, (1, 2, 3, 5))]
    gs0 = [_win_split(g_win)]
    d_h, [land0, oth1235] = _mm(dp, w["win"], "in_proj_dx", tb=True, tk=3840,
                                comm=[_u_pair_send(gs0, (0,)), _u_pair_join(half1235)])
    half123, half45 = half1235[:3], half4 + half1235[3:]
    oth123, oth45 = oth1235[:3], oth4 + oth1235[3:]
    ts0 = _pair_add(gs0[0], land0[0], core, "grad_pair_add_w_in")

    def f_norm1_bwd(c, i, xv, dh, dxr, nw, sc):
        dxn, dsh, dsc, dnw = _norm_bwd(xv, dh.astype(F32), nw, sc)
        return (dxr + dxn, dsh, dsc, dnw)

    grad_x, d_shift1, d_scale1, d_n1w = _rowcall(
        f_norm1_bwd, [_rows(x, rb), _rows(d_h, rb), _rows(dx1, rb), _full(sm["n1w"]), _full(scale1)],
        [_orow(s, D, F32, rb), _oacc(1, D), _oacc(1, D), _oacc(1, D)], n_rows=s, rb=rb, name="norm1_bwd")

    dmod = jnp.concatenate([d_shift1, d_scale1, d_gate1, d_shift2, d_scale2, d_gate2], axis=1)
    small = dict(dmod=dmod, n1w=d_n1w, gb=d_gb, gnw=d_gnw, n2w=d_n2w, cb=d_cb, fnw=d_fnw, w2=d_w2, cw=d_cw)
    return loss, grad_x, half123 + half45, oth123 + oth45, small, ts0


def _win_pieces():
    runs = [(P_GV, 1024, 2048), (P_MA, 5392, 2048), (P_GQ, 0, 1024), (P_AQ, 3088, 2304), (P_LR, 3072, GLA_LR)]
    out = []
    for kc, rc, ln in runs:
        while ln > 0:
            step = min(ln, W_IN_SH - rc % W_IN_SH)
            out.append((kc, rc, step))
            kc, rc, ln = kc + step, rc + step, ln - step
    return out


def _win_assemble(shards):
    rb = 256

    def body(s_ref, o_ref):
        o_ref[:, W_IN:] = jnp.zeros((rb, P_W - W_IN), o_ref.dtype)
        for kc, rc, ln in _win_pieces():
            o_ref[:, kc:kc + ln] = s_ref[rc // W_IN_SH, :, rc % W_IN_SH:rc % W_IN_SH + ln]

    return pl.pallas_call(
        body, name="w_in_assemble", grid=(D // rb,),
        in_specs=[pl.BlockSpec((4, rb, W_IN_SH), lambda i: (0, i, 0))], out_specs=pl.BlockSpec((rb, P_W), lambda i: (i, 0)),
        out_shape=jax.ShapeDtypeStruct((D, P_W), shards.dtype), compiler_params=_params(("parallel",)),
    )(shards)


def _win_split(g):
    rb = 256

    def body(g_ref, o_ref):
        for kc, rc, ln in _win_pieces():
            o_ref[rc // W_IN_SH, :, rc % W_IN_SH:rc % W_IN_SH + ln] = g_ref[:, kc:kc + ln]

    return pl.pallas_call(
        body, name="w_in_grad_split", grid=(D // rb,),
        in_specs=[pl.BlockSpec((rb, P_W), lambda i: (i, 0))], out_specs=pl.BlockSpec((4, rb, W_IN_SH), lambda i: (0, i, 0)),
        out_shape=jax.ShapeDtypeStruct((4, D, W_IN_SH), g.dtype), compiler_params=_params(("parallel",)),
    )(g)


def _ff_to_kernel(a):
    h = W_UP_SH
    return jnp.concatenate([a[:, 0:h], a[:, D_FF:D_FF + h], a[:, h:D_FF], a[:, D_FF + h:]], axis=1)


def _ff_from_kernel(a):
    h = W_UP_SH
    return jnp.concatenate([a[:, 0:h], a[:, 2 * h:3 * h], a[:, h:2 * h], a[:, 3 * h:]], axis=1)


BIG = ("w_in", "w_gla_branch", "w_attn_branch", "w_out", "w_up", "w_down")
SH_SHAPES = ((1024, W_IN_SH), (256, 1024), (256, 256), (256, 1024), (1024, W_UP_SH), (W_DOWN_SH, 1024))
N_BIG = len(BIG)


def _cols_join(t):
    return jnp.concatenate([t[k] for k in range(4)], axis=1)


def _cols_split(t):
    cols = t.shape[1] // 4
    return jnp.stack([t[:, k * cols:(k + 1) * cols] for k in range(4)])


def _me():
    return lax.axis_index("x"), lax.axis_index("y"), lax.axis_index("c")


HBM = pl.BlockSpec(memory_space=pltpu.HBM)
VMEM_SPEC = pl.BlockSpec(memory_space=pltpu.VMEM)


def _allgather8(xs, name):
    rows = xs.shape[0]

    def body(x_ref, out_ref, send_sems, recv_sems, local_sem):
        x, y, c = _me()
        me = 4 * x + 2 * y + c
        mine = pltpu.make_async_copy(x_ref, out_ref.at[me], local_sem)
        mine.start()
        flips = [(k >> 2 & 1, k >> 1 & 1, k & 1) for k in range(1, 8)]

        def peer(f):
            return (jnp.where(f[0] == 1, 1 - x, x), jnp.where(f[1] == 1, 1 - y, y), jnp.where(f[2] == 1, 1 - c, c))

        sends = []
        for k, f in enumerate(flips):
            cp = pltpu.make_async_remote_copy(src_ref=x_ref, dst_ref=out_ref.at[me], send_sem=send_sems.at[k],
                                              recv_sem=recv_sems.at[k], device_id=peer(f), device_id_type=MESH)
            cp.start()
            sends.append(cp)
        for k, f in enumerate(flips):
            px, py, pc = peer(f)
            pltpu.make_async_remote_copy(src_ref=x_ref, dst_ref=out_ref.at[4 * px + 2 * py + pc], send_sem=send_sems.at[k],
                                         recv_sem=recv_sems.at[k], device_id=peer(f), device_id_type=MESH).wait_recv()
        for cp in sends:
            cp.wait_send()
        mine.wait()

    return pl.pallas_call(
        body, name=name, out_shape=jax.ShapeDtypeStruct((8, rows, 128), F32),
        in_specs=[VMEM_SPEC], out_specs=VMEM_SPEC,
        scratch_shapes=[pltpu.SemaphoreType.DMA((7,)), pltpu.SemaphoreType.DMA((7,)), pltpu.SemaphoreType.DMA],
        compiler_params=pltpu.CompilerParams(vmem_limit_bytes=VMEM_LIMIT),
    )(xs)


def _half_rows(i, cc, unit):
    rows = SH_SHAPES[i][0] // 2
    return pl.ds(pl.multiple_of(cc * rows, unit), rows)


def _rc(src, dst, sems, to):
    return pltpu.make_async_remote_copy(src_ref=src, dst_ref=dst, send_sem=sems[0], recv_sem=sems[1], device_id=to, device_id_type=MESH)


def _other_chips(x, y):
    return [(1 - x, y), (x, 1 - y), (1 - x, 1 - y)]


def _u_gather_ici(w_sh, idxs):
    def copies(ins, outs, sem):
        x, y, c = _me()
        res = []
        for j, (px, py) in enumerate(_other_chips(x, y)):
            for n, i in enumerate(idxs):
                src = ins[n].at[0, _half_rows(i, c, 16)]
                res.append((_rc(src, outs[n].at[2 * x + y, _half_rows(i, c, 16)], sem(j * len(idxs) + n), (px, py, c)),
                            _rc(src, outs[n].at[2 * px + py, _half_rows(i, c, 16)], sem(j * len(idxs) + n), (px, py, c))))
        return res

    return dict(ins=[w_sh[i] for i in idxs], outs=[jax.ShapeDtypeStruct((4,) + SH_SHAPES[i], BF16) for i in idxs],
                nsem=3 * len(idxs), alias={}, copies=copies)


def _u_gather_d2d(got, idxs):
    def copies(ins, outs, sem):
        x, y, c = _me()
        res = []
        for j, (px, py) in enumerate(_other_chips(x, y)):
            for n, i in enumerate(idxs):
                src = ins[n].at[2 * px + py, _half_rows(i, c, 16)]
                res.append((_rc(src, outs[n].at[2 * px + py, _half_rows(i, c, 16)], sem(j * len(idxs) + n), (x, y, 1 - c)),
                            _rc(src, outs[n].at[2 * px + py, _half_rows(i, 1 - c, 16)], sem(j * len(idxs) + n), (x, y, 1 - c))))
        return res

    return dict(ins=list(got), outs=[jax.ShapeDtypeStruct(g.shape, g.dtype) for g in got], nsem=3 * len(idxs),
                alias={n: n for n in range(len(idxs))}, copies=copies)


def _u_pair_send(gs, idxs):
    def copies(ins, outs, sem):
        x, y, c = _me()
        res = []
        for n, i in enumerate(idxs):
            for sh in range(4):
                cp = _rc(ins[n].at[sh, _half_rows(i, 1 - c, 8)], outs[n].at[sh], sem(4 * n + sh), (x, y, 1 - c))
                res.append((cp, cp))
        return res

    return dict(ins=list(gs), outs=[jax.ShapeDtypeStruct((4, SH_SHAPES[i][0] // 2, SH_SHAPES[i][1]), F32) for i in idxs],
                nsem=4 * len(idxs), alias={}, copies=copies)


def _u_chip_exchange(ts):
    def copies(ins, outs, sem):
        x, y, c = _me()
        res = []
        for j, (px, py) in enumerate(_other_chips(x, y)):
            for n in range(len(ts)):
                cp = _rc(ins[n].at[2 * px + py], outs[n].at[j], sem(j * len(ts) + n), (px, py, c))
                res.append((cp, cp))
        return res

    return dict(ins=list(ts), outs=[jax.ShapeDtypeStruct((3,) + t.shape[1:], t.dtype) for t in ts], nsem=3 * len(ts),
                alias={}, copies=copies)


def _u_pair_join(hs):
    def copies(ins, outs, sem):
        x, y, c = _me()
        res = []
        for n in range(len(hs)):
            cp = _rc(ins[n], outs[n], sem(n), (x, y, 1 - c))
            res.append((cp, cp))
        return res

    return dict(ins=list(hs), outs=[jax.ShapeDtypeStruct(h.shape, h.dtype) for h in hs], nsem=len(hs), alias={}, copies=copies)


def _comm_phase(units, ci, co, send_sems, recv_sems, start):
    ii = oo = off = 0
    for u in units:
        ni, no = len(u["ins"]), len(u["outs"])
        for st, arrival in u["copies"](ci[ii:ii + ni], co[oo:oo + no], lambda k, off=off: (send_sems.at[off + k], recv_sems.at[off + k])):
            if start:
                st.start()
            else:
                st.wait_send()
                arrival.wait_recv()
        ii, oo, off = ii + ni, oo + no, off + u["nsem"]


def _carry(units, n_in, n_out):
    ins = [a for u in units for a in u["ins"]]
    outs = [o for u in units for o in u["outs"]]
    alias, ii, oo = {}, 0, 0
    for u in units:
        for a, b in u["alias"].items():
            alias[n_in + ii + a] = n_out + oo + b
        ii, oo = ii + len(u["ins"]), oo + len(u["outs"])
    nsem = sum(u["nsem"] for u in units)
    scratch = [pltpu.SemaphoreType.DMA((nsem,)), pltpu.SemaphoreType.DMA((nsem,))] if units else []
    return ins, outs, alias, scratch


def _split_units(units, res):
    out, oo = [], 0
    for u in units:
        out.append(list(res[oo:oo + len(u["outs"])]))
        oo += len(u["outs"])
    return out


def _comm_call(name, units):
    ins, outs, alias, scratch = _carry(units, 0, 0)

    def body(*refs):
        ci, co = refs[:len(ins)], refs[len(ins):len(ins) + len(outs)]
        _comm_phase(units, ci, co, refs[-2], refs[-1], True)
        _comm_phase(units, ci, co, refs[-2], refs[-1], False)

    res = pl.pallas_call(body, name=name, out_shape=outs, in_specs=[HBM] * len(ins), out_specs=[HBM] * len(outs),
                         scratch_shapes=scratch, input_output_aliases=alias)(*ins)
    return _split_units(units, res)


SEM = pl.BlockSpec(memory_space=pltpu.SEMAPHORE)
EFFECT = pltpu.SideEffectType.DATAFLOW_SIDE_EFFECTING


def _unit_start(unit, name, after=()):
    bufs = list(unit["ins"]) + [lax.empty(o.shape, o.dtype) for o in unit["outs"]]
    n_i, n_b, ns = len(unit["ins"]), len(bufs), unit["nsem"]

    def body(*refs):
        send_sems, recv_sems = refs[n_b + len(after)], refs[n_b + len(after) + 1]
        for st, _ in unit["copies"](refs[:n_i], refs[n_i:n_b], lambda k: (send_sems.at[k], recv_sems.at[k])):
            st.start()
        refs[-1][...] = jnp.zeros_like(refs[-1])

    res = pl.pallas_call(
        body, name=name,
        out_shape=[pltpu.SemaphoreType.DMA((ns,)), pltpu.SemaphoreType.DMA((ns,))] + [pltpu.HBM(b.shape, b.dtype) for b in bufs]
        + [jax.ShapeDtypeStruct((8, 128), F32)],
        in_specs=[HBM] * n_b + [pl.BlockSpec(memory_space=pl.ANY)] * len(after), out_specs=[SEM, SEM] + [HBM] * n_b + [VMEM_SPEC],
        input_output_aliases={i: 2 + i for i in range(n_b)},
        compiler_params=pltpu.CompilerParams(has_side_effects=EFFECT),
    )(*[pltpu.with_memory_space_constraint(b, pltpu.HBM) for b in bufs], *after)
    return res[0], res[1], list(res[2:2 + n_b]), res[-1]


def _unit_wait(unit, send_sems, recv_sems, bufs, after, name):
    n_i, n_b = len(unit["ins"]), len(bufs)

    def body(*refs):
        ss, rs = refs[n_b], refs[n_b + 1]
        for st, arrival in unit["copies"](refs[:n_i], refs[n_i:n_b], lambda k: (ss.at[k], rs.at[k])):
            st.wait_send()
            arrival.wait_recv()

    res = pl.pallas_call(
        body, name=name, out_shape=[pltpu.HBM(b.shape, b.dtype) for b in bufs],
        in_specs=[HBM] * n_b + [SEM, SEM] + [pl.BlockSpec(memory_space=pl.ANY)] * len(after), out_specs=[HBM] * n_b,
        input_output_aliases={i: i for i in range(n_b)}, compiler_params=pltpu.CompilerParams(has_side_effects=EFFECT),
    )(*bufs, send_sems, recv_sems, *after)
    return list(res[:n_i]), list(res[n_i:])


def _pair_add(g, land, core, name):
    _, rows, cols = g.shape
    half = rows // 2
    rb = _tile(half, 256, 16)
    nb = half // rb

    def body(c_ref, g_ref, l_ref, o_ref):
        o_ref[...] = (g_ref[...] + l_ref[...]).astype(BF16)

    return pl.pallas_call(
        body, name=name,
        grid_spec=pltpu.PrefetchScalarGridSpec(
            num_scalar_prefetch=1, grid=(4, nb),
            in_specs=[pl.BlockSpec((1, rb, cols), lambda s, i, c_ref: (s, c_ref[0] * nb + i, 0)),
                      pl.BlockSpec((1, rb, cols), lambda s, i, c_ref: (s, i, 0))],
            out_specs=pl.BlockSpec((1, rb, cols), lambda s, i, c_ref: (s, i, 0))),
        out_shape=jax.ShapeDtypeStruct((4, half, cols), BF16),
        compiler_params=_params(("parallel", "parallel")),
    )(core, g, land)


def _chip_sum(t, r, chip, name):
    _, half, cols = t.shape
    rb = _tile(half, 256, 16)

    def body(s_ref, t_ref, r_ref, o_ref):
        o_ref[...] = ((t_ref[0].astype(F32) + r_ref[0].astype(F32)) + r_ref[1].astype(F32)) + r_ref[2].astype(F32)

    return pl.pallas_call(
        body, name=name,
        grid_spec=pltpu.PrefetchScalarGridSpec(
            num_scalar_prefetch=1, grid=(half // rb,),
            in_specs=[pl.BlockSpec((1, rb, cols), lambda i, s_ref: (s_ref[0], i, 0)),
                      pl.BlockSpec((3, rb, cols), lambda i, s_ref: (0, i, 0))],
            out_specs=pl.BlockSpec((rb, cols), lambda i, s_ref: (i, 0))),
        out_shape=jax.ShapeDtypeStruct((half, cols), F32),
        compiler_params=_params(("parallel",)),
    )(chip, t, r)


def _adam_math(wv, gv, mv, vv):
    mn = ADAM_B1 * mv + (1.0 - ADAM_B1) * gv
    vn = ADAM_B2 * vv + (1.0 - ADAM_B2) * (gv * gv)
    m_hat = mn / (1.0 - ADAM_B1 ** ADAM_STEP)
    v_hat = vn / (1.0 - ADAM_B2 ** ADAM_STEP)
    return -ADAM_LR * (m_hat / (jnp.sqrt(v_hat) + ADAM_EPS) + ADAM_WD * wv), mn, vn


def _adamw_halves(wt, mt, vt, mine, theirs, core, name):
    _, rows, cols = wt.shape
    half = rows // 2
    rb = _tile(half, 256, 8)
    nb = half // rb

    def body(c_ref, w_ref, m_ref, v_ref, a_ref, b_ref, g_ref, d_ref, mo_ref, vo_ref):
        gv = jnp.where(pl.program_id(0) == c_ref[0], a_ref[...], b_ref[...])
        dl, mn, vn = _adam_math(w_ref[...], gv, m_ref[...], v_ref[...])
        g_ref[...] = gv
        d_ref[...] = dl
        mo_ref[...] = mn
        vo_ref[...] = vn

    full = pl.BlockSpec((None, rb, cols), lambda hf, i, c_ref: (0, hf * nb + i, 0))
    part = pl.BlockSpec((rb, cols), lambda hf, i, c_ref: (i, 0))
    return pl.pallas_call(
        body, name=name,
        grid_spec=pltpu.PrefetchScalarGridSpec(num_scalar_prefetch=1, grid=(2, nb), in_specs=[full, full, full, part, part],
                                               out_specs=[full] * 4),
        out_shape=[jax.ShapeDtypeStruct((1, rows, cols), F32)] * 4,
        compiler_params=_params(("parallel", "parallel")),
    )(core, wt, mt, vt, mine, theirs)


SG_REP = 144
SG_LOSS = 136
SG_W2, SG_CW = SG_REP, SG_REP + 4 * 16
SG_ROWS = SG_CW + 4 * 40
SP_ROWS = SG_REP + 16 + 40


def _mod_shard(c_all, ada_w_sh):
    def body(c_ref, w_ref, o_ref):
        cv = c_ref[...]
        o_ref[...] = _dg((cv * _sigmoid(cv)).astype(BF16), w_ref[...].astype(BF16), 1, 0)

    return pl.pallas_call(body, name="mod_shard", out_shape=jax.ShapeDtypeStruct((8, 1536), F32),
                          in_specs=[VMEM_SPEC, VMEM_SPEC], out_specs=VMEM_SPEC,
                          compiler_params=pltpu.CompilerParams(vmem_limit_bytes=VMEM_LIMIT))(c_all, ada_w_sh)


def _mod_select(mod_all, ada_b4):
    def body(m_ref, b_ref, o_ref):
        x, y, c = _me()
        me = 4 * x + 2 * y + c
        for sh in range(4):
            o_ref[sh] = m_ref[2 * sh, me] + b_ref[sh]

    return pl.pallas_call(body, name="mod_select", out_shape=jax.ShapeDtypeStruct((4, 12, 128), F32),
                          in_specs=[VMEM_SPEC, VMEM_SPEC], out_specs=VMEM_SPEC)(mod_all, ada_b4)


def _small_reduce(sg_all):
    def body(g_ref, o_ref):
        x, y, c = _me()
        s_me = 2 * x + y
        w2_rows = pl.ds(pl.multiple_of(SG_W2 + 16 * s_me, 8), 16)
        cw_rows = pl.ds(pl.multiple_of(SG_CW + 40 * s_me, 8), 40)
        a = g_ref[0, 0:SG_REP, :]
        b = g_ref[0, w2_rows, :]
        d = g_ref[0, cw_rows, :]
        for dev in range(1, 8):
            a = a + g_ref[dev, 0:SG_REP, :]
            b = b + g_ref[dev, w2_rows, :]
            d = d + g_ref[dev, cw_rows, :]
        o_ref[0:SG_REP, :] = a
        o_ref[SG_REP:SG_REP + 16, :] = b
        o_ref[SG_REP + 16:SP_ROWS, :] = d

    return pl.pallas_call(body, name="small_grad_reduce", out_shape=jax.ShapeDtypeStruct((SP_ROWS, 128), F32),
                          in_specs=[VMEM_SPEC], out_specs=VMEM_SPEC)(sg_all)


def _ada_grad(dmod_all, c_bc):
    def body(g_ref, c_ref, o_ref):
        x, y, c = _me()
        s_me = 2 * x + y
        for k in range(12):
            acc = jnp.zeros((D, 128), F32)
            for b in range(8):
                cv = c_ref[b]
                acc = acc + (cv * _sigmoid(cv)) * g_ref[s_me, k, b:b + 1, :]
            o_ref[:, k * 128:(k + 1) * 128] = acc

    return pl.pallas_call(body, name="ada_w_grad", out_shape=jax.ShapeDtypeStruct((D, 1536), F32),
                          in_specs=[VMEM_SPEC, VMEM_SPEC], out_specs=VMEM_SPEC,
                          compiler_params=pltpu.CompilerParams(vmem_limit_bytes=VMEM_LIMIT))(dmod_all, c_bc)


def _adamw(wt, g, m, v, name):
    rows, cols = wt.shape
    rb = _tile(rows, 256, 8)

    def fn(c, i, wv, gv, mv, vv):
        return _adam_math(wv, gv, mv, vv)

    return _rowcall(fn, [_rows(t, rb) for t in (wt, g, m, v)], [_orow(rows, cols, F32, rb)] * 3,
                    n_rows=rows, rb=rb, name=name)


def _pad_rows(t, rows):
    flat = t.reshape(-1)
    return jnp.pad(flat, (0, rows * 128 - flat.shape[0])).reshape(rows, 128)


SP_LAYOUT = (("ada_b", 48), ("norm1_w", 8), ("gla_gate_b", 8), ("gla_norm_w", 8), ("norm2_w", 8), ("conv_b", 48),
             ("final_norm_w", 8), (None, 8), ("gla_gate_w2", 16), ("conv_w", 40))


def _pack_small(d):
    return jnp.concatenate([jnp.zeros((rows, 128), F32) if n is None else _pad_rows(d[n].astype(F32), rows)
                            for n, rows in SP_LAYOUT], axis=0)


def _unpack_small(pk, shapes):
    out, off = {}, 0
    for n, rows in SP_LAYOUT:
        if n is not None:
            shp = shapes[n]
            out[n] = pk[off:off + rows].reshape(-1)[:math.prod(shp)].reshape(shp)
        off += rows
    return out


def kernel(x, c, positions, ada_w, ada_b, norm1_w, w_in, gla_gate_w2, gla_gate_b, gla_norm_w, w_gla_branch, w_attn_branch, w_out, norm2_w, w_up, conv_w, conv_b, w_down, final_norm_w, loss_target, m_ada_w, m_ada_b, m_norm1_w, m_w_in, m_gla_gate_w2, m_gla_gate_b, m_gla_norm_w, m_w_gla_branch, m_w_attn_branch, m_w_out, m_norm2_w, m_w_up, m_conv_w, m_conv_b, m_w_down, m_final_norm_w, v_ada_w, v_ada_b, v_norm1_w, v_w_in, v_gla_gate_w2, v_gla_gate_b, v_gla_norm_w, v_w_gla_branch, v_w_attn_branch, v_w_out, v_norm2_w, v_w_up, v_conv_w, v_conv_b, v_w_down, v_final_norm_w):
    s = x.shape[1]
    names = ("ada_w", "ada_b", "norm1_w", "w_in", "gla_gate_w2", "gla_gate_b", "gla_norm_w", "w_gla_branch", "w_attn_branch",
             "w_out", "norm2_w", "w_up", "conv_w", "conv_b", "w_down", "final_norm_w")
    wts = dict(zip(names, (ada_w, ada_b, norm1_w, w_in, gla_gate_w2, gla_gate_b, gla_norm_w, w_gla_branch, w_attn_branch,
                           w_out, norm2_w, w_up, conv_w, conv_b, w_down, final_norm_w)))
    ms = dict(zip(names, (m_ada_w, m_ada_b, m_norm1_w, m_w_in, m_gla_gate_w2, m_gla_gate_b, m_gla_norm_w, m_w_gla_branch,
                          m_w_attn_branch, m_w_out, m_norm2_w, m_w_up, m_conv_w, m_conv_b, m_w_down, m_final_norm_w)))
    vs = dict(zip(names, (v_ada_w, v_ada_b, v_norm1_w, v_w_in, v_gla_gate_w2, v_gla_gate_b, v_gla_norm_w, v_w_gla_branch,
                          v_w_attn_branch, v_w_out, v_norm2_w, v_w_up, v_conv_w, v_conv_b, v_w_down, v_final_norm_w)))

    pk0 = jnp.concatenate([_pad_rows(c, 8), _pad_rows(gla_gate_w2, 16), _pad_rows(conv_w, 40)], axis=0)
    sm_all = _allgather8(pk0, "gather_small")
    c_all = sm_all[:, 0:8, :].reshape(8, D)
    w2_full = sm_all[0::2, 8:24, :].transpose(1, 0, 2).reshape(GLA_LR, 512)
    cw_full = sm_all[0::2, 24:64, :].reshape(4, 40 * 128)[:, :3 * W_UP_SH].reshape(4, 3, W_UP_SH).transpose(1, 0, 2).reshape(3, 2 * D_FF)

    mod_sh = _mod_shard(c_all, ada_w[0])
    mod_all = _allgather8(mod_sh.reshape(96, 128), "gather_mod")

    w_sh = [wts[n].astype(BF16) for n in BIG]
    u_g0 = _u_gather_ici(w_sh, (0,))
    g0 = (u_g0,) + _unit_start(u_g0, "gather_w_in_start", after=[mod_all])
    mod = _mod_select(mod_all.reshape(8, 8, 12, 128) + g0[4][0, 0], ada_b.reshape(4, 12, 128)).reshape(6, D)

    core = lax.axis_index("c").astype(jnp.int32).reshape(1)
    chip = (2 * lax.axis_index("x") + lax.axis_index("y")).astype(jnp.int32)
    sm = dict(n1w=norm1_w, n2w=norm2_w, fnw=final_norm_w.reshape(1, D), gnw=gla_norm_w, gb=gla_gate_b,
              w2=jnp.pad(w2_full, ((0, 128 - GLA_LR), (0, 0))), cw=_ff_to_kernel(cw_full), cb=_ff_to_kernel(conv_b))
    loss, grad_x, halves, others, small, ts0 = _local_step(x[0], mod, positions.reshape(s, 1), loss_target[0], sm, w_sh,
                                                               g0, chip, core)

    dcw = _ff_from_kernel(small["cw"]).reshape(3, 4, W_UP_SH).transpose(1, 0, 2)
    dw2 = small["w2"][:GLA_LR].reshape(GLA_LR, 4, 128).transpose(1, 0, 2)
    sg = jnp.concatenate(
        [_pad_rows(small["dmod"], 48), _pad_rows(small["n1w"], 8), _pad_rows(small["gb"], 8), _pad_rows(small["gnw"], 8),
         _pad_rows(small["n2w"], 8), _pad_rows(_ff_from_kernel(small["cb"]), 48), _pad_rows(small["fnw"], 8), _pad_rows(loss, 8)]
        + [_pad_rows(dw2[k], 16) for k in range(4)] + [_pad_rows(dcw[k], 40) for k in range(4)], axis=0)
    sg_all = _allgather8(sg, "gather_small_grads")
    u_ex = _u_chip_exchange([ts0])
    pending = (u_ex,) + _unit_start(u_ex, "grad_exchange_w_in_start", after=[sg_all])
    sg_all = sg_all + pending[4][0, 0]
    g_small_pk = _small_reduce(sg_all)
    dmod_all = sg_all[:, 0:48, :].reshape(8, 4, 12, 128).transpose(1, 2, 0, 3)
    g_ada_w = _ada_grad(dmod_all, jnp.broadcast_to(c_all[:, :, None], (8, D, 128)))

    shapes = {n: wts[n].shape for n in names}
    g_small = _unpack_small(g_small_pk, shapes)
    grads = {"ada_w": g_ada_w.reshape(1, D, 1536), **g_small}
    deltas, new_m, new_v = {}, {}, {}
    for n, mine, theirs in zip(BIG[1:], halves, others):
        grads[n], deltas[n], new_m[n], new_v[n] = _adamw_halves(wts[n], ms[n], vs[n], mine, theirs, core, "adamw_" + n)
    shp = ada_w.shape
    d_, m_, v_ = _adamw(ada_w[0], g_ada_w, m_ada_w[0], v_ada_w[0], "adamw_ada_w")
    deltas["ada_w"], new_m["ada_w"], new_v["ada_w"] = d_.reshape(shp), m_.reshape(shp), v_.reshape(shp)
    d_, m_, v_ = _adamw(_pack_small(wts), g_small_pk, _pack_small(ms), _pack_small(vs), "adamw_small")
    for dst, pk in ((deltas, d_), (new_m, m_), (new_v, v_)):
        dst.update(_unpack_small(pk, shapes))

    [t0], [r0] = _unit_wait(*pending[:4], after=[d_, deltas["ada_w"], deltas["w_up"], deltas["w_down"]], name="grad_exchange_w_in_wait")
    half0 = _chip_sum(t0, r0, chip.reshape(1), "grad_chip_sum_w_in")
    [[oth0]] = _comm_call("grad_join_w_in", [_u_pair_join([half0])])
    grads["w_in"], deltas["w_in"], new_m["w_in"], new_v["w_in"] = _adamw_halves(w_in, m_w_in, v_w_in, half0, oth0, core, "adamw_w_in")

    return (g_small_pk[SG_LOSS, 0], grad_x.reshape(1, s, D), *[grads[n] for n in names], *[deltas[n] for n in names],
            *[new_m[n] for n in names], *[new_v[n] for n in names])
```

```python
import math

import jax
import jax.numpy as jnp
from jax import lax
from jax.experimental import pallas as pl
from jax.experimental.pallas import tpu as pltpu

F32, BF16 = jnp.float32, jnp.bfloat16
MESH = pl.DeviceIdType.MESH

D = 1024
EPS = 1e-6
GLA_H, GLA_DK, GLA_DV, GLA_LR = 4, 128, 256, 16
GLA_TAU = 16.0
GLA_CHUNK = 64
GLA_BLOCK = 512
ATT_GROUPS = ((128, 1), (512, 4), (2048, 16))
ATT_BLK = 128
ATT_HD = 64
ATT_W = 768
D_FF = 2816
ROPE_THETA = 10000.0
P_W = 7680
P_GV, P_GR, P_MA, P_MB, P_GQ, P_GK, P_AQ, P_AK, P_AV, P_LR = 0, 1024, 2048, 3072, 4096, 4608, 5120, 5888, 6656, 7424
W_IN = 7440
W_IN_SH, W_UP_SH, W_DOWN_SH = 1860, 1408, 704
VMEM_LIMIT = 56 * 1024 * 1024
ADAM_LR, ADAM_B1, ADAM_B2, ADAM_EPS, ADAM_WD, ADAM_STEP = 0.001, 0.9, 0.999, 1e-08, 0.01, 10
NEG = -1e30


def _tile(n, target, unit=128):
    best = None
    for t in range(unit, min(n, target) + 1, unit):
        if n % t == 0:
            best = t
    return best or n


def _params(sem):
    return pltpu.CompilerParams(dimension_semantics=sem, vmem_limit_bytes=VMEM_LIMIT)


def _dg(a, b, ca, cb):
    return lax.dot_general(a, b, (((ca,), (cb,)), ((), ())), preferred_element_type=F32)


def _sigmoid(v):
    return 1.0 / (1.0 + jnp.exp(-v))


def _ff_block(j):
    return (j % 2) * 2 + j // 2


def _mm(a, b, name, *, ta=False, tb=False, out_dtype=BF16, tm=1024, tn=1536, tk=1024, n_outer=True, comm=(),
        b_shards=False, o_shards=False):
    m = a.shape[1] if ta else a.shape[0]
    k = a.shape[0] if ta else a.shape[1]
    if b_shards:
        n = b.shape[1] if tb else 4 * W_UP_SH
        tn, tk = (tn, W_UP_SH) if tb else (W_UP_SH, tk)
    else:
        n = b.shape[0] if tb else b.shape[1]
    if o_shards:
        tn = W_UP_SH
    tm, tn, tk = _tile(m, tm), _tile(n, tn), _tile(k, tk)
    nm, nn, nk = m // tm, n // tn, k // tk
    in_out = out_dtype == F32
    c_ins, c_outs, c_alias, c_scratch = _carry(comm, 2, 1)

    def body(a_ref, b_ref, *rest):
        ci, o_ref, co = rest[:len(c_ins)], rest[len(c_ins)], rest[len(c_ins) + 1:len(c_ins) + 1 + len(c_outs)]
        scr = rest[len(c_ins) + 1 + len(c_outs):]
        kk = pl.program_id(2)
        if comm:
            step = (pl.program_id(0) * (nm if n_outer else nn) + pl.program_id(1)) * nk + kk

            @pl.when(step == 0)
            def _():
                _comm_phase(comm, ci, co, scr[-2], scr[-1], True)

        _mm_step(a_ref, b_ref, o_ref, scr, kk)
        if comm:
            @pl.when(step == nm * nn * nk - 1)
            def _():
                _comm_phase(comm, ci, co, scr[-2], scr[-1], False)

    def _mm_step(a_ref, b_ref, o_ref, scr, kk):
        p = _dg(a_ref[...].astype(BF16), b_ref[...].astype(BF16), 0 if ta else 1, 1 if tb else 0)
        if nk == 1:
            o_ref[...] = p.astype(o_ref.dtype)
        else:
            acc = o_ref if in_out else scr[0]

            @pl.when(kk == 0)
            def _():
                acc[...] = p

            @pl.when(kk > 0)
            def _():
                acc[...] += p

            if not in_out:
                @pl.when(kk == nk - 1)
                def _():
                    o_ref[...] = acc[...].astype(o_ref.dtype)

    if n_outer:
        ij = lambda g0, g1: (g1, g0)
        grid = (nn, nm, nk)
    else:
        ij = lambda g0, g1: (g0, g1)
        grid = (nm, nn, nk)
    a_map = (lambda g0, g1, kk: (kk, ij(g0, g1)[0])) if ta else (lambda g0, g1, kk: (ij(g0, g1)[0], kk))
    if b_shards and tb:
        b_spec = pl.BlockSpec((None, tn, tk), lambda g0, g1, kk: (_ff_block(kk), ij(g0, g1)[1], 0))
    elif b_shards:
        b_spec = pl.BlockSpec((None, tk, tn), lambda g0, g1, kk: (_ff_block(ij(g0, g1)[1]), kk, 0))
    elif tb:
        b_spec = pl.BlockSpec((tn, tk), lambda g0, g1, kk: (ij(g0, g1)[1], kk))
    else:
        b_spec = pl.BlockSpec((tk, tn), lambda g0, g1, kk: (kk, ij(g0, g1)[1]))
    if o_shards:
        o_spec = pl.BlockSpec((None, tm, tn), lambda g0, g1, kk: (_ff_block(ij(g0, g1)[1]), ij(g0, g1)[0], 0))
        o_shape = jax.ShapeDtypeStruct((4, m, W_UP_SH), out_dtype)
    else:
        o_spec = pl.BlockSpec((tm, tn), lambda g0, g1, kk: ij(g0, g1))
        o_shape = jax.ShapeDtypeStruct((m, n), out_dtype)
    res = pl.pallas_call(
        body, name=name, grid=grid,
        in_specs=[pl.BlockSpec((tk, tm) if ta else (tm, tk), a_map), b_spec] + [HBM] * len(c_ins),
        out_specs=[o_spec] + [HBM] * len(c_outs),
        out_shape=[o_shape] + c_outs,
        scratch_shapes=([] if (in_out or nk == 1) else [pltpu.VMEM((tm, tn), F32)]) + c_scratch,
        input_output_aliases=c_alias,
        compiler_params=_params(("arbitrary",) * 3 if comm else ("parallel", "parallel", "arbitrary")),
    )(a, b, *c_ins)
    return (res[0], _split_units(comm, res[1:])) if comm else res[0]


def _rows(arr, rb, w=None, j=0):
    w = arr.shape[1] if w is None else w
    if callable(j):
        return arr, pl.BlockSpec((rb, w), lambda c, i: (i, j(c)))
    return arr, pl.BlockSpec((rb, w), lambda c, i: (i, j))


def _full(arr, w=None, j=0):
    w = arr.shape[1] if w is None else w
    if callable(j):
        return arr, pl.BlockSpec((arr.shape[0], w), lambda c, i: (0, j(c)))
    return arr, pl.BlockSpec((arr.shape[0], w), lambda c, i: (0, j))


def _halo(arr, rb, hb, w, j, before):
    per = rb // hb
    last = arr.shape[0] // hb - 1
    if before:
        rmap = lambda i: jnp.maximum(i * per - 1, 0)
    else:
        rmap = lambda i: jnp.minimum((i + 1) * per, last)
    return arr, pl.BlockSpec((hb, w), lambda c, i: (rmap(i), j(c) if callable(j) else j))


def _rowcall(fn, ins, outs, *, n_rows, rb, name, ncol=1, into=None, after=()):
    n_in = len(ins)
    nr = n_rows // rb
    unread = ([] if into is None else [into[0]]) + list(after)
    n_skip = len(unread)

    def body(*refs):
        c, i = pl.program_id(0), pl.program_id(1)
        res = fn(c, i, *[r[...] for r in refs[:n_in]])
        for val, spec, o_ref in zip(res, outs, refs[n_in + n_skip:]):
            if spec[2] == "row":
                o_ref[...] = val.astype(o_ref.dtype)
            else:
                @pl.when(i == 0)
                def _(o_ref=o_ref, val=val):
                    o_ref[...] = val.astype(o_ref.dtype)

                @pl.when(i > 0)
                def _(o_ref=o_ref, val=val):
                    o_ref[...] += val.astype(o_ref.dtype)

    out_specs = []
    for shape, dt, kind, block, col in outs:
        if kind == "row":
            out_specs.append(pl.BlockSpec(block, lambda c, i, col=col: (i, col(c))))
        else:
            out_specs.append(pl.BlockSpec(block, lambda c, i, col=col: (0, col(c))))
    return pl.pallas_call(
        body, name=name, grid=(ncol, nr),
        in_specs=[s for _, s in ins] + [pl.BlockSpec(memory_space=pl.ANY)] * n_skip, out_specs=out_specs,
        out_shape=[jax.ShapeDtypeStruct(o[0], o[1]) for o in outs],
        input_output_aliases={} if into is None else {n_in: into[1]},
        compiler_params=_params(("parallel", "arbitrary")),
    )(*[a for a, _ in ins], *unread)


def _orow(n_rows, w, dt, rb, bw=None, col=lambda c: 0):
    return ((n_rows, w), dt, "row", (rb, bw or w), col)


def _oacc(r, w, bw=None, col=lambda c: 0):
    return ((r, w), F32, "acc", (r, bw or w), col)


def _csum(v):
    return jnp.sum(v, axis=0, keepdims=True)


def _rms(v):
    return lax.rsqrt(jnp.mean(v * v, axis=-1, keepdims=True) + EPS)


def _norm_bwd(xv, dh, w, scale):
    r = _rms(xv)
    xh = xv * r
    dxh = dh * (w * (1.0 + scale))
    dx = r * (dxh - xh * jnp.mean(dxh * xh, axis=-1, keepdims=True))
    t = dh * xh
    return dx, _csum(dh), _csum(t * w), _csum(t * (1.0 + scale))


def _rope_tables(pos_col, invf, s):
    def fn(c, i, pos, f):
        ang = pos.astype(F32) * f
        lane = lax.broadcasted_iota(jnp.int32, ang.shape, 1)
        sign = jnp.where((lane % ATT_HD) < ATT_HD // 2, -1.0, 1.0)
        return jnp.cos(ang), jnp.sin(ang) * sign

    rb = 512
    return _rowcall(fn, [_rows(pos_col, rb), _full(invf)], [_orow(s, 128, F32, rb), _orow(s, 128, F32, rb)],
                    n_rows=s, rb=rb, name="rope_tables")


def _swap_halves(t):
    n = t.shape[1]
    lane = lax.broadcasted_iota(jnp.int32, t.shape, 1)
    return jnp.where((lane % ATT_HD) < ATT_HD // 2, pltpu.roll(t, n - 32, 1), pltpu.roll(t, 32, 1))


def _rope_apply(t, cos, sin_signed, inverse):
    cw = jnp.concatenate([cos] * (t.shape[1] // 128), axis=1)
    sw = jnp.concatenate([sin_signed] * (t.shape[1] // 128), axis=1)
    if inverse:
        sw = -sw
    return t * cw + _swap_halves(t) * sw


DIL_ROWS = 512


def _to_dilated(scr, val, out_ref, r):
    if r == 1:
        out_ref[...] = val.astype(out_ref.dtype)
        return
    n = val.shape[0] // r
    for hh in range(2):
        scr[hh] = val[:, hh * 128:(hh + 1) * 128]
        for pr in range(r):
            out_ref[:, pr * 256 + hh * 128:pr * 256 + (hh + 1) * 128] = scr[hh, pl.ds(pr, n, stride=r), :].astype(out_ref.dtype)


def _from_dilated(scr, in_ref, r):
    if r == 1:
        return in_ref[...].astype(F32)
    n = in_ref.shape[0]
    for hh in range(2):
        for pr in range(r):
            scr[hh, pl.ds(pr, n, stride=r), :] = in_ref[:, pr * 256 + hh * 128:pr * 256 + (hh + 1) * 128].astype(F32)
    return jnp.concatenate([scr[0], scr[1]], axis=1)


def _dil_spec(r):
    return pl.BlockSpec((DIL_ROWS // r, r * 256), lambda i: (i, 0))


def _dil_shape(s, r, dt):
    return jax.ShapeDtypeStruct((s // r, r * 256), dt)


_DIL_SCRATCH = [pltpu.VMEM((2, DIL_ROWS, 128), F32)]
_RS = tuple(r for _, r in ATT_GROUPS)


def _rope_fwd(p, cos_t, sin_t, s):
    def body(*refs):
        ins, cs, sn, outs, scr = refs[:9], refs[9][...], refs[10][...], refs[11:20], refs[20]
        for t in range(3):
            for g, r in enumerate(_RS):
                val = ins[3 * t + g][...].astype(F32)
                _to_dilated(scr, _rope_apply(val, cs, sn, False) if t < 2 else val, outs[3 * t + g], r)

    res = pl.pallas_call(
        body, name="rope", grid=(s // DIL_ROWS,),
        in_specs=[pl.BlockSpec((DIL_ROWS, 256), lambda i, c=base // 256 + g: (i, c)) for base in (P_AQ, P_AK, P_AV) for g in range(3)]
        + [pl.BlockSpec((DIL_ROWS, 128), lambda i: (i, 0))] * 2,
        out_specs=[_dil_spec(r) for _ in range(3) for r in _RS],
        out_shape=[_dil_shape(s, r, BF16) for _ in range(3) for r in _RS],
        scratch_shapes=_DIL_SCRATCH, compiler_params=_params(("parallel",)),
    )(*([p] * 9), cos_t, sin_t)
    return res[0:3], res[3:6], res[6:9]


def _attn_combine(att, s):
    def body(o0, o1, o2, l0, l1, l2, o_ref, lse_ref, od1, od2, ld1, ld2, scr):
        ov = [_from_dilated(scr, ref, r) for ref, r in zip((o0, o1, o2), _RS)]
        lv = [_from_dilated(scr, ref, r) for ref, r in zip((l0, l1, l2), _RS)]
        mx = jnp.maximum(jnp.maximum(lv[0], lv[1]), lv[2])
        ev = [jnp.exp(l - mx) for l in lv]
        z = ev[0] + ev[1] + ev[2]
        o = ((ev[0] * ov[0] + ev[1] * ov[1] + ev[2] * ov[2]) / z).astype(BF16)
        lse = mx + jnp.log(z)
        o_ref[...] = o
        lse_ref[...] = lse
        for ref, r in zip((od1, od2), _RS[1:]):
            _to_dilated(scr, o.astype(F32), ref, r)
        for ref, r in zip((ld1, ld2), _RS[1:]):
            _to_dilated(scr, lse, ref, r)

    return pl.pallas_call(
        body, name="attn_combine", grid=(s // DIL_ROWS,),
        in_specs=[_dil_spec(r) for r in _RS] * 2,
        out_specs=[_dil_spec(1)] * 2 + [_dil_spec(r) for r in _RS[1:]] * 2,
        out_shape=[_dil_shape(s, 1, BF16), _dil_shape(s, 1, F32)] + [_dil_shape(s, r, BF16) for r in _RS[1:]]
        + [_dil_shape(s, r, F32) for r in _RS[1:]],
        scratch_shapes=_DIL_SCRATCH, compiler_params=_params(("parallel",)),
    )(*[a[0] for a in att], *[a[1] for a in att])


def _dilate(t, s):
    def body(t_ref, o1, o2, scr):
        val = t_ref[...].astype(F32)
        for ref, r in zip((o1, o2), _RS[1:]):
            _to_dilated(scr, val, ref, r)

    return pl.pallas_call(
        body, name="attn_dilate", grid=(s // DIL_ROWS,), in_specs=[_dil_spec(1)], out_specs=[_dil_spec(r) for r in _RS[1:]],
        out_shape=[_dil_shape(s, r, t.dtype) for r in _RS[1:]], scratch_shapes=_DIL_SCRATCH, compiler_params=_params(("parallel",)),
    )(t)


def _rope_bwd(datt, d_glr, dp, cos_t, sin_t, s):
    tail = P_W - P_AQ

    def body(*refs):
        ins, cs, sn, glr, o_ref, scr = refs[:9], refs[9][...], refs[10][...], refs[11], refs[13], refs[14]
        for t in range(3):
            for g, r in enumerate(_RS):
                val = _from_dilated(scr, ins[3 * t + g], r)
                o_ref[:, t * ATT_W + g * 256:t * ATT_W + (g + 1) * 256] = (_rope_apply(val, cs, sn, True) if t < 2 else val).astype(BF16)
        o_ref[:, 3 * ATT_W:3 * ATT_W + 128] = glr[...]
        o_ref[:, 3 * ATT_W + 128:] = jnp.zeros((DIL_ROWS, tail - 3 * ATT_W - 128), BF16)

    return pl.pallas_call(
        body, name="rope_bwd", grid=(s // DIL_ROWS,),
        in_specs=[_dil_spec(r) for _ in range(3) for r in _RS] + [pl.BlockSpec((DIL_ROWS, 128), lambda i: (i, 0))] * 3
        + [pl.BlockSpec(memory_space=pl.ANY)],
        out_specs=pl.BlockSpec((DIL_ROWS, tail), lambda i: (i, P_AQ // tail)),
        out_shape=jax.ShapeDtypeStruct((s, P_W), BF16), input_output_aliases={12: 0},
        scratch_shapes=_DIL_SCRATCH, compiler_params=_params(("parallel",)),
    )(*[datt[g][t] for t in range(3) for g in range(3)], cos_t, sin_t, d_glr, dp)


def _tri_dot(tri, t):
    tb = tri.astype(BF16)
    hi = t.astype(BF16)
    r1 = t - hi.astype(F32)
    mid = r1.astype(BF16)
    lo = (r1 - mid.astype(F32)).astype(BF16)
    return _dg(tb, hi, 1, 0) + _dg(tb, mid, 1, 0) + _dg(tb, lo, 1, 0)


def _gla_decays(la_c, tri):
    b = _tri_dot(tri, la_c)
    row = lax.broadcasted_iota(jnp.int32, b.shape, 0)
    bmid = jnp.sum(jnp.where(row == GLA_CHUNK // 2 - 1, b, 0.0), axis=0, keepdims=True)
    blast = jnp.sum(jnp.where(row == GLA_CHUNK - 1, b, 0.0), axis=0, keepdims=True)
    return b, bmid, blast


def _gla_fwd(p, la, s, comm=()):
    tb, ch = GLA_BLOCK, GLA_CHUNK
    nb, nc = s // tb, tb // ch
    scale = GLA_DK ** -0.5
    c_ins, c_outs, c_alias, c_scratch = _carry(comm, 4, 2)

    def body(q_ref, k_ref, v_ref, la_ref, *rest):
        ci, (o_ref, st_ref) = rest[:len(c_ins)], rest[len(c_ins):len(c_ins) + 2]
        co, state = rest[len(c_ins) + 2:len(c_ins) + 2 + len(c_outs)], rest[len(c_ins) + 2 + len(c_outs)]
        step = pl.program_id(0)
        if comm:
            @pl.when(step == 0)
            def _():
                _comm_phase(comm, ci, co, rest[-2], rest[-1], True)

        _gla_fwd_step(q_ref, k_ref, v_ref, la_ref, o_ref, st_ref, state)
        if comm:
            @pl.when(step == nb - 1)
            def _():
                _comm_phase(comm, ci, co, rest[-2], rest[-1], False)

    def _gla_fwd_step(q_ref, k_ref, v_ref, la_ref, o_ref, st_ref, state):
        @pl.when(pl.program_id(0) == 0)
        def _():
            state[...] = jnp.zeros_like(state)

        ri = lax.broadcasted_iota(jnp.int32, (ch, ch), 0)
        ci = lax.broadcasted_iota(jnp.int32, (ch, ch), 1)
        causal = ci <= ri
        tri = causal.astype(F32)

        def chunk(c, carry):
            sl = pl.ds(pl.multiple_of(c * ch, ch), ch)
            b, bmid, blast = _gla_decays(la_ref[sl, :], tri)
            q = q_ref[sl, :].astype(F32) * scale
            k = k_ref[sl, :].astype(F32)
            v = v_ref[sl, :]
            qgt = (q * jnp.exp(b)).astype(BF16)
            qgn = (q * jnp.exp(b - bmid)).astype(BF16)
            kgn = (k * jnp.exp(bmid - b)).astype(BF16)
            kd = (k * jnp.exp(blast - b)).astype(BF16)
            dec = jnp.exp(blast)
            sts = [state[h] for h in range(GLA_H)]
            outs, news = [], []
            for h in range(GLA_H):
                hk, hv = slice(h * GLA_DK, (h + 1) * GLA_DK), slice(h * GLA_DV, (h + 1) * GLA_DV)
                a = jnp.where(causal, _dg(qgn[:, hk], kgn[:, hk], 1, 1), 0.0)
                outs.append(_dg(a.astype(BF16), v[:, hv], 1, 0) + _dg(qgt[:, hk], sts[h].astype(BF16), 1, 1))
                news.append(dec[:, hk] * sts[h] + _dg(v[:, hv], kd[:, hk], 0, 0))
            for h in range(GLA_H):
                st_ref[h, c] = sts[h]
                state[h] = news[h]
            o_ref[sl, :] = jnp.concatenate(outs, axis=1)
            return carry

        lax.fori_loop(0, nc, chunk, 0)

    hw = GLA_H * GLA_DK
    res = pl.pallas_call(
        body, name="gla_fwd", grid=(nb,),
        in_specs=[pl.BlockSpec((tb, hw), lambda t: (t, P_GQ // hw)),
                  pl.BlockSpec((tb, hw), lambda t: (t, P_GK // hw)),
                  pl.BlockSpec((tb, GLA_H * GLA_DV), lambda t: (t, P_GV // (GLA_H * GLA_DV))),
                  pl.BlockSpec((tb, hw), lambda t: (t, 0))] + [HBM] * len(c_ins),
        out_specs=[pl.BlockSpec((tb, GLA_H * GLA_DV), lambda t: (t, 0)),
                   pl.BlockSpec((GLA_H, nc, GLA_DV, GLA_DK), lambda t: (0, t, 0, 0))] + [HBM] * len(c_outs),
        out_shape=[jax.ShapeDtypeStruct((s, GLA_H * GLA_DV), F32),
                   jax.ShapeDtypeStruct((GLA_H, s // ch, GLA_DV, GLA_DK), F32)] + c_outs,
        scratch_shapes=[pltpu.VMEM((GLA_H, GLA_DV, GLA_DK), F32)] + c_scratch,
        input_output_aliases=c_alias,
        compiler_params=_params(("arbitrary",)),
    )(p, p, p, la, *c_ins)
    return res[0], res[1], _split_units(comm, res[2:])


def _gla_bwd(p, la, states, do, s, dp, comm=()):
    tb, ch = GLA_BLOCK, GLA_CHUNK
    nb, nc = s // tb, tb // ch
    scale = GLA_DK ** -0.5
    c_ins, c_outs, c_alias, c_scratch = _carry(comm, 7, 4)

    def body(q_ref, k_ref, v_ref, la_ref, st_ref, do_ref, dp_in, *rest):
        ci, outs = rest[:len(c_ins)], rest[len(c_ins):len(c_ins) + 4]
        co, dstate = rest[len(c_ins) + 4:len(c_ins) + 4 + len(c_outs)], rest[len(c_ins) + 4 + len(c_outs)]
        step = pl.program_id(0)
        if comm:
            @pl.when(step == 0)
            def _():
                _comm_phase(comm, ci, co, rest[-2], rest[-1], True)

        _gla_bwd_step(q_ref, k_ref, v_ref, la_ref, st_ref, do_ref, *outs, dstate)
        if comm:
            @pl.when(step == nb - 1)
            def _():
                _comm_phase(comm, ci, co, rest[-2], rest[-1], False)

    def _gla_bwd_step(q_ref, k_ref, v_ref, la_ref, st_ref, do_ref, dq_ref, dk_ref, dv_ref, dla_ref, dstate):
        @pl.when(pl.program_id(0) == 0)
        def _():
            dstate[...] = jnp.zeros_like(dstate)

        ri = lax.broadcasted_iota(jnp.int32, (ch, ch), 0)
        ci = lax.broadcasted_iota(jnp.int32, (ch, ch), 1)
        causal = ci <= ri
        tri = causal.astype(F32)
        tri_t = (ci >= ri).astype(F32)

        def chunk(cc, carry):
            c = nc - 1 - cc
            sl = pl.ds(pl.multiple_of(c * ch, ch), ch)
            b, bmid, blast = _gla_decays(la_ref[sl, :], tri)
            q = q_ref[sl, :].astype(F32) * scale
            k = k_ref[sl, :].astype(F32)
            v = v_ref[sl, :]
            e_b, e_qn, e_kn, e_kd = jnp.exp(b), jnp.exp(b - bmid), jnp.exp(bmid - b), jnp.exp(blast - b)
            dec = jnp.exp(blast)
            qgt, qgn, kgn, kd = q * e_b, q * e_qn, k * e_kn, k * e_kd
            qgt_b, qgn_b, kgn_b, kd_b = qgt.astype(BF16), qgn.astype(BF16), kgn.astype(BF16), kd.astype(BF16)
            do_b = do_ref[sl, :].astype(BF16)
            st0s = [st_ref[h, c] for h in range(GLA_H)]
            dsts = [dstate[h] for h in range(GLA_H)]
            dqgn, dqgt, dkgn, dkd, dvs, ddec, news = [], [], [], [], [], [], []
            for h in range(GLA_H):
                hk, hv = slice(h * GLA_DK, (h + 1) * GLA_DK), slice(h * GLA_DV, (h + 1) * GLA_DV)
                dst_b = dsts[h].astype(BF16)
                a = jnp.where(causal, _dg(qgn_b[:, hk], kgn_b[:, hk], 1, 1), 0.0).astype(BF16)
                da = jnp.where(causal, _dg(do_b[:, hv], v[:, hv], 1, 1), 0.0).astype(BF16)
                dqgn.append(_dg(da, kgn_b[:, hk], 1, 0))
                dqgt.append(_dg(do_b[:, hv], st0s[h].astype(BF16), 1, 0))
                dkgn.append(_dg(da, qgn_b[:, hk], 0, 0))
                dvs.append(_dg(a, do_b[:, hv], 0, 0) + _dg(kd_b[:, hk], dst_b, 1, 1))
                dkd.append(_dg(v[:, hv], dst_b, 1, 0))
                ddec.append(jnp.sum(st0s[h] * dsts[h], axis=0, keepdims=True))
                news.append(dec[:, hk] * dsts[h] + _dg(do_b[:, hv], qgt_b[:, hk], 0, 0))
            for h in range(GLA_H):
                dstate[h] = news[h]
            cat = lambda parts: jnp.concatenate(parts, axis=1)
            dqgn, dqgt, dkgn, dkd, ddec = cat(dqgn), cat(dqgt), cat(dkgn), cat(dkd), cat(ddec)
            dq_ref[sl, :] = (scale * (dqgn * e_qn + dqgt * e_b)).astype(dq_ref.dtype)
            dk_ref[sl, :] = (dkgn * e_kn + dkd * e_kd).astype(dk_ref.dtype)
            dv_ref[sl, :] = cat(dvs).astype(dv_ref.dtype)
            db = dqgn * qgn + dqgt * qgt - dkgn * kgn - dkd * kd
            extra = jnp.sum(dkd * kd, axis=0, keepdims=True) + ddec * dec
            dla_ref[sl, :] = _tri_dot(tri_t, db) + extra
            return carry

        lax.fori_loop(0, nc, chunk, 0)

    rev = lambda t: nb - 1 - t
    hw, vw = GLA_H * GLA_DK, GLA_H * GLA_DV
    res = pl.pallas_call(
        body, name="gla_bwd", grid=(nb,),
        in_specs=[pl.BlockSpec((tb, hw), lambda t: (rev(t), P_GQ // hw)),
                  pl.BlockSpec((tb, hw), lambda t: (rev(t), P_GK // hw)),
                  pl.BlockSpec((tb, vw), lambda t: (rev(t), P_GV // vw)),
                  pl.BlockSpec((tb, hw), lambda t: (rev(t), 0)),
                  pl.BlockSpec((GLA_H, nc, GLA_DV, GLA_DK), lambda t: (0, rev(t), 0, 0)),
                  pl.BlockSpec((tb, vw), lambda t: (rev(t), 0)), pl.BlockSpec(memory_space=pl.ANY)] + [HBM] * len(c_ins),
        out_specs=[pl.BlockSpec((tb, hw), lambda t: (rev(t), 0)),
                   pl.BlockSpec((tb, hw), lambda t: (rev(t), 0)),
                   pl.BlockSpec((tb, vw), lambda t: (rev(t), P_GV // vw)),
                   pl.BlockSpec((tb, hw), lambda t: (rev(t), 0))] + [HBM] * len(c_outs),
        out_shape=[jax.ShapeDtypeStruct((s, hw), BF16),
                   jax.ShapeDtypeStruct((s, hw), BF16),
                   jax.ShapeDtypeStruct((s, P_W), BF16),
                   jax.ShapeDtypeStruct((s, hw), F32)] + c_outs,
        scratch_shapes=[pltpu.VMEM((GLA_H, GLA_DV, GLA_DK), F32)] + c_scratch,
        input_output_aliases={6: 2, **c_alias},
        compiler_params=_params(("arbitrary",)),
    )(p, p, p, la, states, do, dp, *c_ins)
    return res[0], res[1], res[2], res[3], _split_units(comm, res[4:])


def _head_masks():
    lane = lax.broadcasted_iota(jnp.int32, (1, 4 * ATT_HD), 1)
    return [(lane >= h * ATT_HD) & (lane < (h + 1) * ATT_HD) for h in range(4)]


def _attn_fwd(qv, kv, pv, g, r, s):
    ln = s // r
    nblk = ln // ATT_BLK
    qcol = lambda pr: pr
    vcol = qcol
    prev = lambda n: jnp.maximum(n - 1, 0)

    def body(q_ref, kp_ref, kc_ref, vp_ref, vc_ref, o_ref, lse_ref):
        has_prev = pl.program_id(1) > 0
        ri = lax.broadcasted_iota(jnp.int32, (ATT_BLK, ATT_BLK), 0)
        ci = lax.broadcasted_iota(jnp.int32, (ATT_BLK, ATT_BLK), 1)
        m_cur = ci <= ri
        m_prev = (ci >= ri) & has_prev
        q, kp, kc, vp, vc = q_ref[...], kp_ref[...], kc_ref[...], vp_ref[...], vc_ref[...]
        o = jnp.zeros((ATT_BLK, 256), F32)
        lse = jnp.zeros((ATT_BLK, 256), F32)
        for hm in _head_masks():
            qm = jnp.where(hm, q, jnp.zeros_like(q))
            sc = jnp.where(m_cur, _dg(qm, kc, 1, 1) * 0.125, NEG)
            sp = jnp.where(m_prev, _dg(qm, kp, 1, 1) * 0.125, NEG)
            mx = jnp.maximum(jnp.max(sc, axis=1, keepdims=True), jnp.max(sp, axis=1, keepdims=True))
            pc, pp = jnp.exp(sc - mx), jnp.exp(sp - mx)
            den = jnp.sum(pc, axis=1, keepdims=True) + jnp.sum(pp, axis=1, keepdims=True)
            oh = (_dg(pc.astype(BF16), vc, 1, 0) + _dg(pp.astype(BF16), vp, 1, 0)) / den
            o = jnp.where(hm, oh, o)
            lse = jnp.where(hm, mx + jnp.log(den), lse)
        o_ref[...] = o.astype(o_ref.dtype)
        lse_ref[...] = lse

    blk = (ATT_BLK, 256)
    o, lse = pl.pallas_call(
        body, name=f"attn_fwd_{g}", grid=(r, nblk),
        in_specs=[pl.BlockSpec(blk, lambda pr, n: (n, qcol(pr))),
                  pl.BlockSpec(blk, lambda pr, n: (prev(n), qcol(pr))),
                  pl.BlockSpec(blk, lambda pr, n: (n, qcol(pr))),
                  pl.BlockSpec(blk, lambda pr, n: (prev(n), vcol(pr))),
                  pl.BlockSpec(blk, lambda pr, n: (n, vcol(pr)))],
        out_specs=[pl.BlockSpec(blk, lambda pr, n: (n, pr)), pl.BlockSpec(blk, lambda pr, n: (n, pr))],
        out_shape=[jax.ShapeDtypeStruct((ln, r * 256), BF16), jax.ShapeDtypeStruct((ln, r * 256), F32)],
        compiler_params=_params(("parallel", "parallel")),
    )(qv, kv, kv, pv, pv)
    return o, lse


def _attn_bwd(qv, kv, pv, dov, ov, lv, g, r, s):
    ln = s // r
    nblk = ln // ATT_BLK
    qcol = lambda pr: pr
    vcol = qcol
    prev = lambda n: jnp.maximum(n - 1, 0)
    nxt = lambda n: jnp.minimum(n + 1, nblk - 1)

    def body(qc_ref, qn_ref, kp_ref, kc_ref, vp_ref, vc_ref, doc_ref, don_ref, oc_ref, on_ref, lc_ref, ln_ref,
             dq_ref, dk_ref, dv_ref):
        n = pl.program_id(1)
        has_prev, has_next = n > 0, n < nblk - 1
        ri = lax.broadcasted_iota(jnp.int32, (ATT_BLK, ATT_BLK), 0)
        ci = lax.broadcasted_iota(jnp.int32, (ATT_BLK, ATT_BLK), 1)
        m_cur = ci <= ri
        m_prev = (ci >= ri) & has_prev
        m_next = (ci >= ri) & has_next
        qc, qn, kp, kc, vp, vc = qc_ref[...], qn_ref[...], kp_ref[...], kc_ref[...], vp_ref[...], vc_ref[...]
        doc, don = doc_ref[...], don_ref[...]
        pc_full = doc.astype(F32) * oc_ref[...].astype(F32)
        pn_full = don.astype(F32) * on_ref[...].astype(F32)
        lc, lnx = lc_ref[...], ln_ref[...]
        dq = jnp.zeros((ATT_BLK, 256), F32)
        dk = jnp.zeros((ATT_BLK, 256), F32)
        dv = jnp.zeros((ATT_BLK, 256), F32)
        zb = jnp.zeros_like(qc)
        for hm in _head_masks():
            qcm, qnm = jnp.where(hm, qc, zb), jnp.where(hm, qn, zb)
            docm, donm = jnp.where(hm, doc, zb), jnp.where(hm, don, zb)
            lse_c = jnp.max(jnp.where(hm, lc, NEG), axis=1, keepdims=True)
            lse_n = jnp.max(jnp.where(hm, lnx, NEG), axis=1, keepdims=True)
            del_c = jnp.sum(jnp.where(hm, pc_full, 0.0), axis=1, keepdims=True)
            del_n = jnp.sum(jnp.where(hm, pn_full, 0.0), axis=1, keepdims=True)
            pr_ = jnp.where(m_cur, jnp.exp(_dg(qcm, kc, 1, 1) * 0.125 - lse_c), 0.0)
            ds = (pr_ * (_dg(docm, vc, 1, 1) - del_c) * 0.125).astype(BF16)
            dqh = _dg(ds, kc, 1, 0)
            dkh = _dg(ds, qc, 0, 0)
            dvh = _dg(pr_.astype(BF16), doc, 0, 0)
            pr_ = jnp.where(m_prev, jnp.exp(_dg(qcm, kp, 1, 1) * 0.125 - lse_c), 0.0)
            ds = (pr_ * (_dg(docm, vp, 1, 1) - del_c) * 0.125).astype(BF16)
            dqh = dqh + _dg(ds, kp, 1, 0)
            pr_ = jnp.where(m_next, jnp.exp(_dg(qnm, kc, 1, 1) * 0.125 - lse_n), 0.0)
            ds = (pr_ * (_dg(donm, vc, 1, 1) - del_n) * 0.125).astype(BF16)
            dkh = dkh + _dg(ds, qn, 0, 0)
            dvh = dvh + _dg(pr_.astype(BF16), don, 0, 0)
            dq = jnp.where(hm, dqh, dq)
            dk = jnp.where(hm, dkh, dk)
            dv = jnp.where(hm, dvh, dv)
        dq_ref[...] = dq.astype(dq_ref.dtype)
        dk_ref[...] = dk.astype(dk_ref.dtype)
        dv_ref[...] = dv.astype(dv_ref.dtype)

    blk = (ATT_BLK, 256)
    cur = lambda col: pl.BlockSpec(blk, lambda pr, n: (n, col(pr)))
    prv = lambda col: pl.BlockSpec(blk, lambda pr, n: (prev(n), col(pr)))
    nx = lambda col: pl.BlockSpec(blk, lambda pr, n: (nxt(n), col(pr)))
    own = lambda pr: pr
    outs = pl.pallas_call(
        body, name=f"attn_bwd_{g}", grid=(r, nblk),
        in_specs=[cur(qcol), nx(qcol), prv(qcol), cur(qcol), prv(vcol), cur(vcol),
                  cur(own), nx(own), cur(own), nx(own), cur(own), nx(own)],
        out_specs=[cur(own), cur(own), cur(own)],
        out_shape=[jax.ShapeDtypeStruct((ln, r * 256), BF16)] * 3,
        compiler_params=_params(("parallel", "parallel")),
    )(qv, qv, kv, kv, pv, pv, dov, dov, ov, ov, lv, lv)
    return outs


def _gelu_parts(gv):
    cdf = 0.5 * (1.0 + lax.erf(gv * (2.0 ** -0.5)))
    pdf = jnp.exp(-0.5 * gv * gv) * (1.0 / math.sqrt(2.0 * math.pi))
    return cdf, pdf


def _pick_row(t, k):
    row = lax.broadcasted_iota(jnp.int32, t.shape, 0)
    return jnp.sum(jnp.where(row == k, t, 0.0), axis=0, keepdims=True)


def _shift_rows(u, halo, n):
    row = lax.broadcasted_iota(jnp.int32, u.shape, 0)
    out = pltpu.roll(u, n, 0)
    for k in range(n):
        out = jnp.where(row == k, _pick_row(halo, 16 - n + k), out)
    return out


def _shift_rows_up(u, halo, n):
    rb = u.shape[0]
    row = lax.broadcasted_iota(jnp.int32, u.shape, 0)
    out = pltpu.roll(u, rb - n, 0)
    for k in range(n):
        out = jnp.where(row == rb - n + k, _pick_row(halo, k), out)
    return out


def _conv(u, halo, cw, cb):
    return cb + _pick_row(cw, 0) * _shift_rows(u, halo, 2) + _pick_row(cw, 1) * _shift_rows(u, halo, 1) + _pick_row(cw, 2) * u


def _local_step(x, mod, pos_col, target, sm, w_sh, g0, chip, core):
    s = x.shape[0]
    shift1, scale1, gate1, shift2, scale2, gate2 = [mod[i:i + 1, :] for i in range(6)]
    rb = 256
    chip1 = chip.reshape(1)

    def f_norm1(c, i, xv, nw, sc, sh):
        return ((xv * _rms(xv) * nw) * (1.0 + sc) + sh,)

    (h,) = _rowcall(f_norm1, [_rows(x, rb), _full(sm["n1w"]), _full(scale1), _full(shift1)],
                    [_orow(s, D, BF16, rb)], n_rows=s, rb=rb, name="norm1")
    own = lambda got, i: lax.dynamic_update_slice(got, w_sh[i], (chip, 0, 0))
    invf = jnp.tile(ROPE_THETA ** (-jnp.arange(ATT_HD // 2, dtype=F32) / (ATT_HD // 2)), 4).reshape(1, 128)
    cos_t, sin_t = _rope_tables(pos_col, invf, s)
    _, got0 = _unit_wait(*g0[:4], after=[h, cos_t, sin_t], name="gather_w_in_wait")
    [got0] = _comm_call("gather_w_in_d2d", [_u_gather_d2d(got0, (0,))])
    u_g1 = _u_gather_ici(w_sh, (1, 2, 3, 4, 5))
    g1 = _unit_start(u_g1, "gather_weights_start", after=got0)
    w = dict(win=_win_assemble(own(got0[0], 0), after=g1[3:]))
    p = _mm(h, w["win"], "in_proj", tm=1024, tn=1536)

    def f_gla_pre(c, i, glr, w2, gb):
        z = _dg(glr, w2.astype(BF16), 1, 0) + gb
        return ((jnp.minimum(z, 0.0) - jnp.log(1.0 + jnp.exp(-jnp.abs(z)))) * (1.0 / GLA_TAU),)

    (la,) = _rowcall(f_gla_pre, [_rows(p, rb, 128, P_LR // 128), _full(sm["w2"]), _full(sm["gb"])],
                     [_orow(s, 512, F32, rb)], n_rows=s, rb=rb, name="gla_pre")
    o_gla, states, _ = _gla_fwd(p, la, s)
    _, got = _unit_wait(u_g1, *g1[:3], after=[o_gla], name="gather_weights_wait")
    [got123] = _comm_call("gather_weights_d2d", [_u_gather_d2d(got[:3], (1, 2, 3))])
    got45 = got[3:]
    w.update(wgb=own(got123[0], 1).reshape(1024, D), wab=_cols_join(own(got123[1], 2)), wout=own(got123[2], 3).reshape(D, D))

    def f_gla_post(c, i, ov, gnw, gr):
        on = jnp.concatenate([ov[:, k * 256:(k + 1) * 256] * _rms(ov[:, k * 256:(k + 1) * 256]) * gnw
                              for k in range(GLA_H)], axis=1)
        g = gr.astype(F32)
        return (on * (g * _sigmoid(g)),)

    (og,) = _rowcall(f_gla_post, [_rows(o_gla, rb), _full(sm["gnw"]), _rows(p, rb, 1024, P_GR // 1024)],
                     [_orow(s, 1024, BF16, rb)], n_rows=s, rb=rb, name="gla_post")
    y_gla = _mm(og, w["wgb"], "gla_branch")

    q_d, k_d, v_d = _rope_fwd(p, cos_t, sin_t, s)
    att = [_attn_fwd(q_d[g], k_d[g], v_d[g], g, r, s) for g, r in enumerate(_RS)]
    o_att, lse, o_d1, o_d2, lse_d1, lse_d2 = _attn_combine(att, s)
    y_att = _mm(o_att, w["wab"], "attn_branch")

    def f_merge(c, i, ma, mb, yg, ya):
        return (_sigmoid(ma.astype(F32)) * yg.astype(F32) + _sigmoid(mb.astype(F32)) * ya.astype(F32),)

    (mixed,) = _rowcall(f_merge, [_rows(p, rb, D, P_MA // D), _rows(p, rb, D, P_MB // D), _rows(y_gla, rb), _rows(y_att, rb)],
                        [_orow(s, D, BF16, rb)], n_rows=s, rb=rb, name="merge")
    z1, [got45] = _mm(mixed, w["wout"], "out_proj", comm=[_u_gather_d2d(got45, (4, 5))])
    w.update(wup=own(got45[0], 4), wdown=own(got45[1], 5).reshape(D_FF, D))

    def f_norm2(c, i, xv, z, g1, nw, sc, sh):
        x1 = xv + g1 * z.astype(F32)
        return (x1, (x1 * _rms(x1) * nw) * (1.0 + sc) + sh)

    x1, h2 = _rowcall(f_norm2, [_rows(x, rb), _rows(z1, rb), _full(gate1), _full(sm["n2w"]), _full(scale2), _full(shift2)],
                      [_orow(s, D, F32, rb), _orow(s, D, BF16, rb)], n_rows=s, rb=rb, name="norm2")
    u = _mm(h2, w["wup"], "up_proj", b_shards=True)

    cwid = 2 * W_UP_SH

    def f_ffn(c, i, uv, hl, cw, cb):
        uc = _conv(uv.astype(F32), hl.astype(F32) * (i > 0).astype(F32), cw, cb)
        val, gt = uc[:, :W_UP_SH], uc[:, W_UP_SH:]
        cdf, _ = _gelu_parts(gt)
        return (gt * cdf * val,)

    ccol = lambda c: c
    (hidden,) = _rowcall(f_ffn, [_rows(u, rb, cwid, ccol), _halo(u, rb, 16, cwid, ccol, True),
                                 _full(sm["cw"], cwid, ccol), _full(sm["cb"], cwid, ccol)],
                         [_orow(s, D_FF, BF16, rb, W_UP_SH, ccol)], n_rows=s, rb=rb, name="conv_geglu", ncol=2)
    z2 = _mm(hidden, w["wdown"], "down_proj", tk=D_FF)

    def f_final(c, i, x1v, z, g2, fw, tgt):
        x2 = x1v + g2 * z.astype(F32)
        r = _rms(x2)
        xh = x2 * r
        e = xh * fw - tgt
        loss = 0.5 * jnp.sum(jnp.mean(e * e, axis=-1, keepdims=True), axis=0, keepdims=True)
        dy = e * (1.0 / D)
        dxh = dy * fw
        dx2 = r * (dxh - xh * jnp.mean(dxh * xh, axis=-1, keepdims=True))
        return (loss, dx2, dx2 * g2, _csum(dy * xh), _csum(dx2 * z.astype(F32)))

    loss, dx2, dz2, d_fnw, d_gate2 = _rowcall(
        f_final, [_rows(x1, rb), _rows(z2, rb), _full(gate2), _full(sm["fnw"]), _rows(target, rb)],
        [_oacc(1, 1), _orow(s, D, F32, rb), _orow(s, D, BF16, rb), _oacc(1, D), _oacc(1, D)],
        n_rows=s, rb=rb, name="final_loss")
    d_hidden = _mm(dz2, w["wdown"], "down_proj_dx", tb=True, tn=1408)
    g_wdown = _mm(hidden, dz2, "down_proj_dw", ta=True, out_dtype=F32, tm=1408, tn=1024, tk=2048)

    def f_ffn_bwd(c, i, uv, hl, dh, cw, cb):
        uf = uv.astype(F32)
        hf = hl.astype(F32) * (i > 0).astype(F32)
        u1, u2 = _shift_rows(uf, hf, 1), _shift_rows(uf, hf, 2)
        uc = cb + _pick_row(cw, 0) * u2 + _pick_row(cw, 1) * u1 + _pick_row(cw, 2) * uf
        val, gt = uc[:, :W_UP_SH], uc[:, W_UP_SH:]
        cdf, pdf = _gelu_parts(gt)
        dhf = dh.astype(F32)
        duc = jnp.concatenate([dhf * (gt * cdf), dhf * val * (cdf + gt * pdf)], axis=1)
        dcw = jnp.concatenate([_csum(duc * u2), _csum(duc * u1), _csum(duc * uf)], axis=0)
        return (duc, _csum(duc), dcw)

    duc, d_cb, d_cw = _rowcall(
        f_ffn_bwd, [_rows(u, rb, cwid, ccol), _halo(u, rb, 16, cwid, ccol, True), _rows(d_hidden, rb, W_UP_SH, ccol),
                    _full(sm["cw"], cwid, ccol), _full(sm["cb"], cwid, ccol)],
        [_orow(s, 2 * D_FF, BF16, rb, cwid, ccol), _oacc(1, 2 * D_FF, cwid, ccol), _oacc(3, 2 * D_FF, cwid, ccol)],
        n_rows=s, rb=rb, name="conv_geglu_bwd", ncol=2)

    def f_conv_t(c, i, dv, hl, cw):
        df = dv.astype(F32)
        hf = hl.astype(F32) * (i < s // rb - 1).astype(F32)
        return (_pick_row(cw, 2) * df + _pick_row(cw, 1) * _shift_rows_up(df, hf, 1) + _pick_row(cw, 0) * _shift_rows_up(df, hf, 2),)

    (du,) = _rowcall(f_conv_t, [_rows(duc, rb, cwid, ccol), _halo(duc, rb, 16, cwid, ccol, False), _full(sm["cw"], cwid, ccol)],
                     [_orow(s, 2 * D_FF, BF16, rb, cwid, ccol)], n_rows=s, rb=rb, name="conv_transpose", ncol=2)
    g_wup = _mm(h2, du, "up_proj_dw", ta=True, out_dtype=F32, tm=1024, tk=2048, o_shards=True)
    gs45 = [g_wup, g_wdown.reshape(4, W_DOWN_SH, 1024)]
    d_h2, [land45] = _mm(du, w["wup"], "up_proj_dx", tb=True, b_shards=True, comm=[_u_pair_send(gs45, (4, 5))])
    ts45 = [_pair_add(g, ld, core, "grad_pair_add_" + BIG[i]) for g, ld, i in zip(gs45, land45, (4, 5))]

    def f_norm2_bwd(c, i, x1v, dh, dxr, z, nw, sc, g1):
        dxn, dsh, dsc, dnw = _norm_bwd(x1v, dh.astype(F32), nw, sc)
        dx1 = dxr + dxn
        return (dx1, dx1 * g1, dsh, dsc, dnw, _csum(dx1 * z.astype(F32)))

    dx1, dz1, d_shift2, d_scale2, d_n2w, d_gate1 = _rowcall(
        f_norm2_bwd, [_rows(x1, rb), _rows(d_h2, rb), _rows(dx2, rb), _rows(z1, rb), _full(sm["n2w"]), _full(scale2), _full(gate1)],
        [_orow(s, D, F32, rb), _orow(s, D, BF16, rb), _oacc(1, D), _oacc(1, D), _oacc(1, D), _oacc(1, D)],
        n_rows=s, rb=rb, name="norm2_bwd")
    d_mixed = _mm(dz1, w["wout"], "out_proj_dx", tb=True)
    g_wout = _mm(mixed, dz1, "out_proj_dw", ta=True, out_dtype=F32, tk=2048)

    def f_merge_bwd(c, i, dm, ma, mb, yg, ya):
        dmf, ygf, yaf = dm.astype(F32), yg.astype(F32), ya.astype(F32)
        sa, sb = _sigmoid(ma.astype(F32)), _sigmoid(mb.astype(F32))
        return (dmf * sa, dmf * sb, jnp.concatenate([dmf * ygf * sa * (1.0 - sa), dmf * yaf * sb * (1.0 - sb)], axis=1))

    dy_gla, dy_att, dp = _rowcall(
        f_merge_bwd, [_rows(d_mixed, rb), _rows(p, rb, D, P_MA // D), _rows(p, rb, D, P_MB // D), _rows(y_gla, rb), _rows(y_att, rb)],
        [_orow(s, D, BF16, rb)] * 2 + [_orow(s, P_W, BF16, rb, 2 * D, lambda c: P_MA // (2 * D))], n_rows=s, rb=rb, name="merge_bwd")
    d_og = _mm(dy_gla, w["wgb"], "gla_branch_dx", tb=True)
    g_wgb = _mm(og, dy_gla, "gla_branch_dw", ta=True, out_dtype=F32, tk=2048)
    d_oatt = _mm(dy_att, w["wab"], "attn_branch_dx", tb=True)
    g_wab = _mm(o_att, dy_att, "attn_branch_dw", ta=True, out_dtype=F32, tk=2048)

    def f_gla_post_bwd(c, i, ov, gnw, gr, dog):
        g = gr.astype(F32)
        sg = _sigmoid(g)
        silu = g * sg
        dof = dog.astype(F32)
        don = dof * silu
        on_parts, do_parts, dgn = [], [], jnp.zeros((1, 256), F32)
        for k in range(GLA_H):
            oh = ov[:, k * 256:(k + 1) * 256]
            dh = don[:, k * 256:(k + 1) * 256]
            r = _rms(oh)
            xh = oh * r
            dgn = dgn + _csum(dh * xh)
            dxh = dh * gnw
            do_parts.append(r * (dxh - xh * jnp.mean(dxh * xh, axis=-1, keepdims=True)))
            on_parts.append(xh * gnw)
        on = jnp.concatenate(on_parts, axis=1)
        dgr = dof * on * (sg * (1.0 + g * (1.0 - sg)))
        return (jnp.concatenate(do_parts, axis=1), dgr, dgn)

    do_gla, dp, d_gnw = _rowcall(
        f_gla_post_bwd, [_rows(o_gla, rb), _full(sm["gnw"]), _rows(p, rb, 1024, P_GR // 1024), _rows(d_og, rb)],
        [_orow(s, 1024, F32, rb), _orow(s, P_W, BF16, rb, 1024, lambda c: P_GR // 1024), _oacc(1, 256)],
        n_rows=s, rb=rb, name="gla_post_bwd", into=(dp, 1))
    gs123 = [g_wgb.reshape(4, 256, 1024), _cols_split(g_wab), g_wout.reshape(4, 256, 1024)]
    d_gq, d_gk, dp, d_la, [r4, land123] = _gla_bwd(p, la, states, do_gla, s, dp,
                                                   comm=[_u_chip_exchange(ts45[:1]), _u_pair_send(gs123, (1, 2, 3))])
    half4 = [_chip_sum(ts45[0], r4[0], chip1, "grad_chip_sum_w_up")]
    ts123 = [_pair_add(g, ld, core, "grad_pair_add_" + BIG[i]) for g, ld, i in zip(gs123, land123, (1, 2, 3))]

    def f_gla_pre_bwd(c, i, lav, dlav, glr, w2):
        dz = dlav * (1.0 / GLA_TAU) * (1.0 - jnp.exp(GLA_TAU * lav))
        dzb = dz.astype(BF16)
        return (_dg(dzb, w2.astype(BF16), 1, 1), _csum(dz), _dg(glr, dzb, 0, 0))

    d_glr, d_gb, d_w2 = _rowcall(
        f_gla_pre_bwd, [_rows(la, rb), _rows(d_la, rb), _rows(p, rb, 128, P_LR // 128), _full(sm["w2"])],
        [_orow(s, 128, BF16, rb), _oacc(1, 512), _oacc(128, 512)], n_rows=s, rb=rb, name="gla_pre_bwd")

    do_d = [d_oatt] + list(_dilate(d_oatt, s))
    datt = [_attn_bwd(q_d[g], k_d[g], v_d[g], do_d[g], (o_att, o_d1, o_d2)[g], (lse, lse_d1, lse_d2)[g], g, r, s)
            for g, r in enumerate(_RS)]
    dp = _rope_bwd(datt, d_glr, dp, cos_t, sin_t, s)
    dp = lax.dynamic_update_slice(dp, jnp.concatenate([d_gq, d_gk], axis=1), (0, P_GQ))
    g_win, [r1235, oth4] = _mm(h, dp, "in_proj_dw", ta=True, out_dtype=F32, tm=1024, tn=1536, tk=2048,
                               comm=[_u_chip_exchange(ts123 + ts45[1:]), _u_pair_join(half4)])
    half1235 = [_chip_sum(t, r, chip1, "grad_chip_sum_" + BIG[i]) for t, r, i in zip(ts123 + ts45[1:], r1235, (1, 2, 3, 5))]
    gs0 = [_win_split(g_win)]
    d_h, [land0, oth1235] = _mm(dp, w["win"], "in_proj_dx", tb=True, tk=3840,
                                comm=[_u_pair_send(gs0, (0,)), _u_pair_join(half1235)])
    half123, half45 = half1235[:3], half4 + half1235[3:]
    oth123, oth45 = oth1235[:3], oth4 + oth1235[3:]
    ts0 = _pair_add(gs0[0], land0[0], core, "grad_pair_add_w_in")

    def f_norm1_bwd(c, i, xv, dh, dxr, nw, sc):
        dxn, dsh, dsc, dnw = _norm_bwd(xv, dh.astype(F32), nw, sc)
        return (dxr + dxn, dsh, dsc, dnw)

    grad_x, d_shift1, d_scale1, d_n1w = _rowcall(
        f_norm1_bwd, [_rows(x, rb), _rows(d_h, rb), _rows(dx1, rb), _full(sm["n1w"]), _full(scale1)],
        [_orow(s, D, F32, rb), _oacc(1, D), _oacc(1, D), _oacc(1, D)], n_rows=s, rb=rb, name="norm1_bwd")

    dmod = jnp.concatenate([d_shift1, d_scale1, d_gate1, d_shift2, d_scale2, d_gate2], axis=1)
    small = dict(dmod=dmod, n1w=d_n1w, gb=d_gb, gnw=d_gnw, n2w=d_n2w, cb=d_cb, fnw=d_fnw, w2=d_w2, cw=d_cw)
    return loss, grad_x, half123 + half45, oth123 + oth45, small, ts0


def _win_pieces():
    runs = [(P_GV, 1024, 2048), (P_MA, 5392, 2048), (P_GQ, 0, 1024), (P_AQ, 3088, 2304), (P_LR, 3072, GLA_LR)]
    out = []
    for kc, rc, ln in runs:
        while ln > 0:
            step = min(ln, W_IN_SH - rc % W_IN_SH)
            out.append((kc, rc, step))
            kc, rc, ln = kc + step, rc + step, ln - step
    return out


def _win_assemble(shards, after=()):
    rb = 256

    def body(s_ref, *rest):
        o_ref = rest[-1]
        o_ref[:, W_IN:] = jnp.zeros((rb, P_W - W_IN), o_ref.dtype)
        for kc, rc, ln in _win_pieces():
            o_ref[:, kc:kc + ln] = s_ref[rc // W_IN_SH, :, rc % W_IN_SH:rc % W_IN_SH + ln]

    return pl.pallas_call(
        body, name="w_in_assemble", grid=(D // rb,),
        in_specs=[pl.BlockSpec((4, rb, W_IN_SH), lambda i: (0, i, 0))] + [pl.BlockSpec(memory_space=pl.ANY)] * len(after),
        out_specs=pl.BlockSpec((rb, P_W), lambda i: (i, 0)),
        out_shape=jax.ShapeDtypeStruct((D, P_W), shards.dtype), compiler_params=_params(("parallel",)),
    )(shards, *after)


def _win_split(g):
    rb = 256

    def body(g_ref, o_ref):
        for kc, rc, ln in _win_pieces():
            o_ref[rc // W_IN_SH, :, rc % W_IN_SH:rc % W_IN_SH + ln] = g_ref[:, kc:kc + ln]

    return pl.pallas_call(
        body, name="w_in_grad_split", grid=(D // rb,),
        in_specs=[pl.BlockSpec((rb, P_W), lambda i: (i, 0))], out_specs=pl.BlockSpec((4, rb, W_IN_SH), lambda i: (0, i, 0)),
        out_shape=jax.ShapeDtypeStruct((4, D, W_IN_SH), g.dtype), compiler_params=_params(("parallel",)),
    )(g)


def _ff_to_kernel(a):
    h = W_UP_SH
    return jnp.concatenate([a[:, 0:h], a[:, D_FF:D_FF + h], a[:, h:D_FF], a[:, D_FF + h:]], axis=1)


def _ff_from_kernel(a):
    h = W_UP_SH
    return jnp.concatenate([a[:, 0:h], a[:, 2 * h:3 * h], a[:, h:2 * h], a[:, 3 * h:]], axis=1)


BIG = ("w_in", "w_gla_branch", "w_attn_branch", "w_out", "w_up", "w_down")
SH_SHAPES = ((1024, W_IN_SH), (256, 1024), (256, 256), (256, 1024), (1024, W_UP_SH), (W_DOWN_SH, 1024))
N_BIG = len(BIG)


def _cols_join(t):
    return jnp.concatenate([t[k] for k in range(4)], axis=1)


def _cols_split(t):
    cols = t.shape[1] // 4
    return jnp.stack([t[:, k * cols:(k + 1) * cols] for k in range(4)])


def _me():
    return lax.axis_index("x"), lax.axis_index("y"), lax.axis_index("c")


HBM = pl.BlockSpec(memory_space=pltpu.HBM)
VMEM_SPEC = pl.BlockSpec(memory_space=pltpu.VMEM)


def _allgather8(xs, name):
    rows = xs.shape[0]

    def body(x_ref, out_ref, send_sems, recv_sems, local_sem):
        x, y, c = _me()
        me = 4 * x + 2 * y + c
        mine = pltpu.make_async_copy(x_ref, out_ref.at[me], local_sem)
        mine.start()
        flips = [(k >> 2 & 1, k >> 1 & 1, k & 1) for k in range(1, 8)]

        def peer(f):
            return (jnp.where(f[0] == 1, 1 - x, x), jnp.where(f[1] == 1, 1 - y, y), jnp.where(f[2] == 1, 1 - c, c))

        sends = []
        for k, f in enumerate(flips):
            cp = pltpu.make_async_remote_copy(src_ref=x_ref, dst_ref=out_ref.at[me], send_sem=send_sems.at[k],
                                              recv_sem=recv_sems.at[k], device_id=peer(f), device_id_type=MESH)
            cp.start()
            sends.append(cp)
        for k, f in enumerate(flips):
            px, py, pc = peer(f)
            pltpu.make_async_remote_copy(src_ref=x_ref, dst_ref=out_ref.at[4 * px + 2 * py + pc], send_sem=send_sems.at[k],
                                         recv_sem=recv_sems.at[k], device_id=peer(f), device_id_type=MESH).wait_recv()
        for cp in sends:
            cp.wait_send()
        mine.wait()

    return pl.pallas_call(
        body, name=name, out_shape=jax.ShapeDtypeStruct((8, rows, 128), F32),
        in_specs=[VMEM_SPEC], out_specs=VMEM_SPEC,
        scratch_shapes=[pltpu.SemaphoreType.DMA((7,)), pltpu.SemaphoreType.DMA((7,)), pltpu.SemaphoreType.DMA],
        compiler_params=pltpu.CompilerParams(vmem_limit_bytes=VMEM_LIMIT),
    )(xs)


def _half_rows(i, cc, unit):
    rows = SH_SHAPES[i][0] // 2
    return pl.ds(pl.multiple_of(cc * rows, unit), rows)


def _rc(src, dst, sems, to):
    return pltpu.make_async_remote_copy(src_ref=src, dst_ref=dst, send_sem=sems[0], recv_sem=sems[1], device_id=to, device_id_type=MESH)


def _other_chips(x, y):
    return [(1 - x, y), (x, 1 - y), (1 - x, 1 - y)]


def _u_gather_ici(w_sh, idxs):
    def copies(ins, outs, sem):
        x, y, c = _me()
        res = []
        for j, (px, py) in enumerate(_other_chips(x, y)):
            for n, i in enumerate(idxs):
                src = ins[n].at[0, _half_rows(i, c, 16)]
                res.append((_rc(src, outs[n].at[2 * x + y, _half_rows(i, c, 16)], sem(j * len(idxs) + n), (px, py, c)),
                            _rc(src, outs[n].at[2 * px + py, _half_rows(i, c, 16)], sem(j * len(idxs) + n), (px, py, c))))
        return res

    return dict(ins=[w_sh[i] for i in idxs], outs=[jax.ShapeDtypeStruct((4,) + SH_SHAPES[i], BF16) for i in idxs],
                nsem=3 * len(idxs), alias={}, copies=copies)


def _u_gather_d2d(got, idxs):
    def copies(ins, outs, sem):
        x, y, c = _me()
        res = []
        for j, (px, py) in enumerate(_other_chips(x, y)):
            for n, i in enumerate(idxs):
                src = ins[n].at[2 * px + py, _half_rows(i, c, 16)]
                res.append((_rc(src, outs[n].at[2 * px + py, _half_rows(i, c, 16)], sem(j * len(idxs) + n), (x, y, 1 - c)),
                            _rc(src, outs[n].at[2 * px + py, _half_rows(i, 1 - c, 16)], sem(j * len(idxs) + n), (x, y, 1 - c))))
        return res

    return dict(ins=list(got), outs=[jax.ShapeDtypeStruct(g.shape, g.dtype) for g in got], nsem=3 * len(idxs),
                alias={n: n for n in range(len(idxs))}, copies=copies)


def _u_pair_send(gs, idxs):
    def copies(ins, outs, sem):
        x, y, c = _me()
        res = []
        for n, i in enumerate(idxs):
            for sh in range(4):
                cp = _rc(ins[n].at[sh, _half_rows(i, 1 - c, 8)], outs[n].at[sh], sem(4 * n + sh), (x, y, 1 - c))
                res.append((cp, cp))
        return res

    return dict(ins=list(gs), outs=[jax.ShapeDtypeStruct((4, SH_SHAPES[i][0] // 2, SH_SHAPES[i][1]), F32) for i in idxs],
                nsem=4 * len(idxs), alias={}, copies=copies)


def _u_chip_exchange(ts):
    def copies(ins, outs, sem):
        x, y, c = _me()
        res = []
        for j, (px, py) in enumerate(_other_chips(x, y)):
            for n in range(len(ts)):
                cp = _rc(ins[n].at[2 * px + py], outs[n].at[j], sem(j * len(ts) + n), (px, py, c))
                res.append((cp, cp))
        return res

    return dict(ins=list(ts), outs=[jax.ShapeDtypeStruct((3,) + t.shape[1:], t.dtype) for t in ts], nsem=3 * len(ts),
                alias={}, copies=copies)


def _u_pair_join(hs):
    def copies(ins, outs, sem):
        x, y, c = _me()
        res = []
        for n in range(len(hs)):
            cp = _rc(ins[n], outs[n], sem(n), (x, y, 1 - c))
            res.append((cp, cp))
        return res

    return dict(ins=list(hs), outs=[jax.ShapeDtypeStruct(h.shape, h.dtype) for h in hs], nsem=len(hs), alias={}, copies=copies)


def _comm_phase(units, ci, co, send_sems, recv_sems, start):
    ii = oo = off = 0
    for u in units:
        ni, no = len(u["ins"]), len(u["outs"])
        for st, arrival in u["copies"](ci[ii:ii + ni], co[oo:oo + no], lambda k, off=off: (send_sems.at[off + k], recv_sems.at[off + k])):
            if start:
                st.start()
            else:
                st.wait_send()
                arrival.wait_recv()
        ii, oo, off = ii + ni, oo + no, off + u["nsem"]


def _carry(units, n_in, n_out):
    ins = [a for u in units for a in u["ins"]]
    outs = [o for u in units for o in u["outs"]]
    alias, ii, oo = {}, 0, 0
    for u in units:
        for a, b in u["alias"].items():
            alias[n_in + ii + a] = n_out + oo + b
        ii, oo = ii + len(u["ins"]), oo + len(u["outs"])
    nsem = sum(u["nsem"] for u in units)
    scratch = [pltpu.SemaphoreType.DMA((nsem,)), pltpu.SemaphoreType.DMA((nsem,))] if units else []
    return ins, outs, alias, scratch


def _split_units(units, res):
    out, oo = [], 0
    for u in units:
        out.append(list(res[oo:oo + len(u["outs"])]))
        oo += len(u["outs"])
    return out


def _comm_call(name, units):
    ins, outs, alias, scratch = _carry(units, 0, 0)

    def body(*refs):
        ci, co = refs[:len(ins)], refs[len(ins):len(ins) + len(outs)]
        _comm_phase(units, ci, co, refs[-2], refs[-1], True)
        _comm_phase(units, ci, co, refs[-2], refs[-1], False)

    res = pl.pallas_call(body, name=name, out_shape=outs, in_specs=[HBM] * len(ins), out_specs=[HBM] * len(outs),
                         scratch_shapes=scratch, input_output_aliases=alias)(*ins)
    return _split_units(units, res)


SEM = pl.BlockSpec(memory_space=pltpu.SEMAPHORE)
EFFECT = pltpu.SideEffectType.DATAFLOW_SIDE_EFFECTING


def _unit_start(unit, name, after=()):
    bufs = list(unit["ins"]) + [lax.empty(o.shape, o.dtype) for o in unit["outs"]]
    n_i, n_b, ns = len(unit["ins"]), len(bufs), unit["nsem"]

    def body(*refs):
        send_sems, recv_sems = refs[n_b + len(after)], refs[n_b + len(after) + 1]
        for st, _ in unit["copies"](refs[:n_i], refs[n_i:n_b], lambda k: (send_sems.at[k], recv_sems.at[k])):
            st.start()
        refs[-1][...] = jnp.zeros_like(refs[-1])

    res = pl.pallas_call(
        body, name=name,
        out_shape=[pltpu.SemaphoreType.DMA((ns,)), pltpu.SemaphoreType.DMA((ns,))] + [pltpu.HBM(b.shape, b.dtype) for b in bufs]
        + [jax.ShapeDtypeStruct((8, 128), F32)],
        in_specs=[HBM] * n_b + [pl.BlockSpec(memory_space=pl.ANY)] * len(after), out_specs=[SEM, SEM] + [HBM] * n_b + [VMEM_SPEC],
        input_output_aliases={i: 2 + i for i in range(n_b)},
        compiler_params=pltpu.CompilerParams(has_side_effects=EFFECT),
    )(*[pltpu.with_memory_space_constraint(b, pltpu.HBM) for b in bufs], *after)
    return res[0], res[1], list(res[2:2 + n_b]), res[-1]


def _unit_wait(unit, send_sems, recv_sems, bufs, after, name):
    n_i, n_b = len(unit["ins"]), len(bufs)

    def body(*refs):
        ss, rs = refs[n_b], refs[n_b + 1]
        for st, arrival in unit["copies"](refs[:n_i], refs[n_i:n_b], lambda k: (ss.at[k], rs.at[k])):
            st.wait_send()
            arrival.wait_recv()

    res = pl.pallas_call(
        body, name=name, out_shape=[pltpu.HBM(b.shape, b.dtype) for b in bufs],
        in_specs=[HBM] * n_b + [SEM, SEM] + [pl.BlockSpec(memory_space=pl.ANY)] * len(after), out_specs=[HBM] * n_b,
        input_output_aliases={i: i for i in range(n_b)}, compiler_params=pltpu.CompilerParams(has_side_effects=EFFECT),
    )(*bufs, send_sems, recv_sems, *after)
    return list(res[:n_i]), list(res[n_i:])


def _pair_add(g, land, core, name):
    _, rows, cols = g.shape
    half = rows // 2
    rb = _tile(half, 256, 16)
    nb = half // rb

    def body(c_ref, g_ref, l_ref, o_ref):
        o_ref[...] = (g_ref[...] + l_ref[...]).astype(BF16)

    return pl.pallas_call(
        body, name=name,
        grid_spec=pltpu.PrefetchScalarGridSpec(
            num_scalar_prefetch=1, grid=(4, nb),
            in_specs=[pl.BlockSpec((1, rb, cols), lambda s, i, c_ref: (s, c_ref[0] * nb + i, 0)),
                      pl.BlockSpec((1, rb, cols), lambda s, i, c_ref: (s, i, 0))],
            out_specs=pl.BlockSpec((1, rb, cols), lambda s, i, c_ref: (s, i, 0))),
        out_shape=jax.ShapeDtypeStruct((4, half, cols), BF16),
        compiler_params=_params(("parallel", "parallel")),
    )(core, g, land)


def _chip_sum(t, r, chip, name):
    _, half, cols = t.shape
    rb = _tile(half, 256, 16)

    def body(s_ref, t_ref, r_ref, o_ref):
        o_ref[...] = ((t_ref[0].astype(F32) + r_ref[0].astype(F32)) + r_ref[1].astype(F32)) + r_ref[2].astype(F32)

    return pl.pallas_call(
        body, name=name,
        grid_spec=pltpu.PrefetchScalarGridSpec(
            num_scalar_prefetch=1, grid=(half // rb,),
            in_specs=[pl.BlockSpec((1, rb, cols), lambda i, s_ref: (s_ref[0], i, 0)),
                      pl.BlockSpec((3, rb, cols), lambda i, s_ref: (0, i, 0))],
            out_specs=pl.BlockSpec((rb, cols), lambda i, s_ref: (i, 0))),
        out_shape=jax.ShapeDtypeStruct((half, cols), F32),
        compiler_params=_params(("parallel",)),
    )(chip, t, r)


def _adam_math(wv, gv, mv, vv):
    mn = ADAM_B1 * mv + (1.0 - ADAM_B1) * gv
    vn = ADAM_B2 * vv + (1.0 - ADAM_B2) * (gv * gv)
    m_hat = mn / (1.0 - ADAM_B1 ** ADAM_STEP)
    v_hat = vn / (1.0 - ADAM_B2 ** ADAM_STEP)
    return -ADAM_LR * (m_hat / (jnp.sqrt(v_hat) + ADAM_EPS) + ADAM_WD * wv), mn, vn


def _adamw_halves(wt, mt, vt, mine, theirs, core, name):
    _, rows, cols = wt.shape
    half = rows // 2
    rb = _tile(half, 256, 8)
    nb = half // rb

    def body(c_ref, w_ref, m_ref, v_ref, a_ref, b_ref, g_ref, d_ref, mo_ref, vo_ref):
        gv = jnp.where(pl.program_id(0) == c_ref[0], a_ref[...], b_ref[...])
        dl, mn, vn = _adam_math(w_ref[...], gv, m_ref[...], v_ref[...])
        g_ref[...] = gv
        d_ref[...] = dl
        mo_ref[...] = mn
        vo_ref[...] = vn

    full = pl.BlockSpec((None, rb, cols), lambda hf, i, c_ref: (0, hf * nb + i, 0))
    part = pl.BlockSpec((rb, cols), lambda hf, i, c_ref: (i, 0))
    return pl.pallas_call(
        body, name=name,
        grid_spec=pltpu.PrefetchScalarGridSpec(num_scalar_prefetch=1, grid=(2, nb), in_specs=[full, full, full, part, part],
                                               out_specs=[full] * 4),
        out_shape=[jax.ShapeDtypeStruct((1, rows, cols), F32)] * 4,
        compiler_params=_params(("parallel", "parallel")),
    )(core, wt, mt, vt, mine, theirs)


SG_REP = 144
SG_LOSS = 136
SG_W2, SG_CW = SG_REP, SG_REP + 4 * 16
SG_ROWS = SG_CW + 4 * 40
SP_ROWS = SG_REP + 16 + 40


def _mod_shard(c_all, ada_w_sh):
    def body(c_ref, w_ref, o_ref):
        cv = c_ref[...]
        o_ref[...] = _dg((cv * _sigmoid(cv)).astype(BF16), w_ref[...].astype(BF16), 1, 0)

    return pl.pallas_call(body, name="mod_shard", out_shape=jax.ShapeDtypeStruct((8, 1536), F32),
                          in_specs=[VMEM_SPEC, VMEM_SPEC], out_specs=VMEM_SPEC,
                          compiler_params=pltpu.CompilerParams(vmem_limit_bytes=VMEM_LIMIT))(c_all, ada_w_sh)


def _mod_select(mod_all, ada_b4):
    def body(m_ref, b_ref, o_ref):
        x, y, c = _me()
        me = 4 * x + 2 * y + c
        for sh in range(4):
            o_ref[sh] = m_ref[2 * sh, me] + b_ref[sh]

    return pl.pallas_call(body, name="mod_select", out_shape=jax.ShapeDtypeStruct((4, 12, 128), F32),
                          in_specs=[VMEM_SPEC, VMEM_SPEC], out_specs=VMEM_SPEC)(mod_all, ada_b4)


def _small_reduce(sg_all):
    def body(g_ref, o_ref):
        x, y, c = _me()
        s_me = 2 * x + y
        w2_rows = pl.ds(pl.multiple_of(SG_W2 + 16 * s_me, 8), 16)
        cw_rows = pl.ds(pl.multiple_of(SG_CW + 40 * s_me, 8), 40)
        a = g_ref[0, 0:SG_REP, :]
        b = g_ref[0, w2_rows, :]
        d = g_ref[0, cw_rows, :]
        for dev in range(1, 8):
            a = a + g_ref[dev, 0:SG_REP, :]
            b = b + g_ref[dev, w2_rows, :]
            d = d + g_ref[dev, cw_rows, :]
        o_ref[0:SG_REP, :] = a
        o_ref[SG_REP:SG_REP + 16, :] = b
        o_ref[SG_REP + 16:SP_ROWS, :] = d

    return pl.pallas_call(body, name="small_grad_reduce", out_shape=jax.ShapeDtypeStruct((SP_ROWS, 128), F32),
                          in_specs=[VMEM_SPEC], out_specs=VMEM_SPEC)(sg_all)


def _ada_grad(dmod_all, c_bc):
    def body(g_ref, c_ref, o_ref):
        x, y, c = _me()
        s_me = 2 * x + y
        for k in range(12):
            acc = jnp.zeros((D, 128), F32)
            for b in range(8):
                cv = c_ref[b]
                acc = acc + (cv * _sigmoid(cv)) * g_ref[s_me, k, b:b + 1, :]
            o_ref[:, k * 128:(k + 1) * 128] = acc

    return pl.pallas_call(body, name="ada_w_grad", out_shape=jax.ShapeDtypeStruct((D, 1536), F32),
                          in_specs=[VMEM_SPEC, VMEM_SPEC], out_specs=VMEM_SPEC,
                          compiler_params=pltpu.CompilerParams(vmem_limit_bytes=VMEM_LIMIT))(dmod_all, c_bc)


def _adamw(wt, g, m, v, name):
    rows, cols = wt.shape
    rb = _tile(rows, 256, 8)

    def fn(c, i, wv, gv, mv, vv):
        return _adam_math(wv, gv, mv, vv)

    return _rowcall(fn, [_rows(t, rb) for t in (wt, g, m, v)], [_orow(rows, cols, F32, rb)] * 3,
                    n_rows=rows, rb=rb, name=name)


def _pad_rows(t, rows):
    flat = t.reshape(-1)
    return jnp.pad(flat, (0, rows * 128 - flat.shape[0])).reshape(rows, 128)


SP_LAYOUT = (("ada_b", 48), ("norm1_w", 8), ("gla_gate_b", 8), ("gla_norm_w", 8), ("norm2_w", 8), ("conv_b", 48),
             ("final_norm_w", 8), (None, 8), ("gla_gate_w2", 16), ("conv_w", 40))


def _pack_small(d):
    return jnp.concatenate([jnp.zeros((rows, 128), F32) if n is None else _pad_rows(d[n].astype(F32), rows)
                            for n, rows in SP_LAYOUT], axis=0)


def _unpack_small(pk, shapes):
    out, off = {}, 0
    for n, rows in SP_LAYOUT:
        if n is not None:
            shp = shapes[n]
            out[n] = pk[off:off + rows].reshape(-1)[:math.prod(shp)].reshape(shp)
        off += rows
    return out


def kernel(x, c, positions, ada_w, ada_b, norm1_w, w_in, gla_gate_w2, gla_gate_b, gla_norm_w, w_gla_branch, w_attn_branch, w_out, norm2_w, w_up, conv_w, conv_b, w_down, final_norm_w, loss_target, m_ada_w, m_ada_b, m_norm1_w, m_w_in, m_gla_gate_w2, m_gla_gate_b, m_gla_norm_w, m_w_gla_branch, m_w_attn_branch, m_w_out, m_norm2_w, m_w_up, m_conv_w, m_conv_b, m_w_down, m_final_norm_w, v_ada_w, v_ada_b, v_norm1_w, v_w_in, v_gla_gate_w2, v_gla_gate_b, v_gla_norm_w, v_w_gla_branch, v_w_attn_branch, v_w_out, v_norm2_w, v_w_up, v_conv_w, v_conv_b, v_w_down, v_final_norm_w):
    s = x.shape[1]
    names = ("ada_w", "ada_b", "norm1_w", "w_in", "gla_gate_w2", "gla_gate_b", "gla_norm_w", "w_gla_branch", "w_attn_branch",
             "w_out", "norm2_w", "w_up", "conv_w", "conv_b", "w_down", "final_norm_w")
    wts = dict(zip(names, (ada_w, ada_b, norm1_w, w_in, gla_gate_w2, gla_gate_b, gla_norm_w, w_gla_branch, w_attn_branch,
                           w_out, norm2_w, w_up, conv_w, conv_b, w_down, final_norm_w)))
    ms = dict(zip(names, (m_ada_w, m_ada_b, m_norm1_w, m_w_in, m_gla_gate_w2, m_gla_gate_b, m_gla_norm_w, m_w_gla_branch,
                          m_w_attn_branch, m_w_out, m_norm2_w, m_w_up, m_conv_w, m_conv_b, m_w_down, m_final_norm_w)))
    vs = dict(zip(names, (v_ada_w, v_ada_b, v_norm1_w, v_w_in, v_gla_gate_w2, v_gla_gate_b, v_gla_norm_w, v_w_gla_branch,
                          v_w_attn_branch, v_w_out, v_norm2_w, v_w_up, v_conv_w, v_conv_b, v_w_down, v_final_norm_w)))

    pk0 = jnp.concatenate([_pad_rows(c, 8), _pad_rows(gla_gate_w2, 16), _pad_rows(conv_w, 40)], axis=0)
    sm_all = _allgather8(pk0, "gather_small")
    c_all = sm_all[:, 0:8, :].reshape(8, D)
    w2_full = sm_all[0::2, 8:24, :].transpose(1, 0, 2).reshape(GLA_LR, 512)
    cw_full = sm_all[0::2, 24:64, :].reshape(4, 40 * 128)[:, :3 * W_UP_SH].reshape(4, 3, W_UP_SH).transpose(1, 0, 2).reshape(3, 2 * D_FF)

    mod_sh = _mod_shard(c_all, ada_w[0])
    mod_all = _allgather8(mod_sh.reshape(96, 128), "gather_mod")

    w_sh = [wts[n].astype(BF16) for n in BIG]
    u_g0 = _u_gather_ici(w_sh, (0,))
    g0 = (u_g0,) + _unit_start(u_g0, "gather_w_in_start", after=[mod_all])
    mod = _mod_select(mod_all.reshape(8, 8, 12, 128) + g0[4][0, 0], ada_b.reshape(4, 12, 128)).reshape(6, D)

    core = lax.axis_index("c").astype(jnp.int32).reshape(1)
    chip = (2 * lax.axis_index("x") + lax.axis_index("y")).astype(jnp.int32)
    sm = dict(n1w=norm1_w, n2w=norm2_w, fnw=final_norm_w.reshape(1, D), gnw=gla_norm_w, gb=gla_gate_b,
              w2=jnp.pad(w2_full, ((0, 128 - GLA_LR), (0, 0))), cw=_ff_to_kernel(cw_full), cb=_ff_to_kernel(conv_b))
    loss, grad_x, halves, others, small, ts0 = _local_step(x[0], mod, positions.reshape(s, 1), loss_target[0], sm, w_sh,
                                                               g0, chip, core)

    dcw = _ff_from_kernel(small["cw"]).reshape(3, 4, W_UP_SH).transpose(1, 0, 2)
    dw2 = small["w2"][:GLA_LR].reshape(GLA_LR, 4, 128).transpose(1, 0, 2)
    sg = jnp.concatenate(
        [_pad_rows(small["dmod"], 48), _pad_rows(small["n1w"], 8), _pad_rows(small["gb"], 8), _pad_rows(small["gnw"], 8),
         _pad_rows(small["n2w"], 8), _pad_rows(_ff_from_kernel(small["cb"]), 48), _pad_rows(small["fnw"], 8), _pad_rows(loss, 8)]
        + [_pad_rows(dw2[k], 16) for k in range(4)] + [_pad_rows(dcw[k], 40) for k in range(4)], axis=0)
    sg_all = _allgather8(sg, "gather_small_grads")
    u_ex = _u_chip_exchange([ts0])
    pending = (u_ex,) + _unit_start(u_ex, "grad_exchange_w_in_start", after=[sg_all])
    sg_all = sg_all + pending[4][0, 0]
    g_small_pk = _small_reduce(sg_all)
    dmod_all = sg_all[:, 0:48, :].reshape(8, 4, 12, 128).transpose(1, 2, 0, 3)
    g_ada_w = _ada_grad(dmod_all, jnp.broadcast_to(c_all[:, :, None], (8, D, 128)))

    shapes = {n: wts[n].shape for n in names}
    g_small = _unpack_small(g_small_pk, shapes)
    grads = {"ada_w": g_ada_w.reshape(1, D, 1536), **g_small}
    deltas, new_m, new_v = {}, {}, {}
    for n, mine, theirs in zip(BIG[1:], halves, others):
        grads[n], deltas[n], new_m[n], new_v[n] = _adamw_halves(wts[n], ms[n], vs[n], mine, theirs, core, "adamw_" + n)
    shp = ada_w.shape
    d_, m_, v_ = _adamw(ada_w[0], g_ada_w, m_ada_w[0], v_ada_w[0], "adamw_ada_w")
    deltas["ada_w"], new_m["ada_w"], new_v["ada_w"] = d_.reshape(shp), m_.reshape(shp), v_.reshape(shp)
    d_, m_, v_ = _adamw(_pack_small(wts), g_small_pk, _pack_small(ms), _pack_small(vs), "adamw_small")
    for dst, pk in ((deltas, d_), (new_m, m_), (new_v, v_)):
        dst.update(_unpack_small(pk, shapes))

    [t0], [r0] = _unit_wait(*pending[:4], after=[d_, deltas["ada_w"], deltas["w_up"], deltas["w_down"]], name="grad_exchange_w_in_wait")
    half0 = _chip_sum(t0, r0, chip.reshape(1), "grad_chip_sum_w_in")
    [[oth0]] = _comm_call("grad_join_w_in", [_u_pair_join([half0])])
    grads["w_in"], deltas["w_in"], new_m["w_in"], new_v["w_in"] = _adamw_halves(w_in, m_w_in, v_w_in, half0, oth0, core, "adamw_w_in")

    return (g_small_pk[SG_LOSS, 0], grad_x.reshape(1, s, D), *[grads[n] for n in names], *[deltas[n] for n in names],
            *[new_m[n] for n in names], *[new_v[n] for n in names])
```

```python
import math

import jax
import jax.numpy as jnp
from jax import lax
from jax.experimental import pallas as pl
from jax.experimental.pallas import tpu as pltpu

F32, BF16 = jnp.float32, jnp.bfloat16
MESH = pl.DeviceIdType.MESH

D = 1024
EPS = 1e-6
GLA_H, GLA_DK, GLA_DV, GLA_LR = 4, 128, 256, 16
GLA_TAU = 16.0
GLA_CHUNK = 64
GLA_BLOCK = 512
ATT_GROUPS = ((128, 1), (512, 4), (2048, 16))
ATT_BLK = 128
ATT_HD = 64
ATT_W = 768
D_FF = 2816
ROPE_THETA = 10000.0
P_W = 7680
P_GV, P_GR, P_MA, P_MB, P_GQ, P_GK, P_AQ, P_AK, P_AV, P_LR = 0, 1024, 2048, 3072, 4096, 4608, 5120, 5888, 6656, 7424
W_IN = 7440
W_IN_SH, W_UP_SH, W_DOWN_SH = 1860, 1408, 704
VMEM_LIMIT = 56 * 1024 * 1024
ADAM_LR, ADAM_B1, ADAM_B2, ADAM_EPS, ADAM_WD, ADAM_STEP = 0.001, 0.9, 0.999, 1e-08, 0.01, 10
NEG = -1e30


def _tile(n, target, unit=128):
    best = None
    for t in range(unit, min(n, target) + 1, unit):
        if n % t == 0:
            best = t
    return best or n


def _params(sem):
    return pltpu.CompilerParams(dimension_semantics=sem, vmem_limit_bytes=VMEM_LIMIT)


def _dg(a, b, ca, cb):
    return lax.dot_general(a, b, (((ca,), (cb,)), ((), ())), preferred_element_type=F32)


def _sigmoid(v):
    return 1.0 / (1.0 + jnp.exp(-v))


def _ff_block(j):
    return (j % 2) * 2 + j // 2


def _mm(a, b, name, *, ta=False, tb=False, out_dtype=BF16, tm=1024, tn=1536, tk=1024, n_outer=True, comm=(),
        b_shards=False, o_shards=False):
    m = a.shape[1] if ta else a.shape[0]
    k = a.shape[0] if ta else a.shape[1]
    if b_shards:
        n = b.shape[1] if tb else 4 * W_UP_SH
        tn, tk = (tn, W_UP_SH) if tb else (W_UP_SH, tk)
    else:
        n = b.shape[0] if tb else b.shape[1]
    if o_shards:
        tn = W_UP_SH
    tm, tn, tk = _tile(m, tm), _tile(n, tn), _tile(k, tk)
    nm, nn, nk = m // tm, n // tn, k // tk
    in_out = out_dtype == F32
    c_ins, c_outs, c_alias, c_scratch = _carry(comm, 2, 1)

    def body(a_ref, b_ref, *rest):
        ci, o_ref, co = rest[:len(c_ins)], rest[len(c_ins)], rest[len(c_ins) + 1:len(c_ins) + 1 + len(c_outs)]
        scr = rest[len(c_ins) + 1 + len(c_outs):]
        kk = pl.program_id(2)
        if comm:
            step = (pl.program_id(0) * (nm if n_outer else nn) + pl.program_id(1)) * nk + kk

            @pl.when(step == 0)
            def _():
                _comm_phase(comm, ci, co, scr[-2], scr[-1], True)

        _mm_step(a_ref, b_ref, o_ref, scr, kk)
        if comm:
            @pl.when(step == nm * nn * nk - 1)
            def _():
                _comm_phase(comm, ci, co, scr[-2], scr[-1], False)

    def _mm_step(a_ref, b_ref, o_ref, scr, kk):
        p = _dg(a_ref[...].astype(BF16), b_ref[...].astype(BF16), 0 if ta else 1, 1 if tb else 0)
        if nk == 1:
            o_ref[...] = p.astype(o_ref.dtype)
        else:
            acc = o_ref if in_out else scr[0]

            @pl.when(kk == 0)
            def _():
                acc[...] = p

            @pl.when(kk > 0)
            def _():
                acc[...] += p

            if not in_out:
                @pl.when(kk == nk - 1)
                def _():
                    o_ref[...] = acc[...].astype(o_ref.dtype)

    if n_outer:
        ij = lambda g0, g1: (g1, g0)
        grid = (nn, nm, nk)
    else:
        ij = lambda g0, g1: (g0, g1)
        grid = (nm, nn, nk)
    a_map = (lambda g0, g1, kk: (kk, ij(g0, g1)[0])) if ta else (lambda g0, g1, kk: (ij(g0, g1)[0], kk))
    if b_shards and tb:
        b_spec = pl.BlockSpec((None, tn, tk), lambda g0, g1, kk: (_ff_block(kk), ij(g0, g1)[1], 0))
    elif b_shards:
        b_spec = pl.BlockSpec((None, tk, tn), lambda g0, g1, kk: (_ff_block(ij(g0, g1)[1]), kk, 0))
    elif tb:
        b_spec = pl.BlockSpec((tn, tk), lambda g0, g1, kk: (ij(g0, g1)[1], kk))
    else:
        b_spec = pl.BlockSpec((tk, tn), lambda g0, g1, kk: (kk, ij(g0, g1)[1]))
    if o_shards:
        o_spec = pl.BlockSpec((None, tm, tn), lambda g0, g1, kk: (_ff_block(ij(g0, g1)[1]), ij(g0, g1)[0], 0))
        o_shape = jax.ShapeDtypeStruct((4, m, W_UP_SH), out_dtype)
    else:
        o_spec = pl.BlockSpec((tm, tn), lambda g0, g1, kk: ij(g0, g1))
        o_shape = jax.ShapeDtypeStruct((m, n), out_dtype)
    res = pl.pallas_call(
        body, name=name, grid=grid,
        in_specs=[pl.BlockSpec((tk, tm) if ta else (tm, tk), a_map), b_spec] + [HBM] * len(c_ins),
        out_specs=[o_spec] + [HBM] * len(c_outs),
        out_shape=[o_shape] + c_outs,
        scratch_shapes=([] if (in_out or nk == 1) else [pltpu.VMEM((tm, tn), F32)]) + c_scratch,
        input_output_aliases=c_alias,
        compiler_params=_params(("arbitrary",) * 3 if comm else ("parallel", "parallel", "arbitrary")),
    )(a, b, *c_ins)
    return (res[0], _split_units(comm, res[1:])) if comm else res[0]


def _rows(arr, rb, w=None, j=0):
    w = arr.shape[1] if w is None else w
    if callable(j):
        return arr, pl.BlockSpec((rb, w), lambda c, i: (i, j(c)))
    return arr, pl.BlockSpec((rb, w), lambda c, i: (i, j))


def _full(arr, w=None, j=0):
    w = arr.shape[1] if w is None else w
    if callable(j):
        return arr, pl.BlockSpec((arr.shape[0], w), lambda c, i: (0, j(c)))
    return arr, pl.BlockSpec((arr.shape[0], w), lambda c, i: (0, j))


def _halo(arr, rb, hb, w, j, before):
    per = rb // hb
    last = arr.shape[0] // hb - 1
    if before:
        rmap = lambda i: jnp.maximum(i * per - 1, 0)
    else:
        rmap = lambda i: jnp.minimum((i + 1) * per, last)
    return arr, pl.BlockSpec((hb, w), lambda c, i: (rmap(i), j(c) if callable(j) else j))


def _rowcall(fn, ins, outs, *, n_rows, rb, name, ncol=1, into=None, after=()):
    n_in = len(ins)
    nr = n_rows // rb
    unread = ([] if into is None else [into[0]]) + list(after)
    n_skip = len(unread)

    def body(*refs):
        c, i = pl.program_id(0), pl.program_id(1)
        res = fn(c, i, *[r[...] for r in refs[:n_in]])
        for val, spec, o_ref in zip(res, outs, refs[n_in + n_skip:]):
            if spec[2] == "row":
                o_ref[...] = val.astype(o_ref.dtype)
            else:
                @pl.when(i == 0)
                def _(o_ref=o_ref, val=val):
                    o_ref[...] = val.astype(o_ref.dtype)

                @pl.when(i > 0)
                def _(o_ref=o_ref, val=val):
                    o_ref[...] += val.astype(o_ref.dtype)

    out_specs = []
    for shape, dt, kind, block, col in outs:
        if kind == "row":
            out_specs.append(pl.BlockSpec(block, lambda c, i, col=col: (i, col(c))))
        else:
            out_specs.append(pl.BlockSpec(block, lambda c, i, col=col: (0, col(c))))
    return pl.pallas_call(
        body, name=name, grid=(ncol, nr),
        in_specs=[s for _, s in ins] + [pl.BlockSpec(memory_space=pl.ANY)] * n_skip, out_specs=out_specs,
        out_shape=[jax.ShapeDtypeStruct(o[0], o[1]) for o in outs],
        input_output_aliases={} if into is None else {n_in: into[1]},
        compiler_params=_params(("parallel", "arbitrary")),
    )(*[a for a, _ in ins], *unread)


def _orow(n_rows, w, dt, rb, bw=None, col=lambda c: 0):
    return ((n_rows, w), dt, "row", (rb, bw or w), col)


def _oacc(r, w, bw=None, col=lambda c: 0):
    return ((r, w), F32, "acc", (r, bw or w), col)


def _csum(v):
    return jnp.sum(v, axis=0, keepdims=True)


def _rms(v):
    return lax.rsqrt(jnp.mean(v * v, axis=-1, keepdims=True) + EPS)


def _norm_bwd(xv, dh, w, scale):
    r = _rms(xv)
    xh = xv * r
    dxh = dh * (w * (1.0 + scale))
    dx = r * (dxh - xh * jnp.mean(dxh * xh, axis=-1, keepdims=True))
    t = dh * xh
    return dx, _csum(dh), _csum(t * w), _csum(t * (1.0 + scale))


def _rope_tables(pos_col, invf, s):
    def fn(c, i, pos, f):
        ang = pos.astype(F32) * f
        lane = lax.broadcasted_iota(jnp.int32, ang.shape, 1)
        sign = jnp.where((lane % ATT_HD) < ATT_HD // 2, -1.0, 1.0)
        return jnp.cos(ang), jnp.sin(ang) * sign

    rb = 512
    return _rowcall(fn, [_rows(pos_col, rb), _full(invf)], [_orow(s, 128, F32, rb), _orow(s, 128, F32, rb)],
                    n_rows=s, rb=rb, name="rope_tables")


def _swap_halves(t):
    n = t.shape[1]
    lane = lax.broadcasted_iota(jnp.int32, t.shape, 1)
    return jnp.where((lane % ATT_HD) < ATT_HD // 2, pltpu.roll(t, n - 32, 1), pltpu.roll(t, 32, 1))


def _rope_apply(t, cos, sin_signed, inverse):
    cw = jnp.concatenate([cos] * (t.shape[1] // 128), axis=1)
    sw = jnp.concatenate([sin_signed] * (t.shape[1] // 128), axis=1)
    if inverse:
        sw = -sw
    return t * cw + _swap_halves(t) * sw


DIL_ROWS = 512


def _to_dilated(scr, val, out_ref, r):
    if r == 1:
        out_ref[...] = val.astype(out_ref.dtype)
        return
    n = val.shape[0] // r
    for hh in range(2):
        scr[hh] = val[:, hh * 128:(hh + 1) * 128]
        for pr in range(r):
            out_ref[:, pr * 256 + hh * 128:pr * 256 + (hh + 1) * 128] = scr[hh, pl.ds(pr, n, stride=r), :].astype(out_ref.dtype)


def _from_dilated(scr, in_ref, r):
    if r == 1:
        return in_ref[...].astype(F32)
    n = in_ref.shape[0]
    for hh in range(2):
        for pr in range(r):
            scr[hh, pl.ds(pr, n, stride=r), :] = in_ref[:, pr * 256 + hh * 128:pr * 256 + (hh + 1) * 128].astype(F32)
    return jnp.concatenate([scr[0], scr[1]], axis=1)


def _dil_spec(r):
    return pl.BlockSpec((DIL_ROWS // r, r * 256), lambda i: (i, 0))


def _dil_shape(s, r, dt):
    return jax.ShapeDtypeStruct((s // r, r * 256), dt)


_DIL_SCRATCH = [pltpu.VMEM((2, DIL_ROWS, 128), F32)]
_RS = tuple(r for _, r in ATT_GROUPS)


def _rope_fwd(p, cos_t, sin_t, s):
    def body(*refs):
        ins, cs, sn, outs, scr = refs[:9], refs[9][...], refs[10][...], refs[11:20], refs[20]
        for t in range(3):
            for g, r in enumerate(_RS):
                val = ins[3 * t + g][...].astype(F32)
                _to_dilated(scr, _rope_apply(val, cs, sn, False) if t < 2 else val, outs[3 * t + g], r)

    res = pl.pallas_call(
        body, name="rope", grid=(s // DIL_ROWS,),
        in_specs=[pl.BlockSpec((DIL_ROWS, 256), lambda i, c=base // 256 + g: (i, c)) for base in (P_AQ, P_AK, P_AV) for g in range(3)]
        + [pl.BlockSpec((DIL_ROWS, 128), lambda i: (i, 0))] * 2,
        out_specs=[_dil_spec(r) for _ in range(3) for r in _RS],
        out_shape=[_dil_shape(s, r, BF16) for _ in range(3) for r in _RS],
        scratch_shapes=_DIL_SCRATCH, compiler_params=_params(("parallel",)),
    )(*([p] * 9), cos_t, sin_t)
    return res[0:3], res[3:6], res[6:9]


def _attn_combine(att, s):
    def body(o0, o1, o2, l0, l1, l2, o_ref, lse_ref, od1, od2, ld1, ld2, scr):
        ov = [_from_dilated(scr, ref, r) for ref, r in zip((o0, o1, o2), _RS)]
        lv = [_from_dilated(scr, ref, r) for ref, r in zip((l0, l1, l2), _RS)]
        mx = jnp.maximum(jnp.maximum(lv[0], lv[1]), lv[2])
        ev = [jnp.exp(l - mx) for l in lv]
        z = ev[0] + ev[1] + ev[2]
        o = ((ev[0] * ov[0] + ev[1] * ov[1] + ev[2] * ov[2]) / z).astype(BF16)
        lse = mx + jnp.log(z)
        o_ref[...] = o
        lse_ref[...] = lse
        for ref, r in zip((od1, od2), _RS[1:]):
            _to_dilated(scr, o.astype(F32), ref, r)
        for ref, r in zip((ld1, ld2), _RS[1:]):
            _to_dilated(scr, lse, ref, r)

    return pl.pallas_call(
        body, name="attn_combine", grid=(s // DIL_ROWS,),
        in_specs=[_dil_spec(r) for r in _RS] * 2,
        out_specs=[_dil_spec(1)] * 2 + [_dil_spec(r) for r in _RS[1:]] * 2,
        out_shape=[_dil_shape(s, 1, BF16), _dil_shape(s, 1, F32)] + [_dil_shape(s, r, BF16) for r in _RS[1:]]
        + [_dil_shape(s, r, F32) for r in _RS[1:]],
        scratch_shapes=_DIL_SCRATCH, compiler_params=_params(("parallel",)),
    )(*[a[0] for a in att], *[a[1] for a in att])


def _dilate(t, s):
    def body(t_ref, o1, o2, scr):
        val = t_ref[...].astype(F32)
        for ref, r in zip((o1, o2), _RS[1:]):
            _to_dilated(scr, val, ref, r)

    return pl.pallas_call(
        body, name="attn_dilate", grid=(s // DIL_ROWS,), in_specs=[_dil_spec(1)], out_specs=[_dil_spec(r) for r in _RS[1:]],
        out_shape=[_dil_shape(s, r, t.dtype) for r in _RS[1:]], scratch_shapes=_DIL_SCRATCH, compiler_params=_params(("parallel",)),
    )(t)


def _rope_bwd(datt, d_glr, dp, cos_t, sin_t, s):
    tail = P_W - P_AQ

    def body(*refs):
        ins, cs, sn, glr, o_ref, scr = refs[:9], refs[9][...], refs[10][...], refs[11], refs[13], refs[14]
        for t in range(3):
            for g, r in enumerate(_RS):
                val = _from_dilated(scr, ins[3 * t + g], r)
                o_ref[:, t * ATT_W + g * 256:t * ATT_W + (g + 1) * 256] = (_rope_apply(val, cs, sn, True) if t < 2 else val).astype(BF16)
        o_ref[:, 3 * ATT_W:3 * ATT_W + 128] = glr[...]
        o_ref[:, 3 * ATT_W + 128:] = jnp.zeros((DIL_ROWS, tail - 3 * ATT_W - 128), BF16)

    return pl.pallas_call(
        body, name="rope_bwd", grid=(s // DIL_ROWS,),
        in_specs=[_dil_spec(r) for _ in range(3) for r in _RS] + [pl.BlockSpec((DIL_ROWS, 128), lambda i: (i, 0))] * 3
        + [pl.BlockSpec(memory_space=pl.ANY)],
        out_specs=pl.BlockSpec((DIL_ROWS, tail), lambda i: (i, P_AQ // tail)),
        out_shape=jax.ShapeDtypeStruct((s, P_W), BF16), input_output_aliases={12: 0},
        scratch_shapes=_DIL_SCRATCH, compiler_params=_params(("parallel",)),
    )(*[datt[g][t] for t in range(3) for g in range(3)], cos_t, sin_t, d_glr, dp)


def _tri_dot(tri, t):
    tb = tri.astype(BF16)
    hi = t.astype(BF16)
    r1 = t - hi.astype(F32)
    mid = r1.astype(BF16)
    lo = (r1 - mid.astype(F32)).astype(BF16)
    return _dg(tb, hi, 1, 0) + _dg(tb, mid, 1, 0) + _dg(tb, lo, 1, 0)


def _gla_decays(la_c, tri):
    b = _tri_dot(tri, la_c)
    row = lax.broadcasted_iota(jnp.int32, b.shape, 0)
    bmid = jnp.sum(jnp.where(row == GLA_CHUNK // 2 - 1, b, 0.0), axis=0, keepdims=True)
    blast = jnp.sum(jnp.where(row == GLA_CHUNK - 1, b, 0.0), axis=0, keepdims=True)
    return b, bmid, blast


def _gla_fwd(p, la, s, comm=()):
    tb, ch = GLA_BLOCK, GLA_CHUNK
    nb, nc = s // tb, tb // ch
    scale = GLA_DK ** -0.5
    c_ins, c_outs, c_alias, c_scratch = _carry(comm, 4, 2)

    def body(q_ref, k_ref, v_ref, la_ref, *rest):
        ci, (o_ref, st_ref) = rest[:len(c_ins)], rest[len(c_ins):len(c_ins) + 2]
        co, state = rest[len(c_ins) + 2:len(c_ins) + 2 + len(c_outs)], rest[len(c_ins) + 2 + len(c_outs)]
        step = pl.program_id(0)
        if comm:
            @pl.when(step == 0)
            def _():
                _comm_phase(comm, ci, co, rest[-2], rest[-1], True)

        _gla_fwd_step(q_ref, k_ref, v_ref, la_ref, o_ref, st_ref, state)
        if comm:
            @pl.when(step == nb - 1)
            def _():
                _comm_phase(comm, ci, co, rest[-2], rest[-1], False)

    def _gla_fwd_step(q_ref, k_ref, v_ref, la_ref, o_ref, st_ref, state):
        @pl.when(pl.program_id(0) == 0)
        def _():
            state[...] = jnp.zeros_like(state)

        ri = lax.broadcasted_iota(jnp.int32, (ch, ch), 0)
        ci = lax.broadcasted_iota(jnp.int32, (ch, ch), 1)
        causal = ci <= ri
        tri = causal.astype(F32)

        def chunk(c, carry):
            sl = pl.ds(pl.multiple_of(c * ch, ch), ch)
            b, bmid, blast = _gla_decays(la_ref[sl, :], tri)
            q = q_ref[sl, :].astype(F32) * scale
            k = k_ref[sl, :].astype(F32)
            v = v_ref[sl, :]
            qgt = (q * jnp.exp(b)).astype(BF16)
            qgn = (q * jnp.exp(b - bmid)).astype(BF16)
            kgn = (k * jnp.exp(bmid - b)).astype(BF16)
            kd = (k * jnp.exp(blast - b)).astype(BF16)
            dec = jnp.exp(blast)
            sts = [state[h] for h in range(GLA_H)]
            outs, news = [], []
            for h in range(GLA_H):
                hk, hv = slice(h * GLA_DK, (h + 1) * GLA_DK), slice(h * GLA_DV, (h + 1) * GLA_DV)
                a = jnp.where(causal, _dg(qgn[:, hk], kgn[:, hk], 1, 1), 0.0)
                outs.append(_dg(a.astype(BF16), v[:, hv], 1, 0) + _dg(qgt[:, hk], sts[h].astype(BF16), 1, 1))
                news.append(dec[:, hk] * sts[h] + _dg(v[:, hv], kd[:, hk], 0, 0))
            for h in range(GLA_H):
                st_ref[h, c] = sts[h]
                state[h] = news[h]
            o_ref[sl, :] = jnp.concatenate(outs, axis=1)
            return carry

        lax.fori_loop(0, nc, chunk, 0)

    hw = GLA_H * GLA_DK
    res = pl.pallas_call(
        body, name="gla_fwd", grid=(nb,),
        in_specs=[pl.BlockSpec((tb, hw), lambda t: (t, P_GQ // hw)),
                  pl.BlockSpec((tb, hw), lambda t: (t, P_GK // hw)),
                  pl.BlockSpec((tb, GLA_H * GLA_DV), lambda t: (t, P_GV // (GLA_H * GLA_DV))),
                  pl.BlockSpec((tb, hw), lambda t: (t, 0))] + [HBM] * len(c_ins),
        out_specs=[pl.BlockSpec((tb, GLA_H * GLA_DV), lambda t: (t, 0)),
                   pl.BlockSpec((GLA_H, nc, GLA_DV, GLA_DK), lambda t: (0, t, 0, 0))] + [HBM] * len(c_outs),
        out_shape=[jax.ShapeDtypeStruct((s, GLA_H * GLA_DV), F32),
                   jax.ShapeDtypeStruct((GLA_H, s // ch, GLA_DV, GLA_DK), F32)] + c_outs,
        scratch_shapes=[pltpu.VMEM((GLA_H, GLA_DV, GLA_DK), F32)] + c_scratch,
        input_output_aliases=c_alias,
        compiler_params=_params(("arbitrary",)),
    )(p, p, p, la, *c_ins)
    return res[0], res[1], _split_units(comm, res[2:])


def _gla_bwd(p, la, states, do, s, dp, comm=()):
    tb, ch = GLA_BLOCK, GLA_CHUNK
    nb, nc = s // tb, tb // ch
    scale = GLA_DK ** -0.5
    c_ins, c_outs, c_alias, c_scratch = _carry(comm, 7, 4)

    def body(q_ref, k_ref, v_ref, la_ref, st_ref, do_ref, dp_in, *rest):
        ci, outs = rest[:len(c_ins)], rest[len(c_ins):len(c_ins) + 4]
        co, dstate = rest[len(c_ins) + 4:len(c_ins) + 4 + len(c_outs)], rest[len(c_ins) + 4 + len(c_outs)]
        step = pl.program_id(0)
        if comm:
            @pl.when(step == 0)
            def _():
                _comm_phase(comm, ci, co, rest[-2], rest[-1], True)

        _gla_bwd_step(q_ref, k_ref, v_ref, la_ref, st_ref, do_ref, *outs, dstate)
        if comm:
            @pl.when(step == nb - 1)
            def _():
                _comm_phase(comm, ci, co, rest[-2], rest[-1], False)

    def _gla_bwd_step(q_ref, k_ref, v_ref, la_ref, st_ref, do_ref, dq_ref, dk_ref, dv_ref, dla_ref, dstate):
        @pl.when(pl.program_id(0) == 0)
        def _():
            dstate[...] = jnp.zeros_like(dstate)

        ri = lax.broadcasted_iota(jnp.int32, (ch, ch), 0)
        ci = lax.broadcasted_iota(jnp.int32, (ch, ch), 1)
        causal = ci <= ri
        tri = causal.astype(F32)
        tri_t = (ci >= ri).astype(F32)

        def chunk(cc, carry):
            c = nc - 1 - cc
            sl = pl.ds(pl.multiple_of(c * ch, ch), ch)
            b, bmid, blast = _gla_decays(la_ref[sl, :], tri)
            q = q_ref[sl, :].astype(F32) * scale
            k = k_ref[sl, :].astype(F32)
            v = v_ref[sl, :]
            e_b, e_qn, e_kn, e_kd = jnp.exp(b), jnp.exp(b - bmid), jnp.exp(bmid - b), jnp.exp(blast - b)
            dec = jnp.exp(blast)
            qgt, qgn, kgn, kd = q * e_b, q * e_qn, k * e_kn, k * e_kd
            qgt_b, qgn_b, kgn_b, kd_b = qgt.astype(BF16), qgn.astype(BF16), kgn.astype(BF16), kd.astype(BF16)
            do_b = do_ref[sl, :].astype(BF16)
            st0s = [st_ref[h, c] for h in range(GLA_H)]
            dsts = [dstate[h] for h in range(GLA_H)]
            dqgn, dqgt, dkgn, dkd, dvs, ddec, news = [], [], [], [], [], [], []
            for h in range(GLA_H):
                hk, hv = slice(h * GLA_DK, (h + 1) * GLA_DK), slice(h * GLA_DV, (h + 1) * GLA_DV)
                dst_b = dsts[h].astype(BF16)
                a = jnp.where(causal, _dg(qgn_b[:, hk], kgn_b[:, hk], 1, 1), 0.0).astype(BF16)
                da = jnp.where(causal, _dg(do_b[:, hv], v[:, hv], 1, 1), 0.0).astype(BF16)
                dqgn.append(_dg(da, kgn_b[:, hk], 1, 0))
                dqgt.append(_dg(do_b[:, hv], st0s[h].astype(BF16), 1, 0))
                dkgn.append(_dg(da, qgn_b[:, hk], 0, 0))
                dvs.append(_dg(a, do_b[:, hv], 0, 0) + _dg(kd_b[:, hk], dst_b, 1, 1))
                dkd.append(_dg(v[:, hv], dst_b, 1, 0))
                ddec.append(jnp.sum(st0s[h] * dsts[h], axis=0, keepdims=True))
                news.append(dec[:, hk] * dsts[h] + _dg(do_b[:, hv], qgt_b[:, hk], 0, 0))
            for h in range(GLA_H):
                dstate[h] = news[h]
            cat = lambda parts: jnp.concatenate(parts, axis=1)
            dqgn, dqgt, dkgn, dkd, ddec = cat(dqgn), cat(dqgt), cat(dkgn), cat(dkd), cat(ddec)
            dq_ref[sl, :] = (scale * (dqgn * e_qn + dqgt * e_b)).astype(dq_ref.dtype)
            dk_ref[sl, :] = (dkgn * e_kn + dkd * e_kd).astype(dk_ref.dtype)
            dv_ref[sl, :] = cat(dvs).astype(dv_ref.dtype)
            db = dqgn * qgn + dqgt * qgt - dkgn * kgn - dkd * kd
            extra = jnp.sum(dkd * kd, axis=0, keepdims=True) + ddec * dec
            dla_ref[sl, :] = _tri_dot(tri_t, db) + extra
            return carry

        lax.fori_loop(0, nc, chunk, 0)

    rev = lambda t: nb - 1 - t
    hw, vw = GLA_H * GLA_DK, GLA_H * GLA_DV
    res = pl.pallas_call(
        body, name="gla_bwd", grid=(nb,),
        in_specs=[pl.BlockSpec((tb, hw), lambda t: (rev(t), P_GQ // hw)),
                  pl.BlockSpec((tb, hw), lambda t: (rev(t), P_GK // hw)),
                  pl.BlockSpec((tb, vw), lambda t: (rev(t), P_GV // vw)),
                  pl.BlockSpec((tb, hw), lambda t: (rev(t), 0)),
                  pl.BlockSpec((GLA_H, nc, GLA_DV, GLA_DK), lambda t: (0, rev(t), 0, 0)),
                  pl.BlockSpec((tb, vw), lambda t: (rev(t), 0)), pl.BlockSpec(memory_space=pl.ANY)] + [HBM] * len(c_ins),
        out_specs=[pl.BlockSpec((tb, hw), lambda t: (rev(t), 0)),
                   pl.BlockSpec((tb, hw), lambda t: (rev(t), 0)),
                   pl.BlockSpec((tb, vw), lambda t: (rev(t), P_GV // vw)),
                   pl.BlockSpec((tb, hw), lambda t: (rev(t), 0))] + [HBM] * len(c_outs),
        out_shape=[jax.ShapeDtypeStruct((s, hw), BF16),
                   jax.ShapeDtypeStruct((s, hw), BF16),
                   jax.ShapeDtypeStruct((s, P_W), BF16),
                   jax.ShapeDtypeStruct((s, hw), F32)] + c_outs,
        scratch_shapes=[pltpu.VMEM((GLA_H, GLA_DV, GLA_DK), F32)] + c_scratch,
        input_output_aliases={6: 2, **c_alias},
        compiler_params=_params(("arbitrary",)),
    )(p, p, p, la, states, do, dp, *c_ins)
    return res[0], res[1], res[2], res[3], _split_units(comm, res[4:])


def _head_masks():
    lane = lax.broadcasted_iota(jnp.int32, (1, 4 * ATT_HD), 1)
    return [(lane >= h * ATT_HD) & (lane < (h + 1) * ATT_HD) for h in range(4)]


def _attn_fwd(qv, kv, pv, g, r, s):
    ln = s // r
    nblk = ln // ATT_BLK
    qcol = lambda pr: pr
    vcol = qcol
    prev = lambda n: jnp.maximum(n - 1, 0)

    def body(q_ref, kp_ref, kc_ref, vp_ref, vc_ref, o_ref, lse_ref):
        has_prev = pl.program_id(1) > 0
        ri = lax.broadcasted_iota(jnp.int32, (ATT_BLK, ATT_BLK), 0)
        ci = lax.broadcasted_iota(jnp.int32, (ATT_BLK, ATT_BLK), 1)
        m_cur = ci <= ri
        m_prev = (ci >= ri) & has_prev
        q, kp, kc, vp, vc = q_ref[...], kp_ref[...], kc_ref[...], vp_ref[...], vc_ref[...]
        o = jnp.zeros((ATT_BLK, 256), F32)
        lse = jnp.zeros((ATT_BLK, 256), F32)
        for hm in _head_masks():
            qm = jnp.where(hm, q, jnp.zeros_like(q))
            sc = jnp.where(m_cur, _dg(qm, kc, 1, 1) * 0.125, NEG)
            sp = jnp.where(m_prev, _dg(qm, kp, 1, 1) * 0.125, NEG)
            mx = jnp.maximum(jnp.max(sc, axis=1, keepdims=True), jnp.max(sp, axis=1, keepdims=True))
            pc, pp = jnp.exp(sc - mx), jnp.exp(sp - mx)
            den = jnp.sum(pc, axis=1, keepdims=True) + jnp.sum(pp, axis=1, keepdims=True)
            oh = (_dg(pc.astype(BF16), vc, 1, 0) + _dg(pp.astype(BF16), vp, 1, 0)) / den
            o = jnp.where(hm, oh, o)
            lse = jnp.where(hm, mx + jnp.log(den), lse)
        o_ref[...] = o.astype(o_ref.dtype)
        lse_ref[...] = lse

    blk = (ATT_BLK, 256)
    o, lse = pl.pallas_call(
        body, name=f"attn_fwd_{g}", grid=(r, nblk),
        in_specs=[pl.BlockSpec(blk, lambda pr, n: (n, qcol(pr))),
                  pl.BlockSpec(blk, lambda pr, n: (prev(n), qcol(pr))),
                  pl.BlockSpec(blk, lambda pr, n: (n, qcol(pr))),
                  pl.BlockSpec(blk, lambda pr, n: (prev(n), vcol(pr))),
                  pl.BlockSpec(blk, lambda pr, n: (n, vcol(pr)))],
        out_specs=[pl.BlockSpec(blk, lambda pr, n: (n, pr)), pl.BlockSpec(blk, lambda pr, n: (n, pr))],
        out_shape=[jax.ShapeDtypeStruct((ln, r * 256), BF16), jax.ShapeDtypeStruct((ln, r * 256), F32)],
        compiler_params=_params(("parallel", "parallel")),
    )(qv, kv, kv, pv, pv)
    return o, lse


def _attn_bwd(qv, kv, pv, dov, ov, lv, g, r, s):
    ln = s // r
    nblk = ln // ATT_BLK
    qcol = lambda pr: pr
    vcol = qcol
    prev = lambda n: jnp.maximum(n - 1, 0)
    nxt = lambda n: jnp.minimum(n + 1, nblk - 1)

    def body(qc_ref, qn_ref, kp_ref, kc_ref, vp_ref, vc_ref, doc_ref, don_ref, oc_ref, on_ref, lc_ref, ln_ref,
             dq_ref, dk_ref, dv_ref):
        n = pl.program_id(1)
        has_prev, has_next = n > 0, n < nblk - 1
        ri = lax.broadcasted_iota(jnp.int32, (ATT_BLK, ATT_BLK), 0)
        ci = lax.broadcasted_iota(jnp.int32, (ATT_BLK, ATT_BLK), 1)
        m_cur = ci <= ri
        m_prev = (ci >= ri) & has_prev
        m_next = (ci >= ri) & has_next
        qc, qn, kp, kc, vp, vc = qc_ref[...], qn_ref[...], kp_ref[...], kc_ref[...], vp_ref[...], vc_ref[...]
        doc, don = doc_ref[...], don_ref[...]
        pc_full = doc.astype(F32) * oc_ref[...].astype(F32)
        pn_full = don.astype(F32) * on_ref[...].astype(F32)
        lc, lnx = lc_ref[...], ln_ref[...]
        dq = jnp.zeros((ATT_BLK, 256), F32)
        dk = jnp.zeros((ATT_BLK, 256), F32)
        dv = jnp.zeros((ATT_BLK, 256), F32)
        zb = jnp.zeros_like(qc)
        for hm in _head_masks():
            qcm, qnm = jnp.where(hm, qc, zb), jnp.where(hm, qn, zb)
            docm, donm = jnp.where(hm, doc, zb), jnp.where(hm, don, zb)
            lse_c = jnp.max(jnp.where(hm, lc, NEG), axis=1, keepdims=True)
            lse_n = jnp.max(jnp.where(hm, lnx, NEG), axis=1, keepdims=True)
            del_c = jnp.sum(jnp.where(hm, pc_full, 0.0), axis=1, keepdims=True)
            del_n = jnp.sum(jnp.where(hm, pn_full, 0.0), axis=1, keepdims=True)
            pr_ = jnp.where(m_cur, jnp.exp(_dg(qcm, kc, 1, 1) * 0.125 - lse_c), 0.0)
            ds = (pr_ * (_dg(docm, vc, 1, 1) - del_c) * 0.125).astype(BF16)
            dqh = _dg(ds, kc, 1, 0)
            dkh = _dg(ds, qc, 0, 0)
            dvh = _dg(pr_.astype(BF16), doc, 0, 0)
            pr_ = jnp.where(m_prev, jnp.exp(_dg(qcm, kp, 1, 1) * 0.125 - lse_c), 0.0)
            ds = (pr_ * (_dg(docm, vp, 1, 1) - del_c) * 0.125).astype(BF16)
            dqh = dqh + _dg(ds, kp, 1, 0)
            pr_ = jnp.where(m_next, jnp.exp(_dg(qnm, kc, 1, 1) * 0.125 - lse_n), 0.0)
            ds = (pr_ * (_dg(donm, vc, 1, 1) - del_n) * 0.125).astype(BF16)
            dkh = dkh + _dg(ds, qn, 0, 0)
            dvh = dvh + _dg(pr_.astype(BF16), don, 0, 0)
            dq = jnp.where(hm, dqh, dq)
            dk = jnp.where(hm, dkh, dk)
            dv = jnp.where(hm, dvh, dv)
        dq_ref[...] = dq.astype(dq_ref.dtype)
        dk_ref[...] = dk.astype(dk_ref.dtype)
        dv_ref[...] = dv.astype(dv_ref.dtype)

    blk = (ATT_BLK, 256)
    cur = lambda col: pl.BlockSpec(blk, lambda pr, n: (n, col(pr)))
    prv = lambda col: pl.BlockSpec(blk, lambda pr, n: (prev(n), col(pr)))
    nx = lambda col: pl.BlockSpec(blk, lambda pr, n: (nxt(n), col(pr)))
    own = lambda pr: pr
    outs = pl.pallas_call(
        body, name=f"attn_bwd_{g}", grid=(r, nblk),
        in_specs=[cur(qcol), nx(qcol), prv(qcol), cur(qcol), prv(vcol), cur(vcol),
                  cur(own), nx(own), cur(own), nx(own), cur(own), nx(own)],
        out_specs=[cur(own), cur(own), cur(own)],
        out_shape=[jax.ShapeDtypeStruct((ln, r * 256), BF16)] * 3,
        compiler_params=_params(("parallel", "parallel")),
    )(qv, qv, kv, kv, pv, pv, dov, dov, ov, ov, lv, lv)
    return outs


def _gelu_parts(gv):
    cdf = 0.5 * (1.0 + lax.erf(gv * (2.0 ** -0.5)))
    pdf = jnp.exp(-0.5 * gv * gv) * (1.0 / math.sqrt(2.0 * math.pi))
    return cdf, pdf


def _pick_row(t, k):
    row = lax.broadcasted_iota(jnp.int32, t.shape, 0)
    return jnp.sum(jnp.where(row == k, t, 0.0), axis=0, keepdims=True)


def _shift_rows(u, halo, n):
    row = lax.broadcasted_iota(jnp.int32, u.shape, 0)
    out = pltpu.roll(u, n, 0)
    for k in range(n):
        out = jnp.where(row == k, _pick_row(halo, 16 - n + k), out)
    return out


def _shift_rows_up(u, halo, n):
    rb = u.shape[0]
    row = lax.broadcasted_iota(jnp.int32, u.shape, 0)
    out = pltpu.roll(u, rb - n, 0)
    for k in range(n):
        out = jnp.where(row == rb - n + k, _pick_row(halo, k), out)
    return out


def _conv(u, halo, cw, cb):
    return cb + _pick_row(cw, 0) * _shift_rows(u, halo, 2) + _pick_row(cw, 1) * _shift_rows(u, halo, 1) + _pick_row(cw, 2) * u


def _local_step(x, mod, pos_col, target, sm, w_sh, g0, chip, core):
    s = x.shape[0]
    shift1, scale1, gate1, shift2, scale2, gate2 = [mod[i:i + 1, :] for i in range(6)]
    rb = 512
    chip1 = chip.reshape(1)

    def f_norm1(c, i, xv, nw, sc, sh):
        return ((xv * _rms(xv) * nw) * (1.0 + sc) + sh,)

    (h,) = _rowcall(f_norm1, [_rows(x, rb), _full(sm["n1w"]), _full(scale1), _full(shift1)],
                    [_orow(s, D, BF16, rb)], n_rows=s, rb=rb, name="norm1")
    own = lambda got, i: lax.dynamic_update_slice(got, w_sh[i], (chip, 0, 0))
    invf = jnp.tile(ROPE_THETA ** (-jnp.arange(ATT_HD // 2, dtype=F32) / (ATT_HD // 2)), 4).reshape(1, 128)
    cos_t, sin_t = _rope_tables(pos_col, invf, s)
    _, got0 = _unit_wait(*g0[:4], after=[h, cos_t, sin_t], name="gather_w_in_wait")
    [got0] = _comm_call("gather_w_in_d2d", [_u_gather_d2d(got0, (0,))])
    u_g1 = _u_gather_ici(w_sh, (1, 2, 3, 4, 5))
    g1 = _unit_start(u_g1, "gather_weights_start", after=got0)
    w = dict(win=_win_assemble(own(got0[0], 0), after=g1[3:]))
    p = _mm(h, w["win"], "in_proj", tm=2048, tn=1536)

    def f_gla_pre(c, i, glr, w2, gb):
        z = _dg(glr, w2.astype(BF16), 1, 0) + gb
        return ((jnp.minimum(z, 0.0) - jnp.log(1.0 + jnp.exp(-jnp.abs(z)))) * (1.0 / GLA_TAU),)

    (la,) = _rowcall(f_gla_pre, [_rows(p, rb, 128, P_LR // 128), _full(sm["w2"]), _full(sm["gb"])],
                     [_orow(s, 512, F32, rb)], n_rows=s, rb=rb, name="gla_pre")
    o_gla, states, _ = _gla_fwd(p, la, s)
    _, got = _unit_wait(u_g1, *g1[:3], after=[o_gla], name="gather_weights_wait")
    [got123] = _comm_call("gather_weights_d2d", [_u_gather_d2d(got[:3], (1, 2, 3))])
    got45 = got[3:]
    w.update(wgb=own(got123[0], 1).reshape(1024, D), wab=_cols_join(own(got123[1], 2)), wout=own(got123[2], 3).reshape(D, D))

    def f_gla_post(c, i, ov, gnw, gr):
        on = jnp.concatenate([ov[:, k * 256:(k + 1) * 256] * _rms(ov[:, k * 256:(k + 1) * 256]) * gnw
                              for k in range(GLA_H)], axis=1)
        g = gr.astype(F32)
        return (on * (g * _sigmoid(g)),)

    (og,) = _rowcall(f_gla_post, [_rows(o_gla, rb), _full(sm["gnw"]), _rows(p, rb, 1024, P_GR // 1024)],
                     [_orow(s, 1024, BF16, rb)], n_rows=s, rb=rb, name="gla_post")
    y_gla = _mm(og, w["wgb"], "gla_branch")

    q_d, k_d, v_d = _rope_fwd(p, cos_t, sin_t, s)
    att = [_attn_fwd(q_d[g], k_d[g], v_d[g], g, r, s) for g, r in enumerate(_RS)]
    o_att, lse, o_d1, o_d2, lse_d1, lse_d2 = _attn_combine(att, s)
    y_att = _mm(o_att, w["wab"], "attn_branch")

    def f_merge(c, i, ma, mb, yg, ya):
        return (_sigmoid(ma.astype(F32)) * yg.astype(F32) + _sigmoid(mb.astype(F32)) * ya.astype(F32),)

    (mixed,) = _rowcall(f_merge, [_rows(p, rb, D, P_MA // D), _rows(p, rb, D, P_MB // D), _rows(y_gla, rb), _rows(y_att, rb)],
                        [_orow(s, D, BF16, rb)], n_rows=s, rb=rb, name="merge")
    z1, [got45] = _mm(mixed, w["wout"], "out_proj", comm=[_u_gather_d2d(got45, (4, 5))])
    w.update(wup=own(got45[0], 4), wdown=own(got45[1], 5).reshape(D_FF, D))

    def f_norm2(c, i, xv, z, g1, nw, sc, sh):
        x1 = xv + g1 * z.astype(F32)
        return (x1, (x1 * _rms(x1) * nw) * (1.0 + sc) + sh)

    x1, h2 = _rowcall(f_norm2, [_rows(x, rb), _rows(z1, rb), _full(gate1), _full(sm["n2w"]), _full(scale2), _full(shift2)],
                      [_orow(s, D, F32, rb), _orow(s, D, BF16, rb)], n_rows=s, rb=rb, name="norm2")
    u = _mm(h2, w["wup"], "up_proj", tm=2048, b_shards=True)

    cwid = 2 * W_UP_SH

    def f_ffn(c, i, uv, hl, cw, cb):
        uc = _conv(uv.astype(F32), hl.astype(F32) * (i > 0).astype(F32), cw, cb)
        val, gt = uc[:, :W_UP_SH], uc[:, W_UP_SH:]
        cdf, _ = _gelu_parts(gt)
        return (gt * cdf * val,)

    ccol = lambda c: c
    rw = 256
    (hidden,) = _rowcall(f_ffn, [_rows(u, rw, cwid, ccol), _halo(u, rw, 16, cwid, ccol, True),
                                 _full(sm["cw"], cwid, ccol), _full(sm["cb"], cwid, ccol)],
                         [_orow(s, D_FF, BF16, rw, W_UP_SH, ccol)], n_rows=s, rb=rw, name="conv_geglu", ncol=2)
    z2 = _mm(hidden, w["wdown"], "down_proj", tk=D_FF)

    def f_final(c, i, x1v, z, g2, fw, tgt):
        x2 = x1v + g2 * z.astype(F32)
        r = _rms(x2)
        xh = x2 * r
        e = xh * fw - tgt
        loss = 0.5 * jnp.sum(jnp.mean(e * e, axis=-1, keepdims=True), axis=0, keepdims=True)
        dy = e * (1.0 / D)
        dxh = dy * fw
        dx2 = r * (dxh - xh * jnp.mean(dxh * xh, axis=-1, keepdims=True))
        return (loss, dx2, dx2 * g2, _csum(dy * xh), _csum(dx2 * z.astype(F32)))

    loss, dx2, dz2, d_fnw, d_gate2 = _rowcall(
        f_final, [_rows(x1, rb), _rows(z2, rb), _full(gate2), _full(sm["fnw"]), _rows(target, rb)],
        [_oacc(1, 1), _orow(s, D, F32, rb), _orow(s, D, BF16, rb), _oacc(1, D), _oacc(1, D)],
        n_rows=s, rb=rb, name="final_loss")
    d_hidden = _mm(dz2, w["wdown"], "down_proj_dx", tb=True, tn=1408)
    g_wdown = _mm(hidden, dz2, "down_proj_dw", ta=True, out_dtype=F32, tm=1408, tn=1024, tk=2048)

    def f_ffn_bwd(c, i, uv, hl, dh, cw, cb):
        uf = uv.astype(F32)
        hf = hl.astype(F32) * (i > 0).astype(F32)
        u1, u2 = _shift_rows(uf, hf, 1), _shift_rows(uf, hf, 2)
        uc = cb + _pick_row(cw, 0) * u2 + _pick_row(cw, 1) * u1 + _pick_row(cw, 2) * uf
        val, gt = uc[:, :W_UP_SH], uc[:, W_UP_SH:]
        cdf, pdf = _gelu_parts(gt)
        dhf = dh.astype(F32)
        duc = jnp.concatenate([dhf * (gt * cdf), dhf * val * (cdf + gt * pdf)], axis=1)
        dcw = jnp.concatenate([_csum(duc * u2), _csum(duc * u1), _csum(duc * uf)], axis=0)
        return (duc, _csum(duc), dcw)

    duc, d_cb, d_cw = _rowcall(
        f_ffn_bwd, [_rows(u, rw, cwid, ccol), _halo(u, rw, 16, cwid, ccol, True), _rows(d_hidden, rw, W_UP_SH, ccol),
                    _full(sm["cw"], cwid, ccol), _full(sm["cb"], cwid, ccol)],
        [_orow(s, 2 * D_FF, BF16, rw, cwid, ccol), _oacc(1, 2 * D_FF, cwid, ccol), _oacc(3, 2 * D_FF, cwid, ccol)],
        n_rows=s, rb=rw, name="conv_geglu_bwd", ncol=2)

    def f_conv_t(c, i, dv, hl, cw):
        df = dv.astype(F32)
        hf = hl.astype(F32) * (i < s // rw - 1).astype(F32)
        return (_pick_row(cw, 2) * df + _pick_row(cw, 1) * _shift_rows_up(df, hf, 1) + _pick_row(cw, 0) * _shift_rows_up(df, hf, 2),)

    (du,) = _rowcall(f_conv_t, [_rows(duc, rw, cwid, ccol), _halo(duc, rw, 16, cwid, ccol, False), _full(sm["cw"], cwid, ccol)],
                     [_orow(s, 2 * D_FF, BF16, rw, cwid, ccol)], n_rows=s, rb=rw, name="conv_transpose", ncol=2)
    g_wup = _mm(h2, du, "up_proj_dw", ta=True, out_dtype=F32, tm=1024, tk=2048, o_shards=True)
    gs45 = [g_wup, g_wdown.reshape(4, W_DOWN_SH, 1024)]
    d_h2, [land45] = _mm(du, w["wup"], "up_proj_dx", tb=True, b_shards=True, comm=[_u_pair_send(gs45, (4, 5))])
    ts45 = [_pair_add(g, ld, core, "grad_pair_add_" + BIG[i]) for g, ld, i in zip(gs45, land45, (4, 5))]

    def f_norm2_bwd(c, i, x1v, dh, dxr, z, nw, sc, g1):
        dxn, dsh, dsc, dnw = _norm_bwd(x1v, dh.astype(F32), nw, sc)
        dx1 = dxr + dxn
        return (dx1, dx1 * g1, dsh, dsc, dnw, _csum(dx1 * z.astype(F32)))

    dx1, dz1, d_shift2, d_scale2, d_n2w, d_gate1 = _rowcall(
        f_norm2_bwd, [_rows(x1, rb), _rows(d_h2, rb), _rows(dx2, rb), _rows(z1, rb), _full(sm["n2w"]), _full(scale2), _full(gate1)],
        [_orow(s, D, F32, rb), _orow(s, D, BF16, rb), _oacc(1, D), _oacc(1, D), _oacc(1, D), _oacc(1, D)],
        n_rows=s, rb=rb, name="norm2_bwd")
    d_mixed = _mm(dz1, w["wout"], "out_proj_dx", tb=True)
    g_wout = _mm(mixed, dz1, "out_proj_dw", ta=True, out_dtype=F32, tk=2048)

    def f_merge_bwd(c, i, dm, ma, mb, yg, ya):
        dmf, ygf, yaf = dm.astype(F32), yg.astype(F32), ya.astype(F32)
        sa, sb = _sigmoid(ma.astype(F32)), _sigmoid(mb.astype(F32))
        return (dmf * sa, dmf * sb, jnp.concatenate([dmf * ygf * sa * (1.0 - sa), dmf * yaf * sb * (1.0 - sb)], axis=1))

    dy_gla, dy_att, dp = _rowcall(
        f_merge_bwd, [_rows(d_mixed, rb), _rows(p, rb, D, P_MA // D), _rows(p, rb, D, P_MB // D), _rows(y_gla, rb), _rows(y_att, rb)],
        [_orow(s, D, BF16, rb)] * 2 + [_orow(s, P_W, BF16, rb, 2 * D, lambda c: P_MA // (2 * D))], n_rows=s, rb=rb, name="merge_bwd")
    d_og = _mm(dy_gla, w["wgb"], "gla_branch_dx", tb=True)
    g_wgb = _mm(og, dy_gla, "gla_branch_dw", ta=True, out_dtype=F32, tk=2048)
    d_oatt = _mm(dy_att, w["wab"], "attn_branch_dx", tb=True)
    g_wab = _mm(o_att, dy_att, "attn_branch_dw", ta=True, out_dtype=F32, tk=2048)

    def f_gla_post_bwd(c, i, ov, gnw, gr, dog):
        g = gr.astype(F32)
        sg = _sigmoid(g)
        silu = g * sg
        dof = dog.astype(F32)
        don = dof * silu
        on_parts, do_parts, dgn = [], [], jnp.zeros((1, 256), F32)
        for k in range(GLA_H):
            oh = ov[:, k * 256:(k + 1) * 256]
            dh = don[:, k * 256:(k + 1) * 256]
            r = _rms(oh)
            xh = oh * r
            dgn = dgn + _csum(dh * xh)
            dxh = dh * gnw
            do_parts.append(r * (dxh - xh * jnp.mean(dxh * xh, axis=-1, keepdims=True)))
            on_parts.append(xh * gnw)
        on = jnp.concatenate(on_parts, axis=1)
        dgr = dof * on * (sg * (1.0 + g * (1.0 - sg)))
        return (jnp.concatenate(do_parts, axis=1), dgr, dgn)

    do_gla, dp, d_gnw = _rowcall(
        f_gla_post_bwd, [_rows(o_gla, rb), _full(sm["gnw"]), _rows(p, rb, 1024, P_GR // 1024), _rows(d_og, rb)],
        [_orow(s, 1024, F32, rb), _orow(s, P_W, BF16, rb, 1024, lambda c: P_GR // 1024), _oacc(1, 256)],
        n_rows=s, rb=rb, name="gla_post_bwd", into=(dp, 1))
    gs123 = [g_wgb.reshape(4, 256, 1024), _cols_split(g_wab), g_wout.reshape(4, 256, 1024)]
    d_gq, d_gk, dp, d_la, [r4, land123] = _gla_bwd(p, la, states, do_gla, s, dp,
                                                   comm=[_u_chip_exchange(ts45[:1]), _u_pair_send(gs123, (1, 2, 3))])
    half4 = [_chip_sum(ts45[0], r4[0], chip1, "grad_chip_sum_w_up")]
    ts123 = [_pair_add(g, ld, core, "grad_pair_add_" + BIG[i]) for g, ld, i in zip(gs123, land123, (1, 2, 3))]

    def f_gla_pre_bwd(c, i, lav, dlav, glr, w2):
        dz = dlav * (1.0 / GLA_TAU) * (1.0 - jnp.exp(GLA_TAU * lav))
        dzb = dz.astype(BF16)
        return (_dg(dzb, w2.astype(BF16), 1, 1), _csum(dz), _dg(glr, dzb, 0, 0))

    d_glr, d_gb, d_w2 = _rowcall(
        f_gla_pre_bwd, [_rows(la, rb), _rows(d_la, rb), _rows(p, rb, 128, P_LR // 128), _full(sm["w2"])],
        [_orow(s, 128, BF16, rb), _oacc(1, 512), _oacc(128, 512)], n_rows=s, rb=rb, name="gla_pre_bwd")

    do_d = [d_oatt] + list(_dilate(d_oatt, s))
    datt = [_attn_bwd(q_d[g], k_d[g], v_d[g], do_d[g], (o_att, o_d1, o_d2)[g], (lse, lse_d1, lse_d2)[g], g, r, s)
            for g, r in enumerate(_RS)]
    dp = _rope_bwd(datt, d_glr, dp, cos_t, sin_t, s)
    dp = lax.dynamic_update_slice(dp, jnp.concatenate([d_gq, d_gk], axis=1), (0, P_GQ))
    g_win, [r1235, oth4] = _mm(h, dp, "in_proj_dw", ta=True, out_dtype=F32, tm=1024, tn=1536, tk=2048,
                               comm=[_u_chip_exchange(ts123 + ts45[1:]), _u_pair_join(half4)])
    half1235 = [_chip_sum(t, r, chip1, "grad_chip_sum_" + BIG[i]) for t, r, i in zip(ts123 + ts45[1:], r1235, (1, 2, 3, 5))]
    gs0 = [_win_split(g_win)]
    d_h, [land0, oth1235] = _mm(dp, w["win"], "in_proj_dx", tb=True, tk=3840,
                                comm=[_u_pair_send(gs0, (0,)), _u_pair_join(half1235)])
    half123, half45 = half1235[:3], half4 + half1235[3:]
    oth123, oth45 = oth1235[:3], oth4 + oth1235[3:]
    ts0 = _pair_add(gs0[0], land0[0], core, "grad_pair_add_w_in")

    def f_norm1_bwd(c, i, xv, dh, dxr, nw, sc):
        dxn, dsh, dsc, dnw = _norm_bwd(xv, dh.astype(F32), nw, sc)
        return (dxr + dxn, dsh, dsc, dnw)

    grad_x, d_shift1, d_scale1, d_n1w = _rowcall(
        f_norm1_bwd, [_rows(x, rb), _rows(d_h, rb), _rows(dx1, rb), _full(sm["n1w"]), _full(scale1)],
        [_orow(s, D, F32, rb), _oacc(1, D), _oacc(1, D), _oacc(1, D)], n_rows=s, rb=rb, name="norm1_bwd")

    dmod = jnp.concatenate([d_shift1, d_scale1, d_gate1, d_shift2, d_scale2, d_gate2], axis=1)
    small = dict(dmod=dmod, n1w=d_n1w, gb=d_gb, gnw=d_gnw, n2w=d_n2w, cb=d_cb, fnw=d_fnw, w2=d_w2, cw=d_cw)
    return loss, grad_x, half123 + half45, oth123 + oth45, small, ts0


def _win_pieces():
    runs = [(P_GV, 1024, 2048), (P_MA, 5392, 2048), (P_GQ, 0, 1024), (P_AQ, 3088, 2304), (P_LR, 3072, GLA_LR)]
    out = []
    for kc, rc, ln in runs:
        while ln > 0:
            step = min(ln, W_IN_SH - rc % W_IN_SH)
            out.append((kc, rc, step))
            kc, rc, ln = kc + step, rc + step, ln - step
    return out


def _win_assemble(shards, after=()):
    rb = 256

    def body(s_ref, *rest):
        o_ref = rest[-1]
        o_ref[:, W_IN:] = jnp.zeros((rb, P_W - W_IN), o_ref.dtype)
        for kc, rc, ln in _win_pieces():
            o_ref[:, kc:kc + ln] = s_ref[rc // W_IN_SH, :, rc % W_IN_SH:rc % W_IN_SH + ln]

    return pl.pallas_call(
        body, name="w_in_assemble", grid=(D // rb,),
        in_specs=[pl.BlockSpec((4, rb, W_IN_SH), lambda i: (0, i, 0))] + [pl.BlockSpec(memory_space=pl.ANY)] * len(after),
        out_specs=pl.BlockSpec((rb, P_W), lambda i: (i, 0)),
        out_shape=jax.ShapeDtypeStruct((D, P_W), shards.dtype), compiler_params=_params(("parallel",)),
    )(shards, *after)


def _win_split(g):
    rb = 256

    def body(g_ref, o_ref):
        for kc, rc, ln in _win_pieces():
            o_ref[rc // W_IN_SH, :, rc % W_IN_SH:rc % W_IN_SH + ln] = g_ref[:, kc:kc + ln]

    return pl.pallas_call(
        body, name="w_in_grad_split", grid=(D // rb,),
        in_specs=[pl.BlockSpec((rb, P_W), lambda i: (i, 0))], out_specs=pl.BlockSpec((4, rb, W_IN_SH), lambda i: (0, i, 0)),
        out_shape=jax.ShapeDtypeStruct((4, D, W_IN_SH), g.dtype), compiler_params=_params(("parallel",)),
    )(g)


def _ff_to_kernel(a):
    h = W_UP_SH
    return jnp.concatenate([a[:, 0:h], a[:, D_FF:D_FF + h], a[:, h:D_FF], a[:, D_FF + h:]], axis=1)


def _ff_from_kernel(a):
    h = W_UP_SH
    return jnp.concatenate([a[:, 0:h], a[:, 2 * h:3 * h], a[:, h:2 * h], a[:, 3 * h:]], axis=1)


BIG = ("w_in", "w_gla_branch", "w_attn_branch", "w_out", "w_up", "w_down")
SH_SHAPES = ((1024, W_IN_SH), (256, 1024), (256, 256), (256, 1024), (1024, W_UP_SH), (W_DOWN_SH, 1024))
N_BIG = len(BIG)


def _cols_join(t):
    return jnp.concatenate([t[k] for k in range(4)], axis=1)


def _cols_split(t):
    cols = t.shape[1] // 4
    return jnp.stack([t[:, k * cols:(k + 1) * cols] for k in range(4)])


def _me():
    return lax.axis_index("x"), lax.axis_index("y"), lax.axis_index("c")


HBM = pl.BlockSpec(memory_space=pltpu.HBM)
VMEM_SPEC = pl.BlockSpec(memory_space=pltpu.VMEM)


def _allgather8(xs, name):
    rows = xs.shape[0]

    def body(x_ref, out_ref, send_sems, recv_sems, local_sem):
        x, y, c = _me()
        me = 4 * x + 2 * y + c
        mine = pltpu.make_async_copy(x_ref, out_ref.at[me], local_sem)
        mine.start()
        flips = [(k >> 2 & 1, k >> 1 & 1, k & 1) for k in range(1, 8)]

        def peer(f):
            return (jnp.where(f[0] == 1, 1 - x, x), jnp.where(f[1] == 1, 1 - y, y), jnp.where(f[2] == 1, 1 - c, c))

        sends = []
        for k, f in enumerate(flips):
            cp = pltpu.make_async_remote_copy(src_ref=x_ref, dst_ref=out_ref.at[me], send_sem=send_sems.at[k],
                                              recv_sem=recv_sems.at[k], device_id=peer(f), device_id_type=MESH)
            cp.start()
            sends.append(cp)
        for k, f in enumerate(flips):
            px, py, pc = peer(f)
            pltpu.make_async_remote_copy(src_ref=x_ref, dst_ref=out_ref.at[4 * px + 2 * py + pc], send_sem=send_sems.at[k],
                                         recv_sem=recv_sems.at[k], device_id=peer(f), device_id_type=MESH).wait_recv()
        for cp in sends:
            cp.wait_send()
        mine.wait()

    return pl.pallas_call(
        body, name=name, out_shape=jax.ShapeDtypeStruct((8, rows, 128), F32),
        in_specs=[VMEM_SPEC], out_specs=VMEM_SPEC,
        scratch_shapes=[pltpu.SemaphoreType.DMA((7,)), pltpu.SemaphoreType.DMA((7,)), pltpu.SemaphoreType.DMA],
        compiler_params=pltpu.CompilerParams(vmem_limit_bytes=VMEM_LIMIT),
    )(xs)


def _half_rows(i, cc, unit):
    rows = SH_SHAPES[i][0] // 2
    return pl.ds(pl.multiple_of(cc * rows, unit), rows)


def _rc(src, dst, sems, to):
    return pltpu.make_async_remote_copy(src_ref=src, dst_ref=dst, send_sem=sems[0], recv_sem=sems[1], device_id=to, device_id_type=MESH)


def _other_chips(x, y):
    return [(1 - x, y), (x, 1 - y), (1 - x, 1 - y)]


def _u_gather_ici(w_sh, idxs):
    def copies(ins, outs, sem):
        x, y, c = _me()
        res = []
        for j, (px, py) in enumerate(_other_chips(x, y)):
            for n, i in enumerate(idxs):
                src = ins[n].at[0, _half_rows(i, c, 16)]
                res.append((_rc(src, outs[n].at[2 * x + y, _half_rows(i, c, 16)], sem(j * len(idxs) + n), (px, py, c)),
                            _rc(src, outs[n].at[2 * px + py, _half_rows(i, c, 16)], sem(j * len(idxs) + n), (px, py, c))))
        return res

    return dict(ins=[w_sh[i] for i in idxs], outs=[jax.ShapeDtypeStruct((4,) + SH_SHAPES[i], BF16) for i in idxs],
                nsem=3 * len(idxs), alias={}, copies=copies)


def _u_gather_d2d(got, idxs):
    def copies(ins, outs, sem):
        x, y, c = _me()
        res = []
        for j, (px, py) in enumerate(_other_chips(x, y)):
            for n, i in enumerate(idxs):
                src = ins[n].at[2 * px + py, _half_rows(i, c, 16)]
                res.append((_rc(src, outs[n].at[2 * px + py, _half_rows(i, c, 16)], sem(j * len(idxs) + n), (x, y, 1 - c)),
                            _rc(src, outs[n].at[2 * px + py, _half_rows(i, 1 - c, 16)], sem(j * len(idxs) + n), (x, y, 1 - c))))
        return res

    return dict(ins=list(got), outs=[jax.ShapeDtypeStruct(g.shape, g.dtype) for g in got], nsem=3 * len(idxs),
                alias={n: n for n in range(len(idxs))}, copies=copies)


def _u_pair_send(gs, idxs):
    def copies(ins, outs, sem):
        x, y, c = _me()
        res = []
        for n, i in enumerate(idxs):
            for sh in range(4):
                cp = _rc(ins[n].at[sh, _half_rows(i, 1 - c, 8)], outs[n].at[sh], sem(4 * n + sh), (x, y, 1 - c))
                res.append((cp, cp))
        return res

    return dict(ins=list(gs), outs=[jax.ShapeDtypeStruct((4, SH_SHAPES[i][0] // 2, SH_SHAPES[i][1]), F32) for i in idxs],
                nsem=4 * len(idxs), alias={}, copies=copies)


def _u_chip_exchange(ts):
    def copies(ins, outs, sem):
        x, y, c = _me()
        res = []
        for j, (px, py) in enumerate(_other_chips(x, y)):
            for n in range(len(ts)):
                cp = _rc(ins[n].at[2 * px + py], outs[n].at[j], sem(j * len(ts) + n), (px, py, c))
                res.append((cp, cp))
        return res

    return dict(ins=list(ts), outs=[jax.ShapeDtypeStruct((3,) + t.shape[1:], t.dtype) for t in ts], nsem=3 * len(ts),
                alias={}, copies=copies)


def _u_pair_join(hs):
    def copies(ins, outs, sem):
        x, y, c = _me()
        res = []
        for n in range(len(hs)):
            cp = _rc(ins[n], outs[n], sem(n), (x, y, 1 - c))
            res.append((cp, cp))
        return res

    return dict(ins=list(hs), outs=[jax.ShapeDtypeStruct(h.shape, h.dtype) for h in hs], nsem=len(hs), alias={}, copies=copies)


def _comm_phase(units, ci, co, send_sems, recv_sems, start):
    ii = oo = off = 0
    for u in units:
        ni, no = len(u["ins"]), len(u["outs"])
        for st, arrival in u["copies"](ci[ii:ii + ni], co[oo:oo + no], lambda k, off=off: (send_sems.at[off + k], recv_sems.at[off + k])):
            if start:
                st.start()
            else:
                st.wait_send()
                arrival.wait_recv()
        ii, oo, off = ii + ni, oo + no, off + u["nsem"]


def _carry(units, n_in, n_out):
    ins = [a for u in units for a in u["ins"]]
    outs = [o for u in units for o in u["outs"]]
    alias, ii, oo = {}, 0, 0
    for u in units:
        for a, b in u["alias"].items():
            alias[n_in + ii + a] = n_out + oo + b
        ii, oo = ii + len(u["ins"]), oo + len(u["outs"])
    nsem = sum(u["nsem"] for u in units)
    scratch = [pltpu.SemaphoreType.DMA((nsem,)), pltpu.SemaphoreType.DMA((nsem,))] if units else []
    return ins, outs, alias, scratch


def _split_units(units, res):
    out, oo = [], 0
    for u in units:
        out.append(list(res[oo:oo + len(u["outs"])]))
        oo += len(u["outs"])
    return out


def _comm_call(name, units):
    ins, outs, alias, scratch = _carry(units, 0, 0)

    def body(*refs):
        ci, co = refs[:len(ins)], refs[len(ins):len(ins) + len(outs)]
        _comm_phase(units, ci, co, refs[-2], refs[-1], True)
        _comm_phase(units, ci, co, refs[-2], refs[-1], False)

    res = pl.pallas_call(body, name=name, out_shape=outs, in_specs=[HBM] * len(ins), out_specs=[HBM] * len(outs),
                         scratch_shapes=scratch, input_output_aliases=alias)(*ins)
    return _split_units(units, res)


SEM = pl.BlockSpec(memory_space=pltpu.SEMAPHORE)
EFFECT = pltpu.SideEffectType.DATAFLOW_SIDE_EFFECTING


def _unit_start(unit, name, after=()):
    bufs = list(unit["ins"]) + [lax.empty(o.shape, o.dtype) for o in unit["outs"]]
    n_i, n_b, ns = len(unit["ins"]), len(bufs), unit["nsem"]

    def body(*refs):
        send_sems, recv_sems = refs[n_b + len(after)], refs[n_b + len(after) + 1]
        for st, _ in unit["copies"](refs[:n_i], refs[n_i:n_b], lambda k: (send_sems.at[k], recv_sems.at[k])):
            st.start()
        refs[-1][...] = jnp.zeros_like(refs[-1])

    res = pl.pallas_call(
        body, name=name,
        out_shape=[pltpu.SemaphoreType.DMA((ns,)), pltpu.SemaphoreType.DMA((ns,))] + [pltpu.HBM(b.shape, b.dtype) for b in bufs]
        + [jax.ShapeDtypeStruct((8, 128), F32)],
        in_specs=[HBM] * n_b + [pl.BlockSpec(memory_space=pl.ANY)] * len(after), out_specs=[SEM, SEM] + [HBM] * n_b + [VMEM_SPEC],
        input_output_aliases={i: 2 + i for i in range(n_b)},
        compiler_params=pltpu.CompilerParams(has_side_effects=EFFECT),
    )(*[pltpu.with_memory_space_constraint(b, pltpu.HBM) for b in bufs], *after)
    return res[0], res[1], list(res[2:2 + n_b]), res[-1]


def _unit_wait(unit, send_sems, recv_sems, bufs, after, name):
    n_i, n_b = len(unit["ins"]), len(bufs)

    def body(*refs):
        ss, rs = refs[n_b], refs[n_b + 1]
        for st, arrival in unit["copies"](refs[:n_i], refs[n_i:n_b], lambda k: (ss.at[k], rs.at[k])):
            st.wait_send()
            arrival.wait_recv()

    res = pl.pallas_call(
        body, name=name, out_shape=[pltpu.HBM(b.shape, b.dtype) for b in bufs],
        in_specs=[HBM] * n_b + [SEM, SEM] + [pl.BlockSpec(memory_space=pl.ANY)] * len(after), out_specs=[HBM] * n_b,
        input_output_aliases={i: i for i in range(n_b)}, compiler_params=pltpu.CompilerParams(has_side_effects=EFFECT),
    )(*bufs, send_sems, recv_sems, *after)
    return list(res[:n_i]), list(res[n_i:])


def _pair_add(g, land, core, name):
    _, rows, cols = g.shape
    half = rows // 2
    rb = _tile(half, 256, 16)
    nb = half // rb

    def body(c_ref, g_ref, l_ref, o_ref):
        o_ref[...] = (g_ref[...] + l_ref[...]).astype(BF16)

    return pl.pallas_call(
        body, name=name,
        grid_spec=pltpu.PrefetchScalarGridSpec(
            num_scalar_prefetch=1, grid=(4, nb),
            in_specs=[pl.BlockSpec((1, rb, cols), lambda s, i, c_ref: (s, c_ref[0] * nb + i, 0)),
                      pl.BlockSpec((1, rb, cols), lambda s, i, c_ref: (s, i, 0))],
            out_specs=pl.BlockSpec((1, rb, cols), lambda s, i, c_ref: (s, i, 0))),
        out_shape=jax.ShapeDtypeStruct((4, half, cols), BF16),
        compiler_params=_params(("parallel", "parallel")),
    )(core, g, land)


def _chip_sum(t, r, chip, name):
    _, half, cols = t.shape
    rb = _tile(half, 256, 16)

    def body(s_ref, t_ref, r_ref, o_ref):
        o_ref[...] = ((t_ref[0].astype(F32) + r_ref[0].astype(F32)) + r_ref[1].astype(F32)) + r_ref[2].astype(F32)

    return pl.pallas_call(
        body, name=name,
        grid_spec=pltpu.PrefetchScalarGridSpec(
            num_scalar_prefetch=1, grid=(half // rb,),
            in_specs=[pl.BlockSpec((1, rb, cols), lambda i, s_ref: (s_ref[0], i, 0)),
                      pl.BlockSpec((3, rb, cols), lambda i, s_ref: (0, i, 0))],
            out_specs=pl.BlockSpec((rb, cols), lambda i, s_ref: (i, 0))),
        out_shape=jax.ShapeDtypeStruct((half, cols), F32),
        compiler_params=_params(("parallel",)),
    )(chip, t, r)


def _adam_math(wv, gv, mv, vv):
    mn = ADAM_B1 * mv + (1.0 - ADAM_B1) * gv
    vn = ADAM_B2 * vv + (1.0 - ADAM_B2) * (gv * gv)
    m_hat = mn / (1.0 - ADAM_B1 ** ADAM_STEP)
    v_hat = vn / (1.0 - ADAM_B2 ** ADAM_STEP)
    return -ADAM_LR * (m_hat / (jnp.sqrt(v_hat) + ADAM_EPS) + ADAM_WD * wv), mn, vn


def _adamw_halves(wt, mt, vt, mine, theirs, core, name):
    _, rows, cols = wt.shape
    half = rows // 2
    rb = _tile(half, 256, 8)
    nb = half // rb

    def body(c_ref, w_ref, m_ref, v_ref, a_ref, b_ref, g_ref, d_ref, mo_ref, vo_ref):
        gv = jnp.where(pl.program_id(0) == c_ref[0], a_ref[...], b_ref[...])
        dl, mn, vn = _adam_math(w_ref[...], gv, m_ref[...], v_ref[...])
        g_ref[...] = gv
        d_ref[...] = dl
        mo_ref[...] = mn
        vo_ref[...] = vn

    full = pl.BlockSpec((None, rb, cols), lambda hf, i, c_ref: (0, hf * nb + i, 0))
    part = pl.BlockSpec((rb, cols), lambda hf, i, c_ref: (i, 0))
    return pl.pallas_call(
        body, name=name,
        grid_spec=pltpu.PrefetchScalarGridSpec(num_scalar_prefetch=1, grid=(2, nb), in_specs=[full, full, full, part, part],
                                               out_specs=[full] * 4),
        out_shape=[jax.ShapeDtypeStruct((1, rows, cols), F32)] * 4,
        compiler_params=_params(("parallel", "parallel")),
    )(core, wt, mt, vt, mine, theirs)


SG_REP = 144
SG_LOSS = 136
SG_W2, SG_CW = SG_REP, SG_REP + 4 * 16
SG_ROWS = SG_CW + 4 * 40
SP_ROWS = SG_REP + 16 + 40


def _mod_shard(c_all, ada_w_sh):
    def body(c_ref, w_ref, o_ref):
        cv = c_ref[...]
        o_ref[...] = _dg((cv * _sigmoid(cv)).astype(BF16), w_ref[...].astype(BF16), 1, 0)

    return pl.pallas_call(body, name="mod_shard", out_shape=jax.ShapeDtypeStruct((8, 1536), F32),
                          in_specs=[VMEM_SPEC, VMEM_SPEC], out_specs=VMEM_SPEC,
                          compiler_params=pltpu.CompilerParams(vmem_limit_bytes=VMEM_LIMIT))(c_all, ada_w_sh)


def _mod_select(mod_all, ada_b4):
    def body(m_ref, b_ref, o_ref):
        x, y, c = _me()
        me = 4 * x + 2 * y + c
        for sh in range(4):
            o_ref[sh] = m_ref[2 * sh, me] + b_ref[sh]

    return pl.pallas_call(body, name="mod_select", out_shape=jax.ShapeDtypeStruct((4, 12, 128), F32),
                          in_specs=[VMEM_SPEC, VMEM_SPEC], out_specs=VMEM_SPEC)(mod_all, ada_b4)


def _small_reduce(sg_all):
    def body(g_ref, o_ref):
        x, y, c = _me()
        s_me = 2 * x + y
        w2_rows = pl.ds(pl.multiple_of(SG_W2 + 16 * s_me, 8), 16)
        cw_rows = pl.ds(pl.multiple_of(SG_CW + 40 * s_me, 8), 40)
        a = g_ref[0, 0:SG_REP, :]
        b = g_ref[0, w2_rows, :]
        d = g_ref[0, cw_rows, :]
        for dev in range(1, 8):
            a = a + g_ref[dev, 0:SG_REP, :]
            b = b + g_ref[dev, w2_rows, :]
            d = d + g_ref[dev, cw_rows, :]
        o_ref[0:SG_REP, :] = a
        o_ref[SG_REP:SG_REP + 16, :] = b
        o_ref[SG_REP + 16:SP_ROWS, :] = d

    return pl.pallas_call(body, name="small_grad_reduce", out_shape=jax.ShapeDtypeStruct((SP_ROWS, 128), F32),
                          in_specs=[VMEM_SPEC], out_specs=VMEM_SPEC)(sg_all)


def _ada_grad(dmod_all, c_bc):
    def body(g_ref, c_ref, o_ref):
        x, y, c = _me()
        s_me = 2 * x + y
        for k in range(12):
            acc = jnp.zeros((D, 128), F32)
            for b in range(8):
                cv = c_ref[b]
                acc = acc + (cv * _sigmoid(cv)) * g_ref[s_me, k, b:b + 1, :]
            o_ref[:, k * 128:(k + 1) * 128] = acc

    return pl.pallas_call(body, name="ada_w_grad", out_shape=jax.ShapeDtypeStruct((D, 1536), F32),
                          in_specs=[VMEM_SPEC, VMEM_SPEC], out_specs=VMEM_SPEC,
                          compiler_params=pltpu.CompilerParams(vmem_limit_bytes=VMEM_LIMIT))(dmod_all, c_bc)


def _adamw(wt, g, m, v, name):
    rows, cols = wt.shape
    rb = _tile(rows, 256, 8)

    def fn(c, i, wv, gv, mv, vv):
        return _adam_math(wv, gv, mv, vv)

    return _rowcall(fn, [_rows(t, rb) for t in (wt, g, m, v)], [_orow(rows, cols, F32, rb)] * 3,
                    n_rows=rows, rb=rb, name=name)


def _pad_rows(t, rows):
    flat = t.reshape(-1)
    return jnp.pad(flat, (0, rows * 128 - flat.shape[0])).reshape(rows, 128)


SP_LAYOUT = (("ada_b", 48), ("norm1_w", 8), ("gla_gate_b", 8), ("gla_norm_w", 8), ("norm2_w", 8), ("conv_b", 48),
             ("final_norm_w", 8), (None, 8), ("gla_gate_w2", 16), ("conv_w", 40))


def _pack_small(d):
    return jnp.concatenate([jnp.zeros((rows, 128), F32) if n is None else _pad_rows(d[n].astype(F32), rows)
                            for n, rows in SP_LAYOUT], axis=0)


def _unpack_small(pk, shapes):
    out, off = {}, 0
    for n, rows in SP_LAYOUT:
        if n is not None:
            shp = shapes[n]
            out[n] = pk[off:off + rows].reshape(-1)[:math.prod(shp)].reshape(shp)
        off += rows
    return out


def kernel(x, c, positions, ada_w, ada_b, norm1_w, w_in, gla_gate_w2, gla_gate_b, gla_norm_w, w_gla_branch, w_attn_branch, w_out, norm2_w, w_up, conv_w, conv_b, w_down, final_norm_w, loss_target, m_ada_w, m_ada_b, m_norm1_w, m_w_in, m_gla_gate_w2, m_gla_gate_b, m_gla_norm_w, m_w_gla_branch, m_w_attn_branch, m_w_out, m_norm2_w, m_w_up, m_conv_w, m_conv_b, m_w_down, m_final_norm_w, v_ada_w, v_ada_b, v_norm1_w, v_w_in, v_gla_gate_w2, v_gla_gate_b, v_gla_norm_w, v_w_gla_branch, v_w_attn_branch, v_w_out, v_norm2_w, v_w_up, v_conv_w, v_conv_b, v_w_down, v_final_norm_w):
    s = x.shape[1]
    names = ("ada_w", "ada_b", "norm1_w", "w_in", "gla_gate_w2", "gla_gate_b", "gla_norm_w", "w_gla_branch", "w_attn_branch",
             "w_out", "norm2_w", "w_up", "conv_w", "conv_b", "w_down", "final_norm_w")
    wts = dict(zip(names, (ada_w, ada_b, norm1_w, w_in, gla_gate_w2, gla_gate_b, gla_norm_w, w_gla_branch, w_attn_branch,
                           w_out, norm2_w, w_up, conv_w, conv_b, w_down, final_norm_w)))
    ms = dict(zip(names, (m_ada_w, m_ada_b, m_norm1_w, m_w_in, m_gla_gate_w2, m_gla_gate_b, m_gla_norm_w, m_w_gla_branch,
                          m_w_attn_branch, m_w_out, m_norm2_w, m_w_up, m_conv_w, m_conv_b, m_w_down, m_final_norm_w)))
    vs = dict(zip(names, (v_ada_w, v_ada_b, v_norm1_w, v_w_in, v_gla_gate_w2, v_gla_gate_b, v_gla_norm_w, v_w_gla_branch,
                          v_w_attn_branch, v_w_out, v_norm2_w, v_w_up, v_conv_w, v_conv_b, v_w_down, v_final_norm_w)))

    pk0 = jnp.concatenate([_pad_rows(c, 8), _pad_rows(gla_gate_w2, 16), _pad_rows(conv_w, 40)], axis=0)
    sm_all = _allgather8(pk0, "gather_small")
    c_all = sm_all[:, 0:8, :].reshape(8, D)
    w2_full = sm_all[0::2, 8:24, :].transpose(1, 0, 2).reshape(GLA_LR, 512)
    cw_full = sm_all[0::2, 24:64, :].reshape(4, 40 * 128)[:, :3 * W_UP_SH].reshape(4, 3, W_UP_SH).transpose(1, 0, 2).reshape(3, 2 * D_FF)

    mod_sh = _mod_shard(c_all, ada_w[0])
    mod_all = _allgather8(mod_sh.reshape(96, 128), "gather_mod")

    w_sh = [wts[n].astype(BF16) for n in BIG]
    u_g0 = _u_gather_ici(w_sh, (0,))
    g0 = (u_g0,) + _unit_start(u_g0, "gather_w_in_start", after=[mod_all])
    mod = _mod_select(mod_all.reshape(8, 8, 12, 128) + g0[4][0, 0], ada_b.reshape(4, 12, 128)).reshape(6, D)

    core = lax.axis_index("c").astype(jnp.int32).reshape(1)
    chip = (2 * lax.axis_index("x") + lax.axis_index("y")).astype(jnp.int32)
    sm = dict(n1w=norm1_w, n2w=norm2_w, fnw=final_norm_w.reshape(1, D), gnw=gla_norm_w, gb=gla_gate_b,
              w2=jnp.pad(w2_full, ((0, 128 - GLA_LR), (0, 0))), cw=_ff_to_kernel(cw_full), cb=_ff_to_kernel(conv_b))
    loss, grad_x, halves, others, small, ts0 = _local_step(x[0], mod, positions.reshape(s, 1), loss_target[0], sm, w_sh,
                                                               g0, chip, core)

    dcw = _ff_from_kernel(small["cw"]).reshape(3, 4, W_UP_SH).transpose(1, 0, 2)
    dw2 = small["w2"][:GLA_LR].reshape(GLA_LR, 4, 128).transpose(1, 0, 2)
    sg = jnp.concatenate(
        [_pad_rows(small["dmod"], 48), _pad_rows(small["n1w"], 8), _pad_rows(small["gb"], 8), _pad_rows(small["gnw"], 8),
         _pad_rows(small["n2w"], 8), _pad_rows(_ff_from_kernel(small["cb"]), 48), _pad_rows(small["fnw"], 8), _pad_rows(loss, 8)]
        + [_pad_rows(dw2[k], 16) for k in range(4)] + [_pad_rows(dcw[k], 40) for k in range(4)], axis=0)
    sg_all = _allgather8(sg, "gather_small_grads")
    u_ex = _u_chip_exchange([ts0])
    pending = (u_ex,) + _unit_start(u_ex, "grad_exchange_w_in_start", after=[sg_all])
    sg_all = sg_all + pending[4][0, 0]
    g_small_pk = _small_reduce(sg_all)
    dmod_all = sg_all[:, 0:48, :].reshape(8, 4, 12, 128).transpose(1, 2, 0, 3)
    g_ada_w = _ada_grad(dmod_all, jnp.broadcast_to(c_all[:, :, None], (8, D, 128)))

    shapes = {n: wts[n].shape for n in names}
    g_small = _unpack_small(g_small_pk, shapes)
    grads = {"ada_w": g_ada_w.reshape(1, D, 1536), **g_small}
    deltas, new_m, new_v = {}, {}, {}
    for n, mine, theirs in zip(BIG[1:], halves, others):
        grads[n], deltas[n], new_m[n], new_v[n] = _adamw_halves(wts[n], ms[n], vs[n], mine, theirs, core, "adamw_" + n)
    shp = ada_w.shape
    d_, m_, v_ = _adamw(ada_w[0], g_ada_w, m_ada_w[0], v_ada_w[0], "adamw_ada_w")
    deltas["ada_w"], new_m["ada_w"], new_v["ada_w"] = d_.reshape(shp), m_.reshape(shp), v_.reshape(shp)
    d_, m_, v_ = _adamw(_pack_small(wts), g_small_pk, _pack_small(ms), _pack_small(vs), "adamw_small")
    for dst, pk in ((deltas, d_), (new_m, m_), (new_v, v_)):
        dst.update(_unpack_small(pk, shapes))

    [t0], [r0] = _unit_wait(*pending[:4], after=[d_, deltas["ada_w"], deltas["w_up"], deltas["w_down"]], name="grad_exchange_w_in_wait")
    half0 = _chip_sum(t0, r0, chip.reshape(1), "grad_chip_sum_w_in")
    [[oth0]] = _comm_call("grad_join_w_in", [_u_pair_join([half0])])
    grads["w_in"], deltas["w_in"], new_m["w_in"], new_v["w_in"] = _adamw_halves(w_in, m_w_in, v_w_in, half0, oth0, core, "adamw_w_in")

    return (g_small_pk[SG_LOSS, 0], grad_x.reshape(1, s, D), *[grads[n] for n in names], *[deltas[n] for n in names],
            *[new_m[n] for n in names], *[new_v[n] for n in names])
```

```python
import math

import jax
import jax.numpy as jnp
from jax import lax
from jax.experimental import pallas as pl
from jax.experimental.pallas import tpu as pltpu

F32, BF16 = jnp.float32, jnp.bfloat16
MESH = pl.DeviceIdType.MESH

D = 1024
EPS = 1e-6
GLA_H, GLA_DK, GLA_DV, GLA_LR = 4, 128, 256, 16
GLA_TAU = 16.0
GLA_CHUNK = 64
GLA_BLOCK = 512
ATT_GROUPS = ((128, 1), (512, 4), (2048, 16))
ATT_BLK = 128
ATT_HD = 64
ATT_W = 768
D_FF = 2816
ROPE_THETA = 10000.0
P_W = 7680
P_GV, P_GR, P_MA, P_MB, P_GQ, P_GK, P_AQ, P_AK, P_AV, P_LR = 0, 1024, 2048, 3072, 4096, 4608, 5120, 5888, 6656, 7424
W_IN = 7440
W_IN_SH, W_UP_SH, W_DOWN_SH = 1860, 1408, 704
VMEM_LIMIT = 56 * 1024 * 1024
ADAM_LR, ADAM_B1, ADAM_B2, ADAM_EPS, ADAM_WD, ADAM_STEP = 0.001, 0.9, 0.999, 1e-08, 0.01, 10
NEG = -1e30


def _tile(n, target, unit=128):
    best = None
    for t in range(unit, min(n, target) + 1, unit):
        if n % t == 0:
            best = t
    return best or n


def _params(sem):
    return pltpu.CompilerParams(dimension_semantics=sem, vmem_limit_bytes=VMEM_LIMIT)


def _dg(a, b, ca, cb):
    return lax.dot_general(a, b, (((ca,), (cb,)), ((), ())), preferred_element_type=F32)


def _sigmoid(v):
    return 1.0 / (1.0 + jnp.exp(-v))


def _ff_block(j):
    return (j % 2) * 2 + j // 2


def _mm(a, b, name, *, ta=False, tb=False, out_dtype=BF16, tm=1024, tn=1536, tk=1024, n_outer=True, comm=(),
        b_shards=False, o_shards=False):
    m = a.shape[1] if ta else a.shape[0]
    k = a.shape[0] if ta else a.shape[1]
    if b_shards:
        n = b.shape[1] if tb else 4 * W_UP_SH
        tn, tk = (tn, W_UP_SH) if tb else (W_UP_SH, tk)
    else:
        n = b.shape[0] if tb else b.shape[1]
    if o_shards:
        tn = W_UP_SH
    tm, tn, tk = _tile(m, tm), _tile(n, tn), _tile(k, tk)
    nm, nn, nk = m // tm, n // tn, k // tk
    in_out = out_dtype == F32
    c_ins, c_outs, c_alias, c_scratch = _carry(comm, 2, 1)

    def body(a_ref, b_ref, *rest):
        ci, o_ref, co = rest[:len(c_ins)], rest[len(c_ins)], rest[len(c_ins) + 1:len(c_ins) + 1 + len(c_outs)]
        scr = rest[len(c_ins) + 1 + len(c_outs):]
        kk = pl.program_id(2)
        if comm:
            step = (pl.program_id(0) * (nm if n_outer else nn) + pl.program_id(1)) * nk + kk

            @pl.when(step == 0)
            def _():
                _comm_phase(comm, ci, co, scr[-2], scr[-1], True)

        _mm_step(a_ref, b_ref, o_ref, scr, kk)
        if comm:
            @pl.when(step == nm * nn * nk - 1)
            def _():
                _comm_phase(comm, ci, co, scr[-2], scr[-1], False)

    def _mm_step(a_ref, b_ref, o_ref, scr, kk):
        p = _dg(a_ref[...].astype(BF16), b_ref[...].astype(BF16), 0 if ta else 1, 1 if tb else 0)
        if nk == 1:
            o_ref[...] = p.astype(o_ref.dtype)
        else:
            acc = o_ref if in_out else scr[0]

            @pl.when(kk == 0)
            def _():
                acc[...] = p

            @pl.when(kk > 0)
            def _():
                acc[...] += p

            if not in_out:
                @pl.when(kk == nk - 1)
                def _():
                    o_ref[...] = acc[...].astype(o_ref.dtype)

    if n_outer:
        ij = lambda g0, g1: (g1, g0)
        grid = (nn, nm, nk)
    else:
        ij = lambda g0, g1: (g0, g1)
        grid = (nm, nn, nk)
    a_map = (lambda g0, g1, kk: (kk, ij(g0, g1)[0])) if ta else (lambda g0, g1, kk: (ij(g0, g1)[0], kk))
    if b_shards and tb:
        b_spec = pl.BlockSpec((None, tn, tk), lambda g0, g1, kk: (_ff_block(kk), ij(g0, g1)[1], 0))
    elif b_shards:
        b_spec = pl.BlockSpec((None, tk, tn), lambda g0, g1, kk: (_ff_block(ij(g0, g1)[1]), kk, 0))
    elif tb:
        b_spec = pl.BlockSpec((tn, tk), lambda g0, g1, kk: (ij(g0, g1)[1], kk))
    else:
        b_spec = pl.BlockSpec((tk, tn), lambda g0, g1, kk: (kk, ij(g0, g1)[1]))
    if o_shards:
        o_spec = pl.BlockSpec((None, tm, tn), lambda g0, g1, kk: (_ff_block(ij(g0, g1)[1]), ij(g0, g1)[0], 0))
        o_shape = jax.ShapeDtypeStruct((4, m, W_UP_SH), out_dtype)
    else:
        o_spec = pl.BlockSpec((tm, tn), lambda g0, g1, kk: ij(g0, g1))
        o_shape = jax.ShapeDtypeStruct((m, n), out_dtype)
    res = pl.pallas_call(
        body, name=name, grid=grid,
        in_specs=[pl.BlockSpec((tk, tm) if ta else (tm, tk), a_map), b_spec] + [HBM] * len(c_ins),
        out_specs=[o_spec] + [HBM] * len(c_outs),
        out_shape=[o_shape] + c_outs,
        scratch_shapes=([] if (in_out or nk == 1) else [pltpu.VMEM((tm, tn), F32)]) + c_scratch,
        input_output_aliases=c_alias,
        compiler_params=_params(("arbitrary",) * 3 if comm else ("parallel", "parallel", "arbitrary")),
    )(a, b, *c_ins)
    return (res[0], _split_units(comm, res[1:])) if comm else res[0]


def _rows(arr, rb, w=None, j=0):
    w = arr.shape[1] if w is None else w
    if callable(j):
        return arr, pl.BlockSpec((rb, w), lambda c, i: (i, j(c)))
    return arr, pl.BlockSpec((rb, w), lambda c, i: (i, j))


def _full(arr, w=None, j=0):
    w = arr.shape[1] if w is None else w
    if callable(j):
        return arr, pl.BlockSpec((arr.shape[0], w), lambda c, i: (0, j(c)))
    return arr, pl.BlockSpec((arr.shape[0], w), lambda c, i: (0, j))


def _halo(arr, rb, hb, w, j, before):
    per = rb // hb
    last = arr.shape[0] // hb - 1
    if before:
        rmap = lambda i: jnp.maximum(i * per - 1, 0)
    else:
        rmap = lambda i: jnp.minimum((i + 1) * per, last)
    return arr, pl.BlockSpec((hb, w), lambda c, i: (rmap(i), j(c) if callable(j) else j))


def _rowcall(fn, ins, outs, *, n_rows, rb, name, ncol=1, into=None, after=()):
    n_in = len(ins)
    nr = n_rows // rb
    unread = ([] if into is None else [into[0]]) + list(after)
    n_skip = len(unread)

    def body(*refs):
        c, i = pl.program_id(0), pl.program_id(1)
        res = fn(c, i, *[r[...] for r in refs[:n_in]])
        for val, spec, o_ref in zip(res, outs, refs[n_in + n_skip:]):
            if spec[2] == "row":
                o_ref[...] = val.astype(o_ref.dtype)
            else:
                @pl.when(i == 0)
                def _(o_ref=o_ref, val=val):
                    o_ref[...] = val.astype(o_ref.dtype)

                @pl.when(i > 0)
                def _(o_ref=o_ref, val=val):
                    o_ref[...] += val.astype(o_ref.dtype)

    out_specs = []
    for shape, dt, kind, block, col in outs:
        if kind == "row":
            out_specs.append(pl.BlockSpec(block, lambda c, i, col=col: (i, col(c))))
        else:
            out_specs.append(pl.BlockSpec(block, lambda c, i, col=col: (0, col(c))))
    return pl.pallas_call(
        body, name=name, grid=(ncol, nr),
        in_specs=[s for _, s in ins] + [pl.BlockSpec(memory_space=pl.ANY)] * n_skip, out_specs=out_specs,
        out_shape=[jax.ShapeDtypeStruct(o[0], o[1]) for o in outs],
        input_output_aliases={} if into is None else {n_in: into[1]},
        compiler_params=_params(("parallel", "arbitrary")),
    )(*[a for a, _ in ins], *unread)


def _orow(n_rows, w, dt, rb, bw=None, col=lambda c: 0):
    return ((n_rows, w), dt, "row", (rb, bw or w), col)


def _oacc(r, w, bw=None, col=lambda c: 0):
    return ((r, w), F32, "acc", (r, bw or w), col)


def _csum(v):
    return jnp.sum(v, axis=0, keepdims=True)


def _rms(v):
    return lax.rsqrt(jnp.mean(v * v, axis=-1, keepdims=True) + EPS)


def _norm_bwd(xv, dh, w, scale):
    r = _rms(xv)
    xh = xv * r
    dxh = dh * (w * (1.0 + scale))
    dx = r * (dxh - xh * jnp.mean(dxh * xh, axis=-1, keepdims=True))
    t = dh * xh
    return dx, _csum(dh), _csum(t * w), _csum(t * (1.0 + scale))


def _rope_tables(pos_col, invf, s):
    def fn(c, i, pos, f):
        ang = pos.astype(F32) * f
        lane = lax.broadcasted_iota(jnp.int32, ang.shape, 1)
        sign = jnp.where((lane % ATT_HD) < ATT_HD // 2, -1.0, 1.0)
        return jnp.cos(ang), jnp.sin(ang) * sign

    rb = 512
    return _rowcall(fn, [_rows(pos_col, rb), _full(invf)], [_orow(s, 128, F32, rb), _orow(s, 128, F32, rb)],
                    n_rows=s, rb=rb, name="rope_tables")


def _swap_halves(t):
    n = t.shape[1]
    lane = lax.broadcasted_iota(jnp.int32, t.shape, 1)
    return jnp.where((lane % ATT_HD) < ATT_HD // 2, pltpu.roll(t, n - 32, 1), pltpu.roll(t, 32, 1))


def _rope_apply(t, cos, sin_signed, inverse):
    cw = jnp.concatenate([cos] * (t.shape[1] // 128), axis=1)
    sw = jnp.concatenate([sin_signed] * (t.shape[1] // 128), axis=1)
    if inverse:
        sw = -sw
    return t * cw + _swap_halves(t) * sw


DIL_ROWS = 512


def _to_dilated(scr, val, out_ref, r):
    if r == 1:
        out_ref[...] = val.astype(out_ref.dtype)
        return
    n = val.shape[0] // r
    for hh in range(2):
        scr[hh] = val[:, hh * 128:(hh + 1) * 128]
        for pr in range(r):
            out_ref[:, pr * 256 + hh * 128:pr * 256 + (hh + 1) * 128] = scr[hh, pl.ds(pr, n, stride=r), :].astype(out_ref.dtype)


def _from_dilated(scr, in_ref, r):
    if r == 1:
        return in_ref[...].astype(F32)
    n = in_ref.shape[0]
    for hh in range(2):
        for pr in range(r):
            scr[hh, pl.ds(pr, n, stride=r), :] = in_ref[:, pr * 256 + hh * 128:pr * 256 + (hh + 1) * 128].astype(F32)
    return jnp.concatenate([scr[0], scr[1]], axis=1)


def _dil_spec(r):
    return pl.BlockSpec((DIL_ROWS // r, r * 256), lambda i: (i, 0))


def _dil_shape(s, r, dt):
    return jax.ShapeDtypeStruct((s // r, r * 256), dt)


_DIL_SCRATCH = [pltpu.VMEM((2, DIL_ROWS, 128), F32)]
_RS = tuple(r for _, r in ATT_GROUPS)


def _rope_fwd(p, cos_t, sin_t, s):
    def body(*refs):
        ins, cs, sn, outs, scr = refs[:9], refs[9][...], refs[10][...], refs[11:20], refs[20]
        for t in range(3):
            for g, r in enumerate(_RS):
                val = ins[3 * t + g][...].astype(F32)
                _to_dilated(scr, _rope_apply(val, cs, sn, False) if t < 2 else val, outs[3 * t + g], r)

    res = pl.pallas_call(
        body, name="rope", grid=(s // DIL_ROWS,),
        in_specs=[pl.BlockSpec((DIL_ROWS, 256), lambda i, c=base // 256 + g: (i, c)) for base in (P_AQ, P_AK, P_AV) for g in range(3)]
        + [pl.BlockSpec((DIL_ROWS, 128), lambda i: (i, 0))] * 2,
        out_specs=[_dil_spec(r) for _ in range(3) for r in _RS],
        out_shape=[_dil_shape(s, r, BF16) for _ in range(3) for r in _RS],
        scratch_shapes=_DIL_SCRATCH, compiler_params=_params(("parallel",)),
    )(*([p] * 9), cos_t, sin_t)
    return res[0:3], res[3:6], res[6:9]


def _attn_combine(att, s):
    def body(o0, o1, o2, l0, l1, l2, o_ref, lse_ref, od1, od2, ld1, ld2, scr):
        ov = [_from_dilated(scr, ref, r) for ref, r in zip((o0, o1, o2), _RS)]
        lv = [_from_dilated(scr, ref, r) for ref, r in zip((l0, l1, l2), _RS)]
        mx = jnp.maximum(jnp.maximum(lv[0], lv[1]), lv[2])
        ev = [jnp.exp(l - mx) for l in lv]
        z = ev[0] + ev[1] + ev[2]
        o = ((ev[0] * ov[0] + ev[1] * ov[1] + ev[2] * ov[2]) / z).astype(BF16)
        lse = mx + jnp.log(z)
        o_ref[...] = o
        lse_ref[...] = lse
        for ref, r in zip((od1, od2), _RS[1:]):
            _to_dilated(scr, o.astype(F32), ref, r)
        for ref, r in zip((ld1, ld2), _RS[1:]):
            _to_dilated(scr, lse, ref, r)

    return pl.pallas_call(
        body, name="attn_combine", grid=(s // DIL_ROWS,),
        in_specs=[_dil_spec(r) for r in _RS] * 2,
        out_specs=[_dil_spec(1)] * 2 + [_dil_spec(r) for r in _RS[1:]] * 2,
        out_shape=[_dil_shape(s, 1, BF16), _dil_shape(s, 1, F32)] + [_dil_shape(s, r, BF16) for r in _RS[1:]]
        + [_dil_shape(s, r, F32) for r in _RS[1:]],
        scratch_shapes=_DIL_SCRATCH, compiler_params=_params(("parallel",)),
    )(*[a[0] for a in att], *[a[1] for a in att])


def _dilate(t, s):
    def body(t_ref, o1, o2, scr):
        val = t_ref[...].astype(F32)
        for ref, r in zip((o1, o2), _RS[1:]):
            _to_dilated(scr, val, ref, r)

    return pl.pallas_call(
        body, name="attn_dilate", grid=(s // DIL_ROWS,), in_specs=[_dil_spec(1)], out_specs=[_dil_spec(r) for r in _RS[1:]],
        out_shape=[_dil_shape(s, r, t.dtype) for r in _RS[1:]], scratch_shapes=_DIL_SCRATCH, compiler_params=_params(("parallel",)),
    )(t)


def _rope_bwd(datt, d_glr, dp, cos_t, sin_t, s):
    tail = P_W - P_AQ

    def body(*refs):
        ins, cs, sn, glr, o_ref, scr = refs[:9], refs[9][...], refs[10][...], refs[11], refs[13], refs[14]
        for t in range(3):
            for g, r in enumerate(_RS):
                val = _from_dilated(scr, ins[3 * t + g], r)
                o_ref[:, t * ATT_W + g * 256:t * ATT_W + (g + 1) * 256] = (_rope_apply(val, cs, sn, True) if t < 2 else val).astype(BF16)
        o_ref[:, 3 * ATT_W:3 * ATT_W + 128] = glr[...]
        o_ref[:, 3 * ATT_W + 128:] = jnp.zeros((DIL_ROWS, tail - 3 * ATT_W - 128), BF16)

    return pl.pallas_call(
        body, name="rope_bwd", grid=(s // DIL_ROWS,),
        in_specs=[_dil_spec(r) for _ in range(3) for r in _RS] + [pl.BlockSpec((DIL_ROWS, 128), lambda i: (i, 0))] * 3
        + [pl.BlockSpec(memory_space=pl.ANY)],
        out_specs=pl.BlockSpec((DIL_ROWS, tail), lambda i: (i, P_AQ // tail)),
        out_shape=jax.ShapeDtypeStruct((s, P_W), BF16), input_output_aliases={12: 0},
        scratch_shapes=_DIL_SCRATCH, compiler_params=_params(("parallel",)),
    )(*[datt[g][t] for t in range(3) for g in range(3)], cos_t, sin_t, d_glr, dp)


def _tri_dot(tri, t):
    tb = tri.astype(BF16)
    hi = t.astype(BF16)
    r1 = t - hi.astype(F32)
    mid = r1.astype(BF16)
    lo = (r1 - mid.astype(F32)).astype(BF16)
    return _dg(tb, hi, 1, 0) + _dg(tb, mid, 1, 0) + _dg(tb, lo, 1, 0)


def _gla_decays(la_c, tri):
    b = _tri_dot(tri, la_c)
    row = lax.broadcasted_iota(jnp.int32, b.shape, 0)
    bmid = jnp.sum(jnp.where(row == GLA_CHUNK // 2 - 1, b, 0.0), axis=0, keepdims=True)
    blast = jnp.sum(jnp.where(row == GLA_CHUNK - 1, b, 0.0), axis=0, keepdims=True)
    return b, bmid, blast


def _gla_fwd(p, la, s, comm=()):
    tb, ch = GLA_BLOCK, GLA_CHUNK
    nb, nc = s // tb, tb // ch
    scale = GLA_DK ** -0.5
    c_ins, c_outs, c_alias, c_scratch = _carry(comm, 4, 2)

    def body(q_ref, k_ref, v_ref, la_ref, *rest):
        ci, (o_ref, st_ref) = rest[:len(c_ins)], rest[len(c_ins):len(c_ins) + 2]
        co, state = rest[len(c_ins) + 2:len(c_ins) + 2 + len(c_outs)], rest[len(c_ins) + 2 + len(c_outs)]
        step = pl.program_id(0)
        if comm:
            @pl.when(step == 0)
            def _():
                _comm_phase(comm, ci, co, rest[-2], rest[-1], True)

        _gla_fwd_step(q_ref, k_ref, v_ref, la_ref, o_ref, st_ref, state)
        if comm:
            @pl.when(step == nb - 1)
            def _():
                _comm_phase(comm, ci, co, rest[-2], rest[-1], False)

    def _gla_fwd_step(q_ref, k_ref, v_ref, la_ref, o_ref, st_ref, state):
        @pl.when(pl.program_id(0) == 0)
        def _():
            state[...] = jnp.zeros_like(state)

        ri = lax.broadcasted_iota(jnp.int32, (ch, ch), 0)
        ci = lax.broadcasted_iota(jnp.int32, (ch, ch), 1)
        causal = ci <= ri
        tri = causal.astype(F32)

        def chunk(c, carry):
            sl = pl.ds(pl.multiple_of(c * ch, ch), ch)
            b, bmid, blast = _gla_decays(la_ref[sl, :], tri)
            q = q_ref[sl, :].astype(F32) * scale
            k = k_ref[sl, :].astype(F32)
            v = v_ref[sl, :]
            qgt = (q * jnp.exp(b)).astype(BF16)
            qgn = (q * jnp.exp(b - bmid)).astype(BF16)
            kgn = (k * jnp.exp(bmid - b)).astype(BF16)
            kd = (k * jnp.exp(blast - b)).astype(BF16)
            dec = jnp.exp(blast)
            sts = [state[h] for h in range(GLA_H)]
            outs, news = [], []
            for h in range(GLA_H):
                hk, hv = slice(h * GLA_DK, (h + 1) * GLA_DK), slice(h * GLA_DV, (h + 1) * GLA_DV)
                a = jnp.where(causal, _dg(qgn[:, hk], kgn[:, hk], 1, 1), 0.0)
                outs.append(_dg(a.astype(BF16), v[:, hv], 1, 0) + _dg(qgt[:, hk], sts[h].astype(BF16), 1, 1))
                news.append(dec[:, hk] * sts[h] + _dg(v[:, hv], kd[:, hk], 0, 0))
            for h in range(GLA_H):
                st_ref[h, c] = sts[h]
                state[h] = news[h]
            o_ref[sl, :] = jnp.concatenate(outs, axis=1)
            return carry

        lax.fori_loop(0, nc, chunk, 0)

    hw = GLA_H * GLA_DK
    res = pl.pallas_call(
        body, name="gla_fwd", grid=(nb,),
        in_specs=[pl.BlockSpec((tb, hw), lambda t: (t, P_GQ // hw)),
                  pl.BlockSpec((tb, hw), lambda t: (t, P_GK // hw)),
                  pl.BlockSpec((tb, GLA_H * GLA_DV), lambda t: (t, P_GV // (GLA_H * GLA_DV))),
                  pl.BlockSpec((tb, hw), lambda t: (t, 0))] + [HBM] * len(c_ins),
        out_specs=[pl.BlockSpec((tb, GLA_H * GLA_DV), lambda t: (t, 0)),
                   pl.BlockSpec((GLA_H, nc, GLA_DV, GLA_DK), lambda t: (0, t, 0, 0))] + [HBM] * len(c_outs),
        out_shape=[jax.ShapeDtypeStruct((s, GLA_H * GLA_DV), F32),
                   jax.ShapeDtypeStruct((GLA_H, s // ch, GLA_DV, GLA_DK), F32)] + c_outs,
        scratch_shapes=[pltpu.VMEM((GLA_H, GLA_DV, GLA_DK), F32)] + c_scratch,
        input_output_aliases=c_alias,
        compiler_params=_params(("arbitrary",)),
    )(p, p, p, la, *c_ins)
    return res[0], res[1], _split_units(comm, res[2:])


def _gla_bwd(p, la, states, do, s, dp, comm=()):
    tb, ch = GLA_BLOCK, GLA_CHUNK
    nb, nc = s // tb, tb // ch
    scale = GLA_DK ** -0.5
    c_ins, c_outs, c_alias, c_scratch = _carry(comm, 7, 4)

    def body(q_ref, k_ref, v_ref, la_ref, st_ref, do_ref, dp_in, *rest):
        ci, outs = rest[:len(c_ins)], rest[len(c_ins):len(c_ins) + 4]
        co, dstate = rest[len(c_ins) + 4:len(c_ins) + 4 + len(c_outs)], rest[len(c_ins) + 4 + len(c_outs)]
        step = pl.program_id(0)
        if comm:
            @pl.when(step == 0)
            def _():
                _comm_phase(comm, ci, co, rest[-2], rest[-1], True)

        _gla_bwd_step(q_ref, k_ref, v_ref, la_ref, st_ref, do_ref, *outs, dstate)
        if comm:
            @pl.when(step == nb - 1)
            def _():
                _comm_phase(comm, ci, co, rest[-2], rest[-1], False)

    def _gla_bwd_step(q_ref, k_ref, v_ref, la_ref, st_ref, do_ref, dq_ref, dk_ref, dv_ref, dla_ref, dstate):
        @pl.when(pl.program_id(0) == 0)
        def _():
            dstate[...] = jnp.zeros_like(dstate)

        ri = lax.broadcasted_iota(jnp.int32, (ch, ch), 0)
        ci = lax.broadcasted_iota(jnp.int32, (ch, ch), 1)
        causal = ci <= ri
        tri = causal.astype(F32)
        tri_t = (ci >= ri).astype(F32)

        def chunk(cc, carry):
            c = nc - 1 - cc
            sl = pl.ds(pl.multiple_of(c * ch, ch), ch)
            b, bmid, blast = _gla_decays(la_ref[sl, :], tri)
            q = q_ref[sl, :].astype(F32) * scale
            k = k_ref[sl, :].astype(F32)
            v = v_ref[sl, :]
            e_b, e_qn, e_kn, e_kd = jnp.exp(b), jnp.exp(b - bmid), jnp.exp(bmid - b), jnp.exp(blast - b)
            dec = jnp.exp(blast)
            qgt, qgn, kgn, kd = q * e_b, q * e_qn, k * e_kn, k * e_kd
            qgt_b, qgn_b, kgn_b, kd_b = qgt.astype(BF16), qgn.astype(BF16), kgn.astype(BF16), kd.astype(BF16)
            do_b = do_ref[sl, :].astype(BF16)
            st0s = [st_ref[h, c] for h in range(GLA_H)]
            dsts = [dstate[h] for h in range(GLA_H)]
            dqgn, dqgt, dkgn, dkd, dvs, ddec, news = [], [], [], [], [], [], []
            for h in range(GLA_H):
                hk, hv = slice(h * GLA_DK, (h + 1) * GLA_DK), slice(h * GLA_DV, (h + 1) * GLA_DV)
                dst_b = dsts[h].astype(BF16)
                a = jnp.where(causal, _dg(qgn_b[:, hk], kgn_b[:, hk], 1, 1), 0.0).astype(BF16)
                da = jnp.where(causal, _dg(do_b[:, hv], v[:, hv], 1, 1), 0.0).astype(BF16)
                dqgn.append(_dg(da, kgn_b[:, hk], 1, 0))
                dqgt.append(_dg(do_b[:, hv], st0s[h].astype(BF16), 1, 0))
                dkgn.append(_dg(da, qgn_b[:, hk], 0, 0))
                dvs.append(_dg(a, do_b[:, hv], 0, 0) + _dg(kd_b[:, hk], dst_b, 1, 1))
                dkd.append(_dg(v[:, hv], dst_b, 1, 0))
                ddec.append(jnp.sum(st0s[h] * dsts[h], axis=0, keepdims=True))
                news.append(dec[:, hk] * dsts[h] + _dg(do_b[:, hv], qgt_b[:, hk], 0, 0))
            for h in range(GLA_H):
                dstate[h] = news[h]
            cat = lambda parts: jnp.concatenate(parts, axis=1)
            dqgn, dqgt, dkgn, dkd, ddec = cat(dqgn), cat(dqgt), cat(dkgn), cat(dkd), cat(ddec)
            dq_ref[sl, :] = (scale * (dqgn * e_qn + dqgt * e_b)).astype(dq_ref.dtype)
            dk_ref[sl, :] = (dkgn * e_kn + dkd * e_kd).astype(dk_ref.dtype)
            dv_ref[sl, :] = cat(dvs).astype(dv_ref.dtype)
            db = dqgn * qgn + dqgt * qgt - dkgn * kgn - dkd * kd
            extra = jnp.sum(dkd * kd, axis=0, keepdims=True) + ddec * dec
            dla_ref[sl, :] = _tri_dot(tri_t, db) + extra
            return carry

        lax.fori_loop(0, nc, chunk, 0)

    rev = lambda t: nb - 1 - t
    hw, vw = GLA_H * GLA_DK, GLA_H * GLA_DV
    res = pl.pallas_call(
        body, name="gla_bwd", grid=(nb,),
        in_specs=[pl.BlockSpec((tb, hw), lambda t: (rev(t), P_GQ // hw)),
                  pl.BlockSpec((tb, hw), lambda t: (rev(t), P_GK // hw)),
                  pl.BlockSpec((tb, vw), lambda t: (rev(t), P_GV // vw)),
                  pl.BlockSpec((tb, hw), lambda t: (rev(t), 0)),
                  pl.BlockSpec((GLA_H, nc, GLA_DV, GLA_DK), lambda t: (0, rev(t), 0, 0)),
                  pl.BlockSpec((tb, vw), lambda t: (rev(t), 0)), pl.BlockSpec(memory_space=pl.ANY)] + [HBM] * len(c_ins),
        out_specs=[pl.BlockSpec((tb, hw), lambda t: (rev(t), 0)),
                   pl.BlockSpec((tb, hw), lambda t: (rev(t), 0)),
                   pl.BlockSpec((tb, vw), lambda t: (rev(t), P_GV // vw)),
                   pl.BlockSpec((tb, hw), lambda t: (rev(t), 0))] + [HBM] * len(c_outs),
        out_shape=[jax.ShapeDtypeStruct((s, hw), BF16),
                   jax.ShapeDtypeStruct((s, hw), BF16),
                   jax.ShapeDtypeStruct((s, P_W), BF16),
                   jax.ShapeDtypeStruct((s, hw), F32)] + c_outs,
        scratch_shapes=[pltpu.VMEM((GLA_H, GLA_DV, GLA_DK), F32)] + c_scratch,
        input_output_aliases={6: 2, **c_alias},
        compiler_params=_params(("arbitrary",)),
    )(p, p, p, la, states, do, dp, *c_ins)
    return res[0], res[1], res[2], res[3], _split_units(comm, res[4:])


def _head_masks():
    lane = lax.broadcasted_iota(jnp.int32, (1, 4 * ATT_HD), 1)
    return [(lane >= h * ATT_HD) & (lane < (h + 1) * ATT_HD) for h in range(4)]


def _attn_fwd(qv, kv, pv, g, r, s):
    ln = s // r
    nblk = ln // ATT_BLK
    qcol = lambda pr: pr
    vcol = qcol
    prev = lambda n: jnp.maximum(n - 1, 0)

    def body(q_ref, kp_ref, kc_ref, vp_ref, vc_ref, o_ref, lse_ref):
        has_prev = pl.program_id(1) > 0
        ri = lax.broadcasted_iota(jnp.int32, (ATT_BLK, ATT_BLK), 0)
        ci = lax.broadcasted_iota(jnp.int32, (ATT_BLK, ATT_BLK), 1)
        m_cur = ci <= ri
        m_prev = (ci >= ri) & has_prev
        q, kp, kc, vp, vc = q_ref[...], kp_ref[...], kc_ref[...], vp_ref[...], vc_ref[...]
        o = jnp.zeros((ATT_BLK, 256), F32)
        lse = jnp.zeros((ATT_BLK, 256), F32)
        for hm in _head_masks():
            qm = jnp.where(hm, q, jnp.zeros_like(q))
            sc = jnp.where(m_cur, _dg(qm, kc, 1, 1) * 0.125, NEG)
            sp = jnp.where(m_prev, _dg(qm, kp, 1, 1) * 0.125, NEG)
            mx = jnp.maximum(jnp.max(sc, axis=1, keepdims=True), jnp.max(sp, axis=1, keepdims=True))
            pc, pp = jnp.exp(sc - mx), jnp.exp(sp - mx)
            den = jnp.sum(pc, axis=1, keepdims=True) + jnp.sum(pp, axis=1, keepdims=True)
            oh = (_dg(pc.astype(BF16), vc, 1, 0) + _dg(pp.astype(BF16), vp, 1, 0)) / den
            o = jnp.where(hm, oh, o)
            lse = jnp.where(hm, mx + jnp.log(den), lse)
        o_ref[...] = o.astype(o_ref.dtype)
        lse_ref[...] = lse

    blk = (ATT_BLK, 256)
    o, lse = pl.pallas_call(
        body, name=f"attn_fwd_{g}", grid=(r, nblk),
        in_specs=[pl.BlockSpec(blk, lambda pr, n: (n, qcol(pr))),
                  pl.BlockSpec(blk, lambda pr, n: (prev(n), qcol(pr))),
                  pl.BlockSpec(blk, lambda pr, n: (n, qcol(pr))),
                  pl.BlockSpec(blk, lambda pr, n: (prev(n), vcol(pr))),
                  pl.BlockSpec(blk, lambda pr, n: (n, vcol(pr)))],
        out_specs=[pl.BlockSpec(blk, lambda pr, n: (n, pr)), pl.BlockSpec(blk, lambda pr, n: (n, pr))],
        out_shape=[jax.ShapeDtypeStruct((ln, r * 256), BF16), jax.ShapeDtypeStruct((ln, r * 256), F32)],
        compiler_params=_params(("parallel", "parallel")),
    )(qv, kv, kv, pv, pv)
    return o, lse


def _attn_bwd(qv, kv, pv, dov, ov, lv, g, r, s):
    ln = s // r
    nblk = ln // ATT_BLK
    qcol = lambda pr: pr
    vcol = qcol
    prev = lambda n: jnp.maximum(n - 1, 0)
    nxt = lambda n: jnp.minimum(n + 1, nblk - 1)

    def body(qc_ref, qn_ref, kp_ref, kc_ref, vp_ref, vc_ref, doc_ref, don_ref, oc_ref, on_ref, lc_ref, ln_ref,
             dq_ref, dk_ref, dv_ref):
        n = pl.program_id(1)
        has_prev, has_next = n > 0, n < nblk - 1
        ri = lax.broadcasted_iota(jnp.int32, (ATT_BLK, ATT_BLK), 0)
        ci = lax.broadcasted_iota(jnp.int32, (ATT_BLK, ATT_BLK), 1)
        m_cur = ci <= ri
        m_prev = (ci >= ri) & has_prev
        m_next = (ci >= ri) & has_next
        qc, qn, kp, kc, vp, vc = qc_ref[...], qn_ref[...], kp_ref[...], kc_ref[...], vp_ref[...], vc_ref[...]
        doc, don = doc_ref[...], don_ref[...]
        pc_full = doc.astype(F32) * oc_ref[...].astype(F32)
        pn_full = don.astype(F32) * on_ref[...].astype(F32)
        lc, lnx = lc_ref[...], ln_ref[...]
        dq = jnp.zeros((ATT_BLK, 256), F32)
        dk = jnp.zeros((ATT_BLK, 256), F32)
        dv = jnp.zeros((ATT_BLK, 256), F32)
        zb = jnp.zeros_like(qc)
        for hm in _head_masks():
            qcm, qnm = jnp.where(hm, qc, zb), jnp.where(hm, qn, zb)
            docm, donm = jnp.where(hm, doc, zb), jnp.where(hm, don, zb)
            lse_c = jnp.max(jnp.where(hm, lc, NEG), axis=1, keepdims=True)
            lse_n = jnp.max(jnp.where(hm, lnx, NEG), axis=1, keepdims=True)
            del_c = jnp.sum(jnp.where(hm, pc_full, 0.0), axis=1, keepdims=True)
            del_n = jnp.sum(jnp.where(hm, pn_full, 0.0), axis=1, keepdims=True)
            pr_ = jnp.where(m_cur, jnp.exp(_dg(qcm, kc, 1, 1) * 0.125 - lse_c), 0.0)
            ds = (pr_ * (_dg(docm, vc, 1, 1) - del_c) * 0.125).astype(BF16)
            dqh = _dg(ds, kc, 1, 0)
            dkh = _dg(ds, qc, 0, 0)
            dvh = _dg(pr_.astype(BF16), doc, 0, 0)
            pr_ = jnp.where(m_prev, jnp.exp(_dg(qcm, kp, 1, 1) * 0.125 - lse_c), 0.0)
            ds = (pr_ * (_dg(docm, vp, 1, 1) - del_c) * 0.125).astype(BF16)
            dqh = dqh + _dg(ds, kp, 1, 0)
            pr_ = jnp.where(m_next, jnp.exp(_dg(qnm, kc, 1, 1) * 0.125 - lse_n), 0.0)
            ds = (pr_ * (_dg(donm, vc, 1, 1) - del_n) * 0.125).astype(BF16)
            dkh = dkh + _dg(ds, qn, 0, 0)
            dvh = dvh + _dg(pr_.astype(BF16), don, 0, 0)
            dq = jnp.where(hm, dqh, dq)
            dk = jnp.where(hm, dkh, dk)
            dv = jnp.where(hm, dvh, dv)
        dq_ref[...] = dq.astype(dq_ref.dtype)
        dk_ref[...] = dk.astype(dk_ref.dtype)
        dv_ref[...] = dv.astype(dv_ref.dtype)

    blk = (ATT_BLK, 256)
    cur = lambda col: pl.BlockSpec(blk, lambda pr, n: (n, col(pr)))
    prv = lambda col: pl.BlockSpec(blk, lambda pr, n: (prev(n), col(pr)))
    nx = lambda col: pl.BlockSpec(blk, lambda pr, n: (nxt(n), col(pr)))
    own = lambda pr: pr
    outs = pl.pallas_call(
        body, name=f"attn_bwd_{g}", grid=(r, nblk),
        in_specs=[cur(qcol), nx(qcol), prv(qcol), cur(qcol), prv(vcol), cur(vcol),
                  cur(own), nx(own), cur(own), nx(own), cur(own), nx(own)],
        out_specs=[cur(own), cur(own), cur(own)],
        out_shape=[jax.ShapeDtypeStruct((ln, r * 256), BF16)] * 3,
        compiler_params=_params(("parallel", "parallel")),
    )(qv, qv, kv, kv, pv, pv, dov, dov, ov, ov, lv, lv)
    return outs


def _gelu_parts(gv):
    cdf = 0.5 * (1.0 + lax.erf(gv * (2.0 ** -0.5)))
    pdf = jnp.exp(-0.5 * gv * gv) * (1.0 / math.sqrt(2.0 * math.pi))
    return cdf, pdf


def _pick_row(t, k):
    row = lax.broadcasted_iota(jnp.int32, t.shape, 0)
    return jnp.sum(jnp.where(row == k, t, 0.0), axis=0, keepdims=True)


def _shift_rows(u, halo, n):
    row = lax.broadcasted_iota(jnp.int32, u.shape, 0)
    out = pltpu.roll(u, n, 0)
    for k in range(n):
        out = jnp.where(row == k, _pick_row(halo, 16 - n + k), out)
    return out


def _shift_rows_up(u, halo, n):
    rb = u.shape[0]
    row = lax.broadcasted_iota(jnp.int32, u.shape, 0)
    out = pltpu.roll(u, rb - n, 0)
    for k in range(n):
        out = jnp.where(row == rb - n + k, _pick_row(halo, k), out)
    return out


def _conv(u, halo, cw, cb):
    return cb + _pick_row(cw, 0) * _shift_rows(u, halo, 2) + _pick_row(cw, 1) * _shift_rows(u, halo, 1) + _pick_row(cw, 2) * u


def _local_step(x, mod, pos_col, target, sm, w_sh, g0, chip, core):
    s = x.shape[0]
    shift1, scale1, gate1, shift2, scale2, gate2 = [mod[i:i + 1, :] for i in range(6)]
    rb = 512
    chip1 = chip.reshape(1)

    def f_norm1(c, i, xv, nw, sc, sh):
        return ((xv * _rms(xv) * nw) * (1.0 + sc) + sh,)

    (h,) = _rowcall(f_norm1, [_rows(x, rb), _full(sm["n1w"]), _full(scale1), _full(shift1)],
                    [_orow(s, D, BF16, rb)], n_rows=s, rb=rb, name="norm1")
    own = lambda got, i: lax.dynamic_update_slice(got, w_sh[i], (chip, 0, 0))
    invf = jnp.tile(ROPE_THETA ** (-jnp.arange(ATT_HD // 2, dtype=F32) / (ATT_HD // 2)), 4).reshape(1, 128)
    cos_t, sin_t = _rope_tables(pos_col, invf, s)
    _, got0 = _unit_wait(*g0[:4], after=[h, cos_t, sin_t], name="gather_w_in_wait")
    [got0] = _comm_call("gather_w_in_d2d", [_u_gather_d2d(got0, (0,))])
    u_g1 = _u_gather_ici(w_sh, (1, 2, 3, 4, 5))
    g1 = _unit_start(u_g1, "gather_weights_start", after=got0)
    w = dict(win=_win_assemble(own(got0[0], 0), after=g1[3:]))
    p = _mm(h, w["win"], "in_proj", tm=2048, tn=1536)

    def f_gla_pre(c, i, glr, w2, gb):
        z = _dg(glr, w2.astype(BF16), 1, 0) + gb
        return ((jnp.minimum(z, 0.0) - jnp.log(1.0 + jnp.exp(-jnp.abs(z)))) * (1.0 / GLA_TAU),)

    (la,) = _rowcall(f_gla_pre, [_rows(p, rb, 128, P_LR // 128), _full(sm["w2"]), _full(sm["gb"])],
                     [_orow(s, 512, F32, rb)], n_rows=s, rb=rb, name="gla_pre")
    o_gla, states, _ = _gla_fwd(p, la, s)
    _, got = _unit_wait(u_g1, *g1[:3], after=[o_gla], name="gather_weights_wait")
    [got123] = _comm_call("gather_weights_d2d", [_u_gather_d2d(got[:3], (1, 2, 3))])
    got45 = got[3:]
    w.update(wgb=own(got123[0], 1).reshape(1024, D), wab=_cols_join(own(got123[1], 2)), wout=own(got123[2], 3).reshape(D, D))

    def f_gla_post(c, i, ov, gnw, gr):
        on = jnp.concatenate([ov[:, k * 256:(k + 1) * 256] * _rms(ov[:, k * 256:(k + 1) * 256]) * gnw
                              for k in range(GLA_H)], axis=1)
        g = gr.astype(F32)
        return (on * (g * _sigmoid(g)),)

    (og,) = _rowcall(f_gla_post, [_rows(o_gla, rb), _full(sm["gnw"]), _rows(p, rb, 1024, P_GR // 1024)],
                     [_orow(s, 1024, BF16, rb)], n_rows=s, rb=rb, name="gla_post")
    y_gla = _mm(og, w["wgb"], "gla_branch")

    q_d, k_d, v_d = _rope_fwd(p, cos_t, sin_t, s)
    att = [_attn_fwd(q_d[g], k_d[g], v_d[g], g, r, s) for g, r in enumerate(_RS)]
    o_att, lse, o_d1, o_d2, lse_d1, lse_d2 = _attn_combine(att, s)
    y_att = _mm(o_att, w["wab"], "attn_branch")

    def f_merge(c, i, ma, mb, yg, ya):
        return (_sigmoid(ma.astype(F32)) * yg.astype(F32) + _sigmoid(mb.astype(F32)) * ya.astype(F32),)

    (mixed,) = _rowcall(f_merge, [_rows(p, rb, D, P_MA // D), _rows(p, rb, D, P_MB // D), _rows(y_gla, rb), _rows(y_att, rb)],
                        [_orow(s, D, BF16, rb)], n_rows=s, rb=rb, name="merge")
    z1, [got45] = _mm(mixed, w["wout"], "out_proj", comm=[_u_gather_d2d(got45, (4, 5))])
    w.update(wup=own(got45[0], 4), wdown=own(got45[1], 5).reshape(D_FF, D))

    def f_norm2(c, i, xv, z, g1, nw, sc, sh):
        x1 = xv + g1 * z.astype(F32)
        return (x1, (x1 * _rms(x1) * nw) * (1.0 + sc) + sh)

    x1, h2 = _rowcall(f_norm2, [_rows(x, rb), _rows(z1, rb), _full(gate1), _full(sm["n2w"]), _full(scale2), _full(shift2)],
                      [_orow(s, D, F32, rb), _orow(s, D, BF16, rb)], n_rows=s, rb=rb, name="norm2")
    u = _mm(h2, w["wup"], "up_proj", tm=2048, b_shards=True)

    cwid = 2 * W_UP_SH

    def f_ffn(c, i, uv, hl, cw, cb):
        uc = _conv(uv.astype(F32), hl.astype(F32) * (i > 0).astype(F32), cw, cb)
        val, gt = uc[:, :W_UP_SH], uc[:, W_UP_SH:]
        cdf, _ = _gelu_parts(gt)
        return (gt * cdf * val,)

    ccol = lambda c: c
    rw = 256
    (hidden,) = _rowcall(f_ffn, [_rows(u, rw, cwid, ccol), _halo(u, rw, 16, cwid, ccol, True),
                                 _full(sm["cw"], cwid, ccol), _full(sm["cb"], cwid, ccol)],
                         [_orow(s, D_FF, BF16, rw, W_UP_SH, ccol)], n_rows=s, rb=rw, name="conv_geglu", ncol=2)
    z2 = _mm(hidden, w["wdown"], "down_proj", tm=2048, tk=D_FF)

    def f_final(c, i, x1v, z, g2, fw, tgt):
        x2 = x1v + g2 * z.astype(F32)
        r = _rms(x2)
        xh = x2 * r
        e = xh * fw - tgt
        loss = 0.5 * jnp.sum(jnp.mean(e * e, axis=-1, keepdims=True), axis=0, keepdims=True)
        dy = e * (1.0 / D)
        dxh = dy * fw
        dx2 = r * (dxh - xh * jnp.mean(dxh * xh, axis=-1, keepdims=True))
        return (loss, dx2, dx2 * g2, _csum(dy * xh), _csum(dx2 * z.astype(F32)))

    loss, dx2, dz2, d_fnw, d_gate2 = _rowcall(
        f_final, [_rows(x1, rb), _rows(z2, rb), _full(gate2), _full(sm["fnw"]), _rows(target, rb)],
        [_oacc(1, 1), _orow(s, D, F32, rb), _orow(s, D, BF16, rb), _oacc(1, D), _oacc(1, D)],
        n_rows=s, rb=rb, name="final_loss")
    d_hidden = _mm(dz2, w["wdown"], "down_proj_dx", tb=True, tn=1408)
    g_wdown = _mm(hidden, dz2, "down_proj_dw", ta=True, out_dtype=F32, tm=1408, tn=1024, tk=2048)

    def f_ffn_bwd(c, i, uv, hl, dh, cw, cb):
        uf = uv.astype(F32)
        hf = hl.astype(F32) * (i > 0).astype(F32)
        u1, u2 = _shift_rows(uf, hf, 1), _shift_rows(uf, hf, 2)
        uc = cb + _pick_row(cw, 0) * u2 + _pick_row(cw, 1) * u1 + _pick_row(cw, 2) * uf
        val, gt = uc[:, :W_UP_SH], uc[:, W_UP_SH:]
        cdf, pdf = _gelu_parts(gt)
        dhf = dh.astype(F32)
        duc = jnp.concatenate([dhf * (gt * cdf), dhf * val * (cdf + gt * pdf)], axis=1)
        dcw = jnp.concatenate([_csum(duc * u2), _csum(duc * u1), _csum(duc * uf)], axis=0)
        return (duc, _csum(duc), dcw)

    duc, d_cb, d_cw = _rowcall(
        f_ffn_bwd, [_rows(u, rw, cwid, ccol), _halo(u, rw, 16, cwid, ccol, True), _rows(d_hidden, rw, W_UP_SH, ccol),
                    _full(sm["cw"], cwid, ccol), _full(sm["cb"], cwid, ccol)],
        [_orow(s, 2 * D_FF, BF16, rw, cwid, ccol), _oacc(1, 2 * D_FF, cwid, ccol), _oacc(3, 2 * D_FF, cwid, ccol)],
        n_rows=s, rb=rw, name="conv_geglu_bwd", ncol=2)

    def f_conv_t(c, i, dv, hl, cw):
        df = dv.astype(F32)
        hf = hl.astype(F32) * (i < s // rw - 1).astype(F32)
        return (_pick_row(cw, 2) * df + _pick_row(cw, 1) * _shift_rows_up(df, hf, 1) + _pick_row(cw, 0) * _shift_rows_up(df, hf, 2),)

    (du,) = _rowcall(f_conv_t, [_rows(duc, rw, cwid, ccol), _halo(duc, rw, 16, cwid, ccol, False), _full(sm["cw"], cwid, ccol)],
                     [_orow(s, 2 * D_FF, BF16, rw, cwid, ccol)], n_rows=s, rb=rw, name="conv_transpose", ncol=2)
    g_wup = _mm(h2, du, "up_proj_dw", ta=True, out_dtype=F32, tm=1024, tk=2048, o_shards=True)
    gs45 = [g_wup, g_wdown.reshape(4, W_DOWN_SH, 1024)]
    d_h2, [land45] = _mm(du, w["wup"], "up_proj_dx", tb=True, tm=2048, b_shards=True, comm=[_u_pair_send(gs45, (4, 5))])
    ts45 = [_pair_add(g, ld, core, "grad_pair_add_" + BIG[i]) for g, ld, i in zip(gs45, land45, (4, 5))]
    u_ex4 = _u_chip_exchange(ts45[:1])
    ex4 = _unit_start(u_ex4, "grad_exchange_w_up_start")

    def f_norm2_bwd(c, i, x1v, dh, dxr, z, nw, sc, g1):
        dxn, dsh, dsc, dnw = _norm_bwd(x1v, dh.astype(F32), nw, sc)
        dx1 = dxr + dxn
        return (dx1, dx1 * g1, dsh, dsc, dnw, _csum(dx1 * z.astype(F32)))

    dx1, dz1, d_shift2, d_scale2, d_n2w, d_gate1 = _rowcall(
        f_norm2_bwd, [_rows(x1, rb), _rows(d_h2, rb), _rows(dx2, rb), _rows(z1, rb), _full(sm["n2w"]), _full(scale2), _full(gate1)],
        [_orow(s, D, F32, rb), _orow(s, D, BF16, rb), _oacc(1, D), _oacc(1, D), _oacc(1, D), _oacc(1, D)],
        n_rows=s, rb=rb, name="norm2_bwd", after=ex4[3:])
    d_mixed = _mm(dz1, w["wout"], "out_proj_dx", tb=True)
    g_wout = _mm(mixed, dz1, "out_proj_dw", ta=True, out_dtype=F32, tk=2048)

    def f_merge_bwd(c, i, dm, ma, mb, yg, ya):
        dmf, ygf, yaf = dm.astype(F32), yg.astype(F32), ya.astype(F32)
        sa, sb = _sigmoid(ma.astype(F32)), _sigmoid(mb.astype(F32))
        return (dmf * sa, dmf * sb, jnp.concatenate([dmf * ygf * sa * (1.0 - sa), dmf * yaf * sb * (1.0 - sb)], axis=1))

    dy_gla, dy_att, dp = _rowcall(
        f_merge_bwd, [_rows(d_mixed, rb), _rows(p, rb, D, P_MA // D), _rows(p, rb, D, P_MB // D), _rows(y_gla, rb), _rows(y_att, rb)],
        [_orow(s, D, BF16, rb)] * 2 + [_orow(s, P_W, BF16, rb, 2 * D, lambda c: P_MA // (2 * D))], n_rows=s, rb=rb, name="merge_bwd")
    d_og = _mm(dy_gla, w["wgb"], "gla_branch_dx", tb=True)
    g_wgb = _mm(og, dy_gla, "gla_branch_dw", ta=True, out_dtype=F32, tk=2048)
    d_oatt = _mm(dy_att, w["wab"], "attn_branch_dx", tb=True)
    g_wab = _mm(o_att, dy_att, "attn_branch_dw", ta=True, out_dtype=F32, tk=2048)

    def f_gla_post_bwd(c, i, ov, gnw, gr, dog):
        g = gr.astype(F32)
        sg = _sigmoid(g)
        silu = g * sg
        dof = dog.astype(F32)
        don = dof * silu
        on_parts, do_parts, dgn = [], [], jnp.zeros((1, 256), F32)
        for k in range(GLA_H):
            oh = ov[:, k * 256:(k + 1) * 256]
            dh = don[:, k * 256:(k + 1) * 256]
            r = _rms(oh)
            xh = oh * r
            dgn = dgn + _csum(dh * xh)
            dxh = dh * gnw
            do_parts.append(r * (dxh - xh * jnp.mean(dxh * xh, axis=-1, keepdims=True)))
            on_parts.append(xh * gnw)
        on = jnp.concatenate(on_parts, axis=1)
        dgr = dof * on * (sg * (1.0 + g * (1.0 - sg)))
        return (jnp.concatenate(do_parts, axis=1), dgr, dgn)

    do_gla, dp, d_gnw = _rowcall(
        f_gla_post_bwd, [_rows(o_gla, rb), _full(sm["gnw"]), _rows(p, rb, 1024, P_GR // 1024), _rows(d_og, rb)],
        [_orow(s, 1024, F32, rb), _orow(s, P_W, BF16, rb, 1024, lambda c: P_GR // 1024), _oacc(1, 256)],
        n_rows=s, rb=rb, name="gla_post_bwd", into=(dp, 1))
    gs123 = [g_wgb.reshape(4, 256, 1024), _cols_split(g_wab), g_wout.reshape(4, 256, 1024)]
    d_gq, d_gk, dp, d_la, [land123] = _gla_bwd(p, la, states, do_gla, s, dp, comm=[_u_pair_send(gs123, (1, 2, 3))])
    ts123 = [_pair_add(g, ld, core, "grad_pair_add_" + BIG[i]) for g, ld, i in zip(gs123, land123, (1, 2, 3))]

    def f_gla_pre_bwd(c, i, lav, dlav, glr, w2):
        dz = dlav * (1.0 / GLA_TAU) * (1.0 - jnp.exp(GLA_TAU * lav))
        dzb = dz.astype(BF16)
        return (_dg(dzb, w2.astype(BF16), 1, 1), _csum(dz), _dg(glr, dzb, 0, 0))

    d_glr, d_gb, d_w2 = _rowcall(
        f_gla_pre_bwd, [_rows(la, rb), _rows(d_la, rb), _rows(p, rb, 128, P_LR // 128), _full(sm["w2"])],
        [_orow(s, 128, BF16, rb), _oacc(1, 512), _oacc(128, 512)], n_rows=s, rb=rb, name="gla_pre_bwd")

    do_d = [d_oatt] + list(_dilate(d_oatt, s))
    datt = [_attn_bwd(q_d[g], k_d[g], v_d[g], do_d[g], (o_att, o_d1, o_d2)[g], (lse, lse_d1, lse_d2)[g], g, r, s)
            for g, r in enumerate(_RS)]
    dp = _rope_bwd(datt, d_glr, dp, cos_t, sin_t, s)
    dp = lax.dynamic_update_slice(dp, jnp.concatenate([d_gq, d_gk], axis=1), (0, P_GQ))
    [t4], [r4] = _unit_wait(u_ex4, *ex4[:3], after=[dp], name="grad_exchange_w_up_wait")
    half4 = [_chip_sum(t4, r4, chip1, "grad_chip_sum_w_up")]
    g_win, [r1235, oth4] = _mm(h, dp, "in_proj_dw", ta=True, out_dtype=F32, tm=1024, tn=1536, tk=2048,
                               comm=[_u_chip_exchange(ts123 + ts45[1:]), _u_pair_join(half4)])
    half1235 = [_chip_sum(t, r, chip1, "grad_chip_sum_" + BIG[i]) for t, r, i in zip(ts123 + ts45[1:], r1235, (1, 2, 3, 5))]
    gs0 = [_win_split(g_win)]
    d_h, [land0, oth1235] = _mm(dp, w["win"], "in_proj_dx", tb=True, tk=3840,
                                comm=[_u_pair_send(gs0, (0,)), _u_pair_join(half1235)])
    half123, half45 = half1235[:3], half4 + half1235[3:]
    oth123, oth45 = oth1235[:3], oth4 + oth1235[3:]
    ts0 = _pair_add(gs0[0], land0[0], core, "grad_pair_add_w_in")

    def f_norm1_bwd(c, i, xv, dh, dxr, nw, sc):
        dxn, dsh, dsc, dnw = _norm_bwd(xv, dh.astype(F32), nw, sc)
        return (dxr + dxn, dsh, dsc, dnw)

    grad_x, d_shift1, d_scale1, d_n1w = _rowcall(
        f_norm1_bwd, [_rows(x, rb), _rows(d_h, rb), _rows(dx1, rb), _full(sm["n1w"]), _full(scale1)],
        [_orow(s, D, F32, rb), _oacc(1, D), _oacc(1, D), _oacc(1, D)], n_rows=s, rb=rb, name="norm1_bwd")

    dmod = jnp.concatenate([d_shift1, d_scale1, d_gate1, d_shift2, d_scale2, d_gate2], axis=1)
    small = dict(dmod=dmod, n1w=d_n1w, gb=d_gb, gnw=d_gnw, n2w=d_n2w, cb=d_cb, fnw=d_fnw, w2=d_w2, cw=d_cw)
    return loss, grad_x, half123 + half45, oth123 + oth45, small, ts0


def _win_pieces():
    runs = [(P_GV, 1024, 2048), (P_MA, 5392, 2048), (P_GQ, 0, 1024), (P_AQ, 3088, 2304), (P_LR, 3072, GLA_LR)]
    out = []
    for kc, rc, ln in runs:
        while ln > 0:
            step = min(ln, W_IN_SH - rc % W_IN_SH)
            out.append((kc, rc, step))
            kc, rc, ln = kc + step, rc + step, ln - step
    return out


def _win_assemble(shards, after=()):
    rb = 256

    def body(s_ref, *rest):
        o_ref = rest[-1]
        o_ref[:, W_IN:] = jnp.zeros((rb, P_W - W_IN), o_ref.dtype)
        for kc, rc, ln in _win_pieces():
            o_ref[:, kc:kc + ln] = s_ref[rc // W_IN_SH, :, rc % W_IN_SH:rc % W_IN_SH + ln]

    return pl.pallas_call(
        body, name="w_in_assemble", grid=(D // rb,),
        in_specs=[pl.BlockSpec((4, rb, W_IN_SH), lambda i: (0, i, 0))] + [pl.BlockSpec(memory_space=pl.ANY)] * len(after),
        out_specs=pl.BlockSpec((rb, P_W), lambda i: (i, 0)),
        out_shape=jax.ShapeDtypeStruct((D, P_W), shards.dtype), compiler_params=_params(("parallel",)),
    )(shards, *after)


def _win_split(g):
    rb = 256

    def body(g_ref, o_ref):
        for kc, rc, ln in _win_pieces():
            o_ref[rc // W_IN_SH, :, rc % W_IN_SH:rc % W_IN_SH + ln] = g_ref[:, kc:kc + ln]

    return pl.pallas_call(
        body, name="w_in_grad_split", grid=(D // rb,),
        in_specs=[pl.BlockSpec((rb, P_W), lambda i: (i, 0))], out_specs=pl.BlockSpec((4, rb, W_IN_SH), lambda i: (0, i, 0)),
        out_shape=jax.ShapeDtypeStruct((4, D, W_IN_SH), g.dtype), compiler_params=_params(("parallel",)),
    )(g)


def _ff_to_kernel(a):
    h = W_UP_SH
    return jnp.concatenate([a[:, 0:h], a[:, D_FF:D_FF + h], a[:, h:D_FF], a[:, D_FF + h:]], axis=1)


def _ff_from_kernel(a):
    h = W_UP_SH
    return jnp.concatenate([a[:, 0:h], a[:, 2 * h:3 * h], a[:, h:2 * h], a[:, 3 * h:]], axis=1)


BIG = ("w_in", "w_gla_branch", "w_attn_branch", "w_out", "w_up", "w_down")
SH_SHAPES = ((1024, W_IN_SH), (256, 1024), (256, 256), (256, 1024), (1024, W_UP_SH), (W_DOWN_SH, 1024))
N_BIG = len(BIG)


def _cols_join(t):
    return jnp.concatenate([t[k] for k in range(4)], axis=1)


def _cols_split(t):
    cols = t.shape[1] // 4
    return jnp.stack([t[:, k * cols:(k + 1) * cols] for k in range(4)])


def _me():
    return lax.axis_index("x"), lax.axis_index("y"), lax.axis_index("c")


HBM = pl.BlockSpec(memory_space=pltpu.HBM)
VMEM_SPEC = pl.BlockSpec(memory_space=pltpu.VMEM)


def _allgather8(xs, name):
    rows = xs.shape[0]

    def body(x_ref, out_ref, send_sems, recv_sems, local_sem):
        x, y, c = _me()
        me = 4 * x + 2 * y + c
        mine = pltpu.make_async_copy(x_ref, out_ref.at[me], local_sem)
        mine.start()
        flips = [(k >> 2 & 1, k >> 1 & 1, k & 1) for k in range(1, 8)]

        def peer(f):
            return (jnp.where(f[0] == 1, 1 - x, x), jnp.where(f[1] == 1, 1 - y, y), jnp.where(f[2] == 1, 1 - c, c))

        sends = []
        for k, f in enumerate(flips):
            cp = pltpu.make_async_remote_copy(src_ref=x_ref, dst_ref=out_ref.at[me], send_sem=send_sems.at[k],
                                              recv_sem=recv_sems.at[k], device_id=peer(f), device_id_type=MESH)
            cp.start()
            sends.append(cp)
        for k, f in enumerate(flips):
            px, py, pc = peer(f)
            pltpu.make_async_remote_copy(src_ref=x_ref, dst_ref=out_ref.at[4 * px + 2 * py + pc], send_sem=send_sems.at[k],
                                         recv_sem=recv_sems.at[k], device_id=peer(f), device_id_type=MESH).wait_recv()
        for cp in sends:
            cp.wait_send()
        mine.wait()

    return pl.pallas_call(
        body, name=name, out_shape=jax.ShapeDtypeStruct((8, rows, 128), F32),
        in_specs=[VMEM_SPEC], out_specs=VMEM_SPEC,
        scratch_shapes=[pltpu.SemaphoreType.DMA((7,)), pltpu.SemaphoreType.DMA((7,)), pltpu.SemaphoreType.DMA],
        compiler_params=pltpu.CompilerParams(vmem_limit_bytes=VMEM_LIMIT),
    )(xs)


def _half_rows(i, cc, unit):
    rows = SH_SHAPES[i][0] // 2
    return pl.ds(pl.multiple_of(cc * rows, unit), rows)


def _rc(src, dst, sems, to):
    return pltpu.make_async_remote_copy(src_ref=src, dst_ref=dst, send_sem=sems[0], recv_sem=sems[1], device_id=to, device_id_type=MESH)


def _other_chips(x, y):
    return [(1 - x, y), (x, 1 - y), (1 - x, 1 - y)]


def _u_gather_ici(w_sh, idxs):
    def copies(ins, outs, sem):
        x, y, c = _me()
        res = []
        for j, (px, py) in enumerate(_other_chips(x, y)):
            for n, i in enumerate(idxs):
                src = ins[n].at[0, _half_rows(i, c, 16)]
                res.append((_rc(src, outs[n].at[2 * x + y, _half_rows(i, c, 16)], sem(j * len(idxs) + n), (px, py, c)),
                            _rc(src, outs[n].at[2 * px + py, _half_rows(i, c, 16)], sem(j * len(idxs) + n), (px, py, c))))
        return res

    return dict(ins=[w_sh[i] for i in idxs], outs=[jax.ShapeDtypeStruct((4,) + SH_SHAPES[i], BF16) for i in idxs],
                nsem=3 * len(idxs), alias={}, copies=copies)


def _u_gather_d2d(got, idxs):
    def copies(ins, outs, sem):
        x, y, c = _me()
        res = []
        for j, (px, py) in enumerate(_other_chips(x, y)):
            for n, i in enumerate(idxs):
                src = ins[n].at[2 * px + py, _half_rows(i, c, 16)]
                res.append((_rc(src, outs[n].at[2 * px + py, _half_rows(i, c, 16)], sem(j * len(idxs) + n), (x, y, 1 - c)),
                            _rc(src, outs[n].at[2 * px + py, _half_rows(i, 1 - c, 16)], sem(j * len(idxs) + n), (x, y, 1 - c))))
        return res

    return dict(ins=list(got), outs=[jax.ShapeDtypeStruct(g.shape, g.dtype) for g in got], nsem=3 * len(idxs),
                alias={n: n for n in range(len(idxs))}, copies=copies)


def _u_pair_send(gs, idxs):
    def copies(ins, outs, sem):
        x, y, c = _me()
        res = []
        for n, i in enumerate(idxs):
            for sh in range(4):
                cp = _rc(ins[n].at[sh, _half_rows(i, 1 - c, 8)], outs[n].at[sh], sem(4 * n + sh), (x, y, 1 - c))
                res.append((cp, cp))
        return res

    return dict(ins=list(gs), outs=[jax.ShapeDtypeStruct((4, SH_SHAPES[i][0] // 2, SH_SHAPES[i][1]), F32) for i in idxs],
                nsem=4 * len(idxs), alias={}, copies=copies)


def _u_chip_exchange(ts):
    def copies(ins, outs, sem):
        x, y, c = _me()
        res = []
        for j, (px, py) in enumerate(_other_chips(x, y)):
            for n in range(len(ts)):
                cp = _rc(ins[n].at[2 * px + py], outs[n].at[j], sem(j * len(ts) + n), (px, py, c))
                res.append((cp, cp))
        return res

    return dict(ins=list(ts), outs=[jax.ShapeDtypeStruct((3,) + t.shape[1:], t.dtype) for t in ts], nsem=3 * len(ts),
                alias={}, copies=copies)


def _u_pair_join(hs):
    def copies(ins, outs, sem):
        x, y, c = _me()
        res = []
        for n in range(len(hs)):
            cp = _rc(ins[n], outs[n], sem(n), (x, y, 1 - c))
            res.append((cp, cp))
        return res

    return dict(ins=list(hs), outs=[jax.ShapeDtypeStruct(h.shape, h.dtype) for h in hs], nsem=len(hs), alias={}, copies=copies)


def _comm_phase(units, ci, co, send_sems, recv_sems, start):
    ii = oo = off = 0
    for u in units:
        ni, no = len(u["ins"]), len(u["outs"])
        for st, arrival in u["copies"](ci[ii:ii + ni], co[oo:oo + no], lambda k, off=off: (send_sems.at[off + k], recv_sems.at[off + k])):
            if start:
                st.start()
            else:
                st.wait_send()
                arrival.wait_recv()
        ii, oo, off = ii + ni, oo + no, off + u["nsem"]


def _carry(units, n_in, n_out):
    ins = [a for u in units for a in u["ins"]]
    outs = [o for u in units for o in u["outs"]]
    alias, ii, oo = {}, 0, 0
    for u in units:
        for a, b in u["alias"].items():
            alias[n_in + ii + a] = n_out + oo + b
        ii, oo = ii + len(u["ins"]), oo + len(u["outs"])
    nsem = sum(u["nsem"] for u in units)
    scratch = [pltpu.SemaphoreType.DMA((nsem,)), pltpu.SemaphoreType.DMA((nsem,))] if units else []
    return ins, outs, alias, scratch


def _split_units(units, res):
    out, oo = [], 0
    for u in units:
        out.append(list(res[oo:oo + len(u["outs"])]))
        oo += len(u["outs"])
    return out


def _comm_call(name, units):
    ins, outs, alias, scratch = _carry(units, 0, 0)

    def body(*refs):
        ci, co = refs[:len(ins)], refs[len(ins):len(ins) + len(outs)]
        _comm_phase(units, ci, co, refs[-2], refs[-1], True)
        _comm_phase(units, ci, co, refs[-2], refs[-1], False)

    res = pl.pallas_call(body, name=name, out_shape=outs, in_specs=[HBM] * len(ins), out_specs=[HBM] * len(outs),
                         scratch_shapes=scratch, input_output_aliases=alias)(*ins)
    return _split_units(units, res)


SEM = pl.BlockSpec(memory_space=pltpu.SEMAPHORE)
EFFECT = pltpu.SideEffectType.DATAFLOW_SIDE_EFFECTING


def _unit_start(unit, name, after=()):
    bufs = list(unit["ins"]) + [lax.empty(o.shape, o.dtype) for o in unit["outs"]]
    n_i, n_b, ns = len(unit["ins"]), len(bufs), unit["nsem"]

    def body(*refs):
        send_sems, recv_sems = refs[n_b + len(after)], refs[n_b + len(after) + 1]
        for st, _ in unit["copies"](refs[:n_i], refs[n_i:n_b], lambda k: (send_sems.at[k], recv_sems.at[k])):
            st.start()
        refs[-1][...] = jnp.zeros_like(refs[-1])

    res = pl.pallas_call(
        body, name=name,
        out_shape=[pltpu.SemaphoreType.DMA((ns,)), pltpu.SemaphoreType.DMA((ns,))] + [pltpu.HBM(b.shape, b.dtype) for b in bufs]
        + [jax.ShapeDtypeStruct((8, 128), F32)],
        in_specs=[HBM] * n_b + [pl.BlockSpec(memory_space=pl.ANY)] * len(after), out_specs=[SEM, SEM] + [HBM] * n_b + [VMEM_SPEC],
        input_output_aliases={i: 2 + i for i in range(n_b)},
        compiler_params=pltpu.CompilerParams(has_side_effects=EFFECT),
    )(*[pltpu.with_memory_space_constraint(b, pltpu.HBM) for b in bufs], *after)
    return res[0], res[1], list(res[2:2 + n_b]), res[-1]


def _unit_wait(unit, send_sems, recv_sems, bufs, after, name):
    n_i, n_b = len(unit["ins"]), len(bufs)

    def body(*refs):
        ss, rs = refs[n_b], refs[n_b + 1]
        for st, arrival in unit["copies"](refs[:n_i], refs[n_i:n_b], lambda k: (ss.at[k], rs.at[k])):
            st.wait_send()
            arrival.wait_recv()

    res = pl.pallas_call(
        body, name=name, out_shape=[pltpu.HBM(b.shape, b.dtype) for b in bufs],
        in_specs=[HBM] * n_b + [SEM, SEM] + [pl.BlockSpec(memory_space=pl.ANY)] * len(after), out_specs=[HBM] * n_b,
        input_output_aliases={i: i for i in range(n_b)}, compiler_params=pltpu.CompilerParams(has_side_effects=EFFECT),
    )(*bufs, send_sems, recv_sems, *after)
    return list(res[:n_i]), list(res[n_i:])


def _pair_add(g, land, core, name):
    _, rows, cols = g.shape
    half = rows // 2
    rb = _tile(half, 256, 16)
    nb = half // rb

    def body(c_ref, g_ref, l_ref, o_ref):
        o_ref[...] = (g_ref[...] + l_ref[...]).astype(BF16)

    return pl.pallas_call(
        body, name=name,
        grid_spec=pltpu.PrefetchScalarGridSpec(
            num_scalar_prefetch=1, grid=(4, nb),
            in_specs=[pl.BlockSpec((1, rb, cols), lambda s, i, c_ref: (s, c_ref[0] * nb + i, 0)),
                      pl.BlockSpec((1, rb, cols), lambda s, i, c_ref: (s, i, 0))],
            out_specs=pl.BlockSpec((1, rb, cols), lambda s, i, c_ref: (s, i, 0))),
        out_shape=jax.ShapeDtypeStruct((4, half, cols), BF16),
        compiler_params=_params(("parallel", "parallel")),
    )(core, g, land)


def _chip_sum(t, r, chip, name):
    _, half, cols = t.shape
    rb = _tile(half, 256, 16)

    def body(s_ref, t_ref, r_ref, o_ref):
        o_ref[...] = ((t_ref[0].astype(F32) + r_ref[0].astype(F32)) + r_ref[1].astype(F32)) + r_ref[2].astype(F32)

    return pl.pallas_call(
        body, name=name,
        grid_spec=pltpu.PrefetchScalarGridSpec(
            num_scalar_prefetch=1, grid=(half // rb,),
            in_specs=[pl.BlockSpec((1, rb, cols), lambda i, s_ref: (s_ref[0], i, 0)),
                      pl.BlockSpec((3, rb, cols), lambda i, s_ref: (0, i, 0))],
            out_specs=pl.BlockSpec((rb, cols), lambda i, s_ref: (i, 0))),
        out_shape=jax.ShapeDtypeStruct((half, cols), F32),
        compiler_params=_params(("parallel",)),
    )(chip, t, r)


def _adam_math(wv, gv, mv, vv):
    mn = ADAM_B1 * mv + (1.0 - ADAM_B1) * gv
    vn = ADAM_B2 * vv + (1.0 - ADAM_B2) * (gv * gv)
    m_hat = mn / (1.0 - ADAM_B1 ** ADAM_STEP)
    v_hat = vn / (1.0 - ADAM_B2 ** ADAM_STEP)
    return -ADAM_LR * (m_hat / (jnp.sqrt(v_hat) + ADAM_EPS) + ADAM_WD * wv), mn, vn


def _adamw_halves(wt, mt, vt, mine, theirs, core, name):
    _, rows, cols = wt.shape
    half = rows // 2
    rb = _tile(half, 256, 8)
    nb = half // rb

    def body(c_ref, w_ref, m_ref, v_ref, a_ref, b_ref, g_ref, d_ref, mo_ref, vo_ref):
        gv = jnp.where(pl.program_id(0) == c_ref[0], a_ref[...], b_ref[...])
        dl, mn, vn = _adam_math(w_ref[...], gv, m_ref[...], v_ref[...])
        g_ref[...] = gv
        d_ref[...] = dl
        mo_ref[...] = mn
        vo_ref[...] = vn

    full = pl.BlockSpec((None, rb, cols), lambda hf, i, c_ref: (0, hf * nb + i, 0))
    part = pl.BlockSpec((rb, cols), lambda hf, i, c_ref: (i, 0))
    return pl.pallas_call(
        body, name=name,
        grid_spec=pltpu.PrefetchScalarGridSpec(num_scalar_prefetch=1, grid=(2, nb), in_specs=[full, full, full, part, part],
                                               out_specs=[full] * 4),
        out_shape=[jax.ShapeDtypeStruct((1, rows, cols), F32)] * 4,
        compiler_params=_params(("parallel", "parallel")),
    )(core, wt, mt, vt, mine, theirs)


SG_REP = 144
SG_LOSS = 136
SG_W2, SG_CW = SG_REP, SG_REP + 4 * 16
SG_ROWS = SG_CW + 4 * 40
SP_ROWS = SG_REP + 16 + 40


def _mod_shard(c_all, ada_w_sh):
    def body(c_ref, w_ref, o_ref):
        cv = c_ref[...]
        o_ref[...] = _dg((cv * _sigmoid(cv)).astype(BF16), w_ref[...].astype(BF16), 1, 0)

    return pl.pallas_call(body, name="mod_shard", out_shape=jax.ShapeDtypeStruct((8, 1536), F32),
                          in_specs=[VMEM_SPEC, VMEM_SPEC], out_specs=VMEM_SPEC,
                          compiler_params=pltpu.CompilerParams(vmem_limit_bytes=VMEM_LIMIT))(c_all, ada_w_sh)


def _mod_select(mod_all, ada_b4):
    def body(m_ref, b_ref, o_ref):
        x, y, c = _me()
        me = 4 * x + 2 * y + c
        for sh in range(4):
            o_ref[sh] = m_ref[2 * sh, me] + b_ref[sh]

    return pl.pallas_call(body, name="mod_select", out_shape=jax.ShapeDtypeStruct((4, 12, 128), F32),
                          in_specs=[VMEM_SPEC, VMEM_SPEC], out_specs=VMEM_SPEC)(mod_all, ada_b4)


def _small_reduce(sg_all):
    def body(g_ref, o_ref):
        x, y, c = _me()
        s_me = 2 * x + y
        w2_rows = pl.ds(pl.multiple_of(SG_W2 + 16 * s_me, 8), 16)
        cw_rows = pl.ds(pl.multiple_of(SG_CW + 40 * s_me, 8), 40)
        a = g_ref[0, 0:SG_REP, :]
        b = g_ref[0, w2_rows, :]
        d = g_ref[0, cw_rows, :]
        for dev in range(1, 8):
            a = a + g_ref[dev, 0:SG_REP, :]
            b = b + g_ref[dev, w2_rows, :]
            d = d + g_ref[dev, cw_rows, :]
        o_ref[0:SG_REP, :] = a
        o_ref[SG_REP:SG_REP + 16, :] = b
        o_ref[SG_REP + 16:SP_ROWS, :] = d

    return pl.pallas_call(body, name="small_grad_reduce", out_shape=jax.ShapeDtypeStruct((SP_ROWS, 128), F32),
                          in_specs=[VMEM_SPEC], out_specs=VMEM_SPEC)(sg_all)


def _ada_grad(dmod_all, c_bc):
    def body(g_ref, c_ref, o_ref):
        x, y, c = _me()
        s_me = 2 * x + y
        for k in range(12):
            acc = jnp.zeros((D, 128), F32)
            for b in range(8):
                cv = c_ref[b]
                acc = acc + (cv * _sigmoid(cv)) * g_ref[s_me, k, b:b + 1, :]
            o_ref[:, k * 128:(k + 1) * 128] = acc

    return pl.pallas_call(body, name="ada_w_grad", out_shape=jax.ShapeDtypeStruct((D, 1536), F32),
                          in_specs=[VMEM_SPEC, VMEM_SPEC], out_specs=VMEM_SPEC,
                          compiler_params=pltpu.CompilerParams(vmem_limit_bytes=VMEM_LIMIT))(dmod_all, c_bc)


def _adamw(wt, g, m, v, name):
    rows, cols = wt.shape
    rb = _tile(rows, 256, 8)

    def fn(c, i, wv, gv, mv, vv):
        return _adam_math(wv, gv, mv, vv)

    return _rowcall(fn, [_rows(t, rb) for t in (wt, g, m, v)], [_orow(rows, cols, F32, rb)] * 3,
                    n_rows=rows, rb=rb, name=name)


def _pad_rows(t, rows):
    flat = t.reshape(-1)
    return jnp.pad(flat, (0, rows * 128 - flat.shape[0])).reshape(rows, 128)


SP_LAYOUT = (("ada_b", 48), ("norm1_w", 8), ("gla_gate_b", 8), ("gla_norm_w", 8), ("norm2_w", 8), ("conv_b", 48),
             ("final_norm_w", 8), (None, 8), ("gla_gate_w2", 16), ("conv_w", 40))


def _pack_small(d):
    return jnp.concatenate([jnp.zeros((rows, 128), F32) if n is None else _pad_rows(d[n].astype(F32), rows)
                            for n, rows in SP_LAYOUT], axis=0)


def _unpack_small(pk, shapes):
    out, off = {}, 0
    for n, rows in SP_LAYOUT:
        if n is not None:
            shp = shapes[n]
            out[n] = pk[off:off + rows].reshape(-1)[:math.prod(shp)].reshape(shp)
        off += rows
    return out


def kernel(x, c, positions, ada_w, ada_b, norm1_w, w_in, gla_gate_w2, gla_gate_b, gla_norm_w, w_gla_branch, w_attn_branch, w_out, norm2_w, w_up, conv_w, conv_b, w_down, final_norm_w, loss_target, m_ada_w, m_ada_b, m_norm1_w, m_w_in, m_gla_gate_w2, m_gla_gate_b, m_gla_norm_w, m_w_gla_branch, m_w_attn_branch, m_w_out, m_norm2_w, m_w_up, m_conv_w, m_conv_b, m_w_down, m_final_norm_w, v_ada_w, v_ada_b, v_norm1_w, v_w_in, v_gla_gate_w2, v_gla_gate_b, v_gla_norm_w, v_w_gla_branch, v_w_attn_branch, v_w_out, v_norm2_w, v_w_up, v_conv_w, v_conv_b, v_w_down, v_final_norm_w):
    s = x.shape[1]
    names = ("ada_w", "ada_b", "norm1_w", "w_in", "gla_gate_w2", "gla_gate_b", "gla_norm_w", "w_gla_branch", "w_attn_branch",
             "w_out", "norm2_w", "w_up", "conv_w", "conv_b", "w_down", "final_norm_w")
    wts = dict(zip(names, (ada_w, ada_b, norm1_w, w_in, gla_gate_w2, gla_gate_b, gla_norm_w, w_gla_branch, w_attn_branch,
                           w_out, norm2_w, w_up, conv_w, conv_b, w_down, final_norm_w)))
    ms = dict(zip(names, (m_ada_w, m_ada_b, m_norm1_w, m_w_in, m_gla_gate_w2, m_gla_gate_b, m_gla_norm_w, m_w_gla_branch,
                          m_w_attn_branch, m_w_out, m_norm2_w, m_w_up, m_conv_w, m_conv_b, m_w_down, m_final_norm_w)))
    vs = dict(zip(names, (v_ada_w, v_ada_b, v_norm1_w, v_w_in, v_gla_gate_w2, v_gla_gate_b, v_gla_norm_w, v_w_gla_branch,
                          v_w_attn_branch, v_w_out, v_norm2_w, v_w_up, v_conv_w, v_conv_b, v_w_down, v_final_norm_w)))

    pk0 = jnp.concatenate([_pad_rows(c, 8), _pad_rows(gla_gate_w2, 16), _pad_rows(conv_w, 40)], axis=0)
    sm_all = _allgather8(pk0, "gather_small")
    c_all = sm_all[:, 0:8, :].reshape(8, D)
    w2_full = sm_all[0::2, 8:24, :].transpose(1, 0, 2).reshape(GLA_LR, 512)
    cw_full = sm_all[0::2, 24:64, :].reshape(4, 40 * 128)[:, :3 * W_UP_SH].reshape(4, 3, W_UP_SH).transpose(1, 0, 2).reshape(3, 2 * D_FF)

    mod_sh = _mod_shard(c_all, ada_w[0])
    mod_all = _allgather8(mod_sh.reshape(96, 128), "gather_mod")

    w_sh = [wts[n].astype(BF16) for n in BIG]
    u_g0 = _u_gather_ici(w_sh, (0,))
    g0 = (u_g0,) + _unit_start(u_g0, "gather_w_in_start", after=[mod_all])
    mod = _mod_select(mod_all.reshape(8, 8, 12, 128) + g0[4][0, 0], ada_b.reshape(4, 12, 128)).reshape(6, D)

    core = lax.axis_index("c").astype(jnp.int32).reshape(1)
    chip = (2 * lax.axis_index("x") + lax.axis_index("y")).astype(jnp.int32)
    sm = dict(n1w=norm1_w, n2w=norm2_w, fnw=final_norm_w.reshape(1, D), gnw=gla_norm_w, gb=gla_gate_b,
              w2=jnp.pad(w2_full, ((0, 128 - GLA_LR), (0, 0))), cw=_ff_to_kernel(cw_full), cb=_ff_to_kernel(conv_b))
    loss, grad_x, halves, others, small, ts0 = _local_step(x[0], mod, positions.reshape(s, 1), loss_target[0], sm, w_sh,
                                                               g0, chip, core)

    dcw = _ff_from_kernel(small["cw"]).reshape(3, 4, W_UP_SH).transpose(1, 0, 2)
    dw2 = small["w2"][:GLA_LR].reshape(GLA_LR, 4, 128).transpose(1, 0, 2)
    sg = jnp.concatenate(
        [_pad_rows(small["dmod"], 48), _pad_rows(small["n1w"], 8), _pad_rows(small["gb"], 8), _pad_rows(small["gnw"], 8),
         _pad_rows(small["n2w"], 8), _pad_rows(_ff_from_kernel(small["cb"]), 48), _pad_rows(small["fnw"], 8), _pad_rows(loss, 8)]
        + [_pad_rows(dw2[k], 16) for k in range(4)] + [_pad_rows(dcw[k], 40) for k in range(4)], axis=0)
    sg_all = _allgather8(sg, "gather_small_grads")
    u_ex = _u_chip_exchange([ts0])
    pending = (u_ex,) + _unit_start(u_ex, "grad_exchange_w_in_start", after=[sg_all])
    sg_all = sg_all + pending[4][0, 0]
    g_small_pk = _small_reduce(sg_all)
    dmod_all = sg_all[:, 0:48, :].reshape(8, 4, 12, 128).transpose(1, 2, 0, 3)
    g_ada_w = _ada_grad(dmod_all, jnp.broadcast_to(c_all[:, :, None], (8, D, 128)))

    shapes = {n: wts[n].shape for n in names}
    g_small = _unpack_small(g_small_pk, shapes)
    grads = {"ada_w": g_ada_w.reshape(1, D, 1536), **g_small}
    deltas, new_m, new_v = {}, {}, {}
    for n, mine, theirs in zip(BIG[1:], halves, others):
        grads[n], deltas[n], new_m[n], new_v[n] = _adamw_halves(wts[n], ms[n], vs[n], mine, theirs, core, "adamw_" + n)
    shp = ada_w.shape
    d_, m_, v_ = _adamw(ada_w[0], g_ada_w, m_ada_w[0], v_ada_w[0], "adamw_ada_w")
    deltas["ada_w"], new_m["ada_w"], new_v["ada_w"] = d_.reshape(shp), m_.reshape(shp), v_.reshape(shp)
    d_, m_, v_ = _adamw(_pack_small(wts), g_small_pk, _pack_small(ms), _pack_small(vs), "adamw_small")
    for dst, pk in ((deltas, d_), (new_m, m_), (new_v, v_)):
        dst.update(_unpack_small(pk, shapes))

    [t0], [r0] = _unit_wait(*pending[:4], after=[d_, deltas["ada_w"], deltas["w_up"], deltas["w_down"]], name="grad_exchange_w_in_wait")
    half0 = _chip_sum(t0, r0, chip.reshape(1), "grad_chip_sum_w_in")
    [[oth0]] = _comm_call("grad_join_w_in", [_u_pair_join([half0])])
    grads["w_in"], deltas["w_in"], new_m["w_in"], new_v["w_in"] = _adamw_halves(w_in, m_w_in, v_w_in, half0, oth0, core, "adamw_w_in")

    return (g_small_pk[SG_LOSS, 0], grad_x.reshape(1, s, D), *[grads[n] for n in names], *[deltas[n] for n in names],
            *[new_m[n] for n in names], *[new_v[n] for n in names])
```

```python
import math

import jax
import jax.numpy as jnp
from jax import lax
from jax.experimental import pallas as pl
from jax.experimental.pallas import tpu as pltpu

F32, BF16 = jnp.float32, jnp.bfloat16
MESH = pl.DeviceIdType.MESH

D = 1024
EPS = 1e-6
GLA_H, GLA_DK, GLA_DV, GLA_LR = 4, 128, 256, 16
GLA_TAU = 16.0
GLA_CHUNK = 64
GLA_BLOCK = 512
ATT_GROUPS = ((128, 1), (512, 4), (2048, 16))
ATT_BLK = 128
ATT_HD = 64
ATT_W = 768
D_FF = 2816
ROPE_THETA = 10000.0
P_W = 7680
P_GV, P_GR, P_MA, P_MB, P_GQ, P_GK, P_AQ, P_AK, P_AV, P_LR = 0, 1024, 2048, 3072, 4096, 4608, 5120, 5888, 6656, 7424
W_IN = 7440
W_IN_SH, W_UP_SH, W_DOWN_SH = 1860, 1408, 704
VMEM_LIMIT = 56 * 1024 * 1024
ADAM_LR, ADAM_B1, ADAM_B2, ADAM_EPS, ADAM_WD, ADAM_STEP = 0.001, 0.9, 0.999, 1e-08, 0.01, 10
NEG = -1e30


def _tile(n, target, unit=128):
    best = None
    for t in range(unit, min(n, target) + 1, unit):
        if n % t == 0:
            best = t
    return best or n


def _params(sem):
    return pltpu.CompilerParams(dimension_semantics=sem, vmem_limit_bytes=VMEM_LIMIT)


def _dg(a, b, ca, cb):
    return lax.dot_general(a, b, (((ca,), (cb,)), ((), ())), preferred_element_type=F32)


def _sigmoid(v):
    return 1.0 / (1.0 + jnp.exp(-v))


def _ff_block(j):
    return (j % 2) * 2 + j // 2


def _mm(a, b, name, *, ta=False, tb=False, out_dtype=BF16, tm=1024, tn=1536, tk=1024, n_outer=True, comm=(),
        b_shards=False, o_shards=False):
    m = a.shape[1] if ta else a.shape[0]
    k = a.shape[0] if ta else a.shape[1]
    if b_shards:
        n = b.shape[1] if tb else 4 * W_UP_SH
        tn, tk = (tn, W_UP_SH) if tb else (W_UP_SH, tk)
    else:
        n = b.shape[0] if tb else b.shape[1]
    if o_shards:
        tn = W_UP_SH
    tm, tn, tk = _tile(m, tm), _tile(n, tn), _tile(k, tk)
    nm, nn, nk = m // tm, n // tn, k // tk
    in_out = out_dtype == F32
    c_ins, c_outs, c_alias, c_scratch = _carry(comm, 2, 1)

    def body(a_ref, b_ref, *rest):
        ci, o_ref, co = rest[:len(c_ins)], rest[len(c_ins)], rest[len(c_ins) + 1:len(c_ins) + 1 + len(c_outs)]
        scr = rest[len(c_ins) + 1 + len(c_outs):]
        kk = pl.program_id(2)
        if comm:
            step = (pl.program_id(0) * (nm if n_outer else nn) + pl.program_id(1)) * nk + kk

            @pl.when(step == 0)
            def _():
                _comm_phase(comm, ci, co, scr[-2], scr[-1], True)

        _mm_step(a_ref, b_ref, o_ref, scr, kk)
        if comm:
            @pl.when(step == nm * nn * nk - 1)
            def _():
                _comm_phase(comm, ci, co, scr[-2], scr[-1], False)

    def _mm_step(a_ref, b_ref, o_ref, scr, kk):
        p = _dg(a_ref[...].astype(BF16), b_ref[...].astype(BF16), 0 if ta else 1, 1 if tb else 0)
        if nk == 1:
            o_ref[...] = p.astype(o_ref.dtype)
        else:
            acc = o_ref if in_out else scr[0]

            @pl.when(kk == 0)
            def _():
                acc[...] = p

            @pl.when(kk > 0)
            def _():
                acc[...] += p

            if not in_out:
                @pl.when(kk == nk - 1)
                def _():
                    o_ref[...] = acc[...].astype(o_ref.dtype)

    if n_outer:
        ij = lambda g0, g1: (g1, g0)
        grid = (nn, nm, nk)
    else:
        ij = lambda g0, g1: (g0, g1)
        grid = (nm, nn, nk)
    a_map = (lambda g0, g1, kk: (kk, ij(g0, g1)[0])) if ta else (lambda g0, g1, kk: (ij(g0, g1)[0], kk))
    if b_shards and tb:
        b_spec = pl.BlockSpec((None, tn, tk), lambda g0, g1, kk: (_ff_block(kk), ij(g0, g1)[1], 0))
    elif b_shards:
        b_spec = pl.BlockSpec((None, tk, tn), lambda g0, g1, kk: (_ff_block(ij(g0, g1)[1]), kk, 0))
    elif tb:
        b_spec = pl.BlockSpec((tn, tk), lambda g0, g1, kk: (ij(g0, g1)[1], kk))
    else:
        b_spec = pl.BlockSpec((tk, tn), lambda g0, g1, kk: (kk, ij(g0, g1)[1]))
    if o_shards:
        o_spec = pl.BlockSpec((None, tm, tn), lambda g0, g1, kk: (_ff_block(ij(g0, g1)[1]), ij(g0, g1)[0], 0))
        o_shape = jax.ShapeDtypeStruct((4, m, W_UP_SH), out_dtype)
    else:
        o_spec = pl.BlockSpec((tm, tn), lambda g0, g1, kk: ij(g0, g1))
        o_shape = jax.ShapeDtypeStruct((m, n), out_dtype)
    res = pl.pallas_call(
        body, name=name, grid=grid,
        in_specs=[pl.BlockSpec((tk, tm) if ta else (tm, tk), a_map), b_spec] + [HBM] * len(c_ins),
        out_specs=[o_spec] + [HBM] * len(c_outs),
        out_shape=[o_shape] + c_outs,
        scratch_shapes=([] if (in_out or nk == 1) else [pltpu.VMEM((tm, tn), F32)]) + c_scratch,
        input_output_aliases=c_alias,
        compiler_params=_params(("arbitrary",) * 3 if comm else ("parallel", "parallel", "arbitrary")),
    )(a, b, *c_ins)
    return (res[0], _split_units(comm, res[1:])) if comm else res[0]


def _rows(arr, rb, w=None, j=0):
    w = arr.shape[1] if w is None else w
    if callable(j):
        return arr, pl.BlockSpec((rb, w), lambda c, i: (i, j(c)))
    return arr, pl.BlockSpec((rb, w), lambda c, i: (i, j))


def _full(arr, w=None, j=0):
    w = arr.shape[1] if w is None else w
    if callable(j):
        return arr, pl.BlockSpec((arr.shape[0], w), lambda c, i: (0, j(c)))
    return arr, pl.BlockSpec((arr.shape[0], w), lambda c, i: (0, j))


def _halo(arr, rb, hb, w, j, before):
    per = rb // hb
    last = arr.shape[0] // hb - 1
    if before:
        rmap = lambda i: jnp.maximum(i * per - 1, 0)
    else:
        rmap = lambda i: jnp.minimum((i + 1) * per, last)
    return arr, pl.BlockSpec((hb, w), lambda c, i: (rmap(i), j(c) if callable(j) else j))


def _rowcall(fn, ins, outs, *, n_rows, rb, name, ncol=1, into=None, after=()):
    n_in = len(ins)
    nr = n_rows // rb
    unread = ([] if into is None else [into[0]]) + list(after)
    n_skip = len(unread)

    def body(*refs):
        c, i = pl.program_id(0), pl.program_id(1)
        res = fn(c, i, *[r[...] for r in refs[:n_in]])
        for val, spec, o_ref in zip(res, outs, refs[n_in + n_skip:]):
            if spec[2] == "row":
                o_ref[...] = val.astype(o_ref.dtype)
            else:
                @pl.when(i == 0)
                def _(o_ref=o_ref, val=val):
                    o_ref[...] = val.astype(o_ref.dtype)

                @pl.when(i > 0)
                def _(o_ref=o_ref, val=val):
                    o_ref[...] += val.astype(o_ref.dtype)

    out_specs = []
    for shape, dt, kind, block, col in outs:
        if kind == "row":
            out_specs.append(pl.BlockSpec(block, lambda c, i, col=col: (i, col(c))))
        else:
            out_specs.append(pl.BlockSpec(block, lambda c, i, col=col: (0, col(c))))
    return pl.pallas_call(
        body, name=name, grid=(ncol, nr),
        in_specs=[s for _, s in ins] + [pl.BlockSpec(memory_space=pl.ANY)] * n_skip, out_specs=out_specs,
        out_shape=[jax.ShapeDtypeStruct(o[0], o[1]) for o in outs],
        input_output_aliases={} if into is None else {n_in: into[1]},
        compiler_params=_params(("parallel", "arbitrary")),
    )(*[a for a, _ in ins], *unread)


def _orow(n_rows, w, dt, rb, bw=None, col=lambda c: 0):
    return ((n_rows, w), dt, "row", (rb, bw or w), col)


def _oacc(r, w, bw=None, col=lambda c: 0):
    return ((r, w), F32, "acc", (r, bw or w), col)


def _csum(v):
    return jnp.sum(v, axis=0, keepdims=True)


def _rms(v):
    return lax.rsqrt(jnp.mean(v * v, axis=-1, keepdims=True) + EPS)


def _norm_bwd(xv, dh, w, scale):
    r = _rms(xv)
    xh = xv * r
    dxh = dh * (w * (1.0 + scale))
    dx = r * (dxh - xh * jnp.mean(dxh * xh, axis=-1, keepdims=True))
    t = dh * xh
    return dx, _csum(dh), _csum(t * w), _csum(t * (1.0 + scale))


def _rope_tables(pos_col, invf, s):
    def fn(c, i, pos, f):
        ang = pos.astype(F32) * f
        lane = lax.broadcasted_iota(jnp.int32, ang.shape, 1)
        sign = jnp.where((lane % ATT_HD) < ATT_HD // 2, -1.0, 1.0)
        return jnp.cos(ang), jnp.sin(ang) * sign

    rb = 512
    return _rowcall(fn, [_rows(pos_col, rb), _full(invf)], [_orow(s, 128, F32, rb), _orow(s, 128, F32, rb)],
                    n_rows=s, rb=rb, name="rope_tables")


def _swap_halves(t):
    n = t.shape[1]
    lane = lax.broadcasted_iota(jnp.int32, t.shape, 1)
    return jnp.where((lane % ATT_HD) < ATT_HD // 2, pltpu.roll(t, n - 32, 1), pltpu.roll(t, 32, 1))


def _rope_apply(t, cos, sin_signed, inverse):
    cw = jnp.concatenate([cos] * (t.shape[1] // 128), axis=1)
    sw = jnp.concatenate([sin_signed] * (t.shape[1] // 128), axis=1)
    if inverse:
        sw = -sw
    return t * cw + _swap_halves(t) * sw


DIL_ROWS = 512


def _to_dilated(scr, val, out_ref, r):
    if r == 1:
        out_ref[...] = val.astype(out_ref.dtype)
        return
    n = val.shape[0] // r
    for hh in range(2):
        scr[hh] = val[:, hh * 128:(hh + 1) * 128]
        for pr in range(r):
            out_ref[:, pr * 256 + hh * 128:pr * 256 + (hh + 1) * 128] = scr[hh, pl.ds(pr, n, stride=r), :].astype(out_ref.dtype)


def _from_dilated(scr, in_ref, r):
    if r == 1:
        return in_ref[...].astype(F32)
    n = in_ref.shape[0]
    for hh in range(2):
        for pr in range(r):
            scr[hh, pl.ds(pr, n, stride=r), :] = in_ref[:, pr * 256 + hh * 128:pr * 256 + (hh + 1) * 128].astype(F32)
    return jnp.concatenate([scr[0], scr[1]], axis=1)


def _dil_spec(r):
    return pl.BlockSpec((DIL_ROWS // r, r * 256), lambda i: (i, 0))


def _dil_shape(s, r, dt):
    return jax.ShapeDtypeStruct((s // r, r * 256), dt)


_DIL_SCRATCH = [pltpu.VMEM((2, DIL_ROWS, 128), F32)]
_RS = tuple(r for _, r in ATT_GROUPS)


def _rope_fwd(p, cos_t, sin_t, s):
    def body(*refs):
        ins, cs, sn, outs, scr = refs[:9], refs[9][...], refs[10][...], refs[11:20], refs[20]
        for t in range(3):
            for g, r in enumerate(_RS):
                val = ins[3 * t + g][...].astype(F32)
                _to_dilated(scr, _rope_apply(val, cs, sn, False) if t < 2 else val, outs[3 * t + g], r)

    res = pl.pallas_call(
        body, name="rope", grid=(s // DIL_ROWS,),
        in_specs=[pl.BlockSpec((DIL_ROWS, 256), lambda i, c=base // 256 + g: (i, c)) for base in (P_AQ, P_AK, P_AV) for g in range(3)]
        + [pl.BlockSpec((DIL_ROWS, 128), lambda i: (i, 0))] * 2,
        out_specs=[_dil_spec(r) for _ in range(3) for r in _RS],
        out_shape=[_dil_shape(s, r, BF16) for _ in range(3) for r in _RS],
        scratch_shapes=_DIL_SCRATCH, compiler_params=_params(("parallel",)),
    )(*([p] * 9), cos_t, sin_t)
    return res[0:3], res[3:6], res[6:9]


def _attn_combine(att, s):
    def body(o0, o1, o2, l0, l1, l2, o_ref, lse_ref, od1, od2, ld1, ld2, scr):
        ov = [_from_dilated(scr, ref, r) for ref, r in zip((o0, o1, o2), _RS)]
        lv = [_from_dilated(scr, ref, r) for ref, r in zip((l0, l1, l2), _RS)]
        mx = jnp.maximum(jnp.maximum(lv[0], lv[1]), lv[2])
        ev = [jnp.exp(l - mx) for l in lv]
        z = ev[0] + ev[1] + ev[2]
        o = ((ev[0] * ov[0] + ev[1] * ov[1] + ev[2] * ov[2]) / z).astype(BF16)
        lse = mx + jnp.log(z)
        o_ref[...] = o
        lse_ref[...] = lse
        for ref, r in zip((od1, od2), _RS[1:]):
            _to_dilated(scr, o.astype(F32), ref, r)
        for ref, r in zip((ld1, ld2), _RS[1:]):
            _to_dilated(scr, lse, ref, r)

    return pl.pallas_call(
        body, name="attn_combine", grid=(s // DIL_ROWS,),
        in_specs=[_dil_spec(r) for r in _RS] * 2,
        out_specs=[_dil_spec(1)] * 2 + [_dil_spec(r) for r in _RS[1:]] * 2,
        out_shape=[_dil_shape(s, 1, BF16), _dil_shape(s, 1, F32)] + [_dil_shape(s, r, BF16) for r in _RS[1:]]
        + [_dil_shape(s, r, F32) for r in _RS[1:]],
        scratch_shapes=_DIL_SCRATCH, compiler_params=_params(("parallel",)),
    )(*[a[0] for a in att], *[a[1] for a in att])


def _dilate(t, s):
    def body(t_ref, o1, o2, scr):
        val = t_ref[...].astype(F32)
        for ref, r in zip((o1, o2), _RS[1:]):
            _to_dilated(scr, val, ref, r)

    return pl.pallas_call(
        body, name="attn_dilate", grid=(s // DIL_ROWS,), in_specs=[_dil_spec(1)], out_specs=[_dil_spec(r) for r in _RS[1:]],
        out_shape=[_dil_shape(s, r, t.dtype) for r in _RS[1:]], scratch_shapes=_DIL_SCRATCH, compiler_params=_params(("parallel",)),
    )(t)


def _rope_bwd(datt, d_glr, dp, cos_t, sin_t, s):
    tail = P_W - P_AQ

    def body(*refs):
        ins, cs, sn, glr, o_ref, scr = refs[:9], refs[9][...], refs[10][...], refs[11], refs[13], refs[14]
        for t in range(3):
            for g, r in enumerate(_RS):
                val = _from_dilated(scr, ins[3 * t + g], r)
                o_ref[:, t * ATT_W + g * 256:t * ATT_W + (g + 1) * 256] = (_rope_apply(val, cs, sn, True) if t < 2 else val).astype(BF16)
        o_ref[:, 3 * ATT_W:3 * ATT_W + 128] = glr[...]
        o_ref[:, 3 * ATT_W + 128:] = jnp.zeros((DIL_ROWS, tail - 3 * ATT_W - 128), BF16)

    return pl.pallas_call(
        body, name="rope_bwd", grid=(s // DIL_ROWS,),
        in_specs=[_dil_spec(r) for _ in range(3) for r in _RS] + [pl.BlockSpec((DIL_ROWS, 128), lambda i: (i, 0))] * 3
        + [pl.BlockSpec(memory_space=pl.ANY)],
        out_specs=pl.BlockSpec((DIL_ROWS, tail), lambda i: (i, P_AQ // tail)),
        out_shape=jax.ShapeDtypeStruct((s, P_W), BF16), input_output_aliases={12: 0},
        scratch_shapes=_DIL_SCRATCH, compiler_params=_params(("parallel",)),
    )(*[datt[g][t] for t in range(3) for g in range(3)], cos_t, sin_t, d_glr, dp)


def _tri_dot(tri, t):
    tb = tri.astype(BF16)
    hi = t.astype(BF16)
    r1 = t - hi.astype(F32)
    mid = r1.astype(BF16)
    lo = (r1 - mid.astype(F32)).astype(BF16)
    return _dg(tb, hi, 1, 0) + _dg(tb, mid, 1, 0) + _dg(tb, lo, 1, 0)


def _gla_decays(la_c, tri):
    b = _tri_dot(tri, la_c)
    row = lax.broadcasted_iota(jnp.int32, b.shape, 0)
    bmid = jnp.sum(jnp.where(row == GLA_CHUNK // 2 - 1, b, 0.0), axis=0, keepdims=True)
    blast = jnp.sum(jnp.where(row == GLA_CHUNK - 1, b, 0.0), axis=0, keepdims=True)
    return b, bmid, blast


def _gla_fwd(p, la, s, comm=()):
    tb, ch = GLA_BLOCK, GLA_CHUNK
    nb, nc = s // tb, tb // ch
    scale = GLA_DK ** -0.5
    c_ins, c_outs, c_alias, c_scratch = _carry(comm, 4, 2)

    def body(q_ref, k_ref, v_ref, la_ref, *rest):
        ci, (o_ref, st_ref) = rest[:len(c_ins)], rest[len(c_ins):len(c_ins) + 2]
        co, state = rest[len(c_ins) + 2:len(c_ins) + 2 + len(c_outs)], rest[len(c_ins) + 2 + len(c_outs)]
        step = pl.program_id(0)
        if comm:
            @pl.when(step == 0)
            def _():
                _comm_phase(comm, ci, co, rest[-2], rest[-1], True)

        _gla_fwd_step(q_ref, k_ref, v_ref, la_ref, o_ref, st_ref, state)
        if comm:
            @pl.when(step == nb - 1)
            def _():
                _comm_phase(comm, ci, co, rest[-2], rest[-1], False)

    def _gla_fwd_step(q_ref, k_ref, v_ref, la_ref, o_ref, st_ref, state):
        @pl.when(pl.program_id(0) == 0)
        def _():
            state[...] = jnp.zeros_like(state)

        ri = lax.broadcasted_iota(jnp.int32, (ch, ch), 0)
        ci = lax.broadcasted_iota(jnp.int32, (ch, ch), 1)
        causal = ci <= ri
        tri = causal.astype(F32)

        def chunk(c, carry):
            sl = pl.ds(pl.multiple_of(c * ch, ch), ch)
            b, bmid, blast = _gla_decays(la_ref[sl, :], tri)
            q = q_ref[sl, :].astype(F32) * scale
            k = k_ref[sl, :].astype(F32)
            v = v_ref[sl, :]
            qgt = (q * jnp.exp(b)).astype(BF16)
            qgn = (q * jnp.exp(b - bmid)).astype(BF16)
            kgn = (k * jnp.exp(bmid - b)).astype(BF16)
            kd = (k * jnp.exp(blast - b)).astype(BF16)
            dec = jnp.exp(blast)
            sts = [state[h] for h in range(GLA_H)]
            outs, news = [], []
            for h in range(GLA_H):
                hk, hv = slice(h * GLA_DK, (h + 1) * GLA_DK), slice(h * GLA_DV, (h + 1) * GLA_DV)
                a = jnp.where(causal, _dg(qgn[:, hk], kgn[:, hk], 1, 1), 0.0)
                outs.append(_dg(a.astype(BF16), v[:, hv], 1, 0) + _dg(qgt[:, hk], sts[h].astype(BF16), 1, 1))
                news.append(dec[:, hk] * sts[h] + _dg(v[:, hv], kd[:, hk], 0, 0))
            for h in range(GLA_H):
                st_ref[h, c] = sts[h]
                state[h] = news[h]
            o_ref[sl, :] = jnp.concatenate(outs, axis=1)
            return carry

        lax.fori_loop(0, nc, chunk, 0)

    hw = GLA_H * GLA_DK
    res = pl.pallas_call(
        body, name="gla_fwd", grid=(nb,),
        in_specs=[pl.BlockSpec((tb, hw), lambda t: (t, P_GQ // hw)),
                  pl.BlockSpec((tb, hw), lambda t: (t, P_GK // hw)),
                  pl.BlockSpec((tb, GLA_H * GLA_DV), lambda t: (t, P_GV // (GLA_H * GLA_DV))),
                  pl.BlockSpec((tb, hw), lambda t: (t, 0))] + [HBM] * len(c_ins),
        out_specs=[pl.BlockSpec((tb, GLA_H * GLA_DV), lambda t: (t, 0)),
                   pl.BlockSpec((GLA_H, nc, GLA_DV, GLA_DK), lambda t: (0, t, 0, 0))] + [HBM] * len(c_outs),
        out_shape=[jax.ShapeDtypeStruct((s, GLA_H * GLA_DV), F32),
                   jax.ShapeDtypeStruct((GLA_H, s // ch, GLA_DV, GLA_DK), F32)] + c_outs,
        scratch_shapes=[pltpu.VMEM((GLA_H, GLA_DV, GLA_DK), F32)] + c_scratch,
        input_output_aliases=c_alias,
        compiler_params=_params(("arbitrary",)),
    )(p, p, p, la, *c_ins)
    return res[0], res[1], _split_units(comm, res[2:])


def _gla_bwd(p, la, states, do, s, dp, comm=()):
    tb, ch = GLA_BLOCK, GLA_CHUNK
    nb, nc = s // tb, tb // ch
    scale = GLA_DK ** -0.5
    c_ins, c_outs, c_alias, c_scratch = _carry(comm, 7, 4)

    def body(q_ref, k_ref, v_ref, la_ref, st_ref, do_ref, dp_in, *rest):
        ci, outs = rest[:len(c_ins)], rest[len(c_ins):len(c_ins) + 4]
        co, dstate = rest[len(c_ins) + 4:len(c_ins) + 4 + len(c_outs)], rest[len(c_ins) + 4 + len(c_outs)]
        step = pl.program_id(0)
        if comm:
            @pl.when(step == 0)
            def _():
                _comm_phase(comm, ci, co, rest[-2], rest[-1], True)

        _gla_bwd_step(q_ref, k_ref, v_ref, la_ref, st_ref, do_ref, *outs, dstate)
        if comm:
            @pl.when(step == nb - 1)
            def _():
                _comm_phase(comm, ci, co, rest[-2], rest[-1], False)

    def _gla_bwd_step(q_ref, k_ref, v_ref, la_ref, st_ref, do_ref, dq_ref, dk_ref, dv_ref, dla_ref, dstate):
        @pl.when(pl.program_id(0) == 0)
        def _():
            dstate[...] = jnp.zeros_like(dstate)

        ri = lax.broadcasted_iota(jnp.int32, (ch, ch), 0)
        ci = lax.broadcasted_iota(jnp.int32, (ch, ch), 1)
        causal = ci <= ri
        tri = causal.astype(F32)
        tri_t = (ci >= ri).astype(F32)

        def chunk(cc, carry):
            c = nc - 1 - cc
            sl = pl.ds(pl.multiple_of(c * ch, ch), ch)
            b, bmid, blast = _gla_decays(la_ref[sl, :], tri)
            q = q_ref[sl, :].astype(F32) * scale
            k = k_ref[sl, :].astype(F32)
            v = v_ref[sl, :]
            e_b, e_qn, e_kn, e_kd = jnp.exp(b), jnp.exp(b - bmid), jnp.exp(bmid - b), jnp.exp(blast - b)
            dec = jnp.exp(blast)
            qgt, qgn, kgn, kd = q * e_b, q * e_qn, k * e_kn, k * e_kd
            qgt_b, qgn_b, kgn_b, kd_b = qgt.astype(BF16), qgn.astype(BF16), kgn.astype(BF16), kd.astype(BF16)
            do_b = do_ref[sl, :].astype(BF16)
            st0s = [st_ref[h, c] for h in range(GLA_H)]
            dsts = [dstate[h] for h in range(GLA_H)]
            dqgn, dqgt, dkgn, dkd, dvs, ddec, news = [], [], [], [], [], [], []
            for h in range(GLA_H):
                hk, hv = slice(h * GLA_DK, (h + 1) * GLA_DK), slice(h * GLA_DV, (h + 1) * GLA_DV)
                dst_b = dsts[h].astype(BF16)
                a = jnp.where(causal, _dg(qgn_b[:, hk], kgn_b[:, hk], 1, 1), 0.0).astype(BF16)
                da = jnp.where(causal, _dg(do_b[:, hv], v[:, hv], 1, 1), 0.0).astype(BF16)
                dqgn.append(_dg(da, kgn_b[:, hk], 1, 0))
                dqgt.append(_dg(do_b[:, hv], st0s[h].astype(BF16), 1, 0))
                dkgn.append(_dg(da, qgn_b[:, hk], 0, 0))
                dvs.append(_dg(a, do_b[:, hv], 0, 0) + _dg(kd_b[:, hk], dst_b, 1, 1))
                dkd.append(_dg(v[:, hv], dst_b, 1, 0))
                ddec.append(jnp.sum(st0s[h] * dsts[h], axis=0, keepdims=True))
                news.append(dec[:, hk] * dsts[h] + _dg(do_b[:, hv], qgt_b[:, hk], 0, 0))
            for h in range(GLA_H):
                dstate[h] = news[h]
            cat = lambda parts: jnp.concatenate(parts, axis=1)
            dqgn, dqgt, dkgn, dkd, ddec = cat(dqgn), cat(dqgt), cat(dkgn), cat(dkd), cat(ddec)
            dq_ref[sl, :] = (scale * (dqgn * e_qn + dqgt * e_b)).astype(dq_ref.dtype)
            dk_ref[sl, :] = (dkgn * e_kn + dkd * e_kd).astype(dk_ref.dtype)
            dv_ref[sl, :] = cat(dvs).astype(dv_ref.dtype)
            db = dqgn * qgn + dqgt * qgt - dkgn * kgn - dkd * kd
            extra = jnp.sum(dkd * kd, axis=0, keepdims=True) + ddec * dec
            dla_ref[sl, :] = _tri_dot(tri_t, db) + extra
            return carry

        lax.fori_loop(0, nc, chunk, 0)

    rev = lambda t: nb - 1 - t
    hw, vw = GLA_H * GLA_DK, GLA_H * GLA_DV
    res = pl.pallas_call(
        body, name="gla_bwd", grid=(nb,),
        in_specs=[pl.BlockSpec((tb, hw), lambda t: (rev(t), P_GQ // hw)),
                  pl.BlockSpec((tb, hw), lambda t: (rev(t), P_GK // hw)),
                  pl.BlockSpec((tb, vw), lambda t: (rev(t), P_GV // vw)),
                  pl.BlockSpec((tb, hw), lambda t: (rev(t), 0)),
                  pl.BlockSpec((GLA_H, nc, GLA_DV, GLA_DK), lambda t: (0, rev(t), 0, 0)),
                  pl.BlockSpec((tb, vw), lambda t: (rev(t), 0)), pl.BlockSpec(memory_space=pl.ANY)] + [HBM] * len(c_ins),
        out_specs=[pl.BlockSpec((tb, hw), lambda t: (rev(t), 0)),
                   pl.BlockSpec((tb, hw), lambda t: (rev(t), 0)),
                   pl.BlockSpec((tb, vw), lambda t: (rev(t), P_GV // vw)),
                   pl.BlockSpec((tb, hw), lambda t: (rev(t), 0))] + [HBM] * len(c_outs),
        out_shape=[jax.ShapeDtypeStruct((s, hw), BF16),
                   jax.ShapeDtypeStruct((s, hw), BF16),
                   jax.ShapeDtypeStruct((s, P_W), BF16),
                   jax.ShapeDtypeStruct((s, hw), F32)] + c_outs,
        scratch_shapes=[pltpu.VMEM((GLA_H, GLA_DV, GLA_DK), F32)] + c_scratch,
        input_output_aliases={6: 2, **c_alias},
        compiler_params=_params(("arbitrary",)),
    )(p, p, p, la, states, do, dp, *c_ins)
    return res[0], res[1], res[2], res[3], _split_units(comm, res[4:])


def _head_masks():
    lane = lax.broadcasted_iota(jnp.int32, (1, 4 * ATT_HD), 1)
    return [(lane >= h * ATT_HD) & (lane < (h + 1) * ATT_HD) for h in range(4)]


def _attn_fwd(qv, kv, pv, g, r, s):
    ln = s // r
    nblk = ln // ATT_BLK
    qcol = lambda pr: pr
    vcol = qcol
    prev = lambda n: jnp.maximum(n - 1, 0)

    def body(q_ref, kp_ref, kc_ref, vp_ref, vc_ref, o_ref, lse_ref):
        has_prev = pl.program_id(1) > 0
        ri = lax.broadcasted_iota(jnp.int32, (ATT_BLK, 2 * ATT_BLK), 0)
        ci = lax.broadcasted_iota(jnp.int32, (ATT_BLK, 2 * ATT_BLK), 1)
        mask = ((ci < ATT_BLK) & (ci >= ri) & has_prev) | ((ci >= ATT_BLK) & (ci - ATT_BLK <= ri))
        q = q_ref[...]
        kk = jnp.concatenate([kp_ref[...], kc_ref[...]], axis=0)
        vv = jnp.concatenate([vp_ref[...], vc_ref[...]], axis=0)
        o = jnp.zeros((ATT_BLK, 256), F32)
        lse = jnp.zeros((ATT_BLK, 256), F32)
        for hm in _head_masks():
            qm = jnp.where(hm, q, jnp.zeros_like(q))
            sc = jnp.where(mask, _dg(qm, kk, 1, 1) * 0.125, NEG)
            mx = jnp.max(sc, axis=1, keepdims=True)
            pr_ = jnp.exp(sc - mx)
            den = jnp.sum(pr_, axis=1, keepdims=True)
            oh = _dg(pr_.astype(BF16), vv, 1, 0) / den
            o = jnp.where(hm, oh, o)
            lse = jnp.where(hm, mx + jnp.log(den), lse)
        o_ref[...] = o.astype(o_ref.dtype)
        lse_ref[...] = lse

    blk = (ATT_BLK, 256)
    o, lse = pl.pallas_call(
        body, name=f"attn_fwd_{g}", grid=(r, nblk),
        in_specs=[pl.BlockSpec(blk, lambda pr, n: (n, qcol(pr))),
                  pl.BlockSpec(blk, lambda pr, n: (prev(n), qcol(pr))),
                  pl.BlockSpec(blk, lambda pr, n: (n, qcol(pr))),
                  pl.BlockSpec(blk, lambda pr, n: (prev(n), vcol(pr))),
                  pl.BlockSpec(blk, lambda pr, n: (n, vcol(pr)))],
        out_specs=[pl.BlockSpec(blk, lambda pr, n: (n, pr)), pl.BlockSpec(blk, lambda pr, n: (n, pr))],
        out_shape=[jax.ShapeDtypeStruct((ln, r * 256), BF16), jax.ShapeDtypeStruct((ln, r * 256), F32)],
        compiler_params=_params(("parallel", "parallel")),
    )(qv, kv, kv, pv, pv)
    return o, lse


def _attn_bwd(qv, kv, pv, dov, ov, lv, g, r, s):
    ln = s // r
    nblk = ln // ATT_BLK
    qcol = lambda pr: pr
    vcol = qcol
    prev = lambda n: jnp.maximum(n - 1, 0)
    nxt = lambda n: jnp.minimum(n + 1, nblk - 1)

    def body(qc_ref, qn_ref, kp_ref, kc_ref, vp_ref, vc_ref, doc_ref, don_ref, oc_ref, on_ref, lc_ref, ln_ref,
             dq_ref, dk_ref, dv_ref):
        n = pl.program_id(1)
        has_prev, has_next = n > 0, n < nblk - 1
        ri = lax.broadcasted_iota(jnp.int32, (ATT_BLK, 2 * ATT_BLK), 0)
        ci = lax.broadcasted_iota(jnp.int32, (ATT_BLK, 2 * ATT_BLK), 1)
        m_q = ((ci < ATT_BLK) & (ci >= ri) & has_prev) | ((ci >= ATT_BLK) & (ci - ATT_BLK <= ri))
        rs = lax.broadcasted_iota(jnp.int32, (ATT_BLK, ATT_BLK), 0)
        cs = lax.broadcasted_iota(jnp.int32, (ATT_BLK, ATT_BLK), 1)
        m_next = (cs >= rs) & has_next
        qc, qn, kc, vc = qc_ref[...], qn_ref[...], kc_ref[...], vc_ref[...]
        kk = jnp.concatenate([kp_ref[...], kc], axis=0)
        vv = jnp.concatenate([vp_ref[...], vc], axis=0)
        qq = jnp.concatenate([qc, qn], axis=0)
        doc, don = doc_ref[...], don_ref[...]
        dd = jnp.concatenate([doc, don], axis=0)
        pc_full = doc.astype(F32) * oc_ref[...].astype(F32)
        pn_full = don.astype(F32) * on_ref[...].astype(F32)
        lc, lnx = lc_ref[...], ln_ref[...]
        dq = jnp.zeros((ATT_BLK, 256), F32)
        dk = jnp.zeros((ATT_BLK, 256), F32)
        dv = jnp.zeros((ATT_BLK, 256), F32)
        zb = jnp.zeros_like(qc)
        for hm in _head_masks():
            qcm, qnm = jnp.where(hm, qc, zb), jnp.where(hm, qn, zb)
            docm, donm = jnp.where(hm, doc, zb), jnp.where(hm, don, zb)
            lse_c = jnp.max(jnp.where(hm, lc, NEG), axis=1, keepdims=True)
            lse_n = jnp.max(jnp.where(hm, lnx, NEG), axis=1, keepdims=True)
            del_c = jnp.sum(jnp.where(hm, pc_full, 0.0), axis=1, keepdims=True)
            del_n = jnp.sum(jnp.where(hm, pn_full, 0.0), axis=1, keepdims=True)
            p1 = jnp.where(m_q, jnp.exp(_dg(qcm, kk, 1, 1) * 0.125 - lse_c), 0.0)
            ds1 = (p1 * (_dg(docm, vv, 1, 1) - del_c) * 0.125).astype(BF16)
            dqh = _dg(ds1, kk, 1, 0)
            p2 = jnp.where(m_next, jnp.exp(_dg(qnm, kc, 1, 1) * 0.125 - lse_n), 0.0)
            ds2 = (p2 * (_dg(donm, vc, 1, 1) - del_n) * 0.125).astype(BF16)
            dkh = _dg(jnp.concatenate([ds1[:, ATT_BLK:], ds2], axis=0), qq, 0, 0)
            dvh = _dg(jnp.concatenate([p1[:, ATT_BLK:], p2], axis=0).astype(BF16), dd, 0, 0)
            dq = jnp.where(hm, dqh, dq)
            dk = jnp.where(hm, dkh, dk)
            dv = jnp.where(hm, dvh, dv)
        dq_ref[...] = dq.astype(dq_ref.dtype)
        dk_ref[...] = dk.astype(dk_ref.dtype)
        dv_ref[...] = dv.astype(dv_ref.dtype)

    blk = (ATT_BLK, 256)
    cur = lambda col: pl.BlockSpec(blk, lambda pr, n: (n, col(pr)))
    prv = lambda col: pl.BlockSpec(blk, lambda pr, n: (prev(n), col(pr)))
    nx = lambda col: pl.BlockSpec(blk, lambda pr, n: (nxt(n), col(pr)))
    own = lambda pr: pr
    outs = pl.pallas_call(
        body, name=f"attn_bwd_{g}", grid=(r, nblk),
        in_specs=[cur(qcol), nx(qcol), prv(qcol), cur(qcol), prv(vcol), cur(vcol),
                  cur(own), nx(own), cur(own), nx(own), cur(own), nx(own)],
        out_specs=[cur(own), cur(own), cur(own)],
        out_shape=[jax.ShapeDtypeStruct((ln, r * 256), BF16)] * 3,
        compiler_params=_params(("parallel", "parallel")),
    )(qv, qv, kv, kv, pv, pv, dov, dov, ov, ov, lv, lv)
    return outs


def _gelu_parts(gv):
    cdf = 0.5 * (1.0 + lax.erf(gv * (2.0 ** -0.5)))
    pdf = jnp.exp(-0.5 * gv * gv) * (1.0 / math.sqrt(2.0 * math.pi))
    return cdf, pdf


def _pick_row(t, k):
    row = lax.broadcasted_iota(jnp.int32, t.shape, 0)
    return jnp.sum(jnp.where(row == k, t, 0.0), axis=0, keepdims=True)


def _shift_rows(u, halo, n):
    row = lax.broadcasted_iota(jnp.int32, u.shape, 0)
    out = pltpu.roll(u, n, 0)
    for k in range(n):
        out = jnp.where(row == k, _pick_row(halo, 16 - n + k), out)
    return out


def _shift_rows_up(u, halo, n):
    rb = u.shape[0]
    row = lax.broadcasted_iota(jnp.int32, u.shape, 0)
    out = pltpu.roll(u, rb - n, 0)
    for k in range(n):
        out = jnp.where(row == rb - n + k, _pick_row(halo, k), out)
    return out


def _conv(u, halo, cw, cb):
    return cb + _pick_row(cw, 0) * _shift_rows(u, halo, 2) + _pick_row(cw, 1) * _shift_rows(u, halo, 1) + _pick_row(cw, 2) * u


def _local_step(x, mod, pos_col, target, sm, w_sh, g0, chip, core):
    s = x.shape[0]
    shift1, scale1, gate1, shift2, scale2, gate2 = [mod[i:i + 1, :] for i in range(6)]
    rb = 512
    chip1 = chip.reshape(1)

    def f_norm1(c, i, xv, nw, sc, sh):
        return ((xv * _rms(xv) * nw) * (1.0 + sc) + sh,)

    (h,) = _rowcall(f_norm1, [_rows(x, rb), _full(sm["n1w"]), _full(scale1), _full(shift1)],
                    [_orow(s, D, BF16, rb)], n_rows=s, rb=rb, name="norm1")
    own = lambda got, i: lax.dynamic_update_slice(got, w_sh[i], (chip, 0, 0))
    invf = jnp.tile(ROPE_THETA ** (-jnp.arange(ATT_HD // 2, dtype=F32) / (ATT_HD // 2)), 4).reshape(1, 128)
    cos_t, sin_t = _rope_tables(pos_col, invf, s)
    _, got0 = _unit_wait(*g0[:4], after=[h, cos_t, sin_t], name="gather_w_in_wait")
    [got0] = _comm_call("gather_w_in_d2d", [_u_gather_d2d(got0, (0,))])
    u_g1 = _u_gather_ici(w_sh, (1, 2, 3, 4, 5))
    g1 = _unit_start(u_g1, "gather_weights_start", after=got0)
    w = dict(win=_win_assemble(own(got0[0], 0), after=g1[3:]))
    p = _mm(h, w["win"], "in_proj", tm=2048, tn=1536)

    def f_gla_pre(c, i, glr, w2, gb):
        z = _dg(glr, w2.astype(BF16), 1, 0) + gb
        return ((jnp.minimum(z, 0.0) - jnp.log(1.0 + jnp.exp(-jnp.abs(z)))) * (1.0 / GLA_TAU),)

    (la,) = _rowcall(f_gla_pre, [_rows(p, rb, 128, P_LR // 128), _full(sm["w2"]), _full(sm["gb"])],
                     [_orow(s, 512, F32, rb)], n_rows=s, rb=rb, name="gla_pre")
    o_gla, states, _ = _gla_fwd(p, la, s)
    _, got = _unit_wait(u_g1, *g1[:3], after=[o_gla], name="gather_weights_wait")
    [got123] = _comm_call("gather_weights_d2d", [_u_gather_d2d(got[:3], (1, 2, 3))])
    got45 = got[3:]
    w.update(wgb=own(got123[0], 1).reshape(1024, D), wab=_cols_join(own(got123[1], 2)), wout=own(got123[2], 3).reshape(D, D))

    def f_gla_post(c, i, ov, gnw, gr):
        on = jnp.concatenate([ov[:, k * 256:(k + 1) * 256] * _rms(ov[:, k * 256:(k + 1) * 256]) * gnw
                              for k in range(GLA_H)], axis=1)
        g = gr.astype(F32)
        return (on * (g * _sigmoid(g)),)

    (og,) = _rowcall(f_gla_post, [_rows(o_gla, rb), _full(sm["gnw"]), _rows(p, rb, 1024, P_GR // 1024)],
                     [_orow(s, 1024, BF16, rb)], n_rows=s, rb=rb, name="gla_post")
    y_gla = _mm(og, w["wgb"], "gla_branch")

    q_d, k_d, v_d = _rope_fwd(p, cos_t, sin_t, s)
    att = [_attn_fwd(q_d[g], k_d[g], v_d[g], g, r, s) for g, r in enumerate(_RS)]
    o_att, lse, o_d1, o_d2, lse_d1, lse_d2 = _attn_combine(att, s)
    y_att = _mm(o_att, w["wab"], "attn_branch")

    def f_merge(c, i, ma, mb, yg, ya):
        return (_sigmoid(ma.astype(F32)) * yg.astype(F32) + _sigmoid(mb.astype(F32)) * ya.astype(F32),)

    (mixed,) = _rowcall(f_merge, [_rows(p, rb, D, P_MA // D), _rows(p, rb, D, P_MB // D), _rows(y_gla, rb), _rows(y_att, rb)],
                        [_orow(s, D, BF16, rb)], n_rows=s, rb=rb, name="merge")
    z1, [got45] = _mm(mixed, w["wout"], "out_proj", comm=[_u_gather_d2d(got45, (4, 5))])
    w.update(wup=own(got45[0], 4), wdown=own(got45[1], 5).reshape(D_FF, D))

    def f_norm2(c, i, xv, z, g1, nw, sc, sh):
        x1 = xv + g1 * z.astype(F32)
        return (x1, (x1 * _rms(x1) * nw) * (1.0 + sc) + sh)

    x1, h2 = _rowcall(f_norm2, [_rows(x, rb), _rows(z1, rb), _full(gate1), _full(sm["n2w"]), _full(scale2), _full(shift2)],
                      [_orow(s, D, F32, rb), _orow(s, D, BF16, rb)], n_rows=s, rb=rb, name="norm2")
    u = _mm(h2, w["wup"], "up_proj", tm=2048, b_shards=True)

    cwid = 2 * W_UP_SH

    def f_ffn(c, i, uv, hl, cw, cb):
        uc = _conv(uv.astype(F32), hl.astype(F32) * (i > 0).astype(F32), cw, cb)
        val, gt = uc[:, :W_UP_SH], uc[:, W_UP_SH:]
        cdf, _ = _gelu_parts(gt)
        return (gt * cdf * val,)

    ccol = lambda c: c
    rw = 256
    (hidden,) = _rowcall(f_ffn, [_rows(u, rw, cwid, ccol), _halo(u, rw, 16, cwid, ccol, True),
                                 _full(sm["cw"], cwid, ccol), _full(sm["cb"], cwid, ccol)],
                         [_orow(s, D_FF, BF16, rw, W_UP_SH, ccol)], n_rows=s, rb=rw, name="conv_geglu", ncol=2)
    z2 = _mm(hidden, w["wdown"], "down_proj", tm=2048, tk=D_FF)

    def f_final(c, i, x1v, z, g2, fw, tgt):
        x2 = x1v + g2 * z.astype(F32)
        r = _rms(x2)
        xh = x2 * r
        e = xh * fw - tgt
        loss = 0.5 * jnp.sum(jnp.mean(e * e, axis=-1, keepdims=True), axis=0, keepdims=True)
        dy = e * (1.0 / D)
        dxh = dy * fw
        dx2 = r * (dxh - xh * jnp.mean(dxh * xh, axis=-1, keepdims=True))
        return (loss, dx2, dx2 * g2, _csum(dy * xh), _csum(dx2 * z.astype(F32)))

    loss, dx2, dz2, d_fnw, d_gate2 = _rowcall(
        f_final, [_rows(x1, rb), _rows(z2, rb), _full(gate2), _full(sm["fnw"]), _rows(target, rb)],
        [_oacc(1, 1), _orow(s, D, F32, rb), _orow(s, D, BF16, rb), _oacc(1, D), _oacc(1, D)],
        n_rows=s, rb=rb, name="final_loss")
    d_hidden = _mm(dz2, w["wdown"], "down_proj_dx", tb=True, tn=1408)
    g_wdown = _mm(hidden, dz2, "down_proj_dw", ta=True, out_dtype=F32, tm=1408, tn=1024, tk=2048)

    def f_ffn_bwd(c, i, uv, hl, dh, cw, cb):
        uf = uv.astype(F32)
        hf = hl.astype(F32) * (i > 0).astype(F32)
        u1, u2 = _shift_rows(uf, hf, 1), _shift_rows(uf, hf, 2)
        uc = cb + _pick_row(cw, 0) * u2 + _pick_row(cw, 1) * u1 + _pick_row(cw, 2) * uf
        val, gt = uc[:, :W_UP_SH], uc[:, W_UP_SH:]
        cdf, pdf = _gelu_parts(gt)
        dhf = dh.astype(F32)
        duc = jnp.concatenate([dhf * (gt * cdf), dhf * val * (cdf + gt * pdf)], axis=1)
        dcw = jnp.concatenate([_csum(duc * u2), _csum(duc * u1), _csum(duc * uf)], axis=0)
        return (duc, _csum(duc), dcw)

    duc, d_cb, d_cw = _rowcall(
        f_ffn_bwd, [_rows(u, rw, cwid, ccol), _halo(u, rw, 16, cwid, ccol, True), _rows(d_hidden, rw, W_UP_SH, ccol),
                    _full(sm["cw"], cwid, ccol), _full(sm["cb"], cwid, ccol)],
        [_orow(s, 2 * D_FF, BF16, rw, cwid, ccol), _oacc(1, 2 * D_FF, cwid, ccol), _oacc(3, 2 * D_FF, cwid, ccol)],
        n_rows=s, rb=rw, name="conv_geglu_bwd", ncol=2)

    def f_conv_t(c, i, dv, hl, cw):
        df = dv.astype(F32)
        hf = hl.astype(F32) * (i < s // rw - 1).astype(F32)
        return (_pick_row(cw, 2) * df + _pick_row(cw, 1) * _shift_rows_up(df, hf, 1) + _pick_row(cw, 0) * _shift_rows_up(df, hf, 2),)

    (du,) = _rowcall(f_conv_t, [_rows(duc, rw, cwid, ccol), _halo(duc, rw, 16, cwid, ccol, False), _full(sm["cw"], cwid, ccol)],
                     [_orow(s, 2 * D_FF, BF16, rw, cwid, ccol)], n_rows=s, rb=rw, name="conv_transpose", ncol=2)
    g_wup = _mm(h2, du, "up_proj_dw", ta=True, out_dtype=F32, tm=1024, tk=2048, o_shards=True)
    gs45 = [g_wup, g_wdown.reshape(4, W_DOWN_SH, 1024)]
    d_h2, [land45] = _mm(du, w["wup"], "up_proj_dx", tb=True, tm=2048, b_shards=True, comm=[_u_pair_send(gs45, (4, 5))])
    ts45 = [_pair_add(g, ld, core, "grad_pair_add_" + BIG[i]) for g, ld, i in zip(gs45, land45, (4, 5))]
    u_ex4 = _u_chip_exchange(ts45[:1])
    ex4 = _unit_start(u_ex4, "grad_exchange_w_up_start")

    def f_norm2_bwd(c, i, x1v, dh, dxr, z, nw, sc, g1):
        dxn, dsh, dsc, dnw = _norm_bwd(x1v, dh.astype(F32), nw, sc)
        dx1 = dxr + dxn
        return (dx1, dx1 * g1, dsh, dsc, dnw, _csum(dx1 * z.astype(F32)))

    dx1, dz1, d_shift2, d_scale2, d_n2w, d_gate1 = _rowcall(
        f_norm2_bwd, [_rows(x1, rb), _rows(d_h2, rb), _rows(dx2, rb), _rows(z1, rb), _full(sm["n2w"]), _full(scale2), _full(gate1)],
        [_orow(s, D, F32, rb), _orow(s, D, BF16, rb), _oacc(1, D), _oacc(1, D), _oacc(1, D), _oacc(1, D)],
        n_rows=s, rb=rb, name="norm2_bwd", after=ex4[3:])
    d_mixed = _mm(dz1, w["wout"], "out_proj_dx", tb=True)
    g_wout = _mm(mixed, dz1, "out_proj_dw", ta=True, out_dtype=F32, tk=2048)

    def f_merge_bwd(c, i, dm, ma, mb, yg, ya):
        dmf, ygf, yaf = dm.astype(F32), yg.astype(F32), ya.astype(F32)
        sa, sb = _sigmoid(ma.astype(F32)), _sigmoid(mb.astype(F32))
        return (dmf * sa, dmf * sb, jnp.concatenate([dmf * ygf * sa * (1.0 - sa), dmf * yaf * sb * (1.0 - sb)], axis=1))

    dy_gla, dy_att, dp = _rowcall(
        f_merge_bwd, [_rows(d_mixed, rb), _rows(p, rb, D, P_MA // D), _rows(p, rb, D, P_MB // D), _rows(y_gla, rb), _rows(y_att, rb)],
        [_orow(s, D, BF16, rb)] * 2 + [_orow(s, P_W, BF16, rb, 2 * D, lambda c: P_MA // (2 * D))], n_rows=s, rb=rb, name="merge_bwd")
    d_og = _mm(dy_gla, w["wgb"], "gla_branch_dx", tb=True)
    g_wgb = _mm(og, dy_gla, "gla_branch_dw", ta=True, out_dtype=F32, tk=2048)
    d_oatt = _mm(dy_att, w["wab"], "attn_branch_dx", tb=True)
    g_wab = _mm(o_att, dy_att, "attn_branch_dw", ta=True, out_dtype=F32, tk=2048)

    def f_gla_post_bwd(c, i, ov, gnw, gr, dog):
        g = gr.astype(F32)
        sg = _sigmoid(g)
        silu = g * sg
        dof = dog.astype(F32)
        don = dof * silu
        on_parts, do_parts, dgn = [], [], jnp.zeros((1, 256), F32)
        for k in range(GLA_H):
            oh = ov[:, k * 256:(k + 1) * 256]
            dh = don[:, k * 256:(k + 1) * 256]
            r = _rms(oh)
            xh = oh * r
            dgn = dgn + _csum(dh * xh)
            dxh = dh * gnw
            do_parts.append(r * (dxh - xh * jnp.mean(dxh * xh, axis=-1, keepdims=True)))
            on_parts.append(xh * gnw)
        on = jnp.concatenate(on_parts, axis=1)
        dgr = dof * on * (sg * (1.0 + g * (1.0 - sg)))
        return (jnp.concatenate(do_parts, axis=1), dgr, dgn)

    do_gla, dp, d_gnw = _rowcall(
        f_gla_post_bwd, [_rows(o_gla, rb), _full(sm["gnw"]), _rows(p, rb, 1024, P_GR // 1024), _rows(d_og, rb)],
        [_orow(s, 1024, F32, rb), _orow(s, P_W, BF16, rb, 1024, lambda c: P_GR // 1024), _oacc(1, 256)],
        n_rows=s, rb=rb, name="gla_post_bwd", into=(dp, 1))
    gs123 = [g_wgb.reshape(4, 256, 1024), _cols_split(g_wab), g_wout.reshape(4, 256, 1024)]
    d_gq, d_gk, dp, d_la, [land123] = _gla_bwd(p, la, states, do_gla, s, dp, comm=[_u_pair_send(gs123, (1, 2, 3))])
    ts123 = [_pair_add(g, ld, core, "grad_pair_add_" + BIG[i]) for g, ld, i in zip(gs123, land123, (1, 2, 3))]

    def f_gla_pre_bwd(c, i, lav, dlav, glr, w2):
        dz = dlav * (1.0 / GLA_TAU) * (1.0 - jnp.exp(GLA_TAU * lav))
        dzb = dz.astype(BF16)
        return (_dg(dzb, w2.astype(BF16), 1, 1), _csum(dz), _dg(glr, dzb, 0, 0))

    d_glr, d_gb, d_w2 = _rowcall(
        f_gla_pre_bwd, [_rows(la, rb), _rows(d_la, rb), _rows(p, rb, 128, P_LR // 128), _full(sm["w2"])],
        [_orow(s, 128, BF16, rb), _oacc(1, 512), _oacc(128, 512)], n_rows=s, rb=rb, name="gla_pre_bwd")

    do_d = [d_oatt] + list(_dilate(d_oatt, s))
    datt = [_attn_bwd(q_d[g], k_d[g], v_d[g], do_d[g], (o_att, o_d1, o_d2)[g], (lse, lse_d1, lse_d2)[g], g, r, s)
            for g, r in enumerate(_RS)]
    dp = _rope_bwd(datt, d_glr, dp, cos_t, sin_t, s)
    dp = lax.dynamic_update_slice(dp, jnp.concatenate([d_gq, d_gk], axis=1), (0, P_GQ))
    [t4], [r4] = _unit_wait(u_ex4, *ex4[:3], after=[dp], name="grad_exchange_w_up_wait")
    half4 = [_chip_sum(t4, r4, chip1, "grad_chip_sum_w_up")]
    g_win, [r1235, oth4] = _mm(h, dp, "in_proj_dw", ta=True, out_dtype=F32, tm=1024, tn=1536, tk=2048,
                               comm=[_u_chip_exchange(ts123 + ts45[1:]), _u_pair_join(half4)])
    half1235 = [_chip_sum(t, r, chip1, "grad_chip_sum_" + BIG[i]) for t, r, i in zip(ts123 + ts45[1:], r1235, (1, 2, 3, 5))]
    gs0 = [_win_split(g_win)]
    d_h, [land0, oth1235] = _mm(dp, w["win"], "in_proj_dx", tb=True, tk=3840,
                                comm=[_u_pair_send(gs0, (0,)), _u_pair_join(half1235)])
    half123, half45 = half1235[:3], half4 + half1235[3:]
    oth123, oth45 = oth1235[:3], oth4 + oth1235[3:]
    ts0 = _pair_add(gs0[0], land0[0], core, "grad_pair_add_w_in")

    def f_norm1_bwd(c, i, xv, dh, dxr, nw, sc):
        dxn, dsh, dsc, dnw = _norm_bwd(xv, dh.astype(F32), nw, sc)
        return (dxr + dxn, dsh, dsc, dnw)

    grad_x, d_shift1, d_scale1, d_n1w = _rowcall(
        f_norm1_bwd, [_rows(x, rb), _rows(d_h, rb), _rows(dx1, rb), _full(sm["n1w"]), _full(scale1)],
        [_orow(s, D, F32, rb), _oacc(1, D), _oacc(1, D), _oacc(1, D)], n_rows=s, rb=rb, name="norm1_bwd")

    dmod = jnp.concatenate([d_shift1, d_scale1, d_gate1, d_shift2, d_scale2, d_gate2], axis=1)
    small = dict(dmod=dmod, n1w=d_n1w, gb=d_gb, gnw=d_gnw, n2w=d_n2w, cb=d_cb, fnw=d_fnw, w2=d_w2, cw=d_cw)
    return loss, grad_x, half123 + half45, oth123 + oth45, small, ts0


def _win_pieces():
    runs = [(P_GV, 1024, 2048), (P_MA, 5392, 2048), (P_GQ, 0, 1024), (P_AQ, 3088, 2304), (P_LR, 3072, GLA_LR)]
    out = []
    for kc, rc, ln in runs:
        while ln > 0:
            step = min(ln, W_IN_SH - rc % W_IN_SH)
            out.append((kc, rc, step))
            kc, rc, ln = kc + step, rc + step, ln - step
    return out


def _win_assemble(shards, after=()):
    rb = 256

    def body(s_ref, *rest):
        o_ref = rest[-1]
        o_ref[:, W_IN:] = jnp.zeros((rb, P_W - W_IN), o_ref.dtype)
        for kc, rc, ln in _win_pieces():
            o_ref[:, kc:kc + ln] = s_ref[rc // W_IN_SH, :, rc % W_IN_SH:rc % W_IN_SH + ln]

    return pl.pallas_call(
        body, name="w_in_assemble", grid=(D // rb,),
        in_specs=[pl.BlockSpec((4, rb, W_IN_SH), lambda i: (0, i, 0))] + [pl.BlockSpec(memory_space=pl.ANY)] * len(after),
        out_specs=pl.BlockSpec((rb, P_W), lambda i: (i, 0)),
        out_shape=jax.ShapeDtypeStruct((D, P_W), shards.dtype), compiler_params=_params(("parallel",)),
    )(shards, *after)


def _win_split(g):
    rb = 256

    def body(g_ref, o_ref):
        for kc, rc, ln in _win_pieces():
            o_ref[rc // W_IN_SH, :, rc % W_IN_SH:rc % W_IN_SH + ln] = g_ref[:, kc:kc + ln]

    return pl.pallas_call(
        body, name="w_in_grad_split", grid=(D // rb,),
        in_specs=[pl.BlockSpec((rb, P_W), lambda i: (i, 0))], out_specs=pl.BlockSpec((4, rb, W_IN_SH), lambda i: (0, i, 0)),
        out_shape=jax.ShapeDtypeStruct((4, D, W_IN_SH), g.dtype), compiler_params=_params(("parallel",)),
    )(g)


def _ff_to_kernel(a):
    h = W_UP_SH
    return jnp.concatenate([a[:, 0:h], a[:, D_FF:D_FF + h], a[:, h:D_FF], a[:, D_FF + h:]], axis=1)


def _ff_from_kernel(a):
    h = W_UP_SH
    return jnp.concatenate([a[:, 0:h], a[:, 2 * h:3 * h], a[:, h:2 * h], a[:, 3 * h:]], axis=1)


BIG = ("w_in", "w_gla_branch", "w_attn_branch", "w_out", "w_up", "w_down")
SH_SHAPES = ((1024, W_IN_SH), (256, 1024), (256, 256), (256, 1024), (1024, W_UP_SH), (W_DOWN_SH, 1024))
N_BIG = len(BIG)


def _cols_join(t):
    return jnp.concatenate([t[k] for k in range(4)], axis=1)


def _cols_split(t):
    cols = t.shape[1] // 4
    return jnp.stack([t[:, k * cols:(k + 1) * cols] for k in range(4)])


def _me():
    return lax.axis_index("x"), lax.axis_index("y"), lax.axis_index("c")


HBM = pl.BlockSpec(memory_space=pltpu.HBM)
VMEM_SPEC = pl.BlockSpec(memory_space=pltpu.VMEM)


def _allgather8(xs, name):
    rows = xs.shape[0]

    def body(x_ref, out_ref, send_sems, recv_sems, local_sem):
        x, y, c = _me()
        me = 4 * x + 2 * y + c
        mine = pltpu.make_async_copy(x_ref, out_ref.at[me], local_sem)
        mine.start()
        flips = [(k >> 2 & 1, k >> 1 & 1, k & 1) for k in range(1, 8)]

        def peer(f):
            return (jnp.where(f[0] == 1, 1 - x, x), jnp.where(f[1] == 1, 1 - y, y), jnp.where(f[2] == 1, 1 - c, c))

        sends = []
        for k, f in enumerate(flips):
            cp = pltpu.make_async_remote_copy(src_ref=x_ref, dst_ref=out_ref.at[me], send_sem=send_sems.at[k],
                                              recv_sem=recv_sems.at[k], device_id=peer(f), device_id_type=MESH)
            cp.start()
            sends.append(cp)
        for k, f in enumerate(flips):
            px, py, pc = peer(f)
            pltpu.make_async_remote_copy(src_ref=x_ref, dst_ref=out_ref.at[4 * px + 2 * py + pc], send_sem=send_sems.at[k],
                                         recv_sem=recv_sems.at[k], device_id=peer(f), device_id_type=MESH).wait_recv()
        for cp in sends:
            cp.wait_send()
        mine.wait()

    return pl.pallas_call(
        body, name=name, out_shape=jax.ShapeDtypeStruct((8, rows, 128), F32),
        in_specs=[VMEM_SPEC], out_specs=VMEM_SPEC,
        scratch_shapes=[pltpu.SemaphoreType.DMA((7,)), pltpu.SemaphoreType.DMA((7,)), pltpu.SemaphoreType.DMA],
        compiler_params=pltpu.CompilerParams(vmem_limit_bytes=VMEM_LIMIT),
    )(xs)


def _half_rows(i, cc, unit):
    rows = SH_SHAPES[i][0] // 2
    return pl.ds(pl.multiple_of(cc * rows, unit), rows)


def _rc(src, dst, sems, to):
    return pltpu.make_async_remote_copy(src_ref=src, dst_ref=dst, send_sem=sems[0], recv_sem=sems[1], device_id=to, device_id_type=MESH)


def _other_chips(x, y):
    return [(1 - x, y), (x, 1 - y), (1 - x, 1 - y)]


def _u_gather_ici(w_sh, idxs):
    def copies(ins, outs, sem):
        x, y, c = _me()
        res = []
        for j, (px, py) in enumerate(_other_chips(x, y)):
            for n, i in enumerate(idxs):
                src = ins[n].at[0, _half_rows(i, c, 16)]
                res.append((_rc(src, outs[n].at[2 * x + y, _half_rows(i, c, 16)], sem(j * len(idxs) + n), (px, py, c)),
                            _rc(src, outs[n].at[2 * px + py, _half_rows(i, c, 16)], sem(j * len(idxs) + n), (px, py, c))))
        return res

    return dict(ins=[w_sh[i] for i in idxs], outs=[jax.ShapeDtypeStruct((4,) + SH_SHAPES[i], BF16) for i in idxs],
                nsem=3 * len(idxs), alias={}, copies=copies)


def _u_gather_d2d(got, idxs):
    def copies(ins, outs, sem):
        x, y, c = _me()
        res = []
        for j, (px, py) in enumerate(_other_chips(x, y)):
            for n, i in enumerate(idxs):
                src = ins[n].at[2 * px + py, _half_rows(i, c, 16)]
                res.append((_rc(src, outs[n].at[2 * px + py, _half_rows(i, c, 16)], sem(j * len(idxs) + n), (x, y, 1 - c)),
                            _rc(src, outs[n].at[2 * px + py, _half_rows(i, 1 - c, 16)], sem(j * len(idxs) + n), (x, y, 1 - c))))
        return res

    return dict(ins=list(got), outs=[jax.ShapeDtypeStruct(g.shape, g.dtype) for g in got], nsem=3 * len(idxs),
                alias={n: n for n in range(len(idxs))}, copies=copies)


def _u_pair_send(gs, idxs):
    def copies(ins, outs, sem):
        x, y, c = _me()
        res = []
        for n, i in enumerate(idxs):
            for sh in range(4):
                cp = _rc(ins[n].at[sh, _half_rows(i, 1 - c, 8)], outs[n].at[sh], sem(4 * n + sh), (x, y, 1 - c))
                res.append((cp, cp))
        return res

    return dict(ins=list(gs), outs=[jax.ShapeDtypeStruct((4, SH_SHAPES[i][0] // 2, SH_SHAPES[i][1]), F32) for i in idxs],
                nsem=4 * len(idxs), alias={}, copies=copies)


def _u_chip_exchange(ts):
    def copies(ins, outs, sem):
        x, y, c = _me()
        res = []
        for j, (px, py) in enumerate(_other_chips(x, y)):
            for n in range(len(ts)):
                cp = _rc(ins[n].at[2 * px + py], outs[n].at[j], sem(j * len(ts) + n), (px, py, c))
                res.append((cp, cp))
        return res

    return dict(ins=list(ts), outs=[jax.ShapeDtypeStruct((3,) + t.shape[1:], t.dtype) for t in ts], nsem=3 * len(ts),
                alias={}, copies=copies)


def _u_pair_join(hs):
    def copies(ins, outs, sem):
        x, y, c = _me()
        res = []
        for n in range(len(hs)):
            cp = _rc(ins[n], outs[n], sem(n), (x, y, 1 - c))
            res.append((cp, cp))
        return res

    return dict(ins=list(hs), outs=[jax.ShapeDtypeStruct(h.shape, h.dtype) for h in hs], nsem=len(hs), alias={}, copies=copies)


def _comm_phase(units, ci, co, send_sems, recv_sems, start):
    ii = oo = off = 0
    for u in units:
        ni, no = len(u["ins"]), len(u["outs"])
        for st, arrival in u["copies"](ci[ii:ii + ni], co[oo:oo + no], lambda k, off=off: (send_sems.at[off + k], recv_sems.at[off + k])):
            if start:
                st.start()
            else:
                st.wait_send()
                arrival.wait_recv()
        ii, oo, off = ii + ni, oo + no, off + u["nsem"]


def _carry(units, n_in, n_out):
    ins = [a for u in units for a in u["ins"]]
    outs = [o for u in units for o in u["outs"]]
    alias, ii, oo = {}, 0, 0
    for u in units:
        for a, b in u["alias"].items():
            alias[n_in + ii + a] = n_out + oo + b
        ii, oo = ii + len(u["ins"]), oo + len(u["outs"])
    nsem = sum(u["nsem"] for u in units)
    scratch = [pltpu.SemaphoreType.DMA((nsem,)), pltpu.SemaphoreType.DMA((nsem,))] if units else []
    return ins, outs, alias, scratch


def _split_units(units, res):
    out, oo = [], 0
    for u in units:
        out.append(list(res[oo:oo + len(u["outs"])]))
        oo += len(u["outs"])
    return out


def _comm_call(name, units):
    ins, outs, alias, scratch = _carry(units, 0, 0)

    def body(*refs):
        ci, co = refs[:len(ins)], refs[len(ins):len(ins) + len(outs)]
        _comm_phase(units, ci, co, refs[-2], refs[-1], True)
        _comm_phase(units, ci, co, refs[-2], refs[-1], False)

    res = pl.pallas_call(body, name=name, out_shape=outs, in_specs=[HBM] * len(ins), out_specs=[HBM] * len(outs),
                         scratch_shapes=scratch, input_output_aliases=alias)(*ins)
    return _split_units(units, res)


SEM = pl.BlockSpec(memory_space=pltpu.SEMAPHORE)
EFFECT = pltpu.SideEffectType.DATAFLOW_SIDE_EFFECTING


def _unit_start(unit, name, after=()):
    bufs = list(unit["ins"]) + [lax.empty(o.shape, o.dtype) for o in unit["outs"]]
    n_i, n_b, ns = len(unit["ins"]), len(bufs), unit["nsem"]

    def body(*refs):
        send_sems, recv_sems = refs[n_b + len(after)], refs[n_b + len(after) + 1]
        for st, _ in unit["copies"](refs[:n_i], refs[n_i:n_b], lambda k: (send_sems.at[k], recv_sems.at[k])):
            st.start()
        refs[-1][...] = jnp.zeros_like(refs[-1])

    res = pl.pallas_call(
        body, name=name,
        out_shape=[pltpu.SemaphoreType.DMA((ns,)), pltpu.SemaphoreType.DMA((ns,))] + [pltpu.HBM(b.shape, b.dtype) for b in bufs]
        + [jax.ShapeDtypeStruct((8, 128), F32)],
        in_specs=[HBM] * n_b + [pl.BlockSpec(memory_space=pl.ANY)] * len(after), out_specs=[SEM, SEM] + [HBM] * n_b + [VMEM_SPEC],
        input_output_aliases={i: 2 + i for i in range(n_b)},
        compiler_params=pltpu.CompilerParams(has_side_effects=EFFECT),
    )(*[pltpu.with_memory_space_constraint(b, pltpu.HBM) for b in bufs], *after)
    return res[0], res[1], list(res[2:2 + n_b]), res[-1]


def _unit_wait(unit, send_sems, recv_sems, bufs, after, name):
    n_i, n_b = len(unit["ins"]), len(bufs)

    def body(*refs):
        ss, rs = refs[n_b], refs[n_b + 1]
        for st, arrival in unit["copies"](refs[:n_i], refs[n_i:n_b], lambda k: (ss.at[k], rs.at[k])):
            st.wait_send()
            arrival.wait_recv()

    res = pl.pallas_call(
        body, name=name, out_shape=[pltpu.HBM(b.shape, b.dtype) for b in bufs],
        in_specs=[HBM] * n_b + [SEM, SEM] + [pl.BlockSpec(memory_space=pl.ANY)] * len(after), out_specs=[HBM] * n_b,
        input_output_aliases={i: i for i in range(n_b)}, compiler_params=pltpu.CompilerParams(has_side_effects=EFFECT),
    )(*bufs, send_sems, recv_sems, *after)
    return list(res[:n_i]), list(res[n_i:])


def _pair_add(g, land, core, name):
    _, rows, cols = g.shape
    half = rows // 2
    rb = _tile(half, 256, 16)
    nb = half // rb

    def body(c_ref, g_ref, l_ref, o_ref):
        o_ref[...] = (g_ref[...] + l_ref[...]).astype(BF16)

    return pl.pallas_call(
        body, name=name,
        grid_spec=pltpu.PrefetchScalarGridSpec(
            num_scalar_prefetch=1, grid=(4, nb),
            in_specs=[pl.BlockSpec((1, rb, cols), lambda s, i, c_ref: (s, c_ref[0] * nb + i, 0)),
                      pl.BlockSpec((1, rb, cols), lambda s, i, c_ref: (s, i, 0))],
            out_specs=pl.BlockSpec((1, rb, cols), lambda s, i, c_ref: (s, i, 0))),
        out_shape=jax.ShapeDtypeStruct((4, half, cols), BF16),
        compiler_params=_params(("parallel", "parallel")),
    )(core, g, land)


def _chip_sum(t, r, chip, name):
    _, half, cols = t.shape
    rb = _tile(half, 256, 16)

    def body(s_ref, t_ref, r_ref, o_ref):
        o_ref[...] = ((t_ref[0].astype(F32) + r_ref[0].astype(F32)) + r_ref[1].astype(F32)) + r_ref[2].astype(F32)

    return pl.pallas_call(
        body, name=name,
        grid_spec=pltpu.PrefetchScalarGridSpec(
            num_scalar_prefetch=1, grid=(half // rb,),
            in_specs=[pl.BlockSpec((1, rb, cols), lambda i, s_ref: (s_ref[0], i, 0)),
                      pl.BlockSpec((3, rb, cols), lambda i, s_ref: (0, i, 0))],
            out_specs=pl.BlockSpec((rb, cols), lambda i, s_ref: (i, 0))),
        out_shape=jax.ShapeDtypeStruct((half, cols), F32),
        compiler_params=_params(("parallel",)),
    )(chip, t, r)


def _adam_math(wv, gv, mv, vv):
    mn = ADAM_B1 * mv + (1.0 - ADAM_B1) * gv
    vn = ADAM_B2 * vv + (1.0 - ADAM_B2) * (gv * gv)
    m_hat = mn / (1.0 - ADAM_B1 ** ADAM_STEP)
    v_hat = vn / (1.0 - ADAM_B2 ** ADAM_STEP)
    return -ADAM_LR * (m_hat / (jnp.sqrt(v_hat) + ADAM_EPS) + ADAM_WD * wv), mn, vn


def _adamw_halves(wt, mt, vt, mine, theirs, core, name):
    _, rows, cols = wt.shape
    half = rows // 2
    rb = _tile(half, 256, 8)
    nb = half // rb

    def body(c_ref, w_ref, m_ref, v_ref, a_ref, b_ref, g_ref, d_ref, mo_ref, vo_ref):
        gv = jnp.where(pl.program_id(0) == c_ref[0], a_ref[...], b_ref[...])
        dl, mn, vn = _adam_math(w_ref[...], gv, m_ref[...], v_ref[...])
        g_ref[...] = gv
        d_ref[...] = dl
        mo_ref[...] = mn
        vo_ref[...] = vn

    full = pl.BlockSpec((None, rb, cols), lambda hf, i, c_ref: (0, hf * nb + i, 0))
    part = pl.BlockSpec((rb, cols), lambda hf, i, c_ref: (i, 0))
    return pl.pallas_call(
        body, name=name,
        grid_spec=pltpu.PrefetchScalarGridSpec(num_scalar_prefetch=1, grid=(2, nb), in_specs=[full, full, full, part, part],
                                               out_specs=[full] * 4),
        out_shape=[jax.ShapeDtypeStruct((1, rows, cols), F32)] * 4,
        compiler_params=_params(("parallel", "parallel")),
    )(core, wt, mt, vt, mine, theirs)


SG_REP = 144
SG_LOSS = 136
SG_W2, SG_CW = SG_REP, SG_REP + 4 * 16
SG_ROWS = SG_CW + 4 * 40
SP_ROWS = SG_REP + 16 + 40


def _mod_shard(c_all, ada_w_sh):
    def body(c_ref, w_ref, o_ref):
        cv = c_ref[...]
        o_ref[...] = _dg((cv * _sigmoid(cv)).astype(BF16), w_ref[...].astype(BF16), 1, 0)

    return pl.pallas_call(body, name="mod_shard", out_shape=jax.ShapeDtypeStruct((8, 1536), F32),
                          in_specs=[VMEM_SPEC, VMEM_SPEC], out_specs=VMEM_SPEC,
                          compiler_params=pltpu.CompilerParams(vmem_limit_bytes=VMEM_LIMIT))(c_all, ada_w_sh)


def _mod_select(mod_all, ada_b4):
    def body(m_ref, b_ref, o_ref):
        x, y, c = _me()
        me = 4 * x + 2 * y + c
        for sh in range(4):
            o_ref[sh] = m_ref[2 * sh, me] + b_ref[sh]

    return pl.pallas_call(body, name="mod_select", out_shape=jax.ShapeDtypeStruct((4, 12, 128), F32),
                          in_specs=[VMEM_SPEC, VMEM_SPEC], out_specs=VMEM_SPEC)(mod_all, ada_b4)


def _small_reduce(sg_all):
    def body(g_ref, o_ref):
        x, y, c = _me()
        s_me = 2 * x + y
        w2_rows = pl.ds(pl.multiple_of(SG_W2 + 16 * s_me, 8), 16)
        cw_rows = pl.ds(pl.multiple_of(SG_CW + 40 * s_me, 8), 40)
        a = g_ref[0, 0:SG_REP, :]
        b = g_ref[0, w2_rows, :]
        d = g_ref[0, cw_rows, :]
        for dev in range(1, 8):
            a = a + g_ref[dev, 0:SG_REP, :]
            b = b + g_ref[dev, w2_rows, :]
            d = d + g_ref[dev, cw_rows, :]
        o_ref[0:SG_REP, :] = a
        o_ref[SG_REP:SG_REP + 16, :] = b
        o_ref[SG_REP + 16:SP_ROWS, :] = d

    return pl.pallas_call(body, name="small_grad_reduce", out_shape=jax.ShapeDtypeStruct((SP_ROWS, 128), F32),
                          in_specs=[VMEM_SPEC], out_specs=VMEM_SPEC)(sg_all)


def _ada_grad(dmod_all, c_bc):
    def body(g_ref, c_ref, o_ref):
        x, y, c = _me()
        s_me = 2 * x + y
        for k in range(12):
            acc = jnp.zeros((D, 128), F32)
            for b in range(8):
                cv = c_ref[b]
                acc = acc + (cv * _sigmoid(cv)) * g_ref[s_me, k, b:b + 1, :]
            o_ref[:, k * 128:(k + 1) * 128] = acc

    return pl.pallas_call(body, name="ada_w_grad", out_shape=jax.ShapeDtypeStruct((D, 1536), F32),
                          in_specs=[VMEM_SPEC, VMEM_SPEC], out_specs=VMEM_SPEC,
                          compiler_params=pltpu.CompilerParams(vmem_limit_bytes=VMEM_LIMIT))(dmod_all, c_bc)


def _adamw(wt, g, m, v, name):
    rows, cols = wt.shape
    rb = _tile(rows, 256, 8)

    def fn(c, i, wv, gv, mv, vv):
        return _adam_math(wv, gv, mv, vv)

    return _rowcall(fn, [_rows(t, rb) for t in (wt, g, m, v)], [_orow(rows, cols, F32, rb)] * 3,
                    n_rows=rows, rb=rb, name=name)


def _pad_rows(t, rows):
    flat = t.reshape(-1)
    return jnp.pad(flat, (0, rows * 128 - flat.shape[0])).reshape(rows, 128)


SP_LAYOUT = (("ada_b", 48), ("norm1_w", 8), ("gla_gate_b", 8), ("gla_norm_w", 8), ("norm2_w", 8), ("conv_b", 48),
             ("final_norm_w", 8), (None, 8), ("gla_gate_w2", 16), ("conv_w", 40))


def _pack_small(d):
    return jnp.concatenate([jnp.zeros((rows, 128), F32) if n is None else _pad_rows(d[n].astype(F32), rows)
                            for n, rows in SP_LAYOUT], axis=0)


def _unpack_small(pk, shapes):
    out, off = {}, 0
    for n, rows in SP_LAYOUT:
        if n is not None:
            shp = shapes[n]
            out[n] = pk[off:off + rows].reshape(-1)[:math.prod(shp)].reshape(shp)
        off += rows
    return out


def kernel(x, c, positions, ada_w, ada_b, norm1_w, w_in, gla_gate_w2, gla_gate_b, gla_norm_w, w_gla_branch, w_attn_branch, w_out, norm2_w, w_up, conv_w, conv_b, w_down, final_norm_w, loss_target, m_ada_w, m_ada_b, m_norm1_w, m_w_in, m_gla_gate_w2, m_gla_gate_b, m_gla_norm_w, m_w_gla_branch, m_w_attn_branch, m_w_out, m_norm2_w, m_w_up, m_conv_w, m_conv_b, m_w_down, m_final_norm_w, v_ada_w, v_ada_b, v_norm1_w, v_w_in, v_gla_gate_w2, v_gla_gate_b, v_gla_norm_w, v_w_gla_branch, v_w_attn_branch, v_w_out, v_norm2_w, v_w_up, v_conv_w, v_conv_b, v_w_down, v_final_norm_w):
    s = x.shape[1]
    names = ("ada_w", "ada_b", "norm1_w", "w_in", "gla_gate_w2", "gla_gate_b", "gla_norm_w", "w_gla_branch", "w_attn_branch",
             "w_out", "norm2_w", "w_up", "conv_w", "conv_b", "w_down", "final_norm_w")
    wts = dict(zip(names, (ada_w, ada_b, norm1_w, w_in, gla_gate_w2, gla_gate_b, gla_norm_w, w_gla_branch, w_attn_branch,
                           w_out, norm2_w, w_up, conv_w, conv_b, w_down, final_norm_w)))
    ms = dict(zip(names, (m_ada_w, m_ada_b, m_norm1_w, m_w_in, m_gla_gate_w2, m_gla_gate_b, m_gla_norm_w, m_w_gla_branch,
                          m_w_attn_branch, m_w_out, m_norm2_w, m_w_up, m_conv_w, m_conv_b, m_w_down, m_final_norm_w)))
    vs = dict(zip(names, (v_ada_w, v_ada_b, v_norm1_w, v_w_in, v_gla_gate_w2, v_gla_gate_b, v_gla_norm_w, v_w_gla_branch,
                          v_w_attn_branch, v_w_out, v_norm2_w, v_w_up, v_conv_w, v_conv_b, v_w_down, v_final_norm_w)))

    pk0 = jnp.concatenate([_pad_rows(c, 8), _pad_rows(gla_gate_w2, 16), _pad_rows(conv_w, 40)], axis=0)
    sm_all = _allgather8(pk0, "gather_small")
    c_all = sm_all[:, 0:8, :].reshape(8, D)
    w2_full = sm_all[0::2, 8:24, :].transpose(1, 0, 2).reshape(GLA_LR, 512)
    cw_full = sm_all[0::2, 24:64, :].reshape(4, 40 * 128)[:, :3 * W_UP_SH].reshape(4, 3, W_UP_SH).transpose(1, 0, 2).reshape(3, 2 * D_FF)

    mod_sh = _mod_shard(c_all, ada_w[0])
    mod_all = _allgather8(mod_sh.reshape(96, 128), "gather_mod")

    w_sh = [wts[n].astype(BF16) for n in BIG]
    u_g0 = _u_gather_ici(w_sh, (0,))
    g0 = (u_g0,) + _unit_start(u_g0, "gather_w_in_start", after=[mod_all])
    mod = _mod_select(mod_all.reshape(8, 8, 12, 128) + g0[4][0, 0], ada_b.reshape(4, 12, 128)).reshape(6, D)

    core = lax.axis_index("c").astype(jnp.int32).reshape(1)
    chip = (2 * lax.axis_index("x") + lax.axis_index("y")).astype(jnp.int32)
    sm = dict(n1w=norm1_w, n2w=norm2_w, fnw=final_norm_w.reshape(1, D), gnw=gla_norm_w, gb=gla_gate_b,
              w2=jnp.pad(w2_full, ((0, 128 - GLA_LR), (0, 0))), cw=_ff_to_kernel(cw_full), cb=_ff_to_kernel(conv_b))
    loss, grad_x, halves, others, small, ts0 = _local_step(x[0], mod, positions.reshape(s, 1), loss_target[0], sm, w_sh,
                                                               g0, chip, core)

    dcw = _ff_from_kernel(small["cw"]).reshape(3, 4, W_UP_SH).transpose(1, 0, 2)
    dw2 = small["w2"][:GLA_LR].reshape(GLA_LR, 4, 128).transpose(1, 0, 2)
    sg = jnp.concatenate(
        [_pad_rows(small["dmod"], 48), _pad_rows(small["n1w"], 8), _pad_rows(small["gb"], 8), _pad_rows(small["gnw"], 8),
         _pad_rows(small["n2w"], 8), _pad_rows(_ff_from_kernel(small["cb"]), 48), _pad_rows(small["fnw"], 8), _pad_rows(loss, 8)]
        + [_pad_rows(dw2[k], 16) for k in range(4)] + [_pad_rows(dcw[k], 40) for k in range(4)], axis=0)
    sg_all = _allgather8(sg, "gather_small_grads")
    u_ex = _u_chip_exchange([ts0])
    pending = (u_ex,) + _unit_start(u_ex, "grad_exchange_w_in_start", after=[sg_all])
    sg_all = sg_all + pending[4][0, 0]
    g_small_pk = _small_reduce(sg_all)
    dmod_all = sg_all[:, 0:48, :].reshape(8, 4, 12, 128).transpose(1, 2, 0, 3)
    g_ada_w = _ada_grad(dmod_all, jnp.broadcast_to(c_all[:, :, None], (8, D, 128)))

    shapes = {n: wts[n].shape for n in names}
    g_small = _unpack_small(g_small_pk, shapes)
    grads = {"ada_w": g_ada_w.reshape(1, D, 1536), **g_small}
    deltas, new_m, new_v = {}, {}, {}
    for n, mine, theirs in zip(BIG[1:], halves, others):
        grads[n], deltas[n], new_m[n], new_v[n] = _adamw_halves(wts[n], ms[n], vs[n], mine, theirs, core, "adamw_" + n)
    shp = ada_w.shape
    d_, m_, v_ = _adamw(ada_w[0], g_ada_w, m_ada_w[0], v_ada_w[0], "adamw_ada_w")
    deltas["ada_w"], new_m["ada_w"], new_v["ada_w"] = d_.reshape(shp), m_.reshape(shp), v_.reshape(shp)
    d_, m_, v_ = _adamw(_pack_small(wts), g_small_pk, _pack_small(ms), _pack_small(vs), "adamw_small")
    for dst, pk in ((deltas, d_), (new_m, m_), (new_v, v_)):
        dst.update(_unpack_small(pk, shapes))

    [t0], [r0] = _unit_wait(*pending[:4], after=[d_, deltas["ada_w"], deltas["w_up"], deltas["w_down"]], name="grad_exchange_w_in_wait")
    half0 = _chip_sum(t0, r0, chip.reshape(1), "grad_chip_sum_w_in")
    [[oth0]] = _comm_call("grad_join_w_in", [_u_pair_join([half0])])
    grads["w_in"], deltas["w_in"], new_m["w_in"], new_v["w_in"] = _adamw_halves(w_in, m_w_in, v_w_in, half0, oth0, core, "adamw_w_in")

    return (g_small_pk[SG_LOSS, 0], grad_x.reshape(1, s, D), *[grads[n] for n in names], *[deltas[n] for n in names],
            *[new_m[n] for n in names], *[new_v[n] for n in names])
```

```python
import math

import jax
import jax.numpy as jnp
from jax import lax
from jax.experimental import pallas as pl
from jax.experimental.pallas import tpu as pltpu

F32, BF16 = jnp.float32, jnp.bfloat16
MESH = pl.DeviceIdType.MESH

D = 1024
EPS = 1e-6
GLA_H, GLA_DK, GLA_DV, GLA_LR = 4, 128, 256, 16
GLA_TAU = 16.0
GLA_CHUNK = 64
GLA_BLOCK = 512
ATT_GROUPS = ((128, 1), (512, 4), (2048, 16))
ATT_BLK = 128
ATT_HD = 64
ATT_W = 768
D_FF = 2816
ROPE_THETA = 10000.0
P_W = 7680
P_GV, P_GR, P_MA, P_MB, P_GQ, P_GK, P_AQ, P_AK, P_AV, P_LR = 0, 1024, 2048, 3072, 4096, 4608, 5120, 5888, 6656, 7424
W_IN = 7440
W_IN_SH, W_UP_SH, W_DOWN_SH = 1860, 1408, 704
VMEM_LIMIT = 56 * 1024 * 1024
ADAM_LR, ADAM_B1, ADAM_B2, ADAM_EPS, ADAM_WD, ADAM_STEP = 0.001, 0.9, 0.999, 1e-08, 0.01, 10
NEG = -1e30


def _tile(n, target, unit=128):
    best = None
    for t in range(unit, min(n, target) + 1, unit):
        if n % t == 0:
            best = t
    return best or n


def _params(sem):
    return pltpu.CompilerParams(dimension_semantics=sem, vmem_limit_bytes=VMEM_LIMIT)


def _dg(a, b, ca, cb):
    return lax.dot_general(a, b, (((ca,), (cb,)), ((), ())), preferred_element_type=F32)


def _sigmoid(v):
    return 1.0 / (1.0 + jnp.exp(-v))


def _ff_block(j):
    return (j % 2) * 2 + j // 2


def _mm(a, b, name, *, ta=False, tb=False, out_dtype=BF16, tm=1024, tn=1536, tk=1024, n_outer=True, comm=(),
        b_shards=False, o_shards=False):
    m = a.shape[1] if ta else a.shape[0]
    k = a.shape[0] if ta else a.shape[1]
    if b_shards:
        n = b.shape[1] if tb else 4 * W_UP_SH
        tn, tk = (tn, W_UP_SH) if tb else (W_UP_SH, tk)
    else:
        n = b.shape[0] if tb else b.shape[1]
    if o_shards:
        tn = W_UP_SH
    tm, tn, tk = _tile(m, tm), _tile(n, tn), _tile(k, tk)
    nm, nn, nk = m // tm, n // tn, k // tk
    in_out = out_dtype == F32
    c_ins, c_outs, c_alias, c_scratch = _carry(comm, 2, 1)

    def body(a_ref, b_ref, *rest):
        ci, o_ref, co = rest[:len(c_ins)], rest[len(c_ins)], rest[len(c_ins) + 1:len(c_ins) + 1 + len(c_outs)]
        scr = rest[len(c_ins) + 1 + len(c_outs):]
        kk = pl.program_id(2)
        if comm:
            step = (pl.program_id(0) * (nm if n_outer else nn) + pl.program_id(1)) * nk + kk

            @pl.when(step == 0)
            def _():
                _comm_phase(comm, ci, co, scr[-2], scr[-1], True)

        _mm_step(a_ref, b_ref, o_ref, scr, kk)
        if comm:
            @pl.when(step == nm * nn * nk - 1)
            def _():
                _comm_phase(comm, ci, co, scr[-2], scr[-1], False)

    def _mm_step(a_ref, b_ref, o_ref, scr, kk):
        p = _dg(a_ref[...].astype(BF16), b_ref[...].astype(BF16), 0 if ta else 1, 1 if tb else 0)
        if nk == 1:
            o_ref[...] = p.astype(o_ref.dtype)
        else:
            acc = o_ref if in_out else scr[0]

            @pl.when(kk == 0)
            def _():
                acc[...] = p

            @pl.when(kk > 0)
            def _():
                acc[...] += p

            if not in_out:
                @pl.when(kk == nk - 1)
                def _():
                    o_ref[...] = acc[...].astype(o_ref.dtype)

    if n_outer:
        ij = lambda g0, g1: (g1, g0)
        grid = (nn, nm, nk)
    else:
        ij = lambda g0, g1: (g0, g1)
        grid = (nm, nn, nk)
    a_map = (lambda g0, g1, kk: (kk, ij(g0, g1)[0])) if ta else (lambda g0, g1, kk: (ij(g0, g1)[0], kk))
    if b_shards and tb:
        b_spec = pl.BlockSpec((None, tn, tk), lambda g0, g1, kk: (_ff_block(kk), ij(g0, g1)[1], 0))
    elif b_shards:
        b_spec = pl.BlockSpec((None, tk, tn), lambda g0, g1, kk: (_ff_block(ij(g0, g1)[1]), kk, 0))
    elif tb:
        b_spec = pl.BlockSpec((tn, tk), lambda g0, g1, kk: (ij(g0, g1)[1], kk))
    else:
        b_spec = pl.BlockSpec((tk, tn), lambda g0, g1, kk: (kk, ij(g0, g1)[1]))
    if o_shards:
        o_spec = pl.BlockSpec((None, tm, tn), lambda g0, g1, kk: (_ff_block(ij(g0, g1)[1]), ij(g0, g1)[0], 0))
        o_shape = jax.ShapeDtypeStruct((4, m, W_UP_SH), out_dtype)
    else:
        o_spec = pl.BlockSpec((tm, tn), lambda g0, g1, kk: ij(g0, g1))
        o_shape = jax.ShapeDtypeStruct((m, n), out_dtype)
    res = pl.pallas_call(
        body, name=name, grid=grid,
        in_specs=[pl.BlockSpec((tk, tm) if ta else (tm, tk), a_map), b_spec] + [HBM] * len(c_ins),
        out_specs=[o_spec] + [HBM] * len(c_outs),
        out_shape=[o_shape] + c_outs,
        scratch_shapes=([] if (in_out or nk == 1) else [pltpu.VMEM((tm, tn), F32)]) + c_scratch,
        input_output_aliases=c_alias,
        compiler_params=_params(("arbitrary",) * 3 if comm else ("parallel", "parallel", "arbitrary")),
    )(a, b, *c_ins)
    return (res[0], _split_units(comm, res[1:])) if comm else res[0]


def _rows(arr, rb, w=None, j=0):
    w = arr.shape[1] if w is None else w
    if callable(j):
        return arr, pl.BlockSpec((rb, w), lambda c, i: (i, j(c)))
    return arr, pl.BlockSpec((rb, w), lambda c, i: (i, j))


def _full(arr, w=None, j=0):
    w = arr.shape[1] if w is None else w
    if callable(j):
        return arr, pl.BlockSpec((arr.shape[0], w), lambda c, i: (0, j(c)))
    return arr, pl.BlockSpec((arr.shape[0], w), lambda c, i: (0, j))


def _halo(arr, rb, hb, w, j, before):
    per = rb // hb
    last = arr.shape[0] // hb - 1
    if before:
        rmap = lambda i: jnp.maximum(i * per - 1, 0)
    else:
        rmap = lambda i: jnp.minimum((i + 1) * per, last)
    return arr, pl.BlockSpec((hb, w), lambda c, i: (rmap(i), j(c) if callable(j) else j))


def _rowcall(fn, ins, outs, *, n_rows, rb, name, ncol=1, into=None, after=()):
    n_in = len(ins)
    nr = n_rows // rb
    unread = ([] if into is None else [into[0]]) + list(after)
    n_skip = len(unread)

    def body(*refs):
        c, i = pl.program_id(0), pl.program_id(1)
        res = fn(c, i, *[r[...] for r in refs[:n_in]])
        for val, spec, o_ref in zip(res, outs, refs[n_in + n_skip:]):
            if spec[2] == "row":
                o_ref[...] = val.astype(o_ref.dtype)
            else:
                @pl.when(i == 0)
                def _(o_ref=o_ref, val=val):
                    o_ref[...] = val.astype(o_ref.dtype)

                @pl.when(i > 0)
                def _(o_ref=o_ref, val=val):
                    o_ref[...] += val.astype(o_ref.dtype)

    out_specs = []
    for shape, dt, kind, block, col in outs:
        if kind == "row":
            out_specs.append(pl.BlockSpec(block, lambda c, i, col=col: (i, col(c))))
        else:
            out_specs.append(pl.BlockSpec(block, lambda c, i, col=col: (0, col(c))))
    return pl.pallas_call(
        body, name=name, grid=(ncol, nr),
        in_specs=[s for _, s in ins] + [pl.BlockSpec(memory_space=pl.ANY)] * n_skip, out_specs=out_specs,
        out_shape=[jax.ShapeDtypeStruct(o[0], o[1]) for o in outs],
        input_output_aliases={} if into is None else {n_in: into[1]},
        compiler_params=_params(("parallel", "arbitrary")),
    )(*[a for a, _ in ins], *unread)


def _orow(n_rows, w, dt, rb, bw=None, col=lambda c: 0):
    return ((n_rows, w), dt, "row", (rb, bw or w), col)


def _oacc(r, w, bw=None, col=lambda c: 0):
    return ((r, w), F32, "acc", (r, bw or w), col)


def _csum(v):
    return jnp.sum(v, axis=0, keepdims=True)


def _rms(v):
    return lax.rsqrt(jnp.mean(v * v, axis=-1, keepdims=True) + EPS)


def _norm_bwd(xv, dh, w, scale):
    r = _rms(xv)
    xh = xv * r
    dxh = dh * (w * (1.0 + scale))
    dx = r * (dxh - xh * jnp.mean(dxh * xh, axis=-1, keepdims=True))
    t = dh * xh
    return dx, _csum(dh), _csum(t * w), _csum(t * (1.0 + scale))


def _rope_tables(pos_col, invf, s):
    def fn(c, i, pos, f):
        ang = pos.astype(F32) * f
        lane = lax.broadcasted_iota(jnp.int32, ang.shape, 1)
        sign = jnp.where((lane % ATT_HD) < ATT_HD // 2, -1.0, 1.0)
        return jnp.cos(ang), jnp.sin(ang) * sign

    rb = 512
    return _rowcall(fn, [_rows(pos_col, rb), _full(invf)], [_orow(s, 128, F32, rb), _orow(s, 128, F32, rb)],
                    n_rows=s, rb=rb, name="rope_tables")


def _swap_halves(t):
    n = t.shape[1]
    lane = lax.broadcasted_iota(jnp.int32, t.shape, 1)
    return jnp.where((lane % ATT_HD) < ATT_HD // 2, pltpu.roll(t, n - 32, 1), pltpu.roll(t, 32, 1))


def _rope_apply(t, cos, sin_signed, inverse):
    cw = jnp.concatenate([cos] * (t.shape[1] // 128), axis=1)
    sw = jnp.concatenate([sin_signed] * (t.shape[1] // 128), axis=1)
    if inverse:
        sw = -sw
    return t * cw + _swap_halves(t) * sw


DIL_ROWS = 1024


def _to_dilated(scr, val, out_ref, r):
    if r == 1:
        out_ref[...] = val.astype(out_ref.dtype)
        return
    n = val.shape[0] // r
    for hh in range(2):
        scr[hh] = val[:, hh * 128:(hh + 1) * 128]
        for pr in range(r):
            out_ref[:, pr * 256 + hh * 128:pr * 256 + (hh + 1) * 128] = scr[hh, pl.ds(pr, n, stride=r), :].astype(out_ref.dtype)


def _from_dilated(scr, in_ref, r):
    if r == 1:
        return in_ref[...].astype(F32)
    n = in_ref.shape[0]
    for hh in range(2):
        for pr in range(r):
            scr[hh, pl.ds(pr, n, stride=r), :] = in_ref[:, pr * 256 + hh * 128:pr * 256 + (hh + 1) * 128].astype(F32)
    return jnp.concatenate([scr[0], scr[1]], axis=1)


def _dil_spec(r):
    return pl.BlockSpec((DIL_ROWS // r, r * 256), lambda i: (i, 0))


def _dil_shape(s, r, dt):
    return jax.ShapeDtypeStruct((s // r, r * 256), dt)


_DIL_SCRATCH = [pltpu.VMEM((2, DIL_ROWS, 128), F32)]
_RS = tuple(r for _, r in ATT_GROUPS)


def _rope_fwd(p, cos_t, sin_t, s):
    def body(*refs):
        ins, cs, sn, outs, scr = refs[:9], refs[9][...], refs[10][...], refs[11:20], refs[20]
        for t in range(3):
            for g, r in enumerate(_RS):
                val = ins[3 * t + g][...].astype(F32)
                _to_dilated(scr, _rope_apply(val, cs, sn, False) if t < 2 else val, outs[3 * t + g], r)

    res = pl.pallas_call(
        body, name="rope", grid=(s // DIL_ROWS,),
        in_specs=[pl.BlockSpec((DIL_ROWS, 256), lambda i, c=base // 256 + g: (i, c)) for base in (P_AQ, P_AK, P_AV) for g in range(3)]
        + [pl.BlockSpec((DIL_ROWS, 128), lambda i: (i, 0))] * 2,
        out_specs=[_dil_spec(r) for _ in range(3) for r in _RS],
        out_shape=[_dil_shape(s, r, BF16) for _ in range(3) for r in _RS],
        scratch_shapes=_DIL_SCRATCH, compiler_params=_params(("parallel",)),
    )(*([p] * 9), cos_t, sin_t)
    return res[0:3], res[3:6], res[6:9]


def _attn_combine(att, s):
    def body(o0, o1, o2, l0, l1, l2, o_ref, lse_ref, od1, od2, ld1, ld2, scr):
        ov = [_from_dilated(scr, ref, r) for ref, r in zip((o0, o1, o2), _RS)]
        lv = [_from_dilated(scr, ref, r) for ref, r in zip((l0, l1, l2), _RS)]
        mx = jnp.maximum(jnp.maximum(lv[0], lv[1]), lv[2])
        ev = [jnp.exp(l - mx) for l in lv]
        z = ev[0] + ev[1] + ev[2]
        o = ((ev[0] * ov[0] + ev[1] * ov[1] + ev[2] * ov[2]) / z).astype(BF16)
        lse = mx + jnp.log(z)
        o_ref[...] = o
        lse_ref[...] = lse
        for ref, r in zip((od1, od2), _RS[1:]):
            _to_dilated(scr, o.astype(F32), ref, r)
        for ref, r in zip((ld1, ld2), _RS[1:]):
            _to_dilated(scr, lse, ref, r)

    return pl.pallas_call(
        body, name="attn_combine", grid=(s // DIL_ROWS,),
        in_specs=[_dil_spec(r) for r in _RS] * 2,
        out_specs=[_dil_spec(1)] * 2 + [_dil_spec(r) for r in _RS[1:]] * 2,
        out_shape=[_dil_shape(s, 1, BF16), _dil_shape(s, 1, F32)] + [_dil_shape(s, r, BF16) for r in _RS[1:]]
        + [_dil_shape(s, r, F32) for r in _RS[1:]],
        scratch_shapes=_DIL_SCRATCH, compiler_params=_params(("parallel",)),
    )(*[a[0] for a in att], *[a[1] for a in att])


def _dilate(t, s):
    def body(t_ref, o1, o2, scr):
        val = t_ref[...].astype(F32)
        for ref, r in zip((o1, o2), _RS[1:]):
            _to_dilated(scr, val, ref, r)

    return pl.pallas_call(
        body, name="attn_dilate", grid=(s // DIL_ROWS,), in_specs=[_dil_spec(1)], out_specs=[_dil_spec(r) for r in _RS[1:]],
        out_shape=[_dil_shape(s, r, t.dtype) for r in _RS[1:]], scratch_shapes=_DIL_SCRATCH, compiler_params=_params(("parallel",)),
    )(t)


def _rope_bwd(datt, d_glr, dp, cos_t, sin_t, s):
    tail = P_W - P_AQ

    def body(*refs):
        ins, cs, sn, glr, o_ref, scr = refs[:9], refs[9][...], refs[10][...], refs[11], refs[13], refs[14]
        for t in range(3):
            for g, r in enumerate(_RS):
                val = _from_dilated(scr, ins[3 * t + g], r)
                o_ref[:, t * ATT_W + g * 256:t * ATT_W + (g + 1) * 256] = (_rope_apply(val, cs, sn, True) if t < 2 else val).astype(BF16)
        o_ref[:, 3 * ATT_W:3 * ATT_W + 128] = glr[...]
        o_ref[:, 3 * ATT_W + 128:] = jnp.zeros((DIL_ROWS, tail - 3 * ATT_W - 128), BF16)

    return pl.pallas_call(
        body, name="rope_bwd", grid=(s // DIL_ROWS,),
        in_specs=[_dil_spec(r) for _ in range(3) for r in _RS] + [pl.BlockSpec((DIL_ROWS, 128), lambda i: (i, 0))] * 3
        + [pl.BlockSpec(memory_space=pl.ANY)],
        out_specs=pl.BlockSpec((DIL_ROWS, tail), lambda i: (i, P_AQ // tail)),
        out_shape=jax.ShapeDtypeStruct((s, P_W), BF16), input_output_aliases={12: 0},
        scratch_shapes=_DIL_SCRATCH, compiler_params=_params(("parallel",)),
    )(*[datt[g][t] for t in range(3) for g in range(3)], cos_t, sin_t, d_glr, dp)


def _tri_dot(tri, t):
    tb = tri.astype(BF16)
    hi = t.astype(BF16)
    r1 = t - hi.astype(F32)
    mid = r1.astype(BF16)
    lo = (r1 - mid.astype(F32)).astype(BF16)
    return _dg(tb, hi, 1, 0) + _dg(tb, mid, 1, 0) + _dg(tb, lo, 1, 0)


def _gla_decays(la_c, tri):
    b = _tri_dot(tri, la_c)
    row = lax.broadcasted_iota(jnp.int32, b.shape, 0)
    bmid = jnp.sum(jnp.where(row == GLA_CHUNK // 2 - 1, b, 0.0), axis=0, keepdims=True)
    blast = jnp.sum(jnp.where(row == GLA_CHUNK - 1, b, 0.0), axis=0, keepdims=True)
    return b, bmid, blast


def _gla_fwd(p, la, s, comm=()):
    tb, ch = GLA_BLOCK, GLA_CHUNK
    nb, nc = s // tb, tb // ch
    scale = GLA_DK ** -0.5
    c_ins, c_outs, c_alias, c_scratch = _carry(comm, 4, 2)

    def body(q_ref, k_ref, v_ref, la_ref, *rest):
        ci, (o_ref, st_ref) = rest[:len(c_ins)], rest[len(c_ins):len(c_ins) + 2]
        co, state = rest[len(c_ins) + 2:len(c_ins) + 2 + len(c_outs)], rest[len(c_ins) + 2 + len(c_outs)]
        step = pl.program_id(0)
        if comm:
            @pl.when(step == 0)
            def _():
                _comm_phase(comm, ci, co, rest[-2], rest[-1], True)

        _gla_fwd_step(q_ref, k_ref, v_ref, la_ref, o_ref, st_ref, state)
        if comm:
            @pl.when(step == nb - 1)
            def _():
                _comm_phase(comm, ci, co, rest[-2], rest[-1], False)

    def _gla_fwd_step(q_ref, k_ref, v_ref, la_ref, o_ref, st_ref, state):
        @pl.when(pl.program_id(0) == 0)
        def _():
            state[...] = jnp.zeros_like(state)

        ri = lax.broadcasted_iota(jnp.int32, (ch, ch), 0)
        ci = lax.broadcasted_iota(jnp.int32, (ch, ch), 1)
        causal = ci <= ri
        tri = causal.astype(F32)

        def chunk(c, carry):
            sl = pl.ds(pl.multiple_of(c * ch, ch), ch)
            b, bmid, blast = _gla_decays(la_ref[sl, :], tri)
            q = q_ref[sl, :].astype(F32) * scale
            k = k_ref[sl, :].astype(F32)
            v = v_ref[sl, :]
            qgt = (q * jnp.exp(b)).astype(BF16)
            qgn = (q * jnp.exp(b - bmid)).astype(BF16)
            kgn = (k * jnp.exp(bmid - b)).astype(BF16)
            kd = (k * jnp.exp(blast - b)).astype(BF16)
            dec = jnp.exp(blast)
            sts = [state[h] for h in range(GLA_H)]
            outs, news = [], []
            for h in range(GLA_H):
                hk, hv = slice(h * GLA_DK, (h + 1) * GLA_DK), slice(h * GLA_DV, (h + 1) * GLA_DV)
                a = jnp.where(causal, _dg(qgn[:, hk], kgn[:, hk], 1, 1), 0.0)
                outs.append(_dg(a.astype(BF16), v[:, hv], 1, 0) + _dg(qgt[:, hk], sts[h].astype(BF16), 1, 1))
                news.append(dec[:, hk] * sts[h] + _dg(v[:, hv], kd[:, hk], 0, 0))
            for h in range(GLA_H):
                st_ref[h, c] = sts[h]
                state[h] = news[h]
            o_ref[sl, :] = jnp.concatenate(outs, axis=1)
            return carry

        lax.fori_loop(0, nc, chunk, 0)

    hw = GLA_H * GLA_DK
    res = pl.pallas_call(
        body, name="gla_fwd", grid=(nb,),
        in_specs=[pl.BlockSpec((tb, hw), lambda t: (t, P_GQ // hw)),
                  pl.BlockSpec((tb, hw), lambda t: (t, P_GK // hw)),
                  pl.BlockSpec((tb, GLA_H * GLA_DV), lambda t: (t, P_GV // (GLA_H * GLA_DV))),
                  pl.BlockSpec((tb, hw), lambda t: (t, 0))] + [HBM] * len(c_ins),
        out_specs=[pl.BlockSpec((tb, GLA_H * GLA_DV), lambda t: (t, 0)),
                   pl.BlockSpec((GLA_H, nc, GLA_DV, GLA_DK), lambda t: (0, t, 0, 0))] + [HBM] * len(c_outs),
        out_shape=[jax.ShapeDtypeStruct((s, GLA_H * GLA_DV), F32),
                   jax.ShapeDtypeStruct((GLA_H, s // ch, GLA_DV, GLA_DK), F32)] + c_outs,
        scratch_shapes=[pltpu.VMEM((GLA_H, GLA_DV, GLA_DK), F32)] + c_scratch,
        input_output_aliases=c_alias,
        compiler_params=_params(("arbitrary",)),
    )(p, p, p, la, *c_ins)
    return res[0], res[1], _split_units(comm, res[2:])


def _gla_bwd(p, la, states, do, s, dp, comm=()):
    tb, ch = GLA_BLOCK, GLA_CHUNK
    nb, nc = s // tb, tb // ch
    scale = GLA_DK ** -0.5
    c_ins, c_outs, c_alias, c_scratch = _carry(comm, 7, 4)

    def body(q_ref, k_ref, v_ref, la_ref, st_ref, do_ref, dp_in, *rest):
        ci, outs = rest[:len(c_ins)], rest[len(c_ins):len(c_ins) + 4]
        co, dstate = rest[len(c_ins) + 4:len(c_ins) + 4 + len(c_outs)], rest[len(c_ins) + 4 + len(c_outs)]
        step = pl.program_id(0)
        if comm:
            @pl.when(step == 0)
            def _():
                _comm_phase(comm, ci, co, rest[-2], rest[-1], True)

        _gla_bwd_step(q_ref, k_ref, v_ref, la_ref, st_ref, do_ref, *outs, dstate)
        if comm:
            @pl.when(step == nb - 1)
            def _():
                _comm_phase(comm, ci, co, rest[-2], rest[-1], False)

    def _gla_bwd_step(q_ref, k_ref, v_ref, la_ref, st_ref, do_ref, dq_ref, dk_ref, dv_ref, dla_ref, dstate):
        @pl.when(pl.program_id(0) == 0)
        def _():
            dstate[...] = jnp.zeros_like(dstate)

        ri = lax.broadcasted_iota(jnp.int32, (ch, ch), 0)
        ci = lax.broadcasted_iota(jnp.int32, (ch, ch), 1)
        causal = ci <= ri
        tri = causal.astype(F32)
        tri_t = (ci >= ri).astype(F32)

        def chunk(cc, carry):
            c = nc - 1 - cc
            sl = pl.ds(pl.multiple_of(c * ch, ch), ch)
            b, bmid, blast = _gla_decays(la_ref[sl, :], tri)
            q = q_ref[sl, :].astype(F32) * scale
            k = k_ref[sl, :].astype(F32)
            v = v_ref[sl, :]
            e_b, e_qn, e_kn, e_kd = jnp.exp(b), jnp.exp(b - bmid), jnp.exp(bmid - b), jnp.exp(blast - b)
            dec = jnp.exp(blast)
            qgt, qgn, kgn, kd = q * e_b, q * e_qn, k * e_kn, k * e_kd
            qgt_b, qgn_b, kgn_b, kd_b = qgt.astype(BF16), qgn.astype(BF16), kgn.astype(BF16), kd.astype(BF16)
            do_b = do_ref[sl, :].astype(BF16)
            st0s = [st_ref[h, c] for h in range(GLA_H)]
            dsts = [dstate[h] for h in range(GLA_H)]
            dqgn, dqgt, dkgn, dkd, dvs, ddec, news = [], [], [], [], [], [], []
            for h in range(GLA_H):
                hk, hv = slice(h * GLA_DK, (h + 1) * GLA_DK), slice(h * GLA_DV, (h + 1) * GLA_DV)
                dst_b = dsts[h].astype(BF16)
                a = jnp.where(causal, _dg(qgn_b[:, hk], kgn_b[:, hk], 1, 1), 0.0).astype(BF16)
                da = jnp.where(causal, _dg(do_b[:, hv], v[:, hv], 1, 1), 0.0).astype(BF16)
                dqgn.append(_dg(da, kgn_b[:, hk], 1, 0))
                dqgt.append(_dg(do_b[:, hv], st0s[h].astype(BF16), 1, 0))
                dkgn.append(_dg(da, qgn_b[:, hk], 0, 0))
                dvs.append(_dg(a, do_b[:, hv], 0, 0) + _dg(kd_b[:, hk], dst_b, 1, 1))
                dkd.append(_dg(v[:, hv], dst_b, 1, 0))
                ddec.append(jnp.sum(st0s[h] * dsts[h], axis=0, keepdims=True))
                news.append(dec[:, hk] * dsts[h] + _dg(do_b[:, hv], qgt_b[:, hk], 0, 0))
            for h in range(GLA_H):
                dstate[h] = news[h]
            cat = lambda parts: jnp.concatenate(parts, axis=1)
            dqgn, dqgt, dkgn, dkd, ddec = cat(dqgn), cat(dqgt), cat(dkgn), cat(dkd), cat(ddec)
            dq_ref[sl, :] = (scale * (dqgn * e_qn + dqgt * e_b)).astype(dq_ref.dtype)
            dk_ref[sl, :] = (dkgn * e_kn + dkd * e_kd).astype(dk_ref.dtype)
            dv_ref[sl, :] = cat(dvs).astype(dv_ref.dtype)
            db = dqgn * qgn + dqgt * qgt - dkgn * kgn - dkd * kd
            extra = jnp.sum(dkd * kd, axis=0, keepdims=True) + ddec * dec
            dla_ref[sl, :] = _tri_dot(tri_t, db) + extra
            return carry

        lax.fori_loop(0, nc, chunk, 0)

    rev = lambda t: nb - 1 - t
    hw, vw = GLA_H * GLA_DK, GLA_H * GLA_DV
    res = pl.pallas_call(
        body, name="gla_bwd", grid=(nb,),
        in_specs=[pl.BlockSpec((tb, hw), lambda t: (rev(t), P_GQ // hw)),
                  pl.BlockSpec((tb, hw), lambda t: (rev(t), P_GK // hw)),
                  pl.BlockSpec((tb, vw), lambda t: (rev(t), P_GV // vw)),
                  pl.BlockSpec((tb, hw), lambda t: (rev(t), 0)),
                  pl.BlockSpec((GLA_H, nc, GLA_DV, GLA_DK), lambda t: (0, rev(t), 0, 0)),
                  pl.BlockSpec((tb, vw), lambda t: (rev(t), 0)), pl.BlockSpec(memory_space=pl.ANY)] + [HBM] * len(c_ins),
        out_specs=[pl.BlockSpec((tb, hw), lambda t: (rev(t), 0)),
                   pl.BlockSpec((tb, hw), lambda t: (rev(t), 0)),
                   pl.BlockSpec((tb, vw), lambda t: (rev(t), P_GV // vw)),
                   pl.BlockSpec((tb, hw), lambda t: (rev(t), 0))] + [HBM] * len(c_outs),
        out_shape=[jax.ShapeDtypeStruct((s, hw), BF16),
                   jax.ShapeDtypeStruct((s, hw), BF16),
                   jax.ShapeDtypeStruct((s, P_W), BF16),
                   jax.ShapeDtypeStruct((s, hw), F32)] + c_outs,
        scratch_shapes=[pltpu.VMEM((GLA_H, GLA_DV, GLA_DK), F32)] + c_scratch,
        input_output_aliases={6: 2, **c_alias},
        compiler_params=_params(("arbitrary",)),
    )(p, p, p, la, states, do, dp, *c_ins)
    return res[0], res[1], res[2], res[3], _split_units(comm, res[4:])


def _head_masks():
    lane = lax.broadcasted_iota(jnp.int32, (1, 4 * ATT_HD), 1)
    return [(lane >= h * ATT_HD) & (lane < (h + 1) * ATT_HD) for h in range(4)]


def _attn_fwd(qv, kv, pv, g, r, s):
    ln = s // r
    nblk = ln // ATT_BLK
    qcol = lambda pr: pr
    vcol = qcol
    prev = lambda n: jnp.maximum(n - 1, 0)

    def body(q_ref, kp_ref, kc_ref, vp_ref, vc_ref, o_ref, lse_ref):
        has_prev = pl.program_id(1) > 0
        ri = lax.broadcasted_iota(jnp.int32, (ATT_BLK, ATT_BLK), 0)
        ci = lax.broadcasted_iota(jnp.int32, (ATT_BLK, ATT_BLK), 1)
        m_cur = ci <= ri
        m_prev = (ci >= ri) & has_prev
        q, kp, kc, vp, vc = q_ref[...], kp_ref[...], kc_ref[...], vp_ref[...], vc_ref[...]
        o = jnp.zeros((ATT_BLK, 256), F32)
        lse = jnp.zeros((ATT_BLK, 256), F32)
        for hm in _head_masks():
            qm = jnp.where(hm, q, jnp.zeros_like(q))
            sc = jnp.where(m_cur, _dg(qm, kc, 1, 1) * 0.125, NEG)
            sp = jnp.where(m_prev, _dg(qm, kp, 1, 1) * 0.125, NEG)
            mx = jnp.maximum(jnp.max(sc, axis=1, keepdims=True), jnp.max(sp, axis=1, keepdims=True))
            pc, pp = jnp.exp(sc - mx), jnp.exp(sp - mx)
            den = jnp.sum(pc, axis=1, keepdims=True) + jnp.sum(pp, axis=1, keepdims=True)
            oh = (_dg(pc.astype(BF16), vc, 1, 0) + _dg(pp.astype(BF16), vp, 1, 0)) / den
            o = jnp.where(hm, oh, o)
            lse = jnp.where(hm, mx + jnp.log(den), lse)
        o_ref[...] = o.astype(o_ref.dtype)
        lse_ref[...] = lse

    blk = (ATT_BLK, 256)
    o, lse = pl.pallas_call(
        body, name=f"attn_fwd_{g}", grid=(r, nblk),
        in_specs=[pl.BlockSpec(blk, lambda pr, n: (n, qcol(pr))),
                  pl.BlockSpec(blk, lambda pr, n: (prev(n), qcol(pr))),
                  pl.BlockSpec(blk, lambda pr, n: (n, qcol(pr))),
                  pl.BlockSpec(blk, lambda pr, n: (prev(n), vcol(pr))),
                  pl.BlockSpec(blk, lambda pr, n: (n, vcol(pr)))],
        out_specs=[pl.BlockSpec(blk, lambda pr, n: (n, pr)), pl.BlockSpec(blk, lambda pr, n: (n, pr))],
        out_shape=[jax.ShapeDtypeStruct((ln, r * 256), BF16), jax.ShapeDtypeStruct((ln, r * 256), F32)],
        compiler_params=_params(("parallel", "parallel")),
    )(qv, kv, kv, pv, pv)
    return o, lse


def _attn_bwd(qv, kv, pv, dov, ov, lv, g, r, s):
    ln = s // r
    nblk = ln // ATT_BLK
    qcol = lambda pr: pr
    vcol = qcol
    prev = lambda n: jnp.maximum(n - 1, 0)
    nxt = lambda n: jnp.minimum(n + 1, nblk - 1)

    def body(qc_ref, qn_ref, kp_ref, kc_ref, vp_ref, vc_ref, doc_ref, don_ref, oc_ref, on_ref, lc_ref, ln_ref,
             dq_ref, dk_ref, dv_ref):
        n = pl.program_id(1)
        has_prev, has_next = n > 0, n < nblk - 1
        ri = lax.broadcasted_iota(jnp.int32, (ATT_BLK, ATT_BLK), 0)
        ci = lax.broadcasted_iota(jnp.int32, (ATT_BLK, ATT_BLK), 1)
        m_cur = ci <= ri
        m_prev = (ci >= ri) & has_prev
        m_next = (ci >= ri) & has_next
        qc, qn, kp, kc, vp, vc = qc_ref[...], qn_ref[...], kp_ref[...], kc_ref[...], vp_ref[...], vc_ref[...]
        doc, don = doc_ref[...], don_ref[...]
        pc_full = doc.astype(F32) * oc_ref[...].astype(F32)
        pn_full = don.astype(F32) * on_ref[...].astype(F32)
        lc, lnx = lc_ref[...], ln_ref[...]
        dq = jnp.zeros((ATT_BLK, 256), F32)
        dk = jnp.zeros((ATT_BLK, 256), F32)
        dv = jnp.zeros((ATT_BLK, 256), F32)
        zb = jnp.zeros_like(qc)
        for hm in _head_masks():
            qcm, qnm = jnp.where(hm, qc, zb), jnp.where(hm, qn, zb)
            docm, donm = jnp.where(hm, doc, zb), jnp.where(hm, don, zb)
            lse_c = jnp.max(jnp.where(hm, lc, NEG), axis=1, keepdims=True)
            lse_n = jnp.max(jnp.where(hm, lnx, NEG), axis=1, keepdims=True)
            del_c = jnp.sum(jnp.where(hm, pc_full, 0.0), axis=1, keepdims=True)
            del_n = jnp.sum(jnp.where(hm, pn_full, 0.0), axis=1, keepdims=True)
            pr_ = jnp.where(m_cur, jnp.exp(_dg(qcm, kc, 1, 1) * 0.125 - lse_c), 0.0)
            ds = (pr_ * (_dg(docm, vc, 1, 1) - del_c) * 0.125).astype(BF16)
            dqh = _dg(ds, kc, 1, 0)
            dkh = _dg(ds, qc, 0, 0)
            dvh = _dg(pr_.astype(BF16), doc, 0, 0)
            pr_ = jnp.where(m_prev, jnp.exp(_dg(qcm, kp, 1, 1) * 0.125 - lse_c), 0.0)
            ds = (pr_ * (_dg(docm, vp, 1, 1) - del_c) * 0.125).astype(BF16)
            dqh = dqh + _dg(ds, kp, 1, 0)
            pr_ = jnp.where(m_next, jnp.exp(_dg(qnm, kc, 1, 1) * 0.125 - lse_n), 0.0)
            ds = (pr_ * (_dg(donm, vc, 1, 1) - del_n) * 0.125).astype(BF16)
            dkh = dkh + _dg(ds, qn, 0, 0)
            dvh = dvh + _dg(pr_.astype(BF16), don, 0, 0)
            dq = jnp.where(hm, dqh, dq)
            dk = jnp.where(hm, dkh, dk)
            dv = jnp.where(hm, dvh, dv)
        dq_ref[...] = dq.astype(dq_ref.dtype)
        dk_ref[...] = dk.astype(dk_ref.dtype)
        dv_ref[...] = dv.astype(dv_ref.dtype)

    blk = (ATT_BLK, 256)
    cur = lambda col: pl.BlockSpec(blk, lambda pr, n: (n, col(pr)))
    prv = lambda col: pl.BlockSpec(blk, lambda pr, n: (prev(n), col(pr)))
    nx = lambda col: pl.BlockSpec(blk, lambda pr, n: (nxt(n), col(pr)))
    own = lambda pr: pr
    outs = pl.pallas_call(
        body, name=f"attn_bwd_{g}", grid=(r, nblk),
        in_specs=[cur(qcol), nx(qcol), prv(qcol), cur(qcol), prv(vcol), cur(vcol),
                  cur(own), nx(own), cur(own), nx(own), cur(own), nx(own)],
        out_specs=[cur(own), cur(own), cur(own)],
        out_shape=[jax.ShapeDtypeStruct((ln, r * 256), BF16)] * 3,
        compiler_params=_params(("parallel", "parallel")),
    )(qv, qv, kv, kv, pv, pv, dov, dov, ov, ov, lv, lv)
    return outs


def _gelu_parts(gv):
    cdf = 0.5 * (1.0 + lax.erf(gv * (2.0 ** -0.5)))
    pdf = jnp.exp(-0.5 * gv * gv) * (1.0 / math.sqrt(2.0 * math.pi))
    return cdf, pdf


def _pick_row(t, k):
    row = lax.broadcasted_iota(jnp.int32, t.shape, 0)
    return jnp.sum(jnp.where(row == k, t, 0.0), axis=0, keepdims=True)


def _shift_rows(u, halo, n):
    row = lax.broadcasted_iota(jnp.int32, u.shape, 0)
    out = pltpu.roll(u, n, 0)
    for k in range(n):
        out = jnp.where(row == k, _pick_row(halo, 16 - n + k), out)
    return out


def _shift_rows_up(u, halo, n):
    rb = u.shape[0]
    row = lax.broadcasted_iota(jnp.int32, u.shape, 0)
    out = pltpu.roll(u, rb - n, 0)
    for k in range(n):
        out = jnp.where(row == rb - n + k, _pick_row(halo, k), out)
    return out


def _conv(u, halo, cw, cb):
    return cb + _pick_row(cw, 0) * _shift_rows(u, halo, 2) + _pick_row(cw, 1) * _shift_rows(u, halo, 1) + _pick_row(cw, 2) * u


def _local_step(x, mod, pos_col, target, sm, w_sh, g0, chip, core):
    s = x.shape[0]
    shift1, scale1, gate1, shift2, scale2, gate2 = [mod[i:i + 1, :] for i in range(6)]
    rb = 512
    chip1 = chip.reshape(1)

    def f_norm1(c, i, xv, nw, sc, sh):
        return ((xv * _rms(xv) * nw) * (1.0 + sc) + sh,)

    (h,) = _rowcall(f_norm1, [_rows(x, rb), _full(sm["n1w"]), _full(scale1), _full(shift1)],
                    [_orow(s, D, BF16, rb)], n_rows=s, rb=rb, name="norm1")
    own = lambda got, i: lax.dynamic_update_slice(got, w_sh[i], (chip, 0, 0))
    invf = jnp.tile(ROPE_THETA ** (-jnp.arange(ATT_HD // 2, dtype=F32) / (ATT_HD // 2)), 4).reshape(1, 128)
    cos_t, sin_t = _rope_tables(pos_col, invf, s)
    _, got0 = _unit_wait(*g0[:4], after=[h, cos_t, sin_t], name="gather_w_in_wait")
    [got0] = _comm_call("gather_w_in_d2d", [_u_gather_d2d(got0, (0,))])
    u_g1 = _u_gather_ici(w_sh, (1, 2, 3, 4, 5))
    g1 = _unit_start(u_g1, "gather_weights_start", after=got0)
    w = dict(win=_win_assemble(own(got0[0], 0), after=g1[3:]))
    p = _mm(h, w["win"], "in_proj", tm=2048, tn=1536)

    def f_gla_pre(c, i, glr, w2, gb):
        z = _dg(glr, w2.astype(BF16), 1, 0) + gb
        return ((jnp.minimum(z, 0.0) - jnp.log(1.0 + jnp.exp(-jnp.abs(z)))) * (1.0 / GLA_TAU),)

    (la,) = _rowcall(f_gla_pre, [_rows(p, rb, 128, P_LR // 128), _full(sm["w2"]), _full(sm["gb"])],
                     [_orow(s, 512, F32, rb)], n_rows=s, rb=rb, name="gla_pre")
    o_gla, states, _ = _gla_fwd(p, la, s)
    _, got = _unit_wait(u_g1, *g1[:3], after=[o_gla], name="gather_weights_wait")
    [got123] = _comm_call("gather_weights_d2d", [_u_gather_d2d(got[:3], (1, 2, 3))])
    got45 = got[3:]
    w.update(wgb=own(got123[0], 1).reshape(1024, D), wab=_cols_join(own(got123[1], 2)), wout=own(got123[2], 3).reshape(D, D))

    def f_gla_post(c, i, ov, gnw, gr):
        on = jnp.concatenate([ov[:, k * 256:(k + 1) * 256] * _rms(ov[:, k * 256:(k + 1) * 256]) * gnw
                              for k in range(GLA_H)], axis=1)
        g = gr.astype(F32)
        return (on * (g * _sigmoid(g)),)

    (og,) = _rowcall(f_gla_post, [_rows(o_gla, rb), _full(sm["gnw"]), _rows(p, rb, 1024, P_GR // 1024)],
                     [_orow(s, 1024, BF16, rb)], n_rows=s, rb=rb, name="gla_post")
    y_gla = _mm(og, w["wgb"], "gla_branch")

    q_d, k_d, v_d = _rope_fwd(p, cos_t, sin_t, s)
    att = [_attn_fwd(q_d[g], k_d[g], v_d[g], g, r, s) for g, r in enumerate(_RS)]
    o_att, lse, o_d1, o_d2, lse_d1, lse_d2 = _attn_combine(att, s)
    y_att = _mm(o_att, w["wab"], "attn_branch")

    def f_merge(c, i, ma, mb, yg, ya):
        return (_sigmoid(ma.astype(F32)) * yg.astype(F32) + _sigmoid(mb.astype(F32)) * ya.astype(F32),)

    (mixed,) = _rowcall(f_merge, [_rows(p, rb, D, P_MA // D), _rows(p, rb, D, P_MB // D), _rows(y_gla, rb), _rows(y_att, rb)],
                        [_orow(s, D, BF16, rb)], n_rows=s, rb=rb, name="merge")
    z1, [got45] = _mm(mixed, w["wout"], "out_proj", comm=[_u_gather_d2d(got45, (4, 5))])
    w.update(wup=own(got45[0], 4), wdown=own(got45[1], 5).reshape(D_FF, D))

    def f_norm2(c, i, xv, z, g1, nw, sc, sh):
        x1 = xv + g1 * z.astype(F32)
        return (x1, (x1 * _rms(x1) * nw) * (1.0 + sc) + sh)

    x1, h2 = _rowcall(f_norm2, [_rows(x, rb), _rows(z1, rb), _full(gate1), _full(sm["n2w"]), _full(scale2), _full(shift2)],
                      [_orow(s, D, F32, rb), _orow(s, D, BF16, rb)], n_rows=s, rb=rb, name="norm2")
    u = _mm(h2, w["wup"], "up_proj", tm=2048, b_shards=True)

    cwid = 2 * W_UP_SH

    def f_ffn(c, i, uv, hl, cw, cb):
        uc = _conv(uv.astype(F32), hl.astype(F32) * (i > 0).astype(F32), cw, cb)
        val, gt = uc[:, :W_UP_SH], uc[:, W_UP_SH:]
        cdf, _ = _gelu_parts(gt)
        return (gt * cdf * val,)

    ccol = lambda c: c
    rw = 256
    (hidden,) = _rowcall(f_ffn, [_rows(u, rw, cwid, ccol), _halo(u, rw, 16, cwid, ccol, True),
                                 _full(sm["cw"], cwid, ccol), _full(sm["cb"], cwid, ccol)],
                         [_orow(s, D_FF, BF16, rw, W_UP_SH, ccol)], n_rows=s, rb=rw, name="conv_geglu", ncol=2)
    z2 = _mm(hidden, w["wdown"], "down_proj", tk=D_FF)

    def f_final(c, i, x1v, z, g2, fw, tgt):
        x2 = x1v + g2 * z.astype(F32)
        r = _rms(x2)
        xh = x2 * r
        e = xh * fw - tgt
        loss = 0.5 * jnp.sum(jnp.mean(e * e, axis=-1, keepdims=True), axis=0, keepdims=True)
        dy = e * (1.0 / D)
        dxh = dy * fw
        dx2 = r * (dxh - xh * jnp.mean(dxh * xh, axis=-1, keepdims=True))
        return (loss, dx2, dx2 * g2, _csum(dy * xh), _csum(dx2 * z.astype(F32)))

    loss, dx2, dz2, d_fnw, d_gate2 = _rowcall(
        f_final, [_rows(x1, rb), _rows(z2, rb), _full(gate2), _full(sm["fnw"]), _rows(target, rb)],
        [_oacc(1, 1), _orow(s, D, F32, rb), _orow(s, D, BF16, rb), _oacc(1, D), _oacc(1, D)],
        n_rows=s, rb=rb, name="final_loss")
    d_hidden = _mm(dz2, w["wdown"], "down_proj_dx", tb=True, tn=1408)
    g_wdown = _mm(hidden, dz2, "down_proj_dw", ta=True, out_dtype=F32, tm=1408, tn=1024, tk=2048)

    def f_ffn_bwd(c, i, uv, hl, dh, cw, cb):
        uf = uv.astype(F32)
        hf = hl.astype(F32) * (i > 0).astype(F32)
        u1, u2 = _shift_rows(uf, hf, 1), _shift_rows(uf, hf, 2)
        uc = cb + _pick_row(cw, 0) * u2 + _pick_row(cw, 1) * u1 + _pick_row(cw, 2) * uf
        val, gt = uc[:, :W_UP_SH], uc[:, W_UP_SH:]
        cdf, pdf = _gelu_parts(gt)
        dhf = dh.astype(F32)
        duc = jnp.concatenate([dhf * (gt * cdf), dhf * val * (cdf + gt * pdf)], axis=1)
        dcw = jnp.concatenate([_csum(duc * u2), _csum(duc * u1), _csum(duc * uf)], axis=0)
        return (duc, _csum(duc), dcw)

    duc, d_cb, d_cw = _rowcall(
        f_ffn_bwd, [_rows(u, rw, cwid, ccol), _halo(u, rw, 16, cwid, ccol, True), _rows(d_hidden, rw, W_UP_SH, ccol),
                    _full(sm["cw"], cwid, ccol), _full(sm["cb"], cwid, ccol)],
        [_orow(s, 2 * D_FF, BF16, rw, cwid, ccol), _oacc(1, 2 * D_FF, cwid, ccol), _oacc(3, 2 * D_FF, cwid, ccol)],
        n_rows=s, rb=rw, name="conv_geglu_bwd", ncol=2)

    def f_conv_t(c, i, dv, hl, cw):
        df = dv.astype(F32)
        hf = hl.astype(F32) * (i < s // rw - 1).astype(F32)
        return (_pick_row(cw, 2) * df + _pick_row(cw, 1) * _shift_rows_up(df, hf, 1) + _pick_row(cw, 0) * _shift_rows_up(df, hf, 2),)

    (du,) = _rowcall(f_conv_t, [_rows(duc, rw, cwid, ccol), _halo(duc, rw, 16, cwid, ccol, False), _full(sm["cw"], cwid, ccol)],
                     [_orow(s, 2 * D_FF, BF16, rw, cwid, ccol)], n_rows=s, rb=rw, name="conv_transpose", ncol=2)
    g_wup = _mm(h2, du, "up_proj_dw", ta=True, out_dtype=F32, tm=1024, tk=2048, o_shards=True)
    gs45 = [g_wup, g_wdown.reshape(4, W_DOWN_SH, 1024)]
    d_h2, [land45] = _mm(du, w["wup"], "up_proj_dx", tb=True, tm=2048, b_shards=True, comm=[_u_pair_send(gs45, (4, 5))])
    ts45 = [_pair_add(g, ld, core, "grad_pair_add_" + BIG[i]) for g, ld, i in zip(gs45, land45, (4, 5))]
    u_ex4 = _u_chip_exchange(ts45[:1])
    ex4 = _unit_start(u_ex4, "grad_exchange_w_up_start")

    def f_norm2_bwd(c, i, x1v, dh, dxr, z, nw, sc, g1):
        dxn, dsh, dsc, dnw = _norm_bwd(x1v, dh.astype(F32), nw, sc)
        dx1 = dxr + dxn
        return (dx1, dx1 * g1, dsh, dsc, dnw, _csum(dx1 * z.astype(F32)))

    dx1, dz1, d_shift2, d_scale2, d_n2w, d_gate1 = _rowcall(
        f_norm2_bwd, [_rows(x1, rb), _rows(d_h2, rb), _rows(dx2, rb), _rows(z1, rb), _full(sm["n2w"]), _full(scale2), _full(gate1)],
        [_orow(s, D, F32, rb), _orow(s, D, BF16, rb), _oacc(1, D), _oacc(1, D), _oacc(1, D), _oacc(1, D)],
        n_rows=s, rb=rb, name="norm2_bwd", after=ex4[3:])
    d_mixed = _mm(dz1, w["wout"], "out_proj_dx", tb=True)
    g_wout = _mm(mixed, dz1, "out_proj_dw", ta=True, out_dtype=F32, tk=2048)

    def f_merge_bwd(c, i, dm, ma, mb, yg, ya):
        dmf, ygf, yaf = dm.astype(F32), yg.astype(F32), ya.astype(F32)
        sa, sb = _sigmoid(ma.astype(F32)), _sigmoid(mb.astype(F32))
        return (dmf * sa, dmf * sb, jnp.concatenate([dmf * ygf * sa * (1.0 - sa), dmf * yaf * sb * (1.0 - sb)], axis=1))

    dy_gla, dy_att, dp = _rowcall(
        f_merge_bwd, [_rows(d_mixed, rb), _rows(p, rb, D, P_MA // D), _rows(p, rb, D, P_MB // D), _rows(y_gla, rb), _rows(y_att, rb)],
        [_orow(s, D, BF16, rb)] * 2 + [_orow(s, P_W, BF16, rb, 2 * D, lambda c: P_MA // (2 * D))], n_rows=s, rb=rb, name="merge_bwd")
    d_og = _mm(dy_gla, w["wgb"], "gla_branch_dx", tb=True)
    g_wgb = _mm(og, dy_gla, "gla_branch_dw", ta=True, out_dtype=F32, tk=2048)
    d_oatt = _mm(dy_att, w["wab"], "attn_branch_dx", tb=True)
    g_wab = _mm(o_att, dy_att, "attn_branch_dw", ta=True, out_dtype=F32, tk=2048)

    def f_gla_post_bwd(c, i, ov, gnw, gr, dog):
        g = gr.astype(F32)
        sg = _sigmoid(g)
        silu = g * sg
        dof = dog.astype(F32)
        don = dof * silu
        on_parts, do_parts, dgn = [], [], jnp.zeros((1, 256), F32)
        for k in range(GLA_H):
            oh = ov[:, k * 256:(k + 1) * 256]
            dh = don[:, k * 256:(k + 1) * 256]
            r = _rms(oh)
            xh = oh * r
            dgn = dgn + _csum(dh * xh)
            dxh = dh * gnw
            do_parts.append(r * (dxh - xh * jnp.mean(dxh * xh, axis=-1, keepdims=True)))
            on_parts.append(xh * gnw)
        on = jnp.concatenate(on_parts, axis=1)
        dgr = dof * on * (sg * (1.0 + g * (1.0 - sg)))
        return (jnp.concatenate(do_parts, axis=1), dgr, dgn)

    do_gla, dp, d_gnw = _rowcall(
        f_gla_post_bwd, [_rows(o_gla, rb), _full(sm["gnw"]), _rows(p, rb, 1024, P_GR // 1024), _rows(d_og, rb)],
        [_orow(s, 1024, F32, rb), _orow(s, P_W, BF16, rb, 1024, lambda c: P_GR // 1024), _oacc(1, 256)],
        n_rows=s, rb=rb, name="gla_post_bwd", into=(dp, 1))
    gs123 = [g_wgb.reshape(4, 256, 1024), _cols_split(g_wab), g_wout.reshape(4, 256, 1024)]
    d_gq, d_gk, dp, d_la, [land123] = _gla_bwd(p, la, states, do_gla, s, dp, comm=[_u_pair_send(gs123, (1, 2, 3))])
    ts123 = [_pair_add(g, ld, core, "grad_pair_add_" + BIG[i]) for g, ld, i in zip(gs123, land123, (1, 2, 3))]

    def f_gla_pre_bwd(c, i, lav, dlav, glr, w2):
        dz = dlav * (1.0 / GLA_TAU) * (1.0 - jnp.exp(GLA_TAU * lav))
        dzb = dz.astype(BF16)
        return (_dg(dzb, w2.astype(BF16), 1, 1), _csum(dz), _dg(glr, dzb, 0, 0))

    d_glr, d_gb, d_w2 = _rowcall(
        f_gla_pre_bwd, [_rows(la, rb), _rows(d_la, rb), _rows(p, rb, 128, P_LR // 128), _full(sm["w2"])],
        [_orow(s, 128, BF16, rb), _oacc(1, 512), _oacc(128, 512)], n_rows=s, rb=rb, name="gla_pre_bwd")

    do_d = [d_oatt] + list(_dilate(d_oatt, s))
    datt = [_attn_bwd(q_d[g], k_d[g], v_d[g], do_d[g], (o_att, o_d1, o_d2)[g], (lse, lse_d1, lse_d2)[g], g, r, s)
            for g, r in enumerate(_RS)]
    dp = _rope_bwd(datt, d_glr, dp, cos_t, sin_t, s)
    dp = lax.dynamic_update_slice(dp, jnp.concatenate([d_gq, d_gk], axis=1), (0, P_GQ))
    [t4], [r4] = _unit_wait(u_ex4, *ex4[:3], after=[dp], name="grad_exchange_w_up_wait")
    half4 = [_chip_sum(t4, r4, chip1, "grad_chip_sum_w_up")]
    g_win, [r1235, oth4] = _mm(h, dp, "in_proj_dw", ta=True, out_dtype=F32, tm=1024, tn=1536, tk=2048,
                               comm=[_u_chip_exchange(ts123 + ts45[1:]), _u_pair_join(half4)])
    half1235 = [_chip_sum(t, r, chip1, "grad_chip_sum_" + BIG[i]) for t, r, i in zip(ts123 + ts45[1:], r1235, (1, 2, 3, 5))]
    gs0 = [_win_split(g_win)]
    d_h, [land0, oth1235] = _mm(dp, w["win"], "in_proj_dx", tb=True, tk=3840,
                                comm=[_u_pair_send(gs0, (0,)), _u_pair_join(half1235)])
    half123, half45 = half1235[:3], half4 + half1235[3:]
    oth123, oth45 = oth1235[:3], oth4 + oth1235[3:]
    ts0 = _pair_add(gs0[0], land0[0], core, "grad_pair_add_w_in")

    def f_norm1_bwd(c, i, xv, dh, dxr, nw, sc):
        dxn, dsh, dsc, dnw = _norm_bwd(xv, dh.astype(F32), nw, sc)
        return (dxr + dxn, dsh, dsc, dnw)

    grad_x, d_shift1, d_scale1, d_n1w = _rowcall(
        f_norm1_bwd, [_rows(x, rb), _rows(d_h, rb), _rows(dx1, rb), _full(sm["n1w"]), _full(scale1)],
        [_orow(s, D, F32, rb), _oacc(1, D), _oacc(1, D), _oacc(1, D)], n_rows=s, rb=rb, name="norm1_bwd")

    dmod = jnp.concatenate([d_shift1, d_scale1, d_gate1, d_shift2, d_scale2, d_gate2], axis=1)
    small = dict(dmod=dmod, n1w=d_n1w, gb=d_gb, gnw=d_gnw, n2w=d_n2w, cb=d_cb, fnw=d_fnw, w2=d_w2, cw=d_cw)
    return loss, grad_x, half123 + half45, oth123 + oth45, small, ts0


def _win_pieces():
    runs = [(P_GV, 1024, 2048), (P_MA, 5392, 2048), (P_GQ, 0, 1024), (P_AQ, 3088, 2304), (P_LR, 3072, GLA_LR)]
    out = []
    for kc, rc, ln in runs:
        while ln > 0:
            step = min(ln, W_IN_SH - rc % W_IN_SH)
            out.append((kc, rc, step))
            kc, rc, ln = kc + step, rc + step, ln - step
    return out


def _win_assemble(shards, after=()):
    rb = 256

    def body(s_ref, *rest):
        o_ref = rest[-1]
        o_ref[:, W_IN:] = jnp.zeros((rb, P_W - W_IN), o_ref.dtype)
        for kc, rc, ln in _win_pieces():
            o_ref[:, kc:kc + ln] = s_ref[rc // W_IN_SH, :, rc % W_IN_SH:rc % W_IN_SH + ln]

    return pl.pallas_call(
        body, name="w_in_assemble", grid=(D // rb,),
        in_specs=[pl.BlockSpec((4, rb, W_IN_SH), lambda i: (0, i, 0))] + [pl.BlockSpec(memory_space=pl.ANY)] * len(after),
        out_specs=pl.BlockSpec((rb, P_W), lambda i: (i, 0)),
        out_shape=jax.ShapeDtypeStruct((D, P_W), shards.dtype), compiler_params=_params(("parallel",)),
    )(shards, *after)


def _win_split(g):
    rb = 256

    def body(g_ref, o_ref):
        for kc, rc, ln in _win_pieces():
            o_ref[rc // W_IN_SH, :, rc % W_IN_SH:rc % W_IN_SH + ln] = g_ref[:, kc:kc + ln]

    return pl.pallas_call(
        body, name="w_in_grad_split", grid=(D // rb,),
        in_specs=[pl.BlockSpec((rb, P_W), lambda i: (i, 0))], out_specs=pl.BlockSpec((4, rb, W_IN_SH), lambda i: (0, i, 0)),
        out_shape=jax.ShapeDtypeStruct((4, D, W_IN_SH), g.dtype), compiler_params=_params(("parallel",)),
    )(g)


def _ff_to_kernel(a):
    h = W_UP_SH
    return jnp.concatenate([a[:, 0:h], a[:, D_FF:D_FF + h], a[:, h:D_FF], a[:, D_FF + h:]], axis=1)


def _ff_from_kernel(a):
    h = W_UP_SH
    return jnp.concatenate([a[:, 0:h], a[:, 2 * h:3 * h], a[:, h:2 * h], a[:, 3 * h:]], axis=1)


BIG = ("w_in", "w_gla_branch", "w_attn_branch", "w_out", "w_up", "w_down")
SH_SHAPES = ((1024, W_IN_SH), (256, 1024), (256, 256), (256, 1024), (1024, W_UP_SH), (W_DOWN_SH, 1024))
N_BIG = len(BIG)


def _cols_join(t):
    return jnp.concatenate([t[k] for k in range(4)], axis=1)


def _cols_split(t):
    cols = t.shape[1] // 4
    return jnp.stack([t[:, k * cols:(k + 1) * cols] for k in range(4)])


def _me():
    return lax.axis_index("x"), lax.axis_index("y"), lax.axis_index("c")


HBM = pl.BlockSpec(memory_space=pltpu.HBM)
VMEM_SPEC = pl.BlockSpec(memory_space=pltpu.VMEM)


def _allgather8(xs, name):
    rows = xs.shape[0]

    def body(x_ref, out_ref, send_sems, recv_sems, local_sem):
        x, y, c = _me()
        me = 4 * x + 2 * y + c
        mine = pltpu.make_async_copy(x_ref, out_ref.at[me], local_sem)
        mine.start()
        flips = [(k >> 2 & 1, k >> 1 & 1, k & 1) for k in range(1, 8)]

        def peer(f):
            return (jnp.where(f[0] == 1, 1 - x, x), jnp.where(f[1] == 1, 1 - y, y), jnp.where(f[2] == 1, 1 - c, c))

        sends = []
        for k, f in enumerate(flips):
            cp = pltpu.make_async_remote_copy(src_ref=x_ref, dst_ref=out_ref.at[me], send_sem=send_sems.at[k],
                                              recv_sem=recv_sems.at[k], device_id=peer(f), device_id_type=MESH)
            cp.start()
            sends.append(cp)
        for k, f in enumerate(flips):
            px, py, pc = peer(f)
            pltpu.make_async_remote_copy(src_ref=x_ref, dst_ref=out_ref.at[4 * px + 2 * py + pc], send_sem=send_sems.at[k],
                                         recv_sem=recv_sems.at[k], device_id=peer(f), device_id_type=MESH).wait_recv()
        for cp in sends:
            cp.wait_send()
        mine.wait()

    return pl.pallas_call(
        body, name=name, out_shape=jax.ShapeDtypeStruct((8, rows, 128), F32),
        in_specs=[VMEM_SPEC], out_specs=VMEM_SPEC,
        scratch_shapes=[pltpu.SemaphoreType.DMA((7,)), pltpu.SemaphoreType.DMA((7,)), pltpu.SemaphoreType.DMA],
        compiler_params=pltpu.CompilerParams(vmem_limit_bytes=VMEM_LIMIT),
    )(xs)


def _half_rows(i, cc, unit):
    rows = SH_SHAPES[i][0] // 2
    return pl.ds(pl.multiple_of(cc * rows, unit), rows)


def _rc(src, dst, sems, to):
    return pltpu.make_async_remote_copy(src_ref=src, dst_ref=dst, send_sem=sems[0], recv_sem=sems[1], device_id=to, device_id_type=MESH)


def _other_chips(x, y):
    return [(1 - x, y), (x, 1 - y), (1 - x, 1 - y)]


def _u_gather_ici(w_sh, idxs):
    def copies(ins, outs, sem):
        x, y, c = _me()
        res = []
        for j, (px, py) in enumerate(_other_chips(x, y)):
            for n, i in enumerate(idxs):
                src = ins[n].at[0, _half_rows(i, c, 16)]
                res.append((_rc(src, outs[n].at[2 * x + y, _half_rows(i, c, 16)], sem(j * len(idxs) + n), (px, py, c)),
                            _rc(src, outs[n].at[2 * px + py, _half_rows(i, c, 16)], sem(j * len(idxs) + n), (px, py, c))))
        return res

    return dict(ins=[w_sh[i] for i in idxs], outs=[jax.ShapeDtypeStruct((4,) + SH_SHAPES[i], BF16) for i in idxs],
                nsem=3 * len(idxs), alias={}, copies=copies)


def _u_gather_d2d(got, idxs):
    def copies(ins, outs, sem):
        x, y, c = _me()
        res = []
        for j, (px, py) in enumerate(_other_chips(x, y)):
            for n, i in enumerate(idxs):
                src = ins[n].at[2 * px + py, _half_rows(i, c, 16)]
                res.append((_rc(src, outs[n].at[2 * px + py, _half_rows(i, c, 16)], sem(j * len(idxs) + n), (x, y, 1 - c)),
                            _rc(src, outs[n].at[2 * px + py, _half_rows(i, 1 - c, 16)], sem(j * len(idxs) + n), (x, y, 1 - c))))
        return res

    return dict(ins=list(got), outs=[jax.ShapeDtypeStruct(g.shape, g.dtype) for g in got], nsem=3 * len(idxs),
                alias={n: n for n in range(len(idxs))}, copies=copies)


def _u_pair_send(gs, idxs):
    def copies(ins, outs, sem):
        x, y, c = _me()
        res = []
        for n, i in enumerate(idxs):
            for sh in range(4):
                cp = _rc(ins[n].at[sh, _half_rows(i, 1 - c, 8)], outs[n].at[sh], sem(4 * n + sh), (x, y, 1 - c))
                res.append((cp, cp))
        return res

    return dict(ins=list(gs), outs=[jax.ShapeDtypeStruct((4, SH_SHAPES[i][0] // 2, SH_SHAPES[i][1]), F32) for i in idxs],
                nsem=4 * len(idxs), alias={}, copies=copies)


def _u_chip_exchange(ts):
    def copies(ins, outs, sem):
        x, y, c = _me()
        res = []
        for j, (px, py) in enumerate(_other_chips(x, y)):
            for n in range(len(ts)):
                cp = _rc(ins[n].at[2 * px + py], outs[n].at[j], sem(j * len(ts) + n), (px, py, c))
                res.append((cp, cp))
        return res

    return dict(ins=list(ts), outs=[jax.ShapeDtypeStruct((3,) + t.shape[1:], t.dtype) for t in ts], nsem=3 * len(ts),
                alias={}, copies=copies)


def _u_pair_join(hs):
    def copies(ins, outs, sem):
        x, y, c = _me()
        res = []
        for n in range(len(hs)):
            cp = _rc(ins[n], outs[n], sem(n), (x, y, 1 - c))
            res.append((cp, cp))
        return res

    return dict(ins=list(hs), outs=[jax.ShapeDtypeStruct(h.shape, h.dtype) for h in hs], nsem=len(hs), alias={}, copies=copies)


def _comm_phase(units, ci, co, send_sems, recv_sems, start):
    ii = oo = off = 0
    for u in units:
        ni, no = len(u["ins"]), len(u["outs"])
        for st, arrival in u["copies"](ci[ii:ii + ni], co[oo:oo + no], lambda k, off=off: (send_sems.at[off + k], recv_sems.at[off + k])):
            if start:
                st.start()
            else:
                st.wait_send()
                arrival.wait_recv()
        ii, oo, off = ii + ni, oo + no, off + u["nsem"]


def _carry(units, n_in, n_out):
    ins = [a for u in units for a in u["ins"]]
    outs = [o for u in units for o in u["outs"]]
    alias, ii, oo = {}, 0, 0
    for u in units:
        for a, b in u["alias"].items():
            alias[n_in + ii + a] = n_out + oo + b
        ii, oo = ii + len(u["ins"]), oo + len(u["outs"])
    nsem = sum(u["nsem"] for u in units)
    scratch = [pltpu.SemaphoreType.DMA((nsem,)), pltpu.SemaphoreType.DMA((nsem,))] if units else []
    return ins, outs, alias, scratch


def _split_units(units, res):
    out, oo = [], 0
    for u in units:
        out.append(list(res[oo:oo + len(u["outs"])]))
        oo += len(u["outs"])
    return out


def _comm_call(name, units):
    ins, outs, alias, scratch = _carry(units, 0, 0)

    def body(*refs):
        ci, co = refs[:len(ins)], refs[len(ins):len(ins) + len(outs)]
        _comm_phase(units, ci, co, refs[-2], refs[-1], True)
        _comm_phase(units, ci, co, refs[-2], refs[-1], False)

    res = pl.pallas_call(body, name=name, out_shape=outs, in_specs=[HBM] * len(ins), out_specs=[HBM] * len(outs),
                         scratch_shapes=scratch, input_output_aliases=alias)(*ins)
    return _split_units(units, res)


SEM = pl.BlockSpec(memory_space=pltpu.SEMAPHORE)
EFFECT = pltpu.SideEffectType.DATAFLOW_SIDE_EFFECTING


def _unit_start(unit, name, after=()):
    bufs = list(unit["ins"]) + [lax.empty(o.shape, o.dtype) for o in unit["outs"]]
    n_i, n_b, ns = len(unit["ins"]), len(bufs), unit["nsem"]

    def body(*refs):
        send_sems, recv_sems = refs[n_b + len(after)], refs[n_b + len(after) + 1]
        for st, _ in unit["copies"](refs[:n_i], refs[n_i:n_b], lambda k: (send_sems.at[k], recv_sems.at[k])):
            st.start()
        refs[-1][...] = jnp.zeros_like(refs[-1])

    res = pl.pallas_call(
        body, name=name,
        out_shape=[pltpu.SemaphoreType.DMA((ns,)), pltpu.SemaphoreType.DMA((ns,))] + [pltpu.HBM(b.shape, b.dtype) for b in bufs]
        + [jax.ShapeDtypeStruct((8, 128), F32)],
        in_specs=[HBM] * n_b + [pl.BlockSpec(memory_space=pl.ANY)] * len(after), out_specs=[SEM, SEM] + [HBM] * n_b + [VMEM_SPEC],
        input_output_aliases={i: 2 + i for i in range(n_b)},
        compiler_params=pltpu.CompilerParams(has_side_effects=EFFECT),
    )(*[pltpu.with_memory_space_constraint(b, pltpu.HBM) for b in bufs], *after)
    return res[0], res[1], list(res[2:2 + n_b]), res[-1]


def _unit_wait(unit, send_sems, recv_sems, bufs, after, name):
    n_i, n_b = len(unit["ins"]), len(bufs)

    def body(*refs):
        ss, rs = refs[n_b], refs[n_b + 1]
        for st, arrival in unit["copies"](refs[:n_i], refs[n_i:n_b], lambda k: (ss.at[k], rs.at[k])):
            st.wait_send()
            arrival.wait_recv()

    res = pl.pallas_call(
        body, name=name, out_shape=[pltpu.HBM(b.shape, b.dtype) for b in bufs],
        in_specs=[HBM] * n_b + [SEM, SEM] + [pl.BlockSpec(memory_space=pl.ANY)] * len(after), out_specs=[HBM] * n_b,
        input_output_aliases={i: i for i in range(n_b)}, compiler_params=pltpu.CompilerParams(has_side_effects=EFFECT),
    )(*bufs, send_sems, recv_sems, *after)
    return list(res[:n_i]), list(res[n_i:])


def _pair_add(g, land, core, name):
    _, rows, cols = g.shape
    half = rows // 2
    rb = _tile(half, 256, 16)
    nb = half // rb

    def body(c_ref, g_ref, l_ref, o_ref):
        o_ref[...] = (g_ref[...] + l_ref[...]).astype(BF16)

    return pl.pallas_call(
        body, name=name,
        grid_spec=pltpu.PrefetchScalarGridSpec(
            num_scalar_prefetch=1, grid=(4, nb),
            in_specs=[pl.BlockSpec((1, rb, cols), lambda s, i, c_ref: (s, c_ref[0] * nb + i, 0)),
                      pl.BlockSpec((1, rb, cols), lambda s, i, c_ref: (s, i, 0))],
            out_specs=pl.BlockSpec((1, rb, cols), lambda s, i, c_ref: (s, i, 0))),
        out_shape=jax.ShapeDtypeStruct((4, half, cols), BF16),
        compiler_params=_params(("parallel", "parallel")),
    )(core, g, land)


def _chip_sum(t, r, chip, name):
    _, half, cols = t.shape
    rb = _tile(half, 256, 16)

    def body(s_ref, t_ref, r_ref, o_ref):
        o_ref[...] = ((t_ref[0].astype(F32) + r_ref[0].astype(F32)) + r_ref[1].astype(F32)) + r_ref[2].astype(F32)

    return pl.pallas_call(
        body, name=name,
        grid_spec=pltpu.PrefetchScalarGridSpec(
            num_scalar_prefetch=1, grid=(half // rb,),
            in_specs=[pl.BlockSpec((1, rb, cols), lambda i, s_ref: (s_ref[0], i, 0)),
                      pl.BlockSpec((3, rb, cols), lambda i, s_ref: (0, i, 0))],
            out_specs=pl.BlockSpec((rb, cols), lambda i, s_ref: (i, 0))),
        out_shape=jax.ShapeDtypeStruct((half, cols), F32),
        compiler_params=_params(("parallel",)),
    )(chip, t, r)


def _adam_math(wv, gv, mv, vv):
    mn = ADAM_B1 * mv + (1.0 - ADAM_B1) * gv
    vn = ADAM_B2 * vv + (1.0 - ADAM_B2) * (gv * gv)
    m_hat = mn / (1.0 - ADAM_B1 ** ADAM_STEP)
    v_hat = vn / (1.0 - ADAM_B2 ** ADAM_STEP)
    return -ADAM_LR * (m_hat / (jnp.sqrt(v_hat) + ADAM_EPS) + ADAM_WD * wv), mn, vn


def _adamw_halves(wt, mt, vt, mine, theirs, core, name):
    _, rows, cols = wt.shape
    half = rows // 2
    rb = _tile(half, 256, 8)
    nb = half // rb

    def body(c_ref, w_ref, m_ref, v_ref, a_ref, b_ref, g_ref, d_ref, mo_ref, vo_ref):
        gv = jnp.where(pl.program_id(0) == c_ref[0], a_ref[...], b_ref[...])
        dl, mn, vn = _adam_math(w_ref[...], gv, m_ref[...], v_ref[...])
        g_ref[...] = gv
        d_ref[...] = dl
        mo_ref[...] = mn
        vo_ref[...] = vn

    full = pl.BlockSpec((None, rb, cols), lambda hf, i, c_ref: (0, hf * nb + i, 0))
    part = pl.BlockSpec((rb, cols), lambda hf, i, c_ref: (i, 0))
    return pl.pallas_call(
        body, name=name,
        grid_spec=pltpu.PrefetchScalarGridSpec(num_scalar_prefetch=1, grid=(2, nb), in_specs=[full, full, full, part, part],
                                               out_specs=[full] * 4),
        out_shape=[jax.ShapeDtypeStruct((1, rows, cols), F32)] * 4,
        compiler_params=_params(("parallel", "parallel")),
    )(core, wt, mt, vt, mine, theirs)


SG_REP = 144
SG_LOSS = 136
SG_W2, SG_CW = SG_REP, SG_REP + 4 * 16
SG_ROWS = SG_CW + 4 * 40
SP_ROWS = SG_REP + 16 + 40


def _mod_shard(c_all, ada_w_sh):
    def body(c_ref, w_ref, o_ref):
        cv = c_ref[...]
        o_ref[...] = _dg((cv * _sigmoid(cv)).astype(BF16), w_ref[...].astype(BF16), 1, 0)

    return pl.pallas_call(body, name="mod_shard", out_shape=jax.ShapeDtypeStruct((8, 1536), F32),
                          in_specs=[VMEM_SPEC, VMEM_SPEC], out_specs=VMEM_SPEC,
                          compiler_params=pltpu.CompilerParams(vmem_limit_bytes=VMEM_LIMIT))(c_all, ada_w_sh)


def _mod_select(mod_all, ada_b4):
    def body(m_ref, b_ref, o_ref):
        x, y, c = _me()
        me = 4 * x + 2 * y + c
        for sh in range(4):
            o_ref[sh] = m_ref[2 * sh, me] + b_ref[sh]

    return pl.pallas_call(body, name="mod_select", out_shape=jax.ShapeDtypeStruct((4, 12, 128), F32),
                          in_specs=[VMEM_SPEC, VMEM_SPEC], out_specs=VMEM_SPEC)(mod_all, ada_b4)


def _small_reduce(sg_all):
    def body(g_ref, o_ref):
        x, y, c = _me()
        s_me = 2 * x + y
        w2_rows = pl.ds(pl.multiple_of(SG_W2 + 16 * s_me, 8), 16)
        cw_rows = pl.ds(pl.multiple_of(SG_CW + 40 * s_me, 8), 40)
        a = g_ref[0, 0:SG_REP, :]
        b = g_ref[0, w2_rows, :]
        d = g_ref[0, cw_rows, :]
        for dev in range(1, 8):
            a = a + g_ref[dev, 0:SG_REP, :]
            b = b + g_ref[dev, w2_rows, :]
            d = d + g_ref[dev, cw_rows, :]
        o_ref[0:SG_REP, :] = a
        o_ref[SG_REP:SG_REP + 16, :] = b
        o_ref[SG_REP + 16:SP_ROWS, :] = d

    return pl.pallas_call(body, name="small_grad_reduce", out_shape=jax.ShapeDtypeStruct((SP_ROWS, 128), F32),
                          in_specs=[VMEM_SPEC], out_specs=VMEM_SPEC)(sg_all)


def _ada_grad(dmod_all, c_bc):
    def body(g_ref, c_ref, o_ref):
        x, y, c = _me()
        s_me = 2 * x + y
        for k in range(12):
            acc = jnp.zeros((D, 128), F32)
            for b in range(8):
                cv = c_ref[b]
                acc = acc + (cv * _sigmoid(cv)) * g_ref[s_me, k, b:b + 1, :]
            o_ref[:, k * 128:(k + 1) * 128] = acc

    return pl.pallas_call(body, name="ada_w_grad", out_shape=jax.ShapeDtypeStruct((D, 1536), F32),
                          in_specs=[VMEM_SPEC, VMEM_SPEC], out_specs=VMEM_SPEC,
                          compiler_params=pltpu.CompilerParams(vmem_limit_bytes=VMEM_LIMIT))(dmod_all, c_bc)


def _adamw(wt, g, m, v, name):
    rows, cols = wt.shape
    rb = _tile(rows, 256, 8)

    def fn(c, i, wv, gv, mv, vv):
        return _adam_math(wv, gv, mv, vv)

    return _rowcall(fn, [_rows(t, rb) for t in (wt, g, m, v)], [_orow(rows, cols, F32, rb)] * 3,
                    n_rows=rows, rb=rb, name=name)


def _pad_rows(t, rows):
    flat = t.reshape(-1)
    return jnp.pad(flat, (0, rows * 128 - flat.shape[0])).reshape(rows, 128)


SP_LAYOUT = (("ada_b", 48), ("norm1_w", 8), ("gla_gate_b", 8), ("gla_norm_w", 8), ("norm2_w", 8), ("conv_b", 48),
             ("final_norm_w", 8), (None, 8), ("gla_gate_w2", 16), ("conv_w", 40))


def _pack_small(d):
    return jnp.concatenate([jnp.zeros((rows, 128), F32) if n is None else _pad_rows(d[n].astype(F32), rows)
                            for n, rows in SP_LAYOUT], axis=0)


def _unpack_small(pk, shapes):
    out, off = {}, 0
    for n, rows in SP_LAYOUT:
        if n is not None:
            shp = shapes[n]
            out[n] = pk[off:off + rows].reshape(-1)[:math.prod(shp)].reshape(shp)
        off += rows
    return out


def kernel(x, c, positions, ada_w, ada_b, norm1_w, w_in, gla_gate_w2, gla_gate_b, gla_norm_w, w_gla_branch, w_attn_branch, w_out, norm2_w, w_up, conv_w, conv_b, w_down, final_norm_w, loss_target, m_ada_w, m_ada_b, m_norm1_w, m_w_in, m_gla_gate_w2, m_gla_gate_b, m_gla_norm_w, m_w_gla_branch, m_w_attn_branch, m_w_out, m_norm2_w, m_w_up, m_conv_w, m_conv_b, m_w_down, m_final_norm_w, v_ada_w, v_ada_b, v_norm1_w, v_w_in, v_gla_gate_w2, v_gla_gate_b, v_gla_norm_w, v_w_gla_branch, v_w_attn_branch, v_w_out, v_norm2_w, v_w_up, v_conv_w, v_conv_b, v_w_down, v_final_norm_w):
    s = x.shape[1]
    names = ("ada_w", "ada_b", "norm1_w", "w_in", "gla_gate_w2", "gla_gate_b", "gla_norm_w", "w_gla_branch", "w_attn_branch",
             "w_out", "norm2_w", "w_up", "conv_w", "conv_b", "w_down", "final_norm_w")
    wts = dict(zip(names, (ada_w, ada_b, norm1_w, w_in, gla_gate_w2, gla_gate_b, gla_norm_w, w_gla_branch, w_attn_branch,
                           w_out, norm2_w, w_up, conv_w, conv_b, w_down, final_norm_w)))
    ms = dict(zip(names, (m_ada_w, m_ada_b, m_norm1_w, m_w_in, m_gla_gate_w2, m_gla_gate_b, m_gla_norm_w, m_w_gla_branch,
                          m_w_attn_branch, m_w_out, m_norm2_w, m_w_up, m_conv_w, m_conv_b, m_w_down, m_final_norm_w)))
    vs = dict(zip(names, (v_ada_w, v_ada_b, v_norm1_w, v_w_in, v_gla_gate_w2, v_gla_gate_b, v_gla_norm_w, v_w_gla_branch,
                          v_w_attn_branch, v_w_out, v_norm2_w, v_w_up, v_conv_w, v_conv_b, v_w_down, v_final_norm_w)))

    pk0 = jnp.concatenate([_pad_rows(c, 8), _pad_rows(gla_gate_w2, 16), _pad_rows(conv_w, 40)], axis=0)
    sm_all = _allgather8(pk0, "gather_small")
    c_all = sm_all[:, 0:8, :].reshape(8, D)
    w2_full = sm_all[0::2, 8:24, :].transpose(1, 0, 2).reshape(GLA_LR, 512)
    cw_full = sm_all[0::2, 24:64, :].reshape(4, 40 * 128)[:, :3 * W_UP_SH].reshape(4, 3, W_UP_SH).transpose(1, 0, 2).reshape(3, 2 * D_FF)

    mod_sh = _mod_shard(c_all, ada_w[0])
    mod_all = _allgather8(mod_sh.reshape(96, 128), "gather_mod")

    w_sh = [wts[n].astype(BF16) for n in BIG]
    u_g0 = _u_gather_ici(w_sh, (0,))
    g0 = (u_g0,) + _unit_start(u_g0, "gather_w_in_start", after=[mod_all])
    mod = _mod_select(mod_all.reshape(8, 8, 12, 128) + g0[4][0, 0], ada_b.reshape(4, 12, 128)).reshape(6, D)

    core = lax.axis_index("c").astype(jnp.int32).reshape(1)
    chip = (2 * lax.axis_index("x") + lax.axis_index("y")).astype(jnp.int32)
    sm = dict(n1w=norm1_w, n2w=norm2_w, fnw=final_norm_w.reshape(1, D), gnw=gla_norm_w, gb=gla_gate_b,
              w2=jnp.pad(w2_full, ((0, 128 - GLA_LR), (0, 0))), cw=_ff_to_kernel(cw_full), cb=_ff_to_kernel(conv_b))
    loss, grad_x, halves, others, small, ts0 = _local_step(x[0], mod, positions.reshape(s, 1), loss_target[0], sm, w_sh,
                                                               g0, chip, core)

    dcw = _ff_from_kernel(small["cw"]).reshape(3, 4, W_UP_SH).transpose(1, 0, 2)
    dw2 = small["w2"][:GLA_LR].reshape(GLA_LR, 4, 128).transpose(1, 0, 2)
    sg = jnp.concatenate(
        [_pad_rows(small["dmod"], 48), _pad_rows(small["n1w"], 8), _pad_rows(small["gb"], 8), _pad_rows(small["gnw"], 8),
         _pad_rows(small["n2w"], 8), _pad_rows(_ff_from_kernel(small["cb"]), 48), _pad_rows(small["fnw"], 8), _pad_rows(loss, 8)]
        + [_pad_rows(dw2[k], 16) for k in range(4)] + [_pad_rows(dcw[k], 40) for k in range(4)], axis=0)
    sg_all = _allgather8(sg, "gather_small_grads")
    u_ex = _u_chip_exchange([ts0])
    pending = (u_ex,) + _unit_start(u_ex, "grad_exchange_w_in_start", after=[sg_all])
    sg_all = sg_all + pending[4][0, 0]
    g_small_pk = _small_reduce(sg_all)
    dmod_all = sg_all[:, 0:48, :].reshape(8, 4, 12, 128).transpose(1, 2, 0, 3)
    g_ada_w = _ada_grad(dmod_all, jnp.broadcast_to(c_all[:, :, None], (8, D, 128)))

    shapes = {n: wts[n].shape for n in names}
    g_small = _unpack_small(g_small_pk, shapes)
    grads = {"ada_w": g_ada_w.reshape(1, D, 1536), **g_small}
    deltas, new_m, new_v = {}, {}, {}
    for n, mine, theirs in zip(BIG[1:], halves, others):
        grads[n], deltas[n], new_m[n], new_v[n] = _adamw_halves(wts[n], ms[n], vs[n], mine, theirs, core, "adamw_" + n)
    shp = ada_w.shape
    d_, m_, v_ = _adamw(ada_w[0], g_ada_w, m_ada_w[0], v_ada_w[0], "adamw_ada_w")
    deltas["ada_w"], new_m["ada_w"], new_v["ada_w"] = d_.reshape(shp), m_.reshape(shp), v_.reshape(shp)
    d_, m_, v_ = _adamw(_pack_small(wts), g_small_pk, _pack_small(ms), _pack_small(vs), "adamw_small")
    for dst, pk in ((deltas, d_), (new_m, m_), (new_v, v_)):
        dst.update(_unpack_small(pk, shapes))

    [t0], [r0] = _unit_wait(*pending[:4], after=[d_, deltas["ada_w"], deltas["w_up"], deltas["w_down"]], name="grad_exchange_w_in_wait")
    half0 = _chip_sum(t0, r0, chip.reshape(1), "grad_chip_sum_w_in")
    [[oth0]] = _comm_call("grad_join_w_in", [_u_pair_join([half0])])
    grads["w_in"], deltas["w_in"], new_m["w_in"], new_v["w_in"] = _adamw_halves(w_in, m_w_in, v_w_in, half0, oth0, core, "adamw_w_in")

    return (g_small_pk[SG_LOSS, 0], grad_x.reshape(1, s, D), *[grads[n] for n in names], *[deltas[n] for n in names],
            *[new_m[n] for n in names], *[new_v[n] for n in names])
```

```python
import math

import jax
import jax.numpy as jnp
from jax import lax
from jax.experimental import pallas as pl
from jax.experimental.pallas import tpu as pltpu

F32, BF16 = jnp.float32, jnp.bfloat16
MESH = pl.DeviceIdType.MESH

D = 1024
EPS = 1e-6
GLA_H, GLA_DK, GLA_DV, GLA_LR = 4, 128, 256, 16
GLA_TAU = 16.0
GLA_CHUNK = 64
GLA_BLOCK = 512
ATT_GROUPS = ((128, 1), (512, 4), (2048, 16))
ATT_BLK = 128
ATT_HD = 64
ATT_W = 768
D_FF = 2816
ROPE_THETA = 10000.0
P_W = 7680
P_GV, P_GR, P_MA, P_MB, P_GQ, P_GK, P_AQ, P_AK, P_AV, P_LR = 0, 1024, 2048, 3072, 4096, 4608, 5120, 5888, 6656, 7424
W_IN = 7440
W_IN_SH, W_UP_SH, W_DOWN_SH = 1860, 1408, 704
VMEM_LIMIT = 56 * 1024 * 1024
ADAM_LR, ADAM_B1, ADAM_B2, ADAM_EPS, ADAM_WD, ADAM_STEP = 0.001, 0.9, 0.999, 1e-08, 0.01, 10
NEG = -1e30


def _tile(n, target, unit=128):
    best = None
    for t in range(unit, min(n, target) + 1, unit):
        if n % t == 0:
            best = t
    return best or n


def _params(sem):
    return pltpu.CompilerParams(dimension_semantics=sem, vmem_limit_bytes=VMEM_LIMIT)


def _dg(a, b, ca, cb):
    return lax.dot_general(a, b, (((ca,), (cb,)), ((), ())), preferred_element_type=F32)


def _sigmoid(v):
    return 1.0 / (1.0 + jnp.exp(-v))


def _ff_block(j):
    return (j % 2) * 2 + j // 2


def _mm(a, b, name, *, ta=False, tb=False, out_dtype=BF16, tm=1024, tn=1536, tk=1024, n_outer=True, comm=(),
        b_shards=False, o_shards=False):
    m = a.shape[1] if ta else a.shape[0]
    k = a.shape[0] if ta else a.shape[1]
    if b_shards:
        n = b.shape[1] if tb else 4 * W_UP_SH
        tn, tk = (tn, W_UP_SH) if tb else (W_UP_SH, tk)
    else:
        n = b.shape[0] if tb else b.shape[1]
    if o_shards:
        tn = W_UP_SH
    tm, tn, tk = _tile(m, tm), _tile(n, tn), _tile(k, tk)
    nm, nn, nk = m // tm, n // tn, k // tk
    in_out = out_dtype == F32
    c_ins, c_outs, c_alias, c_scratch = _carry(comm, 2, 1)

    def body(a_ref, b_ref, *rest):
        ci, o_ref, co = rest[:len(c_ins)], rest[len(c_ins)], rest[len(c_ins) + 1:len(c_ins) + 1 + len(c_outs)]
        scr = rest[len(c_ins) + 1 + len(c_outs):]
        kk = pl.program_id(2)
        if comm:
            step = (pl.program_id(0) * (nm if n_outer else nn) + pl.program_id(1)) * nk + kk

            @pl.when(step == 0)
            def _():
                _comm_phase(comm, ci, co, scr[-2], scr[-1], True)

        _mm_step(a_ref, b_ref, o_ref, scr, kk)
        if comm:
            @pl.when(step == nm * nn * nk - 1)
            def _():
                _comm_phase(comm, ci, co, scr[-2], scr[-1], False)

    def _mm_step(a_ref, b_ref, o_ref, scr, kk):
        p = _dg(a_ref[...].astype(BF16), b_ref[...].astype(BF16), 0 if ta else 1, 1 if tb else 0)
        if nk == 1:
            o_ref[...] = p.astype(o_ref.dtype)
        else:
            acc = o_ref if in_out else scr[0]

            @pl.when(kk == 0)
            def _():
                acc[...] = p

            @pl.when(kk > 0)
            def _():
                acc[...] += p

            if not in_out:
                @pl.when(kk == nk - 1)
                def _():
                    o_ref[...] = acc[...].astype(o_ref.dtype)

    if n_outer:
        ij = lambda g0, g1: (g1, g0)
        grid = (nn, nm, nk)
    else:
        ij = lambda g0, g1: (g0, g1)
        grid = (nm, nn, nk)
    a_map = (lambda g0, g1, kk: (kk, ij(g0, g1)[0])) if ta else (lambda g0, g1, kk: (ij(g0, g1)[0], kk))
    if b_shards and tb:
        b_spec = pl.BlockSpec((None, tn, tk), lambda g0, g1, kk: (_ff_block(kk), ij(g0, g1)[1], 0))
    elif b_shards:
        b_spec = pl.BlockSpec((None, tk, tn), lambda g0, g1, kk: (_ff_block(ij(g0, g1)[1]), kk, 0))
    elif tb:
        b_spec = pl.BlockSpec((tn, tk), lambda g0, g1, kk: (ij(g0, g1)[1], kk))
    else:
        b_spec = pl.BlockSpec((tk, tn), lambda g0, g1, kk: (kk, ij(g0, g1)[1]))
    if o_shards:
        o_spec = pl.BlockSpec((None, tm, tn), lambda g0, g1, kk: (_ff_block(ij(g0, g1)[1]), ij(g0, g1)[0], 0))
        o_shape = jax.ShapeDtypeStruct((4, m, W_UP_SH), out_dtype)
    else:
        o_spec = pl.BlockSpec((tm, tn), lambda g0, g1, kk: ij(g0, g1))
        o_shape = jax.ShapeDtypeStruct((m, n), out_dtype)
    res = pl.pallas_call(
        body, name=name, grid=grid,
        in_specs=[pl.BlockSpec((tk, tm) if ta else (tm, tk), a_map), b_spec] + [HBM] * len(c_ins),
        out_specs=[o_spec] + [HBM] * len(c_outs),
        out_shape=[o_shape] + c_outs,
        scratch_shapes=([] if (in_out or nk == 1) else [pltpu.VMEM((tm, tn), F32)]) + c_scratch,
        input_output_aliases=c_alias,
        compiler_params=_params(("arbitrary",) * 3 if comm else ("parallel", "parallel", "arbitrary")),
    )(a, b, *c_ins)
    return (res[0], _split_units(comm, res[1:])) if comm else res[0]


def _rows(arr, rb, w=None, j=0):
    w = arr.shape[1] if w is None else w
    if callable(j):
        return arr, pl.BlockSpec((rb, w), lambda c, i: (i, j(c)))
    return arr, pl.BlockSpec((rb, w), lambda c, i: (i, j))


def _full(arr, w=None, j=0):
    w = arr.shape[1] if w is None else w
    if callable(j):
        return arr, pl.BlockSpec((arr.shape[0], w), lambda c, i: (0, j(c)))
    return arr, pl.BlockSpec((arr.shape[0], w), lambda c, i: (0, j))


def _halo(arr, rb, hb, w, j, before):
    per = rb // hb
    last = arr.shape[0] // hb - 1
    if before:
        rmap = lambda i: jnp.maximum(i * per - 1, 0)
    else:
        rmap = lambda i: jnp.minimum((i + 1) * per, last)
    return arr, pl.BlockSpec((hb, w), lambda c, i: (rmap(i), j(c) if callable(j) else j))


def _rowcall(fn, ins, outs, *, n_rows, rb, name, ncol=1, into=None, after=()):
    n_in = len(ins)
    nr = n_rows // rb
    unread = ([] if into is None else [into[0]]) + list(after)
    n_skip = len(unread)

    def body(*refs):
        c, i = pl.program_id(0), pl.program_id(1)
        res = fn(c, i, *[r[...] for r in refs[:n_in]])
        for val, spec, o_ref in zip(res, outs, refs[n_in + n_skip:]):
            if spec[2] == "row":
                o_ref[...] = val.astype(o_ref.dtype)
            else:
                @pl.when(i == 0)
                def _(o_ref=o_ref, val=val):
                    o_ref[...] = val.astype(o_ref.dtype)

                @pl.when(i > 0)
                def _(o_ref=o_ref, val=val):
                    o_ref[...] += val.astype(o_ref.dtype)

    out_specs = []
    for shape, dt, kind, block, col in outs:
        if kind == "row":
            out_specs.append(pl.BlockSpec(block, lambda c, i, col=col: (i, col(c))))
        else:
            out_specs.append(pl.BlockSpec(block, lambda c, i, col=col: (0, col(c))))
    return pl.pallas_call(
        body, name=name, grid=(ncol, nr),
        in_specs=[s for _, s in ins] + [pl.BlockSpec(memory_space=pl.ANY)] * n_skip, out_specs=out_specs,
        out_shape=[jax.ShapeDtypeStruct(o[0], o[1]) for o in outs],
        input_output_aliases={} if into is None else {n_in: into[1]},
        compiler_params=_params(("parallel", "arbitrary")),
    )(*[a for a, _ in ins], *unread)


def _orow(n_rows, w, dt, rb, bw=None, col=lambda c: 0):
    return ((n_rows, w), dt, "row", (rb, bw or w), col)


def _oacc(r, w, bw=None, col=lambda c: 0):
    return ((r, w), F32, "acc", (r, bw or w), col)


def _csum(v):
    return jnp.sum(v, axis=0, keepdims=True)


def _rms(v):
    return lax.rsqrt(jnp.mean(v * v, axis=-1, keepdims=True) + EPS)


def _norm_bwd(xv, dh, w, scale):
    r = _rms(xv)
    xh = xv * r
    dxh = dh * (w * (1.0 + scale))
    dx = r * (dxh - xh * jnp.mean(dxh * xh, axis=-1, keepdims=True))
    t = dh * xh
    return dx, _csum(dh), _csum(t * w), _csum(t * (1.0 + scale))


def _rope_tables(pos_col, invf, s):
    def fn(c, i, pos, f):
        ang = pos.astype(F32) * f
        lane = lax.broadcasted_iota(jnp.int32, ang.shape, 1)
        sign = jnp.where((lane % ATT_HD) < ATT_HD // 2, -1.0, 1.0)
        return jnp.cos(ang), jnp.sin(ang) * sign

    rb = 512
    return _rowcall(fn, [_rows(pos_col, rb), _full(invf)], [_orow(s, 128, F32, rb), _orow(s, 128, F32, rb)],
                    n_rows=s, rb=rb, name="rope_tables")


def _swap_halves(t):
    n = t.shape[1]
    lane = lax.broadcasted_iota(jnp.int32, t.shape, 1)
    return jnp.where((lane % ATT_HD) < ATT_HD // 2, pltpu.roll(t, n - 32, 1), pltpu.roll(t, 32, 1))


def _rope_apply(t, cos, sin_signed, inverse):
    cw = jnp.concatenate([cos] * (t.shape[1] // 128), axis=1)
    sw = jnp.concatenate([sin_signed] * (t.shape[1] // 128), axis=1)
    if inverse:
        sw = -sw
    return t * cw + _swap_halves(t) * sw


DIL_ROWS = 512


def _to_dilated(scr, val, out_ref, r):
    if r == 1:
        out_ref[...] = val.astype(out_ref.dtype)
        return
    n = val.shape[0] // r
    for hh in range(2):
        scr[hh] = val[:, hh * 128:(hh + 1) * 128]
        for pr in range(r):
            out_ref[:, pr * 256 + hh * 128:pr * 256 + (hh + 1) * 128] = scr[hh, pl.ds(pr, n, stride=r), :].astype(out_ref.dtype)


def _from_dilated(scr, in_ref, r):
    if r == 1:
        return in_ref[...].astype(F32)
    n = in_ref.shape[0]
    for hh in range(2):
        for pr in range(r):
            scr[hh, pl.ds(pr, n, stride=r), :] = in_ref[:, pr * 256 + hh * 128:pr * 256 + (hh + 1) * 128].astype(F32)
    return jnp.concatenate([scr[0], scr[1]], axis=1)


def _dil_spec(r):
    return pl.BlockSpec((DIL_ROWS // r, r * 256), lambda i: (i, 0))


def _dil_shape(s, r, dt):
    return jax.ShapeDtypeStruct((s // r, r * 256), dt)


_DIL_SCRATCH = [pltpu.VMEM((2, DIL_ROWS, 128), F32)]
_RS = tuple(r for _, r in ATT_GROUPS)


def _rope_fwd(p, cos_t, sin_t, s):
    def body(*refs):
        ins, cs, sn, outs, scr = refs[:9], refs[9][...], refs[10][...], refs[11:20], refs[20]
        for t in range(3):
            for g, r in enumerate(_RS):
                val = ins[3 * t + g][...].astype(F32)
                _to_dilated(scr, _rope_apply(val, cs, sn, False) if t < 2 else val, outs[3 * t + g], r)

    res = pl.pallas_call(
        body, name="rope", grid=(s // DIL_ROWS,),
        in_specs=[pl.BlockSpec((DIL_ROWS, 256), lambda i, c=base // 256 + g: (i, c)) for base in (P_AQ, P_AK, P_AV) for g in range(3)]
        + [pl.BlockSpec((DIL_ROWS, 128), lambda i: (i, 0))] * 2,
        out_specs=[_dil_spec(r) for _ in range(3) for r in _RS],
        out_shape=[_dil_shape(s, r, BF16) for _ in range(3) for r in _RS],
        scratch_shapes=_DIL_SCRATCH, compiler_params=_params(("parallel",)),
    )(*([p] * 9), cos_t, sin_t)
    return res[0:3], res[3:6], res[6:9]


def _attn_combine(att, s):
    def body(o0, o1, o2, l0, l1, l2, o_ref, lse_ref, od1, od2, ld1, ld2, scr):
        ov = [_from_dilated(scr, ref, r) for ref, r in zip((o0, o1, o2), _RS)]
        lv = [_from_dilated(scr, ref, r) for ref, r in zip((l0, l1, l2), _RS)]
        mx = jnp.maximum(jnp.maximum(lv[0], lv[1]), lv[2])
        ev = [jnp.exp(l - mx) for l in lv]
        z = ev[0] + ev[1] + ev[2]
        o = ((ev[0] * ov[0] + ev[1] * ov[1] + ev[2] * ov[2]) / z).astype(BF16)
        lse = mx + jnp.log(z)
        o_ref[...] = o
        lse_ref[...] = lse
        for ref, r in zip((od1, od2), _RS[1:]):
            _to_dilated(scr, o.astype(F32), ref, r)
        for ref, r in zip((ld1, ld2), _RS[1:]):
            _to_dilated(scr, lse, ref, r)

    return pl.pallas_call(
        body, name="attn_combine", grid=(s // DIL_ROWS,),
        in_specs=[_dil_spec(r) for r in _RS] * 2,
        out_specs=[_dil_spec(1)] * 2 + [_dil_spec(r) for r in _RS[1:]] * 2,
        out_shape=[_dil_shape(s, 1, BF16), _dil_shape(s, 1, F32)] + [_dil_shape(s, r, BF16) for r in _RS[1:]]
        + [_dil_shape(s, r, F32) for r in _RS[1:]],
        scratch_shapes=_DIL_SCRATCH, compiler_params=_params(("parallel",)),
    )(*[a[0] for a in att], *[a[1] for a in att])


def _dilate(t, s):
    def body(t_ref, o1, o2, scr):
        val = t_ref[...].astype(F32)
        for ref, r in zip((o1, o2), _RS[1:]):
            _to_dilated(scr, val, ref, r)

    return pl.pallas_call(
        body, name="attn_dilate", grid=(s // DIL_ROWS,), in_specs=[_dil_spec(1)], out_specs=[_dil_spec(r) for r in _RS[1:]],
        out_shape=[_dil_shape(s, r, t.dtype) for r in _RS[1:]], scratch_shapes=_DIL_SCRATCH, compiler_params=_params(("parallel",)),
    )(t)


def _rope_bwd(datt, d_glr, dp, cos_t, sin_t, s):
    tail = P_W - P_AQ

    def body(*refs):
        ins, cs, sn, glr, o_ref, scr = refs[:9], refs[9][...], refs[10][...], refs[11], refs[13], refs[14]
        for t in range(3):
            for g, r in enumerate(_RS):
                val = _from_dilated(scr, ins[3 * t + g], r)
                o_ref[:, t * ATT_W + g * 256:t * ATT_W + (g + 1) * 256] = (_rope_apply(val, cs, sn, True) if t < 2 else val).astype(BF16)
        o_ref[:, 3 * ATT_W:3 * ATT_W + 128] = glr[...]
        o_ref[:, 3 * ATT_W + 128:] = jnp.zeros((DIL_ROWS, tail - 3 * ATT_W - 128), BF16)

    return pl.pallas_call(
        body, name="rope_bwd", grid=(s // DIL_ROWS,),
        in_specs=[_dil_spec(r) for _ in range(3) for r in _RS] + [pl.BlockSpec((DIL_ROWS, 128), lambda i: (i, 0))] * 3
        + [pl.BlockSpec(memory_space=pl.ANY)],
        out_specs=pl.BlockSpec((DIL_ROWS, tail), lambda i: (i, P_AQ // tail)),
        out_shape=jax.ShapeDtypeStruct((s, P_W), BF16), input_output_aliases={12: 0},
        scratch_shapes=_DIL_SCRATCH, compiler_params=_params(("parallel",)),
    )(*[datt[g][t] for t in range(3) for g in range(3)], cos_t, sin_t, d_glr, dp)


def _tri_dot(tri, t):
    tb = tri.astype(BF16)
    hi = t.astype(BF16)
    r1 = t - hi.astype(F32)
    mid = r1.astype(BF16)
    lo = (r1 - mid.astype(F32)).astype(BF16)
    return _dg(tb, hi, 1, 0) + _dg(tb, mid, 1, 0) + _dg(tb, lo, 1, 0)


def _gla_decays(la_c, tri):
    b = _tri_dot(tri, la_c)
    row = lax.broadcasted_iota(jnp.int32, b.shape, 0)
    bmid = jnp.sum(jnp.where(row == GLA_CHUNK // 2 - 1, b, 0.0), axis=0, keepdims=True)
    blast = jnp.sum(jnp.where(row == GLA_CHUNK - 1, b, 0.0), axis=0, keepdims=True)
    return b, bmid, blast


def _gla_fwd(p, la, s, comm=()):
    tb, ch = GLA_BLOCK, GLA_CHUNK
    nb, nc = s // tb, tb // ch
    scale = GLA_DK ** -0.5
    c_ins, c_outs, c_alias, c_scratch = _carry(comm, 4, 2)

    def body(q_ref, k_ref, v_ref, la_ref, *rest):
        ci, (o_ref, st_ref) = rest[:len(c_ins)], rest[len(c_ins):len(c_ins) + 2]
        co, state = rest[len(c_ins) + 2:len(c_ins) + 2 + len(c_outs)], rest[len(c_ins) + 2 + len(c_outs)]
        step = pl.program_id(0)
        if comm:
            @pl.when(step == 0)
            def _():
                _comm_phase(comm, ci, co, rest[-2], rest[-1], True)

        _gla_fwd_step(q_ref, k_ref, v_ref, la_ref, o_ref, st_ref, state)
        if comm:
            @pl.when(step == nb - 1)
            def _():
                _comm_phase(comm, ci, co, rest[-2], rest[-1], False)

    def _gla_fwd_step(q_ref, k_ref, v_ref, la_ref, o_ref, st_ref, state):
        @pl.when(pl.program_id(0) == 0)
        def _():
            state[...] = jnp.zeros_like(state)

        ri = lax.broadcasted_iota(jnp.int32, (ch, ch), 0)
        ci = lax.broadcasted_iota(jnp.int32, (ch, ch), 1)
        causal = ci <= ri
        tri = causal.astype(F32)

        def chunk(c, carry):
            sl = pl.ds(pl.multiple_of(c * ch, ch), ch)
            b, bmid, blast = _gla_decays(la_ref[sl, :], tri)
            q = q_ref[sl, :].astype(F32) * scale
            k = k_ref[sl, :].astype(F32)
            v = v_ref[sl, :]
            qgt = (q * jnp.exp(b)).astype(BF16)
            qgn = (q * jnp.exp(b - bmid)).astype(BF16)
            kgn = (k * jnp.exp(bmid - b)).astype(BF16)
            kd = (k * jnp.exp(blast - b)).astype(BF16)
            dec = jnp.exp(blast)
            sts = [state[h] for h in range(GLA_H)]
            outs, news = [], []
            for h in range(GLA_H):
                hk, hv = slice(h * GLA_DK, (h + 1) * GLA_DK), slice(h * GLA_DV, (h + 1) * GLA_DV)
                a = jnp.where(causal, _dg(qgn[:, hk], kgn[:, hk], 1, 1), 0.0)
                outs.append(_dg(a.astype(BF16), v[:, hv], 1, 0) + _dg(qgt[:, hk], sts[h].astype(BF16), 1, 1))
                news.append(dec[:, hk] * sts[h] + _dg(v[:, hv], kd[:, hk], 0, 0))
            for h in range(GLA_H):
                st_ref[h, c] = sts[h]
                state[h] = news[h]
            o_ref[sl, :] = jnp.concatenate(outs, axis=1)
            return carry

        lax.fori_loop(0, nc, chunk, 0)

    hw = GLA_H * GLA_DK
    res = pl.pallas_call(
        body, name="gla_fwd", grid=(nb,),
        in_specs=[pl.BlockSpec((tb, hw), lambda t: (t, P_GQ // hw)),
                  pl.BlockSpec((tb, hw), lambda t: (t, P_GK // hw)),
                  pl.BlockSpec((tb, GLA_H * GLA_DV), lambda t: (t, P_GV // (GLA_H * GLA_DV))),
                  pl.BlockSpec((tb, hw), lambda t: (t, 0))] + [HBM] * len(c_ins),
        out_specs=[pl.BlockSpec((tb, GLA_H * GLA_DV), lambda t: (t, 0)),
                   pl.BlockSpec((GLA_H, nc, GLA_DV, GLA_DK), lambda t: (0, t, 0, 0))] + [HBM] * len(c_outs),
        out_shape=[jax.ShapeDtypeStruct((s, GLA_H * GLA_DV), F32),
                   jax.ShapeDtypeStruct((GLA_H, s // ch, GLA_DV, GLA_DK), F32)] + c_outs,
        scratch_shapes=[pltpu.VMEM((GLA_H, GLA_DV, GLA_DK), F32)] + c_scratch,
        input_output_aliases=c_alias,
        compiler_params=_params(("arbitrary",)),
    )(p, p, p, la, *c_ins)
    return res[0], res[1], _split_units(comm, res[2:])


def _gla_bwd(p, la, states, do, s, dp, comm=()):
    tb, ch = GLA_BLOCK, GLA_CHUNK
    nb, nc = s // tb, tb // ch
    scale = GLA_DK ** -0.5
    c_ins, c_outs, c_alias, c_scratch = _carry(comm, 7, 4)

    def body(q_ref, k_ref, v_ref, la_ref, st_ref, do_ref, dp_in, *rest):
        ci, outs = rest[:len(c_ins)], rest[len(c_ins):len(c_ins) + 4]
        co, dstate = rest[len(c_ins) + 4:len(c_ins) + 4 + len(c_outs)], rest[len(c_ins) + 4 + len(c_outs)]
        step = pl.program_id(0)
        if comm:
            @pl.when(step == 0)
            def _():
                _comm_phase(comm, ci, co, rest[-2], rest[-1], True)

        _gla_bwd_step(q_ref, k_ref, v_ref, la_ref, st_ref, do_ref, *outs, dstate)
        if comm:
            @pl.when(step == nb - 1)
            def _():
                _comm_phase(comm, ci, co, rest[-2], rest[-1], False)

    def _gla_bwd_step(q_ref, k_ref, v_ref, la_ref, st_ref, do_ref, dq_ref, dk_ref, dv_ref, dla_ref, dstate):
        @pl.when(pl.program_id(0) == 0)
        def _():
            dstate[...] = jnp.zeros_like(dstate)

        ri = lax.broadcasted_iota(jnp.int32, (ch, ch), 0)
        ci = lax.broadcasted_iota(jnp.int32, (ch, ch), 1)
        causal = ci <= ri
        tri = causal.astype(F32)
        tri_t = (ci >= ri).astype(F32)

        def chunk(cc, carry):
            c = nc - 1 - cc
            sl = pl.ds(pl.multiple_of(c * ch, ch), ch)
            b, bmid, blast = _gla_decays(la_ref[sl, :], tri)
            q = q_ref[sl, :].astype(F32) * scale
            k = k_ref[sl, :].astype(F32)
            v = v_ref[sl, :]
            e_b, e_qn, e_kn, e_kd = jnp.exp(b), jnp.exp(b - bmid), jnp.exp(bmid - b), jnp.exp(blast - b)
            dec = jnp.exp(blast)
            qgt, qgn, kgn, kd = q * e_b, q * e_qn, k * e_kn, k * e_kd
            qgt_b, qgn_b, kgn_b, kd_b = qgt.astype(BF16), qgn.astype(BF16), kgn.astype(BF16), kd.astype(BF16)
            do_b = do_ref[sl, :].astype(BF16)
            st0s = [st_ref[h, c] for h in range(GLA_H)]
            dsts = [dstate[h] for h in range(GLA_H)]
            dqgn, dqgt, dkgn, dkd, dvs, ddec, news = [], [], [], [], [], [], []
            for h in range(GLA_H):
                hk, hv = slice(h * GLA_DK, (h + 1) * GLA_DK), slice(h * GLA_DV, (h + 1) * GLA_DV)
                dst_b = dsts[h].astype(BF16)
                a = jnp.where(causal, _dg(qgn_b[:, hk], kgn_b[:, hk], 1, 1), 0.0).astype(BF16)
                da = jnp.where(causal, _dg(do_b[:, hv], v[:, hv], 1, 1), 0.0).astype(BF16)
                dqgn.append(_dg(da, kgn_b[:, hk], 1, 0))
                dqgt.append(_dg(do_b[:, hv], st0s[h].astype(BF16), 1, 0))
                dkgn.append(_dg(da, qgn_b[:, hk], 0, 0))
                dvs.append(_dg(a, do_b[:, hv], 0, 0) + _dg(kd_b[:, hk], dst_b, 1, 1))
                dkd.append(_dg(v[:, hv], dst_b, 1, 0))
                ddec.append(jnp.sum(st0s[h] * dsts[h], axis=0, keepdims=True))
                news.append(dec[:, hk] * dsts[h] + _dg(do_b[:, hv], qgt_b[:, hk], 0, 0))
            for h in range(GLA_H):
                dstate[h] = news[h]
            cat = lambda parts: jnp.concatenate(parts, axis=1)
            dqgn, dqgt, dkgn, dkd, ddec = cat(dqgn), cat(dqgt), cat(dkgn), cat(dkd), cat(ddec)
            dq_ref[sl, :] = (scale * (dqgn * e_qn + dqgt * e_b)).astype(dq_ref.dtype)
            dk_ref[sl, :] = (dkgn * e_kn + dkd * e_kd).astype(dk_ref.dtype)
            dv_ref[sl, :] = cat(dvs).astype(dv_ref.dtype)
            db = dqgn * qgn + dqgt * qgt - dkgn * kgn - dkd * kd
            extra = jnp.sum(dkd * kd, axis=0, keepdims=True) + ddec * dec
            dla_ref[sl, :] = _tri_dot(tri_t, db) + extra
            return carry

        lax.fori_loop(0, nc, chunk, 0)

    rev = lambda t: nb - 1 - t
    hw, vw = GLA_H * GLA_DK, GLA_H * GLA_DV
    res = pl.pallas_call(
        body, name="gla_bwd", grid=(nb,),
        in_specs=[pl.BlockSpec((tb, hw), lambda t: (rev(t), P_GQ // hw)),
                  pl.BlockSpec((tb, hw), lambda t: (rev(t), P_GK // hw)),
                  pl.BlockSpec((tb, vw), lambda t: (rev(t), P_GV // vw)),
                  pl.BlockSpec((tb, hw), lambda t: (rev(t), 0)),
                  pl.BlockSpec((GLA_H, nc, GLA_DV, GLA_DK), lambda t: (0, rev(t), 0, 0)),
                  pl.BlockSpec((tb, vw), lambda t: (rev(t), 0)), pl.BlockSpec(memory_space=pl.ANY)] + [HBM] * len(c_ins),
        out_specs=[pl.BlockSpec((tb, hw), lambda t: (rev(t), 0)),
                   pl.BlockSpec((tb, hw), lambda t: (rev(t), 0)),
                   pl.BlockSpec((tb, vw), lambda t: (rev(t), P_GV // vw)),
                   pl.BlockSpec((tb, hw), lambda t: (rev(t), 0))] + [HBM] * len(c_outs),
        out_shape=[jax.ShapeDtypeStruct((s, hw), BF16),
                   jax.ShapeDtypeStruct((s, hw), BF16),
                   jax.ShapeDtypeStruct((s, P_W), BF16),
                   jax.ShapeDtypeStruct((s, hw), F32)] + c_outs,
        scratch_shapes=[pltpu.VMEM((GLA_H, GLA_DV, GLA_DK), F32)] + c_scratch,
        input_output_aliases={6: 2, **c_alias},
        compiler_params=_params(("arbitrary",)),
    )(p, p, p, la, states, do, dp, *c_ins)
    return res[0], res[1], res[2], res[3], _split_units(comm, res[4:])


def _head_masks():
    lane = lax.broadcasted_iota(jnp.int32, (1, 4 * ATT_HD), 1)
    return [(lane >= h * ATT_HD) & (lane < (h + 1) * ATT_HD) for h in range(4)]


def _attn_fwd(qv, kv, pv, g, r, s):
    ln = s // r
    nblk = ln // ATT_BLK
    qcol = lambda pr: pr
    vcol = qcol
    prev = lambda n: jnp.maximum(n - 1, 0)

    def body(q_ref, kp_ref, kc_ref, vp_ref, vc_ref, o_ref, lse_ref):
        has_prev = pl.program_id(1) > 0
        ri = lax.broadcasted_iota(jnp.int32, (ATT_BLK, ATT_BLK), 0)
        ci = lax.broadcasted_iota(jnp.int32, (ATT_BLK, ATT_BLK), 1)
        m_cur = ci <= ri
        m_prev = (ci >= ri) & has_prev
        q, kp, kc, vp, vc = q_ref[...], kp_ref[...], kc_ref[...], vp_ref[...], vc_ref[...]
        o = jnp.zeros((ATT_BLK, 256), F32)
        lse = jnp.zeros((ATT_BLK, 256), F32)
        for hm in _head_masks():
            qm = jnp.where(hm, q, jnp.zeros_like(q))
            sc = jnp.where(m_cur, _dg(qm, kc, 1, 1) * 0.125, NEG)
            sp = jnp.where(m_prev, _dg(qm, kp, 1, 1) * 0.125, NEG)
            mx = jnp.maximum(jnp.max(sc, axis=1, keepdims=True), jnp.max(sp, axis=1, keepdims=True))
            pc, pp = jnp.exp(sc - mx), jnp.exp(sp - mx)
            den = jnp.sum(pc, axis=1, keepdims=True) + jnp.sum(pp, axis=1, keepdims=True)
            oh = (_dg(pc.astype(BF16), vc, 1, 0) + _dg(pp.astype(BF16), vp, 1, 0)) / den
            o = jnp.where(hm, oh, o)
            lse = jnp.where(hm, mx + jnp.log(den), lse)
        o_ref[...] = o.astype(o_ref.dtype)
        lse_ref[...] = lse

    blk = (ATT_BLK, 256)
    o, lse = pl.pallas_call(
        body, name=f"attn_fwd_{g}", grid=(r, nblk),
        in_specs=[pl.BlockSpec(blk, lambda pr, n: (n, qcol(pr))),
                  pl.BlockSpec(blk, lambda pr, n: (prev(n), qcol(pr))),
                  pl.BlockSpec(blk, lambda pr, n: (n, qcol(pr))),
                  pl.BlockSpec(blk, lambda pr, n: (prev(n), vcol(pr))),
                  pl.BlockSpec(blk, lambda pr, n: (n, vcol(pr)))],
        out_specs=[pl.BlockSpec(blk, lambda pr, n: (n, pr)), pl.BlockSpec(blk, lambda pr, n: (n, pr))],
        out_shape=[jax.ShapeDtypeStruct((ln, r * 256), BF16), jax.ShapeDtypeStruct((ln, r * 256), F32)],
        compiler_params=_params(("parallel", "parallel")),
    )(qv, kv, kv, pv, pv)
    return o, lse


def _attn_bwd(qv, kv, pv, dov, ov, lv, g, r, s):
    ln = s // r
    nblk = ln // ATT_BLK
    qcol = lambda pr: pr
    vcol = qcol
    prev = lambda n: jnp.maximum(n - 1, 0)
    nxt = lambda n: jnp.minimum(n + 1, nblk - 1)

    def body(qc_ref, qn_ref, kp_ref, kc_ref, vp_ref, vc_ref, doc_ref, don_ref, oc_ref, on_ref, lc_ref, ln_ref,
             dq_ref, dk_ref, dv_ref):
        n = pl.program_id(1)
        has_prev, has_next = n > 0, n < nblk - 1
        ri = lax.broadcasted_iota(jnp.int32, (ATT_BLK, ATT_BLK), 0)
        ci = lax.broadcasted_iota(jnp.int32, (ATT_BLK, ATT_BLK), 1)
        m_cur = ci <= ri
        m_prev = (ci >= ri) & has_prev
        m_next = (ci >= ri) & has_next
        qc, qn, kp, kc, vp, vc = qc_ref[...], qn_ref[...], kp_ref[...], kc_ref[...], vp_ref[...], vc_ref[...]
        doc, don = doc_ref[...], don_ref[...]
        pc_full = doc.astype(F32) * oc_ref[...].astype(F32)
        pn_full = don.astype(F32) * on_ref[...].astype(F32)
        lc, lnx = lc_ref[...], ln_ref[...]
        dq = jnp.zeros((ATT_BLK, 256), F32)
        dk = jnp.zeros((ATT_BLK, 256), F32)
        dv = jnp.zeros((ATT_BLK, 256), F32)
        zb = jnp.zeros_like(qc)
        for hm in _head_masks():
            qcm, qnm = jnp.where(hm, qc, zb), jnp.where(hm, qn, zb)
            docm, donm = jnp.where(hm, doc, zb), jnp.where(hm, don, zb)
            lse_c = jnp.max(jnp.where(hm, lc, NEG), axis=1, keepdims=True)
            lse_n = jnp.max(jnp.where(hm, lnx, NEG), axis=1, keepdims=True)
            del_c = jnp.sum(jnp.where(hm, pc_full, 0.0), axis=1, keepdims=True)
            del_n = jnp.sum(jnp.where(hm, pn_full, 0.0), axis=1, keepdims=True)
            pr_ = jnp.where(m_cur, jnp.exp(_dg(qcm, kc, 1, 1) * 0.125 - lse_c), 0.0)
            ds = (pr_ * (_dg(docm, vc, 1, 1) - del_c) * 0.125).astype(BF16)
            dqh = _dg(ds, kc, 1, 0)
            dkh = _dg(ds, qc, 0, 0)
            dvh = _dg(pr_.astype(BF16), doc, 0, 0)
            pr_ = jnp.where(m_prev, jnp.exp(_dg(qcm, kp, 1, 1) * 0.125 - lse_c), 0.0)
            ds = (pr_ * (_dg(docm, vp, 1, 1) - del_c) * 0.125).astype(BF16)
            dqh = dqh + _dg(ds, kp, 1, 0)
            pr_ = jnp.where(m_next, jnp.exp(_dg(qnm, kc, 1, 1) * 0.125 - lse_n), 0.0)
            ds = (pr_ * (_dg(donm, vc, 1, 1) - del_n) * 0.125).astype(BF16)
            dkh = dkh + _dg(ds, qn, 0, 0)
            dvh = dvh + _dg(pr_.astype(BF16), don, 0, 0)
            dq = jnp.where(hm, dqh, dq)
            dk = jnp.where(hm, dkh, dk)
            dv = jnp.where(hm, dvh, dv)
        dq_ref[...] = dq.astype(dq_ref.dtype)
        dk_ref[...] = dk.astype(dk_ref.dtype)
        dv_ref[...] = dv.astype(dv_ref.dtype)

    blk = (ATT_BLK, 256)
    cur = lambda col: pl.BlockSpec(blk, lambda pr, n: (n, col(pr)))
    prv = lambda col: pl.BlockSpec(blk, lambda pr, n: (prev(n), col(pr)))
    nx = lambda col: pl.BlockSpec(blk, lambda pr, n: (nxt(n), col(pr)))
    own = lambda pr: pr
    outs = pl.pallas_call(
        body, name=f"attn_bwd_{g}", grid=(r, nblk),
        in_specs=[cur(qcol), nx(qcol), prv(qcol), cur(qcol), prv(vcol), cur(vcol),
                  cur(own), nx(own), cur(own), nx(own), cur(own), nx(own)],
        out_specs=[cur(own), cur(own), cur(own)],
        out_shape=[jax.ShapeDtypeStruct((ln, r * 256), BF16)] * 3,
        compiler_params=_params(("parallel", "parallel")),
    )(qv, qv, kv, kv, pv, pv, dov, dov, ov, ov, lv, lv)
    return outs


def _gelu_parts(gv):
    cdf = 0.5 * (1.0 + lax.erf(gv * (2.0 ** -0.5)))
    pdf = jnp.exp(-0.5 * gv * gv) * (1.0 / math.sqrt(2.0 * math.pi))
    return cdf, pdf


def _pick_row(t, k):
    row = lax.broadcasted_iota(jnp.int32, t.shape, 0)
    return jnp.sum(jnp.where(row == k, t, 0.0), axis=0, keepdims=True)


def _shift_rows(u, halo, n):
    row = lax.broadcasted_iota(jnp.int32, u.shape, 0)
    out = pltpu.roll(u, n, 0)
    for k in range(n):
        out = jnp.where(row == k, _pick_row(halo, 16 - n + k), out)
    return out


def _shift_rows_up(u, halo, n):
    rb = u.shape[0]
    row = lax.broadcasted_iota(jnp.int32, u.shape, 0)
    out = pltpu.roll(u, rb - n, 0)
    for k in range(n):
        out = jnp.where(row == rb - n + k, _pick_row(halo, k), out)
    return out


def _conv(u, halo, cw, cb):
    return cb + _pick_row(cw, 0) * _shift_rows(u, halo, 2) + _pick_row(cw, 1) * _shift_rows(u, halo, 1) + _pick_row(cw, 2) * u


def _local_step(x, mod, pos_col, target, sm, w_sh, g0, chip, core):
    s = x.shape[0]
    shift1, scale1, gate1, shift2, scale2, gate2 = [mod[i:i + 1, :] for i in range(6)]
    rb = 512
    chip1 = chip.reshape(1)

    def f_norm1(c, i, xv, nw, sc, sh):
        return ((xv * _rms(xv) * nw) * (1.0 + sc) + sh,)

    (h,) = _rowcall(f_norm1, [_rows(x, rb), _full(sm["n1w"]), _full(scale1), _full(shift1)],
                    [_orow(s, D, BF16, rb)], n_rows=s, rb=rb, name="norm1")
    own = lambda got, i: lax.dynamic_update_slice(got, w_sh[i], (chip, 0, 0))
    invf = jnp.tile(ROPE_THETA ** (-jnp.arange(ATT_HD // 2, dtype=F32) / (ATT_HD // 2)), 4).reshape(1, 128)
    cos_t, sin_t = _rope_tables(pos_col, invf, s)
    _, got0 = _unit_wait(*g0[:4], after=[h, cos_t, sin_t], name="gather_w_in_wait")
    [got0] = _comm_call("gather_w_in_d2d", [_u_gather_d2d(got0, (0,))])
    u_g1 = _u_gather_ici(w_sh, (1, 2, 3, 4, 5))
    g1 = _unit_start(u_g1, "gather_weights_start", after=got0)
    w = dict(win=_win_assemble(own(got0[0], 0), after=g1[3:]))
    p = _mm(h, w["win"], "in_proj", tm=2048, tn=1536)

    def f_gla_pre(c, i, glr, w2, gb):
        z = _dg(glr, w2.astype(BF16), 1, 0) + gb
        return ((jnp.minimum(z, 0.0) - jnp.log(1.0 + jnp.exp(-jnp.abs(z)))) * (1.0 / GLA_TAU),)

    (la,) = _rowcall(f_gla_pre, [_rows(p, rb, 128, P_LR // 128), _full(sm["w2"]), _full(sm["gb"])],
                     [_orow(s, 512, F32, rb)], n_rows=s, rb=rb, name="gla_pre")
    o_gla, states, _ = _gla_fwd(p, la, s)
    _, got = _unit_wait(u_g1, *g1[:3], after=[o_gla], name="gather_weights_wait")
    [got123] = _comm_call("gather_weights_d2d", [_u_gather_d2d(got[:3], (1, 2, 3))])
    got45 = got[3:]
    w.update(wgb=own(got123[0], 1).reshape(1024, D), wab=_cols_join(own(got123[1], 2)), wout=own(got123[2], 3).reshape(D, D))

    def f_gla_post(c, i, ov, gnw, gr):
        on = jnp.concatenate([ov[:, k * 256:(k + 1) * 256] * _rms(ov[:, k * 256:(k + 1) * 256]) * gnw
                              for k in range(GLA_H)], axis=1)
        g = gr.astype(F32)
        return (on * (g * _sigmoid(g)),)

    (og,) = _rowcall(f_gla_post, [_rows(o_gla, rb), _full(sm["gnw"]), _rows(p, rb, 1024, P_GR // 1024)],
                     [_orow(s, 1024, BF16, rb)], n_rows=s, rb=rb, name="gla_post")
    y_gla = _mm(og, w["wgb"], "gla_branch")

    q_d, k_d, v_d = _rope_fwd(p, cos_t, sin_t, s)
    att = [_attn_fwd(q_d[g], k_d[g], v_d[g], g, r, s) for g, r in enumerate(_RS)]
    o_att, lse, o_d1, o_d2, lse_d1, lse_d2 = _attn_combine(att, s)
    y_att = _mm(o_att, w["wab"], "attn_branch")

    def f_merge(c, i, ma, mb, yg, ya):
        return (_sigmoid(ma.astype(F32)) * yg.astype(F32) + _sigmoid(mb.astype(F32)) * ya.astype(F32),)

    (mixed,) = _rowcall(f_merge, [_rows(p, rb, D, P_MA // D), _rows(p, rb, D, P_MB // D), _rows(y_gla, rb), _rows(y_att, rb)],
                        [_orow(s, D, BF16, rb)], n_rows=s, rb=rb, name="merge")
    z1, [got45] = _mm(mixed, w["wout"], "out_proj", comm=[_u_gather_d2d(got45, (4, 5))])
    w.update(wup=own(got45[0], 4), wdown=own(got45[1], 5).reshape(D_FF, D))

    def f_norm2(c, i, xv, z, g1, nw, sc, sh):
        x1 = xv + g1 * z.astype(F32)
        return (x1, (x1 * _rms(x1) * nw) * (1.0 + sc) + sh)

    x1, h2 = _rowcall(f_norm2, [_rows(x, rb), _rows(z1, rb), _full(gate1), _full(sm["n2w"]), _full(scale2), _full(shift2)],
                      [_orow(s, D, F32, rb), _orow(s, D, BF16, rb)], n_rows=s, rb=rb, name="norm2")
    u = _mm(h2, w["wup"], "up_proj", tm=2048, b_shards=True)

    cwid = 2 * W_UP_SH

    def f_ffn(c, i, uv, hl, cw, cb):
        uc = _conv(uv.astype(F32), hl.astype(F32) * (i > 0).astype(F32), cw, cb)
        val, gt = uc[:, :W_UP_SH], uc[:, W_UP_SH:]
        cdf, _ = _gelu_parts(gt)
        return (gt * cdf * val,)

    ccol = lambda c: c
    rw = 256
    (hidden,) = _rowcall(f_ffn, [_rows(u, rw, cwid, ccol), _halo(u, rw, 16, cwid, ccol, True),
                                 _full(sm["cw"], cwid, ccol), _full(sm["cb"], cwid, ccol)],
                         [_orow(s, D_FF, BF16, rw, W_UP_SH, ccol)], n_rows=s, rb=rw, name="conv_geglu", ncol=2)
    z2 = _mm(hidden, w["wdown"], "down_proj", tk=D_FF)

    def f_final(c, i, x1v, z, g2, fw, tgt):
        x2 = x1v + g2 * z.astype(F32)
        r = _rms(x2)
        xh = x2 * r
        e = xh * fw - tgt
        loss = 0.5 * jnp.sum(jnp.mean(e * e, axis=-1, keepdims=True), axis=0, keepdims=True)
        dy = e * (1.0 / D)
        dxh = dy * fw
        dx2 = r * (dxh - xh * jnp.mean(dxh * xh, axis=-1, keepdims=True))
        return (loss, dx2, dx2 * g2, _csum(dy * xh), _csum(dx2 * z.astype(F32)))

    loss, dx2, dz2, d_fnw, d_gate2 = _rowcall(
        f_final, [_rows(x1, rb), _rows(z2, rb), _full(gate2), _full(sm["fnw"]), _rows(target, rb)],
        [_oacc(1, 1), _orow(s, D, F32, rb), _orow(s, D, BF16, rb), _oacc(1, D), _oacc(1, D)],
        n_rows=s, rb=rb, name="final_loss")
    d_hidden = _mm(dz2, w["wdown"], "down_proj_dx", tb=True, tn=1408)
    g_wdown = _mm(hidden, dz2, "down_proj_dw", ta=True, out_dtype=F32, tm=1408, tn=1024, tk=2048)

    def f_ffn_bwd(c, i, uv, hl, dh, cw, cb):
        uf = uv.astype(F32)
        hf = hl.astype(F32) * (i > 0).astype(F32)
        u1, u2 = _shift_rows(uf, hf, 1), _shift_rows(uf, hf, 2)
        uc = cb + _pick_row(cw, 0) * u2 + _pick_row(cw, 1) * u1 + _pick_row(cw, 2) * uf
        val, gt = uc[:, :W_UP_SH], uc[:, W_UP_SH:]
        cdf, pdf = _gelu_parts(gt)
        dhf = dh.astype(F32)
        duc = jnp.concatenate([dhf * (gt * cdf), dhf * val * (cdf + gt * pdf)], axis=1)
        dcw = jnp.concatenate([_csum(duc * u2), _csum(duc * u1), _csum(duc * uf)], axis=0)
        return (duc, _csum(duc), dcw)

    duc, d_cb, d_cw = _rowcall(
        f_ffn_bwd, [_rows(u, rw, cwid, ccol), _halo(u, rw, 16, cwid, ccol, True), _rows(d_hidden, rw, W_UP_SH, ccol),
                    _full(sm["cw"], cwid, ccol), _full(sm["cb"], cwid, ccol)],
        [_orow(s, 2 * D_FF, BF16, rw, cwid, ccol), _oacc(1, 2 * D_FF, cwid, ccol), _oacc(3, 2 * D_FF, cwid, ccol)],
        n_rows=s, rb=rw, name="conv_geglu_bwd", ncol=2)

    def f_conv_t(c, i, dv, hl, cw):
        df = dv.astype(F32)
        hf = hl.astype(F32) * (i < s // rw - 1).astype(F32)
        return (_pick_row(cw, 2) * df + _pick_row(cw, 1) * _shift_rows_up(df, hf, 1) + _pick_row(cw, 0) * _shift_rows_up(df, hf, 2),)

    (du,) = _rowcall(f_conv_t, [_rows(duc, rw, cwid, ccol), _halo(duc, rw, 16, cwid, ccol, False), _full(sm["cw"], cwid, ccol)],
                     [_orow(s, 2 * D_FF, BF16, rw, cwid, ccol)], n_rows=s, rb=rw, name="conv_transpose", ncol=2)
    g_wup = _mm(h2, du, "up_proj_dw", ta=True, out_dtype=F32, tm=1024, tk=2048, o_shards=True)
    gs45 = [g_wup, g_wdown.reshape(4, W_DOWN_SH, 1024)]
    d_h2, [land45] = _mm(du, w["wup"], "up_proj_dx", tb=True, tm=2048, b_shards=True, comm=[_u_pair_send(gs45, (4, 5))])
    ts45 = [_pair_add(g, ld, core, "grad_pair_add_" + BIG[i]) for g, ld, i in zip(gs45, land45, (4, 5))]
    u_ex4 = _u_chip_exchange(ts45[:1])
    ex4 = _unit_start(u_ex4, "grad_exchange_w_up_start")

    def f_norm2_bwd(c, i, x1v, dh, dxr, z, nw, sc, g1):
        dxn, dsh, dsc, dnw = _norm_bwd(x1v, dh.astype(F32), nw, sc)
        dx1 = dxr + dxn
        return (dx1, dx1 * g1, dsh, dsc, dnw, _csum(dx1 * z.astype(F32)))

    dx1, dz1, d_shift2, d_scale2, d_n2w, d_gate1 = _rowcall(
        f_norm2_bwd, [_rows(x1, rb), _rows(d_h2, rb), _rows(dx2, rb), _rows(z1, rb), _full(sm["n2w"]), _full(scale2), _full(gate1)],
        [_orow(s, D, F32, rb), _orow(s, D, BF16, rb), _oacc(1, D), _oacc(1, D), _oacc(1, D), _oacc(1, D)],
        n_rows=s, rb=rb, name="norm2_bwd", after=ex4[3:])
    d_mixed = _mm(dz1, w["wout"], "out_proj_dx", tb=True)
    g_wout = _mm(mixed, dz1, "out_proj_dw", ta=True, out_dtype=F32, tk=2048)

    def f_merge_bwd(c, i, dm, ma, mb, yg, ya):
        dmf, ygf, yaf = dm.astype(F32), yg.astype(F32), ya.astype(F32)
        sa, sb = _sigmoid(ma.astype(F32)), _sigmoid(mb.astype(F32))
        return (dmf * sa, dmf * sb, jnp.concatenate([dmf * ygf * sa * (1.0 - sa), dmf * yaf * sb * (1.0 - sb)], axis=1))

    dy_gla, dy_att, dp = _rowcall(
        f_merge_bwd, [_rows(d_mixed, rb), _rows(p, rb, D, P_MA // D), _rows(p, rb, D, P_MB // D), _rows(y_gla, rb), _rows(y_att, rb)],
        [_orow(s, D, BF16, rb)] * 2 + [_orow(s, P_W, BF16, rb, 2 * D, lambda c: P_MA // (2 * D))], n_rows=s, rb=rb, name="merge_bwd")
    d_og = _mm(dy_gla, w["wgb"], "gla_branch_dx", tb=True)
    g_wgb = _mm(og, dy_gla, "gla_branch_dw", ta=True, out_dtype=F32, tk=2048)
    d_oatt = _mm(dy_att, w["wab"], "attn_branch_dx", tb=True)
    g_wab = _mm(o_att, dy_att, "attn_branch_dw", ta=True, out_dtype=F32, tk=2048)

    def f_gla_post_bwd(c, i, ov, gnw, gr, dog):
        g = gr.astype(F32)
        sg = _sigmoid(g)
        silu = g * sg
        dof = dog.astype(F32)
        don = dof * silu
        on_parts, do_parts, dgn = [], [], jnp.zeros((1, 256), F32)
        for k in range(GLA_H):
            oh = ov[:, k * 256:(k + 1) * 256]
            dh = don[:, k * 256:(k + 1) * 256]
            r = _rms(oh)
            xh = oh * r
            dgn = dgn + _csum(dh * xh)
            dxh = dh * gnw
            do_parts.append(r * (dxh - xh * jnp.mean(dxh * xh, axis=-1, keepdims=True)))
            on_parts.append(xh * gnw)
        on = jnp.concatenate(on_parts, axis=1)
        dgr = dof * on * (sg * (1.0 + g * (1.0 - sg)))
        return (jnp.concatenate(do_parts, axis=1), dgr, dgn)

    do_gla, dp, d_gnw = _rowcall(
        f_gla_post_bwd, [_rows(o_gla, rb), _full(sm["gnw"]), _rows(p, rb, 1024, P_GR // 1024), _rows(d_og, rb)],
        [_orow(s, 1024, F32, rb), _orow(s, P_W, BF16, rb, 1024, lambda c: P_GR // 1024), _oacc(1, 256)],
        n_rows=s, rb=rb, name="gla_post_bwd", into=(dp, 1))
    gs123 = [g_wgb.reshape(4, 256, 1024), _cols_split(g_wab), g_wout.reshape(4, 256, 1024)]
    d_gq, d_gk, dp, d_la, [land123] = _gla_bwd(p, la, states, do_gla, s, dp, comm=[_u_pair_send(gs123, (1, 2, 3))])
    ts123 = [_pair_add(g, ld, core, "grad_pair_add_" + BIG[i]) for g, ld, i in zip(gs123, land123, (1, 2, 3))]

    def f_gla_pre_bwd(c, i, lav, dlav, glr, w2):
        dz = dlav * (1.0 / GLA_TAU) * (1.0 - jnp.exp(GLA_TAU * lav))
        dzb = dz.astype(BF16)
        return (_dg(dzb, w2.astype(BF16), 1, 1), _csum(dz), _dg(glr, dzb, 0, 0))

    d_glr, d_gb, d_w2 = _rowcall(
        f_gla_pre_bwd, [_rows(la, rb), _rows(d_la, rb), _rows(p, rb, 128, P_LR // 128), _full(sm["w2"])],
        [_orow(s, 128, BF16, rb), _oacc(1, 512), _oacc(128, 512)], n_rows=s, rb=rb, name="gla_pre_bwd")

    do_d = [d_oatt] + list(_dilate(d_oatt, s))
    datt = [_attn_bwd(q_d[g], k_d[g], v_d[g], do_d[g], (o_att, o_d1, o_d2)[g], (lse, lse_d1, lse_d2)[g], g, r, s)
            for g, r in enumerate(_RS)]
    dp = _rope_bwd(datt, d_glr, dp, cos_t, sin_t, s)
    dp = lax.dynamic_update_slice(dp, jnp.concatenate([d_gq, d_gk], axis=1), (0, P_GQ))
    [t4], [r4] = _unit_wait(u_ex4, *ex4[:3], after=[dp], name="grad_exchange_w_up_wait")
    half4 = [_chip_sum(t4, r4, chip1, "grad_chip_sum_w_up")]
    g_win, [r1235, oth4] = _mm(h, dp, "in_proj_dw", ta=True, out_dtype=F32, tm=1024, tn=1536, tk=2048,
                               comm=[_u_chip_exchange(ts123 + ts45[1:]), _u_pair_join(half4)])
    half1235 = [_chip_sum(t, r, chip1, "grad_chip_sum_" + BIG[i]) for t, r, i in zip(ts123 + ts45[1:], r1235, (1, 2, 3, 5))]
    gs0 = [_win_split(g_win)]
    d_h, [land0, oth1235] = _mm(dp, w["win"], "in_proj_dx", tb=True, tk=3840,
                                comm=[_u_pair_send(gs0, (0,)), _u_pair_join(half1235)])
    half123, half45 = half1235[:3], half4 + half1235[3:]
    oth123, oth45 = oth1235[:3], oth4 + oth1235[3:]
    ts0 = _pair_add(gs0[0], land0[0], core, "grad_pair_add_w_in")

    def f_norm1_bwd(c, i, xv, dh, dxr, nw, sc):
        dxn, dsh, dsc, dnw = _norm_bwd(xv, dh.astype(F32), nw, sc)
        return (dxr + dxn, dsh, dsc, dnw)

    grad_x, d_shift1, d_scale1, d_n1w = _rowcall(
        f_norm1_bwd, [_rows(x, rb), _rows(d_h, rb), _rows(dx1, rb), _full(sm["n1w"]), _full(scale1)],
        [_orow(s, D, F32, rb), _oacc(1, D), _oacc(1, D), _oacc(1, D)], n_rows=s, rb=rb, name="norm1_bwd")

    dmod = jnp.concatenate([d_shift1, d_scale1, d_gate1, d_shift2, d_scale2, d_gate2], axis=1)
    small = dict(dmod=dmod, n1w=d_n1w, gb=d_gb, gnw=d_gnw, n2w=d_n2w, cb=d_cb, fnw=d_fnw, w2=d_w2, cw=d_cw)
    return loss, grad_x, half123 + half45, oth123 + oth45, small, ts0


def _win_pieces():
    runs = [(P_GV, 1024, 2048), (P_MA, 5392, 2048), (P_GQ, 0, 1024), (P_AQ, 3088, 2304), (P_LR, 3072, GLA_LR)]
    out = []
    for kc, rc, ln in runs:
        while ln > 0:
            step = min(ln, W_IN_SH - rc % W_IN_SH)
            out.append((kc, rc, step))
            kc, rc, ln = kc + step, rc + step, ln - step
    return out


def _win_assemble(shards, after=()):
    rb = 256

    def body(s_ref, *rest):
        o_ref = rest[-1]
        o_ref[:, W_IN:] = jnp.zeros((rb, P_W - W_IN), o_ref.dtype)
        for kc, rc, ln in _win_pieces():
            o_ref[:, kc:kc + ln] = s_ref[rc // W_IN_SH, :, rc % W_IN_SH:rc % W_IN_SH + ln]

    return pl.pallas_call(
        body, name="w_in_assemble", grid=(D // rb,),
        in_specs=[pl.BlockSpec((4, rb, W_IN_SH), lambda i: (0, i, 0))] + [pl.BlockSpec(memory_space=pl.ANY)] * len(after),
        out_specs=pl.BlockSpec((rb, P_W), lambda i: (i, 0)),
        out_shape=jax.ShapeDtypeStruct((D, P_W), shards.dtype), compiler_params=_params(("parallel",)),
    )(shards, *after)


def _win_split(g):
    rb = 256

    def body(g_ref, o_ref):
        for kc, rc, ln in _win_pieces():
            o_ref[rc // W_IN_SH, :, rc % W_IN_SH:rc % W_IN_SH + ln] = g_ref[:, kc:kc + ln]

    return pl.pallas_call(
        body, name="w_in_grad_split", grid=(D // rb,),
        in_specs=[pl.BlockSpec((rb, P_W), lambda i: (i, 0))], out_specs=pl.BlockSpec((4, rb, W_IN_SH), lambda i: (0, i, 0)),
        out_shape=jax.ShapeDtypeStruct((4, D, W_IN_SH), g.dtype), compiler_params=_params(("parallel",)),
    )(g)


def _ff_to_kernel(a):
    h = W_UP_SH
    return jnp.concatenate([a[:, 0:h], a[:, D_FF:D_FF + h], a[:, h:D_FF], a[:, D_FF + h:]], axis=1)


def _ff_from_kernel(a):
    h = W_UP_SH
    return jnp.concatenate([a[:, 0:h], a[:, 2 * h:3 * h], a[:, h:2 * h], a[:, 3 * h:]], axis=1)


BIG = ("w_in", "w_gla_branch", "w_attn_branch", "w_out", "w_up", "w_down")
SH_SHAPES = ((1024, W_IN_SH), (256, 1024), (256, 256), (256, 1024), (1024, W_UP_SH), (W_DOWN_SH, 1024))
N_BIG = len(BIG)


def _cols_join(t):
    return jnp.concatenate([t[k] for k in range(4)], axis=1)


def _cols_split(t):
    cols = t.shape[1] // 4
    return jnp.stack([t[:, k * cols:(k + 1) * cols] for k in range(4)])


def _me():
    return lax.axis_index("x"), lax.axis_index("y"), lax.axis_index("c")


HBM = pl.BlockSpec(memory_space=pltpu.HBM)
VMEM_SPEC = pl.BlockSpec(memory_space=pltpu.VMEM)


def _allgather8(xs, name):
    rows = xs.shape[0]

    def body(x_ref, out_ref, send_sems, recv_sems, local_sem):
        x, y, c = _me()
        me = 4 * x + 2 * y + c
        mine = pltpu.make_async_copy(x_ref, out_ref.at[me], local_sem)
        mine.start()
        flips = [(k >> 2 & 1, k >> 1 & 1, k & 1) for k in range(1, 8)]

        def peer(f):
            return (jnp.where(f[0] == 1, 1 - x, x), jnp.where(f[1] == 1, 1 - y, y), jnp.where(f[2] == 1, 1 - c, c))

        sends = []
        for k, f in enumerate(flips):
            cp = pltpu.make_async_remote_copy(src_ref=x_ref, dst_ref=out_ref.at[me], send_sem=send_sems.at[k],
                                              recv_sem=recv_sems.at[k], device_id=peer(f), device_id_type=MESH)
            cp.start()
            sends.append(cp)
        for k, f in enumerate(flips):
            px, py, pc = peer(f)
            pltpu.make_async_remote_copy(src_ref=x_ref, dst_ref=out_ref.at[4 * px + 2 * py + pc], send_sem=send_sems.at[k],
                                         recv_sem=recv_sems.at[k], device_id=peer(f), device_id_type=MESH).wait_recv()
        for cp in sends:
            cp.wait_send()
        mine.wait()

    return pl.pallas_call(
        body, name=name, out_shape=jax.ShapeDtypeStruct((8, rows, 128), F32),
        in_specs=[VMEM_SPEC], out_specs=VMEM_SPEC,
        scratch_shapes=[pltpu.SemaphoreType.DMA((7,)), pltpu.SemaphoreType.DMA((7,)), pltpu.SemaphoreType.DMA],
        compiler_params=pltpu.CompilerParams(vmem_limit_bytes=VMEM_LIMIT),
    )(xs)


def _half_rows(i, cc, unit):
    rows = SH_SHAPES[i][0] // 2
    return pl.ds(pl.multiple_of(cc * rows, unit), rows)


def _rc(src, dst, sems, to):
    return pltpu.make_async_remote_copy(src_ref=src, dst_ref=dst, send_sem=sems[0], recv_sem=sems[1], device_id=to, device_id_type=MESH)


def _other_chips(x, y):
    return [(1 - x, y), (x, 1 - y), (1 - x, 1 - y)]


def _u_gather_ici(w_sh, idxs):
    def copies(ins, outs, sem):
        x, y, c = _me()
        res = []
        for j, (px, py) in enumerate(_other_chips(x, y)):
            for n, i in enumerate(idxs):
                src = ins[n].at[0, _half_rows(i, c, 16)]
                res.append((_rc(src, outs[n].at[2 * x + y, _half_rows(i, c, 16)], sem(j * len(idxs) + n), (px, py, c)),
                            _rc(src, outs[n].at[2 * px + py, _half_rows(i, c, 16)], sem(j * len(idxs) + n), (px, py, c))))
        return res

    return dict(ins=[w_sh[i] for i in idxs], outs=[jax.ShapeDtypeStruct((4,) + SH_SHAPES[i], BF16) for i in idxs],
                nsem=3 * len(idxs), alias={}, copies=copies)


def _u_gather_d2d(got, idxs):
    def copies(ins, outs, sem):
        x, y, c = _me()
        res = []
        for j, (px, py) in enumerate(_other_chips(x, y)):
            for n, i in enumerate(idxs):
                src = ins[n].at[2 * px + py, _half_rows(i, c, 16)]
                res.append((_rc(src, outs[n].at[2 * px + py, _half_rows(i, c, 16)], sem(j * len(idxs) + n), (x, y, 1 - c)),
                            _rc(src, outs[n].at[2 * px + py, _half_rows(i, 1 - c, 16)], sem(j * len(idxs) + n), (x, y, 1 - c))))
        return res

    return dict(ins=list(got), outs=[jax.ShapeDtypeStruct(g.shape, g.dtype) for g in got], nsem=3 * len(idxs),
                alias={n: n for n in range(len(idxs))}, copies=copies)


def _u_pair_send(gs, idxs):
    def copies(ins, outs, sem):
        x, y, c = _me()
        res = []
        for n, i in enumerate(idxs):
            for sh in range(4):
                cp = _rc(ins[n].at[sh, _half_rows(i, 1 - c, 8)], outs[n].at[sh], sem(4 * n + sh), (x, y, 1 - c))
                res.append((cp, cp))
        return res

    return dict(ins=list(gs), outs=[jax.ShapeDtypeStruct((4, SH_SHAPES[i][0] // 2, SH_SHAPES[i][1]), F32) for i in idxs],
                nsem=4 * len(idxs), alias={}, copies=copies)


def _u_chip_exchange(ts):
    def copies(ins, outs, sem):
        x, y, c = _me()
        res = []
        for j, (px, py) in enumerate(_other_chips(x, y)):
            for n in range(len(ts)):
                cp = _rc(ins[n].at[2 * px + py], outs[n].at[j], sem(j * len(ts) + n), (px, py, c))
                res.append((cp, cp))
        return res

    return dict(ins=list(ts), outs=[jax.ShapeDtypeStruct((3,) + t.shape[1:], t.dtype) for t in ts], nsem=3 * len(ts),
                alias={}, copies=copies)


def _u_pair_join(hs):
    def copies(ins, outs, sem):
        x, y, c = _me()
        res = []
        for n in range(len(hs)):
            cp = _rc(ins[n], outs[n], sem(n), (x, y, 1 - c))
            res.append((cp, cp))
        return res

    return dict(ins=list(hs), outs=[jax.ShapeDtypeStruct(h.shape, h.dtype) for h in hs], nsem=len(hs), alias={}, copies=copies)


def _comm_phase(units, ci, co, send_sems, recv_sems, start):
    ii = oo = off = 0
    for u in units:
        ni, no = len(u["ins"]), len(u["outs"])
        for st, arrival in u["copies"](ci[ii:ii + ni], co[oo:oo + no], lambda k, off=off: (send_sems.at[off + k], recv_sems.at[off + k])):
            if start:
                st.start()
            else:
                st.wait_send()
                arrival.wait_recv()
        ii, oo, off = ii + ni, oo + no, off + u["nsem"]


def _carry(units, n_in, n_out):
    ins = [a for u in units for a in u["ins"]]
    outs = [o for u in units for o in u["outs"]]
    alias, ii, oo = {}, 0, 0
    for u in units:
        for a, b in u["alias"].items():
            alias[n_in + ii + a] = n_out + oo + b
        ii, oo = ii + len(u["ins"]), oo + len(u["outs"])
    nsem = sum(u["nsem"] for u in units)
    scratch = [pltpu.SemaphoreType.DMA((nsem,)), pltpu.SemaphoreType.DMA((nsem,))] if units else []
    return ins, outs, alias, scratch


def _split_units(units, res):
    out, oo = [], 0
    for u in units:
        out.append(list(res[oo:oo + len(u["outs"])]))
        oo += len(u["outs"])
    return out


def _comm_call(name, units):
    ins, outs, alias, scratch = _carry(units, 0, 0)

    def body(*refs):
        ci, co = refs[:len(ins)], refs[len(ins):len(ins) + len(outs)]
        _comm_phase(units, ci, co, refs[-2], refs[-1], True)
        _comm_phase(units, ci, co, refs[-2], refs[-1], False)

    res = pl.pallas_call(body, name=name, out_shape=outs, in_specs=[HBM] * len(ins), out_specs=[HBM] * len(outs),
                         scratch_shapes=scratch, input_output_aliases=alias)(*ins)
    return _split_units(units, res)


SEM = pl.BlockSpec(memory_space=pltpu.SEMAPHORE)
EFFECT = pltpu.SideEffectType.DATAFLOW_SIDE_EFFECTING


def _unit_start(unit, name, after=()):
    bufs = list(unit["ins"]) + [lax.empty(o.shape, o.dtype) for o in unit["outs"]]
    n_i, n_b, ns = len(unit["ins"]), len(bufs), unit["nsem"]

    def body(*refs):
        send_sems, recv_sems = refs[n_b + len(after)], refs[n_b + len(after) + 1]
        for st, _ in unit["copies"](refs[:n_i], refs[n_i:n_b], lambda k: (send_sems.at[k], recv_sems.at[k])):
            st.start()
        refs[-1][...] = jnp.zeros_like(refs[-1])

    res = pl.pallas_call(
        body, name=name,
        out_shape=[pltpu.SemaphoreType.DMA((ns,)), pltpu.SemaphoreType.DMA((ns,))] + [pltpu.HBM(b.shape, b.dtype) for b in bufs]
        + [jax.ShapeDtypeStruct((8, 128), F32)],
        in_specs=[HBM] * n_b + [pl.BlockSpec(memory_space=pl.ANY)] * len(after), out_specs=[SEM, SEM] + [HBM] * n_b + [VMEM_SPEC],
        input_output_aliases={i: 2 + i for i in range(n_b)},
        compiler_params=pltpu.CompilerParams(has_side_effects=EFFECT),
    )(*[pltpu.with_memory_space_constraint(b, pltpu.HBM) for b in bufs], *after)
    return res[0], res[1], list(res[2:2 + n_b]), res[-1]


def _unit_wait(unit, send_sems, recv_sems, bufs, after, name):
    n_i, n_b = len(unit["ins"]), len(bufs)

    def body(*refs):
        ss, rs = refs[n_b], refs[n_b + 1]
        for st, arrival in unit["copies"](refs[:n_i], refs[n_i:n_b], lambda k: (ss.at[k], rs.at[k])):
            st.wait_send()
            arrival.wait_recv()

    res = pl.pallas_call(
        body, name=name, out_shape=[pltpu.HBM(b.shape, b.dtype) for b in bufs],
        in_specs=[HBM] * n_b + [SEM, SEM] + [pl.BlockSpec(memory_space=pl.ANY)] * len(after), out_specs=[HBM] * n_b,
        input_output_aliases={i: i for i in range(n_b)}, compiler_params=pltpu.CompilerParams(has_side_effects=EFFECT),
    )(*bufs, send_sems, recv_sems, *after)
    return list(res[:n_i]), list(res[n_i:])


def _pair_add(g, land, core, name):
    _, rows, cols = g.shape
    half = rows // 2
    rb = _tile(half, 512, 16)
    nb = half // rb

    def body(c_ref, g_ref, l_ref, o_ref):
        o_ref[...] = (g_ref[...] + l_ref[...]).astype(BF16)

    return pl.pallas_call(
        body, name=name,
        grid_spec=pltpu.PrefetchScalarGridSpec(
            num_scalar_prefetch=1, grid=(4, nb),
            in_specs=[pl.BlockSpec((1, rb, cols), lambda s, i, c_ref: (s, c_ref[0] * nb + i, 0)),
                      pl.BlockSpec((1, rb, cols), lambda s, i, c_ref: (s, i, 0))],
            out_specs=pl.BlockSpec((1, rb, cols), lambda s, i, c_ref: (s, i, 0))),
        out_shape=jax.ShapeDtypeStruct((4, half, cols), BF16),
        compiler_params=_params(("parallel", "parallel")),
    )(core, g, land)


def _chip_sum(t, r, chip, name):
    _, half, cols = t.shape
    rb = _tile(half, 512, 16)

    def body(s_ref, t_ref, r_ref, o_ref):
        o_ref[...] = ((t_ref[0].astype(F32) + r_ref[0].astype(F32)) + r_ref[1].astype(F32)) + r_ref[2].astype(F32)

    return pl.pallas_call(
        body, name=name,
        grid_spec=pltpu.PrefetchScalarGridSpec(
            num_scalar_prefetch=1, grid=(half // rb,),
            in_specs=[pl.BlockSpec((1, rb, cols), lambda i, s_ref: (s_ref[0], i, 0)),
                      pl.BlockSpec((3, rb, cols), lambda i, s_ref: (0, i, 0))],
            out_specs=pl.BlockSpec((rb, cols), lambda i, s_ref: (i, 0))),
        out_shape=jax.ShapeDtypeStruct((half, cols), F32),
        compiler_params=_params(("parallel",)),
    )(chip, t, r)


def _adam_math(wv, gv, mv, vv):
    mn = ADAM_B1 * mv + (1.0 - ADAM_B1) * gv
    vn = ADAM_B2 * vv + (1.0 - ADAM_B2) * (gv * gv)
    m_hat = mn / (1.0 - ADAM_B1 ** ADAM_STEP)
    v_hat = vn / (1.0 - ADAM_B2 ** ADAM_STEP)
    return -ADAM_LR * (m_hat / (jnp.sqrt(v_hat) + ADAM_EPS) + ADAM_WD * wv), mn, vn


def _adamw_halves(wt, mt, vt, mine, theirs, core, name):
    _, rows, cols = wt.shape
    half = rows // 2
    rb = _tile(half, 256, 8)
    nb = half // rb

    def body(c_ref, w_ref, m_ref, v_ref, a_ref, b_ref, g_ref, d_ref, mo_ref, vo_ref):
        gv = jnp.where(pl.program_id(0) == c_ref[0], a_ref[...], b_ref[...])
        dl, mn, vn = _adam_math(w_ref[...], gv, m_ref[...], v_ref[...])
        g_ref[...] = gv
        d_ref[...] = dl
        mo_ref[...] = mn
        vo_ref[...] = vn

    full = pl.BlockSpec((None, rb, cols), lambda hf, i, c_ref: (0, hf * nb + i, 0))
    part = pl.BlockSpec((rb, cols), lambda hf, i, c_ref: (i, 0))
    return pl.pallas_call(
        body, name=name,
        grid_spec=pltpu.PrefetchScalarGridSpec(num_scalar_prefetch=1, grid=(2, nb), in_specs=[full, full, full, part, part],
                                               out_specs=[full] * 4),
        out_shape=[jax.ShapeDtypeStruct((1, rows, cols), F32)] * 4,
        compiler_params=_params(("parallel", "parallel")),
    )(core, wt, mt, vt, mine, theirs)


SG_REP = 144
SG_LOSS = 136
SG_W2, SG_CW = SG_REP, SG_REP + 4 * 16
SG_ROWS = SG_CW + 4 * 40
SP_ROWS = SG_REP + 16 + 40


def _mod_shard(c_all, ada_w_sh):
    def body(c_ref, w_ref, o_ref):
        cv = c_ref[...]
        o_ref[...] = _dg((cv * _sigmoid(cv)).astype(BF16), w_ref[...].astype(BF16), 1, 0)

    return pl.pallas_call(body, name="mod_shard", out_shape=jax.ShapeDtypeStruct((8, 1536), F32),
                          in_specs=[VMEM_SPEC, VMEM_SPEC], out_specs=VMEM_SPEC,
                          compiler_params=pltpu.CompilerParams(vmem_limit_bytes=VMEM_LIMIT))(c_all, ada_w_sh)


def _mod_select(mod_all, ada_b4):
    def body(m_ref, b_ref, o_ref):
        x, y, c = _me()
        me = 4 * x + 2 * y + c
        for sh in range(4):
            o_ref[sh] = m_ref[2 * sh, me] + b_ref[sh]

    return pl.pallas_call(body, name="mod_select", out_shape=jax.ShapeDtypeStruct((4, 12, 128), F32),
                          in_specs=[VMEM_SPEC, VMEM_SPEC], out_specs=VMEM_SPEC)(mod_all, ada_b4)


def _small_reduce(sg_all):
    def body(g_ref, o_ref):
        x, y, c = _me()
        s_me = 2 * x + y
        w2_rows = pl.ds(pl.multiple_of(SG_W2 + 16 * s_me, 8), 16)
        cw_rows = pl.ds(pl.multiple_of(SG_CW + 40 * s_me, 8), 40)
        a = g_ref[0, 0:SG_REP, :]
        b = g_ref[0, w2_rows, :]
        d = g_ref[0, cw_rows, :]
        for dev in range(1, 8):
            a = a + g_ref[dev, 0:SG_REP, :]
            b = b + g_ref[dev, w2_rows, :]
            d = d + g_ref[dev, cw_rows, :]
        o_ref[0:SG_REP, :] = a
        o_ref[SG_REP:SG_REP + 16, :] = b
        o_ref[SG_REP + 16:SP_ROWS, :] = d

    return pl.pallas_call(body, name="small_grad_reduce", out_shape=jax.ShapeDtypeStruct((SP_ROWS, 128), F32),
                          in_specs=[VMEM_SPEC], out_specs=VMEM_SPEC)(sg_all)


def _ada_grad(dmod_all, c_bc):
    def body(g_ref, c_ref, o_ref):
        x, y, c = _me()
        s_me = 2 * x + y
        for k in range(12):
            acc = jnp.zeros((D, 128), F32)
            for b in range(8):
                cv = c_ref[b]
                acc = acc + (cv * _sigmoid(cv)) * g_ref[s_me, k, b:b + 1, :]
            o_ref[:, k * 128:(k + 1) * 128] = acc

    return pl.pallas_call(body, name="ada_w_grad", out_shape=jax.ShapeDtypeStruct((D, 1536), F32),
                          in_specs=[VMEM_SPEC, VMEM_SPEC], out_specs=VMEM_SPEC,
                          compiler_params=pltpu.CompilerParams(vmem_limit_bytes=VMEM_LIMIT))(dmod_all, c_bc)


def _adamw(wt, g, m, v, name):
    rows, cols = wt.shape
    rb = _tile(rows, 256, 8)

    def fn(c, i, wv, gv, mv, vv):
        return _adam_math(wv, gv, mv, vv)

    return _rowcall(fn, [_rows(t, rb) for t in (wt, g, m, v)], [_orow(rows, cols, F32, rb)] * 3,
                    n_rows=rows, rb=rb, name=name)


def _pad_rows(t, rows):
    flat = t.reshape(-1)
    return jnp.pad(flat, (0, rows * 128 - flat.shape[0])).reshape(rows, 128)


SP_LAYOUT = (("ada_b", 48), ("norm1_w", 8), ("gla_gate_b", 8), ("gla_norm_w", 8), ("norm2_w", 8), ("conv_b", 48),
             ("final_norm_w", 8), (None, 8), ("gla_gate_w2", 16), ("conv_w", 40))


def _pack_small(d):
    return jnp.concatenate([jnp.zeros((rows, 128), F32) if n is None else _pad_rows(d[n].astype(F32), rows)
                            for n, rows in SP_LAYOUT], axis=0)


def _unpack_small(pk, shapes):
    out, off = {}, 0
    for n, rows in SP_LAYOUT:
        if n is not None:
            shp = shapes[n]
            out[n] = pk[off:off + rows].reshape(-1)[:math.prod(shp)].reshape(shp)
        off += rows
    return out


def kernel(x, c, positions, ada_w, ada_b, norm1_w, w_in, gla_gate_w2, gla_gate_b, gla_norm_w, w_gla_branch, w_attn_branch, w_out, norm2_w, w_up, conv_w, conv_b, w_down, final_norm_w, loss_target, m_ada_w, m_ada_b, m_norm1_w, m_w_in, m_gla_gate_w2, m_gla_gate_b, m_gla_norm_w, m_w_gla_branch, m_w_attn_branch, m_w_out, m_norm2_w, m_w_up, m_conv_w, m_conv_b, m_w_down, m_final_norm_w, v_ada_w, v_ada_b, v_norm1_w, v_w_in, v_gla_gate_w2, v_gla_gate_b, v_gla_norm_w, v_w_gla_branch, v_w_attn_branch, v_w_out, v_norm2_w, v_w_up, v_conv_w, v_conv_b, v_w_down, v_final_norm_w):
    s = x.shape[1]
    names = ("ada_w", "ada_b", "norm1_w", "w_in", "gla_gate_w2", "gla_gate_b", "gla_norm_w", "w_gla_branch", "w_attn_branch",
             "w_out", "norm2_w", "w_up", "conv_w", "conv_b", "w_down", "final_norm_w")
    wts = dict(zip(names, (ada_w, ada_b, norm1_w, w_in, gla_gate_w2, gla_gate_b, gla_norm_w, w_gla_branch, w_attn_branch,
                           w_out, norm2_w, w_up, conv_w, conv_b, w_down, final_norm_w)))
    ms = dict(zip(names, (m_ada_w, m_ada_b, m_norm1_w, m_w_in, m_gla_gate_w2, m_gla_gate_b, m_gla_norm_w, m_w_gla_branch,
                          m_w_attn_branch, m_w_out, m_norm2_w, m_w_up, m_conv_w, m_conv_b, m_w_down, m_final_norm_w)))
    vs = dict(zip(names, (v_ada_w, v_ada_b, v_norm1_w, v_w_in, v_gla_gate_w2, v_gla_gate_b, v_gla_norm_w, v_w_gla_branch,
                          v_w_attn_branch, v_w_out, v_norm2_w, v_w_up, v_conv_w, v_conv_b, v_w_down, v_final_norm_w)))

    pk0 = jnp.concatenate([_pad_rows(c, 8), _pad_rows(gla_gate_w2, 16), _pad_rows(conv_w, 40)], axis=0)
    sm_all = _allgather8(pk0, "gather_small")
    c_all = sm_all[:, 0:8, :].reshape(8, D)
    w2_full = sm_all[0::2, 8:24, :].transpose(1, 0, 2).reshape(GLA_LR, 512)
    cw_full = sm_all[0::2, 24:64, :].reshape(4, 40 * 128)[:, :3 * W_UP_SH].reshape(4, 3, W_UP_SH).transpose(1, 0, 2).reshape(3, 2 * D_FF)

    mod_sh = _mod_shard(c_all, ada_w[0])
    mod_all = _allgather8(mod_sh.reshape(96, 128), "gather_mod")

    w_sh = [wts[n].astype(BF16) for n in BIG]
    u_g0 = _u_gather_ici(w_sh, (0,))
    g0 = (u_g0,) + _unit_start(u_g0, "gather_w_in_start", after=[mod_all])
    mod = _mod_select(mod_all.reshape(8, 8, 12, 128) + g0[4][0, 0], ada_b.reshape(4, 12, 128)).reshape(6, D)

    core = lax.axis_index("c").astype(jnp.int32).reshape(1)
    chip = (2 * lax.axis_index("x") + lax.axis_index("y")).astype(jnp.int32)
    sm = dict(n1w=norm1_w, n2w=norm2_w, fnw=final_norm_w.reshape(1, D), gnw=gla_norm_w, gb=gla_gate_b,
              w2=jnp.pad(w2_full, ((0, 128 - GLA_LR), (0, 0))), cw=_ff_to_kernel(cw_full), cb=_ff_to_kernel(conv_b))
    loss, grad_x, halves, others, small, ts0 = _local_step(x[0], mod, positions.reshape(s, 1), loss_target[0], sm, w_sh,
                                                               g0, chip, core)

    dcw = _ff_from_kernel(small["cw"]).reshape(3, 4, W_UP_SH).transpose(1, 0, 2)
    dw2 = small["w2"][:GLA_LR].reshape(GLA_LR, 4, 128).transpose(1, 0, 2)
    sg = jnp.concatenate(
        [_pad_rows(small["dmod"], 48), _pad_rows(small["n1w"], 8), _pad_rows(small["gb"], 8), _pad_rows(small["gnw"], 8),
         _pad_rows(small["n2w"], 8), _pad_rows(_ff_from_kernel(small["cb"]), 48), _pad_rows(small["fnw"], 8), _pad_rows(loss, 8)]
        + [_pad_rows(dw2[k], 16) for k in range(4)] + [_pad_rows(dcw[k], 40) for k in range(4)], axis=0)
    sg_all = _allgather8(sg, "gather_small_grads")
    u_ex = _u_chip_exchange([ts0])
    pending = (u_ex,) + _unit_start(u_ex, "grad_exchange_w_in_start", after=[sg_all])
    sg_all = sg_all + pending[4][0, 0]
    g_small_pk = _small_reduce(sg_all)
    dmod_all = sg_all[:, 0:48, :].reshape(8, 4, 12, 128).transpose(1, 2, 0, 3)
    g_ada_w = _ada_grad(dmod_all, jnp.broadcast_to(c_all[:, :, None], (8, D, 128)))

    shapes = {n: wts[n].shape for n in names}
    g_small = _unpack_small(g_small_pk, shapes)
    grads = {"ada_w": g_ada_w.reshape(1, D, 1536), **g_small}
    deltas, new_m, new_v = {}, {}, {}
    for n, mine, theirs in zip(BIG[1:], halves, others):
        grads[n], deltas[n], new_m[n], new_v[n] = _adamw_halves(wts[n], ms[n], vs[n], mine, theirs, core, "adamw_" + n)
    shp = ada_w.shape
    d_, m_, v_ = _adamw(ada_w[0], g_ada_w, m_ada_w[0], v_ada_w[0], "adamw_ada_w")
    deltas["ada_w"], new_m["ada_w"], new_v["ada_w"] = d_.reshape(shp), m_.reshape(shp), v_.reshape(shp)
    d_, m_, v_ = _adamw(_pack_small(wts), g_small_pk, _pack_small(ms), _pack_small(vs), "adamw_small")
    for dst, pk in ((deltas, d_), (new_m, m_), (new_v, v_)):
        dst.update(_unpack_small(pk, shapes))

    [t0], [r0] = _unit_wait(*pending[:4], after=[d_, deltas["ada_w"], deltas["w_up"], deltas["w_down"]], name="grad_exchange_w_in_wait")
    half0 = _chip_sum(t0, r0, chip.reshape(1), "grad_chip_sum_w_in")
    [[oth0]] = _comm_call("grad_join_w_in", [_u_pair_join([half0])])
    grads["w_in"], deltas["w_in"], new_m["w_in"], new_v["w_in"] = _adamw_halves(w_in, m_w_in, v_w_in, half0, oth0, core, "adamw_w_in")

    return (g_small_pk[SG_LOSS, 0], grad_x.reshape(1, s, D), *[grads[n] for n in names], *[deltas[n] for n in names],
            *[new_m[n] for n in names], *[new_v[n] for n in names])
```

```python
import math

import jax
import jax.numpy as jnp
from jax import lax
from jax.experimental import pallas as pl
from jax.experimental.pallas import tpu as pltpu

F32, BF16 = jnp.float32, jnp.bfloat16
MESH = pl.DeviceIdType.MESH

D = 1024
EPS = 1e-6
GLA_H, GLA_DK, GLA_DV, GLA_LR = 4, 128, 256, 16
GLA_TAU = 16.0
GLA_CHUNK = 64
GLA_BLOCK = 512
ATT_GROUPS = ((128, 1), (512, 4), (2048, 16))
ATT_BLK = 128
ATT_HD = 64
ATT_W = 768
D_FF = 2816
ROPE_THETA = 10000.0
P_W = 7680
P_GV, P_GR, P_MA, P_MB, P_GQ, P_GK, P_AQ, P_AK, P_AV, P_LR = 0, 1024, 2048, 3072, 4096, 4608, 5120, 5888, 6656, 7424
W_IN = 7440
W_IN_SH, W_UP_SH, W_DOWN_SH = 1860, 1408, 704
VMEM_LIMIT = 56 * 1024 * 1024
ADAM_LR, ADAM_B1, ADAM_B2, ADAM_EPS, ADAM_WD, ADAM_STEP = 0.001, 0.9, 0.999, 1e-08, 0.01, 10
NEG = -1e30


def _tile(n, target, unit=128):
    best = None
    for t in range(unit, min(n, target) + 1, unit):
        if n % t == 0:
            best = t
    return best or n


def _params(sem):
    return pltpu.CompilerParams(dimension_semantics=sem, vmem_limit_bytes=VMEM_LIMIT)


def _dg(a, b, ca, cb):
    return lax.dot_general(a, b, (((ca,), (cb,)), ((), ())), preferred_element_type=F32)


def _sigmoid(v):
    return 1.0 / (1.0 + jnp.exp(-v))


def _ff_block(j):
    return (j % 2) * 2 + j // 2


def _mm(a, b, name, *, ta=False, tb=False, out_dtype=BF16, tm=1024, tn=1536, tk=1024, n_outer=True, comm=(),
        b_shards=False, o_shards=False):
    m = a.shape[1] if ta else a.shape[0]
    k = a.shape[0] if ta else a.shape[1]
    if b_shards:
        n = b.shape[1] if tb else 4 * W_UP_SH
        tn, tk = (tn, W_UP_SH) if tb else (W_UP_SH, tk)
    else:
        n = b.shape[0] if tb else b.shape[1]
    if o_shards:
        tn = W_UP_SH
    tm, tn, tk = _tile(m, tm), _tile(n, tn), _tile(k, tk)
    nm, nn, nk = m // tm, n // tn, k // tk
    in_out = out_dtype == F32
    c_ins, c_outs, c_alias, c_scratch = _carry(comm, 2, 1)

    def body(a_ref, b_ref, *rest):
        ci, o_ref, co = rest[:len(c_ins)], rest[len(c_ins)], rest[len(c_ins) + 1:len(c_ins) + 1 + len(c_outs)]
        scr = rest[len(c_ins) + 1 + len(c_outs):]
        kk = pl.program_id(2)
        if comm:
            step = (pl.program_id(0) * (nm if n_outer else nn) + pl.program_id(1)) * nk + kk

            @pl.when(step == 0)
            def _():
                _comm_phase(comm, ci, co, scr[-2], scr[-1], True)

        _mm_step(a_ref, b_ref, o_ref, scr, kk)
        if comm:
            @pl.when(step == nm * nn * nk - 1)
            def _():
                _comm_phase(comm, ci, co, scr[-2], scr[-1], False)

    def _mm_step(a_ref, b_ref, o_ref, scr, kk):
        p = _dg(a_ref[...].astype(BF16), b_ref[...].astype(BF16), 0 if ta else 1, 1 if tb else 0)
        if nk == 1:
            o_ref[...] = p.astype(o_ref.dtype)
        else:
            acc = o_ref if in_out else scr[0]

            @pl.when(kk == 0)
            def _():
                acc[...] = p

            @pl.when(kk > 0)
            def _():
                acc[...] += p

            if not in_out:
                @pl.when(kk == nk - 1)
                def _():
                    o_ref[...] = acc[...].astype(o_ref.dtype)

    if n_outer:
        ij = lambda g0, g1: (g1, g0)
        grid = (nn, nm, nk)
    else:
        ij = lambda g0, g1: (g0, g1)
        grid = (nm, nn, nk)
    a_map = (lambda g0, g1, kk: (kk, ij(g0, g1)[0])) if ta else (lambda g0, g1, kk: (ij(g0, g1)[0], kk))
    if b_shards and tb:
        b_spec = pl.BlockSpec((None, tn, tk), lambda g0, g1, kk: (_ff_block(kk), ij(g0, g1)[1], 0))
    elif b_shards:
        b_spec = pl.BlockSpec((None, tk, tn), lambda g0, g1, kk: (_ff_block(ij(g0, g1)[1]), kk, 0))
    elif tb:
        b_spec = pl.BlockSpec((tn, tk), lambda g0, g1, kk: (ij(g0, g1)[1], kk))
    else:
        b_spec = pl.BlockSpec((tk, tn), lambda g0, g1, kk: (kk, ij(g0, g1)[1]))
    if o_shards:
        o_spec = pl.BlockSpec((None, tm, tn), lambda g0, g1, kk: (_ff_block(ij(g0, g1)[1]), ij(g0, g1)[0], 0))
        o_shape = jax.ShapeDtypeStruct((4, m, W_UP_SH), out_dtype)
    else:
        o_spec = pl.BlockSpec((tm, tn), lambda g0, g1, kk: ij(g0, g1))
        o_shape = jax.ShapeDtypeStruct((m, n), out_dtype)
    res = pl.pallas_call(
        body, name=name, grid=grid,
        in_specs=[pl.BlockSpec((tk, tm) if ta else (tm, tk), a_map), b_spec] + [HBM] * len(c_ins),
        out_specs=[o_spec] + [HBM] * len(c_outs),
        out_shape=[o_shape] + c_outs,
        scratch_shapes=([] if (in_out or nk == 1) else [pltpu.VMEM((tm, tn), F32)]) + c_scratch,
        input_output_aliases=c_alias,
        compiler_params=_params(("arbitrary",) * 3 if comm else ("parallel", "parallel", "arbitrary")),
    )(a, b, *c_ins)
    return (res[0], _split_units(comm, res[1:])) if comm else res[0]


def _rows(arr, rb, w=None, j=0):
    w = arr.shape[1] if w is None else w
    if callable(j):
        return arr, pl.BlockSpec((rb, w), lambda c, i: (i, j(c)))
    return arr, pl.BlockSpec((rb, w), lambda c, i: (i, j))


def _full(arr, w=None, j=0):
    w = arr.shape[1] if w is None else w
    if callable(j):
        return arr, pl.BlockSpec((arr.shape[0], w), lambda c, i: (0, j(c)))
    return arr, pl.BlockSpec((arr.shape[0], w), lambda c, i: (0, j))


def _halo(arr, rb, hb, w, j, before):
    per = rb // hb
    last = arr.shape[0] // hb - 1
    if before:
        rmap = lambda i: jnp.maximum(i * per - 1, 0)
    else:
        rmap = lambda i: jnp.minimum((i + 1) * per, last)
    return arr, pl.BlockSpec((hb, w), lambda c, i: (rmap(i), j(c) if callable(j) else j))


def _rowcall(fn, ins, outs, *, n_rows, rb, name, ncol=1, into=None, after=()):
    n_in = len(ins)
    nr = n_rows // rb
    unread = ([] if into is None else [into[0]]) + list(after)
    n_skip = len(unread)

    def body(*refs):
        c, i = pl.program_id(0), pl.program_id(1)
        res = fn(c, i, *[r[...] for r in refs[:n_in]])
        for val, spec, o_ref in zip(res, outs, refs[n_in + n_skip:]):
            if spec[2] == "row":
                o_ref[...] = val.astype(o_ref.dtype)
            else:
                @pl.when(i == 0)
                def _(o_ref=o_ref, val=val):
                    o_ref[...] = val.astype(o_ref.dtype)

                @pl.when(i > 0)
                def _(o_ref=o_ref, val=val):
                    o_ref[...] += val.astype(o_ref.dtype)

    out_specs = []
    for shape, dt, kind, block, col in outs:
        if kind == "row":
            out_specs.append(pl.BlockSpec(block, lambda c, i, col=col: (i, col(c))))
        else:
            out_specs.append(pl.BlockSpec(block, lambda c, i, col=col: (0, col(c))))
    return pl.pallas_call(
        body, name=name, grid=(ncol, nr),
        in_specs=[s for _, s in ins] + [pl.BlockSpec(memory_space=pl.ANY)] * n_skip, out_specs=out_specs,
        out_shape=[jax.ShapeDtypeStruct(o[0], o[1]) for o in outs],
        input_output_aliases={} if into is None else {n_in: into[1]},
        compiler_params=_params(("parallel", "arbitrary")),
    )(*[a for a, _ in ins], *unread)


def _orow(n_rows, w, dt, rb, bw=None, col=lambda c: 0):
    return ((n_rows, w), dt, "row", (rb, bw or w), col)


def _oacc(r, w, bw=None, col=lambda c: 0):
    return ((r, w), F32, "acc", (r, bw or w), col)


def _csum(v):
    return jnp.sum(v, axis=0, keepdims=True)


def _rms(v):
    return lax.rsqrt(jnp.mean(v * v, axis=-1, keepdims=True) + EPS)


def _norm_bwd(xv, dh, w, scale):
    r = _rms(xv)
    xh = xv * r
    dxh = dh * (w * (1.0 + scale))
    dx = r * (dxh - xh * jnp.mean(dxh * xh, axis=-1, keepdims=True))
    t = dh * xh
    return dx, _csum(dh), _csum(t * w), _csum(t * (1.0 + scale))


def _rope_tables(pos_col, invf, s):
    def fn(c, i, pos, f):
        ang = pos.astype(F32) * f
        lane = lax.broadcasted_iota(jnp.int32, ang.shape, 1)
        sign = jnp.where((lane % ATT_HD) < ATT_HD // 2, -1.0, 1.0)
        return jnp.cos(ang), jnp.sin(ang) * sign

    rb = 512
    return _rowcall(fn, [_rows(pos_col, rb), _full(invf)], [_orow(s, 128, F32, rb), _orow(s, 128, F32, rb)],
                    n_rows=s, rb=rb, name="rope_tables")


def _swap_halves(t):
    n = t.shape[1]
    lane = lax.broadcasted_iota(jnp.int32, t.shape, 1)
    return jnp.where((lane % ATT_HD) < ATT_HD // 2, pltpu.roll(t, n - 32, 1), pltpu.roll(t, 32, 1))


def _rope_apply(t, cos, sin_signed, inverse):
    cw = jnp.concatenate([cos] * (t.shape[1] // 128), axis=1)
    sw = jnp.concatenate([sin_signed] * (t.shape[1] // 128), axis=1)
    if inverse:
        sw = -sw
    return t * cw + _swap_halves(t) * sw


DIL_ROWS = 512


def _to_dilated(scr, val, out_ref, r):
    if r == 1:
        out_ref[...] = val.astype(out_ref.dtype)
        return
    n = val.shape[0] // r
    for hh in range(2):
        scr[hh] = val[:, hh * 128:(hh + 1) * 128]
        for pr in range(r):
            out_ref[:, pr * 256 + hh * 128:pr * 256 + (hh + 1) * 128] = scr[hh, pl.ds(pr, n, stride=r), :].astype(out_ref.dtype)


def _from_dilated(scr, in_ref, r):
    if r == 1:
        return in_ref[...].astype(F32)
    n = in_ref.shape[0]
    for hh in range(2):
        for pr in range(r):
            scr[hh, pl.ds(pr, n, stride=r), :] = in_ref[:, pr * 256 + hh * 128:pr * 256 + (hh + 1) * 128].astype(F32)
    return jnp.concatenate([scr[0], scr[1]], axis=1)


def _dil_spec(r):
    return pl.BlockSpec((DIL_ROWS // r, r * 256), lambda i: (i, 0))


def _dil_shape(s, r, dt):
    return jax.ShapeDtypeStruct((s // r, r * 256), dt)


_DIL_SCRATCH = [pltpu.VMEM((2, DIL_ROWS, 128), F32)]
_RS = tuple(r for _, r in ATT_GROUPS)


def _rope_fwd(p, cos_t, sin_t, s):
    def body(*refs):
        ins, cs, sn, outs, scr = refs[:9], refs[9][...], refs[10][...], refs[11:20], refs[20]
        for t in range(3):
            for g, r in enumerate(_RS):
                val = ins[3 * t + g][...].astype(F32)
                _to_dilated(scr, _rope_apply(val, cs, sn, False) if t < 2 else val, outs[3 * t + g], r)

    res = pl.pallas_call(
        body, name="rope", grid=(s // DIL_ROWS,),
        in_specs=[pl.BlockSpec((DIL_ROWS, 256), lambda i, c=base // 256 + g: (i, c)) for base in (P_AQ, P_AK, P_AV) for g in range(3)]
        + [pl.BlockSpec((DIL_ROWS, 128), lambda i: (i, 0))] * 2,
        out_specs=[_dil_spec(r) for _ in range(3) for r in _RS],
        out_shape=[_dil_shape(s, r, BF16) for _ in range(3) for r in _RS],
        scratch_shapes=_DIL_SCRATCH, compiler_params=_params(("parallel",)),
    )(*([p] * 9), cos_t, sin_t)
    return res[0:3], res[3:6], res[6:9]


def _attn_combine(att, s):
    def body(o0, o1, o2, l0, l1, l2, o_ref, lse_ref, od1, od2, ld1, ld2, scr):
        ov = [_from_dilated(scr, ref, r) for ref, r in zip((o0, o1, o2), _RS)]
        lv = [_from_dilated(scr, ref, r) for ref, r in zip((l0, l1, l2), _RS)]
        mx = jnp.maximum(jnp.maximum(lv[0], lv[1]), lv[2])
        ev = [jnp.exp(l - mx) for l in lv]
        z = ev[0] + ev[1] + ev[2]
        o = ((ev[0] * ov[0] + ev[1] * ov[1] + ev[2] * ov[2]) / z).astype(BF16)
        lse = mx + jnp.log(z)
        o_ref[...] = o
        lse_ref[...] = lse
        for ref, r in zip((od1, od2), _RS[1:]):
            _to_dilated(scr, o.astype(F32), ref, r)
        for ref, r in zip((ld1, ld2), _RS[1:]):
            _to_dilated(scr, lse, ref, r)

    return pl.pallas_call(
        body, name="attn_combine", grid=(s // DIL_ROWS,),
        in_specs=[_dil_spec(r) for r in _RS] * 2,
        out_specs=[_dil_spec(1)] * 2 + [_dil_spec(r) for r in _RS[1:]] * 2,
        out_shape=[_dil_shape(s, 1, BF16), _dil_shape(s, 1, F32)] + [_dil_shape(s, r, BF16) for r in _RS[1:]]
        + [_dil_shape(s, r, F32) for r in _RS[1:]],
        scratch_shapes=_DIL_SCRATCH, compiler_params=_params(("parallel",)),
    )(*[a[0] for a in att], *[a[1] for a in att])


def _dilate(t, s):
    def body(t_ref, o1, o2, scr):
        val = t_ref[...].astype(F32)
        for ref, r in zip((o1, o2), _RS[1:]):
            _to_dilated(scr, val, ref, r)

    return pl.pallas_call(
        body, name="attn_dilate", grid=(s // DIL_ROWS,), in_specs=[_dil_spec(1)], out_specs=[_dil_spec(r) for r in _RS[1:]],
        out_shape=[_dil_shape(s, r, t.dtype) for r in _RS[1:]], scratch_shapes=_DIL_SCRATCH, compiler_params=_params(("parallel",)),
    )(t)


def _rope_bwd(datt, d_glr, dp, cos_t, sin_t, s):
    tail = P_W - P_AQ

    def body(*refs):
        ins, cs, sn, glr, o_ref, scr = refs[:9], refs[9][...], refs[10][...], refs[11], refs[13], refs[14]
        for t in range(3):
            for g, r in enumerate(_RS):
                val = _from_dilated(scr, ins[3 * t + g], r)
                o_ref[:, t * ATT_W + g * 256:t * ATT_W + (g + 1) * 256] = (_rope_apply(val, cs, sn, True) if t < 2 else val).astype(BF16)
        o_ref[:, 3 * ATT_W:3 * ATT_W + 128] = glr[...]
        o_ref[:, 3 * ATT_W + 128:] = jnp.zeros((DIL_ROWS, tail - 3 * ATT_W - 128), BF16)

    return pl.pallas_call(
        body, name="rope_bwd", grid=(s // DIL_ROWS,),
        in_specs=[_dil_spec(r) for _ in range(3) for r in _RS] + [pl.BlockSpec((DIL_ROWS, 128), lambda i: (i, 0))] * 3
        + [pl.BlockSpec(memory_space=pl.ANY)],
        out_specs=pl.BlockSpec((DIL_ROWS, tail), lambda i: (i, P_AQ // tail)),
        out_shape=jax.ShapeDtypeStruct((s, P_W), BF16), input_output_aliases={12: 0},
        scratch_shapes=_DIL_SCRATCH, compiler_params=_params(("parallel",)),
    )(*[datt[g][t] for t in range(3) for g in range(3)], cos_t, sin_t, d_glr, dp)


def _tri_dot(tri, t):
    tb = tri.astype(BF16)
    hi = t.astype(BF16)
    r1 = t - hi.astype(F32)
    mid = r1.astype(BF16)
    lo = (r1 - mid.astype(F32)).astype(BF16)
    return _dg(tb, hi, 1, 0) + _dg(tb, mid, 1, 0) + _dg(tb, lo, 1, 0)


def _gla_decays(la_c, tri):
    b = _tri_dot(tri, la_c)
    row = lax.broadcasted_iota(jnp.int32, b.shape, 0)
    bmid = jnp.sum(jnp.where(row == GLA_CHUNK // 2 - 1, b, 0.0), axis=0, keepdims=True)
    blast = jnp.sum(jnp.where(row == GLA_CHUNK - 1, b, 0.0), axis=0, keepdims=True)
    return b, bmid, blast


def _gla_fwd(p, la, s, comm=()):
    tb, ch = GLA_BLOCK, GLA_CHUNK
    nb, nc = s // tb, tb // ch
    scale = GLA_DK ** -0.5
    c_ins, c_outs, c_alias, c_scratch = _carry(comm, 4, 2)

    def body(q_ref, k_ref, v_ref, la_ref, *rest):
        ci, (o_ref, st_ref) = rest[:len(c_ins)], rest[len(c_ins):len(c_ins) + 2]
        co, state = rest[len(c_ins) + 2:len(c_ins) + 2 + len(c_outs)], rest[len(c_ins) + 2 + len(c_outs)]
        step = pl.program_id(0)
        if comm:
            @pl.when(step == 0)
            def _():
                _comm_phase(comm, ci, co, rest[-2], rest[-1], True)

        _gla_fwd_step(q_ref, k_ref, v_ref, la_ref, o_ref, st_ref, state)
        if comm:
            @pl.when(step == nb - 1)
            def _():
                _comm_phase(comm, ci, co, rest[-2], rest[-1], False)

    def _gla_fwd_step(q_ref, k_ref, v_ref, la_ref, o_ref, st_ref, state):
        @pl.when(pl.program_id(0) == 0)
        def _():
            state[...] = jnp.zeros_like(state)

        ri = lax.broadcasted_iota(jnp.int32, (ch, ch), 0)
        ci = lax.broadcasted_iota(jnp.int32, (ch, ch), 1)
        causal = ci <= ri
        tri = causal.astype(F32)

        def chunk(c, carry):
            sl = pl.ds(pl.multiple_of(c * ch, ch), ch)
            b, bmid, blast = _gla_decays(la_ref[sl, :], tri)
            q = q_ref[sl, :].astype(F32) * scale
            k = k_ref[sl, :].astype(F32)
            v = v_ref[sl, :]
            qgt = (q * jnp.exp(b)).astype(BF16)
            qgn = (q * jnp.exp(b - bmid)).astype(BF16)
            kgn = (k * jnp.exp(bmid - b)).astype(BF16)
            kd = (k * jnp.exp(blast - b)).astype(BF16)
            dec = jnp.exp(blast)
            sts = [state[h] for h in range(GLA_H)]
            outs, news = [], []
            for h in range(GLA_H):
                hk, hv = slice(h * GLA_DK, (h + 1) * GLA_DK), slice(h * GLA_DV, (h + 1) * GLA_DV)
                a = jnp.where(causal, _dg(qgn[:, hk], kgn[:, hk], 1, 1), 0.0)
                outs.append(_dg(a.astype(BF16), v[:, hv], 1, 0) + _dg(qgt[:, hk], sts[h].astype(BF16), 1, 1))
                news.append(dec[:, hk] * sts[h] + _dg(v[:, hv], kd[:, hk], 0, 0))
            for h in range(GLA_H):
                st_ref[h, c] = sts[h]
                state[h] = news[h]
            o_ref[sl, :] = jnp.concatenate(outs, axis=1)
            return carry

        lax.fori_loop(0, nc, chunk, 0, unroll=2)

    hw = GLA_H * GLA_DK
    res = pl.pallas_call(
        body, name="gla_fwd", grid=(nb,),
        in_specs=[pl.BlockSpec((tb, hw), lambda t: (t, P_GQ // hw)),
                  pl.BlockSpec((tb, hw), lambda t: (t, P_GK // hw)),
                  pl.BlockSpec((tb, GLA_H * GLA_DV), lambda t: (t, P_GV // (GLA_H * GLA_DV))),
                  pl.BlockSpec((tb, hw), lambda t: (t, 0))] + [HBM] * len(c_ins),
        out_specs=[pl.BlockSpec((tb, GLA_H * GLA_DV), lambda t: (t, 0)),
                   pl.BlockSpec((GLA_H, nc, GLA_DV, GLA_DK), lambda t: (0, t, 0, 0))] + [HBM] * len(c_outs),
        out_shape=[jax.ShapeDtypeStruct((s, GLA_H * GLA_DV), F32),
                   jax.ShapeDtypeStruct((GLA_H, s // ch, GLA_DV, GLA_DK), F32)] + c_outs,
        scratch_shapes=[pltpu.VMEM((GLA_H, GLA_DV, GLA_DK), F32)] + c_scratch,
        input_output_aliases=c_alias,
        compiler_params=_params(("arbitrary",)),
    )(p, p, p, la, *c_ins)
    return res[0], res[1], _split_units(comm, res[2:])


def _gla_bwd(p, la, states, do, s, dp, comm=()):
    tb, ch = GLA_BLOCK, GLA_CHUNK
    nb, nc = s // tb, tb // ch
    scale = GLA_DK ** -0.5
    c_ins, c_outs, c_alias, c_scratch = _carry(comm, 7, 4)

    def body(q_ref, k_ref, v_ref, la_ref, st_ref, do_ref, dp_in, *rest):
        ci, outs = rest[:len(c_ins)], rest[len(c_ins):len(c_ins) + 4]
        co, dstate = rest[len(c_ins) + 4:len(c_ins) + 4 + len(c_outs)], rest[len(c_ins) + 4 + len(c_outs)]
        step = pl.program_id(0)
        if comm:
            @pl.when(step == 0)
            def _():
                _comm_phase(comm, ci, co, rest[-2], rest[-1], True)

        _gla_bwd_step(q_ref, k_ref, v_ref, la_ref, st_ref, do_ref, *outs, dstate)
        if comm:
            @pl.when(step == nb - 1)
            def _():
                _comm_phase(comm, ci, co, rest[-2], rest[-1], False)

    def _gla_bwd_step(q_ref, k_ref, v_ref, la_ref, st_ref, do_ref, dq_ref, dk_ref, dv_ref, dla_ref, dstate):
        @pl.when(pl.program_id(0) == 0)
        def _():
            dstate[...] = jnp.zeros_like(dstate)

        ri = lax.broadcasted_iota(jnp.int32, (ch, ch), 0)
        ci = lax.broadcasted_iota(jnp.int32, (ch, ch), 1)
        causal = ci <= ri
        tri = causal.astype(F32)
        tri_t = (ci >= ri).astype(F32)

        def chunk(cc, carry):
            c = nc - 1 - cc
            sl = pl.ds(pl.multiple_of(c * ch, ch), ch)
            b, bmid, blast = _gla_decays(la_ref[sl, :], tri)
            q = q_ref[sl, :].astype(F32) * scale
            k = k_ref[sl, :].astype(F32)
            v = v_ref[sl, :]
            e_b, e_qn, e_kn, e_kd = jnp.exp(b), jnp.exp(b - bmid), jnp.exp(bmid - b), jnp.exp(blast - b)
            dec = jnp.exp(blast)
            qgt, qgn, kgn, kd = q * e_b, q * e_qn, k * e_kn, k * e_kd
            qgt_b, qgn_b, kgn_b, kd_b = qgt.astype(BF16), qgn.astype(BF16), kgn.astype(BF16), kd.astype(BF16)
            do_b = do_ref[sl, :].astype(BF16)
            st0s = [st_ref[h, c] for h in range(GLA_H)]
            dsts = [dstate[h] for h in range(GLA_H)]
            dqgn, dqgt, dkgn, dkd, dvs, ddec, news = [], [], [], [], [], [], []
            for h in range(GLA_H):
                hk, hv = slice(h * GLA_DK, (h + 1) * GLA_DK), slice(h * GLA_DV, (h + 1) * GLA_DV)
                dst_b = dsts[h].astype(BF16)
                a = jnp.where(causal, _dg(qgn_b[:, hk], kgn_b[:, hk], 1, 1), 0.0).astype(BF16)
                da = jnp.where(causal, _dg(do_b[:, hv], v[:, hv], 1, 1), 0.0).astype(BF16)
                dqgn.append(_dg(da, kgn_b[:, hk], 1, 0))
                dqgt.append(_dg(do_b[:, hv], st0s[h].astype(BF16), 1, 0))
                dkgn.append(_dg(da, qgn_b[:, hk], 0, 0))
                dvs.append(_dg(a, do_b[:, hv], 0, 0) + _dg(kd_b[:, hk], dst_b, 1, 1))
                dkd.append(_dg(v[:, hv], dst_b, 1, 0))
                ddec.append(jnp.sum(st0s[h] * dsts[h], axis=0, keepdims=True))
                news.append(dec[:, hk] * dsts[h] + _dg(do_b[:, hv], qgt_b[:, hk], 0, 0))
            for h in range(GLA_H):
                dstate[h] = news[h]
            cat = lambda parts: jnp.concatenate(parts, axis=1)
            dqgn, dqgt, dkgn, dkd, ddec = cat(dqgn), cat(dqgt), cat(dkgn), cat(dkd), cat(ddec)
            dq_ref[sl, :] = (scale * (dqgn * e_qn + dqgt * e_b)).astype(dq_ref.dtype)
            dk_ref[sl, :] = (dkgn * e_kn + dkd * e_kd).astype(dk_ref.dtype)
            dv_ref[sl, :] = cat(dvs).astype(dv_ref.dtype)
            db = dqgn * qgn + dqgt * qgt - dkgn * kgn - dkd * kd
            extra = jnp.sum(dkd * kd, axis=0, keepdims=True) + ddec * dec
            dla_ref[sl, :] = _tri_dot(tri_t, db) + extra
            return carry

        lax.fori_loop(0, nc, chunk, 0, unroll=2)

    rev = lambda t: nb - 1 - t
    hw, vw = GLA_H * GLA_DK, GLA_H * GLA_DV
    res = pl.pallas_call(
        body, name="gla_bwd", grid=(nb,),
        in_specs=[pl.BlockSpec((tb, hw), lambda t: (rev(t), P_GQ // hw)),
                  pl.BlockSpec((tb, hw), lambda t: (rev(t), P_GK // hw)),
                  pl.BlockSpec((tb, vw), lambda t: (rev(t), P_GV // vw)),
                  pl.BlockSpec((tb, hw), lambda t: (rev(t), 0)),
                  pl.BlockSpec((GLA_H, nc, GLA_DV, GLA_DK), lambda t: (0, rev(t), 0, 0)),
                  pl.BlockSpec((tb, vw), lambda t: (rev(t), 0)), pl.BlockSpec(memory_space=pl.ANY)] + [HBM] * len(c_ins),
        out_specs=[pl.BlockSpec((tb, hw), lambda t: (rev(t), 0)),
                   pl.BlockSpec((tb, hw), lambda t: (rev(t), 0)),
                   pl.BlockSpec((tb, vw), lambda t: (rev(t), P_GV // vw)),
                   pl.BlockSpec((tb, hw), lambda t: (rev(t), 0))] + [HBM] * len(c_outs),
        out_shape=[jax.ShapeDtypeStruct((s, hw), BF16),
                   jax.ShapeDtypeStruct((s, hw), BF16),
                   jax.ShapeDtypeStruct((s, P_W), BF16),
                   jax.ShapeDtypeStruct((s, hw), F32)] + c_outs,
        scratch_shapes=[pltpu.VMEM((GLA_H, GLA_DV, GLA_DK), F32)] + c_scratch,
        input_output_aliases={6: 2, **c_alias},
        compiler_params=_params(("arbitrary",)),
    )(p, p, p, la, states, do, dp, *c_ins)
    return res[0], res[1], res[2], res[3], _split_units(comm, res[4:])


def _head_masks():
    lane = lax.broadcasted_iota(jnp.int32, (1, 4 * ATT_HD), 1)
    return [(lane >= h * ATT_HD) & (lane < (h + 1) * ATT_HD) for h in range(4)]


def _attn_fwd(qv, kv, pv, g, r, s):
    ln = s // r
    nblk = ln // ATT_BLK
    qcol = lambda pr: pr
    vcol = qcol
    prev = lambda n: jnp.maximum(n - 1, 0)

    def body(q_ref, kp_ref, kc_ref, vp_ref, vc_ref, o_ref, lse_ref):
        has_prev = pl.program_id(1) > 0
        ri = lax.broadcasted_iota(jnp.int32, (ATT_BLK, ATT_BLK), 0)
        ci = lax.broadcasted_iota(jnp.int32, (ATT_BLK, ATT_BLK), 1)
        m_cur = ci <= ri
        m_prev = (ci >= ri) & has_prev
        q, kp, kc, vp, vc = q_ref[...], kp_ref[...], kc_ref[...], vp_ref[...], vc_ref[...]
        o = jnp.zeros((ATT_BLK, 256), F32)
        lse = jnp.zeros((ATT_BLK, 256), F32)
        for hm in _head_masks():
            qm = jnp.where(hm, q, jnp.zeros_like(q))
            sc = jnp.where(m_cur, _dg(qm, kc, 1, 1) * 0.125, NEG)
            sp = jnp.where(m_prev, _dg(qm, kp, 1, 1) * 0.125, NEG)
            mx = jnp.maximum(jnp.max(sc, axis=1, keepdims=True), jnp.max(sp, axis=1, keepdims=True))
            pc, pp = jnp.exp(sc - mx), jnp.exp(sp - mx)
            den = jnp.sum(pc, axis=1, keepdims=True) + jnp.sum(pp, axis=1, keepdims=True)
            oh = (_dg(pc.astype(BF16), vc, 1, 0) + _dg(pp.astype(BF16), vp, 1, 0)) / den
            o = jnp.where(hm, oh, o)
            lse = jnp.where(hm, mx + jnp.log(den), lse)
        o_ref[...] = o.astype(o_ref.dtype)
        lse_ref[...] = lse

    blk = (ATT_BLK, 256)
    o, lse = pl.pallas_call(
        body, name=f"attn_fwd_{g}", grid=(r, nblk),
        in_specs=[pl.BlockSpec(blk, lambda pr, n: (n, qcol(pr))),
                  pl.BlockSpec(blk, lambda pr, n: (prev(n), qcol(pr))),
                  pl.BlockSpec(blk, lambda pr, n: (n, qcol(pr))),
                  pl.BlockSpec(blk, lambda pr, n: (prev(n), vcol(pr))),
                  pl.BlockSpec(blk, lambda pr, n: (n, vcol(pr)))],
        out_specs=[pl.BlockSpec(blk, lambda pr, n: (n, pr)), pl.BlockSpec(blk, lambda pr, n: (n, pr))],
        out_shape=[jax.ShapeDtypeStruct((ln, r * 256), BF16), jax.ShapeDtypeStruct((ln, r * 256), F32)],
        compiler_params=_params(("parallel", "parallel")),
    )(qv, kv, kv, pv, pv)
    return o, lse


def _attn_bwd(qv, kv, pv, dov, ov, lv, g, r, s):
    ln = s // r
    nblk = ln // ATT_BLK
    qcol = lambda pr: pr
    vcol = qcol
    prev = lambda n: jnp.maximum(n - 1, 0)
    nxt = lambda n: jnp.minimum(n + 1, nblk - 1)

    def body(qc_ref, qn_ref, kp_ref, kc_ref, vp_ref, vc_ref, doc_ref, don_ref, oc_ref, on_ref, lc_ref, ln_ref,
             dq_ref, dk_ref, dv_ref):
        n = pl.program_id(1)
        has_prev, has_next = n > 0, n < nblk - 1
        ri = lax.broadcasted_iota(jnp.int32, (ATT_BLK, ATT_BLK), 0)
        ci = lax.broadcasted_iota(jnp.int32, (ATT_BLK, ATT_BLK), 1)
        m_cur = ci <= ri
        m_prev = (ci >= ri) & has_prev
        m_next = (ci >= ri) & has_next
        qc, qn, kp, kc, vp, vc = qc_ref[...], qn_ref[...], kp_ref[...], kc_ref[...], vp_ref[...], vc_ref[...]
        doc, don = doc_ref[...], don_ref[...]
        pc_full = doc.astype(F32) * oc_ref[...].astype(F32)
        pn_full = don.astype(F32) * on_ref[...].astype(F32)
        lc, lnx = lc_ref[...], ln_ref[...]
        dq = jnp.zeros((ATT_BLK, 256), F32)
        dk = jnp.zeros((ATT_BLK, 256), F32)
        dv = jnp.zeros((ATT_BLK, 256), F32)
        zb = jnp.zeros_like(qc)
        for hm in _head_masks():
            qcm, qnm = jnp.where(hm, qc, zb), jnp.where(hm, qn, zb)
            docm, donm = jnp.where(hm, doc, zb), jnp.where(hm, don, zb)
            lse_c = jnp.max(jnp.where(hm, lc, NEG), axis=1, keepdims=True)
            lse_n = jnp.max(jnp.where(hm, lnx, NEG), axis=1, keepdims=True)
            del_c = jnp.sum(jnp.where(hm, pc_full, 0.0), axis=1, keepdims=True)
            del_n = jnp.sum(jnp.where(hm, pn_full, 0.0), axis=1, keepdims=True)
            pr_ = jnp.where(m_cur, jnp.exp(_dg(qcm, kc, 1, 1) * 0.125 - lse_c), 0.0)
            ds = (pr_ * (_dg(docm, vc, 1, 1) - del_c) * 0.125).astype(BF16)
            dqh = _dg(ds, kc, 1, 0)
            dkh = _dg(ds, qc, 0, 0)
            dvh = _dg(pr_.astype(BF16), doc, 0, 0)
            pr_ = jnp.where(m_prev, jnp.exp(_dg(qcm, kp, 1, 1) * 0.125 - lse_c), 0.0)
            ds = (pr_ * (_dg(docm, vp, 1, 1) - del_c) * 0.125).astype(BF16)
            dqh = dqh + _dg(ds, kp, 1, 0)
            pr_ = jnp.where(m_next, jnp.exp(_dg(qnm, kc, 1, 1) * 0.125 - lse_n), 0.0)
            ds = (pr_ * (_dg(donm, vc, 1, 1) - del_n) * 0.125).astype(BF16)
            dkh = dkh + _dg(ds, qn, 0, 0)
            dvh = dvh + _dg(pr_.astype(BF16), don, 0, 0)
            dq = jnp.where(hm, dqh, dq)
            dk = jnp.where(hm, dkh, dk)
            dv = jnp.where(hm, dvh, dv)
        dq_ref[...] = dq.astype(dq_ref.dtype)
        dk_ref[...] = dk.astype(dk_ref.dtype)
        dv_ref[...] = dv.astype(dv_ref.dtype)

    blk = (ATT_BLK, 256)
    cur = lambda col: pl.BlockSpec(blk, lambda pr, n: (n, col(pr)))
    prv = lambda col: pl.BlockSpec(blk, lambda pr, n: (prev(n), col(pr)))
    nx = lambda col: pl.BlockSpec(blk, lambda pr, n: (nxt(n), col(pr)))
    own = lambda pr: pr
    outs = pl.pallas_call(
        body, name=f"attn_bwd_{g}", grid=(r, nblk),
        in_specs=[cur(qcol), nx(qcol), prv(qcol), cur(qcol), prv(vcol), cur(vcol),
                  cur(own), nx(own), cur(own), nx(own), cur(own), nx(own)],
        out_specs=[cur(own), cur(own), cur(own)],
        out_shape=[jax.ShapeDtypeStruct((ln, r * 256), BF16)] * 3,
        compiler_params=_params(("parallel", "parallel")),
    )(qv, qv, kv, kv, pv, pv, dov, dov, ov, ov, lv, lv)
    return outs


def _gelu_parts(gv):
    cdf = 0.5 * (1.0 + lax.erf(gv * (2.0 ** -0.5)))
    pdf = jnp.exp(-0.5 * gv * gv) * (1.0 / math.sqrt(2.0 * math.pi))
    return cdf, pdf


def _pick_row(t, k):
    row = lax.broadcasted_iota(jnp.int32, t.shape, 0)
    return jnp.sum(jnp.where(row == k, t, 0.0), axis=0, keepdims=True)


def _shift_rows(u, halo, n):
    row = lax.broadcasted_iota(jnp.int32, u.shape, 0)
    out = pltpu.roll(u, n, 0)
    for k in range(n):
        out = jnp.where(row == k, _pick_row(halo, 16 - n + k), out)
    return out


def _shift_rows_up(u, halo, n):
    rb = u.shape[0]
    row = lax.broadcasted_iota(jnp.int32, u.shape, 0)
    out = pltpu.roll(u, rb - n, 0)
    for k in range(n):
        out = jnp.where(row == rb - n + k, _pick_row(halo, k), out)
    return out


def _conv(u, halo, cw, cb):
    return cb + _pick_row(cw, 0) * _shift_rows(u, halo, 2) + _pick_row(cw, 1) * _shift_rows(u, halo, 1) + _pick_row(cw, 2) * u


def _local_step(x, mod, pos_col, target, sm, w_sh, g0, chip, core):
    s = x.shape[0]
    shift1, scale1, gate1, shift2, scale2, gate2 = [mod[i:i + 1, :] for i in range(6)]
    rb = 512
    chip1 = chip.reshape(1)

    def f_norm1(c, i, xv, nw, sc, sh):
        return ((xv * _rms(xv) * nw) * (1.0 + sc) + sh,)

    (h,) = _rowcall(f_norm1, [_rows(x, rb), _full(sm["n1w"]), _full(scale1), _full(shift1)],
                    [_orow(s, D, BF16, rb)], n_rows=s, rb=rb, name="norm1")
    own = lambda got, i: lax.dynamic_update_slice(got, w_sh[i], (chip, 0, 0))
    invf = jnp.tile(ROPE_THETA ** (-jnp.arange(ATT_HD // 2, dtype=F32) / (ATT_HD // 2)), 4).reshape(1, 128)
    cos_t, sin_t = _rope_tables(pos_col, invf, s)
    _, got0 = _unit_wait(*g0[:4], after=[h, cos_t, sin_t], name="gather_w_in_wait")
    [got0] = _comm_call("gather_w_in_d2d", [_u_gather_d2d(got0, (0,))])
    u_g1 = _u_gather_ici(w_sh, (1, 2, 3, 4, 5))
    g1 = _unit_start(u_g1, "gather_weights_start", after=got0)
    w = dict(win=_win_assemble(own(got0[0], 0), after=g1[3:]))
    p = _mm(h, w["win"], "in_proj", tm=2048, tn=1536)

    def f_gla_pre(c, i, glr, w2, gb):
        z = _dg(glr, w2.astype(BF16), 1, 0) + gb
        return ((jnp.minimum(z, 0.0) - jnp.log(1.0 + jnp.exp(-jnp.abs(z)))) * (1.0 / GLA_TAU),)

    (la,) = _rowcall(f_gla_pre, [_rows(p, rb, 128, P_LR // 128), _full(sm["w2"]), _full(sm["gb"])],
                     [_orow(s, 512, F32, rb)], n_rows=s, rb=rb, name="gla_pre")
    o_gla, states, _ = _gla_fwd(p, la, s)
    _, got = _unit_wait(u_g1, *g1[:3], after=[o_gla], name="gather_weights_wait")
    [got123] = _comm_call("gather_weights_d2d", [_u_gather_d2d(got[:3], (1, 2, 3))])
    got45 = got[3:]
    w.update(wgb=own(got123[0], 1).reshape(1024, D), wab=_cols_join(own(got123[1], 2)), wout=own(got123[2], 3).reshape(D, D))

    def f_gla_post(c, i, ov, gnw, gr):
        on = jnp.concatenate([ov[:, k * 256:(k + 1) * 256] * _rms(ov[:, k * 256:(k + 1) * 256]) * gnw
                              for k in range(GLA_H)], axis=1)
        g = gr.astype(F32)
        return (on * (g * _sigmoid(g)),)

    (og,) = _rowcall(f_gla_post, [_rows(o_gla, rb), _full(sm["gnw"]), _rows(p, rb, 1024, P_GR // 1024)],
                     [_orow(s, 1024, BF16, rb)], n_rows=s, rb=rb, name="gla_post")
    y_gla = _mm(og, w["wgb"], "gla_branch")

    q_d, k_d, v_d = _rope_fwd(p, cos_t, sin_t, s)
    att = [_attn_fwd(q_d[g], k_d[g], v_d[g], g, r, s) for g, r in enumerate(_RS)]
    o_att, lse, o_d1, o_d2, lse_d1, lse_d2 = _attn_combine(att, s)
    y_att = _mm(o_att, w["wab"], "attn_branch")

    def f_merge(c, i, ma, mb, yg, ya):
        return (_sigmoid(ma.astype(F32)) * yg.astype(F32) + _sigmoid(mb.astype(F32)) * ya.astype(F32),)

    (mixed,) = _rowcall(f_merge, [_rows(p, rb, D, P_MA // D), _rows(p, rb, D, P_MB // D), _rows(y_gla, rb), _rows(y_att, rb)],
                        [_orow(s, D, BF16, rb)], n_rows=s, rb=rb, name="merge")
    z1, [got45] = _mm(mixed, w["wout"], "out_proj", comm=[_u_gather_d2d(got45, (4, 5))])
    w.update(wup=own(got45[0], 4), wdown=own(got45[1], 5).reshape(D_FF, D))

    def f_norm2(c, i, xv, z, g1, nw, sc, sh):
        x1 = xv + g1 * z.astype(F32)
        return (x1, (x1 * _rms(x1) * nw) * (1.0 + sc) + sh)

    x1, h2 = _rowcall(f_norm2, [_rows(x, rb), _rows(z1, rb), _full(gate1), _full(sm["n2w"]), _full(scale2), _full(shift2)],
                      [_orow(s, D, F32, rb), _orow(s, D, BF16, rb)], n_rows=s, rb=rb, name="norm2")
    u = _mm(h2, w["wup"], "up_proj", tm=2048, b_shards=True)

    cwid = 2 * W_UP_SH

    def f_ffn(c, i, uv, hl, cw, cb):
        uc = _conv(uv.astype(F32), hl.astype(F32) * (i > 0).astype(F32), cw, cb)
        val, gt = uc[:, :W_UP_SH], uc[:, W_UP_SH:]
        cdf, _ = _gelu_parts(gt)
        return (gt * cdf * val,)

    ccol = lambda c: c
    rw = 256
    (hidden,) = _rowcall(f_ffn, [_rows(u, rw, cwid, ccol), _halo(u, rw, 16, cwid, ccol, True),
                                 _full(sm["cw"], cwid, ccol), _full(sm["cb"], cwid, ccol)],
                         [_orow(s, D_FF, BF16, rw, W_UP_SH, ccol)], n_rows=s, rb=rw, name="conv_geglu", ncol=2)
    z2 = _mm(hidden, w["wdown"], "down_proj", tk=D_FF)

    def f_final(c, i, x1v, z, g2, fw, tgt):
        x2 = x1v + g2 * z.astype(F32)
        r = _rms(x2)
        xh = x2 * r
        e = xh * fw - tgt
        loss = 0.5 * jnp.sum(jnp.mean(e * e, axis=-1, keepdims=True), axis=0, keepdims=True)
        dy = e * (1.0 / D)
        dxh = dy * fw
        dx2 = r * (dxh - xh * jnp.mean(dxh * xh, axis=-1, keepdims=True))
        return (loss, dx2, dx2 * g2, _csum(dy * xh), _csum(dx2 * z.astype(F32)))

    loss, dx2, dz2, d_fnw, d_gate2 = _rowcall(
        f_final, [_rows(x1, rb), _rows(z2, rb), _full(gate2), _full(sm["fnw"]), _rows(target, rb)],
        [_oacc(1, 1), _orow(s, D, F32, rb), _orow(s, D, BF16, rb), _oacc(1, D), _oacc(1, D)],
        n_rows=s, rb=rb, name="final_loss")
    d_hidden = _mm(dz2, w["wdown"], "down_proj_dx", tb=True, tn=1408)
    g_wdown = _mm(hidden, dz2, "down_proj_dw", ta=True, out_dtype=F32, tm=1408, tn=1024, tk=2048)

    def f_ffn_bwd(c, i, uv, hl, dh, cw, cb):
        uf = uv.astype(F32)
        hf = hl.astype(F32) * (i > 0).astype(F32)
        u1, u2 = _shift_rows(uf, hf, 1), _shift_rows(uf, hf, 2)
        uc = cb + _pick_row(cw, 0) * u2 + _pick_row(cw, 1) * u1 + _pick_row(cw, 2) * uf
        val, gt = uc[:, :W_UP_SH], uc[:, W_UP_SH:]
        cdf, pdf = _gelu_parts(gt)
        dhf = dh.astype(F32)
        duc = jnp.concatenate([dhf * (gt * cdf), dhf * val * (cdf + gt * pdf)], axis=1)
        dcw = jnp.concatenate([_csum(duc * u2), _csum(duc * u1), _csum(duc * uf)], axis=0)
        return (duc, _csum(duc), dcw)

    duc, d_cb, d_cw = _rowcall(
        f_ffn_bwd, [_rows(u, rw, cwid, ccol), _halo(u, rw, 16, cwid, ccol, True), _rows(d_hidden, rw, W_UP_SH, ccol),
                    _full(sm["cw"], cwid, ccol), _full(sm["cb"], cwid, ccol)],
        [_orow(s, 2 * D_FF, BF16, rw, cwid, ccol), _oacc(1, 2 * D_FF, cwid, ccol), _oacc(3, 2 * D_FF, cwid, ccol)],
        n_rows=s, rb=rw, name="conv_geglu_bwd", ncol=2)

    def f_conv_t(c, i, dv, hl, cw):
        df = dv.astype(F32)
        hf = hl.astype(F32) * (i < s // rw - 1).astype(F32)
        return (_pick_row(cw, 2) * df + _pick_row(cw, 1) * _shift_rows_up(df, hf, 1) + _pick_row(cw, 0) * _shift_rows_up(df, hf, 2),)

    (du,) = _rowcall(f_conv_t, [_rows(duc, rw, cwid, ccol), _halo(duc, rw, 16, cwid, ccol, False), _full(sm["cw"], cwid, ccol)],
                     [_orow(s, 2 * D_FF, BF16, rw, cwid, ccol)], n_rows=s, rb=rw, name="conv_transpose", ncol=2)
    g_wup = _mm(h2, du, "up_proj_dw", ta=True, out_dtype=F32, tm=1024, tk=2048, o_shards=True)
    gs45 = [g_wup, g_wdown.reshape(4, W_DOWN_SH, 1024)]
    d_h2, [land45] = _mm(du, w["wup"], "up_proj_dx", tb=True, tm=2048, b_shards=True, comm=[_u_pair_send(gs45, (4, 5))])
    ts45 = [_pair_add(g, ld, core, "grad_pair_add_" + BIG[i]) for g, ld, i in zip(gs45, land45, (4, 5))]
    u_ex4 = _u_chip_exchange(ts45[:1])
    ex4 = _unit_start(u_ex4, "grad_exchange_w_up_start")

    def f_norm2_bwd(c, i, x1v, dh, dxr, z, nw, sc, g1):
        dxn, dsh, dsc, dnw = _norm_bwd(x1v, dh.astype(F32), nw, sc)
        dx1 = dxr + dxn
        return (dx1, dx1 * g1, dsh, dsc, dnw, _csum(dx1 * z.astype(F32)))

    dx1, dz1, d_shift2, d_scale2, d_n2w, d_gate1 = _rowcall(
        f_norm2_bwd, [_rows(x1, rb), _rows(d_h2, rb), _rows(dx2, rb), _rows(z1, rb), _full(sm["n2w"]), _full(scale2), _full(gate1)],
        [_orow(s, D, F32, rb), _orow(s, D, BF16, rb), _oacc(1, D), _oacc(1, D), _oacc(1, D), _oacc(1, D)],
        n_rows=s, rb=rb, name="norm2_bwd", after=ex4[3:])
    d_mixed = _mm(dz1, w["wout"], "out_proj_dx", tb=True)
    g_wout = _mm(mixed, dz1, "out_proj_dw", ta=True, out_dtype=F32, tk=2048)

    def f_merge_bwd(c, i, dm, ma, mb, yg, ya):
        dmf, ygf, yaf = dm.astype(F32), yg.astype(F32), ya.astype(F32)
        sa, sb = _sigmoid(ma.astype(F32)), _sigmoid(mb.astype(F32))
        return (dmf * sa, dmf * sb, jnp.concatenate([dmf * ygf * sa * (1.0 - sa), dmf * yaf * sb * (1.0 - sb)], axis=1))

    dy_gla, dy_att, dp = _rowcall(
        f_merge_bwd, [_rows(d_mixed, rb), _rows(p, rb, D, P_MA // D), _rows(p, rb, D, P_MB // D), _rows(y_gla, rb), _rows(y_att, rb)],
        [_orow(s, D, BF16, rb)] * 2 + [_orow(s, P_W, BF16, rb, 2 * D, lambda c: P_MA // (2 * D))], n_rows=s, rb=rb, name="merge_bwd")
    d_og = _mm(dy_gla, w["wgb"], "gla_branch_dx", tb=True)
    g_wgb = _mm(og, dy_gla, "gla_branch_dw", ta=True, out_dtype=F32, tk=2048)
    d_oatt = _mm(dy_att, w["wab"], "attn_branch_dx", tb=True)
    g_wab = _mm(o_att, dy_att, "attn_branch_dw", ta=True, out_dtype=F32, tk=2048)

    def f_gla_post_bwd(c, i, ov, gnw, gr, dog):
        g = gr.astype(F32)
        sg = _sigmoid(g)
        silu = g * sg
        dof = dog.astype(F32)
        don = dof * silu
        on_parts, do_parts, dgn = [], [], jnp.zeros((1, 256), F32)
        for k in range(GLA_H):
            oh = ov[:, k * 256:(k + 1) * 256]
            dh = don[:, k * 256:(k + 1) * 256]
            r = _rms(oh)
            xh = oh * r
            dgn = dgn + _csum(dh * xh)
            dxh = dh * gnw
            do_parts.append(r * (dxh - xh * jnp.mean(dxh * xh, axis=-1, keepdims=True)))
            on_parts.append(xh * gnw)
        on = jnp.concatenate(on_parts, axis=1)
        dgr = dof * on * (sg * (1.0 + g * (1.0 - sg)))
        return (jnp.concatenate(do_parts, axis=1), dgr, dgn)

    do_gla, dp, d_gnw = _rowcall(
        f_gla_post_bwd, [_rows(o_gla, rb), _full(sm["gnw"]), _rows(p, rb, 1024, P_GR // 1024), _rows(d_og, rb)],
        [_orow(s, 1024, F32, rb), _orow(s, P_W, BF16, rb, 1024, lambda c: P_GR // 1024), _oacc(1, 256)],
        n_rows=s, rb=rb, name="gla_post_bwd", into=(dp, 1))
    gs123 = [g_wgb.reshape(4, 256, 1024), _cols_split(g_wab), g_wout.reshape(4, 256, 1024)]
    d_gq, d_gk, dp, d_la, [land123] = _gla_bwd(p, la, states, do_gla, s, dp, comm=[_u_pair_send(gs123, (1, 2, 3))])
    ts123 = [_pair_add(g, ld, core, "grad_pair_add_" + BIG[i]) for g, ld, i in zip(gs123, land123, (1, 2, 3))]

    def f_gla_pre_bwd(c, i, lav, dlav, glr, w2):
        dz = dlav * (1.0 / GLA_TAU) * (1.0 - jnp.exp(GLA_TAU * lav))
        dzb = dz.astype(BF16)
        return (_dg(dzb, w2.astype(BF16), 1, 1), _csum(dz), _dg(glr, dzb, 0, 0))

    d_glr, d_gb, d_w2 = _rowcall(
        f_gla_pre_bwd, [_rows(la, rb), _rows(d_la, rb), _rows(p, rb, 128, P_LR // 128), _full(sm["w2"])],
        [_orow(s, 128, BF16, rb), _oacc(1, 512), _oacc(128, 512)], n_rows=s, rb=rb, name="gla_pre_bwd")

    do_d = [d_oatt] + list(_dilate(d_oatt, s))
    datt = [_attn_bwd(q_d[g], k_d[g], v_d[g], do_d[g], (o_att, o_d1, o_d2)[g], (lse, lse_d1, lse_d2)[g], g, r, s)
            for g, r in enumerate(_RS)]
    dp = _rope_bwd(datt, d_glr, dp, cos_t, sin_t, s)
    dp = lax.dynamic_update_slice(dp, jnp.concatenate([d_gq, d_gk], axis=1), (0, P_GQ))
    [t4], [r4] = _unit_wait(u_ex4, *ex4[:3], after=[dp], name="grad_exchange_w_up_wait")
    half4 = [_chip_sum(t4, r4, chip1, "grad_chip_sum_w_up")]
    g_win, [r1235, oth4] = _mm(h, dp, "in_proj_dw", ta=True, out_dtype=F32, tm=1024, tn=1536, tk=2048,
                               comm=[_u_chip_exchange(ts123 + ts45[1:]), _u_pair_join(half4)])
    half1235 = [_chip_sum(t, r, chip1, "grad_chip_sum_" + BIG[i]) for t, r, i in zip(ts123 + ts45[1:], r1235, (1, 2, 3, 5))]
    gs0 = [_win_split(g_win)]
    d_h, [land0, oth1235] = _mm(dp, w["win"], "in_proj_dx", tb=True, tk=3840,
                                comm=[_u_pair_send(gs0, (0,)), _u_pair_join(half1235)])
    half123, half45 = half1235[:3], half4 + half1235[3:]
    oth123, oth45 = oth1235[:3], oth4 + oth1235[3:]
    ts0 = _pair_add(gs0[0], land0[0], core, "grad_pair_add_w_in")

    def f_norm1_bwd(c, i, xv, dh, dxr, nw, sc):
        dxn, dsh, dsc, dnw = _norm_bwd(xv, dh.astype(F32), nw, sc)
        return (dxr + dxn, dsh, dsc, dnw)

    grad_x, d_shift1, d_scale1, d_n1w = _rowcall(
        f_norm1_bwd, [_rows(x, rb), _rows(d_h, rb), _rows(dx1, rb), _full(sm["n1w"]), _full(scale1)],
        [_orow(s, D, F32, rb), _oacc(1, D), _oacc(1, D), _oacc(1, D)], n_rows=s, rb=rb, name="norm1_bwd")

    dmod = jnp.concatenate([d_shift1, d_scale1, d_gate1, d_shift2, d_scale2, d_gate2], axis=1)
    small = dict(dmod=dmod, n1w=d_n1w, gb=d_gb, gnw=d_gnw, n2w=d_n2w, cb=d_cb, fnw=d_fnw, w2=d_w2, cw=d_cw)
    return loss, grad_x, half123 + half45, oth123 + oth45, small, ts0


def _win_pieces():
    runs = [(P_GV, 1024, 2048), (P_MA, 5392, 2048), (P_GQ, 0, 1024), (P_AQ, 3088, 2304), (P_LR, 3072, GLA_LR)]
    out = []
    for kc, rc, ln in runs:
        while ln > 0:
            step = min(ln, W_IN_SH - rc % W_IN_SH)
            out.append((kc, rc, step))
            kc, rc, ln = kc + step, rc + step, ln - step
    return out


def _win_assemble(shards, after=()):
    rb = 256

    def body(s_ref, *rest):
        o_ref = rest[-1]
        o_ref[:, W_IN:] = jnp.zeros((rb, P_W - W_IN), o_ref.dtype)
        for kc, rc, ln in _win_pieces():
            o_ref[:, kc:kc + ln] = s_ref[rc // W_IN_SH, :, rc % W_IN_SH:rc % W_IN_SH + ln]

    return pl.pallas_call(
        body, name="w_in_assemble", grid=(D // rb,),
        in_specs=[pl.BlockSpec((4, rb, W_IN_SH), lambda i: (0, i, 0))] + [pl.BlockSpec(memory_space=pl.ANY)] * len(after),
        out_specs=pl.BlockSpec((rb, P_W), lambda i: (i, 0)),
        out_shape=jax.ShapeDtypeStruct((D, P_W), shards.dtype), compiler_params=_params(("parallel",)),
    )(shards, *after)


def _win_split(g):
    rb = 256

    def body(g_ref, o_ref):
        for kc, rc, ln in _win_pieces():
            o_ref[rc // W_IN_SH, :, rc % W_IN_SH:rc % W_IN_SH + ln] = g_ref[:, kc:kc + ln]

    return pl.pallas_call(
        body, name="w_in_grad_split", grid=(D // rb,),
        in_specs=[pl.BlockSpec((rb, P_W), lambda i: (i, 0))], out_specs=pl.BlockSpec((4, rb, W_IN_SH), lambda i: (0, i, 0)),
        out_shape=jax.ShapeDtypeStruct((4, D, W_IN_SH), g.dtype), compiler_params=_params(("parallel",)),
    )(g)


def _ff_to_kernel(a):
    h = W_UP_SH
    return jnp.concatenate([a[:, 0:h], a[:, D_FF:D_FF + h], a[:, h:D_FF], a[:, D_FF + h:]], axis=1)


def _ff_from_kernel(a):
    h = W_UP_SH
    return jnp.concatenate([a[:, 0:h], a[:, 2 * h:3 * h], a[:, h:2 * h], a[:, 3 * h:]], axis=1)


BIG = ("w_in", "w_gla_branch", "w_attn_branch", "w_out", "w_up", "w_down")
SH_SHAPES = ((1024, W_IN_SH), (256, 1024), (256, 256), (256, 1024), (1024, W_UP_SH), (W_DOWN_SH, 1024))
N_BIG = len(BIG)


def _cols_join(t):
    return jnp.concatenate([t[k] for k in range(4)], axis=1)


def _cols_split(t):
    cols = t.shape[1] // 4
    return jnp.stack([t[:, k * cols:(k + 1) * cols] for k in range(4)])


def _me():
    return lax.axis_index("x"), lax.axis_index("y"), lax.axis_index("c")


HBM = pl.BlockSpec(memory_space=pltpu.HBM)
VMEM_SPEC = pl.BlockSpec(memory_space=pltpu.VMEM)


def _allgather8(xs, name):
    rows = xs.shape[0]

    def body(x_ref, out_ref, send_sems, recv_sems, local_sem):
        x, y, c = _me()
        me = 4 * x + 2 * y + c
        mine = pltpu.make_async_copy(x_ref, out_ref.at[me], local_sem)
        mine.start()
        flips = [(k >> 2 & 1, k >> 1 & 1, k & 1) for k in range(1, 8)]

        def peer(f):
            return (jnp.where(f[0] == 1, 1 - x, x), jnp.where(f[1] == 1, 1 - y, y), jnp.where(f[2] == 1, 1 - c, c))

        sends = []
        for k, f in enumerate(flips):
            cp = pltpu.make_async_remote_copy(src_ref=x_ref, dst_ref=out_ref.at[me], send_sem=send_sems.at[k],
                                              recv_sem=recv_sems.at[k], device_id=peer(f), device_id_type=MESH)
            cp.start()
            sends.append(cp)
        for k, f in enumerate(flips):
            px, py, pc = peer(f)
            pltpu.make_async_remote_copy(src_ref=x_ref, dst_ref=out_ref.at[4 * px + 2 * py + pc], send_sem=send_sems.at[k],
                                         recv_sem=recv_sems.at[k], device_id=peer(f), device_id_type=MESH).wait_recv()
        for cp in sends:
            cp.wait_send()
        mine.wait()

    return pl.pallas_call(
        body, name=name, out_shape=jax.ShapeDtypeStruct((8, rows, 128), F32),
        in_specs=[VMEM_SPEC], out_specs=VMEM_SPEC,
        scratch_shapes=[pltpu.SemaphoreType.DMA((7,)), pltpu.SemaphoreType.DMA((7,)), pltpu.SemaphoreType.DMA],
        compiler_params=pltpu.CompilerParams(vmem_limit_bytes=VMEM_LIMIT),
    )(xs)


def _half_rows(i, cc, unit):
    rows = SH_SHAPES[i][0] // 2
    return pl.ds(pl.multiple_of(cc * rows, unit), rows)


def _rc(src, dst, sems, to):
    return pltpu.make_async_remote_copy(src_ref=src, dst_ref=dst, send_sem=sems[0], recv_sem=sems[1], device_id=to, device_id_type=MESH)


def _other_chips(x, y):
    return [(1 - x, y), (x, 1 - y), (1 - x, 1 - y)]


def _u_gather_ici(w_sh, idxs):
    def copies(ins, outs, sem):
        x, y, c = _me()
        res = []
        for j, (px, py) in enumerate(_other_chips(x, y)):
            for n, i in enumerate(idxs):
                src = ins[n].at[0, _half_rows(i, c, 16)]
                res.append((_rc(src, outs[n].at[2 * x + y, _half_rows(i, c, 16)], sem(j * len(idxs) + n), (px, py, c)),
                            _rc(src, outs[n].at[2 * px + py, _half_rows(i, c, 16)], sem(j * len(idxs) + n), (px, py, c))))
        return res

    return dict(ins=[w_sh[i] for i in idxs], outs=[jax.ShapeDtypeStruct((4,) + SH_SHAPES[i], BF16) for i in idxs],
                nsem=3 * len(idxs), alias={}, copies=copies)


def _u_gather_d2d(got, idxs):
    def copies(ins, outs, sem):
        x, y, c = _me()
        res = []
        for j, (px, py) in enumerate(_other_chips(x, y)):
            for n, i in enumerate(idxs):
                src = ins[n].at[2 * px + py, _half_rows(i, c, 16)]
                res.append((_rc(src, outs[n].at[2 * px + py, _half_rows(i, c, 16)], sem(j * len(idxs) + n), (x, y, 1 - c)),
                            _rc(src, outs[n].at[2 * px + py, _half_rows(i, 1 - c, 16)], sem(j * len(idxs) + n), (x, y, 1 - c))))
        return res

    return dict(ins=list(got), outs=[jax.ShapeDtypeStruct(g.shape, g.dtype) for g in got], nsem=3 * len(idxs),
                alias={n: n for n in range(len(idxs))}, copies=copies)


def _u_pair_send(gs, idxs):
    def copies(ins, outs, sem):
        x, y, c = _me()
        res = []
        for n, i in enumerate(idxs):
            for sh in range(4):
                cp = _rc(ins[n].at[sh, _half_rows(i, 1 - c, 8)], outs[n].at[sh], sem(4 * n + sh), (x, y, 1 - c))
                res.append((cp, cp))
        return res

    return dict(ins=list(gs), outs=[jax.ShapeDtypeStruct((4, SH_SHAPES[i][0] // 2, SH_SHAPES[i][1]), F32) for i in idxs],
                nsem=4 * len(idxs), alias={}, copies=copies)


def _u_chip_exchange(ts):
    def copies(ins, outs, sem):
        x, y, c = _me()
        res = []
        for j, (px, py) in enumerate(_other_chips(x, y)):
            for n in range(len(ts)):
                cp = _rc(ins[n].at[2 * px + py], outs[n].at[j], sem(j * len(ts) + n), (px, py, c))
                res.append((cp, cp))
        return res

    return dict(ins=list(ts), outs=[jax.ShapeDtypeStruct((3,) + t.shape[1:], t.dtype) for t in ts], nsem=3 * len(ts),
                alias={}, copies=copies)


def _u_pair_join(hs):
    def copies(ins, outs, sem):
        x, y, c = _me()
        res = []
        for n in range(len(hs)):
            cp = _rc(ins[n], outs[n], sem(n), (x, y, 1 - c))
            res.append((cp, cp))
        return res

    return dict(ins=list(hs), outs=[jax.ShapeDtypeStruct(h.shape, h.dtype) for h in hs], nsem=len(hs), alias={}, copies=copies)


def _comm_phase(units, ci, co, send_sems, recv_sems, start):
    ii = oo = off = 0
    for u in units:
        ni, no = len(u["ins"]), len(u["outs"])
        for st, arrival in u["copies"](ci[ii:ii + ni], co[oo:oo + no], lambda k, off=off: (send_sems.at[off + k], recv_sems.at[off + k])):
            if start:
                st.start()
            else:
                st.wait_send()
                arrival.wait_recv()
        ii, oo, off = ii + ni, oo + no, off + u["nsem"]


def _carry(units, n_in, n_out):
    ins = [a for u in units for a in u["ins"]]
    outs = [o for u in units for o in u["outs"]]
    alias, ii, oo = {}, 0, 0
    for u in units:
        for a, b in u["alias"].items():
            alias[n_in + ii + a] = n_out + oo + b
        ii, oo = ii + len(u["ins"]), oo + len(u["outs"])
    nsem = sum(u["nsem"] for u in units)
    scratch = [pltpu.SemaphoreType.DMA((nsem,)), pltpu.SemaphoreType.DMA((nsem,))] if units else []
    return ins, outs, alias, scratch


def _split_units(units, res):
    out, oo = [], 0
    for u in units:
        out.append(list(res[oo:oo + len(u["outs"])]))
        oo += len(u["outs"])
    return out


def _comm_call(name, units):
    ins, outs, alias, scratch = _carry(units, 0, 0)

    def body(*refs):
        ci, co = refs[:len(ins)], refs[len(ins):len(ins) + len(outs)]
        _comm_phase(units, ci, co, refs[-2], refs[-1], True)
        _comm_phase(units, ci, co, refs[-2], refs[-1], False)

    res = pl.pallas_call(body, name=name, out_shape=outs, in_specs=[HBM] * len(ins), out_specs=[HBM] * len(outs),
                         scratch_shapes=scratch, input_output_aliases=alias)(*ins)
    return _split_units(units, res)


SEM = pl.BlockSpec(memory_space=pltpu.SEMAPHORE)
EFFECT = pltpu.SideEffectType.DATAFLOW_SIDE_EFFECTING


def _unit_start(unit, name, after=()):
    bufs = list(unit["ins"]) + [lax.empty(o.shape, o.dtype) for o in unit["outs"]]
    n_i, n_b, ns = len(unit["ins"]), len(bufs), unit["nsem"]

    def body(*refs):
        send_sems, recv_sems = refs[n_b + len(after)], refs[n_b + len(after) + 1]
        for st, _ in unit["copies"](refs[:n_i], refs[n_i:n_b], lambda k: (send_sems.at[k], recv_sems.at[k])):
            st.start()
        refs[-1][...] = jnp.zeros_like(refs[-1])

    res = pl.pallas_call(
        body, name=name,
        out_shape=[pltpu.SemaphoreType.DMA((ns,)), pltpu.SemaphoreType.DMA((ns,))] + [pltpu.HBM(b.shape, b.dtype) for b in bufs]
        + [jax.ShapeDtypeStruct((8, 128), F32)],
        in_specs=[HBM] * n_b + [pl.BlockSpec(memory_space=pl.ANY)] * len(after), out_specs=[SEM, SEM] + [HBM] * n_b + [VMEM_SPEC],
        input_output_aliases={i: 2 + i for i in range(n_b)},
        compiler_params=pltpu.CompilerParams(has_side_effects=EFFECT),
    )(*[pltpu.with_memory_space_constraint(b, pltpu.HBM) for b in bufs], *after)
    return res[0], res[1], list(res[2:2 + n_b]), res[-1]


def _unit_wait(unit, send_sems, recv_sems, bufs, after, name):
    n_i, n_b = len(unit["ins"]), len(bufs)

    def body(*refs):
        ss, rs = refs[n_b], refs[n_b + 1]
        for st, arrival in unit["copies"](refs[:n_i], refs[n_i:n_b], lambda k: (ss.at[k], rs.at[k])):
            st.wait_send()
            arrival.wait_recv()

    res = pl.pallas_call(
        body, name=name, out_shape=[pltpu.HBM(b.shape, b.dtype) for b in bufs],
        in_specs=[HBM] * n_b + [SEM, SEM] + [pl.BlockSpec(memory_space=pl.ANY)] * len(after), out_specs=[HBM] * n_b,
        input_output_aliases={i: i for i in range(n_b)}, compiler_params=pltpu.CompilerParams(has_side_effects=EFFECT),
    )(*bufs, send_sems, recv_sems, *after)
    return list(res[:n_i]), list(res[n_i:])


def _pair_add(g, land, core, name):
    _, rows, cols = g.shape
    half = rows // 2
    rb = _tile(half, 512, 16)
    nb = half // rb

    def body(c_ref, g_ref, l_ref, o_ref):
        o_ref[...] = (g_ref[...] + l_ref[...]).astype(BF16)

    return pl.pallas_call(
        body, name=name,
        grid_spec=pltpu.PrefetchScalarGridSpec(
            num_scalar_prefetch=1, grid=(4, nb),
            in_specs=[pl.BlockSpec((1, rb, cols), lambda s, i, c_ref: (s, c_ref[0] * nb + i, 0)),
                      pl.BlockSpec((1, rb, cols), lambda s, i, c_ref: (s, i, 0))],
            out_specs=pl.BlockSpec((1, rb, cols), lambda s, i, c_ref: (s, i, 0))),
        out_shape=jax.ShapeDtypeStruct((4, half, cols), BF16),
        compiler_params=_params(("parallel", "parallel")),
    )(core, g, land)


def _chip_sum(t, r, chip, name):
    _, half, cols = t.shape
    rb = _tile(half, 512, 16)

    def body(s_ref, t_ref, r_ref, o_ref):
        o_ref[...] = ((t_ref[0].astype(F32) + r_ref[0].astype(F32)) + r_ref[1].astype(F32)) + r_ref[2].astype(F32)

    return pl.pallas_call(
        body, name=name,
        grid_spec=pltpu.PrefetchScalarGridSpec(
            num_scalar_prefetch=1, grid=(half // rb,),
            in_specs=[pl.BlockSpec((1, rb, cols), lambda i, s_ref: (s_ref[0], i, 0)),
                      pl.BlockSpec((3, rb, cols), lambda i, s_ref: (0, i, 0))],
            out_specs=pl.BlockSpec((rb, cols), lambda i, s_ref: (i, 0))),
        out_shape=jax.ShapeDtypeStruct((half, cols), F32),
        compiler_params=_params(("parallel",)),
    )(chip, t, r)


def _adam_math(wv, gv, mv, vv):
    mn = ADAM_B1 * mv + (1.0 - ADAM_B1) * gv
    vn = ADAM_B2 * vv + (1.0 - ADAM_B2) * (gv * gv)
    m_hat = mn / (1.0 - ADAM_B1 ** ADAM_STEP)
    v_hat = vn / (1.0 - ADAM_B2 ** ADAM_STEP)
    return -ADAM_LR * (m_hat / (jnp.sqrt(v_hat) + ADAM_EPS) + ADAM_WD * wv), mn, vn


def _adamw_halves(wt, mt, vt, mine, theirs, core, name):
    _, rows, cols = wt.shape
    half = rows // 2
    rb = _tile(half, 256, 8)
    nb = half // rb

    def body(c_ref, w_ref, m_ref, v_ref, a_ref, b_ref, g_ref, d_ref, mo_ref, vo_ref):
        gv = jnp.where(pl.program_id(0) == c_ref[0], a_ref[...], b_ref[...])
        dl, mn, vn = _adam_math(w_ref[...], gv, m_ref[...], v_ref[...])
        g_ref[...] = gv
        d_ref[...] = dl
        mo_ref[...] = mn
        vo_ref[...] = vn

    full = pl.BlockSpec((None, rb, cols), lambda hf, i, c_ref: (0, hf * nb + i, 0))
    part = pl.BlockSpec((rb, cols), lambda hf, i, c_ref: (i, 0))
    return pl.pallas_call(
        body, name=name,
        grid_spec=pltpu.PrefetchScalarGridSpec(num_scalar_prefetch=1, grid=(2, nb), in_specs=[full, full, full, part, part],
                                               out_specs=[full] * 4),
        out_shape=[jax.ShapeDtypeStruct((1, rows, cols), F32)] * 4,
        compiler_params=_params(("parallel", "parallel")),
    )(core, wt, mt, vt, mine, theirs)


SG_REP = 144
SG_LOSS = 136
SG_W2, SG_CW = SG_REP, SG_REP + 4 * 16
SG_ROWS = SG_CW + 4 * 40
SP_ROWS = SG_REP + 16 + 40


def _mod_shard(c_all, ada_w_sh):
    def body(c_ref, w_ref, o_ref):
        cv = c_ref[...]
        o_ref[...] = _dg((cv * _sigmoid(cv)).astype(BF16), w_ref[...].astype(BF16), 1, 0)

    return pl.pallas_call(body, name="mod_shard", out_shape=jax.ShapeDtypeStruct((8, 1536), F32),
                          in_specs=[VMEM_SPEC, VMEM_SPEC], out_specs=VMEM_SPEC,
                          compiler_params=pltpu.CompilerParams(vmem_limit_bytes=VMEM_LIMIT))(c_all, ada_w_sh)


def _mod_select(mod_all, ada_b4):
    def body(m_ref, b_ref, o_ref):
        x, y, c = _me()
        me = 4 * x + 2 * y + c
        for sh in range(4):
            o_ref[sh] = m_ref[2 * sh, me] + b_ref[sh]

    return pl.pallas_call(body, name="mod_select", out_shape=jax.ShapeDtypeStruct((4, 12, 128), F32),
                          in_specs=[VMEM_SPEC, VMEM_SPEC], out_specs=VMEM_SPEC)(mod_all, ada_b4)


def _small_reduce(sg_all):
    def body(g_ref, o_ref):
        x, y, c = _me()
        s_me = 2 * x + y
        w2_rows = pl.ds(pl.multiple_of(SG_W2 + 16 * s_me, 8), 16)
        cw_rows = pl.ds(pl.multiple_of(SG_CW + 40 * s_me, 8), 40)
        a = g_ref[0, 0:SG_REP, :]
        b = g_ref[0, w2_rows, :]
        d = g_ref[0, cw_rows, :]
        for dev in range(1, 8):
            a = a + g_ref[dev, 0:SG_REP, :]
            b = b + g_ref[dev, w2_rows, :]
            d = d + g_ref[dev, cw_rows, :]
        o_ref[0:SG_REP, :] = a
        o_ref[SG_REP:SG_REP + 16, :] = b
        o_ref[SG_REP + 16:SP_ROWS, :] = d

    return pl.pallas_call(body, name="small_grad_reduce", out_shape=jax.ShapeDtypeStruct((SP_ROWS, 128), F32),
                          in_specs=[VMEM_SPEC], out_specs=VMEM_SPEC)(sg_all)


def _ada_grad(dmod_all, c_bc):
    def body(g_ref, c_ref, o_ref):
        x, y, c = _me()
        s_me = 2 * x + y
        for k in range(12):
            acc = jnp.zeros((D, 128), F32)
            for b in range(8):
                cv = c_ref[b]
                acc = acc + (cv * _sigmoid(cv)) * g_ref[s_me, k, b:b + 1, :]
            o_ref[:, k * 128:(k + 1) * 128] = acc

    return pl.pallas_call(body, name="ada_w_grad", out_shape=jax.ShapeDtypeStruct((D, 1536), F32),
                          in_specs=[VMEM_SPEC, VMEM_SPEC], out_specs=VMEM_SPEC,
                          compiler_params=pltpu.CompilerParams(vmem_limit_bytes=VMEM_LIMIT))(dmod_all, c_bc)


def _adamw(wt, g, m, v, name):
    rows, cols = wt.shape
    rb = _tile(rows, 256, 8)

    def fn(c, i, wv, gv, mv, vv):
        return _adam_math(wv, gv, mv, vv)

    return _rowcall(fn, [_rows(t, rb) for t in (wt, g, m, v)], [_orow(rows, cols, F32, rb)] * 3,
                    n_rows=rows, rb=rb, name=name)


def _pad_rows(t, rows):
    flat = t.reshape(-1)
    return jnp.pad(flat, (0, rows * 128 - flat.shape[0])).reshape(rows, 128)


SP_LAYOUT = (("ada_b", 48), ("norm1_w", 8), ("gla_gate_b", 8), ("gla_norm_w", 8), ("norm2_w", 8), ("conv_b", 48),
             ("final_norm_w", 8), (None, 8), ("gla_gate_w2", 16), ("conv_w", 40))


def _pack_small(d):
    return jnp.concatenate([jnp.zeros((rows, 128), F32) if n is None else _pad_rows(d[n].astype(F32), rows)
                            for n, rows in SP_LAYOUT], axis=0)


def _unpack_small(pk, shapes):
    out, off = {}, 0
    for n, rows in SP_LAYOUT:
        if n is not None:
            shp = shapes[n]
            out[n] = pk[off:off + rows].reshape(-1)[:math.prod(shp)].reshape(shp)
        off += rows
    return out


def kernel(x, c, positions, ada_w, ada_b, norm1_w, w_in, gla_gate_w2, gla_gate_b, gla_norm_w, w_gla_branch, w_attn_branch, w_out, norm2_w, w_up, conv_w, conv_b, w_down, final_norm_w, loss_target, m_ada_w, m_ada_b, m_norm1_w, m_w_in, m_gla_gate_w2, m_gla_gate_b, m_gla_norm_w, m_w_gla_branch, m_w_attn_branch, m_w_out, m_norm2_w, m_w_up, m_conv_w, m_conv_b, m_w_down, m_final_norm_w, v_ada_w, v_ada_b, v_norm1_w, v_w_in, v_gla_gate_w2, v_gla_gate_b, v_gla_norm_w, v_w_gla_branch, v_w_attn_branch, v_w_out, v_norm2_w, v_w_up, v_conv_w, v_conv_b, v_w_down, v_final_norm_w):
    s = x.shape[1]
    names = ("ada_w", "ada_b", "norm1_w", "w_in", "gla_gate_w2", "gla_gate_b", "gla_norm_w", "w_gla_branch", "w_attn_branch",
             "w_out", "norm2_w", "w_up", "conv_w", "conv_b", "w_down", "final_norm_w")
    wts = dict(zip(names, (ada_w, ada_b, norm1_w, w_in, gla_gate_w2, gla_gate_b, gla_norm_w, w_gla_branch, w_attn_branch,
                           w_out, norm2_w, w_up, conv_w, conv_b, w_down, final_norm_w)))
    ms = dict(zip(names, (m_ada_w, m_ada_b, m_norm1_w, m_w_in, m_gla_gate_w2, m_gla_gate_b, m_gla_norm_w, m_w_gla_branch,
                          m_w_attn_branch, m_w_out, m_norm2_w, m_w_up, m_conv_w, m_conv_b, m_w_down, m_final_norm_w)))
    vs = dict(zip(names, (v_ada_w, v_ada_b, v_norm1_w, v_w_in, v_gla_gate_w2, v_gla_gate_b, v_gla_norm_w, v_w_gla_branch,
                          v_w_attn_branch, v_w_out, v_norm2_w, v_w_up, v_conv_w, v_conv_b, v_w_down, v_final_norm_w)))

    pk0 = jnp.concatenate([_pad_rows(c, 8), _pad_rows(gla_gate_w2, 16), _pad_rows(conv_w, 40)], axis=0)
    sm_all = _allgather8(pk0, "gather_small")
    c_all = sm_all[:, 0:8, :].reshape(8, D)
    w2_full = sm_all[0::2, 8:24, :].transpose(1, 0, 2).reshape(GLA_LR, 512)
    cw_full = sm_all[0::2, 24:64, :].reshape(4, 40 * 128)[:, :3 * W_UP_SH].reshape(4, 3, W_UP_SH).transpose(1, 0, 2).reshape(3, 2 * D_FF)

    mod_sh = _mod_shard(c_all, ada_w[0])
    mod_all = _allgather8(mod_sh.reshape(96, 128), "gather_mod")

    w_sh = [wts[n].astype(BF16) for n in BIG]
    u_g0 = _u_gather_ici(w_sh, (0,))
    g0 = (u_g0,) + _unit_start(u_g0, "gather_w_in_start", after=[mod_all])
    mod = _mod_select(mod_all.reshape(8, 8, 12, 128) + g0[4][0, 0], ada_b.reshape(4, 12, 128)).reshape(6, D)

    core = lax.axis_index("c").astype(jnp.int32).reshape(1)
    chip = (2 * lax.axis_index("x") + lax.axis_index("y")).astype(jnp.int32)
    sm = dict(n1w=norm1_w, n2w=norm2_w, fnw=final_norm_w.reshape(1, D), gnw=gla_norm_w, gb=gla_gate_b,
              w2=jnp.pad(w2_full, ((0, 128 - GLA_LR), (0, 0))), cw=_ff_to_kernel(cw_full), cb=_ff_to_kernel(conv_b))
    loss, grad_x, halves, others, small, ts0 = _local_step(x[0], mod, positions.reshape(s, 1), loss_target[0], sm, w_sh,
                                                               g0, chip, core)

    dcw = _ff_from_kernel(small["cw"]).reshape(3, 4, W_UP_SH).transpose(1, 0, 2)
    dw2 = small["w2"][:GLA_LR].reshape(GLA_LR, 4, 128).transpose(1, 0, 2)
    sg = jnp.concatenate(
        [_pad_rows(small["dmod"], 48), _pad_rows(small["n1w"], 8), _pad_rows(small["gb"], 8), _pad_rows(small["gnw"], 8),
         _pad_rows(small["n2w"], 8), _pad_rows(_ff_from_kernel(small["cb"]), 48), _pad_rows(small["fnw"], 8), _pad_rows(loss, 8)]
        + [_pad_rows(dw2[k], 16) for k in range(4)] + [_pad_rows(dcw[k], 40) for k in range(4)], axis=0)
    sg_all = _allgather8(sg, "gather_small_grads")
    u_ex = _u_chip_exchange([ts0])
    pending = (u_ex,) + _unit_start(u_ex, "grad_exchange_w_in_start", after=[sg_all])
    sg_all = sg_all + pending[4][0, 0]
    g_small_pk = _small_reduce(sg_all)
    dmod_all = sg_all[:, 0:48, :].reshape(8, 4, 12, 128).transpose(1, 2, 0, 3)
    g_ada_w = _ada_grad(dmod_all, jnp.broadcast_to(c_all[:, :, None], (8, D, 128)))

    shapes = {n: wts[n].shape for n in names}
    g_small = _unpack_small(g_small_pk, shapes)
    grads = {"ada_w": g_ada_w.reshape(1, D, 1536), **g_small}
    deltas, new_m, new_v = {}, {}, {}
    for n, mine, theirs in zip(BIG[1:], halves, others):
        grads[n], deltas[n], new_m[n], new_v[n] = _adamw_halves(wts[n], ms[n], vs[n], mine, theirs, core, "adamw_" + n)
    shp = ada_w.shape
    d_, m_, v_ = _adamw(ada_w[0], g_ada_w, m_ada_w[0], v_ada_w[0], "adamw_ada_w")
    deltas["ada_w"], new_m["ada_w"], new_v["ada_w"] = d_.reshape(shp), m_.reshape(shp), v_.reshape(shp)
    d_, m_, v_ = _adamw(_pack_small(wts), g_small_pk, _pack_small(ms), _pack_small(vs), "adamw_small")
    for dst, pk in ((deltas, d_), (new_m, m_), (new_v, v_)):
        dst.update(_unpack_small(pk, shapes))

    [t0], [r0] = _unit_wait(*pending[:4], after=[d_, deltas["ada_w"], deltas["w_up"], deltas["w_down"]], name="grad_exchange_w_in_wait")
    half0 = _chip_sum(t0, r0, chip.reshape(1), "grad_chip_sum_w_in")
    [[oth0]] = _comm_call("grad_join_w_in", [_u_pair_join([half0])])
    grads["w_in"], deltas["w_in"], new_m["w_in"], new_v["w_in"] = _adamw_halves(w_in, m_w_in, v_w_in, half0, oth0, core, "adamw_w_in")

    return (g_small_pk[SG_LOSS, 0], grad_x.reshape(1, s, D), *[grads[n] for n in names], *[deltas[n] for n in names],
            *[new_m[n] for n in names], *[new_v[n] for n in names])
```

```python
import math

import jax
import jax.numpy as jnp
from jax import lax
from jax.experimental import pallas as pl
from jax.experimental.pallas import tpu as pltpu

F32, BF16 = jnp.float32, jnp.bfloat16
MESH = pl.DeviceIdType.MESH

D = 1024
EPS = 1e-6
GLA_H, GLA_DK, GLA_DV, GLA_LR = 4, 128, 256, 16
GLA_TAU = 16.0
GLA_CHUNK = 64
GLA_BLOCK = 512
ATT_GROUPS = ((128, 1), (512, 4), (2048, 16))
ATT_BLK = 128
ATT_HD = 64
ATT_W = 768
D_FF = 2816
ROPE_THETA = 10000.0
P_W = 7680
P_GV, P_GR, P_MA, P_MB, P_GQ, P_GK, P_AQ, P_AK, P_AV, P_LR = 0, 1024, 2048, 3072, 4096, 4608, 5120, 5888, 6656, 7424
W_IN = 7440
W_IN_SH, W_UP_SH, W_DOWN_SH = 1860, 1408, 704
VMEM_LIMIT = 56 * 1024 * 1024
ADAM_LR, ADAM_B1, ADAM_B2, ADAM_EPS, ADAM_WD, ADAM_STEP = 0.001, 0.9, 0.999, 1e-08, 0.01, 10
NEG = -1e30


def _tile(n, target, unit=128):
    best = None
    for t in range(unit, min(n, target) + 1, unit):
        if n % t == 0:
            best = t
    return best or n


def _params(sem):
    return pltpu.CompilerParams(dimension_semantics=sem, vmem_limit_bytes=VMEM_LIMIT)


def _dg(a, b, ca, cb):
    return lax.dot_general(a, b, (((ca,), (cb,)), ((), ())), preferred_element_type=F32)


def _sigmoid(v):
    return 1.0 / (1.0 + jnp.exp(-v))


def _ff_block(j):
    return (j % 2) * 2 + j // 2


def _mm(a, b, name, *, ta=False, tb=False, out_dtype=BF16, tm=1024, tn=1536, tk=1024, n_outer=True, comm=(),
        b_shards=False, o_shards=False):
    m = a.shape[1] if ta else a.shape[0]
    k = a.shape[0] if ta else a.shape[1]
    if b_shards:
        n = b.shape[1] if tb else 4 * W_UP_SH
        tn, tk = (tn, W_UP_SH) if tb else (W_UP_SH, tk)
    else:
        n = b.shape[0] if tb else b.shape[1]
    if o_shards:
        tn = W_UP_SH
    tm, tn, tk = _tile(m, tm), _tile(n, tn), _tile(k, tk)
    nm, nn, nk = m // tm, n // tn, k // tk
    in_out = out_dtype == F32
    c_ins, c_outs, c_alias, c_scratch = _carry(comm, 2, 1)

    def body(a_ref, b_ref, *rest):
        ci, o_ref, co = rest[:len(c_ins)], rest[len(c_ins)], rest[len(c_ins) + 1:len(c_ins) + 1 + len(c_outs)]
        scr = rest[len(c_ins) + 1 + len(c_outs):]
        kk = pl.program_id(2)
        if comm:
            step = (pl.program_id(0) * (nm if n_outer else nn) + pl.program_id(1)) * nk + kk

            @pl.when(step == 0)
            def _():
                _comm_phase(comm, ci, co, scr[-2], scr[-1], True)

        _mm_step(a_ref, b_ref, o_ref, scr, kk)
        if comm:
            @pl.when(step == nm * nn * nk - 1)
            def _():
                _comm_phase(comm, ci, co, scr[-2], scr[-1], False)

    def _mm_step(a_ref, b_ref, o_ref, scr, kk):
        p = _dg(a_ref[...].astype(BF16), b_ref[...].astype(BF16), 0 if ta else 1, 1 if tb else 0)
        if nk == 1:
            o_ref[...] = p.astype(o_ref.dtype)
        else:
            acc = o_ref if in_out else scr[0]

            @pl.when(kk == 0)
            def _():
                acc[...] = p

            @pl.when(kk > 0)
            def _():
                acc[...] += p

            if not in_out:
                @pl.when(kk == nk - 1)
                def _():
                    o_ref[...] = acc[...].astype(o_ref.dtype)

    if n_outer:
        ij = lambda g0, g1: (g1, g0)
        grid = (nn, nm, nk)
    else:
        ij = lambda g0, g1: (g0, g1)
        grid = (nm, nn, nk)
    a_map = (lambda g0, g1, kk: (kk, ij(g0, g1)[0])) if ta else (lambda g0, g1, kk: (ij(g0, g1)[0], kk))
    if b_shards and tb:
        b_spec = pl.BlockSpec((None, tn, tk), lambda g0, g1, kk: (_ff_block(kk), ij(g0, g1)[1], 0))
    elif b_shards:
        b_spec = pl.BlockSpec((None, tk, tn), lambda g0, g1, kk: (_ff_block(ij(g0, g1)[1]), kk, 0))
    elif tb:
        b_spec = pl.BlockSpec((tn, tk), lambda g0, g1, kk: (ij(g0, g1)[1], kk))
    else:
        b_spec = pl.BlockSpec((tk, tn), lambda g0, g1, kk: (kk, ij(g0, g1)[1]))
    if o_shards:
        o_spec = pl.BlockSpec((None, tm, tn), lambda g0, g1, kk: (_ff_block(ij(g0, g1)[1]), ij(g0, g1)[0], 0))
        o_shape = jax.ShapeDtypeStruct((4, m, W_UP_SH), out_dtype)
    else:
        o_spec = pl.BlockSpec((tm, tn), lambda g0, g1, kk: ij(g0, g1))
        o_shape = jax.ShapeDtypeStruct((m, n), out_dtype)
    res = pl.pallas_call(
        body, name=name, grid=grid,
        in_specs=[pl.BlockSpec((tk, tm) if ta else (tm, tk), a_map), b_spec] + [HBM] * len(c_ins),
        out_specs=[o_spec] + [HBM] * len(c_outs),
        out_shape=[o_shape] + c_outs,
        scratch_shapes=([] if (in_out or nk == 1) else [pltpu.VMEM((tm, tn), F32)]) + c_scratch,
        input_output_aliases=c_alias,
        compiler_params=_params(("arbitrary",) * 3 if comm else ("parallel", "parallel", "arbitrary")),
    )(a, b, *c_ins)
    return (res[0], _split_units(comm, res[1:])) if comm else res[0]


def _rows(arr, rb, w=None, j=0):
    w = arr.shape[1] if w is None else w
    if callable(j):
        return arr, pl.BlockSpec((rb, w), lambda c, i: (i, j(c)))
    return arr, pl.BlockSpec((rb, w), lambda c, i: (i, j))


def _full(arr, w=None, j=0):
    w = arr.shape[1] if w is None else w
    if callable(j):
        return arr, pl.BlockSpec((arr.shape[0], w), lambda c, i: (0, j(c)))
    return arr, pl.BlockSpec((arr.shape[0], w), lambda c, i: (0, j))


def _halo(arr, rb, hb, w, j, before):
    per = rb // hb
    last = arr.shape[0] // hb - 1
    if before:
        rmap = lambda i: jnp.maximum(i * per - 1, 0)
    else:
        rmap = lambda i: jnp.minimum((i + 1) * per, last)
    return arr, pl.BlockSpec((hb, w), lambda c, i: (rmap(i), j(c) if callable(j) else j))


def _rowcall(fn, ins, outs, *, n_rows, rb, name, ncol=1, into=None, after=()):
    n_in = len(ins)
    nr = n_rows // rb
    unread = ([] if into is None else [into[0]]) + list(after)
    n_skip = len(unread)

    def body(*refs):
        c, i = pl.program_id(0), pl.program_id(1)
        res = fn(c, i, *[r[...] for r in refs[:n_in]])
        for val, spec, o_ref in zip(res, outs, refs[n_in + n_skip:]):
            if spec[2] == "row":
                o_ref[...] = val.astype(o_ref.dtype)
            else:
                @pl.when(i == 0)
                def _(o_ref=o_ref, val=val):
                    o_ref[...] = val.astype(o_ref.dtype)

                @pl.when(i > 0)
                def _(o_ref=o_ref, val=val):
                    o_ref[...] += val.astype(o_ref.dtype)

    out_specs = []
    for shape, dt, kind, block, col in outs:
        if kind == "row":
            out_specs.append(pl.BlockSpec(block, lambda c, i, col=col: (i, col(c))))
        else:
            out_specs.append(pl.BlockSpec(block, lambda c, i, col=col: (0, col(c))))
    return pl.pallas_call(
        body, name=name, grid=(ncol, nr),
        in_specs=[s for _, s in ins] + [pl.BlockSpec(memory_space=pl.ANY)] * n_skip, out_specs=out_specs,
        out_shape=[jax.ShapeDtypeStruct(o[0], o[1]) for o in outs],
        input_output_aliases={} if into is None else {n_in: into[1]},
        compiler_params=_params(("parallel", "arbitrary")),
    )(*[a for a, _ in ins], *unread)


def _orow(n_rows, w, dt, rb, bw=None, col=lambda c: 0):
    return ((n_rows, w), dt, "row", (rb, bw or w), col)


def _oacc(r, w, bw=None, col=lambda c: 0):
    return ((r, w), F32, "acc", (r, bw or w), col)


def _csum(v):
    return jnp.sum(v, axis=0, keepdims=True)


def _rms(v):
    return lax.rsqrt(jnp.mean(v * v, axis=-1, keepdims=True) + EPS)


def _norm_bwd(xv, dh, w, scale):
    r = _rms(xv)
    xh = xv * r
    dxh = dh * (w * (1.0 + scale))
    dx = r * (dxh - xh * jnp.mean(dxh * xh, axis=-1, keepdims=True))
    t = dh * xh
    return dx, _csum(dh), _csum(t * w), _csum(t * (1.0 + scale))


def _rope_tables(pos_col, invf, s):
    def fn(c, i, pos, f):
        ang = pos.astype(F32) * f
        lane = lax.broadcasted_iota(jnp.int32, ang.shape, 1)
        sign = jnp.where((lane % ATT_HD) < ATT_HD // 2, -1.0, 1.0)
        return jnp.cos(ang), jnp.sin(ang) * sign

    rb = 512
    return _rowcall(fn, [_rows(pos_col, rb), _full(invf)], [_orow(s, 128, F32, rb), _orow(s, 128, F32, rb)],
                    n_rows=s, rb=rb, name="rope_tables")


def _swap_halves(t):
    n = t.shape[1]
    lane = lax.broadcasted_iota(jnp.int32, t.shape, 1)
    return jnp.where((lane % ATT_HD) < ATT_HD // 2, pltpu.roll(t, n - 32, 1), pltpu.roll(t, 32, 1))


def _rope_apply(t, cos, sin_signed, inverse):
    cw = jnp.concatenate([cos] * (t.shape[1] // 128), axis=1)
    sw = jnp.concatenate([sin_signed] * (t.shape[1] // 128), axis=1)
    if inverse:
        sw = -sw
    return t * cw + _swap_halves(t) * sw


DIL_ROWS = 512


def _to_dilated(scr, val, out_ref, r):
    if r == 1:
        out_ref[...] = val.astype(out_ref.dtype)
        return
    n = val.shape[0] // r
    for hh in range(2):
        scr[hh] = val[:, hh * 128:(hh + 1) * 128]
        for pr in range(r):
            out_ref[:, pr * 256 + hh * 128:pr * 256 + (hh + 1) * 128] = scr[hh, pl.ds(pr, n, stride=r), :].astype(out_ref.dtype)


def _from_dilated(scr, in_ref, r):
    if r == 1:
        return in_ref[...].astype(F32)
    n = in_ref.shape[0]
    for hh in range(2):
        for pr in range(r):
            scr[hh, pl.ds(pr, n, stride=r), :] = in_ref[:, pr * 256 + hh * 128:pr * 256 + (hh + 1) * 128].astype(F32)
    return jnp.concatenate([scr[0], scr[1]], axis=1)


def _dil_spec(r):
    return pl.BlockSpec((DIL_ROWS // r, r * 256), lambda i: (i, 0))


def _dil_shape(s, r, dt):
    return jax.ShapeDtypeStruct((s // r, r * 256), dt)


_DIL_SCRATCH = [pltpu.VMEM((2, DIL_ROWS, 128), F32)]
_RS = tuple(r for _, r in ATT_GROUPS)


def _rope_fwd(p, cos_t, sin_t, s):
    def body(*refs):
        ins, cs, sn, outs, scr = refs[:9], refs[9][...], refs[10][...], refs[11:20], refs[20]
        for t in range(3):
            for g, r in enumerate(_RS):
                val = ins[3 * t + g][...].astype(F32)
                _to_dilated(scr, _rope_apply(val, cs, sn, False) if t < 2 else val, outs[3 * t + g], r)

    res = pl.pallas_call(
        body, name="rope", grid=(s // DIL_ROWS,),
        in_specs=[pl.BlockSpec((DIL_ROWS, 256), lambda i, c=base // 256 + g: (i, c)) for base in (P_AQ, P_AK, P_AV) for g in range(3)]
        + [pl.BlockSpec((DIL_ROWS, 128), lambda i: (i, 0))] * 2,
        out_specs=[_dil_spec(r) for _ in range(3) for r in _RS],
        out_shape=[_dil_shape(s, r, BF16) for _ in range(3) for r in _RS],
        scratch_shapes=_DIL_SCRATCH, compiler_params=_params(("parallel",)),
    )(*([p] * 9), cos_t, sin_t)
    return res[0:3], res[3:6], res[6:9]


def _attn_combine(att, s):
    def body(o0, o1, o2, l0, l1, l2, o_ref, lse_ref, od1, od2, ld1, ld2, scr):
        ov = [_from_dilated(scr, ref, r) for ref, r in zip((o0, o1, o2), _RS)]
        lv = [_from_dilated(scr, ref, r) for ref, r in zip((l0, l1, l2), _RS)]
        mx = jnp.maximum(jnp.maximum(lv[0], lv[1]), lv[2])
        ev = [jnp.exp(l - mx) for l in lv]
        z = ev[0] + ev[1] + ev[2]
        o = ((ev[0] * ov[0] + ev[1] * ov[1] + ev[2] * ov[2]) / z).astype(BF16)
        lse = mx + jnp.log(z)
        o_ref[...] = o
        lse_ref[...] = lse
        for ref, r in zip((od1, od2), _RS[1:]):
            _to_dilated(scr, o.astype(F32), ref, r)
        for ref, r in zip((ld1, ld2), _RS[1:]):
            _to_dilated(scr, lse, ref, r)

    return pl.pallas_call(
        body, name="attn_combine", grid=(s // DIL_ROWS,),
        in_specs=[_dil_spec(r) for r in _RS] * 2,
        out_specs=[_dil_spec(1)] * 2 + [_dil_spec(r) for r in _RS[1:]] * 2,
        out_shape=[_dil_shape(s, 1, BF16), _dil_shape(s, 1, F32)] + [_dil_shape(s, r, BF16) for r in _RS[1:]]
        + [_dil_shape(s, r, F32) for r in _RS[1:]],
        scratch_shapes=_DIL_SCRATCH, compiler_params=_params(("parallel",)),
    )(*[a[0] for a in att], *[a[1] for a in att])


def _dilate(t, s):
    def body(t_ref, o1, o2, scr):
        val = t_ref[...].astype(F32)
        for ref, r in zip((o1, o2), _RS[1:]):
            _to_dilated(scr, val, ref, r)

    return pl.pallas_call(
        body, name="attn_dilate", grid=(s // DIL_ROWS,), in_specs=[_dil_spec(1)], out_specs=[_dil_spec(r) for r in _RS[1:]],
        out_shape=[_dil_shape(s, r, t.dtype) for r in _RS[1:]], scratch_shapes=_DIL_SCRATCH, compiler_params=_params(("parallel",)),
    )(t)


def _rope_bwd(datt, d_glr, dp, cos_t, sin_t, s):
    tail = P_W - P_AQ

    def body(*refs):
        ins, cs, sn, glr, o_ref, scr = refs[:9], refs[9][...], refs[10][...], refs[11], refs[13], refs[14]
        for t in range(3):
            for g, r in enumerate(_RS):
                val = _from_dilated(scr, ins[3 * t + g], r)
                o_ref[:, t * ATT_W + g * 256:t * ATT_W + (g + 1) * 256] = (_rope_apply(val, cs, sn, True) if t < 2 else val).astype(BF16)
        o_ref[:, 3 * ATT_W:3 * ATT_W + 128] = glr[...]
        o_ref[:, 3 * ATT_W + 128:] = jnp.zeros((DIL_ROWS, tail - 3 * ATT_W - 128), BF16)

    return pl.pallas_call(
        body, name="rope_bwd", grid=(s // DIL_ROWS,),
        in_specs=[_dil_spec(r) for _ in range(3) for r in _RS] + [pl.BlockSpec((DIL_ROWS, 128), lambda i: (i, 0))] * 3
        + [pl.BlockSpec(memory_space=pl.ANY)],
        out_specs=pl.BlockSpec((DIL_ROWS, tail), lambda i: (i, P_AQ // tail)),
        out_shape=jax.ShapeDtypeStruct((s, P_W), BF16), input_output_aliases={12: 0},
        scratch_shapes=_DIL_SCRATCH, compiler_params=_params(("parallel",)),
    )(*[datt[g][t] for t in range(3) for g in range(3)], cos_t, sin_t, d_glr, dp)


def _tri_dot(tri, t):
    tb = tri.astype(BF16)
    hi = t.astype(BF16)
    r1 = t - hi.astype(F32)
    mid = r1.astype(BF16)
    lo = (r1 - mid.astype(F32)).astype(BF16)
    return _dg(tb, hi, 1, 0) + _dg(tb, mid, 1, 0) + _dg(tb, lo, 1, 0)


def _gla_decays(la_c, tri):
    b = _tri_dot(tri, la_c)
    row = lax.broadcasted_iota(jnp.int32, b.shape, 0)
    bmid = jnp.sum(jnp.where(row == GLA_CHUNK // 2 - 1, b, 0.0), axis=0, keepdims=True)
    blast = jnp.sum(jnp.where(row == GLA_CHUNK - 1, b, 0.0), axis=0, keepdims=True)
    return b, bmid, blast


def _gla_fwd(p, la, s, comm=()):
    tb, ch = GLA_BLOCK, GLA_CHUNK
    nb, nc = s // tb, tb // ch
    scale = GLA_DK ** -0.5
    c_ins, c_outs, c_alias, c_scratch = _carry(comm, 4, 2)

    def body(q_ref, k_ref, v_ref, la_ref, *rest):
        ci, (o_ref, st_ref) = rest[:len(c_ins)], rest[len(c_ins):len(c_ins) + 2]
        co, state = rest[len(c_ins) + 2:len(c_ins) + 2 + len(c_outs)], rest[len(c_ins) + 2 + len(c_outs)]
        step = pl.program_id(0)
        if comm:
            @pl.when(step == 0)
            def _():
                _comm_phase(comm, ci, co, rest[-2], rest[-1], True)

        _gla_fwd_step(q_ref, k_ref, v_ref, la_ref, o_ref, st_ref, state)
        if comm:
            @pl.when(step == nb - 1)
            def _():
                _comm_phase(comm, ci, co, rest[-2], rest[-1], False)

    def _gla_fwd_step(q_ref, k_ref, v_ref, la_ref, o_ref, st_ref, state):
        @pl.when(pl.program_id(0) == 0)
        def _():
            state[...] = jnp.zeros_like(state)

        ri = lax.broadcasted_iota(jnp.int32, (ch, ch), 0)
        ci = lax.broadcasted_iota(jnp.int32, (ch, ch), 1)
        causal = ci <= ri
        tri = causal.astype(F32)

        def chunk(c, carry):
            sl = pl.ds(pl.multiple_of(c * ch, ch), ch)
            b, bmid, blast = _gla_decays(la_ref[sl, :], tri)
            q = q_ref[sl, :].astype(F32) * scale
            k = k_ref[sl, :].astype(F32)
            v = v_ref[sl, :]
            qgt = (q * jnp.exp(b)).astype(BF16)
            qgn = (q * jnp.exp(b - bmid)).astype(BF16)
            kgn = (k * jnp.exp(bmid - b)).astype(BF16)
            kd = (k * jnp.exp(blast - b)).astype(BF16)
            dec = jnp.exp(blast)
            sts = [state[h] for h in range(GLA_H)]
            outs, news = [], []
            for h in range(GLA_H):
                hk, hv = slice(h * GLA_DK, (h + 1) * GLA_DK), slice(h * GLA_DV, (h + 1) * GLA_DV)
                a = jnp.where(causal, _dg(qgn[:, hk], kgn[:, hk], 1, 1), 0.0)
                outs.append(_dg(a.astype(BF16), v[:, hv], 1, 0) + _dg(qgt[:, hk], sts[h].astype(BF16), 1, 1))
                news.append(dec[:, hk] * sts[h] + _dg(v[:, hv], kd[:, hk], 0, 0))
            for h in range(GLA_H):
                st_ref[h, c] = sts[h]
                state[h] = news[h]
            o_ref[sl, :] = jnp.concatenate(outs, axis=1)
            return carry

        lax.fori_loop(0, nc, chunk, 0, unroll=4)

    hw = GLA_H * GLA_DK
    res = pl.pallas_call(
        body, name="gla_fwd", grid=(nb,),
        in_specs=[pl.BlockSpec((tb, hw), lambda t: (t, P_GQ // hw)),
                  pl.BlockSpec((tb, hw), lambda t: (t, P_GK // hw)),
                  pl.BlockSpec((tb, GLA_H * GLA_DV), lambda t: (t, P_GV // (GLA_H * GLA_DV))),
                  pl.BlockSpec((tb, hw), lambda t: (t, 0))] + [HBM] * len(c_ins),
        out_specs=[pl.BlockSpec((tb, GLA_H * GLA_DV), lambda t: (t, 0)),
                   pl.BlockSpec((GLA_H, nc, GLA_DV, GLA_DK), lambda t: (0, t, 0, 0))] + [HBM] * len(c_outs),
        out_shape=[jax.ShapeDtypeStruct((s, GLA_H * GLA_DV), F32),
                   jax.ShapeDtypeStruct((GLA_H, s // ch, GLA_DV, GLA_DK), F32)] + c_outs,
        scratch_shapes=[pltpu.VMEM((GLA_H, GLA_DV, GLA_DK), F32)] + c_scratch,
        input_output_aliases=c_alias,
        compiler_params=_params(("arbitrary",)),
    )(p, p, p, la, *c_ins)
    return res[0], res[1], _split_units(comm, res[2:])


def _gla_bwd(p, la, states, do, s, dp, comm=()):
    tb, ch = GLA_BLOCK, GLA_CHUNK
    nb, nc = s // tb, tb // ch
    scale = GLA_DK ** -0.5
    c_ins, c_outs, c_alias, c_scratch = _carry(comm, 7, 4)

    def body(q_ref, k_ref, v_ref, la_ref, st_ref, do_ref, dp_in, *rest):
        ci, outs = rest[:len(c_ins)], rest[len(c_ins):len(c_ins) + 4]
        co, dstate = rest[len(c_ins) + 4:len(c_ins) + 4 + len(c_outs)], rest[len(c_ins) + 4 + len(c_outs)]
        step = pl.program_id(0)
        if comm:
            @pl.when(step == 0)
            def _():
                _comm_phase(comm, ci, co, rest[-2], rest[-1], True)

        _gla_bwd_step(q_ref, k_ref, v_ref, la_ref, st_ref, do_ref, *outs, dstate)
        if comm:
            @pl.when(step == nb - 1)
            def _():
                _comm_phase(comm, ci, co, rest[-2], rest[-1], False)

    def _gla_bwd_step(q_ref, k_ref, v_ref, la_ref, st_ref, do_ref, dq_ref, dk_ref, dv_ref, dla_ref, dstate):
        @pl.when(pl.program_id(0) == 0)
        def _():
            dstate[...] = jnp.zeros_like(dstate)

        ri = lax.broadcasted_iota(jnp.int32, (ch, ch), 0)
        ci = lax.broadcasted_iota(jnp.int32, (ch, ch), 1)
        causal = ci <= ri
        tri = causal.astype(F32)
        tri_t = (ci >= ri).astype(F32)

        def chunk(cc, carry):
            c = nc - 1 - cc
            sl = pl.ds(pl.multiple_of(c * ch, ch), ch)
            b, bmid, blast = _gla_decays(la_ref[sl, :], tri)
            q = q_ref[sl, :].astype(F32) * scale
            k = k_ref[sl, :].astype(F32)
            v = v_ref[sl, :]
            e_b, e_qn, e_kn, e_kd = jnp.exp(b), jnp.exp(b - bmid), jnp.exp(bmid - b), jnp.exp(blast - b)
            dec = jnp.exp(blast)
            qgt, qgn, kgn, kd = q * e_b, q * e_qn, k * e_kn, k * e_kd
            qgt_b, qgn_b, kgn_b, kd_b = qgt.astype(BF16), qgn.astype(BF16), kgn.astype(BF16), kd.astype(BF16)
            do_b = do_ref[sl, :].astype(BF16)
            st0s = [st_ref[h, c] for h in range(GLA_H)]
            dsts = [dstate[h] for h in range(GLA_H)]
            dqgn, dqgt, dkgn, dkd, dvs, ddec, news = [], [], [], [], [], [], []
            for h in range(GLA_H):
                hk, hv = slice(h * GLA_DK, (h + 1) * GLA_DK), slice(h * GLA_DV, (h + 1) * GLA_DV)
                dst_b = dsts[h].astype(BF16)
                a = jnp.where(causal, _dg(qgn_b[:, hk], kgn_b[:, hk], 1, 1), 0.0).astype(BF16)
                da = jnp.where(causal, _dg(do_b[:, hv], v[:, hv], 1, 1), 0.0).astype(BF16)
                dqgn.append(_dg(da, kgn_b[:, hk], 1, 0))
                dqgt.append(_dg(do_b[:, hv], st0s[h].astype(BF16), 1, 0))
                dkgn.append(_dg(da, qgn_b[:, hk], 0, 0))
                dvs.append(_dg(a, do_b[:, hv], 0, 0) + _dg(kd_b[:, hk], dst_b, 1, 1))
                dkd.append(_dg(v[:, hv], dst_b, 1, 0))
                ddec.append(jnp.sum(st0s[h] * dsts[h], axis=0, keepdims=True))
                news.append(dec[:, hk] * dsts[h] + _dg(do_b[:, hv], qgt_b[:, hk], 0, 0))
            for h in range(GLA_H):
                dstate[h] = news[h]
            cat = lambda parts: jnp.concatenate(parts, axis=1)
            dqgn, dqgt, dkgn, dkd, ddec = cat(dqgn), cat(dqgt), cat(dkgn), cat(dkd), cat(ddec)
            dq_ref[sl, :] = (scale * (dqgn * e_qn + dqgt * e_b)).astype(dq_ref.dtype)
            dk_ref[sl, :] = (dkgn * e_kn + dkd * e_kd).astype(dk_ref.dtype)
            dv_ref[sl, :] = cat(dvs).astype(dv_ref.dtype)
            db = dqgn * qgn + dqgt * qgt - dkgn * kgn - dkd * kd
            extra = jnp.sum(dkd * kd, axis=0, keepdims=True) + ddec * dec
            dla_ref[sl, :] = _tri_dot(tri_t, db) + extra
            return carry

        lax.fori_loop(0, nc, chunk, 0, unroll=2)

    rev = lambda t: nb - 1 - t
    hw, vw = GLA_H * GLA_DK, GLA_H * GLA_DV
    res = pl.pallas_call(
        body, name="gla_bwd", grid=(nb,),
        in_specs=[pl.BlockSpec((tb, hw), lambda t: (rev(t), P_GQ // hw)),
                  pl.BlockSpec((tb, hw), lambda t: (rev(t), P_GK // hw)),
                  pl.BlockSpec((tb, vw), lambda t: (rev(t), P_GV // vw)),
                  pl.BlockSpec((tb, hw), lambda t: (rev(t), 0)),
                  pl.BlockSpec((GLA_H, nc, GLA_DV, GLA_DK), lambda t: (0, rev(t), 0, 0)),
                  pl.BlockSpec((tb, vw), lambda t: (rev(t), 0)), pl.BlockSpec(memory_space=pl.ANY)] + [HBM] * len(c_ins),
        out_specs=[pl.BlockSpec((tb, hw), lambda t: (rev(t), 0)),
                   pl.BlockSpec((tb, hw), lambda t: (rev(t), 0)),
                   pl.BlockSpec((tb, vw), lambda t: (rev(t), P_GV // vw)),
                   pl.BlockSpec((tb, hw), lambda t: (rev(t), 0))] + [HBM] * len(c_outs),
        out_shape=[jax.ShapeDtypeStruct((s, hw), BF16),
                   jax.ShapeDtypeStruct((s, hw), BF16),
                   jax.ShapeDtypeStruct((s, P_W), BF16),
                   jax.ShapeDtypeStruct((s, hw), F32)] + c_outs,
        scratch_shapes=[pltpu.VMEM((GLA_H, GLA_DV, GLA_DK), F32)] + c_scratch,
        input_output_aliases={6: 2, **c_alias},
        compiler_params=_params(("arbitrary",)),
    )(p, p, p, la, states, do, dp, *c_ins)
    return res[0], res[1], res[2], res[3], _split_units(comm, res[4:])


def _head_masks():
    lane = lax.broadcasted_iota(jnp.int32, (1, 4 * ATT_HD), 1)
    return [(lane >= h * ATT_HD) & (lane < (h + 1) * ATT_HD) for h in range(4)]


def _attn_fwd(qv, kv, pv, g, r, s):
    ln = s // r
    nblk = ln // ATT_BLK
    qcol = lambda pr: pr
    vcol = qcol
    prev = lambda n: jnp.maximum(n - 1, 0)

    def body(q_ref, kp_ref, kc_ref, vp_ref, vc_ref, o_ref, lse_ref):
        has_prev = pl.program_id(1) > 0
        ri = lax.broadcasted_iota(jnp.int32, (ATT_BLK, ATT_BLK), 0)
        ci = lax.broadcasted_iota(jnp.int32, (ATT_BLK, ATT_BLK), 1)
        m_cur = ci <= ri
        m_prev = (ci >= ri) & has_prev
        q, kp, kc, vp, vc = q_ref[...], kp_ref[...], kc_ref[...], vp_ref[...], vc_ref[...]
        o = jnp.zeros((ATT_BLK, 256), F32)
        lse = jnp.zeros((ATT_BLK, 256), F32)
        for hm in _head_masks():
            qm = jnp.where(hm, q, jnp.zeros_like(q))
            sc = jnp.where(m_cur, _dg(qm, kc, 1, 1) * 0.125, NEG)
            sp = jnp.where(m_prev, _dg(qm, kp, 1, 1) * 0.125, NEG)
            mx = jnp.maximum(jnp.max(sc, axis=1, keepdims=True), jnp.max(sp, axis=1, keepdims=True))
            pc, pp = jnp.exp(sc - mx), jnp.exp(sp - mx)
            den = jnp.sum(pc, axis=1, keepdims=True) + jnp.sum(pp, axis=1, keepdims=True)
            oh = (_dg(pc.astype(BF16), vc, 1, 0) + _dg(pp.astype(BF16), vp, 1, 0)) / den
            o = jnp.where(hm, oh, o)
            lse = jnp.where(hm, mx + jnp.log(den), lse)
        o_ref[...] = o.astype(o_ref.dtype)
        lse_ref[...] = lse

    blk = (ATT_BLK, 256)
    o, lse = pl.pallas_call(
        body, name=f"attn_fwd_{g}", grid=(r, nblk),
        in_specs=[pl.BlockSpec(blk, lambda pr, n: (n, qcol(pr))),
                  pl.BlockSpec(blk, lambda pr, n: (prev(n), qcol(pr))),
                  pl.BlockSpec(blk, lambda pr, n: (n, qcol(pr))),
                  pl.BlockSpec(blk, lambda pr, n: (prev(n), vcol(pr))),
                  pl.BlockSpec(blk, lambda pr, n: (n, vcol(pr)))],
        out_specs=[pl.BlockSpec(blk, lambda pr, n: (n, pr)), pl.BlockSpec(blk, lambda pr, n: (n, pr))],
        out_shape=[jax.ShapeDtypeStruct((ln, r * 256), BF16), jax.ShapeDtypeStruct((ln, r * 256), F32)],
        compiler_params=_params(("parallel", "parallel")),
    )(qv, kv, kv, pv, pv)
    return o, lse


def _attn_bwd(qv, kv, pv, dov, ov, lv, g, r, s):
    ln = s // r
    nblk = ln // ATT_BLK
    qcol = lambda pr: pr
    vcol = qcol
    prev = lambda n: jnp.maximum(n - 1, 0)
    nxt = lambda n: jnp.minimum(n + 1, nblk - 1)

    def body(qc_ref, qn_ref, kp_ref, kc_ref, vp_ref, vc_ref, doc_ref, don_ref, oc_ref, on_ref, lc_ref, ln_ref,
             dq_ref, dk_ref, dv_ref):
        n = pl.program_id(1)
        has_prev, has_next = n > 0, n < nblk - 1
        ri = lax.broadcasted_iota(jnp.int32, (ATT_BLK, ATT_BLK), 0)
        ci = lax.broadcasted_iota(jnp.int32, (ATT_BLK, ATT_BLK), 1)
        m_cur = ci <= ri
        m_prev = (ci >= ri) & has_prev
        m_next = (ci >= ri) & has_next
        qc, qn, kp, kc, vp, vc = qc_ref[...], qn_ref[...], kp_ref[...], kc_ref[...], vp_ref[...], vc_ref[...]
        doc, don = doc_ref[...], don_ref[...]
        pc_full = doc.astype(F32) * oc_ref[...].astype(F32)
        pn_full = don.astype(F32) * on_ref[...].astype(F32)
        lc, lnx = lc_ref[...], ln_ref[...]
        dq = jnp.zeros((ATT_BLK, 256), F32)
        dk = jnp.zeros((ATT_BLK, 256), F32)
        dv = jnp.zeros((ATT_BLK, 256), F32)
        zb = jnp.zeros_like(qc)
        for hm in _head_masks():
            qcm, qnm = jnp.where(hm, qc, zb), jnp.where(hm, qn, zb)
            docm, donm = jnp.where(hm, doc, zb), jnp.where(hm, don, zb)
            lse_c = jnp.max(jnp.where(hm, lc, NEG), axis=1, keepdims=True)
            lse_n = jnp.max(jnp.where(hm, lnx, NEG), axis=1, keepdims=True)
            del_c = jnp.sum(jnp.where(hm, pc_full, 0.0), axis=1, keepdims=True)
            del_n = jnp.sum(jnp.where(hm, pn_full, 0.0), axis=1, keepdims=True)
            pr_ = jnp.where(m_cur, jnp.exp(_dg(qcm, kc, 1, 1) * 0.125 - lse_c), 0.0)
            ds = (pr_ * (_dg(docm, vc, 1, 1) - del_c) * 0.125).astype(BF16)
            dqh = _dg(ds, kc, 1, 0)
            dkh = _dg(ds, qc, 0, 0)
            dvh = _dg(pr_.astype(BF16), doc, 0, 0)
            pr_ = jnp.where(m_prev, jnp.exp(_dg(qcm, kp, 1, 1) * 0.125 - lse_c), 0.0)
            ds = (pr_ * (_dg(docm, vp, 1, 1) - del_c) * 0.125).astype(BF16)
            dqh = dqh + _dg(ds, kp, 1, 0)
            pr_ = jnp.where(m_next, jnp.exp(_dg(qnm, kc, 1, 1) * 0.125 - lse_n), 0.0)
            ds = (pr_ * (_dg(donm, vc, 1, 1) - del_n) * 0.125).astype(BF16)
            dkh = dkh + _dg(ds, qn, 0, 0)
            dvh = dvh + _dg(pr_.astype(BF16), don, 0, 0)
            dq = jnp.where(hm, dqh, dq)
            dk = jnp.where(hm, dkh, dk)
            dv = jnp.where(hm, dvh, dv)
        dq_ref[...] = dq.astype(dq_ref.dtype)
        dk_ref[...] = dk.astype(dk_ref.dtype)
        dv_ref[...] = dv.astype(dv_ref.dtype)

    blk = (ATT_BLK, 256)
    cur = lambda col: pl.BlockSpec(blk, lambda pr, n: (n, col(pr)))
    prv = lambda col: pl.BlockSpec(blk, lambda pr, n: (prev(n), col(pr)))
    nx = lambda col: pl.BlockSpec(blk, lambda pr, n: (nxt(n), col(pr)))
    own = lambda pr: pr
    outs = pl.pallas_call(
        body, name=f"attn_bwd_{g}", grid=(r, nblk),
        in_specs=[cur(qcol), nx(qcol), prv(qcol), cur(qcol), prv(vcol), cur(vcol),
                  cur(own), nx(own), cur(own), nx(own), cur(own), nx(own)],
        out_specs=[cur(own), cur(own), cur(own)],
        out_shape=[jax.ShapeDtypeStruct((ln, r * 256), BF16)] * 3,
        compiler_params=_params(("parallel", "parallel")),
    )(qv, qv, kv, kv, pv, pv, dov, dov, ov, ov, lv, lv)
    return outs


def _gelu_parts(gv):
    cdf = 0.5 * (1.0 + lax.erf(gv * (2.0 ** -0.5)))
    pdf = jnp.exp(-0.5 * gv * gv) * (1.0 / math.sqrt(2.0 * math.pi))
    return cdf, pdf


def _pick_row(t, k):
    row = lax.broadcasted_iota(jnp.int32, t.shape, 0)
    return jnp.sum(jnp.where(row == k, t, 0.0), axis=0, keepdims=True)


def _shift_rows(u, halo, n):
    row = lax.broadcasted_iota(jnp.int32, u.shape, 0)
    out = pltpu.roll(u, n, 0)
    for k in range(n):
        out = jnp.where(row == k, _pick_row(halo, 16 - n + k), out)
    return out


def _shift_rows_up(u, halo, n):
    rb = u.shape[0]
    row = lax.broadcasted_iota(jnp.int32, u.shape, 0)
    out = pltpu.roll(u, rb - n, 0)
    for k in range(n):
        out = jnp.where(row == rb - n + k, _pick_row(halo, k), out)
    return out


def _conv(u, halo, cw, cb):
    return cb + _pick_row(cw, 0) * _shift_rows(u, halo, 2) + _pick_row(cw, 1) * _shift_rows(u, halo, 1) + _pick_row(cw, 2) * u


def _local_step(x, mod, pos_col, target, sm, w_sh, g0, chip, core):
    s = x.shape[0]
    shift1, scale1, gate1, shift2, scale2, gate2 = [mod[i:i + 1, :] for i in range(6)]
    rb = 512
    chip1 = chip.reshape(1)

    def f_norm1(c, i, xv, nw, sc, sh):
        return ((xv * _rms(xv) * nw) * (1.0 + sc) + sh,)

    (h,) = _rowcall(f_norm1, [_rows(x, rb), _full(sm["n1w"]), _full(scale1), _full(shift1)],
                    [_orow(s, D, BF16, rb)], n_rows=s, rb=rb, name="norm1")
    own = lambda got, i: lax.dynamic_update_slice(got, w_sh[i], (chip, 0, 0))
    invf = jnp.tile(ROPE_THETA ** (-jnp.arange(ATT_HD // 2, dtype=F32) / (ATT_HD // 2)), 4).reshape(1, 128)
    cos_t, sin_t = _rope_tables(pos_col, invf, s)
    _, got0 = _unit_wait(*g0[:4], after=[h, cos_t, sin_t], name="gather_w_in_wait")
    [got0] = _comm_call("gather_w_in_d2d", [_u_gather_d2d(got0, (0,))])
    u_g1 = _u_gather_ici(w_sh, (1, 2, 3, 4, 5))
    g1 = _unit_start(u_g1, "gather_weights_start", after=got0)
    w = dict(win=_win_assemble(own(got0[0], 0), after=g1[3:]))
    p = _mm(h, w["win"], "in_proj", tm=2048, tn=1536)

    def f_gla_pre(c, i, glr, w2, gb):
        z = _dg(glr, w2.astype(BF16), 1, 0) + gb
        return ((jnp.minimum(z, 0.0) - jnp.log(1.0 + jnp.exp(-jnp.abs(z)))) * (1.0 / GLA_TAU),)

    (la,) = _rowcall(f_gla_pre, [_rows(p, rb, 128, P_LR // 128), _full(sm["w2"]), _full(sm["gb"])],
                     [_orow(s, 512, F32, rb)], n_rows=s, rb=rb, name="gla_pre")
    o_gla, states, _ = _gla_fwd(p, la, s)
    _, got = _unit_wait(u_g1, *g1[:3], after=[o_gla], name="gather_weights_wait")
    [got123] = _comm_call("gather_weights_d2d", [_u_gather_d2d(got[:3], (1, 2, 3))])
    got45 = got[3:]
    w.update(wgb=own(got123[0], 1).reshape(1024, D), wab=_cols_join(own(got123[1], 2)), wout=own(got123[2], 3).reshape(D, D))

    def f_gla_post(c, i, ov, gnw, gr):
        on = jnp.concatenate([ov[:, k * 256:(k + 1) * 256] * _rms(ov[:, k * 256:(k + 1) * 256]) * gnw
                              for k in range(GLA_H)], axis=1)
        g = gr.astype(F32)
        return (on * (g * _sigmoid(g)),)

    (og,) = _rowcall(f_gla_post, [_rows(o_gla, rb), _full(sm["gnw"]), _rows(p, rb, 1024, P_GR // 1024)],
                     [_orow(s, 1024, BF16, rb)], n_rows=s, rb=rb, name="gla_post")
    y_gla = _mm(og, w["wgb"], "gla_branch")

    q_d, k_d, v_d = _rope_fwd(p, cos_t, sin_t, s)
    att = [_attn_fwd(q_d[g], k_d[g], v_d[g], g, r, s) for g, r in enumerate(_RS)]
    o_att, lse, o_d1, o_d2, lse_d1, lse_d2 = _attn_combine(att, s)
    y_att = _mm(o_att, w["wab"], "attn_branch")

    def f_merge(c, i, ma, mb, yg, ya):
        return (_sigmoid(ma.astype(F32)) * yg.astype(F32) + _sigmoid(mb.astype(F32)) * ya.astype(F32),)

    (mixed,) = _rowcall(f_merge, [_rows(p, rb, D, P_MA // D), _rows(p, rb, D, P_MB // D), _rows(y_gla, rb), _rows(y_att, rb)],
                        [_orow(s, D, BF16, rb)], n_rows=s, rb=rb, name="merge")
    z1, [got45] = _mm(mixed, w["wout"], "out_proj", comm=[_u_gather_d2d(got45, (4, 5))])
    w.update(wup=own(got45[0], 4), wdown=own(got45[1], 5).reshape(D_FF, D))

    def f_norm2(c, i, xv, z, g1, nw, sc, sh):
        x1 = xv + g1 * z.astype(F32)
        return (x1, (x1 * _rms(x1) * nw) * (1.0 + sc) + sh)

    x1, h2 = _rowcall(f_norm2, [_rows(x, rb), _rows(z1, rb), _full(gate1), _full(sm["n2w"]), _full(scale2), _full(shift2)],
                      [_orow(s, D, F32, rb), _orow(s, D, BF16, rb)], n_rows=s, rb=rb, name="norm2")
    u = _mm(h2, w["wup"], "up_proj", tm=2048, b_shards=True)

    cwid = 2 * W_UP_SH

    def f_ffn(c, i, uv, hl, cw, cb):
        uc = _conv(uv.astype(F32), hl.astype(F32) * (i > 0).astype(F32), cw, cb)
        val, gt = uc[:, :W_UP_SH], uc[:, W_UP_SH:]
        cdf, _ = _gelu_parts(gt)
        return (gt * cdf * val,)

    ccol = lambda c: c
    rw = 256
    (hidden,) = _rowcall(f_ffn, [_rows(u, rw, cwid, ccol), _halo(u, rw, 16, cwid, ccol, True),
                                 _full(sm["cw"], cwid, ccol), _full(sm["cb"], cwid, ccol)],
                         [_orow(s, D_FF, BF16, rw, W_UP_SH, ccol)], n_rows=s, rb=rw, name="conv_geglu", ncol=2)
    z2 = _mm(hidden, w["wdown"], "down_proj", tk=D_FF)

    def f_final(c, i, x1v, z, g2, fw, tgt):
        x2 = x1v + g2 * z.astype(F32)
        r = _rms(x2)
        xh = x2 * r
        e = xh * fw - tgt
        loss = 0.5 * jnp.sum(jnp.mean(e * e, axis=-1, keepdims=True), axis=0, keepdims=True)
        dy = e * (1.0 / D)
        dxh = dy * fw
        dx2 = r * (dxh - xh * jnp.mean(dxh * xh, axis=-1, keepdims=True))
        return (loss, dx2, dx2 * g2, _csum(dy * xh), _csum(dx2 * z.astype(F32)))

    loss, dx2, dz2, d_fnw, d_gate2 = _rowcall(
        f_final, [_rows(x1, rb), _rows(z2, rb), _full(gate2), _full(sm["fnw"]), _rows(target, rb)],
        [_oacc(1, 1), _orow(s, D, F32, rb), _orow(s, D, BF16, rb), _oacc(1, D), _oacc(1, D)],
        n_rows=s, rb=rb, name="final_loss")
    d_hidden = _mm(dz2, w["wdown"], "down_proj_dx", tb=True, tn=1408)
    g_wdown = _mm(hidden, dz2, "down_proj_dw", ta=True, out_dtype=F32, tm=1408, tn=1024, tk=2048)

    def f_ffn_bwd(c, i, uv, hl, dh, cw, cb):
        uf = uv.astype(F32)
        hf = hl.astype(F32) * (i > 0).astype(F32)
        u1, u2 = _shift_rows(uf, hf, 1), _shift_rows(uf, hf, 2)
        uc = cb + _pick_row(cw, 0) * u2 + _pick_row(cw, 1) * u1 + _pick_row(cw, 2) * uf
        val, gt = uc[:, :W_UP_SH], uc[:, W_UP_SH:]
        cdf, pdf = _gelu_parts(gt)
        dhf = dh.astype(F32)
        duc = jnp.concatenate([dhf * (gt * cdf), dhf * val * (cdf + gt * pdf)], axis=1)
        dcw = jnp.concatenate([_csum(duc * u2), _csum(duc * u1), _csum(duc * uf)], axis=0)
        return (duc, _csum(duc), dcw)

    duc, d_cb, d_cw = _rowcall(
        f_ffn_bwd, [_rows(u, rw, cwid, ccol), _halo(u, rw, 16, cwid, ccol, True), _rows(d_hidden, rw, W_UP_SH, ccol),
                    _full(sm["cw"], cwid, ccol), _full(sm["cb"], cwid, ccol)],
        [_orow(s, 2 * D_FF, BF16, rw, cwid, ccol), _oacc(1, 2 * D_FF, cwid, ccol), _oacc(3, 2 * D_FF, cwid, ccol)],
        n_rows=s, rb=rw, name="conv_geglu_bwd", ncol=2)

    def f_conv_t(c, i, dv, hl, cw):
        df = dv.astype(F32)
        hf = hl.astype(F32) * (i < s // rw - 1).astype(F32)
        return (_pick_row(cw, 2) * df + _pick_row(cw, 1) * _shift_rows_up(df, hf, 1) + _pick_row(cw, 0) * _shift_rows_up(df, hf, 2),)

    (du,) = _rowcall(f_conv_t, [_rows(duc, rw, cwid, ccol), _halo(duc, rw, 16, cwid, ccol, False), _full(sm["cw"], cwid, ccol)],
                     [_orow(s, 2 * D_FF, BF16, rw, cwid, ccol)], n_rows=s, rb=rw, name="conv_transpose", ncol=2)
    g_wup = _mm(h2, du, "up_proj_dw", ta=True, out_dtype=F32, tm=1024, tk=2048, o_shards=True)
    gs45 = [g_wup, g_wdown.reshape(4, W_DOWN_SH, 1024)]
    d_h2, [land45] = _mm(du, w["wup"], "up_proj_dx", tb=True, tm=2048, b_shards=True, comm=[_u_pair_send(gs45, (4, 5))])
    ts45 = [_pair_add(g, ld, core, "grad_pair_add_" + BIG[i]) for g, ld, i in zip(gs45, land45, (4, 5))]
    u_ex4 = _u_chip_exchange(ts45[:1])
    ex4 = _unit_start(u_ex4, "grad_exchange_w_up_start")

    def f_norm2_bwd(c, i, x1v, dh, dxr, z, nw, sc, g1):
        dxn, dsh, dsc, dnw = _norm_bwd(x1v, dh.astype(F32), nw, sc)
        dx1 = dxr + dxn
        return (dx1, dx1 * g1, dsh, dsc, dnw, _csum(dx1 * z.astype(F32)))

    dx1, dz1, d_shift2, d_scale2, d_n2w, d_gate1 = _rowcall(
        f_norm2_bwd, [_rows(x1, rb), _rows(d_h2, rb), _rows(dx2, rb), _rows(z1, rb), _full(sm["n2w"]), _full(scale2), _full(gate1)],
        [_orow(s, D, F32, rb), _orow(s, D, BF16, rb), _oacc(1, D), _oacc(1, D), _oacc(1, D), _oacc(1, D)],
        n_rows=s, rb=rb, name="norm2_bwd", after=ex4[3:])
    d_mixed = _mm(dz1, w["wout"], "out_proj_dx", tb=True)
    g_wout = _mm(mixed, dz1, "out_proj_dw", ta=True, out_dtype=F32, tk=2048)

    def f_merge_bwd(c, i, dm, ma, mb, yg, ya):
        dmf, ygf, yaf = dm.astype(F32), yg.astype(F32), ya.astype(F32)
        sa, sb = _sigmoid(ma.astype(F32)), _sigmoid(mb.astype(F32))
        return (dmf * sa, dmf * sb, jnp.concatenate([dmf * ygf * sa * (1.0 - sa), dmf * yaf * sb * (1.0 - sb)], axis=1))

    dy_gla, dy_att, dp = _rowcall(
        f_merge_bwd, [_rows(d_mixed, rb), _rows(p, rb, D, P_MA // D), _rows(p, rb, D, P_MB // D), _rows(y_gla, rb), _rows(y_att, rb)],
        [_orow(s, D, BF16, rb)] * 2 + [_orow(s, P_W, BF16, rb, 2 * D, lambda c: P_MA // (2 * D))], n_rows=s, rb=rb, name="merge_bwd")
    d_og = _mm(dy_gla, w["wgb"], "gla_branch_dx", tb=True)
    g_wgb = _mm(og, dy_gla, "gla_branch_dw", ta=True, out_dtype=F32, tk=2048)
    d_oatt = _mm(dy_att, w["wab"], "attn_branch_dx", tb=True)
    g_wab = _mm(o_att, dy_att, "attn_branch_dw", ta=True, out_dtype=F32, tk=2048)

    def f_gla_post_bwd(c, i, ov, gnw, gr, dog):
        g = gr.astype(F32)
        sg = _sigmoid(g)
        silu = g * sg
        dof = dog.astype(F32)
        don = dof * silu
        on_parts, do_parts, dgn = [], [], jnp.zeros((1, 256), F32)
        for k in range(GLA_H):
            oh = ov[:, k * 256:(k + 1) * 256]
            dh = don[:, k * 256:(k + 1) * 256]
            r = _rms(oh)
            xh = oh * r
            dgn = dgn + _csum(dh * xh)
            dxh = dh * gnw
            do_parts.append(r * (dxh - xh * jnp.mean(dxh * xh, axis=-1, keepdims=True)))
            on_parts.append(xh * gnw)
        on = jnp.concatenate(on_parts, axis=1)
        dgr = dof * on * (sg * (1.0 + g * (1.0 - sg)))
        return (jnp.concatenate(do_parts, axis=1), dgr, dgn)

    do_gla, dp, d_gnw = _rowcall(
        f_gla_post_bwd, [_rows(o_gla, rb), _full(sm["gnw"]), _rows(p, rb, 1024, P_GR // 1024), _rows(d_og, rb)],
        [_orow(s, 1024, F32, rb), _orow(s, P_W, BF16, rb, 1024, lambda c: P_GR // 1024), _oacc(1, 256)],
        n_rows=s, rb=rb, name="gla_post_bwd", into=(dp, 1))
    gs123 = [g_wgb.reshape(4, 256, 1024), _cols_split(g_wab), g_wout.reshape(4, 256, 1024)]
    d_gq, d_gk, dp, d_la, [land123] = _gla_bwd(p, la, states, do_gla, s, dp, comm=[_u_pair_send(gs123, (1, 2, 3))])
    ts123 = [_pair_add(g, ld, core, "grad_pair_add_" + BIG[i]) for g, ld, i in zip(gs123, land123, (1, 2, 3))]

    def f_gla_pre_bwd(c, i, lav, dlav, glr, w2):
        dz = dlav * (1.0 / GLA_TAU) * (1.0 - jnp.exp(GLA_TAU * lav))
        dzb = dz.astype(BF16)
        return (_dg(dzb, w2.astype(BF16), 1, 1), _csum(dz), _dg(glr, dzb, 0, 0))

    d_glr, d_gb, d_w2 = _rowcall(
        f_gla_pre_bwd, [_rows(la, rb), _rows(d_la, rb), _rows(p, rb, 128, P_LR // 128), _full(sm["w2"])],
        [_orow(s, 128, BF16, rb), _oacc(1, 512), _oacc(128, 512)], n_rows=s, rb=rb, name="gla_pre_bwd")

    do_d = [d_oatt] + list(_dilate(d_oatt, s))
    datt = [_attn_bwd(q_d[g], k_d[g], v_d[g], do_d[g], (o_att, o_d1, o_d2)[g], (lse, lse_d1, lse_d2)[g], g, r, s)
            for g, r in enumerate(_RS)]
    dp = _rope_bwd(datt, d_glr, dp, cos_t, sin_t, s)
    dp = lax.dynamic_update_slice(dp, jnp.concatenate([d_gq, d_gk], axis=1), (0, P_GQ))
    [t4], [r4] = _unit_wait(u_ex4, *ex4[:3], after=[dp], name="grad_exchange_w_up_wait")
    half4 = [_chip_sum(t4, r4, chip1, "grad_chip_sum_w_up")]
    g_win, [r1235, oth4] = _mm(h, dp, "in_proj_dw", ta=True, out_dtype=F32, tm=1024, tn=1536, tk=2048,
                               comm=[_u_chip_exchange(ts123 + ts45[1:]), _u_pair_join(half4)])
    half1235 = [_chip_sum(t, r, chip1, "grad_chip_sum_" + BIG[i]) for t, r, i in zip(ts123 + ts45[1:], r1235, (1, 2, 3, 5))]
    gs0 = [_win_split(g_win)]
    d_h, [land0, oth1235] = _mm(dp, w["win"], "in_proj_dx", tb=True, tk=3840,
                                comm=[_u_pair_send(gs0, (0,)), _u_pair_join(half1235)])
    half123, half45 = half1235[:3], half4 + half1235[3:]
    oth123, oth45 = oth1235[:3], oth4 + oth1235[3:]
    ts0 = _pair_add(gs0[0], land0[0], core, "grad_pair_add_w_in")

    def f_norm1_bwd(c, i, xv, dh, dxr, nw, sc):
        dxn, dsh, dsc, dnw = _norm_bwd(xv, dh.astype(F32), nw, sc)
        return (dxr + dxn, dsh, dsc, dnw)

    grad_x, d_shift1, d_scale1, d_n1w = _rowcall(
        f_norm1_bwd, [_rows(x, rb), _rows(d_h, rb), _rows(dx1, rb), _full(sm["n1w"]), _full(scale1)],
        [_orow(s, D, F32, rb), _oacc(1, D), _oacc(1, D), _oacc(1, D)], n_rows=s, rb=rb, name="norm1_bwd")

    dmod = jnp.concatenate([d_shift1, d_scale1, d_gate1, d_shift2, d_scale2, d_gate2], axis=1)
    small = dict(dmod=dmod, n1w=d_n1w, gb=d_gb, gnw=d_gnw, n2w=d_n2w, cb=d_cb, fnw=d_fnw, w2=d_w2, cw=d_cw)
    return loss, grad_x, half123 + half45, oth123 + oth45, small, ts0


def _win_pieces():
    runs = [(P_GV, 1024, 2048), (P_MA, 5392, 2048), (P_GQ, 0, 1024), (P_AQ, 3088, 2304), (P_LR, 3072, GLA_LR)]
    out = []
    for kc, rc, ln in runs:
        while ln > 0:
            step = min(ln, W_IN_SH - rc % W_IN_SH)
            out.append((kc, rc, step))
            kc, rc, ln = kc + step, rc + step, ln - step
    return out


def _win_assemble(shards, after=()):
    rb = 256

    def body(s_ref, *rest):
        o_ref = rest[-1]
        o_ref[:, W_IN:] = jnp.zeros((rb, P_W - W_IN), o_ref.dtype)
        for kc, rc, ln in _win_pieces():
            o_ref[:, kc:kc + ln] = s_ref[rc // W_IN_SH, :, rc % W_IN_SH:rc % W_IN_SH + ln]

    return pl.pallas_call(
        body, name="w_in_assemble", grid=(D // rb,),
        in_specs=[pl.BlockSpec((4, rb, W_IN_SH), lambda i: (0, i, 0))] + [pl.BlockSpec(memory_space=pl.ANY)] * len(after),
        out_specs=pl.BlockSpec((rb, P_W), lambda i: (i, 0)),
        out_shape=jax.ShapeDtypeStruct((D, P_W), shards.dtype), compiler_params=_params(("parallel",)),
    )(shards, *after)


def _win_split(g):
    rb = 256

    def body(g_ref, o_ref):
        for kc, rc, ln in _win_pieces():
            o_ref[rc // W_IN_SH, :, rc % W_IN_SH:rc % W_IN_SH + ln] = g_ref[:, kc:kc + ln]

    return pl.pallas_call(
        body, name="w_in_grad_split", grid=(D // rb,),
        in_specs=[pl.BlockSpec((rb, P_W), lambda i: (i, 0))], out_specs=pl.BlockSpec((4, rb, W_IN_SH), lambda i: (0, i, 0)),
        out_shape=jax.ShapeDtypeStruct((4, D, W_IN_SH), g.dtype), compiler_params=_params(("parallel",)),
    )(g)


def _ff_to_kernel(a):
    h = W_UP_SH
    return jnp.concatenate([a[:, 0:h], a[:, D_FF:D_FF + h], a[:, h:D_FF], a[:, D_FF + h:]], axis=1)


def _ff_from_kernel(a):
    h = W_UP_SH
    return jnp.concatenate([a[:, 0:h], a[:, 2 * h:3 * h], a[:, h:2 * h], a[:, 3 * h:]], axis=1)


BIG = ("w_in", "w_gla_branch", "w_attn_branch", "w_out", "w_up", "w_down")
SH_SHAPES = ((1024, W_IN_SH), (256, 1024), (256, 256), (256, 1024), (1024, W_UP_SH), (W_DOWN_SH, 1024))
N_BIG = len(BIG)


def _cols_join(t):
    return jnp.concatenate([t[k] for k in range(4)], axis=1)


def _cols_split(t):
    cols = t.shape[1] // 4
    return jnp.stack([t[:, k * cols:(k + 1) * cols] for k in range(4)])


def _me():
    return lax.axis_index("x"), lax.axis_index("y"), lax.axis_index("c")


HBM = pl.BlockSpec(memory_space=pltpu.HBM)
VMEM_SPEC = pl.BlockSpec(memory_space=pltpu.VMEM)


def _allgather8(xs, name):
    rows = xs.shape[0]

    def body(x_ref, out_ref, send_sems, recv_sems, local_sem):
        x, y, c = _me()
        me = 4 * x + 2 * y + c
        mine = pltpu.make_async_copy(x_ref, out_ref.at[me], local_sem)
        mine.start()
        flips = [(k >> 2 & 1, k >> 1 & 1, k & 1) for k in range(1, 8)]

        def peer(f):
            return (jnp.where(f[0] == 1, 1 - x, x), jnp.where(f[1] == 1, 1 - y, y), jnp.where(f[2] == 1, 1 - c, c))

        sends = []
        for k, f in enumerate(flips):
            cp = pltpu.make_async_remote_copy(src_ref=x_ref, dst_ref=out_ref.at[me], send_sem=send_sems.at[k],
                                              recv_sem=recv_sems.at[k], device_id=peer(f), device_id_type=MESH)
            cp.start()
            sends.append(cp)
        for k, f in enumerate(flips):
            px, py, pc = peer(f)
            pltpu.make_async_remote_copy(src_ref=x_ref, dst_ref=out_ref.at[4 * px + 2 * py + pc], send_sem=send_sems.at[k],
                                         recv_sem=recv_sems.at[k], device_id=peer(f), device_id_type=MESH).wait_recv()
        for cp in sends:
            cp.wait_send()
        mine.wait()

    return pl.pallas_call(
        body, name=name, out_shape=jax.ShapeDtypeStruct((8, rows, 128), F32),
        in_specs=[VMEM_SPEC], out_specs=VMEM_SPEC,
        scratch_shapes=[pltpu.SemaphoreType.DMA((7,)), pltpu.SemaphoreType.DMA((7,)), pltpu.SemaphoreType.DMA],
        compiler_params=pltpu.CompilerParams(vmem_limit_bytes=VMEM_LIMIT),
    )(xs)


def _half_rows(i, cc, unit):
    rows = SH_SHAPES[i][0] // 2
    return pl.ds(pl.multiple_of(cc * rows, unit), rows)


def _rc(src, dst, sems, to):
    return pltpu.make_async_remote_copy(src_ref=src, dst_ref=dst, send_sem=sems[0], recv_sem=sems[1], device_id=to, device_id_type=MESH)


def _other_chips(x, y):
    return [(1 - x, y), (x, 1 - y), (1 - x, 1 - y)]


def _u_gather_ici(w_sh, idxs):
    def copies(ins, outs, sem):
        x, y, c = _me()
        res = []
        for j, (px, py) in enumerate(_other_chips(x, y)):
            for n, i in enumerate(idxs):
                src = ins[n].at[0, _half_rows(i, c, 16)]
                res.append((_rc(src, outs[n].at[2 * x + y, _half_rows(i, c, 16)], sem(j * len(idxs) + n), (px, py, c)),
                            _rc(src, outs[n].at[2 * px + py, _half_rows(i, c, 16)], sem(j * len(idxs) + n), (px, py, c))))
        return res

    return dict(ins=[w_sh[i] for i in idxs], outs=[jax.ShapeDtypeStruct((4,) + SH_SHAPES[i], BF16) for i in idxs],
                nsem=3 * len(idxs), alias={}, copies=copies)


def _u_gather_d2d(got, idxs):
    def copies(ins, outs, sem):
        x, y, c = _me()
        res = []
        for j, (px, py) in enumerate(_other_chips(x, y)):
            for n, i in enumerate(idxs):
                src = ins[n].at[2 * px + py, _half_rows(i, c, 16)]
                res.append((_rc(src, outs[n].at[2 * px + py, _half_rows(i, c, 16)], sem(j * len(idxs) + n), (x, y, 1 - c)),
                            _rc(src, outs[n].at[2 * px + py, _half_rows(i, 1 - c, 16)], sem(j * len(idxs) + n), (x, y, 1 - c))))
        return res

    return dict(ins=list(got), outs=[jax.ShapeDtypeStruct(g.shape, g.dtype) for g in got], nsem=3 * len(idxs),
                alias={n: n for n in range(len(idxs))}, copies=copies)


def _u_pair_send(gs, idxs):
    def copies(ins, outs, sem):
        x, y, c = _me()
        res = []
        for n, i in enumerate(idxs):
            for sh in range(4):
                cp = _rc(ins[n].at[sh, _half_rows(i, 1 - c, 8)], outs[n].at[sh], sem(4 * n + sh), (x, y, 1 - c))
                res.append((cp, cp))
        return res

    return dict(ins=list(gs), outs=[jax.ShapeDtypeStruct((4, SH_SHAPES[i][0] // 2, SH_SHAPES[i][1]), F32) for i in idxs],
                nsem=4 * len(idxs), alias={}, copies=copies)


def _u_chip_exchange(ts):
    def copies(ins, outs, sem):
        x, y, c = _me()
        res = []
        for j, (px, py) in enumerate(_other_chips(x, y)):
            for n in range(len(ts)):
                cp = _rc(ins[n].at[2 * px + py], outs[n].at[j], sem(j * len(ts) + n), (px, py, c))
                res.append((cp, cp))
        return res

    return dict(ins=list(ts), outs=[jax.ShapeDtypeStruct((3,) + t.shape[1:], t.dtype) for t in ts], nsem=3 * len(ts),
                alias={}, copies=copies)


def _u_pair_join(hs):
    def copies(ins, outs, sem):
        x, y, c = _me()
        res = []
        for n in range(len(hs)):
            cp = _rc(ins[n], outs[n], sem(n), (x, y, 1 - c))
            res.append((cp, cp))
        return res

    return dict(ins=list(hs), outs=[jax.ShapeDtypeStruct(h.shape, h.dtype) for h in hs], nsem=len(hs), alias={}, copies=copies)


def _comm_phase(units, ci, co, send_sems, recv_sems, start):
    ii = oo = off = 0
    for u in units:
        ni, no = len(u["ins"]), len(u["outs"])
        for st, arrival in u["copies"](ci[ii:ii + ni], co[oo:oo + no], lambda k, off=off: (send_sems.at[off + k], recv_sems.at[off + k])):
            if start:
                st.start()
            else:
                st.wait_send()
                arrival.wait_recv()
        ii, oo, off = ii + ni, oo + no, off + u["nsem"]


def _carry(units, n_in, n_out):
    ins = [a for u in units for a in u["ins"]]
    outs = [o for u in units for o in u["outs"]]
    alias, ii, oo = {}, 0, 0
    for u in units:
        for a, b in u["alias"].items():
            alias[n_in + ii + a] = n_out + oo + b
        ii, oo = ii + len(u["ins"]), oo + len(u["outs"])
    nsem = sum(u["nsem"] for u in units)
    scratch = [pltpu.SemaphoreType.DMA((nsem,)), pltpu.SemaphoreType.DMA((nsem,))] if units else []
    return ins, outs, alias, scratch


def _split_units(units, res):
    out, oo = [], 0
    for u in units:
        out.append(list(res[oo:oo + len(u["outs"])]))
        oo += len(u["outs"])
    return out


def _comm_call(name, units):
    ins, outs, alias, scratch = _carry(units, 0, 0)

    def body(*refs):
        ci, co = refs[:len(ins)], refs[len(ins):len(ins) + len(outs)]
        _comm_phase(units, ci, co, refs[-2], refs[-1], True)
        _comm_phase(units, ci, co, refs[-2], refs[-1], False)

    res = pl.pallas_call(body, name=name, out_shape=outs, in_specs=[HBM] * len(ins), out_specs=[HBM] * len(outs),
                         scratch_shapes=scratch, input_output_aliases=alias)(*ins)
    return _split_units(units, res)


SEM = pl.BlockSpec(memory_space=pltpu.SEMAPHORE)
EFFECT = pltpu.SideEffectType.DATAFLOW_SIDE_EFFECTING


def _unit_start(unit, name, after=()):
    bufs = list(unit["ins"]) + [lax.empty(o.shape, o.dtype) for o in unit["outs"]]
    n_i, n_b, ns = len(unit["ins"]), len(bufs), unit["nsem"]

    def body(*refs):
        send_sems, recv_sems = refs[n_b + len(after)], refs[n_b + len(after) + 1]
        for st, _ in unit["copies"](refs[:n_i], refs[n_i:n_b], lambda k: (send_sems.at[k], recv_sems.at[k])):
            st.start()
        refs[-1][...] = jnp.zeros_like(refs[-1])

    res = pl.pallas_call(
        body, name=name,
        out_shape=[pltpu.SemaphoreType.DMA((ns,)), pltpu.SemaphoreType.DMA((ns,))] + [pltpu.HBM(b.shape, b.dtype) for b in bufs]
        + [jax.ShapeDtypeStruct((8, 128), F32)],
        in_specs=[HBM] * n_b + [pl.BlockSpec(memory_space=pl.ANY)] * len(after), out_specs=[SEM, SEM] + [HBM] * n_b + [VMEM_SPEC],
        input_output_aliases={i: 2 + i for i in range(n_b)},
        compiler_params=pltpu.CompilerParams(has_side_effects=EFFECT),
    )(*[pltpu.with_memory_space_constraint(b, pltpu.HBM) for b in bufs], *after)
    return res[0], res[1], list(res[2:2 + n_b]), res[-1]


def _unit_wait(unit, send_sems, recv_sems, bufs, after, name):
    n_i, n_b = len(unit["ins"]), len(bufs)

    def body(*refs):
        ss, rs = refs[n_b], refs[n_b + 1]
        for st, arrival in unit["copies"](refs[:n_i], refs[n_i:n_b], lambda k: (ss.at[k], rs.at[k])):
            st.wait_send()
            arrival.wait_recv()

    res = pl.pallas_call(
        body, name=name, out_shape=[pltpu.HBM(b.shape, b.dtype) for b in bufs],
        in_specs=[HBM] * n_b + [SEM, SEM] + [pl.BlockSpec(memory_space=pl.ANY)] * len(after), out_specs=[HBM] * n_b,
        input_output_aliases={i: i for i in range(n_b)}, compiler_params=pltpu.CompilerParams(has_side_effects=EFFECT),
    )(*bufs, send_sems, recv_sems, *after)
    return list(res[:n_i]), list(res[n_i:])


def _pair_add(g, land, core, name):
    _, rows, cols = g.shape
    half = rows // 2
    rb = _tile(half, 512, 16)
    nb = half // rb

    def body(c_ref, g_ref, l_ref, o_ref):
        o_ref[...] = (g_ref[...] + l_ref[...]).astype(BF16)

    return pl.pallas_call(
        body, name=name,
        grid_spec=pltpu.PrefetchScalarGridSpec(
            num_scalar_prefetch=1, grid=(4, nb),
            in_specs=[pl.BlockSpec((1, rb, cols), lambda s, i, c_ref: (s, c_ref[0] * nb + i, 0)),
                      pl.BlockSpec((1, rb, cols), lambda s, i, c_ref: (s, i, 0))],
            out_specs=pl.BlockSpec((1, rb, cols), lambda s, i, c_ref: (s, i, 0))),
        out_shape=jax.ShapeDtypeStruct((4, half, cols), BF16),
        compiler_params=_params(("parallel", "parallel")),
    )(core, g, land)


def _chip_sum(t, r, chip, name):
    _, half, cols = t.shape
    rb = _tile(half, 512, 16)

    def body(s_ref, t_ref, r_ref, o_ref):
        o_ref[...] = ((t_ref[0].astype(F32) + r_ref[0].astype(F32)) + r_ref[1].astype(F32)) + r_ref[2].astype(F32)

    return pl.pallas_call(
        body, name=name,
        grid_spec=pltpu.PrefetchScalarGridSpec(
            num_scalar_prefetch=1, grid=(half // rb,),
            in_specs=[pl.BlockSpec((1, rb, cols), lambda i, s_ref: (s_ref[0], i, 0)),
                      pl.BlockSpec((3, rb, cols), lambda i, s_ref: (0, i, 0))],
            out_specs=pl.BlockSpec((rb, cols), lambda i, s_ref: (i, 0))),
        out_shape=jax.ShapeDtypeStruct((half, cols), F32),
        compiler_params=_params(("parallel",)),
    )(chip, t, r)


def _adam_math(wv, gv, mv, vv):
    mn = ADAM_B1 * mv + (1.0 - ADAM_B1) * gv
    vn = ADAM_B2 * vv + (1.0 - ADAM_B2) * (gv * gv)
    m_hat = mn / (1.0 - ADAM_B1 ** ADAM_STEP)
    v_hat = vn / (1.0 - ADAM_B2 ** ADAM_STEP)
    return -ADAM_LR * (m_hat / (jnp.sqrt(v_hat) + ADAM_EPS) + ADAM_WD * wv), mn, vn


def _adamw_halves(wt, mt, vt, mine, theirs, core, name):
    _, rows, cols = wt.shape
    half = rows // 2
    rb = _tile(half, 256, 8)
    nb = half // rb

    def body(c_ref, w_ref, m_ref, v_ref, a_ref, b_ref, g_ref, d_ref, mo_ref, vo_ref):
        gv = jnp.where(pl.program_id(0) == c_ref[0], a_ref[...], b_ref[...])
        dl, mn, vn = _adam_math(w_ref[...], gv, m_ref[...], v_ref[...])
        g_ref[...] = gv
        d_ref[...] = dl
        mo_ref[...] = mn
        vo_ref[...] = vn

    full = pl.BlockSpec((None, rb, cols), lambda hf, i, c_ref: (0, hf * nb + i, 0))
    part = pl.BlockSpec((rb, cols), lambda hf, i, c_ref: (i, 0))
    return pl.pallas_call(
        body, name=name,
        grid_spec=pltpu.PrefetchScalarGridSpec(num_scalar_prefetch=1, grid=(2, nb), in_specs=[full, full, full, part, part],
                                               out_specs=[full] * 4),
        out_shape=[jax.ShapeDtypeStruct((1, rows, cols), F32)] * 4,
        compiler_params=_params(("parallel", "parallel")),
    )(core, wt, mt, vt, mine, theirs)


SG_REP = 144
SG_LOSS = 136
SG_W2, SG_CW = SG_REP, SG_REP + 4 * 16
SG_ROWS = SG_CW + 4 * 40
SP_ROWS = SG_REP + 16 + 40


def _mod_shard(c_all, ada_w_sh):
    def body(c_ref, w_ref, o_ref):
        cv = c_ref[...]
        o_ref[...] = _dg((cv * _sigmoid(cv)).astype(BF16), w_ref[...].astype(BF16), 1, 0)

    return pl.pallas_call(body, name="mod_shard", out_shape=jax.ShapeDtypeStruct((8, 1536), F32),
                          in_specs=[VMEM_SPEC, VMEM_SPEC], out_specs=VMEM_SPEC,
                          compiler_params=pltpu.CompilerParams(vmem_limit_bytes=VMEM_LIMIT))(c_all, ada_w_sh)


def _mod_select(mod_all, ada_b4):
    def body(m_ref, b_ref, o_ref):
        x, y, c = _me()
        me = 4 * x + 2 * y + c
        for sh in range(4):
            o_ref[sh] = m_ref[2 * sh, me] + b_ref[sh]

    return pl.pallas_call(body, name="mod_select", out_shape=jax.ShapeDtypeStruct((4, 12, 128), F32),
                          in_specs=[VMEM_SPEC, VMEM_SPEC], out_specs=VMEM_SPEC)(mod_all, ada_b4)


def _small_reduce(sg_all):
    def body(g_ref, o_ref):
        x, y, c = _me()
        s_me = 2 * x + y
        w2_rows = pl.ds(pl.multiple_of(SG_W2 + 16 * s_me, 8), 16)
        cw_rows = pl.ds(pl.multiple_of(SG_CW + 40 * s_me, 8), 40)
        a = g_ref[0, 0:SG_REP, :]
        b = g_ref[0, w2_rows, :]
        d = g_ref[0, cw_rows, :]
        for dev in range(1, 8):
            a = a + g_ref[dev, 0:SG_REP, :]
            b = b + g_ref[dev, w2_rows, :]
            d = d + g_ref[dev, cw_rows, :]
        o_ref[0:SG_REP, :] = a
        o_ref[SG_REP:SG_REP + 16, :] = b
        o_ref[SG_REP + 16:SP_ROWS, :] = d

    return pl.pallas_call(body, name="small_grad_reduce", out_shape=jax.ShapeDtypeStruct((SP_ROWS, 128), F32),
                          in_specs=[VMEM_SPEC], out_specs=VMEM_SPEC)(sg_all)


def _ada_grad(dmod_all, c_bc):
    def body(g_ref, c_ref, o_ref):
        x, y, c = _me()
        s_me = 2 * x + y
        for k in range(12):
            acc = jnp.zeros((D, 128), F32)
            for b in range(8):
                cv = c_ref[b]
                acc = acc + (cv * _sigmoid(cv)) * g_ref[s_me, k, b:b + 1, :]
            o_ref[:, k * 128:(k + 1) * 128] = acc

    return pl.pallas_call(body, name="ada_w_grad", out_shape=jax.ShapeDtypeStruct((D, 1536), F32),
                          in_specs=[VMEM_SPEC, VMEM_SPEC], out_specs=VMEM_SPEC,
                          compiler_params=pltpu.CompilerParams(vmem_limit_bytes=VMEM_LIMIT))(dmod_all, c_bc)


def _adamw(wt, g, m, v, name):
    rows, cols = wt.shape
    rb = _tile(rows, 256, 8)

    def fn(c, i, wv, gv, mv, vv):
        return _adam_math(wv, gv, mv, vv)

    return _rowcall(fn, [_rows(t, rb) for t in (wt, g, m, v)], [_orow(rows, cols, F32, rb)] * 3,
                    n_rows=rows, rb=rb, name=name)


def _pad_rows(t, rows):
    flat = t.reshape(-1)
    return jnp.pad(flat, (0, rows * 128 - flat.shape[0])).reshape(rows, 128)


SP_LAYOUT = (("ada_b", 48), ("norm1_w", 8), ("gla_gate_b", 8), ("gla_norm_w", 8), ("norm2_w", 8), ("conv_b", 48),
             ("final_norm_w", 8), (None, 8), ("gla_gate_w2", 16), ("conv_w", 40))


def _pack_small(d):
    return jnp.concatenate([jnp.zeros((rows, 128), F32) if n is None else _pad_rows(d[n].astype(F32), rows)
                            for n, rows in SP_LAYOUT], axis=0)


def _unpack_small(pk, shapes):
    out, off = {}, 0
    for n, rows in SP_LAYOUT:
        if n is not None:
            shp = shapes[n]
            out[n] = pk[off:off + rows].reshape(-1)[:math.prod(shp)].reshape(shp)
        off += rows
    return out


def kernel(x, c, positions, ada_w, ada_b, norm1_w, w_in, gla_gate_w2, gla_gate_b, gla_norm_w, w_gla_branch, w_attn_branch, w_out, norm2_w, w_up, conv_w, conv_b, w_down, final_norm_w, loss_target, m_ada_w, m_ada_b, m_norm1_w, m_w_in, m_gla_gate_w2, m_gla_gate_b, m_gla_norm_w, m_w_gla_branch, m_w_attn_branch, m_w_out, m_norm2_w, m_w_up, m_conv_w, m_conv_b, m_w_down, m_final_norm_w, v_ada_w, v_ada_b, v_norm1_w, v_w_in, v_gla_gate_w2, v_gla_gate_b, v_gla_norm_w, v_w_gla_branch, v_w_attn_branch, v_w_out, v_norm2_w, v_w_up, v_conv_w, v_conv_b, v_w_down, v_final_norm_w):
    s = x.shape[1]
    names = ("ada_w", "ada_b", "norm1_w", "w_in", "gla_gate_w2", "gla_gate_b", "gla_norm_w", "w_gla_branch", "w_attn_branch",
             "w_out", "norm2_w", "w_up", "conv_w", "conv_b", "w_down", "final_norm_w")
    wts = dict(zip(names, (ada_w, ada_b, norm1_w, w_in, gla_gate_w2, gla_gate_b, gla_norm_w, w_gla_branch, w_attn_branch,
                           w_out, norm2_w, w_up, conv_w, conv_b, w_down, final_norm_w)))
    ms = dict(zip(names, (m_ada_w, m_ada_b, m_norm1_w, m_w_in, m_gla_gate_w2, m_gla_gate_b, m_gla_norm_w, m_w_gla_branch,
                          m_w_attn_branch, m_w_out, m_norm2_w, m_w_up, m_conv_w, m_conv_b, m_w_down, m_final_norm_w)))
    vs = dict(zip(names, (v_ada_w, v_ada_b, v_norm1_w, v_w_in, v_gla_gate_w2, v_gla_gate_b, v_gla_norm_w, v_w_gla_branch,
                          v_w_attn_branch, v_w_out, v_norm2_w, v_w_up, v_conv_w, v_conv_b, v_w_down, v_final_norm_w)))

    pk0 = jnp.concatenate([_pad_rows(c, 8), _pad_rows(gla_gate_w2, 16), _pad_rows(conv_w, 40)], axis=0)
    sm_all = _allgather8(pk0, "gather_small")
    c_all = sm_all[:, 0:8, :].reshape(8, D)
    w2_full = sm_all[0::2, 8:24, :].transpose(1, 0, 2).reshape(GLA_LR, 512)
    cw_full = sm_all[0::2, 24:64, :].reshape(4, 40 * 128)[:, :3 * W_UP_SH].reshape(4, 3, W_UP_SH).transpose(1, 0, 2).reshape(3, 2 * D_FF)

    mod_sh = _mod_shard(c_all, ada_w[0])
    mod_all = _allgather8(mod_sh.reshape(96, 128), "gather_mod")

    w_sh = [wts[n].astype(BF16) for n in BIG]
    u_g0 = _u_gather_ici(w_sh, (0,))
    g0 = (u_g0,) + _unit_start(u_g0, "gather_w_in_start", after=[mod_all])
    mod = _mod_select(mod_all.reshape(8, 8, 12, 128) + g0[4][0, 0], ada_b.reshape(4, 12, 128)).reshape(6, D)

    core = lax.axis_index("c").astype(jnp.int32).reshape(1)
    chip = (2 * lax.axis_index("x") + lax.axis_index("y")).astype(jnp.int32)
    sm = dict(n1w=norm1_w, n2w=norm2_w, fnw=final_norm_w.reshape(1, D), gnw=gla_norm_w, gb=gla_gate_b,
              w2=jnp.pad(w2_full, ((0, 128 - GLA_LR), (0, 0))), cw=_ff_to_kernel(cw_full), cb=_ff_to_kernel(conv_b))
    loss, grad_x, halves, others, small, ts0 = _local_step(x[0], mod, positions.reshape(s, 1), loss_target[0], sm, w_sh,
                                                               g0, chip, core)

    dcw = _ff_from_kernel(small["cw"]).reshape(3, 4, W_UP_SH).transpose(1, 0, 2)
    dw2 = small["w2"][:GLA_LR].reshape(GLA_LR, 4, 128).transpose(1, 0, 2)
    sg = jnp.concatenate(
        [_pad_rows(small["dmod"], 48), _pad_rows(small["n1w"], 8), _pad_rows(small["gb"], 8), _pad_rows(small["gnw"], 8),
         _pad_rows(small["n2w"], 8), _pad_rows(_ff_from_kernel(small["cb"]), 48), _pad_rows(small["fnw"], 8), _pad_rows(loss, 8)]
        + [_pad_rows(dw2[k], 16) for k in range(4)] + [_pad_rows(dcw[k], 40) for k in range(4)], axis=0)
    sg_all = _allgather8(sg, "gather_small_grads")
    u_ex = _u_chip_exchange([ts0])
    pending = (u_ex,) + _unit_start(u_ex, "grad_exchange_w_in_start", after=[sg_all])
    sg_all = sg_all + pending[4][0, 0]
    g_small_pk = _small_reduce(sg_all)
    dmod_all = sg_all[:, 0:48, :].reshape(8, 4, 12, 128).transpose(1, 2, 0, 3)
    g_ada_w = _ada_grad(dmod_all, jnp.broadcast_to(c_all[:, :, None], (8, D, 128)))

    shapes = {n: wts[n].shape for n in names}
    g_small = _unpack_small(g_small_pk, shapes)
    grads = {"ada_w": g_ada_w.reshape(1, D, 1536), **g_small}
    deltas, new_m, new_v = {}, {}, {}
    for n, mine, theirs in zip(BIG[1:], halves, others):
        grads[n], deltas[n], new_m[n], new_v[n] = _adamw_halves(wts[n], ms[n], vs[n], mine, theirs, core, "adamw_" + n)
    shp = ada_w.shape
    d_, m_, v_ = _adamw(ada_w[0], g_ada_w, m_ada_w[0], v_ada_w[0], "adamw_ada_w")
    deltas["ada_w"], new_m["ada_w"], new_v["ada_w"] = d_.reshape(shp), m_.reshape(shp), v_.reshape(shp)
    d_, m_, v_ = _adamw(_pack_small(wts), g_small_pk, _pack_small(ms), _pack_small(vs), "adamw_small")
    for dst, pk in ((deltas, d_), (new_m, m_), (new_v, v_)):
        dst.update(_unpack_small(pk, shapes))

    [t0], [r0] = _unit_wait(*pending[:4], after=[d_, deltas["ada_w"], deltas["w_up"], deltas["w_down"]], name="grad_exchange_w_in_wait")
    half0 = _chip_sum(t0, r0, chip.reshape(1), "grad_chip_sum_w_in")
    [[oth0]] = _comm_call("grad_join_w_in", [_u_pair_join([half0])])
    grads["w_in"], deltas["w_in"], new_m["w_in"], new_v["w_in"] = _adamw_halves(w_in, m_w_in, v_w_in, half0, oth0, core, "adamw_w_in")

    return (g_small_pk[SG_LOSS, 0], grad_x.reshape(1, s, D), *[grads[n] for n in names], *[deltas[n] for n in names],
            *[new_m[n] for n in names], *[new_v[n] for n in names])
```

```python
import math

import jax
import jax.numpy as jnp
from jax import lax
from jax.experimental import pallas as pl
from jax.experimental.pallas import tpu as pltpu

F32, BF16 = jnp.float32, jnp.bfloat16
MESH = pl.DeviceIdType.MESH

D = 1024
EPS = 1e-6
GLA_H, GLA_DK, GLA_DV, GLA_LR = 4, 128, 256, 16
GLA_TAU = 16.0
GLA_CHUNK = 64
GLA_BLOCK = 512
ATT_GROUPS = ((128, 1), (512, 4), (2048, 16))
ATT_BLK = 128
ATT_HD = 64
ATT_W = 768
D_FF = 2816
ROPE_THETA = 10000.0
P_W = 7680
P_GV, P_GR, P_MA, P_MB, P_GQ, P_GK, P_AQ, P_AK, P_AV, P_LR = 0, 1024, 2048, 3072, 4096, 4608, 5120, 5888, 6656, 7424
W_IN = 7440
W_IN_SH, W_UP_SH, W_DOWN_SH = 1860, 1408, 704
VMEM_LIMIT = 56 * 1024 * 1024
ADAM_LR, ADAM_B1, ADAM_B2, ADAM_EPS, ADAM_WD, ADAM_STEP = 0.001, 0.9, 0.999, 1e-08, 0.01, 10
NEG = -1e30


def _tile(n, target, unit=128):
    best = None
    for t in range(unit, min(n, target) + 1, unit):
        if n % t == 0:
            best = t
    return best or n


def _params(sem):
    return pltpu.CompilerParams(dimension_semantics=sem, vmem_limit_bytes=VMEM_LIMIT)


def _dg(a, b, ca, cb):
    return lax.dot_general(a, b, (((ca,), (cb,)), ((), ())), preferred_element_type=F32)


def _sigmoid(v):
    return 1.0 / (1.0 + jnp.exp(-v))


def _ff_block(j):
    return (j % 2) * 2 + j // 2


def _mm(a, b, name, *, ta=False, tb=False, out_dtype=BF16, tm=1024, tn=1536, tk=1024, n_outer=True, comm=(),
        b_shards=False, o_shards=False):
    m = a.shape[1] if ta else a.shape[0]
    k = a.shape[0] if ta else a.shape[1]
    if b_shards:
        n = b.shape[1] if tb else 4 * W_UP_SH
        tn, tk = (tn, W_UP_SH) if tb else (W_UP_SH, tk)
    else:
        n = b.shape[0] if tb else b.shape[1]
    if o_shards:
        tn = W_UP_SH
    tm, tn, tk = _tile(m, tm), _tile(n, tn), _tile(k, tk)
    nm, nn, nk = m // tm, n // tn, k // tk
    in_out = out_dtype == F32
    c_ins, c_outs, c_alias, c_scratch = _carry(comm, 2, 1)

    def body(a_ref, b_ref, *rest):
        ci, o_ref, co = rest[:len(c_ins)], rest[len(c_ins)], rest[len(c_ins) + 1:len(c_ins) + 1 + len(c_outs)]
        scr = rest[len(c_ins) + 1 + len(c_outs):]
        kk = pl.program_id(2)
        if comm:
            step = (pl.program_id(0) * (nm if n_outer else nn) + pl.program_id(1)) * nk + kk

            @pl.when(step == 0)
            def _():
                _comm_phase(comm, ci, co, scr[-2], scr[-1], True)

        _mm_step(a_ref, b_ref, o_ref, scr, kk)
        if comm:
            @pl.when(step == nm * nn * nk - 1)
            def _():
                _comm_phase(comm, ci, co, scr[-2], scr[-1], False)

    def _mm_step(a_ref, b_ref, o_ref, scr, kk):
        p = _dg(a_ref[...].astype(BF16), b_ref[...].astype(BF16), 0 if ta else 1, 1 if tb else 0)
        if nk == 1:
            o_ref[...] = p.astype(o_ref.dtype)
        else:
            acc = o_ref if in_out else scr[0]

            @pl.when(kk == 0)
            def _():
                acc[...] = p

            @pl.when(kk > 0)
            def _():
                acc[...] += p

            if not in_out:
                @pl.when(kk == nk - 1)
                def _():
                    o_ref[...] = acc[...].astype(o_ref.dtype)

    if n_outer:
        ij = lambda g0, g1: (g1, g0)
        grid = (nn, nm, nk)
    else:
        ij = lambda g0, g1: (g0, g1)
        grid = (nm, nn, nk)
    a_map = (lambda g0, g1, kk: (kk, ij(g0, g1)[0])) if ta else (lambda g0, g1, kk: (ij(g0, g1)[0], kk))
    if b_shards and tb:
        b_spec = pl.BlockSpec((None, tn, tk), lambda g0, g1, kk: (_ff_block(kk), ij(g0, g1)[1], 0))
    elif b_shards:
        b_spec = pl.BlockSpec((None, tk, tn), lambda g0, g1, kk: (_ff_block(ij(g0, g1)[1]), kk, 0))
    elif tb:
        b_spec = pl.BlockSpec((tn, tk), lambda g0, g1, kk: (ij(g0, g1)[1], kk))
    else:
        b_spec = pl.BlockSpec((tk, tn), lambda g0, g1, kk: (kk, ij(g0, g1)[1]))
    if o_shards:
        o_spec = pl.BlockSpec((None, tm, tn), lambda g0, g1, kk: (_ff_block(ij(g0, g1)[1]), ij(g0, g1)[0], 0))
        o_shape = jax.ShapeDtypeStruct((4, m, W_UP_SH), out_dtype)
    else:
        o_spec = pl.BlockSpec((tm, tn), lambda g0, g1, kk: ij(g0, g1))
        o_shape = jax.ShapeDtypeStruct((m, n), out_dtype)
    res = pl.pallas_call(
        body, name=name, grid=grid,
        in_specs=[pl.BlockSpec((tk, tm) if ta else (tm, tk), a_map), b_spec] + [HBM] * len(c_ins),
        out_specs=[o_spec] + [HBM] * len(c_outs),
        out_shape=[o_shape] + c_outs,
        scratch_shapes=([] if (in_out or nk == 1) else [pltpu.VMEM((tm, tn), F32)]) + c_scratch,
        input_output_aliases=c_alias,
        compiler_params=_params(("arbitrary",) * 3 if comm else ("parallel", "parallel", "arbitrary")),
    )(a, b, *c_ins)
    return (res[0], _split_units(comm, res[1:])) if comm else res[0]


def _rows(arr, rb, w=None, j=0):
    w = arr.shape[1] if w is None else w
    if callable(j):
        return arr, pl.BlockSpec((rb, w), lambda c, i: (i, j(c)))
    return arr, pl.BlockSpec((rb, w), lambda c, i: (i, j))


def _full(arr, w=None, j=0):
    w = arr.shape[1] if w is None else w
    if callable(j):
        return arr, pl.BlockSpec((arr.shape[0], w), lambda c, i: (0, j(c)))
    return arr, pl.BlockSpec((arr.shape[0], w), lambda c, i: (0, j))


def _halo(arr, rb, hb, w, j, before):
    per = rb // hb
    last = arr.shape[0] // hb - 1
    if before:
        rmap = lambda i: jnp.maximum(i * per - 1, 0)
    else:
        rmap = lambda i: jnp.minimum((i + 1) * per, last)
    return arr, pl.BlockSpec((hb, w), lambda c, i: (rmap(i), j(c) if callable(j) else j))


def _rowcall(fn, ins, outs, *, n_rows, rb, name, ncol=1, into=None, after=()):
    n_in = len(ins)
    nr = n_rows // rb
    unread = ([] if into is None else [into[0]]) + list(after)
    n_skip = len(unread)

    def body(*refs):
        c, i = pl.program_id(0), pl.program_id(1)
        res = fn(c, i, *[r[...] for r in refs[:n_in]])
        for val, spec, o_ref in zip(res, outs, refs[n_in + n_skip:]):
            if spec[2] == "row":
                o_ref[...] = val.astype(o_ref.dtype)
            else:
                @pl.when(i == 0)
                def _(o_ref=o_ref, val=val):
                    o_ref[...] = val.astype(o_ref.dtype)

                @pl.when(i > 0)
                def _(o_ref=o_ref, val=val):
                    o_ref[...] += val.astype(o_ref.dtype)

    out_specs = []
    for shape, dt, kind, block, col in outs:
        if kind == "row":
            out_specs.append(pl.BlockSpec(block, lambda c, i, col=col: (i, col(c))))
        else:
            out_specs.append(pl.BlockSpec(block, lambda c, i, col=col: (0, col(c))))
    return pl.pallas_call(
        body, name=name, grid=(ncol, nr),
        in_specs=[s for _, s in ins] + [pl.BlockSpec(memory_space=pl.ANY)] * n_skip, out_specs=out_specs,
        out_shape=[jax.ShapeDtypeStruct(o[0], o[1]) for o in outs],
        input_output_aliases={} if into is None else {n_in: into[1]},
        compiler_params=_params(("parallel", "arbitrary")),
    )(*[a for a, _ in ins], *unread)


def _orow(n_rows, w, dt, rb, bw=None, col=lambda c: 0):
    return ((n_rows, w), dt, "row", (rb, bw or w), col)


def _oacc(r, w, bw=None, col=lambda c: 0):
    return ((r, w), F32, "acc", (r, bw or w), col)


def _csum(v):
    return jnp.sum(v, axis=0, keepdims=True)


def _rms(v):
    return lax.rsqrt(jnp.mean(v * v, axis=-1, keepdims=True) + EPS)


def _norm_bwd(xv, dh, w, scale):
    r = _rms(xv)
    xh = xv * r
    dxh = dh * (w * (1.0 + scale))
    dx = r * (dxh - xh * jnp.mean(dxh * xh, axis=-1, keepdims=True))
    t = dh * xh
    return dx, _csum(dh), _csum(t * w), _csum(t * (1.0 + scale))


def _rope_tables(pos_col, invf, s):
    def fn(c, i, pos, f):
        ang = pos.astype(F32) * f
        lane = lax.broadcasted_iota(jnp.int32, ang.shape, 1)
        sign = jnp.where((lane % ATT_HD) < ATT_HD // 2, -1.0, 1.0)
        return jnp.cos(ang), jnp.sin(ang) * sign

    rb = 512
    return _rowcall(fn, [_rows(pos_col, rb), _full(invf)], [_orow(s, 128, F32, rb), _orow(s, 128, F32, rb)],
                    n_rows=s, rb=rb, name="rope_tables")


def _swap_halves(t):
    n = t.shape[1]
    lane = lax.broadcasted_iota(jnp.int32, t.shape, 1)
    return jnp.where((lane % ATT_HD) < ATT_HD // 2, pltpu.roll(t, n - 32, 1), pltpu.roll(t, 32, 1))


def _rope_apply(t, cos, sin_signed, inverse):
    cw = jnp.concatenate([cos] * (t.shape[1] // 128), axis=1)
    sw = jnp.concatenate([sin_signed] * (t.shape[1] // 128), axis=1)
    if inverse:
        sw = -sw
    return t * cw + _swap_halves(t) * sw


DIL_ROWS = 512


def _to_dilated(scr, val, out_ref, r):
    if r == 1:
        out_ref[...] = val.astype(out_ref.dtype)
        return
    n = val.shape[0] // r
    for hh in range(2):
        scr[hh] = val[:, hh * 128:(hh + 1) * 128]
        for pr in range(r):
            out_ref[:, pr * 256 + hh * 128:pr * 256 + (hh + 1) * 128] = scr[hh, pl.ds(pr, n, stride=r), :].astype(out_ref.dtype)


def _from_dilated(scr, in_ref, r):
    if r == 1:
        return in_ref[...].astype(F32)
    n = in_ref.shape[0]
    for hh in range(2):
        for pr in range(r):
            scr[hh, pl.ds(pr, n, stride=r), :] = in_ref[:, pr * 256 + hh * 128:pr * 256 + (hh + 1) * 128].astype(F32)
    return jnp.concatenate([scr[0], scr[1]], axis=1)


def _dil_spec(r):
    return pl.BlockSpec((DIL_ROWS // r, r * 256), lambda i: (i, 0))


def _dil_shape(s, r, dt):
    return jax.ShapeDtypeStruct((s // r, r * 256), dt)


_DIL_SCRATCH = [pltpu.VMEM((2, DIL_ROWS, 128), F32)]
_RS = tuple(r for _, r in ATT_GROUPS)


def _rope_fwd(p, cos_t, sin_t, s):
    def body(*refs):
        ins, cs, sn, outs, scr = refs[:9], refs[9][...], refs[10][...], refs[11:20], refs[20]
        for t in range(3):
            for g, r in enumerate(_RS):
                val = ins[3 * t + g][...].astype(F32)
                _to_dilated(scr, _rope_apply(val, cs, sn, False) if t < 2 else val, outs[3 * t + g], r)

    res = pl.pallas_call(
        body, name="rope", grid=(s // DIL_ROWS,),
        in_specs=[pl.BlockSpec((DIL_ROWS, 256), lambda i, c=base // 256 + g: (i, c)) for base in (P_AQ, P_AK, P_AV) for g in range(3)]
        + [pl.BlockSpec((DIL_ROWS, 128), lambda i: (i, 0))] * 2,
        out_specs=[_dil_spec(r) for _ in range(3) for r in _RS],
        out_shape=[_dil_shape(s, r, BF16) for _ in range(3) for r in _RS],
        scratch_shapes=_DIL_SCRATCH, compiler_params=_params(("parallel",)),
    )(*([p] * 9), cos_t, sin_t)
    return res[0:3], res[3:6], res[6:9]


def _attn_combine(att, s):
    def body(o0, o1, o2, l0, l1, l2, o_ref, lse_ref, od1, od2, ld1, ld2, scr):
        ov = [_from_dilated(scr, ref, r) for ref, r in zip((o0, o1, o2), _RS)]
        lv = [_from_dilated(scr, ref, r) for ref, r in zip((l0, l1, l2), _RS)]
        mx = jnp.maximum(jnp.maximum(lv[0], lv[1]), lv[2])
        ev = [jnp.exp(l - mx) for l in lv]
        z = ev[0] + ev[1] + ev[2]
        o = ((ev[0] * ov[0] + ev[1] * ov[1] + ev[2] * ov[2]) / z).astype(BF16)
        lse = mx + jnp.log(z)
        o_ref[...] = o
        lse_ref[...] = lse
        for ref, r in zip((od1, od2), _RS[1:]):
            _to_dilated(scr, o.astype(F32), ref, r)
        for ref, r in zip((ld1, ld2), _RS[1:]):
            _to_dilated(scr, lse, ref, r)

    return pl.pallas_call(
        body, name="attn_combine", grid=(s // DIL_ROWS,),
        in_specs=[_dil_spec(r) for r in _RS] * 2,
        out_specs=[_dil_spec(1)] * 2 + [_dil_spec(r) for r in _RS[1:]] * 2,
        out_shape=[_dil_shape(s, 1, BF16), _dil_shape(s, 1, F32)] + [_dil_shape(s, r, BF16) for r in _RS[1:]]
        + [_dil_shape(s, r, F32) for r in _RS[1:]],
        scratch_shapes=_DIL_SCRATCH, compiler_params=_params(("parallel",)),
    )(*[a[0] for a in att], *[a[1] for a in att])


def _dilate(t, s):
    def body(t_ref, o1, o2, scr):
        val = t_ref[...].astype(F32)
        for ref, r in zip((o1, o2), _RS[1:]):
            _to_dilated(scr, val, ref, r)

    return pl.pallas_call(
        body, name="attn_dilate", grid=(s // DIL_ROWS,), in_specs=[_dil_spec(1)], out_specs=[_dil_spec(r) for r in _RS[1:]],
        out_shape=[_dil_shape(s, r, t.dtype) for r in _RS[1:]], scratch_shapes=_DIL_SCRATCH, compiler_params=_params(("parallel",)),
    )(t)


def _rope_bwd(datt, d_glr, dp, cos_t, sin_t, s):
    tail = P_W - P_AQ

    def body(*refs):
        ins, cs, sn, glr, o_ref, scr = refs[:9], refs[9][...], refs[10][...], refs[11], refs[13], refs[14]
        for t in range(3):
            for g, r in enumerate(_RS):
                val = _from_dilated(scr, ins[3 * t + g], r)
                o_ref[:, t * ATT_W + g * 256:t * ATT_W + (g + 1) * 256] = (_rope_apply(val, cs, sn, True) if t < 2 else val).astype(BF16)
        o_ref[:, 3 * ATT_W:3 * ATT_W + 128] = glr[...]
        o_ref[:, 3 * ATT_W + 128:] = jnp.zeros((DIL_ROWS, tail - 3 * ATT_W - 128), BF16)

    return pl.pallas_call(
        body, name="rope_bwd", grid=(s // DIL_ROWS,),
        in_specs=[_dil_spec(r) for _ in range(3) for r in _RS] + [pl.BlockSpec((DIL_ROWS, 128), lambda i: (i, 0))] * 3
        + [pl.BlockSpec(memory_space=pl.ANY)],
        out_specs=pl.BlockSpec((DIL_ROWS, tail), lambda i: (i, P_AQ // tail)),
        out_shape=jax.ShapeDtypeStruct((s, P_W), BF16), input_output_aliases={12: 0},
        scratch_shapes=_DIL_SCRATCH, compiler_params=_params(("parallel",)),
    )(*[datt[g][t] for t in range(3) for g in range(3)], cos_t, sin_t, d_glr, dp)


def _tri_dot(tri, t):
    tb = tri.astype(BF16)
    hi = t.astype(BF16)
    r1 = t - hi.astype(F32)
    mid = r1.astype(BF16)
    lo = (r1 - mid.astype(F32)).astype(BF16)
    return _dg(tb, hi, 1, 0) + _dg(tb, mid, 1, 0) + _dg(tb, lo, 1, 0)


def _gla_decays(la_c, tri):
    b = _tri_dot(tri, la_c)
    row = lax.broadcasted_iota(jnp.int32, b.shape, 0)
    bmid = jnp.sum(jnp.where(row == GLA_CHUNK // 2 - 1, b, 0.0), axis=0, keepdims=True)
    blast = jnp.sum(jnp.where(row == GLA_CHUNK - 1, b, 0.0), axis=0, keepdims=True)
    return b, bmid, blast


def _gla_fwd(p, la, s, comm=()):
    tb, ch = GLA_BLOCK, GLA_CHUNK
    nb, nc = s // tb, tb // ch
    scale = GLA_DK ** -0.5
    c_ins, c_outs, c_alias, c_scratch = _carry(comm, 4, 2)

    def body(q_ref, k_ref, v_ref, la_ref, *rest):
        ci, (o_ref, st_ref) = rest[:len(c_ins)], rest[len(c_ins):len(c_ins) + 2]
        co, state = rest[len(c_ins) + 2:len(c_ins) + 2 + len(c_outs)], rest[len(c_ins) + 2 + len(c_outs)]
        step = pl.program_id(0)
        if comm:
            @pl.when(step == 0)
            def _():
                _comm_phase(comm, ci, co, rest[-2], rest[-1], True)

        _gla_fwd_step(q_ref, k_ref, v_ref, la_ref, o_ref, st_ref, state)
        if comm:
            @pl.when(step == nb - 1)
            def _():
                _comm_phase(comm, ci, co, rest[-2], rest[-1], False)

    def _gla_fwd_step(q_ref, k_ref, v_ref, la_ref, o_ref, st_ref, state):
        @pl.when(pl.program_id(0) == 0)
        def _():
            state[...] = jnp.zeros_like(state)

        ri = lax.broadcasted_iota(jnp.int32, (ch, ch), 0)
        ci = lax.broadcasted_iota(jnp.int32, (ch, ch), 1)
        causal = ci <= ri
        tri = causal.astype(F32)

        def chunk(c, carry):
            sl = pl.ds(pl.multiple_of(c * ch, ch), ch)
            b, bmid, blast = _gla_decays(la_ref[sl, :], tri)
            q = q_ref[sl, :].astype(F32) * scale
            k = k_ref[sl, :].astype(F32)
            v = v_ref[sl, :]
            qgt = (q * jnp.exp(b)).astype(BF16)
            qgn = (q * jnp.exp(b - bmid)).astype(BF16)
            kgn = (k * jnp.exp(bmid - b)).astype(BF16)
            kd = (k * jnp.exp(blast - b)).astype(BF16)
            dec = jnp.exp(blast)
            sts = [state[h] for h in range(GLA_H)]
            outs, news = [], []
            for h in range(GLA_H):
                hk, hv = slice(h * GLA_DK, (h + 1) * GLA_DK), slice(h * GLA_DV, (h + 1) * GLA_DV)
                a = jnp.where(causal, _dg(qgn[:, hk], kgn[:, hk], 1, 1), 0.0)
                outs.append(_dg(a.astype(BF16), v[:, hv], 1, 0) + _dg(qgt[:, hk], sts[h].astype(BF16), 1, 1))
                news.append(dec[:, hk] * sts[h] + _dg(v[:, hv], kd[:, hk], 0, 0))
            for h in range(GLA_H):
                st_ref[h, c] = sts[h]
                state[h] = news[h]
            o_ref[sl, :] = jnp.concatenate(outs, axis=1)
            return carry

        lax.fori_loop(0, nc, chunk, 0, unroll=2)

    hw = GLA_H * GLA_DK
    res = pl.pallas_call(
        body, name="gla_fwd", grid=(nb,),
        in_specs=[pl.BlockSpec((tb, hw), lambda t: (t, P_GQ // hw)),
                  pl.BlockSpec((tb, hw), lambda t: (t, P_GK // hw)),
                  pl.BlockSpec((tb, GLA_H * GLA_DV), lambda t: (t, P_GV // (GLA_H * GLA_DV))),
                  pl.BlockSpec((tb, hw), lambda t: (t, 0))] + [HBM] * len(c_ins),
        out_specs=[pl.BlockSpec((tb, GLA_H * GLA_DV), lambda t: (t, 0)),
                   pl.BlockSpec((GLA_H, nc, GLA_DV, GLA_DK), lambda t: (0, t, 0, 0))] + [HBM] * len(c_outs),
        out_shape=[jax.ShapeDtypeStruct((s, GLA_H * GLA_DV), F32),
                   jax.ShapeDtypeStruct((GLA_H, s // ch, GLA_DV, GLA_DK), F32)] + c_outs,
        scratch_shapes=[pltpu.VMEM((GLA_H, GLA_DV, GLA_DK), F32)] + c_scratch,
        input_output_aliases=c_alias,
        compiler_params=_params(("arbitrary",)),
    )(p, p, p, la, *c_ins)
    return res[0], res[1], _split_units(comm, res[2:])


def _gla_bwd(p, la, states, do, s, dp, comm=()):
    tb, ch = GLA_BLOCK, GLA_CHUNK
    nb, nc = s // tb, tb // ch
    scale = GLA_DK ** -0.5
    c_ins, c_outs, c_alias, c_scratch = _carry(comm, 7, 4)

    def body(q_ref, k_ref, v_ref, la_ref, st_ref, do_ref, dp_in, *rest):
        ci, outs = rest[:len(c_ins)], rest[len(c_ins):len(c_ins) + 4]
        co, dstate = rest[len(c_ins) + 4:len(c_ins) + 4 + len(c_outs)], rest[len(c_ins) + 4 + len(c_outs)]
        step = pl.program_id(0)
        if comm:
            @pl.when(step == 0)
            def _():
                _comm_phase(comm, ci, co, rest[-2], rest[-1], True)

        _gla_bwd_step(q_ref, k_ref, v_ref, la_ref, st_ref, do_ref, *outs, dstate)
        if comm:
            @pl.when(step == nb - 1)
            def _():
                _comm_phase(comm, ci, co, rest[-2], rest[-1], False)

    def _gla_bwd_step(q_ref, k_ref, v_ref, la_ref, st_ref, do_ref, dq_ref, dk_ref, dv_ref, dla_ref, dstate):
        @pl.when(pl.program_id(0) == 0)
        def _():
            dstate[...] = jnp.zeros_like(dstate)

        ri = lax.broadcasted_iota(jnp.int32, (ch, ch), 0)
        ci = lax.broadcasted_iota(jnp.int32, (ch, ch), 1)
        causal = ci <= ri
        tri = causal.astype(F32)
        tri_t = (ci >= ri).astype(F32)

        def chunk(cc, carry):
            c = nc - 1 - cc
            sl = pl.ds(pl.multiple_of(c * ch, ch), ch)
            b, bmid, blast = _gla_decays(la_ref[sl, :], tri)
            q = q_ref[sl, :].astype(F32) * scale
            k = k_ref[sl, :].astype(F32)
            v = v_ref[sl, :]
            e_b, e_qn, e_kn, e_kd = jnp.exp(b), jnp.exp(b - bmid), jnp.exp(bmid - b), jnp.exp(blast - b)
            dec = jnp.exp(blast)
            qgt, qgn, kgn, kd = q * e_b, q * e_qn, k * e_kn, k * e_kd
            qgt_b, qgn_b, kgn_b, kd_b = qgt.astype(BF16), qgn.astype(BF16), kgn.astype(BF16), kd.astype(BF16)
            do_b = do_ref[sl, :].astype(BF16)
            st0s = [st_ref[h, c] for h in range(GLA_H)]
            dsts = [dstate[h] for h in range(GLA_H)]
            dqgn, dqgt, dkgn, dkd, dvs, ddec, news = [], [], [], [], [], [], []
            for h in range(GLA_H):
                hk, hv = slice(h * GLA_DK, (h + 1) * GLA_DK), slice(h * GLA_DV, (h + 1) * GLA_DV)
                dst_b = dsts[h].astype(BF16)
                a = jnp.where(causal, _dg(qgn_b[:, hk], kgn_b[:, hk], 1, 1), 0.0).astype(BF16)
                da = jnp.where(causal, _dg(do_b[:, hv], v[:, hv], 1, 1), 0.0).astype(BF16)
                dqgn.append(_dg(da, kgn_b[:, hk], 1, 0))
                dqgt.append(_dg(do_b[:, hv], st0s[h].astype(BF16), 1, 0))
                dkgn.append(_dg(da, qgn_b[:, hk], 0, 0))
                dvs.append(_dg(a, do_b[:, hv], 0, 0) + _dg(kd_b[:, hk], dst_b, 1, 1))
                dkd.append(_dg(v[:, hv], dst_b, 1, 0))
                ddec.append(jnp.sum(st0s[h] * dsts[h], axis=0, keepdims=True))
                news.append(dec[:, hk] * dsts[h] + _dg(do_b[:, hv], qgt_b[:, hk], 0, 0))
            for h in range(GLA_H):
                dstate[h] = news[h]
            cat = lambda parts: jnp.concatenate(parts, axis=1)
            dqgn, dqgt, dkgn, dkd, ddec = cat(dqgn), cat(dqgt), cat(dkgn), cat(dkd), cat(ddec)
            dq_ref[sl, :] = (scale * (dqgn * e_qn + dqgt * e_b)).astype(dq_ref.dtype)
            dk_ref[sl, :] = (dkgn * e_kn + dkd * e_kd).astype(dk_ref.dtype)
            dv_ref[sl, :] = cat(dvs).astype(dv_ref.dtype)
            db = dqgn * qgn + dqgt * qgt - dkgn * kgn - dkd * kd
            extra = jnp.sum(dkd * kd, axis=0, keepdims=True) + ddec * dec
            dla_ref[sl, :] = _tri_dot(tri_t, db) + extra
            return carry

        lax.fori_loop(0, nc, chunk, 0, unroll=2)

    rev = lambda t: nb - 1 - t
    hw, vw = GLA_H * GLA_DK, GLA_H * GLA_DV
    res = pl.pallas_call(
        body, name="gla_bwd", grid=(nb,),
        in_specs=[pl.BlockSpec((tb, hw), lambda t: (rev(t), P_GQ // hw)),
                  pl.BlockSpec((tb, hw), lambda t: (rev(t), P_GK // hw)),
                  pl.BlockSpec((tb, vw), lambda t: (rev(t), P_GV // vw)),
                  pl.BlockSpec((tb, hw), lambda t: (rev(t), 0)),
                  pl.BlockSpec((GLA_H, nc, GLA_DV, GLA_DK), lambda t: (0, rev(t), 0, 0)),
                  pl.BlockSpec((tb, vw), lambda t: (rev(t), 0)), pl.BlockSpec(memory_space=pl.ANY)] + [HBM] * len(c_ins),
        out_specs=[pl.BlockSpec((tb, hw), lambda t: (rev(t), 0)),
                   pl.BlockSpec((tb, hw), lambda t: (rev(t), 0)),
                   pl.BlockSpec((tb, vw), lambda t: (rev(t), P_GV // vw)),
                   pl.BlockSpec((tb, hw), lambda t: (rev(t), 0))] + [HBM] * len(c_outs),
        out_shape=[jax.ShapeDtypeStruct((s, hw), BF16),
                   jax.ShapeDtypeStruct((s, hw), BF16),
                   jax.ShapeDtypeStruct((s, P_W), BF16),
                   jax.ShapeDtypeStruct((s, hw), F32)] + c_outs,
        scratch_shapes=[pltpu.VMEM((GLA_H, GLA_DV, GLA_DK), F32)] + c_scratch,
        input_output_aliases={6: 2, **c_alias},
        compiler_params=_params(("arbitrary",)),
    )(p, p, p, la, states, do, dp, *c_ins)
    return res[0], res[1], res[2], res[3], _split_units(comm, res[4:])


def _head_masks():
    lane = lax.broadcasted_iota(jnp.int32, (1, 4 * ATT_HD), 1)
    return [(lane >= h * ATT_HD) & (lane < (h + 1) * ATT_HD) for h in range(4)]


def _attn_fwd(qv, kv, pv, g, r, s):
    ln = s // r
    nblk = ln // ATT_BLK
    qcol = lambda pr: pr
    vcol = qcol
    prev = lambda n: jnp.maximum(n - 1, 0)

    def body(q_ref, kp_ref, kc_ref, vp_ref, vc_ref, o_ref, lse_ref):
        has_prev = pl.program_id(1) > 0
        ri = lax.broadcasted_iota(jnp.int32, (ATT_BLK, ATT_BLK), 0)
        ci = lax.broadcasted_iota(jnp.int32, (ATT_BLK, ATT_BLK), 1)
        m_cur = ci <= ri
        m_prev = (ci >= ri) & has_prev
        q, kp, kc, vp, vc = q_ref[...], kp_ref[...], kc_ref[...], vp_ref[...], vc_ref[...]
        o = jnp.zeros((ATT_BLK, 256), F32)
        lse = jnp.zeros((ATT_BLK, 256), F32)
        for hm in _head_masks():
            qm = jnp.where(hm, q, jnp.zeros_like(q))
            sc = jnp.where(m_cur, _dg(qm, kc, 1, 1) * 0.125, NEG)
            sp = jnp.where(m_prev, _dg(qm, kp, 1, 1) * 0.125, NEG)
            mx = jnp.maximum(jnp.max(sc, axis=1, keepdims=True), jnp.max(sp, axis=1, keepdims=True))
            pc, pp = jnp.exp(sc - mx), jnp.exp(sp - mx)
            den = jnp.sum(pc, axis=1, keepdims=True) + jnp.sum(pp, axis=1, keepdims=True)
            oh = (_dg(pc.astype(BF16), vc, 1, 0) + _dg(pp.astype(BF16), vp, 1, 0)) / den
            o = jnp.where(hm, oh, o)
            lse = jnp.where(hm, mx + jnp.log(den), lse)
        o_ref[...] = o.astype(o_ref.dtype)
        lse_ref[...] = lse

    blk = (ATT_BLK, 256)
    o, lse = pl.pallas_call(
        body, name=f"attn_fwd_{g}", grid=(r, nblk),
        in_specs=[pl.BlockSpec(blk, lambda pr, n: (n, qcol(pr))),
                  pl.BlockSpec(blk, lambda pr, n: (prev(n), qcol(pr))),
                  pl.BlockSpec(blk, lambda pr, n: (n, qcol(pr))),
                  pl.BlockSpec(blk, lambda pr, n: (prev(n), vcol(pr))),
                  pl.BlockSpec(blk, lambda pr, n: (n, vcol(pr)))],
        out_specs=[pl.BlockSpec(blk, lambda pr, n: (n, pr)), pl.BlockSpec(blk, lambda pr, n: (n, pr))],
        out_shape=[jax.ShapeDtypeStruct((ln, r * 256), BF16), jax.ShapeDtypeStruct((ln, r * 256), F32)],
        compiler_params=_params(("parallel", "parallel")),
    )(qv, kv, kv, pv, pv)
    return o, lse


def _attn_bwd(qv, kv, pv, dov, ov, lv, g, r, s):
    ln = s // r
    nblk = ln // ATT_BLK
    qcol = lambda pr: pr
    vcol = qcol
    prev = lambda n: jnp.maximum(n - 1, 0)
    nxt = lambda n: jnp.minimum(n + 1, nblk - 1)

    def body(qc_ref, qn_ref, kp_ref, kc_ref, vp_ref, vc_ref, doc_ref, don_ref, oc_ref, on_ref, lc_ref, ln_ref,
             dq_ref, dk_ref, dv_ref):
        n = pl.program_id(1)
        has_prev, has_next = n > 0, n < nblk - 1
        ri = lax.broadcasted_iota(jnp.int32, (ATT_BLK, ATT_BLK), 0)
        ci = lax.broadcasted_iota(jnp.int32, (ATT_BLK, ATT_BLK), 1)
        m_cur = ci <= ri
        m_prev = (ci >= ri) & has_prev
        m_next = (ci >= ri) & has_next
        qc, qn, kp, kc, vp, vc = qc_ref[...], qn_ref[...], kp_ref[...], kc_ref[...], vp_ref[...], vc_ref[...]
        doc, don = doc_ref[...], don_ref[...]
        pc_full = doc.astype(F32) * oc_ref[...].astype(F32)
        pn_full = don.astype(F32) * on_ref[...].astype(F32)
        lc, lnx = lc_ref[...], ln_ref[...]
        dq = jnp.zeros((ATT_BLK, 256), F32)
        dk = jnp.zeros((ATT_BLK, 256), F32)
        dv = jnp.zeros((ATT_BLK, 256), F32)
        zb = jnp.zeros_like(qc)
        for hm in _head_masks():
            qcm, qnm = jnp.where(hm, qc, zb), jnp.where(hm, qn, zb)
            docm, donm = jnp.where(hm, doc, zb), jnp.where(hm, don, zb)
            lse_c = jnp.max(jnp.where(hm, lc, NEG), axis=1, keepdims=True)
            lse_n = jnp.max(jnp.where(hm, lnx, NEG), axis=1, keepdims=True)
            del_c = jnp.sum(jnp.where(hm, pc_full, 0.0), axis=1, keepdims=True)
            del_n = jnp.sum(jnp.where(hm, pn_full, 0.0), axis=1, keepdims=True)
            pr_ = jnp.where(m_cur, jnp.exp(_dg(qcm, kc, 1, 1) * 0.125 - lse_c), 0.0)
            ds = (pr_ * (_dg(docm, vc, 1, 1) - del_c) * 0.125).astype(BF16)
            dqh = _dg(ds, kc, 1, 0)
            dkh = _dg(ds, qc, 0, 0)
            dvh = _dg(pr_.astype(BF16), doc, 0, 0)
            pr_ = jnp.where(m_prev, jnp.exp(_dg(qcm, kp, 1, 1) * 0.125 - lse_c), 0.0)
            ds = (pr_ * (_dg(docm, vp, 1, 1) - del_c) * 0.125).astype(BF16)
            dqh = dqh + _dg(ds, kp, 1, 0)
            pr_ = jnp.where(m_next, jnp.exp(_dg(qnm, kc, 1, 1) * 0.125 - lse_n), 0.0)
            ds = (pr_ * (_dg(donm, vc, 1, 1) - del_n) * 0.125).astype(BF16)
            dkh = dkh + _dg(ds, qn, 0, 0)
            dvh = dvh + _dg(pr_.astype(BF16), don, 0, 0)
            dq = jnp.where(hm, dqh, dq)
            dk = jnp.where(hm, dkh, dk)
            dv = jnp.where(hm, dvh, dv)
        dq_ref[...] = dq.astype(dq_ref.dtype)
        dk_ref[...] = dk.astype(dk_ref.dtype)
        dv_ref[...] = dv.astype(dv_ref.dtype)

    blk = (ATT_BLK, 256)
    cur = lambda col: pl.BlockSpec(blk, lambda pr, n: (n, col(pr)))
    prv = lambda col: pl.BlockSpec(blk, lambda pr, n: (prev(n), col(pr)))
    nx = lambda col: pl.BlockSpec(blk, lambda pr, n: (nxt(n), col(pr)))
    own = lambda pr: pr
    outs = pl.pallas_call(
        body, name=f"attn_bwd_{g}", grid=(r, nblk),
        in_specs=[cur(qcol), nx(qcol), prv(qcol), cur(qcol), prv(vcol), cur(vcol),
                  cur(own), nx(own), cur(own), nx(own), cur(own), nx(own)],
        out_specs=[cur(own), cur(own), cur(own)],
        out_shape=[jax.ShapeDtypeStruct((ln, r * 256), BF16)] * 3,
        compiler_params=_params(("parallel", "parallel")),
    )(qv, qv, kv, kv, pv, pv, dov, dov, ov, ov, lv, lv)
    return outs


def _gelu_parts(gv):
    cdf = 0.5 * (1.0 + lax.erf(gv * (2.0 ** -0.5)))
    pdf = jnp.exp(-0.5 * gv * gv) * (1.0 / math.sqrt(2.0 * math.pi))
    return cdf, pdf


def _pick_row(t, k):
    row = lax.broadcasted_iota(jnp.int32, t.shape, 0)
    return jnp.sum(jnp.where(row == k, t, 0.0), axis=0, keepdims=True)


def _shift_rows(u, halo, n):
    row = lax.broadcasted_iota(jnp.int32, u.shape, 0)
    out = pltpu.roll(u, n, 0)
    for k in range(n):
        out = jnp.where(row == k, _pick_row(halo, 16 - n + k), out)
    return out


def _shift_rows_up(u, halo, n):
    rb = u.shape[0]
    row = lax.broadcasted_iota(jnp.int32, u.shape, 0)
    out = pltpu.roll(u, rb - n, 0)
    for k in range(n):
        out = jnp.where(row == rb - n + k, _pick_row(halo, k), out)
    return out


def _conv(u, halo, cw, cb):
    return cb + _pick_row(cw, 0) * _shift_rows(u, halo, 2) + _pick_row(cw, 1) * _shift_rows(u, halo, 1) + _pick_row(cw, 2) * u


def _local_step(x, mod, pos_col, target, sm, w_sh, g0, chip, core):
    s = x.shape[0]
    shift1, scale1, gate1, shift2, scale2, gate2 = [mod[i:i + 1, :] for i in range(6)]
    rb = 512
    chip1 = chip.reshape(1)

    def f_norm1(c, i, xv, nw, sc, sh):
        return ((xv * _rms(xv) * nw) * (1.0 + sc) + sh,)

    (h,) = _rowcall(f_norm1, [_rows(x, rb), _full(sm["n1w"]), _full(scale1), _full(shift1)],
                    [_orow(s, D, BF16, rb)], n_rows=s, rb=rb, name="norm1")
    own = lambda got, i: lax.dynamic_update_slice(got, w_sh[i], (chip, 0, 0))
    invf = jnp.tile(ROPE_THETA ** (-jnp.arange(ATT_HD // 2, dtype=F32) / (ATT_HD // 2)), 4).reshape(1, 128)
    cos_t, sin_t = _rope_tables(pos_col, invf, s)
    _, got0 = _unit_wait(*g0[:4], after=[h, cos_t, sin_t], name="gather_w_in_wait")
    [got0] = _comm_call("gather_w_in_d2d", [_u_gather_d2d(got0, (0,))])
    u_g1 = _u_gather_ici(w_sh, (1, 2, 3, 4, 5))
    g1 = _unit_start(u_g1, "gather_weights_start", after=got0)
    w = dict(win=_win_assemble(got0[0], w_sh[0], after=g1[3:]))
    p = _mm(h, w["win"], "in_proj", tm=2048, tn=1536)

    def f_gla_pre(c, i, glr, w2, gb):
        z = _dg(glr, w2.astype(BF16), 1, 0) + gb
        return ((jnp.minimum(z, 0.0) - jnp.log(1.0 + jnp.exp(-jnp.abs(z)))) * (1.0 / GLA_TAU),)

    (la,) = _rowcall(f_gla_pre, [_rows(p, rb, 128, P_LR // 128), _full(sm["w2"]), _full(sm["gb"])],
                     [_orow(s, 512, F32, rb)], n_rows=s, rb=rb, name="gla_pre")
    o_gla, states, _ = _gla_fwd(p, la, s)
    _, got = _unit_wait(u_g1, *g1[:3], after=[o_gla], name="gather_weights_wait")
    [got123] = _comm_call("gather_weights_d2d", [_u_gather_d2d(got[:3], (1, 2, 3))])
    got45 = got[3:]
    w.update(wgb=own(got123[0], 1).reshape(1024, D), wab=_cols_join(own(got123[1], 2)), wout=own(got123[2], 3).reshape(D, D))

    def f_gla_post(c, i, ov, gnw, gr):
        on = jnp.concatenate([ov[:, k * 256:(k + 1) * 256] * _rms(ov[:, k * 256:(k + 1) * 256]) * gnw
                              for k in range(GLA_H)], axis=1)
        g = gr.astype(F32)
        return (on * (g * _sigmoid(g)),)

    (og,) = _rowcall(f_gla_post, [_rows(o_gla, rb), _full(sm["gnw"]), _rows(p, rb, 1024, P_GR // 1024)],
                     [_orow(s, 1024, BF16, rb)], n_rows=s, rb=rb, name="gla_post")
    y_gla = _mm(og, w["wgb"], "gla_branch")

    q_d, k_d, v_d = _rope_fwd(p, cos_t, sin_t, s)
    att = [_attn_fwd(q_d[g], k_d[g], v_d[g], g, r, s) for g, r in enumerate(_RS)]
    o_att, lse, o_d1, o_d2, lse_d1, lse_d2 = _attn_combine(att, s)
    y_att = _mm(o_att, w["wab"], "attn_branch")

    def f_merge(c, i, ma, mb, yg, ya):
        return (_sigmoid(ma.astype(F32)) * yg.astype(F32) + _sigmoid(mb.astype(F32)) * ya.astype(F32),)

    (mixed,) = _rowcall(f_merge, [_rows(p, rb, D, P_MA // D), _rows(p, rb, D, P_MB // D), _rows(y_gla, rb), _rows(y_att, rb)],
                        [_orow(s, D, BF16, rb)], n_rows=s, rb=rb, name="merge")
    z1, [got45] = _mm(mixed, w["wout"], "out_proj", comm=[_u_gather_d2d(got45, (4, 5))])
    w.update(wup=own(got45[0], 4), wdown=own(got45[1], 5).reshape(D_FF, D))

    def f_norm2(c, i, xv, z, g1, nw, sc, sh):
        x1 = xv + g1 * z.astype(F32)
        return (x1, (x1 * _rms(x1) * nw) * (1.0 + sc) + sh)

    x1, h2 = _rowcall(f_norm2, [_rows(x, rb), _rows(z1, rb), _full(gate1), _full(sm["n2w"]), _full(scale2), _full(shift2)],
                      [_orow(s, D, F32, rb), _orow(s, D, BF16, rb)], n_rows=s, rb=rb, name="norm2")
    u = _mm(h2, w["wup"], "up_proj", tm=2048, b_shards=True)

    cwid = 2 * W_UP_SH

    def f_ffn(c, i, uv, hl, cw, cb):
        uc = _conv(uv.astype(F32), hl.astype(F32) * (i > 0).astype(F32), cw, cb)
        val, gt = uc[:, :W_UP_SH], uc[:, W_UP_SH:]
        cdf, _ = _gelu_parts(gt)
        return (gt * cdf * val,)

    ccol = lambda c: c
    rw = 256
    (hidden,) = _rowcall(f_ffn, [_rows(u, rw, cwid, ccol), _halo(u, rw, 16, cwid, ccol, True),
                                 _full(sm["cw"], cwid, ccol), _full(sm["cb"], cwid, ccol)],
                         [_orow(s, D_FF, BF16, rw, W_UP_SH, ccol)], n_rows=s, rb=rw, name="conv_geglu", ncol=2)
    z2 = _mm(hidden, w["wdown"], "down_proj", tk=D_FF)

    def f_final(c, i, x1v, z, g2, fw, tgt):
        x2 = x1v + g2 * z.astype(F32)
        r = _rms(x2)
        xh = x2 * r
        e = xh * fw - tgt
        loss = 0.5 * jnp.sum(jnp.mean(e * e, axis=-1, keepdims=True), axis=0, keepdims=True)
        dy = e * (1.0 / D)
        dxh = dy * fw
        dx2 = r * (dxh - xh * jnp.mean(dxh * xh, axis=-1, keepdims=True))
        return (loss, dx2, dx2 * g2, _csum(dy * xh), _csum(dx2 * z.astype(F32)))

    loss, dx2, dz2, d_fnw, d_gate2 = _rowcall(
        f_final, [_rows(x1, rb), _rows(z2, rb), _full(gate2), _full(sm["fnw"]), _rows(target, rb)],
        [_oacc(1, 1), _orow(s, D, F32, rb), _orow(s, D, BF16, rb), _oacc(1, D), _oacc(1, D)],
        n_rows=s, rb=rb, name="final_loss")
    d_hidden = _mm(dz2, w["wdown"], "down_proj_dx", tb=True, tn=1408)
    g_wdown = _mm(hidden, dz2, "down_proj_dw", ta=True, out_dtype=F32, tm=1408, tn=1024, tk=2048)

    def f_ffn_bwd(c, i, uv, hl, dh, cw, cb):
        uf = uv.astype(F32)
        hf = hl.astype(F32) * (i > 0).astype(F32)
        u1, u2 = _shift_rows(uf, hf, 1), _shift_rows(uf, hf, 2)
        uc = cb + _pick_row(cw, 0) * u2 + _pick_row(cw, 1) * u1 + _pick_row(cw, 2) * uf
        val, gt = uc[:, :W_UP_SH], uc[:, W_UP_SH:]
        cdf, pdf = _gelu_parts(gt)
        dhf = dh.astype(F32)
        duc = jnp.concatenate([dhf * (gt * cdf), dhf * val * (cdf + gt * pdf)], axis=1)
        dcw = jnp.concatenate([_csum(duc * u2), _csum(duc * u1), _csum(duc * uf)], axis=0)
        return (duc, _csum(duc), dcw)

    duc, d_cb, d_cw = _rowcall(
        f_ffn_bwd, [_rows(u, rw, cwid, ccol), _halo(u, rw, 16, cwid, ccol, True), _rows(d_hidden, rw, W_UP_SH, ccol),
                    _full(sm["cw"], cwid, ccol), _full(sm["cb"], cwid, ccol)],
        [_orow(s, 2 * D_FF, BF16, rw, cwid, ccol), _oacc(1, 2 * D_FF, cwid, ccol), _oacc(3, 2 * D_FF, cwid, ccol)],
        n_rows=s, rb=rw, name="conv_geglu_bwd", ncol=2)

    def f_conv_t(c, i, dv, hl, cw):
        df = dv.astype(F32)
        hf = hl.astype(F32) * (i < s // rw - 1).astype(F32)
        return (_pick_row(cw, 2) * df + _pick_row(cw, 1) * _shift_rows_up(df, hf, 1) + _pick_row(cw, 0) * _shift_rows_up(df, hf, 2),)

    (du,) = _rowcall(f_conv_t, [_rows(duc, rw, cwid, ccol), _halo(duc, rw, 16, cwid, ccol, False), _full(sm["cw"], cwid, ccol)],
                     [_orow(s, 2 * D_FF, BF16, rw, cwid, ccol)], n_rows=s, rb=rw, name="conv_transpose", ncol=2)
    g_wup = _mm(h2, du, "up_proj_dw", ta=True, out_dtype=F32, tm=1024, tk=2048, o_shards=True)
    gs45 = [g_wup, g_wdown.reshape(4, W_DOWN_SH, 1024)]
    d_h2, [land45] = _mm(du, w["wup"], "up_proj_dx", tb=True, tm=2048, b_shards=True, comm=[_u_pair_send(gs45, (4, 5))])
    ts45 = [_pair_add(g, ld, core, "grad_pair_add_" + BIG[i]) for g, ld, i in zip(gs45, land45, (4, 5))]
    u_ex4 = _u_chip_exchange(ts45[:1])
    ex4 = _unit_start(u_ex4, "grad_exchange_w_up_start")

    def f_norm2_bwd(c, i, x1v, dh, dxr, z, nw, sc, g1):
        dxn, dsh, dsc, dnw = _norm_bwd(x1v, dh.astype(F32), nw, sc)
        dx1 = dxr + dxn
        return (dx1, dx1 * g1, dsh, dsc, dnw, _csum(dx1 * z.astype(F32)))

    dx1, dz1, d_shift2, d_scale2, d_n2w, d_gate1 = _rowcall(
        f_norm2_bwd, [_rows(x1, rb), _rows(d_h2, rb), _rows(dx2, rb), _rows(z1, rb), _full(sm["n2w"]), _full(scale2), _full(gate1)],
        [_orow(s, D, F32, rb), _orow(s, D, BF16, rb), _oacc(1, D), _oacc(1, D), _oacc(1, D), _oacc(1, D)],
        n_rows=s, rb=rb, name="norm2_bwd", after=ex4[3:])
    d_mixed = _mm(dz1, w["wout"], "out_proj_dx", tb=True)
    g_wout = _mm(mixed, dz1, "out_proj_dw", ta=True, out_dtype=F32, tk=2048)

    def f_merge_bwd(c, i, dm, ma, mb, yg, ya):
        dmf, ygf, yaf = dm.astype(F32), yg.astype(F32), ya.astype(F32)
        sa, sb = _sigmoid(ma.astype(F32)), _sigmoid(mb.astype(F32))
        return (dmf * sa, dmf * sb, jnp.concatenate([dmf * ygf * sa * (1.0 - sa), dmf * yaf * sb * (1.0 - sb)], axis=1))

    dy_gla, dy_att, dp = _rowcall(
        f_merge_bwd, [_rows(d_mixed, rb), _rows(p, rb, D, P_MA // D), _rows(p, rb, D, P_MB // D), _rows(y_gla, rb), _rows(y_att, rb)],
        [_orow(s, D, BF16, rb)] * 2 + [_orow(s, P_W, BF16, rb, 2 * D, lambda c: P_MA // (2 * D))], n_rows=s, rb=rb, name="merge_bwd")
    d_og = _mm(dy_gla, w["wgb"], "gla_branch_dx", tb=True)
    g_wgb = _mm(og, dy_gla, "gla_branch_dw", ta=True, out_dtype=F32, tk=2048)
    d_oatt = _mm(dy_att, w["wab"], "attn_branch_dx", tb=True)
    g_wab = _mm(o_att, dy_att, "attn_branch_dw", ta=True, out_dtype=F32, tk=2048)

    def f_gla_post_bwd(c, i, ov, gnw, gr, dog):
        g = gr.astype(F32)
        sg = _sigmoid(g)
        silu = g * sg
        dof = dog.astype(F32)
        don = dof * silu
        on_parts, do_parts, dgn = [], [], jnp.zeros((1, 256), F32)
        for k in range(GLA_H):
            oh = ov[:, k * 256:(k + 1) * 256]
            dh = don[:, k * 256:(k + 1) * 256]
            r = _rms(oh)
            xh = oh * r
            dgn = dgn + _csum(dh * xh)
            dxh = dh * gnw
            do_parts.append(r * (dxh - xh * jnp.mean(dxh * xh, axis=-1, keepdims=True)))
            on_parts.append(xh * gnw)
        on = jnp.concatenate(on_parts, axis=1)
        dgr = dof * on * (sg * (1.0 + g * (1.0 - sg)))
        return (jnp.concatenate(do_parts, axis=1), dgr, dgn)

    do_gla, dp, d_gnw = _rowcall(
        f_gla_post_bwd, [_rows(o_gla, rb), _full(sm["gnw"]), _rows(p, rb, 1024, P_GR // 1024), _rows(d_og, rb)],
        [_orow(s, 1024, F32, rb), _orow(s, P_W, BF16, rb, 1024, lambda c: P_GR // 1024), _oacc(1, 256)],
        n_rows=s, rb=rb, name="gla_post_bwd", into=(dp, 1))
    gs123 = [g_wgb.reshape(4, 256, 1024), _cols_split(g_wab), g_wout.reshape(4, 256, 1024)]
    d_gq, d_gk, dp, d_la, [land123] = _gla_bwd(p, la, states, do_gla, s, dp, comm=[_u_pair_send(gs123, (1, 2, 3))])
    ts123 = [_pair_add(g, ld, core, "grad_pair_add_" + BIG[i]) for g, ld, i in zip(gs123, land123, (1, 2, 3))]

    def f_gla_pre_bwd(c, i, lav, dlav, glr, w2):
        dz = dlav * (1.0 / GLA_TAU) * (1.0 - jnp.exp(GLA_TAU * lav))
        dzb = dz.astype(BF16)
        return (_dg(dzb, w2.astype(BF16), 1, 1), _csum(dz), _dg(glr, dzb, 0, 0))

    d_glr, d_gb, d_w2 = _rowcall(
        f_gla_pre_bwd, [_rows(la, rb), _rows(d_la, rb), _rows(p, rb, 128, P_LR // 128), _full(sm["w2"])],
        [_orow(s, 128, BF16, rb), _oacc(1, 512), _oacc(128, 512)], n_rows=s, rb=rb, name="gla_pre_bwd")

    do_d = [d_oatt] + list(_dilate(d_oatt, s))
    datt = [_attn_bwd(q_d[g], k_d[g], v_d[g], do_d[g], (o_att, o_d1, o_d2)[g], (lse, lse_d1, lse_d2)[g], g, r, s)
            for g, r in enumerate(_RS)]
    dp = _rope_bwd(datt, d_glr, dp, cos_t, sin_t, s)
    dp = lax.dynamic_update_slice(dp, jnp.concatenate([d_gq, d_gk], axis=1), (0, P_GQ))
    [t4], [r4] = _unit_wait(u_ex4, *ex4[:3], after=[dp], name="grad_exchange_w_up_wait")
    half4 = [_chip_sum(t4, r4, chip1, "grad_chip_sum_w_up")]
    g_win, [r1235, oth4] = _mm(h, dp, "in_proj_dw", ta=True, out_dtype=F32, tm=1024, tn=1536, tk=2048,
                               comm=[_u_chip_exchange(ts123 + ts45[1:]), _u_pair_join(half4)])
    half1235 = [_chip_sum(t, r, chip1, "grad_chip_sum_" + BIG[i]) for t, r, i in zip(ts123 + ts45[1:], r1235, (1, 2, 3, 5))]
    gs0 = [_win_split(g_win)]
    d_h, [land0, oth1235] = _mm(dp, w["win"], "in_proj_dx", tb=True, tk=3840,
                                comm=[_u_pair_send(gs0, (0,)), _u_pair_join(half1235)])
    half123, half45 = half1235[:3], half4 + half1235[3:]
    oth123, oth45 = oth1235[:3], oth4 + oth1235[3:]
    ts0 = _pair_add(gs0[0], land0[0], core, "grad_pair_add_w_in")

    def f_norm1_bwd(c, i, xv, dh, dxr, nw, sc):
        dxn, dsh, dsc, dnw = _norm_bwd(xv, dh.astype(F32), nw, sc)
        return (dxr + dxn, dsh, dsc, dnw)

    grad_x, d_shift1, d_scale1, d_n1w = _rowcall(
        f_norm1_bwd, [_rows(x, rb), _rows(d_h, rb), _rows(dx1, rb), _full(sm["n1w"]), _full(scale1)],
        [_orow(s, D, F32, rb), _oacc(1, D), _oacc(1, D), _oacc(1, D)], n_rows=s, rb=rb, name="norm1_bwd")

    dmod = jnp.concatenate([d_shift1, d_scale1, d_gate1, d_shift2, d_scale2, d_gate2], axis=1)
    small = dict(dmod=dmod, n1w=d_n1w, gb=d_gb, gnw=d_gnw, n2w=d_n2w, cb=d_cb, fnw=d_fnw, w2=d_w2, cw=d_cw)
    return loss, grad_x, half123 + half45, oth123 + oth45, small, ts0


def _win_pieces():
    runs = [(P_GV, 1024, 2048), (P_MA, 5392, 2048), (P_GQ, 0, 1024), (P_AQ, 3088, 2304), (P_LR, 3072, GLA_LR)]
    out = []
    for kc, rc, ln in runs:
        while ln > 0:
            step = min(ln, W_IN_SH - rc % W_IN_SH)
            out.append((kc, rc, step))
            kc, rc, ln = kc + step, rc + step, ln - step
    return out


def _win_assemble(shards, own, after=()):
    rb = 256

    def body(s_ref, own_ref, *rest):
        o_ref = rest[-1]
        x, y, _ = _me()
        o_ref[:, W_IN:] = jnp.zeros((rb, P_W - W_IN), o_ref.dtype)
        for kc, rc, ln in _win_pieces():
            sh, lo = rc // W_IN_SH, rc % W_IN_SH
            o_ref[:, kc:kc + ln] = jnp.where(2 * x + y == sh, own_ref[0, :, lo:lo + ln], s_ref[sh, :, lo:lo + ln])

    return pl.pallas_call(
        body, name="w_in_assemble", grid=(D // rb,),
        in_specs=[pl.BlockSpec((4, rb, W_IN_SH), lambda i: (0, i, 0)), pl.BlockSpec((1, rb, W_IN_SH), lambda i: (0, i, 0))]
        + [pl.BlockSpec(memory_space=pl.ANY)] * len(after),
        out_specs=pl.BlockSpec((rb, P_W), lambda i: (i, 0)),
        out_shape=jax.ShapeDtypeStruct((D, P_W), shards.dtype), compiler_params=_params(("parallel",)),
    )(shards, own, *after)


def _win_split(g):
    rb = 256

    def body(g_ref, o_ref):
        for kc, rc, ln in _win_pieces():
            o_ref[rc // W_IN_SH, :, rc % W_IN_SH:rc % W_IN_SH + ln] = g_ref[:, kc:kc + ln]

    return pl.pallas_call(
        body, name="w_in_grad_split", grid=(D // rb,),
        in_specs=[pl.BlockSpec((rb, P_W), lambda i: (i, 0))], out_specs=pl.BlockSpec((4, rb, W_IN_SH), lambda i: (0, i, 0)),
        out_shape=jax.ShapeDtypeStruct((4, D, W_IN_SH), g.dtype), compiler_params=_params(("parallel",)),
    )(g)


def _ff_to_kernel(a):
    h = W_UP_SH
    return jnp.concatenate([a[:, 0:h], a[:, D_FF:D_FF + h], a[:, h:D_FF], a[:, D_FF + h:]], axis=1)


def _ff_from_kernel(a):
    h = W_UP_SH
    return jnp.concatenate([a[:, 0:h], a[:, 2 * h:3 * h], a[:, h:2 * h], a[:, 3 * h:]], axis=1)


BIG = ("w_in", "w_gla_branch", "w_attn_branch", "w_out", "w_up", "w_down")
SH_SHAPES = ((1024, W_IN_SH), (256, 1024), (256, 256), (256, 1024), (1024, W_UP_SH), (W_DOWN_SH, 1024))
N_BIG = len(BIG)


def _cols_join(t):
    return jnp.concatenate([t[k] for k in range(4)], axis=1)


def _cols_split(t):
    cols = t.shape[1] // 4
    return jnp.stack([t[:, k * cols:(k + 1) * cols] for k in range(4)])


def _me():
    return lax.axis_index("x"), lax.axis_index("y"), lax.axis_index("c")


HBM = pl.BlockSpec(memory_space=pltpu.HBM)
VMEM_SPEC = pl.BlockSpec(memory_space=pltpu.VMEM)


def _allgather8(xs, name):
    rows = xs.shape[0]

    def body(x_ref, out_ref, send_sems, recv_sems, local_sem):
        x, y, c = _me()
        me = 4 * x + 2 * y + c
        mine = pltpu.make_async_copy(x_ref, out_ref.at[me], local_sem)
        mine.start()
        flips = [(k >> 2 & 1, k >> 1 & 1, k & 1) for k in range(1, 8)]

        def peer(f):
            return (jnp.where(f[0] == 1, 1 - x, x), jnp.where(f[1] == 1, 1 - y, y), jnp.where(f[2] == 1, 1 - c, c))

        sends = []
        for k, f in enumerate(flips):
            cp = pltpu.make_async_remote_copy(src_ref=x_ref, dst_ref=out_ref.at[me], send_sem=send_sems.at[k],
                                              recv_sem=recv_sems.at[k], device_id=peer(f), device_id_type=MESH)
            cp.start()
            sends.append(cp)
        for k, f in enumerate(flips):
            px, py, pc = peer(f)
            pltpu.make_async_remote_copy(src_ref=x_ref, dst_ref=out_ref.at[4 * px + 2 * py + pc], send_sem=send_sems.at[k],
                                         recv_sem=recv_sems.at[k], device_id=peer(f), device_id_type=MESH).wait_recv()
        for cp in sends:
            cp.wait_send()
        mine.wait()

    return pl.pallas_call(
        body, name=name, out_shape=jax.ShapeDtypeStruct((8, rows, 128), F32),
        in_specs=[VMEM_SPEC], out_specs=VMEM_SPEC,
        scratch_shapes=[pltpu.SemaphoreType.DMA((7,)), pltpu.SemaphoreType.DMA((7,)), pltpu.SemaphoreType.DMA],
        compiler_params=pltpu.CompilerParams(vmem_limit_bytes=VMEM_LIMIT),
    )(xs)


def _half_rows(i, cc, unit):
    rows = SH_SHAPES[i][0] // 2
    return pl.ds(pl.multiple_of(cc * rows, unit), rows)


def _rc(src, dst, sems, to):
    return pltpu.make_async_remote_copy(src_ref=src, dst_ref=dst, send_sem=sems[0], recv_sem=sems[1], device_id=to, device_id_type=MESH)


def _other_chips(x, y):
    return [(1 - x, y), (x, 1 - y), (1 - x, 1 - y)]


def _u_gather_ici(w_sh, idxs):
    def copies(ins, outs, sem):
        x, y, c = _me()
        res = []
        for j, (px, py) in enumerate(_other_chips(x, y)):
            for n, i in enumerate(idxs):
                src = ins[n].at[0, _half_rows(i, c, 16)]
                res.append((_rc(src, outs[n].at[2 * x + y, _half_rows(i, c, 16)], sem(j * len(idxs) + n), (px, py, c)),
                            _rc(src, outs[n].at[2 * px + py, _half_rows(i, c, 16)], sem(j * len(idxs) + n), (px, py, c))))
        return res

    return dict(ins=[w_sh[i] for i in idxs], outs=[jax.ShapeDtypeStruct((4,) + SH_SHAPES[i], BF16) for i in idxs],
                nsem=3 * len(idxs), alias={}, copies=copies)


def _u_gather_d2d(got, idxs):
    def copies(ins, outs, sem):
        x, y, c = _me()
        res = []
        for j, (px, py) in enumerate(_other_chips(x, y)):
            for n, i in enumerate(idxs):
                src = ins[n].at[2 * px + py, _half_rows(i, c, 16)]
                res.append((_rc(src, outs[n].at[2 * px + py, _half_rows(i, c, 16)], sem(j * len(idxs) + n), (x, y, 1 - c)),
                            _rc(src, outs[n].at[2 * px + py, _half_rows(i, 1 - c, 16)], sem(j * len(idxs) + n), (x, y, 1 - c))))
        return res

    return dict(ins=list(got), outs=[jax.ShapeDtypeStruct(g.shape, g.dtype) for g in got], nsem=3 * len(idxs),
                alias={n: n for n in range(len(idxs))}, copies=copies)


def _u_pair_send(gs, idxs):
    def copies(ins, outs, sem):
        x, y, c = _me()
        res = []
        for n, i in enumerate(idxs):
            for sh in range(4):
                cp = _rc(ins[n].at[sh, _half_rows(i, 1 - c, 8)], outs[n].at[sh], sem(4 * n + sh), (x, y, 1 - c))
                res.append((cp, cp))
        return res

    return dict(ins=list(gs), outs=[jax.ShapeDtypeStruct((4, SH_SHAPES[i][0] // 2, SH_SHAPES[i][1]), F32) for i in idxs],
                nsem=4 * len(idxs), alias={}, copies=copies)


def _u_chip_exchange(ts):
    def copies(ins, outs, sem):
        x, y, c = _me()
        res = []
        for j, (px, py) in enumerate(_other_chips(x, y)):
            for n in range(len(ts)):
                cp = _rc(ins[n].at[2 * px + py], outs[n].at[j], sem(j * len(ts) + n), (px, py, c))
                res.append((cp, cp))
        return res

    return dict(ins=list(ts), outs=[jax.ShapeDtypeStruct((3,) + t.shape[1:], t.dtype) for t in ts], nsem=3 * len(ts),
                alias={}, copies=copies)


def _u_pair_join(hs):
    def copies(ins, outs, sem):
        x, y, c = _me()
        res = []
        for n in range(len(hs)):
            cp = _rc(ins[n], outs[n], sem(n), (x, y, 1 - c))
            res.append((cp, cp))
        return res

    return dict(ins=list(hs), outs=[jax.ShapeDtypeStruct(h.shape, h.dtype) for h in hs], nsem=len(hs), alias={}, copies=copies)


def _comm_phase(units, ci, co, send_sems, recv_sems, start):
    ii = oo = off = 0
    for u in units:
        ni, no = len(u["ins"]), len(u["outs"])
        for st, arrival in u["copies"](ci[ii:ii + ni], co[oo:oo + no], lambda k, off=off: (send_sems.at[off + k], recv_sems.at[off + k])):
            if start:
                st.start()
            else:
                st.wait_send()
                arrival.wait_recv()
        ii, oo, off = ii + ni, oo + no, off + u["nsem"]


def _carry(units, n_in, n_out):
    ins = [a for u in units for a in u["ins"]]
    outs = [o for u in units for o in u["outs"]]
    alias, ii, oo = {}, 0, 0
    for u in units:
        for a, b in u["alias"].items():
            alias[n_in + ii + a] = n_out + oo + b
        ii, oo = ii + len(u["ins"]), oo + len(u["outs"])
    nsem = sum(u["nsem"] for u in units)
    scratch = [pltpu.SemaphoreType.DMA((nsem,)), pltpu.SemaphoreType.DMA((nsem,))] if units else []
    return ins, outs, alias, scratch


def _split_units(units, res):
    out, oo = [], 0
    for u in units:
        out.append(list(res[oo:oo + len(u["outs"])]))
        oo += len(u["outs"])
    return out


def _comm_call(name, units):
    ins, outs, alias, scratch = _carry(units, 0, 0)

    def body(*refs):
        ci, co = refs[:len(ins)], refs[len(ins):len(ins) + len(outs)]
        _comm_phase(units, ci, co, refs[-2], refs[-1], True)
        _comm_phase(units, ci, co, refs[-2], refs[-1], False)

    res = pl.pallas_call(body, name=name, out_shape=outs, in_specs=[HBM] * len(ins), out_specs=[HBM] * len(outs),
                         scratch_shapes=scratch, input_output_aliases=alias)(*ins)
    return _split_units(units, res)


SEM = pl.BlockSpec(memory_space=pltpu.SEMAPHORE)
EFFECT = pltpu.SideEffectType.DATAFLOW_SIDE_EFFECTING


def _unit_start(unit, name, after=()):
    bufs = list(unit["ins"]) + [lax.empty(o.shape, o.dtype) for o in unit["outs"]]
    n_i, n_b, ns = len(unit["ins"]), len(bufs), unit["nsem"]

    def body(*refs):
        send_sems, recv_sems = refs[n_b + len(after)], refs[n_b + len(after) + 1]
        for st, _ in unit["copies"](refs[:n_i], refs[n_i:n_b], lambda k: (send_sems.at[k], recv_sems.at[k])):
            st.start()
        refs[-1][...] = jnp.zeros_like(refs[-1])

    res = pl.pallas_call(
        body, name=name,
        out_shape=[pltpu.SemaphoreType.DMA((ns,)), pltpu.SemaphoreType.DMA((ns,))] + [pltpu.HBM(b.shape, b.dtype) for b in bufs]
        + [jax.ShapeDtypeStruct((8, 128), F32)],
        in_specs=[HBM] * n_b + [pl.BlockSpec(memory_space=pl.ANY)] * len(after), out_specs=[SEM, SEM] + [HBM] * n_b + [VMEM_SPEC],
        input_output_aliases={i: 2 + i for i in range(n_b)},
        compiler_params=pltpu.CompilerParams(has_side_effects=EFFECT),
    )(*[pltpu.with_memory_space_constraint(b, pltpu.HBM) for b in bufs], *after)
    return res[0], res[1], list(res[2:2 + n_b]), res[-1]


def _unit_wait(unit, send_sems, recv_sems, bufs, after, name):
    n_i, n_b = len(unit["ins"]), len(bufs)

    def body(*refs):
        ss, rs = refs[n_b], refs[n_b + 1]
        for st, arrival in unit["copies"](refs[:n_i], refs[n_i:n_b], lambda k: (ss.at[k], rs.at[k])):
            st.wait_send()
            arrival.wait_recv()

    res = pl.pallas_call(
        body, name=name, out_shape=[pltpu.HBM(b.shape, b.dtype) for b in bufs],
        in_specs=[HBM] * n_b + [SEM, SEM] + [pl.BlockSpec(memory_space=pl.ANY)] * len(after), out_specs=[HBM] * n_b,
        input_output_aliases={i: i for i in range(n_b)}, compiler_params=pltpu.CompilerParams(has_side_effects=EFFECT),
    )(*bufs, send_sems, recv_sems, *after)
    return list(res[:n_i]), list(res[n_i:])


def _pair_add(g, land, core, name):
    _, rows, cols = g.shape
    half = rows // 2
    rb = _tile(half, 512, 16)
    nb = half // rb

    def body(c_ref, g_ref, l_ref, o_ref):
        o_ref[...] = (g_ref[...] + l_ref[...]).astype(BF16)

    return pl.pallas_call(
        body, name=name,
        grid_spec=pltpu.PrefetchScalarGridSpec(
            num_scalar_prefetch=1, grid=(4, nb),
            in_specs=[pl.BlockSpec((1, rb, cols), lambda s, i, c_ref: (s, c_ref[0] * nb + i, 0)),
                      pl.BlockSpec((1, rb, cols), lambda s, i, c_ref: (s, i, 0))],
            out_specs=pl.BlockSpec((1, rb, cols), lambda s, i, c_ref: (s, i, 0))),
        out_shape=jax.ShapeDtypeStruct((4, half, cols), BF16),
        compiler_params=_params(("parallel", "parallel")),
    )(core, g, land)


def _chip_sum(t, r, chip, name):
    _, half, cols = t.shape
    rb = _tile(half, 512, 16)

    def body(s_ref, t_ref, r_ref, o_ref):
        o_ref[...] = ((t_ref[0].astype(F32) + r_ref[0].astype(F32)) + r_ref[1].astype(F32)) + r_ref[2].astype(F32)

    return pl.pallas_call(
        body, name=name,
        grid_spec=pltpu.PrefetchScalarGridSpec(
            num_scalar_prefetch=1, grid=(half // rb,),
            in_specs=[pl.BlockSpec((1, rb, cols), lambda i, s_ref: (s_ref[0], i, 0)),
                      pl.BlockSpec((3, rb, cols), lambda i, s_ref: (0, i, 0))],
            out_specs=pl.BlockSpec((rb, cols), lambda i, s_ref: (i, 0))),
        out_shape=jax.ShapeDtypeStruct((half, cols), F32),
        compiler_params=_params(("parallel",)),
    )(chip, t, r)


def _adam_math(wv, gv, mv, vv):
    mn = ADAM_B1 * mv + (1.0 - ADAM_B1) * gv
    vn = ADAM_B2 * vv + (1.0 - ADAM_B2) * (gv * gv)
    m_hat = mn / (1.0 - ADAM_B1 ** ADAM_STEP)
    v_hat = vn / (1.0 - ADAM_B2 ** ADAM_STEP)
    return -ADAM_LR * (m_hat / (jnp.sqrt(v_hat) + ADAM_EPS) + ADAM_WD * wv), mn, vn


def _adamw_halves(wt, mt, vt, mine, theirs, core, name):
    _, rows, cols = wt.shape
    half = rows // 2
    rb = _tile(half, 256, 8)
    nb = half // rb

    def body(c_ref, w_ref, m_ref, v_ref, a_ref, b_ref, g_ref, d_ref, mo_ref, vo_ref):
        gv = jnp.where(pl.program_id(0) == c_ref[0], a_ref[...], b_ref[...])
        dl, mn, vn = _adam_math(w_ref[...], gv, m_ref[...], v_ref[...])
        g_ref[...] = gv
        d_ref[...] = dl
        mo_ref[...] = mn
        vo_ref[...] = vn

    full = pl.BlockSpec((None, rb, cols), lambda hf, i, c_ref: (0, hf * nb + i, 0))
    part = pl.BlockSpec((rb, cols), lambda hf, i, c_ref: (i, 0))
    return pl.pallas_call(
        body, name=name,
        grid_spec=pltpu.PrefetchScalarGridSpec(num_scalar_prefetch=1, grid=(2, nb), in_specs=[full, full, full, part, part],
                                               out_specs=[full] * 4),
        out_shape=[jax.ShapeDtypeStruct((1, rows, cols), F32)] * 4,
        compiler_params=_params(("parallel", "parallel")),
    )(core, wt, mt, vt, mine, theirs)


SG_REP = 144
SG_LOSS = 136
SG_W2, SG_CW = SG_REP, SG_REP + 4 * 16
SG_ROWS = SG_CW + 4 * 40
SP_ROWS = SG_REP + 16 + 40


def _mod_shard(c_all, ada_w_sh):
    def body(c_ref, w_ref, o_ref):
        cv = c_ref[...]
        o_ref[...] = _dg((cv * _sigmoid(cv)).astype(BF16), w_ref[...].astype(BF16), 1, 0)

    return pl.pallas_call(body, name="mod_shard", out_shape=jax.ShapeDtypeStruct((8, 1536), F32),
                          in_specs=[VMEM_SPEC, VMEM_SPEC], out_specs=VMEM_SPEC,
                          compiler_params=pltpu.CompilerParams(vmem_limit_bytes=VMEM_LIMIT))(c_all, ada_w_sh)


def _mod_select(mod_all, ada_b4):
    def body(m_ref, b_ref, o_ref):
        x, y, c = _me()
        me = 4 * x + 2 * y + c
        for sh in range(4):
            o_ref[sh] = m_ref[2 * sh, me] + b_ref[sh]

    return pl.pallas_call(body, name="mod_select", out_shape=jax.ShapeDtypeStruct((4, 12, 128), F32),
                          in_specs=[VMEM_SPEC, VMEM_SPEC], out_specs=VMEM_SPEC)(mod_all, ada_b4)


def _small_reduce(sg_all):
    def body(g_ref, o_ref):
        x, y, c = _me()
        s_me = 2 * x + y
        w2_rows = pl.ds(pl.multiple_of(SG_W2 + 16 * s_me, 8), 16)
        cw_rows = pl.ds(pl.multiple_of(SG_CW + 40 * s_me, 8), 40)
        a = g_ref[0, 0:SG_REP, :]
        b = g_ref[0, w2_rows, :]
        d = g_ref[0, cw_rows, :]
        for dev in range(1, 8):
            a = a + g_ref[dev, 0:SG_REP, :]
            b = b + g_ref[dev, w2_rows, :]
            d = d + g_ref[dev, cw_rows, :]
        o_ref[0:SG_REP, :] = a
        o_ref[SG_REP:SG_REP + 16, :] = b
        o_ref[SG_REP + 16:SP_ROWS, :] = d

    return pl.pallas_call(body, name="small_grad_reduce", out_shape=jax.ShapeDtypeStruct((SP_ROWS, 128), F32),
                          in_specs=[VMEM_SPEC], out_specs=VMEM_SPEC)(sg_all)


def _ada_grad(dmod_all, c_bc):
    def body(g_ref, c_ref, o_ref):
        x, y, c = _me()
        s_me = 2 * x + y
        for k in range(12):
            acc = jnp.zeros((D, 128), F32)
            for b in range(8):
                cv = c_ref[b]
                acc = acc + (cv * _sigmoid(cv)) * g_ref[s_me, k, b:b + 1, :]
            o_ref[:, k * 128:(k + 1) * 128] = acc

    return pl.pallas_call(body, name="ada_w_grad", out_shape=jax.ShapeDtypeStruct((D, 1536), F32),
                          in_specs=[VMEM_SPEC, VMEM_SPEC], out_specs=VMEM_SPEC,
                          compiler_params=pltpu.CompilerParams(vmem_limit_bytes=VMEM_LIMIT))(dmod_all, c_bc)


def _adamw(wt, g, m, v, name):
    rows, cols = wt.shape
    rb = _tile(rows, 256, 8)

    def fn(c, i, wv, gv, mv, vv):
        return _adam_math(wv, gv, mv, vv)

    return _rowcall(fn, [_rows(t, rb) for t in (wt, g, m, v)], [_orow(rows, cols, F32, rb)] * 3,
                    n_rows=rows, rb=rb, name=name)


def _pad_rows(t, rows):
    flat = t.reshape(-1)
    return jnp.pad(flat, (0, rows * 128 - flat.shape[0])).reshape(rows, 128)


SP_LAYOUT = (("ada_b", 48), ("norm1_w", 8), ("gla_gate_b", 8), ("gla_norm_w", 8), ("norm2_w", 8), ("conv_b", 48),
             ("final_norm_w", 8), (None, 8), ("gla_gate_w2", 16), ("conv_w", 40))


def _pack_small(d):
    return jnp.concatenate([jnp.zeros((rows, 128), F32) if n is None else _pad_rows(d[n].astype(F32), rows)
                            for n, rows in SP_LAYOUT], axis=0)


def _unpack_small(pk, shapes):
    out, off = {}, 0
    for n, rows in SP_LAYOUT:
        if n is not None:
            shp = shapes[n]
            out[n] = pk[off:off + rows].reshape(-1)[:math.prod(shp)].reshape(shp)
        off += rows
    return out


def kernel(x, c, positions, ada_w, ada_b, norm1_w, w_in, gla_gate_w2, gla_gate_b, gla_norm_w, w_gla_branch, w_attn_branch, w_out, norm2_w, w_up, conv_w, conv_b, w_down, final_norm_w, loss_target, m_ada_w, m_ada_b, m_norm1_w, m_w_in, m_gla_gate_w2, m_gla_gate_b, m_gla_norm_w, m_w_gla_branch, m_w_attn_branch, m_w_out, m_norm2_w, m_w_up, m_conv_w, m_conv_b, m_w_down, m_final_norm_w, v_ada_w, v_ada_b, v_norm1_w, v_w_in, v_gla_gate_w2, v_gla_gate_b, v_gla_norm_w, v_w_gla_branch, v_w_attn_branch, v_w_out, v_norm2_w, v_w_up, v_conv_w, v_conv_b, v_w_down, v_final_norm_w):
    s = x.shape[1]
    names = ("ada_w", "ada_b", "norm1_w", "w_in", "gla_gate_w2", "gla_gate_b", "gla_norm_w", "w_gla_branch", "w_attn_branch",
             "w_out", "norm2_w", "w_up", "conv_w", "conv_b", "w_down", "final_norm_w")
    wts = dict(zip(names, (ada_w, ada_b, norm1_w, w_in, gla_gate_w2, gla_gate_b, gla_norm_w, w_gla_branch, w_attn_branch,
                           w_out, norm2_w, w_up, conv_w, conv_b, w_down, final_norm_w)))
    ms = dict(zip(names, (m_ada_w, m_ada_b, m_norm1_w, m_w_in, m_gla_gate_w2, m_gla_gate_b, m_gla_norm_w, m_w_gla_branch,
                          m_w_attn_branch, m_w_out, m_norm2_w, m_w_up, m_conv_w, m_conv_b, m_w_down, m_final_norm_w)))
    vs = dict(zip(names, (v_ada_w, v_ada_b, v_norm1_w, v_w_in, v_gla_gate_w2, v_gla_gate_b, v_gla_norm_w, v_w_gla_branch,
                          v_w_attn_branch, v_w_out, v_norm2_w, v_w_up, v_conv_w, v_conv_b, v_w_down, v_final_norm_w)))

    pk0 = jnp.concatenate([_pad_rows(c, 8), _pad_rows(gla_gate_w2, 16), _pad_rows(conv_w, 40)], axis=0)
    sm_all = _allgather8(pk0, "gather_small")
    c_all = sm_all[:, 0:8, :].reshape(8, D)
    w2_full = sm_all[0::2, 8:24, :].transpose(1, 0, 2).reshape(GLA_LR, 512)
    cw_full = sm_all[0::2, 24:64, :].reshape(4, 40 * 128)[:, :3 * W_UP_SH].reshape(4, 3, W_UP_SH).transpose(1, 0, 2).reshape(3, 2 * D_FF)

    mod_sh = _mod_shard(c_all, ada_w[0])
    mod_all = _allgather8(mod_sh.reshape(96, 128), "gather_mod")

    w_sh = [wts[n].astype(BF16) for n in BIG]
    u_g0 = _u_gather_ici(w_sh, (0,))
    g0 = (u_g0,) + _unit_start(u_g0, "gather_w_in_start", after=[mod_all])
    mod = _mod_select(mod_all.reshape(8, 8, 12, 128) + g0[4][0, 0], ada_b.reshape(4, 12, 128)).reshape(6, D)

    core = lax.axis_index("c").astype(jnp.int32).reshape(1)
    chip = (2 * lax.axis_index("x") + lax.axis_index("y")).astype(jnp.int32)
    sm = dict(n1w=norm1_w, n2w=norm2_w, fnw=final_norm_w.reshape(1, D), gnw=gla_norm_w, gb=gla_gate_b,
              w2=jnp.pad(w2_full, ((0, 128 - GLA_LR), (0, 0))), cw=_ff_to_kernel(cw_full), cb=_ff_to_kernel(conv_b))
    loss, grad_x, halves, others, small, ts0 = _local_step(x[0], mod, positions.reshape(s, 1), loss_target[0], sm, w_sh,
                                                               g0, chip, core)

    dcw = _ff_from_kernel(small["cw"]).reshape(3, 4, W_UP_SH).transpose(1, 0, 2)
    dw2 = small["w2"][:GLA_LR].reshape(GLA_LR, 4, 128).transpose(1, 0, 2)
    sg = jnp.concatenate(
        [_pad_rows(small["dmod"], 48), _pad_rows(small["n1w"], 8), _pad_rows(small["gb"], 8), _pad_rows(small["gnw"], 8),
         _pad_rows(small["n2w"], 8), _pad_rows(_ff_from_kernel(small["cb"]), 48), _pad_rows(small["fnw"], 8), _pad_rows(loss, 8)]
        + [_pad_rows(dw2[k], 16) for k in range(4)] + [_pad_rows(dcw[k], 40) for k in range(4)], axis=0)
    sg_all = _allgather8(sg, "gather_small_grads")
    u_ex = _u_chip_exchange([ts0])
    pending = (u_ex,) + _unit_start(u_ex, "grad_exchange_w_in_start", after=[sg_all])
    sg_all = sg_all + pending[4][0, 0]
    g_small_pk = _small_reduce(sg_all)
    dmod_all = sg_all[:, 0:48, :].reshape(8, 4, 12, 128).transpose(1, 2, 0, 3)
    g_ada_w = _ada_grad(dmod_all, jnp.broadcast_to(c_all[:, :, None], (8, D, 128)))

    shapes = {n: wts[n].shape for n in names}
    g_small = _unpack_small(g_small_pk, shapes)
    grads = {"ada_w": g_ada_w.reshape(1, D, 1536), **g_small}
    deltas, new_m, new_v = {}, {}, {}
    for n, mine, theirs in zip(BIG[1:], halves, others):
        grads[n], deltas[n], new_m[n], new_v[n] = _adamw_halves(wts[n], ms[n], vs[n], mine, theirs, core, "adamw_" + n)
    shp = ada_w.shape
    d_, m_, v_ = _adamw(ada_w[0], g_ada_w, m_ada_w[0], v_ada_w[0], "adamw_ada_w")
    deltas["ada_w"], new_m["ada_w"], new_v["ada_w"] = d_.reshape(shp), m_.reshape(shp), v_.reshape(shp)
    d_, m_, v_ = _adamw(_pack_small(wts), g_small_pk, _pack_small(ms), _pack_small(vs), "adamw_small")
    for dst, pk in ((deltas, d_), (new_m, m_), (new_v, v_)):
        dst.update(_unpack_small(pk, shapes))

    [t0], [r0] = _unit_wait(*pending[:4], after=[d_, deltas["ada_w"], deltas["w_up"], deltas["w_down"]], name="grad_exchange_w_in_wait")
    half0 = _chip_sum(t0, r0, chip.reshape(1), "grad_chip_sum_w_in")
    [[oth0]] = _comm_call("grad_join_w_in", [_u_pair_join([half0])])
    grads["w_in"], deltas["w_in"], new_m["w_in"], new_v["w_in"] = _adamw_halves(w_in, m_w_in, v_w_in, half0, oth0, core, "adamw_w_in")

    return (g_small_pk[SG_LOSS, 0], grad_x.reshape(1, s, D), *[grads[n] for n in names], *[deltas[n] for n in names],
            *[new_m[n] for n in names], *[new_v[n] for n in names])
```

```python
import math

import jax
import jax.numpy as jnp
from jax import lax
from jax.experimental import pallas as pl
from jax.experimental.pallas import tpu as pltpu

F32, BF16 = jnp.float32, jnp.bfloat16
MESH = pl.DeviceIdType.MESH

D = 1024
EPS = 1e-6
GLA_H, GLA_DK, GLA_DV, GLA_LR = 4, 128, 256, 16
GLA_TAU = 16.0
GLA_CHUNK = 64
GLA_BLOCK = 512
ATT_GROUPS = ((128, 1), (512, 4), (2048, 16))
ATT_BLK = 128
ATT_HD = 64
ATT_W = 768
D_FF = 2816
ROPE_THETA = 10000.0
P_W = 7680
P_GV, P_GR, P_MA, P_MB, P_GQ, P_GK, P_AQ, P_AK, P_AV, P_LR = 0, 1024, 2048, 3072, 4096, 4608, 5120, 5888, 6656, 7424
W_IN = 7440
W_IN_SH, W_UP_SH, W_DOWN_SH = 1860, 1408, 704
VMEM_LIMIT = 56 * 1024 * 1024
ADAM_LR, ADAM_B1, ADAM_B2, ADAM_EPS, ADAM_WD, ADAM_STEP = 0.001, 0.9, 0.999, 1e-08, 0.01, 10
NEG = -1e30


def _tile(n, target, unit=128):
    best = None
    for t in range(unit, min(n, target) + 1, unit):
        if n % t == 0:
            best = t
    return best or n


def _params(sem):
    return pltpu.CompilerParams(dimension_semantics=sem, vmem_limit_bytes=VMEM_LIMIT)


def _dg(a, b, ca, cb):
    return lax.dot_general(a, b, (((ca,), (cb,)), ((), ())), preferred_element_type=F32)


def _sigmoid(v):
    return 1.0 / (1.0 + jnp.exp(-v))


def _ff_block(j):
    return (j % 2) * 2 + j // 2


def _mm(a, b, name, *, ta=False, tb=False, out_dtype=BF16, tm=1024, tn=1536, tk=1024, n_outer=True, comm=(),
        b_shards=False, o_shards=False):
    m = a.shape[1] if ta else a.shape[0]
    k = a.shape[0] if ta else a.shape[1]
    if b_shards:
        n = b.shape[1] if tb else 4 * W_UP_SH
        tn, tk = (tn, W_UP_SH) if tb else (W_UP_SH, tk)
    else:
        n = b.shape[0] if tb else b.shape[1]
    if o_shards:
        tn = W_UP_SH
    tm, tn, tk = _tile(m, tm), _tile(n, tn), _tile(k, tk)
    nm, nn, nk = m // tm, n // tn, k // tk
    in_out = out_dtype == F32
    c_ins, c_outs, c_alias, c_scratch = _carry(comm, 2, 1)

    def body(a_ref, b_ref, *rest):
        ci, o_ref, co = rest[:len(c_ins)], rest[len(c_ins)], rest[len(c_ins) + 1:len(c_ins) + 1 + len(c_outs)]
        scr = rest[len(c_ins) + 1 + len(c_outs):]
        kk = pl.program_id(2)
        if comm:
            step = (pl.program_id(0) * (nm if n_outer else nn) + pl.program_id(1)) * nk + kk

            @pl.when(step == 0)
            def _():
                _comm_phase(comm, ci, co, scr[-2], scr[-1], True)

        _mm_step(a_ref, b_ref, o_ref, scr, kk)
        if comm:
            @pl.when(step == nm * nn * nk - 1)
            def _():
                _comm_phase(comm, ci, co, scr[-2], scr[-1], False)

    def _mm_step(a_ref, b_ref, o_ref, scr, kk):
        p = _dg(a_ref[...].astype(BF16), b_ref[...].astype(BF16), 0 if ta else 1, 1 if tb else 0)
        if nk == 1:
            o_ref[...] = p.astype(o_ref.dtype)
        else:
            acc = o_ref if in_out else scr[0]

            @pl.when(kk == 0)
            def _():
                acc[...] = p

            @pl.when(kk > 0)
            def _():
                acc[...] += p

            if not in_out:
                @pl.when(kk == nk - 1)
                def _():
                    o_ref[...] = acc[...].astype(o_ref.dtype)

    if n_outer:
        ij = lambda g0, g1: (g1, g0)
        grid = (nn, nm, nk)
    else:
        ij = lambda g0, g1: (g0, g1)
        grid = (nm, nn, nk)
    a_map = (lambda g0, g1, kk: (kk, ij(g0, g1)[0])) if ta else (lambda g0, g1, kk: (ij(g0, g1)[0], kk))
    if b_shards and tb:
        b_spec = pl.BlockSpec((None, tn, tk), lambda g0, g1, kk: (_ff_block(kk), ij(g0, g1)[1], 0))
    elif b_shards:
        b_spec = pl.BlockSpec((None, tk, tn), lambda g0, g1, kk: (_ff_block(ij(g0, g1)[1]), kk, 0))
    elif tb:
        b_spec = pl.BlockSpec((tn, tk), lambda g0, g1, kk: (ij(g0, g1)[1], kk))
    else:
        b_spec = pl.BlockSpec((tk, tn), lambda g0, g1, kk: (kk, ij(g0, g1)[1]))
    if o_shards:
        o_spec = pl.BlockSpec((None, tm, tn), lambda g0, g1, kk: (_ff_block(ij(g0, g1)[1]), ij(g0, g1)[0], 0))
        o_shape = jax.ShapeDtypeStruct((4, m, W_UP_SH), out_dtype)
    else:
        o_spec = pl.BlockSpec((tm, tn), lambda g0, g1, kk: ij(g0, g1))
        o_shape = jax.ShapeDtypeStruct((m, n), out_dtype)
    res = pl.pallas_call(
        body, name=name, grid=grid,
        in_specs=[pl.BlockSpec((tk, tm) if ta else (tm, tk), a_map), b_spec] + [HBM] * len(c_ins),
        out_specs=[o_spec] + [HBM] * len(c_outs),
        out_shape=[o_shape] + c_outs,
        scratch_shapes=([] if (in_out or nk == 1) else [pltpu.VMEM((tm, tn), F32)]) + c_scratch,
        input_output_aliases=c_alias,
        compiler_params=_params(("arbitrary",) * 3 if comm else ("parallel", "parallel", "arbitrary")),
    )(a, b, *c_ins)
    return (res[0], _split_units(comm, res[1:])) if comm else res[0]


def _rows(arr, rb, w=None, j=0):
    w = arr.shape[1] if w is None else w
    if callable(j):
        return arr, pl.BlockSpec((rb, w), lambda c, i: (i, j(c)))
    return arr, pl.BlockSpec((rb, w), lambda c, i: (i, j))


def _full(arr, w=None, j=0):
    w = arr.shape[1] if w is None else w
    if callable(j):
        return arr, pl.BlockSpec((arr.shape[0], w), lambda c, i: (0, j(c)))
    return arr, pl.BlockSpec((arr.shape[0], w), lambda c, i: (0, j))


def _halo(arr, rb, hb, w, j, before):
    per = rb // hb
    last = arr.shape[0] // hb - 1
    if before:
        rmap = lambda i: jnp.maximum(i * per - 1, 0)
    else:
        rmap = lambda i: jnp.minimum((i + 1) * per, last)
    return arr, pl.BlockSpec((hb, w), lambda c, i: (rmap(i), j(c) if callable(j) else j))


def _rowcall(fn, ins, outs, *, n_rows, rb, name, ncol=1, into=None, after=()):
    n_in = len(ins)
    nr = n_rows // rb
    unread = ([] if into is None else [into[0]]) + list(after)
    n_skip = len(unread)

    def body(*refs):
        c, i = pl.program_id(0), pl.program_id(1)
        res = fn(c, i, *[r[...] for r in refs[:n_in]])
        for val, spec, o_ref in zip(res, outs, refs[n_in + n_skip:]):
            if spec[2] == "row":
                o_ref[...] = val.astype(o_ref.dtype)
            else:
                @pl.when(i == 0)
                def _(o_ref=o_ref, val=val):
                    o_ref[...] = val.astype(o_ref.dtype)

                @pl.when(i > 0)
                def _(o_ref=o_ref, val=val):
                    o_ref[...] += val.astype(o_ref.dtype)

    out_specs = []
    for shape, dt, kind, block, col in outs:
        if kind == "row":
            out_specs.append(pl.BlockSpec(block, lambda c, i, col=col: (i, col(c))))
        else:
            out_specs.append(pl.BlockSpec(block, lambda c, i, col=col: (0, col(c))))
    return pl.pallas_call(
        body, name=name, grid=(ncol, nr),
        in_specs=[s for _, s in ins] + [pl.BlockSpec(memory_space=pl.ANY)] * n_skip, out_specs=out_specs,
        out_shape=[jax.ShapeDtypeStruct(o[0], o[1]) for o in outs],
        input_output_aliases={} if into is None else {n_in: into[1]},
        compiler_params=_params(("parallel", "arbitrary")),
    )(*[a for a, _ in ins], *unread)


def _orow(n_rows, w, dt, rb, bw=None, col=lambda c: 0):
    return ((n_rows, w), dt, "row", (rb, bw or w), col)


def _oacc(r, w, bw=None, col=lambda c: 0):
    return ((r, w), F32, "acc", (r, bw or w), col)


def _csum(v):
    return jnp.sum(v, axis=0, keepdims=True)


def _rms(v):
    return lax.rsqrt(jnp.mean(v * v, axis=-1, keepdims=True) + EPS)


def _norm_bwd(xv, dh, w, scale):
    r = _rms(xv)
    xh = xv * r
    dxh = dh * (w * (1.0 + scale))
    dx = r * (dxh - xh * jnp.mean(dxh * xh, axis=-1, keepdims=True))
    t = dh * xh
    return dx, _csum(dh), _csum(t * w), _csum(t * (1.0 + scale))


def _rope_tables(pos_col, invf, s):
    def fn(c, i, pos, f):
        ang = pos.astype(F32) * f
        lane = lax.broadcasted_iota(jnp.int32, ang.shape, 1)
        sign = jnp.where((lane % ATT_HD) < ATT_HD // 2, -1.0, 1.0)
        return jnp.cos(ang), jnp.sin(ang) * sign

    rb = 512
    return _rowcall(fn, [_rows(pos_col, rb), _full(invf)], [_orow(s, 128, F32, rb), _orow(s, 128, F32, rb)],
                    n_rows=s, rb=rb, name="rope_tables")


def _swap_halves(t):
    n = t.shape[1]
    lane = lax.broadcasted_iota(jnp.int32, t.shape, 1)
    return jnp.where((lane % ATT_HD) < ATT_HD // 2, pltpu.roll(t, n - 32, 1), pltpu.roll(t, 32, 1))


def _rope_apply(t, cos, sin_signed, inverse):
    cw = jnp.concatenate([cos] * (t.shape[1] // 128), axis=1)
    sw = jnp.concatenate([sin_signed] * (t.shape[1] // 128), axis=1)
    if inverse:
        sw = -sw
    return t * cw + _swap_halves(t) * sw


DIL_ROWS = 512


def _to_dilated(scr, val, out_ref, r):
    if r == 1:
        out_ref[...] = val.astype(out_ref.dtype)
        return
    n = val.shape[0] // r
    for hh in range(2):
        scr[hh] = val[:, hh * 128:(hh + 1) * 128]
        for pr in range(r):
            out_ref[:, pr * 256 + hh * 128:pr * 256 + (hh + 1) * 128] = scr[hh, pl.ds(pr, n, stride=r), :].astype(out_ref.dtype)


def _from_dilated(scr, in_ref, r):
    if r == 1:
        return in_ref[...].astype(F32)
    n = in_ref.shape[0]
    for hh in range(2):
        for pr in range(r):
            scr[hh, pl.ds(pr, n, stride=r), :] = in_ref[:, pr * 256 + hh * 128:pr * 256 + (hh + 1) * 128].astype(F32)
    return jnp.concatenate([scr[0], scr[1]], axis=1)


def _dil_spec(r):
    return pl.BlockSpec((DIL_ROWS // r, r * 256), lambda i: (i, 0))


def _dil_shape(s, r, dt):
    return jax.ShapeDtypeStruct((s // r, r * 256), dt)


_DIL_SCRATCH = [pltpu.VMEM((2, DIL_ROWS, 128), F32)]
_RS = tuple(r for _, r in ATT_GROUPS)


def _rope_fwd(p, cos_t, sin_t, s):
    def body(*refs):
        ins, cs, sn, outs, scr = refs[:9], refs[9][...], refs[10][...], refs[11:20], refs[20]
        for t in range(3):
            for g, r in enumerate(_RS):
                val = ins[3 * t + g][...].astype(F32)
                _to_dilated(scr, _rope_apply(val, cs, sn, False) if t < 2 else val, outs[3 * t + g], r)

    res = pl.pallas_call(
        body, name="rope", grid=(s // DIL_ROWS,),
        in_specs=[pl.BlockSpec((DIL_ROWS, 256), lambda i, c=base // 256 + g: (i, c)) for base in (P_AQ, P_AK, P_AV) for g in range(3)]
        + [pl.BlockSpec((DIL_ROWS, 128), lambda i: (i, 0))] * 2,
        out_specs=[_dil_spec(r) for _ in range(3) for r in _RS],
        out_shape=[_dil_shape(s, r, BF16) for _ in range(3) for r in _RS],
        scratch_shapes=_DIL_SCRATCH, compiler_params=_params(("parallel",)),
    )(*([p] * 9), cos_t, sin_t)
    return res[0:3], res[3:6], res[6:9]


def _attn_combine(att, s):
    def body(o0, o1, o2, l0, l1, l2, o_ref, lse_ref, od1, od2, ld1, ld2, scr):
        ov = [_from_dilated(scr, ref, r) for ref, r in zip((o0, o1, o2), _RS)]
        lv = [_from_dilated(scr, ref, r) for ref, r in zip((l0, l1, l2), _RS)]
        mx = jnp.maximum(jnp.maximum(lv[0], lv[1]), lv[2])
        ev = [jnp.exp(l - mx) for l in lv]
        z = ev[0] + ev[1] + ev[2]
        o = ((ev[0] * ov[0] + ev[1] * ov[1] + ev[2] * ov[2]) / z).astype(BF16)
        lse = mx + jnp.log(z)
        o_ref[...] = o
        lse_ref[...] = lse
        for ref, r in zip((od1, od2), _RS[1:]):
            _to_dilated(scr, o.astype(F32), ref, r)
        for ref, r in zip((ld1, ld2), _RS[1:]):
            _to_dilated(scr, lse, ref, r)

    return pl.pallas_call(
        body, name="attn_combine", grid=(s // DIL_ROWS,),
        in_specs=[_dil_spec(r) for r in _RS] * 2,
        out_specs=[_dil_spec(1)] * 2 + [_dil_spec(r) for r in _RS[1:]] * 2,
        out_shape=[_dil_shape(s, 1, BF16), _dil_shape(s, 1, F32)] + [_dil_shape(s, r, BF16) for r in _RS[1:]]
        + [_dil_shape(s, r, F32) for r in _RS[1:]],
        scratch_shapes=_DIL_SCRATCH, compiler_params=_params(("parallel",)),
    )(*[a[0] for a in att], *[a[1] for a in att])


def _dilate(t, s):
    def body(t_ref, o1, o2, scr):
        val = t_ref[...].astype(F32)
        for ref, r in zip((o1, o2), _RS[1:]):
            _to_dilated(scr, val, ref, r)

    return pl.pallas_call(
        body, name="attn_dilate", grid=(s // DIL_ROWS,), in_specs=[_dil_spec(1)], out_specs=[_dil_spec(r) for r in _RS[1:]],
        out_shape=[_dil_shape(s, r, t.dtype) for r in _RS[1:]], scratch_shapes=_DIL_SCRATCH, compiler_params=_params(("parallel",)),
    )(t)


def _rope_bwd(datt, d_glr, dp, cos_t, sin_t, s):
    tail = P_W - P_AQ

    def body(*refs):
        ins, cs, sn, glr, o_ref, scr = refs[:9], refs[9][...], refs[10][...], refs[11], refs[13], refs[14]
        for t in range(3):
            for g, r in enumerate(_RS):
                val = _from_dilated(scr, ins[3 * t + g], r)
                o_ref[:, t * ATT_W + g * 256:t * ATT_W + (g + 1) * 256] = (_rope_apply(val, cs, sn, True) if t < 2 else val).astype(BF16)
        o_ref[:, 3 * ATT_W:3 * ATT_W + 128] = glr[...]
        o_ref[:, 3 * ATT_W + 128:] = jnp.zeros((DIL_ROWS, tail - 3 * ATT_W - 128), BF16)

    return pl.pallas_call(
        body, name="rope_bwd", grid=(s // DIL_ROWS,),
        in_specs=[_dil_spec(r) for _ in range(3) for r in _RS] + [pl.BlockSpec((DIL_ROWS, 128), lambda i: (i, 0))] * 3
        + [pl.BlockSpec(memory_space=pl.ANY)],
        out_specs=pl.BlockSpec((DIL_ROWS, tail), lambda i: (i, P_AQ // tail)),
        out_shape=jax.ShapeDtypeStruct((s, P_W), BF16), input_output_aliases={12: 0},
        scratch_shapes=_DIL_SCRATCH, compiler_params=_params(("parallel",)),
    )(*[datt[g][t] for t in range(3) for g in range(3)], cos_t, sin_t, d_glr, dp)


def _tri_dot(tri, t):
    tb = tri.astype(BF16)
    hi = t.astype(BF16)
    r1 = t - hi.astype(F32)
    mid = r1.astype(BF16)
    lo = (r1 - mid.astype(F32)).astype(BF16)
    return _dg(tb, hi, 1, 0) + _dg(tb, mid, 1, 0) + _dg(tb, lo, 1, 0)


def _gla_decays(la_c, tri):
    b = _tri_dot(tri, la_c)
    row = lax.broadcasted_iota(jnp.int32, b.shape, 0)
    bmid = jnp.sum(jnp.where(row == GLA_CHUNK // 2 - 1, b, 0.0), axis=0, keepdims=True)
    blast = jnp.sum(jnp.where(row == GLA_CHUNK - 1, b, 0.0), axis=0, keepdims=True)
    return b, bmid, blast


def _gla_fwd(p, la, s, comm=()):
    tb, ch = GLA_BLOCK, GLA_CHUNK
    nb, nc = s // tb, tb // ch
    scale = GLA_DK ** -0.5
    c_ins, c_outs, c_alias, c_scratch = _carry(comm, 4, 2)

    def body(q_ref, k_ref, v_ref, la_ref, *rest):
        ci, (o_ref, st_ref) = rest[:len(c_ins)], rest[len(c_ins):len(c_ins) + 2]
        co, state = rest[len(c_ins) + 2:len(c_ins) + 2 + len(c_outs)], rest[len(c_ins) + 2 + len(c_outs)]
        step = pl.program_id(0)
        if comm:
            @pl.when(step == 0)
            def _():
                _comm_phase(comm, ci, co, rest[-2], rest[-1], True)

        _gla_fwd_step(q_ref, k_ref, v_ref, la_ref, o_ref, st_ref, state)
        if comm:
            @pl.when(step == nb - 1)
            def _():
                _comm_phase(comm, ci, co, rest[-2], rest[-1], False)

    def _gla_fwd_step(q_ref, k_ref, v_ref, la_ref, o_ref, st_ref, state):
        @pl.when(pl.program_id(0) == 0)
        def _():
            state[...] = jnp.zeros_like(state)

        ri = lax.broadcasted_iota(jnp.int32, (ch, ch), 0)
        ci = lax.broadcasted_iota(jnp.int32, (ch, ch), 1)
        causal = ci <= ri
        tri = causal.astype(F32)

        def chunk(c, carry):
            sl = pl.ds(pl.multiple_of(c * ch, ch), ch)
            b, bmid, blast = _gla_decays(la_ref[sl, :], tri)
            q = q_ref[sl, :].astype(F32) * scale
            k = k_ref[sl, :].astype(F32)
            v = v_ref[sl, :]
            qgt = (q * jnp.exp(b)).astype(BF16)
            qgn = (q * jnp.exp(b - bmid)).astype(BF16)
            kgn = (k * jnp.exp(bmid - b)).astype(BF16)
            kd = (k * jnp.exp(blast - b)).astype(BF16)
            dec = jnp.exp(blast)
            sts = [state[h] for h in range(GLA_H)]
            outs, news = [], []
            for h in range(GLA_H):
                hk, hv = slice(h * GLA_DK, (h + 1) * GLA_DK), slice(h * GLA_DV, (h + 1) * GLA_DV)
                a = jnp.where(causal, _dg(qgn[:, hk], kgn[:, hk], 1, 1), 0.0)
                outs.append(_dg(a.astype(BF16), v[:, hv], 1, 0) + _dg(qgt[:, hk], sts[h].astype(BF16), 1, 1))
                news.append(dec[:, hk] * sts[h] + _dg(v[:, hv], kd[:, hk], 0, 0))
            for h in range(GLA_H):
                st_ref[h, c] = sts[h]
                state[h] = news[h]
            o_ref[sl, :] = jnp.concatenate(outs, axis=1)
            return carry

        lax.fori_loop(0, nc, chunk, 0, unroll=2)

    hw = GLA_H * GLA_DK
    res = pl.pallas_call(
        body, name="gla_fwd", grid=(nb,),
        in_specs=[pl.BlockSpec((tb, hw), lambda t: (t, P_GQ // hw)),
                  pl.BlockSpec((tb, hw), lambda t: (t, P_GK // hw)),
                  pl.BlockSpec((tb, GLA_H * GLA_DV), lambda t: (t, P_GV // (GLA_H * GLA_DV))),
                  pl.BlockSpec((tb, hw), lambda t: (t, 0))] + [HBM] * len(c_ins),
        out_specs=[pl.BlockSpec((tb, GLA_H * GLA_DV), lambda t: (t, 0)),
                   pl.BlockSpec((GLA_H, nc, GLA_DV, GLA_DK), lambda t: (0, t, 0, 0))] + [HBM] * len(c_outs),
        out_shape=[jax.ShapeDtypeStruct((s, GLA_H * GLA_DV), F32),
                   jax.ShapeDtypeStruct((GLA_H, s // ch, GLA_DV, GLA_DK), F32)] + c_outs,
        scratch_shapes=[pltpu.VMEM((GLA_H, GLA_DV, GLA_DK), F32)] + c_scratch,
        input_output_aliases=c_alias,
        compiler_params=_params(("arbitrary",)),
    )(p, p, p, la, *c_ins)
    return res[0], res[1], _split_units(comm, res[2:])


def _gla_bwd(p, la, states, do, s, dp, comm=()):
    tb, ch = GLA_BLOCK, GLA_CHUNK
    nb, nc = s // tb, tb // ch
    scale = GLA_DK ** -0.5
    c_ins, c_outs, c_alias, c_scratch = _carry(comm, 7, 4)

    def body(q_ref, k_ref, v_ref, la_ref, st_ref, do_ref, dp_in, *rest):
        ci, outs = rest[:len(c_ins)], rest[len(c_ins):len(c_ins) + 4]
        co, dstate = rest[len(c_ins) + 4:len(c_ins) + 4 + len(c_outs)], rest[len(c_ins) + 4 + len(c_outs)]
        step = pl.program_id(0)
        if comm:
            @pl.when(step == 0)
            def _():
                _comm_phase(comm, ci, co, rest[-2], rest[-1], True)

        _gla_bwd_step(q_ref, k_ref, v_ref, la_ref, st_ref, do_ref, *outs, dstate)
        if comm:
            @pl.when(step == nb - 1)
            def _():
                _comm_phase(comm, ci, co, rest[-2], rest[-1], False)

    def _gla_bwd_step(q_ref, k_ref, v_ref, la_ref, st_ref, do_ref, dq_ref, dk_ref, dv_ref, dla_ref, dstate):
        @pl.when(pl.program_id(0) == 0)
        def _():
            dstate[...] = jnp.zeros_like(dstate)

        ri = lax.broadcasted_iota(jnp.int32, (ch, ch), 0)
        ci = lax.broadcasted_iota(jnp.int32, (ch, ch), 1)
        causal = ci <= ri
        tri = causal.astype(F32)
        tri_t = (ci >= ri).astype(F32)

        def chunk(cc, carry):
            c = nc - 1 - cc
            sl = pl.ds(pl.multiple_of(c * ch, ch), ch)
            b, bmid, blast = _gla_decays(la_ref[sl, :], tri)
            q = q_ref[sl, :].astype(F32) * scale
            k = k_ref[sl, :].astype(F32)
            v = v_ref[sl, :]
            e_b, e_qn, e_kn, e_kd = jnp.exp(b), jnp.exp(b - bmid), jnp.exp(bmid - b), jnp.exp(blast - b)
            dec = jnp.exp(blast)
            qgt, qgn, kgn, kd = q * e_b, q * e_qn, k * e_kn, k * e_kd
            qgt_b, qgn_b, kgn_b, kd_b = qgt.astype(BF16), qgn.astype(BF16), kgn.astype(BF16), kd.astype(BF16)
            do_b = do_ref[sl, :].astype(BF16)
            st0s = [st_ref[h, c] for h in range(GLA_H)]
            dsts = [dstate[h] for h in range(GLA_H)]
            dqgn, dqgt, dkgn, dkd, dvs, ddec, news = [], [], [], [], [], [], []
            for h in range(GLA_H):
                hk, hv = slice(h * GLA_DK, (h + 1) * GLA_DK), slice(h * GLA_DV, (h + 1) * GLA_DV)
                dst_b = dsts[h].astype(BF16)
                a = jnp.where(causal, _dg(qgn_b[:, hk], kgn_b[:, hk], 1, 1), 0.0).astype(BF16)
                da = jnp.where(causal, _dg(do_b[:, hv], v[:, hv], 1, 1), 0.0).astype(BF16)
                dqgn.append(_dg(da, kgn_b[:, hk], 1, 0))
                dqgt.append(_dg(do_b[:, hv], st0s[h].astype(BF16), 1, 0))
                dkgn.append(_dg(da, qgn_b[:, hk], 0, 0))
                dvs.append(_dg(a, do_b[:, hv], 0, 0) + _dg(kd_b[:, hk], dst_b, 1, 1))
                dkd.append(_dg(v[:, hv], dst_b, 1, 0))
                ddec.append(jnp.sum(st0s[h] * dsts[h], axis=0, keepdims=True))
                news.append(dec[:, hk] * dsts[h] + _dg(do_b[:, hv], qgt_b[:, hk], 0, 0))
            for h in range(GLA_H):
                dstate[h] = news[h]
            cat = lambda parts: jnp.concatenate(parts, axis=1)
            dqgn, dqgt, dkgn, dkd, ddec = cat(dqgn), cat(dqgt), cat(dkgn), cat(dkd), cat(ddec)
            dq_ref[sl, :] = (scale * (dqgn * e_qn + dqgt * e_b)).astype(dq_ref.dtype)
            dk_ref[sl, :] = (dkgn * e_kn + dkd * e_kd).astype(dk_ref.dtype)
            dv_ref[sl, :] = cat(dvs).astype(dv_ref.dtype)
            db = dqgn * qgn + dqgt * qgt - dkgn * kgn - dkd * kd
            extra = jnp.sum(dkd * kd, axis=0, keepdims=True) + ddec * dec
            dla_ref[sl, :] = _tri_dot(tri_t, db) + extra
            return carry

        lax.fori_loop(0, nc, chunk, 0, unroll=2)

    rev = lambda t: nb - 1 - t
    hw, vw = GLA_H * GLA_DK, GLA_H * GLA_DV
    res = pl.pallas_call(
        body, name="gla_bwd", grid=(nb,),
        in_specs=[pl.BlockSpec((tb, hw), lambda t: (rev(t), P_GQ // hw)),
                  pl.BlockSpec((tb, hw), lambda t: (rev(t), P_GK // hw)),
                  pl.BlockSpec((tb, vw), lambda t: (rev(t), P_GV // vw)),
                  pl.BlockSpec((tb, hw), lambda t: (rev(t), 0)),
                  pl.BlockSpec((GLA_H, nc, GLA_DV, GLA_DK), lambda t: (0, rev(t), 0, 0)),
                  pl.BlockSpec((tb, vw), lambda t: (rev(t), 0)), pl.BlockSpec(memory_space=pl.ANY)] + [HBM] * len(c_ins),
        out_specs=[pl.BlockSpec((tb, hw), lambda t: (rev(t), 0)),
                   pl.BlockSpec((tb, hw), lambda t: (rev(t), 0)),
                   pl.BlockSpec((tb, vw), lambda t: (rev(t), P_GV // vw)),
                   pl.BlockSpec((tb, hw), lambda t: (rev(t), 0))] + [HBM] * len(c_outs),
        out_shape=[jax.ShapeDtypeStruct((s, hw), BF16),
                   jax.ShapeDtypeStruct((s, hw), BF16),
                   jax.ShapeDtypeStruct((s, P_W), BF16),
                   jax.ShapeDtypeStruct((s, hw), F32)] + c_outs,
        scratch_shapes=[pltpu.VMEM((GLA_H, GLA_DV, GLA_DK), F32)] + c_scratch,
        input_output_aliases={6: 2, **c_alias},
        compiler_params=_params(("arbitrary",)),
    )(p, p, p, la, states, do, dp, *c_ins)
    return res[0], res[1], res[2], res[3], _split_units(comm, res[4:])


def _head_masks():
    lane = lax.broadcasted_iota(jnp.int32, (1, 4 * ATT_HD), 1)
    return [(lane >= h * ATT_HD) & (lane < (h + 1) * ATT_HD) for h in range(4)]


def _attn_fwd(qv, kv, pv, g, r, s):
    ln = s // r
    nblk = ln // ATT_BLK
    qcol = lambda pr: pr
    vcol = qcol
    prev = lambda n: jnp.maximum(n - 1, 0)

    def body(q_ref, kp_ref, kc_ref, vp_ref, vc_ref, o_ref, lse_ref):
        has_prev = pl.program_id(1) > 0
        ri = lax.broadcasted_iota(jnp.int32, (ATT_BLK, ATT_BLK), 0)
        ci = lax.broadcasted_iota(jnp.int32, (ATT_BLK, ATT_BLK), 1)
        m_cur = ci <= ri
        m_prev = (ci >= ri) & has_prev
        q, kp, kc, vp, vc = q_ref[...], kp_ref[...], kc_ref[...], vp_ref[...], vc_ref[...]
        o = jnp.zeros((ATT_BLK, 256), F32)
        lse = jnp.zeros((ATT_BLK, 256), F32)
        for hm in _head_masks():
            qm = jnp.where(hm, q, jnp.zeros_like(q))
            sc = jnp.where(m_cur, _dg(qm, kc, 1, 1) * 0.125, NEG)
            sp = jnp.where(m_prev, _dg(qm, kp, 1, 1) * 0.125, NEG)
            mx = jnp.maximum(jnp.max(sc, axis=1, keepdims=True), jnp.max(sp, axis=1, keepdims=True))
            pc, pp = jnp.exp(sc - mx), jnp.exp(sp - mx)
            den = jnp.sum(pc, axis=1, keepdims=True) + jnp.sum(pp, axis=1, keepdims=True)
            oh = (_dg(pc.astype(BF16), vc, 1, 0) + _dg(pp.astype(BF16), vp, 1, 0)) / den
            o = jnp.where(hm, oh, o)
            lse = jnp.where(hm, mx + jnp.log(den), lse)
        o_ref[...] = o.astype(o_ref.dtype)
        lse_ref[...] = lse

    blk = (ATT_BLK, 256)
    o, lse = pl.pallas_call(
        body, name=f"attn_fwd_{g}", grid=(r, nblk),
        in_specs=[pl.BlockSpec(blk, lambda pr, n: (n, qcol(pr))),
                  pl.BlockSpec(blk, lambda pr, n: (prev(n), qcol(pr))),
                  pl.BlockSpec(blk, lambda pr, n: (n, qcol(pr))),
                  pl.BlockSpec(blk, lambda pr, n: (prev(n), vcol(pr))),
                  pl.BlockSpec(blk, lambda pr, n: (n, vcol(pr)))],
        out_specs=[pl.BlockSpec(blk, lambda pr, n: (n, pr)), pl.BlockSpec(blk, lambda pr, n: (n, pr))],
        out_shape=[jax.ShapeDtypeStruct((ln, r * 256), BF16), jax.ShapeDtypeStruct((ln, r * 256), F32)],
        compiler_params=_params(("parallel", "parallel")),
    )(qv, kv, kv, pv, pv)
    return o, lse


def _attn_bwd(qv, kv, pv, dov, ov, lv, g, r, s):
    ln = s // r
    nblk = ln // ATT_BLK
    qcol = lambda pr: pr
    vcol = qcol
    prev = lambda n: jnp.maximum(n - 1, 0)
    nxt = lambda n: jnp.minimum(n + 1, nblk - 1)

    def body(qc_ref, qn_ref, kp_ref, kc_ref, vp_ref, vc_ref, doc_ref, don_ref, oc_ref, on_ref, lc_ref, ln_ref,
             dq_ref, dk_ref, dv_ref):
        n = pl.program_id(1)
        has_prev, has_next = n > 0, n < nblk - 1
        ri = lax.broadcasted_iota(jnp.int32, (ATT_BLK, ATT_BLK), 0)
        ci = lax.broadcasted_iota(jnp.int32, (ATT_BLK, ATT_BLK), 1)
        m_cur = ci <= ri
        m_prev = (ci >= ri) & has_prev
        m_next = (ci >= ri) & has_next
        qc, qn, kp, kc, vp, vc = qc_ref[...], qn_ref[...], kp_ref[...], kc_ref[...], vp_ref[...], vc_ref[...]
        doc, don = doc_ref[...], don_ref[...]
        pc_full = doc.astype(F32) * oc_ref[...].astype(F32)
        pn_full = don.astype(F32) * on_ref[...].astype(F32)
        lc, lnx = lc_ref[...], ln_ref[...]
        dq = jnp.zeros((ATT_BLK, 256), F32)
        dk = jnp.zeros((ATT_BLK, 256), F32)
        dv = jnp.zeros((ATT_BLK, 256), F32)
        zb = jnp.zeros_like(qc)
        for hm in _head_masks():
            qcm, qnm = jnp.where(hm, qc, zb), jnp.where(hm, qn, zb)
            docm, donm = jnp.where(hm, doc, zb), jnp.where(hm, don, zb)
            lse_c = jnp.max(jnp.where(hm, lc, NEG), axis=1, keepdims=True)
            lse_n = jnp.max(jnp.where(hm, lnx, NEG), axis=1, keepdims=True)
            del_c = jnp.sum(jnp.where(hm, pc_full, 0.0), axis=1, keepdims=True)
            del_n = jnp.sum(jnp.where(hm, pn_full, 0.0), axis=1, keepdims=True)
            pr_ = jnp.where(m_cur, jnp.exp(_dg(qcm, kc, 1, 1) * 0.125 - lse_c), 0.0)
            ds = (pr_ * (_dg(docm, vc, 1, 1) - del_c) * 0.125).astype(BF16)
            dqh = _dg(ds, kc, 1, 0)
            dkh = _dg(ds, qc, 0, 0)
            dvh = _dg(pr_.astype(BF16), doc, 0, 0)
            pr_ = jnp.where(m_prev, jnp.exp(_dg(qcm, kp, 1, 1) * 0.125 - lse_c), 0.0)
            ds = (pr_ * (_dg(docm, vp, 1, 1) - del_c) * 0.125).astype(BF16)
            dqh = dqh + _dg(ds, kp, 1, 0)
            pr_ = jnp.where(m_next, jnp.exp(_dg(qnm, kc, 1, 1) * 0.125 - lse_n), 0.0)
            ds = (pr_ * (_dg(donm, vc, 1, 1) - del_n) * 0.125).astype(BF16)
            dkh = dkh + _dg(ds, qn, 0, 0)
            dvh = dvh + _dg(pr_.astype(BF16), don, 0, 0)
            dq = jnp.where(hm, dqh, dq)
            dk = jnp.where(hm, dkh, dk)
            dv = jnp.where(hm, dvh, dv)
        dq_ref[...] = dq.astype(dq_ref.dtype)
        dk_ref[...] = dk.astype(dk_ref.dtype)
        dv_ref[...] = dv.astype(dv_ref.dtype)

    blk = (ATT_BLK, 256)
    cur = lambda col: pl.BlockSpec(blk, lambda pr, n: (n, col(pr)))
    prv = lambda col: pl.BlockSpec(blk, lambda pr, n: (prev(n), col(pr)))
    nx = lambda col: pl.BlockSpec(blk, lambda pr, n: (nxt(n), col(pr)))
    own = lambda pr: pr
    outs = pl.pallas_call(
        body, name=f"attn_bwd_{g}", grid=(r, nblk),
        in_specs=[cur(qcol), nx(qcol), prv(qcol), cur(qcol), prv(vcol), cur(vcol),
                  cur(own), nx(own), cur(own), nx(own), cur(own), nx(own)],
        out_specs=[cur(own), cur(own), cur(own)],
        out_shape=[jax.ShapeDtypeStruct((ln, r * 256), BF16)] * 3,
        compiler_params=_params(("parallel", "parallel")),
    )(qv, qv, kv, kv, pv, pv, dov, dov, ov, ov, lv, lv)
    return outs


def _gelu_parts(gv):
    cdf = 0.5 * (1.0 + lax.erf(gv * (2.0 ** -0.5)))
    pdf = jnp.exp(-0.5 * gv * gv) * (1.0 / math.sqrt(2.0 * math.pi))
    return cdf, pdf


def _pick_row(t, k):
    row = lax.broadcasted_iota(jnp.int32, t.shape, 0)
    return jnp.sum(jnp.where(row == k, t, 0.0), axis=0, keepdims=True)


def _shift_rows(u, halo, n):
    row = lax.broadcasted_iota(jnp.int32, u.shape, 0)
    out = pltpu.roll(u, n, 0)
    for k in range(n):
        out = jnp.where(row == k, _pick_row(halo, 16 - n + k), out)
    return out


def _shift_rows_up(u, halo, n):
    rb = u.shape[0]
    row = lax.broadcasted_iota(jnp.int32, u.shape, 0)
    out = pltpu.roll(u, rb - n, 0)
    for k in range(n):
        out = jnp.where(row == rb - n + k, _pick_row(halo, k), out)
    return out


def _conv(u, halo, cw, cb):
    return cb + _pick_row(cw, 0) * _shift_rows(u, halo, 2) + _pick_row(cw, 1) * _shift_rows(u, halo, 1) + _pick_row(cw, 2) * u


def _local_step(x, mod, pos_col, target, sm, w_sh, g0, chip, core):
    s = x.shape[0]
    shift1, scale1, gate1, shift2, scale2, gate2 = [mod[i:i + 1, :] for i in range(6)]
    rb = 512
    chip1 = chip.reshape(1)

    def f_norm1(c, i, xv, nw, sc, sh):
        return ((xv * _rms(xv) * nw) * (1.0 + sc) + sh,)

    (h,) = _rowcall(f_norm1, [_rows(x, rb), _full(sm["n1w"]), _full(scale1), _full(shift1)],
                    [_orow(s, D, BF16, rb)], n_rows=s, rb=rb, name="norm1")
    own = lambda got, i: lax.dynamic_update_slice(got, w_sh[i], (chip, 0, 0))
    invf = jnp.tile(ROPE_THETA ** (-jnp.arange(ATT_HD // 2, dtype=F32) / (ATT_HD // 2)), 4).reshape(1, 128)
    cos_t, sin_t = _rope_tables(pos_col, invf, s)
    _, got0 = _unit_wait(*g0[:4], after=[h, cos_t, sin_t], name="gather_w_in_wait")
    [got0] = _comm_call("gather_w_in_d2d", [_u_gather_d2d(got0, (0,))])
    u_g1 = _u_gather_ici(w_sh, (1, 2, 3, 4, 5))
    g1 = _unit_start(u_g1, "gather_weights_start", after=got0)
    w = dict(win=_win_assemble(got0[0], w_sh[0], after=g1[3:]))
    p = _mm(h, w["win"], "in_proj", tm=2048, tn=1536)

    def f_gla_pre(c, i, glr, w2, gb):
        z = _dg(glr, w2.astype(BF16), 1, 0) + gb
        return ((jnp.minimum(z, 0.0) - jnp.log(1.0 + jnp.exp(-jnp.abs(z)))) * (1.0 / GLA_TAU),)

    (la,) = _rowcall(f_gla_pre, [_rows(p, rb, 128, P_LR // 128), _full(sm["w2"]), _full(sm["gb"])],
                     [_orow(s, 512, F32, rb)], n_rows=s, rb=rb, name="gla_pre")
    o_gla, states, _ = _gla_fwd(p, la, s)
    _, got = _unit_wait(u_g1, *g1[:3], after=[o_gla], name="gather_weights_wait")
    [got123] = _comm_call("gather_weights_d2d", [_u_gather_d2d(got[:3], (1, 2, 3))])
    got45 = got[3:]
    w.update(wgb=own(got123[0], 1).reshape(1024, D), wab=_cols_join(own(got123[1], 2)), wout=own(got123[2], 3).reshape(D, D))

    def f_gla_post(c, i, ov, gnw, gr):
        on = jnp.concatenate([ov[:, k * 256:(k + 1) * 256] * _rms(ov[:, k * 256:(k + 1) * 256]) * gnw
                              for k in range(GLA_H)], axis=1)
        g = gr.astype(F32)
        return (on * (g * _sigmoid(g)),)

    (og,) = _rowcall(f_gla_post, [_rows(o_gla, rb), _full(sm["gnw"]), _rows(p, rb, 1024, P_GR // 1024)],
                     [_orow(s, 1024, BF16, rb)], n_rows=s, rb=rb, name="gla_post")
    y_gla = _mm(og, w["wgb"], "gla_branch")

    q_d, k_d, v_d = _rope_fwd(p, cos_t, sin_t, s)
    att = [_attn_fwd(q_d[g], k_d[g], v_d[g], g, r, s) for g, r in enumerate(_RS)]
    o_att, lse, o_d1, o_d2, lse_d1, lse_d2 = _attn_combine(att, s)
    y_att = _mm(o_att, w["wab"], "attn_branch")

    def f_merge(c, i, ma, mb, yg, ya):
        return (_sigmoid(ma.astype(F32)) * yg.astype(F32) + _sigmoid(mb.astype(F32)) * ya.astype(F32),)

    (mixed,) = _rowcall(f_merge, [_rows(p, rb, D, P_MA // D), _rows(p, rb, D, P_MB // D), _rows(y_gla, rb), _rows(y_att, rb)],
                        [_orow(s, D, BF16, rb)], n_rows=s, rb=rb, name="merge")
    z1, [got45] = _mm(mixed, w["wout"], "out_proj", comm=[_u_gather_d2d(got45, (4, 5))])
    w.update(wup=own(got45[0], 4), wdown=own(got45[1], 5).reshape(D_FF, D))

    def f_norm2(c, i, xv, z, g1, nw, sc, sh):
        x1 = xv + g1 * z.astype(F32)
        return (x1, (x1 * _rms(x1) * nw) * (1.0 + sc) + sh)

    x1, h2 = _rowcall(f_norm2, [_rows(x, rb), _rows(z1, rb), _full(gate1), _full(sm["n2w"]), _full(scale2), _full(shift2)],
                      [_orow(s, D, F32, rb), _orow(s, D, BF16, rb)], n_rows=s, rb=rb, name="norm2")
    u = _mm(h2, w["wup"], "up_proj", tm=2048, b_shards=True)

    cwid = 2 * W_UP_SH

    def f_ffn(c, i, uv, hl, cw, cb):
        uc = _conv(uv.astype(F32), hl.astype(F32) * (i > 0).astype(F32), cw, cb)
        val, gt = uc[:, :W_UP_SH], uc[:, W_UP_SH:]
        cdf, _ = _gelu_parts(gt)
        return (gt * cdf * val,)

    ccol = lambda c: c
    rw = 256
    (hidden,) = _rowcall(f_ffn, [_rows(u, rw, cwid, ccol), _halo(u, rw, 16, cwid, ccol, True),
                                 _full(sm["cw"], cwid, ccol), _full(sm["cb"], cwid, ccol)],
                         [_orow(s, D_FF, BF16, rw, W_UP_SH, ccol)], n_rows=s, rb=rw, name="conv_geglu", ncol=2)
    z2 = _mm(hidden, w["wdown"], "down_proj", tk=D_FF)

    def f_final(c, i, x1v, z, g2, fw, tgt):
        x2 = x1v + g2 * z.astype(F32)
        r = _rms(x2)
        xh = x2 * r
        e = xh * fw - tgt
        loss = 0.5 * jnp.sum(jnp.mean(e * e, axis=-1, keepdims=True), axis=0, keepdims=True)
        dy = e * (1.0 / D)
        dxh = dy * fw
        dx2 = r * (dxh - xh * jnp.mean(dxh * xh, axis=-1, keepdims=True))
        return (loss, dx2, dx2 * g2, _csum(dy * xh), _csum(dx2 * z.astype(F32)))

    loss, dx2, dz2, d_fnw, d_gate2 = _rowcall(
        f_final, [_rows(x1, rb), _rows(z2, rb), _full(gate2), _full(sm["fnw"]), _rows(target, rb)],
        [_oacc(1, 1), _orow(s, D, F32, rb), _orow(s, D, BF16, rb), _oacc(1, D), _oacc(1, D)],
        n_rows=s, rb=rb, name="final_loss")
    d_hidden = _mm(dz2, w["wdown"], "down_proj_dx", tb=True, tn=1408)
    g_wdown = _mm(hidden, dz2, "down_proj_dw", ta=True, out_dtype=F32, tm=1408, tn=1024, tk=2048)

    def f_ffn_bwd(c, i, uv, hl, dh, cw, cb):
        uf = uv.astype(F32)
        hf = hl.astype(F32) * (i > 0).astype(F32)
        u1, u2 = _shift_rows(uf, hf, 1), _shift_rows(uf, hf, 2)
        uc = cb + _pick_row(cw, 0) * u2 + _pick_row(cw, 1) * u1 + _pick_row(cw, 2) * uf
        val, gt = uc[:, :W_UP_SH], uc[:, W_UP_SH:]
        cdf, pdf = _gelu_parts(gt)
        dhf = dh.astype(F32)
        duc = jnp.concatenate([dhf * (gt * cdf), dhf * val * (cdf + gt * pdf)], axis=1)
        dcw = jnp.concatenate([_csum(duc * u2), _csum(duc * u1), _csum(duc * uf)], axis=0)
        return (duc, _csum(duc), dcw)

    duc, d_cb, d_cw = _rowcall(
        f_ffn_bwd, [_rows(u, rw, cwid, ccol), _halo(u, rw, 16, cwid, ccol, True), _rows(d_hidden, rw, W_UP_SH, ccol),
                    _full(sm["cw"], cwid, ccol), _full(sm["cb"], cwid, ccol)],
        [_orow(s, 2 * D_FF, BF16, rw, cwid, ccol), _oacc(1, 2 * D_FF, cwid, ccol), _oacc(3, 2 * D_FF, cwid, ccol)],
        n_rows=s, rb=rw, name="conv_geglu_bwd", ncol=2)

    def f_conv_t(c, i, dv, hl, cw):
        df = dv.astype(F32)
        hf = hl.astype(F32) * (i < s // rw - 1).astype(F32)
        return (_pick_row(cw, 2) * df + _pick_row(cw, 1) * _shift_rows_up(df, hf, 1) + _pick_row(cw, 0) * _shift_rows_up(df, hf, 2),)

    (du,) = _rowcall(f_conv_t, [_rows(duc, rw, cwid, ccol), _halo(duc, rw, 16, cwid, ccol, False), _full(sm["cw"], cwid, ccol)],
                     [_orow(s, 2 * D_FF, BF16, rw, cwid, ccol)], n_rows=s, rb=rw, name="conv_transpose", ncol=2)
    g_wup = _mm(h2, du, "up_proj_dw", ta=True, out_dtype=F32, tm=1024, tk=2048, o_shards=True)
    gs45 = [g_wup, g_wdown.reshape(4, W_DOWN_SH, 1024)]
    d_h2, [land45] = _mm(du, w["wup"], "up_proj_dx", tb=True, tm=2048, b_shards=True, comm=[_u_pair_send(gs45, (4, 5))])
    ts45 = [_pair_add(g, ld, core, "grad_pair_add_" + BIG[i]) for g, ld, i in zip(gs45, land45, (4, 5))]
    u_ex4 = _u_chip_exchange(ts45[:1])
    ex4 = _unit_start(u_ex4, "grad_exchange_w_up_start")

    def f_norm2_bwd(c, i, x1v, dh, dxr, z, nw, sc, g1):
        dxn, dsh, dsc, dnw = _norm_bwd(x1v, dh.astype(F32), nw, sc)
        dx1 = dxr + dxn
        return (dx1, dx1 * g1, dsh, dsc, dnw, _csum(dx1 * z.astype(F32)))

    dx1, dz1, d_shift2, d_scale2, d_n2w, d_gate1 = _rowcall(
        f_norm2_bwd, [_rows(x1, rb), _rows(d_h2, rb), _rows(dx2, rb), _rows(z1, rb), _full(sm["n2w"]), _full(scale2), _full(gate1)],
        [_orow(s, D, F32, rb), _orow(s, D, BF16, rb), _oacc(1, D), _oacc(1, D), _oacc(1, D), _oacc(1, D)],
        n_rows=s, rb=rb, name="norm2_bwd", after=ex4[3:])
    d_mixed = _mm(dz1, w["wout"], "out_proj_dx", tb=True)
    g_wout = _mm(mixed, dz1, "out_proj_dw", ta=True, out_dtype=F32, tk=2048)

    def f_merge_bwd(c, i, dm, ma, mb, yg, ya):
        dmf, ygf, yaf = dm.astype(F32), yg.astype(F32), ya.astype(F32)
        sa, sb = _sigmoid(ma.astype(F32)), _sigmoid(mb.astype(F32))
        return (dmf * sa, dmf * sb, jnp.concatenate([dmf * ygf * sa * (1.0 - sa), dmf * yaf * sb * (1.0 - sb)], axis=1))

    dy_gla, dy_att, dp = _rowcall(
        f_merge_bwd, [_rows(d_mixed, rb), _rows(p, rb, D, P_MA // D), _rows(p, rb, D, P_MB // D), _rows(y_gla, rb), _rows(y_att, rb)],
        [_orow(s, D, BF16, rb)] * 2 + [_orow(s, P_W, BF16, rb, 2 * D, lambda c: P_MA // (2 * D))], n_rows=s, rb=rb, name="merge_bwd")
    d_og = _mm(dy_gla, w["wgb"], "gla_branch_dx", tb=True)
    g_wgb = _mm(og, dy_gla, "gla_branch_dw", ta=True, out_dtype=F32, tk=2048)
    d_oatt = _mm(dy_att, w["wab"], "attn_branch_dx", tb=True)
    g_wab = _mm(o_att, dy_att, "attn_branch_dw", ta=True, out_dtype=F32, tk=2048)

    def f_gla_post_bwd(c, i, ov, gnw, gr, dog):
        g = gr.astype(F32)
        sg = _sigmoid(g)
        silu = g * sg
        dof = dog.astype(F32)
        don = dof * silu
        on_parts, do_parts, dgn = [], [], jnp.zeros((1, 256), F32)
        for k in range(GLA_H):
            oh = ov[:, k * 256:(k + 1) * 256]
            dh = don[:, k * 256:(k + 1) * 256]
            r = _rms(oh)
            xh = oh * r
            dgn = dgn + _csum(dh * xh)
            dxh = dh * gnw
            do_parts.append(r * (dxh - xh * jnp.mean(dxh * xh, axis=-1, keepdims=True)))
            on_parts.append(xh * gnw)
        on = jnp.concatenate(on_parts, axis=1)
        dgr = dof * on * (sg * (1.0 + g * (1.0 - sg)))
        return (jnp.concatenate(do_parts, axis=1), dgr, dgn)

    do_gla, dp, d_gnw = _rowcall(
        f_gla_post_bwd, [_rows(o_gla, rb), _full(sm["gnw"]), _rows(p, rb, 1024, P_GR // 1024), _rows(d_og, rb)],
        [_orow(s, 1024, F32, rb), _orow(s, P_W, BF16, rb, 1024, lambda c: P_GR // 1024), _oacc(1, 256)],
        n_rows=s, rb=rb, name="gla_post_bwd", into=(dp, 1))
    gs123 = [g_wgb.reshape(4, 256, 1024), _cols_split(g_wab), g_wout.reshape(4, 256, 1024)]
    d_gq, d_gk, dp, d_la, [land123] = _gla_bwd(p, la, states, do_gla, s, dp, comm=[_u_pair_send(gs123, (1, 2, 3))])
    ts123 = [_pair_add(g, ld, core, "grad_pair_add_" + BIG[i]) for g, ld, i in zip(gs123, land123, (1, 2, 3))]

    def f_gla_pre_bwd(c, i, lav, dlav, glr, w2):
        dz = dlav * (1.0 / GLA_TAU) * (1.0 - jnp.exp(GLA_TAU * lav))
        dzb = dz.astype(BF16)
        return (_dg(dzb, w2.astype(BF16), 1, 1), _csum(dz), _dg(glr, dzb, 0, 0))

    d_glr, d_gb, d_w2 = _rowcall(
        f_gla_pre_bwd, [_rows(la, rb), _rows(d_la, rb), _rows(p, rb, 128, P_LR // 128), _full(sm["w2"])],
        [_orow(s, 128, BF16, rb), _oacc(1, 512), _oacc(128, 512)], n_rows=s, rb=rb, name="gla_pre_bwd")

    do_d = [d_oatt] + list(_dilate(d_oatt, s))
    datt = [_attn_bwd(q_d[g], k_d[g], v_d[g], do_d[g], (o_att, o_d1, o_d2)[g], (lse, lse_d1, lse_d2)[g], g, r, s)
            for g, r in enumerate(_RS)]
    dp = _rope_bwd(datt, d_glr, dp, cos_t, sin_t, s)
    dp = lax.dynamic_update_slice(dp, jnp.concatenate([d_gq, d_gk], axis=1), (0, P_GQ))
    [t4], [r4] = _unit_wait(u_ex4, *ex4[:3], after=[dp], name="grad_exchange_w_up_wait")
    half4 = [_chip_sum(t4, r4, chip1, "grad_chip_sum_w_up")]
    g_win, [r1235, oth4] = _mm(h, dp, "in_proj_dw", ta=True, out_dtype=F32, tm=1024, tn=2560, tk=1024,
                               comm=[_u_chip_exchange(ts123 + ts45[1:]), _u_pair_join(half4)])
    half1235 = [_chip_sum(t, r, chip1, "grad_chip_sum_" + BIG[i]) for t, r, i in zip(ts123 + ts45[1:], r1235, (1, 2, 3, 5))]
    gs0 = [_win_split(g_win)]
    d_h, [land0, oth1235] = _mm(dp, w["win"], "in_proj_dx", tb=True, tk=3840,
                                comm=[_u_pair_send(gs0, (0,)), _u_pair_join(half1235)])
    half123, half45 = half1235[:3], half4 + half1235[3:]
    oth123, oth45 = oth1235[:3], oth4 + oth1235[3:]
    ts0 = _pair_add(gs0[0], land0[0], core, "grad_pair_add_w_in")

    def f_norm1_bwd(c, i, xv, dh, dxr, nw, sc):
        dxn, dsh, dsc, dnw = _norm_bwd(xv, dh.astype(F32), nw, sc)
        return (dxr + dxn, dsh, dsc, dnw)

    grad_x, d_shift1, d_scale1, d_n1w = _rowcall(
        f_norm1_bwd, [_rows(x, rb), _rows(d_h, rb), _rows(dx1, rb), _full(sm["n1w"]), _full(scale1)],
        [_orow(s, D, F32, rb), _oacc(1, D), _oacc(1, D), _oacc(1, D)], n_rows=s, rb=rb, name="norm1_bwd")

    dmod = jnp.concatenate([d_shift1, d_scale1, d_gate1, d_shift2, d_scale2, d_gate2], axis=1)
    small = dict(dmod=dmod, n1w=d_n1w, gb=d_gb, gnw=d_gnw, n2w=d_n2w, cb=d_cb, fnw=d_fnw, w2=d_w2, cw=d_cw)
    return loss, grad_x, half123 + half45, oth123 + oth45, small, ts0


def _win_pieces():
    runs = [(P_GV, 1024, 2048), (P_MA, 5392, 2048), (P_GQ, 0, 1024), (P_AQ, 3088, 2304), (P_LR, 3072, GLA_LR)]
    out = []
    for kc, rc, ln in runs:
        while ln > 0:
            step = min(ln, W_IN_SH - rc % W_IN_SH)
            out.append((kc, rc, step))
            kc, rc, ln = kc + step, rc + step, ln - step
    return out


def _win_assemble(shards, own, after=()):
    rb = 256

    def body(s_ref, own_ref, *rest):
        o_ref = rest[-1]
        x, y, _ = _me()
        o_ref[:, W_IN:] = jnp.zeros((rb, P_W - W_IN), o_ref.dtype)
        for kc, rc, ln in _win_pieces():
            sh, lo = rc // W_IN_SH, rc % W_IN_SH
            o_ref[:, kc:kc + ln] = jnp.where(2 * x + y == sh, own_ref[0, :, lo:lo + ln], s_ref[sh, :, lo:lo + ln])

    return pl.pallas_call(
        body, name="w_in_assemble", grid=(D // rb,),
        in_specs=[pl.BlockSpec((4, rb, W_IN_SH), lambda i: (0, i, 0)), pl.BlockSpec((1, rb, W_IN_SH), lambda i: (0, i, 0))]
        + [pl.BlockSpec(memory_space=pl.ANY)] * len(after),
        out_specs=pl.BlockSpec((rb, P_W), lambda i: (i, 0)),
        out_shape=jax.ShapeDtypeStruct((D, P_W), shards.dtype), compiler_params=_params(("parallel",)),
    )(shards, own, *after)


def _win_split(g):
    rb = 256

    def body(g_ref, o_ref):
        for kc, rc, ln in _win_pieces():
            o_ref[rc // W_IN_SH, :, rc % W_IN_SH:rc % W_IN_SH + ln] = g_ref[:, kc:kc + ln]

    return pl.pallas_call(
        body, name="w_in_grad_split", grid=(D // rb,),
        in_specs=[pl.BlockSpec((rb, P_W), lambda i: (i, 0))], out_specs=pl.BlockSpec((4, rb, W_IN_SH), lambda i: (0, i, 0)),
        out_shape=jax.ShapeDtypeStruct((4, D, W_IN_SH), g.dtype), compiler_params=_params(("parallel",)),
    )(g)


def _ff_to_kernel(a):
    h = W_UP_SH
    return jnp.concatenate([a[:, 0:h], a[:, D_FF:D_FF + h], a[:, h:D_FF], a[:, D_FF + h:]], axis=1)


def _ff_from_kernel(a):
    h = W_UP_SH
    return jnp.concatenate([a[:, 0:h], a[:, 2 * h:3 * h], a[:, h:2 * h], a[:, 3 * h:]], axis=1)


BIG = ("w_in", "w_gla_branch", "w_attn_branch", "w_out", "w_up", "w_down")
SH_SHAPES = ((1024, W_IN_SH), (256, 1024), (256, 256), (256, 1024), (1024, W_UP_SH), (W_DOWN_SH, 1024))
N_BIG = len(BIG)


def _cols_join(t):
    return jnp.concatenate([t[k] for k in range(4)], axis=1)


def _cols_split(t):
    cols = t.shape[1] // 4
    return jnp.stack([t[:, k * cols:(k + 1) * cols] for k in range(4)])


def _me():
    return lax.axis_index("x"), lax.axis_index("y"), lax.axis_index("c")


HBM = pl.BlockSpec(memory_space=pltpu.HBM)
VMEM_SPEC = pl.BlockSpec(memory_space=pltpu.VMEM)


def _allgather8(xs, name):
    rows = xs.shape[0]

    def body(x_ref, out_ref, send_sems, recv_sems, local_sem):
        x, y, c = _me()
        me = 4 * x + 2 * y + c
        mine = pltpu.make_async_copy(x_ref, out_ref.at[me], local_sem)
        mine.start()
        flips = [(k >> 2 & 1, k >> 1 & 1, k & 1) for k in range(1, 8)]

        def peer(f):
            return (jnp.where(f[0] == 1, 1 - x, x), jnp.where(f[1] == 1, 1 - y, y), jnp.where(f[2] == 1, 1 - c, c))

        sends = []
        for k, f in enumerate(flips):
            cp = pltpu.make_async_remote_copy(src_ref=x_ref, dst_ref=out_ref.at[me], send_sem=send_sems.at[k],
                                              recv_sem=recv_sems.at[k], device_id=peer(f), device_id_type=MESH)
            cp.start()
            sends.append(cp)
        for k, f in enumerate(flips):
            px, py, pc = peer(f)
            pltpu.make_async_remote_copy(src_ref=x_ref, dst_ref=out_ref.at[4 * px + 2 * py + pc], send_sem=send_sems.at[k],
                                         recv_sem=recv_sems.at[k], device_id=peer(f), device_id_type=MESH).wait_recv()
        for cp in sends:
            cp.wait_send()
        mine.wait()

    return pl.pallas_call(
        body, name=name, out_shape=jax.ShapeDtypeStruct((8, rows, 128), F32),
        in_specs=[VMEM_SPEC], out_specs=VMEM_SPEC,
        scratch_shapes=[pltpu.SemaphoreType.DMA((7,)), pltpu.SemaphoreType.DMA((7,)), pltpu.SemaphoreType.DMA],
        compiler_params=pltpu.CompilerParams(vmem_limit_bytes=VMEM_LIMIT),
    )(xs)


def _half_rows(i, cc, unit):
    rows = SH_SHAPES[i][0] // 2
    return pl.ds(pl.multiple_of(cc * rows, unit), rows)


def _rc(src, dst, sems, to):
    return pltpu.make_async_remote_copy(src_ref=src, dst_ref=dst, send_sem=sems[0], recv_sem=sems[1], device_id=to, device_id_type=MESH)


def _other_chips(x, y):
    return [(1 - x, y), (x, 1 - y), (1 - x, 1 - y)]


def _u_gather_ici(w_sh, idxs):
    def copies(ins, outs, sem):
        x, y, c = _me()
        res = []
        for j, (px, py) in enumerate(_other_chips(x, y)):
            for n, i in enumerate(idxs):
                src = ins[n].at[0, _half_rows(i, c, 16)]
                res.append((_rc(src, outs[n].at[2 * x + y, _half_rows(i, c, 16)], sem(j * len(idxs) + n), (px, py, c)),
                            _rc(src, outs[n].at[2 * px + py, _half_rows(i, c, 16)], sem(j * len(idxs) + n), (px, py, c))))
        return res

    return dict(ins=[w_sh[i] for i in idxs], outs=[jax.ShapeDtypeStruct((4,) + SH_SHAPES[i], BF16) for i in idxs],
                nsem=3 * len(idxs), alias={}, copies=copies)


def _u_gather_d2d(got, idxs):
    def copies(ins, outs, sem):
        x, y, c = _me()
        res = []
        for j, (px, py) in enumerate(_other_chips(x, y)):
            for n, i in enumerate(idxs):
                src = ins[n].at[2 * px + py, _half_rows(i, c, 16)]
                res.append((_rc(src, outs[n].at[2 * px + py, _half_rows(i, c, 16)], sem(j * len(idxs) + n), (x, y, 1 - c)),
                            _rc(src, outs[n].at[2 * px + py, _half_rows(i, 1 - c, 16)], sem(j * len(idxs) + n), (x, y, 1 - c))))
        return res

    return dict(ins=list(got), outs=[jax.ShapeDtypeStruct(g.shape, g.dtype) for g in got], nsem=3 * len(idxs),
                alias={n: n for n in range(len(idxs))}, copies=copies)


def _u_pair_send(gs, idxs):
    def copies(ins, outs, sem):
        x, y, c = _me()
        res = []
        for n, i in enumerate(idxs):
            for sh in range(4):
                cp = _rc(ins[n].at[sh, _half_rows(i, 1 - c, 8)], outs[n].at[sh], sem(4 * n + sh), (x, y, 1 - c))
                res.append((cp, cp))
        return res

    return dict(ins=list(gs), outs=[jax.ShapeDtypeStruct((4, SH_SHAPES[i][0] // 2, SH_SHAPES[i][1]), F32) for i in idxs],
                nsem=4 * len(idxs), alias={}, copies=copies)


def _u_chip_exchange(ts):
    def copies(ins, outs, sem):
        x, y, c = _me()
        res = []
        for j, (px, py) in enumerate(_other_chips(x, y)):
            for n in range(len(ts)):
                cp = _rc(ins[n].at[2 * px + py], outs[n].at[j], sem(j * len(ts) + n), (px, py, c))
                res.append((cp, cp))
        return res

    return dict(ins=list(ts), outs=[jax.ShapeDtypeStruct((3,) + t.shape[1:], t.dtype) for t in ts], nsem=3 * len(ts),
                alias={}, copies=copies)


def _u_pair_join(hs):
    def copies(ins, outs, sem):
        x, y, c = _me()
        res = []
        for n in range(len(hs)):
            cp = _rc(ins[n], outs[n], sem(n), (x, y, 1 - c))
            res.append((cp, cp))
        return res

    return dict(ins=list(hs), outs=[jax.ShapeDtypeStruct(h.shape, h.dtype) for h in hs], nsem=len(hs), alias={}, copies=copies)


def _comm_phase(units, ci, co, send_sems, recv_sems, start):
    ii = oo = off = 0
    for u in units:
        ni, no = len(u["ins"]), len(u["outs"])
        for st, arrival in u["copies"](ci[ii:ii + ni], co[oo:oo + no], lambda k, off=off: (send_sems.at[off + k], recv_sems.at[off + k])):
            if start:
                st.start()
            else:
                st.wait_send()
                arrival.wait_recv()
        ii, oo, off = ii + ni, oo + no, off + u["nsem"]


def _carry(units, n_in, n_out):
    ins = [a for u in units for a in u["ins"]]
    outs = [o for u in units for o in u["outs"]]
    alias, ii, oo = {}, 0, 0
    for u in units:
        for a, b in u["alias"].items():
            alias[n_in + ii + a] = n_out + oo + b
        ii, oo = ii + len(u["ins"]), oo + len(u["outs"])
    nsem = sum(u["nsem"] for u in units)
    scratch = [pltpu.SemaphoreType.DMA((nsem,)), pltpu.SemaphoreType.DMA((nsem,))] if units else []
    return ins, outs, alias, scratch


def _split_units(units, res):
    out, oo = [], 0
    for u in units:
        out.append(list(res[oo:oo + len(u["outs"])]))
        oo += len(u["outs"])
    return out


def _comm_call(name, units):
    ins, outs, alias, scratch = _carry(units, 0, 0)

    def body(*refs):
        ci, co = refs[:len(ins)], refs[len(ins):len(ins) + len(outs)]
        _comm_phase(units, ci, co, refs[-2], refs[-1], True)
        _comm_phase(units, ci, co, refs[-2], refs[-1], False)

    res = pl.pallas_call(body, name=name, out_shape=outs, in_specs=[HBM] * len(ins), out_specs=[HBM] * len(outs),
                         scratch_shapes=scratch, input_output_aliases=alias)(*ins)
    return _split_units(units, res)


SEM = pl.BlockSpec(memory_space=pltpu.SEMAPHORE)
EFFECT = pltpu.SideEffectType.DATAFLOW_SIDE_EFFECTING


def _unit_start(unit, name, after=()):
    bufs = list(unit["ins"]) + [lax.empty(o.shape, o.dtype) for o in unit["outs"]]
    n_i, n_b, ns = len(unit["ins"]), len(bufs), unit["nsem"]

    def body(*refs):
        send_sems, recv_sems = refs[n_b + len(after)], refs[n_b + len(after) + 1]
        for st, _ in unit["copies"](refs[:n_i], refs[n_i:n_b], lambda k: (send_sems.at[k], recv_sems.at[k])):
            st.start()
        refs[-1][...] = jnp.zeros_like(refs[-1])

    res = pl.pallas_call(
        body, name=name,
        out_shape=[pltpu.SemaphoreType.DMA((ns,)), pltpu.SemaphoreType.DMA((ns,))] + [pltpu.HBM(b.shape, b.dtype) for b in bufs]
        + [jax.ShapeDtypeStruct((8, 128), F32)],
        in_specs=[HBM] * n_b + [pl.BlockSpec(memory_space=pl.ANY)] * len(after), out_specs=[SEM, SEM] + [HBM] * n_b + [VMEM_SPEC],
        input_output_aliases={i: 2 + i for i in range(n_b)},
        compiler_params=pltpu.CompilerParams(has_side_effects=EFFECT),
    )(*[pltpu.with_memory_space_constraint(b, pltpu.HBM) for b in bufs], *after)
    return res[0], res[1], list(res[2:2 + n_b]), res[-1]


def _unit_wait(unit, send_sems, recv_sems, bufs, after, name):
    n_i, n_b = len(unit["ins"]), len(bufs)

    def body(*refs):
        ss, rs = refs[n_b], refs[n_b + 1]
        for st, arrival in unit["copies"](refs[:n_i], refs[n_i:n_b], lambda k: (ss.at[k], rs.at[k])):
            st.wait_send()
            arrival.wait_recv()

    res = pl.pallas_call(
        body, name=name, out_shape=[pltpu.HBM(b.shape, b.dtype) for b in bufs],
        in_specs=[HBM] * n_b + [SEM, SEM] + [pl.BlockSpec(memory_space=pl.ANY)] * len(after), out_specs=[HBM] * n_b,
        input_output_aliases={i: i for i in range(n_b)}, compiler_params=pltpu.CompilerParams(has_side_effects=EFFECT),
    )(*bufs, send_sems, recv_sems, *after)
    return list(res[:n_i]), list(res[n_i:])


def _pair_add(g, land, core, name):
    _, rows, cols = g.shape
    half = rows // 2
    rb = _tile(half, 512, 16)
    nb = half // rb

    def body(c_ref, g_ref, l_ref, o_ref):
        o_ref[...] = (g_ref[...] + l_ref[...]).astype(BF16)

    return pl.pallas_call(
        body, name=name,
        grid_spec=pltpu.PrefetchScalarGridSpec(
            num_scalar_prefetch=1, grid=(4, nb),
            in_specs=[pl.BlockSpec((1, rb, cols), lambda s, i, c_ref: (s, c_ref[0] * nb + i, 0)),
                      pl.BlockSpec((1, rb, cols), lambda s, i, c_ref: (s, i, 0))],
            out_specs=pl.BlockSpec((1, rb, cols), lambda s, i, c_ref: (s, i, 0))),
        out_shape=jax.ShapeDtypeStruct((4, half, cols), BF16),
        compiler_params=_params(("parallel", "parallel")),
    )(core, g, land)


def _chip_sum(t, r, chip, name):
    _, half, cols = t.shape
    rb = _tile(half, 512, 16)

    def body(s_ref, t_ref, r_ref, o_ref):
        o_ref[...] = ((t_ref[0].astype(F32) + r_ref[0].astype(F32)) + r_ref[1].astype(F32)) + r_ref[2].astype(F32)

    return pl.pallas_call(
        body, name=name,
        grid_spec=pltpu.PrefetchScalarGridSpec(
            num_scalar_prefetch=1, grid=(half // rb,),
            in_specs=[pl.BlockSpec((1, rb, cols), lambda i, s_ref: (s_ref[0], i, 0)),
                      pl.BlockSpec((3, rb, cols), lambda i, s_ref: (0, i, 0))],
            out_specs=pl.BlockSpec((rb, cols), lambda i, s_ref: (i, 0))),
        out_shape=jax.ShapeDtypeStruct((half, cols), F32),
        compiler_params=_params(("parallel",)),
    )(chip, t, r)


def _adam_math(wv, gv, mv, vv):
    mn = ADAM_B1 * mv + (1.0 - ADAM_B1) * gv
    vn = ADAM_B2 * vv + (1.0 - ADAM_B2) * (gv * gv)
    m_hat = mn / (1.0 - ADAM_B1 ** ADAM_STEP)
    v_hat = vn / (1.0 - ADAM_B2 ** ADAM_STEP)
    return -ADAM_LR * (m_hat / (jnp.sqrt(v_hat) + ADAM_EPS) + ADAM_WD * wv), mn, vn


def _adamw_halves(wt, mt, vt, mine, theirs, core, name):
    _, rows, cols = wt.shape
    half = rows // 2
    rb = _tile(half, 256, 8)
    nb = half // rb

    def body(c_ref, w_ref, m_ref, v_ref, a_ref, b_ref, g_ref, d_ref, mo_ref, vo_ref):
        gv = jnp.where(pl.program_id(0) == c_ref[0], a_ref[...], b_ref[...])
        dl, mn, vn = _adam_math(w_ref[...], gv, m_ref[...], v_ref[...])
        g_ref[...] = gv
        d_ref[...] = dl
        mo_ref[...] = mn
        vo_ref[...] = vn

    full = pl.BlockSpec((None, rb, cols), lambda hf, i, c_ref: (0, hf * nb + i, 0))
    part = pl.BlockSpec((rb, cols), lambda hf, i, c_ref: (i, 0))
    return pl.pallas_call(
        body, name=name,
        grid_spec=pltpu.PrefetchScalarGridSpec(num_scalar_prefetch=1, grid=(2, nb), in_specs=[full, full, full, part, part],
                                               out_specs=[full] * 4),
        out_shape=[jax.ShapeDtypeStruct((1, rows, cols), F32)] * 4,
        compiler_params=_params(("parallel", "parallel")),
    )(core, wt, mt, vt, mine, theirs)


SG_REP = 144
SG_LOSS = 136
SG_W2, SG_CW = SG_REP, SG_REP + 4 * 16
SG_ROWS = SG_CW + 4 * 40
SP_ROWS = SG_REP + 16 + 40


def _mod_shard(c_all, ada_w_sh):
    def body(c_ref, w_ref, o_ref):
        cv = c_ref[...]
        o_ref[...] = _dg((cv * _sigmoid(cv)).astype(BF16), w_ref[...].astype(BF16), 1, 0)

    return pl.pallas_call(body, name="mod_shard", out_shape=jax.ShapeDtypeStruct((8, 1536), F32),
                          in_specs=[VMEM_SPEC, VMEM_SPEC], out_specs=VMEM_SPEC,
                          compiler_params=pltpu.CompilerParams(vmem_limit_bytes=VMEM_LIMIT))(c_all, ada_w_sh)


def _mod_select(mod_all, ada_b4):
    def body(m_ref, b_ref, o_ref):
        x, y, c = _me()
        me = 4 * x + 2 * y + c
        for sh in range(4):
            o_ref[sh] = m_ref[2 * sh, me] + b_ref[sh]

    return pl.pallas_call(body, name="mod_select", out_shape=jax.ShapeDtypeStruct((4, 12, 128), F32),
                          in_specs=[VMEM_SPEC, VMEM_SPEC], out_specs=VMEM_SPEC)(mod_all, ada_b4)


def _small_reduce(sg_all):
    def body(g_ref, o_ref):
        x, y, c = _me()
        s_me = 2 * x + y
        w2_rows = pl.ds(pl.multiple_of(SG_W2 + 16 * s_me, 8), 16)
        cw_rows = pl.ds(pl.multiple_of(SG_CW + 40 * s_me, 8), 40)
        a = g_ref[0, 0:SG_REP, :]
        b = g_ref[0, w2_rows, :]
        d = g_ref[0, cw_rows, :]
        for dev in range(1, 8):
            a = a + g_ref[dev, 0:SG_REP, :]
            b = b + g_ref[dev, w2_rows, :]
            d = d + g_ref[dev, cw_rows, :]
        o_ref[0:SG_REP, :] = a
        o_ref[SG_REP:SG_REP + 16, :] = b
        o_ref[SG_REP + 16:SP_ROWS, :] = d

    return pl.pallas_call(body, name="small_grad_reduce", out_shape=jax.ShapeDtypeStruct((SP_ROWS, 128), F32),
                          in_specs=[VMEM_SPEC], out_specs=VMEM_SPEC)(sg_all)


def _ada_grad(dmod_all, c_bc):
    def body(g_ref, c_ref, o_ref):
        x, y, c = _me()
        s_me = 2 * x + y
        for k in range(12):
            acc = jnp.zeros((D, 128), F32)
            for b in range(8):
                cv = c_ref[b]
                acc = acc + (cv * _sigmoid(cv)) * g_ref[s_me, k, b:b + 1, :]
            o_ref[:, k * 128:(k + 1) * 128] = acc

    return pl.pallas_call(body, name="ada_w_grad", out_shape=jax.ShapeDtypeStruct((D, 1536), F32),
                          in_specs=[VMEM_SPEC, VMEM_SPEC], out_specs=VMEM_SPEC,
                          compiler_params=pltpu.CompilerParams(vmem_limit_bytes=VMEM_LIMIT))(dmod_all, c_bc)


def _adamw(wt, g, m, v, name):
    rows, cols = wt.shape
    rb = _tile(rows, 256, 8)

    def fn(c, i, wv, gv, mv, vv):
        return _adam_math(wv, gv, mv, vv)

    return _rowcall(fn, [_rows(t, rb) for t in (wt, g, m, v)], [_orow(rows, cols, F32, rb)] * 3,
                    n_rows=rows, rb=rb, name=name)


def _pad_rows(t, rows):
    flat = t.reshape(-1)
    return jnp.pad(flat, (0, rows * 128 - flat.shape[0])).reshape(rows, 128)


SP_LAYOUT = (("ada_b", 48), ("norm1_w", 8), ("gla_gate_b", 8), ("gla_norm_w", 8), ("norm2_w", 8), ("conv_b", 48),
             ("final_norm_w", 8), (None, 8), ("gla_gate_w2", 16), ("conv_w", 40))


def _pack_small(d):
    return jnp.concatenate([jnp.zeros((rows, 128), F32) if n is None else _pad_rows(d[n].astype(F32), rows)
                            for n, rows in SP_LAYOUT], axis=0)


def _unpack_small(pk, shapes):
    out, off = {}, 0
    for n, rows in SP_LAYOUT:
        if n is not None:
            shp = shapes[n]
            out[n] = pk[off:off + rows].reshape(-1)[:math.prod(shp)].reshape(shp)
        off += rows
    return out


def kernel(x, c, positions, ada_w, ada_b, norm1_w, w_in, gla_gate_w2, gla_gate_b, gla_norm_w, w_gla_branch, w_attn_branch, w_out, norm2_w, w_up, conv_w, conv_b, w_down, final_norm_w, loss_target, m_ada_w, m_ada_b, m_norm1_w, m_w_in, m_gla_gate_w2, m_gla_gate_b, m_gla_norm_w, m_w_gla_branch, m_w_attn_branch, m_w_out, m_norm2_w, m_w_up, m_conv_w, m_conv_b, m_w_down, m_final_norm_w, v_ada_w, v_ada_b, v_norm1_w, v_w_in, v_gla_gate_w2, v_gla_gate_b, v_gla_norm_w, v_w_gla_branch, v_w_attn_branch, v_w_out, v_norm2_w, v_w_up, v_conv_w, v_conv_b, v_w_down, v_final_norm_w):
    s = x.shape[1]
    names = ("ada_w", "ada_b", "norm1_w", "w_in", "gla_gate_w2", "gla_gate_b", "gla_norm_w", "w_gla_branch", "w_attn_branch",
             "w_out", "norm2_w", "w_up", "conv_w", "conv_b", "w_down", "final_norm_w")
    wts = dict(zip(names, (ada_w, ada_b, norm1_w, w_in, gla_gate_w2, gla_gate_b, gla_norm_w, w_gla_branch, w_attn_branch,
                           w_out, norm2_w, w_up, conv_w, conv_b, w_down, final_norm_w)))
    ms = dict(zip(names, (m_ada_w, m_ada_b, m_norm1_w, m_w_in, m_gla_gate_w2, m_gla_gate_b, m_gla_norm_w, m_w_gla_branch,
                          m_w_attn_branch, m_w_out, m_norm2_w, m_w_up, m_conv_w, m_conv_b, m_w_down, m_final_norm_w)))
    vs = dict(zip(names, (v_ada_w, v_ada_b, v_norm1_w, v_w_in, v_gla_gate_w2, v_gla_gate_b, v_gla_norm_w, v_w_gla_branch,
                          v_w_attn_branch, v_w_out, v_norm2_w, v_w_up, v_conv_w, v_conv_b, v_w_down, v_final_norm_w)))

    pk0 = jnp.concatenate([_pad_rows(c, 8), _pad_rows(gla_gate_w2, 16), _pad_rows(conv_w, 40)], axis=0)
    sm_all = _allgather8(pk0, "gather_small")
    c_all = sm_all[:, 0:8, :].reshape(8, D)
    w2_full = sm_all[0::2, 8:24, :].transpose(1, 0, 2).reshape(GLA_LR, 512)
    cw_full = sm_all[0::2, 24:64, :].reshape(4, 40 * 128)[:, :3 * W_UP_SH].reshape(4, 3, W_UP_SH).transpose(1, 0, 2).reshape(3, 2 * D_FF)

    mod_sh = _mod_shard(c_all, ada_w[0])
    mod_all = _allgather8(mod_sh.reshape(96, 128), "gather_mod")

    w_sh = [wts[n].astype(BF16) for n in BIG]
    u_g0 = _u_gather_ici(w_sh, (0,))
    g0 = (u_g0,) + _unit_start(u_g0, "gather_w_in_start", after=[mod_all])
    mod = _mod_select(mod_all.reshape(8, 8, 12, 128) + g0[4][0, 0], ada_b.reshape(4, 12, 128)).reshape(6, D)

    core = lax.axis_index("c").astype(jnp.int32).reshape(1)
    chip = (2 * lax.axis_index("x") + lax.axis_index("y")).astype(jnp.int32)
    sm = dict(n1w=norm1_w, n2w=norm2_w, fnw=final_norm_w.reshape(1, D), gnw=gla_norm_w, gb=gla_gate_b,
              w2=jnp.pad(w2_full, ((0, 128 - GLA_LR), (0, 0))), cw=_ff_to_kernel(cw_full), cb=_ff_to_kernel(conv_b))
    loss, grad_x, halves, others, small, ts0 = _local_step(x[0], mod, positions.reshape(s, 1), loss_target[0], sm, w_sh,
                                                               g0, chip, core)

    dcw = _ff_from_kernel(small["cw"]).reshape(3, 4, W_UP_SH).transpose(1, 0, 2)
    dw2 = small["w2"][:GLA_LR].reshape(GLA_LR, 4, 128).transpose(1, 0, 2)
    sg = jnp.concatenate(
        [_pad_rows(small["dmod"], 48), _pad_rows(small["n1w"], 8), _pad_rows(small["gb"], 8), _pad_rows(small["gnw"], 8),
         _pad_rows(small["n2w"], 8), _pad_rows(_ff_from_kernel(small["cb"]), 48), _pad_rows(small["fnw"], 8), _pad_rows(loss, 8)]
        + [_pad_rows(dw2[k], 16) for k in range(4)] + [_pad_rows(dcw[k], 40) for k in range(4)], axis=0)
    sg_all = _allgather8(sg, "gather_small_grads")
    u_ex = _u_chip_exchange([ts0])
    pending = (u_ex,) + _unit_start(u_ex, "grad_exchange_w_in_start", after=[sg_all])
    sg_all = sg_all + pending[4][0, 0]
    g_small_pk = _small_reduce(sg_all)
    dmod_all = sg_all[:, 0:48, :].reshape(8, 4, 12, 128).transpose(1, 2, 0, 3)
    g_ada_w = _ada_grad(dmod_all, jnp.broadcast_to(c_all[:, :, None], (8, D, 128)))

    shapes = {n: wts[n].shape for n in names}
    g_small = _unpack_small(g_small_pk, shapes)
    grads = {"ada_w": g_ada_w.reshape(1, D, 1536), **g_small}
    deltas, new_m, new_v = {}, {}, {}
    for n, mine, theirs in zip(BIG[1:], halves, others):
        grads[n], deltas[n], new_m[n], new_v[n] = _adamw_halves(wts[n], ms[n], vs[n], mine, theirs, core, "adamw_" + n)
    shp = ada_w.shape
    d_, m_, v_ = _adamw(ada_w[0], g_ada_w, m_ada_w[0], v_ada_w[0], "adamw_ada_w")
    deltas["ada_w"], new_m["ada_w"], new_v["ada_w"] = d_.reshape(shp), m_.reshape(shp), v_.reshape(shp)
    d_, m_, v_ = _adamw(_pack_small(wts), g_small_pk, _pack_small(ms), _pack_small(vs), "adamw_small")
    for dst, pk in ((deltas, d_), (new_m, m_), (new_v, v_)):
        dst.update(_unpack_small(pk, shapes))

    [t0], [r0] = _unit_wait(*pending[:4], after=[d_, deltas["ada_w"], deltas["w_up"], deltas["w_down"]], name="grad_exchange_w_in_wait")
    half0 = _chip_sum(t0, r0, chip.reshape(1), "grad_chip_sum_w_in")
    [[oth0]] = _comm_call("grad_join_w_in", [_u_pair_join([half0])])
    grads["w_in"], deltas["w_in"], new_m["w_in"], new_v["w_in"] = _adamw_halves(w_in, m_w_in, v_w_in, half0, oth0, core, "adamw_w_in")

    return (g_small_pk[SG_LOSS, 0], grad_x.reshape(1, s, D), *[grads[n] for n in names], *[deltas[n] for n in names],
            *[new_m[n] for n in names], *[new_v[n] for n in names])
```

```python
import math

import jax
import jax.numpy as jnp
from jax import lax
from jax.experimental import pallas as pl
from jax.experimental.pallas import tpu as pltpu

F32, BF16 = jnp.float32, jnp.bfloat16
MESH = pl.DeviceIdType.MESH

D = 1024
EPS = 1e-6
GLA_H, GLA_DK, GLA_DV, GLA_LR = 4, 128, 256, 16
GLA_TAU = 16.0
GLA_CHUNK = 64
GLA_BLOCK = 1024
ATT_GROUPS = ((128, 1), (512, 4), (2048, 16))
ATT_BLK = 128
ATT_HD = 64
ATT_W = 768
D_FF = 2816
ROPE_THETA = 10000.0
P_W = 7680
P_GV, P_GR, P_MA, P_MB, P_GQ, P_GK, P_AQ, P_AK, P_AV, P_LR = 0, 1024, 2048, 3072, 4096, 4608, 5120, 5888, 6656, 7424
W_IN = 7440
W_IN_SH, W_UP_SH, W_DOWN_SH = 1860, 1408, 704
VMEM_LIMIT = 56 * 1024 * 1024
ADAM_LR, ADAM_B1, ADAM_B2, ADAM_EPS, ADAM_WD, ADAM_STEP = 0.001, 0.9, 0.999, 1e-08, 0.01, 10
NEG = -1e30


def _tile(n, target, unit=128):
    best = None
    for t in range(unit, min(n, target) + 1, unit):
        if n % t == 0:
            best = t
    return best or n


def _params(sem):
    return pltpu.CompilerParams(dimension_semantics=sem, vmem_limit_bytes=VMEM_LIMIT)


def _dg(a, b, ca, cb):
    return lax.dot_general(a, b, (((ca,), (cb,)), ((), ())), preferred_element_type=F32)


def _sigmoid(v):
    return 1.0 / (1.0 + jnp.exp(-v))


def _ff_block(j):
    return (j % 2) * 2 + j // 2


def _mm(a, b, name, *, ta=False, tb=False, out_dtype=BF16, tm=1024, tn=1536, tk=1024, n_outer=True, comm=(),
        b_shards=False, o_shards=False):
    m = a.shape[1] if ta else a.shape[0]
    k = a.shape[0] if ta else a.shape[1]
    if b_shards:
        n = b.shape[1] if tb else 4 * W_UP_SH
        tn, tk = (tn, W_UP_SH) if tb else (W_UP_SH, tk)
    else:
        n = b.shape[0] if tb else b.shape[1]
    if o_shards:
        tn = W_UP_SH
    tm, tn, tk = _tile(m, tm), _tile(n, tn), _tile(k, tk)
    nm, nn, nk = m // tm, n // tn, k // tk
    in_out = out_dtype == F32
    c_ins, c_outs, c_alias, c_scratch = _carry(comm, 2, 1)

    def body(a_ref, b_ref, *rest):
        ci, o_ref, co = rest[:len(c_ins)], rest[len(c_ins)], rest[len(c_ins) + 1:len(c_ins) + 1 + len(c_outs)]
        scr = rest[len(c_ins) + 1 + len(c_outs):]
        kk = pl.program_id(2)
        if comm:
            step = (pl.program_id(0) * (nm if n_outer else nn) + pl.program_id(1)) * nk + kk

            @pl.when(step == 0)
            def _():
                _comm_phase(comm, ci, co, scr[-2], scr[-1], True)

        _mm_step(a_ref, b_ref, o_ref, scr, kk)
        if comm:
            @pl.when(step == nm * nn * nk - 1)
            def _():
                _comm_phase(comm, ci, co, scr[-2], scr[-1], False)

    def _mm_step(a_ref, b_ref, o_ref, scr, kk):
        p = _dg(a_ref[...].astype(BF16), b_ref[...].astype(BF16), 0 if ta else 1, 1 if tb else 0)
        if nk == 1:
            o_ref[...] = p.astype(o_ref.dtype)
        else:
            acc = o_ref if in_out else scr[0]

            @pl.when(kk == 0)
            def _():
                acc[...] = p

            @pl.when(kk > 0)
            def _():
                acc[...] += p

            if not in_out:
                @pl.when(kk == nk - 1)
                def _():
                    o_ref[...] = acc[...].astype(o_ref.dtype)

    if n_outer:
        ij = lambda g0, g1: (g1, g0)
        grid = (nn, nm, nk)
    else:
        ij = lambda g0, g1: (g0, g1)
        grid = (nm, nn, nk)
    a_map = (lambda g0, g1, kk: (kk, ij(g0, g1)[0])) if ta else (lambda g0, g1, kk: (ij(g0, g1)[0], kk))
    if b_shards and tb:
        b_spec = pl.BlockSpec((None, tn, tk), lambda g0, g1, kk: (_ff_block(kk), ij(g0, g1)[1], 0))
    elif b_shards:
        b_spec = pl.BlockSpec((None, tk, tn), lambda g0, g1, kk: (_ff_block(ij(g0, g1)[1]), kk, 0))
    elif tb:
        b_spec = pl.BlockSpec((tn, tk), lambda g0, g1, kk: (ij(g0, g1)[1], kk))
    else:
        b_spec = pl.BlockSpec((tk, tn), lambda g0, g1, kk: (kk, ij(g0, g1)[1]))
    if o_shards:
        o_spec = pl.BlockSpec((None, tm, tn), lambda g0, g1, kk: (_ff_block(ij(g0, g1)[1]), ij(g0, g1)[0], 0))
        o_shape = jax.ShapeDtypeStruct((4, m, W_UP_SH), out_dtype)
    else:
        o_spec = pl.BlockSpec((tm, tn), lambda g0, g1, kk: ij(g0, g1))
        o_shape = jax.ShapeDtypeStruct((m, n), out_dtype)
    res = pl.pallas_call(
        body, name=name, grid=grid,
        in_specs=[pl.BlockSpec((tk, tm) if ta else (tm, tk), a_map), b_spec] + [HBM] * len(c_ins),
        out_specs=[o_spec] + [HBM] * len(c_outs),
        out_shape=[o_shape] + c_outs,
        scratch_shapes=([] if (in_out or nk == 1) else [pltpu.VMEM((tm, tn), F32)]) + c_scratch,
        input_output_aliases=c_alias,
        compiler_params=_params(("arbitrary",) * 3 if comm else ("parallel", "parallel", "arbitrary")),
    )(a, b, *c_ins)
    return (res[0], _split_units(comm, res[1:])) if comm else res[0]


def _rows(arr, rb, w=None, j=0):
    w = arr.shape[1] if w is None else w
    if callable(j):
        return arr, pl.BlockSpec((rb, w), lambda c, i: (i, j(c)))
    return arr, pl.BlockSpec((rb, w), lambda c, i: (i, j))


def _full(arr, w=None, j=0):
    w = arr.shape[1] if w is None else w
    if callable(j):
        return arr, pl.BlockSpec((arr.shape[0], w), lambda c, i: (0, j(c)))
    return arr, pl.BlockSpec((arr.shape[0], w), lambda c, i: (0, j))


def _halo(arr, rb, hb, w, j, before):
    per = rb // hb
    last = arr.shape[0] // hb - 1
    if before:
        rmap = lambda i: jnp.maximum(i * per - 1, 0)
    else:
        rmap = lambda i: jnp.minimum((i + 1) * per, last)
    return arr, pl.BlockSpec((hb, w), lambda c, i: (rmap(i), j(c) if callable(j) else j))


def _rowcall(fn, ins, outs, *, n_rows, rb, name, ncol=1, into=None, after=()):
    n_in = len(ins)
    nr = n_rows // rb
    unread = ([] if into is None else [into[0]]) + list(after)
    n_skip = len(unread)

    def body(*refs):
        c, i = pl.program_id(0), pl.program_id(1)
        res = fn(c, i, *[r[...] for r in refs[:n_in]])
        for val, spec, o_ref in zip(res, outs, refs[n_in + n_skip:]):
            if spec[2] == "row":
                o_ref[...] = val.astype(o_ref.dtype)
            else:
                @pl.when(i == 0)
                def _(o_ref=o_ref, val=val):
                    o_ref[...] = val.astype(o_ref.dtype)

                @pl.when(i > 0)
                def _(o_ref=o_ref, val=val):
                    o_ref[...] += val.astype(o_ref.dtype)

    out_specs = []
    for shape, dt, kind, block, col in outs:
        if kind == "row":
            out_specs.append(pl.BlockSpec(block, lambda c, i, col=col: (i, col(c))))
        else:
            out_specs.append(pl.BlockSpec(block, lambda c, i, col=col: (0, col(c))))
    return pl.pallas_call(
        body, name=name, grid=(ncol, nr),
        in_specs=[s for _, s in ins] + [pl.BlockSpec(memory_space=pl.ANY)] * n_skip, out_specs=out_specs,
        out_shape=[jax.ShapeDtypeStruct(o[0], o[1]) for o in outs],
        input_output_aliases={} if into is None else {n_in: into[1]},
        compiler_params=_params(("parallel", "arbitrary")),
    )(*[a for a, _ in ins], *unread)


def _orow(n_rows, w, dt, rb, bw=None, col=lambda c: 0):
    return ((n_rows, w), dt, "row", (rb, bw or w), col)


def _oacc(r, w, bw=None, col=lambda c: 0):
    return ((r, w), F32, "acc", (r, bw or w), col)


def _csum(v):
    return jnp.sum(v, axis=0, keepdims=True)


def _rms(v):
    return lax.rsqrt(jnp.mean(v * v, axis=-1, keepdims=True) + EPS)


def _norm_bwd(xv, dh, w, scale):
    r = _rms(xv)
    xh = xv * r
    dxh = dh * (w * (1.0 + scale))
    dx = r * (dxh - xh * jnp.mean(dxh * xh, axis=-1, keepdims=True))
    t = dh * xh
    return dx, _csum(dh), _csum(t * w), _csum(t * (1.0 + scale))


def _rope_tables(pos_col, invf, s):
    def fn(c, i, pos, f):
        ang = pos.astype(F32) * f
        lane = lax.broadcasted_iota(jnp.int32, ang.shape, 1)
        sign = jnp.where((lane % ATT_HD) < ATT_HD // 2, -1.0, 1.0)
        return jnp.cos(ang), jnp.sin(ang) * sign

    rb = 512
    return _rowcall(fn, [_rows(pos_col, rb), _full(invf)], [_orow(s, 128, F32, rb), _orow(s, 128, F32, rb)],
                    n_rows=s, rb=rb, name="rope_tables")


def _swap_halves(t):
    n = t.shape[1]
    lane = lax.broadcasted_iota(jnp.int32, t.shape, 1)
    return jnp.where((lane % ATT_HD) < ATT_HD // 2, pltpu.roll(t, n - 32, 1), pltpu.roll(t, 32, 1))


def _rope_apply(t, cos, sin_signed, inverse):
    cw = jnp.concatenate([cos] * (t.shape[1] // 128), axis=1)
    sw = jnp.concatenate([sin_signed] * (t.shape[1] // 128), axis=1)
    if inverse:
        sw = -sw
    return t * cw + _swap_halves(t) * sw


DIL_ROWS = 512


def _to_dilated(scr, val, out_ref, r):
    if r == 1:
        out_ref[...] = val.astype(out_ref.dtype)
        return
    n = val.shape[0] // r
    for hh in range(2):
        scr[hh] = val[:, hh * 128:(hh + 1) * 128]
        for pr in range(r):
            out_ref[:, pr * 256 + hh * 128:pr * 256 + (hh + 1) * 128] = scr[hh, pl.ds(pr, n, stride=r), :].astype(out_ref.dtype)


def _from_dilated(scr, in_ref, r):
    if r == 1:
        return in_ref[...].astype(F32)
    n = in_ref.shape[0]
    for hh in range(2):
        for pr in range(r):
            scr[hh, pl.ds(pr, n, stride=r), :] = in_ref[:, pr * 256 + hh * 128:pr * 256 + (hh + 1) * 128].astype(F32)
    return jnp.concatenate([scr[0], scr[1]], axis=1)


def _dil_spec(r):
    return pl.BlockSpec((DIL_ROWS // r, r * 256), lambda i: (i, 0))


def _dil_shape(s, r, dt):
    return jax.ShapeDtypeStruct((s // r, r * 256), dt)


_DIL_SCRATCH = [pltpu.VMEM((2, DIL_ROWS, 128), F32)]
_RS = tuple(r for _, r in ATT_GROUPS)


def _rope_fwd(p, cos_t, sin_t, s):
    def body(*refs):
        ins, cs, sn, outs, scr = refs[:9], refs[9][...], refs[10][...], refs[11:20], refs[20]
        for t in range(3):
            for g, r in enumerate(_RS):
                val = ins[3 * t + g][...].astype(F32)
                _to_dilated(scr, _rope_apply(val, cs, sn, False) if t < 2 else val, outs[3 * t + g], r)

    res = pl.pallas_call(
        body, name="rope", grid=(s // DIL_ROWS,),
        in_specs=[pl.BlockSpec((DIL_ROWS, 256), lambda i, c=base // 256 + g: (i, c)) for base in (P_AQ, P_AK, P_AV) for g in range(3)]
        + [pl.BlockSpec((DIL_ROWS, 128), lambda i: (i, 0))] * 2,
        out_specs=[_dil_spec(r) for _ in range(3) for r in _RS],
        out_shape=[_dil_shape(s, r, BF16) for _ in range(3) for r in _RS],
        scratch_shapes=_DIL_SCRATCH, compiler_params=_params(("parallel",)),
    )(*([p] * 9), cos_t, sin_t)
    return res[0:3], res[3:6], res[6:9]


def _attn_combine(att, s):
    def body(o0, o1, o2, l0, l1, l2, o_ref, lse_ref, od1, od2, ld1, ld2, scr):
        ov = [_from_dilated(scr, ref, r) for ref, r in zip((o0, o1, o2), _RS)]
        lv = [_from_dilated(scr, ref, r) for ref, r in zip((l0, l1, l2), _RS)]
        mx = jnp.maximum(jnp.maximum(lv[0], lv[1]), lv[2])
        ev = [jnp.exp(l - mx) for l in lv]
        z = ev[0] + ev[1] + ev[2]
        o = ((ev[0] * ov[0] + ev[1] * ov[1] + ev[2] * ov[2]) / z).astype(BF16)
        lse = mx + jnp.log(z)
        o_ref[...] = o
        lse_ref[...] = lse
        for ref, r in zip((od1, od2), _RS[1:]):
            _to_dilated(scr, o.astype(F32), ref, r)
        for ref, r in zip((ld1, ld2), _RS[1:]):
            _to_dilated(scr, lse, ref, r)

    return pl.pallas_call(
        body, name="attn_combine", grid=(s // DIL_ROWS,),
        in_specs=[_dil_spec(r) for r in _RS] * 2,
        out_specs=[_dil_spec(1)] * 2 + [_dil_spec(r) for r in _RS[1:]] * 2,
        out_shape=[_dil_shape(s, 1, BF16), _dil_shape(s, 1, F32)] + [_dil_shape(s, r, BF16) for r in _RS[1:]]
        + [_dil_shape(s, r, F32) for r in _RS[1:]],
        scratch_shapes=_DIL_SCRATCH, compiler_params=_params(("parallel",)),
    )(*[a[0] for a in att], *[a[1] for a in att])


def _dilate(t, s):
    def body(t_ref, o1, o2, scr):
        val = t_ref[...].astype(F32)
        for ref, r in zip((o1, o2), _RS[1:]):
            _to_dilated(scr, val, ref, r)

    return pl.pallas_call(
        body, name="attn_dilate", grid=(s // DIL_ROWS,), in_specs=[_dil_spec(1)], out_specs=[_dil_spec(r) for r in _RS[1:]],
        out_shape=[_dil_shape(s, r, t.dtype) for r in _RS[1:]], scratch_shapes=_DIL_SCRATCH, compiler_params=_params(("parallel",)),
    )(t)


def _rope_bwd(datt, d_glr, dp, cos_t, sin_t, s):
    tail = P_W - P_AQ

    def body(*refs):
        ins, cs, sn, glr, o_ref, scr = refs[:9], refs[9][...], refs[10][...], refs[11], refs[13], refs[14]
        for t in range(3):
            for g, r in enumerate(_RS):
                val = _from_dilated(scr, ins[3 * t + g], r)
                o_ref[:, t * ATT_W + g * 256:t * ATT_W + (g + 1) * 256] = (_rope_apply(val, cs, sn, True) if t < 2 else val).astype(BF16)
        o_ref[:, 3 * ATT_W:3 * ATT_W + 128] = glr[...]
        o_ref[:, 3 * ATT_W + 128:] = jnp.zeros((DIL_ROWS, tail - 3 * ATT_W - 128), BF16)

    return pl.pallas_call(
        body, name="rope_bwd", grid=(s // DIL_ROWS,),
        in_specs=[_dil_spec(r) for _ in range(3) for r in _RS] + [pl.BlockSpec((DIL_ROWS, 128), lambda i: (i, 0))] * 3
        + [pl.BlockSpec(memory_space=pl.ANY)],
        out_specs=pl.BlockSpec((DIL_ROWS, tail), lambda i: (i, P_AQ // tail)),
        out_shape=jax.ShapeDtypeStruct((s, P_W), BF16), input_output_aliases={12: 0},
        scratch_shapes=_DIL_SCRATCH, compiler_params=_params(("parallel",)),
    )(*[datt[g][t] for t in range(3) for g in range(3)], cos_t, sin_t, d_glr, dp)


def _tri_dot(tri, t):
    tb = tri.astype(BF16)
    hi = t.astype(BF16)
    r1 = t - hi.astype(F32)
    mid = r1.astype(BF16)
    lo = (r1 - mid.astype(F32)).astype(BF16)
    return _dg(tb, hi, 1, 0) + _dg(tb, mid, 1, 0) + _dg(tb, lo, 1, 0)


def _gla_decays(la_c, tri):
    b = _tri_dot(tri, la_c)
    row = lax.broadcasted_iota(jnp.int32, b.shape, 0)
    bmid = jnp.sum(jnp.where(row == GLA_CHUNK // 2 - 1, b, 0.0), axis=0, keepdims=True)
    blast = jnp.sum(jnp.where(row == GLA_CHUNK - 1, b, 0.0), axis=0, keepdims=True)
    return b, bmid, blast


def _gla_fwd(p, la, s, comm=()):
    tb, ch = GLA_BLOCK, GLA_CHUNK
    nb, nc = s // tb, tb // ch
    scale = GLA_DK ** -0.5
    c_ins, c_outs, c_alias, c_scratch = _carry(comm, 4, 2)

    def body(q_ref, k_ref, v_ref, la_ref, *rest):
        ci, (o_ref, st_ref) = rest[:len(c_ins)], rest[len(c_ins):len(c_ins) + 2]
        co, state = rest[len(c_ins) + 2:len(c_ins) + 2 + len(c_outs)], rest[len(c_ins) + 2 + len(c_outs)]
        step = pl.program_id(0)
        if comm:
            @pl.when(step == 0)
            def _():
                _comm_phase(comm, ci, co, rest[-2], rest[-1], True)

        _gla_fwd_step(q_ref, k_ref, v_ref, la_ref, o_ref, st_ref, state)
        if comm:
            @pl.when(step == nb - 1)
            def _():
                _comm_phase(comm, ci, co, rest[-2], rest[-1], False)

    def _gla_fwd_step(q_ref, k_ref, v_ref, la_ref, o_ref, st_ref, state):
        @pl.when(pl.program_id(0) == 0)
        def _():
            state[...] = jnp.zeros_like(state)

        ri = lax.broadcasted_iota(jnp.int32, (ch, ch), 0)
        ci = lax.broadcasted_iota(jnp.int32, (ch, ch), 1)
        causal = ci <= ri
        tri = causal.astype(F32)

        def chunk(c, carry):
            sl = pl.ds(pl.multiple_of(c * ch, ch), ch)
            b, bmid, blast = _gla_decays(la_ref[sl, :], tri)
            q = q_ref[sl, :].astype(F32) * scale
            k = k_ref[sl, :].astype(F32)
            v = v_ref[sl, :]
            qgt = (q * jnp.exp(b)).astype(BF16)
            qgn = (q * jnp.exp(b - bmid)).astype(BF16)
            kgn = (k * jnp.exp(bmid - b)).astype(BF16)
            kd = (k * jnp.exp(blast - b)).astype(BF16)
            dec = jnp.exp(blast)
            sts = [state[h] for h in range(GLA_H)]
            outs, news = [], []
            for h in range(GLA_H):
                hk, hv = slice(h * GLA_DK, (h + 1) * GLA_DK), slice(h * GLA_DV, (h + 1) * GLA_DV)
                a = jnp.where(causal, _dg(qgn[:, hk], kgn[:, hk], 1, 1), 0.0)
                outs.append(_dg(a.astype(BF16), v[:, hv], 1, 0) + _dg(qgt[:, hk], sts[h].astype(BF16), 1, 1))
                news.append(dec[:, hk] * sts[h] + _dg(v[:, hv], kd[:, hk], 0, 0))
            for h in range(GLA_H):
                st_ref[h, c] = sts[h]
                state[h] = news[h]
            o_ref[sl, :] = jnp.concatenate(outs, axis=1)
            return carry

        lax.fori_loop(0, nc, chunk, 0, unroll=2)

    hw = GLA_H * GLA_DK
    res = pl.pallas_call(
        body, name="gla_fwd", grid=(nb,),
        in_specs=[pl.BlockSpec((tb, hw), lambda t: (t, P_GQ // hw)),
                  pl.BlockSpec((tb, hw), lambda t: (t, P_GK // hw)),
                  pl.BlockSpec((tb, GLA_H * GLA_DV), lambda t: (t, P_GV // (GLA_H * GLA_DV))),
                  pl.BlockSpec((tb, hw), lambda t: (t, 0))] + [HBM] * len(c_ins),
        out_specs=[pl.BlockSpec((tb, GLA_H * GLA_DV), lambda t: (t, 0)),
                   pl.BlockSpec((GLA_H, nc, GLA_DV, GLA_DK), lambda t: (0, t, 0, 0))] + [HBM] * len(c_outs),
        out_shape=[jax.ShapeDtypeStruct((s, GLA_H * GLA_DV), F32),
                   jax.ShapeDtypeStruct((GLA_H, s // ch, GLA_DV, GLA_DK), F32)] + c_outs,
        scratch_shapes=[pltpu.VMEM((GLA_H, GLA_DV, GLA_DK), F32)] + c_scratch,
        input_output_aliases=c_alias,
        compiler_params=_params(("arbitrary",)),
    )(p, p, p, la, *c_ins)
    return res[0], res[1], _split_units(comm, res[2:])


def _gla_bwd(p, la, states, do, s, dp, comm=()):
    tb, ch = GLA_BLOCK, GLA_CHUNK
    nb, nc = s // tb, tb // ch
    scale = GLA_DK ** -0.5
    c_ins, c_outs, c_alias, c_scratch = _carry(comm, 7, 4)

    def body(q_ref, k_ref, v_ref, la_ref, st_ref, do_ref, dp_in, *rest):
        ci, outs = rest[:len(c_ins)], rest[len(c_ins):len(c_ins) + 4]
        co, dstate = rest[len(c_ins) + 4:len(c_ins) + 4 + len(c_outs)], rest[len(c_ins) + 4 + len(c_outs)]
        step = pl.program_id(0)
        if comm:
            @pl.when(step == 0)
            def _():
                _comm_phase(comm, ci, co, rest[-2], rest[-1], True)

        _gla_bwd_step(q_ref, k_ref, v_ref, la_ref, st_ref, do_ref, *outs, dstate)
        if comm:
            @pl.when(step == nb - 1)
            def _():
                _comm_phase(comm, ci, co, rest[-2], rest[-1], False)

    def _gla_bwd_step(q_ref, k_ref, v_ref, la_ref, st_ref, do_ref, dq_ref, dk_ref, dv_ref, dla_ref, dstate):
        @pl.when(pl.program_id(0) == 0)
        def _():
            dstate[...] = jnp.zeros_like(dstate)

        ri = lax.broadcasted_iota(jnp.int32, (ch, ch), 0)
        ci = lax.broadcasted_iota(jnp.int32, (ch, ch), 1)
        causal = ci <= ri
        tri = causal.astype(F32)
        tri_t = (ci >= ri).astype(F32)

        def chunk(cc, carry):
            c = nc - 1 - cc
            sl = pl.ds(pl.multiple_of(c * ch, ch), ch)
            b, bmid, blast = _gla_decays(la_ref[sl, :], tri)
            q = q_ref[sl, :].astype(F32) * scale
            k = k_ref[sl, :].astype(F32)
            v = v_ref[sl, :]
            e_b, e_qn, e_kn, e_kd = jnp.exp(b), jnp.exp(b - bmid), jnp.exp(bmid - b), jnp.exp(blast - b)
            dec = jnp.exp(blast)
            qgt, qgn, kgn, kd = q * e_b, q * e_qn, k * e_kn, k * e_kd
            qgt_b, qgn_b, kgn_b, kd_b = qgt.astype(BF16), qgn.astype(BF16), kgn.astype(BF16), kd.astype(BF16)
            do_b = do_ref[sl, :].astype(BF16)
            st0s = [st_ref[h, c] for h in range(GLA_H)]
            dsts = [dstate[h] for h in range(GLA_H)]
            dqgn, dqgt, dkgn, dkd, dvs, ddec, news = [], [], [], [], [], [], []
            for h in range(GLA_H):
                hk, hv = slice(h * GLA_DK, (h + 1) * GLA_DK), slice(h * GLA_DV, (h + 1) * GLA_DV)
                dst_b = dsts[h].astype(BF16)
                a = jnp.where(causal, _dg(qgn_b[:, hk], kgn_b[:, hk], 1, 1), 0.0).astype(BF16)
                da = jnp.where(causal, _dg(do_b[:, hv], v[:, hv], 1, 1), 0.0).astype(BF16)
                dqgn.append(_dg(da, kgn_b[:, hk], 1, 0))
                dqgt.append(_dg(do_b[:, hv], st0s[h].astype(BF16), 1, 0))
                dkgn.append(_dg(da, qgn_b[:, hk], 0, 0))
                dvs.append(_dg(a, do_b[:, hv], 0, 0) + _dg(kd_b[:, hk], dst_b, 1, 1))
                dkd.append(_dg(v[:, hv], dst_b, 1, 0))
                ddec.append(jnp.sum(st0s[h] * dsts[h], axis=0, keepdims=True))
                news.append(dec[:, hk] * dsts[h] + _dg(do_b[:, hv], qgt_b[:, hk], 0, 0))
            for h in range(GLA_H):
                dstate[h] = news[h]
            cat = lambda parts: jnp.concatenate(parts, axis=1)
            dqgn, dqgt, dkgn, dkd, ddec = cat(dqgn), cat(dqgt), cat(dkgn), cat(dkd), cat(ddec)
            dq_ref[sl, :] = (scale * (dqgn * e_qn + dqgt * e_b)).astype(dq_ref.dtype)
            dk_ref[sl, :] = (dkgn * e_kn + dkd * e_kd).astype(dk_ref.dtype)
            dv_ref[sl, :] = cat(dvs).astype(dv_ref.dtype)
            db = dqgn * qgn + dqgt * qgt - dkgn * kgn - dkd * kd
            extra = jnp.sum(dkd * kd, axis=0, keepdims=True) + ddec * dec
            dla_ref[sl, :] = _tri_dot(tri_t, db) + extra
            return carry

        lax.fori_loop(0, nc, chunk, 0, unroll=2)

    rev = lambda t: nb - 1 - t
    hw, vw = GLA_H * GLA_DK, GLA_H * GLA_DV
    res = pl.pallas_call(
        body, name="gla_bwd", grid=(nb,),
        in_specs=[pl.BlockSpec((tb, hw), lambda t: (rev(t), P_GQ // hw)),
                  pl.BlockSpec((tb, hw), lambda t: (rev(t), P_GK // hw)),
                  pl.BlockSpec((tb, vw), lambda t: (rev(t), P_GV // vw)),
                  pl.BlockSpec((tb, hw), lambda t: (rev(t), 0)),
                  pl.BlockSpec((GLA_H, nc, GLA_DV, GLA_DK), lambda t: (0, rev(t), 0, 0)),
                  pl.BlockSpec((tb, vw), lambda t: (rev(t), 0)), pl.BlockSpec(memory_space=pl.ANY)] + [HBM] * len(c_ins),
        out_specs=[pl.BlockSpec((tb, hw), lambda t: (rev(t), 0)),
                   pl.BlockSpec((tb, hw), lambda t: (rev(t), 0)),
                   pl.BlockSpec((tb, vw), lambda t: (rev(t), P_GV // vw)),
                   pl.BlockSpec((tb, hw), lambda t: (rev(t), 0))] + [HBM] * len(c_outs),
        out_shape=[jax.ShapeDtypeStruct((s, hw), BF16),
                   jax.ShapeDtypeStruct((s, hw), BF16),
                   jax.ShapeDtypeStruct((s, P_W), BF16),
                   jax.ShapeDtypeStruct((s, hw), F32)] + c_outs,
        scratch_shapes=[pltpu.VMEM((GLA_H, GLA_DV, GLA_DK), F32)] + c_scratch,
        input_output_aliases={6: 2, **c_alias},
        compiler_params=_params(("arbitrary",)),
    )(p, p, p, la, states, do, dp, *c_ins)
    return res[0], res[1], res[2], res[3], _split_units(comm, res[4:])


def _head_masks():
    lane = lax.broadcasted_iota(jnp.int32, (1, 4 * ATT_HD), 1)
    return [(lane >= h * ATT_HD) & (lane < (h + 1) * ATT_HD) for h in range(4)]


def _attn_fwd(qv, kv, pv, g, r, s):
    ln = s // r
    nblk = ln // ATT_BLK
    qcol = lambda pr: pr
    vcol = qcol
    prev = lambda n: jnp.maximum(n - 1, 0)

    def body(q_ref, kp_ref, kc_ref, vp_ref, vc_ref, o_ref, lse_ref):
        has_prev = pl.program_id(1) > 0
        ri = lax.broadcasted_iota(jnp.int32, (ATT_BLK, ATT_BLK), 0)
        ci = lax.broadcasted_iota(jnp.int32, (ATT_BLK, ATT_BLK), 1)
        m_cur = ci <= ri
        m_prev = (ci >= ri) & has_prev
        q, kp, kc, vp, vc = q_ref[...], kp_ref[...], kc_ref[...], vp_ref[...], vc_ref[...]
        o = jnp.zeros((ATT_BLK, 256), F32)
        lse = jnp.zeros((ATT_BLK, 256), F32)
        for hm in _head_masks():
            qm = jnp.where(hm, q, jnp.zeros_like(q))
            sc = jnp.where(m_cur, _dg(qm, kc, 1, 1) * 0.125, NEG)
            sp = jnp.where(m_prev, _dg(qm, kp, 1, 1) * 0.125, NEG)
            mx = jnp.maximum(jnp.max(sc, axis=1, keepdims=True), jnp.max(sp, axis=1, keepdims=True))
            pc, pp = jnp.exp(sc - mx), jnp.exp(sp - mx)
            den = jnp.sum(pc, axis=1, keepdims=True) + jnp.sum(pp, axis=1, keepdims=True)
            oh = (_dg(pc.astype(BF16), vc, 1, 0) + _dg(pp.astype(BF16), vp, 1, 0)) / den
            o = jnp.where(hm, oh, o)
            lse = jnp.where(hm, mx + jnp.log(den), lse)
        o_ref[...] = o.astype(o_ref.dtype)
        lse_ref[...] = lse

    blk = (ATT_BLK, 256)
    o, lse = pl.pallas_call(
        body, name=f"attn_fwd_{g}", grid=(r, nblk),
        in_specs=[pl.BlockSpec(blk, lambda pr, n: (n, qcol(pr))),
                  pl.BlockSpec(blk, lambda pr, n: (prev(n), qcol(pr))),
                  pl.BlockSpec(blk, lambda pr, n: (n, qcol(pr))),
                  pl.BlockSpec(blk, lambda pr, n: (prev(n), vcol(pr))),
                  pl.BlockSpec(blk, lambda pr, n: (n, vcol(pr)))],
        out_specs=[pl.BlockSpec(blk, lambda pr, n: (n, pr)), pl.BlockSpec(blk, lambda pr, n: (n, pr))],
        out_shape=[jax.ShapeDtypeStruct((ln, r * 256), BF16), jax.ShapeDtypeStruct((ln, r * 256), F32)],
        compiler_params=_params(("parallel", "parallel")),
    )(qv, kv, kv, pv, pv)
    return o, lse


def _attn_bwd(qv, kv, pv, dov, ov, lv, g, r, s):
    ln = s // r
    nblk = ln // ATT_BLK
    qcol = lambda pr: pr
    vcol = qcol
    prev = lambda n: jnp.maximum(n - 1, 0)
    nxt = lambda n: jnp.minimum(n + 1, nblk - 1)

    def body(qc_ref, qn_ref, kp_ref, kc_ref, vp_ref, vc_ref, doc_ref, don_ref, oc_ref, on_ref, lc_ref, ln_ref,
             dq_ref, dk_ref, dv_ref):
        n = pl.program_id(1)
        has_prev, has_next = n > 0, n < nblk - 1
        ri = lax.broadcasted_iota(jnp.int32, (ATT_BLK, ATT_BLK), 0)
        ci = lax.broadcasted_iota(jnp.int32, (ATT_BLK, ATT_BLK), 1)
        m_cur = ci <= ri
        m_prev = (ci >= ri) & has_prev
        m_next = (ci >= ri) & has_next
        qc, qn, kp, kc, vp, vc = qc_ref[...], qn_ref[...], kp_ref[...], kc_ref[...], vp_ref[...], vc_ref[...]
        doc, don = doc_ref[...], don_ref[...]
        pc_full = doc.astype(F32) * oc_ref[...].astype(F32)
        pn_full = don.astype(F32) * on_ref[...].astype(F32)
        lc, lnx = lc_ref[...], ln_ref[...]
        dq = jnp.zeros((ATT_BLK, 256), F32)
        dk = jnp.zeros((ATT_BLK, 256), F32)
        dv = jnp.zeros((ATT_BLK, 256), F32)
        zb = jnp.zeros_like(qc)
        for hm in _head_masks():
            qcm, qnm = jnp.where(hm, qc, zb), jnp.where(hm, qn, zb)
            docm, donm = jnp.where(hm, doc, zb), jnp.where(hm, don, zb)
            lse_c = jnp.max(jnp.where(hm, lc, NEG), axis=1, keepdims=True)
            lse_n = jnp.max(jnp.where(hm, lnx, NEG), axis=1, keepdims=True)
            del_c = jnp.sum(jnp.where(hm, pc_full, 0.0), axis=1, keepdims=True)
            del_n = jnp.sum(jnp.where(hm, pn_full, 0.0), axis=1, keepdims=True)
            pr_ = jnp.where(m_cur, jnp.exp(_dg(qcm, kc, 1, 1) * 0.125 - lse_c), 0.0)
            ds = (pr_ * (_dg(docm, vc, 1, 1) - del_c) * 0.125).astype(BF16)
            dqh = _dg(ds, kc, 1, 0)
            dkh = _dg(ds, qc, 0, 0)
            dvh = _dg(pr_.astype(BF16), doc, 0, 0)
            pr_ = jnp.where(m_prev, jnp.exp(_dg(qcm, kp, 1, 1) * 0.125 - lse_c), 0.0)
            ds = (pr_ * (_dg(docm, vp, 1, 1) - del_c) * 0.125).astype(BF16)
            dqh = dqh + _dg(ds, kp, 1, 0)
            pr_ = jnp.where(m_next, jnp.exp(_dg(qnm, kc, 1, 1) * 0.125 - lse_n), 0.0)
            ds = (pr_ * (_dg(donm, vc, 1, 1) - del_n) * 0.125).astype(BF16)
            dkh = dkh + _dg(ds, qn, 0, 0)
            dvh = dvh + _dg(pr_.astype(BF16), don, 0, 0)
            dq = jnp.where(hm, dqh, dq)
            dk = jnp.where(hm, dkh, dk)
            dv = jnp.where(hm, dvh, dv)
        dq_ref[...] = dq.astype(dq_ref.dtype)
        dk_ref[...] = dk.astype(dk_ref.dtype)
        dv_ref[...] = dv.astype(dv_ref.dtype)

    blk = (ATT_BLK, 256)
    cur = lambda col: pl.BlockSpec(blk, lambda pr, n: (n, col(pr)))
    prv = lambda col: pl.BlockSpec(blk, lambda pr, n: (prev(n), col(pr)))
    nx = lambda col: pl.BlockSpec(blk, lambda pr, n: (nxt(n), col(pr)))
    own = lambda pr: pr
    outs = pl.pallas_call(
        body, name=f"attn_bwd_{g}", grid=(r, nblk),
        in_specs=[cur(qcol), nx(qcol), prv(qcol), cur(qcol), prv(vcol), cur(vcol),
                  cur(own), nx(own), cur(own), nx(own), cur(own), nx(own)],
        out_specs=[cur(own), cur(own), cur(own)],
        out_shape=[jax.ShapeDtypeStruct((ln, r * 256), BF16)] * 3,
        compiler_params=_params(("parallel", "parallel")),
    )(qv, qv, kv, kv, pv, pv, dov, dov, ov, ov, lv, lv)
    return outs


def _gelu_parts(gv):
    cdf = 0.5 * (1.0 + lax.erf(gv * (2.0 ** -0.5)))
    pdf = jnp.exp(-0.5 * gv * gv) * (1.0 / math.sqrt(2.0 * math.pi))
    return cdf, pdf


def _pick_row(t, k):
    row = lax.broadcasted_iota(jnp.int32, t.shape, 0)
    return jnp.sum(jnp.where(row == k, t, 0.0), axis=0, keepdims=True)


def _shift_rows(u, halo, n):
    row = lax.broadcasted_iota(jnp.int32, u.shape, 0)
    out = pltpu.roll(u, n, 0)
    for k in range(n):
        out = jnp.where(row == k, _pick_row(halo, 16 - n + k), out)
    return out


def _shift_rows_up(u, halo, n):
    rb = u.shape[0]
    row = lax.broadcasted_iota(jnp.int32, u.shape, 0)
    out = pltpu.roll(u, rb - n, 0)
    for k in range(n):
        out = jnp.where(row == rb - n + k, _pick_row(halo, k), out)
    return out


def _conv(u, halo, cw, cb):
    return cb + _pick_row(cw, 0) * _shift_rows(u, halo, 2) + _pick_row(cw, 1) * _shift_rows(u, halo, 1) + _pick_row(cw, 2) * u


def _local_step(x, mod, pos_col, target, sm, w_sh, g0, chip, core):
    s = x.shape[0]
    shift1, scale1, gate1, shift2, scale2, gate2 = [mod[i:i + 1, :] for i in range(6)]
    rb = 512
    chip1 = chip.reshape(1)

    def f_norm1(c, i, xv, nw, sc, sh):
        return ((xv * _rms(xv) * nw) * (1.0 + sc) + sh,)

    (h,) = _rowcall(f_norm1, [_rows(x, rb), _full(sm["n1w"]), _full(scale1), _full(shift1)],
                    [_orow(s, D, BF16, rb)], n_rows=s, rb=rb, name="norm1")
    own = lambda got, i: lax.dynamic_update_slice(got, w_sh[i], (chip, 0, 0))
    invf = jnp.tile(ROPE_THETA ** (-jnp.arange(ATT_HD // 2, dtype=F32) / (ATT_HD // 2)), 4).reshape(1, 128)
    cos_t, sin_t = _rope_tables(pos_col, invf, s)
    _, got0 = _unit_wait(*g0[:4], after=[h, cos_t, sin_t], name="gather_w_in_wait")
    [got0] = _comm_call("gather_w_in_d2d", [_u_gather_d2d(got0, (0,))])
    u_g1 = _u_gather_ici(w_sh, (1, 2, 3, 4, 5))
    g1 = _unit_start(u_g1, "gather_weights_start", after=got0)
    w = dict(win=_win_assemble(got0[0], w_sh[0], after=g1[3:]))
    p = _mm(h, w["win"], "in_proj", tm=2048, tn=1536)

    def f_gla_pre(c, i, glr, w2, gb):
        z = _dg(glr, w2.astype(BF16), 1, 0) + gb
        return ((jnp.minimum(z, 0.0) - jnp.log(1.0 + jnp.exp(-jnp.abs(z)))) * (1.0 / GLA_TAU),)

    (la,) = _rowcall(f_gla_pre, [_rows(p, rb, 128, P_LR // 128), _full(sm["w2"]), _full(sm["gb"])],
                     [_orow(s, 512, F32, rb)], n_rows=s, rb=rb, name="gla_pre")
    o_gla, states, _ = _gla_fwd(p, la, s)
    _, got = _unit_wait(u_g1, *g1[:3], after=[o_gla], name="gather_weights_wait")
    [got123] = _comm_call("gather_weights_d2d", [_u_gather_d2d(got[:3], (1, 2, 3))])
    got45 = got[3:]
    w.update(wgb=own(got123[0], 1).reshape(1024, D), wab=_cols_join(own(got123[1], 2)), wout=own(got123[2], 3).reshape(D, D))

    def f_gla_post(c, i, ov, gnw, gr):
        on = jnp.concatenate([ov[:, k * 256:(k + 1) * 256] * _rms(ov[:, k * 256:(k + 1) * 256]) * gnw
                              for k in range(GLA_H)], axis=1)
        g = gr.astype(F32)
        return (on * (g * _sigmoid(g)),)

    (og,) = _rowcall(f_gla_post, [_rows(o_gla, rb), _full(sm["gnw"]), _rows(p, rb, 1024, P_GR // 1024)],
                     [_orow(s, 1024, BF16, rb)], n_rows=s, rb=rb, name="gla_post")
    y_gla = _mm(og, w["wgb"], "gla_branch")

    q_d, k_d, v_d = _rope_fwd(p, cos_t, sin_t, s)
    att = [_attn_fwd(q_d[g], k_d[g], v_d[g], g, r, s) for g, r in enumerate(_RS)]
    o_att, lse, o_d1, o_d2, lse_d1, lse_d2 = _attn_combine(att, s)
    y_att = _mm(o_att, w["wab"], "attn_branch")

    def f_merge(c, i, ma, mb, yg, ya):
        return (_sigmoid(ma.astype(F32)) * yg.astype(F32) + _sigmoid(mb.astype(F32)) * ya.astype(F32),)

    (mixed,) = _rowcall(f_merge, [_rows(p, rb, D, P_MA // D), _rows(p, rb, D, P_MB // D), _rows(y_gla, rb), _rows(y_att, rb)],
                        [_orow(s, D, BF16, rb)], n_rows=s, rb=rb, name="merge")
    z1, [got45] = _mm(mixed, w["wout"], "out_proj", comm=[_u_gather_d2d(got45, (4, 5))])
    w.update(wup=own(got45[0], 4), wdown=own(got45[1], 5).reshape(D_FF, D))

    def f_norm2(c, i, xv, z, g1, nw, sc, sh):
        x1 = xv + g1 * z.astype(F32)
        return (x1, (x1 * _rms(x1) * nw) * (1.0 + sc) + sh)

    x1, h2 = _rowcall(f_norm2, [_rows(x, rb), _rows(z1, rb), _full(gate1), _full(sm["n2w"]), _full(scale2), _full(shift2)],
                      [_orow(s, D, F32, rb), _orow(s, D, BF16, rb)], n_rows=s, rb=rb, name="norm2")
    u = _mm(h2, w["wup"], "up_proj", tm=2048, b_shards=True)

    cwid = 2 * W_UP_SH

    def f_ffn(c, i, uv, hl, cw, cb):
        uc = _conv(uv.astype(F32), hl.astype(F32) * (i > 0).astype(F32), cw, cb)
        val, gt = uc[:, :W_UP_SH], uc[:, W_UP_SH:]
        cdf, _ = _gelu_parts(gt)
        return (gt * cdf * val,)

    ccol = lambda c: c
    rw = 256
    (hidden,) = _rowcall(f_ffn, [_rows(u, rw, cwid, ccol), _halo(u, rw, 16, cwid, ccol, True),
                                 _full(sm["cw"], cwid, ccol), _full(sm["cb"], cwid, ccol)],
                         [_orow(s, D_FF, BF16, rw, W_UP_SH, ccol)], n_rows=s, rb=rw, name="conv_geglu", ncol=2)
    z2 = _mm(hidden, w["wdown"], "down_proj", tk=D_FF)

    def f_final(c, i, x1v, z, g2, fw, tgt):
        x2 = x1v + g2 * z.astype(F32)
        r = _rms(x2)
        xh = x2 * r
        e = xh * fw - tgt
        loss = 0.5 * jnp.sum(jnp.mean(e * e, axis=-1, keepdims=True), axis=0, keepdims=True)
        dy = e * (1.0 / D)
        dxh = dy * fw
        dx2 = r * (dxh - xh * jnp.mean(dxh * xh, axis=-1, keepdims=True))
        return (loss, dx2, dx2 * g2, _csum(dy * xh), _csum(dx2 * z.astype(F32)))

    loss, dx2, dz2, d_fnw, d_gate2 = _rowcall(
        f_final, [_rows(x1, rb), _rows(z2, rb), _full(gate2), _full(sm["fnw"]), _rows(target, rb)],
        [_oacc(1, 1), _orow(s, D, F32, rb), _orow(s, D, BF16, rb), _oacc(1, D), _oacc(1, D)],
        n_rows=s, rb=rb, name="final_loss")
    d_hidden = _mm(dz2, w["wdown"], "down_proj_dx", tb=True, tn=1408)
    g_wdown = _mm(hidden, dz2, "down_proj_dw", ta=True, out_dtype=F32, tm=1408, tn=1024, tk=2048)

    def f_ffn_bwd(c, i, uv, hl, dh, cw, cb):
        uf = uv.astype(F32)
        hf = hl.astype(F32) * (i > 0).astype(F32)
        u1, u2 = _shift_rows(uf, hf, 1), _shift_rows(uf, hf, 2)
        uc = cb + _pick_row(cw, 0) * u2 + _pick_row(cw, 1) * u1 + _pick_row(cw, 2) * uf
        val, gt = uc[:, :W_UP_SH], uc[:, W_UP_SH:]
        cdf, pdf = _gelu_parts(gt)
        dhf = dh.astype(F32)
        duc = jnp.concatenate([dhf * (gt * cdf), dhf * val * (cdf + gt * pdf)], axis=1)
        dcw = jnp.concatenate([_csum(duc * u2), _csum(duc * u1), _csum(duc * uf)], axis=0)
        return (duc, _csum(duc), dcw)

    duc, d_cb, d_cw = _rowcall(
        f_ffn_bwd, [_rows(u, rw, cwid, ccol), _halo(u, rw, 16, cwid, ccol, True), _rows(d_hidden, rw, W_UP_SH, ccol),
                    _full(sm["cw"], cwid, ccol), _full(sm["cb"], cwid, ccol)],
        [_orow(s, 2 * D_FF, BF16, rw, cwid, ccol), _oacc(1, 2 * D_FF, cwid, ccol), _oacc(3, 2 * D_FF, cwid, ccol)],
        n_rows=s, rb=rw, name="conv_geglu_bwd", ncol=2)

    def f_conv_t(c, i, dv, hl, cw):
        df = dv.astype(F32)
        hf = hl.astype(F32) * (i < s // rw - 1).astype(F32)
        return (_pick_row(cw, 2) * df + _pick_row(cw, 1) * _shift_rows_up(df, hf, 1) + _pick_row(cw, 0) * _shift_rows_up(df, hf, 2),)

    (du,) = _rowcall(f_conv_t, [_rows(duc, rw, cwid, ccol), _halo(duc, rw, 16, cwid, ccol, False), _full(sm["cw"], cwid, ccol)],
                     [_orow(s, 2 * D_FF, BF16, rw, cwid, ccol)], n_rows=s, rb=rw, name="conv_transpose", ncol=2)
    g_wup = _mm(h2, du, "up_proj_dw", ta=True, out_dtype=F32, tm=1024, tk=2048, o_shards=True)
    gs45 = [g_wup, g_wdown.reshape(4, W_DOWN_SH, 1024)]
    d_h2, [land45] = _mm(du, w["wup"], "up_proj_dx", tb=True, tm=2048, b_shards=True, comm=[_u_pair_send(gs45, (4, 5))])
    ts45 = [_pair_add(g, ld, core, "grad_pair_add_" + BIG[i]) for g, ld, i in zip(gs45, land45, (4, 5))]
    u_ex4 = _u_chip_exchange(ts45[:1])
    ex4 = _unit_start(u_ex4, "grad_exchange_w_up_start")

    def f_norm2_bwd(c, i, x1v, dh, dxr, z, nw, sc, g1):
        dxn, dsh, dsc, dnw = _norm_bwd(x1v, dh.astype(F32), nw, sc)
        dx1 = dxr + dxn
        return (dx1, dx1 * g1, dsh, dsc, dnw, _csum(dx1 * z.astype(F32)))

    dx1, dz1, d_shift2, d_scale2, d_n2w, d_gate1 = _rowcall(
        f_norm2_bwd, [_rows(x1, rb), _rows(d_h2, rb), _rows(dx2, rb), _rows(z1, rb), _full(sm["n2w"]), _full(scale2), _full(gate1)],
        [_orow(s, D, F32, rb), _orow(s, D, BF16, rb), _oacc(1, D), _oacc(1, D), _oacc(1, D), _oacc(1, D)],
        n_rows=s, rb=rb, name="norm2_bwd", after=ex4[3:])
    d_mixed = _mm(dz1, w["wout"], "out_proj_dx", tb=True)
    g_wout = _mm(mixed, dz1, "out_proj_dw", ta=True, out_dtype=F32, tk=2048)

    def f_merge_bwd(c, i, dm, ma, mb, yg, ya):
        dmf, ygf, yaf = dm.astype(F32), yg.astype(F32), ya.astype(F32)
        sa, sb = _sigmoid(ma.astype(F32)), _sigmoid(mb.astype(F32))
        return (dmf * sa, dmf * sb, jnp.concatenate([dmf * ygf * sa * (1.0 - sa), dmf * yaf * sb * (1.0 - sb)], axis=1))

    dy_gla, dy_att, dp = _rowcall(
        f_merge_bwd, [_rows(d_mixed, rb), _rows(p, rb, D, P_MA // D), _rows(p, rb, D, P_MB // D), _rows(y_gla, rb), _rows(y_att, rb)],
        [_orow(s, D, BF16, rb)] * 2 + [_orow(s, P_W, BF16, rb, 2 * D, lambda c: P_MA // (2 * D))], n_rows=s, rb=rb, name="merge_bwd")
    d_og = _mm(dy_gla, w["wgb"], "gla_branch_dx", tb=True)
    g_wgb = _mm(og, dy_gla, "gla_branch_dw", ta=True, out_dtype=F32, tk=2048)
    d_oatt = _mm(dy_att, w["wab"], "attn_branch_dx", tb=True)
    g_wab = _mm(o_att, dy_att, "attn_branch_dw", ta=True, out_dtype=F32, tk=2048)

    def f_gla_post_bwd(c, i, ov, gnw, gr, dog):
        g = gr.astype(F32)
        sg = _sigmoid(g)
        silu = g * sg
        dof = dog.astype(F32)
        don = dof * silu
        on_parts, do_parts, dgn = [], [], jnp.zeros((1, 256), F32)
        for k in range(GLA_H):
            oh = ov[:, k * 256:(k + 1) * 256]
            dh = don[:, k * 256:(k + 1) * 256]
            r = _rms(oh)
            xh = oh * r
            dgn = dgn + _csum(dh * xh)
            dxh = dh * gnw
            do_parts.append(r * (dxh - xh * jnp.mean(dxh * xh, axis=-1, keepdims=True)))
            on_parts.append(xh * gnw)
        on = jnp.concatenate(on_parts, axis=1)
        dgr = dof * on * (sg * (1.0 + g * (1.0 - sg)))
        return (jnp.concatenate(do_parts, axis=1), dgr, dgn)

    do_gla, dp, d_gnw = _rowcall(
        f_gla_post_bwd, [_rows(o_gla, rb), _full(sm["gnw"]), _rows(p, rb, 1024, P_GR // 1024), _rows(d_og, rb)],
        [_orow(s, 1024, F32, rb), _orow(s, P_W, BF16, rb, 1024, lambda c: P_GR // 1024), _oacc(1, 256)],
        n_rows=s, rb=rb, name="gla_post_bwd", into=(dp, 1))
    gs123 = [g_wgb.reshape(4, 256, 1024), _cols_split(g_wab), g_wout.reshape(4, 256, 1024)]
    d_gq, d_gk, dp, d_la, [land123] = _gla_bwd(p, la, states, do_gla, s, dp, comm=[_u_pair_send(gs123, (1, 2, 3))])
    ts123 = [_pair_add(g, ld, core, "grad_pair_add_" + BIG[i]) for g, ld, i in zip(gs123, land123, (1, 2, 3))]

    def f_gla_pre_bwd(c, i, lav, dlav, glr, w2):
        dz = dlav * (1.0 / GLA_TAU) * (1.0 - jnp.exp(GLA_TAU * lav))
        dzb = dz.astype(BF16)
        return (_dg(dzb, w2.astype(BF16), 1, 1), _csum(dz), _dg(glr, dzb, 0, 0))

    d_glr, d_gb, d_w2 = _rowcall(
        f_gla_pre_bwd, [_rows(la, rb), _rows(d_la, rb), _rows(p, rb, 128, P_LR // 128), _full(sm["w2"])],
        [_orow(s, 128, BF16, rb), _oacc(1, 512), _oacc(128, 512)], n_rows=s, rb=rb, name="gla_pre_bwd")

    do_d = [d_oatt] + list(_dilate(d_oatt, s))
    datt = [_attn_bwd(q_d[g], k_d[g], v_d[g], do_d[g], (o_att, o_d1, o_d2)[g], (lse, lse_d1, lse_d2)[g], g, r, s)
            for g, r in enumerate(_RS)]
    dp = _rope_bwd(datt, d_glr, dp, cos_t, sin_t, s)
    dp = lax.dynamic_update_slice(dp, jnp.concatenate([d_gq, d_gk], axis=1), (0, P_GQ))
    [t4], [r4] = _unit_wait(u_ex4, *ex4[:3], after=[dp], name="grad_exchange_w_up_wait")
    half4 = [_chip_sum(t4, r4, chip1, "grad_chip_sum_w_up")]
    g_win, [r1235, oth4] = _mm(h, dp, "in_proj_dw", ta=True, out_dtype=F32, tm=1024, tn=1536, tk=2048,
                               comm=[_u_chip_exchange(ts123 + ts45[1:]), _u_pair_join(half4)])
    half1235 = [_chip_sum(t, r, chip1, "grad_chip_sum_" + BIG[i]) for t, r, i in zip(ts123 + ts45[1:], r1235, (1, 2, 3, 5))]
    gs0 = [_win_split(g_win)]
    d_h, [land0, oth1235] = _mm(dp, w["win"], "in_proj_dx", tb=True, tk=3840,
                                comm=[_u_pair_send(gs0, (0,)), _u_pair_join(half1235)])
    half123, half45 = half1235[:3], half4 + half1235[3:]
    oth123, oth45 = oth1235[:3], oth4 + oth1235[3:]
    ts0 = _pair_add(gs0[0], land0[0], core, "grad_pair_add_w_in")

    def f_norm1_bwd(c, i, xv, dh, dxr, nw, sc):
        dxn, dsh, dsc, dnw = _norm_bwd(xv, dh.astype(F32), nw, sc)
        return (dxr + dxn, dsh, dsc, dnw)

    grad_x, d_shift1, d_scale1, d_n1w = _rowcall(
        f_norm1_bwd, [_rows(x, rb), _rows(d_h, rb), _rows(dx1, rb), _full(sm["n1w"]), _full(scale1)],
        [_orow(s, D, F32, rb), _oacc(1, D), _oacc(1, D), _oacc(1, D)], n_rows=s, rb=rb, name="norm1_bwd")

    dmod = jnp.concatenate([d_shift1, d_scale1, d_gate1, d_shift2, d_scale2, d_gate2], axis=1)
    small = dict(dmod=dmod, n1w=d_n1w, gb=d_gb, gnw=d_gnw, n2w=d_n2w, cb=d_cb, fnw=d_fnw, w2=d_w2, cw=d_cw)
    return loss, grad_x, half123 + half45, oth123 + oth45, small, ts0


def _win_pieces():
    runs = [(P_GV, 1024, 2048), (P_MA, 5392, 2048), (P_GQ, 0, 1024), (P_AQ, 3088, 2304), (P_LR, 3072, GLA_LR)]
    out = []
    for kc, rc, ln in runs:
        while ln > 0:
            step = min(ln, W_IN_SH - rc % W_IN_SH)
            out.append((kc, rc, step))
            kc, rc, ln = kc + step, rc + step, ln - step
    return out


def _win_assemble(shards, own, after=()):
    rb = 256

    def body(s_ref, own_ref, *rest):
        o_ref = rest[-1]
        x, y, _ = _me()
        o_ref[:, W_IN:] = jnp.zeros((rb, P_W - W_IN), o_ref.dtype)
        for kc, rc, ln in _win_pieces():
            sh, lo = rc // W_IN_SH, rc % W_IN_SH
            o_ref[:, kc:kc + ln] = jnp.where(2 * x + y == sh, own_ref[0, :, lo:lo + ln], s_ref[sh, :, lo:lo + ln])

    return pl.pallas_call(
        body, name="w_in_assemble", grid=(D // rb,),
        in_specs=[pl.BlockSpec((4, rb, W_IN_SH), lambda i: (0, i, 0)), pl.BlockSpec((1, rb, W_IN_SH), lambda i: (0, i, 0))]
        + [pl.BlockSpec(memory_space=pl.ANY)] * len(after),
        out_specs=pl.BlockSpec((rb, P_W), lambda i: (i, 0)),
        out_shape=jax.ShapeDtypeStruct((D, P_W), shards.dtype), compiler_params=_params(("parallel",)),
    )(shards, own, *after)


def _win_split(g):
    rb = 256

    def body(g_ref, o_ref):
        for kc, rc, ln in _win_pieces():
            o_ref[rc // W_IN_SH, :, rc % W_IN_SH:rc % W_IN_SH + ln] = g_ref[:, kc:kc + ln]

    return pl.pallas_call(
        body, name="w_in_grad_split", grid=(D // rb,),
        in_specs=[pl.BlockSpec((rb, P_W), lambda i: (i, 0))], out_specs=pl.BlockSpec((4, rb, W_IN_SH), lambda i: (0, i, 0)),
        out_shape=jax.ShapeDtypeStruct((4, D, W_IN_SH), g.dtype), compiler_params=_params(("parallel",)),
    )(g)


def _ff_to_kernel(a):
    h = W_UP_SH
    return jnp.concatenate([a[:, 0:h], a[:, D_FF:D_FF + h], a[:, h:D_FF], a[:, D_FF + h:]], axis=1)


def _ff_from_kernel(a):
    h = W_UP_SH
    return jnp.concatenate([a[:, 0:h], a[:, 2 * h:3 * h], a[:, h:2 * h], a[:, 3 * h:]], axis=1)


BIG = ("w_in", "w_gla_branch", "w_attn_branch", "w_out", "w_up", "w_down")
SH_SHAPES = ((1024, W_IN_SH), (256, 1024), (256, 256), (256, 1024), (1024, W_UP_SH), (W_DOWN_SH, 1024))
N_BIG = len(BIG)


def _cols_join(t):
    return jnp.concatenate([t[k] for k in range(4)], axis=1)


def _cols_split(t):
    cols = t.shape[1] // 4
    return jnp.stack([t[:, k * cols:(k + 1) * cols] for k in range(4)])


def _me():
    return lax.axis_index("x"), lax.axis_index("y"), lax.axis_index("c")


HBM = pl.BlockSpec(memory_space=pltpu.HBM)
VMEM_SPEC = pl.BlockSpec(memory_space=pltpu.VMEM)


def _allgather8(xs, name):
    rows = xs.shape[0]

    def body(x_ref, out_ref, send_sems, recv_sems, local_sem):
        x, y, c = _me()
        me = 4 * x + 2 * y + c
        mine = pltpu.make_async_copy(x_ref, out_ref.at[me], local_sem)
        mine.start()
        flips = [(k >> 2 & 1, k >> 1 & 1, k & 1) for k in range(1, 8)]

        def peer(f):
            return (jnp.where(f[0] == 1, 1 - x, x), jnp.where(f[1] == 1, 1 - y, y), jnp.where(f[2] == 1, 1 - c, c))

        sends = []
        for k, f in enumerate(flips):
            cp = pltpu.make_async_remote_copy(src_ref=x_ref, dst_ref=out_ref.at[me], send_sem=send_sems.at[k],
                                              recv_sem=recv_sems.at[k], device_id=peer(f), device_id_type=MESH)
            cp.start()
            sends.append(cp)
        for k, f in enumerate(flips):
            px, py, pc = peer(f)
            pltpu.make_async_remote_copy(src_ref=x_ref, dst_ref=out_ref.at[4 * px + 2 * py + pc], send_sem=send_sems.at[k],
                                         recv_sem=recv_sems.at[k], device_id=peer(f), device_id_type=MESH).wait_recv()
        for cp in sends:
            cp.wait_send()
        mine.wait()

    return pl.pallas_call(
        body, name=name, out_shape=jax.ShapeDtypeStruct((8, rows, 128), F32),
        in_specs=[VMEM_SPEC], out_specs=VMEM_SPEC,
        scratch_shapes=[pltpu.SemaphoreType.DMA((7,)), pltpu.SemaphoreType.DMA((7,)), pltpu.SemaphoreType.DMA],
        compiler_params=pltpu.CompilerParams(vmem_limit_bytes=VMEM_LIMIT),
    )(xs)


def _half_rows(i, cc, unit):
    rows = SH_SHAPES[i][0] // 2
    return pl.ds(pl.multiple_of(cc * rows, unit), rows)


def _rc(src, dst, sems, to):
    return pltpu.make_async_remote_copy(src_ref=src, dst_ref=dst, send_sem=sems[0], recv_sem=sems[1], device_id=to, device_id_type=MESH)


def _other_chips(x, y):
    return [(1 - x, y), (x, 1 - y), (1 - x, 1 - y)]


def _u_gather_ici(w_sh, idxs):
    def copies(ins, outs, sem):
        x, y, c = _me()
        res = []
        for j, (px, py) in enumerate(_other_chips(x, y)):
            for n, i in enumerate(idxs):
                src = ins[n].at[0, _half_rows(i, c, 16)]
                res.append((_rc(src, outs[n].at[2 * x + y, _half_rows(i, c, 16)], sem(j * len(idxs) + n), (px, py, c)),
                            _rc(src, outs[n].at[2 * px + py, _half_rows(i, c, 16)], sem(j * len(idxs) + n), (px, py, c))))
        return res

    return dict(ins=[w_sh[i] for i in idxs], outs=[jax.ShapeDtypeStruct((4,) + SH_SHAPES[i], BF16) for i in idxs],
                nsem=3 * len(idxs), alias={}, copies=copies)


def _u_gather_d2d(got, idxs):
    def copies(ins, outs, sem):
        x, y, c = _me()
        res = []
        for j, (px, py) in enumerate(_other_chips(x, y)):
            for n, i in enumerate(idxs):
                src = ins[n].at[2 * px + py, _half_rows(i, c, 16)]
                res.append((_rc(src, outs[n].at[2 * px + py, _half_rows(i, c, 16)], sem(j * len(idxs) + n), (x, y, 1 - c)),
                            _rc(src, outs[n].at[2 * px + py, _half_rows(i, 1 - c, 16)], sem(j * len(idxs) + n), (x, y, 1 - c))))
        return res

    return dict(ins=list(got), outs=[jax.ShapeDtypeStruct(g.shape, g.dtype) for g in got], nsem=3 * len(idxs),
                alias={n: n for n in range(len(idxs))}, copies=copies)


def _u_pair_send(gs, idxs):
    def copies(ins, outs, sem):
        x, y, c = _me()
        res = []
        for n, i in enumerate(idxs):
            for sh in range(4):
                cp = _rc(ins[n].at[sh, _half_rows(i, 1 - c, 8)], outs[n].at[sh], sem(4 * n + sh), (x, y, 1 - c))
                res.append((cp, cp))
        return res

    return dict(ins=list(gs), outs=[jax.ShapeDtypeStruct((4, SH_SHAPES[i][0] // 2, SH_SHAPES[i][1]), F32) for i in idxs],
                nsem=4 * len(idxs), alias={}, copies=copies)


def _u_chip_exchange(ts):
    def copies(ins, outs, sem):
        x, y, c = _me()
        res = []
        for j, (px, py) in enumerate(_other_chips(x, y)):
            for n in range(len(ts)):
                cp = _rc(ins[n].at[2 * px + py], outs[n].at[j], sem(j * len(ts) + n), (px, py, c))
                res.append((cp, cp))
        return res

    return dict(ins=list(ts), outs=[jax.ShapeDtypeStruct((3,) + t.shape[1:], t.dtype) for t in ts], nsem=3 * len(ts),
                alias={}, copies=copies)


def _u_pair_join(hs):
    def copies(ins, outs, sem):
        x, y, c = _me()
        res = []
        for n in range(len(hs)):
            cp = _rc(ins[n], outs[n], sem(n), (x, y, 1 - c))
            res.append((cp, cp))
        return res

    return dict(ins=list(hs), outs=[jax.ShapeDtypeStruct(h.shape, h.dtype) for h in hs], nsem=len(hs), alias={}, copies=copies)


def _comm_phase(units, ci, co, send_sems, recv_sems, start):
    ii = oo = off = 0
    for u in units:
        ni, no = len(u["ins"]), len(u["outs"])
        for st, arrival in u["copies"](ci[ii:ii + ni], co[oo:oo + no], lambda k, off=off: (send_sems.at[off + k], recv_sems.at[off + k])):
            if start:
                st.start()
            else:
                st.wait_send()
                arrival.wait_recv()
        ii, oo, off = ii + ni, oo + no, off + u["nsem"]


def _carry(units, n_in, n_out):
    ins = [a for u in units for a in u["ins"]]
    outs = [o for u in units for o in u["outs"]]
    alias, ii, oo = {}, 0, 0
    for u in units:
        for a, b in u["alias"].items():
            alias[n_in + ii + a] = n_out + oo + b
        ii, oo = ii + len(u["ins"]), oo + len(u["outs"])
    nsem = sum(u["nsem"] for u in units)
    scratch = [pltpu.SemaphoreType.DMA((nsem,)), pltpu.SemaphoreType.DMA((nsem,))] if units else []
    return ins, outs, alias, scratch


def _split_units(units, res):
    out, oo = [], 0
    for u in units:
        out.append(list(res[oo:oo + len(u["outs"])]))
        oo += len(u["outs"])
    return out


def _comm_call(name, units):
    ins, outs, alias, scratch = _carry(units, 0, 0)

    def body(*refs):
        ci, co = refs[:len(ins)], refs[len(ins):len(ins) + len(outs)]
        _comm_phase(units, ci, co, refs[-2], refs[-1], True)
        _comm_phase(units, ci, co, refs[-2], refs[-1], False)

    res = pl.pallas_call(body, name=name, out_shape=outs, in_specs=[HBM] * len(ins), out_specs=[HBM] * len(outs),
                         scratch_shapes=scratch, input_output_aliases=alias)(*ins)
    return _split_units(units, res)


SEM = pl.BlockSpec(memory_space=pltpu.SEMAPHORE)
EFFECT = pltpu.SideEffectType.DATAFLOW_SIDE_EFFECTING


def _unit_start(unit, name, after=()):
    bufs = list(unit["ins"]) + [lax.empty(o.shape, o.dtype) for o in unit["outs"]]
    n_i, n_b, ns = len(unit["ins"]), len(bufs), unit["nsem"]

    def body(*refs):
        send_sems, recv_sems = refs[n_b + len(after)], refs[n_b + len(after) + 1]
        for st, _ in unit["copies"](refs[:n_i], refs[n_i:n_b], lambda k: (send_sems.at[k], recv_sems.at[k])):
            st.start()
        refs[-1][...] = jnp.zeros_like(refs[-1])

    res = pl.pallas_call(
        body, name=name,
        out_shape=[pltpu.SemaphoreType.DMA((ns,)), pltpu.SemaphoreType.DMA((ns,))] + [pltpu.HBM(b.shape, b.dtype) for b in bufs]
        + [jax.ShapeDtypeStruct((8, 128), F32)],
        in_specs=[HBM] * n_b + [pl.BlockSpec(memory_space=pl.ANY)] * len(after), out_specs=[SEM, SEM] + [HBM] * n_b + [VMEM_SPEC],
        input_output_aliases={i: 2 + i for i in range(n_b)},
        compiler_params=pltpu.CompilerParams(has_side_effects=EFFECT),
    )(*[pltpu.with_memory_space_constraint(b, pltpu.HBM) for b in bufs], *after)
    return res[0], res[1], list(res[2:2 + n_b]), res[-1]


def _unit_wait(unit, send_sems, recv_sems, bufs, after, name):
    n_i, n_b = len(unit["ins"]), len(bufs)

    def body(*refs):
        ss, rs = refs[n_b], refs[n_b + 1]
        for st, arrival in unit["copies"](refs[:n_i], refs[n_i:n_b], lambda k: (ss.at[k], rs.at[k])):
            st.wait_send()
            arrival.wait_recv()

    res = pl.pallas_call(
        body, name=name, out_shape=[pltpu.HBM(b.shape, b.dtype) for b in bufs],
        in_specs=[HBM] * n_b + [SEM, SEM] + [pl.BlockSpec(memory_space=pl.ANY)] * len(after), out_specs=[HBM] * n_b,
        input_output_aliases={i: i for i in range(n_b)}, compiler_params=pltpu.CompilerParams(has_side_effects=EFFECT),
    )(*bufs, send_sems, recv_sems, *after)
    return list(res[:n_i]), list(res[n_i:])


def _pair_add(g, land, core, name):
    _, rows, cols = g.shape
    half = rows // 2
    rb = _tile(half, 512, 16)
    nb = half // rb

    def body(c_ref, g_ref, l_ref, o_ref):
        o_ref[...] = (g_ref[...] + l_ref[...]).astype(BF16)

    return pl.pallas_call(
        body, name=name,
        grid_spec=pltpu.PrefetchScalarGridSpec(
            num_scalar_prefetch=1, grid=(4, nb),
            in_specs=[pl.BlockSpec((1, rb, cols), lambda s, i, c_ref: (s, c_ref[0] * nb + i, 0)),
                      pl.BlockSpec((1, rb, cols), lambda s, i, c_ref: (s, i, 0))],
            out_specs=pl.BlockSpec((1, rb, cols), lambda s, i, c_ref: (s, i, 0))),
        out_shape=jax.ShapeDtypeStruct((4, half, cols), BF16),
        compiler_params=_params(("parallel", "parallel")),
    )(core, g, land)


def _chip_sum(t, r, chip, name):
    _, half, cols = t.shape
    rb = _tile(half, 512, 16)

    def body(s_ref, t_ref, r_ref, o_ref):
        o_ref[...] = ((t_ref[0].astype(F32) + r_ref[0].astype(F32)) + r_ref[1].astype(F32)) + r_ref[2].astype(F32)

    return pl.pallas_call(
        body, name=name,
        grid_spec=pltpu.PrefetchScalarGridSpec(
            num_scalar_prefetch=1, grid=(half // rb,),
            in_specs=[pl.BlockSpec((1, rb, cols), lambda i, s_ref: (s_ref[0], i, 0)),
                      pl.BlockSpec((3, rb, cols), lambda i, s_ref: (0, i, 0))],
            out_specs=pl.BlockSpec((rb, cols), lambda i, s_ref: (i, 0))),
        out_shape=jax.ShapeDtypeStruct((half, cols), F32),
        compiler_params=_params(("parallel",)),
    )(chip, t, r)


def _adam_math(wv, gv, mv, vv):
    mn = ADAM_B1 * mv + (1.0 - ADAM_B1) * gv
    vn = ADAM_B2 * vv + (1.0 - ADAM_B2) * (gv * gv)
    m_hat = mn / (1.0 - ADAM_B1 ** ADAM_STEP)
    v_hat = vn / (1.0 - ADAM_B2 ** ADAM_STEP)
    return -ADAM_LR * (m_hat / (jnp.sqrt(v_hat) + ADAM_EPS) + ADAM_WD * wv), mn, vn


def _adamw_halves(wt, mt, vt, mine, theirs, core, name):
    _, rows, cols = wt.shape
    half = rows // 2
    rb = _tile(half, 256, 8)
    nb = half // rb

    def body(c_ref, w_ref, m_ref, v_ref, a_ref, b_ref, g_ref, d_ref, mo_ref, vo_ref):
        gv = jnp.where(pl.program_id(0) == c_ref[0], a_ref[...], b_ref[...])
        dl, mn, vn = _adam_math(w_ref[...], gv, m_ref[...], v_ref[...])
        g_ref[...] = gv
        d_ref[...] = dl
        mo_ref[...] = mn
        vo_ref[...] = vn

    full = pl.BlockSpec((None, rb, cols), lambda hf, i, c_ref: (0, hf * nb + i, 0))
    part = pl.BlockSpec((rb, cols), lambda hf, i, c_ref: (i, 0))
    return pl.pallas_call(
        body, name=name,
        grid_spec=pltpu.PrefetchScalarGridSpec(num_scalar_prefetch=1, grid=(2, nb), in_specs=[full, full, full, part, part],
                                               out_specs=[full] * 4),
        out_shape=[jax.ShapeDtypeStruct((1, rows, cols), F32)] * 4,
        compiler_params=_params(("parallel", "parallel")),
    )(core, wt, mt, vt, mine, theirs)


SG_REP = 144
SG_LOSS = 136
SG_W2, SG_CW = SG_REP, SG_REP + 4 * 16
SG_ROWS = SG_CW + 4 * 40
SP_ROWS = SG_REP + 16 + 40


def _mod_shard(c_all, ada_w_sh):
    def body(c_ref, w_ref, o_ref):
        cv = c_ref[...]
        o_ref[...] = _dg((cv * _sigmoid(cv)).astype(BF16), w_ref[...].astype(BF16), 1, 0)

    return pl.pallas_call(body, name="mod_shard", out_shape=jax.ShapeDtypeStruct((8, 1536), F32),
                          in_specs=[VMEM_SPEC, VMEM_SPEC], out_specs=VMEM_SPEC,
                          compiler_params=pltpu.CompilerParams(vmem_limit_bytes=VMEM_LIMIT))(c_all, ada_w_sh)


def _mod_select(mod_all, ada_b4):
    def body(m_ref, b_ref, o_ref):
        x, y, c = _me()
        me = 4 * x + 2 * y + c
        for sh in range(4):
            o_ref[sh] = m_ref[2 * sh, me] + b_ref[sh]

    return pl.pallas_call(body, name="mod_select", out_shape=jax.ShapeDtypeStruct((4, 12, 128), F32),
                          in_specs=[VMEM_SPEC, VMEM_SPEC], out_specs=VMEM_SPEC)(mod_all, ada_b4)


def _small_reduce(sg_all):
    def body(g_ref, o_ref):
        x, y, c = _me()
        s_me = 2 * x + y
        w2_rows = pl.ds(pl.multiple_of(SG_W2 + 16 * s_me, 8), 16)
        cw_rows = pl.ds(pl.multiple_of(SG_CW + 40 * s_me, 8), 40)
        a = g_ref[0, 0:SG_REP, :]
        b = g_ref[0, w2_rows, :]
        d = g_ref[0, cw_rows, :]
        for dev in range(1, 8):
            a = a + g_ref[dev, 0:SG_REP, :]
            b = b + g_ref[dev, w2_rows, :]
            d = d + g_ref[dev, cw_rows, :]
        o_ref[0:SG_REP, :] = a
        o_ref[SG_REP:SG_REP + 16, :] = b
        o_ref[SG_REP + 16:SP_ROWS, :] = d

    return pl.pallas_call(body, name="small_grad_reduce", out_shape=jax.ShapeDtypeStruct((SP_ROWS, 128), F32),
                          in_specs=[VMEM_SPEC], out_specs=VMEM_SPEC)(sg_all)


def _ada_grad(dmod_all, c_bc):
    def body(g_ref, c_ref, o_ref):
        x, y, c = _me()
        s_me = 2 * x + y
        for k in range(12):
            acc = jnp.zeros((D, 128), F32)
            for b in range(8):
                cv = c_ref[b]
                acc = acc + (cv * _sigmoid(cv)) * g_ref[s_me, k, b:b + 1, :]
            o_ref[:, k * 128:(k + 1) * 128] = acc

    return pl.pallas_call(body, name="ada_w_grad", out_shape=jax.ShapeDtypeStruct((D, 1536), F32),
                          in_specs=[VMEM_SPEC, VMEM_SPEC], out_specs=VMEM_SPEC,
                          compiler_params=pltpu.CompilerParams(vmem_limit_bytes=VMEM_LIMIT))(dmod_all, c_bc)


def _adamw(wt, g, m, v, name):
    rows, cols = wt.shape
    rb = _tile(rows, 256, 8)

    def fn(c, i, wv, gv, mv, vv):
        return _adam_math(wv, gv, mv, vv)

    return _rowcall(fn, [_rows(t, rb) for t in (wt, g, m, v)], [_orow(rows, cols, F32, rb)] * 3,
                    n_rows=rows, rb=rb, name=name)


def _pad_rows(t, rows):
    flat = t.reshape(-1)
    return jnp.pad(flat, (0, rows * 128 - flat.shape[0])).reshape(rows, 128)


SP_LAYOUT = (("ada_b", 48), ("norm1_w", 8), ("gla_gate_b", 8), ("gla_norm_w", 8), ("norm2_w", 8), ("conv_b", 48),
             ("final_norm_w", 8), (None, 8), ("gla_gate_w2", 16), ("conv_w", 40))


def _pack_small(d):
    return jnp.concatenate([jnp.zeros((rows, 128), F32) if n is None else _pad_rows(d[n].astype(F32), rows)
                            for n, rows in SP_LAYOUT], axis=0)


def _unpack_small(pk, shapes):
    out, off = {}, 0
    for n, rows in SP_LAYOUT:
        if n is not None:
            shp = shapes[n]
            out[n] = pk[off:off + rows].reshape(-1)[:math.prod(shp)].reshape(shp)
        off += rows
    return out


def kernel(x, c, positions, ada_w, ada_b, norm1_w, w_in, gla_gate_w2, gla_gate_b, gla_norm_w, w_gla_branch, w_attn_branch, w_out, norm2_w, w_up, conv_w, conv_b, w_down, final_norm_w, loss_target, m_ada_w, m_ada_b, m_norm1_w, m_w_in, m_gla_gate_w2, m_gla_gate_b, m_gla_norm_w, m_w_gla_branch, m_w_attn_branch, m_w_out, m_norm2_w, m_w_up, m_conv_w, m_conv_b, m_w_down, m_final_norm_w, v_ada_w, v_ada_b, v_norm1_w, v_w_in, v_gla_gate_w2, v_gla_gate_b, v_gla_norm_w, v_w_gla_branch, v_w_attn_branch, v_w_out, v_norm2_w, v_w_up, v_conv_w, v_conv_b, v_w_down, v_final_norm_w):
    s = x.shape[1]
    names = ("ada_w", "ada_b", "norm1_w", "w_in", "gla_gate_w2", "gla_gate_b", "gla_norm_w", "w_gla_branch", "w_attn_branch",
             "w_out", "norm2_w", "w_up", "conv_w", "conv_b", "w_down", "final_norm_w")
    wts = dict(zip(names, (ada_w, ada_b, norm1_w, w_in, gla_gate_w2, gla_gate_b, gla_norm_w, w_gla_branch, w_attn_branch,
                           w_out, norm2_w, w_up, conv_w, conv_b, w_down, final_norm_w)))
    ms = dict(zip(names, (m_ada_w, m_ada_b, m_norm1_w, m_w_in, m_gla_gate_w2, m_gla_gate_b, m_gla_norm_w, m_w_gla_branch,
                          m_w_attn_branch, m_w_out, m_norm2_w, m_w_up, m_conv_w, m_conv_b, m_w_down, m_final_norm_w)))
    vs = dict(zip(names, (v_ada_w, v_ada_b, v_norm1_w, v_w_in, v_gla_gate_w2, v_gla_gate_b, v_gla_norm_w, v_w_gla_branch,
                          v_w_attn_branch, v_w_out, v_norm2_w, v_w_up, v_conv_w, v_conv_b, v_w_down, v_final_norm_w)))

    pk0 = jnp.concatenate([_pad_rows(c, 8), _pad_rows(gla_gate_w2, 16), _pad_rows(conv_w, 40)], axis=0)
    sm_all = _allgather8(pk0, "gather_small")
    c_all = sm_all[:, 0:8, :].reshape(8, D)
    w2_full = sm_all[0::2, 8:24, :].transpose(1, 0, 2).reshape(GLA_LR, 512)
    cw_full = sm_all[0::2, 24:64, :].reshape(4, 40 * 128)[:, :3 * W_UP_SH].reshape(4, 3, W_UP_SH).transpose(1, 0, 2).reshape(3, 2 * D_FF)

    mod_sh = _mod_shard(c_all, ada_w[0])
    mod_all = _allgather8(mod_sh.reshape(96, 128), "gather_mod")

    w_sh = [wts[n].astype(BF16) for n in BIG]
    u_g0 = _u_gather_ici(w_sh, (0,))
    g0 = (u_g0,) + _unit_start(u_g0, "gather_w_in_start", after=[mod_all])
    mod = _mod_select(mod_all.reshape(8, 8, 12, 128) + g0[4][0, 0], ada_b.reshape(4, 12, 128)).reshape(6, D)

    core = lax.axis_index("c").astype(jnp.int32).reshape(1)
    chip = (2 * lax.axis_index("x") + lax.axis_index("y")).astype(jnp.int32)
    sm = dict(n1w=norm1_w, n2w=norm2_w, fnw=final_norm_w.reshape(1, D), gnw=gla_norm_w, gb=gla_gate_b,
              w2=jnp.pad(w2_full, ((0, 128 - GLA_LR), (0, 0))), cw=_ff_to_kernel(cw_full), cb=_ff_to_kernel(conv_b))
    loss, grad_x, halves, others, small, ts0 = _local_step(x[0], mod, positions.reshape(s, 1), loss_target[0], sm, w_sh,
                                                               g0, chip, core)

    dcw = _ff_from_kernel(small["cw"]).reshape(3, 4, W_UP_SH).transpose(1, 0, 2)
    dw2 = small["w2"][:GLA_LR].reshape(GLA_LR, 4, 128).transpose(1, 0, 2)
    sg = jnp.concatenate(
        [_pad_rows(small["dmod"], 48), _pad_rows(small["n1w"], 8), _pad_rows(small["gb"], 8), _pad_rows(small["gnw"], 8),
         _pad_rows(small["n2w"], 8), _pad_rows(_ff_from_kernel(small["cb"]), 48), _pad_rows(small["fnw"], 8), _pad_rows(loss, 8)]
        + [_pad_rows(dw2[k], 16) for k in range(4)] + [_pad_rows(dcw[k], 40) for k in range(4)], axis=0)
    sg_all = _allgather8(sg, "gather_small_grads")
    u_ex = _u_chip_exchange([ts0])
    pending = (u_ex,) + _unit_start(u_ex, "grad_exchange_w_in_start", after=[sg_all])
    sg_all = sg_all + pending[4][0, 0]
    g_small_pk = _small_reduce(sg_all)
    dmod_all = sg_all[:, 0:48, :].reshape(8, 4, 12, 128).transpose(1, 2, 0, 3)
    g_ada_w = _ada_grad(dmod_all, jnp.broadcast_to(c_all[:, :, None], (8, D, 128)))

    shapes = {n: wts[n].shape for n in names}
    g_small = _unpack_small(g_small_pk, shapes)
    grads = {"ada_w": g_ada_w.reshape(1, D, 1536), **g_small}
    deltas, new_m, new_v = {}, {}, {}
    for n, mine, theirs in zip(BIG[1:], halves, others):
        grads[n], deltas[n], new_m[n], new_v[n] = _adamw_halves(wts[n], ms[n], vs[n], mine, theirs, core, "adamw_" + n)
    shp = ada_w.shape
    d_, m_, v_ = _adamw(ada_w[0], g_ada_w, m_ada_w[0], v_ada_w[0], "adamw_ada_w")
    deltas["ada_w"], new_m["ada_w"], new_v["ada_w"] = d_.reshape(shp), m_.reshape(shp), v_.reshape(shp)
    d_, m_, v_ = _adamw(_pack_small(wts), g_small_pk, _pack_small(ms), _pack_small(vs), "adamw_small")
    for dst, pk in ((deltas, d_), (new_m, m_), (new_v, v_)):
        dst.update(_unpack_small(pk, shapes))

    [t0], [r0] = _unit_wait(*pending[:4], after=[d_, deltas["ada_w"], deltas["w_up"], deltas["w_down"]], name="grad_exchange_w_in_wait")
    half0 = _chip_sum(t0, r0, chip.reshape(1), "grad_chip_sum_w_in")
    [[oth0]] = _comm_call("grad_join_w_in", [_u_pair_join([half0])])
    grads["w_in"], deltas["w_in"], new_m["w_in"], new_v["w_in"] = _adamw_halves(w_in, m_w_in, v_w_in, half0, oth0, core, "adamw_w_in")

    return (g_small_pk[SG_LOSS, 0], grad_x.reshape(1, s, D), *[grads[n] for n in names], *[deltas[n] for n in names],
            *[new_m[n] for n in names], *[new_v[n] for n in names])
```

```python
import math

import jax
import jax.numpy as jnp
from jax import lax
from jax.experimental import pallas as pl
from jax.experimental.pallas import tpu as pltpu

F32, BF16 = jnp.float32, jnp.bfloat16
MESH = pl.DeviceIdType.MESH

D = 1024
EPS = 1e-6
GLA_H, GLA_DK, GLA_DV, GLA_LR = 4, 128, 256, 16
GLA_TAU = 16.0
GLA_CHUNK = 64
GLA_BLOCK = 512
ATT_GROUPS = ((128, 1), (512, 4), (2048, 16))
ATT_BLK = 128
ATT_HD = 64
ATT_W = 768
D_FF = 2816
ROPE_THETA = 10000.0
P_W = 7680
P_GV, P_GR, P_MA, P_MB, P_GQ, P_GK, P_AQ, P_AK, P_AV, P_LR = 0, 1024, 2048, 3072, 4096, 4608, 5120, 5888, 6656, 7424
W_IN = 7440
W_IN_SH, W_UP_SH, W_DOWN_SH = 1860, 1408, 704
VMEM_LIMIT = 56 * 1024 * 1024
ADAM_LR, ADAM_B1, ADAM_B2, ADAM_EPS, ADAM_WD, ADAM_STEP = 0.001, 0.9, 0.999, 1e-08, 0.01, 10
NEG = -1e30


def _tile(n, target, unit=128):
    best = None
    for t in range(unit, min(n, target) + 1, unit):
        if n % t == 0:
            best = t
    return best or n


def _params(sem):
    return pltpu.CompilerParams(dimension_semantics=sem, vmem_limit_bytes=VMEM_LIMIT)


def _dg(a, b, ca, cb):
    return lax.dot_general(a, b, (((ca,), (cb,)), ((), ())), preferred_element_type=F32)


def _sigmoid(v):
    return 1.0 / (1.0 + jnp.exp(-v))


def _ff_block(j):
    return (j % 2) * 2 + j // 2


def _mm(a, b, name, *, ta=False, tb=False, out_dtype=BF16, tm=1024, tn=1536, tk=1024, n_outer=True, comm=(),
        b_shards=False, o_shards=False):
    m = a.shape[1] if ta else a.shape[0]
    k = a.shape[0] if ta else a.shape[1]
    if b_shards:
        n = b.shape[1] if tb else 4 * W_UP_SH
        tn, tk = (tn, W_UP_SH) if tb else (W_UP_SH, tk)
    else:
        n = b.shape[0] if tb else b.shape[1]
    if o_shards:
        tn = W_UP_SH
    tm, tn, tk = _tile(m, tm), _tile(n, tn), _tile(k, tk)
    nm, nn, nk = m // tm, n // tn, k // tk
    in_out = out_dtype == F32
    c_ins, c_outs, c_alias, c_scratch = _carry(comm, 2, 1)

    def body(a_ref, b_ref, *rest):
        ci, o_ref, co = rest[:len(c_ins)], rest[len(c_ins)], rest[len(c_ins) + 1:len(c_ins) + 1 + len(c_outs)]
        scr = rest[len(c_ins) + 1 + len(c_outs):]
        kk = pl.program_id(2)
        if comm:
            step = (pl.program_id(0) * (nm if n_outer else nn) + pl.program_id(1)) * nk + kk

            @pl.when(step == 0)
            def _():
                _comm_phase(comm, ci, co, scr[-2], scr[-1], True)

        _mm_step(a_ref, b_ref, o_ref, scr, kk)
        if comm:
            @pl.when(step == nm * nn * nk - 1)
            def _():
                _comm_phase(comm, ci, co, scr[-2], scr[-1], False)

    def _mm_step(a_ref, b_ref, o_ref, scr, kk):
        p = _dg(a_ref[...].astype(BF16), b_ref[...].astype(BF16), 0 if ta else 1, 1 if tb else 0)
        if nk == 1:
            o_ref[...] = p.astype(o_ref.dtype)
        else:
            acc = o_ref if in_out else scr[0]

            @pl.when(kk == 0)
            def _():
                acc[...] = p

            @pl.when(kk > 0)
            def _():
                acc[...] += p

            if not in_out:
                @pl.when(kk == nk - 1)
                def _():
                    o_ref[...] = acc[...].astype(o_ref.dtype)

    if n_outer:
        ij = lambda g0, g1: (g1, g0)
        grid = (nn, nm, nk)
    else:
        ij = lambda g0, g1: (g0, g1)
        grid = (nm, nn, nk)
    a_map = (lambda g0, g1, kk: (kk, ij(g0, g1)[0])) if ta else (lambda g0, g1, kk: (ij(g0, g1)[0], kk))
    if b_shards and tb:
        b_spec = pl.BlockSpec((None, tn, tk), lambda g0, g1, kk: (_ff_block(kk), ij(g0, g1)[1], 0))
    elif b_shards:
        b_spec = pl.BlockSpec((None, tk, tn), lambda g0, g1, kk: (_ff_block(ij(g0, g1)[1]), kk, 0))
    elif tb:
        b_spec = pl.BlockSpec((tn, tk), lambda g0, g1, kk: (ij(g0, g1)[1], kk))
    else:
        b_spec = pl.BlockSpec((tk, tn), lambda g0, g1, kk: (kk, ij(g0, g1)[1]))
    if o_shards:
        o_spec = pl.BlockSpec((None, tm, tn), lambda g0, g1, kk: (_ff_block(ij(g0, g1)[1]), ij(g0, g1)[0], 0))
        o_shape = jax.ShapeDtypeStruct((4, m, W_UP_SH), out_dtype)
    else:
        o_spec = pl.BlockSpec((tm, tn), lambda g0, g1, kk: ij(g0, g1))
        o_shape = jax.ShapeDtypeStruct((m, n), out_dtype)
    res = pl.pallas_call(
        body, name=name, grid=grid,
        in_specs=[pl.BlockSpec((tk, tm) if ta else (tm, tk), a_map), b_spec] + [HBM] * len(c_ins),
        out_specs=[o_spec] + [HBM] * len(c_outs),
        out_shape=[o_shape] + c_outs,
        scratch_shapes=([] if (in_out or nk == 1) else [pltpu.VMEM((tm, tn), F32)]) + c_scratch,
        input_output_aliases=c_alias,
        compiler_params=_params(("arbitrary",) * 3 if comm else ("parallel", "parallel", "arbitrary")),
    )(a, b, *c_ins)
    return (res[0], _split_units(comm, res[1:])) if comm else res[0]


def _rows(arr, rb, w=None, j=0):
    w = arr.shape[1] if w is None else w
    if callable(j):
        return arr, pl.BlockSpec((rb, w), lambda c, i: (i, j(c)))
    return arr, pl.BlockSpec((rb, w), lambda c, i: (i, j))


def _full(arr, w=None, j=0):
    w = arr.shape[1] if w is None else w
    if callable(j):
        return arr, pl.BlockSpec((arr.shape[0], w), lambda c, i: (0, j(c)))
    return arr, pl.BlockSpec((arr.shape[0], w), lambda c, i: (0, j))


def _halo(arr, rb, hb, w, j, before):
    per = rb // hb
    last = arr.shape[0] // hb - 1
    if before:
        rmap = lambda i: jnp.maximum(i * per - 1, 0)
    else:
        rmap = lambda i: jnp.minimum((i + 1) * per, last)
    return arr, pl.BlockSpec((hb, w), lambda c, i: (rmap(i), j(c) if callable(j) else j))


def _rowcall(fn, ins, outs, *, n_rows, rb, name, ncol=1, into=None, after=()):
    n_in = len(ins)
    nr = n_rows // rb
    unread = ([] if into is None else [into[0]]) + list(after)
    n_skip = len(unread)

    def body(*refs):
        c, i = pl.program_id(0), pl.program_id(1)
        res = fn(c, i, *[r[...] for r in refs[:n_in]])
        for val, spec, o_ref in zip(res, outs, refs[n_in + n_skip:]):
            if spec[2] == "row":
                o_ref[...] = val.astype(o_ref.dtype)
            else:
                @pl.when(i == 0)
                def _(o_ref=o_ref, val=val):
                    o_ref[...] = val.astype(o_ref.dtype)

                @pl.when(i > 0)
                def _(o_ref=o_ref, val=val):
                    o_ref[...] += val.astype(o_ref.dtype)

    out_specs = []
    for shape, dt, kind, block, col in outs:
        if kind == "row":
            out_specs.append(pl.BlockSpec(block, lambda c, i, col=col: (i, col(c))))
        else:
            out_specs.append(pl.BlockSpec(block, lambda c, i, col=col: (0, col(c))))
    return pl.pallas_call(
        body, name=name, grid=(ncol, nr),
        in_specs=[s for _, s in ins] + [pl.BlockSpec(memory_space=pl.ANY)] * n_skip, out_specs=out_specs,
        out_shape=[jax.ShapeDtypeStruct(o[0], o[1]) for o in outs],
        input_output_aliases={} if into is None else {n_in: into[1]},
        compiler_params=_params(("parallel", "arbitrary")),
    )(*[a for a, _ in ins], *unread)


def _orow(n_rows, w, dt, rb, bw=None, col=lambda c: 0):
    return ((n_rows, w), dt, "row", (rb, bw or w), col)


def _oacc(r, w, bw=None, col=lambda c: 0):
    return ((r, w), F32, "acc", (r, bw or w), col)


def _csum(v):
    return jnp.sum(v, axis=0, keepdims=True)


def _rms(v):
    return lax.rsqrt(jnp.mean(v * v, axis=-1, keepdims=True) + EPS)


def _norm_bwd(xv, dh, w, scale):
    r = _rms(xv)
    xh = xv * r
    dxh = dh * (w * (1.0 + scale))
    dx = r * (dxh - xh * jnp.mean(dxh * xh, axis=-1, keepdims=True))
    t = dh * xh
    return dx, _csum(dh), _csum(t * w), _csum(t * (1.0 + scale))


def _rope_tables(pos_col, invf, s):
    def fn(c, i, pos, f):
        ang = pos.astype(F32) * f
        lane = lax.broadcasted_iota(jnp.int32, ang.shape, 1)
        sign = jnp.where((lane % ATT_HD) < ATT_HD // 2, -1.0, 1.0)
        return jnp.cos(ang), jnp.sin(ang) * sign

    rb = 512
    return _rowcall(fn, [_rows(pos_col, rb), _full(invf)], [_orow(s, 128, F32, rb), _orow(s, 128, F32, rb)],
                    n_rows=s, rb=rb, name="rope_tables")


def _swap_halves(t):
    n = t.shape[1]
    lane = lax.broadcasted_iota(jnp.int32, t.shape, 1)
    return jnp.where((lane % ATT_HD) < ATT_HD // 2, pltpu.roll(t, n - 32, 1), pltpu.roll(t, 32, 1))


def _rope_apply(t, cos, sin_signed, inverse):
    cw = jnp.concatenate([cos] * (t.shape[1] // 128), axis=1)
    sw = jnp.concatenate([sin_signed] * (t.shape[1] // 128), axis=1)
    if inverse:
        sw = -sw
    return t * cw + _swap_halves(t) * sw


DIL_ROWS = 512


def _to_dilated(scr, val, out_ref, r):
    if r == 1:
        out_ref[...] = val.astype(out_ref.dtype)
        return
    n = val.shape[0] // r
    for hh in range(2):
        scr[hh] = val[:, hh * 128:(hh + 1) * 128]
        for pr in range(r):
            out_ref[:, pr * 256 + hh * 128:pr * 256 + (hh + 1) * 128] = scr[hh, pl.ds(pr, n, stride=r), :].astype(out_ref.dtype)


def _from_dilated(scr, in_ref, r):
    if r == 1:
        return in_ref[...].astype(F32)
    n = in_ref.shape[0]
    for hh in range(2):
        for pr in range(r):
            scr[hh, pl.ds(pr, n, stride=r), :] = in_ref[:, pr * 256 + hh * 128:pr * 256 + (hh + 1) * 128].astype(F32)
    return jnp.concatenate([scr[0], scr[1]], axis=1)


def _dil_spec(r):
    return pl.BlockSpec((DIL_ROWS // r, r * 256), lambda i: (i, 0))


def _dil_shape(s, r, dt):
    return jax.ShapeDtypeStruct((s // r, r * 256), dt)


_DIL_SCRATCH = [pltpu.VMEM((2, DIL_ROWS, 128), F32)]
_RS = tuple(r for _, r in ATT_GROUPS)


def _rope_fwd(p, cos_t, sin_t, s):
    def body(*refs):
        ins, cs, sn, outs, scr = refs[:9], refs[9][...], refs[10][...], refs[11:20], refs[20]
        for t in range(3):
            for g, r in enumerate(_RS):
                val = ins[3 * t + g][...].astype(F32)
                _to_dilated(scr, _rope_apply(val, cs, sn, False) if t < 2 else val, outs[3 * t + g], r)

    res = pl.pallas_call(
        body, name="rope", grid=(s // DIL_ROWS,),
        in_specs=[pl.BlockSpec((DIL_ROWS, 256), lambda i, c=base // 256 + g: (i, c)) for base in (P_AQ, P_AK, P_AV) for g in range(3)]
        + [pl.BlockSpec((DIL_ROWS, 128), lambda i: (i, 0))] * 2,
        out_specs=[_dil_spec(r) for _ in range(3) for r in _RS],
        out_shape=[_dil_shape(s, r, BF16) for _ in range(3) for r in _RS],
        scratch_shapes=_DIL_SCRATCH, compiler_params=_params(("parallel",)),
    )(*([p] * 9), cos_t, sin_t)
    return res[0:3], res[3:6], res[6:9]


def _attn_combine(att, s):
    def body(o0, o1, o2, l0, l1, l2, o_ref, lse_ref, od1, od2, ld1, ld2, scr):
        ov = [_from_dilated(scr, ref, r) for ref, r in zip((o0, o1, o2), _RS)]
        lv = [_from_dilated(scr, ref, r) for ref, r in zip((l0, l1, l2), _RS)]
        mx = jnp.maximum(jnp.maximum(lv[0], lv[1]), lv[2])
        ev = [jnp.exp(l - mx) for l in lv]
        z = ev[0] + ev[1] + ev[2]
        o = ((ev[0] * ov[0] + ev[1] * ov[1] + ev[2] * ov[2]) / z).astype(BF16)
        lse = mx + jnp.log(z)
        o_ref[...] = o
        lse_ref[...] = lse
        for ref, r in zip((od1, od2), _RS[1:]):
            _to_dilated(scr, o.astype(F32), ref, r)
        for ref, r in zip((ld1, ld2), _RS[1:]):
            _to_dilated(scr, lse, ref, r)

    return pl.pallas_call(
        body, name="attn_combine", grid=(s // DIL_ROWS,),
        in_specs=[_dil_spec(r) for r in _RS] * 2,
        out_specs=[_dil_spec(1)] * 2 + [_dil_spec(r) for r in _RS[1:]] * 2,
        out_shape=[_dil_shape(s, 1, BF16), _dil_shape(s, 1, F32)] + [_dil_shape(s, r, BF16) for r in _RS[1:]]
        + [_dil_shape(s, r, F32) for r in _RS[1:]],
        scratch_shapes=_DIL_SCRATCH, compiler_params=_params(("parallel",)),
    )(*[a[0] for a in att], *[a[1] for a in att])


def _dilate(t, s):
    def body(t_ref, o1, o2, scr):
        val = t_ref[...].astype(F32)
        for ref, r in zip((o1, o2), _RS[1:]):
            _to_dilated(scr, val, ref, r)

    return pl.pallas_call(
        body, name="attn_dilate", grid=(s // DIL_ROWS,), in_specs=[_dil_spec(1)], out_specs=[_dil_spec(r) for r in _RS[1:]],
        out_shape=[_dil_shape(s, r, t.dtype) for r in _RS[1:]], scratch_shapes=_DIL_SCRATCH, compiler_params=_params(("parallel",)),
    )(t)


def _rope_bwd(datt, d_glr, dp, cos_t, sin_t, s):
    tail = P_W - P_AQ

    def body(*refs):
        ins, cs, sn, glr, o_ref, scr = refs[:9], refs[9][...], refs[10][...], refs[11], refs[13], refs[14]
        for t in range(3):
            for g, r in enumerate(_RS):
                val = _from_dilated(scr, ins[3 * t + g], r)
                o_ref[:, t * ATT_W + g * 256:t * ATT_W + (g + 1) * 256] = (_rope_apply(val, cs, sn, True) if t < 2 else val).astype(BF16)
        o_ref[:, 3 * ATT_W:3 * ATT_W + 128] = glr[...]
        o_ref[:, 3 * ATT_W + 128:] = jnp.zeros((DIL_ROWS, tail - 3 * ATT_W - 128), BF16)

    return pl.pallas_call(
        body, name="rope_bwd", grid=(s // DIL_ROWS,),
        in_specs=[_dil_spec(r) for _ in range(3) for r in _RS] + [pl.BlockSpec((DIL_ROWS, 128), lambda i: (i, 0))] * 3
        + [pl.BlockSpec(memory_space=pl.ANY)],
        out_specs=pl.BlockSpec((DIL_ROWS, tail), lambda i: (i, P_AQ // tail)),
        out_shape=jax.ShapeDtypeStruct((s, P_W), BF16), input_output_aliases={12: 0},
        scratch_shapes=_DIL_SCRATCH, compiler_params=_params(("parallel",)),
    )(*[datt[g][t] for t in range(3) for g in range(3)], cos_t, sin_t, d_glr, dp)


def _tri_dot(tri, t):
    tb = tri.astype(BF16)
    hi = t.astype(BF16)
    r1 = t - hi.astype(F32)
    mid = r1.astype(BF16)
    lo = (r1 - mid.astype(F32)).astype(BF16)
    return _dg(tb, hi, 1, 0) + _dg(tb, mid, 1, 0) + _dg(tb, lo, 1, 0)


def _gla_decays(la_c, tri):
    b = _tri_dot(tri, la_c)
    row = lax.broadcasted_iota(jnp.int32, b.shape, 0)
    bmid = jnp.sum(jnp.where(row == GLA_CHUNK // 2 - 1, b, 0.0), axis=0, keepdims=True)
    blast = jnp.sum(jnp.where(row == GLA_CHUNK - 1, b, 0.0), axis=0, keepdims=True)
    return b, bmid, blast


def _gla_fwd(p, la, s, comm=()):
    tb, ch = GLA_BLOCK, GLA_CHUNK
    nb, nc = s // tb, tb // ch
    scale = GLA_DK ** -0.5
    c_ins, c_outs, c_alias, c_scratch = _carry(comm, 4, 2)

    def body(q_ref, k_ref, v_ref, la_ref, *rest):
        ci, (o_ref, st_ref) = rest[:len(c_ins)], rest[len(c_ins):len(c_ins) + 2]
        co, state = rest[len(c_ins) + 2:len(c_ins) + 2 + len(c_outs)], rest[len(c_ins) + 2 + len(c_outs)]
        step = pl.program_id(0)
        if comm:
            @pl.when(step == 0)
            def _():
                _comm_phase(comm, ci, co, rest[-2], rest[-1], True)

        _gla_fwd_step(q_ref, k_ref, v_ref, la_ref, o_ref, st_ref, state)
        if comm:
            @pl.when(step == nb - 1)
            def _():
                _comm_phase(comm, ci, co, rest[-2], rest[-1], False)

    def _gla_fwd_step(q_ref, k_ref, v_ref, la_ref, o_ref, st_ref, state):
        @pl.when(pl.program_id(0) == 0)
        def _():
            state[...] = jnp.zeros_like(state)

        ri = lax.broadcasted_iota(jnp.int32, (ch, ch), 0)
        ci = lax.broadcasted_iota(jnp.int32, (ch, ch), 1)
        causal = ci <= ri
        tri = causal.astype(F32)

        def chunk(c, carry):
            sl = pl.ds(pl.multiple_of(c * ch, ch), ch)
            b, bmid, blast = _gla_decays(la_ref[sl, :], tri)
            q = q_ref[sl, :].astype(F32) * scale
            k = k_ref[sl, :].astype(F32)
            v = v_ref[sl, :]
            qgt = (q * jnp.exp(b)).astype(BF16)
            qgn = (q * jnp.exp(b - bmid)).astype(BF16)
            kgn = (k * jnp.exp(bmid - b)).astype(BF16)
            kd = (k * jnp.exp(blast - b)).astype(BF16)
            dec = jnp.exp(blast)
            sts = [state[h] for h in range(GLA_H)]
            outs, news = [], []
            for h in range(GLA_H):
                hk, hv = slice(h * GLA_DK, (h + 1) * GLA_DK), slice(h * GLA_DV, (h + 1) * GLA_DV)
                a = jnp.where(causal, _dg(qgn[:, hk], kgn[:, hk], 1, 1), 0.0)
                outs.append(_dg(a.astype(BF16), v[:, hv], 1, 0) + _dg(qgt[:, hk], sts[h].astype(BF16), 1, 1))
                news.append(dec[:, hk] * sts[h] + _dg(v[:, hv], kd[:, hk], 0, 0))
            for h in range(GLA_H):
                st_ref[h, c] = sts[h]
                state[h] = news[h]
            o_ref[sl, :] = jnp.concatenate(outs, axis=1)
            return carry

        lax.fori_loop(0, nc, chunk, 0, unroll=2)

    hw = GLA_H * GLA_DK
    res = pl.pallas_call(
        body, name="gla_fwd", grid=(nb,),
        in_specs=[pl.BlockSpec((tb, hw), lambda t: (t, P_GQ // hw)),
                  pl.BlockSpec((tb, hw), lambda t: (t, P_GK // hw)),
                  pl.BlockSpec((tb, GLA_H * GLA_DV), lambda t: (t, P_GV // (GLA_H * GLA_DV))),
                  pl.BlockSpec((tb, hw), lambda t: (t, 0))] + [HBM] * len(c_ins),
        out_specs=[pl.BlockSpec((tb, GLA_H * GLA_DV), lambda t: (t, 0)),
                   pl.BlockSpec((GLA_H, nc, GLA_DV, GLA_DK), lambda t: (0, t, 0, 0))] + [HBM] * len(c_outs),
        out_shape=[jax.ShapeDtypeStruct((s, GLA_H * GLA_DV), F32),
                   jax.ShapeDtypeStruct((GLA_H, s // ch, GLA_DV, GLA_DK), F32)] + c_outs,
        scratch_shapes=[pltpu.VMEM((GLA_H, GLA_DV, GLA_DK), F32)] + c_scratch,
        input_output_aliases=c_alias,
        compiler_params=_params(("arbitrary",)),
    )(p, p, p, la, *c_ins)
    return res[0], res[1], _split_units(comm, res[2:])


def _gla_bwd(p, la, states, do, s, dp, comm=()):
    tb, ch = GLA_BLOCK, GLA_CHUNK
    nb, nc = s // tb, tb // ch
    scale = GLA_DK ** -0.5
    c_ins, c_outs, c_alias, c_scratch = _carry(comm, 7, 4)

    def body(q_ref, k_ref, v_ref, la_ref, st_ref, do_ref, dp_in, *rest):
        ci, outs = rest[:len(c_ins)], rest[len(c_ins):len(c_ins) + 4]
        co, dstate = rest[len(c_ins) + 4:len(c_ins) + 4 + len(c_outs)], rest[len(c_ins) + 4 + len(c_outs)]
        step = pl.program_id(0)
        if comm:
            @pl.when(step == 0)
            def _():
                _comm_phase(comm, ci, co, rest[-2], rest[-1], True)

        _gla_bwd_step(q_ref, k_ref, v_ref, la_ref, st_ref, do_ref, *outs, dstate)
        if comm:
            @pl.when(step == nb - 1)
            def _():
                _comm_phase(comm, ci, co, rest[-2], rest[-1], False)

    def _gla_bwd_step(q_ref, k_ref, v_ref, la_ref, st_ref, do_ref, dq_ref, dk_ref, dv_ref, dla_ref, dstate):
        @pl.when(pl.program_id(0) == 0)
        def _():
            dstate[...] = jnp.zeros_like(dstate)

        ri = lax.broadcasted_iota(jnp.int32, (ch, ch), 0)
        ci = lax.broadcasted_iota(jnp.int32, (ch, ch), 1)
        causal = ci <= ri
        tri = causal.astype(F32)
        tri_t = (ci >= ri).astype(F32)

        def chunk(cc, carry):
            c = nc - 1 - cc
            sl = pl.ds(pl.multiple_of(c * ch, ch), ch)
            b, bmid, blast = _gla_decays(la_ref[sl, :], tri)
            q = q_ref[sl, :].astype(F32) * scale
            k = k_ref[sl, :].astype(F32)
            v = v_ref[sl, :]
            e_b, e_qn, e_kn, e_kd = jnp.exp(b), jnp.exp(b - bmid), jnp.exp(bmid - b), jnp.exp(blast - b)
            dec = jnp.exp(blast)
            qgt, qgn, kgn, kd = q * e_b, q * e_qn, k * e_kn, k * e_kd
            qgt_b, qgn_b, kgn_b, kd_b = qgt.astype(BF16), qgn.astype(BF16), kgn.astype(BF16), kd.astype(BF16)
            do_b = do_ref[sl, :].astype(BF16)
            st0s = [st_ref[h, c] for h in range(GLA_H)]
            dsts = [dstate[h] for h in range(GLA_H)]
            dqgn, dqgt, dkgn, dkd, dvs, ddec, news = [], [], [], [], [], [], []
            for h in range(GLA_H):
                hk, hv = slice(h * GLA_DK, (h + 1) * GLA_DK), slice(h * GLA_DV, (h + 1) * GLA_DV)
                dst_b = dsts[h].astype(BF16)
                a = jnp.where(causal, _dg(qgn_b[:, hk], kgn_b[:, hk], 1, 1), 0.0).astype(BF16)
                da = jnp.where(causal, _dg(do_b[:, hv], v[:, hv], 1, 1), 0.0).astype(BF16)
                dqgn.append(_dg(da, kgn_b[:, hk], 1, 0))
                dqgt.append(_dg(do_b[:, hv], st0s[h].astype(BF16), 1, 0))
                dkgn.append(_dg(da, qgn_b[:, hk], 0, 0))
                dvs.append(_dg(a, do_b[:, hv], 0, 0) + _dg(kd_b[:, hk], dst_b, 1, 1))
                dkd.append(_dg(v[:, hv], dst_b, 1, 0))
                ddec.append(jnp.sum(st0s[h] * dsts[h], axis=0, keepdims=True))
                news.append(dec[:, hk] * dsts[h] + _dg(do_b[:, hv], qgt_b[:, hk], 0, 0))
            for h in range(GLA_H):
                dstate[h] = news[h]
            cat = lambda parts: jnp.concatenate(parts, axis=1)
            dqgn, dqgt, dkgn, dkd, ddec = cat(dqgn), cat(dqgt), cat(dkgn), cat(dkd), cat(ddec)
            dq_ref[sl, :] = (scale * (dqgn * e_qn + dqgt * e_b)).astype(dq_ref.dtype)
            dk_ref[sl, :] = (dkgn * e_kn + dkd * e_kd).astype(dk_ref.dtype)
            dv_ref[sl, :] = cat(dvs).astype(dv_ref.dtype)
            db = dqgn * qgn + dqgt * qgt - dkgn * kgn - dkd * kd
            extra = jnp.sum(dkd * kd, axis=0, keepdims=True) + ddec * dec
            dla_ref[sl, :] = _tri_dot(tri_t, db) + extra
            return carry

        lax.fori_loop(0, nc, chunk, 0, unroll=2)

    rev = lambda t: nb - 1 - t
    hw, vw = GLA_H * GLA_DK, GLA_H * GLA_DV
    res = pl.pallas_call(
        body, name="gla_bwd", grid=(nb,),
        in_specs=[pl.BlockSpec((tb, hw), lambda t: (rev(t), P_GQ // hw)),
                  pl.BlockSpec((tb, hw), lambda t: (rev(t), P_GK // hw)),
                  pl.BlockSpec((tb, vw), lambda t: (rev(t), P_GV // vw)),
                  pl.BlockSpec((tb, hw), lambda t: (rev(t), 0)),
                  pl.BlockSpec((GLA_H, nc, GLA_DV, GLA_DK), lambda t: (0, rev(t), 0, 0)),
                  pl.BlockSpec((tb, vw), lambda t: (rev(t), 0)), pl.BlockSpec(memory_space=pl.ANY)] + [HBM] * len(c_ins),
        out_specs=[pl.BlockSpec((tb, hw), lambda t: (rev(t), 0)),
                   pl.BlockSpec((tb, hw), lambda t: (rev(t), 0)),
                   pl.BlockSpec((tb, vw), lambda t: (rev(t), P_GV // vw)),
                   pl.BlockSpec((tb, hw), lambda t: (rev(t), 0))] + [HBM] * len(c_outs),
        out_shape=[jax.ShapeDtypeStruct((s, hw), BF16),
                   jax.ShapeDtypeStruct((s, hw), BF16),
                   jax.ShapeDtypeStruct((s, P_W), BF16),
                   jax.ShapeDtypeStruct((s, hw), F32)] + c_outs,
        scratch_shapes=[pltpu.VMEM((GLA_H, GLA_DV, GLA_DK), F32)] + c_scratch,
        input_output_aliases={6: 2, **c_alias},
        compiler_params=_params(("arbitrary",)),
    )(p, p, p, la, states, do, dp, *c_ins)
    return res[0], res[1], res[2], res[3], _split_units(comm, res[4:])


def _head_masks():
    lane = lax.broadcasted_iota(jnp.int32, (1, 4 * ATT_HD), 1)
    return [(lane >= h * ATT_HD) & (lane < (h + 1) * ATT_HD) for h in range(4)]


def _attn_fwd(qv, kv, pv, g, r, s):
    ln = s // r
    nblk = ln // ATT_BLK
    qcol = lambda pr: pr
    vcol = qcol
    prev = lambda n: jnp.maximum(n - 1, 0)

    def body(q_ref, kp_ref, kc_ref, vp_ref, vc_ref, o_ref, lse_ref):
        has_prev = pl.program_id(1) > 0
        ri = lax.broadcasted_iota(jnp.int32, (ATT_BLK, ATT_BLK), 0)
        ci = lax.broadcasted_iota(jnp.int32, (ATT_BLK, ATT_BLK), 1)
        m_cur = ci <= ri
        m_prev = (ci >= ri) & has_prev
        q, kp, kc, vp, vc = q_ref[...], kp_ref[...], kc_ref[...], vp_ref[...], vc_ref[...]
        o = jnp.zeros((ATT_BLK, 256), F32)
        lse = jnp.zeros((ATT_BLK, 256), F32)
        for hm in _head_masks():
            qm = jnp.where(hm, q, jnp.zeros_like(q))
            sc = jnp.where(m_cur, _dg(qm, kc, 1, 1) * 0.125, NEG)
            sp = jnp.where(m_prev, _dg(qm, kp, 1, 1) * 0.125, NEG)
            mx = jnp.maximum(jnp.max(sc, axis=1, keepdims=True), jnp.max(sp, axis=1, keepdims=True))
            pc, pp = jnp.exp(sc - mx), jnp.exp(sp - mx)
            den = jnp.sum(pc, axis=1, keepdims=True) + jnp.sum(pp, axis=1, keepdims=True)
            oh = (_dg(pc.astype(BF16), vc, 1, 0) + _dg(pp.astype(BF16), vp, 1, 0)) / den
            o = jnp.where(hm, oh, o)
            lse = jnp.where(hm, mx + jnp.log(den), lse)
        o_ref[...] = o.astype(o_ref.dtype)
        lse_ref[...] = lse

    blk = (ATT_BLK, 256)
    o, lse = pl.pallas_call(
        body, name=f"attn_fwd_{g}", grid=(r, nblk),
        in_specs=[pl.BlockSpec(blk, lambda pr, n: (n, qcol(pr))),
                  pl.BlockSpec(blk, lambda pr, n: (prev(n), qcol(pr))),
                  pl.BlockSpec(blk, lambda pr, n: (n, qcol(pr))),
                  pl.BlockSpec(blk, lambda pr, n: (prev(n), vcol(pr))),
                  pl.BlockSpec(blk, lambda pr, n: (n, vcol(pr)))],
        out_specs=[pl.BlockSpec(blk, lambda pr, n: (n, pr)), pl.BlockSpec(blk, lambda pr, n: (n, pr))],
        out_shape=[jax.ShapeDtypeStruct((ln, r * 256), BF16), jax.ShapeDtypeStruct((ln, r * 256), F32)],
        compiler_params=_params(("parallel", "parallel")),
    )(qv, kv, kv, pv, pv)
    return o, lse


def _attn_bwd(qv, kv, pv, dov, ov, lv, g, r, s):
    ln = s // r
    nblk = ln // ATT_BLK
    qcol = lambda pr: pr
    vcol = qcol
    prev = lambda n: jnp.maximum(n - 1, 0)
    nxt = lambda n: jnp.minimum(n + 1, nblk - 1)

    def body(qc_ref, qn_ref, kp_ref, kc_ref, vp_ref, vc_ref, doc_ref, don_ref, oc_ref, on_ref, lc_ref, ln_ref,
             dq_ref, dk_ref, dv_ref):
        n = pl.program_id(1)
        has_prev, has_next = n > 0, n < nblk - 1
        ri = lax.broadcasted_iota(jnp.int32, (ATT_BLK, ATT_BLK), 0)
        ci = lax.broadcasted_iota(jnp.int32, (ATT_BLK, ATT_BLK), 1)
        m_cur = ci <= ri
        m_prev = (ci >= ri) & has_prev
        m_next = (ci >= ri) & has_next
        qc, qn, kp, kc, vp, vc = qc_ref[...], qn_ref[...], kp_ref[...], kc_ref[...], vp_ref[...], vc_ref[...]
        doc, don = doc_ref[...], don_ref[...]
        pc_full = doc.astype(F32) * oc_ref[...].astype(F32)
        pn_full = don.astype(F32) * on_ref[...].astype(F32)
        lc, lnx = lc_ref[...], ln_ref[...]
        dq = jnp.zeros((ATT_BLK, 256), F32)
        dk = jnp.zeros((ATT_BLK, 256), F32)
        dv = jnp.zeros((ATT_BLK, 256), F32)
        zb = jnp.zeros_like(qc)
        for hm in _head_masks():
            qcm, qnm = jnp.where(hm, qc, zb), jnp.where(hm, qn, zb)
            docm, donm = jnp.where(hm, doc, zb), jnp.where(hm, don, zb)
            lse_c = jnp.max(jnp.where(hm, lc, NEG), axis=1, keepdims=True)
            lse_n = jnp.max(jnp.where(hm, lnx, NEG), axis=1, keepdims=True)
            del_c = jnp.sum(jnp.where(hm, pc_full, 0.0), axis=1, keepdims=True)
            del_n = jnp.sum(jnp.where(hm, pn_full, 0.0), axis=1, keepdims=True)
            pr_ = jnp.where(m_cur, jnp.exp(_dg(qcm, kc, 1, 1) * 0.125 - lse_c), 0.0)
            ds = (pr_ * (_dg(docm, vc, 1, 1) - del_c) * 0.125).astype(BF16)
            dqh = _dg(ds, kc, 1, 0)
            dkh = _dg(ds, qc, 0, 0)
            dvh = _dg(pr_.astype(BF16), doc, 0, 0)
            pr_ = jnp.where(m_prev, jnp.exp(_dg(qcm, kp, 1, 1) * 0.125 - lse_c), 0.0)
            ds = (pr_ * (_dg(docm, vp, 1, 1) - del_c) * 0.125).astype(BF16)
            dqh = dqh + _dg(ds, kp, 1, 0)
            pr_ = jnp.where(m_next, jnp.exp(_dg(qnm, kc, 1, 1) * 0.125 - lse_n), 0.0)
            ds = (pr_ * (_dg(donm, vc, 1, 1) - del_n) * 0.125).astype(BF16)
            dkh = dkh + _dg(ds, qn, 0, 0)
            dvh = dvh + _dg(pr_.astype(BF16), don, 0, 0)
            dq = jnp.where(hm, dqh, dq)
            dk = jnp.where(hm, dkh, dk)
            dv = jnp.where(hm, dvh, dv)
        dq_ref[...] = dq.astype(dq_ref.dtype)
        dk_ref[...] = dk.astype(dk_ref.dtype)
        dv_ref[...] = dv.astype(dv_ref.dtype)

    blk = (ATT_BLK, 256)
    cur = lambda col: pl.BlockSpec(blk, lambda pr, n: (n, col(pr)))
    prv = lambda col: pl.BlockSpec(blk, lambda pr, n: (prev(n), col(pr)))
    nx = lambda col: pl.BlockSpec(blk, lambda pr, n: (nxt(n), col(pr)))
    own = lambda pr: pr
    outs = pl.pallas_call(
        body, name=f"attn_bwd_{g}", grid=(r, nblk),
        in_specs=[cur(qcol), nx(qcol), prv(qcol), cur(qcol), prv(vcol), cur(vcol),
                  cur(own), nx(own), cur(own), nx(own), cur(own), nx(own)],
        out_specs=[cur(own), cur(own), cur(own)],
        out_shape=[jax.ShapeDtypeStruct((ln, r * 256), BF16)] * 3,
        compiler_params=_params(("parallel", "parallel")),
    )(qv, qv, kv, kv, pv, pv, dov, dov, ov, ov, lv, lv)
    return outs


def _gelu_parts(gv):
    cdf = 0.5 * (1.0 + lax.erf(gv * (2.0 ** -0.5)))
    pdf = jnp.exp(-0.5 * gv * gv) * (1.0 / math.sqrt(2.0 * math.pi))
    return cdf, pdf


def _pick_row(t, k):
    row = lax.broadcasted_iota(jnp.int32, t.shape, 0)
    return jnp.sum(jnp.where(row == k, t, 0.0), axis=0, keepdims=True)


def _shift_rows(u, halo, n):
    row = lax.broadcasted_iota(jnp.int32, u.shape, 0)
    out = pltpu.roll(u, n, 0)
    for k in range(n):
        out = jnp.where(row == k, _pick_row(halo, 16 - n + k), out)
    return out


def _shift_rows_up(u, halo, n):
    rb = u.shape[0]
    row = lax.broadcasted_iota(jnp.int32, u.shape, 0)
    out = pltpu.roll(u, rb - n, 0)
    for k in range(n):
        out = jnp.where(row == rb - n + k, _pick_row(halo, k), out)
    return out


def _conv(u, halo, cw, cb):
    return cb + _pick_row(cw, 0) * _shift_rows(u, halo, 2) + _pick_row(cw, 1) * _shift_rows(u, halo, 1) + _pick_row(cw, 2) * u


def _local_step(x, mod, pos_col, target, sm, w_sh, g0, chip, core):
    s = x.shape[0]
    shift1, scale1, gate1, shift2, scale2, gate2 = [mod[i:i + 1, :] for i in range(6)]
    rb = 512
    chip1 = chip.reshape(1)

    def f_norm1(c, i, xv, nw, sc, sh):
        return ((xv * _rms(xv) * nw) * (1.0 + sc) + sh,)

    (h,) = _rowcall(f_norm1, [_rows(x, rb), _full(sm["n1w"]), _full(scale1), _full(shift1)],
                    [_orow(s, D, BF16, rb)], n_rows=s, rb=rb, name="norm1")
    own = lambda got, i: lax.dynamic_update_slice(got, w_sh[i], (chip, 0, 0))
    invf = jnp.tile(ROPE_THETA ** (-jnp.arange(ATT_HD // 2, dtype=F32) / (ATT_HD // 2)), 4).reshape(1, 128)
    cos_t, sin_t = _rope_tables(pos_col, invf, s)
    _, got0 = _unit_wait(*g0[:4], after=[h, cos_t, sin_t], name="gather_w_in_wait")
    [got0] = _comm_call("gather_w_in_d2d", [_u_gather_d2d(got0, (0,))])
    u_g1 = _u_gather_ici(w_sh, (1, 2, 3, 4, 5))
    g1 = _unit_start(u_g1, "gather_weights_start", after=got0)
    w = dict(win=_win_assemble(got0[0], w_sh[0], after=g1[3:]))
    p = _mm(h, w["win"], "in_proj", tm=2048, tn=1536)

    def f_gla_pre(c, i, glr, w2, gb):
        z = _dg(glr, w2.astype(BF16), 1, 0) + gb
        return ((jnp.minimum(z, 0.0) - jnp.log(1.0 + jnp.exp(-jnp.abs(z)))) * (1.0 / GLA_TAU),)

    (la,) = _rowcall(f_gla_pre, [_rows(p, rb, 128, P_LR // 128), _full(sm["w2"]), _full(sm["gb"])],
                     [_orow(s, 512, F32, rb)], n_rows=s, rb=rb, name="gla_pre")
    o_gla, states, _ = _gla_fwd(p, la, s)
    _, got = _unit_wait(u_g1, *g1[:3], after=[o_gla], name="gather_weights_wait")
    [got123] = _comm_call("gather_weights_d2d", [_u_gather_d2d(got[:3], (1, 2, 3))])
    got45 = got[3:]
    w.update(wgb=own(got123[0], 1).reshape(1024, D), wab=_cols_join(own(got123[1], 2)), wout=own(got123[2], 3).reshape(D, D))

    def f_gla_post(c, i, ov, gnw, gr):
        on = jnp.concatenate([ov[:, k * 256:(k + 1) * 256] * _rms(ov[:, k * 256:(k + 1) * 256]) * gnw
                              for k in range(GLA_H)], axis=1)
        g = gr.astype(F32)
        return (on * (g * _sigmoid(g)),)

    (og,) = _rowcall(f_gla_post, [_rows(o_gla, rb), _full(sm["gnw"]), _rows(p, rb, 1024, P_GR // 1024)],
                     [_orow(s, 1024, BF16, rb)], n_rows=s, rb=rb, name="gla_post")
    y_gla = _mm(og, w["wgb"], "gla_branch")

    q_d, k_d, v_d = _rope_fwd(p, cos_t, sin_t, s)
    att = [_attn_fwd(q_d[g], k_d[g], v_d[g], g, r, s) for g, r in enumerate(_RS)]
    o_att, lse, o_d1, o_d2, lse_d1, lse_d2 = _attn_combine(att, s)
    y_att = _mm(o_att, w["wab"], "attn_branch")

    def f_merge(c, i, ma, mb, yg, ya):
        return (_sigmoid(ma.astype(F32)) * yg.astype(F32) + _sigmoid(mb.astype(F32)) * ya.astype(F32),)

    (mixed,) = _rowcall(f_merge, [_rows(p, rb, D, P_MA // D), _rows(p, rb, D, P_MB // D), _rows(y_gla, rb), _rows(y_att, rb)],
                        [_orow(s, D, BF16, rb)], n_rows=s, rb=rb, name="merge")
    z1, [got45] = _mm(mixed, w["wout"], "out_proj", comm=[_u_gather_d2d(got45, (4, 5))])
    w.update(wup=own(got45[0], 4), wdown=own(got45[1], 5).reshape(D_FF, D))

    def f_norm2(c, i, xv, z, g1, nw, sc, sh):
        x1 = xv + g1 * z.astype(F32)
        return (x1, (x1 * _rms(x1) * nw) * (1.0 + sc) + sh)

    x1, h2 = _rowcall(f_norm2, [_rows(x, rb), _rows(z1, rb), _full(gate1), _full(sm["n2w"]), _full(scale2), _full(shift2)],
                      [_orow(s, D, F32, rb), _orow(s, D, BF16, rb)], n_rows=s, rb=rb, name="norm2")
    u = _mm(h2, w["wup"], "up_proj", tm=2048, b_shards=True)

    cwid = 2 * W_UP_SH

    def f_ffn(c, i, uv, hl, cw, cb):
        uc = _conv(uv.astype(F32), hl.astype(F32) * (i > 0).astype(F32), cw, cb)
        val, gt = uc[:, :W_UP_SH], uc[:, W_UP_SH:]
        cdf, _ = _gelu_parts(gt)
        return (gt * cdf * val,)

    ccol = lambda c: c
    rw = 128
    (hidden,) = _rowcall(f_ffn, [_rows(u, rw, cwid, ccol), _halo(u, rw, 16, cwid, ccol, True),
                                 _full(sm["cw"], cwid, ccol), _full(sm["cb"], cwid, ccol)],
                         [_orow(s, D_FF, BF16, rw, W_UP_SH, ccol)], n_rows=s, rb=rw, name="conv_geglu", ncol=2)
    z2 = _mm(hidden, w["wdown"], "down_proj", tk=D_FF)

    def f_final(c, i, x1v, z, g2, fw, tgt):
        x2 = x1v + g2 * z.astype(F32)
        r = _rms(x2)
        xh = x2 * r
        e = xh * fw - tgt
        loss = 0.5 * jnp.sum(jnp.mean(e * e, axis=-1, keepdims=True), axis=0, keepdims=True)
        dy = e * (1.0 / D)
        dxh = dy * fw
        dx2 = r * (dxh - xh * jnp.mean(dxh * xh, axis=-1, keepdims=True))
        return (loss, dx2, dx2 * g2, _csum(dy * xh), _csum(dx2 * z.astype(F32)))

    loss, dx2, dz2, d_fnw, d_gate2 = _rowcall(
        f_final, [_rows(x1, rb), _rows(z2, rb), _full(gate2), _full(sm["fnw"]), _rows(target, rb)],
        [_oacc(1, 1), _orow(s, D, F32, rb), _orow(s, D, BF16, rb), _oacc(1, D), _oacc(1, D)],
        n_rows=s, rb=rb, name="final_loss")
    d_hidden = _mm(dz2, w["wdown"], "down_proj_dx", tb=True, tn=1408)
    g_wdown = _mm(hidden, dz2, "down_proj_dw", ta=True, out_dtype=F32, tm=1408, tn=1024, tk=2048)

    def f_ffn_bwd(c, i, uv, hl, dh, cw, cb):
        uf = uv.astype(F32)
        hf = hl.astype(F32) * (i > 0).astype(F32)
        u1, u2 = _shift_rows(uf, hf, 1), _shift_rows(uf, hf, 2)
        uc = cb + _pick_row(cw, 0) * u2 + _pick_row(cw, 1) * u1 + _pick_row(cw, 2) * uf
        val, gt = uc[:, :W_UP_SH], uc[:, W_UP_SH:]
        cdf, pdf = _gelu_parts(gt)
        dhf = dh.astype(F32)
        duc = jnp.concatenate([dhf * (gt * cdf), dhf * val * (cdf + gt * pdf)], axis=1)
        dcw = jnp.concatenate([_csum(duc * u2), _csum(duc * u1), _csum(duc * uf)], axis=0)
        return (duc, _csum(duc), dcw)

    duc, d_cb, d_cw = _rowcall(
        f_ffn_bwd, [_rows(u, rw, cwid, ccol), _halo(u, rw, 16, cwid, ccol, True), _rows(d_hidden, rw, W_UP_SH, ccol),
                    _full(sm["cw"], cwid, ccol), _full(sm["cb"], cwid, ccol)],
        [_orow(s, 2 * D_FF, BF16, rw, cwid, ccol), _oacc(1, 2 * D_FF, cwid, ccol), _oacc(3, 2 * D_FF, cwid, ccol)],
        n_rows=s, rb=rw, name="conv_geglu_bwd", ncol=2)

    def f_conv_t(c, i, dv, hl, cw):
        df = dv.astype(F32)
        hf = hl.astype(F32) * (i < s // rw - 1).astype(F32)
        return (_pick_row(cw, 2) * df + _pick_row(cw, 1) * _shift_rows_up(df, hf, 1) + _pick_row(cw, 0) * _shift_rows_up(df, hf, 2),)

    (du,) = _rowcall(f_conv_t, [_rows(duc, rw, cwid, ccol), _halo(duc, rw, 16, cwid, ccol, False), _full(sm["cw"], cwid, ccol)],
                     [_orow(s, 2 * D_FF, BF16, rw, cwid, ccol)], n_rows=s, rb=rw, name="conv_transpose", ncol=2)
    g_wup = _mm(h2, du, "up_proj_dw", ta=True, out_dtype=F32, tm=1024, tk=2048, o_shards=True)
    gs45 = [g_wup, g_wdown.reshape(4, W_DOWN_SH, 1024)]
    d_h2, [land45] = _mm(du, w["wup"], "up_proj_dx", tb=True, tm=2048, b_shards=True, comm=[_u_pair_send(gs45, (4, 5))])
    ts45 = [_pair_add(g, ld, core, "grad_pair_add_" + BIG[i]) for g, ld, i in zip(gs45, land45, (4, 5))]
    u_ex4 = _u_chip_exchange(ts45[:1])
    ex4 = _unit_start(u_ex4, "grad_exchange_w_up_start")

    def f_norm2_bwd(c, i, x1v, dh, dxr, z, nw, sc, g1):
        dxn, dsh, dsc, dnw = _norm_bwd(x1v, dh.astype(F32), nw, sc)
        dx1 = dxr + dxn
        return (dx1, dx1 * g1, dsh, dsc, dnw, _csum(dx1 * z.astype(F32)))

    dx1, dz1, d_shift2, d_scale2, d_n2w, d_gate1 = _rowcall(
        f_norm2_bwd, [_rows(x1, rb), _rows(d_h2, rb), _rows(dx2, rb), _rows(z1, rb), _full(sm["n2w"]), _full(scale2), _full(gate1)],
        [_orow(s, D, F32, rb), _orow(s, D, BF16, rb), _oacc(1, D), _oacc(1, D), _oacc(1, D), _oacc(1, D)],
        n_rows=s, rb=rb, name="norm2_bwd", after=ex4[3:])
    d_mixed = _mm(dz1, w["wout"], "out_proj_dx", tb=True)
    g_wout = _mm(mixed, dz1, "out_proj_dw", ta=True, out_dtype=F32, tk=2048)

    def f_merge_bwd(c, i, dm, ma, mb, yg, ya):
        dmf, ygf, yaf = dm.astype(F32), yg.astype(F32), ya.astype(F32)
        sa, sb = _sigmoid(ma.astype(F32)), _sigmoid(mb.astype(F32))
        return (dmf * sa, dmf * sb, jnp.concatenate([dmf * ygf * sa * (1.0 - sa), dmf * yaf * sb * (1.0 - sb)], axis=1))

    dy_gla, dy_att, dp = _rowcall(
        f_merge_bwd, [_rows(d_mixed, rb), _rows(p, rb, D, P_MA // D), _rows(p, rb, D, P_MB // D), _rows(y_gla, rb), _rows(y_att, rb)],
        [_orow(s, D, BF16, rb)] * 2 + [_orow(s, P_W, BF16, rb, 2 * D, lambda c: P_MA // (2 * D))], n_rows=s, rb=rb, name="merge_bwd")
    d_og = _mm(dy_gla, w["wgb"], "gla_branch_dx", tb=True)
    g_wgb = _mm(og, dy_gla, "gla_branch_dw", ta=True, out_dtype=F32, tk=2048)
    d_oatt = _mm(dy_att, w["wab"], "attn_branch_dx", tb=True)
    g_wab = _mm(o_att, dy_att, "attn_branch_dw", ta=True, out_dtype=F32, tk=2048)

    def f_gla_post_bwd(c, i, ov, gnw, gr, dog):
        g = gr.astype(F32)
        sg = _sigmoid(g)
        silu = g * sg
        dof = dog.astype(F32)
        don = dof * silu
        on_parts, do_parts, dgn = [], [], jnp.zeros((1, 256), F32)
        for k in range(GLA_H):
            oh = ov[:, k * 256:(k + 1) * 256]
            dh = don[:, k * 256:(k + 1) * 256]
            r = _rms(oh)
            xh = oh * r
            dgn = dgn + _csum(dh * xh)
            dxh = dh * gnw
            do_parts.append(r * (dxh - xh * jnp.mean(dxh * xh, axis=-1, keepdims=True)))
            on_parts.append(xh * gnw)
        on = jnp.concatenate(on_parts, axis=1)
        dgr = dof * on * (sg * (1.0 + g * (1.0 - sg)))
        return (jnp.concatenate(do_parts, axis=1), dgr, dgn)

    do_gla, dp, d_gnw = _rowcall(
        f_gla_post_bwd, [_rows(o_gla, rb), _full(sm["gnw"]), _rows(p, rb, 1024, P_GR // 1024), _rows(d_og, rb)],
        [_orow(s, 1024, F32, rb), _orow(s, P_W, BF16, rb, 1024, lambda c: P_GR // 1024), _oacc(1, 256)],
        n_rows=s, rb=rb, name="gla_post_bwd", into=(dp, 1))
    gs123 = [g_wgb.reshape(4, 256, 1024), _cols_split(g_wab), g_wout.reshape(4, 256, 1024)]
    d_gq, d_gk, dp, d_la, [land123] = _gla_bwd(p, la, states, do_gla, s, dp, comm=[_u_pair_send(gs123, (1, 2, 3))])
    ts123 = [_pair_add(g, ld, core, "grad_pair_add_" + BIG[i]) for g, ld, i in zip(gs123, land123, (1, 2, 3))]

    def f_gla_pre_bwd(c, i, lav, dlav, glr, w2):
        dz = dlav * (1.0 / GLA_TAU) * (1.0 - jnp.exp(GLA_TAU * lav))
        dzb = dz.astype(BF16)
        return (_dg(dzb, w2.astype(BF16), 1, 1), _csum(dz), _dg(glr, dzb, 0, 0))

    d_glr, d_gb, d_w2 = _rowcall(
        f_gla_pre_bwd, [_rows(la, rb), _rows(d_la, rb), _rows(p, rb, 128, P_LR // 128), _full(sm["w2"])],
        [_orow(s, 128, BF16, rb), _oacc(1, 512), _oacc(128, 512)], n_rows=s, rb=rb, name="gla_pre_bwd")

    do_d = [d_oatt] + list(_dilate(d_oatt, s))
    datt = [_attn_bwd(q_d[g], k_d[g], v_d[g], do_d[g], (o_att, o_d1, o_d2)[g], (lse, lse_d1, lse_d2)[g], g, r, s)
            for g, r in enumerate(_RS)]
    dp = _rope_bwd(datt, d_glr, dp, cos_t, sin_t, s)
    dp = lax.dynamic_update_slice(dp, jnp.concatenate([d_gq, d_gk], axis=1), (0, P_GQ))
    [t4], [r4] = _unit_wait(u_ex4, *ex4[:3], after=[dp], name="grad_exchange_w_up_wait")
    half4 = [_chip_sum(t4, r4, chip1, "grad_chip_sum_w_up")]
    g_win, [r1235, oth4] = _mm(h, dp, "in_proj_dw", ta=True, out_dtype=F32, tm=1024, tn=1536, tk=2048,
                               comm=[_u_chip_exchange(ts123 + ts45[1:]), _u_pair_join(half4)])
    half1235 = [_chip_sum(t, r, chip1, "grad_chip_sum_" + BIG[i]) for t, r, i in zip(ts123 + ts45[1:], r1235, (1, 2, 3, 5))]
    gs0 = [_win_split(g_win)]
    d_h, [land0, oth1235] = _mm(dp, w["win"], "in_proj_dx", tb=True, tk=3840,
                                comm=[_u_pair_send(gs0, (0,)), _u_pair_join(half1235)])
    half123, half45 = half1235[:3], half4 + half1235[3:]
    oth123, oth45 = oth1235[:3], oth4 + oth1235[3:]
    ts0 = _pair_add(gs0[0], land0[0], core, "grad_pair_add_w_in")

    def f_norm1_bwd(c, i, xv, dh, dxr, nw, sc):
        dxn, dsh, dsc, dnw = _norm_bwd(xv, dh.astype(F32), nw, sc)
        return (dxr + dxn, dsh, dsc, dnw)

    grad_x, d_shift1, d_scale1, d_n1w = _rowcall(
        f_norm1_bwd, [_rows(x, rb), _rows(d_h, rb), _rows(dx1, rb), _full(sm["n1w"]), _full(scale1)],
        [_orow(s, D, F32, rb), _oacc(1, D), _oacc(1, D), _oacc(1, D)], n_rows=s, rb=rb, name="norm1_bwd")

    dmod = jnp.concatenate([d_shift1, d_scale1, d_gate1, d_shift2, d_scale2, d_gate2], axis=1)
    small = dict(dmod=dmod, n1w=d_n1w, gb=d_gb, gnw=d_gnw, n2w=d_n2w, cb=d_cb, fnw=d_fnw, w2=d_w2, cw=d_cw)
    return loss, grad_x, half123 + half45, oth123 + oth45, small, ts0


def _win_pieces():
    runs = [(P_GV, 1024, 2048), (P_MA, 5392, 2048), (P_GQ, 0, 1024), (P_AQ, 3088, 2304), (P_LR, 3072, GLA_LR)]
    out = []
    for kc, rc, ln in runs:
        while ln > 0:
            step = min(ln, W_IN_SH - rc % W_IN_SH)
            out.append((kc, rc, step))
            kc, rc, ln = kc + step, rc + step, ln - step
    return out


def _win_assemble(shards, own, after=()):
    rb = 256

    def body(s_ref, own_ref, *rest):
        o_ref = rest[-1]
        x, y, _ = _me()
        o_ref[:, W_IN:] = jnp.zeros((rb, P_W - W_IN), o_ref.dtype)
        for kc, rc, ln in _win_pieces():
            sh, lo = rc // W_IN_SH, rc % W_IN_SH
            o_ref[:, kc:kc + ln] = jnp.where(2 * x + y == sh, own_ref[0, :, lo:lo + ln], s_ref[sh, :, lo:lo + ln])

    return pl.pallas_call(
        body, name="w_in_assemble", grid=(D // rb,),
        in_specs=[pl.BlockSpec((4, rb, W_IN_SH), lambda i: (0, i, 0)), pl.BlockSpec((1, rb, W_IN_SH), lambda i: (0, i, 0))]
        + [pl.BlockSpec(memory_space=pl.ANY)] * len(after),
        out_specs=pl.BlockSpec((rb, P_W), lambda i: (i, 0)),
        out_shape=jax.ShapeDtypeStruct((D, P_W), shards.dtype), compiler_params=_params(("parallel",)),
    )(shards, own, *after)


def _win_split(g):
    rb = 256

    def body(g_ref, o_ref):
        for kc, rc, ln in _win_pieces():
            o_ref[rc // W_IN_SH, :, rc % W_IN_SH:rc % W_IN_SH + ln] = g_ref[:, kc:kc + ln]

    return pl.pallas_call(
        body, name="w_in_grad_split", grid=(D // rb,),
        in_specs=[pl.BlockSpec((rb, P_W), lambda i: (i, 0))], out_specs=pl.BlockSpec((4, rb, W_IN_SH), lambda i: (0, i, 0)),
        out_shape=jax.ShapeDtypeStruct((4, D, W_IN_SH), g.dtype), compiler_params=_params(("parallel",)),
    )(g)


def _ff_to_kernel(a):
    h = W_UP_SH
    return jnp.concatenate([a[:, 0:h], a[:, D_FF:D_FF + h], a[:, h:D_FF], a[:, D_FF + h:]], axis=1)


def _ff_from_kernel(a):
    h = W_UP_SH
    return jnp.concatenate([a[:, 0:h], a[:, 2 * h:3 * h], a[:, h:2 * h], a[:, 3 * h:]], axis=1)


BIG = ("w_in", "w_gla_branch", "w_attn_branch", "w_out", "w_up", "w_down")
SH_SHAPES = ((1024, W_IN_SH), (256, 1024), (256, 256), (256, 1024), (1024, W_UP_SH), (W_DOWN_SH, 1024))
N_BIG = len(BIG)


def _cols_join(t):
    return jnp.concatenate([t[k] for k in range(4)], axis=1)


def _cols_split(t):
    cols = t.shape[1] // 4
    return jnp.stack([t[:, k * cols:(k + 1) * cols] for k in range(4)])


def _me():
    return lax.axis_index("x"), lax.axis_index("y"), lax.axis_index("c")


HBM = pl.BlockSpec(memory_space=pltpu.HBM)
VMEM_SPEC = pl.BlockSpec(memory_space=pltpu.VMEM)


def _allgather8(xs, name):
    rows = xs.shape[0]

    def body(x_ref, out_ref, send_sems, recv_sems, local_sem):
        x, y, c = _me()
        me = 4 * x + 2 * y + c
        mine = pltpu.make_async_copy(x_ref, out_ref.at[me], local_sem)
        mine.start()
        flips = [(k >> 2 & 1, k >> 1 & 1, k & 1) for k in range(1, 8)]

        def peer(f):
            return (jnp.where(f[0] == 1, 1 - x, x), jnp.where(f[1] == 1, 1 - y, y), jnp.where(f[2] == 1, 1 - c, c))

        sends = []
        for k, f in enumerate(flips):
            cp = pltpu.make_async_remote_copy(src_ref=x_ref, dst_ref=out_ref.at[me], send_sem=send_sems.at[k],
                                              recv_sem=recv_sems.at[k], device_id=peer(f), device_id_type=MESH)
            cp.start()
            sends.append(cp)
        for k, f in enumerate(flips):
            px, py, pc = peer(f)
            pltpu.make_async_remote_copy(src_ref=x_ref, dst_ref=out_ref.at[4 * px + 2 * py + pc], send_sem=send_sems.at[k],
                                         recv_sem=recv_sems.at[k], device_id=peer(f), device_id_type=MESH).wait_recv()
        for cp in sends:
            cp.wait_send()
        mine.wait()

    return pl.pallas_call(
        body, name=name, out_shape=jax.ShapeDtypeStruct((8, rows, 128), F32),
        in_specs=[VMEM_SPEC], out_specs=VMEM_SPEC,
        scratch_shapes=[pltpu.SemaphoreType.DMA((7,)), pltpu.SemaphoreType.DMA((7,)), pltpu.SemaphoreType.DMA],
        compiler_params=pltpu.CompilerParams(vmem_limit_bytes=VMEM_LIMIT),
    )(xs)


def _half_rows(i, cc, unit):
    rows = SH_SHAPES[i][0] // 2
    return pl.ds(pl.multiple_of(cc * rows, unit), rows)


def _rc(src, dst, sems, to):
    return pltpu.make_async_remote_copy(src_ref=src, dst_ref=dst, send_sem=sems[0], recv_sem=sems[1], device_id=to, device_id_type=MESH)


def _other_chips(x, y):
    return [(1 - x, y), (x, 1 - y), (1 - x, 1 - y)]


def _u_gather_ici(w_sh, idxs):
    def copies(ins, outs, sem):
        x, y, c = _me()
        res = []
        for j, (px, py) in enumerate(_other_chips(x, y)):
            for n, i in enumerate(idxs):
                src = ins[n].at[0, _half_rows(i, c, 16)]
                res.append((_rc(src, outs[n].at[2 * x + y, _half_rows(i, c, 16)], sem(j * len(idxs) + n), (px, py, c)),
                            _rc(src, outs[n].at[2 * px + py, _half_rows(i, c, 16)], sem(j * len(idxs) + n), (px, py, c))))
        return res

    return dict(ins=[w_sh[i] for i in idxs], outs=[jax.ShapeDtypeStruct((4,) + SH_SHAPES[i], BF16) for i in idxs],
                nsem=3 * len(idxs), alias={}, copies=copies)


def _u_gather_d2d(got, idxs):
    def copies(ins, outs, sem):
        x, y, c = _me()
        res = []
        for j, (px, py) in enumerate(_other_chips(x, y)):
            for n, i in enumerate(idxs):
                src = ins[n].at[2 * px + py, _half_rows(i, c, 16)]
                res.append((_rc(src, outs[n].at[2 * px + py, _half_rows(i, c, 16)], sem(j * len(idxs) + n), (x, y, 1 - c)),
                            _rc(src, outs[n].at[2 * px + py, _half_rows(i, 1 - c, 16)], sem(j * len(idxs) + n), (x, y, 1 - c))))
        return res

    return dict(ins=list(got), outs=[jax.ShapeDtypeStruct(g.shape, g.dtype) for g in got], nsem=3 * len(idxs),
                alias={n: n for n in range(len(idxs))}, copies=copies)


def _u_pair_send(gs, idxs):
    def copies(ins, outs, sem):
        x, y, c = _me()
        res = []
        for n, i in enumerate(idxs):
            for sh in range(4):
                cp = _rc(ins[n].at[sh, _half_rows(i, 1 - c, 8)], outs[n].at[sh], sem(4 * n + sh), (x, y, 1 - c))
                res.append((cp, cp))
        return res

    return dict(ins=list(gs), outs=[jax.ShapeDtypeStruct((4, SH_SHAPES[i][0] // 2, SH_SHAPES[i][1]), F32) for i in idxs],
                nsem=4 * len(idxs), alias={}, copies=copies)


def _u_chip_exchange(ts):
    def copies(ins, outs, sem):
        x, y, c = _me()
        res = []
        for j, (px, py) in enumerate(_other_chips(x, y)):
            for n in range(len(ts)):
                cp = _rc(ins[n].at[2 * px + py], outs[n].at[j], sem(j * len(ts) + n), (px, py, c))
                res.append((cp, cp))
        return res

    return dict(ins=list(ts), outs=[jax.ShapeDtypeStruct((3,) + t.shape[1:], t.dtype) for t in ts], nsem=3 * len(ts),
                alias={}, copies=copies)


def _u_pair_join(hs):
    def copies(ins, outs, sem):
        x, y, c = _me()
        res = []
        for n in range(len(hs)):
            cp = _rc(ins[n], outs[n], sem(n), (x, y, 1 - c))
            res.append((cp, cp))
        return res

    return dict(ins=list(hs), outs=[jax.ShapeDtypeStruct(h.shape, h.dtype) for h in hs], nsem=len(hs), alias={}, copies=copies)


def _comm_phase(units, ci, co, send_sems, recv_sems, start):
    ii = oo = off = 0
    for u in units:
        ni, no = len(u["ins"]), len(u["outs"])
        for st, arrival in u["copies"](ci[ii:ii + ni], co[oo:oo + no], lambda k, off=off: (send_sems.at[off + k], recv_sems.at[off + k])):
            if start:
                st.start()
            else:
                st.wait_send()
                arrival.wait_recv()
        ii, oo, off = ii + ni, oo + no, off + u["nsem"]


def _carry(units, n_in, n_out):
    ins = [a for u in units for a in u["ins"]]
    outs = [o for u in units for o in u["outs"]]
    alias, ii, oo = {}, 0, 0
    for u in units:
        for a, b in u["alias"].items():
            alias[n_in + ii + a] = n_out + oo + b
        ii, oo = ii + len(u["ins"]), oo + len(u["outs"])
    nsem = sum(u["nsem"] for u in units)
    scratch = [pltpu.SemaphoreType.DMA((nsem,)), pltpu.SemaphoreType.DMA((nsem,))] if units else []
    return ins, outs, alias, scratch


def _split_units(units, res):
    out, oo = [], 0
    for u in units:
        out.append(list(res[oo:oo + len(u["outs"])]))
        oo += len(u["outs"])
    return out


def _comm_call(name, units):
    ins, outs, alias, scratch = _carry(units, 0, 0)

    def body(*refs):
        ci, co = refs[:len(ins)], refs[len(ins):len(ins) + len(outs)]
        _comm_phase(units, ci, co, refs[-2], refs[-1], True)
        _comm_phase(units, ci, co, refs[-2], refs[-1], False)

    res = pl.pallas_call(body, name=name, out_shape=outs, in_specs=[HBM] * len(ins), out_specs=[HBM] * len(outs),
                         scratch_shapes=scratch, input_output_aliases=alias)(*ins)
    return _split_units(units, res)


SEM = pl.BlockSpec(memory_space=pltpu.SEMAPHORE)
EFFECT = pltpu.SideEffectType.DATAFLOW_SIDE_EFFECTING


def _unit_start(unit, name, after=()):
    bufs = list(unit["ins"]) + [lax.empty(o.shape, o.dtype) for o in unit["outs"]]
    n_i, n_b, ns = len(unit["ins"]), len(bufs), unit["nsem"]

    def body(*refs):
        send_sems, recv_sems = refs[n_b + len(after)], refs[n_b + len(after) + 1]
        for st, _ in unit["copies"](refs[:n_i], refs[n_i:n_b], lambda k: (send_sems.at[k], recv_sems.at[k])):
            st.start()
        refs[-1][...] = jnp.zeros_like(refs[-1])

    res = pl.pallas_call(
        body, name=name,
        out_shape=[pltpu.SemaphoreType.DMA((ns,)), pltpu.SemaphoreType.DMA((ns,))] + [pltpu.HBM(b.shape, b.dtype) for b in bufs]
        + [jax.ShapeDtypeStruct((8, 128), F32)],
        in_specs=[HBM] * n_b + [pl.BlockSpec(memory_space=pl.ANY)] * len(after), out_specs=[SEM, SEM] + [HBM] * n_b + [VMEM_SPEC],
        input_output_aliases={i: 2 + i for i in range(n_b)},
        compiler_params=pltpu.CompilerParams(has_side_effects=EFFECT),
    )(*[pltpu.with_memory_space_constraint(b, pltpu.HBM) for b in bufs], *after)
    return res[0], res[1], list(res[2:2 + n_b]), res[-1]


def _unit_wait(unit, send_sems, recv_sems, bufs, after, name):
    n_i, n_b = len(unit["ins"]), len(bufs)

    def body(*refs):
        ss, rs = refs[n_b], refs[n_b + 1]
        for st, arrival in unit["copies"](refs[:n_i], refs[n_i:n_b], lambda k: (ss.at[k], rs.at[k])):
            st.wait_send()
            arrival.wait_recv()

    res = pl.pallas_call(
        body, name=name, out_shape=[pltpu.HBM(b.shape, b.dtype) for b in bufs],
        in_specs=[HBM] * n_b + [SEM, SEM] + [pl.BlockSpec(memory_space=pl.ANY)] * len(after), out_specs=[HBM] * n_b,
        input_output_aliases={i: i for i in range(n_b)}, compiler_params=pltpu.CompilerParams(has_side_effects=EFFECT),
    )(*bufs, send_sems, recv_sems, *after)
    return list(res[:n_i]), list(res[n_i:])


def _pair_add(g, land, core, name):
    _, rows, cols = g.shape
    half = rows // 2
    rb = _tile(half, 512, 16)
    nb = half // rb

    def body(c_ref, g_ref, l_ref, o_ref):
        o_ref[...] = (g_ref[...] + l_ref[...]).astype(BF16)

    return pl.pallas_call(
        body, name=name,
        grid_spec=pltpu.PrefetchScalarGridSpec(
            num_scalar_prefetch=1, grid=(4, nb),
            in_specs=[pl.BlockSpec((1, rb, cols), lambda s, i, c_ref: (s, c_ref[0] * nb + i, 0)),
                      pl.BlockSpec((1, rb, cols), lambda s, i, c_ref: (s, i, 0))],
            out_specs=pl.BlockSpec((1, rb, cols), lambda s, i, c_ref: (s, i, 0))),
        out_shape=jax.ShapeDtypeStruct((4, half, cols), BF16),
        compiler_params=_params(("parallel", "parallel")),
    )(core, g, land)


def _chip_sum(t, r, chip, name):
    _, half, cols = t.shape
    rb = _tile(half, 512, 16)

    def body(s_ref, t_ref, r_ref, o_ref):
        o_ref[...] = ((t_ref[0].astype(F32) + r_ref[0].astype(F32)) + r_ref[1].astype(F32)) + r_ref[2].astype(F32)

    return pl.pallas_call(
        body, name=name,
        grid_spec=pltpu.PrefetchScalarGridSpec(
            num_scalar_prefetch=1, grid=(half // rb,),
            in_specs=[pl.BlockSpec((1, rb, cols), lambda i, s_ref: (s_ref[0], i, 0)),
                      pl.BlockSpec((3, rb, cols), lambda i, s_ref: (0, i, 0))],
            out_specs=pl.BlockSpec((rb, cols), lambda i, s_ref: (i, 0))),
        out_shape=jax.ShapeDtypeStruct((half, cols), F32),
        compiler_params=_params(("parallel",)),
    )(chip, t, r)


def _adam_math(wv, gv, mv, vv):
    mn = ADAM_B1 * mv + (1.0 - ADAM_B1) * gv
    vn = ADAM_B2 * vv + (1.0 - ADAM_B2) * (gv * gv)
    m_hat = mn / (1.0 - ADAM_B1 ** ADAM_STEP)
    v_hat = vn / (1.0 - ADAM_B2 ** ADAM_STEP)
    return -ADAM_LR * (m_hat / (jnp.sqrt(v_hat) + ADAM_EPS) + ADAM_WD * wv), mn, vn


def _adamw_halves(wt, mt, vt, mine, theirs, core, name):
    _, rows, cols = wt.shape
    half = rows // 2
    rb = _tile(half, 256, 8)
    nb = half // rb

    def body(c_ref, w_ref, m_ref, v_ref, a_ref, b_ref, g_ref, d_ref, mo_ref, vo_ref):
        gv = jnp.where(pl.program_id(0) == c_ref[0], a_ref[...], b_ref[...])
        dl, mn, vn = _adam_math(w_ref[...], gv, m_ref[...], v_ref[...])
        g_ref[...] = gv
        d_ref[...] = dl
        mo_ref[...] = mn
        vo_ref[...] = vn

    full = pl.BlockSpec((None, rb, cols), lambda hf, i, c_ref: (0, hf * nb + i, 0))
    part = pl.BlockSpec((rb, cols), lambda hf, i, c_ref: (i, 0))
    return pl.pallas_call(
        body, name=name,
        grid_spec=pltpu.PrefetchScalarGridSpec(num_scalar_prefetch=1, grid=(2, nb), in_specs=[full, full, full, part, part],
                                               out_specs=[full] * 4),
        out_shape=[jax.ShapeDtypeStruct((1, rows, cols), F32)] * 4,
        compiler_params=_params(("parallel", "parallel")),
    )(core, wt, mt, vt, mine, theirs)


SG_REP = 144
SG_LOSS = 136
SG_W2, SG_CW = SG_REP, SG_REP + 4 * 16
SG_ROWS = SG_CW + 4 * 40
SP_ROWS = SG_REP + 16 + 40


def _mod_shard(c_all, ada_w_sh):
    def body(c_ref, w_ref, o_ref):
        cv = c_ref[...]
        o_ref[...] = _dg((cv * _sigmoid(cv)).astype(BF16), w_ref[...].astype(BF16), 1, 0)

    return pl.pallas_call(body, name="mod_shard", out_shape=jax.ShapeDtypeStruct((8, 1536), F32),
                          in_specs=[VMEM_SPEC, VMEM_SPEC], out_specs=VMEM_SPEC,
                          compiler_params=pltpu.CompilerParams(vmem_limit_bytes=VMEM_LIMIT))(c_all, ada_w_sh)


def _mod_select(mod_all, ada_b4):
    def body(m_ref, b_ref, o_ref):
        x, y, c = _me()
        me = 4 * x + 2 * y + c
        for sh in range(4):
            o_ref[sh] = m_ref[2 * sh, me] + b_ref[sh]

    return pl.pallas_call(body, name="mod_select", out_shape=jax.ShapeDtypeStruct((4, 12, 128), F32),
                          in_specs=[VMEM_SPEC, VMEM_SPEC], out_specs=VMEM_SPEC)(mod_all, ada_b4)


def _small_reduce(sg_all):
    def body(g_ref, o_ref):
        x, y, c = _me()
        s_me = 2 * x + y
        w2_rows = pl.ds(pl.multiple_of(SG_W2 + 16 * s_me, 8), 16)
        cw_rows = pl.ds(pl.multiple_of(SG_CW + 40 * s_me, 8), 40)
        a = g_ref[0, 0:SG_REP, :]
        b = g_ref[0, w2_rows, :]
        d = g_ref[0, cw_rows, :]
        for dev in range(1, 8):
            a = a + g_ref[dev, 0:SG_REP, :]
            b = b + g_ref[dev, w2_rows, :]
            d = d + g_ref[dev, cw_rows, :]
        o_ref[0:SG_REP, :] = a
        o_ref[SG_REP:SG_REP + 16, :] = b
        o_ref[SG_REP + 16:SP_ROWS, :] = d

    return pl.pallas_call(body, name="small_grad_reduce", out_shape=jax.ShapeDtypeStruct((SP_ROWS, 128), F32),
                          in_specs=[VMEM_SPEC], out_specs=VMEM_SPEC)(sg_all)


def _ada_grad(dmod_all, c_bc):
    def body(g_ref, c_ref, o_ref):
        x, y, c = _me()
        s_me = 2 * x + y
        for k in range(12):
            acc = jnp.zeros((D, 128), F32)
            for b in range(8):
                cv = c_ref[b]
                acc = acc + (cv * _sigmoid(cv)) * g_ref[s_me, k, b:b + 1, :]
            o_ref[:, k * 128:(k + 1) * 128] = acc

    return pl.pallas_call(body, name="ada_w_grad", out_shape=jax.ShapeDtypeStruct((D, 1536), F32),
                          in_specs=[VMEM_SPEC, VMEM_SPEC], out_specs=VMEM_SPEC,
                          compiler_params=pltpu.CompilerParams(vmem_limit_bytes=VMEM_LIMIT))(dmod_all, c_bc)


def _adamw(wt, g, m, v, name):
    rows, cols = wt.shape
    rb = _tile(rows, 256, 8)

    def fn(c, i, wv, gv, mv, vv):
        return _adam_math(wv, gv, mv, vv)

    return _rowcall(fn, [_rows(t, rb) for t in (wt, g, m, v)], [_orow(rows, cols, F32, rb)] * 3,
                    n_rows=rows, rb=rb, name=name)


def _pad_rows(t, rows):
    flat = t.reshape(-1)
    return jnp.pad(flat, (0, rows * 128 - flat.shape[0])).reshape(rows, 128)


SP_LAYOUT = (("ada_b", 48), ("norm1_w", 8), ("gla_gate_b", 8), ("gla_norm_w", 8), ("norm2_w", 8), ("conv_b", 48),
             ("final_norm_w", 8), (None, 8), ("gla_gate_w2", 16), ("conv_w", 40))


def _pack_small(d):
    return jnp.concatenate([jnp.zeros((rows, 128), F32) if n is None else _pad_rows(d[n].astype(F32), rows)
                            for n, rows in SP_LAYOUT], axis=0)


def _unpack_small(pk, shapes):
    out, off = {}, 0
    for n, rows in SP_LAYOUT:
        if n is not None:
            shp = shapes[n]
            out[n] = pk[off:off + rows].reshape(-1)[:math.prod(shp)].reshape(shp)
        off += rows
    return out


def kernel(x, c, positions, ada_w, ada_b, norm1_w, w_in, gla_gate_w2, gla_gate_b, gla_norm_w, w_gla_branch, w_attn_branch, w_out, norm2_w, w_up, conv_w, conv_b, w_down, final_norm_w, loss_target, m_ada_w, m_ada_b, m_norm1_w, m_w_in, m_gla_gate_w2, m_gla_gate_b, m_gla_norm_w, m_w_gla_branch, m_w_attn_branch, m_w_out, m_norm2_w, m_w_up, m_conv_w, m_conv_b, m_w_down, m_final_norm_w, v_ada_w, v_ada_b, v_norm1_w, v_w_in, v_gla_gate_w2, v_gla_gate_b, v_gla_norm_w, v_w_gla_branch, v_w_attn_branch, v_w_out, v_norm2_w, v_w_up, v_conv_w, v_conv_b, v_w_down, v_final_norm_w):
    s = x.shape[1]
    names = ("ada_w", "ada_b", "norm1_w", "w_in", "gla_gate_w2", "gla_gate_b", "gla_norm_w", "w_gla_branch", "w_attn_branch",
             "w_out", "norm2_w", "w_up", "conv_w", "conv_b", "w_down", "final_norm_w")
    wts = dict(zip(names, (ada_w, ada_b, norm1_w, w_in, gla_gate_w2, gla_gate_b, gla_norm_w, w_gla_branch, w_attn_branch,
                           w_out, norm2_w, w_up, conv_w, conv_b, w_down, final_norm_w)))
    ms = dict(zip(names, (m_ada_w, m_ada_b, m_norm1_w, m_w_in, m_gla_gate_w2, m_gla_gate_b, m_gla_norm_w, m_w_gla_branch,
                          m_w_attn_branch, m_w_out, m_norm2_w, m_w_up, m_conv_w, m_conv_b, m_w_down, m_final_norm_w)))
    vs = dict(zip(names, (v_ada_w, v_ada_b, v_norm1_w, v_w_in, v_gla_gate_w2, v_gla_gate_b, v_gla_norm_w, v_w_gla_branch,
                          v_w_attn_branch, v_w_out, v_norm2_w, v_w_up, v_conv_w, v_conv_b, v_w_down, v_final_norm_w)))

    pk0 = jnp.concatenate([_pad_rows(c, 8), _pad_rows(gla_gate_w2, 16), _pad_rows(conv_w, 40)], axis=0)
    sm_all = _allgather8(pk0, "gather_small")
    c_all = sm_all[:, 0:8, :].reshape(8, D)
    w2_full = sm_all[0::2, 8:24, :].transpose(1, 0, 2).reshape(GLA_LR, 512)
    cw_full = sm_all[0::2, 24:64, :].reshape(4, 40 * 128)[:, :3 * W_UP_SH].reshape(4, 3, W_UP_SH).transpose(1, 0, 2).reshape(3, 2 * D_FF)

    mod_sh = _mod_shard(c_all, ada_w[0])
    mod_all = _allgather8(mod_sh.reshape(96, 128), "gather_mod")

    w_sh = [wts[n].astype(BF16) for n in BIG]
    u_g0 = _u_gather_ici(w_sh, (0,))
    g0 = (u_g0,) + _unit_start(u_g0, "gather_w_in_start", after=[mod_all])
    mod = _mod_select(mod_all.reshape(8, 8, 12, 128) + g0[4][0, 0], ada_b.reshape(4, 12, 128)).reshape(6, D)

    core = lax.axis_index("c").astype(jnp.int32).reshape(1)
    chip = (2 * lax.axis_index("x") + lax.axis_index("y")).astype(jnp.int32)
    sm = dict(n1w=norm1_w, n2w=norm2_w, fnw=final_norm_w.reshape(1, D), gnw=gla_norm_w, gb=gla_gate_b,
              w2=jnp.pad(w2_full, ((0, 128 - GLA_LR), (0, 0))), cw=_ff_to_kernel(cw_full), cb=_ff_to_kernel(conv_b))
    loss, grad_x, halves, others, small, ts0 = _local_step(x[0], mod, positions.reshape(s, 1), loss_target[0], sm, w_sh,
                                                               g0, chip, core)

    dcw = _ff_from_kernel(small["cw"]).reshape(3, 4, W_UP_SH).transpose(1, 0, 2)
    dw2 = small["w2"][:GLA_LR].reshape(GLA_LR, 4, 128).transpose(1, 0, 2)
    sg = jnp.concatenate(
        [_pad_rows(small["dmod"], 48), _pad_rows(small["n1w"], 8), _pad_rows(small["gb"], 8), _pad_rows(small["gnw"], 8),
         _pad_rows(small["n2w"], 8), _pad_rows(_ff_from_kernel(small["cb"]), 48), _pad_rows(small["fnw"], 8), _pad_rows(loss, 8)]
        + [_pad_rows(dw2[k], 16) for k in range(4)] + [_pad_rows(dcw[k], 40) for k in range(4)], axis=0)
    sg_all = _allgather8(sg, "gather_small_grads")
    u_ex = _u_chip_exchange([ts0])
    pending = (u_ex,) + _unit_start(u_ex, "grad_exchange_w_in_start", after=[sg_all])
    sg_all = sg_all + pending[4][0, 0]
    g_small_pk = _small_reduce(sg_all)
    dmod_all = sg_all[:, 0:48, :].reshape(8, 4, 12, 128).transpose(1, 2, 0, 3)
    g_ada_w = _ada_grad(dmod_all, jnp.broadcast_to(c_all[:, :, None], (8, D, 128)))

    shapes = {n: wts[n].shape for n in names}
    g_small = _unpack_small(g_small_pk, shapes)
    grads = {"ada_w": g_ada_w.reshape(1, D, 1536), **g_small}
    deltas, new_m, new_v = {}, {}, {}
    for n, mine, theirs in zip(BIG[1:], halves, others):
        grads[n], deltas[n], new_m[n], new_v[n] = _adamw_halves(wts[n], ms[n], vs[n], mine, theirs, core, "adamw_" + n)
    shp = ada_w.shape
    d_, m_, v_ = _adamw(ada_w[0], g_ada_w, m_ada_w[0], v_ada_w[0], "adamw_ada_w")
    deltas["ada_w"], new_m["ada_w"], new_v["ada_w"] = d_.reshape(shp), m_.reshape(shp), v_.reshape(shp)
    d_, m_, v_ = _adamw(_pack_small(wts), g_small_pk, _pack_small(ms), _pack_small(vs), "adamw_small")
    for dst, pk in ((deltas, d_), (new_m, m_), (new_v, v_)):
        dst.update(_unpack_small(pk, shapes))

    [t0], [r0] = _unit_wait(*pending[:4], after=[d_, deltas["ada_w"], deltas["w_up"], deltas["w_down"]], name="grad_exchange_w_in_wait")
    half0 = _chip_sum(t0, r0, chip.reshape(1), "grad_chip_sum_w_in")
    [[oth0]] = _comm_call("grad_join_w_in", [_u_pair_join([half0])])
    grads["w_in"], deltas["w_in"], new_m["w_in"], new_v["w_in"] = _adamw_halves(w_in, m_w_in, v_w_in, half0, oth0, core, "adamw_w_in")

    return (g_small_pk[SG_LOSS, 0], grad_x.reshape(1, s, D), *[grads[n] for n in names], *[deltas[n] for n in names],
            *[new_m[n] for n in names], *[new_v[n] for n in names])
```

```python
import math

import jax
import jax.numpy as jnp
from jax import lax
from jax.experimental import pallas as pl
from jax.experimental.pallas import tpu as pltpu

F32, BF16 = jnp.float32, jnp.bfloat16
MESH = pl.DeviceIdType.MESH

D = 1024
EPS = 1e-6
GLA_H, GLA_DK, GLA_DV, GLA_LR = 4, 128, 256, 16
GLA_TAU = 16.0
GLA_CHUNK = 64
GLA_BLOCK = 512
ATT_GROUPS = ((128, 1), (512, 4), (2048, 16))
ATT_BLK = 128
ATT_HD = 64
ATT_W = 768
D_FF = 2816
ROPE_THETA = 10000.0
P_W = 7680
P_GV, P_GR, P_MA, P_MB, P_GQ, P_GK, P_AQ, P_AK, P_AV, P_LR = 0, 1024, 2048, 3072, 4096, 4608, 5120, 5888, 6656, 7424
W_IN = 7440
W_IN_SH, W_UP_SH, W_DOWN_SH = 1860, 1408, 704
VMEM_LIMIT = 56 * 1024 * 1024
ADAM_LR, ADAM_B1, ADAM_B2, ADAM_EPS, ADAM_WD, ADAM_STEP = 0.001, 0.9, 0.999, 1e-08, 0.01, 10
NEG = -1e30


def _tile(n, target, unit=128):
    best = None
    for t in range(unit, min(n, target) + 1, unit):
        if n % t == 0:
            best = t
    return best or n


def _params(sem):
    return pltpu.CompilerParams(dimension_semantics=sem, vmem_limit_bytes=VMEM_LIMIT)


def _dg(a, b, ca, cb):
    return lax.dot_general(a, b, (((ca,), (cb,)), ((), ())), preferred_element_type=F32)


def _sigmoid(v):
    return 1.0 / (1.0 + jnp.exp(-v))


def _ff_block(j):
    return (j % 2) * 2 + j // 2


def _mm(a, b, name, *, ta=False, tb=False, out_dtype=BF16, tm=1024, tn=1536, tk=1024, n_outer=True, comm=(),
        b_shards=False, o_shards=False):
    m = a.shape[1] if ta else a.shape[0]
    k = a.shape[0] if ta else a.shape[1]
    if b_shards:
        n = b.shape[1] if tb else 4 * W_UP_SH
        tn, tk = (tn, W_UP_SH) if tb else (W_UP_SH, tk)
    else:
        n = b.shape[0] if tb else b.shape[1]
    if o_shards:
        tn = W_UP_SH
    tm, tn, tk = _tile(m, tm), _tile(n, tn), _tile(k, tk)
    nm, nn, nk = m // tm, n // tn, k // tk
    in_out = out_dtype == F32
    c_ins, c_outs, c_alias, c_scratch = _carry(comm, 2, 1)

    def body(a_ref, b_ref, *rest):
        ci, o_ref, co = rest[:len(c_ins)], rest[len(c_ins)], rest[len(c_ins) + 1:len(c_ins) + 1 + len(c_outs)]
        scr = rest[len(c_ins) + 1 + len(c_outs):]
        kk = pl.program_id(2)
        if comm:
            step = (pl.program_id(0) * (nm if n_outer else nn) + pl.program_id(1)) * nk + kk

            @pl.when(step == 0)
            def _():
                _comm_phase(comm, ci, co, scr[-2], scr[-1], True)

        _mm_step(a_ref, b_ref, o_ref, scr, kk)
        if comm:
            @pl.when(step == nm * nn * nk - 1)
            def _():
                _comm_phase(comm, ci, co, scr[-2], scr[-1], False)

    def _mm_step(a_ref, b_ref, o_ref, scr, kk):
        p = _dg(a_ref[...].astype(BF16), b_ref[...].astype(BF16), 0 if ta else 1, 1 if tb else 0)
        if nk == 1:
            o_ref[...] = p.astype(o_ref.dtype)
        else:
            acc = o_ref if in_out else scr[0]

            @pl.when(kk == 0)
            def _():
                acc[...] = p

            @pl.when(kk > 0)
            def _():
                acc[...] += p

            if not in_out:
                @pl.when(kk == nk - 1)
                def _():
                    o_ref[...] = acc[...].astype(o_ref.dtype)

    if n_outer:
        ij = lambda g0, g1: (g1, g0)
        grid = (nn, nm, nk)
    else:
        ij = lambda g0, g1: (g0, g1)
        grid = (nm, nn, nk)
    a_map = (lambda g0, g1, kk: (kk, ij(g0, g1)[0])) if ta else (lambda g0, g1, kk: (ij(g0, g1)[0], kk))
    if b_shards and tb:
        b_spec = pl.BlockSpec((None, tn, tk), lambda g0, g1, kk: (_ff_block(kk), ij(g0, g1)[1], 0))
    elif b_shards:
        b_spec = pl.BlockSpec((None, tk, tn), lambda g0, g1, kk: (_ff_block(ij(g0, g1)[1]), kk, 0))
    elif tb:
        b_spec = pl.BlockSpec((tn, tk), lambda g0, g1, kk: (ij(g0, g1)[1], kk))
    else:
        b_spec = pl.BlockSpec((tk, tn), lambda g0, g1, kk: (kk, ij(g0, g1)[1]))
    if o_shards:
        o_spec = pl.BlockSpec((None, tm, tn), lambda g0, g1, kk: (_ff_block(ij(g0, g1)[1]), ij(g0, g1)[0], 0))
        o_shape = jax.ShapeDtypeStruct((4, m, W_UP_SH), out_dtype)
    else:
        o_spec = pl.BlockSpec((tm, tn), lambda g0, g1, kk: ij(g0, g1))
        o_shape = jax.ShapeDtypeStruct((m, n), out_dtype)
    res = pl.pallas_call(
        body, name=name, grid=grid,
        in_specs=[pl.BlockSpec((tk, tm) if ta else (tm, tk), a_map), b_spec] + [HBM] * len(c_ins),
        out_specs=[o_spec] + [HBM] * len(c_outs),
        out_shape=[o_shape] + c_outs,
        scratch_shapes=([] if (in_out or nk == 1) else [pltpu.VMEM((tm, tn), F32)]) + c_scratch,
        input_output_aliases=c_alias,
        compiler_params=_params(("arbitrary",) * 3 if comm else ("parallel", "parallel", "arbitrary")),
    )(a, b, *c_ins)
    return (res[0], _split_units(comm, res[1:])) if comm else res[0]


def _rows(arr, rb, w=None, j=0):
    w = arr.shape[1] if w is None else w
    if callable(j):
        return arr, pl.BlockSpec((rb, w), lambda c, i: (i, j(c)))
    return arr, pl.BlockSpec((rb, w), lambda c, i: (i, j))


def _full(arr, w=None, j=0):
    w = arr.shape[1] if w is None else w
    if callable(j):
        return arr, pl.BlockSpec((arr.shape[0], w), lambda c, i: (0, j(c)))
    return arr, pl.BlockSpec((arr.shape[0], w), lambda c, i: (0, j))


def _halo(arr, rb, hb, w, j, before):
    per = rb // hb
    last = arr.shape[0] // hb - 1
    if before:
        rmap = lambda i: jnp.maximum(i * per - 1, 0)
    else:
        rmap = lambda i: jnp.minimum((i + 1) * per, last)
    return arr, pl.BlockSpec((hb, w), lambda c, i: (rmap(i), j(c) if callable(j) else j))


def _rowcall(fn, ins, outs, *, n_rows, rb, name, ncol=1, into=None, after=()):
    n_in = len(ins)
    nr = n_rows // rb
    unread = ([] if into is None else [into[0]]) + list(after)
    n_skip = len(unread)

    def body(*refs):
        c, i = pl.program_id(0), pl.program_id(1)
        res = fn(c, i, *[r[...] for r in refs[:n_in]])
        for val, spec, o_ref in zip(res, outs, refs[n_in + n_skip:]):
            if spec[2] == "row":
                o_ref[...] = val.astype(o_ref.dtype)
            else:
                @pl.when(i == 0)
                def _(o_ref=o_ref, val=val):
                    o_ref[...] = val.astype(o_ref.dtype)

                @pl.when(i > 0)
                def _(o_ref=o_ref, val=val):
                    o_ref[...] += val.astype(o_ref.dtype)

    out_specs = []
    for shape, dt, kind, block, col in outs:
        if kind == "row":
            out_specs.append(pl.BlockSpec(block, lambda c, i, col=col: (i, col(c))))
        else:
            out_specs.append(pl.BlockSpec(block, lambda c, i, col=col: (0, col(c))))
    return pl.pallas_call(
        body, name=name, grid=(ncol, nr),
        in_specs=[s for _, s in ins] + [pl.BlockSpec(memory_space=pl.ANY)] * n_skip, out_specs=out_specs,
        out_shape=[jax.ShapeDtypeStruct(o[0], o[1]) for o in outs],
        input_output_aliases={} if into is None else {n_in: into[1]},
        compiler_params=_params(("parallel", "arbitrary")),
    )(*[a for a, _ in ins], *unread)


def _orow(n_rows, w, dt, rb, bw=None, col=lambda c: 0):
    return ((n_rows, w), dt, "row", (rb, bw or w), col)


def _oacc(r, w, bw=None, col=lambda c: 0):
    return ((r, w), F32, "acc", (r, bw or w), col)


def _csum(v):
    return jnp.sum(v, axis=0, keepdims=True)


def _rms(v):
    return lax.rsqrt(jnp.mean(v * v, axis=-1, keepdims=True) + EPS)


def _norm_bwd(xv, dh, w, scale):
    r = _rms(xv)
    xh = xv * r
    dxh = dh * (w * (1.0 + scale))
    dx = r * (dxh - xh * jnp.mean(dxh * xh, axis=-1, keepdims=True))
    t = dh * xh
    return dx, _csum(dh), _csum(t * w), _csum(t * (1.0 + scale))


def _rope_tables(pos_col, invf, s):
    def fn(c, i, pos, f):
        ang = pos.astype(F32) * f
        lane = lax.broadcasted_iota(jnp.int32, ang.shape, 1)
        sign = jnp.where((lane % ATT_HD) < ATT_HD // 2, -1.0, 1.0)
        return jnp.cos(ang), jnp.sin(ang) * sign

    rb = 512
    return _rowcall(fn, [_rows(pos_col, rb), _full(invf)], [_orow(s, 128, F32, rb), _orow(s, 128, F32, rb)],
                    n_rows=s, rb=rb, name="rope_tables")


def _swap_halves(t):
    n = t.shape[1]
    lane = lax.broadcasted_iota(jnp.int32, t.shape, 1)
    return jnp.where((lane % ATT_HD) < ATT_HD // 2, pltpu.roll(t, n - 32, 1), pltpu.roll(t, 32, 1))


def _rope_apply(t, cos, sin_signed, inverse):
    cw = jnp.concatenate([cos] * (t.shape[1] // 128), axis=1)
    sw = jnp.concatenate([sin_signed] * (t.shape[1] // 128), axis=1)
    if inverse:
        sw = -sw
    return t * cw + _swap_halves(t) * sw


DIL_ROWS = 512


def _to_dilated(scr, val, out_ref, r):
    if r == 1:
        out_ref[...] = val.astype(out_ref.dtype)
        return
    n = val.shape[0] // r
    for hh in range(2):
        scr[hh] = val[:, hh * 128:(hh + 1) * 128]
        for pr in range(r):
            out_ref[:, pr * 256 + hh * 128:pr * 256 + (hh + 1) * 128] = scr[hh, pl.ds(pr, n, stride=r), :].astype(out_ref.dtype)


def _from_dilated(scr, in_ref, r):
    if r == 1:
        return in_ref[...].astype(F32)
    n = in_ref.shape[0]
    for hh in range(2):
        for pr in range(r):
            scr[hh, pl.ds(pr, n, stride=r), :] = in_ref[:, pr * 256 + hh * 128:pr * 256 + (hh + 1) * 128].astype(F32)
    return jnp.concatenate([scr[0], scr[1]], axis=1)


def _dil_spec(r):
    return pl.BlockSpec((DIL_ROWS // r, r * 256), lambda i: (i, 0))


def _dil_shape(s, r, dt):
    return jax.ShapeDtypeStruct((s // r, r * 256), dt)


_DIL_SCRATCH = [pltpu.VMEM((2, DIL_ROWS, 128), F32)]
_RS = tuple(r for _, r in ATT_GROUPS)


def _rope_fwd(p, cos_t, sin_t, s):
    def body(*refs):
        ins, cs, sn, outs, scr = refs[:9], refs[9][...], refs[10][...], refs[11:20], refs[20]
        for t in range(3):
            for g, r in enumerate(_RS):
                val = ins[3 * t + g][...].astype(F32)
                _to_dilated(scr, _rope_apply(val, cs, sn, False) if t < 2 else val, outs[3 * t + g], r)

    res = pl.pallas_call(
        body, name="rope", grid=(s // DIL_ROWS,),
        in_specs=[pl.BlockSpec((DIL_ROWS, 256), lambda i, c=base // 256 + g: (i, c)) for base in (P_AQ, P_AK, P_AV) for g in range(3)]
        + [pl.BlockSpec((DIL_ROWS, 128), lambda i: (i, 0))] * 2,
        out_specs=[_dil_spec(r) for _ in range(3) for r in _RS],
        out_shape=[_dil_shape(s, r, BF16) for _ in range(3) for r in _RS],
        scratch_shapes=_DIL_SCRATCH, compiler_params=_params(("parallel",)),
    )(*([p] * 9), cos_t, sin_t)
    return res[0:3], res[3:6], res[6:9]


def _attn_combine(att, s):
    def body(o0, o1, o2, l0, l1, l2, o_ref, lse_ref, od1, od2, ld1, ld2, scr):
        ov = [_from_dilated(scr, ref, r) for ref, r in zip((o0, o1, o2), _RS)]
        lv = [_from_dilated(scr, ref, r) for ref, r in zip((l0, l1, l2), _RS)]
        mx = jnp.maximum(jnp.maximum(lv[0], lv[1]), lv[2])
        ev = [jnp.exp(l - mx) for l in lv]
        z = ev[0] + ev[1] + ev[2]
        o = ((ev[0] * ov[0] + ev[1] * ov[1] + ev[2] * ov[2]) / z).astype(BF16)
        lse = mx + jnp.log(z)
        o_ref[...] = o
        lse_ref[...] = lse
        for ref, r in zip((od1, od2), _RS[1:]):
            _to_dilated(scr, o.astype(F32), ref, r)
        for ref, r in zip((ld1, ld2), _RS[1:]):
            _to_dilated(scr, lse, ref, r)

    return pl.pallas_call(
        body, name="attn_combine", grid=(s // DIL_ROWS,),
        in_specs=[_dil_spec(r) for r in _RS] * 2,
        out_specs=[_dil_spec(1)] * 2 + [_dil_spec(r) for r in _RS[1:]] * 2,
        out_shape=[_dil_shape(s, 1, BF16), _dil_shape(s, 1, F32)] + [_dil_shape(s, r, BF16) for r in _RS[1:]]
        + [_dil_shape(s, r, F32) for r in _RS[1:]],
        scratch_shapes=_DIL_SCRATCH, compiler_params=_params(("parallel",)),
    )(*[a[0] for a in att], *[a[1] for a in att])


def _dilate(t, s):
    def body(t_ref, o1, o2, scr):
        val = t_ref[...].astype(F32)
        for ref, r in zip((o1, o2), _RS[1:]):
            _to_dilated(scr, val, ref, r)

    return pl.pallas_call(
        body, name="attn_dilate", grid=(s // DIL_ROWS,), in_specs=[_dil_spec(1)], out_specs=[_dil_spec(r) for r in _RS[1:]],
        out_shape=[_dil_shape(s, r, t.dtype) for r in _RS[1:]], scratch_shapes=_DIL_SCRATCH, compiler_params=_params(("parallel",)),
    )(t)


def _rope_bwd(datt, d_glr, dp, cos_t, sin_t, s):
    tail = P_W - P_AQ

    def body(*refs):
        ins, cs, sn, glr, o_ref, scr = refs[:9], refs[9][...], refs[10][...], refs[11], refs[13], refs[14]
        for t in range(3):
            for g, r in enumerate(_RS):
                val = _from_dilated(scr, ins[3 * t + g], r)
                o_ref[:, t * ATT_W + g * 256:t * ATT_W + (g + 1) * 256] = (_rope_apply(val, cs, sn, True) if t < 2 else val).astype(BF16)
        o_ref[:, 3 * ATT_W:3 * ATT_W + 128] = glr[...]
        o_ref[:, 3 * ATT_W + 128:] = jnp.zeros((DIL_ROWS, tail - 3 * ATT_W - 128), BF16)

    return pl.pallas_call(
        body, name="rope_bwd", grid=(s // DIL_ROWS,),
        in_specs=[_dil_spec(r) for _ in range(3) for r in _RS] + [pl.BlockSpec((DIL_ROWS, 128), lambda i: (i, 0))] * 3
        + [pl.BlockSpec(memory_space=pl.ANY)],
        out_specs=pl.BlockSpec((DIL_ROWS, tail), lambda i: (i, P_AQ // tail)),
        out_shape=jax.ShapeDtypeStruct((s, P_W), BF16), input_output_aliases={12: 0},
        scratch_shapes=_DIL_SCRATCH, compiler_params=_params(("parallel",)),
    )(*[datt[g][t] for t in range(3) for g in range(3)], cos_t, sin_t, d_glr, dp)


def _tri_dot(tri, t):
    tb = tri.astype(BF16)
    hi = t.astype(BF16)
    r1 = t - hi.astype(F32)
    mid = r1.astype(BF16)
    lo = (r1 - mid.astype(F32)).astype(BF16)
    return _dg(tb, hi, 1, 0) + _dg(tb, mid, 1, 0) + _dg(tb, lo, 1, 0)


def _gla_decays(la_c, tri):
    b = _tri_dot(tri, la_c)
    row = lax.broadcasted_iota(jnp.int32, b.shape, 0)
    bmid = jnp.sum(jnp.where(row == GLA_CHUNK // 2 - 1, b, 0.0), axis=0, keepdims=True)
    blast = jnp.sum(jnp.where(row == GLA_CHUNK - 1, b, 0.0), axis=0, keepdims=True)
    return b, bmid, blast


def _gla_fwd(p, la, s, comm=()):
    tb, ch = GLA_BLOCK, GLA_CHUNK
    nb, nc = s // tb, tb // ch
    scale = GLA_DK ** -0.5
    c_ins, c_outs, c_alias, c_scratch = _carry(comm, 4, 2)

    def body(q_ref, k_ref, v_ref, la_ref, *rest):
        ci, (o_ref, st_ref) = rest[:len(c_ins)], rest[len(c_ins):len(c_ins) + 2]
        co, state = rest[len(c_ins) + 2:len(c_ins) + 2 + len(c_outs)], rest[len(c_ins) + 2 + len(c_outs)]
        step = pl.program_id(0)
        if comm:
            @pl.when(step == 0)
            def _():
                _comm_phase(comm, ci, co, rest[-2], rest[-1], True)

        _gla_fwd_step(q_ref, k_ref, v_ref, la_ref, o_ref, st_ref, state)
        if comm:
            @pl.when(step == nb - 1)
            def _():
                _comm_phase(comm, ci, co, rest[-2], rest[-1], False)

    def _gla_fwd_step(q_ref, k_ref, v_ref, la_ref, o_ref, st_ref, state):
        @pl.when(pl.program_id(0) == 0)
        def _():
            state[...] = jnp.zeros_like(state)

        ri = lax.broadcasted_iota(jnp.int32, (ch, ch), 0)
        ci = lax.broadcasted_iota(jnp.int32, (ch, ch), 1)
        causal = ci <= ri
        tri = causal.astype(F32)

        def chunk(c, carry):
            sl = pl.ds(pl.multiple_of(c * ch, ch), ch)
            b, bmid, blast = _gla_decays(la_ref[sl, :], tri)
            q = q_ref[sl, :].astype(F32) * scale
            k = k_ref[sl, :].astype(F32)
            v = v_ref[sl, :]
            qgt = (q * jnp.exp(b)).astype(BF16)
            qgn = (q * jnp.exp(b - bmid)).astype(BF16)
            kgn = (k * jnp.exp(bmid - b)).astype(BF16)
            kd = (k * jnp.exp(blast - b)).astype(BF16)
            dec = jnp.exp(blast)
            sts = [state[h] for h in range(GLA_H)]
            outs, news = [], []
            for h in range(GLA_H):
                hk, hv = slice(h * GLA_DK, (h + 1) * GLA_DK), slice(h * GLA_DV, (h + 1) * GLA_DV)
                a = jnp.where(causal, _dg(qgn[:, hk], kgn[:, hk], 1, 1), 0.0)
                outs.append(_dg(a.astype(BF16), v[:, hv], 1, 0) + _dg(qgt[:, hk], sts[h].astype(BF16), 1, 1))
                news.append(dec[:, hk] * sts[h] + _dg(v[:, hv], kd[:, hk], 0, 0))
            for h in range(GLA_H):
                st_ref[h, c] = sts[h]
                state[h] = news[h]
            o_ref[sl, :] = jnp.concatenate(outs, axis=1)
            return carry

        lax.fori_loop(0, nc, chunk, 0, unroll=2)

    hw = GLA_H * GLA_DK
    res = pl.pallas_call(
        body, name="gla_fwd", grid=(nb,),
        in_specs=[pl.BlockSpec((tb, hw), lambda t: (t, P_GQ // hw)),
                  pl.BlockSpec((tb, hw), lambda t: (t, P_GK // hw)),
                  pl.BlockSpec((tb, GLA_H * GLA_DV), lambda t: (t, P_GV // (GLA_H * GLA_DV))),
                  pl.BlockSpec((tb, hw), lambda t: (t, 0))] + [HBM] * len(c_ins),
        out_specs=[pl.BlockSpec((tb, GLA_H * GLA_DV), lambda t: (t, 0)),
                   pl.BlockSpec((GLA_H, nc, GLA_DV, GLA_DK), lambda t: (0, t, 0, 0))] + [HBM] * len(c_outs),
        out_shape=[jax.ShapeDtypeStruct((s, GLA_H * GLA_DV), F32),
                   jax.ShapeDtypeStruct((GLA_H, s // ch, GLA_DV, GLA_DK), F32)] + c_outs,
        scratch_shapes=[pltpu.VMEM((GLA_H, GLA_DV, GLA_DK), F32)] + c_scratch,
        input_output_aliases=c_alias,
        compiler_params=_params(("arbitrary",)),
    )(p, p, p, la, *c_ins)
    return res[0], res[1], _split_units(comm, res[2:])


def _gla_bwd(p, la, states, do, s, dp, comm=()):
    tb, ch = GLA_BLOCK, GLA_CHUNK
    nb, nc = s // tb, tb // ch
    scale = GLA_DK ** -0.5
    c_ins, c_outs, c_alias, c_scratch = _carry(comm, 7, 4)

    def body(q_ref, k_ref, v_ref, la_ref, st_ref, do_ref, dp_in, *rest):
        ci, outs = rest[:len(c_ins)], rest[len(c_ins):len(c_ins) + 4]
        co, dstate = rest[len(c_ins) + 4:len(c_ins) + 4 + len(c_outs)], rest[len(c_ins) + 4 + len(c_outs)]
        step = pl.program_id(0)
        if comm:
            @pl.when(step == 0)
            def _():
                _comm_phase(comm, ci, co, rest[-2], rest[-1], True)

        _gla_bwd_step(q_ref, k_ref, v_ref, la_ref, st_ref, do_ref, *outs, dstate)
        if comm:
            @pl.when(step == nb - 1)
            def _():
                _comm_phase(comm, ci, co, rest[-2], rest[-1], False)

    def _gla_bwd_step(q_ref, k_ref, v_ref, la_ref, st_ref, do_ref, dq_ref, dk_ref, dv_ref, dla_ref, dstate):
        @pl.when(pl.program_id(0) == 0)
        def _():
            dstate[...] = jnp.zeros_like(dstate)

        ri = lax.broadcasted_iota(jnp.int32, (ch, ch), 0)
        ci = lax.broadcasted_iota(jnp.int32, (ch, ch), 1)
        causal = ci <= ri
        tri = causal.astype(F32)
        tri_t = (ci >= ri).astype(F32)

        def chunk(cc, carry):
            c = nc - 1 - cc
            sl = pl.ds(pl.multiple_of(c * ch, ch), ch)
            b, bmid, blast = _gla_decays(la_ref[sl, :], tri)
            q = q_ref[sl, :].astype(F32) * scale
            k = k_ref[sl, :].astype(F32)
            v = v_ref[sl, :]
            e_b, e_qn, e_kn, e_kd = jnp.exp(b), jnp.exp(b - bmid), jnp.exp(bmid - b), jnp.exp(blast - b)
            dec = jnp.exp(blast)
            qgt, qgn, kgn, kd = q * e_b, q * e_qn, k * e_kn, k * e_kd
            qgt_b, qgn_b, kgn_b, kd_b = qgt.astype(BF16), qgn.astype(BF16), kgn.astype(BF16), kd.astype(BF16)
            do_b = do_ref[sl, :].astype(BF16)
            st0s = [st_ref[h, c] for h in range(GLA_H)]
            dsts = [dstate[h] for h in range(GLA_H)]
            dqgn, dqgt, dkgn, dkd, dvs, ddec, news = [], [], [], [], [], [], []
            for h in range(GLA_H):
                hk, hv = slice(h * GLA_DK, (h + 1) * GLA_DK), slice(h * GLA_DV, (h + 1) * GLA_DV)
                dst_b = dsts[h].astype(BF16)
                a = jnp.where(causal, _dg(qgn_b[:, hk], kgn_b[:, hk], 1, 1), 0.0).astype(BF16)
                da = jnp.where(causal, _dg(do_b[:, hv], v[:, hv], 1, 1), 0.0).astype(BF16)
                dqgn.append(_dg(da, kgn_b[:, hk], 1, 0))
                dqgt.append(_dg(do_b[:, hv], st0s[h].astype(BF16), 1, 0))
                dkgn.append(_dg(da, qgn_b[:, hk], 0, 0))
                dvs.append(_dg(a, do_b[:, hv], 0, 0) + _dg(kd_b[:, hk], dst_b, 1, 1))
                dkd.append(_dg(v[:, hv], dst_b, 1, 0))
                ddec.append(jnp.sum(st0s[h] * dsts[h], axis=0, keepdims=True))
                news.append(dec[:, hk] * dsts[h] + _dg(do_b[:, hv], qgt_b[:, hk], 0, 0))
            for h in range(GLA_H):
                dstate[h] = news[h]
            cat = lambda parts: jnp.concatenate(parts, axis=1)
            dqgn, dqgt, dkgn, dkd, ddec = cat(dqgn), cat(dqgt), cat(dkgn), cat(dkd), cat(ddec)
            dq_ref[sl, :] = (scale * (dqgn * e_qn + dqgt * e_b)).astype(dq_ref.dtype)
            dk_ref[sl, :] = (dkgn * e_kn + dkd * e_kd).astype(dk_ref.dtype)
            dv_ref[sl, :] = cat(dvs).astype(dv_ref.dtype)
            db = dqgn * qgn + dqgt * qgt - dkgn * kgn - dkd * kd
            extra = jnp.sum(dkd * kd, axis=0, keepdims=True) + ddec * dec
            dla_ref[sl, :] = _tri_dot(tri_t, db) + extra
            return carry

        lax.fori_loop(0, nc, chunk, 0, unroll=2)

    rev = lambda t: nb - 1 - t
    hw, vw = GLA_H * GLA_DK, GLA_H * GLA_DV
    res = pl.pallas_call(
        body, name="gla_bwd", grid=(nb,),
        in_specs=[pl.BlockSpec((tb, hw), lambda t: (rev(t), P_GQ // hw)),
                  pl.BlockSpec((tb, hw), lambda t: (rev(t), P_GK // hw)),
                  pl.BlockSpec((tb, vw), lambda t: (rev(t), P_GV // vw)),
                  pl.BlockSpec((tb, hw), lambda t: (rev(t), 0)),
                  pl.BlockSpec((GLA_H, nc, GLA_DV, GLA_DK), lambda t: (0, rev(t), 0, 0)),
                  pl.BlockSpec((tb, vw), lambda t: (rev(t), 0)), pl.BlockSpec(memory_space=pl.ANY)] + [HBM] * len(c_ins),
        out_specs=[pl.BlockSpec((tb, hw), lambda t: (rev(t), 0)),
                   pl.BlockSpec((tb, hw), lambda t: (rev(t), 0)),
                   pl.BlockSpec((tb, vw), lambda t: (rev(t), P_GV // vw)),
                   pl.BlockSpec((tb, hw), lambda t: (rev(t), 0))] + [HBM] * len(c_outs),
        out_shape=[jax.ShapeDtypeStruct((s, hw), BF16),
                   jax.ShapeDtypeStruct((s, hw), BF16),
                   jax.ShapeDtypeStruct((s, P_W), BF16),
                   jax.ShapeDtypeStruct((s, hw), F32)] + c_outs,
        scratch_shapes=[pltpu.VMEM((GLA_H, GLA_DV, GLA_DK), F32)] + c_scratch,
        input_output_aliases={6: 2, **c_alias},
        compiler_params=_params(("arbitrary",)),
    )(p, p, p, la, states, do, dp, *c_ins)
    return res[0], res[1], res[2], res[3], _split_units(comm, res[4:])


def _head_masks():
    lane = lax.broadcasted_iota(jnp.int32, (1, 4 * ATT_HD), 1)
    return [(lane >= h * ATT_HD) & (lane < (h + 1) * ATT_HD) for h in range(4)]


def _attn_fwd(qv, kv, pv, g, r, s):
    ln = s // r
    nblk = ln // ATT_BLK
    qcol = lambda pr: pr
    vcol = qcol
    prev = lambda n: jnp.maximum(n - 1, 0)

    def body(q_ref, kp_ref, kc_ref, vp_ref, vc_ref, o_ref, lse_ref):
        has_prev = pl.program_id(1) > 0
        ri = lax.broadcasted_iota(jnp.int32, (ATT_BLK, ATT_BLK), 0)
        ci = lax.broadcasted_iota(jnp.int32, (ATT_BLK, ATT_BLK), 1)
        m_cur = ci <= ri
        m_prev = (ci >= ri) & has_prev
        q, kp, kc, vp, vc = q_ref[...], kp_ref[...], kc_ref[...], vp_ref[...], vc_ref[...]
        o = jnp.zeros((ATT_BLK, 256), F32)
        lse = jnp.zeros((ATT_BLK, 256), F32)
        for hm in _head_masks():
            qm = jnp.where(hm, q, jnp.zeros_like(q))
            sc = jnp.where(m_cur, _dg(qm, kc, 1, 1) * 0.125, NEG)
            sp = jnp.where(m_prev, _dg(qm, kp, 1, 1) * 0.125, NEG)
            mx = jnp.maximum(jnp.max(sc, axis=1, keepdims=True), jnp.max(sp, axis=1, keepdims=True))
            pc, pp = jnp.exp(sc - mx), jnp.exp(sp - mx)
            den = jnp.sum(pc, axis=1, keepdims=True) + jnp.sum(pp, axis=1, keepdims=True)
            oh = (_dg(pc.astype(BF16), vc, 1, 0) + _dg(pp.astype(BF16), vp, 1, 0)) / den
            o = jnp.where(hm, oh, o)
            lse = jnp.where(hm, mx + jnp.log(den), lse)
        o_ref[...] = o.astype(o_ref.dtype)
        lse_ref[...] = lse

    blk = (ATT_BLK, 256)
    o, lse = pl.pallas_call(
        body, name=f"attn_fwd_{g}", grid=(r, nblk),
        in_specs=[pl.BlockSpec(blk, lambda pr, n: (n, qcol(pr))),
                  pl.BlockSpec(blk, lambda pr, n: (prev(n), qcol(pr))),
                  pl.BlockSpec(blk, lambda pr, n: (n, qcol(pr))),
                  pl.BlockSpec(blk, lambda pr, n: (prev(n), vcol(pr))),
                  pl.BlockSpec(blk, lambda pr, n: (n, vcol(pr)))],
        out_specs=[pl.BlockSpec(blk, lambda pr, n: (n, pr)), pl.BlockSpec(blk, lambda pr, n: (n, pr))],
        out_shape=[jax.ShapeDtypeStruct((ln, r * 256), BF16), jax.ShapeDtypeStruct((ln, r * 256), F32)],
        compiler_params=_params(("parallel", "parallel")),
    )(qv, kv, kv, pv, pv)
    return o, lse


def _attn_bwd(qv, kv, pv, dov, ov, lv, g, r, s):
    ln = s // r
    nblk = ln // ATT_BLK
    qcol = lambda pr: pr
    vcol = qcol
    prev = lambda n: jnp.maximum(n - 1, 0)
    nxt = lambda n: jnp.minimum(n + 1, nblk - 1)

    def body(qc_ref, qn_ref, kp_ref, kc_ref, vp_ref, vc_ref, doc_ref, don_ref, oc_ref, on_ref, lc_ref, ln_ref,
             dq_ref, dk_ref, dv_ref):
        n = pl.program_id(1)
        has_prev, has_next = n > 0, n < nblk - 1
        ri = lax.broadcasted_iota(jnp.int32, (ATT_BLK, ATT_BLK), 0)
        ci = lax.broadcasted_iota(jnp.int32, (ATT_BLK, ATT_BLK), 1)
        m_cur = ci <= ri
        m_prev = (ci >= ri) & has_prev
        m_next = (ci >= ri) & has_next
        qc, qn, kp, kc, vp, vc = qc_ref[...], qn_ref[...], kp_ref[...], kc_ref[...], vp_ref[...], vc_ref[...]
        doc, don = doc_ref[...], don_ref[...]
        pc_full = doc.astype(F32) * oc_ref[...].astype(F32)
        pn_full = don.astype(F32) * on_ref[...].astype(F32)
        lc, lnx = lc_ref[...], ln_ref[...]
        dq = jnp.zeros((ATT_BLK, 256), F32)
        dk = jnp.zeros((ATT_BLK, 256), F32)
        dv = jnp.zeros((ATT_BLK, 256), F32)
        zb = jnp.zeros_like(qc)
        for hm in _head_masks():
            qcm, qnm = jnp.where(hm, qc, zb), jnp.where(hm, qn, zb)
            docm, donm = jnp.where(hm, doc, zb), jnp.where(hm, don, zb)
            lse_c = jnp.max(jnp.where(hm, lc, NEG), axis=1, keepdims=True)
            lse_n = jnp.max(jnp.where(hm, lnx, NEG), axis=1, keepdims=True)
            del_c = jnp.sum(jnp.where(hm, pc_full, 0.0), axis=1, keepdims=True)
            del_n = jnp.sum(jnp.where(hm, pn_full, 0.0), axis=1, keepdims=True)
            pr_ = jnp.where(m_cur, jnp.exp(_dg(qcm, kc, 1, 1) * 0.125 - lse_c), 0.0)
            ds = (pr_ * (_dg(docm, vc, 1, 1) - del_c) * 0.125).astype(BF16)
            dqh = _dg(ds, kc, 1, 0)
            dkh = _dg(ds, qc, 0, 0)
            dvh = _dg(pr_.astype(BF16), doc, 0, 0)
            pr_ = jnp.where(m_prev, jnp.exp(_dg(qcm, kp, 1, 1) * 0.125 - lse_c), 0.0)
            ds = (pr_ * (_dg(docm, vp, 1, 1) - del_c) * 0.125).astype(BF16)
            dqh = dqh + _dg(ds, kp, 1, 0)
            pr_ = jnp.where(m_next, jnp.exp(_dg(qnm, kc, 1, 1) * 0.125 - lse_n), 0.0)
            ds = (pr_ * (_dg(donm, vc, 1, 1) - del_n) * 0.125).astype(BF16)
            dkh = dkh + _dg(ds, qn, 0, 0)
            dvh = dvh + _dg(pr_.astype(BF16), don, 0, 0)
            dq = jnp.where(hm, dqh, dq)
            dk = jnp.where(hm, dkh, dk)
            dv = jnp.where(hm, dvh, dv)
        dq_ref[...] = dq.astype(dq_ref.dtype)
        dk_ref[...] = dk.astype(dk_ref.dtype)
        dv_ref[...] = dv.astype(dv_ref.dtype)

    blk = (ATT_BLK, 256)
    cur = lambda col: pl.BlockSpec(blk, lambda pr, n: (n, col(pr)))
    prv = lambda col: pl.BlockSpec(blk, lambda pr, n: (prev(n), col(pr)))
    nx = lambda col: pl.BlockSpec(blk, lambda pr, n: (nxt(n), col(pr)))
    own = lambda pr: pr
    outs = pl.pallas_call(
        body, name=f"attn_bwd_{g}", grid=(r, nblk),
        in_specs=[cur(qcol), nx(qcol), prv(qcol), cur(qcol), prv(vcol), cur(vcol),
                  cur(own), nx(own), cur(own), nx(own), cur(own), nx(own)],
        out_specs=[cur(own), cur(own), cur(own)],
        out_shape=[jax.ShapeDtypeStruct((ln, r * 256), BF16)] * 3,
        compiler_params=_params(("parallel", "parallel")),
    )(qv, qv, kv, kv, pv, pv, dov, dov, ov, ov, lv, lv)
    return outs


def _gelu_parts(gv):
    cdf = 0.5 * (1.0 + lax.erf(gv * (2.0 ** -0.5)))
    pdf = jnp.exp(-0.5 * gv * gv) * (1.0 / math.sqrt(2.0 * math.pi))
    return cdf, pdf


def _pick_row(t, k):
    row = lax.broadcasted_iota(jnp.int32, t.shape, 0)
    return jnp.sum(jnp.where(row == k, t, 0.0), axis=0, keepdims=True)


def _shift_rows(u, halo, n):
    row = lax.broadcasted_iota(jnp.int32, u.shape, 0)
    out = pltpu.roll(u, n, 0)
    for k in range(n):
        out = jnp.where(row == k, _pick_row(halo, 16 - n + k), out)
    return out


def _shift_rows_up(u, halo, n):
    rb = u.shape[0]
    row = lax.broadcasted_iota(jnp.int32, u.shape, 0)
    out = pltpu.roll(u, rb - n, 0)
    for k in range(n):
        out = jnp.where(row == rb - n + k, _pick_row(halo, k), out)
    return out


CONV_ROWS, CONV_LANES = 64, 256


def _conv_transpose(duc, cw, s, rb):
    w2 = 2 * W_UP_SH
    nr = s // rb

    def body(d_ref, h_ref, cw_ref, o_ref, scr):
        last = pl.program_id(1) == nr - 1
        scr[0:rb, :] = d_ref[...].astype(F32)
        scr[rb:rb + 16, :] = jnp.where(last, 0.0, h_ref[...].astype(F32))
        for l0 in range(0, w2, CONV_LANES):
            ls = slice(l0, l0 + CONV_LANES)
            c0, c1, c2 = cw_ref[0:1, ls], cw_ref[1:2, ls], cw_ref[2:3, ls]
            for r0 in range(0, rb, CONV_ROWS):
                o_ref[r0:r0 + CONV_ROWS, ls] = (c2 * scr[r0:r0 + CONV_ROWS, ls] + c1 * scr[r0 + 1:r0 + 1 + CONV_ROWS, ls]
                                                + c0 * scr[r0 + 2:r0 + 2 + CONV_ROWS, ls]).astype(o_ref.dtype)

    per = rb // 16
    lastb = s // 16 - 1
    return pl.pallas_call(
        body, name="conv_transpose", grid=(2, nr),
        in_specs=[pl.BlockSpec((rb, w2), lambda c, i: (i, c)),
                  pl.BlockSpec((16, w2), lambda c, i: (jnp.minimum((i + 1) * per, lastb), c)),
                  pl.BlockSpec((3, w2), lambda c, i: (0, c))],
        out_specs=pl.BlockSpec((rb, w2), lambda c, i: (i, c)),
        out_shape=jax.ShapeDtypeStruct(duc.shape, BF16),
        scratch_shapes=[pltpu.VMEM((rb + 16, w2), F32)],
        compiler_params=_params(("parallel", "parallel")),
    )(duc, duc, cw)


def _conv_geglu(u, cw, cb, s, rb):
    w2, hw = 2 * W_UP_SH, W_UP_SH
    nr = s // rb

    def body(u_ref, h_ref, cw_ref, cb_ref, o_ref, scr):
        first = pl.program_id(1) == 0
        scr[0:16, :] = jnp.where(first, 0.0, h_ref[...].astype(F32))
        scr[16:16 + rb, :] = u_ref[...].astype(F32)

        def conv(ls, r0):
            return (cb_ref[0:1, ls] + cw_ref[0:1, ls] * scr[14 + r0:14 + r0 + CONV_ROWS, ls]
                    + cw_ref[1:2, ls] * scr[15 + r0:15 + r0 + CONV_ROWS, ls] + cw_ref[2:3, ls] * scr[16 + r0:16 + r0 + CONV_ROWS, ls])

        for l0 in range(0, hw, 128):
            for r0 in range(0, rb, CONV_ROWS):
                val, gt = conv(slice(l0, l0 + 128), r0), conv(slice(hw + l0, hw + l0 + 128), r0)
                cdf, _ = _gelu_parts(gt)
                o_ref[r0:r0 + CONV_ROWS, l0:l0 + 128] = (gt * cdf * val).astype(o_ref.dtype)

    per = rb // 16
    return pl.pallas_call(
        body, name="conv_geglu", grid=(2, nr),
        in_specs=[pl.BlockSpec((rb, w2), lambda c, i: (i, c)),
                  pl.BlockSpec((16, w2), lambda c, i: (jnp.maximum(i * per - 1, 0), c)),
                  pl.BlockSpec((3, w2), lambda c, i: (0, c)), pl.BlockSpec((1, w2), lambda c, i: (0, c))],
        out_specs=pl.BlockSpec((rb, hw), lambda c, i: (i, c)),
        out_shape=jax.ShapeDtypeStruct((s, D_FF), BF16),
        scratch_shapes=[pltpu.VMEM((rb + 16, w2), F32)],
        compiler_params=_params(("parallel", "parallel")),
    )(u, u, cw, cb)


def _conv(u, halo, cw, cb):
    return cb + _pick_row(cw, 0) * _shift_rows(u, halo, 2) + _pick_row(cw, 1) * _shift_rows(u, halo, 1) + _pick_row(cw, 2) * u


def _local_step(x, mod, pos_col, target, sm, w_sh, g0, chip, core):
    s = x.shape[0]
    shift1, scale1, gate1, shift2, scale2, gate2 = [mod[i:i + 1, :] for i in range(6)]
    rb = 512
    chip1 = chip.reshape(1)

    def f_norm1(c, i, xv, nw, sc, sh):
        return ((xv * _rms(xv) * nw) * (1.0 + sc) + sh,)

    (h,) = _rowcall(f_norm1, [_rows(x, rb), _full(sm["n1w"]), _full(scale1), _full(shift1)],
                    [_orow(s, D, BF16, rb)], n_rows=s, rb=rb, name="norm1")
    own = lambda got, i: lax.dynamic_update_slice(got, w_sh[i], (chip, 0, 0))
    invf = jnp.tile(ROPE_THETA ** (-jnp.arange(ATT_HD // 2, dtype=F32) / (ATT_HD // 2)), 4).reshape(1, 128)
    cos_t, sin_t = _rope_tables(pos_col, invf, s)
    _, got0 = _unit_wait(*g0[:4], after=[h, cos_t, sin_t], name="gather_w_in_wait")
    [got0] = _comm_call("gather_w_in_d2d", [_u_gather_d2d(got0, (0,))])
    u_g1 = _u_gather_ici(w_sh, (1, 2, 3, 4, 5))
    g1 = _unit_start(u_g1, "gather_weights_start", after=got0)
    w = dict(win=_win_assemble(got0[0], w_sh[0], after=g1[3:]))
    p = _mm(h, w["win"], "in_proj", tm=2048, tn=1536)

    def f_gla_pre(c, i, glr, w2, gb):
        z = _dg(glr, w2.astype(BF16), 1, 0) + gb
        return ((jnp.minimum(z, 0.0) - jnp.log(1.0 + jnp.exp(-jnp.abs(z)))) * (1.0 / GLA_TAU),)

    (la,) = _rowcall(f_gla_pre, [_rows(p, rb, 128, P_LR // 128), _full(sm["w2"]), _full(sm["gb"])],
                     [_orow(s, 512, F32, rb)], n_rows=s, rb=rb, name="gla_pre")
    o_gla, states, _ = _gla_fwd(p, la, s)
    _, got = _unit_wait(u_g1, *g1[:3], after=[o_gla], name="gather_weights_wait")
    [got123] = _comm_call("gather_weights_d2d", [_u_gather_d2d(got[:3], (1, 2, 3))])
    got45 = got[3:]
    w.update(wgb=own(got123[0], 1).reshape(1024, D), wab=_cols_join(own(got123[1], 2)), wout=own(got123[2], 3).reshape(D, D))

    def f_gla_post(c, i, ov, gnw, gr):
        on = jnp.concatenate([ov[:, k * 256:(k + 1) * 256] * _rms(ov[:, k * 256:(k + 1) * 256]) * gnw
                              for k in range(GLA_H)], axis=1)
        g = gr.astype(F32)
        return (on * (g * _sigmoid(g)),)

    (og,) = _rowcall(f_gla_post, [_rows(o_gla, rb), _full(sm["gnw"]), _rows(p, rb, 1024, P_GR // 1024)],
                     [_orow(s, 1024, BF16, rb)], n_rows=s, rb=rb, name="gla_post")
    y_gla = _mm(og, w["wgb"], "gla_branch")

    q_d, k_d, v_d = _rope_fwd(p, cos_t, sin_t, s)
    att = [_attn_fwd(q_d[g], k_d[g], v_d[g], g, r, s) for g, r in enumerate(_RS)]
    o_att, lse, o_d1, o_d2, lse_d1, lse_d2 = _attn_combine(att, s)
    y_att = _mm(o_att, w["wab"], "attn_branch")

    def f_merge(c, i, ma, mb, yg, ya):
        return (_sigmoid(ma.astype(F32)) * yg.astype(F32) + _sigmoid(mb.astype(F32)) * ya.astype(F32),)

    (mixed,) = _rowcall(f_merge, [_rows(p, rb, D, P_MA // D), _rows(p, rb, D, P_MB // D), _rows(y_gla, rb), _rows(y_att, rb)],
                        [_orow(s, D, BF16, rb)], n_rows=s, rb=rb, name="merge")
    z1, [got45] = _mm(mixed, w["wout"], "out_proj", comm=[_u_gather_d2d(got45, (4, 5))])
    w.update(wup=own(got45[0], 4), wdown=own(got45[1], 5).reshape(D_FF, D))

    def f_norm2(c, i, xv, z, g1, nw, sc, sh):
        x1 = xv + g1 * z.astype(F32)
        return (x1, (x1 * _rms(x1) * nw) * (1.0 + sc) + sh)

    x1, h2 = _rowcall(f_norm2, [_rows(x, rb), _rows(z1, rb), _full(gate1), _full(sm["n2w"]), _full(scale2), _full(shift2)],
                      [_orow(s, D, F32, rb), _orow(s, D, BF16, rb)], n_rows=s, rb=rb, name="norm2")
    u = _mm(h2, w["wup"], "up_proj", tm=2048, b_shards=True)

    cwid = 2 * W_UP_SH

    ccol = lambda c: c
    rw = 256
    hidden = _conv_geglu(u, sm["cw"], sm["cb"], s, rw)
    z2 = _mm(hidden, w["wdown"], "down_proj", tk=D_FF)

    def f_final(c, i, x1v, z, g2, fw, tgt):
        x2 = x1v + g2 * z.astype(F32)
        r = _rms(x2)
        xh = x2 * r
        e = xh * fw - tgt
        loss = 0.5 * jnp.sum(jnp.mean(e * e, axis=-1, keepdims=True), axis=0, keepdims=True)
        dy = e * (1.0 / D)
        dxh = dy * fw
        dx2 = r * (dxh - xh * jnp.mean(dxh * xh, axis=-1, keepdims=True))
        return (loss, dx2, dx2 * g2, _csum(dy * xh), _csum(dx2 * z.astype(F32)))

    loss, dx2, dz2, d_fnw, d_gate2 = _rowcall(
        f_final, [_rows(x1, rb), _rows(z2, rb), _full(gate2), _full(sm["fnw"]), _rows(target, rb)],
        [_oacc(1, 1), _orow(s, D, F32, rb), _orow(s, D, BF16, rb), _oacc(1, D), _oacc(1, D)],
        n_rows=s, rb=rb, name="final_loss")
    d_hidden = _mm(dz2, w["wdown"], "down_proj_dx", tb=True, tn=1408)
    g_wdown = _mm(hidden, dz2, "down_proj_dw", ta=True, out_dtype=F32, tm=1408, tn=1024, tk=2048)

    def f_ffn_bwd(c, i, uv, hl, dh, cw, cb):
        uf = uv.astype(F32)
        hf = hl.astype(F32) * (i > 0).astype(F32)
        u1, u2 = _shift_rows(uf, hf, 1), _shift_rows(uf, hf, 2)
        uc = cb + _pick_row(cw, 0) * u2 + _pick_row(cw, 1) * u1 + _pick_row(cw, 2) * uf
        val, gt = uc[:, :W_UP_SH], uc[:, W_UP_SH:]
        cdf, pdf = _gelu_parts(gt)
        dhf = dh.astype(F32)
        duc = jnp.concatenate([dhf * (gt * cdf), dhf * val * (cdf + gt * pdf)], axis=1)
        dcw = jnp.concatenate([_csum(duc * u2), _csum(duc * u1), _csum(duc * uf)], axis=0)
        return (duc, _csum(duc), dcw)

    duc, d_cb, d_cw = _rowcall(
        f_ffn_bwd, [_rows(u, rw, cwid, ccol), _halo(u, rw, 16, cwid, ccol, True), _rows(d_hidden, rw, W_UP_SH, ccol),
                    _full(sm["cw"], cwid, ccol), _full(sm["cb"], cwid, ccol)],
        [_orow(s, 2 * D_FF, BF16, rw, cwid, ccol), _oacc(1, 2 * D_FF, cwid, ccol), _oacc(3, 2 * D_FF, cwid, ccol)],
        n_rows=s, rb=rw, name="conv_geglu_bwd", ncol=2)

    du = _conv_transpose(duc, sm["cw"], s, rw)
    g_wup = _mm(h2, du, "up_proj_dw", ta=True, out_dtype=F32, tm=1024, tk=2048, o_shards=True)
    gs45 = [g_wup, g_wdown.reshape(4, W_DOWN_SH, 1024)]
    d_h2, [land45] = _mm(du, w["wup"], "up_proj_dx", tb=True, tm=2048, b_shards=True, comm=[_u_pair_send(gs45, (4, 5))])
    ts45 = [_pair_add(g, ld, core, "grad_pair_add_" + BIG[i]) for g, ld, i in zip(gs45, land45, (4, 5))]
    u_ex4 = _u_chip_exchange(ts45[:1])
    ex4 = _unit_start(u_ex4, "grad_exchange_w_up_start")

    def f_norm2_bwd(c, i, x1v, dh, dxr, z, nw, sc, g1):
        dxn, dsh, dsc, dnw = _norm_bwd(x1v, dh.astype(F32), nw, sc)
        dx1 = dxr + dxn
        return (dx1, dx1 * g1, dsh, dsc, dnw, _csum(dx1 * z.astype(F32)))

    dx1, dz1, d_shift2, d_scale2, d_n2w, d_gate1 = _rowcall(
        f_norm2_bwd, [_rows(x1, rb), _rows(d_h2, rb), _rows(dx2, rb), _rows(z1, rb), _full(sm["n2w"]), _full(scale2), _full(gate1)],
        [_orow(s, D, F32, rb), _orow(s, D, BF16, rb), _oacc(1, D), _oacc(1, D), _oacc(1, D), _oacc(1, D)],
        n_rows=s, rb=rb, name="norm2_bwd", after=ex4[3:])
    d_mixed = _mm(dz1, w["wout"], "out_proj_dx", tb=True)
    g_wout = _mm(mixed, dz1, "out_proj_dw", ta=True, out_dtype=F32, tk=2048)

    def f_merge_bwd(c, i, dm, ma, mb, yg, ya):
        dmf, ygf, yaf = dm.astype(F32), yg.astype(F32), ya.astype(F32)
        sa, sb = _sigmoid(ma.astype(F32)), _sigmoid(mb.astype(F32))
        return (dmf * sa, dmf * sb, jnp.concatenate([dmf * ygf * sa * (1.0 - sa), dmf * yaf * sb * (1.0 - sb)], axis=1))

    dy_gla, dy_att, dp = _rowcall(
        f_merge_bwd, [_rows(d_mixed, rb), _rows(p, rb, D, P_MA // D), _rows(p, rb, D, P_MB // D), _rows(y_gla, rb), _rows(y_att, rb)],
        [_orow(s, D, BF16, rb)] * 2 + [_orow(s, P_W, BF16, rb, 2 * D, lambda c: P_MA // (2 * D))], n_rows=s, rb=rb, name="merge_bwd")
    d_og = _mm(dy_gla, w["wgb"], "gla_branch_dx", tb=True)
    g_wgb = _mm(og, dy_gla, "gla_branch_dw", ta=True, out_dtype=F32, tk=2048)
    d_oatt = _mm(dy_att, w["wab"], "attn_branch_dx", tb=True)
    g_wab = _mm(o_att, dy_att, "attn_branch_dw", ta=True, out_dtype=F32, tk=2048)

    def f_gla_post_bwd(c, i, ov, gnw, gr, dog):
        g = gr.astype(F32)
        sg = _sigmoid(g)
        silu = g * sg
        dof = dog.astype(F32)
        don = dof * silu
        on_parts, do_parts, dgn = [], [], jnp.zeros((1, 256), F32)
        for k in range(GLA_H):
            oh = ov[:, k * 256:(k + 1) * 256]
            dh = don[:, k * 256:(k + 1) * 256]
            r = _rms(oh)
            xh = oh * r
            dgn = dgn + _csum(dh * xh)
            dxh = dh * gnw
            do_parts.append(r * (dxh - xh * jnp.mean(dxh * xh, axis=-1, keepdims=True)))
            on_parts.append(xh * gnw)
        on = jnp.concatenate(on_parts, axis=1)
        dgr = dof * on * (sg * (1.0 + g * (1.0 - sg)))
        return (jnp.concatenate(do_parts, axis=1), dgr, dgn)

    do_gla, dp, d_gnw = _rowcall(
        f_gla_post_bwd, [_rows(o_gla, rb), _full(sm["gnw"]), _rows(p, rb, 1024, P_GR // 1024), _rows(d_og, rb)],
        [_orow(s, 1024, F32, rb), _orow(s, P_W, BF16, rb, 1024, lambda c: P_GR // 1024), _oacc(1, 256)],
        n_rows=s, rb=rb, name="gla_post_bwd", into=(dp, 1))
    gs123 = [g_wgb.reshape(4, 256, 1024), _cols_split(g_wab), g_wout.reshape(4, 256, 1024)]
    d_gq, d_gk, dp, d_la, [land123] = _gla_bwd(p, la, states, do_gla, s, dp, comm=[_u_pair_send(gs123, (1, 2, 3))])
    ts123 = [_pair_add(g, ld, core, "grad_pair_add_" + BIG[i]) for g, ld, i in zip(gs123, land123, (1, 2, 3))]

    def f_gla_pre_bwd(c, i, lav, dlav, glr, w2):
        dz = dlav * (1.0 / GLA_TAU) * (1.0 - jnp.exp(GLA_TAU * lav))
        dzb = dz.astype(BF16)
        return (_dg(dzb, w2.astype(BF16), 1, 1), _csum(dz), _dg(glr, dzb, 0, 0))

    d_glr, d_gb, d_w2 = _rowcall(
        f_gla_pre_bwd, [_rows(la, rb), _rows(d_la, rb), _rows(p, rb, 128, P_LR // 128), _full(sm["w2"])],
        [_orow(s, 128, BF16, rb), _oacc(1, 512), _oacc(128, 512)], n_rows=s, rb=rb, name="gla_pre_bwd")

    do_d = [d_oatt] + list(_dilate(d_oatt, s))
    datt = [_attn_bwd(q_d[g], k_d[g], v_d[g], do_d[g], (o_att, o_d1, o_d2)[g], (lse, lse_d1, lse_d2)[g], g, r, s)
            for g, r in enumerate(_RS)]
    dp = _rope_bwd(datt, d_glr, dp, cos_t, sin_t, s)
    dp = lax.dynamic_update_slice(dp, jnp.concatenate([d_gq, d_gk], axis=1), (0, P_GQ))
    [t4], [r4] = _unit_wait(u_ex4, *ex4[:3], after=[dp], name="grad_exchange_w_up_wait")
    half4 = [_chip_sum(t4, r4, chip1, "grad_chip_sum_w_up")]
    g_win, [r1235, oth4] = _mm(h, dp, "in_proj_dw", ta=True, out_dtype=F32, tm=1024, tn=1536, tk=2048,
                               comm=[_u_chip_exchange(ts123 + ts45[1:]), _u_pair_join(half4)])
    half1235 = [_chip_sum(t, r, chip1, "grad_chip_sum_" + BIG[i]) for t, r, i in zip(ts123 + ts45[1:], r1235, (1, 2, 3, 5))]
    gs0 = [_win_split(g_win)]
    d_h, [land0, oth1235] = _mm(dp, w["win"], "in_proj_dx", tb=True, tk=3840,
                                comm=[_u_pair_send(gs0, (0,)), _u_pair_join(half1235)])
    half123, half45 = half1235[:3], half4 + half1235[3:]
    oth123, oth45 = oth1235[:3], oth4 + oth1235[3:]
    ts0 = _pair_add(gs0[0], land0[0], core, "grad_pair_add_w_in")

    def f_norm1_bwd(c, i, xv, dh, dxr, nw, sc):
        dxn, dsh, dsc, dnw = _norm_bwd(xv, dh.astype(F32), nw, sc)
        return (dxr + dxn, dsh, dsc, dnw)

    grad_x, d_shift1, d_scale1, d_n1w = _rowcall(
        f_norm1_bwd, [_rows(x, rb), _rows(d_h, rb), _rows(dx1, rb), _full(sm["n1w"]), _full(scale1)],
        [_orow(s, D, F32, rb), _oacc(1, D), _oacc(1, D), _oacc(1, D)], n_rows=s, rb=rb, name="norm1_bwd")

    dmod = jnp.concatenate([d_shift1, d_scale1, d_gate1, d_shift2, d_scale2, d_gate2], axis=1)
    small = dict(dmod=dmod, n1w=d_n1w, gb=d_gb, gnw=d_gnw, n2w=d_n2w, cb=d_cb, fnw=d_fnw, w2=d_w2, cw=d_cw)
    return loss, grad_x, half123 + half45, oth123 + oth45, small, ts0


def _win_pieces():
    runs = [(P_GV, 1024, 2048), (P_MA, 5392, 2048), (P_GQ, 0, 1024), (P_AQ, 3088, 2304), (P_LR, 3072, GLA_LR)]
    out = []
    for kc, rc, ln in runs:
        while ln > 0:
            step = min(ln, W_IN_SH - rc % W_IN_SH)
            out.append((kc, rc, step))
            kc, rc, ln = kc + step, rc + step, ln - step
    return out


def _win_assemble(shards, own, after=()):
    rb = 256

    def body(s_ref, own_ref, *rest):
        o_ref = rest[-1]
        x, y, _ = _me()
        o_ref[:, W_IN:] = jnp.zeros((rb, P_W - W_IN), o_ref.dtype)
        for kc, rc, ln in _win_pieces():
            sh, lo = rc // W_IN_SH, rc % W_IN_SH
            o_ref[:, kc:kc + ln] = jnp.where(2 * x + y == sh, own_ref[0, :, lo:lo + ln], s_ref[sh, :, lo:lo + ln])

    return pl.pallas_call(
        body, name="w_in_assemble", grid=(D // rb,),
        in_specs=[pl.BlockSpec((4, rb, W_IN_SH), lambda i: (0, i, 0)), pl.BlockSpec((1, rb, W_IN_SH), lambda i: (0, i, 0))]
        + [pl.BlockSpec(memory_space=pl.ANY)] * len(after),
        out_specs=pl.BlockSpec((rb, P_W), lambda i: (i, 0)),
        out_shape=jax.ShapeDtypeStruct((D, P_W), shards.dtype), compiler_params=_params(("parallel",)),
    )(shards, own, *after)


def _win_split(g):
    rb = 256

    def body(g_ref, o_ref):
        for kc, rc, ln in _win_pieces():
            o_ref[rc // W_IN_SH, :, rc % W_IN_SH:rc % W_IN_SH + ln] = g_ref[:, kc:kc + ln]

    return pl.pallas_call(
        body, name="w_in_grad_split", grid=(D // rb,),
        in_specs=[pl.BlockSpec((rb, P_W), lambda i: (i, 0))], out_specs=pl.BlockSpec((4, rb, W_IN_SH), lambda i: (0, i, 0)),
        out_shape=jax.ShapeDtypeStruct((4, D, W_IN_SH), g.dtype), compiler_params=_params(("parallel",)),
    )(g)


def _ff_to_kernel(a):
    h = W_UP_SH
    return jnp.concatenate([a[:, 0:h], a[:, D_FF:D_FF + h], a[:, h:D_FF], a[:, D_FF + h:]], axis=1)


def _ff_from_kernel(a):
    h = W_UP_SH
    return jnp.concatenate([a[:, 0:h], a[:, 2 * h:3 * h], a[:, h:2 * h], a[:, 3 * h:]], axis=1)


BIG = ("w_in", "w_gla_branch", "w_attn_branch", "w_out", "w_up", "w_down")
SH_SHAPES = ((1024, W_IN_SH), (256, 1024), (256, 256), (256, 1024), (1024, W_UP_SH), (W_DOWN_SH, 1024))
N_BIG = len(BIG)


def _cols_join(t):
    return jnp.concatenate([t[k] for k in range(4)], axis=1)


def _cols_split(t):
    cols = t.shape[1] // 4
    return jnp.stack([t[:, k * cols:(k + 1) * cols] for k in range(4)])


def _me():
    return lax.axis_index("x"), lax.axis_index("y"), lax.axis_index("c")


HBM = pl.BlockSpec(memory_space=pltpu.HBM)
VMEM_SPEC = pl.BlockSpec(memory_space=pltpu.VMEM)


def _allgather8(xs, name):
    rows = xs.shape[0]

    def body(x_ref, out_ref, send_sems, recv_sems, local_sem):
        x, y, c = _me()
        me = 4 * x + 2 * y + c
        mine = pltpu.make_async_copy(x_ref, out_ref.at[me], local_sem)
        mine.start()
        flips = [(k >> 2 & 1, k >> 1 & 1, k & 1) for k in range(1, 8)]

        def peer(f):
            return (jnp.where(f[0] == 1, 1 - x, x), jnp.where(f[1] == 1, 1 - y, y), jnp.where(f[2] == 1, 1 - c, c))

        sends = []
        for k, f in enumerate(flips):
            cp = pltpu.make_async_remote_copy(src_ref=x_ref, dst_ref=out_ref.at[me], send_sem=send_sems.at[k],
                                              recv_sem=recv_sems.at[k], device_id=peer(f), device_id_type=MESH)
            cp.start()
            sends.append(cp)
        for k, f in enumerate(flips):
            px, py, pc = peer(f)
            pltpu.make_async_remote_copy(src_ref=x_ref, dst_ref=out_ref.at[4 * px + 2 * py + pc], send_sem=send_sems.at[k],
                                         recv_sem=recv_sems.at[k], device_id=peer(f), device_id_type=MESH).wait_recv()
        for cp in sends:
            cp.wait_send()
        mine.wait()

    return pl.pallas_call(
        body, name=name, out_shape=jax.ShapeDtypeStruct((8, rows, 128), F32),
        in_specs=[VMEM_SPEC], out_specs=VMEM_SPEC,
        scratch_shapes=[pltpu.SemaphoreType.DMA((7,)), pltpu.SemaphoreType.DMA((7,)), pltpu.SemaphoreType.DMA],
        compiler_params=pltpu.CompilerParams(vmem_limit_bytes=VMEM_LIMIT),
    )(xs)


def _half_rows(i, cc, unit):
    rows = SH_SHAPES[i][0] // 2
    return pl.ds(pl.multiple_of(cc * rows, unit), rows)


def _rc(src, dst, sems, to):
    return pltpu.make_async_remote_copy(src_ref=src, dst_ref=dst, send_sem=sems[0], recv_sem=sems[1], device_id=to, device_id_type=MESH)


def _other_chips(x, y):
    return [(1 - x, y), (x, 1 - y), (1 - x, 1 - y)]


def _u_gather_ici(w_sh, idxs):
    def copies(ins, outs, sem):
        x, y, c = _me()
        res = []
        for j, (px, py) in enumerate(_other_chips(x, y)):
            for n, i in enumerate(idxs):
                src = ins[n].at[0, _half_rows(i, c, 16)]
                res.append((_rc(src, outs[n].at[2 * x + y, _half_rows(i, c, 16)], sem(j * len(idxs) + n), (px, py, c)),
                            _rc(src, outs[n].at[2 * px + py, _half_rows(i, c, 16)], sem(j * len(idxs) + n), (px, py, c))))
        return res

    return dict(ins=[w_sh[i] for i in idxs], outs=[jax.ShapeDtypeStruct((4,) + SH_SHAPES[i], BF16) for i in idxs],
                nsem=3 * len(idxs), alias={}, copies=copies)


def _u_gather_d2d(got, idxs):
    def copies(ins, outs, sem):
        x, y, c = _me()
        res = []
        for j, (px, py) in enumerate(_other_chips(x, y)):
            for n, i in enumerate(idxs):
                src = ins[n].at[2 * px + py, _half_rows(i, c, 16)]
                res.append((_rc(src, outs[n].at[2 * px + py, _half_rows(i, c, 16)], sem(j * len(idxs) + n), (x, y, 1 - c)),
                            _rc(src, outs[n].at[2 * px + py, _half_rows(i, 1 - c, 16)], sem(j * len(idxs) + n), (x, y, 1 - c))))
        return res

    return dict(ins=list(got), outs=[jax.ShapeDtypeStruct(g.shape, g.dtype) for g in got], nsem=3 * len(idxs),
                alias={n: n for n in range(len(idxs))}, copies=copies)


def _u_pair_send(gs, idxs):
    def copies(ins, outs, sem):
        x, y, c = _me()
        res = []
        for n, i in enumerate(idxs):
            for sh in range(4):
                cp = _rc(ins[n].at[sh, _half_rows(i, 1 - c, 8)], outs[n].at[sh], sem(4 * n + sh), (x, y, 1 - c))
                res.append((cp, cp))
        return res

    return dict(ins=list(gs), outs=[jax.ShapeDtypeStruct((4, SH_SHAPES[i][0] // 2, SH_SHAPES[i][1]), F32) for i in idxs],
                nsem=4 * len(idxs), alias={}, copies=copies)


def _u_chip_exchange(ts):
    def copies(ins, outs, sem):
        x, y, c = _me()
        res = []
        for j, (px, py) in enumerate(_other_chips(x, y)):
            for n in range(len(ts)):
                cp = _rc(ins[n].at[2 * px + py], outs[n].at[j], sem(j * len(ts) + n), (px, py, c))
                res.append((cp, cp))
        return res

    return dict(ins=list(ts), outs=[jax.ShapeDtypeStruct((3,) + t.shape[1:], t.dtype) for t in ts], nsem=3 * len(ts),
                alias={}, copies=copies)


def _u_pair_join(hs):
    def copies(ins, outs, sem):
        x, y, c = _me()
        res = []
        for n in range(len(hs)):
            cp = _rc(ins[n], outs[n], sem(n), (x, y, 1 - c))
            res.append((cp, cp))
        return res

    return dict(ins=list(hs), outs=[jax.ShapeDtypeStruct(h.shape, h.dtype) for h in hs], nsem=len(hs), alias={}, copies=copies)


def _comm_phase(units, ci, co, send_sems, recv_sems, start):
    ii = oo = off = 0
    for u in units:
        ni, no = len(u["ins"]), len(u["outs"])
        for st, arrival in u["copies"](ci[ii:ii + ni], co[oo:oo + no], lambda k, off=off: (send_sems.at[off + k], recv_sems.at[off + k])):
            if start:
                st.start()
            else:
                st.wait_send()
                arrival.wait_recv()
        ii, oo, off = ii + ni, oo + no, off + u["nsem"]


def _carry(units, n_in, n_out):
    ins = [a for u in units for a in u["ins"]]
    outs = [o for u in units for o in u["outs"]]
    alias, ii, oo = {}, 0, 0
    for u in units:
        for a, b in u["alias"].items():
            alias[n_in + ii + a] = n_out + oo + b
        ii, oo = ii + len(u["ins"]), oo + len(u["outs"])
    nsem = sum(u["nsem"] for u in units)
    scratch = [pltpu.SemaphoreType.DMA((nsem,)), pltpu.SemaphoreType.DMA((nsem,))] if units else []
    return ins, outs, alias, scratch


def _split_units(units, res):
    out, oo = [], 0
    for u in units:
        out.append(list(res[oo:oo + len(u["outs"])]))
        oo += len(u["outs"])
    return out


def _comm_call(name, units):
    ins, outs, alias, scratch = _carry(units, 0, 0)

    def body(*refs):
        ci, co = refs[:len(ins)], refs[len(ins):len(ins) + len(outs)]
        _comm_phase(units, ci, co, refs[-2], refs[-1], True)
        _comm_phase(units, ci, co, refs[-2], refs[-1], False)

    res = pl.pallas_call(body, name=name, out_shape=outs, in_specs=[HBM] * len(ins), out_specs=[HBM] * len(outs),
                         scratch_shapes=scratch, input_output_aliases=alias)(*ins)
    return _split_units(units, res)


SEM = pl.BlockSpec(memory_space=pltpu.SEMAPHORE)
EFFECT = pltpu.SideEffectType.DATAFLOW_SIDE_EFFECTING


def _unit_start(unit, name, after=()):
    bufs = list(unit["ins"]) + [lax.empty(o.shape, o.dtype) for o in unit["outs"]]
    n_i, n_b, ns = len(unit["ins"]), len(bufs), unit["nsem"]

    def body(*refs):
        send_sems, recv_sems = refs[n_b + len(after)], refs[n_b + len(after) + 1]
        for st, _ in unit["copies"](refs[:n_i], refs[n_i:n_b], lambda k: (send_sems.at[k], recv_sems.at[k])):
            st.start()
        refs[-1][...] = jnp.zeros_like(refs[-1])

    res = pl.pallas_call(
        body, name=name,
        out_shape=[pltpu.SemaphoreType.DMA((ns,)), pltpu.SemaphoreType.DMA((ns,))] + [pltpu.HBM(b.shape, b.dtype) for b in bufs]
        + [jax.ShapeDtypeStruct((8, 128), F32)],
        in_specs=[HBM] * n_b + [pl.BlockSpec(memory_space=pl.ANY)] * len(after), out_specs=[SEM, SEM] + [HBM] * n_b + [VMEM_SPEC],
        input_output_aliases={i: 2 + i for i in range(n_b)},
        compiler_params=pltpu.CompilerParams(has_side_effects=EFFECT),
    )(*[pltpu.with_memory_space_constraint(b, pltpu.HBM) for b in bufs], *after)
    return res[0], res[1], list(res[2:2 + n_b]), res[-1]


def _unit_wait(unit, send_sems, recv_sems, bufs, after, name):
    n_i, n_b = len(unit["ins"]), len(bufs)

    def body(*refs):
        ss, rs = refs[n_b], refs[n_b + 1]
        for st, arrival in unit["copies"](refs[:n_i], refs[n_i:n_b], lambda k: (ss.at[k], rs.at[k])):
            st.wait_send()
            arrival.wait_recv()

    res = pl.pallas_call(
        body, name=name, out_shape=[pltpu.HBM(b.shape, b.dtype) for b in bufs],
        in_specs=[HBM] * n_b + [SEM, SEM] + [pl.BlockSpec(memory_space=pl.ANY)] * len(after), out_specs=[HBM] * n_b,
        input_output_aliases={i: i for i in range(n_b)}, compiler_params=pltpu.CompilerParams(has_side_effects=EFFECT),
    )(*bufs, send_sems, recv_sems, *after)
    return list(res[:n_i]), list(res[n_i:])


def _pair_add(g, land, core, name):
    _, rows, cols = g.shape
    half = rows // 2
    rb = _tile(half, 512, 16)
    nb = half // rb

    def body(c_ref, g_ref, l_ref, o_ref):
        o_ref[...] = (g_ref[...] + l_ref[...]).astype(BF16)

    return pl.pallas_call(
        body, name=name,
        grid_spec=pltpu.PrefetchScalarGridSpec(
            num_scalar_prefetch=1, grid=(4, nb),
            in_specs=[pl.BlockSpec((1, rb, cols), lambda s, i, c_ref: (s, c_ref[0] * nb + i, 0)),
                      pl.BlockSpec((1, rb, cols), lambda s, i, c_ref: (s, i, 0))],
            out_specs=pl.BlockSpec((1, rb, cols), lambda s, i, c_ref: (s, i, 0))),
        out_shape=jax.ShapeDtypeStruct((4, half, cols), BF16),
        compiler_params=_params(("parallel", "parallel")),
    )(core, g, land)


def _chip_sum(t, r, chip, name):
    _, half, cols = t.shape
    rb = _tile(half, 512, 16)

    def body(s_ref, t_ref, r_ref, o_ref):
        o_ref[...] = ((t_ref[0].astype(F32) + r_ref[0].astype(F32)) + r_ref[1].astype(F32)) + r_ref[2].astype(F32)

    return pl.pallas_call(
        body, name=name,
        grid_spec=pltpu.PrefetchScalarGridSpec(
            num_scalar_prefetch=1, grid=(half // rb,),
            in_specs=[pl.BlockSpec((1, rb, cols), lambda i, s_ref: (s_ref[0], i, 0)),
                      pl.BlockSpec((3, rb, cols), lambda i, s_ref: (0, i, 0))],
            out_specs=pl.BlockSpec((rb, cols), lambda i, s_ref: (i, 0))),
        out_shape=jax.ShapeDtypeStruct((half, cols), F32),
        compiler_params=_params(("parallel",)),
    )(chip, t, r)


def _adam_math(wv, gv, mv, vv):
    mn = ADAM_B1 * mv + (1.0 - ADAM_B1) * gv
    vn = ADAM_B2 * vv + (1.0 - ADAM_B2) * (gv * gv)
    m_hat = mn / (1.0 - ADAM_B1 ** ADAM_STEP)
    v_hat = vn / (1.0 - ADAM_B2 ** ADAM_STEP)
    return -ADAM_LR * (m_hat / (jnp.sqrt(v_hat) + ADAM_EPS) + ADAM_WD * wv), mn, vn


def _adamw_halves(wt, mt, vt, mine, theirs, core, name):
    _, rows, cols = wt.shape
    half = rows // 2
    rb = _tile(half, 256, 8)
    nb = half // rb

    def body(c_ref, w_ref, m_ref, v_ref, a_ref, b_ref, g_ref, d_ref, mo_ref, vo_ref):
        gv = jnp.where(pl.program_id(0) == c_ref[0], a_ref[...], b_ref[...])
        dl, mn, vn = _adam_math(w_ref[...], gv, m_ref[...], v_ref[...])
        g_ref[...] = gv
        d_ref[...] = dl
        mo_ref[...] = mn
        vo_ref[...] = vn

    full = pl.BlockSpec((None, rb, cols), lambda hf, i, c_ref: (0, hf * nb + i, 0))
    part = pl.BlockSpec((rb, cols), lambda hf, i, c_ref: (i, 0))
    return pl.pallas_call(
        body, name=name,
        grid_spec=pltpu.PrefetchScalarGridSpec(num_scalar_prefetch=1, grid=(2, nb), in_specs=[full, full, full, part, part],
                                               out_specs=[full] * 4),
        out_shape=[jax.ShapeDtypeStruct((1, rows, cols), F32)] * 4,
        compiler_params=_params(("parallel", "parallel")),
    )(core, wt, mt, vt, mine, theirs)


SG_REP = 144
SG_LOSS = 136
SG_W2, SG_CW = SG_REP, SG_REP + 4 * 16
SG_ROWS = SG_CW + 4 * 40
SP_ROWS = SG_REP + 16 + 40


def _mod_shard(c_all, ada_w_sh):
    def body(c_ref, w_ref, o_ref):
        cv = c_ref[...]
        o_ref[...] = _dg((cv * _sigmoid(cv)).astype(BF16), w_ref[...].astype(BF16), 1, 0)

    return pl.pallas_call(body, name="mod_shard", out_shape=jax.ShapeDtypeStruct((8, 1536), F32),
                          in_specs=[VMEM_SPEC, VMEM_SPEC], out_specs=VMEM_SPEC,
                          compiler_params=pltpu.CompilerParams(vmem_limit_bytes=VMEM_LIMIT))(c_all, ada_w_sh)


def _mod_select(mod_all, ada_b4):
    def body(m_ref, b_ref, o_ref):
        x, y, c = _me()
        me = 4 * x + 2 * y + c
        for sh in range(4):
            o_ref[sh] = m_ref[2 * sh, me] + b_ref[sh]

    return pl.pallas_call(body, name="mod_select", out_shape=jax.ShapeDtypeStruct((4, 12, 128), F32),
                          in_specs=[VMEM_SPEC, VMEM_SPEC], out_specs=VMEM_SPEC)(mod_all, ada_b4)


def _small_reduce(sg_all):
    def body(g_ref, o_ref):
        x, y, c = _me()
        s_me = 2 * x + y
        w2_rows = pl.ds(pl.multiple_of(SG_W2 + 16 * s_me, 8), 16)
        cw_rows = pl.ds(pl.multiple_of(SG_CW + 40 * s_me, 8), 40)
        a = g_ref[0, 0:SG_REP, :]
        b = g_ref[0, w2_rows, :]
        d = g_ref[0, cw_rows, :]
        for dev in range(1, 8):
            a = a + g_ref[dev, 0:SG_REP, :]
            b = b + g_ref[dev, w2_rows, :]
            d = d + g_ref[dev, cw_rows, :]
        o_ref[0:SG_REP, :] = a
        o_ref[SG_REP:SG_REP + 16, :] = b
        o_ref[SG_REP + 16:SP_ROWS, :] = d

    return pl.pallas_call(body, name="small_grad_reduce", out_shape=jax.ShapeDtypeStruct((SP_ROWS, 128), F32),
                          in_specs=[VMEM_SPEC], out_specs=VMEM_SPEC)(sg_all)


def _ada_grad(dmod_all, c_bc):
    def body(g_ref, c_ref, o_ref):
        x, y, c = _me()
        s_me = 2 * x + y
        for k in range(12):
            acc = jnp.zeros((D, 128), F32)
            for b in range(8):
                cv = c_ref[b]
                acc = acc + (cv * _sigmoid(cv)) * g_ref[s_me, k, b:b + 1, :]
            o_ref[:, k * 128:(k + 1) * 128] = acc

    return pl.pallas_call(body, name="ada_w_grad", out_shape=jax.ShapeDtypeStruct((D, 1536), F32),
                          in_specs=[VMEM_SPEC, VMEM_SPEC], out_specs=VMEM_SPEC,
                          compiler_params=pltpu.CompilerParams(vmem_limit_bytes=VMEM_LIMIT))(dmod_all, c_bc)


def _adamw(wt, g, m, v, name):
    rows, cols = wt.shape
    rb = _tile(rows, 256, 8)

    def fn(c, i, wv, gv, mv, vv):
        return _adam_math(wv, gv, mv, vv)

    return _rowcall(fn, [_rows(t, rb) for t in (wt, g, m, v)], [_orow(rows, cols, F32, rb)] * 3,
                    n_rows=rows, rb=rb, name=name)


def _pad_rows(t, rows):
    flat = t.reshape(-1)
    return jnp.pad(flat, (0, rows * 128 - flat.shape[0])).reshape(rows, 128)


SP_LAYOUT = (("ada_b", 48), ("norm1_w", 8), ("gla_gate_b", 8), ("gla_norm_w", 8), ("norm2_w", 8), ("conv_b", 48),
             ("final_norm_w", 8), (None, 8), ("gla_gate_w2", 16), ("conv_w", 40))


def _pack_small(d):
    return jnp.concatenate([jnp.zeros((rows, 128), F32) if n is None else _pad_rows(d[n].astype(F32), rows)
                            for n, rows in SP_LAYOUT], axis=0)


def _unpack_small(pk, shapes):
    out, off = {}, 0
    for n, rows in SP_LAYOUT:
        if n is not None:
            shp = shapes[n]
            out[n] = pk[off:off + rows].reshape(-1)[:math.prod(shp)].reshape(shp)
        off += rows
    return out


def kernel(x, c, positions, ada_w, ada_b, norm1_w, w_in, gla_gate_w2, gla_gate_b, gla_norm_w, w_gla_branch, w_attn_branch, w_out, norm2_w, w_up, conv_w, conv_b, w_down, final_norm_w, loss_target, m_ada_w, m_ada_b, m_norm1_w, m_w_in, m_gla_gate_w2, m_gla_gate_b, m_gla_norm_w, m_w_gla_branch, m_w_attn_branch, m_w_out, m_norm2_w, m_w_up, m_conv_w, m_conv_b, m_w_down, m_final_norm_w, v_ada_w, v_ada_b, v_norm1_w, v_w_in, v_gla_gate_w2, v_gla_gate_b, v_gla_norm_w, v_w_gla_branch, v_w_attn_branch, v_w_out, v_norm2_w, v_w_up, v_conv_w, v_conv_b, v_w_down, v_final_norm_w):
    s = x.shape[1]
    names = ("ada_w", "ada_b", "norm1_w", "w_in", "gla_gate_w2", "gla_gate_b", "gla_norm_w", "w_gla_branch", "w_attn_branch",
             "w_out", "norm2_w", "w_up", "conv_w", "conv_b", "w_down", "final_norm_w")
    wts = dict(zip(names, (ada_w, ada_b, norm1_w, w_in, gla_gate_w2, gla_gate_b, gla_norm_w, w_gla_branch, w_attn_branch,
                           w_out, norm2_w, w_up, conv_w, conv_b, w_down, final_norm_w)))
    ms = dict(zip(names, (m_ada_w, m_ada_b, m_norm1_w, m_w_in, m_gla_gate_w2, m_gla_gate_b, m_gla_norm_w, m_w_gla_branch,
                          m_w_attn_branch, m_w_out, m_norm2_w, m_w_up, m_conv_w, m_conv_b, m_w_down, m_final_norm_w)))
    vs = dict(zip(names, (v_ada_w, v_ada_b, v_norm1_w, v_w_in, v_gla_gate_w2, v_gla_gate_b, v_gla_norm_w, v_w_gla_branch,
                          v_w_attn_branch, v_w_out, v_norm2_w, v_w_up, v_conv_w, v_conv_b, v_w_down, v_final_norm_w)))

    pk0 = jnp.concatenate([_pad_rows(c, 8), _pad_rows(gla_gate_w2, 16), _pad_rows(conv_w, 40)], axis=0)
    sm_all = _allgather8(pk0, "gather_small")
    c_all = sm_all[:, 0:8, :].reshape(8, D)
    w2_full = sm_all[0::2, 8:24, :].transpose(1, 0, 2).reshape(GLA_LR, 512)
    cw_full = sm_all[0::2, 24:64, :].reshape(4, 40 * 128)[:, :3 * W_UP_SH].reshape(4, 3, W_UP_SH).transpose(1, 0, 2).reshape(3, 2 * D_FF)

    mod_sh = _mod_shard(c_all, ada_w[0])
    mod_all = _allgather8(mod_sh.reshape(96, 128), "gather_mod")

    w_sh = [wts[n].astype(BF16) for n in BIG]
    u_g0 = _u_gather_ici(w_sh, (0,))
    g0 = (u_g0,) + _unit_start(u_g0, "gather_w_in_start", after=[mod_all])
    mod = _mod_select(mod_all.reshape(8, 8, 12, 128) + g0[4][0, 0], ada_b.reshape(4, 12, 128)).reshape(6, D)

    core = lax.axis_index("c").astype(jnp.int32).reshape(1)
    chip = (2 * lax.axis_index("x") + lax.axis_index("y")).astype(jnp.int32)
    sm = dict(n1w=norm1_w, n2w=norm2_w, fnw=final_norm_w.reshape(1, D), gnw=gla_norm_w, gb=gla_gate_b,
              w2=jnp.pad(w2_full, ((0, 128 - GLA_LR), (0, 0))), cw=_ff_to_kernel(cw_full), cb=_ff_to_kernel(conv_b))
    loss, grad_x, halves, others, small, ts0 = _local_step(x[0], mod, positions.reshape(s, 1), loss_target[0], sm, w_sh,
                                                               g0, chip, core)

    dcw = _ff_from_kernel(small["cw"]).reshape(3, 4, W_UP_SH).transpose(1, 0, 2)
    dw2 = small["w2"][:GLA_LR].reshape(GLA_LR, 4, 128).transpose(1, 0, 2)
    sg = jnp.concatenate(
        [_pad_rows(small["dmod"], 48), _pad_rows(small["n1w"], 8), _pad_rows(small["gb"], 8), _pad_rows(small["gnw"], 8),
         _pad_rows(small["n2w"], 8), _pad_rows(_ff_from_kernel(small["cb"]), 48), _pad_rows(small["fnw"], 8), _pad_rows(loss, 8)]
        + [_pad_rows(dw2[k], 16) for k in range(4)] + [_pad_rows(dcw[k], 40) for k in range(4)], axis=0)
    sg_all = _allgather8(sg, "gather_small_grads")
    u_ex = _u_chip_exchange([ts0])
    pending = (u_ex,) + _unit_start(u_ex, "grad_exchange_w_in_start", after=[sg_all])
    sg_all = sg_all + pending[4][0, 0]
    g_small_pk = _small_reduce(sg_all)
    dmod_all = sg_all[:, 0:48, :].reshape(8, 4, 12, 128).transpose(1, 2, 0, 3)
    g_ada_w = _ada_grad(dmod_all, jnp.broadcast_to(c_all[:, :, None], (8, D, 128)))

    shapes = {n: wts[n].shape for n in names}
    g_small = _unpack_small(g_small_pk, shapes)
    grads = {"ada_w": g_ada_w.reshape(1, D, 1536), **g_small}
    deltas, new_m, new_v = {}, {}, {}
    for n, mine, theirs in zip(BIG[1:], halves, others):
        grads[n], deltas[n], new_m[n], new_v[n] = _adamw_halves(wts[n], ms[n], vs[n], mine, theirs, core, "adamw_" + n)
    shp = ada_w.shape
    d_, m_, v_ = _adamw(ada_w[0], g_ada_w, m_ada_w[0], v_ada_w[0], "adamw_ada_w")
    deltas["ada_w"], new_m["ada_w"], new_v["ada_w"] = d_.reshape(shp), m_.reshape(shp), v_.reshape(shp)
    d_, m_, v_ = _adamw(_pack_small(wts), g_small_pk, _pack_small(ms), _pack_small(vs), "adamw_small")
    for dst, pk in ((deltas, d_), (new_m, m_), (new_v, v_)):
        dst.update(_unpack_small(pk, shapes))

    [t0], [r0] = _unit_wait(*pending[:4], after=[d_, deltas["ada_w"], deltas["w_up"], deltas["w_down"]], name="grad_exchange_w_in_wait")
    half0 = _chip_sum(t0, r0, chip.reshape(1), "grad_chip_sum_w_in")
    [[oth0]] = _comm_call("grad_join_w_in", [_u_pair_join([half0])])
    grads["w_in"], deltas["w_in"], new_m["w_in"], new_v["w_in"] = _adamw_halves(w_in, m_w_in, v_w_in, half0, oth0, core, "adamw_w_in")

    return (g_small_pk[SG_LOSS, 0], grad_x.reshape(1, s, D), *[grads[n] for n in names], *[deltas[n] for n in names],
            *[new_m[n] for n in names], *[new_v[n] for n in names])
```

```python
import math

import jax
import jax.numpy as jnp
from jax import lax
from jax.experimental import pallas as pl
from jax.experimental.pallas import tpu as pltpu

F32, BF16 = jnp.float32, jnp.bfloat16
MESH = pl.DeviceIdType.MESH

D = 1024
EPS = 1e-6
GLA_H, GLA_DK, GLA_DV, GLA_LR = 4, 128, 256, 16
GLA_TAU = 16.0
GLA_CHUNK = 64
GLA_BLOCK = 512
ATT_GROUPS = ((128, 1), (512, 4), (2048, 16))
ATT_BLK = 128
ATT_HD = 64
ATT_W = 768
D_FF = 2816
ROPE_THETA = 10000.0
P_W = 7680
P_GV, P_GR, P_MA, P_MB, P_GQ, P_GK, P_AQ, P_AK, P_AV, P_LR = 0, 1024, 2048, 3072, 4096, 4608, 5120, 5888, 6656, 7424
W_IN = 7440
W_IN_SH, W_UP_SH, W_DOWN_SH = 1860, 1408, 704
VMEM_LIMIT = 56 * 1024 * 1024
ADAM_LR, ADAM_B1, ADAM_B2, ADAM_EPS, ADAM_WD, ADAM_STEP = 0.001, 0.9, 0.999, 1e-08, 0.01, 10
NEG = -1e30


def _tile(n, target, unit=128):
    best = None
    for t in range(unit, min(n, target) + 1, unit):
        if n % t == 0:
            best = t
    return best or n


def _params(sem):
    return pltpu.CompilerParams(dimension_semantics=sem, vmem_limit_bytes=VMEM_LIMIT)


def _dg(a, b, ca, cb):
    return lax.dot_general(a, b, (((ca,), (cb,)), ((), ())), preferred_element_type=F32)


def _sigmoid(v):
    return 1.0 / (1.0 + jnp.exp(-v))


def _ff_block(j):
    return (j % 2) * 2 + j // 2


def _mm(a, b, name, *, ta=False, tb=False, out_dtype=BF16, tm=1024, tn=1536, tk=1024, n_outer=True, comm=(),
        b_shards=False, o_shards=False):
    m = a.shape[1] if ta else a.shape[0]
    k = a.shape[0] if ta else a.shape[1]
    if b_shards:
        n = b.shape[1] if tb else 4 * W_UP_SH
        tn, tk = (tn, W_UP_SH) if tb else (W_UP_SH, tk)
    else:
        n = b.shape[0] if tb else b.shape[1]
    if o_shards:
        tn = W_UP_SH
    tm, tn, tk = _tile(m, tm), _tile(n, tn), _tile(k, tk)
    nm, nn, nk = m // tm, n // tn, k // tk
    in_out = out_dtype == F32
    c_ins, c_outs, c_alias, c_scratch = _carry(comm, 2, 1)

    def body(a_ref, b_ref, *rest):
        ci, o_ref, co = rest[:len(c_ins)], rest[len(c_ins)], rest[len(c_ins) + 1:len(c_ins) + 1 + len(c_outs)]
        scr = rest[len(c_ins) + 1 + len(c_outs):]
        kk = pl.program_id(2)
        if comm:
            step = (pl.program_id(0) * (nm if n_outer else nn) + pl.program_id(1)) * nk + kk

            @pl.when(step == 0)
            def _():
                _comm_phase(comm, ci, co, scr[-2], scr[-1], True)

        _mm_step(a_ref, b_ref, o_ref, scr, kk)
        if comm:
            @pl.when(step == nm * nn * nk - 1)
            def _():
                _comm_phase(comm, ci, co, scr[-2], scr[-1], False)

    def _mm_step(a_ref, b_ref, o_ref, scr, kk):
        p = _dg(a_ref[...].astype(BF16), b_ref[...].astype(BF16), 0 if ta else 1, 1 if tb else 0)
        if nk == 1:
            o_ref[...] = p.astype(o_ref.dtype)
        else:
            acc = o_ref if in_out else scr[0]

            @pl.when(kk == 0)
            def _():
                acc[...] = p

            @pl.when(kk > 0)
            def _():
                acc[...] += p

            if not in_out:
                @pl.when(kk == nk - 1)
                def _():
                    o_ref[...] = acc[...].astype(o_ref.dtype)

    if n_outer:
        ij = lambda g0, g1: (g1, g0)
        grid = (nn, nm, nk)
    else:
        ij = lambda g0, g1: (g0, g1)
        grid = (nm, nn, nk)
    a_map = (lambda g0, g1, kk: (kk, ij(g0, g1)[0])) if ta else (lambda g0, g1, kk: (ij(g0, g1)[0], kk))
    if b_shards and tb:
        b_spec = pl.BlockSpec((None, tn, tk), lambda g0, g1, kk: (_ff_block(kk), ij(g0, g1)[1], 0))
    elif b_shards:
        b_spec = pl.BlockSpec((None, tk, tn), lambda g0, g1, kk: (_ff_block(ij(g0, g1)[1]), kk, 0))
    elif tb:
        b_spec = pl.BlockSpec((tn, tk), lambda g0, g1, kk: (ij(g0, g1)[1], kk))
    else:
        b_spec = pl.BlockSpec((tk, tn), lambda g0, g1, kk: (kk, ij(g0, g1)[1]))
    if o_shards:
        o_spec = pl.BlockSpec((None, tm, tn), lambda g0, g1, kk: (_ff_block(ij(g0, g1)[1]), ij(g0, g1)[0], 0))
        o_shape = jax.ShapeDtypeStruct((4, m, W_UP_SH), out_dtype)
    else:
        o_spec = pl.BlockSpec((tm, tn), lambda g0, g1, kk: ij(g0, g1))
        o_shape = jax.ShapeDtypeStruct((m, n), out_dtype)
    res = pl.pallas_call(
        body, name=name, grid=grid,
        in_specs=[pl.BlockSpec((tk, tm) if ta else (tm, tk), a_map), b_spec] + [HBM] * len(c_ins),
        out_specs=[o_spec] + [HBM] * len(c_outs),
        out_shape=[o_shape] + c_outs,
        scratch_shapes=([] if (in_out or nk == 1) else [pltpu.VMEM((tm, tn), F32)]) + c_scratch,
        input_output_aliases=c_alias,
        compiler_params=_params(("arbitrary",) * 3 if comm else ("parallel", "parallel", "arbitrary")),
    )(a, b, *c_ins)
    return (res[0], _split_units(comm, res[1:])) if comm else res[0]


def _rows(arr, rb, w=None, j=0):
    w = arr.shape[1] if w is None else w
    if callable(j):
        return arr, pl.BlockSpec((rb, w), lambda c, i: (i, j(c)))
    return arr, pl.BlockSpec((rb, w), lambda c, i: (i, j))


def _full(arr, w=None, j=0):
    w = arr.shape[1] if w is None else w
    if callable(j):
        return arr, pl.BlockSpec((arr.shape[0], w), lambda c, i: (0, j(c)))
    return arr, pl.BlockSpec((arr.shape[0], w), lambda c, i: (0, j))


def _halo(arr, rb, hb, w, j, before):
    per = rb // hb
    last = arr.shape[0] // hb - 1
    if before:
        rmap = lambda i: jnp.maximum(i * per - 1, 0)
    else:
        rmap = lambda i: jnp.minimum((i + 1) * per, last)
    return arr, pl.BlockSpec((hb, w), lambda c, i: (rmap(i), j(c) if callable(j) else j))


def _rowcall(fn, ins, outs, *, n_rows, rb, name, ncol=1, into=None, after=()):
    n_in = len(ins)
    nr = n_rows // rb
    unread = ([] if into is None else [into[0]]) + list(after)
    n_skip = len(unread)

    def body(*refs):
        c, i = pl.program_id(0), pl.program_id(1)
        res = fn(c, i, *[r[...] for r in refs[:n_in]])
        for val, spec, o_ref in zip(res, outs, refs[n_in + n_skip:]):
            if spec[2] == "row":
                o_ref[...] = val.astype(o_ref.dtype)
            else:
                @pl.when(i == 0)
                def _(o_ref=o_ref, val=val):
                    o_ref[...] = val.astype(o_ref.dtype)

                @pl.when(i > 0)
                def _(o_ref=o_ref, val=val):
                    o_ref[...] += val.astype(o_ref.dtype)

    out_specs = []
    for shape, dt, kind, block, col in outs:
        if kind == "row":
            out_specs.append(pl.BlockSpec(block, lambda c, i, col=col: (i, col(c))))
        else:
            out_specs.append(pl.BlockSpec(block, lambda c, i, col=col: (0, col(c))))
    return pl.pallas_call(
        body, name=name, grid=(ncol, nr),
        in_specs=[s for _, s in ins] + [pl.BlockSpec(memory_space=pl.ANY)] * n_skip, out_specs=out_specs,
        out_shape=[jax.ShapeDtypeStruct(o[0], o[1]) for o in outs],
        input_output_aliases={} if into is None else {n_in: into[1]},
        compiler_params=_params(("parallel", "arbitrary")),
    )(*[a for a, _ in ins], *unread)


def _orow(n_rows, w, dt, rb, bw=None, col=lambda c: 0):
    return ((n_rows, w), dt, "row", (rb, bw or w), col)


def _oacc(r, w, bw=None, col=lambda c: 0):
    return ((r, w), F32, "acc", (r, bw or w), col)


def _csum(v):
    return jnp.sum(v, axis=0, keepdims=True)


def _rms(v):
    return lax.rsqrt(jnp.mean(v * v, axis=-1, keepdims=True) + EPS)


def _norm_bwd(xv, dh, w, scale):
    r = _rms(xv)
    xh = xv * r
    dxh = dh * (w * (1.0 + scale))
    dx = r * (dxh - xh * jnp.mean(dxh * xh, axis=-1, keepdims=True))
    t = dh * xh
    return dx, _csum(dh), _csum(t * w), _csum(t * (1.0 + scale))


def _rope_tables(pos_col, invf, s):
    def fn(c, i, pos, f):
        ang = pos.astype(F32) * f
        lane = lax.broadcasted_iota(jnp.int32, ang.shape, 1)
        sign = jnp.where((lane % ATT_HD) < ATT_HD // 2, -1.0, 1.0)
        return jnp.cos(ang), jnp.sin(ang) * sign

    rb = 512
    return _rowcall(fn, [_rows(pos_col, rb), _full(invf)], [_orow(s, 128, F32, rb), _orow(s, 128, F32, rb)],
                    n_rows=s, rb=rb, name="rope_tables")


def _swap_halves(t):
    n = t.shape[1]
    lane = lax.broadcasted_iota(jnp.int32, t.shape, 1)
    return jnp.where((lane % ATT_HD) < ATT_HD // 2, pltpu.roll(t, n - 32, 1), pltpu.roll(t, 32, 1))


def _rope_apply(t, cos, sin_signed, inverse):
    cw = jnp.concatenate([cos] * (t.shape[1] // 128), axis=1)
    sw = jnp.concatenate([sin_signed] * (t.shape[1] // 128), axis=1)
    if inverse:
        sw = -sw
    return t * cw + _swap_halves(t) * sw


DIL_ROWS = 512


def _to_dilated(scr, val, out_ref, r):
    if r == 1:
        out_ref[...] = val.astype(out_ref.dtype)
        return
    n = val.shape[0] // r
    for hh in range(2):
        scr[hh] = val[:, hh * 128:(hh + 1) * 128]
        for pr in range(r):
            out_ref[:, pr * 256 + hh * 128:pr * 256 + (hh + 1) * 128] = scr[hh, pl.ds(pr, n, stride=r), :].astype(out_ref.dtype)


def _from_dilated(scr, in_ref, r):
    if r == 1:
        return in_ref[...].astype(F32)
    n = in_ref.shape[0]
    for hh in range(2):
        for pr in range(r):
            scr[hh, pl.ds(pr, n, stride=r), :] = in_ref[:, pr * 256 + hh * 128:pr * 256 + (hh + 1) * 128].astype(F32)
    return jnp.concatenate([scr[0], scr[1]], axis=1)


def _dil_spec(r):
    return pl.BlockSpec((DIL_ROWS // r, r * 256), lambda i: (i, 0))


def _dil_shape(s, r, dt):
    return jax.ShapeDtypeStruct((s // r, r * 256), dt)


_DIL_SCRATCH = [pltpu.VMEM((2, DIL_ROWS, 128), F32)]
_RS = tuple(r for _, r in ATT_GROUPS)


def _rope_fwd(p, cos_t, sin_t, s):
    def body(*refs):
        ins, cs, sn, outs, scr = refs[:9], refs[9][...], refs[10][...], refs[11:20], refs[20]
        for t in range(3):
            for g, r in enumerate(_RS):
                val = ins[3 * t + g][...].astype(F32)
                _to_dilated(scr, _rope_apply(val, cs, sn, False) if t < 2 else val, outs[3 * t + g], r)

    res = pl.pallas_call(
        body, name="rope", grid=(s // DIL_ROWS,),
        in_specs=[pl.BlockSpec((DIL_ROWS, 256), lambda i, c=base // 256 + g: (i, c)) for base in (P_AQ, P_AK, P_AV) for g in range(3)]
        + [pl.BlockSpec((DIL_ROWS, 128), lambda i: (i, 0))] * 2,
        out_specs=[_dil_spec(r) for _ in range(3) for r in _RS],
        out_shape=[_dil_shape(s, r, BF16) for _ in range(3) for r in _RS],
        scratch_shapes=_DIL_SCRATCH, compiler_params=_params(("parallel",)),
    )(*([p] * 9), cos_t, sin_t)
    return res[0:3], res[3:6], res[6:9]


def _attn_combine(att, s):
    def body(o0, o1, o2, l0, l1, l2, o_ref, lse_ref, od1, od2, ld1, ld2, scr):
        ov = [_from_dilated(scr, ref, r) for ref, r in zip((o0, o1, o2), _RS)]
        lv = [_from_dilated(scr, ref, r) for ref, r in zip((l0, l1, l2), _RS)]
        mx = jnp.maximum(jnp.maximum(lv[0], lv[1]), lv[2])
        ev = [jnp.exp(l - mx) for l in lv]
        z = ev[0] + ev[1] + ev[2]
        o = ((ev[0] * ov[0] + ev[1] * ov[1] + ev[2] * ov[2]) / z).astype(BF16)
        lse = mx + jnp.log(z)
        o_ref[...] = o
        lse_ref[...] = lse
        for ref, r in zip((od1, od2), _RS[1:]):
            _to_dilated(scr, o.astype(F32), ref, r)
        for ref, r in zip((ld1, ld2), _RS[1:]):
            _to_dilated(scr, lse, ref, r)

    return pl.pallas_call(
        body, name="attn_combine", grid=(s // DIL_ROWS,),
        in_specs=[_dil_spec(r) for r in _RS] * 2,
        out_specs=[_dil_spec(1)] * 2 + [_dil_spec(r) for r in _RS[1:]] * 2,
        out_shape=[_dil_shape(s, 1, BF16), _dil_shape(s, 1, F32)] + [_dil_shape(s, r, BF16) for r in _RS[1:]]
        + [_dil_shape(s, r, F32) for r in _RS[1:]],
        scratch_shapes=_DIL_SCRATCH, compiler_params=_params(("parallel",)),
    )(*[a[0] for a in att], *[a[1] for a in att])


def _dilate(t, s):
    def body(t_ref, o1, o2, scr):
        val = t_ref[...].astype(F32)
        for ref, r in zip((o1, o2), _RS[1:]):
            _to_dilated(scr, val, ref, r)

    return pl.pallas_call(
        body, name="attn_dilate", grid=(s // DIL_ROWS,), in_specs=[_dil_spec(1)], out_specs=[_dil_spec(r) for r in _RS[1:]],
        out_shape=[_dil_shape(s, r, t.dtype) for r in _RS[1:]], scratch_shapes=_DIL_SCRATCH, compiler_params=_params(("parallel",)),
    )(t)


def _rope_bwd(datt, d_glr, dp, cos_t, sin_t, s):
    tail = P_W - P_AQ

    def body(*refs):
        ins, cs, sn, glr, o_ref, scr = refs[:9], refs[9][...], refs[10][...], refs[11], refs[13], refs[14]
        for t in range(3):
            for g, r in enumerate(_RS):
                val = _from_dilated(scr, ins[3 * t + g], r)
                o_ref[:, t * ATT_W + g * 256:t * ATT_W + (g + 1) * 256] = (_rope_apply(val, cs, sn, True) if t < 2 else val).astype(BF16)
        o_ref[:, 3 * ATT_W:3 * ATT_W + 128] = glr[...]
        o_ref[:, 3 * ATT_W + 128:] = jnp.zeros((DIL_ROWS, tail - 3 * ATT_W - 128), BF16)

    return pl.pallas_call(
        body, name="rope_bwd", grid=(s // DIL_ROWS,),
        in_specs=[_dil_spec(r) for _ in range(3) for r in _RS] + [pl.BlockSpec((DIL_ROWS, 128), lambda i: (i, 0))] * 3
        + [pl.BlockSpec(memory_space=pl.ANY)],
        out_specs=pl.BlockSpec((DIL_ROWS, tail), lambda i: (i, P_AQ // tail)),
        out_shape=jax.ShapeDtypeStruct((s, P_W), BF16), input_output_aliases={12: 0},
        scratch_shapes=_DIL_SCRATCH, compiler_params=_params(("parallel",)),
    )(*[datt[g][t] for t in range(3) for g in range(3)], cos_t, sin_t, d_glr, dp)


def _tri_dot(tri, t):
    tb = tri.astype(BF16)
    hi = t.astype(BF16)
    r1 = t - hi.astype(F32)
    mid = r1.astype(BF16)
    lo = (r1 - mid.astype(F32)).astype(BF16)
    return _dg(tb, hi, 1, 0) + _dg(tb, mid, 1, 0) + _dg(tb, lo, 1, 0)


def _gla_decays(la_c, tri):
    b = _tri_dot(tri, la_c)
    row = lax.broadcasted_iota(jnp.int32, b.shape, 0)
    bmid = jnp.sum(jnp.where(row == GLA_CHUNK // 2 - 1, b, 0.0), axis=0, keepdims=True)
    blast = jnp.sum(jnp.where(row == GLA_CHUNK - 1, b, 0.0), axis=0, keepdims=True)
    return b, bmid, blast


def _gla_fwd(p, la, s, comm=()):
    tb, ch = GLA_BLOCK, GLA_CHUNK
    nb, nc = s // tb, tb // ch
    scale = GLA_DK ** -0.5
    c_ins, c_outs, c_alias, c_scratch = _carry(comm, 4, 2)

    def body(q_ref, k_ref, v_ref, la_ref, *rest):
        ci, (o_ref, st_ref) = rest[:len(c_ins)], rest[len(c_ins):len(c_ins) + 2]
        co, state = rest[len(c_ins) + 2:len(c_ins) + 2 + len(c_outs)], rest[len(c_ins) + 2 + len(c_outs)]
        step = pl.program_id(0)
        if comm:
            @pl.when(step == 0)
            def _():
                _comm_phase(comm, ci, co, rest[-2], rest[-1], True)

        _gla_fwd_step(q_ref, k_ref, v_ref, la_ref, o_ref, st_ref, state)
        if comm:
            @pl.when(step == nb - 1)
            def _():
                _comm_phase(comm, ci, co, rest[-2], rest[-1], False)

    def _gla_fwd_step(q_ref, k_ref, v_ref, la_ref, o_ref, st_ref, state):
        @pl.when(pl.program_id(0) == 0)
        def _():
            state[...] = jnp.zeros_like(state)

        ri = lax.broadcasted_iota(jnp.int32, (ch, ch), 0)
        ci = lax.broadcasted_iota(jnp.int32, (ch, ch), 1)
        causal = ci <= ri
        tri = causal.astype(F32)

        def chunk(c, carry):
            sl = pl.ds(pl.multiple_of(c * ch, ch), ch)
            b, bmid, blast = _gla_decays(la_ref[sl, :], tri)
            q = q_ref[sl, :].astype(F32) * scale
            k = k_ref[sl, :].astype(F32)
            v = v_ref[sl, :]
            qgt = (q * jnp.exp(b)).astype(BF16)
            qgn = (q * jnp.exp(b - bmid)).astype(BF16)
            kgn = (k * jnp.exp(bmid - b)).astype(BF16)
            kd = (k * jnp.exp(blast - b)).astype(BF16)
            dec = jnp.exp(blast)
            sts = [state[h] for h in range(GLA_H)]
            outs, news = [], []
            for h in range(GLA_H):
                hk, hv = slice(h * GLA_DK, (h + 1) * GLA_DK), slice(h * GLA_DV, (h + 1) * GLA_DV)
                a = jnp.where(causal, _dg(qgn[:, hk], kgn[:, hk], 1, 1), 0.0)
                outs.append(_dg(a.astype(BF16), v[:, hv], 1, 0) + _dg(qgt[:, hk], sts[h].astype(BF16), 1, 1))
                news.append(dec[:, hk] * sts[h] + _dg(v[:, hv], kd[:, hk], 0, 0))
            for h in range(GLA_H):
                st_ref[h, c] = sts[h]
                state[h] = news[h]
            o_ref[sl, :] = jnp.concatenate(outs, axis=1)
            return carry

        lax.fori_loop(0, nc, chunk, 0, unroll=2)

    hw = GLA_H * GLA_DK
    res = pl.pallas_call(
        body, name="gla_fwd", grid=(nb,),
        in_specs=[pl.BlockSpec((tb, hw), lambda t: (t, P_GQ // hw)),
                  pl.BlockSpec((tb, hw), lambda t: (t, P_GK // hw)),
                  pl.BlockSpec((tb, GLA_H * GLA_DV), lambda t: (t, P_GV // (GLA_H * GLA_DV))),
                  pl.BlockSpec((tb, hw), lambda t: (t, 0))] + [HBM] * len(c_ins),
        out_specs=[pl.BlockSpec((tb, GLA_H * GLA_DV), lambda t: (t, 0)),
                   pl.BlockSpec((GLA_H, nc, GLA_DV, GLA_DK), lambda t: (0, t, 0, 0))] + [HBM] * len(c_outs),
        out_shape=[jax.ShapeDtypeStruct((s, GLA_H * GLA_DV), F32),
                   jax.ShapeDtypeStruct((GLA_H, s // ch, GLA_DV, GLA_DK), F32)] + c_outs,
        scratch_shapes=[pltpu.VMEM((GLA_H, GLA_DV, GLA_DK), F32)] + c_scratch,
        input_output_aliases=c_alias,
        compiler_params=_params(("arbitrary",)),
    )(p, p, p, la, *c_ins)
    return res[0], res[1], _split_units(comm, res[2:])


def _gla_bwd(p, la, states, do, s, dp, comm=()):
    tb, ch = GLA_BLOCK, GLA_CHUNK
    nb, nc = s // tb, tb // ch
    scale = GLA_DK ** -0.5
    c_ins, c_outs, c_alias, c_scratch = _carry(comm, 7, 4)

    def body(q_ref, k_ref, v_ref, la_ref, st_ref, do_ref, dp_in, *rest):
        ci, outs = rest[:len(c_ins)], rest[len(c_ins):len(c_ins) + 4]
        co, dstate = rest[len(c_ins) + 4:len(c_ins) + 4 + len(c_outs)], rest[len(c_ins) + 4 + len(c_outs)]
        step = pl.program_id(0)
        if comm:
            @pl.when(step == 0)
            def _():
                _comm_phase(comm, ci, co, rest[-2], rest[-1], True)

        _gla_bwd_step(q_ref, k_ref, v_ref, la_ref, st_ref, do_ref, *outs, dstate)
        if comm:
            @pl.when(step == nb - 1)
            def _():
                _comm_phase(comm, ci, co, rest[-2], rest[-1], False)

    def _gla_bwd_step(q_ref, k_ref, v_ref, la_ref, st_ref, do_ref, dq_ref, dk_ref, dv_ref, dla_ref, dstate):
        @pl.when(pl.program_id(0) == 0)
        def _():
            dstate[...] = jnp.zeros_like(dstate)

        ri = lax.broadcasted_iota(jnp.int32, (ch, ch), 0)
        ci = lax.broadcasted_iota(jnp.int32, (ch, ch), 1)
        causal = ci <= ri
        tri = causal.astype(F32)
        tri_t = (ci >= ri).astype(F32)

        def chunk(cc, carry):
            c = nc - 1 - cc
            sl = pl.ds(pl.multiple_of(c * ch, ch), ch)
            b, bmid, blast = _gla_decays(la_ref[sl, :], tri)
            q = q_ref[sl, :].astype(F32) * scale
            k = k_ref[sl, :].astype(F32)
            v = v_ref[sl, :]
            e_b, e_qn, e_kn, e_kd = jnp.exp(b), jnp.exp(b - bmid), jnp.exp(bmid - b), jnp.exp(blast - b)
            dec = jnp.exp(blast)
            qgt, qgn, kgn, kd = q * e_b, q * e_qn, k * e_kn, k * e_kd
            qgt_b, qgn_b, kgn_b, kd_b = qgt.astype(BF16), qgn.astype(BF16), kgn.astype(BF16), kd.astype(BF16)
            do_b = do_ref[sl, :].astype(BF16)
            st0s = [st_ref[h, c] for h in range(GLA_H)]
            dsts = [dstate[h] for h in range(GLA_H)]
            dqgn, dqgt, dkgn, dkd, dvs, ddec, news = [], [], [], [], [], [], []
            for h in range(GLA_H):
                hk, hv = slice(h * GLA_DK, (h + 1) * GLA_DK), slice(h * GLA_DV, (h + 1) * GLA_DV)
                dst_b = dsts[h].astype(BF16)
                a = jnp.where(causal, _dg(qgn_b[:, hk], kgn_b[:, hk], 1, 1), 0.0).astype(BF16)
                da = jnp.where(causal, _dg(do_b[:, hv], v[:, hv], 1, 1), 0.0).astype(BF16)
                dqgn.append(_dg(da, kgn_b[:, hk], 1, 0))
                dqgt.append(_dg(do_b[:, hv], st0s[h].astype(BF16), 1, 0))
                dkgn.append(_dg(da, qgn_b[:, hk], 0, 0))
                dvs.append(_dg(a, do_b[:, hv], 0, 0) + _dg(kd_b[:, hk], dst_b, 1, 1))
                dkd.append(_dg(v[:, hv], dst_b, 1, 0))
                ddec.append(jnp.sum(st0s[h] * dsts[h], axis=0, keepdims=True))
                news.append(dec[:, hk] * dsts[h] + _dg(do_b[:, hv], qgt_b[:, hk], 0, 0))
            for h in range(GLA_H):
                dstate[h] = news[h]
            cat = lambda parts: jnp.concatenate(parts, axis=1)
            dqgn, dqgt, dkgn, dkd, ddec = cat(dqgn), cat(dqgt), cat(dkgn), cat(dkd), cat(ddec)
            dq_ref[sl, :] = (scale * (dqgn * e_qn + dqgt * e_b)).astype(dq_ref.dtype)
            dk_ref[sl, :] = (dkgn * e_kn + dkd * e_kd).astype(dk_ref.dtype)
            dv_ref[sl, :] = cat(dvs).astype(dv_ref.dtype)
            db = dqgn * qgn + dqgt * qgt - dkgn * kgn - dkd * kd
            extra = jnp.sum(dkd * kd, axis=0, keepdims=True) + ddec * dec
            dla_ref[sl, :] = _tri_dot(tri_t, db) + extra
            return carry

        lax.fori_loop(0, nc, chunk, 0, unroll=2)

    rev = lambda t: nb - 1 - t
    hw, vw = GLA_H * GLA_DK, GLA_H * GLA_DV
    res = pl.pallas_call(
        body, name="gla_bwd", grid=(nb,),
        in_specs=[pl.BlockSpec((tb, hw), lambda t: (rev(t), P_GQ // hw)),
                  pl.BlockSpec((tb, hw), lambda t: (rev(t), P_GK // hw)),
                  pl.BlockSpec((tb, vw), lambda t: (rev(t), P_GV // vw)),
                  pl.BlockSpec((tb, hw), lambda t: (rev(t), 0)),
                  pl.BlockSpec((GLA_H, nc, GLA_DV, GLA_DK), lambda t: (0, rev(t), 0, 0)),
                  pl.BlockSpec((tb, vw), lambda t: (rev(t), 0)), pl.BlockSpec(memory_space=pl.ANY)] + [HBM] * len(c_ins),
        out_specs=[pl.BlockSpec((tb, hw), lambda t: (rev(t), 0)),
                   pl.BlockSpec((tb, hw), lambda t: (rev(t), 0)),
                   pl.BlockSpec((tb, vw), lambda t: (rev(t), P_GV // vw)),
                   pl.BlockSpec((tb, hw), lambda t: (rev(t), 0))] + [HBM] * len(c_outs),
        out_shape=[jax.ShapeDtypeStruct((s, hw), BF16),
                   jax.ShapeDtypeStruct((s, hw), BF16),
                   jax.ShapeDtypeStruct((s, P_W), BF16),
                   jax.ShapeDtypeStruct((s, hw), F32)] + c_outs,
        scratch_shapes=[pltpu.VMEM((GLA_H, GLA_DV, GLA_DK), F32)] + c_scratch,
        input_output_aliases={6: 2, **c_alias},
        compiler_params=_params(("arbitrary",)),
    )(p, p, p, la, states, do, dp, *c_ins)
    return res[0], res[1], res[2], res[3], _split_units(comm, res[4:])


def _head_masks():
    lane = lax.broadcasted_iota(jnp.int32, (1, 4 * ATT_HD), 1)
    return [(lane >= h * ATT_HD) & (lane < (h + 1) * ATT_HD) for h in range(4)]


def _attn_fwd(qv, kv, pv, g, r, s):
    ln = s // r
    nblk = ln // ATT_BLK
    qcol = lambda pr: pr
    vcol = qcol
    prev = lambda n: jnp.maximum(n - 1, 0)

    def body(q_ref, kp_ref, kc_ref, vp_ref, vc_ref, o_ref, lse_ref):
        has_prev = pl.program_id(1) > 0
        ri = lax.broadcasted_iota(jnp.int32, (ATT_BLK, ATT_BLK), 0)
        ci = lax.broadcasted_iota(jnp.int32, (ATT_BLK, ATT_BLK), 1)
        m_cur = ci <= ri
        m_prev = (ci >= ri) & has_prev
        q, kp, kc, vp, vc = q_ref[...], kp_ref[...], kc_ref[...], vp_ref[...], vc_ref[...]
        o = jnp.zeros((ATT_BLK, 256), F32)
        lse = jnp.zeros((ATT_BLK, 256), F32)
        for hm in _head_masks():
            qm = jnp.where(hm, q, jnp.zeros_like(q))
            sc = jnp.where(m_cur, _dg(qm, kc, 1, 1) * 0.125, NEG)
            sp = jnp.where(m_prev, _dg(qm, kp, 1, 1) * 0.125, NEG)
            mx = jnp.maximum(jnp.max(sc, axis=1, keepdims=True), jnp.max(sp, axis=1, keepdims=True))
            pc, pp = jnp.exp(sc - mx), jnp.exp(sp - mx)
            den = jnp.sum(pc, axis=1, keepdims=True) + jnp.sum(pp, axis=1, keepdims=True)
            oh = (_dg(pc.astype(BF16), vc, 1, 0) + _dg(pp.astype(BF16), vp, 1, 0)) / den
            o = jnp.where(hm, oh, o)
            lse = jnp.where(hm, mx + jnp.log(den), lse)
        o_ref[...] = o.astype(o_ref.dtype)
        lse_ref[...] = lse

    blk = (ATT_BLK, 256)
    o, lse = pl.pallas_call(
        body, name=f"attn_fwd_{g}", grid=(r, nblk),
        in_specs=[pl.BlockSpec(blk, lambda pr, n: (n, qcol(pr))),
                  pl.BlockSpec(blk, lambda pr, n: (prev(n), qcol(pr))),
                  pl.BlockSpec(blk, lambda pr, n: (n, qcol(pr))),
                  pl.BlockSpec(blk, lambda pr, n: (prev(n), vcol(pr))),
                  pl.BlockSpec(blk, lambda pr, n: (n, vcol(pr)))],
        out_specs=[pl.BlockSpec(blk, lambda pr, n: (n, pr)), pl.BlockSpec(blk, lambda pr, n: (n, pr))],
        out_shape=[jax.ShapeDtypeStruct((ln, r * 256), BF16), jax.ShapeDtypeStruct((ln, r * 256), F32)],
        compiler_params=_params(("parallel", "parallel")),
    )(qv, kv, kv, pv, pv)
    return o, lse


def _attn_bwd(qv, kv, pv, dov, ov, lv, g, r, s):
    ln = s // r
    nblk = ln // ATT_BLK
    qcol = lambda pr: pr
    vcol = qcol
    prev = lambda n: jnp.maximum(n - 1, 0)
    nxt = lambda n: jnp.minimum(n + 1, nblk - 1)

    def body(qc_ref, qn_ref, kp_ref, kc_ref, vp_ref, vc_ref, doc_ref, don_ref, oc_ref, on_ref, lc_ref, ln_ref,
             dq_ref, dk_ref, dv_ref):
        n = pl.program_id(1)
        has_prev, has_next = n > 0, n < nblk - 1
        ri = lax.broadcasted_iota(jnp.int32, (ATT_BLK, ATT_BLK), 0)
        ci = lax.broadcasted_iota(jnp.int32, (ATT_BLK, ATT_BLK), 1)
        m_cur = ci <= ri
        m_prev = (ci >= ri) & has_prev
        m_next = (ci >= ri) & has_next
        qc, qn, kp, kc, vp, vc = qc_ref[...], qn_ref[...], kp_ref[...], kc_ref[...], vp_ref[...], vc_ref[...]
        doc, don = doc_ref[...], don_ref[...]
        pc_full = doc.astype(F32) * oc_ref[...].astype(F32)
        pn_full = don.astype(F32) * on_ref[...].astype(F32)
        lc, lnx = lc_ref[...], ln_ref[...]
        dq = jnp.zeros((ATT_BLK, 256), F32)
        dk = jnp.zeros((ATT_BLK, 256), F32)
        dv = jnp.zeros((ATT_BLK, 256), F32)
        zb = jnp.zeros_like(qc)
        for hm in _head_masks():
            qcm, qnm = jnp.where(hm, qc, zb), jnp.where(hm, qn, zb)
            docm, donm = jnp.where(hm, doc, zb), jnp.where(hm, don, zb)
            lse_c = jnp.max(jnp.where(hm, lc, NEG), axis=1, keepdims=True)
            lse_n = jnp.max(jnp.where(hm, lnx, NEG), axis=1, keepdims=True)
            del_c = jnp.sum(jnp.where(hm, pc_full, 0.0), axis=1, keepdims=True)
            del_n = jnp.sum(jnp.where(hm, pn_full, 0.0), axis=1, keepdims=True)
            pr_ = jnp.where(m_cur, jnp.exp(_dg(qcm, kc, 1, 1) * 0.125 - lse_c), 0.0)
            ds = (pr_ * (_dg(docm, vc, 1, 1) - del_c) * 0.125).astype(BF16)
            dqh = _dg(ds, kc, 1, 0)
            dkh = _dg(ds, qc, 0, 0)
            dvh = _dg(pr_.astype(BF16), doc, 0, 0)
            pr_ = jnp.where(m_prev, jnp.exp(_dg(qcm, kp, 1, 1) * 0.125 - lse_c), 0.0)
            ds = (pr_ * (_dg(docm, vp, 1, 1) - del_c) * 0.125).astype(BF16)
            dqh = dqh + _dg(ds, kp, 1, 0)
            pr_ = jnp.where(m_next, jnp.exp(_dg(qnm, kc, 1, 1) * 0.125 - lse_n), 0.0)
            ds = (pr_ * (_dg(donm, vc, 1, 1) - del_n) * 0.125).astype(BF16)
            dkh = dkh + _dg(ds, qn, 0, 0)
            dvh = dvh + _dg(pr_.astype(BF16), don, 0, 0)
            dq = jnp.where(hm, dqh, dq)
            dk = jnp.where(hm, dkh, dk)
            dv = jnp.where(hm, dvh, dv)
        dq_ref[...] = dq.astype(dq_ref.dtype)
        dk_ref[...] = dk.astype(dk_ref.dtype)
        dv_ref[...] = dv.astype(dv_ref.dtype)

    blk = (ATT_BLK, 256)
    cur = lambda col: pl.BlockSpec(blk, lambda pr, n: (n, col(pr)))
    prv = lambda col: pl.BlockSpec(blk, lambda pr, n: (prev(n), col(pr)))
    nx = lambda col: pl.BlockSpec(blk, lambda pr, n: (nxt(n), col(pr)))
    own = lambda pr: pr
    outs = pl.pallas_call(
        body, name=f"attn_bwd_{g}", grid=(r, nblk),
        in_specs=[cur(qcol), nx(qcol), prv(qcol), cur(qcol), prv(vcol), cur(vcol),
                  cur(own), nx(own), cur(own), nx(own), cur(own), nx(own)],
        out_specs=[cur(own), cur(own), cur(own)],
        out_shape=[jax.ShapeDtypeStruct((ln, r * 256), BF16)] * 3,
        compiler_params=_params(("parallel", "parallel")),
    )(qv, qv, kv, kv, pv, pv, dov, dov, ov, ov, lv, lv)
    return outs


def _gelu_parts(gv):
    cdf = 0.5 * (1.0 + lax.erf(gv * (2.0 ** -0.5)))
    pdf = jnp.exp(-0.5 * gv * gv) * (1.0 / math.sqrt(2.0 * math.pi))
    return cdf, pdf


def _pick_row(t, k):
    row = lax.broadcasted_iota(jnp.int32, t.shape, 0)
    return jnp.sum(jnp.where(row == k, t, 0.0), axis=0, keepdims=True)


def _shift_rows(u, halo, n):
    row = lax.broadcasted_iota(jnp.int32, u.shape, 0)
    out = pltpu.roll(u, n, 0)
    for k in range(n):
        out = jnp.where(row == k, _pick_row(halo, 16 - n + k), out)
    return out


def _shift_rows_up(u, halo, n):
    rb = u.shape[0]
    row = lax.broadcasted_iota(jnp.int32, u.shape, 0)
    out = pltpu.roll(u, rb - n, 0)
    for k in range(n):
        out = jnp.where(row == rb - n + k, _pick_row(halo, k), out)
    return out


CONV_ROWS, CONV_LANES = 64, 256


def _conv_transpose(duc, cw, s, rb):
    w2 = 2 * W_UP_SH
    nr = s // rb

    def body(d_ref, h_ref, cw_ref, o_ref, scr):
        last = pl.program_id(1) == nr - 1
        scr[0:rb, :] = d_ref[...].astype(F32)
        scr[rb:rb + 16, :] = jnp.where(last, 0.0, h_ref[...].astype(F32))
        for l0 in range(0, w2, CONV_LANES):
            ls = slice(l0, l0 + CONV_LANES)
            c0, c1, c2 = cw_ref[0:1, ls], cw_ref[1:2, ls], cw_ref[2:3, ls]
            for r0 in range(0, rb, CONV_ROWS):
                o_ref[r0:r0 + CONV_ROWS, ls] = (c2 * scr[r0:r0 + CONV_ROWS, ls] + c1 * scr[r0 + 1:r0 + 1 + CONV_ROWS, ls]
                                                + c0 * scr[r0 + 2:r0 + 2 + CONV_ROWS, ls]).astype(o_ref.dtype)

    per = rb // 16
    lastb = s // 16 - 1
    return pl.pallas_call(
        body, name="conv_transpose", grid=(2, nr),
        in_specs=[pl.BlockSpec((rb, w2), lambda c, i: (i, c)),
                  pl.BlockSpec((16, w2), lambda c, i: (jnp.minimum((i + 1) * per, lastb), c)),
                  pl.BlockSpec((3, w2), lambda c, i: (0, c))],
        out_specs=pl.BlockSpec((rb, w2), lambda c, i: (i, c)),
        out_shape=jax.ShapeDtypeStruct(duc.shape, BF16),
        scratch_shapes=[pltpu.VMEM((rb + 16, w2), F32)],
        compiler_params=_params(("parallel", "parallel")),
    )(duc, duc, cw)


def _conv_geglu(u, cw, cb, s, rb):
    w2, hw = 2 * W_UP_SH, W_UP_SH
    nr = s // rb

    def body(u_ref, h_ref, cw_ref, cb_ref, o_ref, scr):
        first = pl.program_id(1) == 0
        scr[0:16, :] = jnp.where(first, 0.0, h_ref[...].astype(F32))
        scr[16:16 + rb, :] = u_ref[...].astype(F32)

        def conv(ls, r0):
            return (cb_ref[0:1, ls] + cw_ref[0:1, ls] * scr[14 + r0:14 + r0 + CONV_ROWS, ls]
                    + cw_ref[1:2, ls] * scr[15 + r0:15 + r0 + CONV_ROWS, ls] + cw_ref[2:3, ls] * scr[16 + r0:16 + r0 + CONV_ROWS, ls])

        for l0 in range(0, hw, 128):
            for r0 in range(0, rb, CONV_ROWS):
                val, gt = conv(slice(l0, l0 + 128), r0), conv(slice(hw + l0, hw + l0 + 128), r0)
                cdf, _ = _gelu_parts(gt)
                o_ref[r0:r0 + CONV_ROWS, l0:l0 + 128] = (gt * cdf * val).astype(o_ref.dtype)

    per = rb // 16
    return pl.pallas_call(
        body, name="conv_geglu", grid=(2, nr),
        in_specs=[pl.BlockSpec((rb, w2), lambda c, i: (i, c)),
                  pl.BlockSpec((16, w2), lambda c, i: (jnp.maximum(i * per - 1, 0), c)),
                  pl.BlockSpec((3, w2), lambda c, i: (0, c)), pl.BlockSpec((1, w2), lambda c, i: (0, c))],
        out_specs=pl.BlockSpec((rb, hw), lambda c, i: (i, c)),
        out_shape=jax.ShapeDtypeStruct((s, D_FF), BF16),
        scratch_shapes=[pltpu.VMEM((rb + 16, w2), F32)],
        compiler_params=_params(("parallel", "parallel")),
    )(u, u, cw, cb)


def _conv_geglu_bwd(u, dh, cw, cb, s, rb):
    w2, hw = 2 * W_UP_SH, W_UP_SH
    nr = s // rb

    def body(u_ref, h_ref, dh_ref, cw_ref, cb_ref, o_ref, dcb_ref, dcw_ref, scr):
        i = pl.program_id(1)
        scr[0:16, :] = jnp.where(i == 0, 0.0, h_ref[...].astype(F32))
        scr[16:16 + rb, :] = u_ref[...].astype(F32)

        @pl.when(i == 0)
        def _():
            dcb_ref[...] = jnp.zeros_like(dcb_ref)
            dcw_ref[...] = jnp.zeros_like(dcw_ref)

        for l0 in range(0, hw, 128):
            lanes = (slice(l0, l0 + 128), slice(hw + l0, hw + l0 + 128))
            part = lambda t: jnp.sum(t.reshape(CONV_ROWS // 8, 8, 128), axis=0)
            sums = [[jnp.zeros((8, 128), F32) for _ in range(4)] for _ in range(2)]
            for r0 in range(0, rb, CONV_ROWS):
                taps = [[scr[14 + k + r0:14 + k + r0 + CONV_ROWS, ls] for k in range(3)] for ls in lanes]
                val, gt = [cb_ref[0:1, ls] + cw_ref[0:1, ls] * t[0] + cw_ref[1:2, ls] * t[1] + cw_ref[2:3, ls] * t[2]
                           for ls, t in zip(lanes, taps)]
                cdf, pdf = _gelu_parts(gt)
                dhf = dh_ref[r0:r0 + CONV_ROWS, l0:l0 + 128].astype(F32)
                ducs = (dhf * (gt * cdf), dhf * val * (cdf + gt * pdf))
                for side in range(2):
                    o_ref[r0:r0 + CONV_ROWS, lanes[side]] = ducs[side].astype(o_ref.dtype)
                    sums[side][3] = sums[side][3] + part(ducs[side])
                    for k in range(3):
                        sums[side][k] = sums[side][k] + part(ducs[side] * taps[side][k])
            for side in range(2):
                dcb_ref[0:1, lanes[side]] += _csum(sums[side][3])
                for k in range(3):
                    dcw_ref[k:k + 1, lanes[side]] += _csum(sums[side][k])

    per = rb // 16
    return pl.pallas_call(
        body, name="conv_geglu_bwd", grid=(2, nr),
        in_specs=[pl.BlockSpec((rb, w2), lambda c, i: (i, c)),
                  pl.BlockSpec((16, w2), lambda c, i: (jnp.maximum(i * per - 1, 0), c)),
                  pl.BlockSpec((rb, hw), lambda c, i: (i, c)),
                  pl.BlockSpec((3, w2), lambda c, i: (0, c)), pl.BlockSpec((1, w2), lambda c, i: (0, c))],
        out_specs=[pl.BlockSpec((rb, w2), lambda c, i: (i, c)), pl.BlockSpec((1, w2), lambda c, i: (0, c)),
                   pl.BlockSpec((3, w2), lambda c, i: (0, c))],
        out_shape=[jax.ShapeDtypeStruct((s, 2 * w2), BF16), jax.ShapeDtypeStruct((1, 2 * w2), F32),
                   jax.ShapeDtypeStruct((3, 2 * w2), F32)],
        scratch_shapes=[pltpu.VMEM((rb + 16, w2), F32)],
        compiler_params=_params(("parallel", "arbitrary")),
    )(u, u, dh, cw, cb)


def _conv(u, halo, cw, cb):
    return cb + _pick_row(cw, 0) * _shift_rows(u, halo, 2) + _pick_row(cw, 1) * _shift_rows(u, halo, 1) + _pick_row(cw, 2) * u


def _local_step(x, mod, pos_col, target, sm, w_sh, g0, chip, core):
    s = x.shape[0]
    shift1, scale1, gate1, shift2, scale2, gate2 = [mod[i:i + 1, :] for i in range(6)]
    rb = 512
    chip1 = chip.reshape(1)

    def f_norm1(c, i, xv, nw, sc, sh):
        return ((xv * _rms(xv) * nw) * (1.0 + sc) + sh,)

    (h,) = _rowcall(f_norm1, [_rows(x, rb), _full(sm["n1w"]), _full(scale1), _full(shift1)],
                    [_orow(s, D, BF16, rb)], n_rows=s, rb=rb, name="norm1")
    own = lambda got, i: lax.dynamic_update_slice(got, w_sh[i], (chip, 0, 0))
    invf = jnp.tile(ROPE_THETA ** (-jnp.arange(ATT_HD // 2, dtype=F32) / (ATT_HD // 2)), 4).reshape(1, 128)
    cos_t, sin_t = _rope_tables(pos_col, invf, s)
    _, got0 = _unit_wait(*g0[:4], after=[h, cos_t, sin_t], name="gather_w_in_wait")
    [got0] = _comm_call("gather_w_in_d2d", [_u_gather_d2d(got0, (0,))])
    u_g1 = _u_gather_ici(w_sh, (1, 2, 3, 4, 5))
    g1 = _unit_start(u_g1, "gather_weights_start", after=got0)
    w = dict(win=_win_assemble(got0[0], w_sh[0], after=g1[3:]))
    p = _mm(h, w["win"], "in_proj", tm=2048, tn=1536)

    def f_gla_pre(c, i, glr, w2, gb):
        z = _dg(glr, w2.astype(BF16), 1, 0) + gb
        return ((jnp.minimum(z, 0.0) - jnp.log(1.0 + jnp.exp(-jnp.abs(z)))) * (1.0 / GLA_TAU),)

    (la,) = _rowcall(f_gla_pre, [_rows(p, rb, 128, P_LR // 128), _full(sm["w2"]), _full(sm["gb"])],
                     [_orow(s, 512, F32, rb)], n_rows=s, rb=rb, name="gla_pre")
    o_gla, states, _ = _gla_fwd(p, la, s)
    _, got = _unit_wait(u_g1, *g1[:3], after=[o_gla], name="gather_weights_wait")
    [got123] = _comm_call("gather_weights_d2d", [_u_gather_d2d(got[:3], (1, 2, 3))])
    got45 = got[3:]
    w.update(wgb=own(got123[0], 1).reshape(1024, D), wab=_cols_join(own(got123[1], 2)), wout=own(got123[2], 3).reshape(D, D))

    def f_gla_post(c, i, ov, gnw, gr):
        on = jnp.concatenate([ov[:, k * 256:(k + 1) * 256] * _rms(ov[:, k * 256:(k + 1) * 256]) * gnw
                              for k in range(GLA_H)], axis=1)
        g = gr.astype(F32)
        return (on * (g * _sigmoid(g)),)

    (og,) = _rowcall(f_gla_post, [_rows(o_gla, rb), _full(sm["gnw"]), _rows(p, rb, 1024, P_GR // 1024)],
                     [_orow(s, 1024, BF16, rb)], n_rows=s, rb=rb, name="gla_post")
    y_gla = _mm(og, w["wgb"], "gla_branch")

    q_d, k_d, v_d = _rope_fwd(p, cos_t, sin_t, s)
    att = [_attn_fwd(q_d[g], k_d[g], v_d[g], g, r, s) for g, r in enumerate(_RS)]
    o_att, lse, o_d1, o_d2, lse_d1, lse_d2 = _attn_combine(att, s)
    y_att = _mm(o_att, w["wab"], "attn_branch")

    def f_merge(c, i, ma, mb, yg, ya):
        return (_sigmoid(ma.astype(F32)) * yg.astype(F32) + _sigmoid(mb.astype(F32)) * ya.astype(F32),)

    (mixed,) = _rowcall(f_merge, [_rows(p, rb, D, P_MA // D), _rows(p, rb, D, P_MB // D), _rows(y_gla, rb), _rows(y_att, rb)],
                        [_orow(s, D, BF16, rb)], n_rows=s, rb=rb, name="merge")
    z1, [got45] = _mm(mixed, w["wout"], "out_proj", comm=[_u_gather_d2d(got45, (4, 5))])
    w.update(wup=own(got45[0], 4), wdown=own(got45[1], 5).reshape(D_FF, D))

    def f_norm2(c, i, xv, z, g1, nw, sc, sh):
        x1 = xv + g1 * z.astype(F32)
        return (x1, (x1 * _rms(x1) * nw) * (1.0 + sc) + sh)

    x1, h2 = _rowcall(f_norm2, [_rows(x, rb), _rows(z1, rb), _full(gate1), _full(sm["n2w"]), _full(scale2), _full(shift2)],
                      [_orow(s, D, F32, rb), _orow(s, D, BF16, rb)], n_rows=s, rb=rb, name="norm2")
    u = _mm(h2, w["wup"], "up_proj", tm=2048, b_shards=True)

    cwid = 2 * W_UP_SH

    ccol = lambda c: c
    rw = 256
    hidden = _conv_geglu(u, sm["cw"], sm["cb"], s, rw)
    z2 = _mm(hidden, w["wdown"], "down_proj", tk=D_FF)

    def f_final(c, i, x1v, z, g2, fw, tgt):
        x2 = x1v + g2 * z.astype(F32)
        r = _rms(x2)
        xh = x2 * r
        e = xh * fw - tgt
        loss = 0.5 * jnp.sum(jnp.mean(e * e, axis=-1, keepdims=True), axis=0, keepdims=True)
        dy = e * (1.0 / D)
        dxh = dy * fw
        dx2 = r * (dxh - xh * jnp.mean(dxh * xh, axis=-1, keepdims=True))
        return (loss, dx2, dx2 * g2, _csum(dy * xh), _csum(dx2 * z.astype(F32)))

    loss, dx2, dz2, d_fnw, d_gate2 = _rowcall(
        f_final, [_rows(x1, rb), _rows(z2, rb), _full(gate2), _full(sm["fnw"]), _rows(target, rb)],
        [_oacc(1, 1), _orow(s, D, F32, rb), _orow(s, D, BF16, rb), _oacc(1, D), _oacc(1, D)],
        n_rows=s, rb=rb, name="final_loss")
    d_hidden = _mm(dz2, w["wdown"], "down_proj_dx", tb=True, tn=1408)
    g_wdown = _mm(hidden, dz2, "down_proj_dw", ta=True, out_dtype=F32, tm=1408, tn=1024, tk=2048)

    duc, d_cb, d_cw = _conv_geglu_bwd(u, d_hidden, sm["cw"], sm["cb"], s, rw)

    du = _conv_transpose(duc, sm["cw"], s, rw)
    g_wup = _mm(h2, du, "up_proj_dw", ta=True, out_dtype=F32, tm=1024, tk=2048, o_shards=True)
    gs45 = [g_wup, g_wdown.reshape(4, W_DOWN_SH, 1024)]
    d_h2, [land45] = _mm(du, w["wup"], "up_proj_dx", tb=True, tm=2048, b_shards=True, comm=[_u_pair_send(gs45, (4, 5))])
    ts45 = [_pair_add(g, ld, core, "grad_pair_add_" + BIG[i]) for g, ld, i in zip(gs45, land45, (4, 5))]
    u_ex4 = _u_chip_exchange(ts45[:1])
    ex4 = _unit_start(u_ex4, "grad_exchange_w_up_start")

    def f_norm2_bwd(c, i, x1v, dh, dxr, z, nw, sc, g1):
        dxn, dsh, dsc, dnw = _norm_bwd(x1v, dh.astype(F32), nw, sc)
        dx1 = dxr + dxn
        return (dx1, dx1 * g1, dsh, dsc, dnw, _csum(dx1 * z.astype(F32)))

    dx1, dz1, d_shift2, d_scale2, d_n2w, d_gate1 = _rowcall(
        f_norm2_bwd, [_rows(x1, rb), _rows(d_h2, rb), _rows(dx2, rb), _rows(z1, rb), _full(sm["n2w"]), _full(scale2), _full(gate1)],
        [_orow(s, D, F32, rb), _orow(s, D, BF16, rb), _oacc(1, D), _oacc(1, D), _oacc(1, D), _oacc(1, D)],
        n_rows=s, rb=rb, name="norm2_bwd", after=ex4[3:])
    d_mixed = _mm(dz1, w["wout"], "out_proj_dx", tb=True)
    g_wout = _mm(mixed, dz1, "out_proj_dw", ta=True, out_dtype=F32, tk=2048)

    def f_merge_bwd(c, i, dm, ma, mb, yg, ya):
        dmf, ygf, yaf = dm.astype(F32), yg.astype(F32), ya.astype(F32)
        sa, sb = _sigmoid(ma.astype(F32)), _sigmoid(mb.astype(F32))
        return (dmf * sa, dmf * sb, jnp.concatenate([dmf * ygf * sa * (1.0 - sa), dmf * yaf * sb * (1.0 - sb)], axis=1))

    dy_gla, dy_att, dp = _rowcall(
        f_merge_bwd, [_rows(d_mixed, rb), _rows(p, rb, D, P_MA // D), _rows(p, rb, D, P_MB // D), _rows(y_gla, rb), _rows(y_att, rb)],
        [_orow(s, D, BF16, rb)] * 2 + [_orow(s, P_W, BF16, rb, 2 * D, lambda c: P_MA // (2 * D))], n_rows=s, rb=rb, name="merge_bwd")
    d_og = _mm(dy_gla, w["wgb"], "gla_branch_dx", tb=True)
    g_wgb = _mm(og, dy_gla, "gla_branch_dw", ta=True, out_dtype=F32, tk=2048)
    d_oatt = _mm(dy_att, w["wab"], "attn_branch_dx", tb=True)
    g_wab = _mm(o_att, dy_att, "attn_branch_dw", ta=True, out_dtype=F32, tk=2048)

    def f_gla_post_bwd(c, i, ov, gnw, gr, dog):
        g = gr.astype(F32)
        sg = _sigmoid(g)
        silu = g * sg
        dof = dog.astype(F32)
        don = dof * silu
        on_parts, do_parts, dgn = [], [], jnp.zeros((1, 256), F32)
        for k in range(GLA_H):
            oh = ov[:, k * 256:(k + 1) * 256]
            dh = don[:, k * 256:(k + 1) * 256]
            r = _rms(oh)
            xh = oh * r
            dgn = dgn + _csum(dh * xh)
            dxh = dh * gnw
            do_parts.append(r * (dxh - xh * jnp.mean(dxh * xh, axis=-1, keepdims=True)))
            on_parts.append(xh * gnw)
        on = jnp.concatenate(on_parts, axis=1)
        dgr = dof * on * (sg * (1.0 + g * (1.0 - sg)))
        return (jnp.concatenate(do_parts, axis=1), dgr, dgn)

    do_gla, dp, d_gnw = _rowcall(
        f_gla_post_bwd, [_rows(o_gla, rb), _full(sm["gnw"]), _rows(p, rb, 1024, P_GR // 1024), _rows(d_og, rb)],
        [_orow(s, 1024, F32, rb), _orow(s, P_W, BF16, rb, 1024, lambda c: P_GR // 1024), _oacc(1, 256)],
        n_rows=s, rb=rb, name="gla_post_bwd", into=(dp, 1))
    gs123 = [g_wgb.reshape(4, 256, 1024), _cols_split(g_wab), g_wout.reshape(4, 256, 1024)]
    d_gq, d_gk, dp, d_la, [land123] = _gla_bwd(p, la, states, do_gla, s, dp, comm=[_u_pair_send(gs123, (1, 2, 3))])
    ts123 = [_pair_add(g, ld, core, "grad_pair_add_" + BIG[i]) for g, ld, i in zip(gs123, land123, (1, 2, 3))]

    def f_gla_pre_bwd(c, i, lav, dlav, glr, w2):
        dz = dlav * (1.0 / GLA_TAU) * (1.0 - jnp.exp(GLA_TAU * lav))
        dzb = dz.astype(BF16)
        return (_dg(dzb, w2.astype(BF16), 1, 1), _csum(dz), _dg(glr, dzb, 0, 0))

    d_glr, d_gb, d_w2 = _rowcall(
        f_gla_pre_bwd, [_rows(la, rb), _rows(d_la, rb), _rows(p, rb, 128, P_LR // 128), _full(sm["w2"])],
        [_orow(s, 128, BF16, rb), _oacc(1, 512), _oacc(128, 512)], n_rows=s, rb=rb, name="gla_pre_bwd")

    do_d = [d_oatt] + list(_dilate(d_oatt, s))
    datt = [_attn_bwd(q_d[g], k_d[g], v_d[g], do_d[g], (o_att, o_d1, o_d2)[g], (lse, lse_d1, lse_d2)[g], g, r, s)
            for g, r in enumerate(_RS)]
    dp = _rope_bwd(datt, d_glr, dp, cos_t, sin_t, s)
    dp = lax.dynamic_update_slice(dp, jnp.concatenate([d_gq, d_gk], axis=1), (0, P_GQ))
    [t4], [r4] = _unit_wait(u_ex4, *ex4[:3], after=[dp], name="grad_exchange_w_up_wait")
    half4 = [_chip_sum(t4, r4, chip1, "grad_chip_sum_w_up")]
    g_win, [r1235, oth4] = _mm(h, dp, "in_proj_dw", ta=True, out_dtype=F32, tm=1024, tn=1536, tk=2048,
                               comm=[_u_chip_exchange(ts123 + ts45[1:]), _u_pair_join(half4)])
    half1235 = [_chip_sum(t, r, chip1, "grad_chip_sum_" + BIG[i]) for t, r, i in zip(ts123 + ts45[1:], r1235, (1, 2, 3, 5))]
    gs0 = [_win_split(g_win)]
    d_h, [land0, oth1235] = _mm(dp, w["win"], "in_proj_dx", tb=True, tk=3840,
                                comm=[_u_pair_send(gs0, (0,)), _u_pair_join(half1235)])
    half123, half45 = half1235[:3], half4 + half1235[3:]
    oth123, oth45 = oth1235[:3], oth4 + oth1235[3:]
    ts0 = _pair_add(gs0[0], land0[0], core, "grad_pair_add_w_in")

    def f_norm1_bwd(c, i, xv, dh, dxr, nw, sc):
        dxn, dsh, dsc, dnw = _norm_bwd(xv, dh.astype(F32), nw, sc)
        return (dxr + dxn, dsh, dsc, dnw)

    grad_x, d_shift1, d_scale1, d_n1w = _rowcall(
        f_norm1_bwd, [_rows(x, rb), _rows(d_h, rb), _rows(dx1, rb), _full(sm["n1w"]), _full(scale1)],
        [_orow(s, D, F32, rb), _oacc(1, D), _oacc(1, D), _oacc(1, D)], n_rows=s, rb=rb, name="norm1_bwd")

    dmod = jnp.concatenate([d_shift1, d_scale1, d_gate1, d_shift2, d_scale2, d_gate2], axis=1)
    small = dict(dmod=dmod, n1w=d_n1w, gb=d_gb, gnw=d_gnw, n2w=d_n2w, cb=d_cb, fnw=d_fnw, w2=d_w2, cw=d_cw)
    return loss, grad_x, half123 + half45, oth123 + oth45, small, ts0


def _win_pieces():
    runs = [(P_GV, 1024, 2048), (P_MA, 5392, 2048), (P_GQ, 0, 1024), (P_AQ, 3088, 2304), (P_LR, 3072, GLA_LR)]
    out = []
    for kc, rc, ln in runs:
        while ln > 0:
            step = min(ln, W_IN_SH - rc % W_IN_SH)
            out.append((kc, rc, step))
            kc, rc, ln = kc + step, rc + step, ln - step
    return out


def _win_assemble(shards, own, after=()):
    rb = 256

    def body(s_ref, own_ref, *rest):
        o_ref = rest[-1]
        x, y, _ = _me()
        o_ref[:, W_IN:] = jnp.zeros((rb, P_W - W_IN), o_ref.dtype)
        for kc, rc, ln in _win_pieces():
            sh, lo = rc // W_IN_SH, rc % W_IN_SH
            o_ref[:, kc:kc + ln] = jnp.where(2 * x + y == sh, own_ref[0, :, lo:lo + ln], s_ref[sh, :, lo:lo + ln])

    return pl.pallas_call(
        body, name="w_in_assemble", grid=(D // rb,),
        in_specs=[pl.BlockSpec((4, rb, W_IN_SH), lambda i: (0, i, 0)), pl.BlockSpec((1, rb, W_IN_SH), lambda i: (0, i, 0))]
        + [pl.BlockSpec(memory_space=pl.ANY)] * len(after),
        out_specs=pl.BlockSpec((rb, P_W), lambda i: (i, 0)),
        out_shape=jax.ShapeDtypeStruct((D, P_W), shards.dtype), compiler_params=_params(("parallel",)),
    )(shards, own, *after)


def _win_split(g):
    rb = 256

    def body(g_ref, o_ref):
        for kc, rc, ln in _win_pieces():
            o_ref[rc // W_IN_SH, :, rc % W_IN_SH:rc % W_IN_SH + ln] = g_ref[:, kc:kc + ln]

    return pl.pallas_call(
        body, name="w_in_grad_split", grid=(D // rb,),
        in_specs=[pl.BlockSpec((rb, P_W), lambda i: (i, 0))], out_specs=pl.BlockSpec((4, rb, W_IN_SH), lambda i: (0, i, 0)),
        out_shape=jax.ShapeDtypeStruct((4, D, W_IN_SH), g.dtype), compiler_params=_params(("parallel",)),
    )(g)


def _ff_to_kernel(a):
    h = W_UP_SH
    return jnp.concatenate([a[:, 0:h], a[:, D_FF:D_FF + h], a[:, h:D_FF], a[:, D_FF + h:]], axis=1)


def _ff_from_kernel(a):
    h = W_UP_SH
    return jnp.concatenate([a[:, 0:h], a[:, 2 * h:3 * h], a[:, h:2 * h], a[:, 3 * h:]], axis=1)


BIG = ("w_in", "w_gla_branch", "w_attn_branch", "w_out", "w_up", "w_down")
SH_SHAPES = ((1024, W_IN_SH), (256, 1024), (256, 256), (256, 1024), (1024, W_UP_SH), (W_DOWN_SH, 1024))
N_BIG = len(BIG)


def _cols_join(t):
    return jnp.concatenate([t[k] for k in range(4)], axis=1)


def _cols_split(t):
    cols = t.shape[1] // 4
    return jnp.stack([t[:, k * cols:(k + 1) * cols] for k in range(4)])


def _me():
    return lax.axis_index("x"), lax.axis_index("y"), lax.axis_index("c")


HBM = pl.BlockSpec(memory_space=pltpu.HBM)
VMEM_SPEC = pl.BlockSpec(memory_space=pltpu.VMEM)


def _allgather8(xs, name):
    rows = xs.shape[0]

    def body(x_ref, out_ref, send_sems, recv_sems, local_sem):
        x, y, c = _me()
        me = 4 * x + 2 * y + c
        mine = pltpu.make_async_copy(x_ref, out_ref.at[me], local_sem)
        mine.start()
        flips = [(k >> 2 & 1, k >> 1 & 1, k & 1) for k in range(1, 8)]

        def peer(f):
            return (jnp.where(f[0] == 1, 1 - x, x), jnp.where(f[1] == 1, 1 - y, y), jnp.where(f[2] == 1, 1 - c, c))

        sends = []
        for k, f in enumerate(flips):
            cp = pltpu.make_async_remote_copy(src_ref=x_ref, dst_ref=out_ref.at[me], send_sem=send_sems.at[k],
                                              recv_sem=recv_sems.at[k], device_id=peer(f), device_id_type=MESH)
            cp.start()
            sends.append(cp)
        for k, f in enumerate(flips):
            px, py, pc = peer(f)
            pltpu.make_async_remote_copy(src_ref=x_ref, dst_ref=out_ref.at[4 * px + 2 * py + pc], send_sem=send_sems.at[k],
                                         recv_sem=recv_sems.at[k], device_id=peer(f), device_id_type=MESH).wait_recv()
        for cp in sends:
            cp.wait_send()
        mine.wait()

    return pl.pallas_call(
        body, name=name, out_shape=jax.ShapeDtypeStruct((8, rows, 128), F32),
        in_specs=[VMEM_SPEC], out_specs=VMEM_SPEC,
        scratch_shapes=[pltpu.SemaphoreType.DMA((7,)), pltpu.SemaphoreType.DMA((7,)), pltpu.SemaphoreType.DMA],
        compiler_params=pltpu.CompilerParams(vmem_limit_bytes=VMEM_LIMIT),
    )(xs)


def _half_rows(i, cc, unit):
    rows = SH_SHAPES[i][0] // 2
    return pl.ds(pl.multiple_of(cc * rows, unit), rows)


def _rc(src, dst, sems, to):
    return pltpu.make_async_remote_copy(src_ref=src, dst_ref=dst, send_sem=sems[0], recv_sem=sems[1], device_id=to, device_id_type=MESH)


def _other_chips(x, y):
    return [(1 - x, y), (x, 1 - y), (1 - x, 1 - y)]


def _u_gather_ici(w_sh, idxs):
    def copies(ins, outs, sem):
        x, y, c = _me()
        res = []
        for j, (px, py) in enumerate(_other_chips(x, y)):
            for n, i in enumerate(idxs):
                src = ins[n].at[0, _half_rows(i, c, 16)]
                res.append((_rc(src, outs[n].at[2 * x + y, _half_rows(i, c, 16)], sem(j * len(idxs) + n), (px, py, c)),
                            _rc(src, outs[n].at[2 * px + py, _half_rows(i, c, 16)], sem(j * len(idxs) + n), (px, py, c))))
        return res

    return dict(ins=[w_sh[i] for i in idxs], outs=[jax.ShapeDtypeStruct((4,) + SH_SHAPES[i], BF16) for i in idxs],
                nsem=3 * len(idxs), alias={}, copies=copies)


def _u_gather_d2d(got, idxs):
    def copies(ins, outs, sem):
        x, y, c = _me()
        res = []
        for j, (px, py) in enumerate(_other_chips(x, y)):
            for n, i in enumerate(idxs):
                src = ins[n].at[2 * px + py, _half_rows(i, c, 16)]
                res.append((_rc(src, outs[n].at[2 * px + py, _half_rows(i, c, 16)], sem(j * len(idxs) + n), (x, y, 1 - c)),
                            _rc(src, outs[n].at[2 * px + py, _half_rows(i, 1 - c, 16)], sem(j * len(idxs) + n), (x, y, 1 - c))))
        return res

    return dict(ins=list(got), outs=[jax.ShapeDtypeStruct(g.shape, g.dtype) for g in got], nsem=3 * len(idxs),
                alias={n: n for n in range(len(idxs))}, copies=copies)


def _u_pair_send(gs, idxs):
    def copies(ins, outs, sem):
        x, y, c = _me()
        res = []
        for n, i in enumerate(idxs):
            for sh in range(4):
                cp = _rc(ins[n].at[sh, _half_rows(i, 1 - c, 8)], outs[n].at[sh], sem(4 * n + sh), (x, y, 1 - c))
                res.append((cp, cp))
        return res

    return dict(ins=list(gs), outs=[jax.ShapeDtypeStruct((4, SH_SHAPES[i][0] // 2, SH_SHAPES[i][1]), F32) for i in idxs],
                nsem=4 * len(idxs), alias={}, copies=copies)


def _u_chip_exchange(ts):
    def copies(ins, outs, sem):
        x, y, c = _me()
        res = []
        for j, (px, py) in enumerate(_other_chips(x, y)):
            for n in range(len(ts)):
                cp = _rc(ins[n].at[2 * px + py], outs[n].at[j], sem(j * len(ts) + n), (px, py, c))
                res.append((cp, cp))
        return res

    return dict(ins=list(ts), outs=[jax.ShapeDtypeStruct((3,) + t.shape[1:], t.dtype) for t in ts], nsem=3 * len(ts),
                alias={}, copies=copies)


def _u_pair_join(hs):
    def copies(ins, outs, sem):
        x, y, c = _me()
        res = []
        for n in range(len(hs)):
            cp = _rc(ins[n], outs[n], sem(n), (x, y, 1 - c))
            res.append((cp, cp))
        return res

    return dict(ins=list(hs), outs=[jax.ShapeDtypeStruct(h.shape, h.dtype) for h in hs], nsem=len(hs), alias={}, copies=copies)


def _comm_phase(units, ci, co, send_sems, recv_sems, start):
    ii = oo = off = 0
    for u in units:
        ni, no = len(u["ins"]), len(u["outs"])
        for st, arrival in u["copies"](ci[ii:ii + ni], co[oo:oo + no], lambda k, off=off: (send_sems.at[off + k], recv_sems.at[off + k])):
            if start:
                st.start()
            else:
                st.wait_send()
                arrival.wait_recv()
        ii, oo, off = ii + ni, oo + no, off + u["nsem"]


def _carry(units, n_in, n_out):
    ins = [a for u in units for a in u["ins"]]
    outs = [o for u in units for o in u["outs"]]
    alias, ii, oo = {}, 0, 0
    for u in units:
        for a, b in u["alias"].items():
            alias[n_in + ii + a] = n_out + oo + b
        ii, oo = ii + len(u["ins"]), oo + len(u["outs"])
    nsem = sum(u["nsem"] for u in units)
    scratch = [pltpu.SemaphoreType.DMA((nsem,)), pltpu.SemaphoreType.DMA((nsem,))] if units else []
    return ins, outs, alias, scratch


def _split_units(units, res):
    out, oo = [], 0
    for u in units:
        out.append(list(res[oo:oo + len(u["outs"])]))
        oo += len(u["outs"])
    return out


def _comm_call(name, units):
    ins, outs, alias, scratch = _carry(units, 0, 0)

    def body(*refs):
        ci, co = refs[:len(ins)], refs[len(ins):len(ins) + len(outs)]
        _comm_phase(units, ci, co, refs[-2], refs[-1], True)
        _comm_phase(units, ci, co, refs[-2], refs[-1], False)

    res = pl.pallas_call(body, name=name, out_shape=outs, in_specs=[HBM] * len(ins), out_specs=[HBM] * len(outs),
                         scratch_shapes=scratch, input_output_aliases=alias)(*ins)
    return _split_units(units, res)


SEM = pl.BlockSpec(memory_space=pltpu.SEMAPHORE)
EFFECT = pltpu.SideEffectType.DATAFLOW_SIDE_EFFECTING


def _unit_start(unit, name, after=()):
    bufs = list(unit["ins"]) + [lax.empty(o.shape, o.dtype) for o in unit["outs"]]
    n_i, n_b, ns = len(unit["ins"]), len(bufs), unit["nsem"]

    def body(*refs):
        send_sems, recv_sems = refs[n_b + len(after)], refs[n_b + len(after) + 1]
        for st, _ in unit["copies"](refs[:n_i], refs[n_i:n_b], lambda k: (send_sems.at[k], recv_sems.at[k])):
            st.start()
        refs[-1][...] = jnp.zeros_like(refs[-1])

    res = pl.pallas_call(
        body, name=name,
        out_shape=[pltpu.SemaphoreType.DMA((ns,)), pltpu.SemaphoreType.DMA((ns,))] + [pltpu.HBM(b.shape, b.dtype) for b in bufs]
        + [jax.ShapeDtypeStruct((8, 128), F32)],
        in_specs=[HBM] * n_b + [pl.BlockSpec(memory_space=pl.ANY)] * len(after), out_specs=[SEM, SEM] + [HBM] * n_b + [VMEM_SPEC],
        input_output_aliases={i: 2 + i for i in range(n_b)},
        compiler_params=pltpu.CompilerParams(has_side_effects=EFFECT),
    )(*[pltpu.with_memory_space_constraint(b, pltpu.HBM) for b in bufs], *after)
    return res[0], res[1], list(res[2:2 + n_b]), res[-1]


def _unit_wait(unit, send_sems, recv_sems, bufs, after, name):
    n_i, n_b = len(unit["ins"]), len(bufs)

    def body(*refs):
        ss, rs = refs[n_b], refs[n_b + 1]
        for st, arrival in unit["copies"](refs[:n_i], refs[n_i:n_b], lambda k: (ss.at[k], rs.at[k])):
            st.wait_send()
            arrival.wait_recv()

    res = pl.pallas_call(
        body, name=name, out_shape=[pltpu.HBM(b.shape, b.dtype) for b in bufs],
        in_specs=[HBM] * n_b + [SEM, SEM] + [pl.BlockSpec(memory_space=pl.ANY)] * len(after), out_specs=[HBM] * n_b,
        input_output_aliases={i: i for i in range(n_b)}, compiler_params=pltpu.CompilerParams(has_side_effects=EFFECT),
    )(*bufs, send_sems, recv_sems, *after)
    return list(res[:n_i]), list(res[n_i:])


def _pair_add(g, land, core, name):
    _, rows, cols = g.shape
    half = rows // 2
    rb = _tile(half, 512, 16)
    nb = half // rb

    def body(c_ref, g_ref, l_ref, o_ref):
        o_ref[...] = (g_ref[...] + l_ref[...]).astype(BF16)

    return pl.pallas_call(
        body, name=name,
        grid_spec=pltpu.PrefetchScalarGridSpec(
            num_scalar_prefetch=1, grid=(4, nb),
            in_specs=[pl.BlockSpec((1, rb, cols), lambda s, i, c_ref: (s, c_ref[0] * nb + i, 0)),
                      pl.BlockSpec((1, rb, cols), lambda s, i, c_ref: (s, i, 0))],
            out_specs=pl.BlockSpec((1, rb, cols), lambda s, i, c_ref: (s, i, 0))),
        out_shape=jax.ShapeDtypeStruct((4, half, cols), BF16),
        compiler_params=_params(("parallel", "parallel")),
    )(core, g, land)


def _chip_sum(t, r, chip, name):
    _, half, cols = t.shape
    rb = _tile(half, 512, 16)

    def body(s_ref, t_ref, r_ref, o_ref):
        o_ref[...] = ((t_ref[0].astype(F32) + r_ref[0].astype(F32)) + r_ref[1].astype(F32)) + r_ref[2].astype(F32)

    return pl.pallas_call(
        body, name=name,
        grid_spec=pltpu.PrefetchScalarGridSpec(
            num_scalar_prefetch=1, grid=(half // rb,),
            in_specs=[pl.BlockSpec((1, rb, cols), lambda i, s_ref: (s_ref[0], i, 0)),
                      pl.BlockSpec((3, rb, cols), lambda i, s_ref: (0, i, 0))],
            out_specs=pl.BlockSpec((rb, cols), lambda i, s_ref: (i, 0))),
        out_shape=jax.ShapeDtypeStruct((half, cols), F32),
        compiler_params=_params(("parallel",)),
    )(chip, t, r)


def _adam_math(wv, gv, mv, vv):
    mn = ADAM_B1 * mv + (1.0 - ADAM_B1) * gv
    vn = ADAM_B2 * vv + (1.0 - ADAM_B2) * (gv * gv)
    m_hat = mn / (1.0 - ADAM_B1 ** ADAM_STEP)
    v_hat = vn / (1.0 - ADAM_B2 ** ADAM_STEP)
    return -ADAM_LR * (m_hat / (jnp.sqrt(v_hat) + ADAM_EPS) + ADAM_WD * wv), mn, vn


def _adamw_halves(wt, mt, vt, mine, theirs, core, name):
    _, rows, cols = wt.shape
    half = rows // 2
    rb = _tile(half, 256, 8)
    nb = half // rb

    def body(c_ref, w_ref, m_ref, v_ref, a_ref, b_ref, g_ref, d_ref, mo_ref, vo_ref):
        gv = jnp.where(pl.program_id(0) == c_ref[0], a_ref[...], b_ref[...])
        dl, mn, vn = _adam_math(w_ref[...], gv, m_ref[...], v_ref[...])
        g_ref[...] = gv
        d_ref[...] = dl
        mo_ref[...] = mn
        vo_ref[...] = vn

    full = pl.BlockSpec((None, rb, cols), lambda hf, i, c_ref: (0, hf * nb + i, 0))
    part = pl.BlockSpec((rb, cols), lambda hf, i, c_ref: (i, 0))
    return pl.pallas_call(
        body, name=name,
        grid_spec=pltpu.PrefetchScalarGridSpec(num_scalar_prefetch=1, grid=(2, nb), in_specs=[full, full, full, part, part],
                                               out_specs=[full] * 4),
        out_shape=[jax.ShapeDtypeStruct((1, rows, cols), F32)] * 4,
        compiler_params=_params(("parallel", "parallel")),
    )(core, wt, mt, vt, mine, theirs)


SG_REP = 144
SG_LOSS = 136
SG_W2, SG_CW = SG_REP, SG_REP + 4 * 16
SG_ROWS = SG_CW + 4 * 40
SP_ROWS = SG_REP + 16 + 40


def _mod_shard(c_all, ada_w_sh):
    def body(c_ref, w_ref, o_ref):
        cv = c_ref[...]
        o_ref[...] = _dg((cv * _sigmoid(cv)).astype(BF16), w_ref[...].astype(BF16), 1, 0)

    return pl.pallas_call(body, name="mod_shard", out_shape=jax.ShapeDtypeStruct((8, 1536), F32),
                          in_specs=[VMEM_SPEC, VMEM_SPEC], out_specs=VMEM_SPEC,
                          compiler_params=pltpu.CompilerParams(vmem_limit_bytes=VMEM_LIMIT))(c_all, ada_w_sh)


def _mod_select(mod_all, ada_b4):
    def body(m_ref, b_ref, o_ref):
        x, y, c = _me()
        me = 4 * x + 2 * y + c
        for sh in range(4):
            o_ref[sh] = m_ref[2 * sh, me] + b_ref[sh]

    return pl.pallas_call(body, name="mod_select", out_shape=jax.ShapeDtypeStruct((4, 12, 128), F32),
                          in_specs=[VMEM_SPEC, VMEM_SPEC], out_specs=VMEM_SPEC)(mod_all, ada_b4)


def _small_reduce(sg_all):
    def body(g_ref, o_ref):
        x, y, c = _me()
        s_me = 2 * x + y
        w2_rows = pl.ds(pl.multiple_of(SG_W2 + 16 * s_me, 8), 16)
        cw_rows = pl.ds(pl.multiple_of(SG_CW + 40 * s_me, 8), 40)
        a = g_ref[0, 0:SG_REP, :]
        b = g_ref[0, w2_rows, :]
        d = g_ref[0, cw_rows, :]
        for dev in range(1, 8):
            a = a + g_ref[dev, 0:SG_REP, :]
            b = b + g_ref[dev, w2_rows, :]
            d = d + g_ref[dev, cw_rows, :]
        o_ref[0:SG_REP, :] = a
        o_ref[SG_REP:SG_REP + 16, :] = b
        o_ref[SG_REP + 16:SP_ROWS, :] = d

    return pl.pallas_call(body, name="small_grad_reduce", out_shape=jax.ShapeDtypeStruct((SP_ROWS, 128), F32),
                          in_specs=[VMEM_SPEC], out_specs=VMEM_SPEC)(sg_all)


def _ada_grad(dmod_all, c_bc):
    def body(g_ref, c_ref, o_ref):
        x, y, c = _me()
        s_me = 2 * x + y
        for k in range(12):
            acc = jnp.zeros((D, 128), F32)
            for b in range(8):
                cv = c_ref[b]
                acc = acc + (cv * _sigmoid(cv)) * g_ref[s_me, k, b:b + 1, :]
            o_ref[:, k * 128:(k + 1) * 128] = acc

    return pl.pallas_call(body, name="ada_w_grad", out_shape=jax.ShapeDtypeStruct((D, 1536), F32),
                          in_specs=[VMEM_SPEC, VMEM_SPEC], out_specs=VMEM_SPEC,
                          compiler_params=pltpu.CompilerParams(vmem_limit_bytes=VMEM_LIMIT))(dmod_all, c_bc)


def _adamw(wt, g, m, v, name):
    rows, cols = wt.shape
    rb = _tile(rows, 256, 8)

    def fn(c, i, wv, gv, mv, vv):
        return _adam_math(wv, gv, mv, vv)

    return _rowcall(fn, [_rows(t, rb) for t in (wt, g, m, v)], [_orow(rows, cols, F32, rb)] * 3,
                    n_rows=rows, rb=rb, name=name)


def _pad_rows(t, rows):
    flat = t.reshape(-1)
    return jnp.pad(flat, (0, rows * 128 - flat.shape[0])).reshape(rows, 128)


SP_LAYOUT = (("ada_b", 48), ("norm1_w", 8), ("gla_gate_b", 8), ("gla_norm_w", 8), ("norm2_w", 8), ("conv_b", 48),
             ("final_norm_w", 8), (None, 8), ("gla_gate_w2", 16), ("conv_w", 40))


def _pack_small(d):
    return jnp.concatenate([jnp.zeros((rows, 128), F32) if n is None else _pad_rows(d[n].astype(F32), rows)
                            for n, rows in SP_LAYOUT], axis=0)


def _unpack_small(pk, shapes):
    out, off = {}, 0
    for n, rows in SP_LAYOUT:
        if n is not None:
            shp = shapes[n]
            out[n] = pk[off:off + rows].reshape(-1)[:math.prod(shp)].reshape(shp)
        off += rows
    return out


def kernel(x, c, positions, ada_w, ada_b, norm1_w, w_in, gla_gate_w2, gla_gate_b, gla_norm_w, w_gla_branch, w_attn_branch, w_out, norm2_w, w_up, conv_w, conv_b, w_down, final_norm_w, loss_target, m_ada_w, m_ada_b, m_norm1_w, m_w_in, m_gla_gate_w2, m_gla_gate_b, m_gla_norm_w, m_w_gla_branch, m_w_attn_branch, m_w_out, m_norm2_w, m_w_up, m_conv_w, m_conv_b, m_w_down, m_final_norm_w, v_ada_w, v_ada_b, v_norm1_w, v_w_in, v_gla_gate_w2, v_gla_gate_b, v_gla_norm_w, v_w_gla_branch, v_w_attn_branch, v_w_out, v_norm2_w, v_w_up, v_conv_w, v_conv_b, v_w_down, v_final_norm_w):
    s = x.shape[1]
    names = ("ada_w", "ada_b", "norm1_w", "w_in", "gla_gate_w2", "gla_gate_b", "gla_norm_w", "w_gla_branch", "w_attn_branch",
             "w_out", "norm2_w", "w_up", "conv_w", "conv_b", "w_down", "final_norm_w")
    wts = dict(zip(names, (ada_w, ada_b, norm1_w, w_in, gla_gate_w2, gla_gate_b, gla_norm_w, w_gla_branch, w_attn_branch,
                           w_out, norm2_w, w_up, conv_w, conv_b, w_down, final_norm_w)))
    ms = dict(zip(names, (m_ada_w, m_ada_b, m_norm1_w, m_w_in, m_gla_gate_w2, m_gla_gate_b, m_gla_norm_w, m_w_gla_branch,
                          m_w_attn_branch, m_w_out, m_norm2_w, m_w_up, m_conv_w, m_conv_b, m_w_down, m_final_norm_w)))
    vs = dict(zip(names, (v_ada_w, v_ada_b, v_norm1_w, v_w_in, v_gla_gate_w2, v_gla_gate_b, v_gla_norm_w, v_w_gla_branch,
                          v_w_attn_branch, v_w_out, v_norm2_w, v_w_up, v_conv_w, v_conv_b, v_w_down, v_final_norm_w)))

    pk0 = jnp.concatenate([_pad_rows(c, 8), _pad_rows(gla_gate_w2, 16), _pad_rows(conv_w, 40)], axis=0)
    sm_all = _allgather8(pk0, "gather_small")
    c_all = sm_all[:, 0:8, :].reshape(8, D)
    w2_full = sm_all[0::2, 8:24, :].transpose(1, 0, 2).reshape(GLA_LR, 512)
    cw_full = sm_all[0::2, 24:64, :].reshape(4, 40 * 128)[:, :3 * W_UP_SH].reshape(4, 3, W_UP_SH).transpose(1, 0, 2).reshape(3, 2 * D_FF)

    mod_sh = _mod_shard(c_all, ada_w[0])
    mod_all = _allgather8(mod_sh.reshape(96, 128), "gather_mod")

    w_sh = [wts[n].astype(BF16) for n in BIG]
    u_g0 = _u_gather_ici(w_sh, (0,))
    g0 = (u_g0,) + _unit_start(u_g0, "gather_w_in_start", after=[mod_all])
    mod = _mod_select(mod_all.reshape(8, 8, 12, 128) + g0[4][0, 0], ada_b.reshape(4, 12, 128)).reshape(6, D)

    core = lax.axis_index("c").astype(jnp.int32).reshape(1)
    chip = (2 * lax.axis_index("x") + lax.axis_index("y")).astype(jnp.int32)
    sm = dict(n1w=norm1_w, n2w=norm2_w, fnw=final_norm_w.reshape(1, D), gnw=gla_norm_w, gb=gla_gate_b,
              w2=jnp.pad(w2_full, ((0, 128 - GLA_LR), (0, 0))), cw=_ff_to_kernel(cw_full), cb=_ff_to_kernel(conv_b))
    loss, grad_x, halves, others, small, ts0 = _local_step(x[0], mod, positions.reshape(s, 1), loss_target[0], sm, w_sh,
                                                               g0, chip, core)

    dcw = _ff_from_kernel(small["cw"]).reshape(3, 4, W_UP_SH).transpose(1, 0, 2)
    dw2 = small["w2"][:GLA_LR].reshape(GLA_LR, 4, 128).transpose(1, 0, 2)
    sg = jnp.concatenate(
        [_pad_rows(small["dmod"], 48), _pad_rows(small["n1w"], 8), _pad_rows(small["gb"], 8), _pad_rows(small["gnw"], 8),
         _pad_rows(small["n2w"], 8), _pad_rows(_ff_from_kernel(small["cb"]), 48), _pad_rows(small["fnw"], 8), _pad_rows(loss, 8)]
        + [_pad_rows(dw2[k], 16) for k in range(4)] + [_pad_rows(dcw[k], 40) for k in range(4)], axis=0)
    sg_all = _allgather8(sg, "gather_small_grads")
    u_ex = _u_chip_exchange([ts0])
    pending = (u_ex,) + _unit_start(u_ex, "grad_exchange_w_in_start", after=[sg_all])
    sg_all = sg_all + pending[4][0, 0]
    g_small_pk = _small_reduce(sg_all)
    dmod_all = sg_all[:, 0:48, :].reshape(8, 4, 12, 128).transpose(1, 2, 0, 3)
    g_ada_w = _ada_grad(dmod_all, jnp.broadcast_to(c_all[:, :, None], (8, D, 128)))

    shapes = {n: wts[n].shape for n in names}
    g_small = _unpack_small(g_small_pk, shapes)
    grads = {"ada_w": g_ada_w.reshape(1, D, 1536), **g_small}
    deltas, new_m, new_v = {}, {}, {}
    for n, mine, theirs in zip(BIG[1:], halves, others):
        grads[n], deltas[n], new_m[n], new_v[n] = _adamw_halves(wts[n], ms[n], vs[n], mine, theirs, core, "adamw_" + n)
    shp = ada_w.shape
    d_, m_, v_ = _adamw(ada_w[0], g_ada_w, m_ada_w[0], v_ada_w[0], "adamw_ada_w")
    deltas["ada_w"], new_m["ada_w"], new_v["ada_w"] = d_.reshape(shp), m_.reshape(shp), v_.reshape(shp)
    d_, m_, v_ = _adamw(_pack_small(wts), g_small_pk, _pack_small(ms), _pack_small(vs), "adamw_small")
    for dst, pk in ((deltas, d_), (new_m, m_), (new_v, v_)):
        dst.update(_unpack_small(pk, shapes))

    [t0], [r0] = _unit_wait(*pending[:4], after=[d_, deltas["ada_w"], deltas["w_up"], deltas["w_down"]], name="grad_exchange_w_in_wait")
    half0 = _chip_sum(t0, r0, chip.reshape(1), "grad_chip_sum_w_in")
    [[oth0]] = _comm_call("grad_join_w_in", [_u_pair_join([half0])])
    grads["w_in"], deltas["w_in"], new_m["w_in"], new_v["w_in"] = _adamw_halves(w_in, m_w_in, v_w_in, half0, oth0, core, "adamw_w_in")

    return (g_small_pk[SG_LOSS, 0], grad_x.reshape(1, s, D), *[grads[n] for n in names], *[deltas[n] for n in names],
            *[new_m[n] for n in names], *[new_v[n] for n in names])
```
